```python
import jax, jax.numpy as jnp
from jax import lax
import numpy as np


D_MODEL = 1024
BATCH = 8
SEQ = 4096
DEPTH = 1

MIX_WIDTH = D_MODEL
ATTN_WIDTH = MIX_WIDTH // 2
ATTN_HEADS = 8
ATTN_HEAD_DIM = ATTN_WIDTH // ATTN_HEADS
DILATED_PATTERNS = ((128, 1), (512, 4), (2048, 16))
ATTN_BLOCK = 128
ATTN_PAD_UNIT = ATTN_BLOCK * 16
HGRN_WIDTH = MIX_WIDTH - ATTN_WIDTH
HGRN_EXPAND = 128
HGRN_HEADS = HGRN_WIDTH // HGRN_EXPAND
HGRN_FDIM = HGRN_EXPAND
HGRN_IDIM = HGRN_WIDTH // HGRN_HEADS
HGRN_CHUNK = 64
D_FF = 2816
CONV_WIDTH = 3
NORM_EPS = 1e-6
IN_SIZES = (ATTN_WIDTH, ATTN_WIDTH, ATTN_WIDTH, HGRN_WIDTH, HGRN_WIDTH, HGRN_WIDTH, HGRN_WIDTH)
IN_TOTAL = sum(IN_SIZES)
IN_SPLITS = [sum(IN_SIZES[:j + 1]) for j in range(len(IN_SIZES) - 1)]

kernel_name = 'hybrid_dilated_attn_hgrn2_convglu'


def rms_norm(x, g):
    xf = x.astype(jnp.float32)
    y = xf * lax.rsqrt(jnp.mean(xf * xf, axis=-1, keepdims=True) + NORM_EPS)
    return (y * g.astype(jnp.float32)).astype(x.dtype)


def dilated_branch(q, k, v, window, dilation):
    B, Sp, H, E = q.shape
    n_keys = window // dilation
    nb = Sp // (dilation * ATTN_BLOCK)
    shape = (B, nb, ATTN_BLOCK, dilation, H, E)
    qb, kb, vb = q.reshape(shape), k.reshape(shape), v.reshape(shape)

    def with_prev(t):
        prev = jnp.pad(t, ((0, 0), (1, 0), (0, 0), (0, 0), (0, 0), (0, 0)))[:, :-1]
        return jnp.concatenate([prev, t], axis=2)

    kw, vw = with_prev(kb), with_prev(vb)
    s = jnp.einsum('bnqrhe,bnkrhe->bnrhqk', qb, kw,
                   preferred_element_type=jnp.float32) * (E ** -0.5)
    qi = jnp.arange(ATTN_BLOCK)[:, None]
    kj = jnp.arange(2 * ATTN_BLOCK)[None, :]
    dist = qi + ATTN_BLOCK - kj
    blk = jnp.arange(nb)[:, None, None]
    valid = (dist >= 0) & (dist <= n_keys) & (blk * ATTN_BLOCK + kj - ATTN_BLOCK >= 0)
    s = jnp.where(valid[None, :, None, None], s, -jnp.inf)
    m = jnp.max(s, axis=-1)
    p = jnp.exp(s - m[..., None])
    l = jnp.sum(p, axis=-1)
    o = jnp.einsum('bnrhqk,bnkrhe->bnqrhe', p.astype(v.dtype), vw,
                   preferred_element_type=jnp.float32).reshape(B, Sp, H, E)
    m = m.transpose(0, 1, 4, 2, 3).reshape(B, Sp, H)
    l = l.transpose(0, 1, 4, 2, 3).reshape(B, Sp, H)
    return o, m, l


def dilated_attention(q, k, v):
    B, S, H, E = q.shape
    Sp = -(-S // ATTN_PAD_UNIT) * ATTN_PAD_UNIT
    pad = ((0, 0), (0, Sp - S), (0, 0), (0, 0))
    q, k, v = jnp.pad(q, pad), jnp.pad(k, pad), jnp.pad(v, pad)
    outs, maxes, sums = [], [], []
    for window, dilation in DILATED_PATTERNS:
        o, m, l = dilated_branch(q, k, v, window, dilation)
        outs.append(o)
        maxes.append(m)
        sums.append(l)
    ms, ls, os_ = jnp.stack(maxes), jnp.stack(sums), jnp.stack(outs)
    w = jnp.exp(ms - jnp.max(ms, axis=0, keepdims=True))
    den = jnp.sum(w * ls, axis=0)
    num = jnp.sum(w[..., None] * os_, axis=0)
    return (num / den[..., None])[:, :S]


def hgrn2_mixer(q, f, i, lb):
    B, S, H, K = q.shape
    V = i.shape[-1]
    C = HGRN_CHUNK
    nc = S // C
    qf = jax.nn.silu(q.astype(jnp.float32))
    forget = lb + (1.0 - lb) * jax.nn.sigmoid(f.astype(jnp.float32))
    key = 1.0 - forget
    log_f = jnp.log(forget)
    iv = i.astype(jnp.float32)
    qf = qf.reshape(B, nc, C, H, K)
    key = key.reshape(B, nc, C, H, K)
    log_f = log_f.reshape(B, nc, C, H, K)
    iv = iv.reshape(B, nc, C, H, V)
    b = jnp.cumsum(log_f, axis=2)
    q_dec = qf * jnp.exp(b)
    k_inv = key * jnp.exp(-b)
    A = jnp.einsum('bnthk,bnshk->bnhts', q_dec, k_inv)
    causal = jnp.tril(jnp.ones((C, C), dtype=bool))
    A = jnp.where(causal, A, 0.0)
    o_intra = jnp.einsum('bnhts,bnshv->bnthv', A, iv)
    b_end = b[:, :, -1]
    k_end = key * jnp.exp(b_end[:, :, None] - b)
    U = jnp.einsum('bnshk,bnshv->bnhkv', k_end, iv)
    decay = jnp.exp(b_end)

    def step(state, xs):
        d, u = xs
        return d[..., None] * state + u, state

    init = jnp.zeros((B, H, K, V), jnp.float32)
    _, states = lax.scan(step, init, (decay.transpose(1, 0, 2, 3), U.transpose(1, 0, 2, 3, 4)))
    states = states.transpose(1, 0, 2, 3, 4)
    o_inter = jnp.einsum('bnthk,bnhkv->bnthv', q_dec, states)
    return (o_intra + o_inter).reshape(B, S, H, V)


def conv_glu(u, w_up, conv_w, conv_b, w_down):
    S = u.shape[1]
    gate, val = jnp.split(u @ w_up, 2, axis=-1)
    gp = jnp.pad(gate, ((0, 0), (CONV_WIDTH - 1, 0), (0, 0)))
    conv = conv_b
    for j in range(CONV_WIDTH):
        conv = conv + conv_w[j] * gp[:, j:j + S]
    return (jax.nn.gelu(conv, approximate=False) * val) @ w_down


def _fwd_setup_inputs(seed: int = 0) -> dict:
    key = jax.random.key(seed)
    ks = jax.random.split(key, 13)
    f32 = jnp.float32
    nrm = lambda k, shape: jax.random.normal(k, shape, f32)
    return {
        'x': nrm(ks[0], (BATCH, SEQ, D_MODEL)),
        'norm1_g': 1.0 + 0.02 * nrm(ks[1], (DEPTH, D_MODEL)),
        'w_in': nrm(ks[2], (DEPTH, D_MODEL, IN_TOTAL)) * D_MODEL ** -0.5,
        'attn_norm_g': 1.0 + 0.02 * nrm(ks[3], (DEPTH, ATTN_WIDTH)),
        'hgrn_norm_g': 1.0 + 0.02 * nrm(ks[4], (DEPTH, HGRN_WIDTH)),
        'hgrn_lb_logits': 0.1 * nrm(ks[5], (DEPTH + 1, HGRN_WIDTH)),
        'w_out': nrm(ks[6], (DEPTH, MIX_WIDTH, D_MODEL)) * MIX_WIDTH ** -0.5,
        'norm2_g': 1.0 + 0.02 * nrm(ks[7], (DEPTH, D_MODEL)),
        'w_up': nrm(ks[8], (DEPTH, D_MODEL, 2 * D_FF)) * D_MODEL ** -0.5,
        'conv_w': nrm(ks[9], (DEPTH, CONV_WIDTH, D_FF)) * CONV_WIDTH ** -0.5,
        'conv_b': 0.02 * nrm(ks[10], (DEPTH, D_FF)),
        'w_down': nrm(ks[11], (DEPTH, D_FF, D_MODEL)) * D_FF ** -0.5,
        'final_norm_g': 1.0 + 0.02 * nrm(ks[12], (D_MODEL,)),
    }


def _fwd_reference(x, norm1_g, w_in, attn_norm_g, hgrn_norm_g, hgrn_lb_logits, w_out,
              norm2_g, w_up, conv_w, conv_b, w_down, final_norm_g):
    B, S, _ = x.shape
    lower_bounds = jnp.cumsum(jax.nn.softmax(hgrn_lb_logits.astype(jnp.float32), axis=0), axis=0)
    h = x
    for layer in range(DEPTH):
        u = rms_norm(h, norm1_g[layer])
        proj = u @ w_in[layer]
        aq, ak, av, hq, hf, hi, hg = jnp.split(proj, IN_SPLITS, axis=-1)
        attn = dilated_attention(aq.reshape(B, S, ATTN_HEADS, ATTN_HEAD_DIM),
                                 ak.reshape(B, S, ATTN_HEADS, ATTN_HEAD_DIM),
                                 av.reshape(B, S, ATTN_HEADS, ATTN_HEAD_DIM))
        attn = rms_norm(attn.reshape(B, S, ATTN_WIDTH), attn_norm_g[layer])
        lb = lower_bounds[layer].reshape(HGRN_HEADS, HGRN_FDIM)
        rec = hgrn2_mixer(hq.reshape(B, S, HGRN_HEADS, HGRN_FDIM),
                          hf.reshape(B, S, HGRN_HEADS, HGRN_FDIM),
                          hi.reshape(B, S, HGRN_HEADS, HGRN_IDIM), lb)
        rec = rms_norm(rec, hgrn_norm_g[layer].reshape(HGRN_HEADS, HGRN_IDIM))
        rec = rec * jax.nn.silu(hg.astype(jnp.float32).reshape(B, S, HGRN_HEADS, HGRN_IDIM))
        mixed = jnp.concatenate([attn.astype(jnp.float32), rec.reshape(B, S, HGRN_WIDTH)], axis=-1)
        h = h + mixed.astype(h.dtype) @ w_out[layer]
        u = rms_norm(h, norm2_g[layer])
        h = h + conv_glu(u, w_up[layer], conv_w[layer], conv_b[layer], w_down[layer]).astype(h.dtype)
    return rms_norm(h, final_norm_g)


import jax as _jax
import jax.numpy as _jnp

TWIN_FORMAT = 'train_step'
FWD_PARAMS = ['x', 'norm1_g', 'w_in', 'attn_norm_g', 'hgrn_norm_g', 'hgrn_lb_logits', 'w_out', 'norm2_g', 'w_up', 'conv_w', 'conv_b', 'w_down', 'final_norm_g']
TWIN_WEIGHTS = ['norm1_g', 'w_in', 'attn_norm_g', 'hgrn_norm_g', 'hgrn_lb_logits', 'w_out', 'norm2_g', 'w_up', 'conv_w', 'conv_b', 'w_down', 'final_norm_g']
TWIN_DIFF_INPUT = 'x'
TWIN_INPUTS = ['x', 'norm1_g', 'w_in', 'attn_norm_g', 'hgrn_norm_g', 'hgrn_lb_logits', 'w_out', 'norm2_g', 'w_up', 'conv_w', 'conv_b', 'w_down', 'final_norm_g', 'loss_target', 'm_norm1_g', 'm_w_in', 'm_attn_norm_g', 'm_hgrn_norm_g', 'm_hgrn_lb_logits', 'm_w_out', 'm_norm2_g', 'm_w_up', 'm_conv_w', 'm_conv_b', 'm_w_down', 'm_final_norm_g', 'v_norm1_g', 'v_w_in', 'v_attn_norm_g', 'v_hgrn_norm_g', 'v_hgrn_lb_logits', 'v_w_out', 'v_norm2_g', 'v_w_up', 'v_conv_w', 'v_conv_b', 'v_w_down', 'v_final_norm_g']
TWIN_OUTPUTS = ['loss', 'grad_x', 'grad_norm1_g', 'grad_w_in', 'grad_attn_norm_g', 'grad_hgrn_norm_g', 'grad_hgrn_lb_logits', 'grad_w_out', 'grad_norm2_g', 'grad_w_up', 'grad_conv_w', 'grad_conv_b', 'grad_w_down', 'grad_final_norm_g', 'delta_norm1_g', 'delta_w_in', 'delta_attn_norm_g', 'delta_hgrn_norm_g', 'delta_hgrn_lb_logits', 'delta_w_out', 'delta_norm2_g', 'delta_w_up', 'delta_conv_w', 'delta_conv_b', 'delta_w_down', 'delta_final_norm_g', 'new_m_norm1_g', 'new_m_w_in', 'new_m_attn_norm_g', 'new_m_hgrn_norm_g', 'new_m_hgrn_lb_logits', 'new_m_w_out', 'new_m_norm2_g', 'new_m_w_up', 'new_m_conv_w', 'new_m_conv_b', 'new_m_w_down', 'new_m_final_norm_g', 'new_v_norm1_g', 'new_v_w_in', 'new_v_attn_norm_g', 'new_v_hgrn_norm_g', 'new_v_hgrn_lb_logits', 'new_v_w_out', 'new_v_norm2_g', 'new_v_w_up', 'new_v_conv_w', 'new_v_conv_b', 'new_v_w_down', 'new_v_final_norm_g']
TWIN_LEAF_KINDS = {'loss': 'loss', 'grad_x': 'grad_x', 'grad_norm1_g': 'grad_w', 'grad_w_in': 'grad_w', 'grad_attn_norm_g': 'grad_w', 'grad_hgrn_norm_g': 'grad_w', 'grad_hgrn_lb_logits': 'grad_w', 'grad_w_out': 'grad_w', 'grad_norm2_g': 'grad_w', 'grad_w_up': 'grad_w', 'grad_conv_w': 'grad_w', 'grad_conv_b': 'grad_w', 'grad_w_down': 'grad_w', 'grad_final_norm_g': 'grad_w', 'delta_norm1_g': 'delta_w', 'delta_w_in': 'delta_w', 'delta_attn_norm_g': 'delta_w', 'delta_hgrn_norm_g': 'delta_w', 'delta_hgrn_lb_logits': 'delta_w', 'delta_w_out': 'delta_w', 'delta_norm2_g': 'delta_w', 'delta_w_up': 'delta_w', 'delta_conv_w': 'delta_w', 'delta_conv_b': 'delta_w', 'delta_w_down': 'delta_w', 'delta_final_norm_g': 'delta_w', 'new_m_norm1_g': 'new_m', 'new_m_w_in': 'new_m', 'new_m_attn_norm_g': 'new_m', 'new_m_hgrn_norm_g': 'new_m', 'new_m_hgrn_lb_logits': 'new_m', 'new_m_w_out': 'new_m', 'new_m_norm2_g': 'new_m', 'new_m_w_up': 'new_m', 'new_m_conv_w': 'new_m', 'new_m_conv_b': 'new_m', 'new_m_w_down': 'new_m', 'new_m_final_norm_g': 'new_m', 'new_v_norm1_g': 'new_v', 'new_v_w_in': 'new_v', 'new_v_attn_norm_g': 'new_v', 'new_v_hgrn_norm_g': 'new_v', 'new_v_hgrn_lb_logits': 'new_v', 'new_v_w_out': 'new_v', 'new_v_norm2_g': 'new_v', 'new_v_w_up': 'new_v', 'new_v_conv_w': 'new_v', 'new_v_conv_b': 'new_v', 'new_v_w_down': 'new_v', 'new_v_final_norm_g': 'new_v'}


def _forward(args):
    return _fwd_reference(*[args[k] for k in FWD_PARAMS])


def _output_shape():
    out = _jax.eval_shape(lambda: _forward(_fwd_setup_inputs(0)))
    return out.shape, out.dtype

N_MICROBATCH = 1
ADAM_LR = 0.001
ADAM_B1 = 0.9
ADAM_B2 = 0.999
ADAM_EPS = 1e-08
ADAM_WD = 0.01
ADAM_STEP = 10
PER_EXAMPLE_BATCH_AXIS = {'x': 0, 'loss_target': 0}
SHARED_INPUTS = []
_WEIGHT_DTYPES = {'norm1_g': _jnp.float32, 'w_in': _jnp.float32, 'attn_norm_g': _jnp.float32, 'hgrn_norm_g': _jnp.float32, 'hgrn_lb_logits': _jnp.float32, 'w_out': _jnp.float32, 'norm2_g': _jnp.float32, 'w_up': _jnp.float32, 'conv_w': _jnp.float32, 'conv_b': _jnp.float32, 'w_down': _jnp.float32, 'final_norm_g': _jnp.float32}
MOMENT_SCALE = {'norm1_g': 2.042907e-01, 'w_in': 1.106930e-01, 'attn_norm_g': 1.730903e-01, 'hgrn_norm_g': 9.491649e-02, 'hgrn_lb_logits': 8.408425e-03, 'w_out': 1.246837e-01, 'norm2_g': 1.273623e-01, 'w_up': 4.933348e-02, 'conv_w': 5.123974e-02, 'conv_b': 4.921345e-02, 'w_down': 8.044532e-02, 'final_norm_g': 3.206668e+01}


def _to_microbatches(a, axis):
    t = _jnp.moveaxis(a, axis, 0)
    t = t.reshape((N_MICROBATCH, t.shape[0] // N_MICROBATCH) + t.shape[1:])
    return _jnp.moveaxis(t, 1, axis + 1)


def setup_inputs(seed: int = 0) -> dict:
    inp = _fwd_setup_inputs(seed)
    key = _jax.random.fold_in(_jax.random.key(seed), 7919)
    shape, _ = _output_shape()
    out = dict(inp)
    out["loss_target"] = _jax.random.normal(_jax.random.fold_in(key, 0), shape, _jnp.float32)
    for i, name in enumerate(TWIN_WEIGHTS):
        w = inp[name].astype(_jnp.float32)
        if MOMENT_SCALE is None:
            s = _jnp.sqrt(_jnp.mean(_jnp.square(w)) + 1e-30)
        else:
            s = MOMENT_SCALE[name]
        km, kv = _jax.random.split(_jax.random.fold_in(key, i + 1))
        out[name] = w
        out["m_" + name] = s * _jax.random.normal(km, w.shape, _jnp.float32)
        out["v_" + name] = (s * s) * _jax.random.uniform(kv, w.shape, _jnp.float32, 0.5, 1.5)
    if N_MICROBATCH > 1:
        for name, axis in PER_EXAMPLE_BATCH_AXIS.items():
            out[name] = _to_microbatches(out[name], axis)
    return {'x': out['x'], 'norm1_g': out['norm1_g'], 'w_in': out['w_in'], 'attn_norm_g': out['attn_norm_g'], 'hgrn_norm_g': out['hgrn_norm_g'], 'hgrn_lb_logits': out['hgrn_lb_logits'], 'w_out': out['w_out'], 'norm2_g': out['norm2_g'], 'w_up': out['w_up'], 'conv_w': out['conv_w'], 'conv_b': out['conv_b'], 'w_down': out['w_down'], 'final_norm_g': out['final_norm_g'], 'loss_target': out['loss_target'], 'm_norm1_g': out['m_norm1_g'], 'm_w_in': out['m_w_in'], 'm_attn_norm_g': out['m_attn_norm_g'], 'm_hgrn_norm_g': out['m_hgrn_norm_g'], 'm_hgrn_lb_logits': out['m_hgrn_lb_logits'], 'm_w_out': out['m_w_out'], 'm_norm2_g': out['m_norm2_g'], 'm_w_up': out['m_w_up'], 'm_conv_w': out['m_conv_w'], 'm_conv_b': out['m_conv_b'], 'm_w_down': out['m_w_down'], 'm_final_norm_g': out['m_final_norm_g'], 'v_norm1_g': out['v_norm1_g'], 'v_w_in': out['v_w_in'], 'v_attn_norm_g': out['v_attn_norm_g'], 'v_hgrn_norm_g': out['v_hgrn_norm_g'], 'v_hgrn_lb_logits': out['v_hgrn_lb_logits'], 'v_w_out': out['v_w_out'], 'v_norm2_g': out['v_norm2_g'], 'v_w_up': out['v_w_up'], 'v_conv_w': out['v_conv_w'], 'v_conv_b': out['v_conv_b'], 'v_w_down': out['v_w_down'], 'v_final_norm_g': out['v_final_norm_g']}


def _loss(weights, diff, rest, loss_target):
    with _jax.named_scope("forward"):
        args = {**rest, TWIN_DIFF_INPUT: diff, **{k: w.astype(_WEIGHT_DTYPES[k]) for k, w in weights.items()}}
        y = _forward(args)
    with _jax.named_scope("loss_head"):
        err = _jnp.square(y.astype(_jnp.float32) - loss_target)
        return 0.5 * _jnp.sum(_jnp.mean(err, axis=-1)) if err.ndim else 0.5 * err


def _adamw(w, g, m, v):
    m = ADAM_B1 * m + (1.0 - ADAM_B1) * g
    v = ADAM_B2 * v + (1.0 - ADAM_B2) * _jnp.square(g)
    m_hat = m / (1.0 - ADAM_B1 ** ADAM_STEP)
    v_hat = v / (1.0 - ADAM_B2 ** ADAM_STEP)
    delta = -ADAM_LR * (m_hat / (_jnp.sqrt(v_hat) + ADAM_EPS) + ADAM_WD * w)
    return delta, m, v


def reference(x, norm1_g, w_in, attn_norm_g, hgrn_norm_g, hgrn_lb_logits, w_out, norm2_g, w_up, conv_w, conv_b, w_down, final_norm_g, loss_target, m_norm1_g, m_w_in, m_attn_norm_g, m_hgrn_norm_g, m_hgrn_lb_logits, m_w_out, m_norm2_g, m_w_up, m_conv_w, m_conv_b, m_w_down, m_final_norm_g, v_norm1_g, v_w_in, v_attn_norm_g, v_hgrn_norm_g, v_hgrn_lb_logits, v_w_out, v_norm2_g, v_w_up, v_conv_w, v_conv_b, v_w_down, v_final_norm_g):
    given = dict(x=x, norm1_g=norm1_g, w_in=w_in, attn_norm_g=attn_norm_g, hgrn_norm_g=hgrn_norm_g, hgrn_lb_logits=hgrn_lb_logits, w_out=w_out, norm2_g=norm2_g, w_up=w_up, conv_w=conv_w, conv_b=conv_b, w_down=w_down, final_norm_g=final_norm_g, loss_target=loss_target, m_norm1_g=m_norm1_g, m_w_in=m_w_in, m_attn_norm_g=m_attn_norm_g, m_hgrn_norm_g=m_hgrn_norm_g, m_hgrn_lb_logits=m_hgrn_lb_logits, m_w_out=m_w_out, m_norm2_g=m_norm2_g, m_w_up=m_w_up, m_conv_w=m_conv_w, m_conv_b=m_conv_b, m_w_down=m_w_down, m_final_norm_g=m_final_norm_g, v_norm1_g=v_norm1_g, v_w_in=v_w_in, v_attn_norm_g=v_attn_norm_g, v_hgrn_norm_g=v_hgrn_norm_g, v_hgrn_lb_logits=v_hgrn_lb_logits, v_w_out=v_w_out, v_norm2_g=v_norm2_g, v_w_up=v_w_up, v_conv_w=v_conv_w, v_conv_b=v_conv_b, v_w_down=v_w_down, v_final_norm_g=v_final_norm_g)
    weights = {n: given[n] for n in TWIN_WEIGHTS}
    shared = {n: given[n] for n in SHARED_INPUTS}
    per_example = {n: given[n] for n in ['x']}
    grad_fn = _jax.value_and_grad(_loss, argnums=(0, 1))

    def one_microbatch(ex, loss_target):
        ex = dict(ex)
        diff = ex.pop(TWIN_DIFF_INPUT)
        return grad_fn(weights, diff, {**shared, **ex}, loss_target)

    if N_MICROBATCH == 1:
        loss, (grad_w, grad_x) = one_microbatch(per_example, given["loss_target"])
    else:
        def body(carry, xs):
            loss_sum, grad_sum = carry
            l_k, (gw_k, gx_k) = one_microbatch(xs[0], xs[1])
            with _jax.named_scope("update"):
                return (loss_sum + l_k, _jax.tree.map(_jnp.add, grad_sum, gw_k)), gx_k

        init = (_jnp.zeros((), _jnp.float32), _jax.tree.map(_jnp.zeros_like, weights))
        (loss, grad_w), grad_x = _jax.lax.scan(body, init, (per_example, given["loss_target"]))
    with _jax.named_scope("update"):
        delta_w, new_m, new_v = {}, {}, {}
        for n in TWIN_WEIGHTS:
            delta_w[n], new_m[n], new_v[n] = _adamw(weights[n], grad_w[n], given["m_" + n], given["v_" + n])
    return (loss, grad_x, *[grad_w[n] for n in TWIN_WEIGHTS], *[delta_w[n] for n in TWIN_WEIGHTS],
            *[new_m[n] for n in TWIN_WEIGHTS], *[new_v[n] for n in TWIN_WEIGHTS])
```

```python
import jax
import jax.numpy as jnp
from jax import lax
from jax.experimental import pallas as pl
from jax.experimental.pallas import tpu as pltpu

F32, BF16 = jnp.float32, jnp.bfloat16
NORM_EPS = 1e-6
ATTN_HEADS, HEAD_DIM, ATTN_BLOCK = 8, 64, 128
DILATIONS = (1, 4, 16)
ATTN_SCALE = HEAD_DIM ** -0.5
HGRN_HEADS, HGRN_DIM, HGRN_CHUNK = 4, 128, 64
D_FF = 2816
ADAM_LR, ADAM_B1, ADAM_B2, ADAM_EPS, ADAM_WD, ADAM_STEP = 0.001, 0.9, 0.999, 1e-08, 0.01, 10
LANES = 128
VMEM_LIMIT_BYTES = 48 * 1024 * 1024
N_DEV = 8
MESH_AXES = ("x", "y", "c")
MESH = pl.DeviceIdType.MESH
HBM = pl.BlockSpec(memory_space=pltpu.HBM)
HIGHEST = lax.Precision.HIGHEST


def _cparams(*sem):
    return pltpu.CompilerParams(dimension_semantics=sem, vmem_limit_bytes=VMEM_LIMIT_BYTES)


def _tile(n, pref):
    if n <= pref:
        return n
    t = (pref // LANES) * LANES
    while n % t:
        t -= LANES
    return t


def _dot(a, b, dims, precision=None):
    return lax.dot_general(a, b, (dims, ((), ())), precision=precision, preferred_element_type=F32)


def _nn(a, b, precision=None):
    return _dot(a, b, ((1,), (0,)), precision)


def _nt(a, b):
    return _dot(a, b, ((1,), (1,)))


def _tn(a, b):
    return _dot(a, b, ((0,), (0,)))


def _sigmoid(x):
    return 1.0 / (1.0 + jnp.exp(-x))


def _mm(a, b, *, name, ta=False, tb=False, out_dtype=F32, add=None, tm=512, tn=512, tk=1024):
    M, K = (a.shape[1], a.shape[0]) if ta else a.shape
    N = b.shape[0] if tb else b.shape[1]
    tm, tn, tk = _tile(M, tm), _tile(N, tn), _tile(K, tk)
    nk = K // tk
    dims = ((0 if ta else 1,), (1 if tb else 0,))

    def body(*refs):
        if add is None:
            a_ref, b_ref, o_ref, *scratch = refs
        else:
            a_ref, b_ref, add_ref, o_ref, *scratch = refs
        p = _dot(a_ref[...].astype(BF16), b_ref[...].astype(BF16), dims)

        def finish(acc):
            if add is not None:
                acc = acc + add_ref[...]
            o_ref[...] = acc.astype(out_dtype)

        if nk == 1:
            finish(p)
        else:
            (acc_ref,) = scratch
            k = pl.program_id(2)

            @pl.when(k == 0)
            def _():
                acc_ref[...] = p

            @pl.when(k > 0)
            def _():
                acc_ref[...] += p

            @pl.when(k == nk - 1)
            def _():
                finish(acc_ref[...])

    a_spec = pl.BlockSpec((tk, tm), lambda i, j, k: (k, i)) if ta else pl.BlockSpec((tm, tk), lambda i, j, k: (i, k))
    b_spec = pl.BlockSpec((tn, tk), lambda i, j, k: (j, k)) if tb else pl.BlockSpec((tk, tn), lambda i, j, k: (k, j))
    o_spec = pl.BlockSpec((tm, tn), lambda i, j, k: (i, j))
    operands, in_specs = [a, b], [a_spec, b_spec]
    if add is not None:
        operands.append(add)
        in_specs.append(o_spec)
    return pl.pallas_call(
        body, name=name, grid=(M // tm, N // tn, nk), in_specs=in_specs, out_specs=o_spec,
        out_shape=jax.ShapeDtypeStruct((M, N), out_dtype),
        scratch_shapes=[pltpu.VMEM((tm, tn), F32)] if nk > 1 else [],
        compiler_params=_cparams("parallel", "parallel", "arbitrary"),
    )(*operands)


def _rms_fwd(x, g, *, name, tm=512):
    S, D = x.shape

    def body(x_ref, g_ref, o_ref):
        xv = x_ref[...]
        r = lax.rsqrt(jnp.mean(xv * xv, axis=-1, keepdims=True) + NORM_EPS)
        o_ref[...] = (xv * r * g_ref[...]).astype(BF16)

    return pl.pallas_call(
        body, name=name, grid=(S // tm,),
        in_specs=[pl.BlockSpec((tm, D), lambda i: (i, 0)), pl.BlockSpec((1, D), lambda i: (0, 0))],
        out_specs=pl.BlockSpec((tm, D), lambda i: (i, 0)), out_shape=jax.ShapeDtypeStruct((S, D), BF16),
        compiler_params=_cparams("parallel"),
    )(x, g)


def _rms_bwd(x, g, du, add, *, name, tm=512):
    S, D = x.shape

    def body(x_ref, g_ref, du_ref, add_ref, dx_ref, dg_ref):
        xv, duv = x_ref[...], du_ref[...]
        r = lax.rsqrt(jnp.mean(xv * xv, axis=-1, keepdims=True) + NORM_EPS)
        xh = xv * r
        dxh = duv * g_ref[...]
        dx_ref[...] = add_ref[...] + r * (dxh - xh * jnp.mean(dxh * xh, axis=-1, keepdims=True))
        part = jnp.sum(duv * xh, axis=0, keepdims=True)

        @pl.when(pl.program_id(0) == 0)
        def _():
            dg_ref[...] = part

        @pl.when(pl.program_id(0) > 0)
        def _():
            dg_ref[...] += part

    row = pl.BlockSpec((tm, D), lambda i: (i, 0))
    vec = pl.BlockSpec((1, D), lambda i: (0, 0))
    return pl.pallas_call(
        body, name=name, grid=(S // tm,), in_specs=[row, vec, row, row], out_specs=[row, vec],
        out_shape=[jax.ShapeDtypeStruct((S, D), F32), jax.ShapeDtypeStruct((1, D), F32)],
        compiler_params=_cparams("arbitrary"),
    )(x, g, du, add)


def _final_loss(h, g, target, *, name, tm=512):
    S, D = h.shape

    def body(h_ref, g_ref, t_ref, dh_ref, dg_ref, loss_ref):
        hv, gv = h_ref[...], g_ref[...]
        r = lax.rsqrt(jnp.mean(hv * hv, axis=-1, keepdims=True) + NORM_EPS)
        xh = hv * r
        err = xh * gv - t_ref[...]
        part_loss = 0.5 * jnp.sum(jnp.mean(err * err, axis=-1, keepdims=True), axis=0, keepdims=True)
        dy = err * (1.0 / D)
        dxh = dy * gv
        dh_ref[...] = r * (dxh - xh * jnp.mean(dxh * xh, axis=-1, keepdims=True))
        part_g = jnp.sum(dy * xh, axis=0, keepdims=True)

        @pl.when(pl.program_id(0) == 0)
        def _():
            dg_ref[...] = part_g
            loss_ref[...] = jnp.broadcast_to(part_loss, (1, LANES))

        @pl.when(pl.program_id(0) > 0)
        def _():
            dg_ref[...] += part_g
            loss_ref[...] += jnp.broadcast_to(part_loss, (1, LANES))

    row = pl.BlockSpec((tm, D), lambda i: (i, 0))
    vec = pl.BlockSpec((1, D), lambda i: (0, 0))
    return pl.pallas_call(
        body, name=name, grid=(S // tm,), in_specs=[row, vec, row],
        out_specs=[row, vec, pl.BlockSpec((1, LANES), lambda i: (0, 0))],
        out_shape=[jax.ShapeDtypeStruct((S, D), F32), jax.ShapeDtypeStruct((1, D), F32),
                   jax.ShapeDtypeStruct((1, LANES), F32)],
        compiler_params=_cparams("arbitrary"),
    )(h, g, target)


def _attn_masks(n):
    qi = lax.broadcasted_iota(jnp.int32, (ATTN_BLOCK, 2 * ATTN_BLOCK), 0)
    kj = lax.broadcasted_iota(jnp.int32, (ATTN_BLOCK, 2 * ATTN_BLOCK), 1)
    dist = qi + ATTN_BLOCK - kj
    valid = (dist >= 0) & (dist <= ATTN_BLOCK) & jnp.logical_or(kj >= ATTN_BLOCK, n > 0)
    lane = lax.broadcasted_iota(jnp.int32, (1, LANES), 1)
    return valid, lane


def _head_value(x2, lane, e):
    return jnp.sum(jnp.where(lane == HEAD_DIM * e, x2, 0.0), axis=-1, keepdims=True)


def _attn_specs(d, W, nb):
    cur = pl.BlockSpec((ATTN_BLOCK, W), lambda r, n: (jnp.minimum(n, nb - 1), r))
    prev = pl.BlockSpec((ATTN_BLOCK, W), lambda r, n: (jnp.maximum(jnp.minimum(n, nb - 1) - 1, 0), r))
    return cur, prev


def _attn_fwd(q, k, v, d, *, name):
    R, DW = q.shape
    W = DW // d
    nb = R // ATTN_BLOCK

    def body(q_ref, kp_ref, kc_ref, vp_ref, vc_ref, o_ref, l_ref):
        valid, lane = _attn_masks(pl.program_id(1))
        for pr in range(W // LANES):
            sl = slice(LANES * pr, LANES * (pr + 1))
            q2 = q_ref[:, sl]
            k2 = jnp.concatenate([kp_ref[:, sl], kc_ref[:, sl]], axis=0)
            v2 = jnp.concatenate([vp_ref[:, sl], vc_ref[:, sl]], axis=0)
            o2 = jnp.zeros((ATTN_BLOCK, LANES), F32)
            l2 = jnp.zeros((ATTN_BLOCK, LANES), F32)
            for e in range(LANES // HEAD_DIM):
                mh = (lane >= HEAD_DIM * e) & (lane < HEAD_DIM * (e + 1))
                s = _nt(jnp.where(mh, q2, jnp.zeros_like(q2)), k2) * ATTN_SCALE
                s = jnp.where(valid, s, -jnp.inf)
                m = jnp.max(s, axis=-1, keepdims=True)
                p = jnp.exp(s - m)
                l = jnp.sum(p, axis=-1, keepdims=True)
                o = _nn(p.astype(BF16), v2) / l
                o2 = jnp.where(mh, o, o2)
                l2 = jnp.where(mh, m + jnp.log(l), l2)
            o_ref[:, sl] = o2
            l_ref[:, sl] = l2

    cur, prev = _attn_specs(d, W, nb)
    return pl.pallas_call(
        body, name=name, grid=(d, nb), in_specs=[cur, prev, cur, prev, cur], out_specs=[cur, cur],
        out_shape=[jax.ShapeDtypeStruct((R, DW), F32), jax.ShapeDtypeStruct((R, DW), F32)],
        compiler_params=_cparams("parallel", "parallel"),
    )(q, k, k, v, v)


def _attn_bwd(q, k, v, do, lse, delta, d, *, name):
    R, DW = q.shape
    W = DW // d
    nb = R // ATTN_BLOCK

    def body(q_ref, kp_ref, kc_ref, vp_ref, vc_ref, do_ref, lse_ref, dl_ref, dq_ref, dk_ref, dv_ref, ck_ref, cv_ref):
        n = pl.program_id(1)

        @pl.when(n == 0)
        def _():
            ck_ref[...] = jnp.zeros_like(ck_ref)
            cv_ref[...] = jnp.zeros_like(cv_ref)

        @pl.when(n < nb)
        def _():
            valid, lane = _attn_masks(n)
            for pr in range(W // LANES):
                sl = slice(LANES * pr, LANES * (pr + 1))
                q2 = q_ref[:, sl]
                k2 = jnp.concatenate([kp_ref[:, sl], kc_ref[:, sl]], axis=0)
                v2 = jnp.concatenate([vp_ref[:, sl], vc_ref[:, sl]], axis=0)
                do2, lse2, dl2 = do_ref[:, sl], lse_ref[:, sl], dl_ref[:, sl]
                dq2 = jnp.zeros((ATTN_BLOCK, LANES), F32)
                dkw = jnp.zeros((2 * ATTN_BLOCK, LANES), F32)
                dvw = jnp.zeros((2 * ATTN_BLOCK, LANES), F32)
                for e in range(LANES // HEAD_DIM):
                    mh = (lane >= HEAD_DIM * e) & (lane < HEAD_DIM * (e + 1))
                    qm = jnp.where(mh, q2, jnp.zeros_like(q2))
                    dom = jnp.where(mh, do2, 0.0).astype(BF16)
                    s = _nt(qm, k2) * ATTN_SCALE
                    p = jnp.where(valid, jnp.exp(s - _head_value(lse2, lane, e)), 0.0)
                    dp = _nt(dom, v2)
                    ds = (p * (dp - _head_value(dl2, lane, e)) * ATTN_SCALE).astype(BF16)
                    dq2 = dq2 + jnp.where(mh, _nn(ds, k2), 0.0)
                    dkw = dkw + _tn(ds, qm)
                    dvw = dvw + _tn(p.astype(BF16), dom)
                dq_ref[:, sl] = dq2
                dk_ref[:, sl] = ck_ref[:, sl] + dkw[:ATTN_BLOCK]
                dv_ref[:, sl] = cv_ref[:, sl] + dvw[:ATTN_BLOCK]
                ck_ref[:, sl] = dkw[ATTN_BLOCK:]
                cv_ref[:, sl] = dvw[ATTN_BLOCK:]

        @pl.when(n == nb)
        def _():
            dk_ref[...] = ck_ref[...]
            dv_ref[...] = cv_ref[...]

    cur, prev = _attn_specs(d, W, nb)
    done = pl.BlockSpec((ATTN_BLOCK, W), lambda r, n: (jnp.maximum(n - 1, 0), r))
    shape = jax.ShapeDtypeStruct((R, DW), F32)
    return pl.pallas_call(
        body, name=name, grid=(d, nb + 1), in_specs=[cur, prev, cur, prev, cur, cur, cur, cur],
        out_specs=[cur, done, done],
        out_shape=[shape, shape, shape],
        scratch_shapes=[pltpu.VMEM((ATTN_BLOCK, W), F32), pltpu.VMEM((ATTN_BLOCK, W), F32)],
        compiler_params=_cparams("arbitrary", "arbitrary"),
    )(q, k, k, v, v, do, lse, delta)


def _attn_merge(outs, lses, g, *, name, tm=256):
    S, W = outs[0].shape

    def body(o1, o2, o3, l1, l2, l3, g_ref, a_ref, lse_ref, an_ref):
        la, lb, lc = l1[...], l2[...], l3[...]
        m = jnp.maximum(jnp.maximum(la, lb), lc)
        wa, wb, wc = jnp.exp(la - m), jnp.exp(lb - m), jnp.exp(lc - m)
        den = wa + wb + wc
        attn = (wa * o1[...] + wb * o2[...] + wc * o3[...]) / den
        a_ref[...] = attn
        lse_ref[...] = m + jnp.log(den)
        r = lax.rsqrt(jnp.mean(attn * attn, axis=-1, keepdims=True) + NORM_EPS)
        an_ref[...] = (attn * r * g_ref[...]).astype(BF16)

    row = pl.BlockSpec((tm, W), lambda i: (i, 0))
    vec = pl.BlockSpec((1, W), lambda i: (0, 0))
    return pl.pallas_call(
        body, name=name, grid=(S // tm,), in_specs=[row] * 6 + [vec], out_specs=[row, row, row],
        out_shape=[jax.ShapeDtypeStruct((S, W), F32), jax.ShapeDtypeStruct((S, W), F32),
                   jax.ShapeDtypeStruct((S, W), BF16)],
        compiler_params=_cparams("parallel"),
    )(*outs, *lses, g)


def _attn_post_bwd(attn, g, dmix, *, name, tm=256):
    S, W = attn.shape

    def body(a_ref, g_ref, dm_ref, do_ref, dl_ref, dg_ref):
        av, duv = a_ref[...], dm_ref[...]
        r = lax.rsqrt(jnp.mean(av * av, axis=-1, keepdims=True) + NORM_EPS)
        xh = av * r
        dxh = duv * g_ref[...]
        dov = r * (dxh - xh * jnp.mean(dxh * xh, axis=-1, keepdims=True))
        do_ref[...] = dov
        shift = HEAD_DIM.bit_length() - 1
        hi = lax.shift_right_logical(lax.broadcasted_iota(jnp.int32, (W, W), 0), shift)
        hj = lax.shift_right_logical(lax.broadcasted_iota(jnp.int32, (W, W), 1), shift)
        dl_ref[...] = _nn(dov * av, jnp.where(hi == hj, 1.0, 0.0).astype(F32), HIGHEST)
        part = jnp.sum(duv * xh, axis=0, keepdims=True)

        @pl.when(pl.program_id(0) == 0)
        def _():
            dg_ref[...] = part

        @pl.when(pl.program_id(0) > 0)
        def _():
            dg_ref[...] += part

    row = pl.BlockSpec((tm, W), lambda i: (i, 0))
    vec = pl.BlockSpec((1, W), lambda i: (0, 0))
    return pl.pallas_call(
        body, name=name, grid=(S // tm,), in_specs=[row, vec, row], out_specs=[row, row, vec],
        out_shape=[jax.ShapeDtypeStruct((S, W), F32), jax.ShapeDtypeStruct((S, W), F32),
                   jax.ShapeDtypeStruct((1, W), F32)],
        compiler_params=_cparams("arbitrary"),
    )(attn, g, dmix)


HG_T = 2 * HGRN_CHUNK


def _hgrn_consts():
    row = lax.broadcasted_iota(jnp.int32, (HG_T, HG_T), 0)
    col = lax.broadcasted_iota(jnp.int32, (HG_T, HG_T), 1)
    same = (row >= HGRN_CHUNK) == (col >= HGRN_CHUNK)
    return row, same & (col <= row), same & (col >= row)


def _lower_bound(logits_ref):
    l0, l1 = logits_ref[0:1, :], logits_ref[1:2, :]
    mx = jnp.maximum(l0, l1)
    e0, e1 = jnp.exp(l0 - mx), jnp.exp(l1 - mx)
    return e0 / (e0 + e1)


def _hgrn_gates(q, f, lb, row, causal):
    C = HGRN_CHUNK
    sg = _sigmoid(f)
    forget = lb + (1.0 - lb) * sg
    key = 1.0 - forget
    logf = jnp.log(forget)
    b = _nn(jnp.where(causal, 1.0, 0.0).astype(F32), logf, HIGHEST)
    bend0 = jnp.sum(logf[:C], axis=0, keepdims=True)
    bend1 = jnp.sum(logf[C:], axis=0, keepdims=True)
    bend = jnp.where(row < C, bend0, bend1)
    eb, emb, eend = jnp.exp(b), jnp.exp(-b), jnp.exp(bend - b)
    sq = _sigmoid(q)
    return dict(sg=sg, forget=forget, key=key, bend0=bend0, bend1=bend1, eb=eb, emb=emb, eend=eend, sq=sq,
                qd=q * sq * eb, ki=key * emb, ke=key * eend)


def _hgrn_fwd(proj, logits, *, name):
    S = proj.shape[0]
    W = HGRN_HEADS * HGRN_DIM
    C = HGRN_CHUNK

    def body(q_ref, f_ref, i_ref, lg_ref, rec_ref, st_ref, s_ref):
        @pl.when(pl.program_id(0) == 0)
        def _():
            s_ref[...] = jnp.zeros_like(s_ref)

        row, causal, _ = _hgrn_consts()
        lb_all = _lower_bound(lg_ref)
        for h in range(HGRN_HEADS):
            sl = slice(HGRN_DIM * h, HGRN_DIM * (h + 1))
            gt = _hgrn_gates(q_ref[:, sl], f_ref[:, sl], lb_all[:, sl], row, causal)
            qd, ki, ke = gt["qd"].astype(BF16), gt["ki"].astype(BF16), gt["ke"].astype(BF16)
            iv = i_ref[:, sl].astype(BF16)
            a = jnp.where(causal, _nt(qd, ki), 0.0)
            o = _nn(a.astype(BF16), iv)
            s0 = s_ref[h]
            st_ref[0, h] = s0
            o0 = _nt(qd[:C], s0.astype(BF16))
            s1 = jnp.exp(gt["bend0"]) * s0 + _tn(iv[:C], ke[:C])
            st_ref[1, h] = s1
            o1 = _nt(qd[C:], s1.astype(BF16))
            s_ref[h] = jnp.exp(gt["bend1"]) * s1 + _tn(iv[C:], ke[C:])
            rec_ref[:, sl] = o + jnp.concatenate([o0, o1], axis=0)

    blk = lambda j: pl.BlockSpec((HG_T, W), lambda t: (t, j))
    return pl.pallas_call(
        body, name=name, grid=(S // HG_T,),
        in_specs=[blk(0), blk(1), blk(2), pl.BlockSpec((2, W), lambda t: (0, 0))],
        out_specs=[blk(0), pl.BlockSpec((2, HGRN_HEADS, HGRN_DIM, HGRN_DIM), lambda t: (t, 0, 0, 0))],
        out_shape=[jax.ShapeDtypeStruct((S, W), F32),
                   jax.ShapeDtypeStruct((S // C, HGRN_HEADS, HGRN_DIM, HGRN_DIM), F32)],
        scratch_shapes=[pltpu.VMEM((HGRN_HEADS, HGRN_DIM, HGRN_DIM), F32)],
        compiler_params=_cparams("arbitrary"),
    )(proj, proj, proj, logits)


def _hgrn_bwd(proj, logits, states, drec, *, name):
    S = proj.shape[0]
    W = HGRN_HEADS * HGRN_DIM
    C = HGRN_CHUNK
    nt = S // HG_T

    def body(q_ref, f_ref, i_ref, lg_ref, st_ref, do_ref, dq_ref, df_ref, di_ref, dlg_ref, ds_ref, dlb_ref):
        t = pl.program_id(0)

        @pl.when(t == 0)
        def _():
            ds_ref[...] = jnp.zeros_like(ds_ref)
            dlb_ref[...] = jnp.zeros_like(dlb_ref)

        row, causal, anti = _hgrn_consts()
        lb_all = _lower_bound(lg_ref)
        for h in range(HGRN_HEADS):
            sl = slice(HGRN_DIM * h, HGRN_DIM * (h + 1))
            q, lb = q_ref[:, sl], lb_all[:, sl]
            gt = _hgrn_gates(q, f_ref[:, sl], lb, row, causal)
            qd, ki, ke = gt["qd"], gt["ki"], gt["ke"]
            qdb, kib, keb = qd.astype(BF16), ki.astype(BF16), ke.astype(BF16)
            iv = i_ref[:, sl].astype(BF16)
            dob = do_ref[:, sl].astype(BF16)
            a = jnp.where(causal, _nt(qdb, kib), 0.0).astype(BF16)
            da = jnp.where(causal, _nt(dob, iv), 0.0).astype(BF16)
            s0, s1 = st_ref[0, h], st_ref[1, h]
            dec0, dec1 = jnp.exp(gt["bend0"]), jnp.exp(gt["bend1"])
            ds1 = ds_ref[h]
            ds1b = ds1.astype(BF16)
            dqd1 = _nn(dob[C:], s1.astype(BF16))
            di1 = _nt(keb[C:], ds1b)
            dke1 = _nn(iv[C:], ds1b)
            ddec1 = jnp.sum(ds1 * s1, axis=0, keepdims=True)
            ds0 = dec1 * ds1 + _tn(dob[C:], qdb[C:])
            ds0b = ds0.astype(BF16)
            dqd0 = _nn(dob[:C], s0.astype(BF16))
            di0 = _nt(keb[:C], ds0b)
            dke0 = _nn(iv[:C], ds0b)
            ddec0 = jnp.sum(ds0 * s0, axis=0, keepdims=True)
            ds_ref[h] = dec0 * ds0 + _tn(dob[:C], qdb[:C])

            dqd = _nn(da, kib) + jnp.concatenate([dqd0, dqd1], axis=0)
            dki = _tn(da, qdb)
            di_ref[:, sl] = (_tn(a, dob) + jnp.concatenate([di0, di1], axis=0)).astype(BF16)
            dke = jnp.concatenate([dke0, dke1], axis=0)
            gke = dke * ke
            dbend0 = jnp.sum(gke[:C], axis=0, keepdims=True) + ddec0 * dec0
            dbend1 = jnp.sum(gke[C:], axis=0, keepdims=True) + ddec1 * dec1
            db = dqd * qd - dki * ki - gke
            db = db + jnp.where(row == C - 1, dbend0, 0.0) + jnp.where(row == HG_T - 1, dbend1, 0.0)
            dlogf = _nn(jnp.where(anti, 1.0, 0.0).astype(F32), db, HIGHEST)
            dforget = dlogf / gt["forget"] - (dki * gt["emb"] + dke * gt["eend"])
            sg, sq = gt["sg"], gt["sq"]
            df_ref[:, sl] = (dforget * (1.0 - lb) * sg * (1.0 - sg)).astype(BF16)
            dlb_ref[:, sl] += jnp.sum(dforget * (1.0 - sg), axis=0, keepdims=True)
            dq_ref[:, sl] = (dqd * gt["eb"] * sq * (1.0 + q * (1.0 - sq))).astype(BF16)

        @pl.when(t == nt - 1)
        def _():
            dl0 = dlb_ref[...] * lb_all * (1.0 - lb_all)
            dlg_ref[0:1, :] = dl0
            dlg_ref[1:2, :] = -dl0

    blk = lambda j: pl.BlockSpec((HG_T, W), lambda t: (nt - 1 - t, j))
    full = pl.BlockSpec((2, W), lambda t: (0, 0))
    out = jax.ShapeDtypeStruct((S, W), BF16)
    return pl.pallas_call(
        body, name=name, grid=(nt,),
        in_specs=[blk(0), blk(1), blk(2), full,
                  pl.BlockSpec((2, HGRN_HEADS, HGRN_DIM, HGRN_DIM), lambda t: (nt - 1 - t, 0, 0, 0)), blk(0)],
        out_specs=[blk(0), blk(0), blk(0), full],
        out_shape=[out, out, out, jax.ShapeDtypeStruct((2, W), F32)],
        scratch_shapes=[pltpu.VMEM((HGRN_HEADS, HGRN_DIM, HGRN_DIM), F32), pltpu.VMEM((1, W), F32)],
        compiler_params=_cparams("arbitrary"),
    )(proj, proj, proj, logits, states, drec)


def _hgrn_post(rec, proj, g, *, name, tm=256):
    S, W = rec.shape

    def body(r_ref, hg_ref, g_ref, o_ref):
        for h in range(HGRN_HEADS):
            sl = slice(HGRN_DIM * h, HGRN_DIM * (h + 1))
            rv, hg = r_ref[:, sl], hg_ref[:, sl]
            r = lax.rsqrt(jnp.mean(rv * rv, axis=-1, keepdims=True) + NORM_EPS)
            o_ref[:, sl] = ((rv * r * g_ref[:, sl]) * (hg * _sigmoid(hg))).astype(BF16)

    return pl.pallas_call(
        body, name=name, grid=(S // tm,),
        in_specs=[pl.BlockSpec((tm, W), lambda i: (i, 0)), pl.BlockSpec((tm, W), lambda i: (i, 3)),
                  pl.BlockSpec((1, W), lambda i: (0, 0))],
        out_specs=pl.BlockSpec((tm, W), lambda i: (i, 0)), out_shape=jax.ShapeDtypeStruct((S, W), BF16),
        compiler_params=_cparams("parallel"),
    )(rec, proj, g)


def _hgrn_post_bwd(rec, proj, g, dmix, *, name, tm=256):
    S, W = rec.shape

    def body(r_ref, hg_ref, g_ref, dm_ref, dr_ref, dhg_ref, dg_ref):
        @pl.when(pl.program_id(0) == 0)
        def _():
            dg_ref[...] = jnp.zeros_like(dg_ref)

        for h in range(HGRN_HEADS):
            sl = slice(HGRN_DIM * h, HGRN_DIM * (h + 1))
            rv, hg, gv, dout = r_ref[:, sl], hg_ref[:, sl], g_ref[:, sl], dm_ref[:, sl]
            r = lax.rsqrt(jnp.mean(rv * rv, axis=-1, keepdims=True) + NORM_EPS)
            xh = rv * r
            sg = _sigmoid(hg)
            drn = dout * (hg * sg)
            dhg_ref[:, sl] = (dout * (xh * gv) * (sg * (1.0 + hg * (1.0 - sg)))).astype(BF16)
            dg_ref[:, sl] += jnp.sum(drn * xh, axis=0, keepdims=True)
            dxh = drn * gv
            dr_ref[:, sl] = r * (dxh - xh * jnp.mean(dxh * xh, axis=-1, keepdims=True))

    row = pl.BlockSpec((tm, W), lambda i: (i, 0))
    vec = pl.BlockSpec((1, W), lambda i: (0, 0))
    return pl.pallas_call(
        body, name=name, grid=(S // tm,),
        in_specs=[row, pl.BlockSpec((tm, W), lambda i: (i, 3)), vec, pl.BlockSpec((tm, W), lambda i: (i, 1))],
        out_specs=[row, row, vec],
        out_shape=[jax.ShapeDtypeStruct((S, W), F32), jax.ShapeDtypeStruct((S, W), BF16),
                   jax.ShapeDtypeStruct((1, W), F32)],
        compiler_params=_cparams("arbitrary"),
    )(rec, proj, g, dmix)


def _shift_down(x, row, k):
    return jnp.where(row >= k, pltpu.roll(x, k, 0), 0.0)


def _shift_up(x, row, k):
    n = x.shape[0]
    return jnp.where(row < n - k, pltpu.roll(x, n - k, 0), 0.0)


def _conv_pre(gate, w_ref, b_ref, row):
    g1, g2 = _shift_down(gate, row, 1), _shift_down(gate, row, 2)
    c = b_ref[...] + w_ref[0:1, :] * g2 + w_ref[1:2, :] * g1 + w_ref[2:3, :] * gate
    cdf = 0.5 * (1.0 + lax.erf(c * (2.0 ** -0.5)))
    return g1, g2, c, cdf


def _glu_specs(S, nf):
    gate = pl.BlockSpec((S, LANES), lambda j: (0, j))
    val = pl.BlockSpec((S, LANES), lambda j: (0, j + nf))
    w = pl.BlockSpec((3, LANES), lambda j: (0, j))
    b = pl.BlockSpec((1, LANES), lambda j: (0, j))
    return gate, val, w, b


def _glu_fwd(up, conv_w, conv_b, *, name):
    S, F2 = up.shape
    nf = F2 // 2 // LANES

    def body(g_ref, v_ref, w_ref, b_ref, o_ref):
        row = lax.broadcasted_iota(jnp.int32, (S, LANES), 0)
        _, _, c, cdf = _conv_pre(g_ref[...], w_ref, b_ref, row)
        o_ref[...] = (c * cdf * v_ref[...]).astype(BF16)

    gate, val, w, b = _glu_specs(S, nf)
    return pl.pallas_call(
        body, name=name, grid=(nf,), in_specs=[gate, val, w, b], out_specs=gate,
        out_shape=jax.ShapeDtypeStruct((S, F2 // 2), BF16), compiler_params=_cparams("parallel"),
    )(up, up, conv_w, conv_b)


def _glu_bwd(up, dact, conv_w, conv_b, *, name):
    S, F2 = up.shape
    nf = F2 // 2 // LANES

    def body(g_ref, v_ref, da_ref, w_ref, b_ref, dg_ref, dv_ref, dw_ref, db_ref):
        row = lax.broadcasted_iota(jnp.int32, (S, LANES), 0)
        gate, dav = g_ref[...], da_ref[...]
        g1, g2, c, cdf = _conv_pre(gate, w_ref, b_ref, row)
        dv_ref[...] = (dav * (c * cdf)).astype(BF16)
        pdf = jnp.exp(-0.5 * c * c) * (1.0 / (2.0 * jnp.pi) ** 0.5)
        dc = dav * v_ref[...] * (cdf + c * pdf)
        dg_ref[...] = (w_ref[2:3, :] * dc + w_ref[1:2, :] * _shift_up(dc, row, 1)
                       + w_ref[0:1, :] * _shift_up(dc, row, 2)).astype(BF16)
        dw_ref[0:1, :] = jnp.sum(dc * g2, axis=0, keepdims=True)
        dw_ref[1:2, :] = jnp.sum(dc * g1, axis=0, keepdims=True)
        dw_ref[2:3, :] = jnp.sum(dc * gate, axis=0, keepdims=True)
        db_ref[...] = jnp.sum(dc, axis=0, keepdims=True)

    gate, val, w, b = _glu_specs(S, nf)
    half = jax.ShapeDtypeStruct((S, F2 // 2), BF16)
    return pl.pallas_call(
        body, name=name, grid=(nf,), in_specs=[gate, val, gate, w, b], out_specs=[gate, gate, w, b],
        out_shape=[half, half, jax.ShapeDtypeStruct((3, F2 // 2), F32), jax.ShapeDtypeStruct((1, F2 // 2), F32)],
        compiler_params=_cparams("parallel"),
    )(up, up, dact, conv_w, conv_b)


def _rows(a):
    return a.reshape(-1, a.shape[-1])


def _row_tile(rows, cols, itemsize=4, budget=1 << 20):
    t = rows
    while t % 32 == 0 and t * cols * itemsize > budget:
        t //= 2
    return t


def _sum_cast(arrs, out_dtype, *, name):
    shape = arrs[0].shape
    flat = [_rows(a) for a in arrs]
    R, C = flat[0].shape
    tr = _row_tile(R, C)

    def body(*refs):
        acc = refs[0][...].astype(F32)
        for r in refs[1:-1]:
            acc = acc + r[...].astype(F32)
        refs[-1][...] = acc.astype(out_dtype)

    spec = pl.BlockSpec((tr, C), lambda i: (i, 0))
    return pl.pallas_call(
        body, name=name, grid=(R // tr,), in_specs=[spec] * len(flat), out_specs=spec,
        out_shape=jax.ShapeDtypeStruct((R, C), out_dtype), compiler_params=_cparams("parallel"),
    )(*flat).reshape(shape)


def _adamw(parts, w, m, v, *, name):
    shape = w.shape
    parts = [_rows(p) for p in parts]
    w2, m2, v2 = _rows(w), _rows(m), _rows(v)
    R, C = w2.shape
    tr = _row_tile(R, C)
    np_ = len(parts)
    c1, c2 = 1.0 - ADAM_B1 ** ADAM_STEP, 1.0 - ADAM_B2 ** ADAM_STEP

    def body(*refs):
        g = refs[0][...].astype(F32)
        for r in refs[1:np_]:
            g = g + r[...].astype(F32)
        w_ref, m_ref, v_ref, g_out, d_out, m_out, v_out = refs[np_:]
        mn = ADAM_B1 * m_ref[...] + (1.0 - ADAM_B1) * g
        vn = ADAM_B2 * v_ref[...] + (1.0 - ADAM_B2) * (g * g)
        g_out[...] = g
        d_out[...] = -ADAM_LR * ((mn / c1) / (jnp.sqrt(vn / c2) + ADAM_EPS) + ADAM_WD * w_ref[...])
        m_out[...] = mn
        v_out[...] = vn

    spec = pl.BlockSpec((tr, C), lambda i: (i, 0))
    out = jax.ShapeDtypeStruct((R, C), F32)
    res = pl.pallas_call(
        body, name=name, grid=(R // tr,), in_specs=[spec] * (np_ + 3), out_specs=[spec] * 4,
        out_shape=[out] * 4, compiler_params=_cparams("parallel"),
    )(*parts, w2, m2, v2)
    return [r.reshape(shape) for r in res]


def _coords():
    return lax.axis_index("x"), lax.axis_index("y"), lax.axis_index("c")


def _all_gather(shards, *, name):
    n = len(shards)

    def body(*refs):
        x_refs, out_refs = refs[:n], refs[n:2 * n]
        send_sems, recv_sems, local_sems = refs[2 * n:]
        x, y, c = _coords()
        me, sibling = (x, y, c), (x, y, 1 - c)
        chips = [(1 - x, y), (x, 1 - y), (1 - x, 1 - y)]

        def slot(a, dev):
            return out_refs[a].at[4 * dev[0] + 2 * dev[1] + dev[2]]

        def copy(a, k, block, to, src=None):
            return pltpu.make_async_remote_copy(
                src_ref=slot(a, block) if src is None else src, dst_ref=slot(a, block),
                send_sem=send_sems.at[7 * a + k], recv_sem=recv_sems.at[7 * a + k], device_id=to, device_id_type=MESH)

        mine = [pltpu.make_async_copy(x_refs[a], slot(a, me), local_sems.at[a]) for a in range(n)]
        for cp in mine:
            cp.start()
        first = []
        for a in range(n):
            first.append(copy(a, 0, me, sibling, src=x_refs[a]))
            first += [copy(a, 1 + j, me, (*chip, c), src=x_refs[a]) for j, chip in enumerate(chips)]
        for cp in first:
            cp.start()
        passed = []
        for j, chip in enumerate(chips):
            for a in range(n):
                copy(a, 1 + j, (*chip, c), me).wait_recv()
                fwd = copy(a, 4 + j, (*chip, c), sibling)
                fwd.start()
                passed.append(fwd)
        for a in range(n):
            copy(a, 0, sibling, me).wait_recv()
            for j, chip in enumerate(chips):
                copy(a, 4 + j, (*chip, 1 - c), me).wait_recv()
        for cp in first + passed:
            cp.wait_send()
        for cp in mine:
            cp.wait()

    return pl.pallas_call(
        body, name=name, in_specs=[HBM] * n, out_specs=[HBM] * n,
        out_shape=[jax.ShapeDtypeStruct((N_DEV, *s.shape), s.dtype) for s in shards],
        scratch_shapes=[pltpu.SemaphoreType.DMA((7 * n,)), pltpu.SemaphoreType.DMA((7 * n,)),
                        pltpu.SemaphoreType.DMA((n,))],
    )(*shards)


def _flip_c(x, y, c):
    return (x, y, 1 - c)


def _flip_y(x, y, c):
    return (x, 1 - y, c)


def _flip_x(x, y, c):
    return (1 - x, y, c)


def _flip_xy(x, y, c):
    return (1 - x, 1 - y, c)


def _exchange(arrs, rels, *, name):
    n, nr = len(arrs), len(rels)

    def body(*refs):
        in_refs, out_refs = refs[:n], refs[n:2 * n]
        send_sems, recv_sems = refs[2 * n:]
        x, y, c = _coords()
        copies = [
            pltpu.make_async_remote_copy(
                src_ref=in_refs[i].at[k], dst_ref=out_refs[i].at[k], send_sem=send_sems.at[nr * i + k],
                recv_sem=recv_sems.at[nr * i + k], device_id=rels[k](x, y, c), device_id_type=MESH)
            for i in range(n) for k in range(nr)]
        for cp in copies:
            cp.start()
        for cp in copies:
            cp.wait()

    return pl.pallas_call(
        body, name=name, in_specs=[HBM] * n, out_specs=[HBM] * n,
        out_shape=[jax.ShapeDtypeStruct(a.shape, a.dtype) for a in arrs],
        scratch_shapes=[pltpu.SemaphoreType.DMA((n * nr,)), pltpu.SemaphoreType.DMA((n * nr,))],
    )(*arrs)


def _reduce_scatter(grads):
    x, y, c = _coords()
    q = 2 * x + y
    keep, send = [], []
    for g in grads:
        g4 = g.reshape(4, 2, *g.shape[1:])
        keep.append(lax.dynamic_index_in_dim(g4, c, axis=1, keepdims=False))
        send.append(lax.dynamic_index_in_dim(g4, 1 - c, axis=1, keepdims=True).reshape(1, 4, *g.shape[1:]))
    got = _exchange(send, [_flip_c], name="rs_sibling")
    partial = [_sum_cast([k, r[0]], BF16, name=f"rs_chip_sum{i}") for i, (k, r) in enumerate(zip(keep, got))]
    out = [jnp.stack([lax.dynamic_index_in_dim(p, jnp.bitwise_xor(q, m), axis=0, keepdims=False) for m in (1, 2, 3)])
           for p in partial]
    got = _exchange(out, [_flip_y, _flip_x, _flip_xy], name="rs_chips")
    return [[lax.dynamic_index_in_dim(p, q, axis=0, keepdims=False), r[0], r[1], r[2]] for p, r in zip(partial, got)]


def _to_branch(a, d):
    return a.reshape(a.shape[0] // d, d * a.shape[1])


def _from_branch(a, d):
    return a.reshape(a.shape[0] * d, a.shape[1] // d)


def _by_device_cols(w):
    K, N = w.shape
    return w.reshape(K, N_DEV, N // N_DEV).transpose(1, 0, 2)


def _gathered_cols(w8):
    return w8.transpose(1, 0, 2).reshape(w8.shape[1], -1)


def kernel(x, norm1_g, w_in, attn_norm_g, hgrn_norm_g, hgrn_lb_logits, w_out, norm2_g, w_up, conv_w, conv_b, w_down, final_norm_g, loss_target, m_norm1_g, m_w_in, m_attn_norm_g, m_hgrn_norm_g, m_hgrn_lb_logits, m_w_out, m_norm2_g, m_w_up, m_conv_w, m_conv_b, m_w_down, m_final_norm_g, v_norm1_g, v_w_in, v_attn_norm_g, v_hgrn_norm_g, v_hgrn_lb_logits, v_w_out, v_norm2_g, v_w_up, v_conv_w, v_conv_b, v_w_down, v_final_norm_g):
    xs, target = x[0], loss_target[0]
    S, D = xs.shape
    AW = ATTN_HEADS * HEAD_DIM
    fng = final_norm_g.reshape(1, D)

    casts = [_sum_cast([w[0]], BF16, name=f"cast_{nm}") for nm, w in
             (("w_in", w_in), ("w_out", w_out), ("w_up", w_up), ("w_down", w_down))]
    g_in, g_out, g_up, g_down, g_cw = _all_gather(casts + [conv_w[0]], name="ag_weights")
    wi = _gathered_cols(g_in)
    wo = g_out.reshape(-1, D)
    wu = _gathered_cols(g_up)
    wd = g_down.reshape(-1, D)
    cw = _gathered_cols(g_cw)
    wi_a, wi_h = wi[:, :3 * AW], wi[:, 3 * AW:]

    u1 = _rms_fwd(xs, norm1_g, name="norm1")
    proj_a = _mm(u1, wi_a, out_dtype=BF16, name="proj_attn")
    proj_h = _mm(u1, wi_h, name="proj_hgrn")
    aq, ak, av = proj_a[:, :AW], proj_a[:, AW:2 * AW], proj_a[:, 2 * AW:]
    br = {d: tuple(_to_branch(t, d) for t in (aq, ak, av)) for d in DILATIONS}
    outs, lses = [], []
    for d in DILATIONS:
        o, l = _attn_fwd(*br[d], d, name=f"attn_fwd_d{d}")
        outs.append(_from_branch(o, d))
        lses.append(_from_branch(l, d))
    attn, lse, attn_n = _attn_merge(outs, lses, attn_norm_g, name="attn_merge")
    rec, states = _hgrn_fwd(proj_h, hgrn_lb_logits, name="hgrn_fwd")
    rec_g = _hgrn_post(rec, proj_h, hgrn_norm_g, name="hgrn_post")
    mixed = jnp.concatenate([attn_n, rec_g], axis=1)
    h1 = _mm(mixed, wo, add=xs, name="out_proj")
    u2 = _rms_fwd(h1, norm2_g, name="norm2")
    up = _mm(u2, wu, name="up_proj")
    act = _glu_fwd(up, cw, conv_b, name="glu_fwd")
    h2 = _mm(act, wd, add=h1, tk=1408, name="down_proj")
    dh2, d_fng, loss_part = _final_loss(h2, fng, target, name="final_loss")

    dact = _mm(dh2, wd, tb=True, name="d_act")
    dw_down = _mm(act, dh2, ta=True, out_dtype=BF16, name="dw_down")
    dgate, dval, d_cw, d_cb = _glu_bwd(up, dact, cw, conv_b, name="glu_bwd")
    dup = jnp.concatenate([dgate, dval], axis=1)
    du2 = _mm(dup, wu, tb=True, tk=1408, name="d_u2")
    dw_up = _mm(u2, dup, ta=True, out_dtype=BF16, name="dw_up")
    dh1, d_n2g = _rms_bwd(h1, norm2_g, du2, dh2, name="norm2_bwd")
    dmix = _mm(dh1, wo, tb=True, name="d_mixed")
    dw_out = _mm(mixed, dh1, ta=True, out_dtype=BF16, name="dw_out")
    drec, dhg, d_hng = _hgrn_post_bwd(rec, proj_h, hgrn_norm_g, dmix, name="hgrn_post_bwd")
    dq_h, df_h, di_h, d_lbl = _hgrn_bwd(proj_h, hgrn_lb_logits, states, drec, name="hgrn_bwd")
    dattn, delta, d_ang = _attn_post_bwd(attn, attn_norm_g, dmix, name="attn_post_bwd")
    dqs, dks, dvs = [], [], []
    for d in DILATIONS:
        dq, dk, dv = _attn_bwd(*br[d], _to_branch(dattn, d), _to_branch(lse, d), _to_branch(delta, d), d,
                               name=f"attn_bwd_d{d}")
        dqs.append(_from_branch(dq, d))
        dks.append(_from_branch(dk, d))
        dvs.append(_from_branch(dv, d))
    dproj = jnp.concatenate([_sum_cast(dqs, BF16, name="dq_sum"), _sum_cast(dks, BF16, name="dk_sum"),
                             _sum_cast(dvs, BF16, name="dv_sum"), dq_h, df_h, di_h, dhg], axis=1)
    du1 = _mm(dproj, wi, tb=True, tk=1792, name="d_u1")
    dw_in = _mm(u1, dproj, ta=True, out_dtype=BF16, name="dw_in")
    grad_x, d_n1g = _rms_bwd(xs, norm1_g, du1, dh1, name="norm1_bwd")

    big = _reduce_scatter([_by_device_cols(dw_in), dw_out.reshape(N_DEV, -1, D), _by_device_cols(dw_up),
                           dw_down.reshape(N_DEV, -1, D)])
    res = {}
    for nm, parts, w, m, v in (("w_in", big[0], w_in, m_w_in, v_w_in), ("w_out", big[1], w_out, m_w_out, v_w_out),
                               ("w_up", big[2], w_up, m_w_up, v_w_up), ("w_down", big[3], w_down, m_w_down, v_w_down)):
        res[nm] = _adamw([p.reshape(w.shape) for p in parts], w, m, v, name=f"adamw_{nm}")

    small = [("norm1_g", d_n1g, norm1_g, m_norm1_g, v_norm1_g),
             ("attn_norm_g", d_ang, attn_norm_g, m_attn_norm_g, v_attn_norm_g),
             ("hgrn_norm_g", d_hng, hgrn_norm_g, m_hgrn_norm_g, v_hgrn_norm_g),
             ("hgrn_lb_logits", d_lbl, hgrn_lb_logits, m_hgrn_lb_logits, v_hgrn_lb_logits),
             ("norm2_g", d_n2g, norm2_g, m_norm2_g, v_norm2_g),
             ("conv_b", d_cb, conv_b, m_conv_b, v_conv_b),
             ("final_norm_g", d_fng, final_norm_g, m_final_norm_g, v_final_norm_g)]
    pack = lambda arrs: jnp.concatenate([a.reshape(1, -1) for a in arrs], axis=1)
    g_small, g_dcw = _all_gather([pack([s[1] for s in small]), d_cw], name="ag_small_grads")
    sm = _adamw([g_small[j] for j in range(N_DEV)], pack([s[2] for s in small]), pack([s[3] for s in small]),
                pack([s[4] for s in small]), name="adamw_small")
    off = 0
    for nm, _, w, _, _ in small:
        res[nm] = [r[:, off:off + w.size].reshape(w.shape) for r in sm]
        off += w.size
    me = 4 * lax.axis_index("x") + 2 * lax.axis_index("y") + lax.axis_index("c")
    ncw = conv_w.shape[-1]
    mine_cw = lax.dynamic_slice_in_dim(g_dcw, me * ncw, ncw, axis=2)
    res["conv_w"] = _adamw([mine_cw[j].reshape(conv_w.shape) for j in range(N_DEV)], conv_w, m_conv_w, v_conv_w,
                           name="adamw_conv_w")

    loss = lax.psum(loss_part[0, 0], MESH_AXES)
    order = ["norm1_g", "w_in", "attn_norm_g", "hgrn_norm_g", "hgrn_lb_logits", "w_out", "norm2_g", "w_up",
             "conv_w", "conv_b", "w_down", "final_norm_g"]
    return (loss, grad_x[None], *[res[nm][0] for nm in order], *[res[nm][1] for nm in order],
            *[res[nm][2] for nm in order], *[res[nm][3] for nm in order])
```

```python
import jax
import jax.numpy as jnp
from jax import lax
from jax.experimental import pallas as pl
from jax.experimental.pallas import tpu as pltpu

F32, BF16 = jnp.float32, jnp.bfloat16
NORM_EPS = 1e-6
ATTN_HEADS, HEAD_DIM, ATTN_BLOCK = 8, 64, 128
DILATIONS = (1, 4, 16)
ATTN_SCALE = HEAD_DIM ** -0.5
ATTN_W = ATTN_HEADS * HEAD_DIM
HGRN_HEADS, HGRN_DIM, HGRN_CHUNK = 4, 128, 64
HGRN_W = HGRN_HEADS * HGRN_DIM
ADAM_LR, ADAM_B1, ADAM_B2, ADAM_EPS, ADAM_WD, ADAM_STEP = 0.001, 0.9, 0.999, 1e-08, 0.01, 10
LANES, SUBLANES = 128, 8
VMEM_LIMIT_BYTES = 56 * 1024 * 1024
N_DEV = 8
MESH_AXES = ("x", "y", "c")
MESH = pl.DeviceIdType.MESH
HBM = pl.BlockSpec(memory_space=pltpu.HBM)
HIGHEST = lax.Precision.HIGHEST


def _cparams(*sem):
    return pltpu.CompilerParams(dimension_semantics=sem, vmem_limit_bytes=VMEM_LIMIT_BYTES)


def _tile(n, pref):
    if n <= pref:
        return n
    t = (pref // LANES) * LANES
    while n % t:
        t -= LANES
    return t


def _resident(shape):
    return pl.BlockSpec(shape, lambda *_: (0,) * len(shape), pipeline_mode=pl.Buffered(1))


def _dot(a, b, dims, precision=None):
    return lax.dot_general(a, b, (dims, ((), ())), precision=precision, preferred_element_type=F32)


def _nn(a, b, precision=None):
    return _dot(a, b, ((1,), (0,)), precision)


def _nt(a, b):
    return _dot(a, b, ((1,), (1,)))


def _tn(a, b):
    return _dot(a, b, ((0,), (0,)))


def _sigmoid(x):
    return 1.0 / (1.0 + jnp.exp(-x))


def _rstd(x):
    return lax.rsqrt(jnp.mean(x * x, axis=-1, keepdims=True) + NORM_EPS)


def _norm_bwd(x, g, du):
    r = _rstd(x)
    xh = x * r
    dxh = du * g
    return r * (dxh - xh * jnp.mean(dxh * xh, axis=-1, keepdims=True)), du * xh


def _accumulate(ref, part, first):
    @pl.when(first)
    def _():
        ref[...] = part

    @pl.when(jnp.logical_not(first))
    def _():
        ref[...] += part


def _mm(a, b, *, name, out_dtype=F32, tm=1024, tn=512):
    M, K = a.shape
    N = b.shape[1]
    tm, tn = _tile(M, tm), _tile(N, tn)

    def body(a_ref, b_ref, o_ref):
        o_ref[...] = _nn(a_ref[...], b_ref[...]).astype(out_dtype)

    return pl.pallas_call(
        body, name=name, grid=(M // tm, N // tn),
        in_specs=[pl.BlockSpec((tm, K), lambda i, j: (i, 0)), pl.BlockSpec((K, tn), lambda i, j: (0, j))],
        out_specs=pl.BlockSpec((tm, tn), lambda i, j: (i, j)), out_shape=jax.ShapeDtypeStruct((M, N), out_dtype),
        compiler_params=_cparams("parallel", "parallel"),
    )(a, b)


def _mm_tn(x, dy, *, name, tm=512, tn=1024):
    S, M = x.shape
    N = dy.shape[1]
    tm, tn = _tile(M, tm), _tile(N, tn)

    def body(x_ref, dy_ref, o_ref, xt_ref):
        @pl.when(pl.program_id(1) == 0)
        def _():
            xt_ref[...] = x_ref[...].T

        o_ref[...] = _nn(xt_ref[...], dy_ref[...]).astype(BF16)

    return pl.pallas_call(
        body, name=name, grid=(M // tm, N // tn),
        in_specs=[pl.BlockSpec((S, tm), lambda i, j: (0, i)), pl.BlockSpec((S, tn), lambda i, j: (0, j))],
        out_specs=pl.BlockSpec((tm, tn), lambda i, j: (i, j)), out_shape=jax.ShapeDtypeStruct((M, N), BF16),
        scratch_shapes=[pltpu.VMEM((tm, S), BF16)], compiler_params=_cparams("parallel", "arbitrary"),
    )(x, dy)


def _proj_attn(x, g, w, *, name, tm=1024, tn=512):
    S, D = x.shape
    N = w.shape[1]

    def body(x_ref, g_ref, w_ref, u_ref, o_ref):
        @pl.when(pl.program_id(1) == 0)
        def _():
            xv = x_ref[...]
            u_ref[...] = (xv * _rstd(xv) * g_ref[...]).astype(BF16)

        o_ref[...] = _nn(u_ref[...], w_ref[...]).astype(BF16)

    return pl.pallas_call(
        body, name=name, grid=(S // tm, N // tn),
        in_specs=[pl.BlockSpec((tm, D), lambda i, j: (i, 0)), pl.BlockSpec((1, D), lambda i, j: (0, 0)),
                  pl.BlockSpec((D, tn), lambda i, j: (0, j))],
        out_specs=[pl.BlockSpec((tm, D), lambda i, j: (i, 0)), pl.BlockSpec((tm, tn), lambda i, j: (i, j))],
        out_shape=[jax.ShapeDtypeStruct((S, D), BF16), jax.ShapeDtypeStruct((S, N), BF16)],
        compiler_params=_cparams("parallel", "arbitrary"),
    )(x, g, w)


PAIR_W = 3 * LANES


def _attn_masks(n):
    qi = lax.broadcasted_iota(jnp.int32, (ATTN_BLOCK, 2 * ATTN_BLOCK), 0)
    kj = lax.broadcasted_iota(jnp.int32, (ATTN_BLOCK, 2 * ATTN_BLOCK), 1)
    dist = qi + ATTN_BLOCK - kj
    valid = (dist >= 0) & (dist <= ATTN_BLOCK) & jnp.logical_or(kj >= ATTN_BLOCK, n > 0)
    lane = lax.broadcasted_iota(jnp.int32, (1, LANES), 1)
    return valid, lane


def _head_value(x2, lane, e):
    return jnp.sum(jnp.where(lane == HEAD_DIM * e, x2, 0.0), axis=-1, keepdims=True)


def _block_rows(ref, base, r, d):
    if d == 1:
        return ref[pl.ds(base, ATTN_BLOCK), :]
    return ref.at[pl.ds(base, ATTN_BLOCK * d)][pl.ds(r, ATTN_BLOCK, stride=d), :]


def _set_block_rows(ref, base, r, d, val):
    if d == 1:
        ref[pl.ds(base, ATTN_BLOCK), :] = val
    else:
        ref.at[pl.ds(base, ATTN_BLOCK * d)][pl.ds(r, ATTN_BLOCK, stride=d), :] = val


def _window(ref, pbase, base, r, d):
    return jnp.concatenate([_block_rows(ref, pbase, r, d), _block_rows(ref, base, r, d)], axis=0).astype(BF16)


def _split_pair(p_ref, qs, ks, vs):
    qs[...] = p_ref[:, 0:LANES].astype(F32)
    ks[...] = p_ref[:, LANES:2 * LANES].astype(F32)
    vs[...] = p_ref[:, 2 * LANES:3 * LANES].astype(F32)


def _for_blocks(S, d, fn):
    span = ATTN_BLOCK * d

    def step(n, carry):
        base = pl.multiple_of(n * span, span)
        pbase = pl.multiple_of(jnp.maximum(n - 1, 0) * span, span)
        for r in range(d):
            fn(n, base, pbase, r)
        return carry

    lax.fori_loop(0, S // span, step, 0)


def _attn_fwd(proj_a, *, name):
    S = proj_a.shape[0]

    def body(p_ref, o_ref, l_ref, qs, ks, vs):
        _split_pair(p_ref, qs, ks, vs)
        for d in DILATIONS:
            def block(n, base, pbase, r, d=d):
                valid, lane = _attn_masks(n)
                q2 = _block_rows(qs, base, r, d).astype(BF16)
                k2, v2 = _window(ks, pbase, base, r, d), _window(vs, pbase, base, r, d)
                o2 = jnp.zeros((ATTN_BLOCK, LANES), F32)
                l2 = jnp.zeros((ATTN_BLOCK, LANES), F32)
                for e in range(LANES // HEAD_DIM):
                    mh = (lane >= HEAD_DIM * e) & (lane < HEAD_DIM * (e + 1))
                    s = _nt(jnp.where(mh, q2, jnp.zeros_like(q2)), k2) * ATTN_SCALE
                    s = jnp.where(valid, s, -jnp.inf)
                    m = jnp.max(s, axis=-1, keepdims=True)
                    p = jnp.exp(s - m)
                    l = jnp.sum(p, axis=-1, keepdims=True)
                    o2 = jnp.where(mh, _nn(p.astype(BF16), v2) / l, o2)
                    l2 = jnp.where(mh, m + jnp.log(l), l2)
                if d != DILATIONS[0]:
                    lo, oo = _block_rows(l_ref, base, r, d), _block_rows(o_ref, base, r, d)
                    ln = jnp.maximum(lo, l2)
                    wa, wb = jnp.exp(lo - ln), jnp.exp(l2 - ln)
                    o2 = (wa * oo + wb * o2) / (wa + wb)
                    l2 = ln + jnp.log(wa + wb)
                _set_block_rows(o_ref, base, r, d, o2)
                _set_block_rows(l_ref, base, r, d, l2)

            _for_blocks(S, d, block)

    slab = pl.BlockSpec((S, LANES), lambda p: (0, p))
    return pl.pallas_call(
        body, name=name, grid=(ATTN_W // LANES,), in_specs=[pl.BlockSpec((S, PAIR_W), lambda p: (0, p))],
        out_specs=[slab, slab],
        out_shape=[jax.ShapeDtypeStruct((S, ATTN_W), F32), jax.ShapeDtypeStruct((S, ATTN_W), F32)],
        scratch_shapes=[pltpu.VMEM((S, LANES), F32)] * 3, compiler_params=_cparams("parallel"),
    )(proj_a)


def _attn_bwd(proj_a, do, lse, delta, *, name):
    S = proj_a.shape[0]

    def body(p_ref, do_ref, lse_ref, dl_ref, o_ref, qs, ks, vs, dqs, dks, dvs):
        _split_pair(p_ref, qs, ks, vs)
        for acc in (dqs, dks, dvs):
            acc[...] = jnp.zeros_like(acc)
        for d in DILATIONS:
            def block(n, base, pbase, r, d=d):
                valid, lane = _attn_masks(n)
                q2 = _block_rows(qs, base, r, d).astype(BF16)
                k2, v2 = _window(ks, pbase, base, r, d), _window(vs, pbase, base, r, d)
                do2, lse2, dl2 = (_block_rows(ref, base, r, d) for ref in (do_ref, lse_ref, dl_ref))
                dq2 = jnp.zeros((ATTN_BLOCK, LANES), F32)
                dkw = jnp.zeros((2 * ATTN_BLOCK, LANES), F32)
                dvw = jnp.zeros((2 * ATTN_BLOCK, LANES), F32)
                for e in range(LANES // HEAD_DIM):
                    mh = (lane >= HEAD_DIM * e) & (lane < HEAD_DIM * (e + 1))
                    qm = jnp.where(mh, q2, jnp.zeros_like(q2))
                    dom = jnp.where(mh, do2, 0.0).astype(BF16)
                    s = _nt(qm, k2) * ATTN_SCALE
                    p = jnp.where(valid, jnp.exp(s - _head_value(lse2, lane, e)), 0.0)
                    dp = _nt(dom, v2)
                    ds = (p * (dp - _head_value(dl2, lane, e)) * ATTN_SCALE).astype(BF16)
                    dq2 = dq2 + jnp.where(mh, _nn(ds, k2), 0.0)
                    dkw = dkw + _tn(ds, qm)
                    dvw = dvw + _tn(p.astype(BF16), dom)
                _set_block_rows(dqs, base, r, d, _block_rows(dqs, base, r, d) + dq2)
                for acc, win in ((dks, dkw), (dvs, dvw)):
                    _set_block_rows(acc, pbase, r, d, _block_rows(acc, pbase, r, d) + win[:ATTN_BLOCK])
                    _set_block_rows(acc, base, r, d, _block_rows(acc, base, r, d) + win[ATTN_BLOCK:])

            _for_blocks(S, d, block)
        o_ref[:, 0:LANES] = dqs[...].astype(BF16)
        o_ref[:, LANES:2 * LANES] = dks[...].astype(BF16)
        o_ref[:, 2 * LANES:3 * LANES] = dvs[...].astype(BF16)

    slab = pl.BlockSpec((S, LANES), lambda p: (0, p))
    pair = pl.BlockSpec((S, PAIR_W), lambda p: (0, p))
    return pl.pallas_call(
        body, name=name, grid=(ATTN_W // LANES,), in_specs=[pair, slab, slab, slab], out_specs=pair,
        out_shape=jax.ShapeDtypeStruct(proj_a.shape, BF16),
        scratch_shapes=[pltpu.VMEM((S, LANES), F32)] * 6, compiler_params=_cparams("parallel"),
    )(proj_a, do, lse, delta)


HG_T = 2 * HGRN_CHUNK


def _hgrn_consts():
    row = lax.broadcasted_iota(jnp.int32, (HG_T, HG_T), 0)
    col = lax.broadcasted_iota(jnp.int32, (HG_T, HG_T), 1)
    same = (row >= HGRN_CHUNK) == (col >= HGRN_CHUNK)
    return row, same & (col <= row), same & (col >= row)


def _lower_bound(logits_ref):
    l0, l1 = logits_ref[0:1, :], logits_ref[1:2, :]
    mx = jnp.maximum(l0, l1)
    e0, e1 = jnp.exp(l0 - mx), jnp.exp(l1 - mx)
    return e0 / (e0 + e1)


def _hgrn_gates(q, f, lb, row, causal):
    C = HGRN_CHUNK
    sg = _sigmoid(f)
    forget = lb + (1.0 - lb) * sg
    key = 1.0 - forget
    logf = jnp.log(forget)
    b = _nn(jnp.where(causal, 1.0, 0.0).astype(F32), logf, HIGHEST)
    bend0 = jnp.sum(logf[:C], axis=0, keepdims=True)
    bend1 = jnp.sum(logf[C:], axis=0, keepdims=True)
    bend = jnp.where(row < C, bend0, bend1)
    eb, emb, eend = jnp.exp(b), jnp.exp(-b), jnp.exp(bend - b)
    sq = _sigmoid(q)
    return dict(sg=sg, forget=forget, key=key, bend0=bend0, bend1=bend1, eb=eb, emb=emb, eend=eend, sq=sq,
                qd=q * sq * eb, ki=key * emb, ke=key * eend)


def _hgrn_fwd(proj, logits, *, name):
    S = proj.shape[0]
    W, C = HGRN_W, HGRN_CHUNK

    def body(q_ref, f_ref, i_ref, lg_ref, rec_ref, st_ref, s_ref):
        @pl.when(pl.program_id(0) == 0)
        def _():
            s_ref[...] = jnp.zeros_like(s_ref)

        row, causal, _ = _hgrn_consts()
        lb_all = _lower_bound(lg_ref)
        for h in range(HGRN_HEADS):
            sl = slice(HGRN_DIM * h, HGRN_DIM * (h + 1))
            gt = _hgrn_gates(q_ref[:, sl], f_ref[:, sl], lb_all[:, sl], row, causal)
            qd, ki, ke = gt["qd"].astype(BF16), gt["ki"].astype(BF16), gt["ke"].astype(BF16)
            iv = i_ref[:, sl].astype(BF16)
            a = jnp.where(causal, _nt(qd, ki), 0.0)
            o = _nn(a.astype(BF16), iv)
            s0 = s_ref[h]
            st_ref[0, h] = s0
            o0 = _nt(qd[:C], s0.astype(BF16))
            s1 = jnp.exp(gt["bend0"]) * s0 + _tn(iv[:C], ke[:C])
            st_ref[1, h] = s1
            o1 = _nt(qd[C:], s1.astype(BF16))
            s_ref[h] = jnp.exp(gt["bend1"]) * s1 + _tn(iv[C:], ke[C:])
            rec_ref[:, sl] = o + jnp.concatenate([o0, o1], axis=0)

    blk = lambda j: pl.BlockSpec((HG_T, W), lambda t: (t, j))
    return pl.pallas_call(
        body, name=name, grid=(S // HG_T,),
        in_specs=[blk(0), blk(1), blk(2), pl.BlockSpec((2, W), lambda t: (0, 0))],
        out_specs=[blk(0), pl.BlockSpec((2, HGRN_HEADS, HGRN_DIM, HGRN_DIM), lambda t: (t, 0, 0, 0))],
        out_shape=[jax.ShapeDtypeStruct((S, W), F32),
                   jax.ShapeDtypeStruct((S // C, HGRN_HEADS, HGRN_DIM, HGRN_DIM), F32)],
        scratch_shapes=[pltpu.VMEM((HGRN_HEADS, HGRN_DIM, HGRN_DIM), F32)],
        compiler_params=_cparams("arbitrary"),
    )(proj, proj, proj, logits)


def _hgrn_bwd(proj, logits, states, drec, *, name):
    S = proj.shape[0]
    W, C = HGRN_W, HGRN_CHUNK
    nt = S // HG_T

    def body(q_ref, f_ref, i_ref, lg_ref, st_ref, do_ref, dp_ref, dlg_ref, ds_ref, dlb_ref):
        t = pl.program_id(0)

        @pl.when(t == 0)
        def _():
            ds_ref[...] = jnp.zeros_like(ds_ref)
            dlb_ref[...] = jnp.zeros_like(dlb_ref)

        row, causal, anti = _hgrn_consts()
        lb_all = _lower_bound(lg_ref)
        for h in range(HGRN_HEADS):
            sl = slice(HGRN_DIM * h, HGRN_DIM * (h + 1))
            q, lb = q_ref[:, sl], lb_all[:, sl]
            gt = _hgrn_gates(q, f_ref[:, sl], lb, row, causal)
            qd, ki, ke = gt["qd"], gt["ki"], gt["ke"]
            qdb, kib, keb = qd.astype(BF16), ki.astype(BF16), ke.astype(BF16)
            iv = i_ref[:, sl].astype(BF16)
            dob = do_ref[:, sl].astype(BF16)
            a = jnp.where(causal, _nt(qdb, kib), 0.0).astype(BF16)
            da = jnp.where(causal, _nt(dob, iv), 0.0).astype(BF16)
            s0, s1 = st_ref[0, h], st_ref[1, h]
            dec0, dec1 = jnp.exp(gt["bend0"]), jnp.exp(gt["bend1"])
            ds1 = ds_ref[h]
            ds1b = ds1.astype(BF16)
            dqd1 = _nn(dob[C:], s1.astype(BF16))
            di1 = _nt(keb[C:], ds1b)
            dke1 = _nn(iv[C:], ds1b)
            ddec1 = jnp.sum(ds1 * s1, axis=0, keepdims=True)
            ds0 = dec1 * ds1 + _tn(dob[C:], qdb[C:])
            ds0b = ds0.astype(BF16)
            dqd0 = _nn(dob[:C], s0.astype(BF16))
            di0 = _nt(keb[:C], ds0b)
            dke0 = _nn(iv[:C], ds0b)
            ddec0 = jnp.sum(ds0 * s0, axis=0, keepdims=True)
            ds_ref[h] = dec0 * ds0 + _tn(dob[:C], qdb[:C])

            dqd = _nn(da, kib) + jnp.concatenate([dqd0, dqd1], axis=0)
            dki = _tn(da, qdb)
            dp_ref[:, 2 * W + HGRN_DIM * h:2 * W + HGRN_DIM * (h + 1)] = (
                _tn(a, dob) + jnp.concatenate([di0, di1], axis=0)).astype(BF16)
            dke = jnp.concatenate([dke0, dke1], axis=0)
            gke = dke * ke
            dbend0 = jnp.sum(gke[:C], axis=0, keepdims=True) + ddec0 * dec0
            dbend1 = jnp.sum(gke[C:], axis=0, keepdims=True) + ddec1 * dec1
            db = dqd * qd - dki * ki - gke
            db = db + jnp.where(row == C - 1, dbend0, 0.0) + jnp.where(row == HG_T - 1, dbend1, 0.0)
            dlogf = _nn(jnp.where(anti, 1.0, 0.0).astype(F32), db, HIGHEST)
            dforget = dlogf / gt["forget"] - (dki * gt["emb"] + dke * gt["eend"])
            sg, sq = gt["sg"], gt["sq"]
            dp_ref[:, W + HGRN_DIM * h:W + HGRN_DIM * (h + 1)] = (dforget * (1.0 - lb) * sg * (1.0 - sg)).astype(BF16)
            dlb_ref[:, sl] += jnp.sum(dforget * (1.0 - sg), axis=0, keepdims=True)
            dp_ref[:, sl] = (dqd * gt["eb"] * sq * (1.0 + q * (1.0 - sq))).astype(BF16)

        @pl.when(t == nt - 1)
        def _():
            dl0 = dlb_ref[...] * lb_all * (1.0 - lb_all)
            dlg_ref[0:1, :] = dl0
            dlg_ref[1:2, :] = -dl0

    blk = lambda j: pl.BlockSpec((HG_T, W), lambda t: (nt - 1 - t, j))
    full = pl.BlockSpec((2, W), lambda t: (0, 0))
    return pl.pallas_call(
        body, name=name, grid=(nt,),
        in_specs=[blk(0), blk(1), blk(2), full,
                  pl.BlockSpec((2, HGRN_HEADS, HGRN_DIM, HGRN_DIM), lambda t: (nt - 1 - t, 0, 0, 0)), blk(0)],
        out_specs=[pl.BlockSpec((HG_T, 3 * W), lambda t: (nt - 1 - t, 0)), full],
        out_shape=[jax.ShapeDtypeStruct((S, 3 * W), BF16), jax.ShapeDtypeStruct((2, W), F32)],
        scratch_shapes=[pltpu.VMEM((HGRN_HEADS, HGRN_DIM, HGRN_DIM), F32), pltpu.VMEM((1, W), F32)],
        compiler_params=_cparams("arbitrary"),
    )(proj, proj, proj, logits, states, drec)


def _out_proj(attn, rec, proj_h, x, g_attn, g_hgrn, g_norm2, w_out, *, name, tm=512):
    S, D = x.shape
    AW, W = ATTN_W, HGRN_W

    def body(a_ref, r_ref, hg_ref, x_ref, ga_ref, gh_ref, g2_ref, w_ref, h_ref, u_ref, m_ref):
        av = a_ref[...]
        m_ref[:, :AW] = (av * _rstd(av) * ga_ref[...]).astype(BF16)
        for h in range(HGRN_HEADS):
            sl = slice(HGRN_DIM * h, HGRN_DIM * (h + 1))
            rv, hg = r_ref[:, sl], hg_ref[:, sl]
            m_ref[:, AW + HGRN_DIM * h:AW + HGRN_DIM * (h + 1)] = (
                (rv * _rstd(rv) * gh_ref[:, sl]) * (hg * _sigmoid(hg))).astype(BF16)
        h1 = x_ref[...] + _nn(m_ref[...], w_ref[...])
        h_ref[...] = h1
        u_ref[...] = (h1 * _rstd(h1) * g2_ref[...]).astype(BF16)

    row = lambda w, j=0: pl.BlockSpec((tm, w), lambda i: (i, j))
    vec = lambda w: pl.BlockSpec((1, w), lambda i: (0, 0))
    return pl.pallas_call(
        body, name=name, grid=(S // tm,),
        in_specs=[row(AW), row(W), row(W, 3), row(D), vec(AW), vec(W), vec(D), _resident(w_out.shape)],
        out_specs=[row(D), row(D), row(AW + W)],
        out_shape=[jax.ShapeDtypeStruct((S, D), F32), jax.ShapeDtypeStruct((S, D), BF16),
                   jax.ShapeDtypeStruct((S, AW + W), BF16)],
        compiler_params=_cparams("parallel"),
    )(attn, rec, proj_h, x, g_attn, g_hgrn, g_norm2, w_out)


def _dmix_post_bwd(dh1b, w_out, attn, rec, proj_h, g_attn, g_hgrn, *, name, tm=512):
    S, D = dh1b.shape
    AW, W = ATTN_W, HGRN_W

    def body(dh_ref, w_ref, a_ref, r_ref, hg_ref, ga_ref, gh_ref, do_ref, dl_ref, dr_ref, dhg_ref, dga_ref, dgh_ref):
        first = pl.program_id(0) == 0
        dmix = _nt(dh_ref[...], w_ref[...])
        av = a_ref[...]
        dov, dga = _norm_bwd(av, ga_ref[...], dmix[:, :AW])
        do_ref[...] = dov
        shift = HEAD_DIM.bit_length() - 1
        hi = lax.shift_right_logical(lax.broadcasted_iota(jnp.int32, (AW, AW), 0), shift)
        hj = lax.shift_right_logical(lax.broadcasted_iota(jnp.int32, (AW, AW), 1), shift)
        dl_ref[...] = _nn(dov * av, jnp.where(hi == hj, 1.0, 0.0).astype(F32), HIGHEST)
        _accumulate(dga_ref, jnp.sum(dga, axis=0, keepdims=True), first)

        @pl.when(first)
        def _():
            dgh_ref[...] = jnp.zeros_like(dgh_ref)

        for h in range(HGRN_HEADS):
            sl = slice(HGRN_DIM * h, HGRN_DIM * (h + 1))
            rv, hg, gv = r_ref[:, sl], hg_ref[:, sl], gh_ref[:, sl]
            dout = dmix[:, AW + HGRN_DIM * h:AW + HGRN_DIM * (h + 1)]
            sg = _sigmoid(hg)
            drv, dgh = _norm_bwd(rv, gv, dout * (hg * sg))
            dr_ref[:, sl] = drv
            dgh_ref[:, sl] += jnp.sum(dgh, axis=0, keepdims=True)
            dhg_ref[:, sl] = (dout * (rv * _rstd(rv) * gv) * (sg * (1.0 + hg * (1.0 - sg)))).astype(BF16)

    row = lambda w, j=0: pl.BlockSpec((tm, w), lambda i: (i, j))
    vec = lambda w: pl.BlockSpec((1, w), lambda i: (0, 0))
    return pl.pallas_call(
        body, name=name, grid=(S // tm,),
        in_specs=[row(D), _resident(w_out.shape), row(AW), row(W), row(W, 3), vec(AW), vec(W)],
        out_specs=[row(AW), row(AW), row(W), row(W), vec(AW), vec(W)],
        out_shape=[jax.ShapeDtypeStruct((S, AW), F32), jax.ShapeDtypeStruct((S, AW), F32),
                   jax.ShapeDtypeStruct((S, W), F32), jax.ShapeDtypeStruct((S, W), BF16),
                   jax.ShapeDtypeStruct((1, AW), F32), jax.ShapeDtypeStruct((1, W), F32)],
        compiler_params=_cparams("arbitrary"),
    )(dh1b, w_out, attn, rec, proj_h, g_attn, g_hgrn)


def _conv_act(g, g1, g2, w_ref, b_ref):
    c = b_ref[...] + w_ref[0:1, :] * g2 + w_ref[1:2, :] * g1 + w_ref[2:3, :] * g
    return c, 0.5 * (1.0 + lax.erf(c * (2.0 ** -0.5)))


def _shift_down(g, halo, row):
    g1 = jnp.where(row == 0, halo[7:8], pltpu.roll(g, 1, 0))
    g2 = jnp.where(row == 0, halo[6:7], jnp.where(row == 1, halo[7:8], pltpu.roll(g, 2, 0)))
    return g1, g2


def _shift_up(x, halo, row):
    n = x.shape[0]
    x1 = jnp.where(row == n - 1, halo[0:1], pltpu.roll(x, n - 1, 0))
    x2 = jnp.where(row == n - 2, halo[0:1], jnp.where(row == n - 1, halo[1:2], pltpu.roll(x, n - 2, 0)))
    return x1, x2


def _up_glu(u, w_up, conv_w, conv_b, *, name, tm=1024, tn=256):
    S, D = u.shape
    F = w_up.shape[1] // 2
    nf = F // tn

    def body(u_ref, wg_ref, wv_ref, cw_ref, cb_ref, g_ref, v_ref, a_ref, halo_ref):
        i, j = pl.program_id(0), pl.program_id(1)

        @pl.when(i == 0)
        def _():
            halo_ref[j] = jnp.zeros((SUBLANES, tn), F32)

        uv = u_ref[...]
        g, v = _nn(uv, wg_ref[...]), _nn(uv, wv_ref[...])
        row = lax.broadcasted_iota(jnp.int32, (tm, tn), 0)
        g1, g2 = _shift_down(g, halo_ref[j], row)
        c, cdf = _conv_act(g, g1, g2, cw_ref, cb_ref)
        a_ref[...] = (c * cdf * v).astype(BF16)
        g_ref[...] = g.astype(BF16)
        v_ref[...] = v.astype(BF16)
        halo_ref[j] = g[tm - SUBLANES:, :]

    col = pl.BlockSpec((tm, tn), lambda i, j: (i, j))
    out = jax.ShapeDtypeStruct((S, F), BF16)
    return pl.pallas_call(
        body, name=name, grid=(S // tm, nf),
        in_specs=[pl.BlockSpec((tm, D), lambda i, j: (i, 0)), pl.BlockSpec((D, tn), lambda i, j: (0, j)),
                  pl.BlockSpec((D, tn), lambda i, j: (0, j + nf)), pl.BlockSpec((3, tn), lambda i, j: (0, j)),
                  pl.BlockSpec((1, tn), lambda i, j: (0, j))],
        out_specs=[col, col, col], out_shape=[out, out, out],
        scratch_shapes=[pltpu.VMEM((nf, SUBLANES, tn), F32)], compiler_params=_cparams("arbitrary", "arbitrary"),
    )(u, w_up, w_up, conv_w, conv_b)


def _dact_glu_bwd(dh2b, w_down, gate, val, conv_w, conv_b, *, name, tm=1024, tn=256):
    S, D = dh2b.shape
    F = gate.shape[1]
    nf, ni = F // tn, S // tm
    hb = tm // SUBLANES

    def body(dh_ref, wd_ref, g_ref, gh_ref, v_ref, cw_ref, cb_ref, dg_ref, dv_ref, dcw_ref, dcb_ref, halo_ref, acc_ref):
        i, j = pl.program_id(0), pl.program_id(1)

        @pl.when(i == 0)
        def _():
            halo_ref[j] = jnp.zeros((SUBLANES, tn), F32)
            acc_ref[j] = jnp.zeros((SUBLANES, tn), F32)

        g = g_ref[...].astype(F32)
        before = jnp.where(i < ni - 1, gh_ref[...].astype(F32), 0.0)
        row = lax.broadcasted_iota(jnp.int32, (tm, tn), 0)
        g1, g2 = _shift_down(g, before[SUBLANES:], row)
        c, cdf = _conv_act(g, g1, g2, cw_ref, cb_ref)
        da = _nt(dh_ref[...], wd_ref[...])
        dv_ref[...] = (da * (c * cdf)).astype(BF16)
        pdf = jnp.exp(-0.5 * c * c) * (1.0 / (2.0 * jnp.pi) ** 0.5)
        dc = da * v_ref[...].astype(F32) * (cdf + c * pdf)
        d1, d2 = _shift_up(dc, halo_ref[j], row)
        dg_ref[...] = (cw_ref[2:3, :] * dc + cw_ref[1:2, :] * d1 + cw_ref[0:1, :] * d2).astype(BF16)
        halo_ref[j] = dc[:SUBLANES, :]
        for k, t in enumerate((dc * g2, dc * g1, dc * g, dc)):
            acc_ref[j, k:k + 1, :] += jnp.sum(t, axis=0, keepdims=True)

        @pl.when((i == ni - 1) & (j == nf - 1))
        def _():
            for jj in range(nf):
                dcw_ref[:, jj * tn:(jj + 1) * tn] = acc_ref[jj, 0:3, :]
                dcb_ref[:, jj * tn:(jj + 1) * tn] = acc_ref[jj, 3:4, :]

    tile = pl.BlockSpec((tm, tn), lambda i, j: (ni - 1 - i, j))
    return pl.pallas_call(
        body, name=name, grid=(ni, nf),
        in_specs=[pl.BlockSpec((tm, D), lambda i, j: (ni - 1 - i, 0)), pl.BlockSpec((tn, D), lambda i, j: (j, 0)),
                  tile, pl.BlockSpec((SUBLANES * 2, tn), lambda i, j: (jnp.maximum((ni - 1 - i) * (hb // 2) - 1, 0), j)),
                  tile, pl.BlockSpec((3, tn), lambda i, j: (0, j)), pl.BlockSpec((1, tn), lambda i, j: (0, j))],
        out_specs=[tile, tile, pl.BlockSpec((3, F), lambda i, j: (0, 0)), pl.BlockSpec((1, F), lambda i, j: (0, 0))],
        out_shape=[jax.ShapeDtypeStruct((S, F), BF16), jax.ShapeDtypeStruct((S, F), BF16),
                   jax.ShapeDtypeStruct((3, F), F32), jax.ShapeDtypeStruct((1, F), F32)],
        scratch_shapes=[pltpu.VMEM((nf, SUBLANES, tn), F32), pltpu.VMEM((nf, SUBLANES, tn), F32)],
        compiler_params=_cparams("arbitrary", "arbitrary"),
    )(dh2b, w_down, gate, gate, val, conv_w, conv_b)


def _down_loss(act, w_down, h1, g, target, *, name, tm=512):
    S, F = act.shape
    D = h1.shape[1]

    def body(a_ref, w_ref, h_ref, g_ref, t_ref, dh_ref, dhb_ref, dg_ref, loss_ref):
        first = pl.program_id(0) == 0
        h2 = h_ref[...] + _nn(a_ref[...], w_ref[...])
        gv = g_ref[...]
        r = _rstd(h2)
        xh = h2 * r
        err = xh * gv - t_ref[...]
        part_loss = 0.5 * jnp.sum(jnp.mean(err * err, axis=-1, keepdims=True), axis=0, keepdims=True)
        dy = err * (1.0 / D)
        dxh = dy * gv
        dh = r * (dxh - xh * jnp.mean(dxh * xh, axis=-1, keepdims=True))
        dh_ref[...] = dh
        dhb_ref[...] = dh.astype(BF16)
        _accumulate(dg_ref, jnp.sum(dy * xh, axis=0, keepdims=True), first)
        _accumulate(loss_ref, jnp.broadcast_to(part_loss, (1, LANES)), first)

    row = lambda w: pl.BlockSpec((tm, w), lambda i: (i, 0))
    vec = lambda w: pl.BlockSpec((1, w), lambda i: (0, 0))
    return pl.pallas_call(
        body, name=name, grid=(S // tm,), in_specs=[row(F), _resident(w_down.shape), row(D), vec(D), row(D)],
        out_specs=[row(D), row(D), vec(D), vec(LANES)],
        out_shape=[jax.ShapeDtypeStruct((S, D), F32), jax.ShapeDtypeStruct((S, D), BF16),
                   jax.ShapeDtypeStruct((1, D), F32), jax.ShapeDtypeStruct((1, LANES), F32)],
        compiler_params=_cparams("arbitrary"),
    )(act, w_down, h1, g, target)


def _grad_norm_input(pieces, w, x, g, add, *, name, tm=512):
    S, D = x.shape
    widths = [p.shape[1] for p in pieces]
    offs = [sum(widths[:k]) for k in range(len(widths))]
    n = len(pieces)

    def body(*refs):
        p_refs = refs[:n]
        w_ref, x_ref, g_ref, add_ref, dx_ref, dxb_ref, dg_ref = refs[n:]
        du = _nt(p_refs[0][...], w_ref[:, offs[0]:offs[0] + widths[0]])
        for k in range(1, n):
            du = du + _nt(p_refs[k][...], w_ref[:, offs[k]:offs[k] + widths[k]])
        dx, dg = _norm_bwd(x_ref[...], g_ref[...], du)
        dx = add_ref[...] + dx
        dx_ref[...] = dx
        dxb_ref[...] = dx.astype(BF16)
        _accumulate(dg_ref, jnp.sum(dg, axis=0, keepdims=True), pl.program_id(0) == 0)

    row = lambda w_: pl.BlockSpec((tm, w_), lambda i: (i, 0))
    vec = pl.BlockSpec((1, D), lambda i: (0, 0))
    return pl.pallas_call(
        body, name=name, grid=(S // tm,),
        in_specs=[row(wd) for wd in widths] + [_resident(w.shape), row(D), vec, row(D)],
        out_specs=[row(D), row(D), vec],
        out_shape=[jax.ShapeDtypeStruct((S, D), F32), jax.ShapeDtypeStruct((S, D), BF16),
                   jax.ShapeDtypeStruct((1, D), F32)],
        compiler_params=_cparams("arbitrary"),
    )(*pieces, w, x, g, add)


def _rows(a):
    return a.reshape(-1, a.shape[-1])


def _row_tile(rows, cols, itemsize=4, budget=1 << 20):
    t = rows
    while t % 32 == 0 and t * cols * itemsize > budget:
        t //= 2
    return t


def _sum_cast(arrs, out_dtype, *, name):
    shape = arrs[0].shape
    flat = [_rows(a) for a in arrs]
    R, C = flat[0].shape
    tr = _row_tile(R, C)

    def body(*refs):
        acc = refs[0][...].astype(F32)
        for r in refs[1:-1]:
            acc = acc + r[...].astype(F32)
        refs[-1][...] = acc.astype(out_dtype)

    spec = pl.BlockSpec((tr, C), lambda i: (i, 0))
    return pl.pallas_call(
        body, name=name, grid=(R // tr,), in_specs=[spec] * len(flat), out_specs=spec,
        out_shape=jax.ShapeDtypeStruct((R, C), out_dtype), compiler_params=_cparams("parallel"),
    )(*flat).reshape(shape)


def _adamw(parts, w, m, v, *, name):
    shape = w.shape
    parts = [_rows(p) for p in parts]
    w2, m2, v2 = _rows(w), _rows(m), _rows(v)
    R, C = w2.shape
    tr = _row_tile(R, C)
    np_ = len(parts)
    c1, c2 = 1.0 - ADAM_B1 ** ADAM_STEP, 1.0 - ADAM_B2 ** ADAM_STEP

    def body(*refs):
        g = refs[0][...].astype(F32)
        for r in refs[1:np_]:
            g = g + r[...].astype(F32)
        w_ref, m_ref, v_ref, g_out, d_out, m_out, v_out = refs[np_:]
        mn = ADAM_B1 * m_ref[...] + (1.0 - ADAM_B1) * g
        vn = ADAM_B2 * v_ref[...] + (1.0 - ADAM_B2) * (g * g)
        g_out[...] = g
        d_out[...] = -ADAM_LR * ((mn / c1) / (jnp.sqrt(vn / c2) + ADAM_EPS) + ADAM_WD * w_ref[...])
        m_out[...] = mn
        v_out[...] = vn

    spec = pl.BlockSpec((tr, C), lambda i: (i, 0))
    out = jax.ShapeDtypeStruct((R, C), F32)
    res = pl.pallas_call(
        body, name=name, grid=(R // tr,), in_specs=[spec] * (np_ + 3), out_specs=[spec] * 4,
        out_shape=[out] * 4, compiler_params=_cparams("parallel"),
    )(*parts, w2, m2, v2)
    return [r.reshape(shape) for r in res]


def _coords():
    return lax.axis_index("x"), lax.axis_index("y"), lax.axis_index("c")


def _all_gather(shards, *, name):
    n = len(shards)

    def body(*refs):
        x_refs, out_refs = refs[:n], refs[n:2 * n]
        send_sems, recv_sems, local_sems = refs[2 * n:]
        x, y, c = _coords()
        me, sibling = (x, y, c), (x, y, 1 - c)
        chips = [(1 - x, y), (x, 1 - y), (1 - x, 1 - y)]

        def slot(a, dev):
            return out_refs[a].at[4 * dev[0] + 2 * dev[1] + dev[2]]

        def copy(a, k, block, to, src=None):
            return pltpu.make_async_remote_copy(
                src_ref=slot(a, block) if src is None else src, dst_ref=slot(a, block),
                send_sem=send_sems.at[7 * a + k], recv_sem=recv_sems.at[7 * a + k], device_id=to, device_id_type=MESH)

        mine = [pltpu.make_async_copy(x_refs[a], slot(a, me), local_sems.at[a]) for a in range(n)]
        for cp in mine:
            cp.start()
        first = []
        for a in range(n):
            first.append(copy(a, 0, me, sibling, src=x_refs[a]))
            first += [copy(a, 1 + j, me, (*chip, c), src=x_refs[a]) for j, chip in enumerate(chips)]
        for cp in first:
            cp.start()
        passed = []
        for j, chip in enumerate(chips):
            for a in range(n):
                copy(a, 1 + j, (*chip, c), me).wait_recv()
                fwd = copy(a, 4 + j, (*chip, c), sibling)
                fwd.start()
                passed.append(fwd)
        for a in range(n):
            copy(a, 0, sibling, me).wait_recv()
            for j, chip in enumerate(chips):
                copy(a, 4 + j, (*chip, 1 - c), me).wait_recv()
        for cp in first + passed:
            cp.wait_send()
        for cp in mine:
            cp.wait()

    return pl.pallas_call(
        body, name=name, in_specs=[HBM] * n, out_specs=[HBM] * n,
        out_shape=[jax.ShapeDtypeStruct((N_DEV, *s.shape), s.dtype) for s in shards],
        scratch_shapes=[pltpu.SemaphoreType.DMA((7 * n,)), pltpu.SemaphoreType.DMA((7 * n,)),
                        pltpu.SemaphoreType.DMA((n,))],
    )(*shards)


def _flip_c(x, y, c):
    return (x, y, 1 - c)


def _flip_y(x, y, c):
    return (x, 1 - y, c)


def _flip_x(x, y, c):
    return (1 - x, y, c)


def _flip_xy(x, y, c):
    return (1 - x, 1 - y, c)


def _exchange(arrs, rels, *, name):
    n, nr = len(arrs), len(rels)

    def body(*refs):
        in_refs, out_refs = refs[:n], refs[n:2 * n]
        send_sems, recv_sems = refs[2 * n:]
        x, y, c = _coords()
        copies = [
            pltpu.make_async_remote_copy(
                src_ref=in_refs[i].at[k], dst_ref=out_refs[i].at[k], send_sem=send_sems.at[nr * i + k],
                recv_sem=recv_sems.at[nr * i + k], device_id=rels[k](x, y, c), device_id_type=MESH)
            for i in range(n) for k in range(nr)]
        for cp in copies:
            cp.start()
        for cp in copies:
            cp.wait()

    return pl.pallas_call(
        body, name=name, in_specs=[HBM] * n, out_specs=[HBM] * n,
        out_shape=[jax.ShapeDtypeStruct(a.shape, a.dtype) for a in arrs],
        scratch_shapes=[pltpu.SemaphoreType.DMA((n * nr,)), pltpu.SemaphoreType.DMA((n * nr,))],
    )(*arrs)


def _reduce_scatter(grads):
    x, y, c = _coords()
    q = 2 * x + y
    keep, send = [], []
    for g in grads:
        g4 = g.reshape(4, 2, *g.shape[1:])
        keep.append(lax.dynamic_index_in_dim(g4, c, axis=1, keepdims=False))
        send.append(lax.dynamic_index_in_dim(g4, 1 - c, axis=1, keepdims=True).reshape(1, 4, *g.shape[1:]))
    got = _exchange(send, [_flip_c], name="rs_sibling")
    partial = [_sum_cast([k, r[0]], BF16, name=f"rs_chip_sum{i}") for i, (k, r) in enumerate(zip(keep, got))]
    out = [jnp.stack([lax.dynamic_index_in_dim(p, jnp.bitwise_xor(q, m), axis=0, keepdims=False) for m in (1, 2, 3)])
           for p in partial]
    got = _exchange(out, [_flip_y, _flip_x, _flip_xy], name="rs_chips")
    return [[lax.dynamic_index_in_dim(p, q, axis=0, keepdims=False), r[0], r[1], r[2]] for p, r in zip(partial, got)]


def _by_device_cols(w):
    K, N = w.shape
    return w.reshape(K, N_DEV, N // N_DEV).transpose(1, 0, 2)


def _gathered_cols(w8):
    return w8.transpose(1, 0, 2).reshape(w8.shape[1], -1)


def _pair_major(w, inverse=False):
    K = w.shape[0]
    a, b = (ATTN_W // LANES, 3) if inverse else (3, ATTN_W // LANES)
    return w.reshape(K, a, b, LANES).transpose(0, 2, 1, 3).reshape(K, 3 * ATTN_W)


def kernel(x, norm1_g, w_in, attn_norm_g, hgrn_norm_g, hgrn_lb_logits, w_out, norm2_g, w_up, conv_w, conv_b, w_down, final_norm_g, loss_target, m_norm1_g, m_w_in, m_attn_norm_g, m_hgrn_norm_g, m_hgrn_lb_logits, m_w_out, m_norm2_g, m_w_up, m_conv_w, m_conv_b, m_w_down, m_final_norm_g, v_norm1_g, v_w_in, v_attn_norm_g, v_hgrn_norm_g, v_hgrn_lb_logits, v_w_out, v_norm2_g, v_w_up, v_conv_w, v_conv_b, v_w_down, v_final_norm_g):
    xs, target = x[0], loss_target[0]
    S, D = xs.shape
    NA = 3 * ATTN_W
    fng = final_norm_g.reshape(1, D)

    casts = [_sum_cast([w[0]], BF16, name=f"cast_{nm}") for nm, w in
             (("w_in", w_in), ("w_out", w_out), ("w_up", w_up), ("w_down", w_down))]
    g_in, g_out, g_up, g_down, g_cw = _all_gather(casts + [conv_w[0]], name="ag_weights")
    wi = _gathered_cols(g_in)
    wi = jnp.concatenate([_pair_major(wi[:, :NA]), wi[:, NA:]], axis=1)
    wo = g_out.reshape(-1, D)
    wu = _gathered_cols(g_up)
    wd = g_down.reshape(-1, D)
    cw = _gathered_cols(g_cw)

    u1, proj_a = _proj_attn(xs, norm1_g, wi[:, :NA], name="proj_attn")
    proj_h = _mm(u1, wi[:, NA:], name="proj_hgrn")
    attn, lse = _attn_fwd(proj_a, name="attn_fwd")
    rec, states = _hgrn_fwd(proj_h, hgrn_lb_logits, name="hgrn_fwd")
    h1, u2, mixed = _out_proj(attn, rec, proj_h, xs, attn_norm_g, hgrn_norm_g, norm2_g, wo, name="out_proj")
    gate, val, act = _up_glu(u2, wu, cw, conv_b, name="up_glu")
    dh2, dh2b, d_fng, loss_part = _down_loss(act, wd, h1, fng, target, name="down_loss")

    dgate, dval, d_cw, d_cb = _dact_glu_bwd(dh2b, wd, gate, val, cw, conv_b, name="dact_glu_bwd")
    dw_down = _mm_tn(act, dh2b, tm=256, name="dw_down")
    dh1, dh1b, d_n2g = _grad_norm_input([dgate, dval], wu, h1, norm2_g, dh2, name="du2_norm2_bwd")
    dw_up = [_mm_tn(u2, dy, tn=256, name=f"dw_up_{nm}") for nm, dy in (("gate", dgate), ("val", dval))]
    dattn, delta, drec, dhg, d_ang, d_hng = _dmix_post_bwd(dh1b, wo, attn, rec, proj_h, attn_norm_g, hgrn_norm_g,
                                                          name="dmix_post_bwd")
    dw_out = _mm_tn(mixed, dh1b, name="dw_out")
    dproj_h, d_lbl = _hgrn_bwd(proj_h, hgrn_lb_logits, states, drec, name="hgrn_bwd")
    dproj_a = _attn_bwd(proj_a, dattn, lse, delta, name="attn_bwd")
    grad_x, _, d_n1g = _grad_norm_input([dproj_a, dproj_h, dhg], wi, xs, norm1_g, dh1, name="du1_norm1_bwd")
    dw_in = jnp.concatenate([_pair_major(_mm_tn(u1, dproj_a, tn=512, name="dw_in_attn"), inverse=True),
                             _mm_tn(u1, dproj_h, tn=512, name="dw_in_hgrn"),
                             _mm_tn(u1, dhg, tn=512, name="dw_in_gate")], axis=1)

    dw_up8 = jnp.concatenate([h.reshape(D, N_DEV // 2, -1).transpose(1, 0, 2) for h in dw_up], axis=0)
    big = _reduce_scatter([_by_device_cols(dw_in), dw_out.reshape(N_DEV, -1, D), dw_up8,
                           dw_down.reshape(N_DEV, -1, D)])
    res = {}
    for nm, parts, w, m, v in (("w_in", big[0], w_in, m_w_in, v_w_in), ("w_out", big[1], w_out, m_w_out, v_w_out),
                               ("w_up", big[2], w_up, m_w_up, v_w_up), ("w_down", big[3], w_down, m_w_down, v_w_down)):
        res[nm] = _adamw([p.reshape(w.shape) for p in parts], w, m, v, name=f"adamw_{nm}")

    small = [("norm1_g", d_n1g, norm1_g, m_norm1_g, v_norm1_g),
             ("attn_norm_g", d_ang, attn_norm_g, m_attn_norm_g, v_attn_norm_g),
             ("hgrn_norm_g", d_hng, hgrn_norm_g, m_hgrn_norm_g, v_hgrn_norm_g),
             ("hgrn_lb_logits", d_lbl, hgrn_lb_logits, m_hgrn_lb_logits, v_hgrn_lb_logits),
             ("norm2_g", d_n2g, norm2_g, m_norm2_g, v_norm2_g),
             ("conv_b", d_cb, conv_b, m_conv_b, v_conv_b),
             ("final_norm_g", d_fng, final_norm_g, m_final_norm_g, v_final_norm_g)]
    pack = lambda arrs: jnp.concatenate([a.reshape(1, -1) for a in arrs], axis=1)
    g_small, g_dcw = _all_gather([pack([s[1] for s in small]), d_cw], name="ag_small_grads")
    sm = _adamw([g_small[j] for j in range(N_DEV)], pack([s[2] for s in small]), pack([s[3] for s in small]),
                pack([s[4] for s in small]), name="adamw_small")
    off = 0
    for nm, _, w, _, _ in small:
        res[nm] = [r[:, off:off + w.size].reshape(w.shape) for r in sm]
        off += w.size
    me = 4 * lax.axis_index("x") + 2 * lax.axis_index("y") + lax.axis_index("c")
    ncw = conv_w.shape[-1]
    mine_cw = lax.dynamic_slice_in_dim(g_dcw, me * ncw, ncw, axis=2)
    res["conv_w"] = _adamw([mine_cw[j].reshape(conv_w.shape) for j in range(N_DEV)], conv_w, m_conv_w, v_conv_w,
                           name="adamw_conv_w")

    loss = lax.psum(loss_part[0, 0], MESH_AXES)
    order = ["norm1_g", "w_in", "attn_norm_g", "hgrn_norm_g", "hgrn_lb_logits", "w_out", "norm2_g", "w_up",
             "conv_w", "conv_b", "w_down", "final_norm_g"]
    return (loss, grad_x[None], *[res[nm][0] for nm in order], *[res[nm][1] for nm in order],
            *[res[nm][2] for nm in order], *[res[nm][3] for nm in order])
```

```python
import jax
import jax.numpy as jnp
from jax import lax
from jax.experimental import pallas as pl
from jax.experimental.pallas import tpu as pltpu

F32, BF16 = jnp.float32, jnp.bfloat16
NORM_EPS = 1e-6
ATTN_HEADS, HEAD_DIM, ATTN_BLOCK = 8, 64, 128
DILATIONS = (1, 4, 16)
ATTN_SCALE = HEAD_DIM ** -0.5
ATTN_W = ATTN_HEADS * HEAD_DIM
HGRN_HEADS, HGRN_DIM, HGRN_CHUNK = 4, 128, 64
HGRN_W = HGRN_HEADS * HGRN_DIM
ADAM_LR, ADAM_B1, ADAM_B2, ADAM_EPS, ADAM_WD, ADAM_STEP = 0.001, 0.9, 0.999, 1e-08, 0.01, 10
LANES, SUBLANES = 128, 8
VMEM_LIMIT_BYTES = 56 * 1024 * 1024
N_DEV = 8
MESH_AXES = ("x", "y", "c")
MESH = pl.DeviceIdType.MESH
HBM = pl.BlockSpec(memory_space=pltpu.HBM)
HIGHEST = lax.Precision.HIGHEST


def _cparams(*sem):
    return pltpu.CompilerParams(dimension_semantics=sem, vmem_limit_bytes=VMEM_LIMIT_BYTES)


def _tile(n, pref):
    if n <= pref:
        return n
    t = (pref // LANES) * LANES
    while n % t:
        t -= LANES
    return t


def _resident(shape):
    return pl.BlockSpec(shape, lambda *_: (0,) * len(shape), pipeline_mode=pl.Buffered(1))


def _dot(a, b, dims, precision=None):
    return lax.dot_general(a, b, (dims, ((), ())), precision=precision, preferred_element_type=F32)


def _nn(a, b, precision=None):
    return _dot(a, b, ((1,), (0,)), precision)


def _nt(a, b):
    return _dot(a, b, ((1,), (1,)))


def _tn(a, b):
    return _dot(a, b, ((0,), (0,)))


def _sigmoid(x):
    return 1.0 / (1.0 + jnp.exp(-x))


def _rstd(x):
    return lax.rsqrt(jnp.mean(x * x, axis=-1, keepdims=True) + NORM_EPS)


def _norm_bwd(x, g, du):
    r = _rstd(x)
    xh = x * r
    dxh = du * g
    return r * (dxh - xh * jnp.mean(dxh * xh, axis=-1, keepdims=True)), du * xh


def _accumulate(ref, part, first):
    @pl.when(first)
    def _():
        ref[...] = part

    @pl.when(jnp.logical_not(first))
    def _():
        ref[...] += part


def _mm(a, b, *, name, out_dtype=F32, tm=1024, tn=512):
    M, K = a.shape
    N = b.shape[1]
    tm, tn = _tile(M, tm), _tile(N, tn)

    def body(a_ref, b_ref, o_ref):
        o_ref[...] = _nn(a_ref[...], b_ref[...]).astype(out_dtype)

    return pl.pallas_call(
        body, name=name, grid=(M // tm, N // tn),
        in_specs=[pl.BlockSpec((tm, K), lambda i, j: (i, 0)), pl.BlockSpec((K, tn), lambda i, j: (0, j))],
        out_specs=pl.BlockSpec((tm, tn), lambda i, j: (i, j)), out_shape=jax.ShapeDtypeStruct((M, N), out_dtype),
        compiler_params=_cparams("parallel", "parallel"),
    )(a, b)


def _mm_tn(x, dy, *, name, tm=512, tn=1024):
    S, M = x.shape
    N = dy.shape[1]
    tm, tn = _tile(M, tm), _tile(N, tn)

    def body(x_ref, dy_ref, o_ref, xt_ref):
        @pl.when(pl.program_id(1) == 0)
        def _():
            xt_ref[...] = x_ref[...].T

        o_ref[...] = _nn(xt_ref[...], dy_ref[...]).astype(BF16)

    return pl.pallas_call(
        body, name=name, grid=(M // tm, N // tn),
        in_specs=[pl.BlockSpec((S, tm), lambda i, j: (0, i)), pl.BlockSpec((S, tn), lambda i, j: (0, j))],
        out_specs=pl.BlockSpec((tm, tn), lambda i, j: (i, j)), out_shape=jax.ShapeDtypeStruct((M, N), BF16),
        scratch_shapes=[pltpu.VMEM((tm, S), BF16)], compiler_params=_cparams("parallel", "arbitrary"),
    )(x, dy)


def _proj_attn(x, g, w, *, name, tm=1024, tn=512):
    S, D = x.shape
    N = w.shape[1]

    def body(x_ref, g_ref, w_ref, u_ref, o_ref):
        @pl.when(pl.program_id(1) == 0)
        def _():
            xv = x_ref[...]
            u_ref[...] = (xv * _rstd(xv) * g_ref[...]).astype(BF16)

        o_ref[...] = _nn(u_ref[...], w_ref[...]).astype(BF16)

    return pl.pallas_call(
        body, name=name, grid=(S // tm, N // tn),
        in_specs=[pl.BlockSpec((tm, D), lambda i, j: (i, 0)), pl.BlockSpec((1, D), lambda i, j: (0, 0)),
                  pl.BlockSpec((D, tn), lambda i, j: (0, j))],
        out_specs=[pl.BlockSpec((tm, D), lambda i, j: (i, 0)), pl.BlockSpec((tm, tn), lambda i, j: (i, j))],
        out_shape=[jax.ShapeDtypeStruct((S, D), BF16), jax.ShapeDtypeStruct((S, N), BF16)],
        compiler_params=_cparams("parallel", "arbitrary"),
    )(x, g, w)


PAIR_W = 3 * LANES


def _attn_masks(n):
    qi = lax.broadcasted_iota(jnp.int32, (ATTN_BLOCK, 2 * ATTN_BLOCK), 0)
    kj = lax.broadcasted_iota(jnp.int32, (ATTN_BLOCK, 2 * ATTN_BLOCK), 1)
    dist = qi + ATTN_BLOCK - kj
    valid = (dist >= 0) & (dist <= ATTN_BLOCK) & jnp.logical_or(kj >= ATTN_BLOCK, n > 0)
    lane = lax.broadcasted_iota(jnp.int32, (1, LANES), 1)
    return valid, lane


def _head_value(x2, lane, e):
    return jnp.sum(jnp.where(lane == HEAD_DIM * e, x2, 0.0), axis=-1, keepdims=True)


def _block_rows(ref, base, r, d):
    if d == 1:
        return ref[pl.ds(base, ATTN_BLOCK), :]
    return ref.at[pl.ds(base, ATTN_BLOCK * d)][pl.ds(r, ATTN_BLOCK, stride=d), :]


def _set_block_rows(ref, base, r, d, val):
    if d == 1:
        ref[pl.ds(base, ATTN_BLOCK), :] = val
    else:
        ref.at[pl.ds(base, ATTN_BLOCK * d)][pl.ds(r, ATTN_BLOCK, stride=d), :] = val


def _window(ref, pbase, base, r, d):
    return jnp.concatenate([_block_rows(ref, pbase, r, d), _block_rows(ref, base, r, d)], axis=0).astype(BF16)


def _split_pair(p_ref, qs, ks, vs):
    qs[...] = p_ref[:, 0:LANES].astype(F32)
    ks[...] = p_ref[:, LANES:2 * LANES].astype(F32)
    vs[...] = p_ref[:, 2 * LANES:3 * LANES].astype(F32)


def _for_blocks(S, d, fn):
    span = ATTN_BLOCK * d

    def step(n, carry):
        base = pl.multiple_of(n * span, span)
        pbase = pl.multiple_of(jnp.maximum(n - 1, 0) * span, span)
        for r in range(d):
            fn(n, base, pbase, r)
        return carry

    lax.fori_loop(0, S // span, step, 0)


def _attn_fwd(proj_a, *, name):
    S = proj_a.shape[0]

    def body(p_ref, o_ref, l_ref, qs, ks, vs):
        _split_pair(p_ref, qs, ks, vs)
        for d in DILATIONS:
            def block(n, base, pbase, r, d=d):
                valid, lane = _attn_masks(n)
                q2 = _block_rows(qs, base, r, d).astype(BF16)
                k2, v2 = _window(ks, pbase, base, r, d), _window(vs, pbase, base, r, d)
                o2 = jnp.zeros((ATTN_BLOCK, LANES), F32)
                l2 = jnp.zeros((ATTN_BLOCK, LANES), F32)
                for e in range(LANES // HEAD_DIM):
                    mh = (lane >= HEAD_DIM * e) & (lane < HEAD_DIM * (e + 1))
                    s = _nt(jnp.where(mh, q2, jnp.zeros_like(q2)), k2) * ATTN_SCALE
                    s = jnp.where(valid, s, -jnp.inf)
                    m = jnp.max(s, axis=-1, keepdims=True)
                    p = jnp.exp(s - m)
                    l = jnp.sum(p, axis=-1, keepdims=True)
                    o2 = jnp.where(mh, _nn(p.astype(BF16), v2) / l, o2)
                    l2 = jnp.where(mh, m + jnp.log(l), l2)
                if d != DILATIONS[0]:
                    lo, oo = _block_rows(l_ref, base, r, d), _block_rows(o_ref, base, r, d)
                    ln = jnp.maximum(lo, l2)
                    wa, wb = jnp.exp(lo - ln), jnp.exp(l2 - ln)
                    o2 = (wa * oo + wb * o2) / (wa + wb)
                    l2 = ln + jnp.log(wa + wb)
                _set_block_rows(o_ref, base, r, d, o2)
                _set_block_rows(l_ref, base, r, d, l2)

            _for_blocks(S, d, block)

    slab = pl.BlockSpec((S, LANES), lambda p: (0, p))
    return pl.pallas_call(
        body, name=name, grid=(ATTN_W // LANES,), in_specs=[pl.BlockSpec((S, PAIR_W), lambda p: (0, p))],
        out_specs=[slab, slab],
        out_shape=[jax.ShapeDtypeStruct((S, ATTN_W), F32), jax.ShapeDtypeStruct((S, ATTN_W), F32)],
        scratch_shapes=[pltpu.VMEM((S, LANES), F32)] * 3, compiler_params=_cparams("parallel"),
    )(proj_a)


def _attn_bwd(proj_a, do, lse, delta, *, name):
    S = proj_a.shape[0]

    def body(p_ref, do_ref, lse_ref, dl_ref, o_ref, qs, ks, vs, dqs, dks, dvs):
        _split_pair(p_ref, qs, ks, vs)
        for acc in (dqs, dks, dvs):
            acc[...] = jnp.zeros_like(acc)
        for d in DILATIONS:
            def block(n, base, pbase, r, d=d):
                valid, lane = _attn_masks(n)
                q2 = _block_rows(qs, base, r, d).astype(BF16)
                k2, v2 = _window(ks, pbase, base, r, d), _window(vs, pbase, base, r, d)
                do2, lse2, dl2 = (_block_rows(ref, base, r, d) for ref in (do_ref, lse_ref, dl_ref))
                dq2 = jnp.zeros((ATTN_BLOCK, LANES), F32)
                dkw = jnp.zeros((2 * ATTN_BLOCK, LANES), F32)
                dvw = jnp.zeros((2 * ATTN_BLOCK, LANES), F32)
                for e in range(LANES // HEAD_DIM):
                    mh = (lane >= HEAD_DIM * e) & (lane < HEAD_DIM * (e + 1))
                    qm = jnp.where(mh, q2, jnp.zeros_like(q2))
                    dom = jnp.where(mh, do2, 0.0).astype(BF16)
                    s = _nt(qm, k2) * ATTN_SCALE
                    p = jnp.where(valid, jnp.exp(s - _head_value(lse2, lane, e)), 0.0)
                    dp = _nt(dom, v2)
                    ds = (p * (dp - _head_value(dl2, lane, e)) * ATTN_SCALE).astype(BF16)
                    dq2 = dq2 + jnp.where(mh, _nn(ds, k2), 0.0)
                    dkw = dkw + _tn(ds, qm)
                    dvw = dvw + _tn(p.astype(BF16), dom)
                _set_block_rows(dqs, base, r, d, _block_rows(dqs, base, r, d) + dq2)
                for acc, win in ((dks, dkw), (dvs, dvw)):
                    _set_block_rows(acc, pbase, r, d, _block_rows(acc, pbase, r, d) + win[:ATTN_BLOCK])
                    _set_block_rows(acc, base, r, d, _block_rows(acc, base, r, d) + win[ATTN_BLOCK:])

            _for_blocks(S, d, block)
        o_ref[:, 0:LANES] = dqs[...].astype(BF16)
        o_ref[:, LANES:2 * LANES] = dks[...].astype(BF16)
        o_ref[:, 2 * LANES:3 * LANES] = dvs[...].astype(BF16)

    slab = pl.BlockSpec((S, LANES), lambda p: (0, p))
    pair = pl.BlockSpec((S, PAIR_W), lambda p: (0, p))
    return pl.pallas_call(
        body, name=name, grid=(ATTN_W // LANES,), in_specs=[pair, slab, slab, slab], out_specs=pair,
        out_shape=jax.ShapeDtypeStruct(proj_a.shape, BF16),
        scratch_shapes=[pltpu.VMEM((S, LANES), F32)] * 6, compiler_params=_cparams("parallel"),
    )(proj_a, do, lse, delta)


HG_T = 2 * HGRN_CHUNK


def _hgrn_consts():
    row = lax.broadcasted_iota(jnp.int32, (HG_T, HG_T), 0)
    col = lax.broadcasted_iota(jnp.int32, (HG_T, HG_T), 1)
    same = (row >= HGRN_CHUNK) == (col >= HGRN_CHUNK)
    return row, same & (col <= row), same & (col >= row)


def _lower_bound(logits_ref):
    l0, l1 = logits_ref[0:1, :], logits_ref[1:2, :]
    mx = jnp.maximum(l0, l1)
    e0, e1 = jnp.exp(l0 - mx), jnp.exp(l1 - mx)
    return e0 / (e0 + e1)


def _hgrn_gates(q, f, lb, row, causal):
    C = HGRN_CHUNK
    sg = _sigmoid(f)
    forget = lb + (1.0 - lb) * sg
    key = 1.0 - forget
    logf = jnp.log(forget)
    b = _nn(jnp.where(causal, 1.0, 0.0).astype(F32), logf, HIGHEST)
    bend0 = jnp.sum(logf[:C], axis=0, keepdims=True)
    bend1 = jnp.sum(logf[C:], axis=0, keepdims=True)
    bend = jnp.where(row < C, bend0, bend1)
    eb, emb, eend = jnp.exp(b), jnp.exp(-b), jnp.exp(bend - b)
    sq = _sigmoid(q)
    return dict(sg=sg, forget=forget, key=key, bend0=bend0, bend1=bend1, eb=eb, emb=emb, eend=eend, sq=sq,
                qd=q * sq * eb, ki=key * emb, ke=key * eend)


def _hgrn_fwd(proj, logits, *, name):
    S = proj.shape[0]
    W, C = HGRN_W, HGRN_CHUNK

    def body(q_ref, f_ref, i_ref, lg_ref, rec_ref, st_ref, s_ref):
        @pl.when(pl.program_id(0) == 0)
        def _():
            s_ref[...] = jnp.zeros_like(s_ref)

        row, causal, _ = _hgrn_consts()
        lb_all = _lower_bound(lg_ref)
        for h in range(HGRN_HEADS):
            sl = slice(HGRN_DIM * h, HGRN_DIM * (h + 1))
            gt = _hgrn_gates(q_ref[:, sl], f_ref[:, sl], lb_all[:, sl], row, causal)
            qd, ki, ke = gt["qd"].astype(BF16), gt["ki"].astype(BF16), gt["ke"].astype(BF16)
            iv = i_ref[:, sl].astype(BF16)
            a = jnp.where(causal, _nt(qd, ki), 0.0)
            o = _nn(a.astype(BF16), iv)
            s0 = s_ref[h]
            st_ref[0, h] = s0
            o0 = _nt(qd[:C], s0.astype(BF16))
            s1 = jnp.exp(gt["bend0"]) * s0 + _tn(iv[:C], ke[:C])
            st_ref[1, h] = s1
            o1 = _nt(qd[C:], s1.astype(BF16))
            s_ref[h] = jnp.exp(gt["bend1"]) * s1 + _tn(iv[C:], ke[C:])
            rec_ref[:, sl] = o + jnp.concatenate([o0, o1], axis=0)

    blk = lambda j: pl.BlockSpec((HG_T, W), lambda t: (t, j))
    return pl.pallas_call(
        body, name=name, grid=(S // HG_T,),
        in_specs=[blk(0), blk(1), blk(2), pl.BlockSpec((2, W), lambda t: (0, 0))],
        out_specs=[blk(0), pl.BlockSpec((2, HGRN_HEADS, HGRN_DIM, HGRN_DIM), lambda t: (t, 0, 0, 0))],
        out_shape=[jax.ShapeDtypeStruct((S, W), F32),
                   jax.ShapeDtypeStruct((S // C, HGRN_HEADS, HGRN_DIM, HGRN_DIM), F32)],
        scratch_shapes=[pltpu.VMEM((HGRN_HEADS, HGRN_DIM, HGRN_DIM), F32)],
        compiler_params=_cparams("arbitrary"),
    )(proj, proj, proj, logits)


def _hgrn_bwd(proj, logits, states, drec, *, name):
    S = proj.shape[0]
    W, C = HGRN_W, HGRN_CHUNK
    nt = S // HG_T

    def body(q_ref, f_ref, i_ref, lg_ref, st_ref, do_ref, dp_ref, dlg_ref, ds_ref, dlb_ref):
        t = pl.program_id(0)

        @pl.when(t == 0)
        def _():
            ds_ref[...] = jnp.zeros_like(ds_ref)
            dlb_ref[...] = jnp.zeros_like(dlb_ref)

        row, causal, anti = _hgrn_consts()
        lb_all = _lower_bound(lg_ref)
        for h in range(HGRN_HEADS):
            sl = slice(HGRN_DIM * h, HGRN_DIM * (h + 1))
            q, lb = q_ref[:, sl], lb_all[:, sl]
            gt = _hgrn_gates(q, f_ref[:, sl], lb, row, causal)
            qd, ki, ke = gt["qd"], gt["ki"], gt["ke"]
            qdb, kib, keb = qd.astype(BF16), ki.astype(BF16), ke.astype(BF16)
            iv = i_ref[:, sl].astype(BF16)
            dob = do_ref[:, sl].astype(BF16)
            a = jnp.where(causal, _nt(qdb, kib), 0.0).astype(BF16)
            da = jnp.where(causal, _nt(dob, iv), 0.0).astype(BF16)
            s0, s1 = st_ref[0, h], st_ref[1, h]
            dec0, dec1 = jnp.exp(gt["bend0"]), jnp.exp(gt["bend1"])
            ds1 = ds_ref[h]
            ds1b = ds1.astype(BF16)
            dqd1 = _nn(dob[C:], s1.astype(BF16))
            di1 = _nt(keb[C:], ds1b)
            dke1 = _nn(iv[C:], ds1b)
            ddec1 = jnp.sum(ds1 * s1, axis=0, keepdims=True)
            ds0 = dec1 * ds1 + _tn(dob[C:], qdb[C:])
            ds0b = ds0.astype(BF16)
            dqd0 = _nn(dob[:C], s0.astype(BF16))
            di0 = _nt(keb[:C], ds0b)
            dke0 = _nn(iv[:C], ds0b)
            ddec0 = jnp.sum(ds0 * s0, axis=0, keepdims=True)
            ds_ref[h] = dec0 * ds0 + _tn(dob[:C], qdb[:C])

            dqd = _nn(da, kib) + jnp.concatenate([dqd0, dqd1], axis=0)
            dki = _tn(da, qdb)
            dp_ref[:, 2 * W + HGRN_DIM * h:2 * W + HGRN_DIM * (h + 1)] = (
                _tn(a, dob) + jnp.concatenate([di0, di1], axis=0)).astype(BF16)
            dke = jnp.concatenate([dke0, dke1], axis=0)
            gke = dke * ke
            dbend0 = jnp.sum(gke[:C], axis=0, keepdims=True) + ddec0 * dec0
            dbend1 = jnp.sum(gke[C:], axis=0, keepdims=True) + ddec1 * dec1
            db = dqd * qd - dki * ki - gke
            db = db + jnp.where(row == C - 1, dbend0, 0.0) + jnp.where(row == HG_T - 1, dbend1, 0.0)
            dlogf = _nn(jnp.where(anti, 1.0, 0.0).astype(F32), db, HIGHEST)
            dforget = dlogf / gt["forget"] - (dki * gt["emb"] + dke * gt["eend"])
            sg, sq = gt["sg"], gt["sq"]
            dp_ref[:, W + HGRN_DIM * h:W + HGRN_DIM * (h + 1)] = (dforget * (1.0 - lb) * sg * (1.0 - sg)).astype(BF16)
            dlb_ref[:, sl] += jnp.sum(dforget * (1.0 - sg), axis=0, keepdims=True)
            dp_ref[:, sl] = (dqd * gt["eb"] * sq * (1.0 + q * (1.0 - sq))).astype(BF16)

        @pl.when(t == nt - 1)
        def _():
            dl0 = dlb_ref[...] * lb_all * (1.0 - lb_all)
            dlg_ref[0:1, :] = dl0
            dlg_ref[1:2, :] = -dl0

    blk = lambda j: pl.BlockSpec((HG_T, W), lambda t: (nt - 1 - t, j))
    full = pl.BlockSpec((2, W), lambda t: (0, 0))
    return pl.pallas_call(
        body, name=name, grid=(nt,),
        in_specs=[blk(0), blk(1), blk(2), full,
                  pl.BlockSpec((2, HGRN_HEADS, HGRN_DIM, HGRN_DIM), lambda t: (nt - 1 - t, 0, 0, 0)), blk(0)],
        out_specs=[pl.BlockSpec((HG_T, 3 * W), lambda t: (nt - 1 - t, 0)), full],
        out_shape=[jax.ShapeDtypeStruct((S, 3 * W), BF16), jax.ShapeDtypeStruct((2, W), F32)],
        scratch_shapes=[pltpu.VMEM((HGRN_HEADS, HGRN_DIM, HGRN_DIM), F32), pltpu.VMEM((1, W), F32)],
        compiler_params=_cparams("arbitrary"),
    )(proj, proj, proj, logits, states, drec)


def _out_proj(attn, rec, proj_h, x, g_attn, g_hgrn, g_norm2, w_out, *, name, tm=512):
    S, D = x.shape
    AW, W = ATTN_W, HGRN_W

    def body(a_ref, r_ref, hg_ref, x_ref, ga_ref, gh_ref, g2_ref, w_ref, h_ref, u_ref, m_ref):
        av = a_ref[...]
        m_ref[:, :AW] = (av * _rstd(av) * ga_ref[...]).astype(BF16)
        for h in range(HGRN_HEADS):
            sl = slice(HGRN_DIM * h, HGRN_DIM * (h + 1))
            rv, hg = r_ref[:, sl], hg_ref[:, sl]
            m_ref[:, AW + HGRN_DIM * h:AW + HGRN_DIM * (h + 1)] = (
                (rv * _rstd(rv) * gh_ref[:, sl]) * (hg * _sigmoid(hg))).astype(BF16)
        h1 = x_ref[...] + _nn(m_ref[...], w_ref[...])
        h_ref[...] = h1
        u_ref[...] = (h1 * _rstd(h1) * g2_ref[...]).astype(BF16)

    row = lambda w, j=0: pl.BlockSpec((tm, w), lambda i: (i, j))
    vec = lambda w: pl.BlockSpec((1, w), lambda i: (0, 0))
    return pl.pallas_call(
        body, name=name, grid=(S // tm,),
        in_specs=[row(AW), row(W), row(W, 3), row(D), vec(AW), vec(W), vec(D), _resident(w_out.shape)],
        out_specs=[row(D), row(D), row(AW + W)],
        out_shape=[jax.ShapeDtypeStruct((S, D), F32), jax.ShapeDtypeStruct((S, D), BF16),
                   jax.ShapeDtypeStruct((S, AW + W), BF16)],
        compiler_params=_cparams("parallel"),
    )(attn, rec, proj_h, x, g_attn, g_hgrn, g_norm2, w_out)


def _dmix_post_bwd(dh1b, w_out, attn, rec, proj_h, g_attn, g_hgrn, *, name, tm=512):
    S, D = dh1b.shape
    AW, W = ATTN_W, HGRN_W

    def body(dh_ref, w_ref, a_ref, r_ref, hg_ref, ga_ref, gh_ref, do_ref, dl_ref, dr_ref, dhg_ref, dga_ref, dgh_ref):
        first = pl.program_id(0) == 0
        dmix = _nt(dh_ref[...], w_ref[...])
        av = a_ref[...]
        dov, dga = _norm_bwd(av, ga_ref[...], dmix[:, :AW])
        do_ref[...] = dov
        shift = HEAD_DIM.bit_length() - 1
        hi = lax.shift_right_logical(lax.broadcasted_iota(jnp.int32, (AW, AW), 0), shift)
        hj = lax.shift_right_logical(lax.broadcasted_iota(jnp.int32, (AW, AW), 1), shift)
        dl_ref[...] = _nn(dov * av, jnp.where(hi == hj, 1.0, 0.0).astype(F32), HIGHEST)
        _accumulate(dga_ref, jnp.sum(dga, axis=0, keepdims=True), first)

        @pl.when(first)
        def _():
            dgh_ref[...] = jnp.zeros_like(dgh_ref)

        for h in range(HGRN_HEADS):
            sl = slice(HGRN_DIM * h, HGRN_DIM * (h + 1))
            rv, hg, gv = r_ref[:, sl], hg_ref[:, sl], gh_ref[:, sl]
            dout = dmix[:, AW + HGRN_DIM * h:AW + HGRN_DIM * (h + 1)]
            sg = _sigmoid(hg)
            drv, dgh = _norm_bwd(rv, gv, dout * (hg * sg))
            dr_ref[:, sl] = drv
            dgh_ref[:, sl] += jnp.sum(dgh, axis=0, keepdims=True)
            dhg_ref[:, sl] = (dout * (rv * _rstd(rv) * gv) * (sg * (1.0 + hg * (1.0 - sg)))).astype(BF16)

    row = lambda w, j=0: pl.BlockSpec((tm, w), lambda i: (i, j))
    vec = lambda w: pl.BlockSpec((1, w), lambda i: (0, 0))
    return pl.pallas_call(
        body, name=name, grid=(S // tm,),
        in_specs=[row(D), _resident(w_out.shape), row(AW), row(W), row(W, 3), vec(AW), vec(W)],
        out_specs=[row(AW), row(AW), row(W), row(W), vec(AW), vec(W)],
        out_shape=[jax.ShapeDtypeStruct((S, AW), F32), jax.ShapeDtypeStruct((S, AW), F32),
                   jax.ShapeDtypeStruct((S, W), F32), jax.ShapeDtypeStruct((S, W), BF16),
                   jax.ShapeDtypeStruct((1, AW), F32), jax.ShapeDtypeStruct((1, W), F32)],
        compiler_params=_cparams("arbitrary"),
    )(dh1b, w_out, attn, rec, proj_h, g_attn, g_hgrn)


def _conv_act(g, g1, g2, w_ref, b_ref):
    c = b_ref[...] + w_ref[0:1, :] * g2 + w_ref[1:2, :] * g1 + w_ref[2:3, :] * g
    return c, 0.5 * (1.0 + lax.erf(c * (2.0 ** -0.5)))


def _shift_down(g, halo, row):
    g1 = jnp.where(row == 0, halo[7:8], pltpu.roll(g, 1, 0))
    g2 = jnp.where(row == 0, halo[6:7], jnp.where(row == 1, halo[7:8], pltpu.roll(g, 2, 0)))
    return g1, g2


def _shift_up(x, halo, row):
    n = x.shape[0]
    x1 = jnp.where(row == n - 1, halo[0:1], pltpu.roll(x, n - 1, 0))
    x2 = jnp.where(row == n - 2, halo[0:1], jnp.where(row == n - 1, halo[1:2], pltpu.roll(x, n - 2, 0)))
    return x1, x2


def _up_glu(u, w_up, conv_w, conv_b, *, name, tm=1024, tn=256):
    S, D = u.shape
    F = w_up.shape[1] // 2
    nf = F // tn

    def body(u_ref, wg_ref, wv_ref, cw_ref, cb_ref, g_ref, v_ref, a_ref, halo_ref):
        i, j = pl.program_id(0), pl.program_id(1)

        @pl.when(i == 0)
        def _():
            halo_ref[j] = jnp.zeros((SUBLANES, tn), F32)

        uv = u_ref[...]
        g, v = _nn(uv, wg_ref[...]), _nn(uv, wv_ref[...])
        row = lax.broadcasted_iota(jnp.int32, (tm, tn), 0)
        g1, g2 = _shift_down(g, halo_ref[j], row)
        c, cdf = _conv_act(g, g1, g2, cw_ref, cb_ref)
        a_ref[...] = (c * cdf * v).astype(BF16)
        g_ref[...] = g.astype(BF16)
        v_ref[...] = v.astype(BF16)
        halo_ref[j] = g[tm - SUBLANES:, :]

    col = pl.BlockSpec((tm, tn), lambda i, j: (i, j))
    out = jax.ShapeDtypeStruct((S, F), BF16)
    return pl.pallas_call(
        body, name=name, grid=(S // tm, nf),
        in_specs=[pl.BlockSpec((tm, D), lambda i, j: (i, 0)), pl.BlockSpec((D, tn), lambda i, j: (0, j)),
                  pl.BlockSpec((D, tn), lambda i, j: (0, j + nf)), pl.BlockSpec((3, tn), lambda i, j: (0, j)),
                  pl.BlockSpec((1, tn), lambda i, j: (0, j))],
        out_specs=[col, col, col], out_shape=[out, out, out],
        scratch_shapes=[pltpu.VMEM((nf, SUBLANES, tn), F32)], compiler_params=_cparams("arbitrary", "arbitrary"),
    )(u, w_up, w_up, conv_w, conv_b)


def _dact_glu_bwd(dh2b, w_down, gate, val, conv_w, conv_b, *, name, tm=1024, tn=256):
    S, D = dh2b.shape
    F = gate.shape[1]
    nf, ni = F // tn, S // tm
    hb = tm // SUBLANES

    def body(dh_ref, wd_ref, g_ref, gh_ref, v_ref, cw_ref, cb_ref, dg_ref, dv_ref, dcw_ref, dcb_ref, halo_ref, acc_ref):
        i, j = pl.program_id(0), pl.program_id(1)

        @pl.when(i == 0)
        def _():
            halo_ref[j] = jnp.zeros((SUBLANES, tn), F32)
            acc_ref[j] = jnp.zeros((SUBLANES, tn), F32)

        g = g_ref[...].astype(F32)
        before = jnp.where(i < ni - 1, gh_ref[...].astype(F32), 0.0)
        row = lax.broadcasted_iota(jnp.int32, (tm, tn), 0)
        g1, g2 = _shift_down(g, before[SUBLANES:], row)
        c, cdf = _conv_act(g, g1, g2, cw_ref, cb_ref)
        da = _nt(dh_ref[...], wd_ref[...])
        dv_ref[...] = (da * (c * cdf)).astype(BF16)
        pdf = jnp.exp(-0.5 * c * c) * (1.0 / (2.0 * jnp.pi) ** 0.5)
        dc = da * v_ref[...].astype(F32) * (cdf + c * pdf)
        d1, d2 = _shift_up(dc, halo_ref[j], row)
        dg_ref[...] = (cw_ref[2:3, :] * dc + cw_ref[1:2, :] * d1 + cw_ref[0:1, :] * d2).astype(BF16)
        halo_ref[j] = dc[:SUBLANES, :]
        for k, t in enumerate((dc * g2, dc * g1, dc * g, dc)):
            acc_ref[j, k:k + 1, :] += jnp.sum(t, axis=0, keepdims=True)

        @pl.when((i == ni - 1) & (j == nf - 1))
        def _():
            for jj in range(nf):
                dcw_ref[:, jj * tn:(jj + 1) * tn] = acc_ref[jj, 0:3, :]
                dcb_ref[:, jj * tn:(jj + 1) * tn] = acc_ref[jj, 3:4, :]

    tile = pl.BlockSpec((tm, tn), lambda i, j: (ni - 1 - i, j))
    return pl.pallas_call(
        body, name=name, grid=(ni, nf),
        in_specs=[pl.BlockSpec((tm, D), lambda i, j: (ni - 1 - i, 0)), pl.BlockSpec((tn, D), lambda i, j: (j, 0)),
                  tile, pl.BlockSpec((SUBLANES * 2, tn), lambda i, j: (jnp.maximum((ni - 1 - i) * (hb // 2) - 1, 0), j)),
                  tile, pl.BlockSpec((3, tn), lambda i, j: (0, j)), pl.BlockSpec((1, tn), lambda i, j: (0, j))],
        out_specs=[tile, tile, pl.BlockSpec((3, F), lambda i, j: (0, 0)), pl.BlockSpec((1, F), lambda i, j: (0, 0))],
        out_shape=[jax.ShapeDtypeStruct((S, F), BF16), jax.ShapeDtypeStruct((S, F), BF16),
                   jax.ShapeDtypeStruct((3, F), F32), jax.ShapeDtypeStruct((1, F), F32)],
        scratch_shapes=[pltpu.VMEM((nf, SUBLANES, tn), F32), pltpu.VMEM((nf, SUBLANES, tn), F32)],
        compiler_params=_cparams("arbitrary", "arbitrary"),
    )(dh2b, w_down, gate, gate, val, conv_w, conv_b)


def _down_loss(act, w_down, h1, g, target, *, name, tm=512):
    S, F = act.shape
    D = h1.shape[1]

    def body(a_ref, w_ref, h_ref, g_ref, t_ref, dh_ref, dhb_ref, dg_ref, loss_ref):
        first = pl.program_id(0) == 0
        h2 = h_ref[...] + _nn(a_ref[...], w_ref[...])
        gv = g_ref[...]
        r = _rstd(h2)
        xh = h2 * r
        err = xh * gv - t_ref[...]
        part_loss = 0.5 * jnp.sum(jnp.mean(err * err, axis=-1, keepdims=True), axis=0, keepdims=True)
        dy = err * (1.0 / D)
        dxh = dy * gv
        dh = r * (dxh - xh * jnp.mean(dxh * xh, axis=-1, keepdims=True))
        dh_ref[...] = dh
        dhb_ref[...] = dh.astype(BF16)
        _accumulate(dg_ref, jnp.sum(dy * xh, axis=0, keepdims=True), first)
        _accumulate(loss_ref, jnp.broadcast_to(part_loss, (1, LANES)), first)

    row = lambda w: pl.BlockSpec((tm, w), lambda i: (i, 0))
    vec = lambda w: pl.BlockSpec((1, w), lambda i: (0, 0))
    return pl.pallas_call(
        body, name=name, grid=(S // tm,), in_specs=[row(F), _resident(w_down.shape), row(D), vec(D), row(D)],
        out_specs=[row(D), row(D), vec(D), vec(LANES)],
        out_shape=[jax.ShapeDtypeStruct((S, D), F32), jax.ShapeDtypeStruct((S, D), BF16),
                   jax.ShapeDtypeStruct((1, D), F32), jax.ShapeDtypeStruct((1, LANES), F32)],
        compiler_params=_cparams("arbitrary"),
    )(act, w_down, h1, g, target)


def _grad_norm_input(pieces, w, x, g, add, *, name, tm=512):
    S, D = x.shape
    widths = [p.shape[1] for p in pieces]
    offs = [sum(widths[:k]) for k in range(len(widths))]
    n = len(pieces)

    def body(*refs):
        p_refs = refs[:n]
        w_ref, x_ref, g_ref, add_ref, dx_ref, dxb_ref, dg_ref = refs[n:]
        du = _nt(p_refs[0][...], w_ref[:, offs[0]:offs[0] + widths[0]])
        for k in range(1, n):
            du = du + _nt(p_refs[k][...], w_ref[:, offs[k]:offs[k] + widths[k]])
        dx, dg = _norm_bwd(x_ref[...], g_ref[...], du)
        dx = add_ref[...] + dx
        dx_ref[...] = dx
        dxb_ref[...] = dx.astype(BF16)
        _accumulate(dg_ref, jnp.sum(dg, axis=0, keepdims=True), pl.program_id(0) == 0)

    row = lambda w_: pl.BlockSpec((tm, w_), lambda i: (i, 0))
    vec = pl.BlockSpec((1, D), lambda i: (0, 0))
    return pl.pallas_call(
        body, name=name, grid=(S // tm,),
        in_specs=[row(wd) for wd in widths] + [_resident(w.shape), row(D), vec, row(D)],
        out_specs=[row(D), row(D), vec],
        out_shape=[jax.ShapeDtypeStruct((S, D), F32), jax.ShapeDtypeStruct((S, D), BF16),
                   jax.ShapeDtypeStruct((1, D), F32)],
        compiler_params=_cparams("arbitrary"),
    )(*pieces, w, x, g, add)


def _rows(a):
    return a.reshape(-1, a.shape[-1])


def _row_tile(rows, cols, itemsize=4, budget=1 << 20):
    t = rows
    while t % 32 == 0 and t * cols * itemsize > budget:
        t //= 2
    return t


def _sum_cast(arrs, out_dtype, *, name):
    shape = arrs[0].shape
    flat = [_rows(a) for a in arrs]
    R, C = flat[0].shape
    tr = _row_tile(R, C)

    def body(*refs):
        acc = refs[0][...].astype(F32)
        for r in refs[1:-1]:
            acc = acc + r[...].astype(F32)
        refs[-1][...] = acc.astype(out_dtype)

    spec = pl.BlockSpec((tr, C), lambda i: (i, 0))
    return pl.pallas_call(
        body, name=name, grid=(R // tr,), in_specs=[spec] * len(flat), out_specs=spec,
        out_shape=jax.ShapeDtypeStruct((R, C), out_dtype), compiler_params=_cparams("parallel"),
    )(*flat).reshape(shape)


def _adamw(parts, w, m, v, *, name):
    shape = w.shape
    parts = [_rows(p) for p in parts]
    w2, m2, v2 = _rows(w), _rows(m), _rows(v)
    R, C = w2.shape
    tr = _row_tile(R, C)
    np_ = len(parts)
    c1, c2 = 1.0 - ADAM_B1 ** ADAM_STEP, 1.0 - ADAM_B2 ** ADAM_STEP

    def body(*refs):
        g = refs[0][...].astype(F32)
        for r in refs[1:np_]:
            g = g + r[...].astype(F32)
        w_ref, m_ref, v_ref, g_out, d_out, m_out, v_out = refs[np_:]
        mn = ADAM_B1 * m_ref[...] + (1.0 - ADAM_B1) * g
        vn = ADAM_B2 * v_ref[...] + (1.0 - ADAM_B2) * (g * g)
        g_out[...] = g
        d_out[...] = -ADAM_LR * ((mn / c1) / (jnp.sqrt(vn / c2) + ADAM_EPS) + ADAM_WD * w_ref[...])
        m_out[...] = mn
        v_out[...] = vn

    spec = pl.BlockSpec((tr, C), lambda i: (i, 0))
    out = jax.ShapeDtypeStruct((R, C), F32)
    res = pl.pallas_call(
        body, name=name, grid=(R // tr,), in_specs=[spec] * (np_ + 3), out_specs=[spec] * 4,
        out_shape=[out] * 4, compiler_params=_cparams("parallel"),
    )(*parts, w2, m2, v2)
    return [r.reshape(shape) for r in res]


def _coords():
    return lax.axis_index("x"), lax.axis_index("y"), lax.axis_index("c")


def _all_gather(shards, *, name):
    n = len(shards)

    def body(*refs):
        x_refs, out_refs = refs[:n], refs[n:2 * n]
        send_sems, recv_sems, local_sems = refs[2 * n:]
        x, y, c = _coords()
        me, sibling = (x, y, c), (x, y, 1 - c)
        chips = [(1 - x, y), (x, 1 - y), (1 - x, 1 - y)]

        def slot(a, dev):
            return out_refs[a].at[4 * dev[0] + 2 * dev[1] + dev[2]]

        def copy(a, k, block, to, src=None):
            return pltpu.make_async_remote_copy(
                src_ref=slot(a, block) if src is None else src, dst_ref=slot(a, block),
                send_sem=send_sems.at[7 * a + k], recv_sem=recv_sems.at[7 * a + k], device_id=to, device_id_type=MESH)

        mine = [pltpu.make_async_copy(x_refs[a], slot(a, me), local_sems.at[a]) for a in range(n)]
        for cp in mine:
            cp.start()
        first = []
        for a in range(n):
            first.append(copy(a, 0, me, sibling, src=x_refs[a]))
            first += [copy(a, 1 + j, me, (*chip, c), src=x_refs[a]) for j, chip in enumerate(chips)]
        for cp in first:
            cp.start()
        passed = []
        for j, chip in enumerate(chips):
            for a in range(n):
                copy(a, 1 + j, (*chip, c), me).wait_recv()
                fwd = copy(a, 4 + j, (*chip, c), sibling)
                fwd.start()
                passed.append(fwd)
        for a in range(n):
            copy(a, 0, sibling, me).wait_recv()
            for j, chip in enumerate(chips):
                copy(a, 4 + j, (*chip, 1 - c), me).wait_recv()
        for cp in first + passed:
            cp.wait_send()
        for cp in mine:
            cp.wait()

    return pl.pallas_call(
        body, name=name, in_specs=[HBM] * n, out_specs=[HBM] * n,
        out_shape=[jax.ShapeDtypeStruct((N_DEV, *s.shape), s.dtype) for s in shards],
        scratch_shapes=[pltpu.SemaphoreType.DMA((7 * n,)), pltpu.SemaphoreType.DMA((7 * n,)),
                        pltpu.SemaphoreType.DMA((n,))],
    )(*shards)


def _flip_c(x, y, c):
    return (x, y, 1 - c)


def _flip_y(x, y, c):
    return (x, 1 - y, c)


def _flip_x(x, y, c):
    return (1 - x, y, c)


def _flip_xy(x, y, c):
    return (1 - x, 1 - y, c)


def _exchange(arrs, rels, *, name):
    n, nr = len(arrs), len(rels)

    def body(*refs):
        in_refs, out_refs = refs[:n], refs[n:2 * n]
        send_sems, recv_sems = refs[2 * n:]
        x, y, c = _coords()
        copies = [
            pltpu.make_async_remote_copy(
                src_ref=in_refs[i].at[k], dst_ref=out_refs[i].at[k], send_sem=send_sems.at[nr * i + k],
                recv_sem=recv_sems.at[nr * i + k], device_id=rels[k](x, y, c), device_id_type=MESH)
            for i in range(n) for k in range(nr)]
        for cp in copies:
            cp.start()
        for cp in copies:
            cp.wait()

    return pl.pallas_call(
        body, name=name, in_specs=[HBM] * n, out_specs=[HBM] * n,
        out_shape=[jax.ShapeDtypeStruct(a.shape, a.dtype) for a in arrs],
        scratch_shapes=[pltpu.SemaphoreType.DMA((n * nr,)), pltpu.SemaphoreType.DMA((n * nr,))],
    )(*arrs)


SEM = pl.BlockSpec(memory_space=pltpu.SEMAPHORE)
SIDE_EFFECT = pltpu.SideEffectType.DATAFLOW_SIDE_EFFECTING


def _in_hbm(a):
    return pltpu.with_memory_space_constraint(a, pltpu.HBM)


def _copies_start(srcs, lands, plan, n_copies, *, name):
    ns, nl = len(srcs), len(lands)

    def body(*refs):
        src_refs, land_refs = refs[:ns], refs[ns:ns + nl]
        send_sems, recv_sems = refs[ns + nl:ns + nl + 2]
        token = refs[-1]
        for k, (src, dst, peer, _) in enumerate(plan(src_refs, land_refs, *_coords())):
            pltpu.make_async_remote_copy(src_ref=src, dst_ref=dst, send_sem=send_sems.at[k], recv_sem=recv_sems.at[k],
                                         device_id=peer, device_id_type=MESH).start()
        token[...] = jnp.zeros_like(token)

    bufs = [*srcs, *lands]
    res = pl.pallas_call(
        body, name=name, in_specs=[HBM] * (ns + nl),
        out_specs=(SEM, SEM, *[HBM] * (ns + nl), pl.BlockSpec(memory_space=pltpu.VMEM)),
        out_shape=(pltpu.SemaphoreType.DMA((n_copies,)), pltpu.SemaphoreType.DMA((n_copies,)),
                   *[pltpu.HBM(b.shape, b.dtype) for b in bufs], jax.ShapeDtypeStruct((SUBLANES, LANES), F32)),
        input_output_aliases={i: 2 + i for i in range(ns + nl)},
        compiler_params=pltpu.CompilerParams(has_side_effects=SIDE_EFFECT),
    )(*[_in_hbm(b) for b in bufs])
    return res[0], res[1], list(res[2:2 + ns]), list(res[2 + ns:2 + ns + nl]), res[-1]


def _copies_wait(started, plan, after, *, name):
    send_sems, recv_sems, srcs, lands, _ = started
    ns, nl = len(srcs), len(lands)

    def body(*refs):
        src_refs, land_refs = refs[:ns], refs[ns:ns + nl]
        send_sems, recv_sems = refs[ns + nl:ns + nl + 2]
        for k, (src, dst, peer, here) in enumerate(plan(src_refs, land_refs, *_coords())):
            pltpu.make_async_remote_copy(src_ref=src, dst_ref=dst, send_sem=send_sems.at[k], recv_sem=recv_sems.at[k],
                                         device_id=peer, device_id_type=MESH).wait_send()
            pltpu.make_async_remote_copy(src_ref=src, dst_ref=here, send_sem=send_sems.at[k], recv_sem=recv_sems.at[k],
                                         device_id=peer, device_id_type=MESH).wait_recv()

    bufs = [*srcs, *lands]
    res = pl.pallas_call(
        body, name=name, in_specs=[HBM] * (ns + nl) + [SEM, SEM, pl.BlockSpec(memory_space=pl.ANY)],
        out_specs=[HBM] * (ns + nl), out_shape=[pltpu.HBM(b.shape, b.dtype) for b in bufs],
        input_output_aliases={i: i for i in range(ns + nl)},
        compiler_params=pltpu.CompilerParams(has_side_effects=SIDE_EFFECT),
    )(*bufs, send_sems, recv_sems, after)
    return list(res[ns:])


def _dev_index(dev):
    return 4 * dev[0] + 2 * dev[1] + dev[2]


def _ag_chips_plan(src_refs, land_refs, x, y, c):
    me = _dev_index((x, y, c))
    return [(src, land.at[me], peer, land.at[_dev_index(peer)])
            for src, land in zip(src_refs, land_refs) for peer in (_flip_y(x, y, c), _flip_x(x, y, c), _flip_xy(x, y, c))]


def _ag_sibling_plan(src_refs, land_refs, x, y, c):
    chips = [(x, y), (x, 1 - y), (1 - x, y), (1 - x, 1 - y)]
    return [(land.at[_dev_index((*chip, c))], land.at[_dev_index((*chip, c))], (x, y, 1 - c),
             land.at[_dev_index((*chip, 1 - c))]) for land in land_refs for chip in chips]


def _rs_direct_plan(src_refs, land_refs, x, y, c):
    plan = []
    for src, land in zip(src_refs, land_refs):
        for m in range(1, N_DEV):
            peer = (x + (m >> 2) * (1 - 2 * x), y + ((m >> 1) & 1) * (1 - 2 * y), c + (m & 1) * (1 - 2 * c))
            plan.append((src.at[_dev_index(peer)], land.at[m - 1], peer, land.at[m - 1]))
    return plan


def _reduce_scatter(grads):
    x, y, c = _coords()
    q = 2 * x + y
    keep, send = [], []
    for g in grads:
        g4 = g.reshape(4, 2, *g.shape[1:])
        keep.append(lax.dynamic_index_in_dim(g4, c, axis=1, keepdims=False))
        send.append(lax.dynamic_index_in_dim(g4, 1 - c, axis=1, keepdims=True).reshape(1, 4, *g.shape[1:]))
    got = _exchange(send, [_flip_c], name="rs_sibling")
    partial = [_sum_cast([k, r[0]], BF16, name=f"rs_chip_sum{i}") for i, (k, r) in enumerate(zip(keep, got))]
    out = [jnp.stack([lax.dynamic_index_in_dim(p, jnp.bitwise_xor(q, m), axis=0, keepdims=False) for m in (1, 2, 3)])
           for p in partial]
    got = _exchange(out, [_flip_y, _flip_x, _flip_xy], name="rs_chips")
    return [[lax.dynamic_index_in_dim(p, q, axis=0, keepdims=False), r[0], r[1], r[2]] for p, r in zip(partial, got)]


def _by_device_cols(w):
    K, N = w.shape
    return w.reshape(K, N_DEV, N // N_DEV).transpose(1, 0, 2)


def _gathered_cols(w8):
    return w8.transpose(1, 0, 2).reshape(w8.shape[1], -1)


def _pair_major(w, inverse=False):
    K = w.shape[0]
    a, b = (ATTN_W // LANES, 3) if inverse else (3, ATTN_W // LANES)
    return w.reshape(K, a, b, LANES).transpose(0, 2, 1, 3).reshape(K, 3 * ATTN_W)


def kernel(x, norm1_g, w_in, attn_norm_g, hgrn_norm_g, hgrn_lb_logits, w_out, norm2_g, w_up, conv_w, conv_b, w_down, final_norm_g, loss_target, m_norm1_g, m_w_in, m_attn_norm_g, m_hgrn_norm_g, m_hgrn_lb_logits, m_w_out, m_norm2_g, m_w_up, m_conv_w, m_conv_b, m_w_down, m_final_norm_g, v_norm1_g, v_w_in, v_attn_norm_g, v_hgrn_norm_g, v_hgrn_lb_logits, v_w_out, v_norm2_g, v_w_up, v_conv_w, v_conv_b, v_w_down, v_final_norm_g):
    xs, target = x[0], loss_target[0]
    S, D = xs.shape
    NA = 3 * ATTN_W
    fng = final_norm_g.reshape(1, D)

    casts = [_sum_cast([w[0]], BF16, name=f"cast_{nm}") for nm, w in
             (("w_in", w_in), ("w_out", w_out), ("w_up", w_up), ("w_down", w_down))]
    me = _dev_index(_coords())
    (g_in,) = _all_gather(casts[:1], name="ag_w_in")
    later = casts[1:] + [conv_w[0]]
    ag1 = _copies_start(later, [lax.empty((N_DEV, *s.shape), s.dtype) for s in later], _ag_chips_plan,
                        3 * len(later), name="ag_chips_start")
    wi = _gathered_cols(g_in)
    wi = jnp.concatenate([_pair_major(wi[:, :NA]), wi[:, NA:]], axis=1)

    u1, proj_a = _proj_attn(xs, norm1_g + ag1[4][0, 0], wi[:, :NA], name="proj_attn")
    proj_h = _mm(u1, wi[:, NA:], name="proj_hgrn")
    attn, lse = _attn_fwd(proj_a, name="attn_fwd")
    lands = _copies_wait(ag1, _ag_chips_plan, attn, name="ag_chips_wait")
    lands = [lax.dynamic_update_index_in_dim(l, s, me, 0) for l, s in zip(lands, later)]
    ag2 = _copies_start([], lands, _ag_sibling_plan, 4 * len(later), name="ag_sibling_start")
    rec, states = _hgrn_fwd(proj_h, hgrn_lb_logits + ag2[4][0, 0], name="hgrn_fwd")
    g_out, g_up, g_down, g_cw = _copies_wait(ag2, _ag_sibling_plan, rec, name="ag_sibling_wait")
    wo = g_out.reshape(-1, D)
    wu = _gathered_cols(g_up)
    wd = g_down.reshape(-1, D)
    cw = _gathered_cols(g_cw)
    h1, u2, mixed = _out_proj(attn, rec, proj_h, xs, attn_norm_g, hgrn_norm_g, norm2_g, wo, name="out_proj")
    gate, val, act = _up_glu(u2, wu, cw, conv_b, name="up_glu")
    dh2, dh2b, d_fng, loss_part = _down_loss(act, wd, h1, fng, target, name="down_loss")

    dgate, dval, d_cw, d_cb = _dact_glu_bwd(dh2b, wd, gate, val, cw, conv_b, name="dact_glu_bwd")
    dw_down = _mm_tn(act, dh2b, tm=256, name="dw_down")
    dh1, dh1b, d_n2g = _grad_norm_input([dgate, dval], wu, h1, norm2_g, dh2, name="du2_norm2_bwd")
    dw_up = [_mm_tn(u2, dy, tn=256, name=f"dw_up_{nm}") for nm, dy in (("gate", dgate), ("val", dval))]
    ffn = [dw_down.reshape(N_DEV, -1, D),
           jnp.concatenate([h.reshape(D, N_DEV // 2, -1).transpose(1, 0, 2) for h in dw_up], axis=0)]
    ffn_own = [lax.dynamic_index_in_dim(g, me, 0, keepdims=False) for g in ffn]
    rs1 = _copies_start(ffn, [lax.empty((N_DEV - 1, *g.shape[1:]), g.dtype) for g in ffn], _rs_direct_plan,
                        (N_DEV - 1) * len(ffn), name="rs_ffn_start")
    dattn, delta, drec, dhg, d_ang, d_hng = _dmix_post_bwd(dh1b, wo, attn, rec, proj_h, attn_norm_g + rs1[4][0, 0],
                                                          hgrn_norm_g, name="dmix_post_bwd")
    dw_out = _mm_tn(mixed, dh1b, name="dw_out")
    dproj_h, d_lbl = _hgrn_bwd(proj_h, hgrn_lb_logits, states, drec, name="hgrn_bwd")
    dproj_a = _attn_bwd(proj_a, dattn, lse, delta, name="attn_bwd")
    grad_x, _, d_n1g = _grad_norm_input([dproj_a, dproj_h, dhg], wi, xs, norm1_g, dh1, name="du1_norm1_bwd")
    dw_in = jnp.concatenate([_pair_major(_mm_tn(u1, dproj_a, tn=512, name="dw_in_attn"), inverse=True),
                             _mm_tn(u1, dproj_h, tn=512, name="dw_in_hgrn"),
                             _mm_tn(u1, dhg, tn=512, name="dw_in_gate")], axis=1)

    big = _reduce_scatter([_by_device_cols(dw_in), dw_out.reshape(N_DEV, -1, D)])
    got = _copies_wait(rs1, _rs_direct_plan, grad_x, name="rs_ffn_wait")
    big += [[own] + [land[m] for m in range(N_DEV - 1)] for own, land in zip(ffn_own, got)]
    res = {}
    for nm, parts, w, m, v in (("w_in", big[0], w_in, m_w_in, v_w_in), ("w_out", big[1], w_out, m_w_out, v_w_out),
                               ("w_down", big[2], w_down, m_w_down, v_w_down), ("w_up", big[3], w_up, m_w_up, v_w_up)):
        res[nm] = _adamw([p.reshape(w.shape) for p in parts], w, m, v, name=f"adamw_{nm}")

    small = [("norm1_g", d_n1g, norm1_g, m_norm1_g, v_norm1_g),
             ("attn_norm_g", d_ang, attn_norm_g, m_attn_norm_g, v_attn_norm_g),
             ("hgrn_norm_g", d_hng, hgrn_norm_g, m_hgrn_norm_g, v_hgrn_norm_g),
             ("hgrn_lb_logits", d_lbl, hgrn_lb_logits, m_hgrn_lb_logits, v_hgrn_lb_logits),
             ("norm2_g", d_n2g, norm2_g, m_norm2_g, v_norm2_g),
             ("conv_b", d_cb, conv_b, m_conv_b, v_conv_b),
             ("final_norm_g", d_fng, final_norm_g, m_final_norm_g, v_final_norm_g)]
    pack = lambda arrs: jnp.concatenate([a.reshape(1, -1) for a in arrs], axis=1)
    g_small, g_dcw = _all_gather([pack([s[1] for s in small]), d_cw], name="ag_small_grads")
    sm = _adamw([g_small[j] for j in range(N_DEV)], pack([s[2] for s in small]), pack([s[3] for s in small]),
                pack([s[4] for s in small]), name="adamw_small")
    off = 0
    for nm, _, w, _, _ in small:
        res[nm] = [r[:, off:off + w.size].reshape(w.shape) for r in sm]
        off += w.size
    me = 4 * lax.axis_index("x") + 2 * lax.axis_index("y") + lax.axis_index("c")
    ncw = conv_w.shape[-1]
    mine_cw = lax.dynamic_slice_in_dim(g_dcw, me * ncw, ncw, axis=2)
    res["conv_w"] = _adamw([mine_cw[j].reshape(conv_w.shape) for j in range(N_DEV)], conv_w, m_conv_w, v_conv_w,
                           name="adamw_conv_w")

    loss = lax.psum(loss_part[0, 0], MESH_AXES)
    order = ["norm1_g", "w_in", "attn_norm_g", "hgrn_norm_g", "hgrn_lb_logits", "w_out", "norm2_g", "w_up",
             "conv_w", "conv_b", "w_down", "final_norm_g"]
    return (loss, grad_x[None], *[res[nm][0] for nm in order], *[res[nm][1] for nm in order],
            *[res[nm][2] for nm in order], *[res[nm][3] for nm in order])
```

```python
import jax
import jax.numpy as jnp
from jax import lax
from jax.experimental import pallas as pl
from jax.experimental.pallas import tpu as pltpu

F32, BF16 = jnp.float32, jnp.bfloat16
NORM_EPS = 1e-6
ATTN_HEADS, HEAD_DIM, ATTN_BLOCK = 8, 64, 128
DILATIONS = (1, 4, 16)
ATTN_SCALE = HEAD_DIM ** -0.5
ATTN_W = ATTN_HEADS * HEAD_DIM
HGRN_HEADS, HGRN_DIM, HGRN_CHUNK = 4, 128, 64
HGRN_W = HGRN_HEADS * HGRN_DIM
ADAM_LR, ADAM_B1, ADAM_B2, ADAM_EPS, ADAM_WD, ADAM_STEP = 0.001, 0.9, 0.999, 1e-08, 0.01, 10
LANES, SUBLANES = 128, 8
VMEM_LIMIT_BYTES = 56 * 1024 * 1024
N_DEV = 8
MESH_AXES = ("x", "y", "c")
MESH = pl.DeviceIdType.MESH
HBM = pl.BlockSpec(memory_space=pltpu.HBM)
HIGHEST = lax.Precision.HIGHEST


def _cparams(*sem):
    return pltpu.CompilerParams(dimension_semantics=sem, vmem_limit_bytes=VMEM_LIMIT_BYTES)


def _tile(n, pref):
    if n <= pref:
        return n
    t = (pref // LANES) * LANES
    while n % t:
        t -= LANES
    return t


def _resident(shape):
    return pl.BlockSpec(shape, lambda *_: (0,) * len(shape), pipeline_mode=pl.Buffered(1))


def _dot(a, b, dims, precision=None):
    return lax.dot_general(a, b, (dims, ((), ())), precision=precision, preferred_element_type=F32)


def _nn(a, b, precision=None):
    return _dot(a, b, ((1,), (0,)), precision)


def _nt(a, b):
    return _dot(a, b, ((1,), (1,)))


def _tn(a, b):
    return _dot(a, b, ((0,), (0,)))


def _sigmoid(x):
    return 1.0 / (1.0 + jnp.exp(-x))


def _rstd(x):
    return lax.rsqrt(jnp.mean(x * x, axis=-1, keepdims=True) + NORM_EPS)


def _norm_bwd(x, g, du):
    r = _rstd(x)
    xh = x * r
    dxh = du * g
    return r * (dxh - xh * jnp.mean(dxh * xh, axis=-1, keepdims=True)), du * xh


def _accumulate(ref, part, first):
    @pl.when(first)
    def _():
        ref[...] = part

    @pl.when(jnp.logical_not(first))
    def _():
        ref[...] += part


def _mm(a, b, *, name, out_dtype=F32, tm=1024, tn=512):
    M, K = a.shape
    N = b.shape[1]
    tm, tn = _tile(M, tm), _tile(N, tn)

    def body(a_ref, b_ref, o_ref):
        o_ref[...] = _nn(a_ref[...], b_ref[...]).astype(out_dtype)

    return pl.pallas_call(
        body, name=name, grid=(M // tm, N // tn),
        in_specs=[pl.BlockSpec((tm, K), lambda i, j: (i, 0)), pl.BlockSpec((K, tn), lambda i, j: (0, j))],
        out_specs=pl.BlockSpec((tm, tn), lambda i, j: (i, j)), out_shape=jax.ShapeDtypeStruct((M, N), out_dtype),
        compiler_params=_cparams("parallel", "parallel"),
    )(a, b)


def _mm_tn(x, dy, *, name, tm=512, tn=1024):
    S, M = x.shape
    N = dy.shape[1]
    tm, tn = _tile(M, tm), _tile(N, tn)

    def body(x_ref, dy_ref, o_ref, xt_ref):
        @pl.when(pl.program_id(1) == 0)
        def _():
            xt_ref[...] = x_ref[...].T

        o_ref[...] = _nn(xt_ref[...], dy_ref[...]).astype(BF16)

    return pl.pallas_call(
        body, name=name, grid=(M // tm, N // tn),
        in_specs=[pl.BlockSpec((S, tm), lambda i, j: (0, i)), pl.BlockSpec((S, tn), lambda i, j: (0, j))],
        out_specs=pl.BlockSpec((tm, tn), lambda i, j: (i, j)), out_shape=jax.ShapeDtypeStruct((M, N), BF16),
        scratch_shapes=[pltpu.VMEM((tm, S), BF16)], compiler_params=_cparams("parallel", "arbitrary"),
    )(x, dy)


def _proj_attn(x, g, w, *, name, tm=1024, tn=512):
    S, D = x.shape
    N = w.shape[1]

    def body(x_ref, g_ref, w_ref, u_ref, o_ref):
        @pl.when(pl.program_id(1) == 0)
        def _():
            xv = x_ref[...]
            u_ref[...] = (xv * _rstd(xv) * g_ref[...]).astype(BF16)

        o_ref[...] = _nn(u_ref[...], w_ref[...]).astype(BF16)

    return pl.pallas_call(
        body, name=name, grid=(S // tm, N // tn),
        in_specs=[pl.BlockSpec((tm, D), lambda i, j: (i, 0)), pl.BlockSpec((1, D), lambda i, j: (0, 0)),
                  pl.BlockSpec((D, tn), lambda i, j: (0, j))],
        out_specs=[pl.BlockSpec((tm, D), lambda i, j: (i, 0)), pl.BlockSpec((tm, tn), lambda i, j: (i, j))],
        out_shape=[jax.ShapeDtypeStruct((S, D), BF16), jax.ShapeDtypeStruct((S, N), BF16)],
        compiler_params=_cparams("parallel", "arbitrary"),
    )(x, g, w)


PAIR_W = 3 * LANES
ATTN_UNROLL_FWD, ATTN_UNROLL_BWD = 4, 2


def _attn_masks(first):
    qi = lax.broadcasted_iota(jnp.int32, (ATTN_BLOCK, 2 * ATTN_BLOCK), 0)
    kj = lax.broadcasted_iota(jnp.int32, (ATTN_BLOCK, 2 * ATTN_BLOCK), 1)
    dist = qi + ATTN_BLOCK - kj
    valid = (dist >= 0) & (dist <= ATTN_BLOCK) & jnp.logical_or(kj >= ATTN_BLOCK, jnp.logical_not(first))
    lane = lax.broadcasted_iota(jnp.int32, (1, LANES), 1)
    return valid, lane


def _for_residue_blocks(S, d, fn):
    span = ATTN_BLOCK * d
    nb = S // span

    def step(n, carry):
        base = pl.multiple_of(n * span, span)
        for r in range(d):
            off = pl.multiple_of((r * nb + n) * ATTN_BLOCK, ATTN_BLOCK)
            fn(lambda ref, r=r: _block_rows(ref, base, r, d),
               lambda ref, val, r=r: _set_block_rows(ref, base, r, d, val), off)
        return carry

    lax.fori_loop(0, nb, step, 0)


def _for_blocks(S, unroll, fn):
    def step(i, carry):
        for u in range(unroll):
            fn(pl.multiple_of((i * unroll + u) * ATTN_BLOCK, ATTN_BLOCK), i * unroll + u)
        return carry

    lax.fori_loop(0, S // ATTN_BLOCK // unroll, step, 0)


def _head_value(x2, lane, e):
    return jnp.sum(jnp.where(lane == HEAD_DIM * e, x2, 0.0), axis=-1, keepdims=True)


def _block_rows(ref, base, r, d):
    if d == 1:
        return ref[pl.ds(base, ATTN_BLOCK), :]
    return ref.at[pl.ds(base, ATTN_BLOCK * d)][pl.ds(r, ATTN_BLOCK, stride=d), :]


def _set_block_rows(ref, base, r, d, val):
    if d == 1:
        ref[pl.ds(base, ATTN_BLOCK), :] = val
    else:
        ref.at[pl.ds(base, ATTN_BLOCK * d)][pl.ds(r, ATTN_BLOCK, stride=d), :] = val


def _split_pair(p_ref, qs, ks, vs, bk, bv):
    qs[...] = p_ref[:, 0:LANES].astype(F32)
    ks[...] = p_ref[:, LANES:2 * LANES].astype(F32)
    vs[...] = p_ref[:, 2 * LANES:3 * LANES].astype(F32)
    bk[0:ATTN_BLOCK, :] = jnp.zeros((ATTN_BLOCK, LANES), bk.dtype)
    bv[0:ATTN_BLOCK, :] = jnp.zeros((ATTN_BLOCK, LANES), bv.dtype)


def _regroup_qkv(rows, off, qs, ks, vs, bq, bk, bv):
    blk, shifted = pl.ds(off, ATTN_BLOCK), pl.ds(off + ATTN_BLOCK, ATTN_BLOCK)
    bq[blk, :] = rows(qs).astype(BF16)
    bk[shifted, :] = rows(ks).astype(BF16)
    bv[shifted, :] = rows(vs).astype(BF16)


def _attn_fwd(proj_a, *, name):
    S = proj_a.shape[0]

    def body(p_ref, o_ref, l_ref, qs, ks, vs, bq, bk, bv, bo, bl):
        _split_pair(p_ref, qs, ks, vs, bk, bv)
        for d in DILATIONS:
            nb = S // (ATTN_BLOCK * d)
            _for_residue_blocks(S, d, lambda rows, _, off: _regroup_qkv(rows, off, qs, ks, vs, bq, bk, bv))

            def block(off, b, nb=nb):
                valid, lane = _attn_masks(jnp.bitwise_and(b, nb - 1) == 0)
                blk, win = pl.ds(off, ATTN_BLOCK), pl.ds(off, 2 * ATTN_BLOCK)
                q2, k2, v2 = bq[blk, :], bk[win, :], bv[win, :]
                o2 = jnp.zeros((ATTN_BLOCK, LANES), F32)
                l2 = jnp.zeros((ATTN_BLOCK, LANES), F32)
                for e in range(LANES // HEAD_DIM):
                    mh = (lane >= HEAD_DIM * e) & (lane < HEAD_DIM * (e + 1))
                    s = _nt(jnp.where(mh, q2, jnp.zeros_like(q2)), k2) * ATTN_SCALE
                    s = jnp.where(valid, s, -jnp.inf)
                    m = jnp.max(s, axis=-1, keepdims=True)
                    p = jnp.exp(s - m)
                    l = jnp.sum(p, axis=-1, keepdims=True)
                    o2 = jnp.where(mh, _nn(p.astype(BF16), v2) / l, o2)
                    l2 = jnp.where(mh, m + jnp.log(l), l2)
                bo[blk, :] = o2
                bl[blk, :] = l2

            _for_blocks(S, ATTN_UNROLL_FWD, block)

            def merge(rows, set_rows, off, d=d):
                blk = pl.ds(off, ATTN_BLOCK)
                o2, l2 = bo[blk, :], bl[blk, :]
                if d != DILATIONS[0]:
                    lo, oo = rows(l_ref), rows(o_ref)
                    ln = jnp.maximum(lo, l2)
                    wa, wb = jnp.exp(lo - ln), jnp.exp(l2 - ln)
                    o2 = (wa * oo + wb * o2) / (wa + wb)
                    l2 = ln + jnp.log(wa + wb)
                set_rows(o_ref, o2)
                set_rows(l_ref, l2)

            _for_residue_blocks(S, d, merge)

    slab = pl.BlockSpec((S, LANES), lambda p: (0, p))
    f32_slab, bf16_slab = pltpu.VMEM((S, LANES), F32), pltpu.VMEM((S, LANES), BF16)
    bf16_window = pltpu.VMEM((S + ATTN_BLOCK, LANES), BF16)
    return pl.pallas_call(
        body, name=name, grid=(ATTN_W // LANES,), in_specs=[pl.BlockSpec((S, PAIR_W), lambda p: (0, p))],
        out_specs=[slab, slab],
        out_shape=[jax.ShapeDtypeStruct((S, ATTN_W), F32), jax.ShapeDtypeStruct((S, ATTN_W), F32)],
        scratch_shapes=[f32_slab] * 3 + [bf16_slab, bf16_window, bf16_window, f32_slab, f32_slab],
        compiler_params=_cparams("parallel"),
    )(proj_a)


def _attn_bwd(proj_a, do, lse, delta, *, name):
    S = proj_a.shape[0]

    def body(p_ref, do_ref, lse_ref, dl_ref, o_ref, qs, ks, vs, dqs, dks, dvs, bq, bk, bv, bdo, blse, bdl, bdq, bdk, bdv):
        _split_pair(p_ref, qs, ks, vs, bk, bv)
        bdk[0:ATTN_BLOCK, :] = jnp.zeros((ATTN_BLOCK, LANES), F32)
        bdv[0:ATTN_BLOCK, :] = jnp.zeros((ATTN_BLOCK, LANES), F32)
        for d in DILATIONS:
            nb = S // (ATTN_BLOCK * d)

            def regroup(rows, _, off):
                _regroup_qkv(rows, off, qs, ks, vs, bq, bk, bv)
                blk = pl.ds(off, ATTN_BLOCK)
                bdo[blk, :] = rows(do_ref).astype(BF16)
                blse[blk, :] = rows(lse_ref)
                bdl[blk, :] = rows(dl_ref)

            _for_residue_blocks(S, d, regroup)

            def block(off, b, nb=nb):
                valid, lane = _attn_masks(jnp.bitwise_and(b, nb - 1) == 0)
                blk, win = pl.ds(off, ATTN_BLOCK), pl.ds(off, 2 * ATTN_BLOCK)
                q2, k2, v2, do2, lse2, dl2 = bq[blk, :], bk[win, :], bv[win, :], bdo[blk, :], blse[blk, :], bdl[blk, :]
                dq2 = jnp.zeros((ATTN_BLOCK, LANES), F32)
                dkw = jnp.zeros((2 * ATTN_BLOCK, LANES), F32)
                dvw = jnp.zeros((2 * ATTN_BLOCK, LANES), F32)
                for e in range(LANES // HEAD_DIM):
                    mh = (lane >= HEAD_DIM * e) & (lane < HEAD_DIM * (e + 1))
                    qm = jnp.where(mh, q2, jnp.zeros_like(q2))
                    dom = jnp.where(mh, do2, jnp.zeros_like(do2))
                    s = _nt(qm, k2) * ATTN_SCALE
                    p = jnp.where(valid, jnp.exp(s - _head_value(lse2, lane, e)), 0.0)
                    dp = _nt(dom, v2)
                    ds = (p * (dp - _head_value(dl2, lane, e)) * ATTN_SCALE).astype(BF16)
                    dq2 = dq2 + jnp.where(mh, _nn(ds, k2), 0.0)
                    dkw = dkw + _tn(ds, qm)
                    dvw = dvw + _tn(p.astype(BF16), dom)
                bdq[blk, :] = dq2
                for acc, win_grad in ((bdk, dkw), (bdv, dvw)):
                    acc[blk, :] += win_grad[:ATTN_BLOCK]
                    acc[pl.ds(off + ATTN_BLOCK, ATTN_BLOCK), :] = win_grad[ATTN_BLOCK:]

            _for_blocks(S, ATTN_UNROLL_BWD, block)

            def scatter(rows, set_rows, off, d=d):
                blk, shifted = pl.ds(off, ATTN_BLOCK), pl.ds(off + ATTN_BLOCK, ATTN_BLOCK)
                for acc, part in ((dqs, bdq[blk, :]), (dks, bdk[shifted, :]), (dvs, bdv[shifted, :])):
                    set_rows(acc, part if d == DILATIONS[0] else rows(acc) + part)

            _for_residue_blocks(S, d, scatter)
        o_ref[:, 0:LANES] = dqs[...].astype(BF16)
        o_ref[:, LANES:2 * LANES] = dks[...].astype(BF16)
        o_ref[:, 2 * LANES:3 * LANES] = dvs[...].astype(BF16)

    slab = pl.BlockSpec((S, LANES), lambda p: (0, p), pipeline_mode=pl.Buffered(1))
    pair = pl.BlockSpec((S, PAIR_W), lambda p: (0, p))
    f32_slab, bf16_slab = pltpu.VMEM((S, LANES), F32), pltpu.VMEM((S, LANES), BF16)
    f32_window, bf16_window = pltpu.VMEM((S + ATTN_BLOCK, LANES), F32), pltpu.VMEM((S + ATTN_BLOCK, LANES), BF16)
    return pl.pallas_call(
        body, name=name, grid=(ATTN_W // LANES,), in_specs=[pair, slab, slab, slab], out_specs=pair,
        out_shape=jax.ShapeDtypeStruct(proj_a.shape, BF16),
        scratch_shapes=[f32_slab] * 6 + [bf16_slab, bf16_window, bf16_window, bf16_slab, f32_slab, f32_slab,
                                         f32_slab, f32_window, f32_window],
        compiler_params=_cparams("parallel"),
    )(proj_a, do, lse, delta)


HG_T = 2 * HGRN_CHUNK


def _hgrn_consts():
    row = lax.broadcasted_iota(jnp.int32, (HG_T, HG_T), 0)
    col = lax.broadcasted_iota(jnp.int32, (HG_T, HG_T), 1)
    same = (row >= HGRN_CHUNK) == (col >= HGRN_CHUNK)
    return row, same & (col <= row), same & (col >= row)


def _lower_bound(logits_ref):
    l0, l1 = logits_ref[0:1, :], logits_ref[1:2, :]
    mx = jnp.maximum(l0, l1)
    e0, e1 = jnp.exp(l0 - mx), jnp.exp(l1 - mx)
    return e0 / (e0 + e1)


def _hgrn_gates(q, f, lb, row, causal):
    C = HGRN_CHUNK
    sg = _sigmoid(f)
    forget = lb + (1.0 - lb) * sg
    key = 1.0 - forget
    logf = jnp.log(forget)
    b = _nn(jnp.where(causal, 1.0, 0.0).astype(F32), logf, HIGHEST)
    bend0 = jnp.sum(logf[:C], axis=0, keepdims=True)
    bend1 = jnp.sum(logf[C:], axis=0, keepdims=True)
    bend = jnp.where(row < C, bend0, bend1)
    eb, emb, eend = jnp.exp(b), jnp.exp(-b), jnp.exp(bend - b)
    sq = _sigmoid(q)
    return dict(sg=sg, forget=forget, key=key, bend0=bend0, bend1=bend1, eb=eb, emb=emb, eend=eend, sq=sq,
                qd=q * sq * eb, ki=key * emb, ke=key * eend)


def _hgrn_fwd(proj, logits, *, name):
    S = proj.shape[0]
    W, C = HGRN_W, HGRN_CHUNK

    def body(q_ref, f_ref, i_ref, lg_ref, rec_ref, st_ref, s_ref):
        @pl.when(pl.program_id(0) == 0)
        def _():
            s_ref[...] = jnp.zeros_like(s_ref)

        row, causal, _ = _hgrn_consts()
        lb_all = _lower_bound(lg_ref)
        for h in range(HGRN_HEADS):
            sl = slice(HGRN_DIM * h, HGRN_DIM * (h + 1))
            gt = _hgrn_gates(q_ref[:, sl], f_ref[:, sl], lb_all[:, sl], row, causal)
            qd, ki, ke = gt["qd"].astype(BF16), gt["ki"].astype(BF16), gt["ke"].astype(BF16)
            iv = i_ref[:, sl].astype(BF16)
            a = jnp.where(causal, _nt(qd, ki), 0.0)
            o = _nn(a.astype(BF16), iv)
            s0 = s_ref[h]
            st_ref[0, h] = s0
            o0 = _nt(qd[:C], s0.astype(BF16))
            s1 = jnp.exp(gt["bend0"]) * s0 + _tn(iv[:C], ke[:C])
            st_ref[1, h] = s1
            o1 = _nt(qd[C:], s1.astype(BF16))
            s_ref[h] = jnp.exp(gt["bend1"]) * s1 + _tn(iv[C:], ke[C:])
            rec_ref[:, sl] = o + jnp.concatenate([o0, o1], axis=0)

    blk = lambda j: pl.BlockSpec((HG_T, W), lambda t: (t, j))
    return pl.pallas_call(
        body, name=name, grid=(S // HG_T,),
        in_specs=[blk(0), blk(1), blk(2), pl.BlockSpec((2, W), lambda t: (0, 0))],
        out_specs=[blk(0), pl.BlockSpec((2, HGRN_HEADS, HGRN_DIM, HGRN_DIM), lambda t: (t, 0, 0, 0))],
        out_shape=[jax.ShapeDtypeStruct((S, W), F32),
                   jax.ShapeDtypeStruct((S // C, HGRN_HEADS, HGRN_DIM, HGRN_DIM), F32)],
        scratch_shapes=[pltpu.VMEM((HGRN_HEADS, HGRN_DIM, HGRN_DIM), F32)],
        compiler_params=_cparams("arbitrary"),
    )(proj, proj, proj, logits)


def _hgrn_bwd(proj, logits, states, drec, *, name):
    S = proj.shape[0]
    W, C = HGRN_W, HGRN_CHUNK
    nt = S // HG_T

    def body(q_ref, f_ref, i_ref, lg_ref, st_ref, do_ref, dp_ref, dlg_ref, ds_ref, dlb_ref):
        t = pl.program_id(0)

        @pl.when(t == 0)
        def _():
            ds_ref[...] = jnp.zeros_like(ds_ref)
            dlb_ref[...] = jnp.zeros_like(dlb_ref)

        row, causal, anti = _hgrn_consts()
        lb_all = _lower_bound(lg_ref)
        for h in range(HGRN_HEADS):
            sl = slice(HGRN_DIM * h, HGRN_DIM * (h + 1))
            q, lb = q_ref[:, sl], lb_all[:, sl]
            gt = _hgrn_gates(q, f_ref[:, sl], lb, row, causal)
            qd, ki, ke = gt["qd"], gt["ki"], gt["ke"]
            qdb, kib, keb = qd.astype(BF16), ki.astype(BF16), ke.astype(BF16)
            iv = i_ref[:, sl].astype(BF16)
            dob = do_ref[:, sl].astype(BF16)
            a = jnp.where(causal, _nt(qdb, kib), 0.0).astype(BF16)
            da = jnp.where(causal, _nt(dob, iv), 0.0).astype(BF16)
            s0, s1 = st_ref[0, h], st_ref[1, h]
            dec0, dec1 = jnp.exp(gt["bend0"]), jnp.exp(gt["bend1"])
            ds1 = ds_ref[h]
            ds1b = ds1.astype(BF16)
            dqd1 = _nn(dob[C:], s1.astype(BF16))
            di1 = _nt(keb[C:], ds1b)
            dke1 = _nn(iv[C:], ds1b)
            ddec1 = jnp.sum(ds1 * s1, axis=0, keepdims=True)
            ds0 = dec1 * ds1 + _tn(dob[C:], qdb[C:])
            ds0b = ds0.astype(BF16)
            dqd0 = _nn(dob[:C], s0.astype(BF16))
            di0 = _nt(keb[:C], ds0b)
            dke0 = _nn(iv[:C], ds0b)
            ddec0 = jnp.sum(ds0 * s0, axis=0, keepdims=True)
            ds_ref[h] = dec0 * ds0 + _tn(dob[:C], qdb[:C])

            dqd = _nn(da, kib) + jnp.concatenate([dqd0, dqd1], axis=0)
            dki = _tn(da, qdb)
            dp_ref[:, 2 * W + HGRN_DIM * h:2 * W + HGRN_DIM * (h + 1)] = (
                _tn(a, dob) + jnp.concatenate([di0, di1], axis=0)).astype(BF16)
            dke = jnp.concatenate([dke0, dke1], axis=0)
            gke = dke * ke
            dbend0 = jnp.sum(gke[:C], axis=0, keepdims=True) + ddec0 * dec0
            dbend1 = jnp.sum(gke[C:], axis=0, keepdims=True) + ddec1 * dec1
            db = dqd * qd - dki * ki - gke
            db = db + jnp.where(row == C - 1, dbend0, 0.0) + jnp.where(row == HG_T - 1, dbend1, 0.0)
            dlogf = _nn(jnp.where(anti, 1.0, 0.0).astype(F32), db, HIGHEST)
            dforget = dlogf / gt["forget"] - (dki * gt["emb"] + dke * gt["eend"])
            sg, sq = gt["sg"], gt["sq"]
            dp_ref[:, W + HGRN_DIM * h:W + HGRN_DIM * (h + 1)] = (dforget * (1.0 - lb) * sg * (1.0 - sg)).astype(BF16)
            dlb_ref[:, sl] += jnp.sum(dforget * (1.0 - sg), axis=0, keepdims=True)
            dp_ref[:, sl] = (dqd * gt["eb"] * sq * (1.0 + q * (1.0 - sq))).astype(BF16)

        @pl.when(t == nt - 1)
        def _():
            dl0 = dlb_ref[...] * lb_all * (1.0 - lb_all)
            dlg_ref[0:1, :] = dl0
            dlg_ref[1:2, :] = -dl0

    blk = lambda j: pl.BlockSpec((HG_T, W), lambda t: (nt - 1 - t, j))
    full = pl.BlockSpec((2, W), lambda t: (0, 0))
    return pl.pallas_call(
        body, name=name, grid=(nt,),
        in_specs=[blk(0), blk(1), blk(2), full,
                  pl.BlockSpec((2, HGRN_HEADS, HGRN_DIM, HGRN_DIM), lambda t: (nt - 1 - t, 0, 0, 0)), blk(0)],
        out_specs=[pl.BlockSpec((HG_T, 3 * W), lambda t: (nt - 1 - t, 0)), full],
        out_shape=[jax.ShapeDtypeStruct((S, 3 * W), BF16), jax.ShapeDtypeStruct((2, W), F32)],
        scratch_shapes=[pltpu.VMEM((HGRN_HEADS, HGRN_DIM, HGRN_DIM), F32), pltpu.VMEM((1, W), F32)],
        compiler_params=_cparams("arbitrary"),
    )(proj, proj, proj, logits, states, drec)


def _out_proj(attn, rec, proj_h, x, g_attn, g_hgrn, g_norm2, w_out, *, name, tm=512):
    S, D = x.shape
    AW, W = ATTN_W, HGRN_W

    def body(a_ref, r_ref, hg_ref, x_ref, ga_ref, gh_ref, g2_ref, w_ref, h_ref, u_ref, m_ref):
        av = a_ref[...]
        m_ref[:, :AW] = (av * _rstd(av) * ga_ref[...]).astype(BF16)
        for h in range(HGRN_HEADS):
            sl = slice(HGRN_DIM * h, HGRN_DIM * (h + 1))
            rv, hg = r_ref[:, sl], hg_ref[:, sl]
            m_ref[:, AW + HGRN_DIM * h:AW + HGRN_DIM * (h + 1)] = (
                (rv * _rstd(rv) * gh_ref[:, sl]) * (hg * _sigmoid(hg))).astype(BF16)
        h1 = x_ref[...] + _nn(m_ref[...], w_ref[...])
        h_ref[...] = h1
        u_ref[...] = (h1 * _rstd(h1) * g2_ref[...]).astype(BF16)

    row = lambda w, j=0: pl.BlockSpec((tm, w), lambda i: (i, j))
    vec = lambda w: pl.BlockSpec((1, w), lambda i: (0, 0))
    return pl.pallas_call(
        body, name=name, grid=(S // tm,),
        in_specs=[row(AW), row(W), row(W, 3), row(D), vec(AW), vec(W), vec(D), _resident(w_out.shape)],
        out_specs=[row(D), row(D), row(AW + W)],
        out_shape=[jax.ShapeDtypeStruct((S, D), F32), jax.ShapeDtypeStruct((S, D), BF16),
                   jax.ShapeDtypeStruct((S, AW + W), BF16)],
        compiler_params=_cparams("parallel"),
    )(attn, rec, proj_h, x, g_attn, g_hgrn, g_norm2, w_out)


def _dmix_post_bwd(dh1b, w_out, attn, rec, proj_h, g_attn, g_hgrn, *, name, tm=512):
    S, D = dh1b.shape
    AW, W = ATTN_W, HGRN_W

    def body(dh_ref, w_ref, a_ref, r_ref, hg_ref, ga_ref, gh_ref, do_ref, dl_ref, dr_ref, dhg_ref, dga_ref, dgh_ref):
        first = pl.program_id(0) == 0
        dmix = _nt(dh_ref[...], w_ref[...])
        av = a_ref[...]
        dov, dga = _norm_bwd(av, ga_ref[...], dmix[:, :AW])
        do_ref[...] = dov
        shift = HEAD_DIM.bit_length() - 1
        hi = lax.shift_right_logical(lax.broadcasted_iota(jnp.int32, (AW, AW), 0), shift)
        hj = lax.shift_right_logical(lax.broadcasted_iota(jnp.int32, (AW, AW), 1), shift)
        dl_ref[...] = _nn(dov * av, jnp.where(hi == hj, 1.0, 0.0).astype(F32), HIGHEST)
        _accumulate(dga_ref, jnp.sum(dga, axis=0, keepdims=True), first)

        @pl.when(first)
        def _():
            dgh_ref[...] = jnp.zeros_like(dgh_ref)

        for h in range(HGRN_HEADS):
            sl = slice(HGRN_DIM * h, HGRN_DIM * (h + 1))
            rv, hg, gv = r_ref[:, sl], hg_ref[:, sl], gh_ref[:, sl]
            dout = dmix[:, AW + HGRN_DIM * h:AW + HGRN_DIM * (h + 1)]
            sg = _sigmoid(hg)
            drv, dgh = _norm_bwd(rv, gv, dout * (hg * sg))
            dr_ref[:, sl] = drv
            dgh_ref[:, sl] += jnp.sum(dgh, axis=0, keepdims=True)
            dhg_ref[:, sl] = (dout * (rv * _rstd(rv) * gv) * (sg * (1.0 + hg * (1.0 - sg)))).astype(BF16)

    row = lambda w, j=0: pl.BlockSpec((tm, w), lambda i: (i, j))
    vec = lambda w: pl.BlockSpec((1, w), lambda i: (0, 0))
    return pl.pallas_call(
        body, name=name, grid=(S // tm,),
        in_specs=[row(D), _resident(w_out.shape), row(AW), row(W), row(W, 3), vec(AW), vec(W)],
        out_specs=[row(AW), row(AW), row(W), row(W), vec(AW), vec(W)],
        out_shape=[jax.ShapeDtypeStruct((S, AW), F32), jax.ShapeDtypeStruct((S, AW), F32),
                   jax.ShapeDtypeStruct((S, W), F32), jax.ShapeDtypeStruct((S, W), BF16),
                   jax.ShapeDtypeStruct((1, AW), F32), jax.ShapeDtypeStruct((1, W), F32)],
        compiler_params=_cparams("arbitrary"),
    )(dh1b, w_out, attn, rec, proj_h, g_attn, g_hgrn)


def _conv_act(g, g1, g2, w_ref, b_ref):
    c = b_ref[...] + w_ref[0:1, :] * g2 + w_ref[1:2, :] * g1 + w_ref[2:3, :] * g
    return c, 0.5 * (1.0 + lax.erf(c * (2.0 ** -0.5)))


def _shift_down(g, halo, row):
    g1 = jnp.where(row == 0, halo[7:8], pltpu.roll(g, 1, 0))
    g2 = jnp.where(row == 0, halo[6:7], jnp.where(row == 1, halo[7:8], pltpu.roll(g, 2, 0)))
    return g1, g2


def _shift_up(x, halo, row):
    n = x.shape[0]
    x1 = jnp.where(row == n - 1, halo[0:1], pltpu.roll(x, n - 1, 0))
    x2 = jnp.where(row == n - 2, halo[0:1], jnp.where(row == n - 1, halo[1:2], pltpu.roll(x, n - 2, 0)))
    return x1, x2


def _up_glu(u, w_up, conv_w, conv_b, *, name, tm=1024, tn=256):
    S, D = u.shape
    F = w_up.shape[1] // 2
    nf = F // tn

    def body(u_ref, wg_ref, wv_ref, cw_ref, cb_ref, g_ref, v_ref, a_ref, halo_ref):
        i, j = pl.program_id(0), pl.program_id(1)

        @pl.when(i == 0)
        def _():
            halo_ref[j] = jnp.zeros((SUBLANES, tn), F32)

        uv = u_ref[...]
        g, v = _nn(uv, wg_ref[...]), _nn(uv, wv_ref[...])
        row = lax.broadcasted_iota(jnp.int32, (tm, tn), 0)
        g1, g2 = _shift_down(g, halo_ref[j], row)
        c, cdf = _conv_act(g, g1, g2, cw_ref, cb_ref)
        a_ref[...] = (c * cdf * v).astype(BF16)
        g_ref[...] = g.astype(BF16)
        v_ref[...] = v.astype(BF16)
        halo_ref[j] = g[tm - SUBLANES:, :]

    col = pl.BlockSpec((tm, tn), lambda i, j: (i, j))
    out = jax.ShapeDtypeStruct((S, F), BF16)
    return pl.pallas_call(
        body, name=name, grid=(S // tm, nf),
        in_specs=[pl.BlockSpec((tm, D), lambda i, j: (i, 0)), pl.BlockSpec((D, tn), lambda i, j: (0, j)),
                  pl.BlockSpec((D, tn), lambda i, j: (0, j + nf)), pl.BlockSpec((3, tn), lambda i, j: (0, j)),
                  pl.BlockSpec((1, tn), lambda i, j: (0, j))],
        out_specs=[col, col, col], out_shape=[out, out, out],
        scratch_shapes=[pltpu.VMEM((nf, SUBLANES, tn), F32)], compiler_params=_cparams("arbitrary", "arbitrary"),
    )(u, w_up, w_up, conv_w, conv_b)


def _dact_glu_bwd(dh2b, w_down, gate, val, conv_w, conv_b, *, name, tm=1024, tn=256):
    S, D = dh2b.shape
    F = gate.shape[1]
    nf, ni = F // tn, S // tm
    hb = tm // SUBLANES

    def body(dh_ref, wd_ref, g_ref, gh_ref, v_ref, cw_ref, cb_ref, dg_ref, dv_ref, dcw_ref, dcb_ref, halo_ref, acc_ref):
        i, j = pl.program_id(0), pl.program_id(1)

        @pl.when(i == 0)
        def _():
            halo_ref[j] = jnp.zeros((SUBLANES, tn), F32)
            acc_ref[j] = jnp.zeros((SUBLANES, tn), F32)

        g = g_ref[...].astype(F32)
        before = jnp.where(i < ni - 1, gh_ref[...].astype(F32), 0.0)
        row = lax.broadcasted_iota(jnp.int32, (tm, tn), 0)
        g1, g2 = _shift_down(g, before[SUBLANES:], row)
        c, cdf = _conv_act(g, g1, g2, cw_ref, cb_ref)
        da = _nt(dh_ref[...], wd_ref[...])
        dv_ref[...] = (da * (c * cdf)).astype(BF16)
        pdf = jnp.exp(-0.5 * c * c) * (1.0 / (2.0 * jnp.pi) ** 0.5)
        dc = da * v_ref[...].astype(F32) * (cdf + c * pdf)
        d1, d2 = _shift_up(dc, halo_ref[j], row)
        dg_ref[...] = (cw_ref[2:3, :] * dc + cw_ref[1:2, :] * d1 + cw_ref[0:1, :] * d2).astype(BF16)
        halo_ref[j] = dc[:SUBLANES, :]
        for k, t in enumerate((dc * g2, dc * g1, dc * g, dc)):
            acc_ref[j, k:k + 1, :] += jnp.sum(t, axis=0, keepdims=True)

        @pl.when((i == ni - 1) & (j == nf - 1))
        def _():
            for jj in range(nf):
                dcw_ref[:, jj * tn:(jj + 1) * tn] = acc_ref[jj, 0:3, :]
                dcb_ref[:, jj * tn:(jj + 1) * tn] = acc_ref[jj, 3:4, :]

    tile = pl.BlockSpec((tm, tn), lambda i, j: (ni - 1 - i, j))
    return pl.pallas_call(
        body, name=name, grid=(ni, nf),
        in_specs=[pl.BlockSpec((tm, D), lambda i, j: (ni - 1 - i, 0)), pl.BlockSpec((tn, D), lambda i, j: (j, 0)),
                  tile, pl.BlockSpec((SUBLANES * 2, tn), lambda i, j: (jnp.maximum((ni - 1 - i) * (hb // 2) - 1, 0), j)),
                  tile, pl.BlockSpec((3, tn), lambda i, j: (0, j)), pl.BlockSpec((1, tn), lambda i, j: (0, j))],
        out_specs=[tile, tile, pl.BlockSpec((3, F), lambda i, j: (0, 0)), pl.BlockSpec((1, F), lambda i, j: (0, 0))],
        out_shape=[jax.ShapeDtypeStruct((S, F), BF16), jax.ShapeDtypeStruct((S, F), BF16),
                   jax.ShapeDtypeStruct((3, F), F32), jax.ShapeDtypeStruct((1, F), F32)],
        scratch_shapes=[pltpu.VMEM((nf, SUBLANES, tn), F32), pltpu.VMEM((nf, SUBLANES, tn), F32)],
        compiler_params=_cparams("arbitrary", "arbitrary"),
    )(dh2b, w_down, gate, gate, val, conv_w, conv_b)


def _down_loss(act, w_down, h1, g, target, *, name, tm=512):
    S, F = act.shape
    D = h1.shape[1]

    def body(a_ref, w_ref, h_ref, g_ref, t_ref, dh_ref, dhb_ref, dg_ref, loss_ref):
        first = pl.program_id(0) == 0
        h2 = h_ref[...] + _nn(a_ref[...], w_ref[...])
        gv = g_ref[...]
        r = _rstd(h2)
        xh = h2 * r
        err = xh * gv - t_ref[...]
        part_loss = 0.5 * jnp.sum(jnp.mean(err * err, axis=-1, keepdims=True), axis=0, keepdims=True)
        dy = err * (1.0 / D)
        dxh = dy * gv
        dh = r * (dxh - xh * jnp.mean(dxh * xh, axis=-1, keepdims=True))
        dh_ref[...] = dh
        dhb_ref[...] = dh.astype(BF16)
        _accumulate(dg_ref, jnp.sum(dy * xh, axis=0, keepdims=True), first)
        _accumulate(loss_ref, jnp.broadcast_to(part_loss, (1, LANES)), first)

    row = lambda w: pl.BlockSpec((tm, w), lambda i: (i, 0))
    vec = lambda w: pl.BlockSpec((1, w), lambda i: (0, 0))
    return pl.pallas_call(
        body, name=name, grid=(S // tm,), in_specs=[row(F), _resident(w_down.shape), row(D), vec(D), row(D)],
        out_specs=[row(D), row(D), vec(D), vec(LANES)],
        out_shape=[jax.ShapeDtypeStruct((S, D), F32), jax.ShapeDtypeStruct((S, D), BF16),
                   jax.ShapeDtypeStruct((1, D), F32), jax.ShapeDtypeStruct((1, LANES), F32)],
        compiler_params=_cparams("arbitrary"),
    )(act, w_down, h1, g, target)


def _grad_norm_input(pieces, w, x, g, add, *, name, tm=512):
    S, D = x.shape
    widths = [p.shape[1] for p in pieces]
    offs = [sum(widths[:k]) for k in range(len(widths))]
    n = len(pieces)

    def body(*refs):
        p_refs = refs[:n]
        w_ref, x_ref, g_ref, add_ref, dx_ref, dxb_ref, dg_ref = refs[n:]
        du = _nt(p_refs[0][...], w_ref[:, offs[0]:offs[0] + widths[0]])
        for k in range(1, n):
            du = du + _nt(p_refs[k][...], w_ref[:, offs[k]:offs[k] + widths[k]])
        dx, dg = _norm_bwd(x_ref[...], g_ref[...], du)
        dx = add_ref[...] + dx
        dx_ref[...] = dx
        dxb_ref[...] = dx.astype(BF16)
        _accumulate(dg_ref, jnp.sum(dg, axis=0, keepdims=True), pl.program_id(0) == 0)

    row = lambda w_: pl.BlockSpec((tm, w_), lambda i: (i, 0))
    vec = pl.BlockSpec((1, D), lambda i: (0, 0))
    return pl.pallas_call(
        body, name=name, grid=(S // tm,),
        in_specs=[row(wd) for wd in widths] + [_resident(w.shape), row(D), vec, row(D)],
        out_specs=[row(D), row(D), vec],
        out_shape=[jax.ShapeDtypeStruct((S, D), F32), jax.ShapeDtypeStruct((S, D), BF16),
                   jax.ShapeDtypeStruct((1, D), F32)],
        compiler_params=_cparams("arbitrary"),
    )(*pieces, w, x, g, add)


def _rows(a):
    return a.reshape(-1, a.shape[-1])


def _row_tile(rows, cols, itemsize=4, budget=1 << 20):
    t = rows
    while t % 32 == 0 and t * cols * itemsize > budget:
        t //= 2
    return t


def _sum_cast(arrs, out_dtype, *, name):
    shape = arrs[0].shape
    flat = [_rows(a) for a in arrs]
    R, C = flat[0].shape
    tr = _row_tile(R, C)

    def body(*refs):
        acc = refs[0][...].astype(F32)
        for r in refs[1:-1]:
            acc = acc + r[...].astype(F32)
        refs[-1][...] = acc.astype(out_dtype)

    spec = pl.BlockSpec((tr, C), lambda i: (i, 0))
    return pl.pallas_call(
        body, name=name, grid=(R // tr,), in_specs=[spec] * len(flat), out_specs=spec,
        out_shape=jax.ShapeDtypeStruct((R, C), out_dtype), compiler_params=_cparams("parallel"),
    )(*flat).reshape(shape)


def _adamw(parts, w, m, v, *, name):
    shape = w.shape
    parts = [_rows(p) for p in parts]
    w2, m2, v2 = _rows(w), _rows(m), _rows(v)
    R, C = w2.shape
    tr = _row_tile(R, C)
    np_ = len(parts)
    c1, c2 = 1.0 - ADAM_B1 ** ADAM_STEP, 1.0 - ADAM_B2 ** ADAM_STEP

    def body(*refs):
        g = refs[0][...].astype(F32)
        for r in refs[1:np_]:
            g = g + r[...].astype(F32)
        w_ref, m_ref, v_ref, g_out, d_out, m_out, v_out = refs[np_:]
        mn = ADAM_B1 * m_ref[...] + (1.0 - ADAM_B1) * g
        vn = ADAM_B2 * v_ref[...] + (1.0 - ADAM_B2) * (g * g)
        g_out[...] = g
        d_out[...] = -ADAM_LR * ((mn / c1) / (jnp.sqrt(vn / c2) + ADAM_EPS) + ADAM_WD * w_ref[...])
        m_out[...] = mn
        v_out[...] = vn

    spec = pl.BlockSpec((tr, C), lambda i: (i, 0))
    out = jax.ShapeDtypeStruct((R, C), F32)
    res = pl.pallas_call(
        body, name=name, grid=(R // tr,), in_specs=[spec] * (np_ + 3), out_specs=[spec] * 4,
        out_shape=[out] * 4, compiler_params=_cparams("parallel"),
    )(*parts, w2, m2, v2)
    return [r.reshape(shape) for r in res]


def _coords():
    return lax.axis_index("x"), lax.axis_index("y"), lax.axis_index("c")


def _all_gather(shards, *, name):
    n = len(shards)

    def body(*refs):
        x_refs, out_refs = refs[:n], refs[n:2 * n]
        send_sems, recv_sems, local_sems = refs[2 * n:]
        x, y, c = _coords()
        me, sibling = (x, y, c), (x, y, 1 - c)
        chips = [(1 - x, y), (x, 1 - y), (1 - x, 1 - y)]

        def slot(a, dev):
            return out_refs[a].at[4 * dev[0] + 2 * dev[1] + dev[2]]

        def copy(a, k, block, to, src=None):
            return pltpu.make_async_remote_copy(
                src_ref=slot(a, block) if src is None else src, dst_ref=slot(a, block),
                send_sem=send_sems.at[7 * a + k], recv_sem=recv_sems.at[7 * a + k], device_id=to, device_id_type=MESH)

        mine = [pltpu.make_async_copy(x_refs[a], slot(a, me), local_sems.at[a]) for a in range(n)]
        for cp in mine:
            cp.start()
        first = []
        for a in range(n):
            first.append(copy(a, 0, me, sibling, src=x_refs[a]))
            first += [copy(a, 1 + j, me, (*chip, c), src=x_refs[a]) for j, chip in enumerate(chips)]
        for cp in first:
            cp.start()
        passed = []
        for j, chip in enumerate(chips):
            for a in range(n):
                copy(a, 1 + j, (*chip, c), me).wait_recv()
                fwd = copy(a, 4 + j, (*chip, c), sibling)
                fwd.start()
                passed.append(fwd)
        for a in range(n):
            copy(a, 0, sibling, me).wait_recv()
            for j, chip in enumerate(chips):
                copy(a, 4 + j, (*chip, 1 - c), me).wait_recv()
        for cp in first + passed:
            cp.wait_send()
        for cp in mine:
            cp.wait()

    return pl.pallas_call(
        body, name=name, in_specs=[HBM] * n, out_specs=[HBM] * n,
        out_shape=[jax.ShapeDtypeStruct((N_DEV, *s.shape), s.dtype) for s in shards],
        scratch_shapes=[pltpu.SemaphoreType.DMA((7 * n,)), pltpu.SemaphoreType.DMA((7 * n,)),
                        pltpu.SemaphoreType.DMA((n,))],
    )(*shards)


def _flip_y(x, y, c):
    return (x, 1 - y, c)


def _flip_x(x, y, c):
    return (1 - x, y, c)


def _flip_xy(x, y, c):
    return (1 - x, 1 - y, c)


SEM = pl.BlockSpec(memory_space=pltpu.SEMAPHORE)
SIDE_EFFECT = pltpu.SideEffectType.DATAFLOW_SIDE_EFFECTING


def _in_hbm(a):
    return pltpu.with_memory_space_constraint(a, pltpu.HBM)


def _copies_start(srcs, lands, plan, n_copies, *, name):
    ns, nl = len(srcs), len(lands)

    def body(*refs):
        src_refs, land_refs = refs[:ns], refs[ns:ns + nl]
        send_sems, recv_sems = refs[ns + nl:ns + nl + 2]
        token = refs[-1]
        for k, (src, dst, peer, _) in enumerate(plan(src_refs, land_refs, *_coords())):
            pltpu.make_async_remote_copy(src_ref=src, dst_ref=dst, send_sem=send_sems.at[k], recv_sem=recv_sems.at[k],
                                         device_id=peer, device_id_type=MESH).start()
        token[...] = jnp.zeros_like(token)

    bufs = [*srcs, *lands]
    res = pl.pallas_call(
        body, name=name, in_specs=[HBM] * (ns + nl),
        out_specs=(SEM, SEM, *[HBM] * (ns + nl), pl.BlockSpec(memory_space=pltpu.VMEM)),
        out_shape=(pltpu.SemaphoreType.DMA((n_copies,)), pltpu.SemaphoreType.DMA((n_copies,)),
                   *[pltpu.HBM(b.shape, b.dtype) for b in bufs], jax.ShapeDtypeStruct((SUBLANES, LANES), F32)),
        input_output_aliases={i: 2 + i for i in range(ns + nl)},
        compiler_params=pltpu.CompilerParams(has_side_effects=SIDE_EFFECT),
    )(*[_in_hbm(b) for b in bufs])
    return res[0], res[1], list(res[2:2 + ns]), list(res[2 + ns:2 + ns + nl]), res[-1]


def _copies_wait(started, plan, after, *, name):
    send_sems, recv_sems, srcs, lands, _ = started
    ns, nl = len(srcs), len(lands)

    def body(*refs):
        src_refs, land_refs = refs[:ns], refs[ns:ns + nl]
        send_sems, recv_sems = refs[ns + nl:ns + nl + 2]
        for k, (src, dst, peer, here) in enumerate(plan(src_refs, land_refs, *_coords())):
            pltpu.make_async_remote_copy(src_ref=src, dst_ref=dst, send_sem=send_sems.at[k], recv_sem=recv_sems.at[k],
                                         device_id=peer, device_id_type=MESH).wait_send()
            pltpu.make_async_remote_copy(src_ref=src, dst_ref=here, send_sem=send_sems.at[k], recv_sem=recv_sems.at[k],
                                         device_id=peer, device_id_type=MESH).wait_recv()

    bufs = [*srcs, *lands]
    res = pl.pallas_call(
        body, name=name, in_specs=[HBM] * (ns + nl) + [SEM, SEM, pl.BlockSpec(memory_space=pl.ANY)],
        out_specs=[HBM] * (ns + nl), out_shape=[pltpu.HBM(b.shape, b.dtype) for b in bufs],
        input_output_aliases={i: i for i in range(ns + nl)},
        compiler_params=pltpu.CompilerParams(has_side_effects=SIDE_EFFECT),
    )(*bufs, send_sems, recv_sems, after)
    return list(res[ns:])


def _dev_index(dev):
    return 4 * dev[0] + 2 * dev[1] + dev[2]


def _ag_chips_plan(src_refs, land_refs, x, y, c):
    me = _dev_index((x, y, c))
    return [(src, land.at[me], peer, land.at[_dev_index(peer)])
            for src, land in zip(src_refs, land_refs) for peer in (_flip_y(x, y, c), _flip_x(x, y, c), _flip_xy(x, y, c))]


def _ag_sibling_plan(src_refs, land_refs, x, y, c):
    chips = [(x, y), (x, 1 - y), (1 - x, y), (1 - x, 1 - y)]
    return [(land.at[_dev_index((*chip, c))], land.at[_dev_index((*chip, c))], (x, y, 1 - c),
             land.at[_dev_index((*chip, 1 - c))]) for land in land_refs for chip in chips]


def _rs_direct_plan(src_refs, land_refs, x, y, c):
    plan = []
    for src, land in zip(src_refs, land_refs):
        for m in range(1, N_DEV):
            peer = (x + (m >> 2) * (1 - 2 * x), y + ((m >> 1) & 1) * (1 - 2 * y), c + (m & 1) * (1 - 2 * c))
            plan.append((src.at[_dev_index(peer)], land.at[m - 1], peer, land.at[m - 1]))
    return plan


def _rs_start(grads, me, *, name):
    own = [lax.dynamic_index_in_dim(g, me, 0, keepdims=False) for g in grads]
    lands = [lax.empty((N_DEV - 1, *g.shape[1:]), g.dtype) for g in grads]
    return _copies_start(grads, lands, _rs_direct_plan, (N_DEV - 1) * len(grads), name=name), own


def _rs_finish(started, after, *, name):
    handle, own = started
    got = _copies_wait(handle, _rs_direct_plan, after, name=name)
    return [[o] + [land[m] for m in range(N_DEV - 1)] for o, land in zip(own, got)]


def _by_device_cols(w):
    K, N = w.shape
    return w.reshape(K, N_DEV, N // N_DEV).transpose(1, 0, 2)


def _gathered_cols(w8):
    return w8.transpose(1, 0, 2).reshape(w8.shape[1], -1)


def _pair_major(w, inverse=False):
    K = w.shape[0]
    a, b = (ATTN_W // LANES, 3) if inverse else (3, ATTN_W // LANES)
    return w.reshape(K, a, b, LANES).transpose(0, 2, 1, 3).reshape(K, 3 * ATTN_W)


def kernel(x, norm1_g, w_in, attn_norm_g, hgrn_norm_g, hgrn_lb_logits, w_out, norm2_g, w_up, conv_w, conv_b, w_down, final_norm_g, loss_target, m_norm1_g, m_w_in, m_attn_norm_g, m_hgrn_norm_g, m_hgrn_lb_logits, m_w_out, m_norm2_g, m_w_up, m_conv_w, m_conv_b, m_w_down, m_final_norm_g, v_norm1_g, v_w_in, v_attn_norm_g, v_hgrn_norm_g, v_hgrn_lb_logits, v_w_out, v_norm2_g, v_w_up, v_conv_w, v_conv_b, v_w_down, v_final_norm_g):
    xs, target = x[0], loss_target[0]
    S, D = xs.shape
    NA = 3 * ATTN_W
    fng = final_norm_g.reshape(1, D)

    casts = [_sum_cast([w[0]], BF16, name=f"cast_{nm}") for nm, w in
             (("w_in", w_in), ("w_out", w_out), ("w_up", w_up), ("w_down", w_down))]
    me = _dev_index(_coords())
    (g_in,) = _all_gather(casts[:1], name="ag_w_in")
    later = casts[1:] + [conv_w[0]]
    ag1 = _copies_start(later, [lax.empty((N_DEV, *s.shape), s.dtype) for s in later], _ag_chips_plan,
                        3 * len(later), name="ag_chips_start")
    wi = _gathered_cols(g_in)
    wi = jnp.concatenate([_pair_major(wi[:, :NA]), wi[:, NA:]], axis=1)

    u1, proj_a = _proj_attn(xs, norm1_g + ag1[4][0, 0], wi[:, :NA], name="proj_attn")
    proj_h = _mm(u1, wi[:, NA:], name="proj_hgrn")
    attn, lse = _attn_fwd(proj_a, name="attn_fwd")
    lands = _copies_wait(ag1, _ag_chips_plan, attn, name="ag_chips_wait")
    lands = [lax.dynamic_update_index_in_dim(l, s, me, 0) for l, s in zip(lands, later)]
    ag2 = _copies_start([], lands, _ag_sibling_plan, 4 * len(later), name="ag_sibling_start")
    rec, states = _hgrn_fwd(proj_h, hgrn_lb_logits + ag2[4][0, 0], name="hgrn_fwd")
    g_out, g_up, g_down, g_cw = _copies_wait(ag2, _ag_sibling_plan, rec, name="ag_sibling_wait")
    wo = g_out.reshape(-1, D)
    wu = _gathered_cols(g_up)
    wd = g_down.reshape(-1, D)
    cw = _gathered_cols(g_cw)
    h1, u2, mixed = _out_proj(attn, rec, proj_h, xs, attn_norm_g, hgrn_norm_g, norm2_g, wo, name="out_proj")
    gate, val, act = _up_glu(u2, wu, cw, conv_b, name="up_glu")
    dh2, dh2b, d_fng, loss_part = _down_loss(act, wd, h1, fng, target, name="down_loss")

    dgate, dval, d_cw, d_cb = _dact_glu_bwd(dh2b, wd, gate, val, cw, conv_b, name="dact_glu_bwd")
    dw_down = _mm_tn(act, dh2b, tm=256, name="dw_down")
    dh1, dh1b, d_n2g = _grad_norm_input([dgate, dval], wu, h1, norm2_g, dh2, name="du2_norm2_bwd")
    dw_up = [_mm_tn(u2, dy, tn=256, name=f"dw_up_{nm}") for nm, dy in (("gate", dgate), ("val", dval))]
    rs_ffn = _rs_start([dw_down.reshape(N_DEV, -1, D),
                        jnp.concatenate([h.reshape(D, N_DEV // 2, -1).transpose(1, 0, 2) for h in dw_up], axis=0)],
                       me, name="rs_ffn_start")
    dattn, delta, drec, dhg, d_ang, d_hng = _dmix_post_bwd(dh1b, wo, attn, rec, proj_h, attn_norm_g + rs_ffn[0][4][0, 0],
                                                          hgrn_norm_g, name="dmix_post_bwd")
    dw_out = _mm_tn(mixed, dh1b, name="dw_out")
    rs_out = _rs_start([dw_out.reshape(N_DEV, -1, D)], me, name="rs_out_start")
    dproj_h, d_lbl = _hgrn_bwd(proj_h, hgrn_lb_logits + rs_out[0][4][0, 0], states, drec, name="hgrn_bwd")
    dproj_a = _attn_bwd(proj_a, dattn, lse, delta, name="attn_bwd")
    dw_in = jnp.concatenate([_pair_major(_mm_tn(u1, dproj_a, tn=512, name="dw_in_attn"), inverse=True),
                             _mm_tn(u1, dproj_h, tn=512, name="dw_in_hgrn"),
                             _mm_tn(u1, dhg, tn=512, name="dw_in_gate")], axis=1)
    rs_in = _rs_start([_by_device_cols(dw_in)], me, name="rs_in_start")
    grad_x, _, d_n1g = _grad_norm_input([dproj_a, dproj_h, dhg], wi, xs, norm1_g + rs_in[0][4][0, 0], dh1,
                                        name="du1_norm1_bwd")

    res = {}

    def update(nm, parts, w, m, v):
        res[nm] = _adamw([p.reshape(w.shape) for p in parts], w, m, v, name=f"adamw_{nm}")

    g_down, g_up = _rs_finish(rs_ffn, grad_x, name="rs_ffn_wait")
    update("w_down", g_down, w_down, m_w_down, v_w_down)
    update("w_up", g_up, w_up, m_w_up, v_w_up)
    (g_out,) = _rs_finish(rs_out, grad_x, name="rs_out_wait")
    update("w_out", g_out, w_out, m_w_out, v_w_out)
    (g_in,) = _rs_finish(rs_in, res["w_up"][1], name="rs_in_wait")
    update("w_in", g_in, w_in, m_w_in, v_w_in)

    small = [("norm1_g", d_n1g, norm1_g, m_norm1_g, v_norm1_g),
             ("attn_norm_g", d_ang, attn_norm_g, m_attn_norm_g, v_attn_norm_g),
             ("hgrn_norm_g", d_hng, hgrn_norm_g, m_hgrn_norm_g, v_hgrn_norm_g),
             ("hgrn_lb_logits", d_lbl, hgrn_lb_logits, m_hgrn_lb_logits, v_hgrn_lb_logits),
             ("norm2_g", d_n2g, norm2_g, m_norm2_g, v_norm2_g),
             ("conv_b", d_cb, conv_b, m_conv_b, v_conv_b),
             ("final_norm_g", d_fng, final_norm_g, m_final_norm_g, v_final_norm_g)]
    pack = lambda arrs: jnp.concatenate([a.reshape(1, -1) for a in arrs], axis=1)
    g_small, g_dcw = _all_gather([pack([s[1] for s in small]), d_cw], name="ag_small_grads")
    sm = _adamw([g_small[j] for j in range(N_DEV)], pack([s[2] for s in small]), pack([s[3] for s in small]),
                pack([s[4] for s in small]), name="adamw_small")
    off = 0
    for nm, _, w, _, _ in small:
        res[nm] = [r[:, off:off + w.size].reshape(w.shape) for r in sm]
        off += w.size
    ncw = conv_w.shape[-1]
    mine_cw = lax.dynamic_slice_in_dim(g_dcw, me * ncw, ncw, axis=2)
    res["conv_w"] = _adamw([mine_cw[j].reshape(conv_w.shape) for j in range(N_DEV)], conv_w, m_conv_w, v_conv_w,
                           name="adamw_conv_w")

    loss = lax.psum(loss_part[0, 0], MESH_AXES)
    order = ["norm1_g", "w_in", "attn_norm_g", "hgrn_norm_g", "hgrn_lb_logits", "w_out", "norm2_g", "w_up",
             "conv_w", "conv_b", "w_down", "final_norm_g"]
    return (loss, grad_x[None], *[res[nm][0] for nm in order], *[res[nm][1] for nm in order],
            *[res[nm][2] for nm in order], *[res[nm][3] for nm in order])
```

```python
import jax
import jax.numpy as jnp
from jax import lax
from jax.experimental import pallas as pl
from jax.experimental.pallas import tpu as pltpu

F32, BF16 = jnp.float32, jnp.bfloat16
NORM_EPS = 1e-6
ATTN_HEADS, HEAD_DIM, ATTN_BLOCK = 8, 64, 128
DILATIONS = (1, 4, 16)
ATTN_SCALE = HEAD_DIM ** -0.5
ATTN_W = ATTN_HEADS * HEAD_DIM
HGRN_HEADS, HGRN_DIM, HGRN_CHUNK = 4, 128, 64
HGRN_W = HGRN_HEADS * HGRN_DIM
ADAM_LR, ADAM_B1, ADAM_B2, ADAM_EPS, ADAM_WD, ADAM_STEP = 0.001, 0.9, 0.999, 1e-08, 0.01, 10
LANES, SUBLANES = 128, 8
VMEM_LIMIT_BYTES = 56 * 1024 * 1024
N_DEV = 8
MESH_AXES = ("x", "y", "c")
MESH = pl.DeviceIdType.MESH
HBM = pl.BlockSpec(memory_space=pltpu.HBM)
HIGHEST = lax.Precision.HIGHEST


def _cparams(*sem):
    return pltpu.CompilerParams(dimension_semantics=sem, vmem_limit_bytes=VMEM_LIMIT_BYTES)


def _tile(n, pref):
    if n <= pref:
        return n
    t = (pref // LANES) * LANES
    while n % t:
        t -= LANES
    return t


def _resident(shape):
    return pl.BlockSpec(shape, lambda *_: (0,) * len(shape), pipeline_mode=pl.Buffered(1))


def _dot(a, b, dims, precision=None):
    return lax.dot_general(a, b, (dims, ((), ())), precision=precision, preferred_element_type=F32)


def _nn(a, b, precision=None):
    return _dot(a, b, ((1,), (0,)), precision)


def _nt(a, b):
    return _dot(a, b, ((1,), (1,)))


def _tn(a, b):
    return _dot(a, b, ((0,), (0,)))


def _sigmoid(x):
    return 1.0 / (1.0 + jnp.exp(-x))


def _rstd(x):
    return lax.rsqrt(jnp.mean(x * x, axis=-1, keepdims=True) + NORM_EPS)


def _norm_bwd(x, g, du):
    r = _rstd(x)
    xh = x * r
    dxh = du * g
    return r * (dxh - xh * jnp.mean(dxh * xh, axis=-1, keepdims=True)), du * xh


def _accumulate(ref, part, first):
    @pl.when(first)
    def _():
        ref[...] = part

    @pl.when(jnp.logical_not(first))
    def _():
        ref[...] += part


def _mm(a, b, *, name, out_dtype=F32, tm=1024, tn=512):
    M, K = a.shape
    N = b.shape[1]
    tm, tn = _tile(M, tm), _tile(N, tn)

    def body(a_ref, b_ref, o_ref):
        o_ref[...] = _nn(a_ref[...], b_ref[...]).astype(out_dtype)

    return pl.pallas_call(
        body, name=name, grid=(M // tm, N // tn),
        in_specs=[pl.BlockSpec((tm, K), lambda i, j: (i, 0)), pl.BlockSpec((K, tn), lambda i, j: (0, j))],
        out_specs=pl.BlockSpec((tm, tn), lambda i, j: (i, j)), out_shape=jax.ShapeDtypeStruct((M, N), out_dtype),
        compiler_params=_cparams("parallel", "parallel"),
    )(a, b)


def _mm_tn(x, dy, *, name, tm=512, tn=1024):
    S, M = x.shape
    N = dy.shape[1]
    tm, tn = _tile(M, tm), _tile(N, tn)

    def body(x_ref, dy_ref, o_ref, xt_ref):
        @pl.when(pl.program_id(1) == 0)
        def _():
            xt_ref[...] = x_ref[...].T

        o_ref[...] = _nn(xt_ref[...], dy_ref[...]).astype(BF16)

    return pl.pallas_call(
        body, name=name, grid=(M // tm, N // tn),
        in_specs=[pl.BlockSpec((S, tm), lambda i, j: (0, i)), pl.BlockSpec((S, tn), lambda i, j: (0, j))],
        out_specs=pl.BlockSpec((tm, tn), lambda i, j: (i, j)), out_shape=jax.ShapeDtypeStruct((M, N), BF16),
        scratch_shapes=[pltpu.VMEM((tm, S), BF16)], compiler_params=_cparams("parallel", "arbitrary"),
    )(x, dy)


def _proj_attn(x, g, w, *, name, tm=1024, tn=512):
    S, D = x.shape
    N = w.shape[1]

    def body(x_ref, g_ref, w_ref, u_ref, o_ref):
        @pl.when(pl.program_id(1) == 0)
        def _():
            xv = x_ref[...]
            u_ref[...] = (xv * _rstd(xv) * g_ref[...]).astype(BF16)

        o_ref[...] = _nn(u_ref[...], w_ref[...]).astype(BF16)

    return pl.pallas_call(
        body, name=name, grid=(S // tm, N // tn),
        in_specs=[pl.BlockSpec((tm, D), lambda i, j: (i, 0)), pl.BlockSpec((1, D), lambda i, j: (0, 0)),
                  pl.BlockSpec((D, tn), lambda i, j: (0, j))],
        out_specs=[pl.BlockSpec((tm, D), lambda i, j: (i, 0)), pl.BlockSpec((tm, tn), lambda i, j: (i, j))],
        out_shape=[jax.ShapeDtypeStruct((S, D), BF16), jax.ShapeDtypeStruct((S, N), BF16)],
        compiler_params=_cparams("parallel", "arbitrary"),
    )(x, g, w)


PAIR_W = 3 * LANES
ATTN_UNROLL_FWD, ATTN_UNROLL_BWD = 4, 4


def _attn_masks(first):
    qi = lax.broadcasted_iota(jnp.int32, (ATTN_BLOCK, 2 * ATTN_BLOCK), 0)
    kj = lax.broadcasted_iota(jnp.int32, (ATTN_BLOCK, 2 * ATTN_BLOCK), 1)
    dist = qi + ATTN_BLOCK - kj
    valid = (dist >= 0) & (dist <= ATTN_BLOCK) & jnp.logical_or(kj >= ATTN_BLOCK, jnp.logical_not(first))
    lane = lax.broadcasted_iota(jnp.int32, (1, LANES), 1)
    return valid, lane


def _for_residue_blocks(S, d, fn):
    span = ATTN_BLOCK * d
    nb = S // span

    def step(n, carry):
        base = pl.multiple_of(n * span, span)
        for r in range(d):
            off = pl.multiple_of((r * nb + n) * ATTN_BLOCK, ATTN_BLOCK)
            fn(lambda ref, r=r: _block_rows(ref, base, r, d),
               lambda ref, val, r=r: _set_block_rows(ref, base, r, d, val), off)
        return carry

    lax.fori_loop(0, nb, step, 0)


def _for_blocks(S, unroll, fn):
    def step(i, carry):
        fn([(pl.multiple_of((i * unroll + u) * ATTN_BLOCK, ATTN_BLOCK), i * unroll + u) for u in range(unroll)])
        return carry

    lax.fori_loop(0, S // ATTN_BLOCK // unroll, step, 0)


def _head_value(x2, lane, e):
    return jnp.sum(jnp.where(lane == HEAD_DIM * e, x2, 0.0), axis=-1, keepdims=True)


def _block_rows(ref, base, r, d):
    if d == 1:
        return ref[pl.ds(base, ATTN_BLOCK), :]
    return ref.at[pl.ds(base, ATTN_BLOCK * d)][pl.ds(r, ATTN_BLOCK, stride=d), :]


def _set_block_rows(ref, base, r, d, val):
    if d == 1:
        ref[pl.ds(base, ATTN_BLOCK), :] = val
    else:
        ref.at[pl.ds(base, ATTN_BLOCK * d)][pl.ds(r, ATTN_BLOCK, stride=d), :] = val


def _split_pair(p_ref, qs, ks, vs, bk, bv):
    qs[...] = p_ref[:, 0:LANES].astype(F32)
    ks[...] = p_ref[:, LANES:2 * LANES].astype(F32)
    vs[...] = p_ref[:, 2 * LANES:3 * LANES].astype(F32)
    bk[0:ATTN_BLOCK, :] = jnp.zeros((ATTN_BLOCK, LANES), bk.dtype)
    bv[0:ATTN_BLOCK, :] = jnp.zeros((ATTN_BLOCK, LANES), bv.dtype)


def _regroup_qkv(rows, off, qs, ks, vs, bq, bk, bv):
    blk, shifted = pl.ds(off, ATTN_BLOCK), pl.ds(off + ATTN_BLOCK, ATTN_BLOCK)
    bq[blk, :] = rows(qs).astype(BF16)
    bk[shifted, :] = rows(ks).astype(BF16)
    bv[shifted, :] = rows(vs).astype(BF16)


def _attn_fwd(proj_a, *, name):
    S = proj_a.shape[0]

    def body(p_ref, o_ref, l_ref, qs, ks, vs, bq, bk, bv, bo, bl):
        _split_pair(p_ref, qs, ks, vs, bk, bv)
        for d in DILATIONS:
            nb = S // (ATTN_BLOCK * d)
            _for_residue_blocks(S, d, lambda rows, _, off: _regroup_qkv(rows, off, qs, ks, vs, bq, bk, bv))

            def blocks(group, nb=nb):
                lane = lax.broadcasted_iota(jnp.int32, (1, LANES), 1)
                heads = [(lane >= HEAD_DIM * e) & (lane < HEAD_DIM * (e + 1)) for e in range(LANES // HEAD_DIM)]
                wins = [pl.ds(off, 2 * ATTN_BLOCK) for off, _ in group]
                s = [[_nt(jnp.where(mh, bq[pl.ds(off, ATTN_BLOCK), :], jnp.zeros((ATTN_BLOCK, LANES), BF16)), bk[win, :])
                      for mh in heads] for (off, _), win in zip(group, wins)]
                p, m, l = [], [], []
                for (off, b), su in zip(group, s):
                    valid, _ = _attn_masks(jnp.bitwise_and(b, nb - 1) == 0)
                    sm = [jnp.where(valid, x * ATTN_SCALE, -jnp.inf) for x in su]
                    m.append([jnp.max(x, axis=-1, keepdims=True) for x in sm])
                    p.append([jnp.exp(x - mx) for x, mx in zip(sm, m[-1])])
                    l.append([jnp.sum(x, axis=-1, keepdims=True) for x in p[-1]])
                o = [[_nn(x.astype(BF16), bv[win, :]) for x in pu] for pu, win in zip(p, wins)]
                for (off, _), ou, mu, lu in zip(group, o, m, l):
                    o2 = jnp.zeros((ATTN_BLOCK, LANES), F32)
                    l2 = jnp.zeros((ATTN_BLOCK, LANES), F32)
                    for mh, oe, me_, le in zip(heads, ou, mu, lu):
                        o2 = jnp.where(mh, oe / le, o2)
                        l2 = jnp.where(mh, me_ + jnp.log(le), l2)
                    bo[pl.ds(off, ATTN_BLOCK), :] = o2
                    bl[pl.ds(off, ATTN_BLOCK), :] = l2

            _for_blocks(S, ATTN_UNROLL_FWD, blocks)

            def merge(rows, set_rows, off, d=d):
                blk = pl.ds(off, ATTN_BLOCK)
                o2, l2 = bo[blk, :], bl[blk, :]
                if d != DILATIONS[0]:
                    lo, oo = rows(l_ref), rows(o_ref)
                    ln = jnp.maximum(lo, l2)
                    wa, wb = jnp.exp(lo - ln), jnp.exp(l2 - ln)
                    o2 = (wa * oo + wb * o2) / (wa + wb)
                    l2 = ln + jnp.log(wa + wb)
                set_rows(o_ref, o2)
                set_rows(l_ref, l2)

            _for_residue_blocks(S, d, merge)

    slab = pl.BlockSpec((S, LANES), lambda p: (0, p))
    f32_slab, bf16_slab = pltpu.VMEM((S, LANES), F32), pltpu.VMEM((S, LANES), BF16)
    bf16_window = pltpu.VMEM((S + ATTN_BLOCK, LANES), BF16)
    return pl.pallas_call(
        body, name=name, grid=(ATTN_W // LANES,), in_specs=[pl.BlockSpec((S, PAIR_W), lambda p: (0, p))],
        out_specs=[slab, slab],
        out_shape=[jax.ShapeDtypeStruct((S, ATTN_W), F32), jax.ShapeDtypeStruct((S, ATTN_W), F32)],
        scratch_shapes=[f32_slab] * 3 + [bf16_slab, bf16_window, bf16_window, f32_slab, f32_slab],
        compiler_params=_cparams("parallel"),
    )(proj_a)


def _attn_bwd(proj_a, do, lse, delta, *, name):
    S = proj_a.shape[0]

    def body(p_ref, do_ref, lse_ref, dl_ref, o_ref, qs, ks, vs, dqs, dks, dvs, bq, bk, bv, bdo, blse, bdl, bdq, bdk, bdv):
        _split_pair(p_ref, qs, ks, vs, bk, bv)
        bdk[0:ATTN_BLOCK, :] = jnp.zeros((ATTN_BLOCK, LANES), F32)
        bdv[0:ATTN_BLOCK, :] = jnp.zeros((ATTN_BLOCK, LANES), F32)
        for d in DILATIONS:
            nb = S // (ATTN_BLOCK * d)

            def regroup(rows, _, off):
                _regroup_qkv(rows, off, qs, ks, vs, bq, bk, bv)
                blk = pl.ds(off, ATTN_BLOCK)
                bdo[blk, :] = rows(do_ref).astype(BF16)
                blse[blk, :] = rows(lse_ref)
                bdl[blk, :] = rows(dl_ref)

            _for_residue_blocks(S, d, regroup)

            def blocks(group, nb=nb):
                lane = lax.broadcasted_iota(jnp.int32, (1, LANES), 1)
                heads = [(lane >= HEAD_DIM * e) & (lane < HEAD_DIM * (e + 1)) for e in range(LANES // HEAD_DIM)]
                zero = jnp.zeros((ATTN_BLOCK, LANES), BF16)
                chains = [(off, b, e, mh) for off, b in group for e, mh in enumerate(heads)]
                qm = [jnp.where(mh, bq[pl.ds(off, ATTN_BLOCK), :], zero) for off, _, _, mh in chains]
                dom = [jnp.where(mh, bdo[pl.ds(off, ATTN_BLOCK), :], zero) for off, _, _, mh in chains]
                s = [_nt(x, bk[pl.ds(off, 2 * ATTN_BLOCK), :]) for x, (off, _, _, _) in zip(qm, chains)]
                dp = [_nt(x, bv[pl.ds(off, 2 * ATTN_BLOCK), :]) for x, (off, _, _, _) in zip(dom, chains)]
                p, ds = [], []
                for (off, b, e, _), sc, dpc in zip(chains, s, dp):
                    valid, _ = _attn_masks(jnp.bitwise_and(b, nb - 1) == 0)
                    blk = pl.ds(off, ATTN_BLOCK)
                    pc = jnp.where(valid, jnp.exp(sc * ATTN_SCALE - _head_value(blse[blk, :], lane, e)), 0.0)
                    ds.append((pc * (dpc - _head_value(bdl[blk, :], lane, e)) * ATTN_SCALE).astype(BF16))
                    p.append(pc.astype(BF16))
                dq = [_nn(x, bk[pl.ds(off, 2 * ATTN_BLOCK), :]) for x, (off, _, _, _) in zip(ds, chains)]
                dk = [_tn(x, y) for x, y in zip(ds, qm)]
                dv = [_tn(x, y) for x, y in zip(p, dom)]
                nh = len(heads)
                for u, (off, _) in enumerate(group):
                    dq2 = jnp.zeros((ATTN_BLOCK, LANES), F32)
                    for mh, x in zip(heads, dq[nh * u:nh * (u + 1)]):
                        dq2 = jnp.where(mh, x, dq2)
                    bdq[pl.ds(off, ATTN_BLOCK), :] = dq2
                    for acc, grads in ((bdk, dk), (bdv, dv)):
                        win_grad = sum(grads[nh * u + 1:nh * (u + 1)], grads[nh * u])
                        acc[pl.ds(off, ATTN_BLOCK), :] += win_grad[:ATTN_BLOCK]
                        acc[pl.ds(off + ATTN_BLOCK, ATTN_BLOCK), :] = win_grad[ATTN_BLOCK:]

            _for_blocks(S, ATTN_UNROLL_BWD, blocks)

            def scatter(rows, set_rows, off, d=d):
                blk, shifted = pl.ds(off, ATTN_BLOCK), pl.ds(off + ATTN_BLOCK, ATTN_BLOCK)
                for acc, part in ((dqs, bdq[blk, :]), (dks, bdk[shifted, :]), (dvs, bdv[shifted, :])):
                    set_rows(acc, part if d == DILATIONS[0] else rows(acc) + part)

            _for_residue_blocks(S, d, scatter)
        o_ref[:, 0:LANES] = dqs[...].astype(BF16)
        o_ref[:, LANES:2 * LANES] = dks[...].astype(BF16)
        o_ref[:, 2 * LANES:3 * LANES] = dvs[...].astype(BF16)

    slab = pl.BlockSpec((S, LANES), lambda p: (0, p), pipeline_mode=pl.Buffered(1))
    pair = pl.BlockSpec((S, PAIR_W), lambda p: (0, p))
    f32_slab, bf16_slab = pltpu.VMEM((S, LANES), F32), pltpu.VMEM((S, LANES), BF16)
    f32_window, bf16_window = pltpu.VMEM((S + ATTN_BLOCK, LANES), F32), pltpu.VMEM((S + ATTN_BLOCK, LANES), BF16)
    return pl.pallas_call(
        body, name=name, grid=(ATTN_W // LANES,), in_specs=[pair, slab, slab, slab], out_specs=pair,
        out_shape=jax.ShapeDtypeStruct(proj_a.shape, BF16),
        scratch_shapes=[f32_slab] * 6 + [bf16_slab, bf16_window, bf16_window, bf16_slab, f32_slab, f32_slab,
                                         f32_slab, f32_window, f32_window],
        compiler_params=_cparams("parallel"),
    )(proj_a, do, lse, delta)


HG_T = 2 * HGRN_CHUNK


def _hgrn_consts():
    row = lax.broadcasted_iota(jnp.int32, (HG_T, HG_T), 0)
    col = lax.broadcasted_iota(jnp.int32, (HG_T, HG_T), 1)
    same = (row >= HGRN_CHUNK) == (col >= HGRN_CHUNK)
    return row, same & (col <= row), same & (col >= row)


def _lower_bound(logits_ref):
    l0, l1 = logits_ref[0:1, :], logits_ref[1:2, :]
    mx = jnp.maximum(l0, l1)
    e0, e1 = jnp.exp(l0 - mx), jnp.exp(l1 - mx)
    return e0 / (e0 + e1)


def _hgrn_gates(q, f, lb, row, causal):
    C = HGRN_CHUNK
    sg = _sigmoid(f)
    forget = lb + (1.0 - lb) * sg
    key = 1.0 - forget
    logf = jnp.log(forget)
    b = _nn(jnp.where(causal, 1.0, 0.0).astype(F32), logf, HIGHEST)
    bend0 = jnp.sum(logf[:C], axis=0, keepdims=True)
    bend1 = jnp.sum(logf[C:], axis=0, keepdims=True)
    bend = jnp.where(row < C, bend0, bend1)
    eb, emb, eend = jnp.exp(b), jnp.exp(-b), jnp.exp(bend - b)
    sq = _sigmoid(q)
    return dict(sg=sg, forget=forget, key=key, bend0=bend0, bend1=bend1, eb=eb, emb=emb, eend=eend, sq=sq,
                qd=q * sq * eb, ki=key * emb, ke=key * eend)


def _hgrn_fwd(proj, logits, *, name):
    S = proj.shape[0]
    W, C = HGRN_W, HGRN_CHUNK

    def body(q_ref, f_ref, i_ref, lg_ref, rec_ref, st_ref, s_ref):
        @pl.when(pl.program_id(0) == 0)
        def _():
            s_ref[...] = jnp.zeros_like(s_ref)

        row, causal, _ = _hgrn_consts()
        lb_all = _lower_bound(lg_ref)
        for h in range(HGRN_HEADS):
            sl = slice(HGRN_DIM * h, HGRN_DIM * (h + 1))
            gt = _hgrn_gates(q_ref[:, sl], f_ref[:, sl], lb_all[:, sl], row, causal)
            qd, ki, ke = gt["qd"].astype(BF16), gt["ki"].astype(BF16), gt["ke"].astype(BF16)
            iv = i_ref[:, sl].astype(BF16)
            a = jnp.where(causal, _nt(qd, ki), 0.0)
            o = _nn(a.astype(BF16), iv)
            s0 = s_ref[h]
            st_ref[0, h] = s0
            o0 = _nt(qd[:C], s0.astype(BF16))
            s1 = jnp.exp(gt["bend0"]) * s0 + _tn(iv[:C], ke[:C])
            st_ref[1, h] = s1
            o1 = _nt(qd[C:], s1.astype(BF16))
            s_ref[h] = jnp.exp(gt["bend1"]) * s1 + _tn(iv[C:], ke[C:])
            rec_ref[:, sl] = o + jnp.concatenate([o0, o1], axis=0)

    blk = lambda j: pl.BlockSpec((HG_T, W), lambda t: (t, j))
    return pl.pallas_call(
        body, name=name, grid=(S // HG_T,),
        in_specs=[blk(0), blk(1), blk(2), pl.BlockSpec((2, W), lambda t: (0, 0))],
        out_specs=[blk(0), pl.BlockSpec((2, HGRN_HEADS, HGRN_DIM, HGRN_DIM), lambda t: (t, 0, 0, 0))],
        out_shape=[jax.ShapeDtypeStruct((S, W), F32),
                   jax.ShapeDtypeStruct((S // C, HGRN_HEADS, HGRN_DIM, HGRN_DIM), F32)],
        scratch_shapes=[pltpu.VMEM((HGRN_HEADS, HGRN_DIM, HGRN_DIM), F32)],
        compiler_params=_cparams("arbitrary"),
    )(proj, proj, proj, logits)


def _hgrn_bwd(proj, logits, states, drec, *, name):
    S = proj.shape[0]
    W, C = HGRN_W, HGRN_CHUNK
    nt = S // HG_T

    def body(q_ref, f_ref, i_ref, lg_ref, st_ref, do_ref, dp_ref, dlg_ref, ds_ref, dlb_ref):
        t = pl.program_id(0)

        @pl.when(t == 0)
        def _():
            ds_ref[...] = jnp.zeros_like(ds_ref)
            dlb_ref[...] = jnp.zeros_like(dlb_ref)

        row, causal, anti = _hgrn_consts()
        lb_all = _lower_bound(lg_ref)
        for h in range(HGRN_HEADS):
            sl = slice(HGRN_DIM * h, HGRN_DIM * (h + 1))
            q, lb = q_ref[:, sl], lb_all[:, sl]
            gt = _hgrn_gates(q, f_ref[:, sl], lb, row, causal)
            qd, ki, ke = gt["qd"], gt["ki"], gt["ke"]
            qdb, kib, keb = qd.astype(BF16), ki.astype(BF16), ke.astype(BF16)
            iv = i_ref[:, sl].astype(BF16)
            dob = do_ref[:, sl].astype(BF16)
            a = jnp.where(causal, _nt(qdb, kib), 0.0).astype(BF16)
            da = jnp.where(causal, _nt(dob, iv), 0.0).astype(BF16)
            s0, s1 = st_ref[0, h], st_ref[1, h]
            dec0, dec1 = jnp.exp(gt["bend0"]), jnp.exp(gt["bend1"])
            ds1 = ds_ref[h]
            ds1b = ds1.astype(BF16)
            dqd1 = _nn(dob[C:], s1.astype(BF16))
            di1 = _nt(keb[C:], ds1b)
            dke1 = _nn(iv[C:], ds1b)
            ddec1 = jnp.sum(ds1 * s1, axis=0, keepdims=True)
            ds0 = dec1 * ds1 + _tn(dob[C:], qdb[C:])
            ds0b = ds0.astype(BF16)
            dqd0 = _nn(dob[:C], s0.astype(BF16))
            di0 = _nt(keb[:C], ds0b)
            dke0 = _nn(iv[:C], ds0b)
            ddec0 = jnp.sum(ds0 * s0, axis=0, keepdims=True)
            ds_ref[h] = dec0 * ds0 + _tn(dob[:C], qdb[:C])

            dqd = _nn(da, kib) + jnp.concatenate([dqd0, dqd1], axis=0)
            dki = _tn(da, qdb)
            dp_ref[:, 2 * W + HGRN_DIM * h:2 * W + HGRN_DIM * (h + 1)] = (
                _tn(a, dob) + jnp.concatenate([di0, di1], axis=0)).astype(BF16)
            dke = jnp.concatenate([dke0, dke1], axis=0)
            gke = dke * ke
            dbend0 = jnp.sum(gke[:C], axis=0, keepdims=True) + ddec0 * dec0
            dbend1 = jnp.sum(gke[C:], axis=0, keepdims=True) + ddec1 * dec1
            db = dqd * qd - dki * ki - gke
            db = db + jnp.where(row == C - 1, dbend0, 0.0) + jnp.where(row == HG_T - 1, dbend1, 0.0)
            dlogf = _nn(jnp.where(anti, 1.0, 0.0).astype(F32), db, HIGHEST)
            dforget = dlogf / gt["forget"] - (dki * gt["emb"] + dke * gt["eend"])
            sg, sq = gt["sg"], gt["sq"]
            dp_ref[:, W + HGRN_DIM * h:W + HGRN_DIM * (h + 1)] = (dforget * (1.0 - lb) * sg * (1.0 - sg)).astype(BF16)
            dlb_ref[:, sl] += jnp.sum(dforget * (1.0 - sg), axis=0, keepdims=True)
            dp_ref[:, sl] = (dqd * gt["eb"] * sq * (1.0 + q * (1.0 - sq))).astype(BF16)

        @pl.when(t == nt - 1)
        def _():
            dl0 = dlb_ref[...] * lb_all * (1.0 - lb_all)
            dlg_ref[0:1, :] = dl0
            dlg_ref[1:2, :] = -dl0

    blk = lambda j: pl.BlockSpec((HG_T, W), lambda t: (nt - 1 - t, j))
    full = pl.BlockSpec((2, W), lambda t: (0, 0))
    return pl.pallas_call(
        body, name=name, grid=(nt,),
        in_specs=[blk(0), blk(1), blk(2), full,
                  pl.BlockSpec((2, HGRN_HEADS, HGRN_DIM, HGRN_DIM), lambda t: (nt - 1 - t, 0, 0, 0)), blk(0)],
        out_specs=[pl.BlockSpec((HG_T, 3 * W), lambda t: (nt - 1 - t, 0)), full],
        out_shape=[jax.ShapeDtypeStruct((S, 3 * W), BF16), jax.ShapeDtypeStruct((2, W), F32)],
        scratch_shapes=[pltpu.VMEM((HGRN_HEADS, HGRN_DIM, HGRN_DIM), F32), pltpu.VMEM((1, W), F32)],
        compiler_params=_cparams("arbitrary"),
    )(proj, proj, proj, logits, states, drec)


def _out_proj(attn, rec, proj_h, x, g_attn, g_hgrn, g_norm2, w_out, *, name, tm=512):
    S, D = x.shape
    AW, W = ATTN_W, HGRN_W

    def body(a_ref, r_ref, hg_ref, x_ref, ga_ref, gh_ref, g2_ref, w_ref, h_ref, u_ref, m_ref):
        av = a_ref[...]
        m_ref[:, :AW] = (av * _rstd(av) * ga_ref[...]).astype(BF16)
        for h in range(HGRN_HEADS):
            sl = slice(HGRN_DIM * h, HGRN_DIM * (h + 1))
            rv, hg = r_ref[:, sl], hg_ref[:, sl]
            m_ref[:, AW + HGRN_DIM * h:AW + HGRN_DIM * (h + 1)] = (
                (rv * _rstd(rv) * gh_ref[:, sl]) * (hg * _sigmoid(hg))).astype(BF16)
        h1 = x_ref[...] + _nn(m_ref[...], w_ref[...])
        h_ref[...] = h1
        u_ref[...] = (h1 * _rstd(h1) * g2_ref[...]).astype(BF16)

    row = lambda w, j=0: pl.BlockSpec((tm, w), lambda i: (i, j))
    vec = lambda w: pl.BlockSpec((1, w), lambda i: (0, 0))
    return pl.pallas_call(
        body, name=name, grid=(S // tm,),
        in_specs=[row(AW), row(W), row(W, 3), row(D), vec(AW), vec(W), vec(D), _resident(w_out.shape)],
        out_specs=[row(D), row(D), row(AW + W)],
        out_shape=[jax.ShapeDtypeStruct((S, D), F32), jax.ShapeDtypeStruct((S, D), BF16),
                   jax.ShapeDtypeStruct((S, AW + W), BF16)],
        compiler_params=_cparams("parallel"),
    )(attn, rec, proj_h, x, g_attn, g_hgrn, g_norm2, w_out)


def _dmix_post_bwd(dh1b, w_out, attn, rec, proj_h, g_attn, g_hgrn, *, name, tm=512):
    S, D = dh1b.shape
    AW, W = ATTN_W, HGRN_W

    def body(dh_ref, w_ref, a_ref, r_ref, hg_ref, ga_ref, gh_ref, do_ref, dl_ref, dr_ref, dhg_ref, dga_ref, dgh_ref):
        first = pl.program_id(0) == 0
        dmix = _nt(dh_ref[...], w_ref[...])
        av = a_ref[...]
        dov, dga = _norm_bwd(av, ga_ref[...], dmix[:, :AW])
        do_ref[...] = dov
        shift = HEAD_DIM.bit_length() - 1
        hi = lax.shift_right_logical(lax.broadcasted_iota(jnp.int32, (AW, AW), 0), shift)
        hj = lax.shift_right_logical(lax.broadcasted_iota(jnp.int32, (AW, AW), 1), shift)
        dl_ref[...] = _nn(dov * av, jnp.where(hi == hj, 1.0, 0.0).astype(F32), HIGHEST)
        _accumulate(dga_ref, jnp.sum(dga, axis=0, keepdims=True), first)

        @pl.when(first)
        def _():
            dgh_ref[...] = jnp.zeros_like(dgh_ref)

        for h in range(HGRN_HEADS):
            sl = slice(HGRN_DIM * h, HGRN_DIM * (h + 1))
            rv, hg, gv = r_ref[:, sl], hg_ref[:, sl], gh_ref[:, sl]
            dout = dmix[:, AW + HGRN_DIM * h:AW + HGRN_DIM * (h + 1)]
            sg = _sigmoid(hg)
            drv, dgh = _norm_bwd(rv, gv, dout * (hg * sg))
            dr_ref[:, sl] = drv
            dgh_ref[:, sl] += jnp.sum(dgh, axis=0, keepdims=True)
            dhg_ref[:, sl] = (dout * (rv * _rstd(rv) * gv) * (sg * (1.0 + hg * (1.0 - sg)))).astype(BF16)

    row = lambda w, j=0: pl.BlockSpec((tm, w), lambda i: (i, j))
    vec = lambda w: pl.BlockSpec((1, w), lambda i: (0, 0))
    return pl.pallas_call(
        body, name=name, grid=(S // tm,),
        in_specs=[row(D), _resident(w_out.shape), row(AW), row(W), row(W, 3), vec(AW), vec(W)],
        out_specs=[row(AW), row(AW), row(W), row(W), vec(AW), vec(W)],
        out_shape=[jax.ShapeDtypeStruct((S, AW), F32), jax.ShapeDtypeStruct((S, AW), F32),
                   jax.ShapeDtypeStruct((S, W), F32), jax.ShapeDtypeStruct((S, W), BF16),
                   jax.ShapeDtypeStruct((1, AW), F32), jax.ShapeDtypeStruct((1, W), F32)],
        compiler_params=_cparams("arbitrary"),
    )(dh1b, w_out, attn, rec, proj_h, g_attn, g_hgrn)


def _conv_act(g, g1, g2, w_ref, b_ref):
    c = b_ref[...] + w_ref[0:1, :] * g2 + w_ref[1:2, :] * g1 + w_ref[2:3, :] * g
    return c, 0.5 * (1.0 + lax.erf(c * (2.0 ** -0.5)))


def _shift_down(g, halo, row):
    g1 = jnp.where(row == 0, halo[7:8], pltpu.roll(g, 1, 0))
    g2 = jnp.where(row == 0, halo[6:7], jnp.where(row == 1, halo[7:8], pltpu.roll(g, 2, 0)))
    return g1, g2


def _shift_up(x, halo, row):
    n = x.shape[0]
    x1 = jnp.where(row == n - 1, halo[0:1], pltpu.roll(x, n - 1, 0))
    x2 = jnp.where(row == n - 2, halo[0:1], jnp.where(row == n - 1, halo[1:2], pltpu.roll(x, n - 2, 0)))
    return x1, x2


def _up_glu(u, w_up, conv_w, conv_b, *, name, tm=1024, tn=256):
    S, D = u.shape
    F = w_up.shape[1] // 2
    nf = F // tn

    def body(u_ref, wg_ref, wv_ref, cw_ref, cb_ref, g_ref, v_ref, a_ref, halo_ref):
        i, j = pl.program_id(0), pl.program_id(1)

        @pl.when(i == 0)
        def _():
            halo_ref[j] = jnp.zeros((SUBLANES, tn), F32)

        uv = u_ref[...]
        g, v = _nn(uv, wg_ref[...]), _nn(uv, wv_ref[...])
        row = lax.broadcasted_iota(jnp.int32, (tm, tn), 0)
        g1, g2 = _shift_down(g, halo_ref[j], row)
        c, cdf = _conv_act(g, g1, g2, cw_ref, cb_ref)
        a_ref[...] = (c * cdf * v).astype(BF16)
        g_ref[...] = g.astype(BF16)
        v_ref[...] = v.astype(BF16)
        halo_ref[j] = g[tm - SUBLANES:, :]

    col = pl.BlockSpec((tm, tn), lambda i, j: (i, j))
    out = jax.ShapeDtypeStruct((S, F), BF16)
    return pl.pallas_call(
        body, name=name, grid=(S // tm, nf),
        in_specs=[pl.BlockSpec((tm, D), lambda i, j: (i, 0)), pl.BlockSpec((D, tn), lambda i, j: (0, j)),
                  pl.BlockSpec((D, tn), lambda i, j: (0, j + nf)), pl.BlockSpec((3, tn), lambda i, j: (0, j)),
                  pl.BlockSpec((1, tn), lambda i, j: (0, j))],
        out_specs=[col, col, col], out_shape=[out, out, out],
        scratch_shapes=[pltpu.VMEM((nf, SUBLANES, tn), F32)], compiler_params=_cparams("arbitrary", "arbitrary"),
    )(u, w_up, w_up, conv_w, conv_b)


def _dact_glu_bwd(dh2b, w_down, gate, val, conv_w, conv_b, *, name, tm=1024, tn=256):
    S, D = dh2b.shape
    F = gate.shape[1]
    nf, ni = F // tn, S // tm
    hb = tm // SUBLANES

    def body(dh_ref, wd_ref, g_ref, gh_ref, v_ref, cw_ref, cb_ref, dg_ref, dv_ref, dcw_ref, dcb_ref, halo_ref, acc_ref):
        i, j = pl.program_id(0), pl.program_id(1)

        @pl.when(i == 0)
        def _():
            halo_ref[j] = jnp.zeros((SUBLANES, tn), F32)
            acc_ref[j] = jnp.zeros((SUBLANES, tn), F32)

        g = g_ref[...].astype(F32)
        before = jnp.where(i < ni - 1, gh_ref[...].astype(F32), 0.0)
        row = lax.broadcasted_iota(jnp.int32, (tm, tn), 0)
        g1, g2 = _shift_down(g, before[SUBLANES:], row)
        c, cdf = _conv_act(g, g1, g2, cw_ref, cb_ref)
        da = _nt(dh_ref[...], wd_ref[...])
        dv_ref[...] = (da * (c * cdf)).astype(BF16)
        pdf = jnp.exp(-0.5 * c * c) * (1.0 / (2.0 * jnp.pi) ** 0.5)
        dc = da * v_ref[...].astype(F32) * (cdf + c * pdf)
        d1, d2 = _shift_up(dc, halo_ref[j], row)
        dg_ref[...] = (cw_ref[2:3, :] * dc + cw_ref[1:2, :] * d1 + cw_ref[0:1, :] * d2).astype(BF16)
        halo_ref[j] = dc[:SUBLANES, :]
        for k, t in enumerate((dc * g2, dc * g1, dc * g, dc)):
            acc_ref[j, k:k + 1, :] += jnp.sum(t, axis=0, keepdims=True)

        @pl.when((i == ni - 1) & (j == nf - 1))
        def _():
            for jj in range(nf):
                dcw_ref[:, jj * tn:(jj + 1) * tn] = acc_ref[jj, 0:3, :]
                dcb_ref[:, jj * tn:(jj + 1) * tn] = acc_ref[jj, 3:4, :]

    tile = pl.BlockSpec((tm, tn), lambda i, j: (ni - 1 - i, j))
    return pl.pallas_call(
        body, name=name, grid=(ni, nf),
        in_specs=[pl.BlockSpec((tm, D), lambda i, j: (ni - 1 - i, 0)), pl.BlockSpec((tn, D), lambda i, j: (j, 0)),
                  tile, pl.BlockSpec((SUBLANES * 2, tn), lambda i, j: (jnp.maximum((ni - 1 - i) * (hb // 2) - 1, 0), j)),
                  tile, pl.BlockSpec((3, tn), lambda i, j: (0, j)), pl.BlockSpec((1, tn), lambda i, j: (0, j))],
        out_specs=[tile, tile, pl.BlockSpec((3, F), lambda i, j: (0, 0)), pl.BlockSpec((1, F), lambda i, j: (0, 0))],
        out_shape=[jax.ShapeDtypeStruct((S, F), BF16), jax.ShapeDtypeStruct((S, F), BF16),
                   jax.ShapeDtypeStruct((3, F), F32), jax.ShapeDtypeStruct((1, F), F32)],
        scratch_shapes=[pltpu.VMEM((nf, SUBLANES, tn), F32), pltpu.VMEM((nf, SUBLANES, tn), F32)],
        compiler_params=_cparams("arbitrary", "arbitrary"),
    )(dh2b, w_down, gate, gate, val, conv_w, conv_b)


def _down_loss(act, w_down, h1, g, target, *, name, tm=512):
    S, F = act.shape
    D = h1.shape[1]

    def body(a_ref, w_ref, h_ref, g_ref, t_ref, dh_ref, dhb_ref, dg_ref, loss_ref):
        first = pl.program_id(0) == 0
        h2 = h_ref[...] + _nn(a_ref[...], w_ref[...])
        gv = g_ref[...]
        r = _rstd(h2)
        xh = h2 * r
        err = xh * gv - t_ref[...]
        part_loss = 0.5 * jnp.sum(jnp.mean(err * err, axis=-1, keepdims=True), axis=0, keepdims=True)
        dy = err * (1.0 / D)
        dxh = dy * gv
        dh = r * (dxh - xh * jnp.mean(dxh * xh, axis=-1, keepdims=True))
        dh_ref[...] = dh
        dhb_ref[...] = dh.astype(BF16)
        _accumulate(dg_ref, jnp.sum(dy * xh, axis=0, keepdims=True), first)
        _accumulate(loss_ref, jnp.broadcast_to(part_loss, (1, LANES)), first)

    row = lambda w: pl.BlockSpec((tm, w), lambda i: (i, 0))
    vec = lambda w: pl.BlockSpec((1, w), lambda i: (0, 0))
    return pl.pallas_call(
        body, name=name, grid=(S // tm,), in_specs=[row(F), _resident(w_down.shape), row(D), vec(D), row(D)],
        out_specs=[row(D), row(D), vec(D), vec(LANES)],
        out_shape=[jax.ShapeDtypeStruct((S, D), F32), jax.ShapeDtypeStruct((S, D), BF16),
                   jax.ShapeDtypeStruct((1, D), F32), jax.ShapeDtypeStruct((1, LANES), F32)],
        compiler_params=_cparams("arbitrary"),
    )(act, w_down, h1, g, target)


def _grad_norm_input(pieces, ws, x, g, add, *, name, tm=512):
    S, D = x.shape
    widths = [p.shape[1] for p in pieces]
    n, nw = len(pieces), len(ws)
    where, wi, off = [], 0, 0
    for wd in widths:
        if off == ws[wi].shape[1]:
            wi, off = wi + 1, 0
        where.append((wi, off))
        off += wd

    def body(*refs):
        p_refs, w_refs = refs[:n], refs[n:n + nw]
        x_ref, g_ref, add_ref, dx_ref, dxb_ref, dg_ref = refs[n + nw:]
        du = None
        for k, (wi, off) in enumerate(where):
            term = _nt(p_refs[k][...], w_refs[wi][:, off:off + widths[k]])
            du = term if du is None else du + term
        dx, dg = _norm_bwd(x_ref[...], g_ref[...], du)
        dx = add_ref[...] + dx
        dx_ref[...] = dx
        dxb_ref[...] = dx.astype(BF16)
        _accumulate(dg_ref, jnp.sum(dg, axis=0, keepdims=True), pl.program_id(0) == 0)

    row = lambda w_: pl.BlockSpec((tm, w_), lambda i: (i, 0))
    vec = pl.BlockSpec((1, D), lambda i: (0, 0))
    return pl.pallas_call(
        body, name=name, grid=(S // tm,),
        in_specs=[row(wd) for wd in widths] + [_resident(w.shape) for w in ws] + [row(D), vec, row(D)],
        out_specs=[row(D), row(D), vec],
        out_shape=[jax.ShapeDtypeStruct((S, D), F32), jax.ShapeDtypeStruct((S, D), BF16),
                   jax.ShapeDtypeStruct((1, D), F32)],
        compiler_params=_cparams("arbitrary"),
    )(*pieces, *ws, x, g, add)


def _rows(a):
    return a.reshape(-1, a.shape[-1])


def _row_tile(rows, cols, itemsize=4, budget=1 << 20):
    t = rows
    while t % 32 == 0 and t * cols * itemsize > budget:
        t //= 2
    return t


def _sum_cast(arrs, out_dtype, *, name):
    shape = arrs[0].shape
    flat = [_rows(a) for a in arrs]
    R, C = flat[0].shape
    tr = _row_tile(R, C)

    def body(*refs):
        acc = refs[0][...].astype(F32)
        for r in refs[1:-1]:
            acc = acc + r[...].astype(F32)
        refs[-1][...] = acc.astype(out_dtype)

    spec = pl.BlockSpec((tr, C), lambda i: (i, 0))
    return pl.pallas_call(
        body, name=name, grid=(R // tr,), in_specs=[spec] * len(flat), out_specs=spec,
        out_shape=jax.ShapeDtypeStruct((R, C), out_dtype), compiler_params=_cparams("parallel"),
    )(*flat).reshape(shape)


def _adamw(parts, w, m, v, *, name):
    shape = w.shape
    w2, m2, v2 = _rows(w), _rows(m), _rows(v)
    R, C = w2.shape
    parts = [p.reshape(-1, R, C) for p in parts]
    tr = _row_tile(R, C)
    np_ = len(parts)
    c1, c2 = 1.0 - ADAM_B1 ** ADAM_STEP, 1.0 - ADAM_B2 ** ADAM_STEP

    def body(*refs):
        terms = [(r, k) for r in refs[:np_] for k in range(r.shape[0])]
        g = terms[0][0][terms[0][1]].astype(F32)
        for r, k in terms[1:]:
            g = g + r[k].astype(F32)
        w_ref, m_ref, v_ref, g_out, d_out, m_out, v_out = refs[np_:]
        mn = ADAM_B1 * m_ref[...] + (1.0 - ADAM_B1) * g
        vn = ADAM_B2 * v_ref[...] + (1.0 - ADAM_B2) * (g * g)
        g_out[...] = g
        d_out[...] = -ADAM_LR * ((mn / c1) / (jnp.sqrt(vn / c2) + ADAM_EPS) + ADAM_WD * w_ref[...])
        m_out[...] = mn
        v_out[...] = vn

    spec = pl.BlockSpec((tr, C), lambda i: (i, 0))
    out = jax.ShapeDtypeStruct((R, C), F32)
    stacks = [pl.BlockSpec((p.shape[0], tr, C), lambda i: (0, i, 0)) for p in parts]
    res = pl.pallas_call(
        body, name=name, grid=(R // tr,), in_specs=stacks + [spec] * 3, out_specs=[spec] * 4,
        out_shape=[out] * 4, compiler_params=_cparams("parallel"),
    )(*parts, w2, m2, v2)
    return [r.reshape(shape) for r in res]


def _coords():
    return lax.axis_index("x"), lax.axis_index("y"), lax.axis_index("c")


def _all_gather(shards, *, name):
    n = len(shards)

    def body(*refs):
        x_refs, out_refs = refs[:n], refs[n:2 * n]
        send_sems, recv_sems, local_sems = refs[2 * n:]
        x, y, c = _coords()
        me, sibling = (x, y, c), (x, y, 1 - c)
        chips = [(1 - x, y), (x, 1 - y), (1 - x, 1 - y)]

        def slot(a, dev):
            return out_refs[a].at[4 * dev[0] + 2 * dev[1] + dev[2]]

        def copy(a, k, block, to, src=None):
            return pltpu.make_async_remote_copy(
                src_ref=slot(a, block) if src is None else src, dst_ref=slot(a, block),
                send_sem=send_sems.at[7 * a + k], recv_sem=recv_sems.at[7 * a + k], device_id=to, device_id_type=MESH)

        mine = [pltpu.make_async_copy(x_refs[a], slot(a, me), local_sems.at[a]) for a in range(n)]
        for cp in mine:
            cp.start()
        first = []
        for a in range(n):
            first.append(copy(a, 0, me, sibling, src=x_refs[a]))
            first += [copy(a, 1 + j, me, (*chip, c), src=x_refs[a]) for j, chip in enumerate(chips)]
        for cp in first:
            cp.start()
        passed = []
        for j, chip in enumerate(chips):
            for a in range(n):
                copy(a, 1 + j, (*chip, c), me).wait_recv()
                fwd = copy(a, 4 + j, (*chip, c), sibling)
                fwd.start()
                passed.append(fwd)
        for a in range(n):
            copy(a, 0, sibling, me).wait_recv()
            for j, chip in enumerate(chips):
                copy(a, 4 + j, (*chip, 1 - c), me).wait_recv()
        for cp in first + passed:
            cp.wait_send()
        for cp in mine:
            cp.wait()

    return pl.pallas_call(
        body, name=name, in_specs=[HBM] * n, out_specs=[HBM] * n,
        out_shape=[jax.ShapeDtypeStruct((N_DEV, *s.shape), s.dtype) for s in shards],
        scratch_shapes=[pltpu.SemaphoreType.DMA((7 * n,)), pltpu.SemaphoreType.DMA((7 * n,)),
                        pltpu.SemaphoreType.DMA((n,))],
    )(*shards)


def _flip_y(x, y, c):
    return (x, 1 - y, c)


def _flip_x(x, y, c):
    return (1 - x, y, c)


def _flip_xy(x, y, c):
    return (1 - x, 1 - y, c)


SEM = pl.BlockSpec(memory_space=pltpu.SEMAPHORE)
SIDE_EFFECT = pltpu.SideEffectType.DATAFLOW_SIDE_EFFECTING


def _in_hbm(a):
    return pltpu.with_memory_space_constraint(a, pltpu.HBM)


def _copies_start(srcs, lands, plan, n_copies, *, name):
    ns, nl = len(srcs), len(lands)

    def body(*refs):
        src_refs, land_refs = refs[:ns], refs[ns:ns + nl]
        send_sems, recv_sems = refs[ns + nl:ns + nl + 2]
        token = refs[-1]
        for k, (src, dst, peer, _) in enumerate(plan(src_refs, land_refs, *_coords())):
            pltpu.make_async_remote_copy(src_ref=src, dst_ref=dst, send_sem=send_sems.at[k], recv_sem=recv_sems.at[k],
                                         device_id=peer, device_id_type=MESH).start()
        token[...] = jnp.zeros_like(token)

    bufs = [*srcs, *lands]
    res = pl.pallas_call(
        body, name=name, in_specs=[HBM] * (ns + nl),
        out_specs=(SEM, SEM, *[HBM] * (ns + nl), pl.BlockSpec(memory_space=pltpu.VMEM)),
        out_shape=(pltpu.SemaphoreType.DMA((n_copies,)), pltpu.SemaphoreType.DMA((n_copies,)),
                   *[pltpu.HBM(b.shape, b.dtype) for b in bufs], jax.ShapeDtypeStruct((SUBLANES, LANES), F32)),
        input_output_aliases={i: 2 + i for i in range(ns + nl)},
        compiler_params=pltpu.CompilerParams(has_side_effects=SIDE_EFFECT),
    )(*[_in_hbm(b) for b in bufs])
    return res[0], res[1], list(res[2:2 + ns]), list(res[2 + ns:2 + ns + nl]), res[-1]


def _copies_wait(started, plan, after, *, name):
    send_sems, recv_sems, srcs, lands, _ = started
    ns, nl = len(srcs), len(lands)

    def body(*refs):
        src_refs, land_refs = refs[:ns], refs[ns:ns + nl]
        send_sems, recv_sems = refs[ns + nl:ns + nl + 2]
        for k, (src, dst, peer, here) in enumerate(plan(src_refs, land_refs, *_coords())):
            pltpu.make_async_remote_copy(src_ref=src, dst_ref=dst, send_sem=send_sems.at[k], recv_sem=recv_sems.at[k],
                                         device_id=peer, device_id_type=MESH).wait_send()
            pltpu.make_async_remote_copy(src_ref=src, dst_ref=here, send_sem=send_sems.at[k], recv_sem=recv_sems.at[k],
                                         device_id=peer, device_id_type=MESH).wait_recv()

    bufs = [*srcs, *lands]
    res = pl.pallas_call(
        body, name=name, in_specs=[HBM] * (ns + nl) + [SEM, SEM, pl.BlockSpec(memory_space=pl.ANY)],
        out_specs=[HBM] * (ns + nl), out_shape=[pltpu.HBM(b.shape, b.dtype) for b in bufs],
        input_output_aliases={i: i for i in range(ns + nl)},
        compiler_params=pltpu.CompilerParams(has_side_effects=SIDE_EFFECT),
    )(*bufs, send_sems, recv_sems, after)
    return list(res[ns:])


def _dev_index(dev):
    return 4 * dev[0] + 2 * dev[1] + dev[2]


def _ag_chips_plan(src_refs, land_refs, x, y, c):
    me = _dev_index((x, y, c))
    return [(src, land.at[me], peer, land.at[_dev_index(peer)])
            for src, land in zip(src_refs, land_refs) for peer in (_flip_y(x, y, c), _flip_x(x, y, c), _flip_xy(x, y, c))]


def _ag_sibling_plan(src_refs, land_refs, x, y, c):
    chips = [(x, y), (x, 1 - y), (1 - x, y), (1 - x, 1 - y)]
    return [(land.at[_dev_index((*chip, c))], land.at[_dev_index((*chip, c))], (x, y, 1 - c),
             land.at[_dev_index((*chip, 1 - c))]) for land in land_refs for chip in chips]


def _rs_direct_plan(src_refs, land_refs, x, y, c):
    plan = []
    for src, land in zip(src_refs, land_refs):
        for m in range(1, N_DEV):
            peer = (x + (m >> 2) * (1 - 2 * x), y + ((m >> 1) & 1) * (1 - 2 * y), c + (m & 1) * (1 - 2 * c))
            plan.append((src.at[_dev_index(peer)], land.at[m - 1], peer, land.at[m - 1]))
    return plan


def _rs_start(grads, me, *, name):
    own = [lax.dynamic_index_in_dim(g, me, 0, keepdims=False) for g in grads]
    lands = [lax.empty((N_DEV - 1, *g.shape[1:]), g.dtype) for g in grads]
    return _copies_start(grads, lands, _rs_direct_plan, (N_DEV - 1) * len(grads), name=name), own


def _rs_finish(started, after, *, name):
    handle, own = started
    got = _copies_wait(handle, _rs_direct_plan, after, name=name)
    return [[o, land] for o, land in zip(own, got)]


def _by_device_cols(w):
    K, N = w.shape
    return w.reshape(K, N_DEV, N // N_DEV).transpose(1, 0, 2)


def _gathered_cols(w8):
    return w8.transpose(1, 0, 2).reshape(w8.shape[1], -1)


def _pair_major(w, inverse=False):
    K = w.shape[0]
    a, b = (ATTN_W // LANES, 3) if inverse else (3, ATTN_W // LANES)
    return w.reshape(K, a, b, LANES).transpose(0, 2, 1, 3).reshape(K, 3 * ATTN_W)


def kernel(x, norm1_g, w_in, attn_norm_g, hgrn_norm_g, hgrn_lb_logits, w_out, norm2_g, w_up, conv_w, conv_b, w_down, final_norm_g, loss_target, m_norm1_g, m_w_in, m_attn_norm_g, m_hgrn_norm_g, m_hgrn_lb_logits, m_w_out, m_norm2_g, m_w_up, m_conv_w, m_conv_b, m_w_down, m_final_norm_g, v_norm1_g, v_w_in, v_attn_norm_g, v_hgrn_norm_g, v_hgrn_lb_logits, v_w_out, v_norm2_g, v_w_up, v_conv_w, v_conv_b, v_w_down, v_final_norm_g):
    xs, target = x[0], loss_target[0]
    S, D = xs.shape
    NA = 3 * ATTN_W
    fng = final_norm_g.reshape(1, D)

    casts = [_sum_cast([w[0]], BF16, name=f"cast_{nm}") for nm, w in
             (("w_in", w_in), ("w_out", w_out), ("w_up", w_up), ("w_down", w_down))]
    me = _dev_index(_coords())
    (g_in,) = _all_gather(casts[:1], name="ag_w_in")
    later = casts[1:] + [conv_w[0]]
    ag1 = _copies_start(later, [lax.empty((N_DEV, *s.shape), s.dtype) for s in later], _ag_chips_plan,
                        3 * len(later), name="ag_chips_start")
    wi = _gathered_cols(g_in)
    wi_a, wi_h = _pair_major(wi[:, :NA]), wi[:, NA:]

    u1, proj_a = _proj_attn(xs, norm1_g + ag1[4][0, 0], wi_a, name="proj_attn")
    proj_h = _mm(u1, wi_h, name="proj_hgrn")
    attn, lse = _attn_fwd(proj_a, name="attn_fwd")
    lands = _copies_wait(ag1, _ag_chips_plan, attn, name="ag_chips_wait")
    lands = [lax.dynamic_update_index_in_dim(l, s, me, 0) for l, s in zip(lands, later)]
    ag2 = _copies_start([], lands, _ag_sibling_plan, 4 * len(later), name="ag_sibling_start")
    rec, states = _hgrn_fwd(proj_h, hgrn_lb_logits + ag2[4][0, 0], name="hgrn_fwd")
    g_out, g_up, g_down, g_cw = _copies_wait(ag2, _ag_sibling_plan, rec, name="ag_sibling_wait")
    wo = g_out.reshape(-1, D)
    wu = _gathered_cols(g_up)
    wd = g_down.reshape(-1, D)
    cw = _gathered_cols(g_cw)
    h1, u2, mixed = _out_proj(attn, rec, proj_h, xs, attn_norm_g, hgrn_norm_g, norm2_g, wo, name="out_proj")
    gate, val, act = _up_glu(u2, wu, cw, conv_b, name="up_glu")
    dh2, dh2b, d_fng, loss_part = _down_loss(act, wd, h1, fng, target, name="down_loss")

    dgate, dval, d_cw, d_cb = _dact_glu_bwd(dh2b, wd, gate, val, cw, conv_b, name="dact_glu_bwd")
    dw_down = _mm_tn(act, dh2b, tm=256, name="dw_down")
    dh1, dh1b, d_n2g = _grad_norm_input([dgate, dval], [wu], h1, norm2_g, dh2, name="du2_norm2_bwd")
    dw_up = [_mm_tn(u2, dy, tn=256, name=f"dw_up_{nm}") for nm, dy in (("gate", dgate), ("val", dval))]
    rs_ffn = _rs_start([dw_down.reshape(N_DEV, -1, D),
                        jnp.concatenate([h.reshape(D, N_DEV // 2, -1).transpose(1, 0, 2) for h in dw_up], axis=0)],
                       me, name="rs_ffn_start")
    dattn, delta, drec, dhg, d_ang, d_hng = _dmix_post_bwd(dh1b, wo, attn, rec, proj_h, attn_norm_g + rs_ffn[0][4][0, 0],
                                                          hgrn_norm_g, name="dmix_post_bwd")
    dw_out = _mm_tn(mixed, dh1b, name="dw_out")
    rs_out = _rs_start([dw_out.reshape(N_DEV, -1, D)], me, name="rs_out_start")
    dproj_h, d_lbl = _hgrn_bwd(proj_h, hgrn_lb_logits + rs_out[0][4][0, 0], states, drec, name="hgrn_bwd")
    dproj_a = _attn_bwd(proj_a, dattn, lse, delta, name="attn_bwd")
    dw_in = jnp.concatenate([_pair_major(_mm_tn(u1, dproj_a, tn=512, name="dw_in_attn"), inverse=True),
                             _mm_tn(u1, dproj_h, tn=512, name="dw_in_hgrn"),
                             _mm_tn(u1, dhg, tn=512, name="dw_in_gate")], axis=1)
    rs_in = _rs_start([_by_device_cols(dw_in)], me, name="rs_in_start")
    grad_x, _, d_n1g = _grad_norm_input([dproj_a, dproj_h, dhg], [wi_a, wi_h], xs, norm1_g + rs_in[0][4][0, 0], dh1,
                                        name="du1_norm1_bwd")

    res = {}

    def update(nm, parts, w, m, v):
        res[nm] = _adamw(parts, w, m, v, name=f"adamw_{nm}")

    g_down, g_up = _rs_finish(rs_ffn, grad_x, name="rs_ffn_wait")
    update("w_down", g_down, w_down, m_w_down, v_w_down)
    update("w_up", g_up, w_up, m_w_up, v_w_up)
    (g_out,) = _rs_finish(rs_out, grad_x, name="rs_out_wait")
    update("w_out", g_out, w_out, m_w_out, v_w_out)
    (g_in,) = _rs_finish(rs_in, res["w_up"][1], name="rs_in_wait")
    update("w_in", g_in, w_in, m_w_in, v_w_in)

    small = [("norm1_g", d_n1g, norm1_g, m_norm1_g, v_norm1_g),
             ("attn_norm_g", d_ang, attn_norm_g, m_attn_norm_g, v_attn_norm_g),
             ("hgrn_norm_g", d_hng, hgrn_norm_g, m_hgrn_norm_g, v_hgrn_norm_g),
             ("hgrn_lb_logits", d_lbl, hgrn_lb_logits, m_hgrn_lb_logits, v_hgrn_lb_logits),
             ("norm2_g", d_n2g, norm2_g, m_norm2_g, v_norm2_g),
             ("conv_b", d_cb, conv_b, m_conv_b, v_conv_b),
             ("final_norm_g", d_fng, final_norm_g, m_final_norm_g, v_final_norm_g)]
    pack = lambda arrs: jnp.concatenate([a.reshape(1, -1) for a in arrs], axis=1)
    g_small, g_dcw = _all_gather([pack([s[1] for s in small]), d_cw], name="ag_small_grads")
    sm = _adamw([g_small], pack([s[2] for s in small]), pack([s[3] for s in small]),
                pack([s[4] for s in small]), name="adamw_small")
    off = 0
    for nm, _, w, _, _ in small:
        res[nm] = [r[:, off:off + w.size].reshape(w.shape) for r in sm]
        off += w.size
    ncw = conv_w.shape[-1]
    mine_cw = lax.dynamic_slice_in_dim(g_dcw, me * ncw, ncw, axis=2)
    res["conv_w"] = _adamw([mine_cw], conv_w, m_conv_w, v_conv_w, name="adamw_conv_w")

    loss = lax.psum(loss_part[0, 0], MESH_AXES)
    order = ["norm1_g", "w_in", "attn_norm_g", "hgrn_norm_g", "hgrn_lb_logits", "w_out", "norm2_g", "w_up",
             "conv_w", "conv_b", "w_down", "final_norm_g"]
    return (loss, grad_x[None], *[res[nm][0] for nm in order], *[res[nm][1] for nm in order],
            *[res[nm][2] for nm in order], *[res[nm][3] for nm in order])
```

```python
import jax
import jax.numpy as jnp
from jax import lax
from jax.experimental import pallas as pl
from jax.experimental.pallas import tpu as pltpu

F32, BF16 = jnp.float32, jnp.bfloat16
NORM_EPS = 1e-6
ATTN_HEADS, HEAD_DIM, ATTN_BLOCK = 8, 64, 128
DILATIONS = (1, 4, 16)
ATTN_SCALE = HEAD_DIM ** -0.5
ATTN_W = ATTN_HEADS * HEAD_DIM
HGRN_HEADS, HGRN_DIM, HGRN_CHUNK = 4, 128, 64
HGRN_W = HGRN_HEADS * HGRN_DIM
ADAM_LR, ADAM_B1, ADAM_B2, ADAM_EPS, ADAM_WD, ADAM_STEP = 0.001, 0.9, 0.999, 1e-08, 0.01, 10
LANES, SUBLANES = 128, 8
VMEM_LIMIT_BYTES = 56 * 1024 * 1024
N_DEV = 8
MESH_AXES = ("x", "y", "c")
MESH = pl.DeviceIdType.MESH
HBM = pl.BlockSpec(memory_space=pltpu.HBM)
HIGHEST = lax.Precision.HIGHEST


def _cparams(*sem):
    return pltpu.CompilerParams(dimension_semantics=sem, vmem_limit_bytes=VMEM_LIMIT_BYTES)


def _tile(n, pref):
    if n <= pref:
        return n
    t = (pref // LANES) * LANES
    while n % t:
        t -= LANES
    return t


def _resident(shape):
    return pl.BlockSpec(shape, lambda *_: (0,) * len(shape), pipeline_mode=pl.Buffered(1))


def _dot(a, b, dims, precision=None):
    return lax.dot_general(a, b, (dims, ((), ())), precision=precision, preferred_element_type=F32)


def _nn(a, b, precision=None):
    return _dot(a, b, ((1,), (0,)), precision)


def _nt(a, b):
    return _dot(a, b, ((1,), (1,)))


def _tn(a, b):
    return _dot(a, b, ((0,), (0,)))


def _sigmoid(x):
    return 1.0 / (1.0 + jnp.exp(-x))


def _rstd(x):
    return lax.rsqrt(jnp.mean(x * x, axis=-1, keepdims=True) + NORM_EPS)


def _norm_bwd(x, g, du):
    r = _rstd(x)
    xh = x * r
    dxh = du * g
    return r * (dxh - xh * jnp.mean(dxh * xh, axis=-1, keepdims=True)), du * xh


def _accumulate(ref, part, first):
    @pl.when(first)
    def _():
        ref[...] = part

    @pl.when(jnp.logical_not(first))
    def _():
        ref[...] += part


def _mm(a, b, *, name, out_dtype=F32, tm=1024, tn=512):
    M, K = a.shape
    N = b.shape[1]
    tm, tn = _tile(M, tm), _tile(N, tn)

    def body(a_ref, b_ref, o_ref):
        o_ref[...] = _nn(a_ref[...], b_ref[...]).astype(out_dtype)

    return pl.pallas_call(
        body, name=name, grid=(M // tm, N // tn),
        in_specs=[pl.BlockSpec((tm, K), lambda i, j: (i, 0)), pl.BlockSpec((K, tn), lambda i, j: (0, j))],
        out_specs=pl.BlockSpec((tm, tn), lambda i, j: (i, j)), out_shape=jax.ShapeDtypeStruct((M, N), out_dtype),
        compiler_params=_cparams("parallel", "parallel"),
    )(a, b)


def _mm_tn(x, dy, *, name, tm=512, tn=1024):
    S, M = x.shape
    N = dy.shape[1]
    tm, tn = _tile(M, tm), _tile(N, tn)

    def body(x_ref, dy_ref, o_ref, xt_ref):
        @pl.when(pl.program_id(1) == 0)
        def _():
            xt_ref[...] = x_ref[...].T

        o_ref[...] = _nn(xt_ref[...], dy_ref[...]).astype(BF16)

    return pl.pallas_call(
        body, name=name, grid=(M // tm, N // tn),
        in_specs=[pl.BlockSpec((S, tm), lambda i, j: (0, i)), pl.BlockSpec((S, tn), lambda i, j: (0, j))],
        out_specs=pl.BlockSpec((tm, tn), lambda i, j: (i, j)), out_shape=jax.ShapeDtypeStruct((M, N), BF16),
        scratch_shapes=[pltpu.VMEM((tm, S), BF16)], compiler_params=_cparams("parallel", "arbitrary"),
    )(x, dy)


def _proj_attn(x, g, w, *, name, tm=1024, tn=512):
    S, D = x.shape
    N = w.shape[1]

    def body(x_ref, g_ref, w_ref, u_ref, o_ref):
        @pl.when(pl.program_id(1) == 0)
        def _():
            xv = x_ref[...]
            u_ref[...] = (xv * _rstd(xv) * g_ref[...]).astype(BF16)

        o_ref[...] = _nn(u_ref[...], w_ref[...]).astype(BF16)

    return pl.pallas_call(
        body, name=name, grid=(S // tm, N // tn),
        in_specs=[pl.BlockSpec((tm, D), lambda i, j: (i, 0)), pl.BlockSpec((1, D), lambda i, j: (0, 0)),
                  pl.BlockSpec((D, tn), lambda i, j: (0, j))],
        out_specs=[pl.BlockSpec((tm, D), lambda i, j: (i, 0)), pl.BlockSpec((tm, tn), lambda i, j: (i, j))],
        out_shape=[jax.ShapeDtypeStruct((S, D), BF16), jax.ShapeDtypeStruct((S, N), BF16)],
        compiler_params=_cparams("parallel", "arbitrary"),
    )(x, g, w)


PAIR_W = 3 * LANES
ATTN_UNROLL_FWD, ATTN_UNROLL_BWD = 4, 4


def _attn_masks(first):
    qi = lax.broadcasted_iota(jnp.int32, (ATTN_BLOCK, 2 * ATTN_BLOCK), 0)
    kj = lax.broadcasted_iota(jnp.int32, (ATTN_BLOCK, 2 * ATTN_BLOCK), 1)
    dist = qi + ATTN_BLOCK - kj
    valid = (dist >= 0) & (dist <= ATTN_BLOCK) & jnp.logical_or(kj >= ATTN_BLOCK, jnp.logical_not(first))
    lane = lax.broadcasted_iota(jnp.int32, (1, LANES), 1)
    return valid, lane


def _for_residue_blocks(S, d, fn):
    span = ATTN_BLOCK * d
    nb = S // span

    def step(n, carry):
        base = pl.multiple_of(n * span, span)
        for r in range(d):
            off = pl.multiple_of((r * nb + n) * ATTN_BLOCK, ATTN_BLOCK)
            fn(lambda ref, r=r: _block_rows(ref, base, r, d),
               lambda ref, val, r=r: _set_block_rows(ref, base, r, d, val), off)
        return carry

    lax.fori_loop(0, nb, step, 0)


def _for_blocks(S, unroll, fn):
    def step(i, carry):
        fn([(pl.multiple_of((i * unroll + u) * ATTN_BLOCK, ATTN_BLOCK), i * unroll + u) for u in range(unroll)])
        return carry

    lax.fori_loop(0, S // ATTN_BLOCK // unroll, step, 0)


def _head_value(x2, lane, e):
    return jnp.sum(jnp.where(lane == HEAD_DIM * e, x2, 0.0), axis=-1, keepdims=True)


def _block_rows(ref, base, r, d):
    if d == 1:
        return ref[pl.ds(base, ATTN_BLOCK), :]
    return ref.at[pl.ds(base, ATTN_BLOCK * d)][pl.ds(r, ATTN_BLOCK, stride=d), :]


def _set_block_rows(ref, base, r, d, val):
    if d == 1:
        ref[pl.ds(base, ATTN_BLOCK), :] = val
    else:
        ref.at[pl.ds(base, ATTN_BLOCK * d)][pl.ds(r, ATTN_BLOCK, stride=d), :] = val


def _split_pair(p_ref, qs, ks, vs, bk, bv):
    qs[...] = p_ref[:, 0:LANES].astype(F32)
    ks[...] = p_ref[:, LANES:2 * LANES].astype(F32)
    vs[...] = p_ref[:, 2 * LANES:3 * LANES].astype(F32)
    bk[0:ATTN_BLOCK, :] = jnp.zeros((ATTN_BLOCK, LANES), bk.dtype)
    bv[0:ATTN_BLOCK, :] = jnp.zeros((ATTN_BLOCK, LANES), bv.dtype)


def _regroup_qkv(rows, off, qs, ks, vs, bq, bk, bv):
    blk, shifted = pl.ds(off, ATTN_BLOCK), pl.ds(off + ATTN_BLOCK, ATTN_BLOCK)
    bq[blk, :] = rows(qs).astype(BF16)
    bk[shifted, :] = rows(ks).astype(BF16)
    bv[shifted, :] = rows(vs).astype(BF16)


def _attn_fwd(proj_a, *, name):
    S = proj_a.shape[0]

    def body(p_ref, o_ref, l_ref, qs, ks, vs, bq, bk, bv, bo, bl):
        _split_pair(p_ref, qs, ks, vs, bk, bv)
        for d in DILATIONS:
            nb = S // (ATTN_BLOCK * d)
            _for_residue_blocks(S, d, lambda rows, _, off: _regroup_qkv(rows, off, qs, ks, vs, bq, bk, bv))

            def blocks(group, nb=nb):
                lane = lax.broadcasted_iota(jnp.int32, (1, LANES), 1)
                heads = [(lane >= HEAD_DIM * e) & (lane < HEAD_DIM * (e + 1)) for e in range(LANES // HEAD_DIM)]
                wins = [pl.ds(off, 2 * ATTN_BLOCK) for off, _ in group]
                s = [[_nt(jnp.where(mh, bq[pl.ds(off, ATTN_BLOCK), :], jnp.zeros((ATTN_BLOCK, LANES), BF16)), bk[win, :])
                      for mh in heads] for (off, _), win in zip(group, wins)]
                p, m, l = [], [], []
                for (off, b), su in zip(group, s):
                    valid, _ = _attn_masks(jnp.bitwise_and(b, nb - 1) == 0)
                    sm = [jnp.where(valid, x * ATTN_SCALE, -jnp.inf) for x in su]
                    m.append([jnp.max(x, axis=-1, keepdims=True) for x in sm])
                    p.append([jnp.exp(x - mx) for x, mx in zip(sm, m[-1])])
                    l.append([jnp.sum(x, axis=-1, keepdims=True) for x in p[-1]])
                o = [[_nn(x.astype(BF16), bv[win, :]) for x in pu] for pu, win in zip(p, wins)]
                for (off, _), ou, mu, lu in zip(group, o, m, l):
                    o2 = jnp.zeros((ATTN_BLOCK, LANES), F32)
                    l2 = jnp.zeros((ATTN_BLOCK, LANES), F32)
                    for mh, oe, me_, le in zip(heads, ou, mu, lu):
                        o2 = jnp.where(mh, oe / le, o2)
                        l2 = jnp.where(mh, me_ + jnp.log(le), l2)
                    bo[pl.ds(off, ATTN_BLOCK), :] = o2
                    bl[pl.ds(off, ATTN_BLOCK), :] = l2

            _for_blocks(S, ATTN_UNROLL_FWD, blocks)

            def merge(rows, set_rows, off, d=d):
                blk = pl.ds(off, ATTN_BLOCK)
                o2, l2 = bo[blk, :], bl[blk, :]
                if d != DILATIONS[0]:
                    lo, oo = rows(l_ref), rows(o_ref)
                    ln = jnp.maximum(lo, l2)
                    wa, wb = jnp.exp(lo - ln), jnp.exp(l2 - ln)
                    o2 = (wa * oo + wb * o2) / (wa + wb)
                    l2 = ln + jnp.log(wa + wb)
                set_rows(o_ref, o2)
                set_rows(l_ref, l2)

            _for_residue_blocks(S, d, merge)

    slab = pl.BlockSpec((S, LANES), lambda p: (0, p))
    f32_slab, bf16_slab = pltpu.VMEM((S, LANES), F32), pltpu.VMEM((S, LANES), BF16)
    bf16_window = pltpu.VMEM((S + ATTN_BLOCK, LANES), BF16)
    return pl.pallas_call(
        body, name=name, grid=(ATTN_W // LANES,), in_specs=[pl.BlockSpec((S, PAIR_W), lambda p: (0, p))],
        out_specs=[slab, slab],
        out_shape=[jax.ShapeDtypeStruct((S, ATTN_W), F32), jax.ShapeDtypeStruct((S, ATTN_W), F32)],
        scratch_shapes=[f32_slab] * 3 + [bf16_slab, bf16_window, bf16_window, f32_slab, f32_slab],
        compiler_params=_cparams("parallel"),
    )(proj_a)


def _attn_bwd(proj_a, do, lse, delta, *, name):
    S = proj_a.shape[0]

    def body(p_ref, do_ref, lse_ref, dl_ref, o_ref, qs, ks, vs, dqs, dks, dvs, bq, bk, bv, bdo, blse, bdl, bdq, bdk, bdv):
        _split_pair(p_ref, qs, ks, vs, bk, bv)
        bdk[0:ATTN_BLOCK, :] = jnp.zeros((ATTN_BLOCK, LANES), F32)
        bdv[0:ATTN_BLOCK, :] = jnp.zeros((ATTN_BLOCK, LANES), F32)
        for d in DILATIONS:
            nb = S // (ATTN_BLOCK * d)

            def regroup(rows, _, off):
                _regroup_qkv(rows, off, qs, ks, vs, bq, bk, bv)
                blk = pl.ds(off, ATTN_BLOCK)
                bdo[blk, :] = rows(do_ref).astype(BF16)
                blse[blk, :] = rows(lse_ref)
                bdl[blk, :] = rows(dl_ref)

            _for_residue_blocks(S, d, regroup)

            def blocks(group, nb=nb):
                lane = lax.broadcasted_iota(jnp.int32, (1, LANES), 1)
                heads = [(lane >= HEAD_DIM * e) & (lane < HEAD_DIM * (e + 1)) for e in range(LANES // HEAD_DIM)]
                zero = jnp.zeros((ATTN_BLOCK, LANES), BF16)
                chains = [(off, b, e, mh) for off, b in group for e, mh in enumerate(heads)]
                qm = [jnp.where(mh, bq[pl.ds(off, ATTN_BLOCK), :], zero) for off, _, _, mh in chains]
                dom = [jnp.where(mh, bdo[pl.ds(off, ATTN_BLOCK), :], zero) for off, _, _, mh in chains]
                s = [_nt(x, bk[pl.ds(off, 2 * ATTN_BLOCK), :]) for x, (off, _, _, _) in zip(qm, chains)]
                dp = [_nt(x, bv[pl.ds(off, 2 * ATTN_BLOCK), :]) for x, (off, _, _, _) in zip(dom, chains)]
                p, ds = [], []
                for (off, b, e, _), sc, dpc in zip(chains, s, dp):
                    valid, _ = _attn_masks(jnp.bitwise_and(b, nb - 1) == 0)
                    blk = pl.ds(off, ATTN_BLOCK)
                    pc = jnp.where(valid, jnp.exp(sc * ATTN_SCALE - _head_value(blse[blk, :], lane, e)), 0.0)
                    ds.append((pc * (dpc - _head_value(bdl[blk, :], lane, e)) * ATTN_SCALE).astype(BF16))
                    p.append(pc.astype(BF16))
                dq = [_nn(x, bk[pl.ds(off, 2 * ATTN_BLOCK), :]) for x, (off, _, _, _) in zip(ds, chains)]
                dk = [_tn(x, y) for x, y in zip(ds, qm)]
                dv = [_tn(x, y) for x, y in zip(p, dom)]
                nh = len(heads)
                for u, (off, _) in enumerate(group):
                    dq2 = jnp.zeros((ATTN_BLOCK, LANES), F32)
                    for mh, x in zip(heads, dq[nh * u:nh * (u + 1)]):
                        dq2 = jnp.where(mh, x, dq2)
                    bdq[pl.ds(off, ATTN_BLOCK), :] = dq2
                    for acc, grads in ((bdk, dk), (bdv, dv)):
                        win_grad = sum(grads[nh * u + 1:nh * (u + 1)], grads[nh * u])
                        acc[pl.ds(off, ATTN_BLOCK), :] += win_grad[:ATTN_BLOCK]
                        acc[pl.ds(off + ATTN_BLOCK, ATTN_BLOCK), :] = win_grad[ATTN_BLOCK:]

            _for_blocks(S, ATTN_UNROLL_BWD, blocks)

            def scatter(rows, set_rows, off, d=d):
                blk, shifted = pl.ds(off, ATTN_BLOCK), pl.ds(off + ATTN_BLOCK, ATTN_BLOCK)
                for acc, part in ((dqs, bdq[blk, :]), (dks, bdk[shifted, :]), (dvs, bdv[shifted, :])):
                    set_rows(acc, part if d == DILATIONS[0] else rows(acc) + part)

            _for_residue_blocks(S, d, scatter)
        o_ref[:, 0:LANES] = dqs[...].astype(BF16)
        o_ref[:, LANES:2 * LANES] = dks[...].astype(BF16)
        o_ref[:, 2 * LANES:3 * LANES] = dvs[...].astype(BF16)

    slab = pl.BlockSpec((S, LANES), lambda p: (0, p), pipeline_mode=pl.Buffered(1))
    pair = pl.BlockSpec((S, PAIR_W), lambda p: (0, p))
    f32_slab, bf16_slab = pltpu.VMEM((S, LANES), F32), pltpu.VMEM((S, LANES), BF16)
    f32_window, bf16_window = pltpu.VMEM((S + ATTN_BLOCK, LANES), F32), pltpu.VMEM((S + ATTN_BLOCK, LANES), BF16)
    return pl.pallas_call(
        body, name=name, grid=(ATTN_W // LANES,), in_specs=[pair, slab, slab, slab], out_specs=pair,
        out_shape=jax.ShapeDtypeStruct(proj_a.shape, BF16),
        scratch_shapes=[f32_slab] * 6 + [bf16_slab, bf16_window, bf16_window, bf16_slab, f32_slab, f32_slab,
                                         f32_slab, f32_window, f32_window],
        compiler_params=_cparams("parallel"),
    )(proj_a, do, lse, delta)


HG_T = 2 * HGRN_CHUNK


def _hgrn_consts():
    row = lax.broadcasted_iota(jnp.int32, (HG_T, HG_T), 0)
    col = lax.broadcasted_iota(jnp.int32, (HG_T, HG_T), 1)
    same = (row >= HGRN_CHUNK) == (col >= HGRN_CHUNK)
    return row, same & (col <= row), same & (col >= row)


def _lower_bound(logits_ref):
    l0, l1 = logits_ref[0:1, :], logits_ref[1:2, :]
    mx = jnp.maximum(l0, l1)
    e0, e1 = jnp.exp(l0 - mx), jnp.exp(l1 - mx)
    return e0 / (e0 + e1)


def _hgrn_gates(qs, fs, lbs, row, causal):
    C = HGRN_CHUNK
    tri = jnp.where(causal, 1.0, 0.0).astype(F32)
    sgs = [_sigmoid(f) for f in fs]
    forgets = [lb + (1.0 - lb) * sg for lb, sg in zip(lbs, sgs)]
    logfs = [jnp.log(forget) for forget in forgets]
    bs = [_nn(tri, logf, HIGHEST) for logf in logfs]
    out = []
    for q, sg, forget, logf, b in zip(qs, sgs, forgets, logfs, bs):
        key = 1.0 - forget
        bend0 = jnp.sum(logf[:C], axis=0, keepdims=True)
        bend1 = jnp.sum(logf[C:], axis=0, keepdims=True)
        bend = jnp.where(row < C, bend0, bend1)
        eb, emb, eend = jnp.exp(b), jnp.exp(-b), jnp.exp(bend - b)
        sq = _sigmoid(q)
        out.append(dict(sg=sg, forget=forget, key=key, bend0=bend0, bend1=bend1, eb=eb, emb=emb, eend=eend, sq=sq,
                        qd=q * sq * eb, ki=key * emb, ke=key * eend))
    return out


def _hgrn_fwd(proj, logits, *, name):
    S = proj.shape[0]
    W, C = HGRN_W, HGRN_CHUNK

    def body(q_ref, f_ref, i_ref, lg_ref, rec_ref, st_ref, s_ref):
        @pl.when(pl.program_id(0) == 0)
        def _():
            s_ref[...] = jnp.zeros_like(s_ref)

        row, causal, _ = _hgrn_consts()
        lb_all = _lower_bound(lg_ref)
        heads = range(HGRN_HEADS)
        sls = [slice(HGRN_DIM * h, HGRN_DIM * (h + 1)) for h in heads]
        gts = _hgrn_gates([q_ref[:, sl] for sl in sls], [f_ref[:, sl] for sl in sls], [lb_all[:, sl] for sl in sls],
                          row, causal)
        qd, ki, ke = ([gt[k].astype(BF16) for gt in gts] for k in ("qd", "ki", "ke"))
        iv = [i_ref[:, sl].astype(BF16) for sl in sls]
        s0 = [s_ref[h] for h in heads]
        a = [_nt(qd[h], ki[h]) for h in heads]
        o0 = [_nt(qd[h][:C], s0[h].astype(BF16)) for h in heads]
        u0 = [_tn(iv[h][:C], ke[h][:C]) for h in heads]
        u1 = [_tn(iv[h][C:], ke[h][C:]) for h in heads]
        s1 = [jnp.exp(gts[h]["bend0"]) * s0[h] + u0[h] for h in heads]
        o = [_nn(jnp.where(causal, a[h], 0.0).astype(BF16), iv[h]) for h in heads]
        o1 = [_nt(qd[h][C:], s1[h].astype(BF16)) for h in heads]
        for h in heads:
            st_ref[0, h] = s0[h]
            st_ref[1, h] = s1[h]
            s_ref[h] = jnp.exp(gts[h]["bend1"]) * s1[h] + u1[h]
            rec_ref[:, sls[h]] = o[h] + jnp.concatenate([o0[h], o1[h]], axis=0)

    blk = lambda j: pl.BlockSpec((HG_T, W), lambda t: (t, j))
    return pl.pallas_call(
        body, name=name, grid=(S // HG_T,),
        in_specs=[blk(0), blk(1), blk(2), pl.BlockSpec((2, W), lambda t: (0, 0))],
        out_specs=[blk(0), pl.BlockSpec((2, HGRN_HEADS, HGRN_DIM, HGRN_DIM), lambda t: (t, 0, 0, 0))],
        out_shape=[jax.ShapeDtypeStruct((S, W), F32),
                   jax.ShapeDtypeStruct((S // C, HGRN_HEADS, HGRN_DIM, HGRN_DIM), F32)],
        scratch_shapes=[pltpu.VMEM((HGRN_HEADS, HGRN_DIM, HGRN_DIM), F32)],
        compiler_params=_cparams("arbitrary"),
    )(proj, proj, proj, logits)


def _hgrn_bwd(proj, logits, states, drec, *, name):
    S = proj.shape[0]
    W, C = HGRN_W, HGRN_CHUNK
    nt = S // HG_T

    def body(q_ref, f_ref, i_ref, lg_ref, st_ref, do_ref, dp_ref, dlg_ref, ds_ref, dlb_ref):
        t = pl.program_id(0)

        @pl.when(t == 0)
        def _():
            ds_ref[...] = jnp.zeros_like(ds_ref)
            dlb_ref[...] = jnp.zeros_like(dlb_ref)

        row, causal, anti = _hgrn_consts()
        lb_all = _lower_bound(lg_ref)
        heads = range(HGRN_HEADS)
        sls = [slice(HGRN_DIM * h, HGRN_DIM * (h + 1)) for h in heads]
        qs, lbs = [q_ref[:, sl] for sl in sls], [lb_all[:, sl] for sl in sls]
        gts = _hgrn_gates(qs, [f_ref[:, sl] for sl in sls], lbs, row, causal)
        qd, ki, ke = ([gt[k] for gt in gts] for k in ("qd", "ki", "ke"))
        qdb, kib, keb = ([x.astype(BF16) for x in xs] for xs in (qd, ki, ke))
        iv = [i_ref[:, sl].astype(BF16) for sl in sls]
        dob = [do_ref[:, sl].astype(BF16) for sl in sls]
        s0, s1, ds1 = [st_ref[0, h] for h in heads], [st_ref[1, h] for h in heads], [ds_ref[h] for h in heads]
        ds1b = [x.astype(BF16) for x in ds1]
        dec0, dec1 = [jnp.exp(gt["bend0"]) for gt in gts], [jnp.exp(gt["bend1"]) for gt in gts]
        a = [_nt(qdb[h], kib[h]) for h in heads]
        da = [_nt(dob[h], iv[h]) for h in heads]
        dqd1 = [_nn(dob[h][C:], s1[h].astype(BF16)) for h in heads]
        dqd0 = [_nn(dob[h][:C], s0[h].astype(BF16)) for h in heads]
        di1 = [_nt(keb[h][C:], ds1b[h]) for h in heads]
        dke1 = [_nn(iv[h][C:], ds1b[h]) for h in heads]
        t1 = [_tn(dob[h][C:], qdb[h][C:]) for h in heads]
        t0 = [_tn(dob[h][:C], qdb[h][:C]) for h in heads]
        ds0 = [dec1[h] * ds1[h] + t1[h] for h in heads]
        ds0b = [x.astype(BF16) for x in ds0]
        a = [jnp.where(causal, x, 0.0).astype(BF16) for x in a]
        da = [jnp.where(causal, x, 0.0).astype(BF16) for x in da]
        di0 = [_nt(keb[h][:C], ds0b[h]) for h in heads]
        dke0 = [_nn(iv[h][:C], ds0b[h]) for h in heads]
        dqd_a = [_nn(da[h], kib[h]) for h in heads]
        dki = [_tn(da[h], qdb[h]) for h in heads]
        di_a = [_tn(a[h], dob[h]) for h in heads]
        dqd, dke, db = [], [], []
        for h in heads:
            ds_ref[h] = dec0[h] * ds0[h] + t0[h]
            ddec1 = jnp.sum(ds1[h] * s1[h], axis=0, keepdims=True)
            ddec0 = jnp.sum(ds0[h] * s0[h], axis=0, keepdims=True)
            dqd.append(dqd_a[h] + jnp.concatenate([dqd0[h], dqd1[h]], axis=0))
            dp_ref[:, 2 * W + HGRN_DIM * h:2 * W + HGRN_DIM * (h + 1)] = (
                di_a[h] + jnp.concatenate([di0[h], di1[h]], axis=0)).astype(BF16)
            dke.append(jnp.concatenate([dke0[h], dke1[h]], axis=0))
            gke = dke[h] * ke[h]
            dbend0 = jnp.sum(gke[:C], axis=0, keepdims=True) + ddec0 * dec0[h]
            dbend1 = jnp.sum(gke[C:], axis=0, keepdims=True) + ddec1 * dec1[h]
            dbh = dqd[h] * qd[h] - dki[h] * ki[h] - gke
            db.append(dbh + jnp.where(row == C - 1, dbend0, 0.0) + jnp.where(row == HG_T - 1, dbend1, 0.0))
        tri = jnp.where(anti, 1.0, 0.0).astype(F32)
        dlogf = [_nn(tri, db[h], HIGHEST) for h in heads]
        for h in heads:
            gt, lb, q = gts[h], lbs[h], qs[h]
            dforget = dlogf[h] / gt["forget"] - (dki[h] * gt["emb"] + dke[h] * gt["eend"])
            sg, sq = gt["sg"], gt["sq"]
            dp_ref[:, W + HGRN_DIM * h:W + HGRN_DIM * (h + 1)] = (dforget * (1.0 - lb) * sg * (1.0 - sg)).astype(BF16)
            dlb_ref[:, sls[h]] += jnp.sum(dforget * (1.0 - sg), axis=0, keepdims=True)
            dp_ref[:, sls[h]] = (dqd[h] * gt["eb"] * sq * (1.0 + q * (1.0 - sq))).astype(BF16)

        @pl.when(t == nt - 1)
        def _():
            dl0 = dlb_ref[...] * lb_all * (1.0 - lb_all)
            dlg_ref[0:1, :] = dl0
            dlg_ref[1:2, :] = -dl0

    blk = lambda j: pl.BlockSpec((HG_T, W), lambda t: (nt - 1 - t, j))
    full = pl.BlockSpec((2, W), lambda t: (0, 0))
    return pl.pallas_call(
        body, name=name, grid=(nt,),
        in_specs=[blk(0), blk(1), blk(2), full,
                  pl.BlockSpec((2, HGRN_HEADS, HGRN_DIM, HGRN_DIM), lambda t: (nt - 1 - t, 0, 0, 0)), blk(0)],
        out_specs=[pl.BlockSpec((HG_T, 3 * W), lambda t: (nt - 1 - t, 0)), full],
        out_shape=[jax.ShapeDtypeStruct((S, 3 * W), BF16), jax.ShapeDtypeStruct((2, W), F32)],
        scratch_shapes=[pltpu.VMEM((HGRN_HEADS, HGRN_DIM, HGRN_DIM), F32), pltpu.VMEM((1, W), F32)],
        compiler_params=_cparams("arbitrary"),
    )(proj, proj, proj, logits, states, drec)


def _out_proj(attn, rec, proj_h, x, g_attn, g_hgrn, g_norm2, w_out, *, name, tm=512):
    S, D = x.shape
    AW, W = ATTN_W, HGRN_W

    def body(a_ref, r_ref, hg_ref, x_ref, ga_ref, gh_ref, g2_ref, w_ref, h_ref, u_ref, m_ref):
        av = a_ref[...]
        m_ref[:, :AW] = (av * _rstd(av) * ga_ref[...]).astype(BF16)
        for h in range(HGRN_HEADS):
            sl = slice(HGRN_DIM * h, HGRN_DIM * (h + 1))
            rv, hg = r_ref[:, sl], hg_ref[:, sl]
            m_ref[:, AW + HGRN_DIM * h:AW + HGRN_DIM * (h + 1)] = (
                (rv * _rstd(rv) * gh_ref[:, sl]) * (hg * _sigmoid(hg))).astype(BF16)
        h1 = x_ref[...] + _nn(m_ref[...], w_ref[...])
        h_ref[...] = h1
        u_ref[...] = (h1 * _rstd(h1) * g2_ref[...]).astype(BF16)

    row = lambda w, j=0: pl.BlockSpec((tm, w), lambda i: (i, j))
    vec = lambda w: pl.BlockSpec((1, w), lambda i: (0, 0))
    return pl.pallas_call(
        body, name=name, grid=(S // tm,),
        in_specs=[row(AW), row(W), row(W, 3), row(D), vec(AW), vec(W), vec(D), _resident(w_out.shape)],
        out_specs=[row(D), row(D), row(AW + W)],
        out_shape=[jax.ShapeDtypeStruct((S, D), F32), jax.ShapeDtypeStruct((S, D), BF16),
                   jax.ShapeDtypeStruct((S, AW + W), BF16)],
        compiler_params=_cparams("parallel"),
    )(attn, rec, proj_h, x, g_attn, g_hgrn, g_norm2, w_out)


def _dmix_post_bwd(dh1b, w_out, attn, rec, proj_h, g_attn, g_hgrn, *, name, tm=512):
    S, D = dh1b.shape
    AW, W = ATTN_W, HGRN_W

    def body(dh_ref, w_ref, a_ref, r_ref, hg_ref, ga_ref, gh_ref, do_ref, dl_ref, dr_ref, dhg_ref, dga_ref, dgh_ref):
        first = pl.program_id(0) == 0
        dmix = _nt(dh_ref[...], w_ref[...])
        av = a_ref[...]
        dov, dga = _norm_bwd(av, ga_ref[...], dmix[:, :AW])
        do_ref[...] = dov
        shift = HEAD_DIM.bit_length() - 1
        hi = lax.shift_right_logical(lax.broadcasted_iota(jnp.int32, (AW, AW), 0), shift)
        hj = lax.shift_right_logical(lax.broadcasted_iota(jnp.int32, (AW, AW), 1), shift)
        dl_ref[...] = _nn(dov * av, jnp.where(hi == hj, 1.0, 0.0).astype(F32), HIGHEST)
        _accumulate(dga_ref, jnp.sum(dga, axis=0, keepdims=True), first)

        @pl.when(first)
        def _():
            dgh_ref[...] = jnp.zeros_like(dgh_ref)

        for h in range(HGRN_HEADS):
            sl = slice(HGRN_DIM * h, HGRN_DIM * (h + 1))
            rv, hg, gv = r_ref[:, sl], hg_ref[:, sl], gh_ref[:, sl]
            dout = dmix[:, AW + HGRN_DIM * h:AW + HGRN_DIM * (h + 1)]
            sg = _sigmoid(hg)
            drv, dgh = _norm_bwd(rv, gv, dout * (hg * sg))
            dr_ref[:, sl] = drv
            dgh_ref[:, sl] += jnp.sum(dgh, axis=0, keepdims=True)
            dhg_ref[:, sl] = (dout * (rv * _rstd(rv) * gv) * (sg * (1.0 + hg * (1.0 - sg)))).astype(BF16)

    row = lambda w, j=0: pl.BlockSpec((tm, w), lambda i: (i, j))
    vec = lambda w: pl.BlockSpec((1, w), lambda i: (0, 0))
    return pl.pallas_call(
        body, name=name, grid=(S // tm,),
        in_specs=[row(D), _resident(w_out.shape), row(AW), row(W), row(W, 3), vec(AW), vec(W)],
        out_specs=[row(AW), row(AW), row(W), row(W), vec(AW), vec(W)],
        out_shape=[jax.ShapeDtypeStruct((S, AW), F32), jax.ShapeDtypeStruct((S, AW), F32),
                   jax.ShapeDtypeStruct((S, W), F32), jax.ShapeDtypeStruct((S, W), BF16),
                   jax.ShapeDtypeStruct((1, AW), F32), jax.ShapeDtypeStruct((1, W), F32)],
        compiler_params=_cparams("arbitrary"),
    )(dh1b, w_out, attn, rec, proj_h, g_attn, g_hgrn)


def _conv_act(g, g1, g2, w_ref, b_ref):
    c = b_ref[...] + w_ref[0:1, :] * g2 + w_ref[1:2, :] * g1 + w_ref[2:3, :] * g
    return c, 0.5 * (1.0 + lax.erf(c * (2.0 ** -0.5)))


def _shift_down(g, halo, row):
    g1 = jnp.where(row == 0, halo[7:8], pltpu.roll(g, 1, 0))
    g2 = jnp.where(row == 0, halo[6:7], jnp.where(row == 1, halo[7:8], pltpu.roll(g, 2, 0)))
    return g1, g2


def _shift_up(x, halo, row):
    n = x.shape[0]
    x1 = jnp.where(row == n - 1, halo[0:1], pltpu.roll(x, n - 1, 0))
    x2 = jnp.where(row == n - 2, halo[0:1], jnp.where(row == n - 1, halo[1:2], pltpu.roll(x, n - 2, 0)))
    return x1, x2


def _up_glu(u, w_up, conv_w, conv_b, *, name, tm=1024, tn=256):
    S, D = u.shape
    F = w_up.shape[1] // 2
    nf = F // tn

    def body(u_ref, wg_ref, wv_ref, cw_ref, cb_ref, g_ref, v_ref, a_ref, halo_ref):
        i, j = pl.program_id(0), pl.program_id(1)

        @pl.when(i == 0)
        def _():
            halo_ref[j] = jnp.zeros((SUBLANES, tn), F32)

        uv = u_ref[...]
        g, v = _nn(uv, wg_ref[...]), _nn(uv, wv_ref[...])
        row = lax.broadcasted_iota(jnp.int32, (tm, tn), 0)
        g1, g2 = _shift_down(g, halo_ref[j], row)
        c, cdf = _conv_act(g, g1, g2, cw_ref, cb_ref)
        a_ref[...] = (c * cdf * v).astype(BF16)
        g_ref[...] = g.astype(BF16)
        v_ref[...] = v.astype(BF16)
        halo_ref[j] = g[tm - SUBLANES:, :]

    col = pl.BlockSpec((tm, tn), lambda i, j: (i, j))
    out = jax.ShapeDtypeStruct((S, F), BF16)
    return pl.pallas_call(
        body, name=name, grid=(S // tm, nf),
        in_specs=[pl.BlockSpec((tm, D), lambda i, j: (i, 0)), pl.BlockSpec((D, tn), lambda i, j: (0, j)),
                  pl.BlockSpec((D, tn), lambda i, j: (0, j + nf)), pl.BlockSpec((3, tn), lambda i, j: (0, j)),
                  pl.BlockSpec((1, tn), lambda i, j: (0, j))],
        out_specs=[col, col, col], out_shape=[out, out, out],
        scratch_shapes=[pltpu.VMEM((nf, SUBLANES, tn), F32)], compiler_params=_cparams("arbitrary", "arbitrary"),
    )(u, w_up, w_up, conv_w, conv_b)


def _dact_glu_bwd(dh2b, w_down, gate, val, conv_w, conv_b, *, name, tm=1024, tn=256):
    S, D = dh2b.shape
    F = gate.shape[1]
    nf, ni = F // tn, S // tm
    hb = tm // SUBLANES

    def body(dh_ref, wd_ref, g_ref, gh_ref, v_ref, cw_ref, cb_ref, dg_ref, dv_ref, dcw_ref, dcb_ref, halo_ref, acc_ref):
        i, j = pl.program_id(0), pl.program_id(1)

        @pl.when(i == 0)
        def _():
            halo_ref[j] = jnp.zeros((SUBLANES, tn), F32)
            acc_ref[j] = jnp.zeros((SUBLANES, tn), F32)

        g = g_ref[...].astype(F32)
        before = jnp.where(i < ni - 1, gh_ref[...].astype(F32), 0.0)
        row = lax.broadcasted_iota(jnp.int32, (tm, tn), 0)
        g1, g2 = _shift_down(g, before[SUBLANES:], row)
        c, cdf = _conv_act(g, g1, g2, cw_ref, cb_ref)
        da = _nt(dh_ref[...], wd_ref[...])
        dv_ref[...] = (da * (c * cdf)).astype(BF16)
        pdf = jnp.exp(-0.5 * c * c) * (1.0 / (2.0 * jnp.pi) ** 0.5)
        dc = da * v_ref[...].astype(F32) * (cdf + c * pdf)
        d1, d2 = _shift_up(dc, halo_ref[j], row)
        dg_ref[...] = (cw_ref[2:3, :] * dc + cw_ref[1:2, :] * d1 + cw_ref[0:1, :] * d2).astype(BF16)
        halo_ref[j] = dc[:SUBLANES, :]
        for k, t in enumerate((dc * g2, dc * g1, dc * g, dc)):
            acc_ref[j, k:k + 1, :] += jnp.sum(t, axis=0, keepdims=True)

        @pl.when((i == ni - 1) & (j == nf - 1))
        def _():
            for jj in range(nf):
                dcw_ref[:, jj * tn:(jj + 1) * tn] = acc_ref[jj, 0:3, :]
                dcb_ref[:, jj * tn:(jj + 1) * tn] = acc_ref[jj, 3:4, :]

    tile = pl.BlockSpec((tm, tn), lambda i, j: (ni - 1 - i, j))
    return pl.pallas_call(
        body, name=name, grid=(ni, nf),
        in_specs=[pl.BlockSpec((tm, D), lambda i, j: (ni - 1 - i, 0)), pl.BlockSpec((tn, D), lambda i, j: (j, 0)),
                  tile, pl.BlockSpec((SUBLANES * 2, tn), lambda i, j: (jnp.maximum((ni - 1 - i) * (hb // 2) - 1, 0), j)),
                  tile, pl.BlockSpec((3, tn), lambda i, j: (0, j)), pl.BlockSpec((1, tn), lambda i, j: (0, j))],
        out_specs=[tile, tile, pl.BlockSpec((3, F), lambda i, j: (0, 0)), pl.BlockSpec((1, F), lambda i, j: (0, 0))],
        out_shape=[jax.ShapeDtypeStruct((S, F), BF16), jax.ShapeDtypeStruct((S, F), BF16),
                   jax.ShapeDtypeStruct((3, F), F32), jax.ShapeDtypeStruct((1, F), F32)],
        scratch_shapes=[pltpu.VMEM((nf, SUBLANES, tn), F32), pltpu.VMEM((nf, SUBLANES, tn), F32)],
        compiler_params=_cparams("arbitrary", "arbitrary"),
    )(dh2b, w_down, gate, gate, val, conv_w, conv_b)


def _down_loss(act, w_down, h1, g, target, *, name, tm=512):
    S, F = act.shape
    D = h1.shape[1]

    def body(a_ref, w_ref, h_ref, g_ref, t_ref, dh_ref, dhb_ref, dg_ref, loss_ref):
        first = pl.program_id(0) == 0
        h2 = h_ref[...] + _nn(a_ref[...], w_ref[...])
        gv = g_ref[...]
        r = _rstd(h2)
        xh = h2 * r
        err = xh * gv - t_ref[...]
        part_loss = 0.5 * jnp.sum(jnp.mean(err * err, axis=-1, keepdims=True), axis=0, keepdims=True)
        dy = err * (1.0 / D)
        dxh = dy * gv
        dh = r * (dxh - xh * jnp.mean(dxh * xh, axis=-1, keepdims=True))
        dh_ref[...] = dh
        dhb_ref[...] = dh.astype(BF16)
        _accumulate(dg_ref, jnp.sum(dy * xh, axis=0, keepdims=True), first)
        _accumulate(loss_ref, jnp.broadcast_to(part_loss, (1, LANES)), first)

    row = lambda w: pl.BlockSpec((tm, w), lambda i: (i, 0))
    vec = lambda w: pl.BlockSpec((1, w), lambda i: (0, 0))
    return pl.pallas_call(
        body, name=name, grid=(S // tm,), in_specs=[row(F), _resident(w_down.shape), row(D), vec(D), row(D)],
        out_specs=[row(D), row(D), vec(D), vec(LANES)],
        out_shape=[jax.ShapeDtypeStruct((S, D), F32), jax.ShapeDtypeStruct((S, D), BF16),
                   jax.ShapeDtypeStruct((1, D), F32), jax.ShapeDtypeStruct((1, LANES), F32)],
        compiler_params=_cparams("arbitrary"),
    )(act, w_down, h1, g, target)


def _grad_norm_input(pieces, ws, x, g, add, *, name, tm=512):
    S, D = x.shape
    widths = [p.shape[1] for p in pieces]
    n, nw = len(pieces), len(ws)
    where, wi, off = [], 0, 0
    for wd in widths:
        if off == ws[wi].shape[1]:
            wi, off = wi + 1, 0
        where.append((wi, off))
        off += wd

    def body(*refs):
        p_refs, w_refs = refs[:n], refs[n:n + nw]
        x_ref, g_ref, add_ref, dx_ref, dxb_ref, dg_ref = refs[n + nw:]
        du = None
        for k, (wi, off) in enumerate(where):
            term = _nt(p_refs[k][...], w_refs[wi][:, off:off + widths[k]])
            du = term if du is None else du + term
        dx, dg = _norm_bwd(x_ref[...], g_ref[...], du)
        dx = add_ref[...] + dx
        dx_ref[...] = dx
        dxb_ref[...] = dx.astype(BF16)
        _accumulate(dg_ref, jnp.sum(dg, axis=0, keepdims=True), pl.program_id(0) == 0)

    row = lambda w_: pl.BlockSpec((tm, w_), lambda i: (i, 0))
    vec = pl.BlockSpec((1, D), lambda i: (0, 0))
    return pl.pallas_call(
        body, name=name, grid=(S // tm,),
        in_specs=[row(wd) for wd in widths] + [_resident(w.shape) for w in ws] + [row(D), vec, row(D)],
        out_specs=[row(D), row(D), vec],
        out_shape=[jax.ShapeDtypeStruct((S, D), F32), jax.ShapeDtypeStruct((S, D), BF16),
                   jax.ShapeDtypeStruct((1, D), F32)],
        compiler_params=_cparams("arbitrary"),
    )(*pieces, *ws, x, g, add)


def _rows(a):
    return a.reshape(-1, a.shape[-1])


def _row_tile(rows, cols, itemsize=4, budget=1 << 20):
    t = rows
    while t % 32 == 0 and t * cols * itemsize > budget:
        t //= 2
    return t


def _sum_cast(arrs, out_dtype, *, name):
    shape = arrs[0].shape
    flat = [_rows(a) for a in arrs]
    R, C = flat[0].shape
    tr = _row_tile(R, C)

    def body(*refs):
        acc = refs[0][...].astype(F32)
        for r in refs[1:-1]:
            acc = acc + r[...].astype(F32)
        refs[-1][...] = acc.astype(out_dtype)

    spec = pl.BlockSpec((tr, C), lambda i: (i, 0))
    return pl.pallas_call(
        body, name=name, grid=(R // tr,), in_specs=[spec] * len(flat), out_specs=spec,
        out_shape=jax.ShapeDtypeStruct((R, C), out_dtype), compiler_params=_cparams("parallel"),
    )(*flat).reshape(shape)


def _adamw(parts, w, m, v, *, name):
    shape = w.shape
    w2, m2, v2 = _rows(w), _rows(m), _rows(v)
    R, C = w2.shape
    parts = [p.reshape(-1, R, C) for p in parts]
    tr = _row_tile(R, C)
    np_ = len(parts)
    c1, c2 = 1.0 - ADAM_B1 ** ADAM_STEP, 1.0 - ADAM_B2 ** ADAM_STEP

    def body(*refs):
        terms = [(r, k) for r in refs[:np_] for k in range(r.shape[0])]
        g = terms[0][0][terms[0][1]].astype(F32)
        for r, k in terms[1:]:
            g = g + r[k].astype(F32)
        w_ref, m_ref, v_ref, g_out, d_out, m_out, v_out = refs[np_:]
        mn = ADAM_B1 * m_ref[...] + (1.0 - ADAM_B1) * g
        vn = ADAM_B2 * v_ref[...] + (1.0 - ADAM_B2) * (g * g)
        g_out[...] = g
        d_out[...] = -ADAM_LR * ((mn / c1) / (jnp.sqrt(vn / c2) + ADAM_EPS) + ADAM_WD * w_ref[...])
        m_out[...] = mn
        v_out[...] = vn

    spec = pl.BlockSpec((tr, C), lambda i: (i, 0))
    out = jax.ShapeDtypeStruct((R, C), F32)
    stacks = [pl.BlockSpec((p.shape[0], tr, C), lambda i: (0, i, 0)) for p in parts]
    res = pl.pallas_call(
        body, name=name, grid=(R // tr,), in_specs=stacks + [spec] * 3, out_specs=[spec] * 4,
        out_shape=[out] * 4, compiler_params=_cparams("parallel"),
    )(*parts, w2, m2, v2)
    return [r.reshape(shape) for r in res]


def _coords():
    return lax.axis_index("x"), lax.axis_index("y"), lax.axis_index("c")


def _all_gather(shards, *, name):
    n = len(shards)

    def body(*refs):
        x_refs, out_refs = refs[:n], refs[n:2 * n]
        send_sems, recv_sems, local_sems = refs[2 * n:]
        x, y, c = _coords()
        me, sibling = (x, y, c), (x, y, 1 - c)
        chips = [(1 - x, y), (x, 1 - y), (1 - x, 1 - y)]

        def slot(a, dev):
            return out_refs[a].at[4 * dev[0] + 2 * dev[1] + dev[2]]

        def copy(a, k, block, to, src=None):
            return pltpu.make_async_remote_copy(
                src_ref=slot(a, block) if src is None else src, dst_ref=slot(a, block),
                send_sem=send_sems.at[7 * a + k], recv_sem=recv_sems.at[7 * a + k], device_id=to, device_id_type=MESH)

        mine = [pltpu.make_async_copy(x_refs[a], slot(a, me), local_sems.at[a]) for a in range(n)]
        for cp in mine:
            cp.start()
        first = []
        for a in range(n):
            first.append(copy(a, 0, me, sibling, src=x_refs[a]))
            first += [copy(a, 1 + j, me, (*chip, c), src=x_refs[a]) for j, chip in enumerate(chips)]
        for cp in first:
            cp.start()
        passed = []
        for j, chip in enumerate(chips):
            for a in range(n):
                copy(a, 1 + j, (*chip, c), me).wait_recv()
                fwd = copy(a, 4 + j, (*chip, c), sibling)
                fwd.start()
                passed.append(fwd)
        for a in range(n):
            copy(a, 0, sibling, me).wait_recv()
            for j, chip in enumerate(chips):
                copy(a, 4 + j, (*chip, 1 - c), me).wait_recv()
        for cp in first + passed:
            cp.wait_send()
        for cp in mine:
            cp.wait()

    return pl.pallas_call(
        body, name=name, in_specs=[HBM] * n, out_specs=[HBM] * n,
        out_shape=[jax.ShapeDtypeStruct((N_DEV, *s.shape), s.dtype) for s in shards],
        scratch_shapes=[pltpu.SemaphoreType.DMA((7 * n,)), pltpu.SemaphoreType.DMA((7 * n,)),
                        pltpu.SemaphoreType.DMA((n,))],
    )(*shards)


def _flip_y(x, y, c):
    return (x, 1 - y, c)


def _flip_x(x, y, c):
    return (1 - x, y, c)


def _flip_xy(x, y, c):
    return (1 - x, 1 - y, c)


SEM = pl.BlockSpec(memory_space=pltpu.SEMAPHORE)
SIDE_EFFECT = pltpu.SideEffectType.DATAFLOW_SIDE_EFFECTING


def _in_hbm(a):
    return pltpu.with_memory_space_constraint(a, pltpu.HBM)


def _copies_start(srcs, lands, plan, n_copies, *, name):
    ns, nl = len(srcs), len(lands)

    def body(*refs):
        src_refs, land_refs = refs[:ns], refs[ns:ns + nl]
        send_sems, recv_sems = refs[ns + nl:ns + nl + 2]
        token = refs[-1]
        for k, (src, dst, peer, _) in enumerate(plan(src_refs, land_refs, *_coords())):
            pltpu.make_async_remote_copy(src_ref=src, dst_ref=dst, send_sem=send_sems.at[k], recv_sem=recv_sems.at[k],
                                         device_id=peer, device_id_type=MESH).start()
        token[...] = jnp.zeros_like(token)

    bufs = [*srcs, *lands]
    res = pl.pallas_call(
        body, name=name, in_specs=[HBM] * (ns + nl),
        out_specs=(SEM, SEM, *[HBM] * (ns + nl), pl.BlockSpec(memory_space=pltpu.VMEM)),
        out_shape=(pltpu.SemaphoreType.DMA((n_copies,)), pltpu.SemaphoreType.DMA((n_copies,)),
                   *[pltpu.HBM(b.shape, b.dtype) for b in bufs], jax.ShapeDtypeStruct((SUBLANES, LANES), F32)),
        input_output_aliases={i: 2 + i for i in range(ns + nl)},
        compiler_params=pltpu.CompilerParams(has_side_effects=SIDE_EFFECT),
    )(*[_in_hbm(b) for b in bufs])
    return res[0], res[1], list(res[2:2 + ns]), list(res[2 + ns:2 + ns + nl]), res[-1]


def _copies_wait(started, plan, after, *, name):
    send_sems, recv_sems, srcs, lands, _ = started
    ns, nl = len(srcs), len(lands)

    def body(*refs):
        src_refs, land_refs = refs[:ns], refs[ns:ns + nl]
        send_sems, recv_sems = refs[ns + nl:ns + nl + 2]
        for k, (src, dst, peer, here) in enumerate(plan(src_refs, land_refs, *_coords())):
            pltpu.make_async_remote_copy(src_ref=src, dst_ref=dst, send_sem=send_sems.at[k], recv_sem=recv_sems.at[k],
                                         device_id=peer, device_id_type=MESH).wait_send()
            pltpu.make_async_remote_copy(src_ref=src, dst_ref=here, send_sem=send_sems.at[k], recv_sem=recv_sems.at[k],
                                         device_id=peer, device_id_type=MESH).wait_recv()

    bufs = [*srcs, *lands]
    res = pl.pallas_call(
        body, name=name, in_specs=[HBM] * (ns + nl) + [SEM, SEM, pl.BlockSpec(memory_space=pl.ANY)],
        out_specs=[HBM] * (ns + nl), out_shape=[pltpu.HBM(b.shape, b.dtype) for b in bufs],
        input_output_aliases={i: i for i in range(ns + nl)},
        compiler_params=pltpu.CompilerParams(has_side_effects=SIDE_EFFECT),
    )(*bufs, send_sems, recv_sems, after)
    return list(res[ns:])


def _dev_index(dev):
    return 4 * dev[0] + 2 * dev[1] + dev[2]


def _ag_chips_plan(src_refs, land_refs, x, y, c):
    me = _dev_index((x, y, c))
    return [(src, land.at[me], peer, land.at[_dev_index(peer)])
            for src, land in zip(src_refs, land_refs) for peer in (_flip_y(x, y, c), _flip_x(x, y, c), _flip_xy(x, y, c))]


def _ag_sibling_plan(src_refs, land_refs, x, y, c):
    chips = [(x, y), (x, 1 - y), (1 - x, y), (1 - x, 1 - y)]
    return [(land.at[_dev_index((*chip, c))], land.at[_dev_index((*chip, c))], (x, y, 1 - c),
             land.at[_dev_index((*chip, 1 - c))]) for land in land_refs for chip in chips]


def _rs_direct_plan(src_refs, land_refs, x, y, c):
    plan = []
    for src, land in zip(src_refs, land_refs):
        for m in range(1, N_DEV):
            peer = (x + (m >> 2) * (1 - 2 * x), y + ((m >> 1) & 1) * (1 - 2 * y), c + (m & 1) * (1 - 2 * c))
            plan.append((src.at[_dev_index(peer)], land.at[m - 1], peer, land.at[m - 1]))
    return plan


def _rs_start(grads, me, *, name):
    own = [lax.dynamic_index_in_dim(g, me, 0, keepdims=False) for g in grads]
    lands = [lax.empty((N_DEV - 1, *g.shape[1:]), g.dtype) for g in grads]
    return _copies_start(grads, lands, _rs_direct_plan, (N_DEV - 1) * len(grads), name=name), own


def _rs_finish(started, after, *, name):
    handle, own = started
    got = _copies_wait(handle, _rs_direct_plan, after, name=name)
    return [[o, land] for o, land in zip(own, got)]


def _by_device_cols(w):
    K, N = w.shape
    return w.reshape(K, N_DEV, N // N_DEV).transpose(1, 0, 2)


def _gathered_cols(w8):
    return w8.transpose(1, 0, 2).reshape(w8.shape[1], -1)


def _pair_major(w, inverse=False):
    K = w.shape[0]
    a, b = (ATTN_W // LANES, 3) if inverse else (3, ATTN_W // LANES)
    return w.reshape(K, a, b, LANES).transpose(0, 2, 1, 3).reshape(K, 3 * ATTN_W)


def kernel(x, norm1_g, w_in, attn_norm_g, hgrn_norm_g, hgrn_lb_logits, w_out, norm2_g, w_up, conv_w, conv_b, w_down, final_norm_g, loss_target, m_norm1_g, m_w_in, m_attn_norm_g, m_hgrn_norm_g, m_hgrn_lb_logits, m_w_out, m_norm2_g, m_w_up, m_conv_w, m_conv_b, m_w_down, m_final_norm_g, v_norm1_g, v_w_in, v_attn_norm_g, v_hgrn_norm_g, v_hgrn_lb_logits, v_w_out, v_norm2_g, v_w_up, v_conv_w, v_conv_b, v_w_down, v_final_norm_g):
    xs, target = x[0], loss_target[0]
    S, D = xs.shape
    NA = 3 * ATTN_W
    fng = final_norm_g.reshape(1, D)

    casts = [_sum_cast([w[0]], BF16, name=f"cast_{nm}") for nm, w in
             (("w_in", w_in), ("w_out", w_out), ("w_up", w_up), ("w_down", w_down))]
    me = _dev_index(_coords())
    (g_in,) = _all_gather(casts[:1], name="ag_w_in")
    later = casts[1:] + [conv_w[0]]
    ag1 = _copies_start(later, [lax.empty((N_DEV, *s.shape), s.dtype) for s in later], _ag_chips_plan,
                        3 * len(later), name="ag_chips_start")
    wi = _gathered_cols(g_in)
    wi_a, wi_h = _pair_major(wi[:, :NA]), wi[:, NA:]

    u1, proj_a = _proj_attn(xs, norm1_g + ag1[4][0, 0], wi_a, name="proj_attn")
    proj_h = _mm(u1, wi_h, name="proj_hgrn")
    attn, lse = _attn_fwd(proj_a, name="attn_fwd")
    lands = _copies_wait(ag1, _ag_chips_plan, attn, name="ag_chips_wait")
    lands = [lax.dynamic_update_index_in_dim(l, s, me, 0) for l, s in zip(lands, later)]
    ag2 = _copies_start([], lands, _ag_sibling_plan, 4 * len(later), name="ag_sibling_start")
    rec, states = _hgrn_fwd(proj_h, hgrn_lb_logits + ag2[4][0, 0], name="hgrn_fwd")
    g_out, g_up, g_down, g_cw = _copies_wait(ag2, _ag_sibling_plan, rec, name="ag_sibling_wait")
    wo = g_out.reshape(-1, D)
    wu = _gathered_cols(g_up)
    wd = g_down.reshape(-1, D)
    cw = _gathered_cols(g_cw)
    h1, u2, mixed = _out_proj(attn, rec, proj_h, xs, attn_norm_g, hgrn_norm_g, norm2_g, wo, name="out_proj")
    gate, val, act = _up_glu(u2, wu, cw, conv_b, name="up_glu")
    dh2, dh2b, d_fng, loss_part = _down_loss(act, wd, h1, fng, target, name="down_loss")

    dgate, dval, d_cw, d_cb = _dact_glu_bwd(dh2b, wd, gate, val, cw, conv_b, name="dact_glu_bwd")
    dw_down = _mm_tn(act, dh2b, tm=256, name="dw_down")
    dh1, dh1b, d_n2g = _grad_norm_input([dgate, dval], [wu], h1, norm2_g, dh2, name="du2_norm2_bwd")
    dw_up = [_mm_tn(u2, dy, tn=256, name=f"dw_up_{nm}") for nm, dy in (("gate", dgate), ("val", dval))]
    rs_ffn = _rs_start([dw_down.reshape(N_DEV, -1, D),
                        jnp.concatenate([h.reshape(D, N_DEV // 2, -1).transpose(1, 0, 2) for h in dw_up], axis=0)],
                       me, name="rs_ffn_start")
    dattn, delta, drec, dhg, d_ang, d_hng = _dmix_post_bwd(dh1b, wo, attn, rec, proj_h, attn_norm_g + rs_ffn[0][4][0, 0],
                                                          hgrn_norm_g, name="dmix_post_bwd")
    dw_out = _mm_tn(mixed, dh1b, name="dw_out")
    rs_out = _rs_start([dw_out.reshape(N_DEV, -1, D)], me, name="rs_out_start")
    dproj_h, d_lbl = _hgrn_bwd(proj_h, hgrn_lb_logits + rs_out[0][4][0, 0], states, drec, name="hgrn_bwd")
    dproj_a = _attn_bwd(proj_a, dattn, lse, delta, name="attn_bwd")
    dw_in = jnp.concatenate([_pair_major(_mm_tn(u1, dproj_a, tn=512, name="dw_in_attn"), inverse=True),
                             _mm_tn(u1, dproj_h, tn=512, name="dw_in_hgrn"),
                             _mm_tn(u1, dhg, tn=512, name="dw_in_gate")], axis=1)
    rs_in = _rs_start([_by_device_cols(dw_in)], me, name="rs_in_start")
    grad_x, _, d_n1g = _grad_norm_input([dproj_a, dproj_h, dhg], [wi_a, wi_h], xs, norm1_g + rs_in[0][4][0, 0], dh1,
                                        name="du1_norm1_bwd")

    res = {}

    def update(nm, parts, w, m, v):
        res[nm] = _adamw(parts, w, m, v, name=f"adamw_{nm}")

    g_down, g_up = _rs_finish(rs_ffn, grad_x, name="rs_ffn_wait")
    update("w_down", g_down, w_down, m_w_down, v_w_down)
    update("w_up", g_up, w_up, m_w_up, v_w_up)
    (g_out,) = _rs_finish(rs_out, grad_x, name="rs_out_wait")
    update("w_out", g_out, w_out, m_w_out, v_w_out)
    (g_in,) = _rs_finish(rs_in, res["w_up"][1], name="rs_in_wait")
    update("w_in", g_in, w_in, m_w_in, v_w_in)

    small = [("norm1_g", d_n1g, norm1_g, m_norm1_g, v_norm1_g),
             ("attn_norm_g", d_ang, attn_norm_g, m_attn_norm_g, v_attn_norm_g),
             ("hgrn_norm_g", d_hng, hgrn_norm_g, m_hgrn_norm_g, v_hgrn_norm_g),
             ("hgrn_lb_logits", d_lbl, hgrn_lb_logits, m_hgrn_lb_logits, v_hgrn_lb_logits),
             ("norm2_g", d_n2g, norm2_g, m_norm2_g, v_norm2_g),
             ("conv_b", d_cb, conv_b, m_conv_b, v_conv_b),
             ("final_norm_g", d_fng, final_norm_g, m_final_norm_g, v_final_norm_g)]
    pack = lambda arrs: jnp.concatenate([a.reshape(1, -1) for a in arrs], axis=1)
    g_small, g_dcw = _all_gather([pack([s[1] for s in small]), d_cw], name="ag_small_grads")
    sm = _adamw([g_small], pack([s[2] for s in small]), pack([s[3] for s in small]),
                pack([s[4] for s in small]), name="adamw_small")
    off = 0
    for nm, _, w, _, _ in small:
        res[nm] = [r[:, off:off + w.size].reshape(w.shape) for r in sm]
        off += w.size
    ncw = conv_w.shape[-1]
    mine_cw = lax.dynamic_slice_in_dim(g_dcw, me * ncw, ncw, axis=2)
    res["conv_w"] = _adamw([mine_cw], conv_w, m_conv_w, v_conv_w, name="adamw_conv_w")

    loss = lax.psum(loss_part[0, 0], MESH_AXES)
    order = ["norm1_g", "w_in", "attn_norm_g", "hgrn_norm_g", "hgrn_lb_logits", "w_out", "norm2_g", "w_up",
             "conv_w", "conv_b", "w_down", "final_norm_g"]
    return (loss, grad_x[None], *[res[nm][0] for nm in order], *[res[nm][1] for nm in order],
            *[res[nm][2] for nm in order], *[res[nm][3] for nm in order])
```

```python
import jax
import jax.numpy as jnp
from jax import lax
from jax.experimental import pallas as pl
from jax.experimental.pallas import tpu as pltpu

F32, BF16 = jnp.float32, jnp.bfloat16
NORM_EPS = 1e-6
ATTN_HEADS, HEAD_DIM, ATTN_BLOCK = 8, 64, 128
DILATIONS = (1, 4, 16)
ATTN_SCALE = HEAD_DIM ** -0.5
ATTN_W = ATTN_HEADS * HEAD_DIM
HGRN_HEADS, HGRN_DIM, HGRN_CHUNK = 4, 128, 64
HGRN_W = HGRN_HEADS * HGRN_DIM
ADAM_LR, ADAM_B1, ADAM_B2, ADAM_EPS, ADAM_WD, ADAM_STEP = 0.001, 0.9, 0.999, 1e-08, 0.01, 10
LANES, SUBLANES = 128, 8
VMEM_LIMIT_BYTES = 56 * 1024 * 1024
N_DEV = 8
MESH = pl.DeviceIdType.MESH
HBM = pl.BlockSpec(memory_space=pltpu.HBM)
HIGHEST = lax.Precision.HIGHEST


def _cparams(*sem):
    return pltpu.CompilerParams(dimension_semantics=sem, vmem_limit_bytes=VMEM_LIMIT_BYTES)


def _tile(n, pref):
    if n <= pref:
        return n
    t = (pref // LANES) * LANES
    while n % t:
        t -= LANES
    return t


def _resident(shape):
    return pl.BlockSpec(shape, lambda *_: (0,) * len(shape), pipeline_mode=pl.Buffered(1))


def _dot(a, b, dims, precision=None):
    return lax.dot_general(a, b, (dims, ((), ())), precision=precision, preferred_element_type=F32)


def _nn(a, b, precision=None):
    return _dot(a, b, ((1,), (0,)), precision)


def _nt(a, b):
    return _dot(a, b, ((1,), (1,)))


def _tn(a, b):
    return _dot(a, b, ((0,), (0,)))


def _sigmoid(x):
    return 1.0 / (1.0 + jnp.exp(-x))


def _rstd(x):
    return lax.rsqrt(jnp.mean(x * x, axis=-1, keepdims=True) + NORM_EPS)


def _norm_bwd(x, g, du):
    r = _rstd(x)
    xh = x * r
    dxh = du * g
    return r * (dxh - xh * jnp.mean(dxh * xh, axis=-1, keepdims=True)), du * xh


def _row_halves(tm):
    return [pl.ds(0, tm // 2), pl.ds(tm // 2, tm // 2)]


def _accumulate(ref, part, first):
    @pl.when(first)
    def _():
        ref[...] = part

    @pl.when(jnp.logical_not(first))
    def _():
        ref[...] += part


def _mm(a, b, *, name, out_dtype=F32, tm=1024, tn=512):
    M, K = a.shape
    N = b.shape[1]
    tm, tn = _tile(M, tm), _tile(N, tn)

    def body(a_ref, b_ref, o_ref):
        o_ref[...] = _nn(a_ref[...], b_ref[...]).astype(out_dtype)

    return pl.pallas_call(
        body, name=name, grid=(M // tm, N // tn),
        in_specs=[pl.BlockSpec((tm, K), lambda i, j: (i, 0)), pl.BlockSpec((K, tn), lambda i, j: (0, j))],
        out_specs=pl.BlockSpec((tm, tn), lambda i, j: (i, j)), out_shape=jax.ShapeDtypeStruct((M, N), out_dtype),
        compiler_params=_cparams("parallel", "parallel"),
    )(a, b)


def _mm_tn(x, dy, *, name, tm=512, tn=1024):
    S, M = x.shape
    N = dy.shape[1]
    tm, tn = _tile(M, tm), _tile(N, tn)

    def body(x_ref, dy_ref, o_ref, xt_ref):
        @pl.when(pl.program_id(1) == 0)
        def _():
            xt_ref[...] = x_ref[...].T

        o_ref[...] = _nn(xt_ref[...], dy_ref[...]).astype(BF16)

    return pl.pallas_call(
        body, name=name, grid=(M // tm, N // tn),
        in_specs=[pl.BlockSpec((S, tm), lambda i, j: (0, i)), pl.BlockSpec((S, tn), lambda i, j: (0, j))],
        out_specs=pl.BlockSpec((tm, tn), lambda i, j: (i, j)), out_shape=jax.ShapeDtypeStruct((M, N), BF16),
        scratch_shapes=[pltpu.VMEM((tm, S), BF16)], compiler_params=_cparams("parallel", "arbitrary"),
    )(x, dy)


def _proj_attn(x, g, w, *, name, tm=1024, tn=512):
    S, D = x.shape
    N = w.shape[1]

    def body(x_ref, g_ref, w_ref, u_ref, o_ref):
        @pl.when(pl.program_id(1) == 0)
        def _():
            xv = x_ref[...]
            u_ref[...] = (xv * _rstd(xv) * g_ref[...]).astype(BF16)

        o_ref[...] = _nn(u_ref[...], w_ref[...]).astype(BF16)

    return pl.pallas_call(
        body, name=name, grid=(S // tm, N // tn),
        in_specs=[pl.BlockSpec((tm, D), lambda i, j: (i, 0)), pl.BlockSpec((1, D), lambda i, j: (0, 0)),
                  pl.BlockSpec((D, tn), lambda i, j: (0, j))],
        out_specs=[pl.BlockSpec((tm, D), lambda i, j: (i, 0)), pl.BlockSpec((tm, tn), lambda i, j: (i, j))],
        out_shape=[jax.ShapeDtypeStruct((S, D), BF16), jax.ShapeDtypeStruct((S, N), BF16)],
        compiler_params=_cparams("parallel", "arbitrary"),
    )(x, g, w)


PAIR_W = 3 * LANES
ATTN_UNROLL_FWD, ATTN_UNROLL_BWD = 4, 4


def _attn_masks(first):
    qi = lax.broadcasted_iota(jnp.int32, (ATTN_BLOCK, 2 * ATTN_BLOCK), 0)
    kj = lax.broadcasted_iota(jnp.int32, (ATTN_BLOCK, 2 * ATTN_BLOCK), 1)
    dist = qi + ATTN_BLOCK - kj
    valid = (dist >= 0) & (dist <= ATTN_BLOCK) & jnp.logical_or(kj >= ATTN_BLOCK, jnp.logical_not(first))
    lane = lax.broadcasted_iota(jnp.int32, (1, LANES), 1)
    return valid, lane


def _for_residue_blocks(S, d, fn):
    span = ATTN_BLOCK * d
    nb = S // span

    def step(n, carry):
        base = pl.multiple_of(n * span, span)
        for r in range(d):
            off = pl.multiple_of((r * nb + n) * ATTN_BLOCK, ATTN_BLOCK)
            fn(lambda ref, r=r: _block_rows(ref, base, r, d),
               lambda ref, val, r=r: _set_block_rows(ref, base, r, d, val), off)
        return carry

    lax.fori_loop(0, nb, step, 0)


def _for_blocks(S, unroll, fn):
    def step(i, carry):
        fn([(pl.multiple_of((i * unroll + u) * ATTN_BLOCK, ATTN_BLOCK), i * unroll + u) for u in range(unroll)])
        return carry

    lax.fori_loop(0, S // ATTN_BLOCK // unroll, step, 0)


def _head_value(x2, lane, e):
    return jnp.sum(jnp.where(lane == HEAD_DIM * e, x2, 0.0), axis=-1, keepdims=True)


def _block_rows(ref, base, r, d):
    if d == 1:
        return ref[pl.ds(base, ATTN_BLOCK), :]
    return ref.at[pl.ds(base, ATTN_BLOCK * d)][pl.ds(r, ATTN_BLOCK, stride=d), :]


def _set_block_rows(ref, base, r, d, val):
    if d == 1:
        ref[pl.ds(base, ATTN_BLOCK), :] = val
    else:
        ref.at[pl.ds(base, ATTN_BLOCK * d)][pl.ds(r, ATTN_BLOCK, stride=d), :] = val


def _split_pair(p_ref, qs, ks, vs, bk, bv):
    qs[...] = p_ref[:, 0:LANES].astype(F32)
    ks[...] = p_ref[:, LANES:2 * LANES].astype(F32)
    vs[...] = p_ref[:, 2 * LANES:3 * LANES].astype(F32)
    bk[0:ATTN_BLOCK, :] = jnp.zeros((ATTN_BLOCK, LANES), bk.dtype)
    bv[0:ATTN_BLOCK, :] = jnp.zeros((ATTN_BLOCK, LANES), bv.dtype)


def _regroup_qkv(rows, off, qs, ks, vs, bq, bk, bv):
    blk, shifted = pl.ds(off, ATTN_BLOCK), pl.ds(off + ATTN_BLOCK, ATTN_BLOCK)
    bq[blk, :] = rows(qs).astype(BF16)
    bk[shifted, :] = rows(ks).astype(BF16)
    bv[shifted, :] = rows(vs).astype(BF16)


def _attn_fwd(proj_a, *, name):
    S = proj_a.shape[0]

    def body(p_ref, o_ref, l_ref, qs, ks, vs, bq, bk, bv, bo, bl):
        _split_pair(p_ref, qs, ks, vs, bk, bv)
        for d in DILATIONS:
            nb = S // (ATTN_BLOCK * d)
            _for_residue_blocks(S, d, lambda rows, _, off: _regroup_qkv(rows, off, qs, ks, vs, bq, bk, bv))

            def blocks(group, nb=nb):
                lane = lax.broadcasted_iota(jnp.int32, (1, LANES), 1)
                heads = [(lane >= HEAD_DIM * e) & (lane < HEAD_DIM * (e + 1)) for e in range(LANES // HEAD_DIM)]
                wins = [pl.ds(off, 2 * ATTN_BLOCK) for off, _ in group]
                s = [[_nt(jnp.where(mh, bq[pl.ds(off, ATTN_BLOCK), :], jnp.zeros((ATTN_BLOCK, LANES), BF16)), bk[win, :])
                      for mh in heads] for (off, _), win in zip(group, wins)]
                p, m, l = [], [], []
                for (off, b), su in zip(group, s):
                    valid, _ = _attn_masks(jnp.bitwise_and(b, nb - 1) == 0)
                    sm = [jnp.where(valid, x * ATTN_SCALE, -jnp.inf) for x in su]
                    m.append([jnp.max(x, axis=-1, keepdims=True) for x in sm])
                    p.append([jnp.exp(x - mx) for x, mx in zip(sm, m[-1])])
                    l.append([jnp.sum(x, axis=-1, keepdims=True) for x in p[-1]])
                o = [[_nn(x.astype(BF16), bv[win, :]) for x in pu] for pu, win in zip(p, wins)]
                for (off, _), ou, mu, lu in zip(group, o, m, l):
                    o2 = jnp.zeros((ATTN_BLOCK, LANES), F32)
                    l2 = jnp.zeros((ATTN_BLOCK, LANES), F32)
                    for mh, oe, me_, le in zip(heads, ou, mu, lu):
                        o2 = jnp.where(mh, oe / le, o2)
                        l2 = jnp.where(mh, me_ + jnp.log(le), l2)
                    bo[pl.ds(off, ATTN_BLOCK), :] = o2
                    bl[pl.ds(off, ATTN_BLOCK), :] = l2

            _for_blocks(S, ATTN_UNROLL_FWD, blocks)

            def merge(rows, set_rows, off, d=d):
                blk = pl.ds(off, ATTN_BLOCK)
                o2, l2 = bo[blk, :], bl[blk, :]
                if d != DILATIONS[0]:
                    lo, oo = rows(l_ref), rows(o_ref)
                    ln = jnp.maximum(lo, l2)
                    wa, wb = jnp.exp(lo - ln), jnp.exp(l2 - ln)
                    o2 = (wa * oo + wb * o2) / (wa + wb)
                    l2 = ln + jnp.log(wa + wb)
                set_rows(o_ref, o2)
                set_rows(l_ref, l2)

            _for_residue_blocks(S, d, merge)

    slab = pl.BlockSpec((S, LANES), lambda p: (0, p))
    f32_slab, bf16_slab = pltpu.VMEM((S, LANES), F32), pltpu.VMEM((S, LANES), BF16)
    bf16_window = pltpu.VMEM((S + ATTN_BLOCK, LANES), BF16)
    return pl.pallas_call(
        body, name=name, grid=(ATTN_W // LANES,), in_specs=[pl.BlockSpec((S, PAIR_W), lambda p: (0, p))],
        out_specs=[slab, slab],
        out_shape=[jax.ShapeDtypeStruct((S, ATTN_W), F32), jax.ShapeDtypeStruct((S, ATTN_W), F32)],
        scratch_shapes=[f32_slab] * 3 + [bf16_slab, bf16_window, bf16_window, f32_slab, f32_slab],
        compiler_params=_cparams("parallel"),
    )(proj_a)


def _attn_bwd(proj_a, do, lse, delta, *, name):
    S = proj_a.shape[0]

    def body(p_ref, do_ref, lse_ref, dl_ref, o_ref, qs, ks, vs, dqs, dks, dvs, bq, bk, bv, bdo, blse, bdl, bdq, bdk, bdv):
        _split_pair(p_ref, qs, ks, vs, bk, bv)
        bdk[0:ATTN_BLOCK, :] = jnp.zeros((ATTN_BLOCK, LANES), F32)
        bdv[0:ATTN_BLOCK, :] = jnp.zeros((ATTN_BLOCK, LANES), F32)
        for d in DILATIONS:
            nb = S // (ATTN_BLOCK * d)

            def regroup(rows, _, off):
                _regroup_qkv(rows, off, qs, ks, vs, bq, bk, bv)
                blk = pl.ds(off, ATTN_BLOCK)
                bdo[blk, :] = rows(do_ref).astype(BF16)
                blse[blk, :] = rows(lse_ref)
                bdl[blk, :] = rows(dl_ref)

            _for_residue_blocks(S, d, regroup)

            def blocks(group, nb=nb):
                lane = lax.broadcasted_iota(jnp.int32, (1, LANES), 1)
                heads = [(lane >= HEAD_DIM * e) & (lane < HEAD_DIM * (e + 1)) for e in range(LANES // HEAD_DIM)]
                zero = jnp.zeros((ATTN_BLOCK, LANES), BF16)
                chains = [(off, b, e, mh) for off, b in group for e, mh in enumerate(heads)]
                qm = [jnp.where(mh, bq[pl.ds(off, ATTN_BLOCK), :], zero) for off, _, _, mh in chains]
                dom = [jnp.where(mh, bdo[pl.ds(off, ATTN_BLOCK), :], zero) for off, _, _, mh in chains]
                s = [_nt(x, bk[pl.ds(off, 2 * ATTN_BLOCK), :]) for x, (off, _, _, _) in zip(qm, chains)]
                dp = [_nt(x, bv[pl.ds(off, 2 * ATTN_BLOCK), :]) for x, (off, _, _, _) in zip(dom, chains)]
                p, ds = [], []
                for (off, b, e, _), sc, dpc in zip(chains, s, dp):
                    valid, _ = _attn_masks(jnp.bitwise_and(b, nb - 1) == 0)
                    blk = pl.ds(off, ATTN_BLOCK)
                    pc = jnp.where(valid, jnp.exp(sc * ATTN_SCALE - _head_value(blse[blk, :], lane, e)), 0.0)
                    ds.append((pc * (dpc - _head_value(bdl[blk, :], lane, e)) * ATTN_SCALE).astype(BF16))
                    p.append(pc.astype(BF16))
                dq = [_nn(x, bk[pl.ds(off, 2 * ATTN_BLOCK), :]) for x, (off, _, _, _) in zip(ds, chains)]
                dk = [_tn(x, y) for x, y in zip(ds, qm)]
                dv = [_tn(x, y) for x, y in zip(p, dom)]
                nh = len(heads)
                for u, (off, _) in enumerate(group):
                    dq2 = jnp.zeros((ATTN_BLOCK, LANES), F32)
                    for mh, x in zip(heads, dq[nh * u:nh * (u + 1)]):
                        dq2 = jnp.where(mh, x, dq2)
                    bdq[pl.ds(off, ATTN_BLOCK), :] = dq2
                    for acc, grads in ((bdk, dk), (bdv, dv)):
                        win_grad = sum(grads[nh * u + 1:nh * (u + 1)], grads[nh * u])
                        acc[pl.ds(off, ATTN_BLOCK), :] += win_grad[:ATTN_BLOCK]
                        acc[pl.ds(off + ATTN_BLOCK, ATTN_BLOCK), :] = win_grad[ATTN_BLOCK:]

            _for_blocks(S, ATTN_UNROLL_BWD, blocks)

            def scatter(rows, set_rows, off, d=d):
                blk, shifted = pl.ds(off, ATTN_BLOCK), pl.ds(off + ATTN_BLOCK, ATTN_BLOCK)
                for acc, part in ((dqs, bdq[blk, :]), (dks, bdk[shifted, :]), (dvs, bdv[shifted, :])):
                    set_rows(acc, part if d == DILATIONS[0] else rows(acc) + part)

            _for_residue_blocks(S, d, scatter)
        o_ref[:, 0:LANES] = dqs[...].astype(BF16)
        o_ref[:, LANES:2 * LANES] = dks[...].astype(BF16)
        o_ref[:, 2 * LANES:3 * LANES] = dvs[...].astype(BF16)

    slab = pl.BlockSpec((S, LANES), lambda p: (0, p), pipeline_mode=pl.Buffered(1))
    pair = pl.BlockSpec((S, PAIR_W), lambda p: (0, p))
    f32_slab, bf16_slab = pltpu.VMEM((S, LANES), F32), pltpu.VMEM((S, LANES), BF16)
    f32_window, bf16_window = pltpu.VMEM((S + ATTN_BLOCK, LANES), F32), pltpu.VMEM((S + ATTN_BLOCK, LANES), BF16)
    return pl.pallas_call(
        body, name=name, grid=(ATTN_W // LANES,), in_specs=[pair, slab, slab, slab], out_specs=pair,
        out_shape=jax.ShapeDtypeStruct(proj_a.shape, BF16),
        scratch_shapes=[f32_slab] * 6 + [bf16_slab, bf16_window, bf16_window, bf16_slab, f32_slab, f32_slab,
                                         f32_slab, f32_window, f32_window],
        compiler_params=_cparams("parallel"),
    )(proj_a, do, lse, delta)


HG_T = 2 * HGRN_CHUNK


def _hgrn_consts():
    row = lax.broadcasted_iota(jnp.int32, (HG_T, HG_T), 0)
    col = lax.broadcasted_iota(jnp.int32, (HG_T, HG_T), 1)
    same = (row >= HGRN_CHUNK) == (col >= HGRN_CHUNK)
    return row, same & (col <= row), same & (col >= row)


def _lower_bound(logits_ref):
    l0, l1 = logits_ref[0:1, :], logits_ref[1:2, :]
    mx = jnp.maximum(l0, l1)
    e0, e1 = jnp.exp(l0 - mx), jnp.exp(l1 - mx)
    return e0 / (e0 + e1)


def _hgrn_gates(qs, fs, lbs, row, causal):
    C = HGRN_CHUNK
    tri = jnp.where(causal, 1.0, 0.0).astype(F32)
    sgs = [_sigmoid(f) for f in fs]
    forgets = [lb + (1.0 - lb) * sg for lb, sg in zip(lbs, sgs)]
    logfs = [jnp.log(forget) for forget in forgets]
    bs = [_nn(tri, logf, HIGHEST) for logf in logfs]
    out = []
    for q, sg, forget, logf, b in zip(qs, sgs, forgets, logfs, bs):
        key = 1.0 - forget
        bend0 = jnp.sum(logf[:C], axis=0, keepdims=True)
        bend1 = jnp.sum(logf[C:], axis=0, keepdims=True)
        bend = jnp.where(row < C, bend0, bend1)
        eb, emb, eend = jnp.exp(b), jnp.exp(-b), jnp.exp(bend - b)
        sq = _sigmoid(q)
        out.append(dict(sg=sg, forget=forget, key=key, bend0=bend0, bend1=bend1, eb=eb, emb=emb, eend=eend, sq=sq,
                        qd=q * sq * eb, ki=key * emb, ke=key * eend))
    return out


def _hgrn_fwd(proj, logits, *, name):
    S = proj.shape[0]
    W, C = HGRN_W, HGRN_CHUNK

    def body(q_ref, f_ref, i_ref, lg_ref, rec_ref, st_ref, s_ref):
        @pl.when(pl.program_id(0) == 0)
        def _():
            s_ref[...] = jnp.zeros_like(s_ref)

        row, causal, _ = _hgrn_consts()
        lb_all = _lower_bound(lg_ref)
        heads = range(HGRN_HEADS)
        sls = [slice(HGRN_DIM * h, HGRN_DIM * (h + 1)) for h in heads]
        gts = _hgrn_gates([q_ref[:, sl] for sl in sls], [f_ref[:, sl] for sl in sls], [lb_all[:, sl] for sl in sls],
                          row, causal)
        qd, ki, ke = ([gt[k].astype(BF16) for gt in gts] for k in ("qd", "ki", "ke"))
        iv = [i_ref[:, sl].astype(BF16) for sl in sls]
        s0 = [s_ref[h] for h in heads]
        a = [_nt(qd[h], ki[h]) for h in heads]
        o0 = [_nt(qd[h][:C], s0[h].astype(BF16)) for h in heads]
        u0 = [_tn(iv[h][:C], ke[h][:C]) for h in heads]
        u1 = [_tn(iv[h][C:], ke[h][C:]) for h in heads]
        s1 = [jnp.exp(gts[h]["bend0"]) * s0[h] + u0[h] for h in heads]
        o = [_nn(jnp.where(causal, a[h], 0.0).astype(BF16), iv[h]) for h in heads]
        o1 = [_nt(qd[h][C:], s1[h].astype(BF16)) for h in heads]
        for h in heads:
            st_ref[0, h] = s0[h]
            st_ref[1, h] = s1[h]
            s_ref[h] = jnp.exp(gts[h]["bend1"]) * s1[h] + u1[h]
            rec_ref[:, sls[h]] = o[h] + jnp.concatenate([o0[h], o1[h]], axis=0)

    blk = lambda j: pl.BlockSpec((HG_T, W), lambda t: (t, j))
    return pl.pallas_call(
        body, name=name, grid=(S // HG_T,),
        in_specs=[blk(0), blk(1), blk(2), pl.BlockSpec((2, W), lambda t: (0, 0))],
        out_specs=[blk(0), pl.BlockSpec((2, HGRN_HEADS, HGRN_DIM, HGRN_DIM), lambda t: (t, 0, 0, 0))],
        out_shape=[jax.ShapeDtypeStruct((S, W), F32),
                   jax.ShapeDtypeStruct((S // C, HGRN_HEADS, HGRN_DIM, HGRN_DIM), F32)],
        scratch_shapes=[pltpu.VMEM((HGRN_HEADS, HGRN_DIM, HGRN_DIM), F32)],
        compiler_params=_cparams("arbitrary"),
    )(proj, proj, proj, logits)


def _hgrn_bwd(proj, logits, states, drec, *, name):
    S = proj.shape[0]
    W, C = HGRN_W, HGRN_CHUNK
    nt = S // HG_T

    def body(q_ref, f_ref, i_ref, lg_ref, st_ref, do_ref, dp_ref, dlg_ref, ds_ref, dlb_ref):
        t = pl.program_id(0)

        @pl.when(t == 0)
        def _():
            ds_ref[...] = jnp.zeros_like(ds_ref)
            dlb_ref[...] = jnp.zeros_like(dlb_ref)

        row, causal, anti = _hgrn_consts()
        lb_all = _lower_bound(lg_ref)
        heads = range(HGRN_HEADS)
        sls = [slice(HGRN_DIM * h, HGRN_DIM * (h + 1)) for h in heads]
        qs, lbs = [q_ref[:, sl] for sl in sls], [lb_all[:, sl] for sl in sls]
        gts = _hgrn_gates(qs, [f_ref[:, sl] for sl in sls], lbs, row, causal)
        qd, ki, ke = ([gt[k] for gt in gts] for k in ("qd", "ki", "ke"))
        qdb, kib, keb = ([x.astype(BF16) for x in xs] for xs in (qd, ki, ke))
        iv = [i_ref[:, sl].astype(BF16) for sl in sls]
        dob = [do_ref[:, sl].astype(BF16) for sl in sls]
        s0, s1, ds1 = [st_ref[0, h] for h in heads], [st_ref[1, h] for h in heads], [ds_ref[h] for h in heads]
        ds1b = [x.astype(BF16) for x in ds1]
        dec0, dec1 = [jnp.exp(gt["bend0"]) for gt in gts], [jnp.exp(gt["bend1"]) for gt in gts]
        a = [_nt(qdb[h], kib[h]) for h in heads]
        da = [_nt(dob[h], iv[h]) for h in heads]
        dqd1 = [_nn(dob[h][C:], s1[h].astype(BF16)) for h in heads]
        dqd0 = [_nn(dob[h][:C], s0[h].astype(BF16)) for h in heads]
        di1 = [_nt(keb[h][C:], ds1b[h]) for h in heads]
        dke1 = [_nn(iv[h][C:], ds1b[h]) for h in heads]
        t1 = [_tn(dob[h][C:], qdb[h][C:]) for h in heads]
        t0 = [_tn(dob[h][:C], qdb[h][:C]) for h in heads]
        ds0 = [dec1[h] * ds1[h] + t1[h] for h in heads]
        ds0b = [x.astype(BF16) for x in ds0]
        a = [jnp.where(causal, x, 0.0).astype(BF16) for x in a]
        da = [jnp.where(causal, x, 0.0).astype(BF16) for x in da]
        di0 = [_nt(keb[h][:C], ds0b[h]) for h in heads]
        dke0 = [_nn(iv[h][:C], ds0b[h]) for h in heads]
        dqd_a = [_nn(da[h], kib[h]) for h in heads]
        dki = [_tn(da[h], qdb[h]) for h in heads]
        di_a = [_tn(a[h], dob[h]) for h in heads]
        dqd, dke, db = [], [], []
        for h in heads:
            ds_ref[h] = dec0[h] * ds0[h] + t0[h]
            ddec1 = jnp.sum(ds1[h] * s1[h], axis=0, keepdims=True)
            ddec0 = jnp.sum(ds0[h] * s0[h], axis=0, keepdims=True)
            dqd.append(dqd_a[h] + jnp.concatenate([dqd0[h], dqd1[h]], axis=0))
            dp_ref[:, 2 * W + HGRN_DIM * h:2 * W + HGRN_DIM * (h + 1)] = (
                di_a[h] + jnp.concatenate([di0[h], di1[h]], axis=0)).astype(BF16)
            dke.append(jnp.concatenate([dke0[h], dke1[h]], axis=0))
            gke = dke[h] * ke[h]
            dbend0 = jnp.sum(gke[:C], axis=0, keepdims=True) + ddec0 * dec0[h]
            dbend1 = jnp.sum(gke[C:], axis=0, keepdims=True) + ddec1 * dec1[h]
            dbh = dqd[h] * qd[h] - dki[h] * ki[h] - gke
            db.append(dbh + jnp.where(row == C - 1, dbend0, 0.0) + jnp.where(row == HG_T - 1, dbend1, 0.0))
        tri = jnp.where(anti, 1.0, 0.0).astype(F32)
        dlogf = [_nn(tri, db[h], HIGHEST) for h in heads]
        for h in heads:
            gt, lb, q = gts[h], lbs[h], qs[h]
            dforget = dlogf[h] / gt["forget"] - (dki[h] * gt["emb"] + dke[h] * gt["eend"])
            sg, sq = gt["sg"], gt["sq"]
            dp_ref[:, W + HGRN_DIM * h:W + HGRN_DIM * (h + 1)] = (dforget * (1.0 - lb) * sg * (1.0 - sg)).astype(BF16)
            dlb_ref[:, sls[h]] += jnp.sum(dforget * (1.0 - sg), axis=0, keepdims=True)
            dp_ref[:, sls[h]] = (dqd[h] * gt["eb"] * sq * (1.0 + q * (1.0 - sq))).astype(BF16)

        @pl.when(t == nt - 1)
        def _():
            dl0 = dlb_ref[...] * lb_all * (1.0 - lb_all)
            dlg_ref[0:1, :] = dl0
            dlg_ref[1:2, :] = -dl0

    blk = lambda j: pl.BlockSpec((HG_T, W), lambda t: (nt - 1 - t, j))
    full = pl.BlockSpec((2, W), lambda t: (0, 0))
    return pl.pallas_call(
        body, name=name, grid=(nt,),
        in_specs=[blk(0), blk(1), blk(2), full,
                  pl.BlockSpec((2, HGRN_HEADS, HGRN_DIM, HGRN_DIM), lambda t: (nt - 1 - t, 0, 0, 0)), blk(0)],
        out_specs=[pl.BlockSpec((HG_T, 3 * W), lambda t: (nt - 1 - t, 0)), full],
        out_shape=[jax.ShapeDtypeStruct((S, 3 * W), BF16), jax.ShapeDtypeStruct((2, W), F32)],
        scratch_shapes=[pltpu.VMEM((HGRN_HEADS, HGRN_DIM, HGRN_DIM), F32), pltpu.VMEM((1, W), F32)],
        compiler_params=_cparams("arbitrary"),
    )(proj, proj, proj, logits, states, drec)


def _out_proj(attn, rec, proj_h, x, g_attn, g_hgrn, g_norm2, w_out, *, name, tm=512):
    S, D = x.shape
    AW, W = ATTN_W, HGRN_W

    def body(a_ref, r_ref, hg_ref, x_ref, ga_ref, gh_ref, g2_ref, w_ref, h_ref, u_ref, m_ref):
        av = a_ref[...]
        m_ref[:, :AW] = (av * _rstd(av) * ga_ref[...]).astype(BF16)
        for h in range(HGRN_HEADS):
            sl = slice(HGRN_DIM * h, HGRN_DIM * (h + 1))
            rv, hg = r_ref[:, sl], hg_ref[:, sl]
            m_ref[:, AW + HGRN_DIM * h:AW + HGRN_DIM * (h + 1)] = (
                (rv * _rstd(rv) * gh_ref[:, sl]) * (hg * _sigmoid(hg))).astype(BF16)
        h1 = x_ref[...] + _nn(m_ref[...], w_ref[...])
        h_ref[...] = h1
        u_ref[...] = (h1 * _rstd(h1) * g2_ref[...]).astype(BF16)

    row = lambda w, j=0: pl.BlockSpec((tm, w), lambda i: (i, j))
    vec = lambda w: pl.BlockSpec((1, w), lambda i: (0, 0))
    return pl.pallas_call(
        body, name=name, grid=(S // tm,),
        in_specs=[row(AW), row(W), row(W, 3), row(D), vec(AW), vec(W), vec(D), _resident(w_out.shape)],
        out_specs=[row(D), row(D), row(AW + W)],
        out_shape=[jax.ShapeDtypeStruct((S, D), F32), jax.ShapeDtypeStruct((S, D), BF16),
                   jax.ShapeDtypeStruct((S, AW + W), BF16)],
        compiler_params=_cparams("parallel"),
    )(attn, rec, proj_h, x, g_attn, g_hgrn, g_norm2, w_out)


def _dmix_post_bwd(dh1b, w_out, attn, rec, proj_h, g_attn, g_hgrn, *, name, tm=512):
    S, D = dh1b.shape
    AW, W = ATTN_W, HGRN_W

    def body(dh_ref, w_ref, a_ref, r_ref, hg_ref, ga_ref, gh_ref, do_ref, dl_ref, dr_ref, dhg_ref, dga_ref, dgh_ref):
        first = pl.program_id(0) == 0
        dmix = _nt(dh_ref[...], w_ref[...])
        av = a_ref[...]
        dov, dga = _norm_bwd(av, ga_ref[...], dmix[:, :AW])
        do_ref[...] = dov
        shift = HEAD_DIM.bit_length() - 1
        hi = lax.shift_right_logical(lax.broadcasted_iota(jnp.int32, (AW, AW), 0), shift)
        hj = lax.shift_right_logical(lax.broadcasted_iota(jnp.int32, (AW, AW), 1), shift)
        prod = dov * av
        hi_part = prod.astype(BF16)
        lo_part = (prod - hi_part.astype(F32)).astype(BF16)
        same_head = jnp.where(hi == hj, 1.0, 0.0).astype(BF16)
        dl_ref[...] = _nn(hi_part, same_head) + _nn(lo_part, same_head)
        _accumulate(dga_ref, jnp.sum(dga, axis=0, keepdims=True), first)

        @pl.when(first)
        def _():
            dgh_ref[...] = jnp.zeros_like(dgh_ref)

        for h in range(HGRN_HEADS):
            sl = slice(HGRN_DIM * h, HGRN_DIM * (h + 1))
            rv, hg, gv = r_ref[:, sl], hg_ref[:, sl], gh_ref[:, sl]
            dout = dmix[:, AW + HGRN_DIM * h:AW + HGRN_DIM * (h + 1)]
            sg = _sigmoid(hg)
            drv, dgh = _norm_bwd(rv, gv, dout * (hg * sg))
            dr_ref[:, sl] = drv
            dgh_ref[:, sl] += jnp.sum(dgh, axis=0, keepdims=True)
            dhg_ref[:, sl] = (dout * (rv * _rstd(rv) * gv) * (sg * (1.0 + hg * (1.0 - sg)))).astype(BF16)

    row = lambda w, j=0: pl.BlockSpec((tm, w), lambda i: (i, j))
    vec = lambda w: pl.BlockSpec((1, w), lambda i: (0, 0))
    return pl.pallas_call(
        body, name=name, grid=(S // tm,),
        in_specs=[row(D), _resident(w_out.shape), row(AW), row(W), row(W, 3), vec(AW), vec(W)],
        out_specs=[row(AW), row(AW), row(W), row(W), vec(AW), vec(W)],
        out_shape=[jax.ShapeDtypeStruct((S, AW), F32), jax.ShapeDtypeStruct((S, AW), F32),
                   jax.ShapeDtypeStruct((S, W), F32), jax.ShapeDtypeStruct((S, W), BF16),
                   jax.ShapeDtypeStruct((1, AW), F32), jax.ShapeDtypeStruct((1, W), F32)],
        compiler_params=_cparams("arbitrary"),
    )(dh1b, w_out, attn, rec, proj_h, g_attn, g_hgrn)


def _conv_act(g, g1, g2, w_ref, b_ref):
    c = b_ref[...] + w_ref[0:1, :] * g2 + w_ref[1:2, :] * g1 + w_ref[2:3, :] * g
    return c, 0.5 * (1.0 + lax.erf(c * (2.0 ** -0.5)))


def _shift_down(g, halo, row):
    g1 = jnp.where(row == 0, halo[7:8], pltpu.roll(g, 1, 0))
    g2 = jnp.where(row == 0, halo[6:7], jnp.where(row == 1, halo[7:8], pltpu.roll(g, 2, 0)))
    return g1, g2


def _shift_up(x, halo, row):
    n = x.shape[0]
    x1 = jnp.where(row == n - 1, halo[0:1], pltpu.roll(x, n - 1, 0))
    x2 = jnp.where(row == n - 2, halo[0:1], jnp.where(row == n - 1, halo[1:2], pltpu.roll(x, n - 2, 0)))
    return x1, x2


def _up_glu(u, w_up, conv_w, conv_b, *, name, tm=1024, tn=256):
    S, D = u.shape
    F = w_up.shape[1] // 2
    nf = F // tn

    def body(u_ref, wg_ref, wv_ref, cw_ref, cb_ref, g_ref, v_ref, a_ref, halo_ref):
        i, j = pl.program_id(0), pl.program_id(1)

        @pl.when(i == 0)
        def _():
            halo_ref[j] = jnp.zeros((SUBLANES, tn), F32)

        uv = u_ref[...]
        g, v = _nn(uv, wg_ref[...]), _nn(uv, wv_ref[...])
        row = lax.broadcasted_iota(jnp.int32, (tm, tn), 0)
        g1, g2 = _shift_down(g, halo_ref[j], row)
        c, cdf = _conv_act(g, g1, g2, cw_ref, cb_ref)
        a_ref[...] = (c * cdf * v).astype(BF16)
        g_ref[...] = g.astype(BF16)
        v_ref[...] = v.astype(BF16)
        halo_ref[j] = g[tm - SUBLANES:, :]

    col = pl.BlockSpec((tm, tn), lambda i, j: (i, j))
    out = jax.ShapeDtypeStruct((S, F), BF16)
    return pl.pallas_call(
        body, name=name, grid=(S // tm, nf),
        in_specs=[pl.BlockSpec((tm, D), lambda i, j: (i, 0)), pl.BlockSpec((D, tn), lambda i, j: (0, j)),
                  pl.BlockSpec((D, tn), lambda i, j: (0, j + nf)), pl.BlockSpec((3, tn), lambda i, j: (0, j)),
                  pl.BlockSpec((1, tn), lambda i, j: (0, j))],
        out_specs=[col, col, col], out_shape=[out, out, out],
        scratch_shapes=[pltpu.VMEM((nf, SUBLANES, tn), F32)], compiler_params=_cparams("arbitrary", "arbitrary"),
    )(u, w_up, w_up, conv_w, conv_b)


def _dact_glu_bwd(dh2b, w_down, gate, val, conv_w, conv_b, *, name, tm=1024, tn=256):
    S, D = dh2b.shape
    F = gate.shape[1]
    nf, ni = F // tn, S // tm
    hb = tm // SUBLANES

    def body(dh_ref, wd_ref, g_ref, gh_ref, v_ref, cw_ref, cb_ref, dg_ref, dv_ref, dcw_ref, dcb_ref, halo_ref, acc_ref):
        i, j = pl.program_id(0), pl.program_id(1)

        @pl.when(i == 0)
        def _():
            halo_ref[j] = jnp.zeros((SUBLANES, tn), F32)
            acc_ref[j] = jnp.zeros((SUBLANES, tn), F32)

        g = g_ref[...].astype(F32)
        before = jnp.where(i < ni - 1, gh_ref[...].astype(F32), 0.0)
        row = lax.broadcasted_iota(jnp.int32, (tm, tn), 0)
        g1, g2 = _shift_down(g, before[SUBLANES:], row)
        c, cdf = _conv_act(g, g1, g2, cw_ref, cb_ref)
        da = _nt(dh_ref[...], wd_ref[...])
        dv_ref[...] = (da * (c * cdf)).astype(BF16)
        pdf = jnp.exp(-0.5 * c * c) * (1.0 / (2.0 * jnp.pi) ** 0.5)
        dc = da * v_ref[...].astype(F32) * (cdf + c * pdf)
        d1, d2 = _shift_up(dc, halo_ref[j], row)
        dg_ref[...] = (cw_ref[2:3, :] * dc + cw_ref[1:2, :] * d1 + cw_ref[0:1, :] * d2).astype(BF16)
        halo_ref[j] = dc[:SUBLANES, :]
        for k, t in enumerate((dc * g2, dc * g1, dc * g, dc)):
            acc_ref[j, k:k + 1, :] += jnp.sum(t, axis=0, keepdims=True)

        @pl.when((i == ni - 1) & (j == nf - 1))
        def _():
            for jj in range(nf):
                dcw_ref[:, jj * tn:(jj + 1) * tn] = acc_ref[jj, 0:3, :]
                dcb_ref[:, jj * tn:(jj + 1) * tn] = acc_ref[jj, 3:4, :]

    tile = pl.BlockSpec((tm, tn), lambda i, j: (ni - 1 - i, j))
    return pl.pallas_call(
        body, name=name, grid=(ni, nf),
        in_specs=[pl.BlockSpec((tm, D), lambda i, j: (ni - 1 - i, 0)), pl.BlockSpec((tn, D), lambda i, j: (j, 0)),
                  tile, pl.BlockSpec((SUBLANES * 2, tn), lambda i, j: (jnp.maximum((ni - 1 - i) * (hb // 2) - 1, 0), j)),
                  tile, pl.BlockSpec((3, tn), lambda i, j: (0, j)), pl.BlockSpec((1, tn), lambda i, j: (0, j))],
        out_specs=[tile, tile, pl.BlockSpec((3, F), lambda i, j: (0, 0)), pl.BlockSpec((1, F), lambda i, j: (0, 0))],
        out_shape=[jax.ShapeDtypeStruct((S, F), BF16), jax.ShapeDtypeStruct((S, F), BF16),
                   jax.ShapeDtypeStruct((3, F), F32), jax.ShapeDtypeStruct((1, F), F32)],
        scratch_shapes=[pltpu.VMEM((nf, SUBLANES, tn), F32), pltpu.VMEM((nf, SUBLANES, tn), F32)],
        compiler_params=_cparams("arbitrary", "arbitrary"),
    )(dh2b, w_down, gate, gate, val, conv_w, conv_b)


def _down_loss(act, w_down, h1, g, target, *, name, tm=512):
    S, F = act.shape
    D = h1.shape[1]

    def body(a_ref, w_ref, h_ref, g_ref, t_ref, dh_ref, dhb_ref, dg_ref, loss_ref):
        first = pl.program_id(0) == 0
        h2 = h_ref[...] + _nn(a_ref[...], w_ref[...])
        gv = g_ref[...]
        r = _rstd(h2)
        xh = h2 * r
        err = xh * gv - t_ref[...]
        part_loss = 0.5 * jnp.sum(jnp.mean(err * err, axis=-1, keepdims=True), axis=0, keepdims=True)
        dy = err * (1.0 / D)
        dxh = dy * gv
        dh = r * (dxh - xh * jnp.mean(dxh * xh, axis=-1, keepdims=True))
        dh_ref[...] = dh
        dhb_ref[...] = dh.astype(BF16)
        _accumulate(dg_ref, jnp.sum(dy * xh, axis=0, keepdims=True), first)
        _accumulate(loss_ref, jnp.broadcast_to(part_loss, (1, LANES)), first)

    row = lambda w: pl.BlockSpec((tm, w), lambda i: (i, 0))
    vec = lambda w: pl.BlockSpec((1, w), lambda i: (0, 0))
    return pl.pallas_call(
        body, name=name, grid=(S // tm,), in_specs=[row(F), _resident(w_down.shape), row(D), vec(D), row(D)],
        out_specs=[row(D), row(D), vec(D), vec(LANES)],
        out_shape=[jax.ShapeDtypeStruct((S, D), F32), jax.ShapeDtypeStruct((S, D), BF16),
                   jax.ShapeDtypeStruct((1, D), F32), jax.ShapeDtypeStruct((1, LANES), F32)],
        compiler_params=_cparams("arbitrary"),
    )(act, w_down, h1, g, target)


def _grad_norm_input(pieces, ws, x, g, add, *, name, tm=512):
    S, D = x.shape
    widths = [p.shape[1] for p in pieces]
    n, nw = len(pieces), len(ws)
    where, wi, off = [], 0, 0
    for wd in widths:
        if off == ws[wi].shape[1]:
            wi, off = wi + 1, 0
        where.append((wi, off))
        off += wd

    def body(*refs):
        p_refs, w_refs = refs[:n], refs[n:n + nw]
        x_ref, g_ref, add_ref, dx_ref, dxb_ref, dg_ref = refs[n + nw:]
        halves = _row_halves(tm)
        du = []
        for rows in halves:
            terms = [_nt(p_refs[k][rows, :], w_refs[wi][:, off:off + widths[k]]) for k, (wi, off) in enumerate(where)]
            du.append(sum(terms[1:], terms[0]))
        dg_sum = None
        for rows, duh in zip(halves, du):
            dx, dg = _norm_bwd(x_ref[rows, :], g_ref[...], duh)
            dx = add_ref[rows, :] + dx
            dx_ref[rows, :] = dx
            dxb_ref[rows, :] = dx.astype(BF16)
            part = jnp.sum(dg, axis=0, keepdims=True)
            dg_sum = part if dg_sum is None else dg_sum + part
        _accumulate(dg_ref, dg_sum, pl.program_id(0) == 0)

    row = lambda w_: pl.BlockSpec((tm, w_), lambda i: (i, 0))
    vec = pl.BlockSpec((1, D), lambda i: (0, 0))
    return pl.pallas_call(
        body, name=name, grid=(S // tm,),
        in_specs=[row(wd) for wd in widths] + [_resident(w.shape) for w in ws] + [row(D), vec, row(D)],
        out_specs=[row(D), row(D), vec],
        out_shape=[jax.ShapeDtypeStruct((S, D), F32), jax.ShapeDtypeStruct((S, D), BF16),
                   jax.ShapeDtypeStruct((1, D), F32)],
        compiler_params=_cparams("arbitrary"),
    )(*pieces, *ws, x, g, add)


def _rows(a):
    return a.reshape(-1, a.shape[-1])


def _row_tile(rows, cols, itemsize=4, budget=1 << 20):
    t = rows
    while t % 32 == 0 and t * cols * itemsize > budget:
        t //= 2
    return t


def _sum_cast(arrs, out_dtype, *, name):
    shape = arrs[0].shape
    flat = [_rows(a) for a in arrs]
    R, C = flat[0].shape
    tr = _row_tile(R, C)

    def body(*refs):
        acc = refs[0][...].astype(F32)
        for r in refs[1:-1]:
            acc = acc + r[...].astype(F32)
        refs[-1][...] = acc.astype(out_dtype)

    spec = pl.BlockSpec((tr, C), lambda i: (i, 0))
    return pl.pallas_call(
        body, name=name, grid=(R // tr,), in_specs=[spec] * len(flat), out_specs=spec,
        out_shape=jax.ShapeDtypeStruct((R, C), out_dtype), compiler_params=_cparams("parallel"),
    )(*flat).reshape(shape)


def _adamw(parts, w, m, v, *, name):
    shape = w.shape
    w2, m2, v2 = _rows(w), _rows(m), _rows(v)
    R, C = w2.shape
    parts = [p.reshape(-1, R, C) for p in parts]
    tr = _row_tile(R, C)
    np_ = len(parts)
    c1, c2 = 1.0 - ADAM_B1 ** ADAM_STEP, 1.0 - ADAM_B2 ** ADAM_STEP

    def body(*refs):
        terms = [(r, k) for r in refs[:np_] for k in range(r.shape[0])]
        g = terms[0][0][terms[0][1]].astype(F32)
        for r, k in terms[1:]:
            g = g + r[k].astype(F32)
        w_ref, m_ref, v_ref, g_out, d_out, m_out, v_out = refs[np_:]
        mn = ADAM_B1 * m_ref[...] + (1.0 - ADAM_B1) * g
        vn = ADAM_B2 * v_ref[...] + (1.0 - ADAM_B2) * (g * g)
        g_out[...] = g
        d_out[...] = -ADAM_LR * ((mn / c1) / (jnp.sqrt(vn / c2) + ADAM_EPS) + ADAM_WD * w_ref[...])
        m_out[...] = mn
        v_out[...] = vn

    spec = pl.BlockSpec((tr, C), lambda i: (i, 0))
    out = jax.ShapeDtypeStruct((R, C), F32)
    stacks = [pl.BlockSpec((p.shape[0], tr, C), lambda i: (0, i, 0)) for p in parts]
    res = pl.pallas_call(
        body, name=name, grid=(R // tr,), in_specs=stacks + [spec] * 3, out_specs=[spec] * 4,
        out_shape=[out] * 4, compiler_params=_cparams("parallel"),
    )(*parts, w2, m2, v2)
    return [r.reshape(shape) for r in res]


def _coords():
    return lax.axis_index("x"), lax.axis_index("y"), lax.axis_index("c")


def _all_gather(shards, *, name):
    n = len(shards)

    def body(*refs):
        x_refs, out_refs = refs[:n], refs[n:2 * n]
        send_sems, recv_sems, local_sems = refs[2 * n:]
        x, y, c = _coords()
        me, sibling = (x, y, c), (x, y, 1 - c)
        chips = [(1 - x, y), (x, 1 - y), (1 - x, 1 - y)]

        def slot(a, dev):
            return out_refs[a].at[4 * dev[0] + 2 * dev[1] + dev[2]]

        def copy(a, k, block, to, src=None):
            return pltpu.make_async_remote_copy(
                src_ref=slot(a, block) if src is None else src, dst_ref=slot(a, block),
                send_sem=send_sems.at[7 * a + k], recv_sem=recv_sems.at[7 * a + k], device_id=to, device_id_type=MESH)

        mine = [pltpu.make_async_copy(x_refs[a], slot(a, me), local_sems.at[a]) for a in range(n)]
        for cp in mine:
            cp.start()
        first = []
        for a in range(n):
            first.append(copy(a, 0, me, sibling, src=x_refs[a]))
            first += [copy(a, 1 + j, me, (*chip, c), src=x_refs[a]) for j, chip in enumerate(chips)]
        for cp in first:
            cp.start()
        passed = []
        for j, chip in enumerate(chips):
            for a in range(n):
                copy(a, 1 + j, (*chip, c), me).wait_recv()
                fwd = copy(a, 4 + j, (*chip, c), sibling)
                fwd.start()
                passed.append(fwd)
        for a in range(n):
            copy(a, 0, sibling, me).wait_recv()
            for j, chip in enumerate(chips):
                copy(a, 4 + j, (*chip, 1 - c), me).wait_recv()
        for cp in first + passed:
            cp.wait_send()
        for cp in mine:
            cp.wait()

    return pl.pallas_call(
        body, name=name, in_specs=[HBM] * n, out_specs=[HBM] * n,
        out_shape=[jax.ShapeDtypeStruct((N_DEV, *s.shape), s.dtype) for s in shards],
        scratch_shapes=[pltpu.SemaphoreType.DMA((7 * n,)), pltpu.SemaphoreType.DMA((7 * n,)),
                        pltpu.SemaphoreType.DMA((n,))],
    )(*shards)


def _flip_y(x, y, c):
    return (x, 1 - y, c)


def _flip_x(x, y, c):
    return (1 - x, y, c)


def _flip_xy(x, y, c):
    return (1 - x, 1 - y, c)


SEM = pl.BlockSpec(memory_space=pltpu.SEMAPHORE)
SIDE_EFFECT = pltpu.SideEffectType.DATAFLOW_SIDE_EFFECTING


def _in_hbm(a):
    return pltpu.with_memory_space_constraint(a, pltpu.HBM)


def _copies_start(srcs, lands, plan, n_copies, *, name):
    ns, nl = len(srcs), len(lands)

    def body(*refs):
        src_refs, land_refs = refs[:ns], refs[ns:ns + nl]
        send_sems, recv_sems = refs[ns + nl:ns + nl + 2]
        token = refs[-1]
        for k, (src, dst, peer, _) in enumerate(plan(src_refs, land_refs, *_coords())):
            pltpu.make_async_remote_copy(src_ref=src, dst_ref=dst, send_sem=send_sems.at[k], recv_sem=recv_sems.at[k],
                                         device_id=peer, device_id_type=MESH).start()
        token[...] = jnp.zeros_like(token)

    bufs = [*srcs, *lands]
    res = pl.pallas_call(
        body, name=name, in_specs=[HBM] * (ns + nl),
        out_specs=(SEM, SEM, *[HBM] * (ns + nl), pl.BlockSpec(memory_space=pltpu.VMEM)),
        out_shape=(pltpu.SemaphoreType.DMA((n_copies,)), pltpu.SemaphoreType.DMA((n_copies,)),
                   *[pltpu.HBM(b.shape, b.dtype) for b in bufs], jax.ShapeDtypeStruct((SUBLANES, LANES), F32)),
        input_output_aliases={i: 2 + i for i in range(ns + nl)},
        compiler_params=pltpu.CompilerParams(has_side_effects=SIDE_EFFECT),
    )(*[_in_hbm(b) for b in bufs])
    return res[0], res[1], list(res[2:2 + ns]), list(res[2 + ns:2 + ns + nl]), res[-1]


def _copies_wait(started, plan, after, *, name):
    send_sems, recv_sems, srcs, lands, _ = started
    ns, nl = len(srcs), len(lands)

    def body(*refs):
        src_refs, land_refs = refs[:ns], refs[ns:ns + nl]
        send_sems, recv_sems = refs[ns + nl:ns + nl + 2]
        for k, (src, dst, peer, here) in enumerate(plan(src_refs, land_refs, *_coords())):
            pltpu.make_async_remote_copy(src_ref=src, dst_ref=dst, send_sem=send_sems.at[k], recv_sem=recv_sems.at[k],
                                         device_id=peer, device_id_type=MESH).wait_send()
            pltpu.make_async_remote_copy(src_ref=src, dst_ref=here, send_sem=send_sems.at[k], recv_sem=recv_sems.at[k],
                                         device_id=peer, device_id_type=MESH).wait_recv()

    bufs = [*srcs, *lands]
    res = pl.pallas_call(
        body, name=name, in_specs=[HBM] * (ns + nl) + [SEM, SEM, pl.BlockSpec(memory_space=pl.ANY)],
        out_specs=[HBM] * (ns + nl), out_shape=[pltpu.HBM(b.shape, b.dtype) for b in bufs],
        input_output_aliases={i: i for i in range(ns + nl)},
        compiler_params=pltpu.CompilerParams(has_side_effects=SIDE_EFFECT),
    )(*bufs, send_sems, recv_sems, after)
    return list(res[ns:])


def _dev_index(dev):
    return 4 * dev[0] + 2 * dev[1] + dev[2]


def _ag_chips_plan(src_refs, land_refs, x, y, c):
    me = _dev_index((x, y, c))
    return [(src, land.at[me], peer, land.at[_dev_index(peer)])
            for src, land in zip(src_refs, land_refs) for peer in (_flip_y(x, y, c), _flip_x(x, y, c), _flip_xy(x, y, c))]


def _ag_sibling_plan(src_refs, land_refs, x, y, c):
    chips = [(x, y), (x, 1 - y), (1 - x, y), (1 - x, 1 - y)]
    return [(land.at[_dev_index((*chip, c))], land.at[_dev_index((*chip, c))], (x, y, 1 - c),
             land.at[_dev_index((*chip, 1 - c))]) for land in land_refs for chip in chips]


def _rs_direct_plan(src_refs, land_refs, x, y, c):
    plan = []
    for src, land in zip(src_refs, land_refs):
        for m in range(1, N_DEV):
            peer = (x + (m >> 2) * (1 - 2 * x), y + ((m >> 1) & 1) * (1 - 2 * y), c + (m & 1) * (1 - 2 * c))
            plan.append((src.at[_dev_index(peer)], land.at[m - 1], peer, land.at[m - 1]))
    return plan


def _rs_start(grads, me, *, name):
    own = [lax.dynamic_index_in_dim(g, me, 0, keepdims=False) for g in grads]
    lands = [lax.empty((N_DEV - 1, *g.shape[1:]), g.dtype) for g in grads]
    return _copies_start(grads, lands, _rs_direct_plan, (N_DEV - 1) * len(grads), name=name), own


def _rs_finish(started, after, *, name):
    handle, own = started
    got = _copies_wait(handle, _rs_direct_plan, after, name=name)
    return [[o, land] for o, land in zip(own, got)]


def _by_device_cols(w):
    K, N = w.shape
    return w.reshape(K, N_DEV, N // N_DEV).transpose(1, 0, 2)


def _gathered_cols(w8):
    return w8.transpose(1, 0, 2).reshape(w8.shape[1], -1)


def _pair_major(w, inverse=False):
    K = w.shape[0]
    a, b = (ATTN_W // LANES, 3) if inverse else (3, ATTN_W // LANES)
    return w.reshape(K, a, b, LANES).transpose(0, 2, 1, 3).reshape(K, 3 * ATTN_W)


def kernel(x, norm1_g, w_in, attn_norm_g, hgrn_norm_g, hgrn_lb_logits, w_out, norm2_g, w_up, conv_w, conv_b, w_down, final_norm_g, loss_target, m_norm1_g, m_w_in, m_attn_norm_g, m_hgrn_norm_g, m_hgrn_lb_logits, m_w_out, m_norm2_g, m_w_up, m_conv_w, m_conv_b, m_w_down, m_final_norm_g, v_norm1_g, v_w_in, v_attn_norm_g, v_hgrn_norm_g, v_hgrn_lb_logits, v_w_out, v_norm2_g, v_w_up, v_conv_w, v_conv_b, v_w_down, v_final_norm_g):
    xs, target = x[0], loss_target[0]
    S, D = xs.shape
    NA = 3 * ATTN_W
    fng = final_norm_g.reshape(1, D)

    casts = [_sum_cast([w[0]], BF16, name=f"cast_{nm}") for nm, w in
             (("w_in", w_in), ("w_out", w_out), ("w_up", w_up), ("w_down", w_down))]
    me = _dev_index(_coords())
    (g_in,) = _all_gather(casts[:1], name="ag_w_in")
    later = casts[1:] + [conv_w[0]]
    ag1 = _copies_start(later, [lax.empty((N_DEV, *s.shape), s.dtype) for s in later], _ag_chips_plan,
                        3 * len(later), name="ag_chips_start")
    wi = _gathered_cols(g_in)
    wi_a, wi_h = _pair_major(wi[:, :NA]), wi[:, NA:]

    u1, proj_a = _proj_attn(xs, norm1_g + ag1[4][0, 0], wi_a, name="proj_attn")
    proj_h = _mm(u1, wi_h, name="proj_hgrn")
    attn, lse = _attn_fwd(proj_a, name="attn_fwd")
    lands = _copies_wait(ag1, _ag_chips_plan, attn, name="ag_chips_wait")
    lands = [lax.dynamic_update_index_in_dim(l, s, me, 0) for l, s in zip(lands, later)]
    ag2 = _copies_start([], lands, _ag_sibling_plan, 4 * len(later), name="ag_sibling_start")
    rec, states = _hgrn_fwd(proj_h, hgrn_lb_logits + ag2[4][0, 0], name="hgrn_fwd")
    g_out, g_up, g_down, g_cw = _copies_wait(ag2, _ag_sibling_plan, rec, name="ag_sibling_wait")
    wo = g_out.reshape(-1, D)
    wu = _gathered_cols(g_up)
    wd = g_down.reshape(-1, D)
    cw = _gathered_cols(g_cw)
    h1, u2, mixed = _out_proj(attn, rec, proj_h, xs, attn_norm_g, hgrn_norm_g, norm2_g, wo, name="out_proj")
    gate, val, act = _up_glu(u2, wu, cw, conv_b, name="up_glu")
    dh2, dh2b, d_fng, loss_part = _down_loss(act, wd, h1, fng, target, name="down_loss")

    dgate, dval, d_cw, d_cb = _dact_glu_bwd(dh2b, wd, gate, val, cw, conv_b, name="dact_glu_bwd")
    dw_down = _mm_tn(act, dh2b, tm=256, name="dw_down")
    dh1, dh1b, d_n2g = _grad_norm_input([dgate, dval], [wu], h1, norm2_g, dh2, name="du2_norm2_bwd")
    dw_up = [_mm_tn(u2, dy, tm=1024, tn=256, name=f"dw_up_{nm}") for nm, dy in (("gate", dgate), ("val", dval))]
    rs_ffn = _rs_start([dw_down.reshape(N_DEV, -1, D),
                        jnp.concatenate([h.reshape(D, N_DEV // 2, -1).transpose(1, 0, 2) for h in dw_up], axis=0)],
                       me, name="rs_ffn_start")
    dattn, delta, drec, dhg, d_ang, d_hng = _dmix_post_bwd(dh1b, wo, attn, rec, proj_h, attn_norm_g + rs_ffn[0][4][0, 0],
                                                          hgrn_norm_g, name="dmix_post_bwd")
    dw_out = _mm_tn(mixed, dh1b, name="dw_out")
    rs_out = _rs_start([dw_out.reshape(N_DEV, -1, D)], me, name="rs_out_start")
    dproj_h, d_lbl = _hgrn_bwd(proj_h, hgrn_lb_logits + rs_out[0][4][0, 0], states, drec, name="hgrn_bwd")
    dproj_a = _attn_bwd(proj_a, dattn, lse, delta, name="attn_bwd")
    dw_in = jnp.concatenate([_pair_major(_mm_tn(u1, dproj_a, tm=1024, tn=512, name="dw_in_attn"), inverse=True),
                             _mm_tn(u1, dproj_h, tm=1024, tn=512, name="dw_in_hgrn"),
                             _mm_tn(u1, dhg, tm=1024, tn=512, name="dw_in_gate")], axis=1)
    rs_in = _rs_start([_by_device_cols(dw_in)], me, name="rs_in_start")
    grad_x, _, d_n1g = _grad_norm_input([dproj_a, dproj_h, dhg], [wi_a, wi_h], xs, norm1_g + rs_in[0][4][0, 0], dh1,
                                        name="du1_norm1_bwd")

    res = {}

    def update(nm, parts, w, m, v):
        res[nm] = _adamw(parts, w, m, v, name=f"adamw_{nm}")

    g_down, g_up = _rs_finish(rs_ffn, grad_x, name="rs_ffn_wait")
    update("w_down", g_down, w_down, m_w_down, v_w_down)
    update("w_up", g_up, w_up, m_w_up, v_w_up)
    (g_out,) = _rs_finish(rs_out, grad_x, name="rs_out_wait")
    update("w_out", g_out, w_out, m_w_out, v_w_out)
    (g_in,) = _rs_finish(rs_in, res["w_up"][1], name="rs_in_wait")
    update("w_in", g_in, w_in, m_w_in, v_w_in)

    no_state = jnp.zeros_like(loss_part)
    small = [("loss", loss_part, no_state, no_state, no_state),
             ("norm1_g", d_n1g, norm1_g, m_norm1_g, v_norm1_g),
             ("attn_norm_g", d_ang, attn_norm_g, m_attn_norm_g, v_attn_norm_g),
             ("hgrn_norm_g", d_hng, hgrn_norm_g, m_hgrn_norm_g, v_hgrn_norm_g),
             ("hgrn_lb_logits", d_lbl, hgrn_lb_logits, m_hgrn_lb_logits, v_hgrn_lb_logits),
             ("norm2_g", d_n2g, norm2_g, m_norm2_g, v_norm2_g),
             ("conv_b", d_cb, conv_b, m_conv_b, v_conv_b),
             ("final_norm_g", d_fng, final_norm_g, m_final_norm_g, v_final_norm_g)]
    pack = lambda arrs: jnp.concatenate([a.reshape(1, -1) for a in arrs], axis=1)
    g_small, g_dcw = _all_gather([pack([s[1] for s in small]), d_cw], name="ag_small_grads")
    sm = _adamw([g_small], pack([s[2] for s in small]), pack([s[3] for s in small]),
                pack([s[4] for s in small]), name="adamw_small")
    off = 0
    for nm, _, w, _, _ in small:
        res[nm] = [r[:, off:off + w.size].reshape(w.shape) for r in sm]
        off += w.size
    ncw = conv_w.shape[-1]
    mine_cw = lax.dynamic_slice_in_dim(g_dcw, me * ncw, ncw, axis=2)
    res["conv_w"] = _adamw([mine_cw], conv_w, m_conv_w, v_conv_w, name="adamw_conv_w")

    loss = res["loss"][0][0, 0]
    order = ["norm1_g", "w_in", "attn_norm_g", "hgrn_norm_g", "hgrn_lb_logits", "w_out", "norm2_g", "w_up",
             "conv_w", "conv_b", "w_down", "final_norm_g"]
    return (loss, grad_x[None], *[res[nm][0] for nm in order], *[res[nm][1] for nm in order],
            *[res[nm][2] for nm in order], *[res[nm][3] for nm in order])
```

```python
import jax
import jax.numpy as jnp
from jax import lax
from jax.experimental import pallas as pl
from jax.experimental.pallas import tpu as pltpu

F32, BF16 = jnp.float32, jnp.bfloat16
NORM_EPS = 1e-6
ATTN_HEADS, HEAD_DIM, ATTN_BLOCK = 8, 64, 128
DILATIONS = (1, 4, 16)
ATTN_SCALE = HEAD_DIM ** -0.5
ATTN_W = ATTN_HEADS * HEAD_DIM
HGRN_HEADS, HGRN_DIM, HGRN_CHUNK = 4, 128, 64
HGRN_W = HGRN_HEADS * HGRN_DIM
ADAM_LR, ADAM_B1, ADAM_B2, ADAM_EPS, ADAM_WD, ADAM_STEP = 0.001, 0.9, 0.999, 1e-08, 0.01, 10
LANES, SUBLANES = 128, 8
VMEM_LIMIT_BYTES = 56 * 1024 * 1024
N_DEV = 8
MESH = pl.DeviceIdType.MESH
HBM = pl.BlockSpec(memory_space=pltpu.HBM)
HIGHEST = lax.Precision.HIGHEST


def _cparams(*sem):
    return pltpu.CompilerParams(dimension_semantics=sem, vmem_limit_bytes=VMEM_LIMIT_BYTES)


def _tile(n, pref):
    if n <= pref:
        return n
    t = (pref // LANES) * LANES
    while n % t:
        t -= LANES
    return t


def _resident(shape):
    return pl.BlockSpec(shape, lambda *_: (0,) * len(shape), pipeline_mode=pl.Buffered(1))


def _dot(a, b, dims, precision=None):
    return lax.dot_general(a, b, (dims, ((), ())), precision=precision, preferred_element_type=F32)


def _nn(a, b, precision=None):
    return _dot(a, b, ((1,), (0,)), precision)


def _nt(a, b):
    return _dot(a, b, ((1,), (1,)))


def _tn(a, b):
    return _dot(a, b, ((0,), (0,)))


def _sigmoid(x):
    return 1.0 / (1.0 + jnp.exp(-x))


def _rstd(x):
    return lax.rsqrt(jnp.mean(x * x, axis=-1, keepdims=True) + NORM_EPS)


def _norm_bwd(x, g, du):
    r = _rstd(x)
    xh = x * r
    dxh = du * g
    return r * (dxh - xh * jnp.mean(dxh * xh, axis=-1, keepdims=True)), du * xh


def _row_halves(tm):
    return [pl.ds(0, tm // 2), pl.ds(tm // 2, tm // 2)]


def _accumulate(ref, part, first):
    @pl.when(first)
    def _():
        ref[...] = part

    @pl.when(jnp.logical_not(first))
    def _():
        ref[...] += part


def _mm(a, b, *, name, out_dtype=F32, tm=1024, tn=512):
    M, K = a.shape
    N = b.shape[1]
    tm, tn = _tile(M, tm), _tile(N, tn)

    def body(a_ref, b_ref, o_ref):
        o_ref[...] = _nn(a_ref[...], b_ref[...]).astype(out_dtype)

    return pl.pallas_call(
        body, name=name, grid=(M // tm, N // tn),
        in_specs=[pl.BlockSpec((tm, K), lambda i, j: (i, 0)), pl.BlockSpec((K, tn), lambda i, j: (0, j))],
        out_specs=pl.BlockSpec((tm, tn), lambda i, j: (i, j)), out_shape=jax.ShapeDtypeStruct((M, N), out_dtype),
        compiler_params=_cparams("parallel", "parallel"),
    )(a, b)


def _mm_tn(x, dy, *, name, tm=512, tn=1024):
    S, M = x.shape
    N = dy.shape[1]
    tm, tn = _tile(M, tm), _tile(N, tn)

    def body(x_ref, dy_ref, o_ref, xt_ref):
        @pl.when(pl.program_id(1) == 0)
        def _():
            xt_ref[...] = x_ref[...].T

        o_ref[...] = _nn(xt_ref[...], dy_ref[...]).astype(BF16)

    return pl.pallas_call(
        body, name=name, grid=(M // tm, N // tn),
        in_specs=[pl.BlockSpec((S, tm), lambda i, j: (0, i)), pl.BlockSpec((S, tn), lambda i, j: (0, j))],
        out_specs=pl.BlockSpec((tm, tn), lambda i, j: (i, j)), out_shape=jax.ShapeDtypeStruct((M, N), BF16),
        scratch_shapes=[pltpu.VMEM((tm, S), BF16)], compiler_params=_cparams("parallel", "arbitrary"),
    )(x, dy)


def _proj_attn(x, g, w, *, name, tm=1024, tn=512):
    S, D = x.shape
    N = w.shape[1]

    def body(x_ref, g_ref, w_ref, u_ref, o_ref):
        @pl.when(pl.program_id(1) == 0)
        def _():
            xv = x_ref[...]
            u_ref[...] = (xv * _rstd(xv) * g_ref[...]).astype(BF16)

        o_ref[...] = _nn(u_ref[...], w_ref[...]).astype(BF16)

    return pl.pallas_call(
        body, name=name, grid=(S // tm, N // tn),
        in_specs=[pl.BlockSpec((tm, D), lambda i, j: (i, 0)), pl.BlockSpec((1, D), lambda i, j: (0, 0)),
                  pl.BlockSpec((D, tn), lambda i, j: (0, j))],
        out_specs=[pl.BlockSpec((tm, D), lambda i, j: (i, 0)), pl.BlockSpec((tm, tn), lambda i, j: (i, j))],
        out_shape=[jax.ShapeDtypeStruct((S, D), BF16), jax.ShapeDtypeStruct((S, N), BF16)],
        compiler_params=_cparams("parallel", "arbitrary"),
    )(x, g, w)


PAIR_W = 3 * LANES
ATTN_UNROLL_FWD, ATTN_UNROLL_BWD = 4, 4


def _attn_masks(first):
    qi = lax.broadcasted_iota(jnp.int32, (ATTN_BLOCK, 2 * ATTN_BLOCK), 0)
    kj = lax.broadcasted_iota(jnp.int32, (ATTN_BLOCK, 2 * ATTN_BLOCK), 1)
    dist = qi + ATTN_BLOCK - kj
    valid = (dist >= 0) & (dist <= ATTN_BLOCK) & jnp.logical_or(kj >= ATTN_BLOCK, jnp.logical_not(first))
    lane = lax.broadcasted_iota(jnp.int32, (1, LANES), 1)
    return valid, lane


def _for_residue_blocks(S, d, fn):
    span = ATTN_BLOCK * d
    nb = S // span

    def step(n, carry):
        base = pl.multiple_of(n * span, span)
        for r in range(d):
            off = pl.multiple_of((r * nb + n) * ATTN_BLOCK, ATTN_BLOCK)
            fn(lambda ref, r=r: _block_rows(ref, base, r, d),
               lambda ref, val, r=r: _set_block_rows(ref, base, r, d, val), off)
        return carry

    lax.fori_loop(0, nb, step, 0)


def _for_blocks(S, unroll, fn):
    def step(i, carry):
        fn([(pl.multiple_of((i * unroll + u) * ATTN_BLOCK, ATTN_BLOCK), i * unroll + u) for u in range(unroll)])
        return carry

    lax.fori_loop(0, S // ATTN_BLOCK // unroll, step, 0)


def _head_value(x2, lane, e):
    return jnp.sum(jnp.where(lane == HEAD_DIM * e, x2, 0.0), axis=-1, keepdims=True)


def _block_rows(ref, base, r, d):
    if d == 1:
        return ref[pl.ds(base, ATTN_BLOCK), :]
    return ref.at[pl.ds(base, ATTN_BLOCK * d)][pl.ds(r, ATTN_BLOCK, stride=d), :]


def _set_block_rows(ref, base, r, d, val):
    if d == 1:
        ref[pl.ds(base, ATTN_BLOCK), :] = val
    else:
        ref.at[pl.ds(base, ATTN_BLOCK * d)][pl.ds(r, ATTN_BLOCK, stride=d), :] = val


def _split_pair(p_ref, qs, ks, vs, bk, bv):
    qs[...] = p_ref[:, 0:LANES].astype(F32)
    ks[...] = p_ref[:, LANES:2 * LANES].astype(F32)
    vs[...] = p_ref[:, 2 * LANES:3 * LANES].astype(F32)
    bk[0:ATTN_BLOCK, :] = jnp.zeros((ATTN_BLOCK, LANES), bk.dtype)
    bv[0:ATTN_BLOCK, :] = jnp.zeros((ATTN_BLOCK, LANES), bv.dtype)


def _regroup_qkv(rows, off, qs, ks, vs, bq, bk, bv):
    blk, shifted = pl.ds(off, ATTN_BLOCK), pl.ds(off + ATTN_BLOCK, ATTN_BLOCK)
    bq[blk, :] = rows(qs).astype(BF16)
    bk[shifted, :] = rows(ks).astype(BF16)
    bv[shifted, :] = rows(vs).astype(BF16)


def _attn_fwd(proj_a, *, name):
    S = proj_a.shape[0]

    def body(p_ref, o_ref, l_ref, qs, ks, vs, bq, bk, bv, bo, bl):
        _split_pair(p_ref, qs, ks, vs, bk, bv)
        for d in DILATIONS:
            nb = S // (ATTN_BLOCK * d)
            _for_residue_blocks(S, d, lambda rows, _, off: _regroup_qkv(rows, off, qs, ks, vs, bq, bk, bv))

            def blocks(group, nb=nb):
                lane = lax.broadcasted_iota(jnp.int32, (1, LANES), 1)
                heads = [(lane >= HEAD_DIM * e) & (lane < HEAD_DIM * (e + 1)) for e in range(LANES // HEAD_DIM)]
                wins = [pl.ds(off, 2 * ATTN_BLOCK) for off, _ in group]
                s = [[_nt(jnp.where(mh, bq[pl.ds(off, ATTN_BLOCK), :], jnp.zeros((ATTN_BLOCK, LANES), BF16)), bk[win, :])
                      for mh in heads] for (off, _), win in zip(group, wins)]
                p, m, l = [], [], []
                for (off, b), su in zip(group, s):
                    valid, _ = _attn_masks(jnp.bitwise_and(b, nb - 1) == 0)
                    sm = [jnp.where(valid, x * ATTN_SCALE, -jnp.inf) for x in su]
                    m.append([jnp.max(x, axis=-1, keepdims=True) for x in sm])
                    p.append([jnp.exp(x - mx) for x, mx in zip(sm, m[-1])])
                    l.append([jnp.sum(x, axis=-1, keepdims=True) for x in p[-1]])
                o = [[_nn(x.astype(BF16), bv[win, :]) for x in pu] for pu, win in zip(p, wins)]
                for (off, _), ou, mu, lu in zip(group, o, m, l):
                    o2 = jnp.zeros((ATTN_BLOCK, LANES), F32)
                    l2 = jnp.zeros((ATTN_BLOCK, LANES), F32)
                    for mh, oe, me_, le in zip(heads, ou, mu, lu):
                        o2 = jnp.where(mh, oe / le, o2)
                        l2 = jnp.where(mh, me_ + jnp.log(le), l2)
                    bo[pl.ds(off, ATTN_BLOCK), :] = o2
                    bl[pl.ds(off, ATTN_BLOCK), :] = l2

            _for_blocks(S, ATTN_UNROLL_FWD, blocks)

            def merge(rows, set_rows, off, d=d):
                blk = pl.ds(off, ATTN_BLOCK)
                o2, l2 = bo[blk, :], bl[blk, :]
                if d != DILATIONS[0]:
                    lo, oo = rows(l_ref), rows(o_ref)
                    ln = jnp.maximum(lo, l2)
                    wa, wb = jnp.exp(lo - ln), jnp.exp(l2 - ln)
                    o2 = (wa * oo + wb * o2) / (wa + wb)
                    l2 = ln + jnp.log(wa + wb)
                set_rows(o_ref, o2)
                set_rows(l_ref, l2)

            _for_residue_blocks(S, d, merge)

    slab = pl.BlockSpec((S, LANES), lambda p: (0, p))
    f32_slab, bf16_slab = pltpu.VMEM((S, LANES), F32), pltpu.VMEM((S, LANES), BF16)
    bf16_window = pltpu.VMEM((S + ATTN_BLOCK, LANES), BF16)
    return pl.pallas_call(
        body, name=name, grid=(ATTN_W // LANES,), in_specs=[pl.BlockSpec((S, PAIR_W), lambda p: (0, p))],
        out_specs=[slab, slab],
        out_shape=[jax.ShapeDtypeStruct((S, ATTN_W), F32), jax.ShapeDtypeStruct((S, ATTN_W), F32)],
        scratch_shapes=[f32_slab] * 3 + [bf16_slab, bf16_window, bf16_window, f32_slab, f32_slab],
        compiler_params=_cparams("parallel"),
    )(proj_a)


def _attn_bwd(proj_a, do, lse, delta, *, name):
    S = proj_a.shape[0]

    def body(p_ref, do_ref, lse_ref, dl_ref, o_ref, qs, ks, vs, dqs, dks, dvs, bq, bk, bv, bdo, blse, bdl, bdq, bdk, bdv):
        _split_pair(p_ref, qs, ks, vs, bk, bv)
        bdk[0:ATTN_BLOCK, :] = jnp.zeros((ATTN_BLOCK, LANES), F32)
        bdv[0:ATTN_BLOCK, :] = jnp.zeros((ATTN_BLOCK, LANES), F32)
        for d in DILATIONS:
            nb = S // (ATTN_BLOCK * d)

            def regroup(rows, _, off):
                _regroup_qkv(rows, off, qs, ks, vs, bq, bk, bv)
                blk = pl.ds(off, ATTN_BLOCK)
                bdo[blk, :] = rows(do_ref).astype(BF16)
                blse[blk, :] = rows(lse_ref)
                bdl[blk, :] = rows(dl_ref)

            _for_residue_blocks(S, d, regroup)

            def blocks(group, nb=nb):
                lane = lax.broadcasted_iota(jnp.int32, (1, LANES), 1)
                heads = [(lane >= HEAD_DIM * e) & (lane < HEAD_DIM * (e + 1)) for e in range(LANES // HEAD_DIM)]
                zero = jnp.zeros((ATTN_BLOCK, LANES), BF16)
                chains = [(off, b, e, mh) for off, b in group for e, mh in enumerate(heads)]
                qm = [jnp.where(mh, bq[pl.ds(off, ATTN_BLOCK), :], zero) for off, _, _, mh in chains]
                dom = [jnp.where(mh, bdo[pl.ds(off, ATTN_BLOCK), :], zero) for off, _, _, mh in chains]
                s = [_nt(x, bk[pl.ds(off, 2 * ATTN_BLOCK), :]) for x, (off, _, _, _) in zip(qm, chains)]
                dp = [_nt(x, bv[pl.ds(off, 2 * ATTN_BLOCK), :]) for x, (off, _, _, _) in zip(dom, chains)]
                p, ds = [], []
                for (off, b, e, _), sc, dpc in zip(chains, s, dp):
                    valid, _ = _attn_masks(jnp.bitwise_and(b, nb - 1) == 0)
                    blk = pl.ds(off, ATTN_BLOCK)
                    pc = jnp.where(valid, jnp.exp(sc * ATTN_SCALE - _head_value(blse[blk, :], lane, e)), 0.0)
                    ds.append((pc * (dpc - _head_value(bdl[blk, :], lane, e)) * ATTN_SCALE).astype(BF16))
                    p.append(pc.astype(BF16))
                dq = [_nn(x, bk[pl.ds(off, 2 * ATTN_BLOCK), :]) for x, (off, _, _, _) in zip(ds, chains)]
                dk = [_tn(x, y) for x, y in zip(ds, qm)]
                dv = [_tn(x, y) for x, y in zip(p, dom)]
                nh = len(heads)
                for u, (off, _) in enumerate(group):
                    dq2 = jnp.zeros((ATTN_BLOCK, LANES), F32)
                    for mh, x in zip(heads, dq[nh * u:nh * (u + 1)]):
                        dq2 = jnp.where(mh, x, dq2)
                    bdq[pl.ds(off, ATTN_BLOCK), :] = dq2
                    for acc, grads in ((bdk, dk), (bdv, dv)):
                        win_grad = sum(grads[nh * u + 1:nh * (u + 1)], grads[nh * u])
                        acc[pl.ds(off, ATTN_BLOCK), :] += win_grad[:ATTN_BLOCK]
                        acc[pl.ds(off + ATTN_BLOCK, ATTN_BLOCK), :] = win_grad[ATTN_BLOCK:]

            _for_blocks(S, ATTN_UNROLL_BWD, blocks)

            def scatter(rows, set_rows, off, d=d):
                blk, shifted = pl.ds(off, ATTN_BLOCK), pl.ds(off + ATTN_BLOCK, ATTN_BLOCK)
                for acc, part in ((dqs, bdq[blk, :]), (dks, bdk[shifted, :]), (dvs, bdv[shifted, :])):
                    set_rows(acc, part if d == DILATIONS[0] else rows(acc) + part)

            _for_residue_blocks(S, d, scatter)
        o_ref[:, 0:LANES] = dqs[...].astype(BF16)
        o_ref[:, LANES:2 * LANES] = dks[...].astype(BF16)
        o_ref[:, 2 * LANES:3 * LANES] = dvs[...].astype(BF16)

    slab = pl.BlockSpec((S, LANES), lambda p: (0, p), pipeline_mode=pl.Buffered(1))
    pair = pl.BlockSpec((S, PAIR_W), lambda p: (0, p))
    f32_slab, bf16_slab = pltpu.VMEM((S, LANES), F32), pltpu.VMEM((S, LANES), BF16)
    f32_window, bf16_window = pltpu.VMEM((S + ATTN_BLOCK, LANES), F32), pltpu.VMEM((S + ATTN_BLOCK, LANES), BF16)
    return pl.pallas_call(
        body, name=name, grid=(ATTN_W // LANES,), in_specs=[pair, slab, slab, slab], out_specs=pair,
        out_shape=jax.ShapeDtypeStruct(proj_a.shape, BF16),
        scratch_shapes=[f32_slab] * 6 + [bf16_slab, bf16_window, bf16_window, bf16_slab, f32_slab, f32_slab,
                                         f32_slab, f32_window, f32_window],
        compiler_params=_cparams("parallel"),
    )(proj_a, do, lse, delta)


HG_T = 2 * HGRN_CHUNK


def _hgrn_consts():
    row = lax.broadcasted_iota(jnp.int32, (HG_T, HG_T), 0)
    col = lax.broadcasted_iota(jnp.int32, (HG_T, HG_T), 1)
    same = (row >= HGRN_CHUNK) == (col >= HGRN_CHUNK)
    return row, same & (col <= row), same & (col >= row)


def _lower_bound(logits_ref):
    l0, l1 = logits_ref[0:1, :], logits_ref[1:2, :]
    mx = jnp.maximum(l0, l1)
    e0, e1 = jnp.exp(l0 - mx), jnp.exp(l1 - mx)
    return e0 / (e0 + e1)


def _hgrn_gates(qs, fs, lbs, row, causal):
    C = HGRN_CHUNK
    tri = jnp.where(causal, 1.0, 0.0).astype(F32)
    sgs = [_sigmoid(f) for f in fs]
    forgets = [lb + (1.0 - lb) * sg for lb, sg in zip(lbs, sgs)]
    logfs = [jnp.log(forget) for forget in forgets]
    bs = [_nn(tri, logf, HIGHEST) for logf in logfs]
    out = []
    for q, sg, forget, logf, b in zip(qs, sgs, forgets, logfs, bs):
        key = 1.0 - forget
        bend0 = jnp.sum(logf[:C], axis=0, keepdims=True)
        bend1 = jnp.sum(logf[C:], axis=0, keepdims=True)
        bend = jnp.where(row < C, bend0, bend1)
        eb, emb, eend = jnp.exp(b), jnp.exp(-b), jnp.exp(bend - b)
        sq = _sigmoid(q)
        out.append(dict(sg=sg, forget=forget, key=key, bend0=bend0, bend1=bend1, eb=eb, emb=emb, eend=eend, sq=sq,
                        qd=q * sq * eb, ki=key * emb, ke=key * eend))
    return out


def _hgrn_fwd(proj, logits, *, name):
    S = proj.shape[0]
    W, C = HGRN_W, HGRN_CHUNK

    def body(q_ref, f_ref, i_ref, lg_ref, rec_ref, st_ref, s_ref):
        @pl.when(pl.program_id(0) == 0)
        def _():
            s_ref[...] = jnp.zeros_like(s_ref)

        row, causal, _ = _hgrn_consts()
        lb_all = _lower_bound(lg_ref)
        heads = range(HGRN_HEADS)
        sls = [slice(HGRN_DIM * h, HGRN_DIM * (h + 1)) for h in heads]
        gts = _hgrn_gates([q_ref[:, sl] for sl in sls], [f_ref[:, sl] for sl in sls], [lb_all[:, sl] for sl in sls],
                          row, causal)
        qd, ki, ke = ([gt[k].astype(BF16) for gt in gts] for k in ("qd", "ki", "ke"))
        iv = [i_ref[:, sl].astype(BF16) for sl in sls]
        s0 = [s_ref[h] for h in heads]
        a = [_nt(qd[h], ki[h]) for h in heads]
        o0 = [_nt(qd[h][:C], s0[h].astype(BF16)) for h in heads]
        u0 = [_tn(iv[h][:C], ke[h][:C]) for h in heads]
        u1 = [_tn(iv[h][C:], ke[h][C:]) for h in heads]
        s1 = [jnp.exp(gts[h]["bend0"]) * s0[h] + u0[h] for h in heads]
        o = [_nn(jnp.where(causal, a[h], 0.0).astype(BF16), iv[h]) for h in heads]
        o1 = [_nt(qd[h][C:], s1[h].astype(BF16)) for h in heads]
        for h in heads:
            st_ref[0, h] = s0[h]
            st_ref[1, h] = s1[h]
            s_ref[h] = jnp.exp(gts[h]["bend1"]) * s1[h] + u1[h]
            rec_ref[:, sls[h]] = o[h] + jnp.concatenate([o0[h], o1[h]], axis=0)

    blk = lambda j: pl.BlockSpec((HG_T, W), lambda t: (t, j))
    return pl.pallas_call(
        body, name=name, grid=(S // HG_T,),
        in_specs=[blk(0), blk(1), blk(2), pl.BlockSpec((2, W), lambda t: (0, 0))],
        out_specs=[blk(0), pl.BlockSpec((2, HGRN_HEADS, HGRN_DIM, HGRN_DIM), lambda t: (t, 0, 0, 0))],
        out_shape=[jax.ShapeDtypeStruct((S, W), F32),
                   jax.ShapeDtypeStruct((S // C, HGRN_HEADS, HGRN_DIM, HGRN_DIM), F32)],
        scratch_shapes=[pltpu.VMEM((HGRN_HEADS, HGRN_DIM, HGRN_DIM), F32)],
        compiler_params=_cparams("arbitrary"),
    )(proj, proj, proj, logits)


def _hgrn_bwd(proj, logits, states, drec, *, name):
    S = proj.shape[0]
    W, C = HGRN_W, HGRN_CHUNK
    nt = S // HG_T

    def body(q_ref, f_ref, i_ref, lg_ref, st_ref, do_ref, dp_ref, dlg_ref, ds_ref, dlb_ref):
        t = pl.program_id(0)

        @pl.when(t == 0)
        def _():
            ds_ref[...] = jnp.zeros_like(ds_ref)
            dlb_ref[...] = jnp.zeros_like(dlb_ref)

        row, causal, anti = _hgrn_consts()
        lb_all = _lower_bound(lg_ref)
        heads = range(HGRN_HEADS)
        sls = [slice(HGRN_DIM * h, HGRN_DIM * (h + 1)) for h in heads]
        qs, lbs = [q_ref[:, sl] for sl in sls], [lb_all[:, sl] for sl in sls]
        gts = _hgrn_gates(qs, [f_ref[:, sl] for sl in sls], lbs, row, causal)
        qd, ki, ke = ([gt[k] for gt in gts] for k in ("qd", "ki", "ke"))
        qdb, kib, keb = ([x.astype(BF16) for x in xs] for xs in (qd, ki, ke))
        iv = [i_ref[:, sl].astype(BF16) for sl in sls]
        dob = [do_ref[:, sl].astype(BF16) for sl in sls]
        s0, s1, ds1 = [st_ref[0, h] for h in heads], [st_ref[1, h] for h in heads], [ds_ref[h] for h in heads]
        ds1b = [x.astype(BF16) for x in ds1]
        dec0, dec1 = [jnp.exp(gt["bend0"]) for gt in gts], [jnp.exp(gt["bend1"]) for gt in gts]
        a = [_nt(qdb[h], kib[h]) for h in heads]
        da = [_nt(dob[h], iv[h]) for h in heads]
        dqd1 = [_nn(dob[h][C:], s1[h].astype(BF16)) for h in heads]
        dqd0 = [_nn(dob[h][:C], s0[h].astype(BF16)) for h in heads]
        di1 = [_nt(keb[h][C:], ds1b[h]) for h in heads]
        dke1 = [_nn(iv[h][C:], ds1b[h]) for h in heads]
        t1 = [_tn(dob[h][C:], qdb[h][C:]) for h in heads]
        t0 = [_tn(dob[h][:C], qdb[h][:C]) for h in heads]
        ds0 = [dec1[h] * ds1[h] + t1[h] for h in heads]
        ds0b = [x.astype(BF16) for x in ds0]
        a = [jnp.where(causal, x, 0.0).astype(BF16) for x in a]
        da = [jnp.where(causal, x, 0.0).astype(BF16) for x in da]
        di0 = [_nt(keb[h][:C], ds0b[h]) for h in heads]
        dke0 = [_nn(iv[h][:C], ds0b[h]) for h in heads]
        dqd_a = [_nn(da[h], kib[h]) for h in heads]
        dki = [_tn(da[h], qdb[h]) for h in heads]
        di_a = [_tn(a[h], dob[h]) for h in heads]
        dqd, dke, db = [], [], []
        for h in heads:
            ds_ref[h] = dec0[h] * ds0[h] + t0[h]
            ddec1 = jnp.sum(ds1[h] * s1[h], axis=0, keepdims=True)
            ddec0 = jnp.sum(ds0[h] * s0[h], axis=0, keepdims=True)
            dqd.append(dqd_a[h] + jnp.concatenate([dqd0[h], dqd1[h]], axis=0))
            dp_ref[:, 2 * W + HGRN_DIM * h:2 * W + HGRN_DIM * (h + 1)] = (
                di_a[h] + jnp.concatenate([di0[h], di1[h]], axis=0)).astype(BF16)
            dke.append(jnp.concatenate([dke0[h], dke1[h]], axis=0))
            gke = dke[h] * ke[h]
            dbend0 = jnp.sum(gke[:C], axis=0, keepdims=True) + ddec0 * dec0[h]
            dbend1 = jnp.sum(gke[C:], axis=0, keepdims=True) + ddec1 * dec1[h]
            dbh = dqd[h] * qd[h] - dki[h] * ki[h] - gke
            db.append(dbh + jnp.where(row == C - 1, dbend0, 0.0) + jnp.where(row == HG_T - 1, dbend1, 0.0))
        tri = jnp.where(anti, 1.0, 0.0).astype(F32)
        dlogf = [_nn(tri, db[h], HIGHEST) for h in heads]
        for h in heads:
            gt, lb, q = gts[h], lbs[h], qs[h]
            dforget = dlogf[h] / gt["forget"] - (dki[h] * gt["emb"] + dke[h] * gt["eend"])
            sg, sq = gt["sg"], gt["sq"]
            dp_ref[:, W + HGRN_DIM * h:W + HGRN_DIM * (h + 1)] = (dforget * (1.0 - lb) * sg * (1.0 - sg)).astype(BF16)
            dlb_ref[:, sls[h]] += jnp.sum(dforget * (1.0 - sg), axis=0, keepdims=True)
            dp_ref[:, sls[h]] = (dqd[h] * gt["eb"] * sq * (1.0 + q * (1.0 - sq))).astype(BF16)

        @pl.when(t == nt - 1)
        def _():
            dl0 = dlb_ref[...] * lb_all * (1.0 - lb_all)
            dlg_ref[0:1, :] = dl0
            dlg_ref[1:2, :] = -dl0

    blk = lambda j: pl.BlockSpec((HG_T, W), lambda t: (nt - 1 - t, j))
    full = pl.BlockSpec((2, W), lambda t: (0, 0))
    return pl.pallas_call(
        body, name=name, grid=(nt,),
        in_specs=[blk(0), blk(1), blk(2), full,
                  pl.BlockSpec((2, HGRN_HEADS, HGRN_DIM, HGRN_DIM), lambda t: (nt - 1 - t, 0, 0, 0)), blk(0)],
        out_specs=[pl.BlockSpec((HG_T, 3 * W), lambda t: (nt - 1 - t, 0)), full],
        out_shape=[jax.ShapeDtypeStruct((S, 3 * W), BF16), jax.ShapeDtypeStruct((2, W), F32)],
        scratch_shapes=[pltpu.VMEM((HGRN_HEADS, HGRN_DIM, HGRN_DIM), F32), pltpu.VMEM((1, W), F32)],
        compiler_params=_cparams("arbitrary"),
    )(proj, proj, proj, logits, states, drec)


def _out_proj(attn, rec, proj_h, x, g_attn, g_hgrn, g_norm2, w_out, *, name, tm=512):
    S, D = x.shape
    AW, W = ATTN_W, HGRN_W

    def body(a_ref, r_ref, hg_ref, x_ref, ga_ref, gh_ref, g2_ref, w_ref, h_ref, u_ref, m_ref):
        av = a_ref[...]
        m_ref[:, :AW] = (av * _rstd(av) * ga_ref[...]).astype(BF16)
        for h in range(HGRN_HEADS):
            sl = slice(HGRN_DIM * h, HGRN_DIM * (h + 1))
            rv, hg = r_ref[:, sl], hg_ref[:, sl]
            m_ref[:, AW + HGRN_DIM * h:AW + HGRN_DIM * (h + 1)] = (
                (rv * _rstd(rv) * gh_ref[:, sl]) * (hg * _sigmoid(hg))).astype(BF16)
        h1 = x_ref[...] + _nn(m_ref[...], w_ref[...])
        h_ref[...] = h1
        u_ref[...] = (h1 * _rstd(h1) * g2_ref[...]).astype(BF16)

    row = lambda w, j=0: pl.BlockSpec((tm, w), lambda i: (i, j))
    vec = lambda w: pl.BlockSpec((1, w), lambda i: (0, 0))
    return pl.pallas_call(
        body, name=name, grid=(S // tm,),
        in_specs=[row(AW), row(W), row(W, 3), row(D), vec(AW), vec(W), vec(D), _resident(w_out.shape)],
        out_specs=[row(D), row(D), row(AW + W)],
        out_shape=[jax.ShapeDtypeStruct((S, D), F32), jax.ShapeDtypeStruct((S, D), BF16),
                   jax.ShapeDtypeStruct((S, AW + W), BF16)],
        compiler_params=_cparams("parallel"),
    )(attn, rec, proj_h, x, g_attn, g_hgrn, g_norm2, w_out)


def _dmix_post_bwd(dh1b, w_out, attn, rec, proj_h, g_attn, g_hgrn, *, name, tm=512):
    S, D = dh1b.shape
    AW, W = ATTN_W, HGRN_W

    def body(dh_ref, w_ref, a_ref, r_ref, hg_ref, ga_ref, gh_ref, do_ref, dl_ref, dr_ref, dhg_ref, dga_ref, dgh_ref):
        first = pl.program_id(0) == 0
        dmix = _nt(dh_ref[...], w_ref[...])
        av = a_ref[...]
        dov, dga = _norm_bwd(av, ga_ref[...], dmix[:, :AW])
        do_ref[...] = dov
        shift = HEAD_DIM.bit_length() - 1
        hi = lax.shift_right_logical(lax.broadcasted_iota(jnp.int32, (AW, AW), 0), shift)
        hj = lax.shift_right_logical(lax.broadcasted_iota(jnp.int32, (AW, AW), 1), shift)
        prod = dov * av
        hi_part = prod.astype(BF16)
        lo_part = (prod - hi_part.astype(F32)).astype(BF16)
        same_head = jnp.where(hi == hj, 1.0, 0.0).astype(BF16)
        dl_ref[...] = _nn(hi_part, same_head) + _nn(lo_part, same_head)
        _accumulate(dga_ref, jnp.sum(dga, axis=0, keepdims=True), first)

        @pl.when(first)
        def _():
            dgh_ref[...] = jnp.zeros_like(dgh_ref)

        for h in range(HGRN_HEADS):
            sl = slice(HGRN_DIM * h, HGRN_DIM * (h + 1))
            rv, hg, gv = r_ref[:, sl], hg_ref[:, sl], gh_ref[:, sl]
            dout = dmix[:, AW + HGRN_DIM * h:AW + HGRN_DIM * (h + 1)]
            sg = _sigmoid(hg)
            drv, dgh = _norm_bwd(rv, gv, dout * (hg * sg))
            dr_ref[:, sl] = drv
            dgh_ref[:, sl] += jnp.sum(dgh, axis=0, keepdims=True)
            dhg_ref[:, sl] = (dout * (rv * _rstd(rv) * gv) * (sg * (1.0 + hg * (1.0 - sg)))).astype(BF16)

    row = lambda w, j=0: pl.BlockSpec((tm, w), lambda i: (i, j))
    vec = lambda w: pl.BlockSpec((1, w), lambda i: (0, 0))
    return pl.pallas_call(
        body, name=name, grid=(S // tm,),
        in_specs=[row(D), _resident(w_out.shape), row(AW), row(W), row(W, 3), vec(AW), vec(W)],
        out_specs=[row(AW), row(AW), row(W), row(W), vec(AW), vec(W)],
        out_shape=[jax.ShapeDtypeStruct((S, AW), F32), jax.ShapeDtypeStruct((S, AW), F32),
                   jax.ShapeDtypeStruct((S, W), F32), jax.ShapeDtypeStruct((S, W), BF16),
                   jax.ShapeDtypeStruct((1, AW), F32), jax.ShapeDtypeStruct((1, W), F32)],
        compiler_params=_cparams("arbitrary"),
    )(dh1b, w_out, attn, rec, proj_h, g_attn, g_hgrn)


def _conv_act(g, g1, g2, w_ref, b_ref):
    c = b_ref[...] + w_ref[0:1, :] * g2 + w_ref[1:2, :] * g1 + w_ref[2:3, :] * g
    return c, 0.5 * (1.0 + lax.erf(c * (2.0 ** -0.5)))


def _shift_down(g, halo, row):
    g1 = jnp.where(row == 0, halo[7:8], pltpu.roll(g, 1, 0))
    g2 = jnp.where(row == 0, halo[6:7], jnp.where(row == 1, halo[7:8], pltpu.roll(g, 2, 0)))
    return g1, g2


def _shift_up(x, halo, row):
    n = x.shape[0]
    x1 = jnp.where(row == n - 1, halo[0:1], pltpu.roll(x, n - 1, 0))
    x2 = jnp.where(row == n - 2, halo[0:1], jnp.where(row == n - 1, halo[1:2], pltpu.roll(x, n - 2, 0)))
    return x1, x2


def _up_glu(u, w_up, conv_w, conv_b, *, name, tm=1024, tn=256):
    S, D = u.shape
    F = w_up.shape[1] // 2
    nf = F // tn

    def body(u_ref, wg_ref, wv_ref, cw_ref, cb_ref, g_ref, v_ref, a_ref, halo_ref):
        i, j = pl.program_id(0), pl.program_id(1)

        @pl.when(i == 0)
        def _():
            halo_ref[j] = jnp.zeros((SUBLANES, tn), F32)

        uv = u_ref[...]
        g, v = _nn(uv, wg_ref[...]), _nn(uv, wv_ref[...])
        row = lax.broadcasted_iota(jnp.int32, (tm, tn), 0)
        g1, g2 = _shift_down(g, halo_ref[j], row)
        c, cdf = _conv_act(g, g1, g2, cw_ref, cb_ref)
        a_ref[...] = (c * cdf * v).astype(BF16)
        g_ref[...] = g.astype(BF16)
        v_ref[...] = v.astype(BF16)
        halo_ref[j] = g[tm - SUBLANES:, :]

    col = pl.BlockSpec((tm, tn), lambda i, j: (i, j))
    out = jax.ShapeDtypeStruct((S, F), BF16)
    return pl.pallas_call(
        body, name=name, grid=(S // tm, nf),
        in_specs=[pl.BlockSpec((tm, D), lambda i, j: (i, 0)), pl.BlockSpec((D, tn), lambda i, j: (0, j)),
                  pl.BlockSpec((D, tn), lambda i, j: (0, j + nf)), pl.BlockSpec((3, tn), lambda i, j: (0, j)),
                  pl.BlockSpec((1, tn), lambda i, j: (0, j))],
        out_specs=[col, col, col], out_shape=[out, out, out],
        scratch_shapes=[pltpu.VMEM((nf, SUBLANES, tn), F32)], compiler_params=_cparams("arbitrary", "arbitrary"),
    )(u, w_up, w_up, conv_w, conv_b)


def _dact_glu_bwd(dh2b, w_down, gate, val, conv_w, conv_b, *, name, tm=1024, tn=256):
    S, D = dh2b.shape
    F = gate.shape[1]
    nf, ni = F // tn, S // tm
    hb = tm // SUBLANES

    def body(dh_ref, wd_ref, g_ref, gh_ref, v_ref, cw_ref, cb_ref, dg_ref, dv_ref, dcw_ref, dcb_ref, halo_ref, acc_ref):
        i, j = pl.program_id(0), pl.program_id(1)

        @pl.when(i == 0)
        def _():
            halo_ref[j] = jnp.zeros((SUBLANES, tn), F32)
            acc_ref[j] = jnp.zeros((SUBLANES, tn), F32)

        g = g_ref[...].astype(F32)
        before = jnp.where(i < ni - 1, gh_ref[...].astype(F32), 0.0)
        row = lax.broadcasted_iota(jnp.int32, (tm, tn), 0)
        g1, g2 = _shift_down(g, before[SUBLANES:], row)
        c, cdf = _conv_act(g, g1, g2, cw_ref, cb_ref)
        da = _nt(dh_ref[...], wd_ref[...])
        dv_ref[...] = (da * (c * cdf)).astype(BF16)
        pdf = jnp.exp(-0.5 * c * c) * (1.0 / (2.0 * jnp.pi) ** 0.5)
        dc = da * v_ref[...].astype(F32) * (cdf + c * pdf)
        d1, d2 = _shift_up(dc, halo_ref[j], row)
        dg_ref[...] = (cw_ref[2:3, :] * dc + cw_ref[1:2, :] * d1 + cw_ref[0:1, :] * d2).astype(BF16)
        halo_ref[j] = dc[:SUBLANES, :]
        for k, t in enumerate((dc * g2, dc * g1, dc * g, dc)):
            acc_ref[j, k:k + 1, :] += jnp.sum(t, axis=0, keepdims=True)

        @pl.when((i == ni - 1) & (j == nf - 1))
        def _():
            for jj in range(nf):
                dcw_ref[:, jj * tn:(jj + 1) * tn] = acc_ref[jj, 0:3, :]
                dcb_ref[:, jj * tn:(jj + 1) * tn] = acc_ref[jj, 3:4, :]

    tile = pl.BlockSpec((tm, tn), lambda i, j: (ni - 1 - i, j))
    return pl.pallas_call(
        body, name=name, grid=(ni, nf),
        in_specs=[pl.BlockSpec((tm, D), lambda i, j: (ni - 1 - i, 0)), pl.BlockSpec((tn, D), lambda i, j: (j, 0)),
                  tile, pl.BlockSpec((SUBLANES * 2, tn), lambda i, j: (jnp.maximum((ni - 1 - i) * (hb // 2) - 1, 0), j)),
                  tile, pl.BlockSpec((3, tn), lambda i, j: (0, j)), pl.BlockSpec((1, tn), lambda i, j: (0, j))],
        out_specs=[tile, tile, pl.BlockSpec((3, F), lambda i, j: (0, 0)), pl.BlockSpec((1, F), lambda i, j: (0, 0))],
        out_shape=[jax.ShapeDtypeStruct((S, F), BF16), jax.ShapeDtypeStruct((S, F), BF16),
                   jax.ShapeDtypeStruct((3, F), F32), jax.ShapeDtypeStruct((1, F), F32)],
        scratch_shapes=[pltpu.VMEM((nf, SUBLANES, tn), F32), pltpu.VMEM((nf, SUBLANES, tn), F32)],
        compiler_params=_cparams("arbitrary", "arbitrary"),
    )(dh2b, w_down, gate, gate, val, conv_w, conv_b)


def _down_loss(act, w_down, h1, g, target, *, name, tm=512):
    S, F = act.shape
    D = h1.shape[1]

    def body(a_ref, w_ref, h_ref, g_ref, t_ref, dh_ref, dhb_ref, dg_ref, loss_ref):
        first = pl.program_id(0) == 0
        h2 = h_ref[...] + _nn(a_ref[...], w_ref[...])
        gv = g_ref[...]
        r = _rstd(h2)
        xh = h2 * r
        err = xh * gv - t_ref[...]
        part_loss = 0.5 * jnp.sum(jnp.mean(err * err, axis=-1, keepdims=True), axis=0, keepdims=True)
        dy = err * (1.0 / D)
        dxh = dy * gv
        dh = r * (dxh - xh * jnp.mean(dxh * xh, axis=-1, keepdims=True))
        dh_ref[...] = dh
        dhb_ref[...] = dh.astype(BF16)
        _accumulate(dg_ref, jnp.sum(dy * xh, axis=0, keepdims=True), first)
        _accumulate(loss_ref, jnp.broadcast_to(part_loss, (1, LANES)), first)

    row = lambda w: pl.BlockSpec((tm, w), lambda i: (i, 0))
    vec = lambda w: pl.BlockSpec((1, w), lambda i: (0, 0))
    return pl.pallas_call(
        body, name=name, grid=(S // tm,), in_specs=[row(F), _resident(w_down.shape), row(D), vec(D), row(D)],
        out_specs=[row(D), row(D), vec(D), vec(LANES)],
        out_shape=[jax.ShapeDtypeStruct((S, D), F32), jax.ShapeDtypeStruct((S, D), BF16),
                   jax.ShapeDtypeStruct((1, D), F32), jax.ShapeDtypeStruct((1, LANES), F32)],
        compiler_params=_cparams("arbitrary"),
    )(act, w_down, h1, g, target)


def _grad_norm_input(pieces, ws, x, g, add, *, name, tm=512):
    S, D = x.shape
    widths = [p.shape[1] for p in pieces]
    n, nw = len(pieces), len(ws)
    where, wi, off = [], 0, 0
    for wd in widths:
        if off == ws[wi].shape[1]:
            wi, off = wi + 1, 0
        where.append((wi, off))
        off += wd

    def body(*refs):
        p_refs, w_refs = refs[:n], refs[n:n + nw]
        x_ref, g_ref, add_ref, dx_ref, dxb_ref, dg_ref = refs[n + nw:]
        halves = _row_halves(tm)
        du = []
        for rows in halves:
            terms = [_nt(p_refs[k][rows, :], w_refs[wi][:, off:off + widths[k]]) for k, (wi, off) in enumerate(where)]
            du.append(sum(terms[1:], terms[0]))
        dg_sum = None
        for rows, duh in zip(halves, du):
            dx, dg = _norm_bwd(x_ref[rows, :], g_ref[...], duh)
            dx = add_ref[rows, :] + dx
            dx_ref[rows, :] = dx
            dxb_ref[rows, :] = dx.astype(BF16)
            part = jnp.sum(dg, axis=0, keepdims=True)
            dg_sum = part if dg_sum is None else dg_sum + part
        _accumulate(dg_ref, dg_sum, pl.program_id(0) == 0)

    row = lambda w_: pl.BlockSpec((tm, w_), lambda i: (i, 0))
    vec = pl.BlockSpec((1, D), lambda i: (0, 0))
    return pl.pallas_call(
        body, name=name, grid=(S // tm,),
        in_specs=[row(wd) for wd in widths] + [_resident(w.shape) for w in ws] + [row(D), vec, row(D)],
        out_specs=[row(D), row(D), vec],
        out_shape=[jax.ShapeDtypeStruct((S, D), F32), jax.ShapeDtypeStruct((S, D), BF16),
                   jax.ShapeDtypeStruct((1, D), F32)],
        compiler_params=_cparams("arbitrary"),
    )(*pieces, *ws, x, g, add)


def _rows(a):
    return a.reshape(-1, a.shape[-1])


def _row_tile(rows, cols, itemsize=4, budget=1 << 20):
    t = rows
    while t % 32 == 0 and t * cols * itemsize > budget:
        t //= 2
    return t


def _sum_cast(arrs, out_dtype, *, name):
    shape = arrs[0].shape
    flat = [_rows(a) for a in arrs]
    R, C = flat[0].shape
    tr = _row_tile(R, C)

    def body(*refs):
        acc = refs[0][...].astype(F32)
        for r in refs[1:-1]:
            acc = acc + r[...].astype(F32)
        refs[-1][...] = acc.astype(out_dtype)

    spec = pl.BlockSpec((tr, C), lambda i: (i, 0))
    return pl.pallas_call(
        body, name=name, grid=(R // tr,), in_specs=[spec] * len(flat), out_specs=spec,
        out_shape=jax.ShapeDtypeStruct((R, C), out_dtype), compiler_params=_cparams("parallel"),
    )(*flat).reshape(shape)


def _adamw(parts, w, m, v, *, name):
    shape = w.shape
    w2, m2, v2 = _rows(w), _rows(m), _rows(v)
    R, C = w2.shape
    parts = [p.reshape(-1, R, C) for p in parts]
    tr = _row_tile(R, C)
    np_ = len(parts)
    c1, c2 = 1.0 - ADAM_B1 ** ADAM_STEP, 1.0 - ADAM_B2 ** ADAM_STEP

    def body(*refs):
        terms = [(r, k) for r in refs[:np_] for k in range(r.shape[0])]
        g = terms[0][0][terms[0][1]].astype(F32)
        for r, k in terms[1:]:
            g = g + r[k].astype(F32)
        w_ref, m_ref, v_ref, g_out, d_out, m_out, v_out = refs[np_:]
        mn = ADAM_B1 * m_ref[...] + (1.0 - ADAM_B1) * g
        vn = ADAM_B2 * v_ref[...] + (1.0 - ADAM_B2) * (g * g)
        g_out[...] = g
        d_out[...] = -ADAM_LR * ((mn / c1) / (jnp.sqrt(vn / c2) + ADAM_EPS) + ADAM_WD * w_ref[...])
        m_out[...] = mn
        v_out[...] = vn

    spec = pl.BlockSpec((tr, C), lambda i: (i, 0))
    out = jax.ShapeDtypeStruct((R, C), F32)
    stacks = [pl.BlockSpec((p.shape[0], tr, C), lambda i: (0, i, 0)) for p in parts]
    res = pl.pallas_call(
        body, name=name, grid=(R // tr,), in_specs=stacks + [spec] * 3, out_specs=[spec] * 4,
        out_shape=[out] * 4, compiler_params=_cparams("parallel"),
    )(*parts, w2, m2, v2)
    return [r.reshape(shape) for r in res]


def _coords():
    return lax.axis_index("x"), lax.axis_index("y"), lax.axis_index("c")


def _all_gather(shards, *, name):
    n = len(shards)

    def body(*refs):
        x_refs, out_refs = refs[:n], refs[n:2 * n]
        send_sems, recv_sems, local_sems = refs[2 * n:]
        x, y, c = _coords()
        me, sibling = (x, y, c), (x, y, 1 - c)
        chips = [(1 - x, y), (x, 1 - y), (1 - x, 1 - y)]

        def slot(a, dev):
            return out_refs[a].at[4 * dev[0] + 2 * dev[1] + dev[2]]

        def copy(a, k, block, to, src=None):
            return pltpu.make_async_remote_copy(
                src_ref=slot(a, block) if src is None else src, dst_ref=slot(a, block),
                send_sem=send_sems.at[7 * a + k], recv_sem=recv_sems.at[7 * a + k], device_id=to, device_id_type=MESH)

        mine = [pltpu.make_async_copy(x_refs[a], slot(a, me), local_sems.at[a]) for a in range(n)]
        for cp in mine:
            cp.start()
        first = []
        for a in range(n):
            first.append(copy(a, 0, me, sibling, src=x_refs[a]))
            first += [copy(a, 1 + j, me, (*chip, c), src=x_refs[a]) for j, chip in enumerate(chips)]
        for cp in first:
            cp.start()
        passed = []
        for j, chip in enumerate(chips):
            for a in range(n):
                copy(a, 1 + j, (*chip, c), me).wait_recv()
                fwd = copy(a, 4 + j, (*chip, c), sibling)
                fwd.start()
                passed.append(fwd)
        for a in range(n):
            copy(a, 0, sibling, me).wait_recv()
            for j, chip in enumerate(chips):
                copy(a, 4 + j, (*chip, 1 - c), me).wait_recv()
        for cp in first + passed:
            cp.wait_send()
        for cp in mine:
            cp.wait()

    return pl.pallas_call(
        body, name=name, in_specs=[HBM] * n, out_specs=[HBM] * n,
        out_shape=[jax.ShapeDtypeStruct((N_DEV, *s.shape), s.dtype) for s in shards],
        scratch_shapes=[pltpu.SemaphoreType.DMA((7 * n,)), pltpu.SemaphoreType.DMA((7 * n,)),
                        pltpu.SemaphoreType.DMA((n,))],
    )(*shards)


def _flip_y(x, y, c):
    return (x, 1 - y, c)


def _flip_x(x, y, c):
    return (1 - x, y, c)


def _flip_xy(x, y, c):
    return (1 - x, 1 - y, c)


SEM = pl.BlockSpec(memory_space=pltpu.SEMAPHORE)
SIDE_EFFECT = pltpu.SideEffectType.DATAFLOW_SIDE_EFFECTING


def _in_hbm(a):
    return pltpu.with_memory_space_constraint(a, pltpu.HBM)


def _copies_start(srcs, lands, plan, n_copies, *, name):
    ns, nl = len(srcs), len(lands)

    def body(*refs):
        src_refs, land_refs = refs[:ns], refs[ns:ns + nl]
        send_sems, recv_sems = refs[ns + nl:ns + nl + 2]
        token = refs[-1]
        for k, (src, dst, peer, _) in enumerate(plan(src_refs, land_refs, *_coords())):
            pltpu.make_async_remote_copy(src_ref=src, dst_ref=dst, send_sem=send_sems.at[k], recv_sem=recv_sems.at[k],
                                         device_id=peer, device_id_type=MESH).start()
        token[...] = jnp.zeros_like(token)

    bufs = [*srcs, *lands]
    res = pl.pallas_call(
        body, name=name, in_specs=[HBM] * (ns + nl),
        out_specs=(SEM, SEM, *[HBM] * (ns + nl), pl.BlockSpec(memory_space=pltpu.VMEM)),
        out_shape=(pltpu.SemaphoreType.DMA((n_copies,)), pltpu.SemaphoreType.DMA((n_copies,)),
                   *[pltpu.HBM(b.shape, b.dtype) for b in bufs], jax.ShapeDtypeStruct((SUBLANES, LANES), F32)),
        input_output_aliases={i: 2 + i for i in range(ns + nl)},
        compiler_params=pltpu.CompilerParams(has_side_effects=SIDE_EFFECT),
    )(*[_in_hbm(b) for b in bufs])
    return res[0], res[1], list(res[2:2 + ns]), list(res[2 + ns:2 + ns + nl]), res[-1]


def _copies_wait(started, plan, after, *, name):
    send_sems, recv_sems, srcs, lands, _ = started
    ns, nl = len(srcs), len(lands)

    def body(*refs):
        src_refs, land_refs = refs[:ns], refs[ns:ns + nl]
        send_sems, recv_sems = refs[ns + nl:ns + nl + 2]
        for k, (src, dst, peer, here) in enumerate(plan(src_refs, land_refs, *_coords())):
            pltpu.make_async_remote_copy(src_ref=src, dst_ref=dst, send_sem=send_sems.at[k], recv_sem=recv_sems.at[k],
                                         device_id=peer, device_id_type=MESH).wait_send()
            pltpu.make_async_remote_copy(src_ref=src, dst_ref=here, send_sem=send_sems.at[k], recv_sem=recv_sems.at[k],
                                         device_id=peer, device_id_type=MESH).wait_recv()

    bufs = [*srcs, *lands]
    res = pl.pallas_call(
        body, name=name, in_specs=[HBM] * (ns + nl) + [SEM, SEM, pl.BlockSpec(memory_space=pl.ANY)],
        out_specs=[HBM] * (ns + nl), out_shape=[pltpu.HBM(b.shape, b.dtype) for b in bufs],
        input_output_aliases={i: i for i in range(ns + nl)},
        compiler_params=pltpu.CompilerParams(has_side_effects=SIDE_EFFECT),
    )(*bufs, send_sems, recv_sems, after)
    return list(res[ns:])


def _dev_index(dev):
    return 4 * dev[0] + 2 * dev[1] + dev[2]


def _ag_chips_plan(src_refs, land_refs, x, y, c):
    me = _dev_index((x, y, c))
    return [(src, land.at[me], peer, land.at[_dev_index(peer)])
            for src, land in zip(src_refs, land_refs) for peer in (_flip_y(x, y, c), _flip_x(x, y, c), _flip_xy(x, y, c))]


def _ag_sibling_plan(src_refs, land_refs, x, y, c):
    chips = [(x, y), (x, 1 - y), (1 - x, y), (1 - x, 1 - y)]
    return [(land.at[_dev_index((*chip, c))], land.at[_dev_index((*chip, c))], (x, y, 1 - c),
             land.at[_dev_index((*chip, 1 - c))]) for land in land_refs for chip in chips]


def _ag_direct_plan(src_refs, land_refs, x, y, c):
    me = _dev_index((x, y, c))
    plan = []
    for src, land in zip(src_refs, land_refs):
        for m in range(1, N_DEV):
            peer = (x + (m >> 2) * (1 - 2 * x), y + ((m >> 1) & 1) * (1 - 2 * y), c + (m & 1) * (1 - 2 * c))
            plan.append((src, land.at[me], peer, land.at[_dev_index(peer)]))
    return plan


def _rs_direct_plan(src_refs, land_refs, x, y, c):
    plan = []
    for src, land in zip(src_refs, land_refs):
        for m in range(1, N_DEV):
            peer = (x + (m >> 2) * (1 - 2 * x), y + ((m >> 1) & 1) * (1 - 2 * y), c + (m & 1) * (1 - 2 * c))
            plan.append((src.at[_dev_index(peer)], land.at[m - 1], peer, land.at[m - 1]))
    return plan


def _rs_start(grads, me, *, name):
    own = [lax.dynamic_index_in_dim(g, me, 0, keepdims=False) for g in grads]
    lands = [lax.empty((N_DEV - 1, *g.shape[1:]), g.dtype) for g in grads]
    return _copies_start(grads, lands, _rs_direct_plan, (N_DEV - 1) * len(grads), name=name), own


def _rs_finish(started, after, *, name):
    handle, own = started
    got = _copies_wait(handle, _rs_direct_plan, after, name=name)
    return [[o, land] for o, land in zip(own, got)]


def _by_device_cols(w):
    K, N = w.shape
    return w.reshape(K, N_DEV, N // N_DEV).transpose(1, 0, 2)


def _gathered_cols(w8):
    return w8.transpose(1, 0, 2).reshape(w8.shape[1], -1)


def _pair_major(w, inverse=False):
    K = w.shape[0]
    a, b = (ATTN_W // LANES, 3) if inverse else (3, ATTN_W // LANES)
    return w.reshape(K, a, b, LANES).transpose(0, 2, 1, 3).reshape(K, 3 * ATTN_W)


def kernel(x, norm1_g, w_in, attn_norm_g, hgrn_norm_g, hgrn_lb_logits, w_out, norm2_g, w_up, conv_w, conv_b, w_down, final_norm_g, loss_target, m_norm1_g, m_w_in, m_attn_norm_g, m_hgrn_norm_g, m_hgrn_lb_logits, m_w_out, m_norm2_g, m_w_up, m_conv_w, m_conv_b, m_w_down, m_final_norm_g, v_norm1_g, v_w_in, v_attn_norm_g, v_hgrn_norm_g, v_hgrn_lb_logits, v_w_out, v_norm2_g, v_w_up, v_conv_w, v_conv_b, v_w_down, v_final_norm_g):
    xs, target = x[0], loss_target[0]
    S, D = xs.shape
    NA = 3 * ATTN_W
    fng = final_norm_g.reshape(1, D)

    casts = [_sum_cast([w[0]], BF16, name=f"cast_{nm}") for nm, w in
             (("w_in", w_in), ("w_out", w_out), ("w_up", w_up), ("w_down", w_down))]
    me = _dev_index(_coords())
    (g_in,) = _all_gather(casts[:1], name="ag_w_in")
    later = casts[1:] + [conv_w[0]]
    ag1 = _copies_start(later, [lax.empty((N_DEV, *s.shape), s.dtype) for s in later], _ag_chips_plan,
                        3 * len(later), name="ag_chips_start")
    wi = _gathered_cols(g_in)
    wi_a, wi_h = _pair_major(wi[:, :NA]), wi[:, NA:]

    u1, proj_a = _proj_attn(xs, norm1_g + ag1[4][0, 0], wi_a, name="proj_attn")
    proj_h = _mm(u1, wi_h, name="proj_hgrn")
    attn, lse = _attn_fwd(proj_a, name="attn_fwd")
    lands = _copies_wait(ag1, _ag_chips_plan, attn, name="ag_chips_wait")
    lands = [lax.dynamic_update_index_in_dim(l, s, me, 0) for l, s in zip(lands, later)]
    ag2 = _copies_start([], lands, _ag_sibling_plan, 4 * len(later), name="ag_sibling_start")
    rec, states = _hgrn_fwd(proj_h, hgrn_lb_logits + ag2[4][0, 0], name="hgrn_fwd")
    g_out, g_up, g_down, g_cw = _copies_wait(ag2, _ag_sibling_plan, rec, name="ag_sibling_wait")
    wo = g_out.reshape(-1, D)
    wu = _gathered_cols(g_up)
    wd = g_down.reshape(-1, D)
    cw = _gathered_cols(g_cw)
    h1, u2, mixed = _out_proj(attn, rec, proj_h, xs, attn_norm_g, hgrn_norm_g, norm2_g, wo, name="out_proj")
    gate, val, act = _up_glu(u2, wu, cw, conv_b, name="up_glu")
    dh2, dh2b, d_fng, loss_part = _down_loss(act, wd, h1, fng, target, name="down_loss")

    dgate, dval, d_cw, d_cb = _dact_glu_bwd(dh2b, wd, gate, val, cw, conv_b, name="dact_glu_bwd")
    dw_down = _mm_tn(act, dh2b, tm=256, name="dw_down")
    dh1, dh1b, d_n2g = _grad_norm_input([dgate, dval], [wu], h1, norm2_g, dh2, name="du2_norm2_bwd")
    dw_up = [_mm_tn(u2, dy, tm=1024, tn=256, name=f"dw_up_{nm}") for nm, dy in (("gate", dgate), ("val", dval))]
    rs_ffn = _rs_start([dw_down.reshape(N_DEV, -1, D),
                        jnp.concatenate([h.reshape(D, N_DEV // 2, -1).transpose(1, 0, 2) for h in dw_up], axis=0)],
                       me, name="rs_ffn_start")
    dattn, delta, drec, dhg, d_ang, d_hng = _dmix_post_bwd(dh1b, wo, attn, rec, proj_h, attn_norm_g + rs_ffn[0][4][0, 0],
                                                          hgrn_norm_g, name="dmix_post_bwd")
    dw_out = _mm_tn(mixed, dh1b, name="dw_out")
    rs_out = _rs_start([dw_out.reshape(N_DEV, -1, D)], me, name="rs_out_start")
    dproj_h, d_lbl = _hgrn_bwd(proj_h, hgrn_lb_logits + rs_out[0][4][0, 0], states, drec, name="hgrn_bwd")
    no_state = jnp.zeros_like(loss_part)
    small = [("loss", loss_part, no_state, no_state, no_state),
             ("attn_norm_g", d_ang, attn_norm_g, m_attn_norm_g, v_attn_norm_g),
             ("hgrn_norm_g", d_hng, hgrn_norm_g, m_hgrn_norm_g, v_hgrn_norm_g),
             ("hgrn_lb_logits", d_lbl, hgrn_lb_logits, m_hgrn_lb_logits, v_hgrn_lb_logits),
             ("norm2_g", d_n2g, norm2_g, m_norm2_g, v_norm2_g),
             ("conv_b", d_cb, conv_b, m_conv_b, v_conv_b),
             ("final_norm_g", d_fng, final_norm_g, m_final_norm_g, v_final_norm_g)]
    pack = lambda arrs: jnp.concatenate([a.reshape(1, -1) for a in arrs], axis=1)
    small_own = [pack([s[1] for s in small]), d_cw]
    ag_small = _copies_start(small_own, [lax.empty((N_DEV, *s.shape), s.dtype) for s in small_own], _ag_direct_plan,
                             (N_DEV - 1) * len(small_own), name="ag_small_start")
    dproj_a = _attn_bwd(proj_a, dattn, lse, delta, name="attn_bwd")
    dw_in = jnp.concatenate([_pair_major(_mm_tn(u1, dproj_a, tm=1024, tn=512, name="dw_in_attn"), inverse=True),
                             _mm_tn(u1, dproj_h, tm=1024, tn=512, name="dw_in_hgrn"),
                             _mm_tn(u1, dhg, tm=1024, tn=512, name="dw_in_gate")], axis=1)
    rs_in = _rs_start([_by_device_cols(dw_in)], me, name="rs_in_start")
    grad_x, _, d_n1g = _grad_norm_input([dproj_a, dproj_h, dhg], [wi_a, wi_h], xs, norm1_g + rs_in[0][4][0, 0], dh1,
                                        name="du1_norm1_bwd")

    res = {}

    def update(nm, parts, w, m, v):
        res[nm] = _adamw(parts, w, m, v, name=f"adamw_{nm}")

    g_down, g_up = _rs_finish(rs_ffn, grad_x, name="rs_ffn_wait")
    update("w_down", g_down, w_down, m_w_down, v_w_down)
    update("w_up", g_up, w_up, m_w_up, v_w_up)
    (g_out,) = _rs_finish(rs_out, grad_x, name="rs_out_wait")
    update("w_out", g_out, w_out, m_w_out, v_w_out)
    (g_in,) = _rs_finish(rs_in, res["w_up"][1], name="rs_in_wait")
    update("w_in", g_in, w_in, m_w_in, v_w_in)

    g_small, g_dcw = [lax.dynamic_update_index_in_dim(l, s, me, 0)
                      for l, s in zip(_copies_wait(ag_small, _ag_direct_plan, grad_x, name="ag_small_wait"), small_own)]
    sm = _adamw([g_small], pack([s[2] for s in small]), pack([s[3] for s in small]),
                pack([s[4] for s in small]), name="adamw_small")
    off = 0
    for nm, _, w, _, _ in small:
        res[nm] = [r[:, off:off + w.size].reshape(w.shape) for r in sm]
        off += w.size
    ncw = conv_w.shape[-1]
    mine_cw = lax.dynamic_slice_in_dim(g_dcw, me * ncw, ncw, axis=2)
    res["conv_w"] = _adamw([mine_cw], conv_w, m_conv_w, v_conv_w, name="adamw_conv_w")
    late, _ = lax.optimization_barrier((d_n1g, res["w_in"][1]))
    update("norm1_g", _all_gather([late], name="ag_norm1_grad"), norm1_g, m_norm1_g, v_norm1_g)

    loss = res["loss"][0][0, 0]
    order = ["norm1_g", "w_in", "attn_norm_g", "hgrn_norm_g", "hgrn_lb_logits", "w_out", "norm2_g", "w_up",
             "conv_w", "conv_b", "w_down", "final_norm_g"]
    return (loss, grad_x[None], *[res[nm][0] for nm in order], *[res[nm][1] for nm in order],
            *[res[nm][2] for nm in order], *[res[nm][3] for nm in order])
```

```python
import jax
import jax.numpy as jnp
from jax import lax
from jax.experimental import pallas as pl
from jax.experimental.pallas import tpu as pltpu

F32, BF16 = jnp.float32, jnp.bfloat16
NORM_EPS = 1e-6
ATTN_HEADS, HEAD_DIM, ATTN_BLOCK = 8, 64, 128
DILATIONS = (1, 4, 16)
ATTN_SCALE = HEAD_DIM ** -0.5
ATTN_W = ATTN_HEADS * HEAD_DIM
HGRN_HEADS, HGRN_DIM, HGRN_CHUNK = 4, 128, 64
HGRN_W = HGRN_HEADS * HGRN_DIM
ADAM_LR, ADAM_B1, ADAM_B2, ADAM_EPS, ADAM_WD, ADAM_STEP = 0.001, 0.9, 0.999, 1e-08, 0.01, 10
LANES, SUBLANES = 128, 8
VMEM_LIMIT_BYTES = 56 * 1024 * 1024
N_DEV = 8
MESH = pl.DeviceIdType.MESH
HBM = pl.BlockSpec(memory_space=pltpu.HBM)
HIGHEST = lax.Precision.HIGHEST


def _cparams(*sem):
    return pltpu.CompilerParams(dimension_semantics=sem, vmem_limit_bytes=VMEM_LIMIT_BYTES)


def _tile(n, pref):
    if n <= pref:
        return n
    t = (pref // LANES) * LANES
    while n % t:
        t -= LANES
    return t


def _resident(shape):
    return pl.BlockSpec(shape, lambda *_: (0,) * len(shape), pipeline_mode=pl.Buffered(1))


def _dot(a, b, dims, precision=None):
    return lax.dot_general(a, b, (dims, ((), ())), precision=precision, preferred_element_type=F32)


def _nn(a, b, precision=None):
    return _dot(a, b, ((1,), (0,)), precision)


def _nt(a, b):
    return _dot(a, b, ((1,), (1,)))


def _tn(a, b):
    return _dot(a, b, ((0,), (0,)))


def _sigmoid(x):
    return 1.0 / (1.0 + jnp.exp(-x))


def _rstd(x):
    return lax.rsqrt(jnp.mean(x * x, axis=-1, keepdims=True) + NORM_EPS)


def _norm_bwd(x, g, du):
    r = _rstd(x)
    xh = x * r
    dxh = du * g
    return r * (dxh - xh * jnp.mean(dxh * xh, axis=-1, keepdims=True)), du * xh


def _row_halves(tm):
    return [pl.ds(0, tm // 2), pl.ds(tm // 2, tm // 2)]


def _accumulate(ref, part, first):
    @pl.when(first)
    def _():
        ref[...] = part

    @pl.when(jnp.logical_not(first))
    def _():
        ref[...] += part


def _mm(a, b, *, name, out_dtype=F32, tm=1024, tn=512):
    M, K = a.shape
    N = b.shape[1]
    tm, tn = _tile(M, tm), _tile(N, tn)

    def body(a_ref, b_ref, o_ref):
        o_ref[...] = _nn(a_ref[...], b_ref[...]).astype(out_dtype)

    return pl.pallas_call(
        body, name=name, grid=(M // tm, N // tn),
        in_specs=[pl.BlockSpec((tm, K), lambda i, j: (i, 0)), pl.BlockSpec((K, tn), lambda i, j: (0, j))],
        out_specs=pl.BlockSpec((tm, tn), lambda i, j: (i, j)), out_shape=jax.ShapeDtypeStruct((M, N), out_dtype),
        compiler_params=_cparams("parallel", "parallel"),
    )(a, b)


def _mm_tn(x, dy, *, name, tm=512, tn=1024):
    S, M = x.shape
    N = dy.shape[1]
    tm, tn = _tile(M, tm), _tile(N, tn)

    def body(x_ref, dy_ref, o_ref, xt_ref):
        @pl.when(pl.program_id(1) == 0)
        def _():
            xt_ref[...] = x_ref[...].T

        o_ref[...] = _nn(xt_ref[...], dy_ref[...]).astype(BF16)

    return pl.pallas_call(
        body, name=name, grid=(M // tm, N // tn),
        in_specs=[pl.BlockSpec((S, tm), lambda i, j: (0, i)), pl.BlockSpec((S, tn), lambda i, j: (0, j))],
        out_specs=pl.BlockSpec((tm, tn), lambda i, j: (i, j)), out_shape=jax.ShapeDtypeStruct((M, N), BF16),
        scratch_shapes=[pltpu.VMEM((tm, S), BF16)], compiler_params=_cparams("parallel", "arbitrary"),
    )(x, dy)


def _proj_attn(x, g, w, *, name, tm=1024, tn=512):
    S, D = x.shape
    N = w.shape[1]

    def body(x_ref, g_ref, w_ref, u_ref, o_ref):
        @pl.when(pl.program_id(1) == 0)
        def _():
            xv = x_ref[...]
            u_ref[...] = (xv * _rstd(xv) * g_ref[...]).astype(BF16)

        o_ref[...] = _nn(u_ref[...], w_ref[...]).astype(BF16)

    return pl.pallas_call(
        body, name=name, grid=(S // tm, N // tn),
        in_specs=[pl.BlockSpec((tm, D), lambda i, j: (i, 0)), pl.BlockSpec((1, D), lambda i, j: (0, 0)),
                  pl.BlockSpec((D, tn), lambda i, j: (0, j))],
        out_specs=[pl.BlockSpec((tm, D), lambda i, j: (i, 0)), pl.BlockSpec((tm, tn), lambda i, j: (i, j))],
        out_shape=[jax.ShapeDtypeStruct((S, D), BF16), jax.ShapeDtypeStruct((S, N), BF16)],
        compiler_params=_cparams("parallel", "arbitrary"),
    )(x, g, w)


PAIR_W = 3 * LANES
ATTN_UNROLL_FWD, ATTN_UNROLL_BWD = 4, 4


def _attn_masks(first):
    qi = lax.broadcasted_iota(jnp.int32, (ATTN_BLOCK, 2 * ATTN_BLOCK), 0)
    kj = lax.broadcasted_iota(jnp.int32, (ATTN_BLOCK, 2 * ATTN_BLOCK), 1)
    dist = qi + ATTN_BLOCK - kj
    valid = (dist >= 0) & (dist <= ATTN_BLOCK) & jnp.logical_or(kj >= ATTN_BLOCK, jnp.logical_not(first))
    lane = lax.broadcasted_iota(jnp.int32, (1, LANES), 1)
    return valid, lane


def _for_residue_blocks(S, d, fn):
    span = ATTN_BLOCK * d
    nb = S // span

    def step(n, carry):
        base = pl.multiple_of(n * span, span)
        for r in range(d):
            off = pl.multiple_of((r * nb + n) * ATTN_BLOCK, ATTN_BLOCK)
            fn(lambda ref, r=r: _block_rows(ref, base, r, d),
               lambda ref, val, r=r: _set_block_rows(ref, base, r, d, val), off)
        return carry

    lax.fori_loop(0, nb, step, 0)


def _for_blocks(S, unroll, fn):
    def step(i, carry):
        fn([(pl.multiple_of((i * unroll + u) * ATTN_BLOCK, ATTN_BLOCK), i * unroll + u) for u in range(unroll)])
        return carry

    lax.fori_loop(0, S // ATTN_BLOCK // unroll, step, 0)


def _head_value(x2, lane, e):
    return jnp.sum(jnp.where(lane == HEAD_DIM * e, x2, 0.0), axis=-1, keepdims=True)


def _block_rows(ref, base, r, d):
    if d == 1:
        return ref[pl.ds(base, ATTN_BLOCK), :]
    return ref.at[pl.ds(base, ATTN_BLOCK * d)][pl.ds(r, ATTN_BLOCK, stride=d), :]


def _set_block_rows(ref, base, r, d, val):
    if d == 1:
        ref[pl.ds(base, ATTN_BLOCK), :] = val
    else:
        ref.at[pl.ds(base, ATTN_BLOCK * d)][pl.ds(r, ATTN_BLOCK, stride=d), :] = val


def _order4_to_16(src, dst, pad):
    S = src.shape[0]
    q4, q16 = S // 4, S // 16
    for r in range(4):
        for a in range(4):
            for n in range(q16 // ATTN_BLOCK):
                rows = src.at[pl.ds(r * q4 + 4 * ATTN_BLOCK * n, 4 * ATTN_BLOCK)][pl.ds(a, ATTN_BLOCK, stride=4), :]
                dst[pl.ds(pad + (4 * a + r) * q16 + ATTN_BLOCK * n, ATTN_BLOCK), :] = rows.astype(dst.dtype)


def _order16_to_4(src, pad, dst):
    S = dst.shape[0]
    q4, q16 = S // 4, S // 16
    for r in range(4):
        for a in range(4):
            for n in range(q16 // ATTN_BLOCK):
                rows = src[pl.ds(pad + (4 * a + r) * q16 + ATTN_BLOCK * n, ATTN_BLOCK), :]
                dst.at[pl.ds(r * q4 + 4 * ATTN_BLOCK * n, 4 * ATTN_BLOCK)][pl.ds(a, ATTN_BLOCK, stride=4), :] = rows


def _regroup(S, d, pairs, tmp):
    for src, dst, pad in pairs:
        if d == 16:
            def to_tmp(rows, _, off, src=src):
                tmp[pl.ds(off, ATTN_BLOCK), :] = rows(src)

            _for_residue_blocks(S, 4, to_tmp)
            _order4_to_16(tmp, dst, pad)
    if d != 16:
        def to_dst(rows, _, off):
            for src, dst, pad in pairs:
                dst[pl.ds(pad + off, ATTN_BLOCK), :] = rows(src).astype(dst.dtype)

        _for_residue_blocks(S, d, to_dst)


def _split_pair(p_ref, qs, ks, vs, bk, bv):
    qs[...] = p_ref[:, 0:LANES].astype(F32)
    ks[...] = p_ref[:, LANES:2 * LANES].astype(F32)
    vs[...] = p_ref[:, 2 * LANES:3 * LANES].astype(F32)
    bk[0:ATTN_BLOCK, :] = jnp.zeros((ATTN_BLOCK, LANES), bk.dtype)
    bv[0:ATTN_BLOCK, :] = jnp.zeros((ATTN_BLOCK, LANES), bv.dtype)


def _attn_fwd(proj_a, *, name):
    S = proj_a.shape[0]

    def body(p_ref, o_ref, l_ref, qs, ks, vs, bq, bk, bv, bo, bl, to, tl):
        _split_pair(p_ref, qs, ks, vs, bk, bv)
        for d in DILATIONS:
            nb = S // (ATTN_BLOCK * d)
            _regroup(S, d, ((qs, bq, 0), (ks, bk, ATTN_BLOCK), (vs, bv, ATTN_BLOCK)), to)

            def blocks(group, nb=nb):
                lane = lax.broadcasted_iota(jnp.int32, (1, LANES), 1)
                heads = [(lane >= HEAD_DIM * e) & (lane < HEAD_DIM * (e + 1)) for e in range(LANES // HEAD_DIM)]
                wins = [pl.ds(off, 2 * ATTN_BLOCK) for off, _ in group]
                s = [[_nt(jnp.where(mh, bq[pl.ds(off, ATTN_BLOCK), :], jnp.zeros((ATTN_BLOCK, LANES), BF16)), bk[win, :])
                      for mh in heads] for (off, _), win in zip(group, wins)]
                p, m, l = [], [], []
                for (off, b), su in zip(group, s):
                    valid, _ = _attn_masks(jnp.bitwise_and(b, nb - 1) == 0)
                    sm = [jnp.where(valid, x * ATTN_SCALE, -jnp.inf) for x in su]
                    m.append([jnp.max(x, axis=-1, keepdims=True) for x in sm])
                    p.append([jnp.exp(x - mx) for x, mx in zip(sm, m[-1])])
                    l.append([jnp.sum(x, axis=-1, keepdims=True) for x in p[-1]])
                o = [[_nn(x.astype(BF16), bv[win, :]) for x in pu] for pu, win in zip(p, wins)]
                for (off, _), ou, mu, lu in zip(group, o, m, l):
                    o2 = jnp.zeros((ATTN_BLOCK, LANES), F32)
                    l2 = jnp.zeros((ATTN_BLOCK, LANES), F32)
                    for mh, oe, me_, le in zip(heads, ou, mu, lu):
                        o2 = jnp.where(mh, oe / le, o2)
                        l2 = jnp.where(mh, me_ + jnp.log(le), l2)
                    bo[pl.ds(off, ATTN_BLOCK), :] = o2
                    bl[pl.ds(off, ATTN_BLOCK), :] = l2

            _for_blocks(S, ATTN_UNROLL_FWD, blocks)

            if d == 16:
                _order16_to_4(bo, 0, to)
                _order16_to_4(bl, 0, tl)
            src_o, src_l = (to, tl) if d == 16 else (bo, bl)

            def merge(rows, set_rows, off, d=d, src_o=src_o, src_l=src_l):
                blk = pl.ds(off, ATTN_BLOCK)
                o2, l2 = src_o[blk, :], src_l[blk, :]
                if d != DILATIONS[0]:
                    lo, oo = rows(l_ref), rows(o_ref)
                    ln = jnp.maximum(lo, l2)
                    wa, wb = jnp.exp(lo - ln), jnp.exp(l2 - ln)
                    o2 = (wa * oo + wb * o2) / (wa + wb)
                    l2 = ln + jnp.log(wa + wb)
                set_rows(o_ref, o2)
                set_rows(l_ref, l2)

            _for_residue_blocks(S, min(d, 4), merge)

    slab = pl.BlockSpec((S, LANES), lambda p: (0, p))
    f32_slab, bf16_slab = pltpu.VMEM((S, LANES), F32), pltpu.VMEM((S, LANES), BF16)
    bf16_window = pltpu.VMEM((S + ATTN_BLOCK, LANES), BF16)
    return pl.pallas_call(
        body, name=name, grid=(ATTN_W // LANES,), in_specs=[pl.BlockSpec((S, PAIR_W), lambda p: (0, p))],
        out_specs=[slab, slab],
        out_shape=[jax.ShapeDtypeStruct((S, ATTN_W), F32), jax.ShapeDtypeStruct((S, ATTN_W), F32)],
        scratch_shapes=[f32_slab] * 3 + [bf16_slab, bf16_window, bf16_window] + [f32_slab] * 4,
        compiler_params=_cparams("parallel"),
    )(proj_a)


def _attn_bwd(proj_a, do, lse, delta, *, name):
    S = proj_a.shape[0]

    def body(p_ref, do_ref, lse_ref, dl_ref, o_ref, qs, ks, vs, dqs, dks, dvs, bq, bk, bv, bdo, blse, bdl, bdq, bdk, bdv,
             tmp):
        _split_pair(p_ref, qs, ks, vs, bk, bv)
        bdk[0:ATTN_BLOCK, :] = jnp.zeros((ATTN_BLOCK, LANES), F32)
        bdv[0:ATTN_BLOCK, :] = jnp.zeros((ATTN_BLOCK, LANES), F32)
        for d in DILATIONS:
            nb = S // (ATTN_BLOCK * d)
            _regroup(S, d, ((qs, bq, 0), (ks, bk, ATTN_BLOCK), (vs, bv, ATTN_BLOCK), (do_ref, bdo, 0),
                            (lse_ref, blse, 0), (dl_ref, bdl, 0)), tmp)

            def blocks(group, nb=nb):
                lane = lax.broadcasted_iota(jnp.int32, (1, LANES), 1)
                heads = [(lane >= HEAD_DIM * e) & (lane < HEAD_DIM * (e + 1)) for e in range(LANES // HEAD_DIM)]
                zero = jnp.zeros((ATTN_BLOCK, LANES), BF16)
                chains = [(off, b, e, mh) for off, b in group for e, mh in enumerate(heads)]
                qm = [jnp.where(mh, bq[pl.ds(off, ATTN_BLOCK), :], zero) for off, _, _, mh in chains]
                dom = [jnp.where(mh, bdo[pl.ds(off, ATTN_BLOCK), :], zero) for off, _, _, mh in chains]
                s = [_nt(x, bk[pl.ds(off, 2 * ATTN_BLOCK), :]) for x, (off, _, _, _) in zip(qm, chains)]
                dp = [_nt(x, bv[pl.ds(off, 2 * ATTN_BLOCK), :]) for x, (off, _, _, _) in zip(dom, chains)]
                p, ds = [], []
                for (off, b, e, _), sc, dpc in zip(chains, s, dp):
                    valid, _ = _attn_masks(jnp.bitwise_and(b, nb - 1) == 0)
                    blk = pl.ds(off, ATTN_BLOCK)
                    pc = jnp.where(valid, jnp.exp(sc * ATTN_SCALE - _head_value(blse[blk, :], lane, e)), 0.0)
                    ds.append((pc * (dpc - _head_value(bdl[blk, :], lane, e)) * ATTN_SCALE).astype(BF16))
                    p.append(pc.astype(BF16))
                dq = [_nn(x, bk[pl.ds(off, 2 * ATTN_BLOCK), :]) for x, (off, _, _, _) in zip(ds, chains)]
                dk = [_tn(x, y) for x, y in zip(ds, qm)]
                dv = [_tn(x, y) for x, y in zip(p, dom)]
                nh = len(heads)
                for u, (off, _) in enumerate(group):
                    dq2 = jnp.zeros((ATTN_BLOCK, LANES), F32)
                    for mh, x in zip(heads, dq[nh * u:nh * (u + 1)]):
                        dq2 = jnp.where(mh, x, dq2)
                    bdq[pl.ds(off, ATTN_BLOCK), :] = dq2
                    for acc, grads in ((bdk, dk), (bdv, dv)):
                        win_grad = sum(grads[nh * u + 1:nh * (u + 1)], grads[nh * u])
                        acc[pl.ds(off, ATTN_BLOCK), :] += win_grad[:ATTN_BLOCK]
                        acc[pl.ds(off + ATTN_BLOCK, ATTN_BLOCK), :] = win_grad[ATTN_BLOCK:]

            _for_blocks(S, ATTN_UNROLL_BWD, blocks)

            outs = ((dqs, bdq, 0), (dks, bdk, ATTN_BLOCK), (dvs, bdv, ATTN_BLOCK))
            if d == 16:
                for acc, grad, pad in outs:
                    _order16_to_4(grad, pad, tmp)

                    def add(rows, set_rows, off, acc=acc):
                        set_rows(acc, rows(acc) + tmp[pl.ds(off, ATTN_BLOCK), :])

                    _for_residue_blocks(S, 4, add)
            else:
                def scatter(rows, set_rows, off, d=d):
                    for acc, grad, pad in outs:
                        part = grad[pl.ds(pad + off, ATTN_BLOCK), :]
                        set_rows(acc, part if d == DILATIONS[0] else rows(acc) + part)

                _for_residue_blocks(S, d, scatter)
        o_ref[:, 0:LANES] = dqs[...].astype(BF16)
        o_ref[:, LANES:2 * LANES] = dks[...].astype(BF16)
        o_ref[:, 2 * LANES:3 * LANES] = dvs[...].astype(BF16)

    slab = pl.BlockSpec((S, LANES), lambda p: (0, p), pipeline_mode=pl.Buffered(1))
    pair = pl.BlockSpec((S, PAIR_W), lambda p: (0, p))
    f32_slab, bf16_slab = pltpu.VMEM((S, LANES), F32), pltpu.VMEM((S, LANES), BF16)
    f32_window, bf16_window = pltpu.VMEM((S + ATTN_BLOCK, LANES), F32), pltpu.VMEM((S + ATTN_BLOCK, LANES), BF16)
    return pl.pallas_call(
        body, name=name, grid=(ATTN_W // LANES,), in_specs=[pair, slab, slab, slab], out_specs=pair,
        out_shape=jax.ShapeDtypeStruct(proj_a.shape, BF16),
        scratch_shapes=[f32_slab] * 6 + [bf16_slab, bf16_window, bf16_window, bf16_slab, f32_slab, f32_slab,
                                         f32_slab, f32_window, f32_window, f32_slab],
        compiler_params=_cparams("parallel"),
    )(proj_a, do, lse, delta)


HG_T = 2 * HGRN_CHUNK


def _hgrn_consts():
    row = lax.broadcasted_iota(jnp.int32, (HG_T, HG_T), 0)
    col = lax.broadcasted_iota(jnp.int32, (HG_T, HG_T), 1)
    same = (row >= HGRN_CHUNK) == (col >= HGRN_CHUNK)
    return row, same & (col <= row), same & (col >= row)


def _lower_bound(logits_ref):
    l0, l1 = logits_ref[0:1, :], logits_ref[1:2, :]
    mx = jnp.maximum(l0, l1)
    e0, e1 = jnp.exp(l0 - mx), jnp.exp(l1 - mx)
    return e0 / (e0 + e1)


def _hgrn_gates(qs, fs, lbs, row, causal):
    C = HGRN_CHUNK
    tri = jnp.where(causal, 1.0, 0.0).astype(F32)
    sgs = [_sigmoid(f) for f in fs]
    forgets = [lb + (1.0 - lb) * sg for lb, sg in zip(lbs, sgs)]
    logfs = [jnp.log(forget) for forget in forgets]
    bs = [_nn(tri, logf, HIGHEST) for logf in logfs]
    out = []
    for q, sg, forget, logf, b in zip(qs, sgs, forgets, logfs, bs):
        key = 1.0 - forget
        bend0 = jnp.sum(logf[:C], axis=0, keepdims=True)
        bend1 = jnp.sum(logf[C:], axis=0, keepdims=True)
        bend = jnp.where(row < C, bend0, bend1)
        eb, emb, eend = jnp.exp(b), jnp.exp(-b), jnp.exp(bend - b)
        sq = _sigmoid(q)
        out.append(dict(sg=sg, forget=forget, key=key, bend0=bend0, bend1=bend1, eb=eb, emb=emb, eend=eend, sq=sq,
                        qd=q * sq * eb, ki=key * emb, ke=key * eend))
    return out


def _hgrn_fwd(proj, logits, *, name):
    S = proj.shape[0]
    W, C = HGRN_W, HGRN_CHUNK

    def body(q_ref, f_ref, i_ref, lg_ref, rec_ref, st_ref, s_ref):
        @pl.when(pl.program_id(0) == 0)
        def _():
            s_ref[...] = jnp.zeros_like(s_ref)

        row, causal, _ = _hgrn_consts()
        lb_all = _lower_bound(lg_ref)
        heads = range(HGRN_HEADS)
        sls = [slice(HGRN_DIM * h, HGRN_DIM * (h + 1)) for h in heads]
        gts = _hgrn_gates([q_ref[:, sl] for sl in sls], [f_ref[:, sl] for sl in sls], [lb_all[:, sl] for sl in sls],
                          row, causal)
        qd, ki, ke = ([gt[k].astype(BF16) for gt in gts] for k in ("qd", "ki", "ke"))
        iv = [i_ref[:, sl].astype(BF16) for sl in sls]
        s0 = [s_ref[h] for h in heads]
        a = [_nt(qd[h], ki[h]) for h in heads]
        o0 = [_nt(qd[h][:C], s0[h].astype(BF16)) for h in heads]
        u0 = [_tn(iv[h][:C], ke[h][:C]) for h in heads]
        u1 = [_tn(iv[h][C:], ke[h][C:]) for h in heads]
        s1 = [jnp.exp(gts[h]["bend0"]) * s0[h] + u0[h] for h in heads]
        o = [_nn(jnp.where(causal, a[h], 0.0).astype(BF16), iv[h]) for h in heads]
        o1 = [_nt(qd[h][C:], s1[h].astype(BF16)) for h in heads]
        for h in heads:
            st_ref[0, h] = s0[h]
            st_ref[1, h] = s1[h]
            s_ref[h] = jnp.exp(gts[h]["bend1"]) * s1[h] + u1[h]
            rec_ref[:, sls[h]] = o[h] + jnp.concatenate([o0[h], o1[h]], axis=0)

    blk = lambda j: pl.BlockSpec((HG_T, W), lambda t: (t, j))
    return pl.pallas_call(
        body, name=name, grid=(S // HG_T,),
        in_specs=[blk(0), blk(1), blk(2), pl.BlockSpec((2, W), lambda t: (0, 0))],
        out_specs=[blk(0), pl.BlockSpec((2, HGRN_HEADS, HGRN_DIM, HGRN_DIM), lambda t: (t, 0, 0, 0))],
        out_shape=[jax.ShapeDtypeStruct((S, W), F32),
                   jax.ShapeDtypeStruct((S // C, HGRN_HEADS, HGRN_DIM, HGRN_DIM), F32)],
        scratch_shapes=[pltpu.VMEM((HGRN_HEADS, HGRN_DIM, HGRN_DIM), F32)],
        compiler_params=_cparams("arbitrary"),
    )(proj, proj, proj, logits)


def _hgrn_bwd(proj, logits, states, drec, *, name):
    S = proj.shape[0]
    W, C = HGRN_W, HGRN_CHUNK
    nt = S // HG_T

    def body(q_ref, f_ref, i_ref, lg_ref, st_ref, do_ref, dp_ref, dlg_ref, ds_ref, dlb_ref):
        t = pl.program_id(0)

        @pl.when(t == 0)
        def _():
            ds_ref[...] = jnp.zeros_like(ds_ref)
            dlb_ref[...] = jnp.zeros_like(dlb_ref)

        row, causal, anti = _hgrn_consts()
        lb_all = _lower_bound(lg_ref)
        heads = range(HGRN_HEADS)
        sls = [slice(HGRN_DIM * h, HGRN_DIM * (h + 1)) for h in heads]
        qs, lbs = [q_ref[:, sl] for sl in sls], [lb_all[:, sl] for sl in sls]
        gts = _hgrn_gates(qs, [f_ref[:, sl] for sl in sls], lbs, row, causal)
        qd, ki, ke = ([gt[k] for gt in gts] for k in ("qd", "ki", "ke"))
        qdb, kib, keb = ([x.astype(BF16) for x in xs] for xs in (qd, ki, ke))
        iv = [i_ref[:, sl].astype(BF16) for sl in sls]
        dob = [do_ref[:, sl].astype(BF16) for sl in sls]
        s0, s1, ds1 = [st_ref[0, h] for h in heads], [st_ref[1, h] for h in heads], [ds_ref[h] for h in heads]
        ds1b = [x.astype(BF16) for x in ds1]
        dec0, dec1 = [jnp.exp(gt["bend0"]) for gt in gts], [jnp.exp(gt["bend1"]) for gt in gts]
        a = [_nt(qdb[h], kib[h]) for h in heads]
        da = [_nt(dob[h], iv[h]) for h in heads]
        dqd1 = [_nn(dob[h][C:], s1[h].astype(BF16)) for h in heads]
        dqd0 = [_nn(dob[h][:C], s0[h].astype(BF16)) for h in heads]
        di1 = [_nt(keb[h][C:], ds1b[h]) for h in heads]
        dke1 = [_nn(iv[h][C:], ds1b[h]) for h in heads]
        t1 = [_tn(dob[h][C:], qdb[h][C:]) for h in heads]
        t0 = [_tn(dob[h][:C], qdb[h][:C]) for h in heads]
        ds0 = [dec1[h] * ds1[h] + t1[h] for h in heads]
        ds0b = [x.astype(BF16) for x in ds0]
        a = [jnp.where(causal, x, 0.0).astype(BF16) for x in a]
        da = [jnp.where(causal, x, 0.0).astype(BF16) for x in da]
        di0 = [_nt(keb[h][:C], ds0b[h]) for h in heads]
        dke0 = [_nn(iv[h][:C], ds0b[h]) for h in heads]
        dqd_a = [_nn(da[h], kib[h]) for h in heads]
        dki = [_tn(da[h], qdb[h]) for h in heads]
        di_a = [_tn(a[h], dob[h]) for h in heads]
        dqd, dke, db = [], [], []
        for h in heads:
            ds_ref[h] = dec0[h] * ds0[h] + t0[h]
            ddec1 = jnp.sum(ds1[h] * s1[h], axis=0, keepdims=True)
            ddec0 = jnp.sum(ds0[h] * s0[h], axis=0, keepdims=True)
            dqd.append(dqd_a[h] + jnp.concatenate([dqd0[h], dqd1[h]], axis=0))
            dp_ref[:, 2 * W + HGRN_DIM * h:2 * W + HGRN_DIM * (h + 1)] = (
                di_a[h] + jnp.concatenate([di0[h], di1[h]], axis=0)).astype(BF16)
            dke.append(jnp.concatenate([dke0[h], dke1[h]], axis=0))
            gke = dke[h] * ke[h]
            dbend0 = jnp.sum(gke[:C], axis=0, keepdims=True) + ddec0 * dec0[h]
            dbend1 = jnp.sum(gke[C:], axis=0, keepdims=True) + ddec1 * dec1[h]
            dbh = dqd[h] * qd[h] - dki[h] * ki[h] - gke
            db.append(dbh + jnp.where(row == C - 1, dbend0, 0.0) + jnp.where(row == HG_T - 1, dbend1, 0.0))
        tri = jnp.where(anti, 1.0, 0.0).astype(F32)
        dlogf = [_nn(tri, db[h], HIGHEST) for h in heads]
        for h in heads:
            gt, lb, q = gts[h], lbs[h], qs[h]
            dforget = dlogf[h] / gt["forget"] - (dki[h] * gt["emb"] + dke[h] * gt["eend"])
            sg, sq = gt["sg"], gt["sq"]
            dp_ref[:, W + HGRN_DIM * h:W + HGRN_DIM * (h + 1)] = (dforget * (1.0 - lb) * sg * (1.0 - sg)).astype(BF16)
            dlb_ref[:, sls[h]] += jnp.sum(dforget * (1.0 - sg), axis=0, keepdims=True)
            dp_ref[:, sls[h]] = (dqd[h] * gt["eb"] * sq * (1.0 + q * (1.0 - sq))).astype(BF16)

        @pl.when(t == nt - 1)
        def _():
            dl0 = dlb_ref[...] * lb_all * (1.0 - lb_all)
            dlg_ref[0:1, :] = dl0
            dlg_ref[1:2, :] = -dl0

    blk = lambda j: pl.BlockSpec((HG_T, W), lambda t: (nt - 1 - t, j))
    full = pl.BlockSpec((2, W), lambda t: (0, 0))
    return pl.pallas_call(
        body, name=name, grid=(nt,),
        in_specs=[blk(0), blk(1), blk(2), full,
                  pl.BlockSpec((2, HGRN_HEADS, HGRN_DIM, HGRN_DIM), lambda t: (nt - 1 - t, 0, 0, 0)), blk(0)],
        out_specs=[pl.BlockSpec((HG_T, 3 * W), lambda t: (nt - 1 - t, 0)), full],
        out_shape=[jax.ShapeDtypeStruct((S, 3 * W), BF16), jax.ShapeDtypeStruct((2, W), F32)],
        scratch_shapes=[pltpu.VMEM((HGRN_HEADS, HGRN_DIM, HGRN_DIM), F32), pltpu.VMEM((1, W), F32)],
        compiler_params=_cparams("arbitrary"),
    )(proj, proj, proj, logits, states, drec)


def _out_proj(attn, rec, proj_h, x, g_attn, g_hgrn, g_norm2, w_out, *, name, tm=512):
    S, D = x.shape
    AW, W = ATTN_W, HGRN_W

    def body(a_ref, r_ref, hg_ref, x_ref, ga_ref, gh_ref, g2_ref, w_ref, h_ref, u_ref, m_ref):
        av = a_ref[...]
        m_ref[:, :AW] = (av * _rstd(av) * ga_ref[...]).astype(BF16)
        for h in range(HGRN_HEADS):
            sl = slice(HGRN_DIM * h, HGRN_DIM * (h + 1))
            rv, hg = r_ref[:, sl], hg_ref[:, sl]
            m_ref[:, AW + HGRN_DIM * h:AW + HGRN_DIM * (h + 1)] = (
                (rv * _rstd(rv) * gh_ref[:, sl]) * (hg * _sigmoid(hg))).astype(BF16)
        h1 = x_ref[...] + _nn(m_ref[...], w_ref[...])
        h_ref[...] = h1
        u_ref[...] = (h1 * _rstd(h1) * g2_ref[...]).astype(BF16)

    row = lambda w, j=0: pl.BlockSpec((tm, w), lambda i: (i, j))
    vec = lambda w: pl.BlockSpec((1, w), lambda i: (0, 0))
    return pl.pallas_call(
        body, name=name, grid=(S // tm,),
        in_specs=[row(AW), row(W), row(W, 3), row(D), vec(AW), vec(W), vec(D), _resident(w_out.shape)],
        out_specs=[row(D), row(D), row(AW + W)],
        out_shape=[jax.ShapeDtypeStruct((S, D), F32), jax.ShapeDtypeStruct((S, D), BF16),
                   jax.ShapeDtypeStruct((S, AW + W), BF16)],
        compiler_params=_cparams("parallel"),
    )(attn, rec, proj_h, x, g_attn, g_hgrn, g_norm2, w_out)


def _dmix_post_bwd(dh1b, w_out, attn, rec, proj_h, g_attn, g_hgrn, *, name, tm=512):
    S, D = dh1b.shape
    AW, W = ATTN_W, HGRN_W

    def body(dh_ref, w_ref, a_ref, r_ref, hg_ref, ga_ref, gh_ref, do_ref, dl_ref, dr_ref, dhg_ref, dga_ref, dgh_ref):
        first = pl.program_id(0) == 0
        dmix = _nt(dh_ref[...], w_ref[...])
        av = a_ref[...]
        dov, dga = _norm_bwd(av, ga_ref[...], dmix[:, :AW])
        do_ref[...] = dov
        shift = HEAD_DIM.bit_length() - 1
        hi = lax.shift_right_logical(lax.broadcasted_iota(jnp.int32, (AW, AW), 0), shift)
        hj = lax.shift_right_logical(lax.broadcasted_iota(jnp.int32, (AW, AW), 1), shift)
        prod = dov * av
        hi_part = prod.astype(BF16)
        lo_part = (prod - hi_part.astype(F32)).astype(BF16)
        same_head = jnp.where(hi == hj, 1.0, 0.0).astype(BF16)
        dl_ref[...] = _nn(hi_part, same_head) + _nn(lo_part, same_head)
        _accumulate(dga_ref, jnp.sum(dga, axis=0, keepdims=True), first)

        @pl.when(first)
        def _():
            dgh_ref[...] = jnp.zeros_like(dgh_ref)

        for h in range(HGRN_HEADS):
            sl = slice(HGRN_DIM * h, HGRN_DIM * (h + 1))
            rv, hg, gv = r_ref[:, sl], hg_ref[:, sl], gh_ref[:, sl]
            dout = dmix[:, AW + HGRN_DIM * h:AW + HGRN_DIM * (h + 1)]
            sg = _sigmoid(hg)
            drv, dgh = _norm_bwd(rv, gv, dout * (hg * sg))
            dr_ref[:, sl] = drv
            dgh_ref[:, sl] += jnp.sum(dgh, axis=0, keepdims=True)
            dhg_ref[:, sl] = (dout * (rv * _rstd(rv) * gv) * (sg * (1.0 + hg * (1.0 - sg)))).astype(BF16)

    row = lambda w, j=0: pl.BlockSpec((tm, w), lambda i: (i, j))
    vec = lambda w: pl.BlockSpec((1, w), lambda i: (0, 0))
    return pl.pallas_call(
        body, name=name, grid=(S // tm,),
        in_specs=[row(D), _resident(w_out.shape), row(AW), row(W), row(W, 3), vec(AW), vec(W)],
        out_specs=[row(AW), row(AW), row(W), row(W), vec(AW), vec(W)],
        out_shape=[jax.ShapeDtypeStruct((S, AW), F32), jax.ShapeDtypeStruct((S, AW), F32),
                   jax.ShapeDtypeStruct((S, W), F32), jax.ShapeDtypeStruct((S, W), BF16),
                   jax.ShapeDtypeStruct((1, AW), F32), jax.ShapeDtypeStruct((1, W), F32)],
        compiler_params=_cparams("arbitrary"),
    )(dh1b, w_out, attn, rec, proj_h, g_attn, g_hgrn)


def _conv_act(g, g1, g2, w_ref, b_ref):
    c = b_ref[...] + w_ref[0:1, :] * g2 + w_ref[1:2, :] * g1 + w_ref[2:3, :] * g
    return c, 0.5 * (1.0 + lax.erf(c * (2.0 ** -0.5)))


def _shift_down(g, halo, row):
    g1 = jnp.where(row == 0, halo[7:8], pltpu.roll(g, 1, 0))
    g2 = jnp.where(row == 0, halo[6:7], jnp.where(row == 1, halo[7:8], pltpu.roll(g, 2, 0)))
    return g1, g2


def _shift_up(x, halo, row):
    n = x.shape[0]
    x1 = jnp.where(row == n - 1, halo[0:1], pltpu.roll(x, n - 1, 0))
    x2 = jnp.where(row == n - 2, halo[0:1], jnp.where(row == n - 1, halo[1:2], pltpu.roll(x, n - 2, 0)))
    return x1, x2


def _up_glu(u, w_up, conv_w, conv_b, *, name, tm=1024, tn=256):
    S, D = u.shape
    F = w_up.shape[1] // 2
    nf = F // tn

    def body(u_ref, wg_ref, wv_ref, cw_ref, cb_ref, g_ref, v_ref, a_ref, halo_ref):
        i, j = pl.program_id(0), pl.program_id(1)

        @pl.when(i == 0)
        def _():
            halo_ref[j] = jnp.zeros((SUBLANES, tn), F32)

        uv = u_ref[...]
        g, v = _nn(uv, wg_ref[...]), _nn(uv, wv_ref[...])
        row = lax.broadcasted_iota(jnp.int32, (tm, tn), 0)
        g1, g2 = _shift_down(g, halo_ref[j], row)
        c, cdf = _conv_act(g, g1, g2, cw_ref, cb_ref)
        a_ref[...] = (c * cdf * v).astype(BF16)
        g_ref[...] = g.astype(BF16)
        v_ref[...] = v.astype(BF16)
        halo_ref[j] = g[tm - SUBLANES:, :]

    col = pl.BlockSpec((tm, tn), lambda i, j: (i, j))
    out = jax.ShapeDtypeStruct((S, F), BF16)
    return pl.pallas_call(
        body, name=name, grid=(S // tm, nf),
        in_specs=[pl.BlockSpec((tm, D), lambda i, j: (i, 0)), pl.BlockSpec((D, tn), lambda i, j: (0, j)),
                  pl.BlockSpec((D, tn), lambda i, j: (0, j + nf)), pl.BlockSpec((3, tn), lambda i, j: (0, j)),
                  pl.BlockSpec((1, tn), lambda i, j: (0, j))],
        out_specs=[col, col, col], out_shape=[out, out, out],
        scratch_shapes=[pltpu.VMEM((nf, SUBLANES, tn), F32)], compiler_params=_cparams("arbitrary", "arbitrary"),
    )(u, w_up, w_up, conv_w, conv_b)


def _dact_glu_bwd(dh2b, w_down, gate, val, conv_w, conv_b, *, name, tm=1024, tn=256):
    S, D = dh2b.shape
    F = gate.shape[1]
    nf, ni = F // tn, S // tm
    hb = tm // SUBLANES

    def body(dh_ref, wd_ref, g_ref, gh_ref, v_ref, cw_ref, cb_ref, dg_ref, dv_ref, dcw_ref, dcb_ref, halo_ref, acc_ref):
        i, j = pl.program_id(0), pl.program_id(1)

        @pl.when(i == 0)
        def _():
            halo_ref[j] = jnp.zeros((SUBLANES, tn), F32)
            acc_ref[j] = jnp.zeros((SUBLANES, tn), F32)

        g = g_ref[...].astype(F32)
        before = jnp.where(i < ni - 1, gh_ref[...].astype(F32), 0.0)
        row = lax.broadcasted_iota(jnp.int32, (tm, tn), 0)
        g1, g2 = _shift_down(g, before[SUBLANES:], row)
        c, cdf = _conv_act(g, g1, g2, cw_ref, cb_ref)
        da = _nt(dh_ref[...], wd_ref[...])
        dv_ref[...] = (da * (c * cdf)).astype(BF16)
        pdf = jnp.exp(-0.5 * c * c) * (1.0 / (2.0 * jnp.pi) ** 0.5)
        dc = da * v_ref[...].astype(F32) * (cdf + c * pdf)
        d1, d2 = _shift_up(dc, halo_ref[j], row)
        dg_ref[...] = (cw_ref[2:3, :] * dc + cw_ref[1:2, :] * d1 + cw_ref[0:1, :] * d2).astype(BF16)
        halo_ref[j] = dc[:SUBLANES, :]
        for k, t in enumerate((dc * g2, dc * g1, dc * g, dc)):
            acc_ref[j, k:k + 1, :] += jnp.sum(t, axis=0, keepdims=True)

        @pl.when((i == ni - 1) & (j == nf - 1))
        def _():
            for jj in range(nf):
                dcw_ref[:, jj * tn:(jj + 1) * tn] = acc_ref[jj, 0:3, :]
                dcb_ref[:, jj * tn:(jj + 1) * tn] = acc_ref[jj, 3:4, :]

    tile = pl.BlockSpec((tm, tn), lambda i, j: (ni - 1 - i, j))
    return pl.pallas_call(
        body, name=name, grid=(ni, nf),
        in_specs=[pl.BlockSpec((tm, D), lambda i, j: (ni - 1 - i, 0)), pl.BlockSpec((tn, D), lambda i, j: (j, 0)),
                  tile, pl.BlockSpec((SUBLANES * 2, tn), lambda i, j: (jnp.maximum((ni - 1 - i) * (hb // 2) - 1, 0), j)),
                  tile, pl.BlockSpec((3, tn), lambda i, j: (0, j)), pl.BlockSpec((1, tn), lambda i, j: (0, j))],
        out_specs=[tile, tile, pl.BlockSpec((3, F), lambda i, j: (0, 0)), pl.BlockSpec((1, F), lambda i, j: (0, 0))],
        out_shape=[jax.ShapeDtypeStruct((S, F), BF16), jax.ShapeDtypeStruct((S, F), BF16),
                   jax.ShapeDtypeStruct((3, F), F32), jax.ShapeDtypeStruct((1, F), F32)],
        scratch_shapes=[pltpu.VMEM((nf, SUBLANES, tn), F32), pltpu.VMEM((nf, SUBLANES, tn), F32)],
        compiler_params=_cparams("arbitrary", "arbitrary"),
    )(dh2b, w_down, gate, gate, val, conv_w, conv_b)


def _down_loss(act, w_down, h1, g, target, *, name, tm=512):
    S, F = act.shape
    D = h1.shape[1]

    def body(a_ref, w_ref, h_ref, g_ref, t_ref, dh_ref, dhb_ref, dg_ref, loss_ref):
        first = pl.program_id(0) == 0
        h2 = h_ref[...] + _nn(a_ref[...], w_ref[...])
        gv = g_ref[...]
        r = _rstd(h2)
        xh = h2 * r
        err = xh * gv - t_ref[...]
        part_loss = 0.5 * jnp.sum(jnp.mean(err * err, axis=-1, keepdims=True), axis=0, keepdims=True)
        dy = err * (1.0 / D)
        dxh = dy * gv
        dh = r * (dxh - xh * jnp.mean(dxh * xh, axis=-1, keepdims=True))
        dh_ref[...] = dh
        dhb_ref[...] = dh.astype(BF16)
        _accumulate(dg_ref, jnp.sum(dy * xh, axis=0, keepdims=True), first)
        _accumulate(loss_ref, jnp.broadcast_to(part_loss, (1, LANES)), first)

    row = lambda w: pl.BlockSpec((tm, w), lambda i: (i, 0))
    vec = lambda w: pl.BlockSpec((1, w), lambda i: (0, 0))
    return pl.pallas_call(
        body, name=name, grid=(S // tm,), in_specs=[row(F), _resident(w_down.shape), row(D), vec(D), row(D)],
        out_specs=[row(D), row(D), vec(D), vec(LANES)],
        out_shape=[jax.ShapeDtypeStruct((S, D), F32), jax.ShapeDtypeStruct((S, D), BF16),
                   jax.ShapeDtypeStruct((1, D), F32), jax.ShapeDtypeStruct((1, LANES), F32)],
        compiler_params=_cparams("arbitrary"),
    )(act, w_down, h1, g, target)


def _grad_norm_input(pieces, ws, x, g, add, *, name, tm=512):
    S, D = x.shape
    widths = [p.shape[1] for p in pieces]
    n, nw = len(pieces), len(ws)
    where, wi, off = [], 0, 0
    for wd in widths:
        if off == ws[wi].shape[1]:
            wi, off = wi + 1, 0
        where.append((wi, off))
        off += wd

    def body(*refs):
        p_refs, w_refs = refs[:n], refs[n:n + nw]
        x_ref, g_ref, add_ref, dx_ref, dxb_ref, dg_ref = refs[n + nw:]
        halves = _row_halves(tm)
        du = []
        for rows in halves:
            terms = [_nt(p_refs[k][rows, :], w_refs[wi][:, off:off + widths[k]]) for k, (wi, off) in enumerate(where)]
            du.append(sum(terms[1:], terms[0]))
        dg_sum = None
        for rows, duh in zip(halves, du):
            dx, dg = _norm_bwd(x_ref[rows, :], g_ref[...], duh)
            dx = add_ref[rows, :] + dx
            dx_ref[rows, :] = dx
            dxb_ref[rows, :] = dx.astype(BF16)
            part = jnp.sum(dg, axis=0, keepdims=True)
            dg_sum = part if dg_sum is None else dg_sum + part
        _accumulate(dg_ref, dg_sum, pl.program_id(0) == 0)

    row = lambda w_: pl.BlockSpec((tm, w_), lambda i: (i, 0))
    vec = pl.BlockSpec((1, D), lambda i: (0, 0))
    return pl.pallas_call(
        body, name=name, grid=(S // tm,),
        in_specs=[row(wd) for wd in widths] + [_resident(w.shape) for w in ws] + [row(D), vec, row(D)],
        out_specs=[row(D), row(D), vec],
        out_shape=[jax.ShapeDtypeStruct((S, D), F32), jax.ShapeDtypeStruct((S, D), BF16),
                   jax.ShapeDtypeStruct((1, D), F32)],
        compiler_params=_cparams("arbitrary"),
    )(*pieces, *ws, x, g, add)


def _rows(a):
    return a.reshape(-1, a.shape[-1])


def _row_tile(rows, cols, itemsize=4, budget=1 << 20):
    t = rows
    while t % 32 == 0 and t * cols * itemsize > budget:
        t //= 2
    return t


def _sum_cast(arrs, out_dtype, *, name):
    shape = arrs[0].shape
    flat = [_rows(a) for a in arrs]
    R, C = flat[0].shape
    tr = _row_tile(R, C)

    def body(*refs):
        acc = refs[0][...].astype(F32)
        for r in refs[1:-1]:
            acc = acc + r[...].astype(F32)
        refs[-1][...] = acc.astype(out_dtype)

    spec = pl.BlockSpec((tr, C), lambda i: (i, 0))
    return pl.pallas_call(
        body, name=name, grid=(R // tr,), in_specs=[spec] * len(flat), out_specs=spec,
        out_shape=jax.ShapeDtypeStruct((R, C), out_dtype), compiler_params=_cparams("parallel"),
    )(*flat).reshape(shape)


def _adamw(parts, w, m, v, *, name):
    shape = w.shape
    w2, m2, v2 = _rows(w), _rows(m), _rows(v)
    R, C = w2.shape
    parts = [p.reshape(-1, R, C) for p in parts]
    tr = _row_tile(R, C)
    np_ = len(parts)
    c1, c2 = 1.0 - ADAM_B1 ** ADAM_STEP, 1.0 - ADAM_B2 ** ADAM_STEP

    def body(*refs):
        terms = [(r, k) for r in refs[:np_] for k in range(r.shape[0])]
        g = terms[0][0][terms[0][1]].astype(F32)
        for r, k in terms[1:]:
            g = g + r[k].astype(F32)
        w_ref, m_ref, v_ref, g_out, d_out, m_out, v_out = refs[np_:]
        mn = ADAM_B1 * m_ref[...] + (1.0 - ADAM_B1) * g
        vn = ADAM_B2 * v_ref[...] + (1.0 - ADAM_B2) * (g * g)
        g_out[...] = g
        d_out[...] = -ADAM_LR * ((mn / c1) / (jnp.sqrt(vn / c2) + ADAM_EPS) + ADAM_WD * w_ref[...])
        m_out[...] = mn
        v_out[...] = vn

    spec = pl.BlockSpec((tr, C), lambda i: (i, 0))
    out = jax.ShapeDtypeStruct((R, C), F32)
    stacks = [pl.BlockSpec((p.shape[0], tr, C), lambda i: (0, i, 0)) for p in parts]
    res = pl.pallas_call(
        body, name=name, grid=(R // tr,), in_specs=stacks + [spec] * 3, out_specs=[spec] * 4,
        out_shape=[out] * 4, compiler_params=_cparams("parallel"),
    )(*parts, w2, m2, v2)
    return [r.reshape(shape) for r in res]


def _coords():
    return lax.axis_index("x"), lax.axis_index("y"), lax.axis_index("c")


def _all_gather(shards, *, name):
    n = len(shards)

    def body(*refs):
        x_refs, out_refs = refs[:n], refs[n:2 * n]
        send_sems, recv_sems, local_sems = refs[2 * n:]
        x, y, c = _coords()
        me, sibling = (x, y, c), (x, y, 1 - c)
        chips = [(1 - x, y), (x, 1 - y), (1 - x, 1 - y)]

        def slot(a, dev):
            return out_refs[a].at[4 * dev[0] + 2 * dev[1] + dev[2]]

        def copy(a, k, block, to, src=None):
            return pltpu.make_async_remote_copy(
                src_ref=slot(a, block) if src is None else src, dst_ref=slot(a, block),
                send_sem=send_sems.at[7 * a + k], recv_sem=recv_sems.at[7 * a + k], device_id=to, device_id_type=MESH)

        mine = [pltpu.make_async_copy(x_refs[a], slot(a, me), local_sems.at[a]) for a in range(n)]
        for cp in mine:
            cp.start()
        first = []
        for a in range(n):
            first.append(copy(a, 0, me, sibling, src=x_refs[a]))
            first += [copy(a, 1 + j, me, (*chip, c), src=x_refs[a]) for j, chip in enumerate(chips)]
        for cp in first:
            cp.start()
        passed = []
        for j, chip in enumerate(chips):
            for a in range(n):
                copy(a, 1 + j, (*chip, c), me).wait_recv()
                fwd = copy(a, 4 + j, (*chip, c), sibling)
                fwd.start()
                passed.append(fwd)
        for a in range(n):
            copy(a, 0, sibling, me).wait_recv()
            for j, chip in enumerate(chips):
                copy(a, 4 + j, (*chip, 1 - c), me).wait_recv()
        for cp in first + passed:
            cp.wait_send()
        for cp in mine:
            cp.wait()

    return pl.pallas_call(
        body, name=name, in_specs=[HBM] * n, out_specs=[HBM] * n,
        out_shape=[jax.ShapeDtypeStruct((N_DEV, *s.shape), s.dtype) for s in shards],
        scratch_shapes=[pltpu.SemaphoreType.DMA((7 * n,)), pltpu.SemaphoreType.DMA((7 * n,)),
                        pltpu.SemaphoreType.DMA((n,))],
    )(*shards)


def _flip_y(x, y, c):
    return (x, 1 - y, c)


def _flip_x(x, y, c):
    return (1 - x, y, c)


def _flip_xy(x, y, c):
    return (1 - x, 1 - y, c)


SEM = pl.BlockSpec(memory_space=pltpu.SEMAPHORE)
SIDE_EFFECT = pltpu.SideEffectType.DATAFLOW_SIDE_EFFECTING


def _in_hbm(a):
    return pltpu.with_memory_space_constraint(a, pltpu.HBM)


def _copies_start(srcs, lands, plan, n_copies, *, name):
    ns, nl = len(srcs), len(lands)

    def body(*refs):
        src_refs, land_refs = refs[:ns], refs[ns:ns + nl]
        send_sems, recv_sems = refs[ns + nl:ns + nl + 2]
        token = refs[-1]
        for k, (src, dst, peer, _) in enumerate(plan(src_refs, land_refs, *_coords())):
            pltpu.make_async_remote_copy(src_ref=src, dst_ref=dst, send_sem=send_sems.at[k], recv_sem=recv_sems.at[k],
                                         device_id=peer, device_id_type=MESH).start()
        token[...] = jnp.zeros_like(token)

    bufs = [*srcs, *lands]
    res = pl.pallas_call(
        body, name=name, in_specs=[HBM] * (ns + nl),
        out_specs=(SEM, SEM, *[HBM] * (ns + nl), pl.BlockSpec(memory_space=pltpu.VMEM)),
        out_shape=(pltpu.SemaphoreType.DMA((n_copies,)), pltpu.SemaphoreType.DMA((n_copies,)),
                   *[pltpu.HBM(b.shape, b.dtype) for b in bufs], jax.ShapeDtypeStruct((SUBLANES, LANES), F32)),
        input_output_aliases={i: 2 + i for i in range(ns + nl)},
        compiler_params=pltpu.CompilerParams(has_side_effects=SIDE_EFFECT),
    )(*[_in_hbm(b) for b in bufs])
    return res[0], res[1], list(res[2:2 + ns]), list(res[2 + ns:2 + ns + nl]), res[-1]


def _copies_wait(started, plan, after, *, name):
    send_sems, recv_sems, srcs, lands, _ = started
    ns, nl = len(srcs), len(lands)

    def body(*refs):
        src_refs, land_refs = refs[:ns], refs[ns:ns + nl]
        send_sems, recv_sems = refs[ns + nl:ns + nl + 2]
        for k, (src, dst, peer, here) in enumerate(plan(src_refs, land_refs, *_coords())):
            pltpu.make_async_remote_copy(src_ref=src, dst_ref=dst, send_sem=send_sems.at[k], recv_sem=recv_sems.at[k],
                                         device_id=peer, device_id_type=MESH).wait_send()
            pltpu.make_async_remote_copy(src_ref=src, dst_ref=here, send_sem=send_sems.at[k], recv_sem=recv_sems.at[k],
                                         device_id=peer, device_id_type=MESH).wait_recv()

    bufs = [*srcs, *lands]
    res = pl.pallas_call(
        body, name=name, in_specs=[HBM] * (ns + nl) + [SEM, SEM, pl.BlockSpec(memory_space=pl.ANY)],
        out_specs=[HBM] * (ns + nl), out_shape=[pltpu.HBM(b.shape, b.dtype) for b in bufs],
        input_output_aliases={i: i for i in range(ns + nl)},
        compiler_params=pltpu.CompilerParams(has_side_effects=SIDE_EFFECT),
    )(*bufs, send_sems, recv_sems, after)
    return list(res[ns:])


def _dev_index(dev):
    return 4 * dev[0] + 2 * dev[1] + dev[2]


def _ag_chips_plan(src_refs, land_refs, x, y, c):
    me = _dev_index((x, y, c))
    return [(src, land.at[me], peer, land.at[_dev_index(peer)])
            for src, land in zip(src_refs, land_refs) for peer in (_flip_y(x, y, c), _flip_x(x, y, c), _flip_xy(x, y, c))]


def _ag_sibling_plan(src_refs, land_refs, x, y, c):
    chips = [(x, y), (x, 1 - y), (1 - x, y), (1 - x, 1 - y)]
    return [(land.at[_dev_index((*chip, c))], land.at[_dev_index((*chip, c))], (x, y, 1 - c),
             land.at[_dev_index((*chip, 1 - c))]) for land in land_refs for chip in chips]


def _ag_direct_plan(src_refs, land_refs, x, y, c):
    me = _dev_index((x, y, c))
    plan = []
    for src, land in zip(src_refs, land_refs):
        for m in range(1, N_DEV):
            peer = (x + (m >> 2) * (1 - 2 * x), y + ((m >> 1) & 1) * (1 - 2 * y), c + (m & 1) * (1 - 2 * c))
            plan.append((src, land.at[me], peer, land.at[_dev_index(peer)]))
    return plan


def _rs_direct_plan(src_refs, land_refs, x, y, c):
    plan = []
    for src, land in zip(src_refs, land_refs):
        for m in range(1, N_DEV):
            peer = (x + (m >> 2) * (1 - 2 * x), y + ((m >> 1) & 1) * (1 - 2 * y), c + (m & 1) * (1 - 2 * c))
            plan.append((src.at[_dev_index(peer)], land.at[m - 1], peer, land.at[m - 1]))
    return plan


def _rs_start(grads, me, *, name):
    own = [lax.dynamic_index_in_dim(g, me, 0, keepdims=False) for g in grads]
    lands = [lax.empty((N_DEV - 1, *g.shape[1:]), g.dtype) for g in grads]
    return _copies_start(grads, lands, _rs_direct_plan, (N_DEV - 1) * len(grads), name=name), own


def _rs_finish(started, after, *, name):
    handle, own = started
    got = _copies_wait(handle, _rs_direct_plan, after, name=name)
    return [[o, land] for o, land in zip(own, got)]


def _by_device_cols(w):
    K, N = w.shape
    return w.reshape(K, N_DEV, N // N_DEV).transpose(1, 0, 2)


def _gathered_cols(w8):
    return w8.transpose(1, 0, 2).reshape(w8.shape[1], -1)


def _pair_major(w, inverse=False):
    K = w.shape[0]
    a, b = (ATTN_W // LANES, 3) if inverse else (3, ATTN_W // LANES)
    return w.reshape(K, a, b, LANES).transpose(0, 2, 1, 3).reshape(K, 3 * ATTN_W)


def kernel(x, norm1_g, w_in, attn_norm_g, hgrn_norm_g, hgrn_lb_logits, w_out, norm2_g, w_up, conv_w, conv_b, w_down, final_norm_g, loss_target, m_norm1_g, m_w_in, m_attn_norm_g, m_hgrn_norm_g, m_hgrn_lb_logits, m_w_out, m_norm2_g, m_w_up, m_conv_w, m_conv_b, m_w_down, m_final_norm_g, v_norm1_g, v_w_in, v_attn_norm_g, v_hgrn_norm_g, v_hgrn_lb_logits, v_w_out, v_norm2_g, v_w_up, v_conv_w, v_conv_b, v_w_down, v_final_norm_g):
    xs, target = x[0], loss_target[0]
    S, D = xs.shape
    NA = 3 * ATTN_W
    fng = final_norm_g.reshape(1, D)

    casts = [_sum_cast([w[0]], BF16, name=f"cast_{nm}") for nm, w in
             (("w_in", w_in), ("w_out", w_out), ("w_up", w_up), ("w_down", w_down))]
    me = _dev_index(_coords())
    (g_in,) = _all_gather(casts[:1], name="ag_w_in")
    later = casts[1:] + [conv_w[0]]
    ag1 = _copies_start(later, [lax.empty((N_DEV, *s.shape), s.dtype) for s in later], _ag_chips_plan,
                        3 * len(later), name="ag_chips_start")
    wi = _gathered_cols(g_in)
    wi_a, wi_h = _pair_major(wi[:, :NA]), wi[:, NA:]

    u1, proj_a = _proj_attn(xs, norm1_g + ag1[4][0, 0], wi_a, name="proj_attn")
    proj_h = _mm(u1, wi_h, name="proj_hgrn")
    attn, lse = _attn_fwd(proj_a, name="attn_fwd")
    lands = _copies_wait(ag1, _ag_chips_plan, attn, name="ag_chips_wait")
    lands = [lax.dynamic_update_index_in_dim(l, s, me, 0) for l, s in zip(lands, later)]
    ag2 = _copies_start([], lands, _ag_sibling_plan, 4 * len(later), name="ag_sibling_start")
    rec, states = _hgrn_fwd(proj_h, hgrn_lb_logits + ag2[4][0, 0], name="hgrn_fwd")
    g_out, g_up, g_down, g_cw = _copies_wait(ag2, _ag_sibling_plan, rec, name="ag_sibling_wait")
    wo = g_out.reshape(-1, D)
    wu = _gathered_cols(g_up)
    wd = g_down.reshape(-1, D)
    cw = _gathered_cols(g_cw)
    h1, u2, mixed = _out_proj(attn, rec, proj_h, xs, attn_norm_g, hgrn_norm_g, norm2_g, wo, name="out_proj")
    gate, val, act = _up_glu(u2, wu, cw, conv_b, name="up_glu")
    dh2, dh2b, d_fng, loss_part = _down_loss(act, wd, h1, fng, target, name="down_loss")

    dgate, dval, d_cw, d_cb = _dact_glu_bwd(dh2b, wd, gate, val, cw, conv_b, name="dact_glu_bwd")
    dw_down = _mm_tn(act, dh2b, tm=256, name="dw_down")
    dh1, dh1b, d_n2g = _grad_norm_input([dgate, dval], [wu], h1, norm2_g, dh2, name="du2_norm2_bwd")
    dw_up = [_mm_tn(u2, dy, tm=1024, tn=256, name=f"dw_up_{nm}") for nm, dy in (("gate", dgate), ("val", dval))]
    rs_ffn = _rs_start([dw_down.reshape(N_DEV, -1, D),
                        jnp.concatenate([h.reshape(D, N_DEV // 2, -1).transpose(1, 0, 2) for h in dw_up], axis=0)],
                       me, name="rs_ffn_start")
    dattn, delta, drec, dhg, d_ang, d_hng = _dmix_post_bwd(dh1b, wo, attn, rec, proj_h, attn_norm_g + rs_ffn[0][4][0, 0],
                                                          hgrn_norm_g, name="dmix_post_bwd")
    dw_out = _mm_tn(mixed, dh1b, name="dw_out")
    rs_out = _rs_start([dw_out.reshape(N_DEV, -1, D)], me, name="rs_out_start")
    dproj_h, d_lbl = _hgrn_bwd(proj_h, hgrn_lb_logits + rs_out[0][4][0, 0], states, drec, name="hgrn_bwd")
    no_state = jnp.zeros_like(loss_part)
    small = [("loss", loss_part, no_state, no_state, no_state),
             ("attn_norm_g", d_ang, attn_norm_g, m_attn_norm_g, v_attn_norm_g),
             ("hgrn_norm_g", d_hng, hgrn_norm_g, m_hgrn_norm_g, v_hgrn_norm_g),
             ("hgrn_lb_logits", d_lbl, hgrn_lb_logits, m_hgrn_lb_logits, v_hgrn_lb_logits),
             ("norm2_g", d_n2g, norm2_g, m_norm2_g, v_norm2_g),
             ("conv_b", d_cb, conv_b, m_conv_b, v_conv_b),
             ("final_norm_g", d_fng, final_norm_g, m_final_norm_g, v_final_norm_g)]
    pack = lambda arrs: jnp.concatenate([a.reshape(1, -1) for a in arrs], axis=1)
    small_own = [pack([s[1] for s in small]), d_cw]
    ag_small = _copies_start(small_own, [lax.empty((N_DEV, *s.shape), s.dtype) for s in small_own], _ag_direct_plan,
                             (N_DEV - 1) * len(small_own), name="ag_small_start")
    dproj_a = _attn_bwd(proj_a, dattn, lse, delta, name="attn_bwd")
    dw_in = jnp.concatenate([_pair_major(_mm_tn(u1, dproj_a, tm=1024, tn=512, name="dw_in_attn"), inverse=True),
                             _mm_tn(u1, dproj_h, tm=1024, tn=512, name="dw_in_hgrn"),
                             _mm_tn(u1, dhg, tm=1024, tn=512, name="dw_in_gate")], axis=1)
    rs_in = _rs_start([_by_device_cols(dw_in)], me, name="rs_in_start")
    grad_x, _, d_n1g = _grad_norm_input([dproj_a, dproj_h, dhg], [wi_a, wi_h], xs, norm1_g + rs_in[0][4][0, 0], dh1,
                                        name="du1_norm1_bwd")

    res = {}

    def update(nm, parts, w, m, v):
        res[nm] = _adamw(parts, w, m, v, name=f"adamw_{nm}")

    g_down, g_up = _rs_finish(rs_ffn, grad_x, name="rs_ffn_wait")
    update("w_down", g_down, w_down, m_w_down, v_w_down)
    update("w_up", g_up, w_up, m_w_up, v_w_up)
    (g_out,) = _rs_finish(rs_out, grad_x, name="rs_out_wait")
    update("w_out", g_out, w_out, m_w_out, v_w_out)
    (g_in,) = _rs_finish(rs_in, res["w_up"][1], name="rs_in_wait")
    update("w_in", g_in, w_in, m_w_in, v_w_in)

    g_small, g_dcw = [lax.dynamic_update_index_in_dim(l, s, me, 0)
                      for l, s in zip(_copies_wait(ag_small, _ag_direct_plan, grad_x, name="ag_small_wait"), small_own)]
    sm = _adamw([g_small], pack([s[2] for s in small]), pack([s[3] for s in small]),
                pack([s[4] for s in small]), name="adamw_small")
    off = 0
    for nm, _, w, _, _ in small:
        res[nm] = [r[:, off:off + w.size].reshape(w.shape) for r in sm]
        off += w.size
    ncw = conv_w.shape[-1]
    mine_cw = lax.dynamic_slice_in_dim(g_dcw, me * ncw, ncw, axis=2)
    res["conv_w"] = _adamw([mine_cw], conv_w, m_conv_w, v_conv_w, name="adamw_conv_w")
    late, _ = lax.optimization_barrier((d_n1g, res["w_in"][1]))
    update("norm1_g", _all_gather([late], name="ag_norm1_grad"), norm1_g, m_norm1_g, v_norm1_g)

    loss = res["loss"][0][0, 0]
    order = ["norm1_g", "w_in", "attn_norm_g", "hgrn_norm_g", "hgrn_lb_logits", "w_out", "norm2_g", "w_up",
             "conv_w", "conv_b", "w_down", "final_norm_g"]
    return (loss, grad_x[None], *[res[nm][0] for nm in order], *[res[nm][1] for nm in order],
            *[res[nm][2] for nm in order], *[res[nm][3] for nm in order])
```

```python
import jax
import jax.numpy as jnp
from jax import lax
from jax.experimental import pallas as pl
from jax.experimental.pallas import tpu as pltpu

F32, BF16 = jnp.float32, jnp.bfloat16
NORM_EPS = 1e-6
ATTN_HEADS, HEAD_DIM, ATTN_BLOCK = 8, 64, 128
DILATIONS = (1, 4, 16)
ATTN_SCALE = HEAD_DIM ** -0.5
ATTN_W = ATTN_HEADS * HEAD_DIM
HGRN_HEADS, HGRN_DIM, HGRN_CHUNK = 4, 128, 64
HGRN_W = HGRN_HEADS * HGRN_DIM
ADAM_LR, ADAM_B1, ADAM_B2, ADAM_EPS, ADAM_WD, ADAM_STEP = 0.001, 0.9, 0.999, 1e-08, 0.01, 10
LANES, SUBLANES = 128, 8
VMEM_LIMIT_BYTES = 56 * 1024 * 1024
N_DEV = 8
MESH = pl.DeviceIdType.MESH
HBM = pl.BlockSpec(memory_space=pltpu.HBM)
HIGHEST = lax.Precision.HIGHEST


def _cparams(*sem):
    return pltpu.CompilerParams(dimension_semantics=sem, vmem_limit_bytes=VMEM_LIMIT_BYTES)


def _tile(n, pref):
    if n <= pref:
        return n
    t = (pref // LANES) * LANES
    while n % t:
        t -= LANES
    return t


def _resident(shape):
    return pl.BlockSpec(shape, lambda *_: (0,) * len(shape), pipeline_mode=pl.Buffered(1))


def _dot(a, b, dims, precision=None):
    return lax.dot_general(a, b, (dims, ((), ())), precision=precision, preferred_element_type=F32)


def _nn(a, b, precision=None):
    return _dot(a, b, ((1,), (0,)), precision)


def _nt(a, b):
    return _dot(a, b, ((1,), (1,)))


def _tn(a, b):
    return _dot(a, b, ((0,), (0,)))


def _sigmoid(x):
    return 1.0 / (1.0 + jnp.exp(-x))


def _rstd(x):
    return lax.rsqrt(jnp.mean(x * x, axis=-1, keepdims=True) + NORM_EPS)


def _norm_bwd(x, g, du):
    r = _rstd(x)
    xh = x * r
    dxh = du * g
    return r * (dxh - xh * jnp.mean(dxh * xh, axis=-1, keepdims=True)), du * xh


def _row_halves(tm):
    return [pl.ds(0, tm // 2), pl.ds(tm // 2, tm // 2)]


def _accumulate(ref, part, first):
    @pl.when(first)
    def _():
        ref[...] = part

    @pl.when(jnp.logical_not(first))
    def _():
        ref[...] += part


def _mm(a, b, *, name, out_dtype=F32, tm=1024, tn=512):
    M, K = a.shape
    N = b.shape[1]
    tm, tn = _tile(M, tm), _tile(N, tn)

    def body(a_ref, b_ref, o_ref):
        o_ref[...] = _nn(a_ref[...], b_ref[...]).astype(out_dtype)

    return pl.pallas_call(
        body, name=name, grid=(M // tm, N // tn),
        in_specs=[pl.BlockSpec((tm, K), lambda i, j: (i, 0)), pl.BlockSpec((K, tn), lambda i, j: (0, j))],
        out_specs=pl.BlockSpec((tm, tn), lambda i, j: (i, j)), out_shape=jax.ShapeDtypeStruct((M, N), out_dtype),
        compiler_params=_cparams("parallel", "parallel"),
    )(a, b)


def _mm_tn(x, dy, *, name, tm=512, tn=1024):
    S, M = x.shape
    N = dy.shape[1]
    tm, tn = _tile(M, tm), _tile(N, tn)

    def body(x_ref, dy_ref, o_ref, xt_ref):
        @pl.when(pl.program_id(1) == 0)
        def _():
            xt_ref[...] = x_ref[...].T

        o_ref[...] = _nn(xt_ref[...], dy_ref[...]).astype(BF16)

    return pl.pallas_call(
        body, name=name, grid=(M // tm, N // tn),
        in_specs=[pl.BlockSpec((S, tm), lambda i, j: (0, i)), pl.BlockSpec((S, tn), lambda i, j: (0, j))],
        out_specs=pl.BlockSpec((tm, tn), lambda i, j: (i, j)), out_shape=jax.ShapeDtypeStruct((M, N), BF16),
        scratch_shapes=[pltpu.VMEM((tm, S), BF16)], compiler_params=_cparams("parallel", "arbitrary"),
    )(x, dy)


def _proj_attn(x, g, w, *, name, tm=1024, tn=512):
    S, D = x.shape
    N = w.shape[1]

    def body(x_ref, g_ref, w_ref, u_ref, o_ref):
        @pl.when(pl.program_id(1) == 0)
        def _():
            xv = x_ref[...]
            u_ref[...] = (xv * _rstd(xv) * g_ref[...]).astype(BF16)

        o_ref[...] = _nn(u_ref[...], w_ref[...]).astype(BF16)

    return pl.pallas_call(
        body, name=name, grid=(S // tm, N // tn),
        in_specs=[pl.BlockSpec((tm, D), lambda i, j: (i, 0)), pl.BlockSpec((1, D), lambda i, j: (0, 0)),
                  pl.BlockSpec((D, tn), lambda i, j: (0, j))],
        out_specs=[pl.BlockSpec((tm, D), lambda i, j: (i, 0)), pl.BlockSpec((tm, tn), lambda i, j: (i, j))],
        out_shape=[jax.ShapeDtypeStruct((S, D), BF16), jax.ShapeDtypeStruct((S, N), BF16)],
        compiler_params=_cparams("parallel", "arbitrary"),
    )(x, g, w)


PAIR_W = 3 * LANES
ATTN_UNROLL_FWD, ATTN_UNROLL_BWD = 4, 4


def _attn_masks(first):
    qi = lax.broadcasted_iota(jnp.int32, (ATTN_BLOCK, 2 * ATTN_BLOCK), 0)
    kj = lax.broadcasted_iota(jnp.int32, (ATTN_BLOCK, 2 * ATTN_BLOCK), 1)
    dist = qi + ATTN_BLOCK - kj
    valid = (dist >= 0) & (dist <= ATTN_BLOCK) & jnp.logical_or(kj >= ATTN_BLOCK, jnp.logical_not(first))
    lane = lax.broadcasted_iota(jnp.int32, (1, LANES), 1)
    return valid, lane


def _for_residue_blocks(S, d, fn):
    span = ATTN_BLOCK * d
    nb = S // span

    def step(n, carry):
        base = pl.multiple_of(n * span, span)
        for r in range(d):
            off = pl.multiple_of((r * nb + n) * ATTN_BLOCK, ATTN_BLOCK)
            fn(lambda ref, r=r: _block_rows(ref, base, r, d),
               lambda ref, val, r=r: _set_block_rows(ref, base, r, d, val), off)
        return carry

    lax.fori_loop(0, nb, step, 0)


def _for_blocks(S, unroll, fn):
    def step(i, carry):
        fn([(pl.multiple_of((i * unroll + u) * ATTN_BLOCK, ATTN_BLOCK), i * unroll + u) for u in range(unroll)])
        return carry

    lax.fori_loop(0, S // ATTN_BLOCK // unroll, step, 0)


def _head_value(x2, lane, e):
    return jnp.sum(jnp.where(lane == HEAD_DIM * e, x2, 0.0), axis=-1, keepdims=True)


def _block_rows(ref, base, r, d):
    if d == 1:
        return ref[pl.ds(base, ATTN_BLOCK), :]
    return ref.at[pl.ds(base, ATTN_BLOCK * d)][pl.ds(r, ATTN_BLOCK, stride=d), :]


def _set_block_rows(ref, base, r, d, val):
    if d == 1:
        ref[pl.ds(base, ATTN_BLOCK), :] = val
    else:
        ref.at[pl.ds(base, ATTN_BLOCK * d)][pl.ds(r, ATTN_BLOCK, stride=d), :] = val


def _order4_to_16(src, dst, pad):
    S = src.shape[0]
    q4, q16 = S // 4, S // 16
    for r in range(4):
        for a in range(4):
            for n in range(q16 // ATTN_BLOCK):
                rows = src.at[pl.ds(r * q4 + 4 * ATTN_BLOCK * n, 4 * ATTN_BLOCK)][pl.ds(a, ATTN_BLOCK, stride=4), :]
                dst[pl.ds(pad + (4 * a + r) * q16 + ATTN_BLOCK * n, ATTN_BLOCK), :] = rows.astype(dst.dtype)


def _order16_to_4(src, pad, dst):
    S = dst.shape[0]
    q4, q16 = S // 4, S // 16
    for r in range(4):
        for a in range(4):
            for n in range(q16 // ATTN_BLOCK):
                rows = src[pl.ds(pad + (4 * a + r) * q16 + ATTN_BLOCK * n, ATTN_BLOCK), :]
                dst.at[pl.ds(r * q4 + 4 * ATTN_BLOCK * n, 4 * ATTN_BLOCK)][pl.ds(a, ATTN_BLOCK, stride=4), :] = rows


def _regroup(S, d, pairs, tmp):
    for src, dst, pad in pairs:
        if d == 16:
            def to_tmp(rows, _, off, src=src):
                tmp[pl.ds(off, ATTN_BLOCK), :] = rows(src)

            _for_residue_blocks(S, 4, to_tmp)
            _order4_to_16(tmp, dst, pad)
    if d != 16:
        def to_dst(rows, _, off):
            for src, dst, pad in pairs:
                dst[pl.ds(pad + off, ATTN_BLOCK), :] = rows(src).astype(dst.dtype)

        _for_residue_blocks(S, d, to_dst)


def _split_pair(p_ref, qs, ks, vs, bk, bv):
    qs[...] = p_ref[:, 0:LANES].astype(F32)
    ks[...] = p_ref[:, LANES:2 * LANES].astype(F32)
    vs[...] = p_ref[:, 2 * LANES:3 * LANES].astype(F32)
    bk[0:ATTN_BLOCK, :] = jnp.zeros((ATTN_BLOCK, LANES), bk.dtype)
    bv[0:ATTN_BLOCK, :] = jnp.zeros((ATTN_BLOCK, LANES), bv.dtype)


def _attn_fwd(proj_a, *, name):
    S = proj_a.shape[0]

    def body(p_ref, o_ref, l_ref, qs, ks, vs, bq, bk, bv, bo, bl, to, tl):
        _split_pair(p_ref, qs, ks, vs, bk, bv)
        for d in DILATIONS:
            nb = S // (ATTN_BLOCK * d)
            _regroup(S, d, ((qs, bq, 0), (ks, bk, ATTN_BLOCK), (vs, bv, ATTN_BLOCK)), to)

            def blocks(group, nb=nb):
                lane = lax.broadcasted_iota(jnp.int32, (1, LANES), 1)
                heads = [(lane >= HEAD_DIM * e) & (lane < HEAD_DIM * (e + 1)) for e in range(LANES // HEAD_DIM)]
                wins = [pl.ds(off, 2 * ATTN_BLOCK) for off, _ in group]
                s = [[_nt(jnp.where(mh, bq[pl.ds(off, ATTN_BLOCK), :], jnp.zeros((ATTN_BLOCK, LANES), BF16)), bk[win, :])
                      for mh in heads] for (off, _), win in zip(group, wins)]
                p, m, l = [], [], []
                for (off, b), su in zip(group, s):
                    valid, _ = _attn_masks(jnp.bitwise_and(b, nb - 1) == 0)
                    sm = [jnp.where(valid, x * ATTN_SCALE, -jnp.inf) for x in su]
                    m.append([jnp.max(x, axis=-1, keepdims=True) for x in sm])
                    p.append([jnp.exp(x - mx) for x, mx in zip(sm, m[-1])])
                    l.append([jnp.sum(x, axis=-1, keepdims=True) for x in p[-1]])
                o = [[_nn(x.astype(BF16), bv[win, :]) for x in pu] for pu, win in zip(p, wins)]
                for (off, _), ou, mu, lu in zip(group, o, m, l):
                    o2 = jnp.zeros((ATTN_BLOCK, LANES), F32)
                    l2 = jnp.zeros((ATTN_BLOCK, LANES), F32)
                    for mh, oe, me_, le in zip(heads, ou, mu, lu):
                        o2 = jnp.where(mh, oe / le, o2)
                        l2 = jnp.where(mh, me_ + jnp.log(le), l2)
                    bo[pl.ds(off, ATTN_BLOCK), :] = o2
                    bl[pl.ds(off, ATTN_BLOCK), :] = l2

            _for_blocks(S, ATTN_UNROLL_FWD, blocks)

            if d == 16:
                _order16_to_4(bo, 0, to)
                _order16_to_4(bl, 0, tl)
            src_o, src_l = (to, tl) if d == 16 else (bo, bl)

            def merge(rows, set_rows, off, d=d, src_o=src_o, src_l=src_l):
                blk = pl.ds(off, ATTN_BLOCK)
                o2, l2 = src_o[blk, :], src_l[blk, :]
                if d != DILATIONS[0]:
                    lo, oo = rows(l_ref), rows(o_ref)
                    ln = jnp.maximum(lo, l2)
                    wa, wb = jnp.exp(lo - ln), jnp.exp(l2 - ln)
                    o2 = (wa * oo + wb * o2) / (wa + wb)
                    l2 = ln + jnp.log(wa + wb)
                set_rows(o_ref, o2)
                set_rows(l_ref, l2)

            _for_residue_blocks(S, min(d, 4), merge)

    slab = pl.BlockSpec((S, LANES), lambda p: (0, p))
    f32_slab, bf16_slab = pltpu.VMEM((S, LANES), F32), pltpu.VMEM((S, LANES), BF16)
    bf16_window = pltpu.VMEM((S + ATTN_BLOCK, LANES), BF16)
    return pl.pallas_call(
        body, name=name, grid=(ATTN_W // LANES,), in_specs=[pl.BlockSpec((S, PAIR_W), lambda p: (0, p))],
        out_specs=[slab, slab],
        out_shape=[jax.ShapeDtypeStruct((S, ATTN_W), F32), jax.ShapeDtypeStruct((S, ATTN_W), F32)],
        scratch_shapes=[f32_slab] * 3 + [bf16_slab, bf16_window, bf16_window] + [f32_slab] * 4,
        compiler_params=_cparams("parallel"),
    )(proj_a)


def _attn_bwd(proj_a, do, lse, delta, *, name):
    S = proj_a.shape[0]

    def body(p_ref, do_ref, lse_ref, dl_ref, o_ref, qs, ks, vs, dqs, dks, dvs, bq, bk, bv, bdo, blse, bdl, bdq, bdk, bdv,
             tmp):
        _split_pair(p_ref, qs, ks, vs, bk, bv)
        bdk[0:ATTN_BLOCK, :] = jnp.zeros((ATTN_BLOCK, LANES), F32)
        bdv[0:ATTN_BLOCK, :] = jnp.zeros((ATTN_BLOCK, LANES), F32)
        for d in DILATIONS:
            nb = S // (ATTN_BLOCK * d)
            _regroup(S, d, ((qs, bq, 0), (ks, bk, ATTN_BLOCK), (vs, bv, ATTN_BLOCK), (do_ref, bdo, 0),
                            (lse_ref, blse, 0), (dl_ref, bdl, 0)), tmp)

            def blocks(group, nb=nb):
                lane = lax.broadcasted_iota(jnp.int32, (1, LANES), 1)
                heads = [(lane >= HEAD_DIM * e) & (lane < HEAD_DIM * (e + 1)) for e in range(LANES // HEAD_DIM)]
                zero = jnp.zeros((ATTN_BLOCK, LANES), BF16)
                chains = [(off, b, e, mh) for off, b in group for e, mh in enumerate(heads)]
                qm = [jnp.where(mh, bq[pl.ds(off, ATTN_BLOCK), :], zero) for off, _, _, mh in chains]
                dom = [jnp.where(mh, bdo[pl.ds(off, ATTN_BLOCK), :], zero) for off, _, _, mh in chains]
                s = [_nt(x, bk[pl.ds(off, 2 * ATTN_BLOCK), :]) for x, (off, _, _, _) in zip(qm, chains)]
                dp = [_nt(x, bv[pl.ds(off, 2 * ATTN_BLOCK), :]) for x, (off, _, _, _) in zip(dom, chains)]
                p, ds = [], []
                for (off, b, e, _), sc, dpc in zip(chains, s, dp):
                    valid, _ = _attn_masks(jnp.bitwise_and(b, nb - 1) == 0)
                    blk = pl.ds(off, ATTN_BLOCK)
                    pc = jnp.where(valid, jnp.exp(sc * ATTN_SCALE - _head_value(blse[blk, :], lane, e)), 0.0)
                    ds.append((pc * (dpc - _head_value(bdl[blk, :], lane, e)) * ATTN_SCALE).astype(BF16))
                    p.append(pc.astype(BF16))
                dq = [_nn(x, bk[pl.ds(off, 2 * ATTN_BLOCK), :]) for x, (off, _, _, _) in zip(ds, chains)]
                dk = [_tn(x, y) for x, y in zip(ds, qm)]
                dv = [_tn(x, y) for x, y in zip(p, dom)]
                nh = len(heads)
                for u, (off, _) in enumerate(group):
                    dq2 = jnp.zeros((ATTN_BLOCK, LANES), F32)
                    for mh, x in zip(heads, dq[nh * u:nh * (u + 1)]):
                        dq2 = jnp.where(mh, x, dq2)
                    bdq[pl.ds(off, ATTN_BLOCK), :] = dq2
                    for acc, grads in ((bdk, dk), (bdv, dv)):
                        win_grad = sum(grads[nh * u + 1:nh * (u + 1)], grads[nh * u])
                        acc[pl.ds(off, ATTN_BLOCK), :] += win_grad[:ATTN_BLOCK]
                        acc[pl.ds(off + ATTN_BLOCK, ATTN_BLOCK), :] = win_grad[ATTN_BLOCK:]

            _for_blocks(S, ATTN_UNROLL_BWD, blocks)

            outs = ((dqs, bdq, 0), (dks, bdk, ATTN_BLOCK), (dvs, bdv, ATTN_BLOCK))
            if d == 16:
                for acc, grad, pad in outs:
                    _order16_to_4(grad, pad, tmp)

                    def add(rows, set_rows, off, acc=acc):
                        set_rows(acc, rows(acc) + tmp[pl.ds(off, ATTN_BLOCK), :])

                    _for_residue_blocks(S, 4, add)
            else:
                def scatter(rows, set_rows, off, d=d):
                    for acc, grad, pad in outs:
                        part = grad[pl.ds(pad + off, ATTN_BLOCK), :]
                        set_rows(acc, part if d == DILATIONS[0] else rows(acc) + part)

                _for_residue_blocks(S, d, scatter)
        o_ref[:, 0:LANES] = dqs[...].astype(BF16)
        o_ref[:, LANES:2 * LANES] = dks[...].astype(BF16)
        o_ref[:, 2 * LANES:3 * LANES] = dvs[...].astype(BF16)

    slab = pl.BlockSpec((S, LANES), lambda p: (0, p), pipeline_mode=pl.Buffered(1))
    pair = pl.BlockSpec((S, PAIR_W), lambda p: (0, p))
    f32_slab, bf16_slab = pltpu.VMEM((S, LANES), F32), pltpu.VMEM((S, LANES), BF16)
    f32_window, bf16_window = pltpu.VMEM((S + ATTN_BLOCK, LANES), F32), pltpu.VMEM((S + ATTN_BLOCK, LANES), BF16)
    return pl.pallas_call(
        body, name=name, grid=(ATTN_W // LANES,), in_specs=[pair, slab, slab, slab], out_specs=pair,
        out_shape=jax.ShapeDtypeStruct(proj_a.shape, BF16),
        scratch_shapes=[f32_slab] * 6 + [bf16_slab, bf16_window, bf16_window, bf16_slab, f32_slab, f32_slab,
                                         f32_slab, f32_window, f32_window, f32_slab],
        compiler_params=_cparams("parallel"),
    )(proj_a, do, lse, delta)


HG_T = 2 * HGRN_CHUNK


def _hgrn_consts():
    row = lax.broadcasted_iota(jnp.int32, (HG_T, HG_T), 0)
    col = lax.broadcasted_iota(jnp.int32, (HG_T, HG_T), 1)
    same = (row >= HGRN_CHUNK) == (col >= HGRN_CHUNK)
    return row, same & (col <= row), same & (col >= row)


def _lower_bound(logits_ref):
    l0, l1 = logits_ref[0:1, :], logits_ref[1:2, :]
    mx = jnp.maximum(l0, l1)
    e0, e1 = jnp.exp(l0 - mx), jnp.exp(l1 - mx)
    return e0 / (e0 + e1)


def _hgrn_gates(qs, fs, lbs, row, causal):
    C = HGRN_CHUNK
    tri = jnp.where(causal, 1.0, 0.0).astype(F32)
    sgs = [_sigmoid(f) for f in fs]
    forgets = [lb + (1.0 - lb) * sg for lb, sg in zip(lbs, sgs)]
    logfs = [jnp.log(forget) for forget in forgets]
    bs = [_nn(tri, logf, HIGHEST) for logf in logfs]
    out = []
    for q, sg, forget, logf, b in zip(qs, sgs, forgets, logfs, bs):
        key = 1.0 - forget
        bend0 = jnp.sum(logf[:C], axis=0, keepdims=True)
        bend1 = jnp.sum(logf[C:], axis=0, keepdims=True)
        bend = jnp.where(row < C, bend0, bend1)
        eb, emb, eend = jnp.exp(b), jnp.exp(-b), jnp.exp(bend - b)
        sq = _sigmoid(q)
        out.append(dict(sg=sg, forget=forget, key=key, bend0=bend0, bend1=bend1, eb=eb, emb=emb, eend=eend, sq=sq,
                        qd=q * sq * eb, ki=key * emb, ke=key * eend))
    return out


def _hgrn_fwd(proj, logits, *, name):
    S = proj.shape[0]
    W, C = HGRN_W, HGRN_CHUNK

    def body(q_ref, f_ref, i_ref, lg_ref, rec_ref, st_ref, s_ref):
        @pl.when(pl.program_id(0) == 0)
        def _():
            s_ref[...] = jnp.zeros_like(s_ref)

        row, causal, _ = _hgrn_consts()
        lb_all = _lower_bound(lg_ref)
        heads = range(HGRN_HEADS)
        sls = [slice(HGRN_DIM * h, HGRN_DIM * (h + 1)) for h in heads]
        gts = _hgrn_gates([q_ref[:, sl] for sl in sls], [f_ref[:, sl] for sl in sls], [lb_all[:, sl] for sl in sls],
                          row, causal)
        qd, ki, ke = ([gt[k].astype(BF16) for gt in gts] for k in ("qd", "ki", "ke"))
        iv = [i_ref[:, sl].astype(BF16) for sl in sls]
        s0 = [s_ref[h] for h in heads]
        a = [_nt(qd[h], ki[h]) for h in heads]
        o0 = [_nt(qd[h][:C], s0[h].astype(BF16)) for h in heads]
        u0 = [_tn(iv[h][:C], ke[h][:C]) for h in heads]
        u1 = [_tn(iv[h][C:], ke[h][C:]) for h in heads]
        s1 = [jnp.exp(gts[h]["bend0"]) * s0[h] + u0[h] for h in heads]
        o = [_nn(jnp.where(causal, a[h], 0.0).astype(BF16), iv[h]) for h in heads]
        o1 = [_nt(qd[h][C:], s1[h].astype(BF16)) for h in heads]
        for h in heads:
            st_ref[0, h] = s0[h]
            st_ref[1, h] = s1[h]
            s_ref[h] = jnp.exp(gts[h]["bend1"]) * s1[h] + u1[h]
            rec_ref[:, sls[h]] = o[h] + jnp.concatenate([o0[h], o1[h]], axis=0)

    blk = lambda j: pl.BlockSpec((HG_T, W), lambda t: (t, j))
    return pl.pallas_call(
        body, name=name, grid=(S // HG_T,),
        in_specs=[blk(0), blk(1), blk(2), pl.BlockSpec((2, W), lambda t: (0, 0))],
        out_specs=[blk(0), pl.BlockSpec((2, HGRN_HEADS, HGRN_DIM, HGRN_DIM), lambda t: (t, 0, 0, 0))],
        out_shape=[jax.ShapeDtypeStruct((S, W), F32),
                   jax.ShapeDtypeStruct((S // C, HGRN_HEADS, HGRN_DIM, HGRN_DIM), F32)],
        scratch_shapes=[pltpu.VMEM((HGRN_HEADS, HGRN_DIM, HGRN_DIM), F32)],
        compiler_params=_cparams("arbitrary"),
    )(proj, proj, proj, logits)


def _hgrn_bwd(proj, logits, states, drec, *, name):
    S = proj.shape[0]
    W, C = HGRN_W, HGRN_CHUNK
    nt = S // HG_T

    def body(q_ref, f_ref, i_ref, lg_ref, st_ref, do_ref, dp_ref, dlg_ref, ds_ref, dlb_ref):
        t = pl.program_id(0)

        @pl.when(t == 0)
        def _():
            ds_ref[...] = jnp.zeros_like(ds_ref)
            dlb_ref[...] = jnp.zeros_like(dlb_ref)

        row, causal, anti = _hgrn_consts()
        lb_all = _lower_bound(lg_ref)
        heads = range(HGRN_HEADS)
        sls = [slice(HGRN_DIM * h, HGRN_DIM * (h + 1)) for h in heads]
        qs, lbs = [q_ref[:, sl] for sl in sls], [lb_all[:, sl] for sl in sls]
        gts = _hgrn_gates(qs, [f_ref[:, sl] for sl in sls], lbs, row, causal)
        qd, ki, ke = ([gt[k] for gt in gts] for k in ("qd", "ki", "ke"))
        qdb, kib, keb = ([x.astype(BF16) for x in xs] for xs in (qd, ki, ke))
        iv = [i_ref[:, sl].astype(BF16) for sl in sls]
        dob = [do_ref[:, sl].astype(BF16) for sl in sls]
        s0, s1, ds1 = [st_ref[0, h] for h in heads], [st_ref[1, h] for h in heads], [ds_ref[h] for h in heads]
        ds1b = [x.astype(BF16) for x in ds1]
        dec0, dec1 = [jnp.exp(gt["bend0"]) for gt in gts], [jnp.exp(gt["bend1"]) for gt in gts]
        a = [_nt(qdb[h], kib[h]) for h in heads]
        da = [_nt(dob[h], iv[h]) for h in heads]
        dqd1 = [_nn(dob[h][C:], s1[h].astype(BF16)) for h in heads]
        dqd0 = [_nn(dob[h][:C], s0[h].astype(BF16)) for h in heads]
        di1 = [_nt(keb[h][C:], ds1b[h]) for h in heads]
        dke1 = [_nn(iv[h][C:], ds1b[h]) for h in heads]
        t1 = [_tn(dob[h][C:], qdb[h][C:]) for h in heads]
        t0 = [_tn(dob[h][:C], qdb[h][:C]) for h in heads]
        ds0 = [dec1[h] * ds1[h] + t1[h] for h in heads]
        ds0b = [x.astype(BF16) for x in ds0]
        a = [jnp.where(causal, x, 0.0).astype(BF16) for x in a]
        da = [jnp.where(causal, x, 0.0).astype(BF16) for x in da]
        di0 = [_nt(keb[h][:C], ds0b[h]) for h in heads]
        dke0 = [_nn(iv[h][:C], ds0b[h]) for h in heads]
        dqd_a = [_nn(da[h], kib[h]) for h in heads]
        dki = [_tn(da[h], qdb[h]) for h in heads]
        di_a = [_tn(a[h], dob[h]) for h in heads]
        dqd, dke, db = [], [], []
        for h in heads:
            ds_ref[h] = dec0[h] * ds0[h] + t0[h]
            ddec1 = jnp.sum(ds1[h] * s1[h], axis=0, keepdims=True)
            ddec0 = jnp.sum(ds0[h] * s0[h], axis=0, keepdims=True)
            dqd.append(dqd_a[h] + jnp.concatenate([dqd0[h], dqd1[h]], axis=0))
            dp_ref[:, 2 * W + HGRN_DIM * h:2 * W + HGRN_DIM * (h + 1)] = (
                di_a[h] + jnp.concatenate([di0[h], di1[h]], axis=0)).astype(BF16)
            dke.append(jnp.concatenate([dke0[h], dke1[h]], axis=0))
            gke = dke[h] * ke[h]
            dbend0 = jnp.sum(gke[:C], axis=0, keepdims=True) + ddec0 * dec0[h]
            dbend1 = jnp.sum(gke[C:], axis=0, keepdims=True) + ddec1 * dec1[h]
            dbh = dqd[h] * qd[h] - dki[h] * ki[h] - gke
            db.append(dbh + jnp.where(row == C - 1, dbend0, 0.0) + jnp.where(row == HG_T - 1, dbend1, 0.0))
        tri = jnp.where(anti, 1.0, 0.0).astype(F32)
        dlogf = [_nn(tri, db[h], HIGHEST) for h in heads]
        for h in heads:
            gt, lb, q = gts[h], lbs[h], qs[h]
            dforget = dlogf[h] / gt["forget"] - (dki[h] * gt["emb"] + dke[h] * gt["eend"])
            sg, sq = gt["sg"], gt["sq"]
            dp_ref[:, W + HGRN_DIM * h:W + HGRN_DIM * (h + 1)] = (dforget * (1.0 - lb) * sg * (1.0 - sg)).astype(BF16)
            dlb_ref[:, sls[h]] += jnp.sum(dforget * (1.0 - sg), axis=0, keepdims=True)
            dp_ref[:, sls[h]] = (dqd[h] * gt["eb"] * sq * (1.0 + q * (1.0 - sq))).astype(BF16)

        @pl.when(t == nt - 1)
        def _():
            dl0 = dlb_ref[...] * lb_all * (1.0 - lb_all)
            dlg_ref[0:1, :] = dl0
            dlg_ref[1:2, :] = -dl0

    blk = lambda j: pl.BlockSpec((HG_T, W), lambda t: (nt - 1 - t, j))
    full = pl.BlockSpec((2, W), lambda t: (0, 0))
    return pl.pallas_call(
        body, name=name, grid=(nt,),
        in_specs=[blk(0), blk(1), blk(2), full,
                  pl.BlockSpec((2, HGRN_HEADS, HGRN_DIM, HGRN_DIM), lambda t: (nt - 1 - t, 0, 0, 0)), blk(0)],
        out_specs=[pl.BlockSpec((HG_T, 3 * W), lambda t: (nt - 1 - t, 0)), full],
        out_shape=[jax.ShapeDtypeStruct((S, 3 * W), BF16), jax.ShapeDtypeStruct((2, W), F32)],
        scratch_shapes=[pltpu.VMEM((HGRN_HEADS, HGRN_DIM, HGRN_DIM), F32), pltpu.VMEM((1, W), F32)],
        compiler_params=_cparams("arbitrary"),
    )(proj, proj, proj, logits, states, drec)


def _out_proj(attn, rec, proj_h, x, g_attn, g_hgrn, g_norm2, w_out, *, name, tm=512):
    S, D = x.shape
    AW, W = ATTN_W, HGRN_W

    def body(a_ref, r_ref, hg_ref, x_ref, ga_ref, gh_ref, g2_ref, w_ref, h_ref, u_ref, m_ref):
        av = a_ref[...]
        m_ref[:, :AW] = (av * _rstd(av) * ga_ref[...]).astype(BF16)
        for h in range(HGRN_HEADS):
            sl = slice(HGRN_DIM * h, HGRN_DIM * (h + 1))
            rv, hg = r_ref[:, sl], hg_ref[:, sl]
            m_ref[:, AW + HGRN_DIM * h:AW + HGRN_DIM * (h + 1)] = (
                (rv * _rstd(rv) * gh_ref[:, sl]) * (hg * _sigmoid(hg))).astype(BF16)
        h1 = x_ref[...] + _nn(m_ref[...], w_ref[...])
        h_ref[...] = h1
        u_ref[...] = (h1 * _rstd(h1) * g2_ref[...]).astype(BF16)

    row = lambda w, j=0: pl.BlockSpec((tm, w), lambda i: (i, j))
    vec = lambda w: pl.BlockSpec((1, w), lambda i: (0, 0))
    return pl.pallas_call(
        body, name=name, grid=(S // tm,),
        in_specs=[row(AW), row(W), row(W, 3), row(D), vec(AW), vec(W), vec(D), _resident(w_out.shape)],
        out_specs=[row(D), row(D), row(AW + W)],
        out_shape=[jax.ShapeDtypeStruct((S, D), F32), jax.ShapeDtypeStruct((S, D), BF16),
                   jax.ShapeDtypeStruct((S, AW + W), BF16)],
        compiler_params=_cparams("parallel"),
    )(attn, rec, proj_h, x, g_attn, g_hgrn, g_norm2, w_out)


def _dmix_post_bwd(dh1b, w_out, attn, rec, proj_h, g_attn, g_hgrn, *, name, tm=512):
    S, D = dh1b.shape
    AW, W = ATTN_W, HGRN_W

    def body(dh_ref, w_ref, a_ref, r_ref, hg_ref, ga_ref, gh_ref, do_ref, dl_ref, dr_ref, dhg_ref, dga_ref, dgh_ref):
        first = pl.program_id(0) == 0
        dmix = _nt(dh_ref[...], w_ref[...])
        av = a_ref[...]
        dov, dga = _norm_bwd(av, ga_ref[...], dmix[:, :AW])
        do_ref[...] = dov
        shift = HEAD_DIM.bit_length() - 1
        hi = lax.shift_right_logical(lax.broadcasted_iota(jnp.int32, (AW, AW), 0), shift)
        hj = lax.shift_right_logical(lax.broadcasted_iota(jnp.int32, (AW, AW), 1), shift)
        prod = dov * av
        hi_part = prod.astype(BF16)
        lo_part = (prod - hi_part.astype(F32)).astype(BF16)
        same_head = jnp.where(hi == hj, 1.0, 0.0).astype(BF16)
        dl_ref[...] = _nn(hi_part, same_head) + _nn(lo_part, same_head)
        _accumulate(dga_ref, jnp.sum(dga, axis=0, keepdims=True), first)

        @pl.when(first)
        def _():
            dgh_ref[...] = jnp.zeros_like(dgh_ref)

        for h in range(HGRN_HEADS):
            sl = slice(HGRN_DIM * h, HGRN_DIM * (h + 1))
            rv, hg, gv = r_ref[:, sl], hg_ref[:, sl], gh_ref[:, sl]
            dout = dmix[:, AW + HGRN_DIM * h:AW + HGRN_DIM * (h + 1)]
            sg = _sigmoid(hg)
            drv, dgh = _norm_bwd(rv, gv, dout * (hg * sg))
            dr_ref[:, sl] = drv
            dgh_ref[:, sl] += jnp.sum(dgh, axis=0, keepdims=True)
            dhg_ref[:, sl] = (dout * (rv * _rstd(rv) * gv) * (sg * (1.0 + hg * (1.0 - sg)))).astype(BF16)

    row = lambda w, j=0: pl.BlockSpec((tm, w), lambda i: (i, j))
    vec = lambda w: pl.BlockSpec((1, w), lambda i: (0, 0))
    return pl.pallas_call(
        body, name=name, grid=(S // tm,),
        in_specs=[row(D), _resident(w_out.shape), row(AW), row(W), row(W, 3), vec(AW), vec(W)],
        out_specs=[row(AW), row(AW), row(W), row(W), vec(AW), vec(W)],
        out_shape=[jax.ShapeDtypeStruct((S, AW), F32), jax.ShapeDtypeStruct((S, AW), F32),
                   jax.ShapeDtypeStruct((S, W), F32), jax.ShapeDtypeStruct((S, W), BF16),
                   jax.ShapeDtypeStruct((1, AW), F32), jax.ShapeDtypeStruct((1, W), F32)],
        compiler_params=_cparams("arbitrary"),
    )(dh1b, w_out, attn, rec, proj_h, g_attn, g_hgrn)


def _conv_act(g, g1, g2, w_ref, b_ref):
    c = b_ref[...] + w_ref[0:1, :] * g2 + w_ref[1:2, :] * g1 + w_ref[2:3, :] * g
    return c, 0.5 * (1.0 + lax.erf(c * (2.0 ** -0.5)))


def _shift_down(g, halo, row):
    g1 = jnp.where(row == 0, halo[7:8], pltpu.roll(g, 1, 0))
    g2 = jnp.where(row == 0, halo[6:7], jnp.where(row == 1, halo[7:8], pltpu.roll(g, 2, 0)))
    return g1, g2


def _shift_up(x, halo, row):
    n = x.shape[0]
    x1 = jnp.where(row == n - 1, halo[0:1], pltpu.roll(x, n - 1, 0))
    x2 = jnp.where(row == n - 2, halo[0:1], jnp.where(row == n - 1, halo[1:2], pltpu.roll(x, n - 2, 0)))
    return x1, x2


def _up_glu(u, w_up, conv_w, conv_b, *, name, tm=1024, tn=256):
    S, D = u.shape
    F = w_up.shape[1] // 2
    nf = F // tn

    def body(u_ref, wg_ref, wv_ref, cw_ref, cb_ref, g_ref, v_ref, a_ref, halo_ref):
        i, j = pl.program_id(0), pl.program_id(1)

        @pl.when(i == 0)
        def _():
            halo_ref[j] = jnp.zeros((SUBLANES, tn), F32)

        uv = u_ref[...]
        g, v = _nn(uv, wg_ref[...]), _nn(uv, wv_ref[...])
        row = lax.broadcasted_iota(jnp.int32, (tm, tn), 0)
        g1, g2 = _shift_down(g, halo_ref[j], row)
        c, cdf = _conv_act(g, g1, g2, cw_ref, cb_ref)
        a_ref[...] = (c * cdf * v).astype(BF16)
        g_ref[...] = g.astype(BF16)
        v_ref[...] = v.astype(BF16)
        halo_ref[j] = g[tm - SUBLANES:, :]

    col = pl.BlockSpec((tm, tn), lambda i, j: (i, j))
    out = jax.ShapeDtypeStruct((S, F), BF16)
    return pl.pallas_call(
        body, name=name, grid=(S // tm, nf),
        in_specs=[pl.BlockSpec((tm, D), lambda i, j: (i, 0)), pl.BlockSpec((D, tn), lambda i, j: (0, j)),
                  pl.BlockSpec((D, tn), lambda i, j: (0, j + nf)), pl.BlockSpec((3, tn), lambda i, j: (0, j)),
                  pl.BlockSpec((1, tn), lambda i, j: (0, j))],
        out_specs=[col, col, col], out_shape=[out, out, out],
        scratch_shapes=[pltpu.VMEM((nf, SUBLANES, tn), F32)], compiler_params=_cparams("arbitrary", "arbitrary"),
    )(u, w_up, w_up, conv_w, conv_b)


def _dact_glu_bwd(dh2b, w_down, gate, val, conv_w, conv_b, *, name, tm=1024, tn=256):
    S, D = dh2b.shape
    F = gate.shape[1]
    nf, ni = F // tn, S // tm
    hb = tm // SUBLANES

    def body(dh_ref, wd_ref, g_ref, gh_ref, v_ref, cw_ref, cb_ref, dg_ref, dv_ref, dcw_ref, dcb_ref, halo_ref, acc_ref):
        i, j = pl.program_id(0), pl.program_id(1)

        @pl.when(i == 0)
        def _():
            halo_ref[j] = jnp.zeros((SUBLANES, tn), F32)
            acc_ref[j] = jnp.zeros((SUBLANES, tn), F32)

        g = g_ref[...].astype(F32)
        before = jnp.where(i < ni - 1, gh_ref[...].astype(F32), 0.0)
        row = lax.broadcasted_iota(jnp.int32, (tm, tn), 0)
        g1, g2 = _shift_down(g, before[SUBLANES:], row)
        c, cdf = _conv_act(g, g1, g2, cw_ref, cb_ref)
        da = _nt(dh_ref[...], wd_ref[...])
        dv_ref[...] = (da * (c * cdf)).astype(BF16)
        pdf = jnp.exp(-0.5 * c * c) * (1.0 / (2.0 * jnp.pi) ** 0.5)
        dc = da * v_ref[...].astype(F32) * (cdf + c * pdf)
        d1, d2 = _shift_up(dc, halo_ref[j], row)
        dg_ref[...] = (cw_ref[2:3, :] * dc + cw_ref[1:2, :] * d1 + cw_ref[0:1, :] * d2).astype(BF16)
        halo_ref[j] = dc[:SUBLANES, :]
        for k, t in enumerate((dc * g2, dc * g1, dc * g, dc)):
            acc_ref[j, k:k + 1, :] += jnp.sum(t, axis=0, keepdims=True)

        @pl.when((i == ni - 1) & (j == nf - 1))
        def _():
            for jj in range(nf):
                dcw_ref[:, jj * tn:(jj + 1) * tn] = acc_ref[jj, 0:3, :]
                dcb_ref[:, jj * tn:(jj + 1) * tn] = acc_ref[jj, 3:4, :]

    tile = pl.BlockSpec((tm, tn), lambda i, j: (ni - 1 - i, j))
    return pl.pallas_call(
        body, name=name, grid=(ni, nf),
        in_specs=[pl.BlockSpec((tm, D), lambda i, j: (ni - 1 - i, 0)), pl.BlockSpec((tn, D), lambda i, j: (j, 0)),
                  tile, pl.BlockSpec((SUBLANES * 2, tn), lambda i, j: (jnp.maximum((ni - 1 - i) * (hb // 2) - 1, 0), j)),
                  tile, pl.BlockSpec((3, tn), lambda i, j: (0, j)), pl.BlockSpec((1, tn), lambda i, j: (0, j))],
        out_specs=[tile, tile, pl.BlockSpec((3, F), lambda i, j: (0, 0)), pl.BlockSpec((1, F), lambda i, j: (0, 0))],
        out_shape=[jax.ShapeDtypeStruct((S, F), BF16), jax.ShapeDtypeStruct((S, F), BF16),
                   jax.ShapeDtypeStruct((3, F), F32), jax.ShapeDtypeStruct((1, F), F32)],
        scratch_shapes=[pltpu.VMEM((nf, SUBLANES, tn), F32), pltpu.VMEM((nf, SUBLANES, tn), F32)],
        compiler_params=_cparams("arbitrary", "arbitrary"),
    )(dh2b, w_down, gate, gate, val, conv_w, conv_b)


def _down_loss(act, w_down, h1, g, target, *, name, tm=512):
    S, F = act.shape
    D = h1.shape[1]

    def body(a_ref, w_ref, h_ref, g_ref, t_ref, dh_ref, dhb_ref, dg_ref, loss_ref):
        first = pl.program_id(0) == 0
        h2 = h_ref[...] + _nn(a_ref[...], w_ref[...])
        gv = g_ref[...]
        r = _rstd(h2)
        xh = h2 * r
        err = xh * gv - t_ref[...]
        part_loss = 0.5 * jnp.sum(jnp.mean(err * err, axis=-1, keepdims=True), axis=0, keepdims=True)
        dy = err * (1.0 / D)
        dxh = dy * gv
        dh = r * (dxh - xh * jnp.mean(dxh * xh, axis=-1, keepdims=True))
        dh_ref[...] = dh
        dhb_ref[...] = dh.astype(BF16)
        _accumulate(dg_ref, jnp.sum(dy * xh, axis=0, keepdims=True), first)
        _accumulate(loss_ref, jnp.broadcast_to(part_loss, (1, LANES)), first)

    row = lambda w: pl.BlockSpec((tm, w), lambda i: (i, 0))
    vec = lambda w: pl.BlockSpec((1, w), lambda i: (0, 0))
    return pl.pallas_call(
        body, name=name, grid=(S // tm,), in_specs=[row(F), _resident(w_down.shape), row(D), vec(D), row(D)],
        out_specs=[row(D), row(D), vec(D), vec(LANES)],
        out_shape=[jax.ShapeDtypeStruct((S, D), F32), jax.ShapeDtypeStruct((S, D), BF16),
                   jax.ShapeDtypeStruct((1, D), F32), jax.ShapeDtypeStruct((1, LANES), F32)],
        compiler_params=_cparams("arbitrary"),
    )(act, w_down, h1, g, target)


def _grad_norm_input(pieces, ws, x, g, add, *, name, tm=512):
    S, D = x.shape
    widths = [p.shape[1] for p in pieces]
    n, nw = len(pieces), len(ws)
    where, wi, off = [], 0, 0
    for wd in widths:
        if off == ws[wi].shape[1]:
            wi, off = wi + 1, 0
        where.append((wi, off))
        off += wd

    def body(*refs):
        p_refs, w_refs = refs[:n], refs[n:n + nw]
        x_ref, g_ref, add_ref, dx_ref, dxb_ref, dg_ref = refs[n + nw:]
        halves = _row_halves(tm)
        du = []
        for rows in halves:
            terms = [_nt(p_refs[k][rows, :], w_refs[wi][:, off:off + widths[k]]) for k, (wi, off) in enumerate(where)]
            du.append(sum(terms[1:], terms[0]))
        dg_sum = None
        for rows, duh in zip(halves, du):
            dx, dg = _norm_bwd(x_ref[rows, :], g_ref[...], duh)
            dx = add_ref[rows, :] + dx
            dx_ref[rows, :] = dx
            dxb_ref[rows, :] = dx.astype(BF16)
            part = jnp.sum(dg, axis=0, keepdims=True)
            dg_sum = part if dg_sum is None else dg_sum + part
        _accumulate(dg_ref, dg_sum, pl.program_id(0) == 0)

    row = lambda w_: pl.BlockSpec((tm, w_), lambda i: (i, 0))
    vec = pl.BlockSpec((1, D), lambda i: (0, 0))
    return pl.pallas_call(
        body, name=name, grid=(S // tm,),
        in_specs=[row(wd) for wd in widths] + [_resident(w.shape) for w in ws] + [row(D), vec, row(D)],
        out_specs=[row(D), row(D), vec],
        out_shape=[jax.ShapeDtypeStruct((S, D), F32), jax.ShapeDtypeStruct((S, D), BF16),
                   jax.ShapeDtypeStruct((1, D), F32)],
        compiler_params=_cparams("arbitrary"),
    )(*pieces, *ws, x, g, add)


def _rows(a):
    return a.reshape(-1, a.shape[-1])


def _row_tile(rows, cols, itemsize=4, budget=1 << 20):
    t = rows
    while t % 32 == 0 and t * cols * itemsize > budget:
        t //= 2
    return t


def _sum_cast(arrs, out_dtype, *, name):
    shape = arrs[0].shape
    flat = [_rows(a) for a in arrs]
    R, C = flat[0].shape
    tr = _row_tile(R, C)

    def body(*refs):
        acc = refs[0][...].astype(F32)
        for r in refs[1:-1]:
            acc = acc + r[...].astype(F32)
        refs[-1][...] = acc.astype(out_dtype)

    spec = pl.BlockSpec((tr, C), lambda i: (i, 0))
    return pl.pallas_call(
        body, name=name, grid=(R // tr,), in_specs=[spec] * len(flat), out_specs=spec,
        out_shape=jax.ShapeDtypeStruct((R, C), out_dtype), compiler_params=_cparams("parallel"),
    )(*flat).reshape(shape)


def _adamw(parts, w, m, v, *, name):
    shape = w.shape
    w2, m2, v2 = _rows(w), _rows(m), _rows(v)
    R, C = w2.shape
    parts = [p.reshape(-1, R, C) for p in parts]
    tr = _row_tile(R, C)
    np_ = len(parts)
    c1, c2 = 1.0 - ADAM_B1 ** ADAM_STEP, 1.0 - ADAM_B2 ** ADAM_STEP

    def body(*refs):
        terms = [(r, k) for r in refs[:np_] for k in range(r.shape[0])]
        g = terms[0][0][terms[0][1]].astype(F32)
        for r, k in terms[1:]:
            g = g + r[k].astype(F32)
        w_ref, m_ref, v_ref, g_out, d_out, m_out, v_out = refs[np_:]
        mn = ADAM_B1 * m_ref[...] + (1.0 - ADAM_B1) * g
        vn = ADAM_B2 * v_ref[...] + (1.0 - ADAM_B2) * (g * g)
        g_out[...] = g
        d_out[...] = -ADAM_LR * ((mn / c1) / (jnp.sqrt(vn / c2) + ADAM_EPS) + ADAM_WD * w_ref[...])
        m_out[...] = mn
        v_out[...] = vn

    spec = pl.BlockSpec((tr, C), lambda i: (i, 0))
    out = jax.ShapeDtypeStruct((R, C), F32)
    stacks = [pl.BlockSpec((p.shape[0], tr, C), lambda i: (0, i, 0)) for p in parts]
    res = pl.pallas_call(
        body, name=name, grid=(R // tr,), in_specs=stacks + [spec] * 3, out_specs=[spec] * 4,
        out_shape=[out] * 4, compiler_params=_cparams("parallel"),
    )(*parts, w2, m2, v2)
    return [r.reshape(shape) for r in res]


def _adamw_packed(stack, widths, params, *, name):
    c1, c2 = 1.0 - ADAM_B1 ** ADAM_STEP, 1.0 - ADAM_B2 ** ADAM_STEP
    k = stack.shape[0]
    flat = [None if p is None else [_rows(a) for a in p] for p in params]
    n_in = sum(3 for p in flat if p is not None)

    def body(*refs):
        s_ref, ins, outs = refs[0], list(refs[1:1 + n_in]), list(refs[1 + n_in:])
        off = 0
        for width, p in zip(widths, flat):
            rows = 1 if p is None else p[0].shape[0]
            cols = width // rows
            w_ref, m_ref, v_ref = (None, None, None) if p is None else (ins.pop(0), ins.pop(0), ins.pop(0))
            o_refs = [outs.pop(0) for _ in range(1 if p is None else 4)]
            for r in range(rows):
                seg = slice(off + r * cols, off + (r + 1) * cols)
                g = s_ref[0, :, seg]
                for j in range(1, k):
                    g = g + s_ref[j, :, seg]
                o_refs[0][r:r + 1, :] = g
                if p is not None:
                    row = slice(r, r + 1)
                    mn = ADAM_B1 * m_ref[row, :] + (1.0 - ADAM_B1) * g
                    vn = ADAM_B2 * v_ref[row, :] + (1.0 - ADAM_B2) * (g * g)
                    o_refs[1][row, :] = -ADAM_LR * ((mn / c1) / (jnp.sqrt(vn / c2) + ADAM_EPS) + ADAM_WD * w_ref[row, :])
                    o_refs[2][row, :] = mn
                    o_refs[3][row, :] = vn
            off += width

    operands, out_shape = [stack], []
    for width, p in zip(widths, flat):
        if p is None:
            out_shape.append(jax.ShapeDtypeStruct((1, width), F32))
        else:
            operands += p
            out_shape += [jax.ShapeDtypeStruct(p[0].shape, F32)] * 4
    res = list(pl.pallas_call(body, name=name, out_shape=out_shape)(*operands))
    out = []
    for p, orig in zip(flat, params):
        n = 1 if p is None else 4
        out.append([r if orig is None else r.reshape(orig[0].shape) for r in res[:n]])
        res = res[n:]
    return out


def _coords():
    return lax.axis_index("x"), lax.axis_index("y"), lax.axis_index("c")


def _all_gather(shards, *, name):
    n = len(shards)

    def body(*refs):
        x_refs, out_refs = refs[:n], refs[n:2 * n]
        send_sems, recv_sems, local_sems = refs[2 * n:]
        x, y, c = _coords()
        me, sibling = (x, y, c), (x, y, 1 - c)
        chips = [(1 - x, y), (x, 1 - y), (1 - x, 1 - y)]

        def slot(a, dev):
            return out_refs[a].at[4 * dev[0] + 2 * dev[1] + dev[2]]

        def copy(a, k, block, to, src=None):
            return pltpu.make_async_remote_copy(
                src_ref=slot(a, block) if src is None else src, dst_ref=slot(a, block),
                send_sem=send_sems.at[7 * a + k], recv_sem=recv_sems.at[7 * a + k], device_id=to, device_id_type=MESH)

        mine = [pltpu.make_async_copy(x_refs[a], slot(a, me), local_sems.at[a]) for a in range(n)]
        for cp in mine:
            cp.start()
        first = []
        for a in range(n):
            first.append(copy(a, 0, me, sibling, src=x_refs[a]))
            first += [copy(a, 1 + j, me, (*chip, c), src=x_refs[a]) for j, chip in enumerate(chips)]
        for cp in first:
            cp.start()
        passed = []
        for j, chip in enumerate(chips):
            for a in range(n):
                copy(a, 1 + j, (*chip, c), me).wait_recv()
                fwd = copy(a, 4 + j, (*chip, c), sibling)
                fwd.start()
                passed.append(fwd)
        for a in range(n):
            copy(a, 0, sibling, me).wait_recv()
            for j, chip in enumerate(chips):
                copy(a, 4 + j, (*chip, 1 - c), me).wait_recv()
        for cp in first + passed:
            cp.wait_send()
        for cp in mine:
            cp.wait()

    return pl.pallas_call(
        body, name=name, in_specs=[HBM] * n, out_specs=[HBM] * n,
        out_shape=[jax.ShapeDtypeStruct((N_DEV, *s.shape), s.dtype) for s in shards],
        scratch_shapes=[pltpu.SemaphoreType.DMA((7 * n,)), pltpu.SemaphoreType.DMA((7 * n,)),
                        pltpu.SemaphoreType.DMA((n,))],
    )(*shards)


def _flip_y(x, y, c):
    return (x, 1 - y, c)


def _flip_x(x, y, c):
    return (1 - x, y, c)


def _flip_xy(x, y, c):
    return (1 - x, 1 - y, c)


SEM = pl.BlockSpec(memory_space=pltpu.SEMAPHORE)
SIDE_EFFECT = pltpu.SideEffectType.DATAFLOW_SIDE_EFFECTING


def _in_hbm(a):
    return pltpu.with_memory_space_constraint(a, pltpu.HBM)


def _copies_start(srcs, lands, plan, n_copies, *, name):
    ns, nl = len(srcs), len(lands)

    def body(*refs):
        src_refs, land_refs = refs[:ns], refs[ns:ns + nl]
        send_sems, recv_sems = refs[ns + nl:ns + nl + 2]
        token = refs[-1]
        for k, (src, dst, peer, _) in enumerate(plan(src_refs, land_refs, *_coords())):
            pltpu.make_async_remote_copy(src_ref=src, dst_ref=dst, send_sem=send_sems.at[k], recv_sem=recv_sems.at[k],
                                         device_id=peer, device_id_type=MESH).start()
        token[...] = jnp.zeros_like(token)

    bufs = [*srcs, *lands]
    res = pl.pallas_call(
        body, name=name, in_specs=[HBM] * (ns + nl),
        out_specs=(SEM, SEM, *[HBM] * (ns + nl), pl.BlockSpec(memory_space=pltpu.VMEM)),
        out_shape=(pltpu.SemaphoreType.DMA((n_copies,)), pltpu.SemaphoreType.DMA((n_copies,)),
                   *[pltpu.HBM(b.shape, b.dtype) for b in bufs], jax.ShapeDtypeStruct((SUBLANES, LANES), F32)),
        input_output_aliases={i: 2 + i for i in range(ns + nl)},
        compiler_params=pltpu.CompilerParams(has_side_effects=SIDE_EFFECT),
    )(*[_in_hbm(b) for b in bufs])
    return res[0], res[1], list(res[2:2 + ns]), list(res[2 + ns:2 + ns + nl]), res[-1]


def _copies_wait(started, plan, after, *, name):
    send_sems, recv_sems, srcs, lands, _ = started
    ns, nl = len(srcs), len(lands)

    def body(*refs):
        src_refs, land_refs = refs[:ns], refs[ns:ns + nl]
        send_sems, recv_sems = refs[ns + nl:ns + nl + 2]
        for k, (src, dst, peer, here) in enumerate(plan(src_refs, land_refs, *_coords())):
            pltpu.make_async_remote_copy(src_ref=src, dst_ref=dst, send_sem=send_sems.at[k], recv_sem=recv_sems.at[k],
                                         device_id=peer, device_id_type=MESH).wait_send()
            pltpu.make_async_remote_copy(src_ref=src, dst_ref=here, send_sem=send_sems.at[k], recv_sem=recv_sems.at[k],
                                         device_id=peer, device_id_type=MESH).wait_recv()

    bufs = [*srcs, *lands]
    res = pl.pallas_call(
        body, name=name, in_specs=[HBM] * (ns + nl) + [SEM, SEM, pl.BlockSpec(memory_space=pl.ANY)],
        out_specs=[HBM] * (ns + nl), out_shape=[pltpu.HBM(b.shape, b.dtype) for b in bufs],
        input_output_aliases={i: i for i in range(ns + nl)},
        compiler_params=pltpu.CompilerParams(has_side_effects=SIDE_EFFECT),
    )(*bufs, send_sems, recv_sems, after)
    return list(res[ns:])


def _dev_index(dev):
    return 4 * dev[0] + 2 * dev[1] + dev[2]


def _ag_chips_plan(src_refs, land_refs, x, y, c):
    me = _dev_index((x, y, c))
    return [(src, land.at[me], peer, land.at[_dev_index(peer)])
            for src, land in zip(src_refs, land_refs) for peer in (_flip_y(x, y, c), _flip_x(x, y, c), _flip_xy(x, y, c))]


def _ag_sibling_plan(src_refs, land_refs, x, y, c):
    chips = [(x, y), (x, 1 - y), (1 - x, y), (1 - x, 1 - y)]
    return [(land.at[_dev_index((*chip, c))], land.at[_dev_index((*chip, c))], (x, y, 1 - c),
             land.at[_dev_index((*chip, 1 - c))]) for land in land_refs for chip in chips]


def _ag_direct_plan(src_refs, land_refs, x, y, c):
    me = _dev_index((x, y, c))
    plan = []
    for src, land in zip(src_refs, land_refs):
        for m in range(1, N_DEV):
            peer = (x + (m >> 2) * (1 - 2 * x), y + ((m >> 1) & 1) * (1 - 2 * y), c + (m & 1) * (1 - 2 * c))
            plan.append((src, land.at[me], peer, land.at[_dev_index(peer)]))
    return plan


def _rs_direct_plan(src_refs, land_refs, x, y, c):
    plan = []
    for src, land in zip(src_refs, land_refs):
        for m in range(1, N_DEV):
            peer = (x + (m >> 2) * (1 - 2 * x), y + ((m >> 1) & 1) * (1 - 2 * y), c + (m & 1) * (1 - 2 * c))
            plan.append((src.at[_dev_index(peer)], land.at[m - 1], peer, land.at[m - 1]))
    return plan


def _rs_start(grads, me, *, name):
    own = [lax.dynamic_index_in_dim(g, me, 0, keepdims=False) for g in grads]
    lands = [lax.empty((N_DEV - 1, *g.shape[1:]), g.dtype) for g in grads]
    return _copies_start(grads, lands, _rs_direct_plan, (N_DEV - 1) * len(grads), name=name), own


def _rs_finish(started, after, *, name):
    handle, own = started
    got = _copies_wait(handle, _rs_direct_plan, after, name=name)
    return [[o, land] for o, land in zip(own, got)]


def _by_device_cols(w):
    K, N = w.shape
    return w.reshape(K, N_DEV, N // N_DEV).transpose(1, 0, 2)


def _gathered_cols(w8):
    return w8.transpose(1, 0, 2).reshape(w8.shape[1], -1)


def _pair_major(w, inverse=False):
    K = w.shape[0]
    a, b = (ATTN_W // LANES, 3) if inverse else (3, ATTN_W // LANES)
    return w.reshape(K, a, b, LANES).transpose(0, 2, 1, 3).reshape(K, 3 * ATTN_W)


def kernel(x, norm1_g, w_in, attn_norm_g, hgrn_norm_g, hgrn_lb_logits, w_out, norm2_g, w_up, conv_w, conv_b, w_down, final_norm_g, loss_target, m_norm1_g, m_w_in, m_attn_norm_g, m_hgrn_norm_g, m_hgrn_lb_logits, m_w_out, m_norm2_g, m_w_up, m_conv_w, m_conv_b, m_w_down, m_final_norm_g, v_norm1_g, v_w_in, v_attn_norm_g, v_hgrn_norm_g, v_hgrn_lb_logits, v_w_out, v_norm2_g, v_w_up, v_conv_w, v_conv_b, v_w_down, v_final_norm_g):
    xs, target = x[0], loss_target[0]
    S, D = xs.shape
    NA = 3 * ATTN_W
    fng = final_norm_g.reshape(1, D)

    casts = [_sum_cast([w[0]], BF16, name=f"cast_{nm}") for nm, w in
             (("w_in", w_in), ("w_out", w_out), ("w_up", w_up), ("w_down", w_down))]
    me = _dev_index(_coords())
    (g_in,) = _all_gather(casts[:1], name="ag_w_in")
    later = casts[1:] + [conv_w[0]]
    ag1 = _copies_start(later, [lax.empty((N_DEV, *s.shape), s.dtype) for s in later], _ag_chips_plan,
                        3 * len(later), name="ag_chips_start")
    wi = _gathered_cols(g_in)
    wi_a, wi_h = _pair_major(wi[:, :NA]), wi[:, NA:]

    u1, proj_a = _proj_attn(xs, norm1_g + ag1[4][0, 0], wi_a, name="proj_attn")
    proj_h = _mm(u1, wi_h, name="proj_hgrn")
    attn, lse = _attn_fwd(proj_a, name="attn_fwd")
    lands = _copies_wait(ag1, _ag_chips_plan, attn, name="ag_chips_wait")
    lands = [lax.dynamic_update_index_in_dim(l, s, me, 0) for l, s in zip(lands, later)]
    ag2 = _copies_start([], lands, _ag_sibling_plan, 4 * len(later), name="ag_sibling_start")
    rec, states = _hgrn_fwd(proj_h, hgrn_lb_logits + ag2[4][0, 0], name="hgrn_fwd")
    g_out, g_up, g_down, g_cw = _copies_wait(ag2, _ag_sibling_plan, rec, name="ag_sibling_wait")
    wo = g_out.reshape(-1, D)
    wu = _gathered_cols(g_up)
    wd = g_down.reshape(-1, D)
    cw = _gathered_cols(g_cw)
    h1, u2, mixed = _out_proj(attn, rec, proj_h, xs, attn_norm_g, hgrn_norm_g, norm2_g, wo, name="out_proj")
    gate, val, act = _up_glu(u2, wu, cw, conv_b, name="up_glu")
    dh2, dh2b, d_fng, loss_part = _down_loss(act, wd, h1, fng, target, name="down_loss")

    dgate, dval, d_cw, d_cb = _dact_glu_bwd(dh2b, wd, gate, val, cw, conv_b, name="dact_glu_bwd")
    dw_down = _mm_tn(act, dh2b, tm=256, name="dw_down")
    dh1, dh1b, d_n2g = _grad_norm_input([dgate, dval], [wu], h1, norm2_g, dh2, name="du2_norm2_bwd")
    dw_up = [_mm_tn(u2, dy, tm=1024, tn=256, name=f"dw_up_{nm}") for nm, dy in (("gate", dgate), ("val", dval))]
    rs_ffn = _rs_start([dw_down.reshape(N_DEV, -1, D),
                        jnp.concatenate([h.reshape(D, N_DEV // 2, -1).transpose(1, 0, 2) for h in dw_up], axis=0)],
                       me, name="rs_ffn_start")
    dattn, delta, drec, dhg, d_ang, d_hng = _dmix_post_bwd(dh1b, wo, attn, rec, proj_h, attn_norm_g + rs_ffn[0][4][0, 0],
                                                          hgrn_norm_g, name="dmix_post_bwd")
    dw_out = _mm_tn(mixed, dh1b, name="dw_out")
    rs_out = _rs_start([dw_out.reshape(N_DEV, -1, D)], me, name="rs_out_start")
    dproj_h, d_lbl = _hgrn_bwd(proj_h, hgrn_lb_logits + rs_out[0][4][0, 0], states, drec, name="hgrn_bwd")
    small = [("loss", loss_part, None, None, None),
             ("attn_norm_g", d_ang, attn_norm_g, m_attn_norm_g, v_attn_norm_g),
             ("hgrn_norm_g", d_hng, hgrn_norm_g, m_hgrn_norm_g, v_hgrn_norm_g),
             ("hgrn_lb_logits", d_lbl, hgrn_lb_logits, m_hgrn_lb_logits, v_hgrn_lb_logits),
             ("norm2_g", d_n2g, norm2_g, m_norm2_g, v_norm2_g),
             ("conv_b", d_cb, conv_b, m_conv_b, v_conv_b),
             ("final_norm_g", d_fng, final_norm_g, m_final_norm_g, v_final_norm_g)]
    pack = lambda arrs: jnp.concatenate([a.reshape(1, -1) for a in arrs], axis=1)
    small_own = [pack([s[1] for s in small]), d_cw]
    ag_small = _copies_start(small_own, [lax.empty((N_DEV, *s.shape), s.dtype) for s in small_own], _ag_direct_plan,
                             (N_DEV - 1) * len(small_own), name="ag_small_start")
    dproj_a = _attn_bwd(proj_a, dattn, lse, delta, name="attn_bwd")
    dw_in = jnp.concatenate([_pair_major(_mm_tn(u1, dproj_a, tm=1024, tn=512, name="dw_in_attn"), inverse=True),
                             _mm_tn(u1, dproj_h, tm=1024, tn=512, name="dw_in_hgrn"),
                             _mm_tn(u1, dhg, tm=1024, tn=512, name="dw_in_gate")], axis=1)
    rs_in = _rs_start([_by_device_cols(dw_in)], me, name="rs_in_start")
    grad_x, _, d_n1g = _grad_norm_input([dproj_a, dproj_h, dhg], [wi_a, wi_h], xs, norm1_g + rs_in[0][4][0, 0], dh1,
                                        name="du1_norm1_bwd")

    res = {}

    def update(nm, parts, w, m, v):
        res[nm] = _adamw(parts, w, m, v, name=f"adamw_{nm}")

    g_down, g_up = _rs_finish(rs_ffn, grad_x, name="rs_ffn_wait")
    update("w_down", g_down, w_down, m_w_down, v_w_down)
    update("w_up", g_up, w_up, m_w_up, v_w_up)
    (g_out,) = _rs_finish(rs_out, grad_x, name="rs_out_wait")
    update("w_out", g_out, w_out, m_w_out, v_w_out)
    (g_in,) = _rs_finish(rs_in, res["w_up"][1], name="rs_in_wait")
    update("w_in", g_in, w_in, m_w_in, v_w_in)

    g_small, g_dcw = [lax.dynamic_update_index_in_dim(l, s, me, 0)
                      for l, s in zip(_copies_wait(ag_small, _ag_direct_plan, grad_x, name="ag_small_wait"), small_own)]
    sm = _adamw_packed(g_small, [s[1].size for s in small], [None if s[2] is None else s[2:] for s in small],
                       name="adamw_small")
    for (nm, *_), r in zip(small, sm):
        res[nm] = r
    ncw = conv_w.shape[-1]
    mine_cw = lax.dynamic_slice_in_dim(g_dcw, me * ncw, ncw, axis=2)
    res["conv_w"] = _adamw([mine_cw], conv_w, m_conv_w, v_conv_w, name="adamw_conv_w")
    late, _ = lax.optimization_barrier((d_n1g, res["w_in"][1]))
    update("norm1_g", _all_gather([late], name="ag_norm1_grad"), norm1_g, m_norm1_g, v_norm1_g)

    loss = res["loss"][0][0, 0]
    order = ["norm1_g", "w_in", "attn_norm_g", "hgrn_norm_g", "hgrn_lb_logits", "w_out", "norm2_g", "w_up",
             "conv_w", "conv_b", "w_down", "final_norm_g"]
    return (loss, grad_x[None], *[res[nm][0] for nm in order], *[res[nm][1] for nm in order],
            *[res[nm][2] for nm in order], *[res[nm][3] for nm in order])
```

```python
import jax
import jax.numpy as jnp
from jax import lax
from jax.experimental import pallas as pl
from jax.experimental.pallas import tpu as pltpu

F32, BF16 = jnp.float32, jnp.bfloat16
NORM_EPS = 1e-6
ATTN_HEADS, HEAD_DIM, ATTN_BLOCK = 8, 64, 128
DILATIONS = (1, 4, 16)
ATTN_SCALE = HEAD_DIM ** -0.5
ATTN_W = ATTN_HEADS * HEAD_DIM
HGRN_HEADS, HGRN_DIM, HGRN_CHUNK = 4, 128, 64
HGRN_W = HGRN_HEADS * HGRN_DIM
ADAM_LR, ADAM_B1, ADAM_B2, ADAM_EPS, ADAM_WD, ADAM_STEP = 0.001, 0.9, 0.999, 1e-08, 0.01, 10
LANES, SUBLANES = 128, 8
VMEM_LIMIT_BYTES = 56 * 1024 * 1024
N_DEV = 8
MESH = pl.DeviceIdType.MESH
HBM = pl.BlockSpec(memory_space=pltpu.HBM)
HIGHEST = lax.Precision.HIGHEST


def _cparams(*sem):
    return pltpu.CompilerParams(dimension_semantics=sem, vmem_limit_bytes=VMEM_LIMIT_BYTES)


def _tile(n, pref):
    if n <= pref:
        return n
    t = (pref // LANES) * LANES
    while n % t:
        t -= LANES
    return t


def _resident(shape):
    return pl.BlockSpec(shape, lambda *_: (0,) * len(shape), pipeline_mode=pl.Buffered(1))


def _dot(a, b, dims, precision=None):
    return lax.dot_general(a, b, (dims, ((), ())), precision=precision, preferred_element_type=F32)


def _nn(a, b, precision=None):
    return _dot(a, b, ((1,), (0,)), precision)


def _nt(a, b):
    return _dot(a, b, ((1,), (1,)))


def _tn(a, b):
    return _dot(a, b, ((0,), (0,)))


def _sigmoid(x):
    return 1.0 / (1.0 + jnp.exp(-x))


def _rstd(x):
    return lax.rsqrt(jnp.mean(x * x, axis=-1, keepdims=True) + NORM_EPS)


def _norm_bwd(x, g, du):
    r = _rstd(x)
    xh = x * r
    dxh = du * g
    return r * (dxh - xh * jnp.mean(dxh * xh, axis=-1, keepdims=True)), du * xh


def _row_halves(tm):
    return [pl.ds(0, tm // 2), pl.ds(tm // 2, tm // 2)]


def _accumulate(ref, part, first):
    @pl.when(first)
    def _():
        ref[...] = part

    @pl.when(jnp.logical_not(first))
    def _():
        ref[...] += part


def _mm_nt(a, bt, row0, n, *, name, out_dtype=F32, tm=1024, tn=512):
    M, K = a.shape
    tm, tn = _tile(M, tm), _tile(n, tn)
    j0 = row0 // tn

    def body(a_ref, b_ref, o_ref):
        o_ref[...] = _nt(a_ref[...], b_ref[...]).astype(out_dtype)

    return pl.pallas_call(
        body, name=name, grid=(M // tm, n // tn),
        in_specs=[pl.BlockSpec((tm, K), lambda i, j: (i, 0)), pl.BlockSpec((tn, K), lambda i, j: (j + j0, 0))],
        out_specs=pl.BlockSpec((tm, tn), lambda i, j: (i, j)), out_shape=jax.ShapeDtypeStruct((M, n), out_dtype),
        compiler_params=_cparams("parallel", "parallel"),
    )(a, bt)


def _mm_tn(x, dy, *, name, tm=512, tn=1024):
    S, M = x.shape
    N = dy.shape[1]
    tm, tn = _tile(M, tm), _tile(N, tn)

    def body(x_ref, dy_ref, o_ref, xt_ref):
        @pl.when(pl.program_id(1) == 0)
        def _():
            xt_ref[...] = x_ref[...].T

        o_ref[...] = _nn(xt_ref[...], dy_ref[...]).astype(BF16)

    return pl.pallas_call(
        body, name=name, grid=(M // tm, N // tn),
        in_specs=[pl.BlockSpec((S, tm), lambda i, j: (0, i)), pl.BlockSpec((S, tn), lambda i, j: (0, j))],
        out_specs=pl.BlockSpec((tm, tn), lambda i, j: (i, j)), out_shape=jax.ShapeDtypeStruct((M, N), BF16),
        scratch_shapes=[pltpu.VMEM((tm, S), BF16)], compiler_params=_cparams("parallel", "arbitrary"),
    )(x, dy)


def _proj_attn(x, g, wt, *, name, tm=1024, tn=512):
    S, D = x.shape
    N = wt.shape[0]

    def body(x_ref, g_ref, w_ref, u_ref, o_ref):
        @pl.when(pl.program_id(1) == 0)
        def _():
            xv = x_ref[...]
            u_ref[...] = (xv * _rstd(xv) * g_ref[...]).astype(BF16)

        o_ref[...] = _nt(u_ref[...], w_ref[...]).astype(BF16)

    return pl.pallas_call(
        body, name=name, grid=(S // tm, N // tn),
        in_specs=[pl.BlockSpec((tm, D), lambda i, j: (i, 0)), pl.BlockSpec((1, D), lambda i, j: (0, 0)),
                  pl.BlockSpec((tn, D), lambda i, j: (j, 0))],
        out_specs=[pl.BlockSpec((tm, D), lambda i, j: (i, 0)), pl.BlockSpec((tm, tn), lambda i, j: (i, j))],
        out_shape=[jax.ShapeDtypeStruct((S, D), BF16), jax.ShapeDtypeStruct((S, N), BF16)],
        compiler_params=_cparams("parallel", "arbitrary"),
    )(x, g, wt)


PAIR_W = 3 * LANES
ATTN_UNROLL_FWD, ATTN_UNROLL_BWD = 4, 4


def _attn_masks(first):
    qi = lax.broadcasted_iota(jnp.int32, (ATTN_BLOCK, 2 * ATTN_BLOCK), 0)
    kj = lax.broadcasted_iota(jnp.int32, (ATTN_BLOCK, 2 * ATTN_BLOCK), 1)
    dist = qi + ATTN_BLOCK - kj
    valid = (dist >= 0) & (dist <= ATTN_BLOCK) & jnp.logical_or(kj >= ATTN_BLOCK, jnp.logical_not(first))
    lane = lax.broadcasted_iota(jnp.int32, (1, LANES), 1)
    return valid, lane


def _for_residue_blocks(S, d, fn):
    span = ATTN_BLOCK * d
    nb = S // span

    def step(n, carry):
        base = pl.multiple_of(n * span, span)
        for r in range(d):
            off = pl.multiple_of((r * nb + n) * ATTN_BLOCK, ATTN_BLOCK)
            fn(lambda ref, r=r: _block_rows(ref, base, r, d),
               lambda ref, val, r=r: _set_block_rows(ref, base, r, d, val), off)
        return carry

    lax.fori_loop(0, nb, step, 0)


def _for_blocks(S, unroll, fn):
    def step(i, carry):
        fn([(pl.multiple_of((i * unroll + u) * ATTN_BLOCK, ATTN_BLOCK), i * unroll + u) for u in range(unroll)])
        return carry

    lax.fori_loop(0, S // ATTN_BLOCK // unroll, step, 0)


def _head_value(x2, lane, e):
    return jnp.sum(jnp.where(lane == HEAD_DIM * e, x2, 0.0), axis=-1, keepdims=True)


def _block_rows(ref, base, r, d):
    if d == 1:
        return ref[pl.ds(base, ATTN_BLOCK), :]
    return ref.at[pl.ds(base, ATTN_BLOCK * d)][pl.ds(r, ATTN_BLOCK, stride=d), :]


def _set_block_rows(ref, base, r, d, val):
    if d == 1:
        ref[pl.ds(base, ATTN_BLOCK), :] = val
    else:
        ref.at[pl.ds(base, ATTN_BLOCK * d)][pl.ds(r, ATTN_BLOCK, stride=d), :] = val


def _order4_to_16(src, dst, pad):
    S = src.shape[0]
    q4, q16 = S // 4, S // 16
    for r in range(4):
        for a in range(4):
            for n in range(q16 // ATTN_BLOCK):
                rows = src.at[pl.ds(r * q4 + 4 * ATTN_BLOCK * n, 4 * ATTN_BLOCK)][pl.ds(a, ATTN_BLOCK, stride=4), :]
                dst[pl.ds(pad + (4 * a + r) * q16 + ATTN_BLOCK * n, ATTN_BLOCK), :] = rows.astype(dst.dtype)


def _order16_to_4(src, pad, dst):
    S = dst.shape[0]
    q4, q16 = S // 4, S // 16
    for r in range(4):
        for a in range(4):
            for n in range(q16 // ATTN_BLOCK):
                rows = src[pl.ds(pad + (4 * a + r) * q16 + ATTN_BLOCK * n, ATTN_BLOCK), :]
                dst.at[pl.ds(r * q4 + 4 * ATTN_BLOCK * n, 4 * ATTN_BLOCK)][pl.ds(a, ATTN_BLOCK, stride=4), :] = rows


def _regroup(S, d, pairs, tmp):
    for src, dst, pad in pairs:
        if d == 16:
            def to_tmp(rows, _, off, src=src):
                tmp[pl.ds(off, ATTN_BLOCK), :] = rows(src)

            _for_residue_blocks(S, 4, to_tmp)
            _order4_to_16(tmp, dst, pad)
    if d != 16:
        def to_dst(rows, _, off):
            for src, dst, pad in pairs:
                dst[pl.ds(pad + off, ATTN_BLOCK), :] = rows(src).astype(dst.dtype)

        _for_residue_blocks(S, d, to_dst)


def _split_pair(p_ref, qs, ks, vs, bk, bv):
    qs[...] = p_ref[:, 0:LANES].astype(F32)
    ks[...] = p_ref[:, LANES:2 * LANES].astype(F32)
    vs[...] = p_ref[:, 2 * LANES:3 * LANES].astype(F32)
    bk[0:ATTN_BLOCK, :] = jnp.zeros((ATTN_BLOCK, LANES), bk.dtype)
    bv[0:ATTN_BLOCK, :] = jnp.zeros((ATTN_BLOCK, LANES), bv.dtype)


def _attn_fwd(proj_a, *, name):
    S = proj_a.shape[0]

    def body(p_ref, o_ref, l_ref, qs, ks, vs, bq, bk, bv, bo, bl, to, tl):
        _split_pair(p_ref, qs, ks, vs, bk, bv)
        for d in DILATIONS:
            nb = S // (ATTN_BLOCK * d)
            _regroup(S, d, ((qs, bq, 0), (ks, bk, ATTN_BLOCK), (vs, bv, ATTN_BLOCK)), to)

            def blocks(group, nb=nb):
                lane = lax.broadcasted_iota(jnp.int32, (1, LANES), 1)
                heads = [(lane >= HEAD_DIM * e) & (lane < HEAD_DIM * (e + 1)) for e in range(LANES // HEAD_DIM)]
                wins = [pl.ds(off, 2 * ATTN_BLOCK) for off, _ in group]
                s = [[_nt(jnp.where(mh, bq[pl.ds(off, ATTN_BLOCK), :], jnp.zeros((ATTN_BLOCK, LANES), BF16)), bk[win, :])
                      for mh in heads] for (off, _), win in zip(group, wins)]
                p, m, l = [], [], []
                for (off, b), su in zip(group, s):
                    valid, _ = _attn_masks(jnp.bitwise_and(b, nb - 1) == 0)
                    sm = [jnp.where(valid, x * ATTN_SCALE, -jnp.inf) for x in su]
                    m.append([jnp.max(x, axis=-1, keepdims=True) for x in sm])
                    p.append([jnp.exp(x - mx) for x, mx in zip(sm, m[-1])])
                    l.append([jnp.sum(x, axis=-1, keepdims=True) for x in p[-1]])
                o = [[_nn(x.astype(BF16), bv[win, :]) for x in pu] for pu, win in zip(p, wins)]
                for (off, _), ou, mu, lu in zip(group, o, m, l):
                    o2 = jnp.zeros((ATTN_BLOCK, LANES), F32)
                    l2 = jnp.zeros((ATTN_BLOCK, LANES), F32)
                    for mh, oe, me_, le in zip(heads, ou, mu, lu):
                        o2 = jnp.where(mh, oe / le, o2)
                        l2 = jnp.where(mh, me_ + jnp.log(le), l2)
                    bo[pl.ds(off, ATTN_BLOCK), :] = o2
                    bl[pl.ds(off, ATTN_BLOCK), :] = l2

            _for_blocks(S, ATTN_UNROLL_FWD, blocks)

            if d == 16:
                _order16_to_4(bo, 0, to)
                _order16_to_4(bl, 0, tl)
            src_o, src_l = (to, tl) if d == 16 else (bo, bl)

            def merge(rows, set_rows, off, d=d, src_o=src_o, src_l=src_l):
                blk = pl.ds(off, ATTN_BLOCK)
                o2, l2 = src_o[blk, :], src_l[blk, :]
                if d != DILATIONS[0]:
                    lo, oo = rows(l_ref), rows(o_ref)
                    ln = jnp.maximum(lo, l2)
                    wa, wb = jnp.exp(lo - ln), jnp.exp(l2 - ln)
                    o2 = (wa * oo + wb * o2) / (wa + wb)
                    l2 = ln + jnp.log(wa + wb)
                set_rows(o_ref, o2)
                set_rows(l_ref, l2)

            _for_residue_blocks(S, min(d, 4), merge)

    slab = pl.BlockSpec((S, LANES), lambda p: (0, p))
    f32_slab, bf16_slab = pltpu.VMEM((S, LANES), F32), pltpu.VMEM((S, LANES), BF16)
    bf16_window = pltpu.VMEM((S + ATTN_BLOCK, LANES), BF16)
    return pl.pallas_call(
        body, name=name, grid=(ATTN_W // LANES,), in_specs=[pl.BlockSpec((S, PAIR_W), lambda p: (0, p))],
        out_specs=[slab, slab],
        out_shape=[jax.ShapeDtypeStruct((S, ATTN_W), F32), jax.ShapeDtypeStruct((S, ATTN_W), F32)],
        scratch_shapes=[f32_slab] * 3 + [bf16_slab, bf16_window, bf16_window] + [f32_slab] * 4,
        compiler_params=_cparams("parallel"),
    )(proj_a)


def _attn_bwd(proj_a, do, lse, delta, *, name):
    S = proj_a.shape[0]

    def body(p_ref, do_ref, lse_ref, dl_ref, o_ref, qs, ks, vs, dqs, dks, dvs, bq, bk, bv, bdo, blse, bdl, bdq, bdk, bdv,
             tmp):
        _split_pair(p_ref, qs, ks, vs, bk, bv)
        bdk[0:ATTN_BLOCK, :] = jnp.zeros((ATTN_BLOCK, LANES), F32)
        bdv[0:ATTN_BLOCK, :] = jnp.zeros((ATTN_BLOCK, LANES), F32)
        for d in DILATIONS:
            nb = S // (ATTN_BLOCK * d)
            _regroup(S, d, ((qs, bq, 0), (ks, bk, ATTN_BLOCK), (vs, bv, ATTN_BLOCK), (do_ref, bdo, 0),
                            (lse_ref, blse, 0), (dl_ref, bdl, 0)), tmp)

            def blocks(group, nb=nb):
                lane = lax.broadcasted_iota(jnp.int32, (1, LANES), 1)
                heads = [(lane >= HEAD_DIM * e) & (lane < HEAD_DIM * (e + 1)) for e in range(LANES // HEAD_DIM)]
                zero = jnp.zeros((ATTN_BLOCK, LANES), BF16)
                chains = [(off, b, e, mh) for off, b in group for e, mh in enumerate(heads)]
                qm = [jnp.where(mh, bq[pl.ds(off, ATTN_BLOCK), :], zero) for off, _, _, mh in chains]
                dom = [jnp.where(mh, bdo[pl.ds(off, ATTN_BLOCK), :], zero) for off, _, _, mh in chains]
                s = [_nt(x, bk[pl.ds(off, 2 * ATTN_BLOCK), :]) for x, (off, _, _, _) in zip(qm, chains)]
                dp = [_nt(x, bv[pl.ds(off, 2 * ATTN_BLOCK), :]) for x, (off, _, _, _) in zip(dom, chains)]
                p, ds = [], []
                for (off, b, e, _), sc, dpc in zip(chains, s, dp):
                    valid, _ = _attn_masks(jnp.bitwise_and(b, nb - 1) == 0)
                    blk = pl.ds(off, ATTN_BLOCK)
                    pc = jnp.where(valid, jnp.exp(sc * ATTN_SCALE - _head_value(blse[blk, :], lane, e)), 0.0)
                    ds.append((pc * (dpc - _head_value(bdl[blk, :], lane, e)) * ATTN_SCALE).astype(BF16))
                    p.append(pc.astype(BF16))
                dq = [_nn(x, bk[pl.ds(off, 2 * ATTN_BLOCK), :]) for x, (off, _, _, _) in zip(ds, chains)]
                dk = [_tn(x, y) for x, y in zip(ds, qm)]
                dv = [_tn(x, y) for x, y in zip(p, dom)]
                nh = len(heads)
                for u, (off, _) in enumerate(group):
                    dq2 = jnp.zeros((ATTN_BLOCK, LANES), F32)
                    for mh, x in zip(heads, dq[nh * u:nh * (u + 1)]):
                        dq2 = jnp.where(mh, x, dq2)
                    bdq[pl.ds(off, ATTN_BLOCK), :] = dq2
                    for acc, grads in ((bdk, dk), (bdv, dv)):
                        win_grad = sum(grads[nh * u + 1:nh * (u + 1)], grads[nh * u])
                        acc[pl.ds(off, ATTN_BLOCK), :] += win_grad[:ATTN_BLOCK]
                        acc[pl.ds(off + ATTN_BLOCK, ATTN_BLOCK), :] = win_grad[ATTN_BLOCK:]

            _for_blocks(S, ATTN_UNROLL_BWD, blocks)

            outs = ((dqs, bdq, 0), (dks, bdk, ATTN_BLOCK), (dvs, bdv, ATTN_BLOCK))
            if d == 16:
                for acc, grad, pad in outs:
                    _order16_to_4(grad, pad, tmp)

                    def add(rows, set_rows, off, acc=acc):
                        set_rows(acc, rows(acc) + tmp[pl.ds(off, ATTN_BLOCK), :])

                    _for_residue_blocks(S, 4, add)
            else:
                def scatter(rows, set_rows, off, d=d):
                    for acc, grad, pad in outs:
                        part = grad[pl.ds(pad + off, ATTN_BLOCK), :]
                        set_rows(acc, part if d == DILATIONS[0] else rows(acc) + part)

                _for_residue_blocks(S, d, scatter)
        o_ref[:, 0:LANES] = dqs[...].astype(BF16)
        o_ref[:, LANES:2 * LANES] = dks[...].astype(BF16)
        o_ref[:, 2 * LANES:3 * LANES] = dvs[...].astype(BF16)

    slab = pl.BlockSpec((S, LANES), lambda p: (0, p), pipeline_mode=pl.Buffered(1))
    pair = pl.BlockSpec((S, PAIR_W), lambda p: (0, p))
    f32_slab, bf16_slab = pltpu.VMEM((S, LANES), F32), pltpu.VMEM((S, LANES), BF16)
    f32_window, bf16_window = pltpu.VMEM((S + ATTN_BLOCK, LANES), F32), pltpu.VMEM((S + ATTN_BLOCK, LANES), BF16)
    return pl.pallas_call(
        body, name=name, grid=(ATTN_W // LANES,), in_specs=[pair, slab, slab, slab], out_specs=pair,
        out_shape=jax.ShapeDtypeStruct(proj_a.shape, BF16),
        scratch_shapes=[f32_slab] * 6 + [bf16_slab, bf16_window, bf16_window, bf16_slab, f32_slab, f32_slab,
                                         f32_slab, f32_window, f32_window, f32_slab],
        compiler_params=_cparams("parallel"),
    )(proj_a, do, lse, delta)


HG_T = 2 * HGRN_CHUNK


def _hgrn_consts():
    row = lax.broadcasted_iota(jnp.int32, (HG_T, HG_T), 0)
    col = lax.broadcasted_iota(jnp.int32, (HG_T, HG_T), 1)
    same = (row >= HGRN_CHUNK) == (col >= HGRN_CHUNK)
    return row, same & (col <= row), same & (col >= row)


def _lower_bound(logits_ref):
    l0, l1 = logits_ref[0:1, :], logits_ref[1:2, :]
    mx = jnp.maximum(l0, l1)
    e0, e1 = jnp.exp(l0 - mx), jnp.exp(l1 - mx)
    return e0 / (e0 + e1)


def _hgrn_gates(qs, fs, lbs, row, causal):
    C = HGRN_CHUNK
    tri = jnp.where(causal, 1.0, 0.0).astype(F32)
    sgs = [_sigmoid(f) for f in fs]
    forgets = [lb + (1.0 - lb) * sg for lb, sg in zip(lbs, sgs)]
    logfs = [jnp.log(forget) for forget in forgets]
    bs = [_nn(tri, logf, HIGHEST) for logf in logfs]
    out = []
    for q, sg, forget, logf, b in zip(qs, sgs, forgets, logfs, bs):
        key = 1.0 - forget
        bend0 = jnp.sum(logf[:C], axis=0, keepdims=True)
        bend1 = jnp.sum(logf[C:], axis=0, keepdims=True)
        bend = jnp.where(row < C, bend0, bend1)
        eb, emb, eend = jnp.exp(b), jnp.exp(-b), jnp.exp(bend - b)
        sq = _sigmoid(q)
        out.append(dict(sg=sg, forget=forget, key=key, bend0=bend0, bend1=bend1, eb=eb, emb=emb, eend=eend, sq=sq,
                        qd=q * sq * eb, ki=key * emb, ke=key * eend))
    return out


def _hgrn_fwd(proj, logits, *, name):
    S = proj.shape[0]
    W, C = HGRN_W, HGRN_CHUNK

    def body(q_ref, f_ref, i_ref, lg_ref, rec_ref, st_ref, s_ref):
        @pl.when(pl.program_id(0) == 0)
        def _():
            s_ref[...] = jnp.zeros_like(s_ref)

        row, causal, _ = _hgrn_consts()
        lb_all = _lower_bound(lg_ref)
        heads = range(HGRN_HEADS)
        sls = [slice(HGRN_DIM * h, HGRN_DIM * (h + 1)) for h in heads]
        gts = _hgrn_gates([q_ref[:, sl] for sl in sls], [f_ref[:, sl] for sl in sls], [lb_all[:, sl] for sl in sls],
                          row, causal)
        qd, ki, ke = ([gt[k].astype(BF16) for gt in gts] for k in ("qd", "ki", "ke"))
        iv = [i_ref[:, sl].astype(BF16) for sl in sls]
        s0 = [s_ref[h] for h in heads]
        a = [_nt(qd[h], ki[h]) for h in heads]
        o0 = [_nt(qd[h][:C], s0[h].astype(BF16)) for h in heads]
        u0 = [_tn(iv[h][:C], ke[h][:C]) for h in heads]
        u1 = [_tn(iv[h][C:], ke[h][C:]) for h in heads]
        s1 = [jnp.exp(gts[h]["bend0"]) * s0[h] + u0[h] for h in heads]
        o = [_nn(jnp.where(causal, a[h], 0.0).astype(BF16), iv[h]) for h in heads]
        o1 = [_nt(qd[h][C:], s1[h].astype(BF16)) for h in heads]
        for h in heads:
            st_ref[0, h] = s0[h]
            st_ref[1, h] = s1[h]
            s_ref[h] = jnp.exp(gts[h]["bend1"]) * s1[h] + u1[h]
            rec_ref[:, sls[h]] = o[h] + jnp.concatenate([o0[h], o1[h]], axis=0)

    blk = lambda j: pl.BlockSpec((HG_T, W), lambda t: (t, j))
    return pl.pallas_call(
        body, name=name, grid=(S // HG_T,),
        in_specs=[blk(0), blk(1), blk(2), pl.BlockSpec((2, W), lambda t: (0, 0))],
        out_specs=[blk(0), pl.BlockSpec((2, HGRN_HEADS, HGRN_DIM, HGRN_DIM), lambda t: (t, 0, 0, 0))],
        out_shape=[jax.ShapeDtypeStruct((S, W), F32),
                   jax.ShapeDtypeStruct((S // C, HGRN_HEADS, HGRN_DIM, HGRN_DIM), F32)],
        scratch_shapes=[pltpu.VMEM((HGRN_HEADS, HGRN_DIM, HGRN_DIM), F32)],
        compiler_params=_cparams("arbitrary"),
    )(proj, proj, proj, logits)


def _hgrn_bwd(proj, logits, states, drec, *, name):
    S = proj.shape[0]
    W, C = HGRN_W, HGRN_CHUNK
    nt = S // HG_T

    def body(q_ref, f_ref, i_ref, lg_ref, st_ref, do_ref, dp_ref, dlg_ref, ds_ref, dlb_ref):
        t = pl.program_id(0)

        @pl.when(t == 0)
        def _():
            ds_ref[...] = jnp.zeros_like(ds_ref)
            dlb_ref[...] = jnp.zeros_like(dlb_ref)

        row, causal, anti = _hgrn_consts()
        lb_all = _lower_bound(lg_ref)
        heads = range(HGRN_HEADS)
        sls = [slice(HGRN_DIM * h, HGRN_DIM * (h + 1)) for h in heads]
        qs, lbs = [q_ref[:, sl] for sl in sls], [lb_all[:, sl] for sl in sls]
        gts = _hgrn_gates(qs, [f_ref[:, sl] for sl in sls], lbs, row, causal)
        qd, ki, ke = ([gt[k] for gt in gts] for k in ("qd", "ki", "ke"))
        qdb, kib, keb = ([x.astype(BF16) for x in xs] for xs in (qd, ki, ke))
        iv = [i_ref[:, sl].astype(BF16) for sl in sls]
        dob = [do_ref[:, sl].astype(BF16) for sl in sls]
        s0, s1, ds1 = [st_ref[0, h] for h in heads], [st_ref[1, h] for h in heads], [ds_ref[h] for h in heads]
        ds1b = [x.astype(BF16) for x in ds1]
        dec0, dec1 = [jnp.exp(gt["bend0"]) for gt in gts], [jnp.exp(gt["bend1"]) for gt in gts]
        a = [_nt(qdb[h], kib[h]) for h in heads]
        da = [_nt(dob[h], iv[h]) for h in heads]
        dqd1 = [_nn(dob[h][C:], s1[h].astype(BF16)) for h in heads]
        dqd0 = [_nn(dob[h][:C], s0[h].astype(BF16)) for h in heads]
        di1 = [_nt(keb[h][C:], ds1b[h]) for h in heads]
        dke1 = [_nn(iv[h][C:], ds1b[h]) for h in heads]
        t1 = [_tn(dob[h][C:], qdb[h][C:]) for h in heads]
        t0 = [_tn(dob[h][:C], qdb[h][:C]) for h in heads]
        ds0 = [dec1[h] * ds1[h] + t1[h] for h in heads]
        ds0b = [x.astype(BF16) for x in ds0]
        a = [jnp.where(causal, x, 0.0).astype(BF16) for x in a]
        da = [jnp.where(causal, x, 0.0).astype(BF16) for x in da]
        di0 = [_nt(keb[h][:C], ds0b[h]) for h in heads]
        dke0 = [_nn(iv[h][:C], ds0b[h]) for h in heads]
        dqd_a = [_nn(da[h], kib[h]) for h in heads]
        dki = [_tn(da[h], qdb[h]) for h in heads]
        di_a = [_tn(a[h], dob[h]) for h in heads]
        dqd, dke, db = [], [], []
        for h in heads:
            ds_ref[h] = dec0[h] * ds0[h] + t0[h]
            ddec1 = jnp.sum(ds1[h] * s1[h], axis=0, keepdims=True)
            ddec0 = jnp.sum(ds0[h] * s0[h], axis=0, keepdims=True)
            dqd.append(dqd_a[h] + jnp.concatenate([dqd0[h], dqd1[h]], axis=0))
            dp_ref[:, 2 * W + HGRN_DIM * h:2 * W + HGRN_DIM * (h + 1)] = (
                di_a[h] + jnp.concatenate([di0[h], di1[h]], axis=0)).astype(BF16)
            dke.append(jnp.concatenate([dke0[h], dke1[h]], axis=0))
            gke = dke[h] * ke[h]
            dbend0 = jnp.sum(gke[:C], axis=0, keepdims=True) + ddec0 * dec0[h]
            dbend1 = jnp.sum(gke[C:], axis=0, keepdims=True) + ddec1 * dec1[h]
            dbh = dqd[h] * qd[h] - dki[h] * ki[h] - gke
            db.append(dbh + jnp.where(row == C - 1, dbend0, 0.0) + jnp.where(row == HG_T - 1, dbend1, 0.0))
        tri = jnp.where(anti, 1.0, 0.0).astype(F32)
        dlogf = [_nn(tri, db[h], HIGHEST) for h in heads]
        for h in heads:
            gt, lb, q = gts[h], lbs[h], qs[h]
            dforget = dlogf[h] / gt["forget"] - (dki[h] * gt["emb"] + dke[h] * gt["eend"])
            sg, sq = gt["sg"], gt["sq"]
            dp_ref[:, W + HGRN_DIM * h:W + HGRN_DIM * (h + 1)] = (dforget * (1.0 - lb) * sg * (1.0 - sg)).astype(BF16)
            dlb_ref[:, sls[h]] += jnp.sum(dforget * (1.0 - sg), axis=0, keepdims=True)
            dp_ref[:, sls[h]] = (dqd[h] * gt["eb"] * sq * (1.0 + q * (1.0 - sq))).astype(BF16)

        @pl.when(t == nt - 1)
        def _():
            dl0 = dlb_ref[...] * lb_all * (1.0 - lb_all)
            dlg_ref[0:1, :] = dl0
            dlg_ref[1:2, :] = -dl0

    blk = lambda j: pl.BlockSpec((HG_T, W), lambda t: (nt - 1 - t, j))
    full = pl.BlockSpec((2, W), lambda t: (0, 0))
    return pl.pallas_call(
        body, name=name, grid=(nt,),
        in_specs=[blk(0), blk(1), blk(2), full,
                  pl.BlockSpec((2, HGRN_HEADS, HGRN_DIM, HGRN_DIM), lambda t: (nt - 1 - t, 0, 0, 0)), blk(0)],
        out_specs=[pl.BlockSpec((HG_T, 3 * W), lambda t: (nt - 1 - t, 0)), full],
        out_shape=[jax.ShapeDtypeStruct((S, 3 * W), BF16), jax.ShapeDtypeStruct((2, W), F32)],
        scratch_shapes=[pltpu.VMEM((HGRN_HEADS, HGRN_DIM, HGRN_DIM), F32), pltpu.VMEM((1, W), F32)],
        compiler_params=_cparams("arbitrary"),
    )(proj, proj, proj, logits, states, drec)


def _out_proj(attn, rec, proj_h, x, g_attn, g_hgrn, g_norm2, w_out, *, name, tm=512):
    S, D = x.shape
    AW, W = ATTN_W, HGRN_W

    def body(a_ref, r_ref, hg_ref, x_ref, ga_ref, gh_ref, g2_ref, w_ref, h_ref, u_ref, m_ref):
        av = a_ref[...]
        m_ref[:, :AW] = (av * _rstd(av) * ga_ref[...]).astype(BF16)
        for h in range(HGRN_HEADS):
            sl = slice(HGRN_DIM * h, HGRN_DIM * (h + 1))
            rv, hg = r_ref[:, sl], hg_ref[:, sl]
            m_ref[:, AW + HGRN_DIM * h:AW + HGRN_DIM * (h + 1)] = (
                (rv * _rstd(rv) * gh_ref[:, sl]) * (hg * _sigmoid(hg))).astype(BF16)
        h1 = x_ref[...] + _nn(m_ref[...], w_ref[...])
        h_ref[...] = h1
        u_ref[...] = (h1 * _rstd(h1) * g2_ref[...]).astype(BF16)

    row = lambda w, j=0: pl.BlockSpec((tm, w), lambda i: (i, j))
    vec = lambda w: pl.BlockSpec((1, w), lambda i: (0, 0))
    return pl.pallas_call(
        body, name=name, grid=(S // tm,),
        in_specs=[row(AW), row(W), row(W, 3), row(D), vec(AW), vec(W), vec(D), _resident(w_out.shape)],
        out_specs=[row(D), row(D), row(AW + W)],
        out_shape=[jax.ShapeDtypeStruct((S, D), F32), jax.ShapeDtypeStruct((S, D), BF16),
                   jax.ShapeDtypeStruct((S, AW + W), BF16)],
        compiler_params=_cparams("parallel"),
    )(attn, rec, proj_h, x, g_attn, g_hgrn, g_norm2, w_out)


def _dmix_post_bwd(dh1b, w_out, attn, rec, proj_h, g_attn, g_hgrn, *, name, tm=512):
    S, D = dh1b.shape
    AW, W = ATTN_W, HGRN_W

    def body(dh_ref, w_ref, a_ref, r_ref, hg_ref, ga_ref, gh_ref, do_ref, dl_ref, dr_ref, dhg_ref, dga_ref, dgh_ref):
        first = pl.program_id(0) == 0
        dmix = _nt(dh_ref[...], w_ref[...])
        av = a_ref[...]
        dov, dga = _norm_bwd(av, ga_ref[...], dmix[:, :AW])
        do_ref[...] = dov
        shift = HEAD_DIM.bit_length() - 1
        hi = lax.shift_right_logical(lax.broadcasted_iota(jnp.int32, (AW, AW), 0), shift)
        hj = lax.shift_right_logical(lax.broadcasted_iota(jnp.int32, (AW, AW), 1), shift)
        prod = dov * av
        hi_part = prod.astype(BF16)
        lo_part = (prod - hi_part.astype(F32)).astype(BF16)
        same_head = jnp.where(hi == hj, 1.0, 0.0).astype(BF16)
        dl_ref[...] = _nn(hi_part, same_head) + _nn(lo_part, same_head)
        _accumulate(dga_ref, jnp.sum(dga, axis=0, keepdims=True), first)

        @pl.when(first)
        def _():
            dgh_ref[...] = jnp.zeros_like(dgh_ref)

        for h in range(HGRN_HEADS):
            sl = slice(HGRN_DIM * h, HGRN_DIM * (h + 1))
            rv, hg, gv = r_ref[:, sl], hg_ref[:, sl], gh_ref[:, sl]
            dout = dmix[:, AW + HGRN_DIM * h:AW + HGRN_DIM * (h + 1)]
            sg = _sigmoid(hg)
            drv, dgh = _norm_bwd(rv, gv, dout * (hg * sg))
            dr_ref[:, sl] = drv
            dgh_ref[:, sl] += jnp.sum(dgh, axis=0, keepdims=True)
            dhg_ref[:, sl] = (dout * (rv * _rstd(rv) * gv) * (sg * (1.0 + hg * (1.0 - sg)))).astype(BF16)

    row = lambda w, j=0: pl.BlockSpec((tm, w), lambda i: (i, j))
    vec = lambda w: pl.BlockSpec((1, w), lambda i: (0, 0))
    return pl.pallas_call(
        body, name=name, grid=(S // tm,),
        in_specs=[row(D), _resident(w_out.shape), row(AW), row(W), row(W, 3), vec(AW), vec(W)],
        out_specs=[row(AW), row(AW), row(W), row(W), vec(AW), vec(W)],
        out_shape=[jax.ShapeDtypeStruct((S, AW), F32), jax.ShapeDtypeStruct((S, AW), F32),
                   jax.ShapeDtypeStruct((S, W), F32), jax.ShapeDtypeStruct((S, W), BF16),
                   jax.ShapeDtypeStruct((1, AW), F32), jax.ShapeDtypeStruct((1, W), F32)],
        compiler_params=_cparams("arbitrary"),
    )(dh1b, w_out, attn, rec, proj_h, g_attn, g_hgrn)


def _conv_act(g, g1, g2, w_ref, b_ref):
    c = b_ref[...] + w_ref[0:1, :] * g2 + w_ref[1:2, :] * g1 + w_ref[2:3, :] * g
    return c, 0.5 * (1.0 + lax.erf(c * (2.0 ** -0.5)))


def _shift_down(g, halo, row):
    g1 = jnp.where(row == 0, halo[7:8], pltpu.roll(g, 1, 0))
    g2 = jnp.where(row == 0, halo[6:7], jnp.where(row == 1, halo[7:8], pltpu.roll(g, 2, 0)))
    return g1, g2


def _shift_up(x, halo, row):
    n = x.shape[0]
    x1 = jnp.where(row == n - 1, halo[0:1], pltpu.roll(x, n - 1, 0))
    x2 = jnp.where(row == n - 2, halo[0:1], jnp.where(row == n - 1, halo[1:2], pltpu.roll(x, n - 2, 0)))
    return x1, x2


def _up_glu(u, wt_up, conv_w, conv_b, *, name, tm=1024, tn=256):
    S, D = u.shape
    F = wt_up.shape[0] // 2
    nf = F // tn

    def body(u_ref, wg_ref, wv_ref, cw_ref, cb_ref, g_ref, v_ref, a_ref, halo_ref):
        i, j = pl.program_id(0), pl.program_id(1)

        @pl.when(i == 0)
        def _():
            halo_ref[j] = jnp.zeros((SUBLANES, tn), F32)

        uv = u_ref[...]
        g, v = _nt(uv, wg_ref[...]), _nt(uv, wv_ref[...])
        row = lax.broadcasted_iota(jnp.int32, (tm, tn), 0)
        g1, g2 = _shift_down(g, halo_ref[j], row)
        c, cdf = _conv_act(g, g1, g2, cw_ref, cb_ref)
        a_ref[...] = (c * cdf * v).astype(BF16)
        g_ref[...] = g.astype(BF16)
        v_ref[...] = v.astype(BF16)
        halo_ref[j] = g[tm - SUBLANES:, :]

    col = pl.BlockSpec((tm, tn), lambda i, j: (i, j))
    out = jax.ShapeDtypeStruct((S, F), BF16)
    return pl.pallas_call(
        body, name=name, grid=(S // tm, nf),
        in_specs=[pl.BlockSpec((tm, D), lambda i, j: (i, 0)), pl.BlockSpec((tn, D), lambda i, j: (j, 0)),
                  pl.BlockSpec((tn, D), lambda i, j: (j + nf, 0)), pl.BlockSpec((3, tn), lambda i, j: (0, j)),
                  pl.BlockSpec((1, tn), lambda i, j: (0, j))],
        out_specs=[col, col, col], out_shape=[out, out, out],
        scratch_shapes=[pltpu.VMEM((nf, SUBLANES, tn), F32)], compiler_params=_cparams("arbitrary", "arbitrary"),
    )(u, wt_up, wt_up, conv_w, conv_b)


def _dact_glu_bwd(dh2b, w_down, gate, val, conv_w, conv_b, *, name, tm=1024, tn=256):
    S, D = dh2b.shape
    F = gate.shape[1]
    nf, ni = F // tn, S // tm
    hb = tm // SUBLANES

    def body(dh_ref, wd_ref, g_ref, gh_ref, v_ref, cw_ref, cb_ref, dg_ref, dv_ref, dcw_ref, dcb_ref, halo_ref, acc_ref):
        i, j = pl.program_id(0), pl.program_id(1)

        @pl.when(i == 0)
        def _():
            halo_ref[j] = jnp.zeros((SUBLANES, tn), F32)
            acc_ref[j] = jnp.zeros((SUBLANES, tn), F32)

        g = g_ref[...].astype(F32)
        before = jnp.where(i < ni - 1, gh_ref[...].astype(F32), 0.0)
        row = lax.broadcasted_iota(jnp.int32, (tm, tn), 0)
        g1, g2 = _shift_down(g, before[SUBLANES:], row)
        c, cdf = _conv_act(g, g1, g2, cw_ref, cb_ref)
        da = _nt(dh_ref[...], wd_ref[...])
        dv_ref[...] = (da * (c * cdf)).astype(BF16)
        pdf = jnp.exp(-0.5 * c * c) * (1.0 / (2.0 * jnp.pi) ** 0.5)
        dc = da * v_ref[...].astype(F32) * (cdf + c * pdf)
        d1, d2 = _shift_up(dc, halo_ref[j], row)
        dg_ref[...] = (cw_ref[2:3, :] * dc + cw_ref[1:2, :] * d1 + cw_ref[0:1, :] * d2).astype(BF16)
        halo_ref[j] = dc[:SUBLANES, :]
        for k, t in enumerate((dc * g2, dc * g1, dc * g, dc)):
            acc_ref[j, k:k + 1, :] += jnp.sum(t, axis=0, keepdims=True)

        @pl.when((i == ni - 1) & (j == nf - 1))
        def _():
            for jj in range(nf):
                dcw_ref[:, jj * tn:(jj + 1) * tn] = acc_ref[jj, 0:3, :]
                dcb_ref[:, jj * tn:(jj + 1) * tn] = acc_ref[jj, 3:4, :]

    tile = pl.BlockSpec((tm, tn), lambda i, j: (ni - 1 - i, j))
    return pl.pallas_call(
        body, name=name, grid=(ni, nf),
        in_specs=[pl.BlockSpec((tm, D), lambda i, j: (ni - 1 - i, 0)), pl.BlockSpec((tn, D), lambda i, j: (j, 0)),
                  tile, pl.BlockSpec((SUBLANES * 2, tn), lambda i, j: (jnp.maximum((ni - 1 - i) * (hb // 2) - 1, 0), j)),
                  tile, pl.BlockSpec((3, tn), lambda i, j: (0, j)), pl.BlockSpec((1, tn), lambda i, j: (0, j))],
        out_specs=[tile, tile, pl.BlockSpec((3, F), lambda i, j: (0, 0)), pl.BlockSpec((1, F), lambda i, j: (0, 0))],
        out_shape=[jax.ShapeDtypeStruct((S, F), BF16), jax.ShapeDtypeStruct((S, F), BF16),
                   jax.ShapeDtypeStruct((3, F), F32), jax.ShapeDtypeStruct((1, F), F32)],
        scratch_shapes=[pltpu.VMEM((nf, SUBLANES, tn), F32), pltpu.VMEM((nf, SUBLANES, tn), F32)],
        compiler_params=_cparams("arbitrary", "arbitrary"),
    )(dh2b, w_down, gate, gate, val, conv_w, conv_b)


def _down_loss(act, w_down, h1, g, target, *, name, tm=512):
    S, F = act.shape
    D = h1.shape[1]

    def body(a_ref, w_ref, h_ref, g_ref, t_ref, dh_ref, dhb_ref, dg_ref, loss_ref):
        first = pl.program_id(0) == 0
        h2 = h_ref[...] + _nn(a_ref[...], w_ref[...])
        gv = g_ref[...]
        r = _rstd(h2)
        xh = h2 * r
        err = xh * gv - t_ref[...]
        part_loss = 0.5 * jnp.sum(jnp.mean(err * err, axis=-1, keepdims=True), axis=0, keepdims=True)
        dy = err * (1.0 / D)
        dxh = dy * gv
        dh = r * (dxh - xh * jnp.mean(dxh * xh, axis=-1, keepdims=True))
        dh_ref[...] = dh
        dhb_ref[...] = dh.astype(BF16)
        _accumulate(dg_ref, jnp.sum(dy * xh, axis=0, keepdims=True), first)
        _accumulate(loss_ref, jnp.broadcast_to(part_loss, (1, LANES)), first)

    row = lambda w: pl.BlockSpec((tm, w), lambda i: (i, 0))
    vec = lambda w: pl.BlockSpec((1, w), lambda i: (0, 0))
    return pl.pallas_call(
        body, name=name, grid=(S // tm,), in_specs=[row(F), _resident(w_down.shape), row(D), vec(D), row(D)],
        out_specs=[row(D), row(D), vec(D), vec(LANES)],
        out_shape=[jax.ShapeDtypeStruct((S, D), F32), jax.ShapeDtypeStruct((S, D), BF16),
                   jax.ShapeDtypeStruct((1, D), F32), jax.ShapeDtypeStruct((1, LANES), F32)],
        compiler_params=_cparams("arbitrary"),
    )(act, w_down, h1, g, target)


def _grad_norm_input(pieces, ws, x, g, add, *, name, tm=512):
    S, D = x.shape
    widths = [p.shape[1] for p in pieces]
    n, nw = len(pieces), len(ws)
    where, wi, off = [], 0, ws[0][1]
    for wd in widths:
        if off == ws[wi][0].shape[0]:
            wi, off = wi + 1, ws[wi + 1][1]
        where.append((wi, off))
        off += wd
    ws = [w for w, _ in ws]

    def body(*refs):
        p_refs, w_refs = refs[:n], refs[n:n + nw]
        x_ref, g_ref, add_ref, dx_ref, dxb_ref, dg_ref = refs[n + nw:]
        halves = _row_halves(tm)
        du = []
        for rows in halves:
            terms = [_nn(p_refs[k][rows, :], w_refs[wi][off:off + widths[k], :]) for k, (wi, off) in enumerate(where)]
            du.append(sum(terms[1:], terms[0]))
        dg_sum = None
        for rows, duh in zip(halves, du):
            dx, dg = _norm_bwd(x_ref[rows, :], g_ref[...], duh)
            dx = add_ref[rows, :] + dx
            dx_ref[rows, :] = dx
            dxb_ref[rows, :] = dx.astype(BF16)
            part = jnp.sum(dg, axis=0, keepdims=True)
            dg_sum = part if dg_sum is None else dg_sum + part
        _accumulate(dg_ref, dg_sum, pl.program_id(0) == 0)

    row = lambda w_: pl.BlockSpec((tm, w_), lambda i: (i, 0))
    vec = pl.BlockSpec((1, D), lambda i: (0, 0))
    return pl.pallas_call(
        body, name=name, grid=(S // tm,),
        in_specs=[row(wd) for wd in widths] + [_resident(w.shape) for w in ws] + [row(D), vec, row(D)],
        out_specs=[row(D), row(D), vec],
        out_shape=[jax.ShapeDtypeStruct((S, D), F32), jax.ShapeDtypeStruct((S, D), BF16),
                   jax.ShapeDtypeStruct((1, D), F32)],
        compiler_params=_cparams("arbitrary"),
    )(*pieces, *ws, x, g, add)


def _rows(a):
    return a.reshape(-1, a.shape[-1])


def _row_tile(rows, cols, itemsize=4, budget=1 << 20):
    t = rows
    while t % 32 == 0 and t * cols * itemsize > budget:
        t //= 2
    return t


def _sum_cast(arrs, out_dtype, *, name):
    shape = arrs[0].shape
    flat = [_rows(a) for a in arrs]
    R, C = flat[0].shape
    tr = _row_tile(R, C)

    def body(*refs):
        acc = refs[0][...].astype(F32)
        for r in refs[1:-1]:
            acc = acc + r[...].astype(F32)
        refs[-1][...] = acc.astype(out_dtype)

    spec = pl.BlockSpec((tr, C), lambda i: (i, 0))
    return pl.pallas_call(
        body, name=name, grid=(R // tr,), in_specs=[spec] * len(flat), out_specs=spec,
        out_shape=jax.ShapeDtypeStruct((R, C), out_dtype), compiler_params=_cparams("parallel"),
    )(*flat).reshape(shape)


def _adamw(parts, w, m, v, *, name):
    shape = w.shape
    w2, m2, v2 = _rows(w), _rows(m), _rows(v)
    R, C = w2.shape
    parts = [p.reshape(-1, R, C) for p in parts]
    tr = _row_tile(R, C)
    np_ = len(parts)
    c1, c2 = 1.0 - ADAM_B1 ** ADAM_STEP, 1.0 - ADAM_B2 ** ADAM_STEP

    def body(*refs):
        terms = [(r, k) for r in refs[:np_] for k in range(r.shape[0])]
        g = terms[0][0][terms[0][1]].astype(F32)
        for r, k in terms[1:]:
            g = g + r[k].astype(F32)
        w_ref, m_ref, v_ref, g_out, d_out, m_out, v_out = refs[np_:]
        mn = ADAM_B1 * m_ref[...] + (1.0 - ADAM_B1) * g
        vn = ADAM_B2 * v_ref[...] + (1.0 - ADAM_B2) * (g * g)
        g_out[...] = g
        d_out[...] = -ADAM_LR * ((mn / c1) / (jnp.sqrt(vn / c2) + ADAM_EPS) + ADAM_WD * w_ref[...])
        m_out[...] = mn
        v_out[...] = vn

    spec = pl.BlockSpec((tr, C), lambda i: (i, 0))
    out = jax.ShapeDtypeStruct((R, C), F32)
    stacks = [pl.BlockSpec((p.shape[0], tr, C), lambda i: (0, i, 0)) for p in parts]
    res = pl.pallas_call(
        body, name=name, grid=(R // tr,), in_specs=stacks + [spec] * 3, out_specs=[spec] * 4,
        out_shape=[out] * 4, compiler_params=_cparams("parallel"),
    )(*parts, w2, m2, v2)
    return [r.reshape(shape) for r in res]


def _adamw_packed(stack, widths, params, *, name):
    c1, c2 = 1.0 - ADAM_B1 ** ADAM_STEP, 1.0 - ADAM_B2 ** ADAM_STEP
    k = stack.shape[0]
    flat = [None if p is None else [_rows(a) for a in p] for p in params]
    n_in = sum(3 for p in flat if p is not None)

    def body(*refs):
        s_ref, ins, outs = refs[0], list(refs[1:1 + n_in]), list(refs[1 + n_in:])
        off = 0
        for width, p in zip(widths, flat):
            rows = 1 if p is None else p[0].shape[0]
            cols = width // rows
            w_ref, m_ref, v_ref = (None, None, None) if p is None else (ins.pop(0), ins.pop(0), ins.pop(0))
            o_refs = [outs.pop(0) for _ in range(1 if p is None else 4)]
            for r in range(rows):
                seg = slice(off + r * cols, off + (r + 1) * cols)
                g = s_ref[0, :, seg]
                for j in range(1, k):
                    g = g + s_ref[j, :, seg]
                o_refs[0][r:r + 1, :] = g
                if p is not None:
                    row = slice(r, r + 1)
                    mn = ADAM_B1 * m_ref[row, :] + (1.0 - ADAM_B1) * g
                    vn = ADAM_B2 * v_ref[row, :] + (1.0 - ADAM_B2) * (g * g)
                    o_refs[1][row, :] = -ADAM_LR * ((mn / c1) / (jnp.sqrt(vn / c2) + ADAM_EPS) + ADAM_WD * w_ref[row, :])
                    o_refs[2][row, :] = mn
                    o_refs[3][row, :] = vn
            off += width

    operands, out_shape = [stack], []
    for width, p in zip(widths, flat):
        if p is None:
            out_shape.append(jax.ShapeDtypeStruct((1, width), F32))
        else:
            operands += p
            out_shape += [jax.ShapeDtypeStruct(p[0].shape, F32)] * 4
    res = list(pl.pallas_call(body, name=name, out_shape=out_shape)(*operands))
    out = []
    for p, orig in zip(flat, params):
        n = 1 if p is None else 4
        out.append([r if orig is None else r.reshape(orig[0].shape) for r in res[:n]])
        res = res[n:]
    return out


def _coords():
    return lax.axis_index("x"), lax.axis_index("y"), lax.axis_index("c")


def _all_gather(shards, *, name):
    n = len(shards)

    def body(*refs):
        x_refs, out_refs = refs[:n], refs[n:2 * n]
        send_sems, recv_sems, local_sems = refs[2 * n:]
        x, y, c = _coords()
        me, sibling = (x, y, c), (x, y, 1 - c)
        chips = [(1 - x, y), (x, 1 - y), (1 - x, 1 - y)]

        def slot(a, dev):
            return out_refs[a].at[4 * dev[0] + 2 * dev[1] + dev[2]]

        def copy(a, k, block, to, src=None):
            return pltpu.make_async_remote_copy(
                src_ref=slot(a, block) if src is None else src, dst_ref=slot(a, block),
                send_sem=send_sems.at[7 * a + k], recv_sem=recv_sems.at[7 * a + k], device_id=to, device_id_type=MESH)

        mine = [pltpu.make_async_copy(x_refs[a], slot(a, me), local_sems.at[a]) for a in range(n)]
        for cp in mine:
            cp.start()
        first = []
        for a in range(n):
            first.append(copy(a, 0, me, sibling, src=x_refs[a]))
            first += [copy(a, 1 + j, me, (*chip, c), src=x_refs[a]) for j, chip in enumerate(chips)]
        for cp in first:
            cp.start()
        passed = []
        for j, chip in enumerate(chips):
            for a in range(n):
                copy(a, 1 + j, (*chip, c), me).wait_recv()
                fwd = copy(a, 4 + j, (*chip, c), sibling)
                fwd.start()
                passed.append(fwd)
        for a in range(n):
            copy(a, 0, sibling, me).wait_recv()
            for j, chip in enumerate(chips):
                copy(a, 4 + j, (*chip, 1 - c), me).wait_recv()
        for cp in first + passed:
            cp.wait_send()
        for cp in mine:
            cp.wait()

    return pl.pallas_call(
        body, name=name, in_specs=[HBM] * n, out_specs=[HBM] * n,
        out_shape=[jax.ShapeDtypeStruct((N_DEV, *s.shape), s.dtype) for s in shards],
        scratch_shapes=[pltpu.SemaphoreType.DMA((7 * n,)), pltpu.SemaphoreType.DMA((7 * n,)),
                        pltpu.SemaphoreType.DMA((n,))],
    )(*shards)


def _flip_y(x, y, c):
    return (x, 1 - y, c)


def _flip_x(x, y, c):
    return (1 - x, y, c)


def _flip_xy(x, y, c):
    return (1 - x, 1 - y, c)


SEM = pl.BlockSpec(memory_space=pltpu.SEMAPHORE)
SIDE_EFFECT = pltpu.SideEffectType.DATAFLOW_SIDE_EFFECTING


def _in_hbm(a):
    return pltpu.with_memory_space_constraint(a, pltpu.HBM)


def _copies_start(srcs, lands, plan, n_copies, *, name):
    ns, nl = len(srcs), len(lands)

    def body(*refs):
        src_refs, land_refs = refs[:ns], refs[ns:ns + nl]
        send_sems, recv_sems = refs[ns + nl:ns + nl + 2]
        token = refs[-1]
        for k, (src, dst, peer, _) in enumerate(plan(src_refs, land_refs, *_coords())):
            pltpu.make_async_remote_copy(src_ref=src, dst_ref=dst, send_sem=send_sems.at[k], recv_sem=recv_sems.at[k],
                                         device_id=peer, device_id_type=MESH).start()
        token[...] = jnp.zeros_like(token)

    bufs = [*srcs, *lands]
    res = pl.pallas_call(
        body, name=name, in_specs=[HBM] * (ns + nl),
        out_specs=(SEM, SEM, *[HBM] * (ns + nl), pl.BlockSpec(memory_space=pltpu.VMEM)),
        out_shape=(pltpu.SemaphoreType.DMA((n_copies,)), pltpu.SemaphoreType.DMA((n_copies,)),
                   *[pltpu.HBM(b.shape, b.dtype) for b in bufs], jax.ShapeDtypeStruct((SUBLANES, LANES), F32)),
        input_output_aliases={i: 2 + i for i in range(ns + nl)},
        compiler_params=pltpu.CompilerParams(has_side_effects=SIDE_EFFECT),
    )(*[_in_hbm(b) for b in bufs])
    return res[0], res[1], list(res[2:2 + ns]), list(res[2 + ns:2 + ns + nl]), res[-1]


def _copies_wait(started, plan, after, *, name):
    send_sems, recv_sems, srcs, lands, _ = started
    ns, nl = len(srcs), len(lands)

    def body(*refs):
        src_refs, land_refs = refs[:ns], refs[ns:ns + nl]
        send_sems, recv_sems = refs[ns + nl:ns + nl + 2]
        for k, (src, dst, peer, here) in enumerate(plan(src_refs, land_refs, *_coords())):
            pltpu.make_async_remote_copy(src_ref=src, dst_ref=dst, send_sem=send_sems.at[k], recv_sem=recv_sems.at[k],
                                         device_id=peer, device_id_type=MESH).wait_send()
            pltpu.make_async_remote_copy(src_ref=src, dst_ref=here, send_sem=send_sems.at[k], recv_sem=recv_sems.at[k],
                                         device_id=peer, device_id_type=MESH).wait_recv()

    bufs = [*srcs, *lands]
    res = pl.pallas_call(
        body, name=name, in_specs=[HBM] * (ns + nl) + [SEM, SEM, pl.BlockSpec(memory_space=pl.ANY)],
        out_specs=[HBM] * (ns + nl), out_shape=[pltpu.HBM(b.shape, b.dtype) for b in bufs],
        input_output_aliases={i: i for i in range(ns + nl)},
        compiler_params=pltpu.CompilerParams(has_side_effects=SIDE_EFFECT),
    )(*bufs, send_sems, recv_sems, after)
    return list(res[ns:])


def _dev_index(dev):
    return 4 * dev[0] + 2 * dev[1] + dev[2]


def _ag_chips_plan(src_refs, land_refs, x, y, c):
    me = _dev_index((x, y, c))
    return [(src, land.at[me], peer, land.at[_dev_index(peer)])
            for src, land in zip(src_refs, land_refs) for peer in (_flip_y(x, y, c), _flip_x(x, y, c), _flip_xy(x, y, c))]


def _ag_sibling_plan(src_refs, land_refs, x, y, c):
    chips = [(x, y), (x, 1 - y), (1 - x, y), (1 - x, 1 - y)]
    return [(land.at[_dev_index((*chip, c))], land.at[_dev_index((*chip, c))], (x, y, 1 - c),
             land.at[_dev_index((*chip, 1 - c))]) for land in land_refs for chip in chips]


def _ag_direct_plan(src_refs, land_refs, x, y, c):
    me = _dev_index((x, y, c))
    plan = []
    for src, land in zip(src_refs, land_refs):
        for m in range(1, N_DEV):
            peer = (x + (m >> 2) * (1 - 2 * x), y + ((m >> 1) & 1) * (1 - 2 * y), c + (m & 1) * (1 - 2 * c))
            plan.append((src, land.at[me], peer, land.at[_dev_index(peer)]))
    return plan


def _rs_direct_plan(src_refs, land_refs, x, y, c):
    plan = []
    for src, land in zip(src_refs, land_refs):
        for m in range(1, N_DEV):
            peer = (x + (m >> 2) * (1 - 2 * x), y + ((m >> 1) & 1) * (1 - 2 * y), c + (m & 1) * (1 - 2 * c))
            plan.append((src.at[_dev_index(peer)], land.at[m - 1], peer, land.at[m - 1]))
    return plan


def _rs_start(grads, me, *, name):
    own = [lax.dynamic_index_in_dim(g, me, 0, keepdims=False) for g in grads]
    lands = [lax.empty((N_DEV - 1, *g.shape[1:]), g.dtype) for g in grads]
    return _copies_start(grads, lands, _rs_direct_plan, (N_DEV - 1) * len(grads), name=name), own


def _rs_finish(started, after, *, name):
    handle, own = started
    got = _copies_wait(handle, _rs_direct_plan, after, name=name)
    return [[o, land] for o, land in zip(own, got)]


def _gathered_cols(w8):
    return w8.transpose(1, 0, 2).reshape(w8.shape[1], -1)


def _pair_major(wt, inverse=False):
    a, b = (ATTN_W // LANES, 3) if inverse else (3, ATTN_W // LANES)
    return wt.reshape(a, b, LANES, -1).transpose(1, 0, 2, 3).reshape(3 * ATTN_W, -1)


def kernel(x, norm1_g, w_in, attn_norm_g, hgrn_norm_g, hgrn_lb_logits, w_out, norm2_g, w_up, conv_w, conv_b, w_down, final_norm_g, loss_target, m_norm1_g, m_w_in, m_attn_norm_g, m_hgrn_norm_g, m_hgrn_lb_logits, m_w_out, m_norm2_g, m_w_up, m_conv_w, m_conv_b, m_w_down, m_final_norm_g, v_norm1_g, v_w_in, v_attn_norm_g, v_hgrn_norm_g, v_hgrn_lb_logits, v_w_out, v_norm2_g, v_w_up, v_conv_w, v_conv_b, v_w_down, v_final_norm_g):
    xs, target = x[0], loss_target[0]
    S, D = xs.shape
    NA = 3 * ATTN_W
    fng = final_norm_g.reshape(1, D)

    t = lambda a: a[0].T
    casts = [_sum_cast([w], BF16, name=f"cast_{nm}") for nm, w in
             (("w_in", t(w_in)), ("w_out", w_out[0]), ("w_up", t(w_up)), ("w_down", w_down[0]))]
    me = _dev_index(_coords())
    (g_in,) = _all_gather(casts[:1], name="ag_w_in")
    later = casts[1:] + [conv_w[0]]
    ag1 = _copies_start(later, [lax.empty((N_DEV, *s.shape), s.dtype) for s in later], _ag_chips_plan,
                        3 * len(later), name="ag_chips_start")
    wi = g_in.reshape(-1, D)
    wi_a = _pair_major(wi[:NA])

    u1, proj_a = _proj_attn(xs, norm1_g + ag1[4][0, 0], wi_a, name="proj_attn")
    proj_h = _mm_nt(u1, wi, NA, wi.shape[0] - NA, name="proj_hgrn")
    attn, lse = _attn_fwd(proj_a, name="attn_fwd")
    lands = _copies_wait(ag1, _ag_chips_plan, attn, name="ag_chips_wait")
    lands = [lax.dynamic_update_index_in_dim(l, s, me, 0) for l, s in zip(lands, later)]
    ag2 = _copies_start([], lands, _ag_sibling_plan, 4 * len(later), name="ag_sibling_start")
    rec, states = _hgrn_fwd(proj_h, hgrn_lb_logits + ag2[4][0, 0], name="hgrn_fwd")
    g_out, g_up, g_down, g_cw = _copies_wait(ag2, _ag_sibling_plan, rec, name="ag_sibling_wait")
    wo = g_out.reshape(-1, D)
    wu = g_up.reshape(-1, D)
    wd = g_down.reshape(-1, D)
    cw = _gathered_cols(g_cw)
    h1, u2, mixed = _out_proj(attn, rec, proj_h, xs, attn_norm_g, hgrn_norm_g, norm2_g, wo, name="out_proj")
    gate, val, act = _up_glu(u2, wu, cw, conv_b, name="up_glu")
    dh2, dh2b, d_fng, loss_part = _down_loss(act, wd, h1, fng, target, name="down_loss")

    dgate, dval, d_cw, d_cb = _dact_glu_bwd(dh2b, wd, gate, val, cw, conv_b, name="dact_glu_bwd")
    dw_down = _mm_tn(act, dh2b, tm=256, name="dw_down")
    dh1, dh1b, d_n2g = _grad_norm_input([dgate, dval], [(wu, 0)], h1, norm2_g, dh2, name="du2_norm2_bwd")
    dw_up = jnp.concatenate([_mm_tn(dy, u2, tm=256, name=f"dw_up_{nm}") for nm, dy in (("gate", dgate), ("val", dval))])
    rs_ffn = _rs_start([dw_down.reshape(N_DEV, -1, D), dw_up.reshape(N_DEV, -1, D)], me, name="rs_ffn_start")
    dattn, delta, drec, dhg, d_ang, d_hng = _dmix_post_bwd(dh1b, wo, attn, rec, proj_h, attn_norm_g + rs_ffn[0][4][0, 0],
                                                          hgrn_norm_g, name="dmix_post_bwd")
    dw_out = _mm_tn(mixed, dh1b, name="dw_out")
    rs_out = _rs_start([dw_out.reshape(N_DEV, -1, D)], me, name="rs_out_start")
    dproj_h, d_lbl = _hgrn_bwd(proj_h, hgrn_lb_logits + rs_out[0][4][0, 0], states, drec, name="hgrn_bwd")
    small = [("loss", loss_part, None, None, None),
             ("attn_norm_g", d_ang, attn_norm_g, m_attn_norm_g, v_attn_norm_g),
             ("hgrn_norm_g", d_hng, hgrn_norm_g, m_hgrn_norm_g, v_hgrn_norm_g),
             ("hgrn_lb_logits", d_lbl, hgrn_lb_logits, m_hgrn_lb_logits, v_hgrn_lb_logits),
             ("norm2_g", d_n2g, norm2_g, m_norm2_g, v_norm2_g),
             ("conv_b", d_cb, conv_b, m_conv_b, v_conv_b),
             ("final_norm_g", d_fng, final_norm_g, m_final_norm_g, v_final_norm_g)]
    pack = lambda arrs: jnp.concatenate([a.reshape(1, -1) for a in arrs], axis=1)
    small_own = [pack([s[1] for s in small]), d_cw]
    ag_small = _copies_start(small_own, [lax.empty((N_DEV, *s.shape), s.dtype) for s in small_own], _ag_direct_plan,
                             (N_DEV - 1) * len(small_own), name="ag_small_start")
    dproj_a = _attn_bwd(proj_a, dattn, lse, delta, name="attn_bwd")
    dw_in = jnp.concatenate([_pair_major(_mm_tn(dproj_a, u1, tm=256, name="dw_in_attn"), inverse=True),
                             _mm_tn(dproj_h, u1, tm=256, name="dw_in_hgrn"),
                             _mm_tn(dhg, u1, tm=256, name="dw_in_gate")])
    rs_in = _rs_start([dw_in.reshape(N_DEV, -1, D)], me, name="rs_in_start")
    grad_x, _, d_n1g = _grad_norm_input([dproj_a, dproj_h, dhg], [(wi_a, 0), (wi, NA)], xs,
                                        norm1_g + rs_in[0][4][0, 0], dh1, name="du1_norm1_bwd")

    res = {}

    def update(nm, parts, w, m, v, transposed=False):
        if transposed:
            res[nm] = [r.T[None] for r in _adamw(parts, t(w), t(m), t(v), name=f"adamw_{nm}")]
        else:
            res[nm] = _adamw(parts, w, m, v, name=f"adamw_{nm}")

    g_down, g_up = _rs_finish(rs_ffn, grad_x, name="rs_ffn_wait")
    update("w_down", g_down, w_down, m_w_down, v_w_down)
    update("w_up", g_up, w_up, m_w_up, v_w_up, transposed=True)
    (g_out,) = _rs_finish(rs_out, grad_x, name="rs_out_wait")
    update("w_out", g_out, w_out, m_w_out, v_w_out)
    (g_in,) = _rs_finish(rs_in, res["w_up"][1], name="rs_in_wait")
    update("w_in", g_in, w_in, m_w_in, v_w_in, transposed=True)

    g_small, g_dcw = [lax.dynamic_update_index_in_dim(l, s, me, 0)
                      for l, s in zip(_copies_wait(ag_small, _ag_direct_plan, grad_x, name="ag_small_wait"), small_own)]
    sm = _adamw_packed(g_small, [s[1].size for s in small], [None if s[2] is None else s[2:] for s in small],
                       name="adamw_small")
    for (nm, *_), r in zip(small, sm):
        res[nm] = r
    ncw = conv_w.shape[-1]
    mine_cw = lax.dynamic_slice_in_dim(g_dcw, me * ncw, ncw, axis=2)
    res["conv_w"] = _adamw([mine_cw], conv_w, m_conv_w, v_conv_w, name="adamw_conv_w")
    late, _ = lax.optimization_barrier((d_n1g, res["w_in"][1]))
    update("norm1_g", _all_gather([late], name="ag_norm1_grad"), norm1_g, m_norm1_g, v_norm1_g)

    loss = res["loss"][0][0, 0]
    order = ["norm1_g", "w_in", "attn_norm_g", "hgrn_norm_g", "hgrn_lb_logits", "w_out", "norm2_g", "w_up",
             "conv_w", "conv_b", "w_down", "final_norm_g"]
    return (loss, grad_x[None], *[res[nm][0] for nm in order], *[res[nm][1] for nm in order],
            *[res[nm][2] for nm in order], *[res[nm][3] for nm in order])
```

```python
import jax
import jax.numpy as jnp
from jax import lax
from jax.experimental import pallas as pl
from jax.experimental.pallas import tpu as pltpu

F32, BF16 = jnp.float32, jnp.bfloat16
NORM_EPS = 1e-6
ATTN_HEADS, HEAD_DIM, ATTN_BLOCK = 8, 64, 128
DILATIONS = (1, 4, 16)
ATTN_SCALE = HEAD_DIM ** -0.5
ATTN_W = ATTN_HEADS * HEAD_DIM
HGRN_HEADS, HGRN_DIM, HGRN_CHUNK = 4, 128, 64
HGRN_W = HGRN_HEADS * HGRN_DIM
ADAM_LR, ADAM_B1, ADAM_B2, ADAM_EPS, ADAM_WD, ADAM_STEP = 0.001, 0.9, 0.999, 1e-08, 0.01, 10
LANES, SUBLANES = 128, 8
VMEM_LIMIT_BYTES = 56 * 1024 * 1024
N_DEV = 8
MESH = pl.DeviceIdType.MESH
HBM = pl.BlockSpec(memory_space=pltpu.HBM)
HIGHEST = lax.Precision.HIGHEST


def _cparams(*sem):
    return pltpu.CompilerParams(dimension_semantics=sem, vmem_limit_bytes=VMEM_LIMIT_BYTES)


def _tile(n, pref):
    if n <= pref:
        return n
    t = (pref // LANES) * LANES
    while n % t:
        t -= LANES
    return t


def _resident(shape):
    return pl.BlockSpec(shape, lambda *_: (0,) * len(shape), pipeline_mode=pl.Buffered(1))


def _dot(a, b, dims, precision=None):
    return lax.dot_general(a, b, (dims, ((), ())), precision=precision, preferred_element_type=F32)


def _nn(a, b, precision=None):
    return _dot(a, b, ((1,), (0,)), precision)


def _nt(a, b):
    return _dot(a, b, ((1,), (1,)))


def _tn(a, b):
    return _dot(a, b, ((0,), (0,)))


def _sigmoid(x):
    return 1.0 / (1.0 + jnp.exp(-x))


def _rstd(x):
    return lax.rsqrt(jnp.mean(x * x, axis=-1, keepdims=True) + NORM_EPS)


def _norm_bwd(x, g, du):
    r = _rstd(x)
    xh = x * r
    dxh = du * g
    return r * (dxh - xh * jnp.mean(dxh * xh, axis=-1, keepdims=True)), du * xh


def _row_halves(tm):
    return [pl.ds(0, tm // 2), pl.ds(tm // 2, tm // 2)]


def _accumulate(ref, part, first):
    @pl.when(first)
    def _():
        ref[...] = part

    @pl.when(jnp.logical_not(first))
    def _():
        ref[...] += part


def _mm_nt(a, bt, row0, n, *, name, out_dtype=F32, tm=1024, tn=512):
    M, K = a.shape
    tm, tn = _tile(M, tm), _tile(n, tn)
    j0 = row0 // tn

    def body(a_ref, b_ref, o_ref):
        o_ref[...] = _nt(a_ref[...], b_ref[...]).astype(out_dtype)

    return pl.pallas_call(
        body, name=name, grid=(M // tm, n // tn),
        in_specs=[pl.BlockSpec((tm, K), lambda i, j: (i, 0)), pl.BlockSpec((tn, K), lambda i, j: (j + j0, 0))],
        out_specs=pl.BlockSpec((tm, tn), lambda i, j: (i, j)), out_shape=jax.ShapeDtypeStruct((M, n), out_dtype),
        compiler_params=_cparams("parallel", "parallel"),
    )(a, bt)


def _mm_tn(x, dy, *, name, tm=512, tn=1024):
    S, M = x.shape
    N = dy.shape[1]
    tm, tn = _tile(M, tm), _tile(N, tn)

    def body(x_ref, dy_ref, o_ref, xt_ref):
        @pl.when(pl.program_id(1) == 0)
        def _():
            xt_ref[...] = x_ref[...].T

        o_ref[...] = _nn(xt_ref[...], dy_ref[...]).astype(BF16)

    return pl.pallas_call(
        body, name=name, grid=(M // tm, N // tn),
        in_specs=[pl.BlockSpec((S, tm), lambda i, j: (0, i)), pl.BlockSpec((S, tn), lambda i, j: (0, j))],
        out_specs=pl.BlockSpec((tm, tn), lambda i, j: (i, j)), out_shape=jax.ShapeDtypeStruct((M, N), BF16),
        scratch_shapes=[pltpu.VMEM((tm, S), BF16)], compiler_params=_cparams("parallel", "arbitrary"),
    )(x, dy)


def _proj_attn(x, g, wt, *, name, tm=1024, tn=512):
    S, D = x.shape
    N = wt.shape[0]

    def body(x_ref, g_ref, w_ref, u_ref, o_ref):
        @pl.when(pl.program_id(1) == 0)
        def _():
            xv = x_ref[...]
            u_ref[...] = (xv * _rstd(xv) * g_ref[...]).astype(BF16)

        o_ref[...] = _nt(u_ref[...], w_ref[...]).astype(BF16)

    return pl.pallas_call(
        body, name=name, grid=(S // tm, N // tn),
        in_specs=[pl.BlockSpec((tm, D), lambda i, j: (i, 0)), pl.BlockSpec((1, D), lambda i, j: (0, 0)),
                  pl.BlockSpec((tn, D), lambda i, j: (j, 0))],
        out_specs=[pl.BlockSpec((tm, D), lambda i, j: (i, 0)), pl.BlockSpec((tm, tn), lambda i, j: (i, j))],
        out_shape=[jax.ShapeDtypeStruct((S, D), BF16), jax.ShapeDtypeStruct((S, N), BF16)],
        compiler_params=_cparams("parallel", "arbitrary"),
    )(x, g, wt)


PAIR_W = 3 * LANES
ATTN_UNROLL_FWD, ATTN_UNROLL_BWD = 4, 4


def _attn_masks(first):
    qi = lax.broadcasted_iota(jnp.int32, (ATTN_BLOCK, 2 * ATTN_BLOCK), 0)
    kj = lax.broadcasted_iota(jnp.int32, (ATTN_BLOCK, 2 * ATTN_BLOCK), 1)
    dist = qi + ATTN_BLOCK - kj
    valid = (dist >= 0) & (dist <= ATTN_BLOCK) & jnp.logical_or(kj >= ATTN_BLOCK, jnp.logical_not(first))
    lane = lax.broadcasted_iota(jnp.int32, (1, LANES), 1)
    return valid, lane


def _for_residue_blocks(S, d, fn):
    span = ATTN_BLOCK * d
    nb = S // span

    def step(n, carry):
        base = pl.multiple_of(n * span, span)
        for r in range(d):
            off = pl.multiple_of((r * nb + n) * ATTN_BLOCK, ATTN_BLOCK)
            fn(lambda ref, r=r: _block_rows(ref, base, r, d),
               lambda ref, val, r=r: _set_block_rows(ref, base, r, d, val), off)
        return carry

    lax.fori_loop(0, nb, step, 0)


def _for_blocks(S, unroll, fn):
    def step(i, carry):
        fn([(pl.multiple_of((i * unroll + u) * ATTN_BLOCK, ATTN_BLOCK), i * unroll + u) for u in range(unroll)])
        return carry

    lax.fori_loop(0, S // ATTN_BLOCK // unroll, step, 0)


def _head_value(x2, lane, e):
    return jnp.sum(jnp.where(lane == HEAD_DIM * e, x2, 0.0), axis=-1, keepdims=True)


def _block_rows(ref, base, r, d):
    if d == 1:
        return ref[pl.ds(base, ATTN_BLOCK), :]
    return ref.at[pl.ds(base, ATTN_BLOCK * d)][pl.ds(r, ATTN_BLOCK, stride=d), :]


def _set_block_rows(ref, base, r, d, val):
    if d == 1:
        ref[pl.ds(base, ATTN_BLOCK), :] = val
    else:
        ref.at[pl.ds(base, ATTN_BLOCK * d)][pl.ds(r, ATTN_BLOCK, stride=d), :] = val


def _order4_to_16(src, dst, pad):
    S = src.shape[0]
    q4, q16 = S // 4, S // 16
    for r in range(4):
        for a in range(4):
            for n in range(q16 // ATTN_BLOCK):
                rows = src.at[pl.ds(r * q4 + 4 * ATTN_BLOCK * n, 4 * ATTN_BLOCK)][pl.ds(a, ATTN_BLOCK, stride=4), :]
                dst[pl.ds(pad + (4 * a + r) * q16 + ATTN_BLOCK * n, ATTN_BLOCK), :] = rows.astype(dst.dtype)


def _order16_to_4(src, pad, dst):
    S = dst.shape[0]
    q4, q16 = S // 4, S // 16
    for r in range(4):
        for a in range(4):
            for n in range(q16 // ATTN_BLOCK):
                rows = src[pl.ds(pad + (4 * a + r) * q16 + ATTN_BLOCK * n, ATTN_BLOCK), :]
                dst.at[pl.ds(r * q4 + 4 * ATTN_BLOCK * n, 4 * ATTN_BLOCK)][pl.ds(a, ATTN_BLOCK, stride=4), :] = rows


def _regroup(S, d, pairs, tmp):
    for src, dst, pad in pairs:
        if d == 16:
            def to_tmp(rows, _, off, src=src):
                tmp[pl.ds(off, ATTN_BLOCK), :] = rows(src)

            _for_residue_blocks(S, 4, to_tmp)
            _order4_to_16(tmp, dst, pad)
    if d != 16:
        def to_dst(rows, _, off):
            for src, dst, pad in pairs:
                dst[pl.ds(pad + off, ATTN_BLOCK), :] = rows(src).astype(dst.dtype)

        _for_residue_blocks(S, d, to_dst)


def _split_pair(p_ref, qs, ks, vs, bk, bv):
    qs[...] = p_ref[:, 0:LANES].astype(F32)
    ks[...] = p_ref[:, LANES:2 * LANES].astype(F32)
    vs[...] = p_ref[:, 2 * LANES:3 * LANES].astype(F32)
    bk[0:ATTN_BLOCK, :] = jnp.zeros((ATTN_BLOCK, LANES), bk.dtype)
    bv[0:ATTN_BLOCK, :] = jnp.zeros((ATTN_BLOCK, LANES), bv.dtype)


def _attn_fwd(proj_a, *, name):
    S = proj_a.shape[0]

    def body(p_ref, o_ref, l_ref, qs, ks, vs, bq, bk, bv, bo, bl, to, tl):
        _split_pair(p_ref, qs, ks, vs, bk, bv)
        for d in DILATIONS:
            nb = S // (ATTN_BLOCK * d)
            _regroup(S, d, ((qs, bq, 0), (ks, bk, ATTN_BLOCK), (vs, bv, ATTN_BLOCK)), to)

            def blocks(group, nb=nb):
                lane = lax.broadcasted_iota(jnp.int32, (1, LANES), 1)
                heads = [(lane >= HEAD_DIM * e) & (lane < HEAD_DIM * (e + 1)) for e in range(LANES // HEAD_DIM)]
                wins = [pl.ds(off, 2 * ATTN_BLOCK) for off, _ in group]
                s = [[_nt(jnp.where(mh, bq[pl.ds(off, ATTN_BLOCK), :], jnp.zeros((ATTN_BLOCK, LANES), BF16)), bk[win, :])
                      for mh in heads] for (off, _), win in zip(group, wins)]
                p, m, l = [], [], []
                for (off, b), su in zip(group, s):
                    valid, _ = _attn_masks(jnp.bitwise_and(b, nb - 1) == 0)
                    sm = [jnp.where(valid, x * ATTN_SCALE, -jnp.inf) for x in su]
                    m.append([jnp.max(x, axis=-1, keepdims=True) for x in sm])
                    p.append([jnp.exp(x - mx) for x, mx in zip(sm, m[-1])])
                    l.append([jnp.sum(x, axis=-1, keepdims=True) for x in p[-1]])
                o = [[_nn(x.astype(BF16), bv[win, :]) for x in pu] for pu, win in zip(p, wins)]
                for (off, _), ou, mu, lu in zip(group, o, m, l):
                    o2 = jnp.zeros((ATTN_BLOCK, LANES), F32)
                    l2 = jnp.zeros((ATTN_BLOCK, LANES), F32)
                    for mh, oe, me_, le in zip(heads, ou, mu, lu):
                        o2 = jnp.where(mh, oe / le, o2)
                        l2 = jnp.where(mh, me_ + jnp.log(le), l2)
                    bo[pl.ds(off, ATTN_BLOCK), :] = o2
                    bl[pl.ds(off, ATTN_BLOCK), :] = l2

            _for_blocks(S, ATTN_UNROLL_FWD, blocks)

            if d == 16:
                _order16_to_4(bo, 0, to)
                _order16_to_4(bl, 0, tl)
            src_o, src_l = (to, tl) if d == 16 else (bo, bl)

            def merge(rows, set_rows, off, d=d, src_o=src_o, src_l=src_l):
                blk = pl.ds(off, ATTN_BLOCK)
                o2, l2 = src_o[blk, :], src_l[blk, :]
                if d != DILATIONS[0]:
                    lo, oo = rows(l_ref), rows(o_ref)
                    ln = jnp.maximum(lo, l2)
                    wa, wb = jnp.exp(lo - ln), jnp.exp(l2 - ln)
                    o2 = (wa * oo + wb * o2) / (wa + wb)
                    l2 = ln + jnp.log(wa + wb)
                set_rows(o_ref, o2)
                set_rows(l_ref, l2)

            _for_residue_blocks(S, min(d, 4), merge)

    slab = pl.BlockSpec((S, LANES), lambda p: (0, p))
    f32_slab, bf16_slab = pltpu.VMEM((S, LANES), F32), pltpu.VMEM((S, LANES), BF16)
    bf16_window = pltpu.VMEM((S + ATTN_BLOCK, LANES), BF16)
    return pl.pallas_call(
        body, name=name, grid=(ATTN_W // LANES,), in_specs=[pl.BlockSpec((S, PAIR_W), lambda p: (0, p))],
        out_specs=[slab, slab],
        out_shape=[jax.ShapeDtypeStruct((S, ATTN_W), F32), jax.ShapeDtypeStruct((S, ATTN_W), F32)],
        scratch_shapes=[f32_slab] * 3 + [bf16_slab, bf16_window, bf16_window] + [f32_slab] * 4,
        compiler_params=_cparams("parallel"),
    )(proj_a)


def _attn_bwd(proj_a, do, lse, delta, *, name):
    S = proj_a.shape[0]

    def body(p_ref, do_ref, lse_ref, dl_ref, o_ref, qs, ks, vs, dqs, dks, dvs, bq, bk, bv, bdo, blse, bdl, bdq, bdk, bdv,
             tmp):
        _split_pair(p_ref, qs, ks, vs, bk, bv)
        bdk[0:ATTN_BLOCK, :] = jnp.zeros((ATTN_BLOCK, LANES), F32)
        bdv[0:ATTN_BLOCK, :] = jnp.zeros((ATTN_BLOCK, LANES), F32)
        for d in DILATIONS:
            nb = S // (ATTN_BLOCK * d)
            _regroup(S, d, ((qs, bq, 0), (ks, bk, ATTN_BLOCK), (vs, bv, ATTN_BLOCK), (do_ref, bdo, 0),
                            (lse_ref, blse, 0), (dl_ref, bdl, 0)), tmp)

            def blocks(group, nb=nb):
                lane = lax.broadcasted_iota(jnp.int32, (1, LANES), 1)
                heads = [(lane >= HEAD_DIM * e) & (lane < HEAD_DIM * (e + 1)) for e in range(LANES // HEAD_DIM)]
                zero = jnp.zeros((ATTN_BLOCK, LANES), BF16)
                chains = [(off, b, e, mh) for off, b in group for e, mh in enumerate(heads)]
                qm = [jnp.where(mh, bq[pl.ds(off, ATTN_BLOCK), :], zero) for off, _, _, mh in chains]
                dom = [jnp.where(mh, bdo[pl.ds(off, ATTN_BLOCK), :], zero) for off, _, _, mh in chains]
                s = [_nt(x, bk[pl.ds(off, 2 * ATTN_BLOCK), :]) for x, (off, _, _, _) in zip(qm, chains)]
                dp = [_nt(x, bv[pl.ds(off, 2 * ATTN_BLOCK), :]) for x, (off, _, _, _) in zip(dom, chains)]
                p, ds = [], []
                for (off, b, e, _), sc, dpc in zip(chains, s, dp):
                    valid, _ = _attn_masks(jnp.bitwise_and(b, nb - 1) == 0)
                    blk = pl.ds(off, ATTN_BLOCK)
                    pc = jnp.where(valid, jnp.exp(sc * ATTN_SCALE - _head_value(blse[blk, :], lane, e)), 0.0)
                    ds.append((pc * (dpc - _head_value(bdl[blk, :], lane, e)) * ATTN_SCALE).astype(BF16))
                    p.append(pc.astype(BF16))
                dq = [_nn(x, bk[pl.ds(off, 2 * ATTN_BLOCK), :]) for x, (off, _, _, _) in zip(ds, chains)]
                dk = [_tn(x, y) for x, y in zip(ds, qm)]
                dv = [_tn(x, y) for x, y in zip(p, dom)]
                nh = len(heads)
                for u, (off, _) in enumerate(group):
                    dq2 = jnp.zeros((ATTN_BLOCK, LANES), F32)
                    for mh, x in zip(heads, dq[nh * u:nh * (u + 1)]):
                        dq2 = jnp.where(mh, x, dq2)
                    bdq[pl.ds(off, ATTN_BLOCK), :] = dq2
                    for acc, grads in ((bdk, dk), (bdv, dv)):
                        win_grad = sum(grads[nh * u + 1:nh * (u + 1)], grads[nh * u])
                        acc[pl.ds(off, ATTN_BLOCK), :] += win_grad[:ATTN_BLOCK]
                        acc[pl.ds(off + ATTN_BLOCK, ATTN_BLOCK), :] = win_grad[ATTN_BLOCK:]

            _for_blocks(S, ATTN_UNROLL_BWD, blocks)

            outs = ((dqs, bdq, 0), (dks, bdk, ATTN_BLOCK), (dvs, bdv, ATTN_BLOCK))
            if d == 16:
                for acc, grad, pad in outs:
                    _order16_to_4(grad, pad, tmp)

                    def add(rows, set_rows, off, acc=acc):
                        set_rows(acc, rows(acc) + tmp[pl.ds(off, ATTN_BLOCK), :])

                    _for_residue_blocks(S, 4, add)
            else:
                def scatter(rows, set_rows, off, d=d):
                    for acc, grad, pad in outs:
                        part = grad[pl.ds(pad + off, ATTN_BLOCK), :]
                        set_rows(acc, part if d == DILATIONS[0] else rows(acc) + part)

                _for_residue_blocks(S, d, scatter)
        o_ref[:, 0:LANES] = dqs[...].astype(BF16)
        o_ref[:, LANES:2 * LANES] = dks[...].astype(BF16)
        o_ref[:, 2 * LANES:3 * LANES] = dvs[...].astype(BF16)

    slab = pl.BlockSpec((S, LANES), lambda p: (0, p), pipeline_mode=pl.Buffered(1))
    pair = pl.BlockSpec((S, PAIR_W), lambda p: (0, p))
    f32_slab, bf16_slab = pltpu.VMEM((S, LANES), F32), pltpu.VMEM((S, LANES), BF16)
    f32_window, bf16_window = pltpu.VMEM((S + ATTN_BLOCK, LANES), F32), pltpu.VMEM((S + ATTN_BLOCK, LANES), BF16)
    return pl.pallas_call(
        body, name=name, grid=(ATTN_W // LANES,), in_specs=[pair, slab, slab, slab], out_specs=pair,
        out_shape=jax.ShapeDtypeStruct(proj_a.shape, BF16),
        scratch_shapes=[f32_slab] * 6 + [bf16_slab, bf16_window, bf16_window, bf16_slab, f32_slab, f32_slab,
                                         f32_slab, f32_window, f32_window, f32_slab],
        compiler_params=_cparams("parallel"),
    )(proj_a, do, lse, delta)


HG_T = 2 * HGRN_CHUNK


def _hgrn_consts():
    row = lax.broadcasted_iota(jnp.int32, (HG_T, HG_T), 0)
    col = lax.broadcasted_iota(jnp.int32, (HG_T, HG_T), 1)
    same = (row >= HGRN_CHUNK) == (col >= HGRN_CHUNK)
    return row, same & (col <= row), same & (col >= row)


def _lower_bound(logits_ref):
    l0, l1 = logits_ref[0:1, :], logits_ref[1:2, :]
    mx = jnp.maximum(l0, l1)
    e0, e1 = jnp.exp(l0 - mx), jnp.exp(l1 - mx)
    return e0 / (e0 + e1)


def _hgrn_gates(qs, fs, lbs, row, causal):
    C = HGRN_CHUNK
    tri = jnp.where(causal, 1.0, 0.0).astype(F32)
    sgs = [_sigmoid(f) for f in fs]
    forgets = [lb + (1.0 - lb) * sg for lb, sg in zip(lbs, sgs)]
    logfs = [jnp.log(forget) for forget in forgets]
    bs = [_nn(tri, logf, HIGHEST) for logf in logfs]
    out = []
    for q, sg, forget, logf, b in zip(qs, sgs, forgets, logfs, bs):
        key = 1.0 - forget
        bend0 = jnp.sum(logf[:C], axis=0, keepdims=True)
        bend1 = jnp.sum(logf[C:], axis=0, keepdims=True)
        bend = jnp.where(row < C, bend0, bend1)
        eb, emb, eend = jnp.exp(b), jnp.exp(-b), jnp.exp(bend - b)
        sq = _sigmoid(q)
        out.append(dict(sg=sg, forget=forget, key=key, bend0=bend0, bend1=bend1, eb=eb, emb=emb, eend=eend, sq=sq,
                        qd=q * sq * eb, ki=key * emb, ke=key * eend))
    return out


def _hgrn_fwd(proj, logits, *, name):
    S = proj.shape[0]
    W, C = HGRN_W, HGRN_CHUNK

    def body(q_ref, f_ref, i_ref, lg_ref, rec_ref, st_ref, s_ref):
        @pl.when(pl.program_id(0) == 0)
        def _():
            s_ref[...] = jnp.zeros_like(s_ref)

        row, causal, _ = _hgrn_consts()
        lb_all = _lower_bound(lg_ref)
        heads = range(HGRN_HEADS)
        sls = [slice(HGRN_DIM * h, HGRN_DIM * (h + 1)) for h in heads]
        gts = _hgrn_gates([q_ref[:, sl] for sl in sls], [f_ref[:, sl] for sl in sls], [lb_all[:, sl] for sl in sls],
                          row, causal)
        qd, ki, ke = ([gt[k].astype(BF16) for gt in gts] for k in ("qd", "ki", "ke"))
        iv = [i_ref[:, sl].astype(BF16) for sl in sls]
        s0 = [s_ref[h] for h in heads]
        a = [_nt(qd[h], ki[h]) for h in heads]
        o0 = [_nt(qd[h][:C], s0[h].astype(BF16)) for h in heads]
        u0 = [_tn(iv[h][:C], ke[h][:C]) for h in heads]
        u1 = [_tn(iv[h][C:], ke[h][C:]) for h in heads]
        s1 = [jnp.exp(gts[h]["bend0"]) * s0[h] + u0[h] for h in heads]
        o = [_nn(jnp.where(causal, a[h], 0.0).astype(BF16), iv[h]) for h in heads]
        o1 = [_nt(qd[h][C:], s1[h].astype(BF16)) for h in heads]
        for h in heads:
            st_ref[0, h] = s0[h]
            st_ref[1, h] = s1[h]
            s_ref[h] = jnp.exp(gts[h]["bend1"]) * s1[h] + u1[h]
            rec_ref[:, sls[h]] = o[h] + jnp.concatenate([o0[h], o1[h]], axis=0)

    blk = lambda j: pl.BlockSpec((HG_T, W), lambda t: (t, j))
    return pl.pallas_call(
        body, name=name, grid=(S // HG_T,),
        in_specs=[blk(0), blk(1), blk(2), pl.BlockSpec((2, W), lambda t: (0, 0))],
        out_specs=[blk(0), pl.BlockSpec((2, HGRN_HEADS, HGRN_DIM, HGRN_DIM), lambda t: (t, 0, 0, 0))],
        out_shape=[jax.ShapeDtypeStruct((S, W), F32),
                   jax.ShapeDtypeStruct((S // C, HGRN_HEADS, HGRN_DIM, HGRN_DIM), F32)],
        scratch_shapes=[pltpu.VMEM((HGRN_HEADS, HGRN_DIM, HGRN_DIM), F32)],
        compiler_params=_cparams("arbitrary"),
    )(proj, proj, proj, logits)


def _hgrn_bwd(proj, logits, states, drec, *, name):
    S = proj.shape[0]
    W, C = HGRN_W, HGRN_CHUNK
    nt = S // HG_T

    def body(q_ref, f_ref, i_ref, lg_ref, st_ref, do_ref, dp_ref, dlg_ref, ds_ref, dlb_ref):
        t = pl.program_id(0)

        @pl.when(t == 0)
        def _():
            ds_ref[...] = jnp.zeros_like(ds_ref)
            dlb_ref[...] = jnp.zeros_like(dlb_ref)

        row, causal, anti = _hgrn_consts()
        lb_all = _lower_bound(lg_ref)
        heads = range(HGRN_HEADS)
        sls = [slice(HGRN_DIM * h, HGRN_DIM * (h + 1)) for h in heads]
        qs, lbs = [q_ref[:, sl] for sl in sls], [lb_all[:, sl] for sl in sls]
        gts = _hgrn_gates(qs, [f_ref[:, sl] for sl in sls], lbs, row, causal)
        qd, ki, ke = ([gt[k] for gt in gts] for k in ("qd", "ki", "ke"))
        qdb, kib, keb = ([x.astype(BF16) for x in xs] for xs in (qd, ki, ke))
        iv = [i_ref[:, sl].astype(BF16) for sl in sls]
        dob = [do_ref[:, sl].astype(BF16) for sl in sls]
        s0, s1, ds1 = [st_ref[0, h] for h in heads], [st_ref[1, h] for h in heads], [ds_ref[h] for h in heads]
        ds1b = [x.astype(BF16) for x in ds1]
        dec0, dec1 = [jnp.exp(gt["bend0"]) for gt in gts], [jnp.exp(gt["bend1"]) for gt in gts]
        a = [_nt(qdb[h], kib[h]) for h in heads]
        da = [_nt(dob[h], iv[h]) for h in heads]
        dqd1 = [_nn(dob[h][C:], s1[h].astype(BF16)) for h in heads]
        dqd0 = [_nn(dob[h][:C], s0[h].astype(BF16)) for h in heads]
        di1 = [_nt(keb[h][C:], ds1b[h]) for h in heads]
        dke1 = [_nn(iv[h][C:], ds1b[h]) for h in heads]
        t1 = [_tn(dob[h][C:], qdb[h][C:]) for h in heads]
        t0 = [_tn(dob[h][:C], qdb[h][:C]) for h in heads]
        ds0 = [dec1[h] * ds1[h] + t1[h] for h in heads]
        ds0b = [x.astype(BF16) for x in ds0]
        a = [jnp.where(causal, x, 0.0).astype(BF16) for x in a]
        da = [jnp.where(causal, x, 0.0).astype(BF16) for x in da]
        di0 = [_nt(keb[h][:C], ds0b[h]) for h in heads]
        dke0 = [_nn(iv[h][:C], ds0b[h]) for h in heads]
        dqd_a = [_nn(da[h], kib[h]) for h in heads]
        dki = [_tn(da[h], qdb[h]) for h in heads]
        di_a = [_tn(a[h], dob[h]) for h in heads]
        dqd, dke, db = [], [], []
        for h in heads:
            ds_ref[h] = dec0[h] * ds0[h] + t0[h]
            ddec1 = jnp.sum(ds1[h] * s1[h], axis=0, keepdims=True)
            ddec0 = jnp.sum(ds0[h] * s0[h], axis=0, keepdims=True)
            dqd.append(dqd_a[h] + jnp.concatenate([dqd0[h], dqd1[h]], axis=0))
            dp_ref[:, 2 * W + HGRN_DIM * h:2 * W + HGRN_DIM * (h + 1)] = (
                di_a[h] + jnp.concatenate([di0[h], di1[h]], axis=0)).astype(BF16)
            dke.append(jnp.concatenate([dke0[h], dke1[h]], axis=0))
            gke = dke[h] * ke[h]
            dbend0 = jnp.sum(gke[:C], axis=0, keepdims=True) + ddec0 * dec0[h]
            dbend1 = jnp.sum(gke[C:], axis=0, keepdims=True) + ddec1 * dec1[h]
            dbh = dqd[h] * qd[h] - dki[h] * ki[h] - gke
            db.append(dbh + jnp.where(row == C - 1, dbend0, 0.0) + jnp.where(row == HG_T - 1, dbend1, 0.0))
        tri = jnp.where(anti, 1.0, 0.0).astype(F32)
        dlogf = [_nn(tri, db[h], HIGHEST) for h in heads]
        for h in heads:
            gt, lb, q = gts[h], lbs[h], qs[h]
            dforget = dlogf[h] / gt["forget"] - (dki[h] * gt["emb"] + dke[h] * gt["eend"])
            sg, sq = gt["sg"], gt["sq"]
            dp_ref[:, W + HGRN_DIM * h:W + HGRN_DIM * (h + 1)] = (dforget * (1.0 - lb) * sg * (1.0 - sg)).astype(BF16)
            dlb_ref[:, sls[h]] += jnp.sum(dforget * (1.0 - sg), axis=0, keepdims=True)
            dp_ref[:, sls[h]] = (dqd[h] * gt["eb"] * sq * (1.0 + q * (1.0 - sq))).astype(BF16)

        @pl.when(t == nt - 1)
        def _():
            dl0 = dlb_ref[...] * lb_all * (1.0 - lb_all)
            dlg_ref[0:1, :] = dl0
            dlg_ref[1:2, :] = -dl0

    blk = lambda j: pl.BlockSpec((HG_T, W), lambda t: (nt - 1 - t, j))
    full = pl.BlockSpec((2, W), lambda t: (0, 0))
    return pl.pallas_call(
        body, name=name, grid=(nt,),
        in_specs=[blk(0), blk(1), blk(2), full,
                  pl.BlockSpec((2, HGRN_HEADS, HGRN_DIM, HGRN_DIM), lambda t: (nt - 1 - t, 0, 0, 0)), blk(0)],
        out_specs=[pl.BlockSpec((HG_T, 3 * W), lambda t: (nt - 1 - t, 0)), full],
        out_shape=[jax.ShapeDtypeStruct((S, 3 * W), BF16), jax.ShapeDtypeStruct((2, W), F32)],
        scratch_shapes=[pltpu.VMEM((HGRN_HEADS, HGRN_DIM, HGRN_DIM), F32), pltpu.VMEM((1, W), F32)],
        compiler_params=_cparams("arbitrary"),
    )(proj, proj, proj, logits, states, drec)


def _out_proj(attn, rec, proj_h, x, g_attn, g_hgrn, g_norm2, w_out, *, name, tm=512):
    S, D = x.shape
    AW, W = ATTN_W, HGRN_W

    def body(a_ref, r_ref, hg_ref, x_ref, ga_ref, gh_ref, g2_ref, w_ref, h_ref, u_ref, m_ref):
        av = a_ref[...]
        m_ref[:, :AW] = (av * _rstd(av) * ga_ref[...]).astype(BF16)
        for h in range(HGRN_HEADS):
            sl = slice(HGRN_DIM * h, HGRN_DIM * (h + 1))
            rv, hg = r_ref[:, sl], hg_ref[:, sl]
            m_ref[:, AW + HGRN_DIM * h:AW + HGRN_DIM * (h + 1)] = (
                (rv * _rstd(rv) * gh_ref[:, sl]) * (hg * _sigmoid(hg))).astype(BF16)
        h1 = x_ref[...] + _nn(m_ref[...], w_ref[...])
        h_ref[...] = h1
        u_ref[...] = (h1 * _rstd(h1) * g2_ref[...]).astype(BF16)

    row = lambda w, j=0: pl.BlockSpec((tm, w), lambda i: (i, j))
    vec = lambda w: pl.BlockSpec((1, w), lambda i: (0, 0))
    return pl.pallas_call(
        body, name=name, grid=(S // tm,),
        in_specs=[row(AW), row(W), row(W, 3), row(D), vec(AW), vec(W), vec(D), _resident(w_out.shape)],
        out_specs=[row(D), row(D), row(AW + W)],
        out_shape=[jax.ShapeDtypeStruct((S, D), F32), jax.ShapeDtypeStruct((S, D), BF16),
                   jax.ShapeDtypeStruct((S, AW + W), BF16)],
        compiler_params=_cparams("parallel"),
    )(attn, rec, proj_h, x, g_attn, g_hgrn, g_norm2, w_out)


def _dmix_post_bwd(dh1b, w_out, attn, rec, proj_h, g_attn, g_hgrn, *, name, tm=512):
    S, D = dh1b.shape
    AW, W = ATTN_W, HGRN_W

    def body(dh_ref, w_ref, a_ref, r_ref, hg_ref, ga_ref, gh_ref, do_ref, dl_ref, dr_ref, dhg_ref, dga_ref, dgh_ref):
        first = pl.program_id(0) == 0
        dmix = _nt(dh_ref[...], w_ref[...])
        av = a_ref[...]
        dov, dga = _norm_bwd(av, ga_ref[...], dmix[:, :AW])
        do_ref[...] = dov
        shift = HEAD_DIM.bit_length() - 1
        hi = lax.shift_right_logical(lax.broadcasted_iota(jnp.int32, (AW, AW), 0), shift)
        hj = lax.shift_right_logical(lax.broadcasted_iota(jnp.int32, (AW, AW), 1), shift)
        prod = dov * av
        hi_part = prod.astype(BF16)
        lo_part = (prod - hi_part.astype(F32)).astype(BF16)
        same_head = jnp.where(hi == hj, 1.0, 0.0).astype(BF16)
        dl_ref[...] = _nn(hi_part, same_head) + _nn(lo_part, same_head)
        _accumulate(dga_ref, jnp.sum(dga, axis=0, keepdims=True), first)

        @pl.when(first)
        def _():
            dgh_ref[...] = jnp.zeros_like(dgh_ref)

        for h in range(HGRN_HEADS):
            sl = slice(HGRN_DIM * h, HGRN_DIM * (h + 1))
            rv, hg, gv = r_ref[:, sl], hg_ref[:, sl], gh_ref[:, sl]
            dout = dmix[:, AW + HGRN_DIM * h:AW + HGRN_DIM * (h + 1)]
            sg = _sigmoid(hg)
            drv, dgh = _norm_bwd(rv, gv, dout * (hg * sg))
            dr_ref[:, sl] = drv
            dgh_ref[:, sl] += jnp.sum(dgh, axis=0, keepdims=True)
            dhg_ref[:, sl] = (dout * (rv * _rstd(rv) * gv) * (sg * (1.0 + hg * (1.0 - sg)))).astype(BF16)

    row = lambda w, j=0: pl.BlockSpec((tm, w), lambda i: (i, j))
    vec = lambda w: pl.BlockSpec((1, w), lambda i: (0, 0))
    return pl.pallas_call(
        body, name=name, grid=(S // tm,),
        in_specs=[row(D), _resident(w_out.shape), row(AW), row(W), row(W, 3), vec(AW), vec(W)],
        out_specs=[row(AW), row(AW), row(W), row(W), vec(AW), vec(W)],
        out_shape=[jax.ShapeDtypeStruct((S, AW), F32), jax.ShapeDtypeStruct((S, AW), F32),
                   jax.ShapeDtypeStruct((S, W), F32), jax.ShapeDtypeStruct((S, W), BF16),
                   jax.ShapeDtypeStruct((1, AW), F32), jax.ShapeDtypeStruct((1, W), F32)],
        compiler_params=_cparams("arbitrary"),
    )(dh1b, w_out, attn, rec, proj_h, g_attn, g_hgrn)


def _conv_act(g, g1, g2, w_ref, b_ref):
    c = b_ref[...] + w_ref[0:1, :] * g2 + w_ref[1:2, :] * g1 + w_ref[2:3, :] * g
    return c, 0.5 * (1.0 + lax.erf(c * (2.0 ** -0.5)))


def _shift_down(g, halo, row):
    g1 = jnp.where(row == 0, halo[7:8], pltpu.roll(g, 1, 0))
    g2 = jnp.where(row == 0, halo[6:7], jnp.where(row == 1, halo[7:8], pltpu.roll(g, 2, 0)))
    return g1, g2


def _shift_up(x, halo, row):
    n = x.shape[0]
    x1 = jnp.where(row == n - 1, halo[0:1], pltpu.roll(x, n - 1, 0))
    x2 = jnp.where(row == n - 2, halo[0:1], jnp.where(row == n - 1, halo[1:2], pltpu.roll(x, n - 2, 0)))
    return x1, x2


def _up_glu(u, wt_up, conv_w, conv_b, *, name, tm=1024, tn=256):
    S, D = u.shape
    F = wt_up.shape[0] // 2
    nf = F // tn

    def body(u_ref, wg_ref, wv_ref, cw_ref, cb_ref, g_ref, v_ref, a_ref, halo_ref):
        i, j = pl.program_id(0), pl.program_id(1)

        @pl.when(i == 0)
        def _():
            halo_ref[j] = jnp.zeros((SUBLANES, tn), F32)

        uv = u_ref[...]
        g, v = _nt(uv, wg_ref[...]), _nt(uv, wv_ref[...])
        row = lax.broadcasted_iota(jnp.int32, (tm, tn), 0)
        g1, g2 = _shift_down(g, halo_ref[j], row)
        c, cdf = _conv_act(g, g1, g2, cw_ref, cb_ref)
        a_ref[...] = (c * cdf * v).astype(BF16)
        g_ref[...] = g.astype(BF16)
        v_ref[...] = v.astype(BF16)
        halo_ref[j] = g[tm - SUBLANES:, :]

    col = pl.BlockSpec((tm, tn), lambda i, j: (i, j))
    out = jax.ShapeDtypeStruct((S, F), BF16)
    return pl.pallas_call(
        body, name=name, grid=(S // tm, nf),
        in_specs=[pl.BlockSpec((tm, D), lambda i, j: (i, 0)), pl.BlockSpec((tn, D), lambda i, j: (j, 0)),
                  pl.BlockSpec((tn, D), lambda i, j: (j + nf, 0)), pl.BlockSpec((3, tn), lambda i, j: (0, j)),
                  pl.BlockSpec((1, tn), lambda i, j: (0, j))],
        out_specs=[col, col, col], out_shape=[out, out, out],
        scratch_shapes=[pltpu.VMEM((nf, SUBLANES, tn), F32)], compiler_params=_cparams("arbitrary", "arbitrary"),
    )(u, wt_up, wt_up, conv_w, conv_b)


def _dact_glu_bwd(dh2b, w_down, gate, val, conv_w, conv_b, *, name, tm=1024, tn=256):
    S, D = dh2b.shape
    F = gate.shape[1]
    nf, ni = F // tn, S // tm
    hb = tm // SUBLANES

    def body(dh_ref, wd_ref, g_ref, gh_ref, v_ref, cw_ref, cb_ref, dg_ref, dv_ref, dcw_ref, dcb_ref, halo_ref, acc_ref):
        i, j = pl.program_id(0), pl.program_id(1)

        @pl.when(i == 0)
        def _():
            halo_ref[j] = jnp.zeros((SUBLANES, tn), F32)
            acc_ref[j] = jnp.zeros((SUBLANES, tn), F32)

        g = g_ref[...].astype(F32)
        before = jnp.where(i < ni - 1, gh_ref[...].astype(F32), 0.0)
        row = lax.broadcasted_iota(jnp.int32, (tm, tn), 0)
        g1, g2 = _shift_down(g, before[SUBLANES:], row)
        c, cdf = _conv_act(g, g1, g2, cw_ref, cb_ref)
        da = _nt(dh_ref[...], wd_ref[...])
        dv_ref[...] = (da * (c * cdf)).astype(BF16)
        pdf = jnp.exp(-0.5 * c * c) * (1.0 / (2.0 * jnp.pi) ** 0.5)
        dc = da * v_ref[...].astype(F32) * (cdf + c * pdf)
        d1, d2 = _shift_up(dc, halo_ref[j], row)
        dg_ref[...] = (cw_ref[2:3, :] * dc + cw_ref[1:2, :] * d1 + cw_ref[0:1, :] * d2).astype(BF16)
        halo_ref[j] = dc[:SUBLANES, :]
        for k, t in enumerate((dc * g2, dc * g1, dc * g, dc)):
            acc_ref[j, k:k + 1, :] += jnp.sum(t, axis=0, keepdims=True)

        @pl.when((i == ni - 1) & (j == nf - 1))
        def _():
            for jj in range(nf):
                dcw_ref[:, jj * tn:(jj + 1) * tn] = acc_ref[jj, 0:3, :]
                dcb_ref[:, jj * tn:(jj + 1) * tn] = acc_ref[jj, 3:4, :]

    tile = pl.BlockSpec((tm, tn), lambda i, j: (ni - 1 - i, j))
    return pl.pallas_call(
        body, name=name, grid=(ni, nf),
        in_specs=[pl.BlockSpec((tm, D), lambda i, j: (ni - 1 - i, 0)), pl.BlockSpec((tn, D), lambda i, j: (j, 0)),
                  tile, pl.BlockSpec((SUBLANES * 2, tn), lambda i, j: (jnp.maximum((ni - 1 - i) * (hb // 2) - 1, 0), j)),
                  tile, pl.BlockSpec((3, tn), lambda i, j: (0, j)), pl.BlockSpec((1, tn), lambda i, j: (0, j))],
        out_specs=[tile, tile, pl.BlockSpec((3, F), lambda i, j: (0, 0)), pl.BlockSpec((1, F), lambda i, j: (0, 0))],
        out_shape=[jax.ShapeDtypeStruct((S, F), BF16), jax.ShapeDtypeStruct((S, F), BF16),
                   jax.ShapeDtypeStruct((3, F), F32), jax.ShapeDtypeStruct((1, F), F32)],
        scratch_shapes=[pltpu.VMEM((nf, SUBLANES, tn), F32), pltpu.VMEM((nf, SUBLANES, tn), F32)],
        compiler_params=_cparams("arbitrary", "arbitrary"),
    )(dh2b, w_down, gate, gate, val, conv_w, conv_b)


def _down_loss(act, w_down, h1, g, target, *, name, tm=512):
    S, F = act.shape
    D = h1.shape[1]

    def body(a_ref, w_ref, h_ref, g_ref, t_ref, dh_ref, dhb_ref, dg_ref, loss_ref):
        first = pl.program_id(0) == 0
        h2 = h_ref[...] + _nn(a_ref[...], w_ref[...])
        gv = g_ref[...]
        r = _rstd(h2)
        xh = h2 * r
        err = xh * gv - t_ref[...]
        part_loss = 0.5 * jnp.sum(jnp.mean(err * err, axis=-1, keepdims=True), axis=0, keepdims=True)
        dy = err * (1.0 / D)
        dxh = dy * gv
        dh = r * (dxh - xh * jnp.mean(dxh * xh, axis=-1, keepdims=True))
        dh_ref[...] = dh
        dhb_ref[...] = dh.astype(BF16)
        _accumulate(dg_ref, jnp.sum(dy * xh, axis=0, keepdims=True), first)
        _accumulate(loss_ref, jnp.broadcast_to(part_loss, (1, LANES)), first)

    row = lambda w: pl.BlockSpec((tm, w), lambda i: (i, 0))
    vec = lambda w: pl.BlockSpec((1, w), lambda i: (0, 0))
    return pl.pallas_call(
        body, name=name, grid=(S // tm,), in_specs=[row(F), _resident(w_down.shape), row(D), vec(D), row(D)],
        out_specs=[row(D), row(D), vec(D), vec(LANES)],
        out_shape=[jax.ShapeDtypeStruct((S, D), F32), jax.ShapeDtypeStruct((S, D), BF16),
                   jax.ShapeDtypeStruct((1, D), F32), jax.ShapeDtypeStruct((1, LANES), F32)],
        compiler_params=_cparams("arbitrary"),
    )(act, w_down, h1, g, target)


def _grad_norm_input(pieces, ws, x, g, add, *, name, tm=512):
    S, D = x.shape
    widths = [p.shape[1] for p in pieces]
    n, nw = len(pieces), len(ws)
    where, wi, off = [], 0, ws[0][1]
    for wd in widths:
        if off == ws[wi][0].shape[0]:
            wi, off = wi + 1, ws[wi + 1][1]
        where.append((wi, off))
        off += wd
    ws = [w for w, _ in ws]

    def body(*refs):
        p_refs, w_refs = refs[:n], refs[n:n + nw]
        x_ref, g_ref, add_ref, dx_ref, dxb_ref, dg_ref = refs[n + nw:]
        halves = _row_halves(tm)
        du = []
        for rows in halves:
            terms = [_nn(p_refs[k][rows, :], w_refs[wi][off:off + widths[k], :]) for k, (wi, off) in enumerate(where)]
            du.append(sum(terms[1:], terms[0]))
        dg_sum = None
        for rows, duh in zip(halves, du):
            dx, dg = _norm_bwd(x_ref[rows, :], g_ref[...], duh)
            dx = add_ref[rows, :] + dx
            dx_ref[rows, :] = dx
            dxb_ref[rows, :] = dx.astype(BF16)
            part = jnp.sum(dg, axis=0, keepdims=True)
            dg_sum = part if dg_sum is None else dg_sum + part
        _accumulate(dg_ref, dg_sum, pl.program_id(0) == 0)

    row = lambda w_: pl.BlockSpec((tm, w_), lambda i: (i, 0))
    vec = pl.BlockSpec((1, D), lambda i: (0, 0))
    return pl.pallas_call(
        body, name=name, grid=(S // tm,),
        in_specs=[row(wd) for wd in widths] + [_resident(w.shape) for w in ws] + [row(D), vec, row(D)],
        out_specs=[row(D), row(D), vec],
        out_shape=[jax.ShapeDtypeStruct((S, D), F32), jax.ShapeDtypeStruct((S, D), BF16),
                   jax.ShapeDtypeStruct((1, D), F32)],
        compiler_params=_cparams("arbitrary"),
    )(*pieces, *ws, x, g, add)


def _rows(a):
    return a.reshape(-1, a.shape[-1])


def _row_tile(rows, cols, itemsize=4, budget=1 << 20):
    t = rows
    while t % 32 == 0 and t * cols * itemsize > budget:
        t //= 2
    return t


def _sum_cast(arrs, out_dtype, *, name):
    shape = arrs[0].shape
    flat = [_rows(a) for a in arrs]
    R, C = flat[0].shape
    tr = _row_tile(R, C)

    def body(*refs):
        acc = refs[0][...].astype(F32)
        for r in refs[1:-1]:
            acc = acc + r[...].astype(F32)
        refs[-1][...] = acc.astype(out_dtype)

    spec = pl.BlockSpec((tr, C), lambda i: (i, 0))
    return pl.pallas_call(
        body, name=name, grid=(R // tr,), in_specs=[spec] * len(flat), out_specs=spec,
        out_shape=jax.ShapeDtypeStruct((R, C), out_dtype), compiler_params=_cparams("parallel"),
    )(*flat).reshape(shape)


def _adamw(parts, w, m, v, *, name):
    shape = w.shape
    w2, m2, v2 = _rows(w), _rows(m), _rows(v)
    R, C = w2.shape
    parts = [p.reshape(-1, R, C) for p in parts]
    tr = _row_tile(R, C)
    np_ = len(parts)
    c1, c2 = 1.0 - ADAM_B1 ** ADAM_STEP, 1.0 - ADAM_B2 ** ADAM_STEP

    def body(*refs):
        terms = [(r, k) for r in refs[:np_] for k in range(r.shape[0])]
        g = terms[0][0][terms[0][1]].astype(F32)
        for r, k in terms[1:]:
            g = g + r[k].astype(F32)
        w_ref, m_ref, v_ref, g_out, d_out, m_out, v_out = refs[np_:]
        mn = ADAM_B1 * m_ref[...] + (1.0 - ADAM_B1) * g
        vn = ADAM_B2 * v_ref[...] + (1.0 - ADAM_B2) * (g * g)
        g_out[...] = g
        d_out[...] = -ADAM_LR * ((mn / c1) / (jnp.sqrt(vn / c2) + ADAM_EPS) + ADAM_WD * w_ref[...])
        m_out[...] = mn
        v_out[...] = vn

    spec = pl.BlockSpec((tr, C), lambda i: (i, 0))
    out = jax.ShapeDtypeStruct((R, C), F32)
    stacks = [pl.BlockSpec((p.shape[0], tr, C), lambda i: (0, i, 0)) for p in parts]
    res = pl.pallas_call(
        body, name=name, grid=(R // tr,), in_specs=stacks + [spec] * 3, out_specs=[spec] * 4,
        out_shape=[out] * 4, compiler_params=_cparams("parallel"),
    )(*parts, w2, m2, v2)
    return [r.reshape(shape) for r in res]


def _adamw_packed(stack, widths, params, *, name):
    c1, c2 = 1.0 - ADAM_B1 ** ADAM_STEP, 1.0 - ADAM_B2 ** ADAM_STEP
    k = stack.shape[0]
    flat = [None if p is None else [_rows(a) for a in p] for p in params]
    n_in = sum(3 for p in flat if p is not None)

    def body(*refs):
        s_ref, ins, outs = refs[0], list(refs[1:1 + n_in]), list(refs[1 + n_in:])
        off = 0
        for width, p in zip(widths, flat):
            rows = 1 if p is None else p[0].shape[0]
            cols = width // rows
            w_ref, m_ref, v_ref = (None, None, None) if p is None else (ins.pop(0), ins.pop(0), ins.pop(0))
            o_refs = [outs.pop(0) for _ in range(1 if p is None else 4)]
            for r in range(rows):
                seg = slice(off + r * cols, off + (r + 1) * cols)
                g = s_ref[0, :, seg]
                for j in range(1, k):
                    g = g + s_ref[j, :, seg]
                o_refs[0][r:r + 1, :] = g
                if p is not None:
                    row = slice(r, r + 1)
                    mn = ADAM_B1 * m_ref[row, :] + (1.0 - ADAM_B1) * g
                    vn = ADAM_B2 * v_ref[row, :] + (1.0 - ADAM_B2) * (g * g)
                    o_refs[1][row, :] = -ADAM_LR * ((mn / c1) / (jnp.sqrt(vn / c2) + ADAM_EPS) + ADAM_WD * w_ref[row, :])
                    o_refs[2][row, :] = mn
                    o_refs[3][row, :] = vn
            off += width

    operands, out_shape = [stack], []
    for width, p in zip(widths, flat):
        if p is None:
            out_shape.append(jax.ShapeDtypeStruct((1, width), F32))
        else:
            operands += p
            out_shape += [jax.ShapeDtypeStruct(p[0].shape, F32)] * 4
    res = list(pl.pallas_call(body, name=name, out_shape=out_shape)(*operands))
    out = []
    for p, orig in zip(flat, params):
        n = 1 if p is None else 4
        out.append([r if orig is None else r.reshape(orig[0].shape) for r in res[:n]])
        res = res[n:]
    return out


def _coords():
    return lax.axis_index("x"), lax.axis_index("y"), lax.axis_index("c")


def _all_gather(shards, *, name):
    n = len(shards)

    def body(*refs):
        x_refs, out_refs = refs[:n], refs[n:2 * n]
        send_sems, recv_sems, local_sems = refs[2 * n:]
        x, y, c = _coords()
        me, sibling = (x, y, c), (x, y, 1 - c)
        chips = [(1 - x, y), (x, 1 - y), (1 - x, 1 - y)]

        def slot(a, dev):
            return out_refs[a].at[4 * dev[0] + 2 * dev[1] + dev[2]]

        def copy(a, k, block, to, src=None):
            return pltpu.make_async_remote_copy(
                src_ref=slot(a, block) if src is None else src, dst_ref=slot(a, block),
                send_sem=send_sems.at[7 * a + k], recv_sem=recv_sems.at[7 * a + k], device_id=to, device_id_type=MESH)

        mine = [pltpu.make_async_copy(x_refs[a], slot(a, me), local_sems.at[a]) for a in range(n)]
        for cp in mine:
            cp.start()
        first = []
        for a in range(n):
            first.append(copy(a, 0, me, sibling, src=x_refs[a]))
            first += [copy(a, 1 + j, me, (*chip, c), src=x_refs[a]) for j, chip in enumerate(chips)]
        for cp in first:
            cp.start()
        passed = []
        for j, chip in enumerate(chips):
            for a in range(n):
                copy(a, 1 + j, (*chip, c), me).wait_recv()
                fwd = copy(a, 4 + j, (*chip, c), sibling)
                fwd.start()
                passed.append(fwd)
        for a in range(n):
            copy(a, 0, sibling, me).wait_recv()
            for j, chip in enumerate(chips):
                copy(a, 4 + j, (*chip, 1 - c), me).wait_recv()
        for cp in first + passed:
            cp.wait_send()
        for cp in mine:
            cp.wait()

    return pl.pallas_call(
        body, name=name, in_specs=[HBM] * n, out_specs=[HBM] * n,
        out_shape=[jax.ShapeDtypeStruct((N_DEV, *s.shape), s.dtype) for s in shards],
        scratch_shapes=[pltpu.SemaphoreType.DMA((7 * n,)), pltpu.SemaphoreType.DMA((7 * n,)),
                        pltpu.SemaphoreType.DMA((n,))],
    )(*shards)


def _flip_y(x, y, c):
    return (x, 1 - y, c)


def _flip_x(x, y, c):
    return (1 - x, y, c)


def _flip_xy(x, y, c):
    return (1 - x, 1 - y, c)


SEM = pl.BlockSpec(memory_space=pltpu.SEMAPHORE)
SIDE_EFFECT = pltpu.SideEffectType.DATAFLOW_SIDE_EFFECTING


def _in_hbm(a):
    return pltpu.with_memory_space_constraint(a, pltpu.HBM)


def _copies_start(srcs, lands, plan, n_copies, *, name):
    ns, nl = len(srcs), len(lands)

    def body(*refs):
        src_refs, land_refs = refs[:ns], refs[ns:ns + nl]
        send_sems, recv_sems = refs[ns + nl:ns + nl + 2]
        token = refs[-1]
        for k, (src, dst, peer, _) in enumerate(plan(src_refs, land_refs, *_coords())):
            pltpu.make_async_remote_copy(src_ref=src, dst_ref=dst, send_sem=send_sems.at[k], recv_sem=recv_sems.at[k],
                                         device_id=peer, device_id_type=MESH).start()
        token[...] = jnp.zeros_like(token)

    bufs = [*srcs, *lands]
    res = pl.pallas_call(
        body, name=name, in_specs=[HBM] * (ns + nl),
        out_specs=(SEM, SEM, *[HBM] * (ns + nl), pl.BlockSpec(memory_space=pltpu.VMEM)),
        out_shape=(pltpu.SemaphoreType.DMA((n_copies,)), pltpu.SemaphoreType.DMA((n_copies,)),
                   *[pltpu.HBM(b.shape, b.dtype) for b in bufs], jax.ShapeDtypeStruct((SUBLANES, LANES), F32)),
        input_output_aliases={i: 2 + i for i in range(ns + nl)},
        compiler_params=pltpu.CompilerParams(has_side_effects=SIDE_EFFECT),
    )(*[_in_hbm(b) for b in bufs])
    return res[0], res[1], list(res[2:2 + ns]), list(res[2 + ns:2 + ns + nl]), res[-1]


def _copies_wait(started, plan, after, *, name):
    send_sems, recv_sems, srcs, lands, _ = started
    ns, nl = len(srcs), len(lands)

    def body(*refs):
        src_refs, land_refs = refs[:ns], refs[ns:ns + nl]
        send_sems, recv_sems = refs[ns + nl:ns + nl + 2]
        for k, (src, dst, peer, here) in enumerate(plan(src_refs, land_refs, *_coords())):
            pltpu.make_async_remote_copy(src_ref=src, dst_ref=dst, send_sem=send_sems.at[k], recv_sem=recv_sems.at[k],
                                         device_id=peer, device_id_type=MESH).wait_send()
            pltpu.make_async_remote_copy(src_ref=src, dst_ref=here, send_sem=send_sems.at[k], recv_sem=recv_sems.at[k],
                                         device_id=peer, device_id_type=MESH).wait_recv()

    bufs = [*srcs, *lands]
    res = pl.pallas_call(
        body, name=name, in_specs=[HBM] * (ns + nl) + [SEM, SEM, pl.BlockSpec(memory_space=pl.ANY)],
        out_specs=[HBM] * (ns + nl), out_shape=[pltpu.HBM(b.shape, b.dtype) for b in bufs],
        input_output_aliases={i: i for i in range(ns + nl)},
        compiler_params=pltpu.CompilerParams(has_side_effects=SIDE_EFFECT),
    )(*bufs, send_sems, recv_sems, after)
    return list(res[ns:])


def _dev_index(dev):
    return 4 * dev[0] + 2 * dev[1] + dev[2]


def _ag_chips_plan(src_refs, land_refs, x, y, c):
    me = _dev_index((x, y, c))
    return [(src, land.at[me], peer, land.at[_dev_index(peer)])
            for src, land in zip(src_refs, land_refs) for peer in (_flip_y(x, y, c), _flip_x(x, y, c), _flip_xy(x, y, c))]


def _ag_sibling_plan(src_refs, land_refs, x, y, c):
    chips = [(x, y), (x, 1 - y), (1 - x, y), (1 - x, 1 - y)]
    return [(land.at[_dev_index((*chip, c))], land.at[_dev_index((*chip, c))], (x, y, 1 - c),
             land.at[_dev_index((*chip, 1 - c))]) for land in land_refs for chip in chips]


def _ag_direct_plan(src_refs, land_refs, x, y, c):
    me = _dev_index((x, y, c))
    plan = []
    for src, land in zip(src_refs, land_refs):
        for m in range(1, N_DEV):
            peer = (x + (m >> 2) * (1 - 2 * x), y + ((m >> 1) & 1) * (1 - 2 * y), c + (m & 1) * (1 - 2 * c))
            plan.append((src, land.at[me], peer, land.at[_dev_index(peer)]))
    return plan


def _rs_direct_plan(src_refs, land_refs, x, y, c):
    plan = []
    for src, land in zip(src_refs, land_refs):
        for m in range(1, N_DEV):
            peer = (x + (m >> 2) * (1 - 2 * x), y + ((m >> 1) & 1) * (1 - 2 * y), c + (m & 1) * (1 - 2 * c))
            plan.append((src.at[_dev_index(peer)], land.at[m - 1], peer, land.at[m - 1]))
    return plan


def _rs_start(grads, me, *, name):
    own = [lax.dynamic_index_in_dim(g, me, 0, keepdims=False) for g in grads]
    lands = [lax.empty((N_DEV - 1, *g.shape[1:]), g.dtype) for g in grads]
    return _copies_start(grads, lands, _rs_direct_plan, (N_DEV - 1) * len(grads), name=name), own


def _rs_finish(started, after, *, name):
    handle, own = started
    got = _copies_wait(handle, _rs_direct_plan, after, name=name)
    return [[o, land] for o, land in zip(own, got)]


def _gathered_cols(w8):
    return w8.transpose(1, 0, 2).reshape(w8.shape[1], -1)


def _pair_major(wt, inverse=False):
    a, b = (ATTN_W // LANES, 3) if inverse else (3, ATTN_W // LANES)
    return wt.reshape(a, b, LANES, -1).transpose(1, 0, 2, 3).reshape(3 * ATTN_W, -1)


def kernel(x, norm1_g, w_in, attn_norm_g, hgrn_norm_g, hgrn_lb_logits, w_out, norm2_g, w_up, conv_w, conv_b, w_down, final_norm_g, loss_target, m_norm1_g, m_w_in, m_attn_norm_g, m_hgrn_norm_g, m_hgrn_lb_logits, m_w_out, m_norm2_g, m_w_up, m_conv_w, m_conv_b, m_w_down, m_final_norm_g, v_norm1_g, v_w_in, v_attn_norm_g, v_hgrn_norm_g, v_hgrn_lb_logits, v_w_out, v_norm2_g, v_w_up, v_conv_w, v_conv_b, v_w_down, v_final_norm_g):
    xs, target = x[0], loss_target[0]
    S, D = xs.shape
    NA = 3 * ATTN_W
    fng = final_norm_g.reshape(1, D)

    t = lambda a: a[0].T
    casts = [_sum_cast([w], BF16, name=f"cast_{nm}") for nm, w in
             (("w_in", t(w_in)), ("w_out", w_out[0]), ("w_up", t(w_up)), ("w_down", w_down[0]))]
    me = _dev_index(_coords())
    (g_in,) = _all_gather(casts[:1], name="ag_w_in")
    later, _ = lax.optimization_barrier((casts[1:] + [conv_w[0]], g_in))
    ag1 = _copies_start(later, [lax.empty((N_DEV, *s.shape), s.dtype) for s in later], _ag_chips_plan,
                        3 * len(later), name="ag_chips_start")
    wi = g_in.reshape(-1, D)
    wi_a = _pair_major(wi[:NA])

    u1, proj_a = _proj_attn(xs, norm1_g + ag1[4][0, 0], wi_a, name="proj_attn")
    proj_h = _mm_nt(u1, wi, NA, wi.shape[0] - NA, name="proj_hgrn")
    attn, lse = _attn_fwd(proj_a, name="attn_fwd")
    lands = _copies_wait(ag1, _ag_chips_plan, attn, name="ag_chips_wait")
    lands = [lax.dynamic_update_index_in_dim(l, s, me, 0) for l, s in zip(lands, later)]
    ag2 = _copies_start([], lands, _ag_sibling_plan, 4 * len(later), name="ag_sibling_start")
    rec, states = _hgrn_fwd(proj_h, hgrn_lb_logits + ag2[4][0, 0], name="hgrn_fwd")
    g_out, g_up, g_down, g_cw = _copies_wait(ag2, _ag_sibling_plan, rec, name="ag_sibling_wait")
    wo = g_out.reshape(-1, D)
    wu = g_up.reshape(-1, D)
    wd = g_down.reshape(-1, D)
    cw = _gathered_cols(g_cw)
    h1, u2, mixed = _out_proj(attn, rec, proj_h, xs, attn_norm_g, hgrn_norm_g, norm2_g, wo, name="out_proj")
    gate, val, act = _up_glu(u2, wu, cw, conv_b, name="up_glu")
    dh2, dh2b, d_fng, loss_part = _down_loss(act, wd, h1, fng, target, name="down_loss")

    dgate, dval, d_cw, d_cb = _dact_glu_bwd(dh2b, wd, gate, val, cw, conv_b, name="dact_glu_bwd")
    dw_down = _mm_tn(act, dh2b, tm=256, name="dw_down")
    dh1, dh1b, d_n2g = _grad_norm_input([dgate, dval], [(wu, 0)], h1, norm2_g, dh2, name="du2_norm2_bwd")
    dw_up = jnp.concatenate([_mm_tn(dy, u2, tm=256, name=f"dw_up_{nm}") for nm, dy in (("gate", dgate), ("val", dval))])
    rs_ffn = _rs_start([dw_down.reshape(N_DEV, -1, D), dw_up.reshape(N_DEV, -1, D)], me, name="rs_ffn_start")
    dattn, delta, drec, dhg, d_ang, d_hng = _dmix_post_bwd(dh1b, wo, attn, rec, proj_h, attn_norm_g + rs_ffn[0][4][0, 0],
                                                          hgrn_norm_g, name="dmix_post_bwd")
    dw_out = _mm_tn(mixed, dh1b, name="dw_out")
    rs_out = _rs_start([dw_out.reshape(N_DEV, -1, D)], me, name="rs_out_start")
    dproj_h, d_lbl = _hgrn_bwd(proj_h, hgrn_lb_logits + rs_out[0][4][0, 0], states, drec, name="hgrn_bwd")
    small = [("loss", loss_part, None, None, None),
             ("attn_norm_g", d_ang, attn_norm_g, m_attn_norm_g, v_attn_norm_g),
             ("hgrn_norm_g", d_hng, hgrn_norm_g, m_hgrn_norm_g, v_hgrn_norm_g),
             ("hgrn_lb_logits", d_lbl, hgrn_lb_logits, m_hgrn_lb_logits, v_hgrn_lb_logits),
             ("norm2_g", d_n2g, norm2_g, m_norm2_g, v_norm2_g),
             ("conv_b", d_cb, conv_b, m_conv_b, v_conv_b),
             ("final_norm_g", d_fng, final_norm_g, m_final_norm_g, v_final_norm_g)]
    pack = lambda arrs: jnp.concatenate([a.reshape(1, -1) for a in arrs], axis=1)
    small_own = [pack([s[1] for s in small]), d_cw]
    ag_small = _copies_start(small_own, [lax.empty((N_DEV, *s.shape), s.dtype) for s in small_own], _ag_direct_plan,
                             (N_DEV - 1) * len(small_own), name="ag_small_start")
    dproj_a = _attn_bwd(proj_a, dattn, lse, delta, name="attn_bwd")
    dw_in = jnp.concatenate([_pair_major(_mm_tn(dproj_a, u1, tm=256, name="dw_in_attn"), inverse=True),
                             _mm_tn(dproj_h, u1, tm=256, name="dw_in_hgrn"),
                             _mm_tn(dhg, u1, tm=256, name="dw_in_gate")])
    rs_in = _rs_start([dw_in.reshape(N_DEV, -1, D)], me, name="rs_in_start")
    grad_x, _, d_n1g = _grad_norm_input([dproj_a, dproj_h, dhg], [(wi_a, 0), (wi, NA)], xs,
                                        norm1_g + rs_in[0][4][0, 0], dh1, name="du1_norm1_bwd")

    res = {}

    def update(nm, parts, w, m, v, transposed=False):
        if transposed:
            res[nm] = [r.T[None] for r in _adamw(parts, t(w), t(m), t(v), name=f"adamw_{nm}")]
        else:
            res[nm] = _adamw(parts, w, m, v, name=f"adamw_{nm}")

    g_down, g_up = _rs_finish(rs_ffn, grad_x, name="rs_ffn_wait")
    update("w_down", g_down, w_down, m_w_down, v_w_down)
    update("w_up", g_up, w_up, m_w_up, v_w_up, transposed=True)
    (g_out,) = _rs_finish(rs_out, grad_x, name="rs_out_wait")
    update("w_out", g_out, w_out, m_w_out, v_w_out)
    (g_in,) = _rs_finish(rs_in, res["w_up"][1], name="rs_in_wait")
    update("w_in", g_in, w_in, m_w_in, v_w_in, transposed=True)

    g_small, g_dcw = [lax.dynamic_update_index_in_dim(l, s, me, 0)
                      for l, s in zip(_copies_wait(ag_small, _ag_direct_plan, grad_x, name="ag_small_wait"), small_own)]
    sm = _adamw_packed(g_small, [s[1].size for s in small], [None if s[2] is None else s[2:] for s in small],
                       name="adamw_small")
    for (nm, *_), r in zip(small, sm):
        res[nm] = r
    ncw = conv_w.shape[-1]
    mine_cw = lax.dynamic_slice_in_dim(g_dcw, me * ncw, ncw, axis=2)
    res["conv_w"] = _adamw([mine_cw], conv_w, m_conv_w, v_conv_w, name="adamw_conv_w")
    late, _ = lax.optimization_barrier((d_n1g, res["w_in"][1]))
    update("norm1_g", _all_gather([late], name="ag_norm1_grad"), norm1_g, m_norm1_g, v_norm1_g)

    loss = res["loss"][0][0, 0]
    order = ["norm1_g", "w_in", "attn_norm_g", "hgrn_norm_g", "hgrn_lb_logits", "w_out", "norm2_g", "w_up",
             "conv_w", "conv_b", "w_down", "final_norm_g"]
    return (loss, grad_x[None], *[res[nm][0] for nm in order], *[res[nm][1] for nm in order],
            *[res[nm][2] for nm in order], *[res[nm][3] for nm in order])
```

```python
import jax
import jax.numpy as jnp
from jax import lax
from jax.experimental import pallas as pl
from jax.experimental.pallas import tpu as pltpu

F32, BF16 = jnp.float32, jnp.bfloat16
NORM_EPS = 1e-6
ATTN_HEADS, HEAD_DIM, ATTN_BLOCK = 8, 64, 128
DILATIONS = (1, 4, 16)
ATTN_SCALE = HEAD_DIM ** -0.5
ATTN_W = ATTN_HEADS * HEAD_DIM
HGRN_HEADS, HGRN_DIM, HGRN_CHUNK = 4, 128, 64
HGRN_W = HGRN_HEADS * HGRN_DIM
ADAM_LR, ADAM_B1, ADAM_B2, ADAM_EPS, ADAM_WD, ADAM_STEP = 0.001, 0.9, 0.999, 1e-08, 0.01, 10
LANES, SUBLANES = 128, 8
VMEM_LIMIT_BYTES = 56 * 1024 * 1024
N_DEV = 8
MESH = pl.DeviceIdType.MESH
HBM = pl.BlockSpec(memory_space=pltpu.HBM)
HIGHEST = lax.Precision.HIGHEST


def _cparams(*sem):
    return pltpu.CompilerParams(dimension_semantics=sem, vmem_limit_bytes=VMEM_LIMIT_BYTES)


def _tile(n, pref):
    if n <= pref:
        return n
    t = (pref // LANES) * LANES
    while n % t:
        t -= LANES
    return t


def _resident(shape):
    return pl.BlockSpec(shape, lambda *_: (0,) * len(shape), pipeline_mode=pl.Buffered(1))


def _dot(a, b, dims, precision=None):
    return lax.dot_general(a, b, (dims, ((), ())), precision=precision, preferred_element_type=F32)


def _nn(a, b, precision=None):
    return _dot(a, b, ((1,), (0,)), precision)


def _nt(a, b):
    return _dot(a, b, ((1,), (1,)))


def _tn(a, b):
    return _dot(a, b, ((0,), (0,)))


def _sigmoid(x):
    return 1.0 / (1.0 + jnp.exp(-x))


def _rstd(x):
    return lax.rsqrt(jnp.mean(x * x, axis=-1, keepdims=True) + NORM_EPS)


def _norm_bwd(x, g, du):
    r = _rstd(x)
    xh = x * r
    dxh = du * g
    return r * (dxh - xh * jnp.mean(dxh * xh, axis=-1, keepdims=True)), du * xh


def _row_halves(tm):
    return [pl.ds(0, tm // 2), pl.ds(tm // 2, tm // 2)]


def _accumulate(ref, part, first):
    @pl.when(first)
    def _():
        ref[...] = part

    @pl.when(jnp.logical_not(first))
    def _():
        ref[...] += part


def _mm_nt(a, bt, row0, n, *, name, out_dtype=F32, tm=1024, tn=512):
    M, K = a.shape
    tm, tn = _tile(M, tm), _tile(n, tn)
    j0 = row0 // tn

    def body(a_ref, b_ref, o_ref):
        o_ref[...] = _nt(a_ref[...], b_ref[...]).astype(out_dtype)

    return pl.pallas_call(
        body, name=name, grid=(M // tm, n // tn),
        in_specs=[pl.BlockSpec((tm, K), lambda i, j: (i, 0)), pl.BlockSpec((tn, K), lambda i, j: (j + j0, 0))],
        out_specs=pl.BlockSpec((tm, tn), lambda i, j: (i, j)), out_shape=jax.ShapeDtypeStruct((M, n), out_dtype),
        compiler_params=_cparams("parallel", "parallel"),
    )(a, bt)


def _mm_tn(x, dy, *, name, tm=512, tn=1024, rows=None, row_block=None, into=None):
    S, M = x.shape
    N = dy.shape[1]
    tm, tn = _tile(M, tm), _tile(N, tn)
    row_block = row_block or (lambda i: i)

    def body(x_ref, dy_ref, *rest):
        o_ref, xt_ref = rest[-2:]

        @pl.when(pl.program_id(1) == 0)
        def _():
            xt_ref[...] = x_ref[...].T

        o_ref[...] = _nn(xt_ref[...], dy_ref[...]).astype(BF16)

    operands = [x, dy] + ([] if into is None else [into])
    return pl.pallas_call(
        body, name=name, grid=(M // tm, N // tn),
        in_specs=[pl.BlockSpec((S, tm), lambda i, j: (0, i)), pl.BlockSpec((S, tn), lambda i, j: (0, j))]
        + ([] if into is None else [pl.BlockSpec(memory_space=pl.ANY)]),
        out_specs=pl.BlockSpec((tm, tn), lambda i, j: (row_block(i), j)),
        out_shape=jax.ShapeDtypeStruct((rows or M, N), BF16),
        input_output_aliases={} if into is None else {2: 0},
        scratch_shapes=[pltpu.VMEM((tm, S), BF16)], compiler_params=_cparams("parallel", "arbitrary"),
    )(*operands)


def _proj_attn(x, g, wt, *, name, tm=1024, tn=512):
    S, D = x.shape
    N = wt.shape[0]

    def body(x_ref, g_ref, w_ref, u_ref, o_ref):
        @pl.when(pl.program_id(1) == 0)
        def _():
            xv = x_ref[...]
            u_ref[...] = (xv * _rstd(xv) * g_ref[...]).astype(BF16)

        o_ref[...] = _nt(u_ref[...], w_ref[...]).astype(BF16)

    return pl.pallas_call(
        body, name=name, grid=(S // tm, N // tn),
        in_specs=[pl.BlockSpec((tm, D), lambda i, j: (i, 0)), pl.BlockSpec((1, D), lambda i, j: (0, 0)),
                  pl.BlockSpec((tn, D), lambda i, j: (j, 0))],
        out_specs=[pl.BlockSpec((tm, D), lambda i, j: (i, 0)), pl.BlockSpec((tm, tn), lambda i, j: (i, j))],
        out_shape=[jax.ShapeDtypeStruct((S, D), BF16), jax.ShapeDtypeStruct((S, N), BF16)],
        compiler_params=_cparams("parallel", "arbitrary"),
    )(x, g, wt)


PAIR_W = 3 * LANES
ATTN_UNROLL_FWD, ATTN_UNROLL_BWD = 4, 4


def _attn_masks(first):
    qi = lax.broadcasted_iota(jnp.int32, (ATTN_BLOCK, 2 * ATTN_BLOCK), 0)
    kj = lax.broadcasted_iota(jnp.int32, (ATTN_BLOCK, 2 * ATTN_BLOCK), 1)
    dist = qi + ATTN_BLOCK - kj
    valid = (dist >= 0) & (dist <= ATTN_BLOCK) & jnp.logical_or(kj >= ATTN_BLOCK, jnp.logical_not(first))
    lane = lax.broadcasted_iota(jnp.int32, (1, LANES), 1)
    return valid, lane


def _for_residue_blocks(S, d, fn):
    span = ATTN_BLOCK * d
    nb = S // span

    def step(n, carry):
        base = pl.multiple_of(n * span, span)
        for r in range(d):
            off = pl.multiple_of((r * nb + n) * ATTN_BLOCK, ATTN_BLOCK)
            fn(lambda ref, r=r: _block_rows(ref, base, r, d),
               lambda ref, val, r=r: _set_block_rows(ref, base, r, d, val), off)
        return carry

    lax.fori_loop(0, nb, step, 0)


def _for_blocks(S, unroll, fn):
    def step(i, carry):
        fn([(pl.multiple_of((i * unroll + u) * ATTN_BLOCK, ATTN_BLOCK), i * unroll + u) for u in range(unroll)])
        return carry

    lax.fori_loop(0, S // ATTN_BLOCK // unroll, step, 0)


def _head_value(x2, lane, e):
    return jnp.sum(jnp.where(lane == HEAD_DIM * e, x2, 0.0), axis=-1, keepdims=True)


def _block_rows(ref, base, r, d):
    if d == 1:
        return ref[pl.ds(base, ATTN_BLOCK), :]
    return ref.at[pl.ds(base, ATTN_BLOCK * d)][pl.ds(r, ATTN_BLOCK, stride=d), :]


def _set_block_rows(ref, base, r, d, val):
    if d == 1:
        ref[pl.ds(base, ATTN_BLOCK), :] = val
    else:
        ref.at[pl.ds(base, ATTN_BLOCK * d)][pl.ds(r, ATTN_BLOCK, stride=d), :] = val


def _order4_to_16(src, dst, pad):
    S = src.shape[0]
    q4, q16 = S // 4, S // 16
    for r in range(4):
        for a in range(4):
            for n in range(q16 // ATTN_BLOCK):
                rows = src.at[pl.ds(r * q4 + 4 * ATTN_BLOCK * n, 4 * ATTN_BLOCK)][pl.ds(a, ATTN_BLOCK, stride=4), :]
                dst[pl.ds(pad + (4 * a + r) * q16 + ATTN_BLOCK * n, ATTN_BLOCK), :] = rows.astype(dst.dtype)


def _order16_to_4(src, pad, dst):
    S = dst.shape[0]
    q4, q16 = S // 4, S // 16
    for r in range(4):
        for a in range(4):
            for n in range(q16 // ATTN_BLOCK):
                rows = src[pl.ds(pad + (4 * a + r) * q16 + ATTN_BLOCK * n, ATTN_BLOCK), :]
                dst.at[pl.ds(r * q4 + 4 * ATTN_BLOCK * n, 4 * ATTN_BLOCK)][pl.ds(a, ATTN_BLOCK, stride=4), :] = rows


def _regroup(S, d, pairs, tmp):
    for src, dst, pad in pairs:
        if d == 16:
            def to_tmp(rows, _, off, src=src):
                tmp[pl.ds(off, ATTN_BLOCK), :] = rows(src)

            _for_residue_blocks(S, 4, to_tmp)
            _order4_to_16(tmp, dst, pad)
    if d != 16:
        def to_dst(rows, _, off):
            for src, dst, pad in pairs:
                dst[pl.ds(pad + off, ATTN_BLOCK), :] = rows(src).astype(dst.dtype)

        _for_residue_blocks(S, d, to_dst)


def _split_pair(p_ref, qs, ks, vs, bk, bv):
    qs[...] = p_ref[:, 0:LANES].astype(F32)
    ks[...] = p_ref[:, LANES:2 * LANES].astype(F32)
    vs[...] = p_ref[:, 2 * LANES:3 * LANES].astype(F32)
    bk[0:ATTN_BLOCK, :] = jnp.zeros((ATTN_BLOCK, LANES), bk.dtype)
    bv[0:ATTN_BLOCK, :] = jnp.zeros((ATTN_BLOCK, LANES), bv.dtype)


def _attn_fwd(proj_a, *, name):
    S = proj_a.shape[0]

    def body(p_ref, o_ref, l_ref, qs, ks, vs, bq, bk, bv, bo, bl, to, tl):
        _split_pair(p_ref, qs, ks, vs, bk, bv)
        for d in DILATIONS:
            nb = S // (ATTN_BLOCK * d)
            _regroup(S, d, ((qs, bq, 0), (ks, bk, ATTN_BLOCK), (vs, bv, ATTN_BLOCK)), to)

            def blocks(group, nb=nb):
                lane = lax.broadcasted_iota(jnp.int32, (1, LANES), 1)
                heads = [(lane >= HEAD_DIM * e) & (lane < HEAD_DIM * (e + 1)) for e in range(LANES // HEAD_DIM)]
                wins = [pl.ds(off, 2 * ATTN_BLOCK) for off, _ in group]
                s = [[_nt(jnp.where(mh, bq[pl.ds(off, ATTN_BLOCK), :], jnp.zeros((ATTN_BLOCK, LANES), BF16)), bk[win, :])
                      for mh in heads] for (off, _), win in zip(group, wins)]
                p, m, l = [], [], []
                for (off, b), su in zip(group, s):
                    valid, _ = _attn_masks(jnp.bitwise_and(b, nb - 1) == 0)
                    sm = [jnp.where(valid, x * ATTN_SCALE, -jnp.inf) for x in su]
                    m.append([jnp.max(x, axis=-1, keepdims=True) for x in sm])
                    p.append([jnp.exp(x - mx) for x, mx in zip(sm, m[-1])])
                    l.append([jnp.sum(x, axis=-1, keepdims=True) for x in p[-1]])
                o = [[_nn(x.astype(BF16), bv[win, :]) for x in pu] for pu, win in zip(p, wins)]
                for (off, _), ou, mu, lu in zip(group, o, m, l):
                    o2 = jnp.zeros((ATTN_BLOCK, LANES), F32)
                    l2 = jnp.zeros((ATTN_BLOCK, LANES), F32)
                    for mh, oe, me_, le in zip(heads, ou, mu, lu):
                        o2 = jnp.where(mh, oe / le, o2)
                        l2 = jnp.where(mh, me_ + jnp.log(le), l2)
                    bo[pl.ds(off, ATTN_BLOCK), :] = o2
                    bl[pl.ds(off, ATTN_BLOCK), :] = l2

            _for_blocks(S, ATTN_UNROLL_FWD, blocks)

            if d == 16:
                _order16_to_4(bo, 0, to)
                _order16_to_4(bl, 0, tl)
            src_o, src_l = (to, tl) if d == 16 else (bo, bl)

            def merge(rows, set_rows, off, d=d, src_o=src_o, src_l=src_l):
                blk = pl.ds(off, ATTN_BLOCK)
                o2, l2 = src_o[blk, :], src_l[blk, :]
                if d != DILATIONS[0]:
                    lo, oo = rows(l_ref), rows(o_ref)
                    ln = jnp.maximum(lo, l2)
                    wa, wb = jnp.exp(lo - ln), jnp.exp(l2 - ln)
                    o2 = (wa * oo + wb * o2) / (wa + wb)
                    l2 = ln + jnp.log(wa + wb)
                set_rows(o_ref, o2)
                set_rows(l_ref, l2)

            _for_residue_blocks(S, min(d, 4), merge)

    slab = pl.BlockSpec((S, LANES), lambda p: (0, p))
    f32_slab, bf16_slab = pltpu.VMEM((S, LANES), F32), pltpu.VMEM((S, LANES), BF16)
    bf16_window = pltpu.VMEM((S + ATTN_BLOCK, LANES), BF16)
    return pl.pallas_call(
        body, name=name, grid=(ATTN_W // LANES,), in_specs=[pl.BlockSpec((S, PAIR_W), lambda p: (0, p))],
        out_specs=[slab, slab],
        out_shape=[jax.ShapeDtypeStruct((S, ATTN_W), F32), jax.ShapeDtypeStruct((S, ATTN_W), F32)],
        scratch_shapes=[f32_slab] * 3 + [bf16_slab, bf16_window, bf16_window] + [f32_slab] * 4,
        compiler_params=_cparams("parallel"),
    )(proj_a)


def _attn_bwd(proj_a, do, lse, delta, *, name):
    S = proj_a.shape[0]

    def body(p_ref, do_ref, lse_ref, dl_ref, o_ref, qs, ks, vs, dqs, dks, dvs, bq, bk, bv, bdo, blse, bdl, bdq, bdk, bdv,
             tmp):
        _split_pair(p_ref, qs, ks, vs, bk, bv)
        bdk[0:ATTN_BLOCK, :] = jnp.zeros((ATTN_BLOCK, LANES), F32)
        bdv[0:ATTN_BLOCK, :] = jnp.zeros((ATTN_BLOCK, LANES), F32)
        for d in DILATIONS:
            nb = S // (ATTN_BLOCK * d)
            _regroup(S, d, ((qs, bq, 0), (ks, bk, ATTN_BLOCK), (vs, bv, ATTN_BLOCK), (do_ref, bdo, 0),
                            (lse_ref, blse, 0), (dl_ref, bdl, 0)), tmp)

            def blocks(group, nb=nb):
                lane = lax.broadcasted_iota(jnp.int32, (1, LANES), 1)
                heads = [(lane >= HEAD_DIM * e) & (lane < HEAD_DIM * (e + 1)) for e in range(LANES // HEAD_DIM)]
                zero = jnp.zeros((ATTN_BLOCK, LANES), BF16)
                chains = [(off, b, e, mh) for off, b in group for e, mh in enumerate(heads)]
                qm = [jnp.where(mh, bq[pl.ds(off, ATTN_BLOCK), :], zero) for off, _, _, mh in chains]
                dom = [jnp.where(mh, bdo[pl.ds(off, ATTN_BLOCK), :], zero) for off, _, _, mh in chains]
                s = [_nt(x, bk[pl.ds(off, 2 * ATTN_BLOCK), :]) for x, (off, _, _, _) in zip(qm, chains)]
                dp = [_nt(x, bv[pl.ds(off, 2 * ATTN_BLOCK), :]) for x, (off, _, _, _) in zip(dom, chains)]
                p, ds = [], []
                for (off, b, e, _), sc, dpc in zip(chains, s, dp):
                    valid, _ = _attn_masks(jnp.bitwise_and(b, nb - 1) == 0)
                    blk = pl.ds(off, ATTN_BLOCK)
                    pc = jnp.where(valid, jnp.exp(sc * ATTN_SCALE - _head_value(blse[blk, :], lane, e)), 0.0)
                    ds.append((pc * (dpc - _head_value(bdl[blk, :], lane, e)) * ATTN_SCALE).astype(BF16))
                    p.append(pc.astype(BF16))
                dq = [_nn(x, bk[pl.ds(off, 2 * ATTN_BLOCK), :]) for x, (off, _, _, _) in zip(ds, chains)]
                dk = [_tn(x, y) for x, y in zip(ds, qm)]
                dv = [_tn(x, y) for x, y in zip(p, dom)]
                nh = len(heads)
                for u, (off, _) in enumerate(group):
                    dq2 = jnp.zeros((ATTN_BLOCK, LANES), F32)
                    for mh, x in zip(heads, dq[nh * u:nh * (u + 1)]):
                        dq2 = jnp.where(mh, x, dq2)
                    bdq[pl.ds(off, ATTN_BLOCK), :] = dq2
                    for acc, grads in ((bdk, dk), (bdv, dv)):
                        win_grad = sum(grads[nh * u + 1:nh * (u + 1)], grads[nh * u])
                        acc[pl.ds(off, ATTN_BLOCK), :] += win_grad[:ATTN_BLOCK]
                        acc[pl.ds(off + ATTN_BLOCK, ATTN_BLOCK), :] = win_grad[ATTN_BLOCK:]

            _for_blocks(S, ATTN_UNROLL_BWD, blocks)

            outs = ((dqs, bdq, 0), (dks, bdk, ATTN_BLOCK), (dvs, bdv, ATTN_BLOCK))
            if d == 16:
                for acc, grad, pad in outs:
                    _order16_to_4(grad, pad, tmp)

                    def add(rows, set_rows, off, acc=acc):
                        set_rows(acc, rows(acc) + tmp[pl.ds(off, ATTN_BLOCK), :])

                    _for_residue_blocks(S, 4, add)
            else:
                def scatter(rows, set_rows, off, d=d):
                    for acc, grad, pad in outs:
                        part = grad[pl.ds(pad + off, ATTN_BLOCK), :]
                        set_rows(acc, part if d == DILATIONS[0] else rows(acc) + part)

                _for_residue_blocks(S, d, scatter)
        o_ref[:, 0:LANES] = dqs[...].astype(BF16)
        o_ref[:, LANES:2 * LANES] = dks[...].astype(BF16)
        o_ref[:, 2 * LANES:3 * LANES] = dvs[...].astype(BF16)

    slab = pl.BlockSpec((S, LANES), lambda p: (0, p), pipeline_mode=pl.Buffered(1))
    pair = pl.BlockSpec((S, PAIR_W), lambda p: (0, p))
    f32_slab, bf16_slab = pltpu.VMEM((S, LANES), F32), pltpu.VMEM((S, LANES), BF16)
    f32_window, bf16_window = pltpu.VMEM((S + ATTN_BLOCK, LANES), F32), pltpu.VMEM((S + ATTN_BLOCK, LANES), BF16)
    return pl.pallas_call(
        body, name=name, grid=(ATTN_W // LANES,), in_specs=[pair, slab, slab, slab], out_specs=pair,
        out_shape=jax.ShapeDtypeStruct(proj_a.shape, BF16),
        scratch_shapes=[f32_slab] * 6 + [bf16_slab, bf16_window, bf16_window, bf16_slab, f32_slab, f32_slab,
                                         f32_slab, f32_window, f32_window, f32_slab],
        compiler_params=_cparams("parallel"),
    )(proj_a, do, lse, delta)


HG_T = 2 * HGRN_CHUNK


def _hgrn_consts():
    row = lax.broadcasted_iota(jnp.int32, (HG_T, HG_T), 0)
    col = lax.broadcasted_iota(jnp.int32, (HG_T, HG_T), 1)
    same = (row >= HGRN_CHUNK) == (col >= HGRN_CHUNK)
    return row, same & (col <= row), same & (col >= row)


def _lower_bound(logits_ref):
    l0, l1 = logits_ref[0:1, :], logits_ref[1:2, :]
    mx = jnp.maximum(l0, l1)
    e0, e1 = jnp.exp(l0 - mx), jnp.exp(l1 - mx)
    return e0 / (e0 + e1)


def _hgrn_gates(qs, fs, lbs, row, causal):
    C = HGRN_CHUNK
    tri = jnp.where(causal, 1.0, 0.0).astype(F32)
    sgs = [_sigmoid(f) for f in fs]
    forgets = [lb + (1.0 - lb) * sg for lb, sg in zip(lbs, sgs)]
    logfs = [jnp.log(forget) for forget in forgets]
    bs = [_nn(tri, logf, HIGHEST) for logf in logfs]
    out = []
    for q, sg, forget, logf, b in zip(qs, sgs, forgets, logfs, bs):
        key = 1.0 - forget
        bend0 = jnp.sum(logf[:C], axis=0, keepdims=True)
        bend1 = jnp.sum(logf[C:], axis=0, keepdims=True)
        bend = jnp.where(row < C, bend0, bend1)
        eb, emb, eend = jnp.exp(b), jnp.exp(-b), jnp.exp(bend - b)
        sq = _sigmoid(q)
        out.append(dict(sg=sg, forget=forget, key=key, bend0=bend0, bend1=bend1, eb=eb, emb=emb, eend=eend, sq=sq,
                        qd=q * sq * eb, ki=key * emb, ke=key * eend))
    return out


def _hgrn_fwd(proj, logits, *, name):
    S = proj.shape[0]
    W, C = HGRN_W, HGRN_CHUNK

    def body(q_ref, f_ref, i_ref, lg_ref, rec_ref, st_ref, s_ref):
        @pl.when(pl.program_id(0) == 0)
        def _():
            s_ref[...] = jnp.zeros_like(s_ref)

        row, causal, _ = _hgrn_consts()
        lb_all = _lower_bound(lg_ref)
        heads = range(HGRN_HEADS)
        sls = [slice(HGRN_DIM * h, HGRN_DIM * (h + 1)) for h in heads]
        gts = _hgrn_gates([q_ref[:, sl] for sl in sls], [f_ref[:, sl] for sl in sls], [lb_all[:, sl] for sl in sls],
                          row, causal)
        qd, ki, ke = ([gt[k].astype(BF16) for gt in gts] for k in ("qd", "ki", "ke"))
        iv = [i_ref[:, sl].astype(BF16) for sl in sls]
        s0 = [s_ref[h] for h in heads]
        a = [_nt(qd[h], ki[h]) for h in heads]
        o0 = [_nt(qd[h][:C], s0[h].astype(BF16)) for h in heads]
        u0 = [_tn(iv[h][:C], ke[h][:C]) for h in heads]
        u1 = [_tn(iv[h][C:], ke[h][C:]) for h in heads]
        s1 = [jnp.exp(gts[h]["bend0"]) * s0[h] + u0[h] for h in heads]
        o = [_nn(jnp.where(causal, a[h], 0.0).astype(BF16), iv[h]) for h in heads]
        o1 = [_nt(qd[h][C:], s1[h].astype(BF16)) for h in heads]
        for h in heads:
            st_ref[0, h] = s0[h]
            st_ref[1, h] = s1[h]
            s_ref[h] = jnp.exp(gts[h]["bend1"]) * s1[h] + u1[h]
            rec_ref[:, sls[h]] = o[h] + jnp.concatenate([o0[h], o1[h]], axis=0)

    blk = lambda j: pl.BlockSpec((HG_T, W), lambda t: (t, j))
    return pl.pallas_call(
        body, name=name, grid=(S // HG_T,),
        in_specs=[blk(0), blk(1), blk(2), pl.BlockSpec((2, W), lambda t: (0, 0))],
        out_specs=[blk(0), pl.BlockSpec((2, HGRN_HEADS, HGRN_DIM, HGRN_DIM), lambda t: (t, 0, 0, 0))],
        out_shape=[jax.ShapeDtypeStruct((S, W), F32),
                   jax.ShapeDtypeStruct((S // C, HGRN_HEADS, HGRN_DIM, HGRN_DIM), F32)],
        scratch_shapes=[pltpu.VMEM((HGRN_HEADS, HGRN_DIM, HGRN_DIM), F32)],
        compiler_params=_cparams("arbitrary"),
    )(proj, proj, proj, logits)


def _hgrn_bwd(proj, logits, states, drec, *, name):
    S = proj.shape[0]
    W, C = HGRN_W, HGRN_CHUNK
    nt = S // HG_T

    def body(q_ref, f_ref, i_ref, lg_ref, st_ref, do_ref, dp_ref, dlg_ref, ds_ref, dlb_ref):
        t = pl.program_id(0)

        @pl.when(t == 0)
        def _():
            ds_ref[...] = jnp.zeros_like(ds_ref)
            dlb_ref[...] = jnp.zeros_like(dlb_ref)

        row, causal, anti = _hgrn_consts()
        lb_all = _lower_bound(lg_ref)
        heads = range(HGRN_HEADS)
        sls = [slice(HGRN_DIM * h, HGRN_DIM * (h + 1)) for h in heads]
        qs, lbs = [q_ref[:, sl] for sl in sls], [lb_all[:, sl] for sl in sls]
        gts = _hgrn_gates(qs, [f_ref[:, sl] for sl in sls], lbs, row, causal)
        qd, ki, ke = ([gt[k] for gt in gts] for k in ("qd", "ki", "ke"))
        qdb, kib, keb = ([x.astype(BF16) for x in xs] for xs in (qd, ki, ke))
        iv = [i_ref[:, sl].astype(BF16) for sl in sls]
        dob = [do_ref[:, sl].astype(BF16) for sl in sls]
        s0, s1, ds1 = [st_ref[0, h] for h in heads], [st_ref[1, h] for h in heads], [ds_ref[h] for h in heads]
        ds1b = [x.astype(BF16) for x in ds1]
        dec0, dec1 = [jnp.exp(gt["bend0"]) for gt in gts], [jnp.exp(gt["bend1"]) for gt in gts]
        a = [_nt(qdb[h], kib[h]) for h in heads]
        da = [_nt(dob[h], iv[h]) for h in heads]
        dqd1 = [_nn(dob[h][C:], s1[h].astype(BF16)) for h in heads]
        dqd0 = [_nn(dob[h][:C], s0[h].astype(BF16)) for h in heads]
        di1 = [_nt(keb[h][C:], ds1b[h]) for h in heads]
        dke1 = [_nn(iv[h][C:], ds1b[h]) for h in heads]
        t1 = [_tn(dob[h][C:], qdb[h][C:]) for h in heads]
        t0 = [_tn(dob[h][:C], qdb[h][:C]) for h in heads]
        ds0 = [dec1[h] * ds1[h] + t1[h] for h in heads]
        ds0b = [x.astype(BF16) for x in ds0]
        a = [jnp.where(causal, x, 0.0).astype(BF16) for x in a]
        da = [jnp.where(causal, x, 0.0).astype(BF16) for x in da]
        di0 = [_nt(keb[h][:C], ds0b[h]) for h in heads]
        dke0 = [_nn(iv[h][:C], ds0b[h]) for h in heads]
        dqd_a = [_nn(da[h], kib[h]) for h in heads]
        dki = [_tn(da[h], qdb[h]) for h in heads]
        di_a = [_tn(a[h], dob[h]) for h in heads]
        dqd, dke, db = [], [], []
        for h in heads:
            ds_ref[h] = dec0[h] * ds0[h] + t0[h]
            ddec1 = jnp.sum(ds1[h] * s1[h], axis=0, keepdims=True)
            ddec0 = jnp.sum(ds0[h] * s0[h], axis=0, keepdims=True)
            dqd.append(dqd_a[h] + jnp.concatenate([dqd0[h], dqd1[h]], axis=0))
            dp_ref[:, 2 * W + HGRN_DIM * h:2 * W + HGRN_DIM * (h + 1)] = (
                di_a[h] + jnp.concatenate([di0[h], di1[h]], axis=0)).astype(BF16)
            dke.append(jnp.concatenate([dke0[h], dke1[h]], axis=0))
            gke = dke[h] * ke[h]
            dbend0 = jnp.sum(gke[:C], axis=0, keepdims=True) + ddec0 * dec0[h]
            dbend1 = jnp.sum(gke[C:], axis=0, keepdims=True) + ddec1 * dec1[h]
            dbh = dqd[h] * qd[h] - dki[h] * ki[h] - gke
            db.append(dbh + jnp.where(row == C - 1, dbend0, 0.0) + jnp.where(row == HG_T - 1, dbend1, 0.0))
        tri = jnp.where(anti, 1.0, 0.0).astype(F32)
        dlogf = [_nn(tri, db[h], HIGHEST) for h in heads]
        for h in heads:
            gt, lb, q = gts[h], lbs[h], qs[h]
            dforget = dlogf[h] / gt["forget"] - (dki[h] * gt["emb"] + dke[h] * gt["eend"])
            sg, sq = gt["sg"], gt["sq"]
            dp_ref[:, W + HGRN_DIM * h:W + HGRN_DIM * (h + 1)] = (dforget * (1.0 - lb) * sg * (1.0 - sg)).astype(BF16)
            dlb_ref[:, sls[h]] += jnp.sum(dforget * (1.0 - sg), axis=0, keepdims=True)
            dp_ref[:, sls[h]] = (dqd[h] * gt["eb"] * sq * (1.0 + q * (1.0 - sq))).astype(BF16)

        @pl.when(t == nt - 1)
        def _():
            dl0 = dlb_ref[...] * lb_all * (1.0 - lb_all)
            dlg_ref[0:1, :] = dl0
            dlg_ref[1:2, :] = -dl0

    blk = lambda j: pl.BlockSpec((HG_T, W), lambda t: (nt - 1 - t, j))
    full = pl.BlockSpec((2, W), lambda t: (0, 0))
    return pl.pallas_call(
        body, name=name, grid=(nt,),
        in_specs=[blk(0), blk(1), blk(2), full,
                  pl.BlockSpec((2, HGRN_HEADS, HGRN_DIM, HGRN_DIM), lambda t: (nt - 1 - t, 0, 0, 0)), blk(0)],
        out_specs=[pl.BlockSpec((HG_T, 3 * W), lambda t: (nt - 1 - t, 0)), full],
        out_shape=[jax.ShapeDtypeStruct((S, 3 * W), BF16), jax.ShapeDtypeStruct((2, W), F32)],
        scratch_shapes=[pltpu.VMEM((HGRN_HEADS, HGRN_DIM, HGRN_DIM), F32), pltpu.VMEM((1, W), F32)],
        compiler_params=_cparams("arbitrary"),
    )(proj, proj, proj, logits, states, drec)


def _out_proj(attn, rec, proj_h, x, g_attn, g_hgrn, g_norm2, w_out, *, name, tm=512):
    S, D = x.shape
    AW, W = ATTN_W, HGRN_W

    def body(a_ref, r_ref, hg_ref, x_ref, ga_ref, gh_ref, g2_ref, w_ref, h_ref, u_ref, m_ref):
        av = a_ref[...]
        m_ref[:, :AW] = (av * _rstd(av) * ga_ref[...]).astype(BF16)
        for h in range(HGRN_HEADS):
            sl = slice(HGRN_DIM * h, HGRN_DIM * (h + 1))
            rv, hg = r_ref[:, sl], hg_ref[:, sl]
            m_ref[:, AW + HGRN_DIM * h:AW + HGRN_DIM * (h + 1)] = (
                (rv * _rstd(rv) * gh_ref[:, sl]) * (hg * _sigmoid(hg))).astype(BF16)
        h1 = x_ref[...] + _nn(m_ref[...], w_ref[...])
        h_ref[...] = h1
        u_ref[...] = (h1 * _rstd(h1) * g2_ref[...]).astype(BF16)

    row = lambda w, j=0: pl.BlockSpec((tm, w), lambda i: (i, j))
    vec = lambda w: pl.BlockSpec((1, w), lambda i: (0, 0))
    return pl.pallas_call(
        body, name=name, grid=(S // tm,),
        in_specs=[row(AW), row(W), row(W, 3), row(D), vec(AW), vec(W), vec(D), _resident(w_out.shape)],
        out_specs=[row(D), row(D), row(AW + W)],
        out_shape=[jax.ShapeDtypeStruct((S, D), F32), jax.ShapeDtypeStruct((S, D), BF16),
                   jax.ShapeDtypeStruct((S, AW + W), BF16)],
        compiler_params=_cparams("parallel"),
    )(attn, rec, proj_h, x, g_attn, g_hgrn, g_norm2, w_out)


def _dmix_post_bwd(dh1b, w_out, attn, rec, proj_h, g_attn, g_hgrn, *, name, tm=512):
    S, D = dh1b.shape
    AW, W = ATTN_W, HGRN_W

    def body(dh_ref, w_ref, a_ref, r_ref, hg_ref, ga_ref, gh_ref, do_ref, dl_ref, dr_ref, dhg_ref, dga_ref, dgh_ref):
        first = pl.program_id(0) == 0
        dmix = _nt(dh_ref[...], w_ref[...])
        av = a_ref[...]
        dov, dga = _norm_bwd(av, ga_ref[...], dmix[:, :AW])
        do_ref[...] = dov
        shift = HEAD_DIM.bit_length() - 1
        hi = lax.shift_right_logical(lax.broadcasted_iota(jnp.int32, (AW, AW), 0), shift)
        hj = lax.shift_right_logical(lax.broadcasted_iota(jnp.int32, (AW, AW), 1), shift)
        prod = dov * av
        hi_part = prod.astype(BF16)
        lo_part = (prod - hi_part.astype(F32)).astype(BF16)
        same_head = jnp.where(hi == hj, 1.0, 0.0).astype(BF16)
        dl_ref[...] = _nn(hi_part, same_head) + _nn(lo_part, same_head)
        _accumulate(dga_ref, jnp.sum(dga, axis=0, keepdims=True), first)

        @pl.when(first)
        def _():
            dgh_ref[...] = jnp.zeros_like(dgh_ref)

        for h in range(HGRN_HEADS):
            sl = slice(HGRN_DIM * h, HGRN_DIM * (h + 1))
            rv, hg, gv = r_ref[:, sl], hg_ref[:, sl], gh_ref[:, sl]
            dout = dmix[:, AW + HGRN_DIM * h:AW + HGRN_DIM * (h + 1)]
            sg = _sigmoid(hg)
            drv, dgh = _norm_bwd(rv, gv, dout * (hg * sg))
            dr_ref[:, sl] = drv
            dgh_ref[:, sl] += jnp.sum(dgh, axis=0, keepdims=True)
            dhg_ref[:, sl] = (dout * (rv * _rstd(rv) * gv) * (sg * (1.0 + hg * (1.0 - sg)))).astype(BF16)

    row = lambda w, j=0: pl.BlockSpec((tm, w), lambda i: (i, j))
    vec = lambda w: pl.BlockSpec((1, w), lambda i: (0, 0))
    return pl.pallas_call(
        body, name=name, grid=(S // tm,),
        in_specs=[row(D), _resident(w_out.shape), row(AW), row(W), row(W, 3), vec(AW), vec(W)],
        out_specs=[row(AW), row(AW), row(W), row(W), vec(AW), vec(W)],
        out_shape=[jax.ShapeDtypeStruct((S, AW), F32), jax.ShapeDtypeStruct((S, AW), F32),
                   jax.ShapeDtypeStruct((S, W), F32), jax.ShapeDtypeStruct((S, W), BF16),
                   jax.ShapeDtypeStruct((1, AW), F32), jax.ShapeDtypeStruct((1, W), F32)],
        compiler_params=_cparams("arbitrary"),
    )(dh1b, w_out, attn, rec, proj_h, g_attn, g_hgrn)


def _conv_act(g, g1, g2, w_ref, b_ref):
    c = b_ref[...] + w_ref[0:1, :] * g2 + w_ref[1:2, :] * g1 + w_ref[2:3, :] * g
    return c, 0.5 * (1.0 + lax.erf(c * (2.0 ** -0.5)))


def _shift_down(g, halo, row):
    g1 = jnp.where(row == 0, halo[7:8], pltpu.roll(g, 1, 0))
    g2 = jnp.where(row == 0, halo[6:7], jnp.where(row == 1, halo[7:8], pltpu.roll(g, 2, 0)))
    return g1, g2


def _shift_up(x, halo, row):
    n = x.shape[0]
    x1 = jnp.where(row == n - 1, halo[0:1], pltpu.roll(x, n - 1, 0))
    x2 = jnp.where(row == n - 2, halo[0:1], jnp.where(row == n - 1, halo[1:2], pltpu.roll(x, n - 2, 0)))
    return x1, x2


def _up_glu(u, wt_up, conv_w, conv_b, *, name, tm=1024, tn=256):
    S, D = u.shape
    F = wt_up.shape[0] // 2
    nf = F // tn

    def body(u_ref, wg_ref, wv_ref, cw_ref, cb_ref, g_ref, v_ref, a_ref, halo_ref):
        i, j = pl.program_id(0), pl.program_id(1)

        @pl.when(i == 0)
        def _():
            halo_ref[j] = jnp.zeros((SUBLANES, tn), F32)

        uv = u_ref[...]
        g, v = _nt(uv, wg_ref[...]), _nt(uv, wv_ref[...])
        row = lax.broadcasted_iota(jnp.int32, (tm, tn), 0)
        g1, g2 = _shift_down(g, halo_ref[j], row)
        c, cdf = _conv_act(g, g1, g2, cw_ref, cb_ref)
        a_ref[...] = (c * cdf * v).astype(BF16)
        g_ref[...] = g.astype(BF16)
        v_ref[...] = v.astype(BF16)
        halo_ref[j] = g[tm - SUBLANES:, :]

    col = pl.BlockSpec((tm, tn), lambda i, j: (i, j))
    out = jax.ShapeDtypeStruct((S, F), BF16)
    return pl.pallas_call(
        body, name=name, grid=(S // tm, nf),
        in_specs=[pl.BlockSpec((tm, D), lambda i, j: (i, 0)), pl.BlockSpec((tn, D), lambda i, j: (j, 0)),
                  pl.BlockSpec((tn, D), lambda i, j: (j + nf, 0)), pl.BlockSpec((3, tn), lambda i, j: (0, j)),
                  pl.BlockSpec((1, tn), lambda i, j: (0, j))],
        out_specs=[col, col, col], out_shape=[out, out, out],
        scratch_shapes=[pltpu.VMEM((nf, SUBLANES, tn), F32)], compiler_params=_cparams("arbitrary", "arbitrary"),
    )(u, wt_up, wt_up, conv_w, conv_b)


def _dact_glu_bwd(dh2b, w_down, gate, val, conv_w, conv_b, *, name, tm=1024, tn=256):
    S, D = dh2b.shape
    F = gate.shape[1]
    nf, ni = F // tn, S // tm
    hb = tm // SUBLANES

    def body(dh_ref, wd_ref, g_ref, gh_ref, v_ref, cw_ref, cb_ref, dg_ref, dv_ref, dcw_ref, dcb_ref, halo_ref, acc_ref):
        i, j = pl.program_id(0), pl.program_id(1)

        @pl.when(i == 0)
        def _():
            halo_ref[j] = jnp.zeros((SUBLANES, tn), F32)
            acc_ref[j] = jnp.zeros((SUBLANES, tn), F32)

        g = g_ref[...].astype(F32)
        before = jnp.where(i < ni - 1, gh_ref[...].astype(F32), 0.0)
        row = lax.broadcasted_iota(jnp.int32, (tm, tn), 0)
        g1, g2 = _shift_down(g, before[SUBLANES:], row)
        c, cdf = _conv_act(g, g1, g2, cw_ref, cb_ref)
        da = _nt(dh_ref[...], wd_ref[...])
        dv_ref[...] = (da * (c * cdf)).astype(BF16)
        pdf = jnp.exp(-0.5 * c * c) * (1.0 / (2.0 * jnp.pi) ** 0.5)
        dc = da * v_ref[...].astype(F32) * (cdf + c * pdf)
        d1, d2 = _shift_up(dc, halo_ref[j], row)
        dg_ref[...] = (cw_ref[2:3, :] * dc + cw_ref[1:2, :] * d1 + cw_ref[0:1, :] * d2).astype(BF16)
        halo_ref[j] = dc[:SUBLANES, :]
        for k, t in enumerate((dc * g2, dc * g1, dc * g, dc)):
            acc_ref[j, k:k + 1, :] += jnp.sum(t, axis=0, keepdims=True)

        @pl.when((i == ni - 1) & (j == nf - 1))
        def _():
            for jj in range(nf):
                dcw_ref[:, jj * tn:(jj + 1) * tn] = acc_ref[jj, 0:3, :]
                dcb_ref[:, jj * tn:(jj + 1) * tn] = acc_ref[jj, 3:4, :]

    tile = pl.BlockSpec((tm, tn), lambda i, j: (ni - 1 - i, j))
    return pl.pallas_call(
        body, name=name, grid=(ni, nf),
        in_specs=[pl.BlockSpec((tm, D), lambda i, j: (ni - 1 - i, 0)), pl.BlockSpec((tn, D), lambda i, j: (j, 0)),
                  tile, pl.BlockSpec((SUBLANES * 2, tn), lambda i, j: (jnp.maximum((ni - 1 - i) * (hb // 2) - 1, 0), j)),
                  tile, pl.BlockSpec((3, tn), lambda i, j: (0, j)), pl.BlockSpec((1, tn), lambda i, j: (0, j))],
        out_specs=[tile, tile, pl.BlockSpec((3, F), lambda i, j: (0, 0)), pl.BlockSpec((1, F), lambda i, j: (0, 0))],
        out_shape=[jax.ShapeDtypeStruct((S, F), BF16), jax.ShapeDtypeStruct((S, F), BF16),
                   jax.ShapeDtypeStruct((3, F), F32), jax.ShapeDtypeStruct((1, F), F32)],
        scratch_shapes=[pltpu.VMEM((nf, SUBLANES, tn), F32), pltpu.VMEM((nf, SUBLANES, tn), F32)],
        compiler_params=_cparams("arbitrary", "arbitrary"),
    )(dh2b, w_down, gate, gate, val, conv_w, conv_b)


def _down_loss(act, w_down, h1, g, target, *, name, tm=512):
    S, F = act.shape
    D = h1.shape[1]

    def body(a_ref, w_ref, h_ref, g_ref, t_ref, dh_ref, dhb_ref, dg_ref, loss_ref):
        first = pl.program_id(0) == 0
        h2 = h_ref[...] + _nn(a_ref[...], w_ref[...])
        gv = g_ref[...]
        r = _rstd(h2)
        xh = h2 * r
        err = xh * gv - t_ref[...]
        part_loss = 0.5 * jnp.sum(jnp.mean(err * err, axis=-1, keepdims=True), axis=0, keepdims=True)
        dy = err * (1.0 / D)
        dxh = dy * gv
        dh = r * (dxh - xh * jnp.mean(dxh * xh, axis=-1, keepdims=True))
        dh_ref[...] = dh
        dhb_ref[...] = dh.astype(BF16)
        _accumulate(dg_ref, jnp.sum(dy * xh, axis=0, keepdims=True), first)
        _accumulate(loss_ref, jnp.broadcast_to(part_loss, (1, LANES)), first)

    row = lambda w: pl.BlockSpec((tm, w), lambda i: (i, 0))
    vec = lambda w: pl.BlockSpec((1, w), lambda i: (0, 0))
    return pl.pallas_call(
        body, name=name, grid=(S // tm,), in_specs=[row(F), _resident(w_down.shape), row(D), vec(D), row(D)],
        out_specs=[row(D), row(D), vec(D), vec(LANES)],
        out_shape=[jax.ShapeDtypeStruct((S, D), F32), jax.ShapeDtypeStruct((S, D), BF16),
                   jax.ShapeDtypeStruct((1, D), F32), jax.ShapeDtypeStruct((1, LANES), F32)],
        compiler_params=_cparams("arbitrary"),
    )(act, w_down, h1, g, target)


def _grad_norm_input(pieces, ws, x, g, add, *, name, tm=512):
    S, D = x.shape
    widths = [p.shape[1] for p in pieces]
    n, nw = len(pieces), len(ws)
    where, wi, off = [], 0, ws[0][1]
    for wd in widths:
        if off == ws[wi][0].shape[0]:
            wi, off = wi + 1, ws[wi + 1][1]
        where.append((wi, off))
        off += wd
    ws = [w for w, _ in ws]

    def body(*refs):
        p_refs, w_refs = refs[:n], refs[n:n + nw]
        x_ref, g_ref, add_ref, dx_ref, dxb_ref, dg_ref = refs[n + nw:]
        halves = _row_halves(tm)
        du = []
        for rows in halves:
            terms = [_nn(p_refs[k][rows, :], w_refs[wi][off:off + widths[k], :]) for k, (wi, off) in enumerate(where)]
            du.append(sum(terms[1:], terms[0]))
        dg_sum = None
        for rows, duh in zip(halves, du):
            dx, dg = _norm_bwd(x_ref[rows, :], g_ref[...], duh)
            dx = add_ref[rows, :] + dx
            dx_ref[rows, :] = dx
            dxb_ref[rows, :] = dx.astype(BF16)
            part = jnp.sum(dg, axis=0, keepdims=True)
            dg_sum = part if dg_sum is None else dg_sum + part
        _accumulate(dg_ref, dg_sum, pl.program_id(0) == 0)

    row = lambda w_: pl.BlockSpec((tm, w_), lambda i: (i, 0))
    vec = pl.BlockSpec((1, D), lambda i: (0, 0))
    return pl.pallas_call(
        body, name=name, grid=(S // tm,),
        in_specs=[row(wd) for wd in widths] + [_resident(w.shape) for w in ws] + [row(D), vec, row(D)],
        out_specs=[row(D), row(D), vec],
        out_shape=[jax.ShapeDtypeStruct((S, D), F32), jax.ShapeDtypeStruct((S, D), BF16),
                   jax.ShapeDtypeStruct((1, D), F32)],
        compiler_params=_cparams("arbitrary"),
    )(*pieces, *ws, x, g, add)


def _rows(a):
    return a.reshape(-1, a.shape[-1])


def _row_tile(rows, cols, itemsize=4, budget=1 << 20):
    t = rows
    while t % 32 == 0 and t * cols * itemsize > budget:
        t //= 2
    return t


def _sum_cast(arrs, out_dtype, *, name):
    shape = arrs[0].shape
    flat = [_rows(a) for a in arrs]
    R, C = flat[0].shape
    tr = _row_tile(R, C)

    def body(*refs):
        acc = refs[0][...].astype(F32)
        for r in refs[1:-1]:
            acc = acc + r[...].astype(F32)
        refs[-1][...] = acc.astype(out_dtype)

    spec = pl.BlockSpec((tr, C), lambda i: (i, 0))
    return pl.pallas_call(
        body, name=name, grid=(R // tr,), in_specs=[spec] * len(flat), out_specs=spec,
        out_shape=jax.ShapeDtypeStruct((R, C), out_dtype), compiler_params=_cparams("parallel"),
    )(*flat).reshape(shape)


def _adamw(parts, w, m, v, *, name):
    shape = w.shape
    w2, m2, v2 = _rows(w), _rows(m), _rows(v)
    R, C = w2.shape
    parts = [p.reshape(-1, R, C) for p in parts]
    tr = _row_tile(R, C)
    np_ = len(parts)
    c1, c2 = 1.0 - ADAM_B1 ** ADAM_STEP, 1.0 - ADAM_B2 ** ADAM_STEP

    def body(*refs):
        terms = [(r, k) for r in refs[:np_] for k in range(r.shape[0])]
        g = terms[0][0][terms[0][1]].astype(F32)
        for r, k in terms[1:]:
            g = g + r[k].astype(F32)
        w_ref, m_ref, v_ref, g_out, d_out, m_out, v_out = refs[np_:]
        mn = ADAM_B1 * m_ref[...] + (1.0 - ADAM_B1) * g
        vn = ADAM_B2 * v_ref[...] + (1.0 - ADAM_B2) * (g * g)
        g_out[...] = g
        d_out[...] = -ADAM_LR * ((mn / c1) / (jnp.sqrt(vn / c2) + ADAM_EPS) + ADAM_WD * w_ref[...])
        m_out[...] = mn
        v_out[...] = vn

    spec = pl.BlockSpec((tr, C), lambda i: (i, 0))
    out = jax.ShapeDtypeStruct((R, C), F32)
    stacks = [pl.BlockSpec((p.shape[0], tr, C), lambda i: (0, i, 0)) for p in parts]
    res = pl.pallas_call(
        body, name=name, grid=(R // tr,), in_specs=stacks + [spec] * 3, out_specs=[spec] * 4,
        out_shape=[out] * 4, compiler_params=_cparams("parallel"),
    )(*parts, w2, m2, v2)
    return [r.reshape(shape) for r in res]


def _adamw_packed(stack, widths, params, *, name):
    c1, c2 = 1.0 - ADAM_B1 ** ADAM_STEP, 1.0 - ADAM_B2 ** ADAM_STEP
    k = stack.shape[0]
    flat = [None if p is None else [_rows(a) for a in p] for p in params]
    n_in = sum(3 for p in flat if p is not None)

    def body(*refs):
        s_ref, ins, outs = refs[0], list(refs[1:1 + n_in]), list(refs[1 + n_in:])
        off = 0
        for width, p in zip(widths, flat):
            rows = 1 if p is None else p[0].shape[0]
            cols = width // rows
            w_ref, m_ref, v_ref = (None, None, None) if p is None else (ins.pop(0), ins.pop(0), ins.pop(0))
            o_refs = [outs.pop(0) for _ in range(1 if p is None else 4)]
            for r in range(rows):
                seg = slice(off + r * cols, off + (r + 1) * cols)
                g = s_ref[0, :, seg]
                for j in range(1, k):
                    g = g + s_ref[j, :, seg]
                o_refs[0][r:r + 1, :] = g
                if p is not None:
                    row = slice(r, r + 1)
                    mn = ADAM_B1 * m_ref[row, :] + (1.0 - ADAM_B1) * g
                    vn = ADAM_B2 * v_ref[row, :] + (1.0 - ADAM_B2) * (g * g)
                    o_refs[1][row, :] = -ADAM_LR * ((mn / c1) / (jnp.sqrt(vn / c2) + ADAM_EPS) + ADAM_WD * w_ref[row, :])
                    o_refs[2][row, :] = mn
                    o_refs[3][row, :] = vn
            off += width

    operands, out_shape = [stack], []
    for width, p in zip(widths, flat):
        if p is None:
            out_shape.append(jax.ShapeDtypeStruct((1, width), F32))
        else:
            operands += p
            out_shape += [jax.ShapeDtypeStruct(p[0].shape, F32)] * 4
    res = list(pl.pallas_call(body, name=name, out_shape=out_shape)(*operands))
    out = []
    for p, orig in zip(flat, params):
        n = 1 if p is None else 4
        out.append([r if orig is None else r.reshape(orig[0].shape) for r in res[:n]])
        res = res[n:]
    return out


def _coords():
    return lax.axis_index("x"), lax.axis_index("y"), lax.axis_index("c")


def _all_gather(shards, *, name):
    n = len(shards)

    def body(*refs):
        x_refs, out_refs = refs[:n], refs[n:2 * n]
        send_sems, recv_sems, local_sems = refs[2 * n:]
        x, y, c = _coords()
        me, sibling = (x, y, c), (x, y, 1 - c)
        chips = [(1 - x, y), (x, 1 - y), (1 - x, 1 - y)]

        def slot(a, dev):
            return out_refs[a].at[4 * dev[0] + 2 * dev[1] + dev[2]]

        def copy(a, k, block, to, src=None):
            return pltpu.make_async_remote_copy(
                src_ref=slot(a, block) if src is None else src, dst_ref=slot(a, block),
                send_sem=send_sems.at[7 * a + k], recv_sem=recv_sems.at[7 * a + k], device_id=to, device_id_type=MESH)

        mine = [pltpu.make_async_copy(x_refs[a], slot(a, me), local_sems.at[a]) for a in range(n)]
        for cp in mine:
            cp.start()
        first = []
        for a in range(n):
            first.append(copy(a, 0, me, sibling, src=x_refs[a]))
            first += [copy(a, 1 + j, me, (*chip, c), src=x_refs[a]) for j, chip in enumerate(chips)]
        for cp in first:
            cp.start()
        passed = []
        for j, chip in enumerate(chips):
            for a in range(n):
                copy(a, 1 + j, (*chip, c), me).wait_recv()
                fwd = copy(a, 4 + j, (*chip, c), sibling)
                fwd.start()
                passed.append(fwd)
        for a in range(n):
            copy(a, 0, sibling, me).wait_recv()
            for j, chip in enumerate(chips):
                copy(a, 4 + j, (*chip, 1 - c), me).wait_recv()
        for cp in first + passed:
            cp.wait_send()
        for cp in mine:
            cp.wait()

    return pl.pallas_call(
        body, name=name, in_specs=[HBM] * n, out_specs=[HBM] * n,
        out_shape=[jax.ShapeDtypeStruct((N_DEV, *s.shape), s.dtype) for s in shards],
        scratch_shapes=[pltpu.SemaphoreType.DMA((7 * n,)), pltpu.SemaphoreType.DMA((7 * n,)),
                        pltpu.SemaphoreType.DMA((n,))],
    )(*shards)


def _flip_y(x, y, c):
    return (x, 1 - y, c)


def _flip_x(x, y, c):
    return (1 - x, y, c)


def _flip_xy(x, y, c):
    return (1 - x, 1 - y, c)


SEM = pl.BlockSpec(memory_space=pltpu.SEMAPHORE)
SIDE_EFFECT = pltpu.SideEffectType.DATAFLOW_SIDE_EFFECTING


def _in_hbm(a):
    return pltpu.with_memory_space_constraint(a, pltpu.HBM)


def _copies_start(srcs, lands, plan, n_copies, *, name):
    ns, nl = len(srcs), len(lands)

    def body(*refs):
        src_refs, land_refs = refs[:ns], refs[ns:ns + nl]
        send_sems, recv_sems = refs[ns + nl:ns + nl + 2]
        token = refs[-1]
        for k, (src, dst, peer, _) in enumerate(plan(src_refs, land_refs, *_coords())):
            pltpu.make_async_remote_copy(src_ref=src, dst_ref=dst, send_sem=send_sems.at[k], recv_sem=recv_sems.at[k],
                                         device_id=peer, device_id_type=MESH).start()
        token[...] = jnp.zeros_like(token)

    bufs = [*srcs, *lands]
    res = pl.pallas_call(
        body, name=name, in_specs=[HBM] * (ns + nl),
        out_specs=(SEM, SEM, *[HBM] * (ns + nl), pl.BlockSpec(memory_space=pltpu.VMEM)),
        out_shape=(pltpu.SemaphoreType.DMA((n_copies,)), pltpu.SemaphoreType.DMA((n_copies,)),
                   *[pltpu.HBM(b.shape, b.dtype) for b in bufs], jax.ShapeDtypeStruct((SUBLANES, LANES), F32)),
        input_output_aliases={i: 2 + i for i in range(ns + nl)},
        compiler_params=pltpu.CompilerParams(has_side_effects=SIDE_EFFECT),
    )(*[_in_hbm(b) for b in bufs])
    return res[0], res[1], list(res[2:2 + ns]), list(res[2 + ns:2 + ns + nl]), res[-1]


def _copies_wait(started, plan, after, *, name):
    send_sems, recv_sems, srcs, lands, _ = started
    ns, nl = len(srcs), len(lands)

    def body(*refs):
        src_refs, land_refs = refs[:ns], refs[ns:ns + nl]
        send_sems, recv_sems = refs[ns + nl:ns + nl + 2]
        for k, (src, dst, peer, here) in enumerate(plan(src_refs, land_refs, *_coords())):
            pltpu.make_async_remote_copy(src_ref=src, dst_ref=dst, send_sem=send_sems.at[k], recv_sem=recv_sems.at[k],
                                         device_id=peer, device_id_type=MESH).wait_send()
            pltpu.make_async_remote_copy(src_ref=src, dst_ref=here, send_sem=send_sems.at[k], recv_sem=recv_sems.at[k],
                                         device_id=peer, device_id_type=MESH).wait_recv()

    bufs = [*srcs, *lands]
    res = pl.pallas_call(
        body, name=name, in_specs=[HBM] * (ns + nl) + [SEM, SEM, pl.BlockSpec(memory_space=pl.ANY)],
        out_specs=[HBM] * (ns + nl), out_shape=[pltpu.HBM(b.shape, b.dtype) for b in bufs],
        input_output_aliases={i: i for i in range(ns + nl)},
        compiler_params=pltpu.CompilerParams(has_side_effects=SIDE_EFFECT),
    )(*bufs, send_sems, recv_sems, after)
    return list(res[ns:])


def _dev_index(dev):
    return 4 * dev[0] + 2 * dev[1] + dev[2]


def _ag_chips_plan(src_refs, land_refs, x, y, c):
    me = _dev_index((x, y, c))
    return [(src, land.at[me], peer, land.at[_dev_index(peer)])
            for src, land in zip(src_refs, land_refs) for peer in (_flip_y(x, y, c), _flip_x(x, y, c), _flip_xy(x, y, c))]


def _ag_sibling_plan(src_refs, land_refs, x, y, c):
    chips = [(x, y), (x, 1 - y), (1 - x, y), (1 - x, 1 - y)]
    return [(land.at[_dev_index((*chip, c))], land.at[_dev_index((*chip, c))], (x, y, 1 - c),
             land.at[_dev_index((*chip, 1 - c))]) for land in land_refs for chip in chips]


def _ag_direct_plan(src_refs, land_refs, x, y, c):
    me = _dev_index((x, y, c))
    plan = []
    for src, land in zip(src_refs, land_refs):
        for m in range(1, N_DEV):
            peer = (x + (m >> 2) * (1 - 2 * x), y + ((m >> 1) & 1) * (1 - 2 * y), c + (m & 1) * (1 - 2 * c))
            plan.append((src, land.at[me], peer, land.at[_dev_index(peer)]))
    return plan


def _rs_direct_plan(src_refs, land_refs, x, y, c):
    plan = []
    for src, land in zip(src_refs, land_refs):
        for m in range(1, N_DEV):
            peer = (x + (m >> 2) * (1 - 2 * x), y + ((m >> 1) & 1) * (1 - 2 * y), c + (m & 1) * (1 - 2 * c))
            plan.append((src.at[_dev_index(peer)], land.at[m - 1], peer, land.at[m - 1]))
    return plan


def _rs_start(grads, me, *, name):
    own = [lax.dynamic_index_in_dim(g, me, 0, keepdims=False) for g in grads]
    lands = [lax.empty((N_DEV - 1, *g.shape[1:]), g.dtype) for g in grads]
    return _copies_start(grads, lands, _rs_direct_plan, (N_DEV - 1) * len(grads), name=name), own


def _rs_finish(started, after, *, name):
    handle, own = started
    got = _copies_wait(handle, _rs_direct_plan, after, name=name)
    return [[o, land] for o, land in zip(own, got)]


def _gathered_cols(w8):
    return w8.transpose(1, 0, 2).reshape(w8.shape[1], -1)


def _pair_major(wt):
    return wt.reshape(3, ATTN_W // LANES, LANES, -1).transpose(1, 0, 2, 3).reshape(3 * ATTN_W, -1)


def kernel(x, norm1_g, w_in, attn_norm_g, hgrn_norm_g, hgrn_lb_logits, w_out, norm2_g, w_up, conv_w, conv_b, w_down, final_norm_g, loss_target, m_norm1_g, m_w_in, m_attn_norm_g, m_hgrn_norm_g, m_hgrn_lb_logits, m_w_out, m_norm2_g, m_w_up, m_conv_w, m_conv_b, m_w_down, m_final_norm_g, v_norm1_g, v_w_in, v_attn_norm_g, v_hgrn_norm_g, v_hgrn_lb_logits, v_w_out, v_norm2_g, v_w_up, v_conv_w, v_conv_b, v_w_down, v_final_norm_g):
    xs, target = x[0], loss_target[0]
    S, D = xs.shape
    NA = 3 * ATTN_W
    fng = final_norm_g.reshape(1, D)

    t = lambda a: a[0].T
    casts = [_sum_cast([w], BF16, name=f"cast_{nm}") for nm, w in
             (("w_in", t(w_in)), ("w_out", w_out[0]), ("w_up", t(w_up)), ("w_down", w_down[0]))]
    me = _dev_index(_coords())
    (g_in,) = _all_gather(casts[:1], name="ag_w_in")
    later, _ = lax.optimization_barrier((casts[1:] + [conv_w[0]], g_in))
    ag1 = _copies_start(later, [lax.empty((N_DEV, *s.shape), s.dtype) for s in later], _ag_chips_plan,
                        3 * len(later), name="ag_chips_start")
    wi = g_in.reshape(-1, D)
    wi_a = _pair_major(wi[:NA])

    u1, proj_a = _proj_attn(xs, norm1_g + ag1[4][0, 0], wi_a, name="proj_attn")
    proj_h = _mm_nt(u1, wi, NA, wi.shape[0] - NA, name="proj_hgrn")
    attn, lse = _attn_fwd(proj_a, name="attn_fwd")
    lands = _copies_wait(ag1, _ag_chips_plan, attn, name="ag_chips_wait")
    lands = [lax.dynamic_update_index_in_dim(l, s, me, 0) for l, s in zip(lands, later)]
    ag2 = _copies_start([], lands, _ag_sibling_plan, 4 * len(later), name="ag_sibling_start")
    rec, states = _hgrn_fwd(proj_h, hgrn_lb_logits + ag2[4][0, 0], name="hgrn_fwd")
    g_out, g_up, g_down, g_cw = _copies_wait(ag2, _ag_sibling_plan, rec, name="ag_sibling_wait")
    wo = g_out.reshape(-1, D)
    wu = g_up.reshape(-1, D)
    wd = g_down.reshape(-1, D)
    cw = _gathered_cols(g_cw)
    h1, u2, mixed = _out_proj(attn, rec, proj_h, xs, attn_norm_g, hgrn_norm_g, norm2_g, wo, name="out_proj")
    gate, val, act = _up_glu(u2, wu, cw, conv_b, name="up_glu")
    dh2, dh2b, d_fng, loss_part = _down_loss(act, wd, h1, fng, target, name="down_loss")

    dgate, dval, d_cw, d_cb = _dact_glu_bwd(dh2b, wd, gate, val, cw, conv_b, name="dact_glu_bwd")
    dw_down = _mm_tn(act, dh2b, tm=256, name="dw_down")
    dh1, dh1b, d_n2g = _grad_norm_input([dgate, dval], [(wu, 0)], h1, norm2_g, dh2, name="du2_norm2_bwd")
    F = dgate.shape[1]
    dw_up = _mm_tn(dgate, u2, tm=256, rows=2 * F, name="dw_up_gate")
    dw_up = _mm_tn(dval, u2, tm=256, rows=2 * F, row_block=lambda i: i + F // 256, into=dw_up, name="dw_up_val")
    rs_ffn = _rs_start([dw_down.reshape(N_DEV, -1, D), dw_up.reshape(N_DEV, -1, D)], me, name="rs_ffn_start")
    dattn, delta, drec, dhg, d_ang, d_hng = _dmix_post_bwd(dh1b, wo, attn, rec, proj_h, attn_norm_g + rs_ffn[0][4][0, 0],
                                                          hgrn_norm_g, name="dmix_post_bwd")
    dw_out = _mm_tn(mixed, dh1b, name="dw_out")
    rs_out = _rs_start([dw_out.reshape(N_DEV, -1, D)], me, name="rs_out_start")
    dproj_h, d_lbl = _hgrn_bwd(proj_h, hgrn_lb_logits + rs_out[0][4][0, 0], states, drec, name="hgrn_bwd")
    small = [("loss", loss_part, None, None, None),
             ("attn_norm_g", d_ang, attn_norm_g, m_attn_norm_g, v_attn_norm_g),
             ("hgrn_norm_g", d_hng, hgrn_norm_g, m_hgrn_norm_g, v_hgrn_norm_g),
             ("hgrn_lb_logits", d_lbl, hgrn_lb_logits, m_hgrn_lb_logits, v_hgrn_lb_logits),
             ("norm2_g", d_n2g, norm2_g, m_norm2_g, v_norm2_g),
             ("conv_b", d_cb, conv_b, m_conv_b, v_conv_b),
             ("final_norm_g", d_fng, final_norm_g, m_final_norm_g, v_final_norm_g)]
    pack = lambda arrs: jnp.concatenate([a.reshape(1, -1) for a in arrs], axis=1)
    small_own = [pack([s[1] for s in small]), d_cw]
    ag_small = _copies_start(small_own, [lax.empty((N_DEV, *s.shape), s.dtype) for s in small_own], _ag_direct_plan,
                             (N_DEV - 1) * len(small_own), name="ag_small_start")
    dproj_a = _attn_bwd(proj_a, dattn, lse, delta, name="attn_bwd")
    pairs = ATTN_W // LANES
    dw_in = _mm_tn(dproj_a, u1, tm=LANES, rows=wi.shape[0], row_block=lambda i: pairs * (i % 3) + i // 3,
                   name="dw_in_attn")
    dw_in = _mm_tn(dproj_h, u1, tm=256, rows=wi.shape[0], row_block=lambda i: i + NA // 256, into=dw_in,
                   name="dw_in_hgrn")
    dw_in = _mm_tn(dhg, u1, tm=256, rows=wi.shape[0], row_block=lambda i: i + (NA + 3 * HGRN_W) // 256, into=dw_in,
                   name="dw_in_gate")
    dw_in, _ = lax.optimization_barrier((dw_in, ag_small[4]))
    rs_in = _rs_start([dw_in.reshape(N_DEV, -1, D)], me, name="rs_in_start")
    grad_x, _, d_n1g = _grad_norm_input([dproj_a, dproj_h, dhg], [(wi_a, 0), (wi, NA)], xs,
                                        norm1_g + rs_in[0][4][0, 0], dh1, name="du1_norm1_bwd")

    res = {}

    def update(nm, parts, w, m, v, transposed=False):
        if transposed:
            res[nm] = [r.T[None] for r in _adamw(parts, t(w), t(m), t(v), name=f"adamw_{nm}")]
        else:
            res[nm] = _adamw(parts, w, m, v, name=f"adamw_{nm}")

    g_down, g_up = _rs_finish(rs_ffn, grad_x, name="rs_ffn_wait")
    update("w_down", g_down, w_down, m_w_down, v_w_down)
    update("w_up", g_up, w_up, m_w_up, v_w_up, transposed=True)
    (g_out,) = _rs_finish(rs_out, grad_x, name="rs_out_wait")
    update("w_out", g_out, w_out, m_w_out, v_w_out)
    (g_in,) = _rs_finish(rs_in, res["w_up"][1], name="rs_in_wait")
    update("w_in", g_in, w_in, m_w_in, v_w_in, transposed=True)

    g_small, g_dcw = [lax.dynamic_update_index_in_dim(l, s, me, 0)
                      for l, s in zip(_copies_wait(ag_small, _ag_direct_plan, grad_x, name="ag_small_wait"), small_own)]
    sm = _adamw_packed(g_small, [s[1].size for s in small], [None if s[2] is None else s[2:] for s in small],
                       name="adamw_small")
    for (nm, *_), r in zip(small, sm):
        res[nm] = r
    ncw = conv_w.shape[-1]
    mine_cw = lax.dynamic_slice_in_dim(g_dcw, me * ncw, ncw, axis=2)
    res["conv_w"] = _adamw([mine_cw], conv_w, m_conv_w, v_conv_w, name="adamw_conv_w")
    late, _ = lax.optimization_barrier((d_n1g, res["w_in"][1]))
    update("norm1_g", _all_gather([late], name="ag_norm1_grad"), norm1_g, m_norm1_g, v_norm1_g)

    loss = res["loss"][0][0, 0]
    order = ["norm1_g", "w_in", "attn_norm_g", "hgrn_norm_g", "hgrn_lb_logits", "w_out", "norm2_g", "w_up",
             "conv_w", "conv_b", "w_down", "final_norm_g"]
    return (loss, grad_x[None], *[res[nm][0] for nm in order], *[res[nm][1] for nm in order],
            *[res[nm][2] for nm in order], *[res[nm][3] for nm in order])
```

```python
import jax
import jax.numpy as jnp
from jax import lax
from jax.experimental import pallas as pl
from jax.experimental.pallas import tpu as pltpu

F32, BF16 = jnp.float32, jnp.bfloat16
NORM_EPS = 1e-6
ATTN_HEADS, HEAD_DIM, ATTN_BLOCK = 8, 64, 128
DILATIONS = (1, 4, 16)
ATTN_SCALE = HEAD_DIM ** -0.5
ATTN_W = ATTN_HEADS * HEAD_DIM
HGRN_HEADS, HGRN_DIM, HGRN_CHUNK = 4, 128, 64
HGRN_W = HGRN_HEADS * HGRN_DIM
ADAM_LR, ADAM_B1, ADAM_B2, ADAM_EPS, ADAM_WD, ADAM_STEP = 0.001, 0.9, 0.999, 1e-08, 0.01, 10
LANES, SUBLANES = 128, 8
VMEM_LIMIT_BYTES = 56 * 1024 * 1024
N_DEV = 8
MESH = pl.DeviceIdType.MESH
HBM = pl.BlockSpec(memory_space=pltpu.HBM)
HIGHEST = lax.Precision.HIGHEST


def _cparams(*sem):
    return pltpu.CompilerParams(dimension_semantics=sem, vmem_limit_bytes=VMEM_LIMIT_BYTES)


def _tile(n, pref):
    if n <= pref:
        return n
    t = (pref // LANES) * LANES
    while n % t:
        t -= LANES
    return t


def _resident(shape):
    return pl.BlockSpec(shape, lambda *_: (0,) * len(shape), pipeline_mode=pl.Buffered(1))


def _dot(a, b, dims, precision=None):
    return lax.dot_general(a, b, (dims, ((), ())), precision=precision, preferred_element_type=F32)


def _nn(a, b, precision=None):
    return _dot(a, b, ((1,), (0,)), precision)


def _nt(a, b):
    return _dot(a, b, ((1,), (1,)))


def _tn(a, b):
    return _dot(a, b, ((0,), (0,)))


def _sigmoid(x):
    return 1.0 / (1.0 + jnp.exp(-x))


def _rstd(x):
    return lax.rsqrt(jnp.mean(x * x, axis=-1, keepdims=True) + NORM_EPS)


def _norm_bwd(x, g, du):
    r = _rstd(x)
    xh = x * r
    dxh = du * g
    return r * (dxh - xh * jnp.mean(dxh * xh, axis=-1, keepdims=True)), du * xh


def _row_halves(tm):
    return [pl.ds(0, tm // 2), pl.ds(tm // 2, tm // 2)]


def _accumulate(ref, part, first):
    @pl.when(first)
    def _():
        ref[...] = part

    @pl.when(jnp.logical_not(first))
    def _():
        ref[...] += part


def _mm_nt(a, bt, row0, n, *, name, out_dtype=F32, tm=1024, tn=512):
    M, K = a.shape
    tm, tn = _tile(M, tm), _tile(n, tn)
    j0 = row0 // tn

    def body(a_ref, b_ref, o_ref):
        o_ref[...] = _nt(a_ref[...], b_ref[...]).astype(out_dtype)

    return pl.pallas_call(
        body, name=name, grid=(M // tm, n // tn),
        in_specs=[pl.BlockSpec((tm, K), lambda i, j: (i, 0)), pl.BlockSpec((tn, K), lambda i, j: (j + j0, 0))],
        out_specs=pl.BlockSpec((tm, tn), lambda i, j: (i, j)), out_shape=jax.ShapeDtypeStruct((M, n), out_dtype),
        compiler_params=_cparams("parallel", "parallel"),
    )(a, bt)


def _mm_tn(x, dy, *, name, tm=512, tn=1024, rows=None, row_block=None, into=None):
    S, M = x.shape
    N = dy.shape[1]
    tm, tn = _tile(M, tm), _tile(N, tn)
    row_block = row_block or (lambda i: i)

    def body(x_ref, dy_ref, *rest):
        o_ref, xt_ref = rest[-2:]

        @pl.when(pl.program_id(1) == 0)
        def _():
            xt_ref[...] = x_ref[...].T

        o_ref[...] = _nn(xt_ref[...], dy_ref[...]).astype(BF16)

    operands = [x, dy] + ([] if into is None else [into])
    return pl.pallas_call(
        body, name=name, grid=(M // tm, N // tn),
        in_specs=[pl.BlockSpec((S, tm), lambda i, j: (0, i)), pl.BlockSpec((S, tn), lambda i, j: (0, j))]
        + ([] if into is None else [pl.BlockSpec(memory_space=pl.ANY)]),
        out_specs=pl.BlockSpec((tm, tn), lambda i, j: (row_block(i), j)),
        out_shape=jax.ShapeDtypeStruct((rows or M, N), BF16),
        input_output_aliases={} if into is None else {2: 0},
        scratch_shapes=[pltpu.VMEM((tm, S), BF16)], compiler_params=_cparams("parallel", "arbitrary"),
    )(*operands)


def _proj_attn(x, g, wt, *, name, tm=1024, tn=512):
    S, D = x.shape
    N = wt.shape[0]

    def body(x_ref, g_ref, w_ref, u_ref, o_ref):
        @pl.when(pl.program_id(1) == 0)
        def _():
            xv = x_ref[...]
            u_ref[...] = (xv * _rstd(xv) * g_ref[...]).astype(BF16)

        o_ref[...] = _nt(u_ref[...], w_ref[...]).astype(BF16)

    return pl.pallas_call(
        body, name=name, grid=(S // tm, N // tn),
        in_specs=[pl.BlockSpec((tm, D), lambda i, j: (i, 0)), pl.BlockSpec((1, D), lambda i, j: (0, 0)),
                  pl.BlockSpec((tn, D), lambda i, j: (j, 0))],
        out_specs=[pl.BlockSpec((tm, D), lambda i, j: (i, 0)), pl.BlockSpec((tm, tn), lambda i, j: (i, j))],
        out_shape=[jax.ShapeDtypeStruct((S, D), BF16), jax.ShapeDtypeStruct((S, N), BF16)],
        compiler_params=_cparams("parallel", "arbitrary"),
    )(x, g, wt)


PAIR_W = 3 * LANES
ATTN_UNROLL_FWD, ATTN_UNROLL_BWD = 4, 4


def _attn_masks(first):
    qi = lax.broadcasted_iota(jnp.int32, (ATTN_BLOCK, 2 * ATTN_BLOCK), 0)
    kj = lax.broadcasted_iota(jnp.int32, (ATTN_BLOCK, 2 * ATTN_BLOCK), 1)
    dist = qi + ATTN_BLOCK - kj
    valid = (dist >= 0) & (dist <= ATTN_BLOCK) & jnp.logical_or(kj >= ATTN_BLOCK, jnp.logical_not(first))
    lane = lax.broadcasted_iota(jnp.int32, (1, LANES), 1)
    return valid, lane


def _for_residue_blocks(S, d, fn):
    span = ATTN_BLOCK * d
    nb = S // span

    def step(n, carry):
        base = pl.multiple_of(n * span, span)
        for r in range(d):
            off = pl.multiple_of((r * nb + n) * ATTN_BLOCK, ATTN_BLOCK)
            fn(lambda ref, r=r: _block_rows(ref, base, r, d),
               lambda ref, val, r=r: _set_block_rows(ref, base, r, d, val), off)
        return carry

    lax.fori_loop(0, nb, step, 0)


def _for_blocks(S, unroll, fn):
    def step(i, carry):
        fn([(pl.multiple_of((i * unroll + u) * ATTN_BLOCK, ATTN_BLOCK), i * unroll + u) for u in range(unroll)])
        return carry

    lax.fori_loop(0, S // ATTN_BLOCK // unroll, step, 0)


def _head_value(x2, lane, e):
    return jnp.sum(jnp.where(lane == HEAD_DIM * e, x2, 0.0), axis=-1, keepdims=True)


def _block_rows(ref, base, r, d):
    if d == 1:
        return ref[pl.ds(base, ATTN_BLOCK), :]
    return ref.at[pl.ds(base, ATTN_BLOCK * d)][pl.ds(r, ATTN_BLOCK, stride=d), :]


def _set_block_rows(ref, base, r, d, val):
    if d == 1:
        ref[pl.ds(base, ATTN_BLOCK), :] = val
    else:
        ref.at[pl.ds(base, ATTN_BLOCK * d)][pl.ds(r, ATTN_BLOCK, stride=d), :] = val


def _order4_to_16(src, dst, pad):
    S = src.shape[0]
    q4, q16 = S // 4, S // 16
    for r in range(4):
        for a in range(4):
            for n in range(q16 // ATTN_BLOCK):
                rows = src.at[pl.ds(r * q4 + 4 * ATTN_BLOCK * n, 4 * ATTN_BLOCK)][pl.ds(a, ATTN_BLOCK, stride=4), :]
                dst[pl.ds(pad + (4 * a + r) * q16 + ATTN_BLOCK * n, ATTN_BLOCK), :] = rows.astype(dst.dtype)


def _order16_to_4(src, pad, dst):
    S = dst.shape[0]
    q4, q16 = S // 4, S // 16
    for r in range(4):
        for a in range(4):
            for n in range(q16 // ATTN_BLOCK):
                rows = src[pl.ds(pad + (4 * a + r) * q16 + ATTN_BLOCK * n, ATTN_BLOCK), :]
                dst.at[pl.ds(r * q4 + 4 * ATTN_BLOCK * n, 4 * ATTN_BLOCK)][pl.ds(a, ATTN_BLOCK, stride=4), :] = rows


def _regroup(S, d, pairs, tmp):
    for src, dst, pad in pairs:
        if d == 16:
            def to_tmp(rows, _, off, src=src):
                tmp[pl.ds(off, ATTN_BLOCK), :] = rows(src)

            _for_residue_blocks(S, 4, to_tmp)
            _order4_to_16(tmp, dst, pad)
    if d != 16:
        def to_dst(rows, _, off):
            for src, dst, pad in pairs:
                dst[pl.ds(pad + off, ATTN_BLOCK), :] = rows(src).astype(dst.dtype)

        _for_residue_blocks(S, d, to_dst)


def _split_pair(p_ref, qs, ks, vs, bk, bv):
    qs[...] = p_ref[:, 0:LANES].astype(F32)
    ks[...] = p_ref[:, LANES:2 * LANES].astype(F32)
    vs[...] = p_ref[:, 2 * LANES:3 * LANES].astype(F32)
    bk[0:ATTN_BLOCK, :] = jnp.zeros((ATTN_BLOCK, LANES), bk.dtype)
    bv[0:ATTN_BLOCK, :] = jnp.zeros((ATTN_BLOCK, LANES), bv.dtype)


def _attn_fwd(proj_a, *, name):
    S = proj_a.shape[0]

    def body(p_ref, o_ref, l_ref, qs, ks, vs, bq, bk, bv, bo, bl, to, tl):
        _split_pair(p_ref, qs, ks, vs, bk, bv)
        for d in DILATIONS:
            nb = S // (ATTN_BLOCK * d)
            _regroup(S, d, ((qs, bq, 0), (ks, bk, ATTN_BLOCK), (vs, bv, ATTN_BLOCK)), to)

            def blocks(group, nb=nb):
                lane = lax.broadcasted_iota(jnp.int32, (1, LANES), 1)
                heads = [(lane >= HEAD_DIM * e) & (lane < HEAD_DIM * (e + 1)) for e in range(LANES // HEAD_DIM)]
                wins = [pl.ds(off, 2 * ATTN_BLOCK) for off, _ in group]
                s = [[_nt(jnp.where(mh, bq[pl.ds(off, ATTN_BLOCK), :], jnp.zeros((ATTN_BLOCK, LANES), BF16)), bk[win, :])
                      for mh in heads] for (off, _), win in zip(group, wins)]
                p, m, l = [], [], []
                for (off, b), su in zip(group, s):
                    valid, _ = _attn_masks(jnp.bitwise_and(b, nb - 1) == 0)
                    sm = [jnp.where(valid, x * ATTN_SCALE, -jnp.inf) for x in su]
                    m.append([jnp.max(x, axis=-1, keepdims=True) for x in sm])
                    p.append([jnp.exp(x - mx) for x, mx in zip(sm, m[-1])])
                    l.append([jnp.sum(x, axis=-1, keepdims=True) for x in p[-1]])
                o = [[_nn(x.astype(BF16), bv[win, :]) for x in pu] for pu, win in zip(p, wins)]
                for (off, _), ou, mu, lu in zip(group, o, m, l):
                    o2 = jnp.zeros((ATTN_BLOCK, LANES), F32)
                    l2 = jnp.zeros((ATTN_BLOCK, LANES), F32)
                    for mh, oe, me_, le in zip(heads, ou, mu, lu):
                        o2 = jnp.where(mh, oe / le, o2)
                        l2 = jnp.where(mh, me_ + jnp.log(le), l2)
                    bo[pl.ds(off, ATTN_BLOCK), :] = o2
                    bl[pl.ds(off, ATTN_BLOCK), :] = l2

            _for_blocks(S, ATTN_UNROLL_FWD, blocks)

            if d == 16:
                _order16_to_4(bo, 0, to)
                _order16_to_4(bl, 0, tl)
            src_o, src_l = (to, tl) if d == 16 else (bo, bl)

            def merge(rows, set_rows, off, d=d, src_o=src_o, src_l=src_l):
                blk = pl.ds(off, ATTN_BLOCK)
                o2, l2 = src_o[blk, :], src_l[blk, :]
                if d != DILATIONS[0]:
                    lo, oo = rows(l_ref), rows(o_ref)
                    ln = jnp.maximum(lo, l2)
                    wa, wb = jnp.exp(lo - ln), jnp.exp(l2 - ln)
                    o2 = (wa * oo + wb * o2) / (wa + wb)
                    l2 = ln + jnp.log(wa + wb)
                set_rows(o_ref, o2)
                set_rows(l_ref, l2)

            _for_residue_blocks(S, min(d, 4), merge)

    slab = pl.BlockSpec((S, LANES), lambda p: (0, p))
    f32_slab, bf16_slab = pltpu.VMEM((S, LANES), F32), pltpu.VMEM((S, LANES), BF16)
    bf16_window = pltpu.VMEM((S + ATTN_BLOCK, LANES), BF16)
    return pl.pallas_call(
        body, name=name, grid=(ATTN_W // LANES,), in_specs=[pl.BlockSpec((S, PAIR_W), lambda p: (0, p))],
        out_specs=[slab, slab],
        out_shape=[jax.ShapeDtypeStruct((S, ATTN_W), F32), jax.ShapeDtypeStruct((S, ATTN_W), F32)],
        scratch_shapes=[f32_slab] * 3 + [bf16_slab, bf16_window, bf16_window] + [f32_slab] * 4,
        compiler_params=_cparams("parallel"),
    )(proj_a)


def _attn_bwd(proj_a, do, lse, delta, *, name):
    S = proj_a.shape[0]

    def body(p_ref, do_ref, lse_ref, dl_ref, o_ref, qs, ks, vs, dqs, dks, dvs, bq, bk, bv, bdo, blse, bdl, bdq, bdk, bdv,
             tmp):
        _split_pair(p_ref, qs, ks, vs, bk, bv)
        bdk[0:ATTN_BLOCK, :] = jnp.zeros((ATTN_BLOCK, LANES), F32)
        bdv[0:ATTN_BLOCK, :] = jnp.zeros((ATTN_BLOCK, LANES), F32)
        for d in DILATIONS:
            nb = S // (ATTN_BLOCK * d)
            _regroup(S, d, ((qs, bq, 0), (ks, bk, ATTN_BLOCK), (vs, bv, ATTN_BLOCK), (do_ref, bdo, 0),
                            (lse_ref, blse, 0), (dl_ref, bdl, 0)), tmp)

            def blocks(group, nb=nb):
                lane = lax.broadcasted_iota(jnp.int32, (1, LANES), 1)
                heads = [(lane >= HEAD_DIM * e) & (lane < HEAD_DIM * (e + 1)) for e in range(LANES // HEAD_DIM)]
                zero = jnp.zeros((ATTN_BLOCK, LANES), BF16)
                chains = [(off, b, e, mh) for off, b in group for e, mh in enumerate(heads)]
                qm = [jnp.where(mh, bq[pl.ds(off, ATTN_BLOCK), :], zero) for off, _, _, mh in chains]
                dom = [jnp.where(mh, bdo[pl.ds(off, ATTN_BLOCK), :], zero) for off, _, _, mh in chains]
                s = [_nt(x, bk[pl.ds(off, 2 * ATTN_BLOCK), :]) for x, (off, _, _, _) in zip(qm, chains)]
                dp = [_nt(x, bv[pl.ds(off, 2 * ATTN_BLOCK), :]) for x, (off, _, _, _) in zip(dom, chains)]
                p, ds = [], []
                for (off, b, e, _), sc, dpc in zip(chains, s, dp):
                    valid, _ = _attn_masks(jnp.bitwise_and(b, nb - 1) == 0)
                    blk = pl.ds(off, ATTN_BLOCK)
                    pc = jnp.where(valid, jnp.exp(sc * ATTN_SCALE - _head_value(blse[blk, :], lane, e)), 0.0)
                    ds.append((pc * (dpc - _head_value(bdl[blk, :], lane, e)) * ATTN_SCALE).astype(BF16))
                    p.append(pc.astype(BF16))
                dq = [_nn(x, bk[pl.ds(off, 2 * ATTN_BLOCK), :]) for x, (off, _, _, _) in zip(ds, chains)]
                dk = [_tn(x, y) for x, y in zip(ds, qm)]
                dv = [_tn(x, y) for x, y in zip(p, dom)]
                nh = len(heads)
                for u, (off, _) in enumerate(group):
                    dq2 = jnp.zeros((ATTN_BLOCK, LANES), F32)
                    for mh, x in zip(heads, dq[nh * u:nh * (u + 1)]):
                        dq2 = jnp.where(mh, x, dq2)
                    bdq[pl.ds(off, ATTN_BLOCK), :] = dq2
                    for acc, grads in ((bdk, dk), (bdv, dv)):
                        win_grad = sum(grads[nh * u + 1:nh * (u + 1)], grads[nh * u])
                        acc[pl.ds(off, ATTN_BLOCK), :] += win_grad[:ATTN_BLOCK]
                        acc[pl.ds(off + ATTN_BLOCK, ATTN_BLOCK), :] = win_grad[ATTN_BLOCK:]

            _for_blocks(S, ATTN_UNROLL_BWD, blocks)

            outs = ((dqs, bdq, 0), (dks, bdk, ATTN_BLOCK), (dvs, bdv, ATTN_BLOCK))
            if d == 16:
                for acc, grad, pad in outs:
                    _order16_to_4(grad, pad, tmp)

                    def add(rows, set_rows, off, acc=acc):
                        set_rows(acc, rows(acc) + tmp[pl.ds(off, ATTN_BLOCK), :])

                    _for_residue_blocks(S, 4, add)
            else:
                def scatter(rows, set_rows, off, d=d):
                    for acc, grad, pad in outs:
                        part = grad[pl.ds(pad + off, ATTN_BLOCK), :]
                        set_rows(acc, part if d == DILATIONS[0] else rows(acc) + part)

                _for_residue_blocks(S, d, scatter)
        o_ref[:, 0:LANES] = dqs[...].astype(BF16)
        o_ref[:, LANES:2 * LANES] = dks[...].astype(BF16)
        o_ref[:, 2 * LANES:3 * LANES] = dvs[...].astype(BF16)

    slab = pl.BlockSpec((S, LANES), lambda p: (0, p), pipeline_mode=pl.Buffered(1))
    pair = pl.BlockSpec((S, PAIR_W), lambda p: (0, p))
    f32_slab, bf16_slab = pltpu.VMEM((S, LANES), F32), pltpu.VMEM((S, LANES), BF16)
    f32_window, bf16_window = pltpu.VMEM((S + ATTN_BLOCK, LANES), F32), pltpu.VMEM((S + ATTN_BLOCK, LANES), BF16)
    return pl.pallas_call(
        body, name=name, grid=(ATTN_W // LANES,), in_specs=[pair, slab, slab, slab], out_specs=pair,
        out_shape=jax.ShapeDtypeStruct(proj_a.shape, BF16),
        scratch_shapes=[f32_slab] * 6 + [bf16_slab, bf16_window, bf16_window, bf16_slab, f32_slab, f32_slab,
                                         f32_slab, f32_window, f32_window, f32_slab],
        compiler_params=_cparams("parallel"),
    )(proj_a, do, lse, delta)


HG_T = 2 * HGRN_CHUNK


def _hgrn_consts():
    row = lax.broadcasted_iota(jnp.int32, (HG_T, HG_T), 0)
    col = lax.broadcasted_iota(jnp.int32, (HG_T, HG_T), 1)
    same = (row >= HGRN_CHUNK) == (col >= HGRN_CHUNK)
    return row, same & (col <= row), same & (col >= row)


def _lower_bound(logits_ref):
    l0, l1 = logits_ref[0:1, :], logits_ref[1:2, :]
    mx = jnp.maximum(l0, l1)
    e0, e1 = jnp.exp(l0 - mx), jnp.exp(l1 - mx)
    return e0 / (e0 + e1)


def _hgrn_gates(qs, fs, lbs, row, causal):
    C = HGRN_CHUNK
    tri = jnp.where(causal, 1.0, 0.0).astype(F32)
    sgs = [_sigmoid(f) for f in fs]
    forgets = [lb + (1.0 - lb) * sg for lb, sg in zip(lbs, sgs)]
    logfs = [jnp.log(forget) for forget in forgets]
    bs = [_nn(tri, logf, HIGHEST) for logf in logfs]
    out = []
    for q, sg, forget, logf, b in zip(qs, sgs, forgets, logfs, bs):
        key = 1.0 - forget
        bend0 = jnp.sum(logf[:C], axis=0, keepdims=True)
        bend1 = jnp.sum(logf[C:], axis=0, keepdims=True)
        bend = jnp.where(row < C, bend0, bend1)
        eb, emb, eend = jnp.exp(b), jnp.exp(-b), jnp.exp(bend - b)
        sq = _sigmoid(q)
        out.append(dict(sg=sg, forget=forget, key=key, bend0=bend0, bend1=bend1, eb=eb, emb=emb, eend=eend, sq=sq,
                        qd=q * sq * eb, ki=key * emb, ke=key * eend))
    return out


def _hgrn_fwd(proj, logits, *, name):
    S = proj.shape[0]
    W, C = HGRN_W, HGRN_CHUNK

    def body(q_ref, f_ref, i_ref, lg_ref, rec_ref, st_ref, s_ref):
        @pl.when(pl.program_id(0) == 0)
        def _():
            s_ref[...] = jnp.zeros_like(s_ref)

        row, causal, _ = _hgrn_consts()
        lb_all = _lower_bound(lg_ref)
        heads = range(HGRN_HEADS)
        sls = [slice(HGRN_DIM * h, HGRN_DIM * (h + 1)) for h in heads]
        gts = _hgrn_gates([q_ref[:, sl] for sl in sls], [f_ref[:, sl] for sl in sls], [lb_all[:, sl] for sl in sls],
                          row, causal)
        qd, ki, ke = ([gt[k].astype(BF16) for gt in gts] for k in ("qd", "ki", "ke"))
        iv = [i_ref[:, sl].astype(BF16) for sl in sls]
        s0 = [s_ref[h] for h in heads]
        a = [_nt(qd[h], ki[h]) for h in heads]
        o0 = [_nt(qd[h][:C], s0[h].astype(BF16)) for h in heads]
        u0 = [_tn(iv[h][:C], ke[h][:C]) for h in heads]
        u1 = [_tn(iv[h][C:], ke[h][C:]) for h in heads]
        s1 = [jnp.exp(gts[h]["bend0"]) * s0[h] + u0[h] for h in heads]
        o = [_nn(jnp.where(causal, a[h], 0.0).astype(BF16), iv[h]) for h in heads]
        o1 = [_nt(qd[h][C:], s1[h].astype(BF16)) for h in heads]
        for h in heads:
            st_ref[0, h] = s0[h]
            st_ref[1, h] = s1[h]
            s_ref[h] = jnp.exp(gts[h]["bend1"]) * s1[h] + u1[h]
            rec_ref[:, sls[h]] = o[h] + jnp.concatenate([o0[h], o1[h]], axis=0)

    blk = lambda j: pl.BlockSpec((HG_T, W), lambda t: (t, j))
    return pl.pallas_call(
        body, name=name, grid=(S // HG_T,),
        in_specs=[blk(0), blk(1), blk(2), pl.BlockSpec((2, W), lambda t: (0, 0))],
        out_specs=[blk(0), pl.BlockSpec((2, HGRN_HEADS, HGRN_DIM, HGRN_DIM), lambda t: (t, 0, 0, 0))],
        out_shape=[jax.ShapeDtypeStruct((S, W), F32),
                   jax.ShapeDtypeStruct((S // C, HGRN_HEADS, HGRN_DIM, HGRN_DIM), F32)],
        scratch_shapes=[pltpu.VMEM((HGRN_HEADS, HGRN_DIM, HGRN_DIM), F32)],
        compiler_params=_cparams("arbitrary"),
    )(proj, proj, proj, logits)


def _hgrn_bwd(proj, logits, states, drec, *, name):
    S = proj.shape[0]
    W, C = HGRN_W, HGRN_CHUNK
    nt = S // HG_T

    def body(q_ref, f_ref, i_ref, lg_ref, st_ref, do_ref, dp_ref, dlg_ref, ds_ref, dlb_ref):
        t = pl.program_id(0)

        @pl.when(t == 0)
        def _():
            ds_ref[...] = jnp.zeros_like(ds_ref)
            dlb_ref[...] = jnp.zeros_like(dlb_ref)

        row, causal, anti = _hgrn_consts()
        lb_all = _lower_bound(lg_ref)
        heads = range(HGRN_HEADS)
        sls = [slice(HGRN_DIM * h, HGRN_DIM * (h + 1)) for h in heads]
        qs, lbs = [q_ref[:, sl] for sl in sls], [lb_all[:, sl] for sl in sls]
        gts = _hgrn_gates(qs, [f_ref[:, sl] for sl in sls], lbs, row, causal)
        qd, ki, ke = ([gt[k] for gt in gts] for k in ("qd", "ki", "ke"))
        qdb, kib, keb = ([x.astype(BF16) for x in xs] for xs in (qd, ki, ke))
        iv = [i_ref[:, sl].astype(BF16) for sl in sls]
        dob = [do_ref[:, sl].astype(BF16) for sl in sls]
        s0, s1, ds1 = [st_ref[0, h] for h in heads], [st_ref[1, h] for h in heads], [ds_ref[h] for h in heads]
        ds1b = [x.astype(BF16) for x in ds1]
        dec0, dec1 = [jnp.exp(gt["bend0"]) for gt in gts], [jnp.exp(gt["bend1"]) for gt in gts]
        a = [_nt(qdb[h], kib[h]) for h in heads]
        da = [_nt(dob[h], iv[h]) for h in heads]
        dqd1 = [_nn(dob[h][C:], s1[h].astype(BF16)) for h in heads]
        dqd0 = [_nn(dob[h][:C], s0[h].astype(BF16)) for h in heads]
        di1 = [_nt(keb[h][C:], ds1b[h]) for h in heads]
        dke1 = [_nn(iv[h][C:], ds1b[h]) for h in heads]
        t1 = [_tn(dob[h][C:], qdb[h][C:]) for h in heads]
        t0 = [_tn(dob[h][:C], qdb[h][:C]) for h in heads]
        ds0 = [dec1[h] * ds1[h] + t1[h] for h in heads]
        ds0b = [x.astype(BF16) for x in ds0]
        a = [jnp.where(causal, x, 0.0).astype(BF16) for x in a]
        da = [jnp.where(causal, x, 0.0).astype(BF16) for x in da]
        di0 = [_nt(keb[h][:C], ds0b[h]) for h in heads]
        dke0 = [_nn(iv[h][:C], ds0b[h]) for h in heads]
        dqd_a = [_nn(da[h], kib[h]) for h in heads]
        dki = [_tn(da[h], qdb[h]) for h in heads]
        di_a = [_tn(a[h], dob[h]) for h in heads]
        dqd, dke, db = [], [], []
        for h in heads:
            ds_ref[h] = dec0[h] * ds0[h] + t0[h]
            ddec1 = jnp.sum(ds1[h] * s1[h], axis=0, keepdims=True)
            ddec0 = jnp.sum(ds0[h] * s0[h], axis=0, keepdims=True)
            dqd.append(dqd_a[h] + jnp.concatenate([dqd0[h], dqd1[h]], axis=0))
            dp_ref[:, 2 * W + HGRN_DIM * h:2 * W + HGRN_DIM * (h + 1)] = (
                di_a[h] + jnp.concatenate([di0[h], di1[h]], axis=0)).astype(BF16)
            dke.append(jnp.concatenate([dke0[h], dke1[h]], axis=0))
            gke = dke[h] * ke[h]
            dbend0 = jnp.sum(gke[:C], axis=0, keepdims=True) + ddec0 * dec0[h]
            dbend1 = jnp.sum(gke[C:], axis=0, keepdims=True) + ddec1 * dec1[h]
            dbh = dqd[h] * qd[h] - dki[h] * ki[h] - gke
            db.append(dbh + jnp.where(row == C - 1, dbend0, 0.0) + jnp.where(row == HG_T - 1, dbend1, 0.0))
        tri = jnp.where(anti, 1.0, 0.0).astype(F32)
        dlogf = [_nn(tri, db[h], HIGHEST) for h in heads]
        for h in heads:
            gt, lb, q = gts[h], lbs[h], qs[h]
            dforget = dlogf[h] / gt["forget"] - (dki[h] * gt["emb"] + dke[h] * gt["eend"])
            sg, sq = gt["sg"], gt["sq"]
            dp_ref[:, W + HGRN_DIM * h:W + HGRN_DIM * (h + 1)] = (dforget * (1.0 - lb) * sg * (1.0 - sg)).astype(BF16)
            dlb_ref[:, sls[h]] += jnp.sum(dforget * (1.0 - sg), axis=0, keepdims=True)
            dp_ref[:, sls[h]] = (dqd[h] * gt["eb"] * sq * (1.0 + q * (1.0 - sq))).astype(BF16)

        @pl.when(t == nt - 1)
        def _():
            dl0 = dlb_ref[...] * lb_all * (1.0 - lb_all)
            dlg_ref[0:1, :] = dl0
            dlg_ref[1:2, :] = -dl0

    blk = lambda j: pl.BlockSpec((HG_T, W), lambda t: (nt - 1 - t, j))
    full = pl.BlockSpec((2, W), lambda t: (0, 0))
    return pl.pallas_call(
        body, name=name, grid=(nt,),
        in_specs=[blk(0), blk(1), blk(2), full,
                  pl.BlockSpec((2, HGRN_HEADS, HGRN_DIM, HGRN_DIM), lambda t: (nt - 1 - t, 0, 0, 0)), blk(0)],
        out_specs=[pl.BlockSpec((HG_T, 3 * W), lambda t: (nt - 1 - t, 0)), full],
        out_shape=[jax.ShapeDtypeStruct((S, 3 * W), BF16), jax.ShapeDtypeStruct((2, W), F32)],
        scratch_shapes=[pltpu.VMEM((HGRN_HEADS, HGRN_DIM, HGRN_DIM), F32), pltpu.VMEM((1, W), F32)],
        compiler_params=_cparams("arbitrary"),
    )(proj, proj, proj, logits, states, drec)


def _out_proj(attn, rec, proj_h, x, g_attn, g_hgrn, g_norm2, w_out, *, name, tm=512):
    S, D = x.shape
    AW, W = ATTN_W, HGRN_W

    def body(a_ref, r_ref, hg_ref, x_ref, ga_ref, gh_ref, g2_ref, w_ref, h_ref, u_ref, m_ref):
        av = a_ref[...]
        m_ref[:, :AW] = (av * _rstd(av) * ga_ref[...]).astype(BF16)
        for h in range(HGRN_HEADS):
            sl = slice(HGRN_DIM * h, HGRN_DIM * (h + 1))
            rv, hg = r_ref[:, sl], hg_ref[:, sl]
            m_ref[:, AW + HGRN_DIM * h:AW + HGRN_DIM * (h + 1)] = (
                (rv * _rstd(rv) * gh_ref[:, sl]) * (hg * _sigmoid(hg))).astype(BF16)
        h1 = x_ref[...] + _nn(m_ref[...], w_ref[...])
        h_ref[...] = h1
        u_ref[...] = (h1 * _rstd(h1) * g2_ref[...]).astype(BF16)

    row = lambda w, j=0: pl.BlockSpec((tm, w), lambda i: (i, j))
    vec = lambda w: pl.BlockSpec((1, w), lambda i: (0, 0))
    return pl.pallas_call(
        body, name=name, grid=(S // tm,),
        in_specs=[row(AW), row(W), row(W, 3), row(D), vec(AW), vec(W), vec(D), _resident(w_out.shape)],
        out_specs=[row(D), row(D), row(AW + W)],
        out_shape=[jax.ShapeDtypeStruct((S, D), F32), jax.ShapeDtypeStruct((S, D), BF16),
                   jax.ShapeDtypeStruct((S, AW + W), BF16)],
        compiler_params=_cparams("parallel"),
    )(attn, rec, proj_h, x, g_attn, g_hgrn, g_norm2, w_out)


def _dmix_post_bwd(dh1b, w_out, attn, rec, proj_h, g_attn, g_hgrn, *, name, tm=512):
    S, D = dh1b.shape
    AW, W = ATTN_W, HGRN_W

    def body(dh_ref, w_ref, a_ref, r_ref, hg_ref, ga_ref, gh_ref, do_ref, dl_ref, dr_ref, dhg_ref, dga_ref, dgh_ref):
        first = pl.program_id(0) == 0
        dmix = _nt(dh_ref[...], w_ref[...])
        av = a_ref[...]
        dov, dga = _norm_bwd(av, ga_ref[...], dmix[:, :AW])
        do_ref[...] = dov
        shift = HEAD_DIM.bit_length() - 1
        hi = lax.shift_right_logical(lax.broadcasted_iota(jnp.int32, (AW, AW), 0), shift)
        hj = lax.shift_right_logical(lax.broadcasted_iota(jnp.int32, (AW, AW), 1), shift)
        prod = dov * av
        hi_part = prod.astype(BF16)
        lo_part = (prod - hi_part.astype(F32)).astype(BF16)
        same_head = jnp.where(hi == hj, 1.0, 0.0).astype(BF16)
        dl_ref[...] = _nn(hi_part, same_head) + _nn(lo_part, same_head)
        _accumulate(dga_ref, jnp.sum(dga, axis=0, keepdims=True), first)

        @pl.when(first)
        def _():
            dgh_ref[...] = jnp.zeros_like(dgh_ref)

        for h in range(HGRN_HEADS):
            sl = slice(HGRN_DIM * h, HGRN_DIM * (h + 1))
            rv, hg, gv = r_ref[:, sl], hg_ref[:, sl], gh_ref[:, sl]
            dout = dmix[:, AW + HGRN_DIM * h:AW + HGRN_DIM * (h + 1)]
            sg = _sigmoid(hg)
            drv, dgh = _norm_bwd(rv, gv, dout * (hg * sg))
            dr_ref[:, sl] = drv
            dgh_ref[:, sl] += jnp.sum(dgh, axis=0, keepdims=True)
            dhg_ref[:, sl] = (dout * (rv * _rstd(rv) * gv) * (sg * (1.0 + hg * (1.0 - sg)))).astype(BF16)

    row = lambda w, j=0: pl.BlockSpec((tm, w), lambda i: (i, j))
    vec = lambda w: pl.BlockSpec((1, w), lambda i: (0, 0))
    return pl.pallas_call(
        body, name=name, grid=(S // tm,),
        in_specs=[row(D), _resident(w_out.shape), row(AW), row(W), row(W, 3), vec(AW), vec(W)],
        out_specs=[row(AW), row(AW), row(W), row(W), vec(AW), vec(W)],
        out_shape=[jax.ShapeDtypeStruct((S, AW), F32), jax.ShapeDtypeStruct((S, AW), F32),
                   jax.ShapeDtypeStruct((S, W), F32), jax.ShapeDtypeStruct((S, W), BF16),
                   jax.ShapeDtypeStruct((1, AW), F32), jax.ShapeDtypeStruct((1, W), F32)],
        compiler_params=_cparams("arbitrary"),
    )(dh1b, w_out, attn, rec, proj_h, g_attn, g_hgrn)


def _conv_act(g, g1, g2, w_ref, b_ref):
    c = b_ref[...] + w_ref[0:1, :] * g2 + w_ref[1:2, :] * g1 + w_ref[2:3, :] * g
    return c, 0.5 * (1.0 + lax.erf(c * (2.0 ** -0.5)))


def _shift_down(g, halo, row):
    g1 = jnp.where(row == 0, halo[7:8], pltpu.roll(g, 1, 0))
    g2 = jnp.where(row == 0, halo[6:7], jnp.where(row == 1, halo[7:8], pltpu.roll(g, 2, 0)))
    return g1, g2


def _shift_up(x, halo, row):
    n = x.shape[0]
    x1 = jnp.where(row == n - 1, halo[0:1], pltpu.roll(x, n - 1, 0))
    x2 = jnp.where(row == n - 2, halo[0:1], jnp.where(row == n - 1, halo[1:2], pltpu.roll(x, n - 2, 0)))
    return x1, x2


def _up_glu(u, wt_up, conv_w, conv_b, *, name, tm=1024, tn=256):
    S, D = u.shape
    F = wt_up.shape[0] // 2
    nf = F // tn

    def body(u_ref, wg_ref, wv_ref, cw_ref, cb_ref, g_ref, v_ref, a_ref, halo_ref):
        i, j = pl.program_id(0), pl.program_id(1)

        @pl.when(i == 0)
        def _():
            halo_ref[j] = jnp.zeros((SUBLANES, tn), F32)

        uv = u_ref[...]
        g, v = _nt(uv, wg_ref[...]), _nt(uv, wv_ref[...])
        row = lax.broadcasted_iota(jnp.int32, (tm, tn), 0)
        g1, g2 = _shift_down(g, halo_ref[j], row)
        c, cdf = _conv_act(g, g1, g2, cw_ref, cb_ref)
        a_ref[...] = (c * cdf * v).astype(BF16)
        g_ref[...] = g.astype(BF16)
        v_ref[...] = v.astype(BF16)
        halo_ref[j] = g[tm - SUBLANES:, :]

    col = pl.BlockSpec((tm, tn), lambda i, j: (i, j))
    out = jax.ShapeDtypeStruct((S, F), BF16)
    return pl.pallas_call(
        body, name=name, grid=(S // tm, nf),
        in_specs=[pl.BlockSpec((tm, D), lambda i, j: (i, 0)), pl.BlockSpec((tn, D), lambda i, j: (j, 0)),
                  pl.BlockSpec((tn, D), lambda i, j: (j + nf, 0)), pl.BlockSpec((3, tn), lambda i, j: (0, j)),
                  pl.BlockSpec((1, tn), lambda i, j: (0, j))],
        out_specs=[col, col, col], out_shape=[out, out, out],
        scratch_shapes=[pltpu.VMEM((nf, SUBLANES, tn), F32)], compiler_params=_cparams("arbitrary", "arbitrary"),
    )(u, wt_up, wt_up, conv_w, conv_b)


def _dact_glu_bwd(dh2b, w_down, gate, val, conv_w, conv_b, *, name, tm=1024, tn=256):
    S, D = dh2b.shape
    F = gate.shape[1]
    nf, ni = F // tn, S // tm
    hb = tm // SUBLANES

    def body(dh_ref, wd_ref, g_ref, gh_ref, v_ref, cw_ref, cb_ref, dg_ref, dv_ref, dcw_ref, dcb_ref, halo_ref, acc_ref):
        i, j = pl.program_id(0), pl.program_id(1)

        @pl.when(i == 0)
        def _():
            halo_ref[j] = jnp.zeros((SUBLANES, tn), F32)
            acc_ref[j] = jnp.zeros((SUBLANES, tn), F32)

        g = g_ref[...].astype(F32)
        before = jnp.where(i < ni - 1, gh_ref[...].astype(F32), 0.0)
        row = lax.broadcasted_iota(jnp.int32, (tm, tn), 0)
        g1, g2 = _shift_down(g, before[SUBLANES:], row)
        c, cdf = _conv_act(g, g1, g2, cw_ref, cb_ref)
        da = _nt(dh_ref[...], wd_ref[...])
        dv_ref[...] = (da * (c * cdf)).astype(BF16)
        pdf = jnp.exp(-0.5 * c * c) * (1.0 / (2.0 * jnp.pi) ** 0.5)
        dc = da * v_ref[...].astype(F32) * (cdf + c * pdf)
        d1, d2 = _shift_up(dc, halo_ref[j], row)
        dg_ref[...] = (cw_ref[2:3, :] * dc + cw_ref[1:2, :] * d1 + cw_ref[0:1, :] * d2).astype(BF16)
        halo_ref[j] = dc[:SUBLANES, :]
        for k, t in enumerate((dc * g2, dc * g1, dc * g, dc)):
            acc_ref[j, k:k + 1, :] += jnp.sum(t, axis=0, keepdims=True)

        @pl.when((i == ni - 1) & (j == nf - 1))
        def _():
            for jj in range(nf):
                dcw_ref[:, jj * tn:(jj + 1) * tn] = acc_ref[jj, 0:3, :]
                dcb_ref[:, jj * tn:(jj + 1) * tn] = acc_ref[jj, 3:4, :]

    tile = pl.BlockSpec((tm, tn), lambda i, j: (ni - 1 - i, j))
    return pl.pallas_call(
        body, name=name, grid=(ni, nf),
        in_specs=[pl.BlockSpec((tm, D), lambda i, j: (ni - 1 - i, 0)), pl.BlockSpec((tn, D), lambda i, j: (j, 0)),
                  tile, pl.BlockSpec((SUBLANES * 2, tn), lambda i, j: (jnp.maximum((ni - 1 - i) * (hb // 2) - 1, 0), j)),
                  tile, pl.BlockSpec((3, tn), lambda i, j: (0, j)), pl.BlockSpec((1, tn), lambda i, j: (0, j))],
        out_specs=[tile, tile, pl.BlockSpec((3, F), lambda i, j: (0, 0)), pl.BlockSpec((1, F), lambda i, j: (0, 0))],
        out_shape=[jax.ShapeDtypeStruct((S, F), BF16), jax.ShapeDtypeStruct((S, F), BF16),
                   jax.ShapeDtypeStruct((3, F), F32), jax.ShapeDtypeStruct((1, F), F32)],
        scratch_shapes=[pltpu.VMEM((nf, SUBLANES, tn), F32), pltpu.VMEM((nf, SUBLANES, tn), F32)],
        compiler_params=_cparams("arbitrary", "arbitrary"),
    )(dh2b, w_down, gate, gate, val, conv_w, conv_b)


def _down_loss(act, w_down, h1, g, target, *, name, tm=512):
    S, F = act.shape
    D = h1.shape[1]

    def body(a_ref, w_ref, h_ref, g_ref, t_ref, dh_ref, dhb_ref, dg_ref, loss_ref):
        first = pl.program_id(0) == 0
        h2 = h_ref[...] + _nn(a_ref[...], w_ref[...])
        gv = g_ref[...]
        r = _rstd(h2)
        xh = h2 * r
        err = xh * gv - t_ref[...]
        part_loss = 0.5 * jnp.sum(jnp.mean(err * err, axis=-1, keepdims=True), axis=0, keepdims=True)
        dy = err * (1.0 / D)
        dxh = dy * gv
        dh = r * (dxh - xh * jnp.mean(dxh * xh, axis=-1, keepdims=True))
        dh_ref[...] = dh
        dhb_ref[...] = dh.astype(BF16)
        _accumulate(dg_ref, jnp.sum(dy * xh, axis=0, keepdims=True), first)
        _accumulate(loss_ref, jnp.broadcast_to(part_loss, (1, LANES)), first)

    row = lambda w: pl.BlockSpec((tm, w), lambda i: (i, 0))
    vec = lambda w: pl.BlockSpec((1, w), lambda i: (0, 0))
    return pl.pallas_call(
        body, name=name, grid=(S // tm,), in_specs=[row(F), _resident(w_down.shape), row(D), vec(D), row(D)],
        out_specs=[row(D), row(D), vec(D), vec(LANES)],
        out_shape=[jax.ShapeDtypeStruct((S, D), F32), jax.ShapeDtypeStruct((S, D), BF16),
                   jax.ShapeDtypeStruct((1, D), F32), jax.ShapeDtypeStruct((1, LANES), F32)],
        compiler_params=_cparams("arbitrary"),
    )(act, w_down, h1, g, target)


def _grad_norm_input(pieces, ws, x, g, add, *, name, tm=512):
    S, D = x.shape
    widths = [p.shape[1] for p in pieces]
    n, nw = len(pieces), len(ws)
    where, wi, off = [], 0, ws[0][1]
    for wd in widths:
        if off == ws[wi][0].shape[0]:
            wi, off = wi + 1, ws[wi + 1][1]
        where.append((wi, off))
        off += wd
    ws = [w for w, _ in ws]

    def body(*refs):
        p_refs, w_refs = refs[:n], refs[n:n + nw]
        x_ref, g_ref, add_ref, dx_ref, dxb_ref, dg_ref = refs[n + nw:]
        halves = _row_halves(tm)
        du = []
        for rows in halves:
            terms = [_nn(p_refs[k][rows, :], w_refs[wi][off:off + widths[k], :]) for k, (wi, off) in enumerate(where)]
            du.append(sum(terms[1:], terms[0]))
        dg_sum = None
        for rows, duh in zip(halves, du):
            dx, dg = _norm_bwd(x_ref[rows, :], g_ref[...], duh)
            dx = add_ref[rows, :] + dx
            dx_ref[rows, :] = dx
            dxb_ref[rows, :] = dx.astype(BF16)
            part = jnp.sum(dg, axis=0, keepdims=True)
            dg_sum = part if dg_sum is None else dg_sum + part
        _accumulate(dg_ref, dg_sum, pl.program_id(0) == 0)

    row = lambda w_: pl.BlockSpec((tm, w_), lambda i: (i, 0))
    vec = pl.BlockSpec((1, D), lambda i: (0, 0))
    return pl.pallas_call(
        body, name=name, grid=(S // tm,),
        in_specs=[row(wd) for wd in widths] + [_resident(w.shape) for w in ws] + [row(D), vec, row(D)],
        out_specs=[row(D), row(D), vec],
        out_shape=[jax.ShapeDtypeStruct((S, D), F32), jax.ShapeDtypeStruct((S, D), BF16),
                   jax.ShapeDtypeStruct((1, D), F32)],
        compiler_params=_cparams("arbitrary"),
    )(*pieces, *ws, x, g, add)


def _rows(a):
    return a.reshape(-1, a.shape[-1])


def _row_tile(rows, cols, itemsize=4, budget=1 << 20):
    t = rows
    while t % 32 == 0 and t * cols * itemsize > budget:
        t //= 2
    return t


def _sum_cast(arrs, out_dtype, *, name):
    shape = arrs[0].shape
    flat = [_rows(a) for a in arrs]
    R, C = flat[0].shape
    tr = _row_tile(R, C)

    def body(*refs):
        acc = refs[0][...].astype(F32)
        for r in refs[1:-1]:
            acc = acc + r[...].astype(F32)
        refs[-1][...] = acc.astype(out_dtype)

    spec = pl.BlockSpec((tr, C), lambda i: (i, 0))
    return pl.pallas_call(
        body, name=name, grid=(R // tr,), in_specs=[spec] * len(flat), out_specs=spec,
        out_shape=jax.ShapeDtypeStruct((R, C), out_dtype), compiler_params=_cparams("parallel"),
    )(*flat).reshape(shape)


def _adamw(parts, w, m, v, *, name):
    shape = w.shape
    w2, m2, v2 = _rows(w), _rows(m), _rows(v)
    R, C = w2.shape
    parts = [p.reshape(-1, R, C) for p in parts]
    tr = _row_tile(R, C)
    np_ = len(parts)
    c1, c2 = 1.0 - ADAM_B1 ** ADAM_STEP, 1.0 - ADAM_B2 ** ADAM_STEP

    def body(*refs):
        terms = [(r, k) for r in refs[:np_] for k in range(r.shape[0])]
        g = terms[0][0][terms[0][1]].astype(F32)
        for r, k in terms[1:]:
            g = g + r[k].astype(F32)
        w_ref, m_ref, v_ref, g_out, d_out, m_out, v_out = refs[np_:]
        mn = ADAM_B1 * m_ref[...] + (1.0 - ADAM_B1) * g
        vn = ADAM_B2 * v_ref[...] + (1.0 - ADAM_B2) * (g * g)
        g_out[...] = g
        d_out[...] = -ADAM_LR * ((mn / c1) / (jnp.sqrt(vn / c2) + ADAM_EPS) + ADAM_WD * w_ref[...])
        m_out[...] = mn
        v_out[...] = vn

    spec = pl.BlockSpec((tr, C), lambda i: (i, 0))
    out = jax.ShapeDtypeStruct((R, C), F32)
    stacks = [pl.BlockSpec((p.shape[0], tr, C), lambda i: (0, i, 0)) for p in parts]
    res = pl.pallas_call(
        body, name=name, grid=(R // tr,), in_specs=stacks + [spec] * 3, out_specs=[spec] * 4,
        out_shape=[out] * 4, compiler_params=_cparams("parallel"),
    )(*parts, w2, m2, v2)
    return [r.reshape(shape) for r in res]


def _adamw_packed(stack, widths, params, *, name):
    c1, c2 = 1.0 - ADAM_B1 ** ADAM_STEP, 1.0 - ADAM_B2 ** ADAM_STEP
    k = stack.shape[0]
    flat = [None if p is None else [_rows(a) for a in p] for p in params]
    n_in = sum(3 for p in flat if p is not None)

    def body(*refs):
        s_ref, ins, outs = refs[0], list(refs[1:1 + n_in]), list(refs[1 + n_in:])
        off = 0
        for width, p in zip(widths, flat):
            rows = 1 if p is None else p[0].shape[0]
            cols = width // rows
            w_ref, m_ref, v_ref = (None, None, None) if p is None else (ins.pop(0), ins.pop(0), ins.pop(0))
            o_refs = [outs.pop(0) for _ in range(1 if p is None else 4)]
            for r in range(rows):
                seg = slice(off + r * cols, off + (r + 1) * cols)
                g = s_ref[0, :, seg]
                for j in range(1, k):
                    g = g + s_ref[j, :, seg]
                o_refs[0][r:r + 1, :] = g
                if p is not None:
                    row = slice(r, r + 1)
                    mn = ADAM_B1 * m_ref[row, :] + (1.0 - ADAM_B1) * g
                    vn = ADAM_B2 * v_ref[row, :] + (1.0 - ADAM_B2) * (g * g)
                    o_refs[1][row, :] = -ADAM_LR * ((mn / c1) / (jnp.sqrt(vn / c2) + ADAM_EPS) + ADAM_WD * w_ref[row, :])
                    o_refs[2][row, :] = mn
                    o_refs[3][row, :] = vn
            off += width

    operands, out_shape = [stack], []
    for width, p in zip(widths, flat):
        if p is None:
            out_shape.append(jax.ShapeDtypeStruct((1, width), F32))
        else:
            operands += p
            out_shape += [jax.ShapeDtypeStruct(p[0].shape, F32)] * 4
    res = list(pl.pallas_call(body, name=name, out_shape=out_shape)(*operands))
    out = []
    for p, orig in zip(flat, params):
        n = 1 if p is None else 4
        out.append([r if orig is None else r.reshape(orig[0].shape) for r in res[:n]])
        res = res[n:]
    return out


def _coords():
    return lax.axis_index("x"), lax.axis_index("y"), lax.axis_index("c")


def _all_gather(shards, *, name):
    n = len(shards)

    def body(*refs):
        x_refs, out_refs = refs[:n], refs[n:2 * n]
        send_sems, recv_sems, local_sems = refs[2 * n:]
        x, y, c = _coords()
        me, sibling = (x, y, c), (x, y, 1 - c)
        chips = [(1 - x, y), (x, 1 - y), (1 - x, 1 - y)]

        def slot(a, dev):
            return out_refs[a].at[4 * dev[0] + 2 * dev[1] + dev[2]]

        def copy(a, k, block, to, src=None):
            return pltpu.make_async_remote_copy(
                src_ref=slot(a, block) if src is None else src, dst_ref=slot(a, block),
                send_sem=send_sems.at[7 * a + k], recv_sem=recv_sems.at[7 * a + k], device_id=to, device_id_type=MESH)

        mine = [pltpu.make_async_copy(x_refs[a], slot(a, me), local_sems.at[a]) for a in range(n)]
        for cp in mine:
            cp.start()
        first = []
        for a in range(n):
            first.append(copy(a, 0, me, sibling, src=x_refs[a]))
            first += [copy(a, 1 + j, me, (*chip, c), src=x_refs[a]) for j, chip in enumerate(chips)]
        for cp in first:
            cp.start()
        passed = []
        for j, chip in enumerate(chips):
            for a in range(n):
                copy(a, 1 + j, (*chip, c), me).wait_recv()
                fwd = copy(a, 4 + j, (*chip, c), sibling)
                fwd.start()
                passed.append(fwd)
        for a in range(n):
            copy(a, 0, sibling, me).wait_recv()
            for j, chip in enumerate(chips):
                copy(a, 4 + j, (*chip, 1 - c), me).wait_recv()
        for cp in first + passed:
            cp.wait_send()
        for cp in mine:
            cp.wait()

    return pl.pallas_call(
        body, name=name, in_specs=[HBM] * n, out_specs=[HBM] * n,
        out_shape=[jax.ShapeDtypeStruct((N_DEV, *s.shape), s.dtype) for s in shards],
        scratch_shapes=[pltpu.SemaphoreType.DMA((7 * n,)), pltpu.SemaphoreType.DMA((7 * n,)),
                        pltpu.SemaphoreType.DMA((n,))],
    )(*shards)


def _flip_y(x, y, c):
    return (x, 1 - y, c)


def _flip_x(x, y, c):
    return (1 - x, y, c)


def _flip_xy(x, y, c):
    return (1 - x, 1 - y, c)


SEM = pl.BlockSpec(memory_space=pltpu.SEMAPHORE)
SIDE_EFFECT = pltpu.SideEffectType.DATAFLOW_SIDE_EFFECTING


def _in_hbm(a):
    return pltpu.with_memory_space_constraint(a, pltpu.HBM)


def _copies_start(srcs, lands, plan, n_copies, *, name, after=None):
    ns, nl = len(srcs), len(lands)
    extra = [] if after is None else [after]

    def body(*refs):
        src_refs, land_refs = refs[:ns], refs[ns:ns + nl]
        send_sems, recv_sems = refs[ns + nl + len(extra):ns + nl + len(extra) + 2]
        token = refs[-1]
        for k, (src, dst, peer, _) in enumerate(plan(src_refs, land_refs, *_coords())):
            pltpu.make_async_remote_copy(src_ref=src, dst_ref=dst, send_sem=send_sems.at[k], recv_sem=recv_sems.at[k],
                                         device_id=peer, device_id_type=MESH).start()
        token[...] = jnp.zeros_like(token)

    bufs = [*srcs, *lands]
    res = pl.pallas_call(
        body, name=name, in_specs=[HBM] * (ns + nl) + [pl.BlockSpec(memory_space=pl.ANY)] * len(extra),
        out_specs=(SEM, SEM, *[HBM] * (ns + nl), pl.BlockSpec(memory_space=pltpu.VMEM)),
        out_shape=(pltpu.SemaphoreType.DMA((n_copies,)), pltpu.SemaphoreType.DMA((n_copies,)),
                   *[pltpu.HBM(b.shape, b.dtype) for b in bufs], jax.ShapeDtypeStruct((SUBLANES, LANES), F32)),
        input_output_aliases={i: 2 + i for i in range(ns + nl)},
        compiler_params=pltpu.CompilerParams(has_side_effects=SIDE_EFFECT),
    )(*[_in_hbm(b) for b in bufs], *extra)
    return res[0], res[1], list(res[2:2 + ns]), list(res[2 + ns:2 + ns + nl]), res[-1]


def _copies_wait(started, plan, after, *, name):
    send_sems, recv_sems, srcs, lands, _ = started
    ns, nl = len(srcs), len(lands)

    def body(*refs):
        src_refs, land_refs = refs[:ns], refs[ns:ns + nl]
        send_sems, recv_sems = refs[ns + nl:ns + nl + 2]
        for k, (src, dst, peer, here) in enumerate(plan(src_refs, land_refs, *_coords())):
            pltpu.make_async_remote_copy(src_ref=src, dst_ref=dst, send_sem=send_sems.at[k], recv_sem=recv_sems.at[k],
                                         device_id=peer, device_id_type=MESH).wait_send()
            pltpu.make_async_remote_copy(src_ref=src, dst_ref=here, send_sem=send_sems.at[k], recv_sem=recv_sems.at[k],
                                         device_id=peer, device_id_type=MESH).wait_recv()

    bufs = [*srcs, *lands]
    res = pl.pallas_call(
        body, name=name, in_specs=[HBM] * (ns + nl) + [SEM, SEM, pl.BlockSpec(memory_space=pl.ANY)],
        out_specs=[HBM] * (ns + nl), out_shape=[pltpu.HBM(b.shape, b.dtype) for b in bufs],
        input_output_aliases={i: i for i in range(ns + nl)},
        compiler_params=pltpu.CompilerParams(has_side_effects=SIDE_EFFECT),
    )(*bufs, send_sems, recv_sems, after)
    return list(res[ns:])


def _dev_index(dev):
    return 4 * dev[0] + 2 * dev[1] + dev[2]


def _ag_chips_plan(src_refs, land_refs, x, y, c):
    me = _dev_index((x, y, c))
    return [(src, land.at[me], peer, land.at[_dev_index(peer)])
            for src, land in zip(src_refs, land_refs) for peer in (_flip_y(x, y, c), _flip_x(x, y, c), _flip_xy(x, y, c))]


def _ag_sibling_plan(src_refs, land_refs, x, y, c):
    chips = [(x, y), (x, 1 - y), (1 - x, y), (1 - x, 1 - y)]
    return [(land.at[_dev_index((*chip, c))], land.at[_dev_index((*chip, c))], (x, y, 1 - c),
             land.at[_dev_index((*chip, 1 - c))]) for land in land_refs for chip in chips]


def _ag_direct_plan(src_refs, land_refs, x, y, c):
    me = _dev_index((x, y, c))
    plan = []
    for src, land in zip(src_refs, land_refs):
        for m in range(1, N_DEV):
            peer = (x + (m >> 2) * (1 - 2 * x), y + ((m >> 1) & 1) * (1 - 2 * y), c + (m & 1) * (1 - 2 * c))
            plan.append((src, land.at[me], peer, land.at[_dev_index(peer)]))
    return plan


def _rs_direct_plan(src_refs, land_refs, x, y, c):
    plan = []
    for src, land in zip(src_refs, land_refs):
        for m in range(1, N_DEV):
            peer = (x + (m >> 2) * (1 - 2 * x), y + ((m >> 1) & 1) * (1 - 2 * y), c + (m & 1) * (1 - 2 * c))
            plan.append((src.at[_dev_index(peer)], land.at[m - 1], peer, land.at[m - 1]))
    return plan


def _rs_start(grads, me, *, name, after=None):
    own = [lax.dynamic_index_in_dim(g, me, 0, keepdims=False) for g in grads]
    lands = [lax.empty((N_DEV - 1, *g.shape[1:]), g.dtype) for g in grads]
    return _copies_start(grads, lands, _rs_direct_plan, (N_DEV - 1) * len(grads), name=name, after=after), own


def _rs_finish(started, after, *, name):
    handle, own = started
    got = _copies_wait(handle, _rs_direct_plan, after, name=name)
    return [[o, land] for o, land in zip(own, got)]


def _gathered_cols(w8):
    return w8.transpose(1, 0, 2).reshape(w8.shape[1], -1)


def _pair_major(wt):
    return wt.reshape(3, ATTN_W // LANES, LANES, -1).transpose(1, 0, 2, 3).reshape(3 * ATTN_W, -1)


def kernel(x, norm1_g, w_in, attn_norm_g, hgrn_norm_g, hgrn_lb_logits, w_out, norm2_g, w_up, conv_w, conv_b, w_down, final_norm_g, loss_target, m_norm1_g, m_w_in, m_attn_norm_g, m_hgrn_norm_g, m_hgrn_lb_logits, m_w_out, m_norm2_g, m_w_up, m_conv_w, m_conv_b, m_w_down, m_final_norm_g, v_norm1_g, v_w_in, v_attn_norm_g, v_hgrn_norm_g, v_hgrn_lb_logits, v_w_out, v_norm2_g, v_w_up, v_conv_w, v_conv_b, v_w_down, v_final_norm_g):
    xs, target = x[0], loss_target[0]
    S, D = xs.shape
    NA = 3 * ATTN_W
    fng = final_norm_g.reshape(1, D)

    t = lambda a: a[0].T
    casts = [_sum_cast([w], BF16, name=f"cast_{nm}") for nm, w in
             (("w_in", t(w_in)), ("w_out", w_out[0]), ("w_up", t(w_up)), ("w_down", w_down[0]))]
    me = _dev_index(_coords())
    (g_in,) = _all_gather(casts[:1], name="ag_w_in")
    later = casts[1:] + [conv_w[0]]
    ag1 = _copies_start(later, [lax.empty((N_DEV, *s.shape), s.dtype) for s in later], _ag_chips_plan,
                        3 * len(later), name="ag_chips_start", after=g_in)
    wi = g_in.reshape(-1, D)
    wi_a = _pair_major(wi[:NA])

    u1, proj_a = _proj_attn(xs, norm1_g + ag1[4][0, 0], wi_a, name="proj_attn")
    proj_h = _mm_nt(u1, wi, NA, wi.shape[0] - NA, name="proj_hgrn")
    attn, lse = _attn_fwd(proj_a, name="attn_fwd")
    lands = _copies_wait(ag1, _ag_chips_plan, attn, name="ag_chips_wait")
    lands = [lax.dynamic_update_index_in_dim(l, s, me, 0) for l, s in zip(lands, later)]
    ag2 = _copies_start([], lands, _ag_sibling_plan, 4 * len(later), name="ag_sibling_start")
    rec, states = _hgrn_fwd(proj_h, hgrn_lb_logits + ag2[4][0, 0], name="hgrn_fwd")
    g_out, g_up, g_down, g_cw = _copies_wait(ag2, _ag_sibling_plan, rec, name="ag_sibling_wait")
    wo = g_out.reshape(-1, D)
    wu = g_up.reshape(-1, D)
    wd = g_down.reshape(-1, D)
    cw = _gathered_cols(g_cw)
    h1, u2, mixed = _out_proj(attn, rec, proj_h, xs, attn_norm_g, hgrn_norm_g, norm2_g, wo, name="out_proj")
    gate, val, act = _up_glu(u2, wu, cw, conv_b, name="up_glu")
    dh2, dh2b, d_fng, loss_part = _down_loss(act, wd, h1, fng, target, name="down_loss")

    dgate, dval, d_cw, d_cb = _dact_glu_bwd(dh2b, wd, gate, val, cw, conv_b, name="dact_glu_bwd")
    dw_down = _mm_tn(act, dh2b, tm=256, name="dw_down")
    dh1, dh1b, d_n2g = _grad_norm_input([dgate, dval], [(wu, 0)], h1, norm2_g, dh2, name="du2_norm2_bwd")
    F = dgate.shape[1]
    dw_up = _mm_tn(dgate, u2, tm=256, rows=2 * F, name="dw_up_gate")
    dw_up = _mm_tn(dval, u2, tm=256, rows=2 * F, row_block=lambda i: i + F // 256, into=dw_up, name="dw_up_val")
    rs_ffn = _rs_start([dw_down.reshape(N_DEV, -1, D), dw_up.reshape(N_DEV, -1, D)], me, name="rs_ffn_start")
    dattn, delta, drec, dhg, d_ang, d_hng = _dmix_post_bwd(dh1b, wo, attn, rec, proj_h, attn_norm_g + rs_ffn[0][4][0, 0],
                                                          hgrn_norm_g, name="dmix_post_bwd")
    dw_out = _mm_tn(mixed, dh1b, name="dw_out")
    rs_out = _rs_start([dw_out.reshape(N_DEV, -1, D)], me, name="rs_out_start")
    dproj_h, d_lbl = _hgrn_bwd(proj_h, hgrn_lb_logits + rs_out[0][4][0, 0], states, drec, name="hgrn_bwd")
    small = [("loss", loss_part, None, None, None),
             ("attn_norm_g", d_ang, attn_norm_g, m_attn_norm_g, v_attn_norm_g),
             ("hgrn_norm_g", d_hng, hgrn_norm_g, m_hgrn_norm_g, v_hgrn_norm_g),
             ("hgrn_lb_logits", d_lbl, hgrn_lb_logits, m_hgrn_lb_logits, v_hgrn_lb_logits),
             ("norm2_g", d_n2g, norm2_g, m_norm2_g, v_norm2_g),
             ("conv_b", d_cb, conv_b, m_conv_b, v_conv_b),
             ("final_norm_g", d_fng, final_norm_g, m_final_norm_g, v_final_norm_g)]
    pack = lambda arrs: jnp.concatenate([a.reshape(1, -1) for a in arrs], axis=1)
    small_own = [pack([s[1] for s in small]), d_cw]
    ag_small = _copies_start(small_own, [lax.empty((N_DEV, *s.shape), s.dtype) for s in small_own], _ag_direct_plan,
                             (N_DEV - 1) * len(small_own), name="ag_small_start")
    dproj_a = _attn_bwd(proj_a, dattn, lse, delta, name="attn_bwd")
    pairs = ATTN_W // LANES
    dw_in = _mm_tn(dproj_a, u1, tm=LANES, rows=wi.shape[0], row_block=lambda i: pairs * (i % 3) + i // 3,
                   name="dw_in_attn")
    dw_in = _mm_tn(dproj_h, u1, tm=256, rows=wi.shape[0], row_block=lambda i: i + NA // 256, into=dw_in,
                   name="dw_in_hgrn")
    dw_in = _mm_tn(dhg, u1, tm=256, rows=wi.shape[0], row_block=lambda i: i + (NA + 3 * HGRN_W) // 256, into=dw_in,
                   name="dw_in_gate")
    rs_in = _rs_start([dw_in.reshape(N_DEV, -1, D)], me, name="rs_in_start", after=ag_small[4])
    grad_x, _, d_n1g = _grad_norm_input([dproj_a, dproj_h, dhg], [(wi_a, 0), (wi, NA)], xs,
                                        norm1_g + rs_in[0][4][0, 0], dh1, name="du1_norm1_bwd")

    res = {}

    def update(nm, parts, w, m, v, transposed=False):
        if transposed:
            res[nm] = [r.T[None] for r in _adamw(parts, t(w), t(m), t(v), name=f"adamw_{nm}")]
        else:
            res[nm] = _adamw(parts, w, m, v, name=f"adamw_{nm}")

    g_down, g_up = _rs_finish(rs_ffn, grad_x, name="rs_ffn_wait")
    update("w_down", g_down, w_down, m_w_down, v_w_down)
    update("w_up", g_up, w_up, m_w_up, v_w_up, transposed=True)
    (g_out,) = _rs_finish(rs_out, grad_x, name="rs_out_wait")
    update("w_out", g_out, w_out, m_w_out, v_w_out)
    (g_in,) = _rs_finish(rs_in, res["w_up"][1], name="rs_in_wait")
    update("w_in", g_in, w_in, m_w_in, v_w_in, transposed=True)

    g_small, g_dcw = [lax.dynamic_update_index_in_dim(l, s, me, 0)
                      for l, s in zip(_copies_wait(ag_small, _ag_direct_plan, grad_x, name="ag_small_wait"), small_own)]
    sm = _adamw_packed(g_small, [s[1].size for s in small], [None if s[2] is None else s[2:] for s in small],
                       name="adamw_small")
    for (nm, *_), r in zip(small, sm):
        res[nm] = r
    ncw = conv_w.shape[-1]
    mine_cw = lax.dynamic_slice_in_dim(g_dcw, me * ncw, ncw, axis=2)
    res["conv_w"] = _adamw([mine_cw], conv_w, m_conv_w, v_conv_w, name="adamw_conv_w")
    late, _ = lax.optimization_barrier((d_n1g, res["w_in"][1]))
    update("norm1_g", _all_gather([late], name="ag_norm1_grad"), norm1_g, m_norm1_g, v_norm1_g)

    loss = res["loss"][0][0, 0]
    order = ["norm1_g", "w_in", "attn_norm_g", "hgrn_norm_g", "hgrn_lb_logits", "w_out", "norm2_g", "w_up",
             "conv_w", "conv_b", "w_down", "final_norm_g"]
    return (loss, grad_x[None], *[res[nm][0] for nm in order], *[res[nm][1] for nm in order],
            *[res[nm][2] for nm in order], *[res[nm][3] for nm in order])
```

```python
import jax
import jax.numpy as jnp
from jax import lax
from jax.experimental import pallas as pl
from jax.experimental.pallas import tpu as pltpu

F32, BF16 = jnp.float32, jnp.bfloat16
NORM_EPS = 1e-6
ATTN_HEADS, HEAD_DIM, ATTN_BLOCK = 8, 64, 128
DILATIONS = (1, 4, 16)
ATTN_SCALE = HEAD_DIM ** -0.5
ATTN_W = ATTN_HEADS * HEAD_DIM
HGRN_HEADS, HGRN_DIM, HGRN_CHUNK = 4, 128, 64
HGRN_W = HGRN_HEADS * HGRN_DIM
ADAM_LR, ADAM_B1, ADAM_B2, ADAM_EPS, ADAM_WD, ADAM_STEP = 0.001, 0.9, 0.999, 1e-08, 0.01, 10
LANES, SUBLANES = 128, 8
VMEM_LIMIT_BYTES = 56 * 1024 * 1024
N_DEV = 8
MESH = pl.DeviceIdType.MESH
HBM = pl.BlockSpec(memory_space=pltpu.HBM)
HIGHEST = lax.Precision.HIGHEST


def _cparams(*sem):
    return pltpu.CompilerParams(dimension_semantics=sem, vmem_limit_bytes=VMEM_LIMIT_BYTES)


def _tile(n, pref):
    if n <= pref:
        return n
    t = (pref // LANES) * LANES
    while n % t:
        t -= LANES
    return t


def _resident(shape):
    return pl.BlockSpec(shape, lambda *_: (0,) * len(shape), pipeline_mode=pl.Buffered(1))


def _dot(a, b, dims, precision=None):
    return lax.dot_general(a, b, (dims, ((), ())), precision=precision, preferred_element_type=F32)


def _nn(a, b, precision=None):
    return _dot(a, b, ((1,), (0,)), precision)


def _nt(a, b):
    return _dot(a, b, ((1,), (1,)))


def _tn(a, b):
    return _dot(a, b, ((0,), (0,)))


def _sigmoid(x):
    return 1.0 / (1.0 + jnp.exp(-x))


def _rstd(x):
    return lax.rsqrt(jnp.mean(x * x, axis=-1, keepdims=True) + NORM_EPS)


def _norm_bwd(x, g, du):
    r = _rstd(x)
    xh = x * r
    dxh = du * g
    return r * (dxh - xh * jnp.mean(dxh * xh, axis=-1, keepdims=True)), du * xh


def _row_halves(tm):
    return [pl.ds(0, tm // 2), pl.ds(tm // 2, tm // 2)]


def _accumulate(ref, part, first):
    @pl.when(first)
    def _():
        ref[...] = part

    @pl.when(jnp.logical_not(first))
    def _():
        ref[...] += part


def _mm_nt(a, bt, row0, n, *, name, out_dtype=F32, tm=1024, tn=512):
    M, K = a.shape
    tm, tn = _tile(M, tm), _tile(n, tn)
    j0 = row0 // tn

    def body(a_ref, b_ref, o_ref):
        o_ref[...] = _nt(a_ref[...], b_ref[...]).astype(out_dtype)

    return pl.pallas_call(
        body, name=name, grid=(M // tm, n // tn),
        in_specs=[pl.BlockSpec((tm, K), lambda i, j: (i, 0)), pl.BlockSpec((tn, K), lambda i, j: (j + j0, 0))],
        out_specs=pl.BlockSpec((tm, tn), lambda i, j: (i, j)), out_shape=jax.ShapeDtypeStruct((M, n), out_dtype),
        compiler_params=_cparams("parallel", "parallel"),
    )(a, bt)


def _mm_tn(x, dy, *, name, tm=512, tn=1024, rows=None, row_block=None, into=None):
    S, M = x.shape
    N = dy.shape[1]
    tm, tn = _tile(M, tm), _tile(N, tn)
    row_block = row_block or (lambda i: i)

    def body(x_ref, dy_ref, *rest):
        o_ref, xt_ref = rest[-2:]

        @pl.when(pl.program_id(1) == 0)
        def _():
            xt_ref[...] = x_ref[...].T

        o_ref[...] = _nn(xt_ref[...], dy_ref[...]).astype(BF16)

    operands = [x, dy] + ([] if into is None else [into])
    return pl.pallas_call(
        body, name=name, grid=(M // tm, N // tn),
        in_specs=[pl.BlockSpec((S, tm), lambda i, j: (0, i)), pl.BlockSpec((S, tn), lambda i, j: (0, j))]
        + ([] if into is None else [pl.BlockSpec(memory_space=pl.ANY)]),
        out_specs=pl.BlockSpec((tm, tn), lambda i, j: (row_block(i), j)),
        out_shape=jax.ShapeDtypeStruct((rows or M, N), BF16),
        input_output_aliases={} if into is None else {2: 0},
        scratch_shapes=[pltpu.VMEM((tm, S), BF16)], compiler_params=_cparams("parallel", "arbitrary"),
    )(*operands)


def _proj_attn(x, g, wt, *, name, tm=1024, tn=512):
    S, D = x.shape
    N = wt.shape[0]

    def body(x_ref, g_ref, w_ref, u_ref, o_ref):
        @pl.when(pl.program_id(1) == 0)
        def _():
            xv = x_ref[...]
            u_ref[...] = (xv * _rstd(xv) * g_ref[...]).astype(BF16)

        o_ref[...] = _nt(u_ref[...], w_ref[...]).astype(BF16)

    return pl.pallas_call(
        body, name=name, grid=(S // tm, N // tn),
        in_specs=[pl.BlockSpec((tm, D), lambda i, j: (i, 0)), pl.BlockSpec((1, D), lambda i, j: (0, 0)),
                  pl.BlockSpec((tn, D), lambda i, j: (j, 0))],
        out_specs=[pl.BlockSpec((tm, D), lambda i, j: (i, 0)), pl.BlockSpec((tm, tn), lambda i, j: (i, j))],
        out_shape=[jax.ShapeDtypeStruct((S, D), BF16), jax.ShapeDtypeStruct((S, N), BF16)],
        compiler_params=_cparams("parallel", "arbitrary"),
    )(x, g, wt)


PAIR_W = 3 * LANES
ATTN_UNROLL_FWD, ATTN_UNROLL_BWD = 4, 4


def _attn_masks(first):
    qi = lax.broadcasted_iota(jnp.int32, (ATTN_BLOCK, 2 * ATTN_BLOCK), 0)
    kj = lax.broadcasted_iota(jnp.int32, (ATTN_BLOCK, 2 * ATTN_BLOCK), 1)
    dist = qi + ATTN_BLOCK - kj
    valid = (dist >= 0) & (dist <= ATTN_BLOCK) & jnp.logical_or(kj >= ATTN_BLOCK, jnp.logical_not(first))
    lane = lax.broadcasted_iota(jnp.int32, (1, LANES), 1)
    return valid, lane


def _for_residue_blocks(S, d, fn):
    span = ATTN_BLOCK * d
    nb = S // span

    def step(n, carry):
        base = pl.multiple_of(n * span, span)
        for r in range(d):
            off = pl.multiple_of((r * nb + n) * ATTN_BLOCK, ATTN_BLOCK)
            fn(lambda ref, r=r: _block_rows(ref, base, r, d),
               lambda ref, val, r=r: _set_block_rows(ref, base, r, d, val), off)
        return carry

    lax.fori_loop(0, nb, step, 0)


def _for_blocks(S, unroll, fn):
    def step(i, carry):
        fn([(pl.multiple_of((i * unroll + u) * ATTN_BLOCK, ATTN_BLOCK), i * unroll + u) for u in range(unroll)])
        return carry

    lax.fori_loop(0, S // ATTN_BLOCK // unroll, step, 0)


def _head_value(x2, lane, e):
    return jnp.sum(jnp.where(lane == HEAD_DIM * e, x2, 0.0), axis=-1, keepdims=True)


def _block_rows(ref, base, r, d):
    if d == 1:
        return ref[pl.ds(base, ATTN_BLOCK), :]
    return ref.at[pl.ds(base, ATTN_BLOCK * d)][pl.ds(r, ATTN_BLOCK, stride=d), :]


def _set_block_rows(ref, base, r, d, val):
    if d == 1:
        ref[pl.ds(base, ATTN_BLOCK), :] = val
    else:
        ref.at[pl.ds(base, ATTN_BLOCK * d)][pl.ds(r, ATTN_BLOCK, stride=d), :] = val


def _order4_to_16(src, dst, pad):
    S = src.shape[0]
    q4, q16 = S // 4, S // 16
    for r in range(4):
        for a in range(4):
            for n in range(q16 // ATTN_BLOCK):
                rows = src.at[pl.ds(r * q4 + 4 * ATTN_BLOCK * n, 4 * ATTN_BLOCK)][pl.ds(a, ATTN_BLOCK, stride=4), :]
                dst[pl.ds(pad + (4 * a + r) * q16 + ATTN_BLOCK * n, ATTN_BLOCK), :] = rows.astype(dst.dtype)


def _order16_to_4(src, pad, dst):
    S = dst.shape[0]
    q4, q16 = S // 4, S // 16
    for r in range(4):
        for a in range(4):
            for n in range(q16 // ATTN_BLOCK):
                rows = src[pl.ds(pad + (4 * a + r) * q16 + ATTN_BLOCK * n, ATTN_BLOCK), :]
                dst.at[pl.ds(r * q4 + 4 * ATTN_BLOCK * n, 4 * ATTN_BLOCK)][pl.ds(a, ATTN_BLOCK, stride=4), :] = rows


def _regroup(S, d, pairs, tmp):
    for src, dst, pad in pairs:
        if d == 16:
            def to_tmp(rows, _, off, src=src):
                tmp[pl.ds(off, ATTN_BLOCK), :] = rows(src)

            _for_residue_blocks(S, 4, to_tmp)
            _order4_to_16(tmp, dst, pad)
    if d != 16:
        def to_dst(rows, _, off):
            for src, dst, pad in pairs:
                dst[pl.ds(pad + off, ATTN_BLOCK), :] = rows(src).astype(dst.dtype)

        _for_residue_blocks(S, d, to_dst)


def _split_pair(p_ref, qs, ks, vs, bk, bv):
    qs[...] = p_ref[:, 0:LANES].astype(F32)
    ks[...] = p_ref[:, LANES:2 * LANES].astype(F32)
    vs[...] = p_ref[:, 2 * LANES:3 * LANES].astype(F32)
    bk[0:ATTN_BLOCK, :] = jnp.zeros((ATTN_BLOCK, LANES), bk.dtype)
    bv[0:ATTN_BLOCK, :] = jnp.zeros((ATTN_BLOCK, LANES), bv.dtype)


def _attn_fwd(proj_a, *, name):
    S = proj_a.shape[0]

    def body(p_ref, o_ref, l_ref, qs, ks, vs, bq, bk, bv, bo, bl, to, tl):
        _split_pair(p_ref, qs, ks, vs, bk, bv)
        for d in DILATIONS:
            nb = S // (ATTN_BLOCK * d)
            _regroup(S, d, ((qs, bq, 0), (ks, bk, ATTN_BLOCK), (vs, bv, ATTN_BLOCK)), to)

            def blocks(group, nb=nb):
                lane = lax.broadcasted_iota(jnp.int32, (1, LANES), 1)
                heads = [(lane >= HEAD_DIM * e) & (lane < HEAD_DIM * (e + 1)) for e in range(LANES // HEAD_DIM)]
                wins = [pl.ds(off, 2 * ATTN_BLOCK) for off, _ in group]
                s = [[_nt(jnp.where(mh, bq[pl.ds(off, ATTN_BLOCK), :], jnp.zeros((ATTN_BLOCK, LANES), BF16)), bk[win, :])
                      for mh in heads] for (off, _), win in zip(group, wins)]
                p, m, l = [], [], []
                for (off, b), su in zip(group, s):
                    valid, _ = _attn_masks(jnp.bitwise_and(b, nb - 1) == 0)
                    sm = [jnp.where(valid, x * ATTN_SCALE, -jnp.inf) for x in su]
                    m.append([jnp.max(x, axis=-1, keepdims=True) for x in sm])
                    p.append([jnp.exp(x - mx) for x, mx in zip(sm, m[-1])])
                    l.append([jnp.sum(x, axis=-1, keepdims=True) for x in p[-1]])
                o = [[_nn(x.astype(BF16), bv[win, :]) for x in pu] for pu, win in zip(p, wins)]
                for (off, _), ou, mu, lu in zip(group, o, m, l):
                    o2 = jnp.zeros((ATTN_BLOCK, LANES), F32)
                    l2 = jnp.zeros((ATTN_BLOCK, LANES), F32)
                    for mh, oe, me_, le in zip(heads, ou, mu, lu):
                        o2 = jnp.where(mh, oe / le, o2)
                        l2 = jnp.where(mh, me_ + jnp.log(le), l2)
                    bo[pl.ds(off, ATTN_BLOCK), :] = o2
                    bl[pl.ds(off, ATTN_BLOCK), :] = l2

            _for_blocks(S, ATTN_UNROLL_FWD, blocks)

            if d == 16:
                _order16_to_4(bo, 0, to)
                _order16_to_4(bl, 0, tl)
            src_o, src_l = (to, tl) if d == 16 else (bo, bl)

            def merge(rows, set_rows, off, d=d, src_o=src_o, src_l=src_l):
                blk = pl.ds(off, ATTN_BLOCK)
                o2, l2 = src_o[blk, :], src_l[blk, :]
                if d != DILATIONS[0]:
                    lo, oo = rows(l_ref), rows(o_ref)
                    ln = jnp.maximum(lo, l2)
                    wa, wb = jnp.exp(lo - ln), jnp.exp(l2 - ln)
                    o2 = (wa * oo + wb * o2) / (wa + wb)
                    l2 = ln + jnp.log(wa + wb)
                set_rows(o_ref, o2)
                set_rows(l_ref, l2)

            _for_residue_blocks(S, min(d, 4), merge)

    slab = pl.BlockSpec((S, LANES), lambda p: (0, p))
    f32_slab, bf16_slab = pltpu.VMEM((S, LANES), F32), pltpu.VMEM((S, LANES), BF16)
    bf16_window = pltpu.VMEM((S + ATTN_BLOCK, LANES), BF16)
    return pl.pallas_call(
        body, name=name, grid=(ATTN_W // LANES,), in_specs=[pl.BlockSpec((S, PAIR_W), lambda p: (0, p))],
        out_specs=[slab, slab],
        out_shape=[jax.ShapeDtypeStruct((S, ATTN_W), F32), jax.ShapeDtypeStruct((S, ATTN_W), F32)],
        scratch_shapes=[f32_slab] * 3 + [bf16_slab, bf16_window, bf16_window] + [f32_slab] * 4,
        compiler_params=_cparams("parallel"),
    )(proj_a)


def _attn_bwd(proj_a, do, lse, delta, *, name):
    S = proj_a.shape[0]

    def body(p_ref, do_ref, lse_ref, dl_ref, o_ref, qs, ks, vs, dqs, dks, dvs, bq, bk, bv, bdo, blse, bdl, bdq, bdk, bdv,
             tmp):
        _split_pair(p_ref, qs, ks, vs, bk, bv)
        bdk[0:ATTN_BLOCK, :] = jnp.zeros((ATTN_BLOCK, LANES), F32)
        bdv[0:ATTN_BLOCK, :] = jnp.zeros((ATTN_BLOCK, LANES), F32)
        for d in DILATIONS:
            nb = S // (ATTN_BLOCK * d)
            _regroup(S, d, ((qs, bq, 0), (ks, bk, ATTN_BLOCK), (vs, bv, ATTN_BLOCK), (do_ref, bdo, 0),
                            (lse_ref, blse, 0), (dl_ref, bdl, 0)), tmp)

            def blocks(group, nb=nb):
                lane = lax.broadcasted_iota(jnp.int32, (1, LANES), 1)
                heads = [(lane >= HEAD_DIM * e) & (lane < HEAD_DIM * (e + 1)) for e in range(LANES // HEAD_DIM)]
                zero = jnp.zeros((ATTN_BLOCK, LANES), BF16)
                chains = [(off, b, e, mh) for off, b in group for e, mh in enumerate(heads)]
                qm = [jnp.where(mh, bq[pl.ds(off, ATTN_BLOCK), :], zero) for off, _, _, mh in chains]
                dom = [jnp.where(mh, bdo[pl.ds(off, ATTN_BLOCK), :], zero) for off, _, _, mh in chains]
                s = [_nt(x, bk[pl.ds(off, 2 * ATTN_BLOCK), :]) for x, (off, _, _, _) in zip(qm, chains)]
                dp = [_nt(x, bv[pl.ds(off, 2 * ATTN_BLOCK), :]) for x, (off, _, _, _) in zip(dom, chains)]
                p, ds = [], []
                for (off, b, e, _), sc, dpc in zip(chains, s, dp):
                    valid, _ = _attn_masks(jnp.bitwise_and(b, nb - 1) == 0)
                    blk = pl.ds(off, ATTN_BLOCK)
                    pc = jnp.where(valid, jnp.exp(sc * ATTN_SCALE - _head_value(blse[blk, :], lane, e)), 0.0)
                    ds.append((pc * (dpc - _head_value(bdl[blk, :], lane, e)) * ATTN_SCALE).astype(BF16))
                    p.append(pc.astype(BF16))
                dq = [_nn(x, bk[pl.ds(off, 2 * ATTN_BLOCK), :]) for x, (off, _, _, _) in zip(ds, chains)]
                dk = [_tn(x, y) for x, y in zip(ds, qm)]
                dv = [_tn(x, y) for x, y in zip(p, dom)]
                nh = len(heads)
                for u, (off, _) in enumerate(group):
                    dq2 = jnp.zeros((ATTN_BLOCK, LANES), F32)
                    for mh, x in zip(heads, dq[nh * u:nh * (u + 1)]):
                        dq2 = jnp.where(mh, x, dq2)
                    bdq[pl.ds(off, ATTN_BLOCK), :] = dq2
                    for acc, grads in ((bdk, dk), (bdv, dv)):
                        win_grad = sum(grads[nh * u + 1:nh * (u + 1)], grads[nh * u])
                        acc[pl.ds(off, ATTN_BLOCK), :] += win_grad[:ATTN_BLOCK]
                        acc[pl.ds(off + ATTN_BLOCK, ATTN_BLOCK), :] = win_grad[ATTN_BLOCK:]

            _for_blocks(S, ATTN_UNROLL_BWD, blocks)

            outs = ((dqs, bdq, 0), (dks, bdk, ATTN_BLOCK), (dvs, bdv, ATTN_BLOCK))
            if d == 16:
                for acc, grad, pad in outs:
                    _order16_to_4(grad, pad, tmp)

                    def add(rows, set_rows, off, acc=acc):
                        set_rows(acc, rows(acc) + tmp[pl.ds(off, ATTN_BLOCK), :])

                    _for_residue_blocks(S, 4, add)
            else:
                def scatter(rows, set_rows, off, d=d):
                    for acc, grad, pad in outs:
                        part = grad[pl.ds(pad + off, ATTN_BLOCK), :]
                        set_rows(acc, part if d == DILATIONS[0] else rows(acc) + part)

                _for_residue_blocks(S, d, scatter)
        o_ref[:, 0:LANES] = dqs[...].astype(BF16)
        o_ref[:, LANES:2 * LANES] = dks[...].astype(BF16)
        o_ref[:, 2 * LANES:3 * LANES] = dvs[...].astype(BF16)

    slab = pl.BlockSpec((S, LANES), lambda p: (0, p), pipeline_mode=pl.Buffered(1))
    pair = pl.BlockSpec((S, PAIR_W), lambda p: (0, p))
    f32_slab, bf16_slab = pltpu.VMEM((S, LANES), F32), pltpu.VMEM((S, LANES), BF16)
    f32_window, bf16_window = pltpu.VMEM((S + ATTN_BLOCK, LANES), F32), pltpu.VMEM((S + ATTN_BLOCK, LANES), BF16)
    return pl.pallas_call(
        body, name=name, grid=(ATTN_W // LANES,), in_specs=[pair, slab, slab, slab], out_specs=pair,
        out_shape=jax.ShapeDtypeStruct(proj_a.shape, BF16),
        scratch_shapes=[f32_slab] * 6 + [bf16_slab, bf16_window, bf16_window, bf16_slab, f32_slab, f32_slab,
                                         f32_slab, f32_window, f32_window, f32_slab],
        compiler_params=_cparams("parallel"),
    )(proj_a, do, lse, delta)


HG_T = 2 * HGRN_CHUNK
HG_GROUPS = 2
HG_STEP = HG_GROUPS * HG_T


def _hgrn_consts():
    row = lax.broadcasted_iota(jnp.int32, (HG_T, HG_T), 0)
    col = lax.broadcasted_iota(jnp.int32, (HG_T, HG_T), 1)
    same = (row >= HGRN_CHUNK) == (col >= HGRN_CHUNK)
    return row, same & (col <= row), same & (col >= row)


def _lower_bound(logits_ref):
    l0, l1 = logits_ref[0:1, :], logits_ref[1:2, :]
    mx = jnp.maximum(l0, l1)
    e0, e1 = jnp.exp(l0 - mx), jnp.exp(l1 - mx)
    return e0 / (e0 + e1)


def _hgrn_chains():
    chains = [(g, h) for g in range(HG_GROUPS) for h in range(HGRN_HEADS)]
    rows = [pl.ds(HG_T * g, HG_T) for g, _ in chains]
    lanes = [slice(HGRN_DIM * h, HGRN_DIM * (h + 1)) for _, h in chains]
    return chains, rows, lanes


def _hgrn_gates(qs, fs, lbs, row, causal):
    C = HGRN_CHUNK
    tri = jnp.where(causal, 1.0, 0.0).astype(F32)
    sgs = [_sigmoid(f) for f in fs]
    forgets = [lb + (1.0 - lb) * sg for lb, sg in zip(lbs, sgs)]
    logfs = [jnp.log(forget) for forget in forgets]
    bs = [_nn(tri, logf, HIGHEST) for logf in logfs]
    out = []
    for q, sg, forget, logf, b in zip(qs, sgs, forgets, logfs, bs):
        key = 1.0 - forget
        bend0 = jnp.sum(logf[:C], axis=0, keepdims=True)
        bend1 = jnp.sum(logf[C:], axis=0, keepdims=True)
        bend = jnp.where(row < C, bend0, bend1)
        eb, emb, eend = jnp.exp(b), jnp.exp(-b), jnp.exp(bend - b)
        sq = _sigmoid(q)
        out.append(dict(sg=sg, forget=forget, key=key, bend0=bend0, bend1=bend1, eb=eb, emb=emb, eend=eend, sq=sq,
                        qd=q * sq * eb, ki=key * emb, ke=key * eend))
    return out


def _hgrn_fwd(proj, logits, *, name):
    S = proj.shape[0]
    W, C = HGRN_W, HGRN_CHUNK

    def body(q_ref, f_ref, i_ref, lg_ref, rec_ref, st_ref, s_ref):
        @pl.when(pl.program_id(0) == 0)
        def _():
            s_ref[...] = jnp.zeros_like(s_ref)

        row, causal, _ = _hgrn_consts()
        lb_all = _lower_bound(lg_ref)
        chains, rows, lanes = _hgrn_chains()
        n = range(len(chains))
        gts = _hgrn_gates([q_ref[rows[c], lanes[c]] for c in n], [f_ref[rows[c], lanes[c]] for c in n],
                          [lb_all[:, lanes[c]] for c in n], row, causal)
        qd, ki, ke = ([gt[k].astype(BF16) for gt in gts] for k in ("qd", "ki", "ke"))
        iv = [i_ref[rows[c], lanes[c]].astype(BF16) for c in n]
        a = [_nt(qd[c], ki[c]) for c in n]
        u0 = [_tn(iv[c][:C], ke[c][:C]) for c in n]
        u1 = [_tn(iv[c][C:], ke[c][C:]) for c in n]
        state = [s_ref[h] for h in range(HGRN_HEADS)]
        s0, s1 = [], []
        for c, (g, h) in enumerate(chains):
            s0.append(state[h])
            s1.append(jnp.exp(gts[c]["bend0"]) * s0[c] + u0[c])
            state[h] = jnp.exp(gts[c]["bend1"]) * s1[c] + u1[c]
        o0 = [_nt(qd[c][:C], s0[c].astype(BF16)) for c in n]
        o1 = [_nt(qd[c][C:], s1[c].astype(BF16)) for c in n]
        o = [_nn(jnp.where(causal, a[c], 0.0).astype(BF16), iv[c]) for c in n]
        for c, (g, h) in enumerate(chains):
            st_ref[2 * g, h] = s0[c]
            st_ref[2 * g + 1, h] = s1[c]
            rec_ref[rows[c], lanes[c]] = o[c] + jnp.concatenate([o0[c], o1[c]], axis=0)
        for h in range(HGRN_HEADS):
            s_ref[h] = state[h]

    blk = lambda j: pl.BlockSpec((HG_STEP, W), lambda t: (t, j))
    return pl.pallas_call(
        body, name=name, grid=(S // HG_STEP,),
        in_specs=[blk(0), blk(1), blk(2), pl.BlockSpec((2, W), lambda t: (0, 0))],
        out_specs=[blk(0), pl.BlockSpec((2 * HG_GROUPS, HGRN_HEADS, HGRN_DIM, HGRN_DIM), lambda t: (t, 0, 0, 0))],
        out_shape=[jax.ShapeDtypeStruct((S, W), F32),
                   jax.ShapeDtypeStruct((S // C, HGRN_HEADS, HGRN_DIM, HGRN_DIM), F32)],
        scratch_shapes=[pltpu.VMEM((HGRN_HEADS, HGRN_DIM, HGRN_DIM), F32)],
        compiler_params=_cparams("arbitrary"),
    )(proj, proj, proj, logits)


def _hgrn_bwd(proj, logits, states, drec, *, name):
    S = proj.shape[0]
    W, C = HGRN_W, HGRN_CHUNK
    nt = S // HG_STEP

    def body(q_ref, f_ref, i_ref, lg_ref, st_ref, do_ref, dp_ref, dlg_ref, ds_ref, dlb_ref):
        t = pl.program_id(0)

        @pl.when(t == 0)
        def _():
            ds_ref[...] = jnp.zeros_like(ds_ref)
            dlb_ref[...] = jnp.zeros_like(dlb_ref)

        row, causal, anti = _hgrn_consts()
        lb_all = _lower_bound(lg_ref)
        chains, rows, lanes = _hgrn_chains()
        n = range(len(chains))
        qs, lbs = [q_ref[rows[c], lanes[c]] for c in n], [lb_all[:, lanes[c]] for c in n]
        gts = _hgrn_gates(qs, [f_ref[rows[c], lanes[c]] for c in n], lbs, row, causal)
        qd, ki, ke = ([gt[k] for gt in gts] for k in ("qd", "ki", "ke"))
        qdb, kib, keb = ([x.astype(BF16) for x in xs] for xs in (qd, ki, ke))
        iv = [i_ref[rows[c], lanes[c]].astype(BF16) for c in n]
        dob = [do_ref[rows[c], lanes[c]].astype(BF16) for c in n]
        s0 = [st_ref[2 * g, h] for g, h in chains]
        s1 = [st_ref[2 * g + 1, h] for g, h in chains]
        dec0, dec1 = [jnp.exp(gt["bend0"]) for gt in gts], [jnp.exp(gt["bend1"]) for gt in gts]
        a = [_nt(qdb[c], kib[c]) for c in n]
        da = [_nt(dob[c], iv[c]) for c in n]
        dqd1 = [_nn(dob[c][C:], s1[c].astype(BF16)) for c in n]
        dqd0 = [_nn(dob[c][:C], s0[c].astype(BF16)) for c in n]
        t1 = [_tn(dob[c][C:], qdb[c][C:]) for c in n]
        t0 = [_tn(dob[c][:C], qdb[c][:C]) for c in n]
        carry = [ds_ref[h] for h in range(HGRN_HEADS)]
        ds1, ds0 = [None] * len(chains), [None] * len(chains)
        for c in reversed(n):
            h = chains[c][1]
            ds1[c] = carry[h]
            ds0[c] = dec1[c] * ds1[c] + t1[c]
            carry[h] = dec0[c] * ds0[c] + t0[c]
        for h in range(HGRN_HEADS):
            ds_ref[h] = carry[h]
        ds1b, ds0b = [x.astype(BF16) for x in ds1], [x.astype(BF16) for x in ds0]
        a = [jnp.where(causal, x, 0.0).astype(BF16) for x in a]
        da = [jnp.where(causal, x, 0.0).astype(BF16) for x in da]
        di1 = [_nt(keb[c][C:], ds1b[c]) for c in n]
        dke1 = [_nn(iv[c][C:], ds1b[c]) for c in n]
        di0 = [_nt(keb[c][:C], ds0b[c]) for c in n]
        dke0 = [_nn(iv[c][:C], ds0b[c]) for c in n]
        dqd_a = [_nn(da[c], kib[c]) for c in n]
        dki = [_tn(da[c], qdb[c]) for c in n]
        di_a = [_tn(a[c], dob[c]) for c in n]
        dqd, dke, db = [], [], []
        for c in n:
            ddec1 = jnp.sum(ds1[c] * s1[c], axis=0, keepdims=True)
            ddec0 = jnp.sum(ds0[c] * s0[c], axis=0, keepdims=True)
            dqd.append(dqd_a[c] + jnp.concatenate([dqd0[c], dqd1[c]], axis=0))
            h = chains[c][1]
            dp_ref[rows[c], 2 * W + HGRN_DIM * h:2 * W + HGRN_DIM * (h + 1)] = (
                di_a[c] + jnp.concatenate([di0[c], di1[c]], axis=0)).astype(BF16)
            dke.append(jnp.concatenate([dke0[c], dke1[c]], axis=0))
            gke = dke[c] * ke[c]
            dbend0 = jnp.sum(gke[:C], axis=0, keepdims=True) + ddec0 * dec0[c]
            dbend1 = jnp.sum(gke[C:], axis=0, keepdims=True) + ddec1 * dec1[c]
            dbc = dqd[c] * qd[c] - dki[c] * ki[c] - gke
            db.append(dbc + jnp.where(row == C - 1, dbend0, 0.0) + jnp.where(row == HG_T - 1, dbend1, 0.0))
        tri = jnp.where(anti, 1.0, 0.0).astype(F32)
        dlogf = [_nn(tri, db[c], HIGHEST) for c in n]
        for c in n:
            gt, lb, q, h = gts[c], lbs[c], qs[c], chains[c][1]
            dforget = dlogf[c] / gt["forget"] - (dki[c] * gt["emb"] + dke[c] * gt["eend"])
            sg, sq = gt["sg"], gt["sq"]
            dp_ref[rows[c], W + HGRN_DIM * h:W + HGRN_DIM * (h + 1)] = (
                dforget * (1.0 - lb) * sg * (1.0 - sg)).astype(BF16)
            dlb_ref[:, lanes[c]] += jnp.sum(dforget * (1.0 - sg), axis=0, keepdims=True)
            dp_ref[rows[c], lanes[c]] = (dqd[c] * gt["eb"] * sq * (1.0 + q * (1.0 - sq))).astype(BF16)

        @pl.when(t == nt - 1)
        def _():
            dl0 = dlb_ref[...] * lb_all * (1.0 - lb_all)
            dlg_ref[0:1, :] = dl0
            dlg_ref[1:2, :] = -dl0

    blk = lambda j: pl.BlockSpec((HG_STEP, W), lambda t: (nt - 1 - t, j))
    full = pl.BlockSpec((2, W), lambda t: (0, 0))
    return pl.pallas_call(
        body, name=name, grid=(nt,),
        in_specs=[blk(0), blk(1), blk(2), full,
                  pl.BlockSpec((2 * HG_GROUPS, HGRN_HEADS, HGRN_DIM, HGRN_DIM), lambda t: (nt - 1 - t, 0, 0, 0)), blk(0)],
        out_specs=[pl.BlockSpec((HG_STEP, 3 * W), lambda t: (nt - 1 - t, 0)), full],
        out_shape=[jax.ShapeDtypeStruct((S, 3 * W), BF16), jax.ShapeDtypeStruct((2, W), F32)],
        scratch_shapes=[pltpu.VMEM((HGRN_HEADS, HGRN_DIM, HGRN_DIM), F32), pltpu.VMEM((1, W), F32)],
        compiler_params=_cparams("arbitrary"),
    )(proj, proj, proj, logits, states, drec)


def _out_proj(attn, rec, proj_h, x, g_attn, g_hgrn, g_norm2, w_out, *, name, tm=512):
    S, D = x.shape
    AW, W = ATTN_W, HGRN_W

    def body(a_ref, r_ref, hg_ref, x_ref, ga_ref, gh_ref, g2_ref, w_ref, h_ref, u_ref, m_ref):
        av = a_ref[...]
        m_ref[:, :AW] = (av * _rstd(av) * ga_ref[...]).astype(BF16)
        for h in range(HGRN_HEADS):
            sl = slice(HGRN_DIM * h, HGRN_DIM * (h + 1))
            rv, hg = r_ref[:, sl], hg_ref[:, sl]
            m_ref[:, AW + HGRN_DIM * h:AW + HGRN_DIM * (h + 1)] = (
                (rv * _rstd(rv) * gh_ref[:, sl]) * (hg * _sigmoid(hg))).astype(BF16)
        h1 = x_ref[...] + _nn(m_ref[...], w_ref[...])
        h_ref[...] = h1
        u_ref[...] = (h1 * _rstd(h1) * g2_ref[...]).astype(BF16)

    row = lambda w, j=0: pl.BlockSpec((tm, w), lambda i: (i, j))
    vec = lambda w: pl.BlockSpec((1, w), lambda i: (0, 0))
    return pl.pallas_call(
        body, name=name, grid=(S // tm,),
        in_specs=[row(AW), row(W), row(W, 3), row(D), vec(AW), vec(W), vec(D), _resident(w_out.shape)],
        out_specs=[row(D), row(D), row(AW + W)],
        out_shape=[jax.ShapeDtypeStruct((S, D), F32), jax.ShapeDtypeStruct((S, D), BF16),
                   jax.ShapeDtypeStruct((S, AW + W), BF16)],
        compiler_params=_cparams("parallel"),
    )(attn, rec, proj_h, x, g_attn, g_hgrn, g_norm2, w_out)


def _dmix_post_bwd(dh1b, w_out, attn, rec, proj_h, g_attn, g_hgrn, *, name, tm=512):
    S, D = dh1b.shape
    AW, W = ATTN_W, HGRN_W

    def body(dh_ref, w_ref, a_ref, r_ref, hg_ref, ga_ref, gh_ref, do_ref, dl_ref, dr_ref, dhg_ref, dga_ref, dgh_ref):
        first = pl.program_id(0) == 0
        dmix = _nt(dh_ref[...], w_ref[...])
        av = a_ref[...]
        dov, dga = _norm_bwd(av, ga_ref[...], dmix[:, :AW])
        do_ref[...] = dov
        shift = HEAD_DIM.bit_length() - 1
        hi = lax.shift_right_logical(lax.broadcasted_iota(jnp.int32, (AW, AW), 0), shift)
        hj = lax.shift_right_logical(lax.broadcasted_iota(jnp.int32, (AW, AW), 1), shift)
        prod = dov * av
        hi_part = prod.astype(BF16)
        lo_part = (prod - hi_part.astype(F32)).astype(BF16)
        same_head = jnp.where(hi == hj, 1.0, 0.0).astype(BF16)
        dl_ref[...] = _nn(hi_part, same_head) + _nn(lo_part, same_head)
        _accumulate(dga_ref, jnp.sum(dga, axis=0, keepdims=True), first)

        @pl.when(first)
        def _():
            dgh_ref[...] = jnp.zeros_like(dgh_ref)

        for h in range(HGRN_HEADS):
            sl = slice(HGRN_DIM * h, HGRN_DIM * (h + 1))
            rv, hg, gv = r_ref[:, sl], hg_ref[:, sl], gh_ref[:, sl]
            dout = dmix[:, AW + HGRN_DIM * h:AW + HGRN_DIM * (h + 1)]
            sg = _sigmoid(hg)
            drv, dgh = _norm_bwd(rv, gv, dout * (hg * sg))
            dr_ref[:, sl] = drv
            dgh_ref[:, sl] += jnp.sum(dgh, axis=0, keepdims=True)
            dhg_ref[:, sl] = (dout * (rv * _rstd(rv) * gv) * (sg * (1.0 + hg * (1.0 - sg)))).astype(BF16)

    row = lambda w, j=0: pl.BlockSpec((tm, w), lambda i: (i, j))
    vec = lambda w: pl.BlockSpec((1, w), lambda i: (0, 0))
    return pl.pallas_call(
        body, name=name, grid=(S // tm,),
        in_specs=[row(D), _resident(w_out.shape), row(AW), row(W), row(W, 3), vec(AW), vec(W)],
        out_specs=[row(AW), row(AW), row(W), row(W), vec(AW), vec(W)],
        out_shape=[jax.ShapeDtypeStruct((S, AW), F32), jax.ShapeDtypeStruct((S, AW), F32),
                   jax.ShapeDtypeStruct((S, W), F32), jax.ShapeDtypeStruct((S, W), BF16),
                   jax.ShapeDtypeStruct((1, AW), F32), jax.ShapeDtypeStruct((1, W), F32)],
        compiler_params=_cparams("arbitrary"),
    )(dh1b, w_out, attn, rec, proj_h, g_attn, g_hgrn)


def _conv_act(g, g1, g2, w_ref, b_ref):
    c = b_ref[...] + w_ref[0:1, :] * g2 + w_ref[1:2, :] * g1 + w_ref[2:3, :] * g
    return c, 0.5 * (1.0 + lax.erf(c * (2.0 ** -0.5)))


def _shift_down(g, halo, row):
    g1 = jnp.where(row == 0, halo[7:8], pltpu.roll(g, 1, 0))
    g2 = jnp.where(row == 0, halo[6:7], jnp.where(row == 1, halo[7:8], pltpu.roll(g, 2, 0)))
    return g1, g2


def _shift_up(x, halo, row):
    n = x.shape[0]
    x1 = jnp.where(row == n - 1, halo[0:1], pltpu.roll(x, n - 1, 0))
    x2 = jnp.where(row == n - 2, halo[0:1], jnp.where(row == n - 1, halo[1:2], pltpu.roll(x, n - 2, 0)))
    return x1, x2


def _up_glu(u, wt_up, conv_w, conv_b, *, name, tm=1024, tn=256):
    S, D = u.shape
    F = wt_up.shape[0] // 2
    nf = F // tn

    def body(u_ref, wg_ref, wv_ref, cw_ref, cb_ref, g_ref, v_ref, a_ref, halo_ref):
        i, j = pl.program_id(0), pl.program_id(1)

        @pl.when(i == 0)
        def _():
            halo_ref[j] = jnp.zeros((SUBLANES, tn), F32)

        uv = u_ref[...]
        g, v = _nt(uv, wg_ref[...]), _nt(uv, wv_ref[...])
        row = lax.broadcasted_iota(jnp.int32, (tm, tn), 0)
        g1, g2 = _shift_down(g, halo_ref[j], row)
        c, cdf = _conv_act(g, g1, g2, cw_ref, cb_ref)
        a_ref[...] = (c * cdf * v).astype(BF16)
        g_ref[...] = g.astype(BF16)
        v_ref[...] = v.astype(BF16)
        halo_ref[j] = g[tm - SUBLANES:, :]

    col = pl.BlockSpec((tm, tn), lambda i, j: (i, j))
    out = jax.ShapeDtypeStruct((S, F), BF16)
    return pl.pallas_call(
        body, name=name, grid=(S // tm, nf),
        in_specs=[pl.BlockSpec((tm, D), lambda i, j: (i, 0)), pl.BlockSpec((tn, D), lambda i, j: (j, 0)),
                  pl.BlockSpec((tn, D), lambda i, j: (j + nf, 0)), pl.BlockSpec((3, tn), lambda i, j: (0, j)),
                  pl.BlockSpec((1, tn), lambda i, j: (0, j))],
        out_specs=[col, col, col], out_shape=[out, out, out],
        scratch_shapes=[pltpu.VMEM((nf, SUBLANES, tn), F32)], compiler_params=_cparams("arbitrary", "arbitrary"),
    )(u, wt_up, wt_up, conv_w, conv_b)


def _dact_glu_bwd(dh2b, w_down, gate, val, conv_w, conv_b, *, name, tm=1024, tn=256):
    S, D = dh2b.shape
    F = gate.shape[1]
    nf, ni = F // tn, S // tm
    hb = tm // SUBLANES

    def body(dh_ref, wd_ref, g_ref, gh_ref, v_ref, cw_ref, cb_ref, dg_ref, dv_ref, dcw_ref, dcb_ref, halo_ref, acc_ref):
        i, j = pl.program_id(0), pl.program_id(1)

        @pl.when(i == 0)
        def _():
            halo_ref[j] = jnp.zeros((SUBLANES, tn), F32)
            acc_ref[j] = jnp.zeros((SUBLANES, tn), F32)

        g = g_ref[...].astype(F32)
        before = jnp.where(i < ni - 1, gh_ref[...].astype(F32), 0.0)
        row = lax.broadcasted_iota(jnp.int32, (tm, tn), 0)
        g1, g2 = _shift_down(g, before[SUBLANES:], row)
        c, cdf = _conv_act(g, g1, g2, cw_ref, cb_ref)
        da = _nt(dh_ref[...], wd_ref[...])
        dv_ref[...] = (da * (c * cdf)).astype(BF16)
        pdf = jnp.exp(-0.5 * c * c) * (1.0 / (2.0 * jnp.pi) ** 0.5)
        dc = da * v_ref[...].astype(F32) * (cdf + c * pdf)
        d1, d2 = _shift_up(dc, halo_ref[j], row)
        dg_ref[...] = (cw_ref[2:3, :] * dc + cw_ref[1:2, :] * d1 + cw_ref[0:1, :] * d2).astype(BF16)
        halo_ref[j] = dc[:SUBLANES, :]
        for k, t in enumerate((dc * g2, dc * g1, dc * g, dc)):
            acc_ref[j, k:k + 1, :] += jnp.sum(t, axis=0, keepdims=True)

        @pl.when((i == ni - 1) & (j == nf - 1))
        def _():
            for jj in range(nf):
                dcw_ref[:, jj * tn:(jj + 1) * tn] = acc_ref[jj, 0:3, :]
                dcb_ref[:, jj * tn:(jj + 1) * tn] = acc_ref[jj, 3:4, :]

    tile = pl.BlockSpec((tm, tn), lambda i, j: (ni - 1 - i, j))
    return pl.pallas_call(
        body, name=name, grid=(ni, nf),
        in_specs=[pl.BlockSpec((tm, D), lambda i, j: (ni - 1 - i, 0)), pl.BlockSpec((tn, D), lambda i, j: (j, 0)),
                  tile, pl.BlockSpec((SUBLANES * 2, tn), lambda i, j: (jnp.maximum((ni - 1 - i) * (hb // 2) - 1, 0), j)),
                  tile, pl.BlockSpec((3, tn), lambda i, j: (0, j)), pl.BlockSpec((1, tn), lambda i, j: (0, j))],
        out_specs=[tile, tile, pl.BlockSpec((3, F), lambda i, j: (0, 0)), pl.BlockSpec((1, F), lambda i, j: (0, 0))],
        out_shape=[jax.ShapeDtypeStruct((S, F), BF16), jax.ShapeDtypeStruct((S, F), BF16),
                   jax.ShapeDtypeStruct((3, F), F32), jax.ShapeDtypeStruct((1, F), F32)],
        scratch_shapes=[pltpu.VMEM((nf, SUBLANES, tn), F32), pltpu.VMEM((nf, SUBLANES, tn), F32)],
        compiler_params=_cparams("arbitrary", "arbitrary"),
    )(dh2b, w_down, gate, gate, val, conv_w, conv_b)


def _down_loss(act, w_down, h1, g, target, *, name, tm=512):
    S, F = act.shape
    D = h1.shape[1]

    def body(a_ref, w_ref, h_ref, g_ref, t_ref, dh_ref, dhb_ref, dg_ref, loss_ref):
        first = pl.program_id(0) == 0
        h2 = h_ref[...] + _nn(a_ref[...], w_ref[...])
        gv = g_ref[...]
        r = _rstd(h2)
        xh = h2 * r
        err = xh * gv - t_ref[...]
        part_loss = 0.5 * jnp.sum(jnp.mean(err * err, axis=-1, keepdims=True), axis=0, keepdims=True)
        dy = err * (1.0 / D)
        dxh = dy * gv
        dh = r * (dxh - xh * jnp.mean(dxh * xh, axis=-1, keepdims=True))
        dh_ref[...] = dh
        dhb_ref[...] = dh.astype(BF16)
        _accumulate(dg_ref, jnp.sum(dy * xh, axis=0, keepdims=True), first)
        _accumulate(loss_ref, jnp.broadcast_to(part_loss, (1, LANES)), first)

    row = lambda w: pl.BlockSpec((tm, w), lambda i: (i, 0))
    vec = lambda w: pl.BlockSpec((1, w), lambda i: (0, 0))
    return pl.pallas_call(
        body, name=name, grid=(S // tm,), in_specs=[row(F), _resident(w_down.shape), row(D), vec(D), row(D)],
        out_specs=[row(D), row(D), vec(D), vec(LANES)],
        out_shape=[jax.ShapeDtypeStruct((S, D), F32), jax.ShapeDtypeStruct((S, D), BF16),
                   jax.ShapeDtypeStruct((1, D), F32), jax.ShapeDtypeStruct((1, LANES), F32)],
        compiler_params=_cparams("arbitrary"),
    )(act, w_down, h1, g, target)


def _grad_norm_input(pieces, ws, x, g, add, *, name, tm=512):
    S, D = x.shape
    widths = [p.shape[1] for p in pieces]
    n, nw = len(pieces), len(ws)
    where, wi, off = [], 0, ws[0][1]
    for wd in widths:
        if off == ws[wi][0].shape[0]:
            wi, off = wi + 1, ws[wi + 1][1]
        where.append((wi, off))
        off += wd
    ws = [w for w, _ in ws]

    def body(*refs):
        p_refs, w_refs = refs[:n], refs[n:n + nw]
        x_ref, g_ref, add_ref, dx_ref, dxb_ref, dg_ref = refs[n + nw:]
        halves = _row_halves(tm)
        du = []
        for rows in halves:
            terms = [_nn(p_refs[k][rows, :], w_refs[wi][off:off + widths[k], :]) for k, (wi, off) in enumerate(where)]
            du.append(sum(terms[1:], terms[0]))
        dg_sum = None
        for rows, duh in zip(halves, du):
            dx, dg = _norm_bwd(x_ref[rows, :], g_ref[...], duh)
            dx = add_ref[rows, :] + dx
            dx_ref[rows, :] = dx
            dxb_ref[rows, :] = dx.astype(BF16)
            part = jnp.sum(dg, axis=0, keepdims=True)
            dg_sum = part if dg_sum is None else dg_sum + part
        _accumulate(dg_ref, dg_sum, pl.program_id(0) == 0)

    row = lambda w_: pl.BlockSpec((tm, w_), lambda i: (i, 0))
    vec = pl.BlockSpec((1, D), lambda i: (0, 0))
    return pl.pallas_call(
        body, name=name, grid=(S // tm,),
        in_specs=[row(wd) for wd in widths] + [_resident(w.shape) for w in ws] + [row(D), vec, row(D)],
        out_specs=[row(D), row(D), vec],
        out_shape=[jax.ShapeDtypeStruct((S, D), F32), jax.ShapeDtypeStruct((S, D), BF16),
                   jax.ShapeDtypeStruct((1, D), F32)],
        compiler_params=_cparams("arbitrary"),
    )(*pieces, *ws, x, g, add)


def _rows(a):
    return a.reshape(-1, a.shape[-1])


def _row_tile(rows, cols, itemsize=4, budget=1 << 20):
    t = rows
    while t % 32 == 0 and t * cols * itemsize > budget:
        t //= 2
    return t


def _sum_cast(arrs, out_dtype, *, name):
    shape = arrs[0].shape
    flat = [_rows(a) for a in arrs]
    R, C = flat[0].shape
    tr = _row_tile(R, C)

    def body(*refs):
        acc = refs[0][...].astype(F32)
        for r in refs[1:-1]:
            acc = acc + r[...].astype(F32)
        refs[-1][...] = acc.astype(out_dtype)

    spec = pl.BlockSpec((tr, C), lambda i: (i, 0))
    return pl.pallas_call(
        body, name=name, grid=(R // tr,), in_specs=[spec] * len(flat), out_specs=spec,
        out_shape=jax.ShapeDtypeStruct((R, C), out_dtype), compiler_params=_cparams("parallel"),
    )(*flat).reshape(shape)


def _adamw(parts, w, m, v, *, name):
    shape = w.shape
    w2, m2, v2 = _rows(w), _rows(m), _rows(v)
    R, C = w2.shape
    parts = [p.reshape(-1, R, C) for p in parts]
    tr = _row_tile(R, C)
    np_ = len(parts)
    c1, c2 = 1.0 - ADAM_B1 ** ADAM_STEP, 1.0 - ADAM_B2 ** ADAM_STEP

    def body(*refs):
        terms = [(r, k) for r in refs[:np_] for k in range(r.shape[0])]
        g = terms[0][0][terms[0][1]].astype(F32)
        for r, k in terms[1:]:
            g = g + r[k].astype(F32)
        w_ref, m_ref, v_ref, g_out, d_out, m_out, v_out = refs[np_:]
        mn = ADAM_B1 * m_ref[...] + (1.0 - ADAM_B1) * g
        vn = ADAM_B2 * v_ref[...] + (1.0 - ADAM_B2) * (g * g)
        g_out[...] = g
        d_out[...] = -ADAM_LR * ((mn / c1) / (jnp.sqrt(vn / c2) + ADAM_EPS) + ADAM_WD * w_ref[...])
        m_out[...] = mn
        v_out[...] = vn

    spec = pl.BlockSpec((tr, C), lambda i: (i, 0))
    out = jax.ShapeDtypeStruct((R, C), F32)
    stacks = [pl.BlockSpec((p.shape[0], tr, C), lambda i: (0, i, 0)) for p in parts]
    res = pl.pallas_call(
        body, name=name, grid=(R // tr,), in_specs=stacks + [spec] * 3, out_specs=[spec] * 4,
        out_shape=[out] * 4, compiler_params=_cparams("parallel"),
    )(*parts, w2, m2, v2)
    return [r.reshape(shape) for r in res]


def _adamw_packed(stack, widths, params, *, name):
    c1, c2 = 1.0 - ADAM_B1 ** ADAM_STEP, 1.0 - ADAM_B2 ** ADAM_STEP
    k = stack.shape[0]
    flat = [None if p is None else [_rows(a) for a in p] for p in params]
    n_in = sum(3 for p in flat if p is not None)

    def body(*refs):
        s_ref, ins, outs = refs[0], list(refs[1:1 + n_in]), list(refs[1 + n_in:])
        off = 0
        for width, p in zip(widths, flat):
            rows = 1 if p is None else p[0].shape[0]
            cols = width // rows
            w_ref, m_ref, v_ref = (None, None, None) if p is None else (ins.pop(0), ins.pop(0), ins.pop(0))
            o_refs = [outs.pop(0) for _ in range(1 if p is None else 4)]
            for r in range(rows):
                seg = slice(off + r * cols, off + (r + 1) * cols)
                g = s_ref[0, :, seg]
                for j in range(1, k):
                    g = g + s_ref[j, :, seg]
                o_refs[0][r:r + 1, :] = g
                if p is not None:
                    row = slice(r, r + 1)
                    mn = ADAM_B1 * m_ref[row, :] + (1.0 - ADAM_B1) * g
                    vn = ADAM_B2 * v_ref[row, :] + (1.0 - ADAM_B2) * (g * g)
                    o_refs[1][row, :] = -ADAM_LR * ((mn / c1) / (jnp.sqrt(vn / c2) + ADAM_EPS) + ADAM_WD * w_ref[row, :])
                    o_refs[2][row, :] = mn
                    o_refs[3][row, :] = vn
            off += width

    operands, out_shape = [stack], []
    for width, p in zip(widths, flat):
        if p is None:
            out_shape.append(jax.ShapeDtypeStruct((1, width), F32))
        else:
            operands += p
            out_shape += [jax.ShapeDtypeStruct(p[0].shape, F32)] * 4
    res = list(pl.pallas_call(body, name=name, out_shape=out_shape)(*operands))
    out = []
    for p, orig in zip(flat, params):
        n = 1 if p is None else 4
        out.append([r if orig is None else r.reshape(orig[0].shape) for r in res[:n]])
        res = res[n:]
    return out


def _coords():
    return lax.axis_index("x"), lax.axis_index("y"), lax.axis_index("c")


def _all_gather(shards, *, name):
    n = len(shards)

    def body(*refs):
        x_refs, out_refs = refs[:n], refs[n:2 * n]
        send_sems, recv_sems, local_sems = refs[2 * n:]
        x, y, c = _coords()
        me, sibling = (x, y, c), (x, y, 1 - c)
        chips = [(1 - x, y), (x, 1 - y), (1 - x, 1 - y)]

        def slot(a, dev):
            return out_refs[a].at[4 * dev[0] + 2 * dev[1] + dev[2]]

        def copy(a, k, block, to, src=None):
            return pltpu.make_async_remote_copy(
                src_ref=slot(a, block) if src is None else src, dst_ref=slot(a, block),
                send_sem=send_sems.at[7 * a + k], recv_sem=recv_sems.at[7 * a + k], device_id=to, device_id_type=MESH)

        mine = [pltpu.make_async_copy(x_refs[a], slot(a, me), local_sems.at[a]) for a in range(n)]
        for cp in mine:
            cp.start()
        first = []
        for a in range(n):
            first.append(copy(a, 0, me, sibling, src=x_refs[a]))
            first += [copy(a, 1 + j, me, (*chip, c), src=x_refs[a]) for j, chip in enumerate(chips)]
        for cp in first:
            cp.start()
        passed = []
        for j, chip in enumerate(chips):
            for a in range(n):
                copy(a, 1 + j, (*chip, c), me).wait_recv()
                fwd = copy(a, 4 + j, (*chip, c), sibling)
                fwd.start()
                passed.append(fwd)
        for a in range(n):
            copy(a, 0, sibling, me).wait_recv()
            for j, chip in enumerate(chips):
                copy(a, 4 + j, (*chip, 1 - c), me).wait_recv()
        for cp in first + passed:
            cp.wait_send()
        for cp in mine:
            cp.wait()

    return pl.pallas_call(
        body, name=name, in_specs=[HBM] * n, out_specs=[HBM] * n,
        out_shape=[jax.ShapeDtypeStruct((N_DEV, *s.shape), s.dtype) for s in shards],
        scratch_shapes=[pltpu.SemaphoreType.DMA((7 * n,)), pltpu.SemaphoreType.DMA((7 * n,)),
                        pltpu.SemaphoreType.DMA((n,))],
    )(*shards)


def _flip_y(x, y, c):
    return (x, 1 - y, c)


def _flip_x(x, y, c):
    return (1 - x, y, c)


def _flip_xy(x, y, c):
    return (1 - x, 1 - y, c)


SEM = pl.BlockSpec(memory_space=pltpu.SEMAPHORE)
SIDE_EFFECT = pltpu.SideEffectType.DATAFLOW_SIDE_EFFECTING


def _in_hbm(a):
    return pltpu.with_memory_space_constraint(a, pltpu.HBM)


def _copies_start(srcs, lands, plan, n_copies, *, name, after=None):
    ns, nl = len(srcs), len(lands)
    extra = [] if after is None else [after]

    def body(*refs):
        src_refs, land_refs = refs[:ns], refs[ns:ns + nl]
        send_sems, recv_sems = refs[ns + nl + len(extra):ns + nl + len(extra) + 2]
        token = refs[-1]
        for k, (src, dst, peer, _) in enumerate(plan(src_refs, land_refs, *_coords())):
            pltpu.make_async_remote_copy(src_ref=src, dst_ref=dst, send_sem=send_sems.at[k], recv_sem=recv_sems.at[k],
                                         device_id=peer, device_id_type=MESH).start()
        token[...] = jnp.zeros_like(token)

    bufs = [*srcs, *lands]
    res = pl.pallas_call(
        body, name=name, in_specs=[HBM] * (ns + nl) + [pl.BlockSpec(memory_space=pl.ANY)] * len(extra),
        out_specs=(SEM, SEM, *[HBM] * (ns + nl), pl.BlockSpec(memory_space=pltpu.VMEM)),
        out_shape=(pltpu.SemaphoreType.DMA((n_copies,)), pltpu.SemaphoreType.DMA((n_copies,)),
                   *[pltpu.HBM(b.shape, b.dtype) for b in bufs], jax.ShapeDtypeStruct((SUBLANES, LANES), F32)),
        input_output_aliases={i: 2 + i for i in range(ns + nl)},
        compiler_params=pltpu.CompilerParams(has_side_effects=SIDE_EFFECT),
    )(*[_in_hbm(b) for b in bufs], *extra)
    return res[0], res[1], list(res[2:2 + ns]), list(res[2 + ns:2 + ns + nl]), res[-1]


def _copies_wait(started, plan, after, *, name):
    send_sems, recv_sems, srcs, lands, _ = started
    ns, nl = len(srcs), len(lands)

    def body(*refs):
        src_refs, land_refs = refs[:ns], refs[ns:ns + nl]
        send_sems, recv_sems = refs[ns + nl:ns + nl + 2]
        for k, (src, dst, peer, here) in enumerate(plan(src_refs, land_refs, *_coords())):
            pltpu.make_async_remote_copy(src_ref=src, dst_ref=dst, send_sem=send_sems.at[k], recv_sem=recv_sems.at[k],
                                         device_id=peer, device_id_type=MESH).wait_send()
            pltpu.make_async_remote_copy(src_ref=src, dst_ref=here, send_sem=send_sems.at[k], recv_sem=recv_sems.at[k],
                                         device_id=peer, device_id_type=MESH).wait_recv()

    bufs = [*srcs, *lands]
    res = pl.pallas_call(
        body, name=name, in_specs=[HBM] * (ns + nl) + [SEM, SEM, pl.BlockSpec(memory_space=pl.ANY)],
        out_specs=[HBM] * (ns + nl), out_shape=[pltpu.HBM(b.shape, b.dtype) for b in bufs],
        input_output_aliases={i: i for i in range(ns + nl)},
        compiler_params=pltpu.CompilerParams(has_side_effects=SIDE_EFFECT),
    )(*bufs, send_sems, recv_sems, after)
    return list(res[ns:])


def _dev_index(dev):
    return 4 * dev[0] + 2 * dev[1] + dev[2]


def _ag_chips_plan(src_refs, land_refs, x, y, c):
    me = _dev_index((x, y, c))
    return [(src, land.at[me], peer, land.at[_dev_index(peer)])
            for src, land in zip(src_refs, land_refs) for peer in (_flip_y(x, y, c), _flip_x(x, y, c), _flip_xy(x, y, c))]


def _ag_sibling_plan(src_refs, land_refs, x, y, c):
    chips = [(x, y), (x, 1 - y), (1 - x, y), (1 - x, 1 - y)]
    return [(land.at[_dev_index((*chip, c))], land.at[_dev_index((*chip, c))], (x, y, 1 - c),
             land.at[_dev_index((*chip, 1 - c))]) for land in land_refs for chip in chips]


def _ag_direct_plan(src_refs, land_refs, x, y, c):
    me = _dev_index((x, y, c))
    plan = []
    for src, land in zip(src_refs, land_refs):
        for m in range(1, N_DEV):
            peer = (x + (m >> 2) * (1 - 2 * x), y + ((m >> 1) & 1) * (1 - 2 * y), c + (m & 1) * (1 - 2 * c))
            plan.append((src, land.at[me], peer, land.at[_dev_index(peer)]))
    return plan


def _rs_direct_plan(src_refs, land_refs, x, y, c):
    plan = []
    for src, land in zip(src_refs, land_refs):
        for m in range(1, N_DEV):
            peer = (x + (m >> 2) * (1 - 2 * x), y + ((m >> 1) & 1) * (1 - 2 * y), c + (m & 1) * (1 - 2 * c))
            plan.append((src.at[_dev_index(peer)], land.at[m - 1], peer, land.at[m - 1]))
    return plan


def _rs_start(grads, me, *, name, after=None):
    own = [lax.dynamic_index_in_dim(g, me, 0, keepdims=False) for g in grads]
    lands = [lax.empty((N_DEV - 1, *g.shape[1:]), g.dtype) for g in grads]
    return _copies_start(grads, lands, _rs_direct_plan, (N_DEV - 1) * len(grads), name=name, after=after), own


def _rs_finish(started, after, *, name):
    handle, own = started
    got = _copies_wait(handle, _rs_direct_plan, after, name=name)
    return [[o, land] for o, land in zip(own, got)]


def _gathered_cols(w8):
    return w8.transpose(1, 0, 2).reshape(w8.shape[1], -1)


def _pair_major(wt):
    return wt.reshape(3, ATTN_W // LANES, LANES, -1).transpose(1, 0, 2, 3).reshape(3 * ATTN_W, -1)


def kernel(x, norm1_g, w_in, attn_norm_g, hgrn_norm_g, hgrn_lb_logits, w_out, norm2_g, w_up, conv_w, conv_b, w_down, final_norm_g, loss_target, m_norm1_g, m_w_in, m_attn_norm_g, m_hgrn_norm_g, m_hgrn_lb_logits, m_w_out, m_norm2_g, m_w_up, m_conv_w, m_conv_b, m_w_down, m_final_norm_g, v_norm1_g, v_w_in, v_attn_norm_g, v_hgrn_norm_g, v_hgrn_lb_logits, v_w_out, v_norm2_g, v_w_up, v_conv_w, v_conv_b, v_w_down, v_final_norm_g):
    xs, target = x[0], loss_target[0]
    S, D = xs.shape
    NA = 3 * ATTN_W
    fng = final_norm_g.reshape(1, D)

    t = lambda a: a[0].T
    casts = [_sum_cast([w], BF16, name=f"cast_{nm}") for nm, w in
             (("w_in", t(w_in)), ("w_out", w_out[0]), ("w_up", t(w_up)), ("w_down", w_down[0]))]
    me = _dev_index(_coords())
    (g_in,) = _all_gather(casts[:1], name="ag_w_in")
    later = casts[1:] + [conv_w[0]]
    ag1 = _copies_start(later, [lax.empty((N_DEV, *s.shape), s.dtype) for s in later], _ag_chips_plan,
                        3 * len(later), name="ag_chips_start", after=g_in)
    wi = g_in.reshape(-1, D)
    wi_a = _pair_major(wi[:NA])

    u1, proj_a = _proj_attn(xs, norm1_g + ag1[4][0, 0], wi_a, name="proj_attn")
    proj_h = _mm_nt(u1, wi, NA, wi.shape[0] - NA, name="proj_hgrn")
    attn, lse = _attn_fwd(proj_a, name="attn_fwd")
    lands = _copies_wait(ag1, _ag_chips_plan, attn, name="ag_chips_wait")
    lands = [lax.dynamic_update_index_in_dim(l, s, me, 0) for l, s in zip(lands, later)]
    ag2 = _copies_start([], lands, _ag_sibling_plan, 4 * len(later), name="ag_sibling_start")
    rec, states = _hgrn_fwd(proj_h, hgrn_lb_logits + ag2[4][0, 0], name="hgrn_fwd")
    g_out, g_up, g_down, g_cw = _copies_wait(ag2, _ag_sibling_plan, rec, name="ag_sibling_wait")
    wo = g_out.reshape(-1, D)
    wu = g_up.reshape(-1, D)
    wd = g_down.reshape(-1, D)
    cw = _gathered_cols(g_cw)
    h1, u2, mixed = _out_proj(attn, rec, proj_h, xs, attn_norm_g, hgrn_norm_g, norm2_g, wo, name="out_proj")
    gate, val, act = _up_glu(u2, wu, cw, conv_b, name="up_glu")
    dh2, dh2b, d_fng, loss_part = _down_loss(act, wd, h1, fng, target, name="down_loss")

    dgate, dval, d_cw, d_cb = _dact_glu_bwd(dh2b, wd, gate, val, cw, conv_b, name="dact_glu_bwd")
    dw_down = _mm_tn(act, dh2b, tm=256, name="dw_down")
    dh1, dh1b, d_n2g = _grad_norm_input([dgate, dval], [(wu, 0)], h1, norm2_g, dh2, name="du2_norm2_bwd")
    F = dgate.shape[1]
    dw_up = _mm_tn(dgate, u2, tm=256, rows=2 * F, name="dw_up_gate")
    dw_up = _mm_tn(dval, u2, tm=256, rows=2 * F, row_block=lambda i: i + F // 256, into=dw_up, name="dw_up_val")
    rs_ffn = _rs_start([dw_down.reshape(N_DEV, -1, D), dw_up.reshape(N_DEV, -1, D)], me, name="rs_ffn_start")
    dattn, delta, drec, dhg, d_ang, d_hng = _dmix_post_bwd(dh1b, wo, attn, rec, proj_h, attn_norm_g + rs_ffn[0][4][0, 0],
                                                          hgrn_norm_g, name="dmix_post_bwd")
    dw_out = _mm_tn(mixed, dh1b, name="dw_out")
    rs_out = _rs_start([dw_out.reshape(N_DEV, -1, D)], me, name="rs_out_start")
    dproj_h, d_lbl = _hgrn_bwd(proj_h, hgrn_lb_logits + rs_out[0][4][0, 0], states, drec, name="hgrn_bwd")
    small = [("loss", loss_part, None, None, None),
             ("attn_norm_g", d_ang, attn_norm_g, m_attn_norm_g, v_attn_norm_g),
             ("hgrn_norm_g", d_hng, hgrn_norm_g, m_hgrn_norm_g, v_hgrn_norm_g),
             ("hgrn_lb_logits", d_lbl, hgrn_lb_logits, m_hgrn_lb_logits, v_hgrn_lb_logits),
             ("norm2_g", d_n2g, norm2_g, m_norm2_g, v_norm2_g),
             ("conv_b", d_cb, conv_b, m_conv_b, v_conv_b),
             ("final_norm_g", d_fng, final_norm_g, m_final_norm_g, v_final_norm_g)]
    pack = lambda arrs: jnp.concatenate([a.reshape(1, -1) for a in arrs], axis=1)
    small_own = [pack([s[1] for s in small]), d_cw]
    ag_small = _copies_start(small_own, [lax.empty((N_DEV, *s.shape), s.dtype) for s in small_own], _ag_direct_plan,
                             (N_DEV - 1) * len(small_own), name="ag_small_start")
    dproj_a = _attn_bwd(proj_a, dattn, lse, delta, name="attn_bwd")
    pairs = ATTN_W // LANES
    dw_in = _mm_tn(dproj_a, u1, tm=LANES, rows=wi.shape[0], row_block=lambda i: pairs * (i % 3) + i // 3,
                   name="dw_in_attn")
    dw_in = _mm_tn(dproj_h, u1, tm=256, rows=wi.shape[0], row_block=lambda i: i + NA // 256, into=dw_in,
                   name="dw_in_hgrn")
    dw_in = _mm_tn(dhg, u1, tm=256, rows=wi.shape[0], row_block=lambda i: i + (NA + 3 * HGRN_W) // 256, into=dw_in,
                   name="dw_in_gate")
    rs_in = _rs_start([dw_in.reshape(N_DEV, -1, D)], me, name="rs_in_start", after=ag_small[4])
    grad_x, _, d_n1g = _grad_norm_input([dproj_a, dproj_h, dhg], [(wi_a, 0), (wi, NA)], xs,
                                        norm1_g + rs_in[0][4][0, 0], dh1, name="du1_norm1_bwd")

    res = {}

    def update(nm, parts, w, m, v, transposed=False):
        if transposed:
            res[nm] = [r.T[None] for r in _adamw(parts, t(w), t(m), t(v), name=f"adamw_{nm}")]
        else:
            res[nm] = _adamw(parts, w, m, v, name=f"adamw_{nm}")

    g_down, g_up = _rs_finish(rs_ffn, grad_x, name="rs_ffn_wait")
    update("w_down", g_down, w_down, m_w_down, v_w_down)
    update("w_up", g_up, w_up, m_w_up, v_w_up, transposed=True)
    (g_out,) = _rs_finish(rs_out, grad_x, name="rs_out_wait")
    update("w_out", g_out, w_out, m_w_out, v_w_out)
    (g_in,) = _rs_finish(rs_in, res["w_up"][1], name="rs_in_wait")
    update("w_in", g_in, w_in, m_w_in, v_w_in, transposed=True)

    g_small, g_dcw = [lax.dynamic_update_index_in_dim(l, s, me, 0)
                      for l, s in zip(_copies_wait(ag_small, _ag_direct_plan, grad_x, name="ag_small_wait"), small_own)]
    sm = _adamw_packed(g_small, [s[1].size for s in small], [None if s[2] is None else s[2:] for s in small],
                       name="adamw_small")
    for (nm, *_), r in zip(small, sm):
        res[nm] = r
    ncw = conv_w.shape[-1]
    mine_cw = lax.dynamic_slice_in_dim(g_dcw, me * ncw, ncw, axis=2)
    res["conv_w"] = _adamw([mine_cw], conv_w, m_conv_w, v_conv_w, name="adamw_conv_w")
    late, _ = lax.optimization_barrier((d_n1g, res["w_in"][1]))
    update("norm1_g", _all_gather([late], name="ag_norm1_grad"), norm1_g, m_norm1_g, v_norm1_g)

    loss = res["loss"][0][0, 0]
    order = ["norm1_g", "w_in", "attn_norm_g", "hgrn_norm_g", "hgrn_lb_logits", "w_out", "norm2_g", "w_up",
             "conv_w", "conv_b", "w_down", "final_norm_g"]
    return (loss, grad_x[None], *[res[nm][0] for nm in order], *[res[nm][1] for nm in order],
            *[res[nm][2] for nm in order], *[res[nm][3] for nm in order])
```

```python
import jax
import jax.numpy as jnp
from jax import lax
from jax.experimental import pallas as pl
from jax.experimental.pallas import tpu as pltpu

F32, BF16 = jnp.float32, jnp.bfloat16
NORM_EPS = 1e-6
ATTN_HEADS, HEAD_DIM, ATTN_BLOCK = 8, 64, 128
DILATIONS = (1, 4, 16)
ATTN_SCALE = HEAD_DIM ** -0.5
ATTN_W = ATTN_HEADS * HEAD_DIM
HGRN_HEADS, HGRN_DIM, HGRN_CHUNK = 4, 128, 64
HGRN_W = HGRN_HEADS * HGRN_DIM
ADAM_LR, ADAM_B1, ADAM_B2, ADAM_EPS, ADAM_WD, ADAM_STEP = 0.001, 0.9, 0.999, 1e-08, 0.01, 10
LANES, SUBLANES = 128, 8
VMEM_LIMIT_BYTES = 56 * 1024 * 1024
N_DEV = 8
MESH = pl.DeviceIdType.MESH
HBM = pl.BlockSpec(memory_space=pltpu.HBM)
HIGHEST = lax.Precision.HIGHEST


def _cparams(*sem):
    return pltpu.CompilerParams(dimension_semantics=sem, vmem_limit_bytes=VMEM_LIMIT_BYTES)


def _tile(n, pref):
    if n <= pref:
        return n
    t = (pref // LANES) * LANES
    while n % t:
        t -= LANES
    return t


def _resident(shape):
    return pl.BlockSpec(shape, lambda *_: (0,) * len(shape), pipeline_mode=pl.Buffered(1))


def _dot(a, b, dims, precision=None):
    return lax.dot_general(a, b, (dims, ((), ())), precision=precision, preferred_element_type=F32)


def _nn(a, b, precision=None):
    return _dot(a, b, ((1,), (0,)), precision)


def _nt(a, b):
    return _dot(a, b, ((1,), (1,)))


def _tn(a, b):
    return _dot(a, b, ((0,), (0,)))


def _sigmoid(x):
    return 1.0 / (1.0 + jnp.exp(-x))


def _rstd(x):
    return lax.rsqrt(jnp.mean(x * x, axis=-1, keepdims=True) + NORM_EPS)


def _norm_bwd(x, g, du):
    r = _rstd(x)
    xh = x * r
    dxh = du * g
    return r * (dxh - xh * jnp.mean(dxh * xh, axis=-1, keepdims=True)), du * xh


def _row_halves(tm):
    return [pl.ds(0, tm // 2), pl.ds(tm // 2, tm // 2)]


def _accumulate(ref, part, first):
    @pl.when(first)
    def _():
        ref[...] = part

    @pl.when(jnp.logical_not(first))
    def _():
        ref[...] += part


def _proj_hgrn(u, wt, row0, *, name, tm=1024):
    M, K = u.shape
    W = HGRN_W
    j0 = row0 // W

    def body(a_ref, b_ref, o_ref, f_ref):
        prod = _nt(a_ref[...], b_ref[...])
        o_ref[...] = prod.astype(BF16)

        @pl.when(pl.program_id(1) == 1)
        def _():
            f_ref[...] = prod

    return pl.pallas_call(
        body, name=name, grid=(M // tm, 4),
        in_specs=[pl.BlockSpec((tm, K), lambda i, j: (i, 0)), pl.BlockSpec((W, K), lambda i, j: (j + j0, 0))],
        out_specs=[pl.BlockSpec((tm, W), lambda i, j: (i, j)), pl.BlockSpec((tm, W), lambda i, j: (i, 0))],
        out_shape=[jax.ShapeDtypeStruct((M, 4 * W), BF16), jax.ShapeDtypeStruct((M, W), F32)],
        compiler_params=_cparams("parallel", "arbitrary"),
    )(u, wt)


def _mm_tn(x, dy, *, name, tm=512, tn=1024, rows=None, row_block=None, into=None):
    S, M = x.shape
    N = dy.shape[1]
    tm, tn = _tile(M, tm), _tile(N, tn)
    row_block = row_block or (lambda i: i)

    def body(x_ref, dy_ref, *rest):
        o_ref, xt_ref = rest[-2:]

        @pl.when(pl.program_id(1) == 0)
        def _():
            xt_ref[...] = x_ref[...].T

        o_ref[...] = _nn(xt_ref[...], dy_ref[...]).astype(BF16)

    operands = [x, dy] + ([] if into is None else [into])
    return pl.pallas_call(
        body, name=name, grid=(M // tm, N // tn),
        in_specs=[pl.BlockSpec((S, tm), lambda i, j: (0, i)), pl.BlockSpec((S, tn), lambda i, j: (0, j))]
        + ([] if into is None else [pl.BlockSpec(memory_space=pl.ANY)]),
        out_specs=pl.BlockSpec((tm, tn), lambda i, j: (row_block(i), j)),
        out_shape=jax.ShapeDtypeStruct((rows or M, N), BF16),
        input_output_aliases={} if into is None else {2: 0},
        scratch_shapes=[pltpu.VMEM((tm, S), BF16)], compiler_params=_cparams("parallel", "arbitrary"),
    )(*operands)


def _proj_attn(x, g, wt, *, name, tm=1024, tn=512):
    S, D = x.shape
    N = wt.shape[0]

    def body(x_ref, g_ref, w_ref, u_ref, o_ref):
        @pl.when(pl.program_id(1) == 0)
        def _():
            xv = x_ref[...]
            u_ref[...] = (xv * _rstd(xv) * g_ref[...]).astype(BF16)

        o_ref[...] = _nt(u_ref[...], w_ref[...]).astype(BF16)

    return pl.pallas_call(
        body, name=name, grid=(S // tm, N // tn),
        in_specs=[pl.BlockSpec((tm, D), lambda i, j: (i, 0)), pl.BlockSpec((1, D), lambda i, j: (0, 0)),
                  pl.BlockSpec((tn, D), lambda i, j: (j, 0))],
        out_specs=[pl.BlockSpec((tm, D), lambda i, j: (i, 0)), pl.BlockSpec((tm, tn), lambda i, j: (i, j))],
        out_shape=[jax.ShapeDtypeStruct((S, D), BF16), jax.ShapeDtypeStruct((S, N), BF16)],
        compiler_params=_cparams("parallel", "arbitrary"),
    )(x, g, wt)


PAIR_W = 3 * LANES
ATTN_UNROLL_FWD, ATTN_UNROLL_BWD = 4, 4


def _attn_masks(first):
    qi = lax.broadcasted_iota(jnp.int32, (ATTN_BLOCK, 2 * ATTN_BLOCK), 0)
    kj = lax.broadcasted_iota(jnp.int32, (ATTN_BLOCK, 2 * ATTN_BLOCK), 1)
    dist = qi + ATTN_BLOCK - kj
    valid = (dist >= 0) & (dist <= ATTN_BLOCK) & jnp.logical_or(kj >= ATTN_BLOCK, jnp.logical_not(first))
    lane = lax.broadcasted_iota(jnp.int32, (1, LANES), 1)
    return valid, lane


def _for_residue_blocks(S, d, fn):
    span = ATTN_BLOCK * d
    nb = S // span

    def step(n, carry):
        base = pl.multiple_of(n * span, span)
        for r in range(d):
            off = pl.multiple_of((r * nb + n) * ATTN_BLOCK, ATTN_BLOCK)
            fn(lambda ref, r=r: _block_rows(ref, base, r, d),
               lambda ref, val, r=r: _set_block_rows(ref, base, r, d, val), off)
        return carry

    lax.fori_loop(0, nb, step, 0)


def _for_blocks(S, unroll, fn):
    def step(i, carry):
        fn([(pl.multiple_of((i * unroll + u) * ATTN_BLOCK, ATTN_BLOCK), i * unroll + u) for u in range(unroll)])
        return carry

    lax.fori_loop(0, S // ATTN_BLOCK // unroll, step, 0)


def _head_value(x2, lane, e):
    return jnp.sum(jnp.where(lane == HEAD_DIM * e, x2, 0.0), axis=-1, keepdims=True)


def _block_rows(ref, base, r, d):
    if d == 1:
        return ref[pl.ds(base, ATTN_BLOCK), :]
    return ref.at[pl.ds(base, ATTN_BLOCK * d)][pl.ds(r, ATTN_BLOCK, stride=d), :]


def _set_block_rows(ref, base, r, d, val):
    if d == 1:
        ref[pl.ds(base, ATTN_BLOCK), :] = val
    else:
        ref.at[pl.ds(base, ATTN_BLOCK * d)][pl.ds(r, ATTN_BLOCK, stride=d), :] = val


def _order4_to_16(src, dst, pad):
    S = src.shape[0]
    q4, q16 = S // 4, S // 16
    for r in range(4):
        for a in range(4):
            for n in range(q16 // ATTN_BLOCK):
                rows = src.at[pl.ds(r * q4 + 4 * ATTN_BLOCK * n, 4 * ATTN_BLOCK)][pl.ds(a, ATTN_BLOCK, stride=4), :]
                dst[pl.ds(pad + (4 * a + r) * q16 + ATTN_BLOCK * n, ATTN_BLOCK), :] = rows.astype(dst.dtype)


def _order16_to_4(src, pad, dst):
    S = dst.shape[0]
    q4, q16 = S // 4, S // 16
    for r in range(4):
        for a in range(4):
            for n in range(q16 // ATTN_BLOCK):
                rows = src[pl.ds(pad + (4 * a + r) * q16 + ATTN_BLOCK * n, ATTN_BLOCK), :]
                dst.at[pl.ds(r * q4 + 4 * ATTN_BLOCK * n, 4 * ATTN_BLOCK)][pl.ds(a, ATTN_BLOCK, stride=4), :] = rows


def _regroup(S, d, pairs, tmp):
    for src, dst, pad in pairs:
        if d == 16:
            def to_tmp(rows, _, off, src=src):
                tmp[pl.ds(off, ATTN_BLOCK), :] = rows(src)

            _for_residue_blocks(S, 4, to_tmp)
            _order4_to_16(tmp, dst, pad)
    if d != 16:
        def to_dst(rows, _, off):
            for src, dst, pad in pairs:
                dst[pl.ds(pad + off, ATTN_BLOCK), :] = rows(src).astype(dst.dtype)

        _for_residue_blocks(S, d, to_dst)


def _split_pair(p_ref, qs, ks, vs, bk, bv):
    qs[...] = p_ref[:, 0:LANES].astype(F32)
    ks[...] = p_ref[:, LANES:2 * LANES].astype(F32)
    vs[...] = p_ref[:, 2 * LANES:3 * LANES].astype(F32)
    bk[0:ATTN_BLOCK, :] = jnp.zeros((ATTN_BLOCK, LANES), bk.dtype)
    bv[0:ATTN_BLOCK, :] = jnp.zeros((ATTN_BLOCK, LANES), bv.dtype)


def _attn_fwd(proj_a, *, name):
    S = proj_a.shape[0]

    def body(p_ref, o_ref, l_ref, qs, ks, vs, bq, bk, bv, bo, bl, to, tl):
        _split_pair(p_ref, qs, ks, vs, bk, bv)
        for d in DILATIONS:
            nb = S // (ATTN_BLOCK * d)
            _regroup(S, d, ((qs, bq, 0), (ks, bk, ATTN_BLOCK), (vs, bv, ATTN_BLOCK)), to)

            def blocks(group, nb=nb):
                lane = lax.broadcasted_iota(jnp.int32, (1, LANES), 1)
                heads = [(lane >= HEAD_DIM * e) & (lane < HEAD_DIM * (e + 1)) for e in range(LANES // HEAD_DIM)]
                wins = [pl.ds(off, 2 * ATTN_BLOCK) for off, _ in group]
                s = [[_nt(jnp.where(mh, bq[pl.ds(off, ATTN_BLOCK), :], jnp.zeros((ATTN_BLOCK, LANES), BF16)), bk[win, :])
                      for mh in heads] for (off, _), win in zip(group, wins)]
                p, m, l = [], [], []
                for (off, b), su in zip(group, s):
                    valid, _ = _attn_masks(jnp.bitwise_and(b, nb - 1) == 0)
                    sm = [jnp.where(valid, x * ATTN_SCALE, -jnp.inf) for x in su]
                    m.append([jnp.max(x, axis=-1, keepdims=True) for x in sm])
                    p.append([jnp.exp(x - mx) for x, mx in zip(sm, m[-1])])
                    l.append([jnp.sum(x, axis=-1, keepdims=True) for x in p[-1]])
                o = [[_nn(x.astype(BF16), bv[win, :]) for x in pu] for pu, win in zip(p, wins)]
                for (off, _), ou, mu, lu in zip(group, o, m, l):
                    o2 = jnp.zeros((ATTN_BLOCK, LANES), F32)
                    l2 = jnp.zeros((ATTN_BLOCK, LANES), F32)
                    for mh, oe, me_, le in zip(heads, ou, mu, lu):
                        o2 = jnp.where(mh, oe / le, o2)
                        l2 = jnp.where(mh, me_ + jnp.log(le), l2)
                    bo[pl.ds(off, ATTN_BLOCK), :] = o2
                    bl[pl.ds(off, ATTN_BLOCK), :] = l2

            _for_blocks(S, ATTN_UNROLL_FWD, blocks)

            if d == 16:
                _order16_to_4(bo, 0, to)
                _order16_to_4(bl, 0, tl)
            src_o, src_l = (to, tl) if d == 16 else (bo, bl)

            def merge(rows, set_rows, off, d=d, src_o=src_o, src_l=src_l):
                blk = pl.ds(off, ATTN_BLOCK)
                o2, l2 = src_o[blk, :], src_l[blk, :]
                if d != DILATIONS[0]:
                    lo, oo = rows(l_ref), rows(o_ref)
                    ln = jnp.maximum(lo, l2)
                    wa, wb = jnp.exp(lo - ln), jnp.exp(l2 - ln)
                    o2 = (wa * oo + wb * o2) / (wa + wb)
                    l2 = ln + jnp.log(wa + wb)
                set_rows(o_ref, o2)
                set_rows(l_ref, l2)

            _for_residue_blocks(S, min(d, 4), merge)

    slab = pl.BlockSpec((S, LANES), lambda p: (0, p))
    f32_slab, bf16_slab = pltpu.VMEM((S, LANES), F32), pltpu.VMEM((S, LANES), BF16)
    bf16_window = pltpu.VMEM((S + ATTN_BLOCK, LANES), BF16)
    return pl.pallas_call(
        body, name=name, grid=(ATTN_W // LANES,), in_specs=[pl.BlockSpec((S, PAIR_W), lambda p: (0, p))],
        out_specs=[slab, slab],
        out_shape=[jax.ShapeDtypeStruct((S, ATTN_W), F32), jax.ShapeDtypeStruct((S, ATTN_W), F32)],
        scratch_shapes=[f32_slab] * 3 + [bf16_slab, bf16_window, bf16_window] + [f32_slab] * 4,
        compiler_params=_cparams("parallel"),
    )(proj_a)


def _attn_bwd(proj_a, do, lse, delta, *, name):
    S = proj_a.shape[0]

    def body(p_ref, do_ref, lse_ref, dl_ref, o_ref, qs, ks, vs, dqs, dks, dvs, bq, bk, bv, bdo, blse, bdl, bdq, bdk, bdv,
             tmp):
        _split_pair(p_ref, qs, ks, vs, bk, bv)
        bdk[0:ATTN_BLOCK, :] = jnp.zeros((ATTN_BLOCK, LANES), F32)
        bdv[0:ATTN_BLOCK, :] = jnp.zeros((ATTN_BLOCK, LANES), F32)
        for d in DILATIONS:
            nb = S // (ATTN_BLOCK * d)
            _regroup(S, d, ((qs, bq, 0), (ks, bk, ATTN_BLOCK), (vs, bv, ATTN_BLOCK), (do_ref, bdo, 0),
                            (lse_ref, blse, 0), (dl_ref, bdl, 0)), tmp)

            def blocks(group, nb=nb):
                lane = lax.broadcasted_iota(jnp.int32, (1, LANES), 1)
                heads = [(lane >= HEAD_DIM * e) & (lane < HEAD_DIM * (e + 1)) for e in range(LANES // HEAD_DIM)]
                zero = jnp.zeros((ATTN_BLOCK, LANES), BF16)
                chains = [(off, b, e, mh) for off, b in group for e, mh in enumerate(heads)]
                qm = [jnp.where(mh, bq[pl.ds(off, ATTN_BLOCK), :], zero) for off, _, _, mh in chains]
                dom = [jnp.where(mh, bdo[pl.ds(off, ATTN_BLOCK), :], zero) for off, _, _, mh in chains]
                s = [_nt(x, bk[pl.ds(off, 2 * ATTN_BLOCK), :]) for x, (off, _, _, _) in zip(qm, chains)]
                dp = [_nt(x, bv[pl.ds(off, 2 * ATTN_BLOCK), :]) for x, (off, _, _, _) in zip(dom, chains)]
                p, ds = [], []
                for (off, b, e, _), sc, dpc in zip(chains, s, dp):
                    valid, _ = _attn_masks(jnp.bitwise_and(b, nb - 1) == 0)
                    blk = pl.ds(off, ATTN_BLOCK)
                    pc = jnp.where(valid, jnp.exp(sc * ATTN_SCALE - _head_value(blse[blk, :], lane, e)), 0.0)
                    ds.append((pc * (dpc - _head_value(bdl[blk, :], lane, e)) * ATTN_SCALE).astype(BF16))
                    p.append(pc.astype(BF16))
                dq = [_nn(x, bk[pl.ds(off, 2 * ATTN_BLOCK), :]) for x, (off, _, _, _) in zip(ds, chains)]
                dk = [_tn(x, y) for x, y in zip(ds, qm)]
                dv = [_tn(x, y) for x, y in zip(p, dom)]
                nh = len(heads)
                for u, (off, _) in enumerate(group):
                    dq2 = jnp.zeros((ATTN_BLOCK, LANES), F32)
                    for mh, x in zip(heads, dq[nh * u:nh * (u + 1)]):
                        dq2 = jnp.where(mh, x, dq2)
                    bdq[pl.ds(off, ATTN_BLOCK), :] = dq2
                    for acc, grads in ((bdk, dk), (bdv, dv)):
                        win_grad = sum(grads[nh * u + 1:nh * (u + 1)], grads[nh * u])
                        acc[pl.ds(off, ATTN_BLOCK), :] += win_grad[:ATTN_BLOCK]
                        acc[pl.ds(off + ATTN_BLOCK, ATTN_BLOCK), :] = win_grad[ATTN_BLOCK:]

            _for_blocks(S, ATTN_UNROLL_BWD, blocks)

            outs = ((dqs, bdq, 0), (dks, bdk, ATTN_BLOCK), (dvs, bdv, ATTN_BLOCK))
            if d == 16:
                for acc, grad, pad in outs:
                    _order16_to_4(grad, pad, tmp)

                    def add(rows, set_rows, off, acc=acc):
                        set_rows(acc, rows(acc) + tmp[pl.ds(off, ATTN_BLOCK), :])

                    _for_residue_blocks(S, 4, add)
            else:
                def scatter(rows, set_rows, off, d=d):
                    for acc, grad, pad in outs:
                        part = grad[pl.ds(pad + off, ATTN_BLOCK), :]
                        set_rows(acc, part if d == DILATIONS[0] else rows(acc) + part)

                _for_residue_blocks(S, d, scatter)
        o_ref[:, 0:LANES] = dqs[...].astype(BF16)
        o_ref[:, LANES:2 * LANES] = dks[...].astype(BF16)
        o_ref[:, 2 * LANES:3 * LANES] = dvs[...].astype(BF16)

    slab = pl.BlockSpec((S, LANES), lambda p: (0, p))
    pair = pl.BlockSpec((S, PAIR_W), lambda p: (0, p))
    f32_slab, bf16_slab = pltpu.VMEM((S, LANES), F32), pltpu.VMEM((S, LANES), BF16)
    f32_window, bf16_window = pltpu.VMEM((S + ATTN_BLOCK, LANES), F32), pltpu.VMEM((S + ATTN_BLOCK, LANES), BF16)
    return pl.pallas_call(
        body, name=name, grid=(ATTN_W // LANES,), in_specs=[pair, slab, slab, slab], out_specs=pair,
        out_shape=jax.ShapeDtypeStruct(proj_a.shape, BF16),
        scratch_shapes=[f32_slab] * 6 + [bf16_slab, bf16_window, bf16_window, bf16_slab, f32_slab, f32_slab,
                                         f32_slab, f32_window, f32_window, f32_slab],
        compiler_params=_cparams("parallel"),
    )(proj_a, do, lse, delta)


HG_T = 2 * HGRN_CHUNK
HG_GROUPS = 2
HG_STEP = HG_GROUPS * HG_T


def _hgrn_consts():
    row = lax.broadcasted_iota(jnp.int32, (HG_T, HG_T), 0)
    col = lax.broadcasted_iota(jnp.int32, (HG_T, HG_T), 1)
    same = (row >= HGRN_CHUNK) == (col >= HGRN_CHUNK)
    return row, same & (col <= row), same & (col >= row)


def _lower_bound(logits_ref):
    l0, l1 = logits_ref[0:1, :], logits_ref[1:2, :]
    mx = jnp.maximum(l0, l1)
    e0, e1 = jnp.exp(l0 - mx), jnp.exp(l1 - mx)
    return e0 / (e0 + e1)


def _hgrn_chains():
    chains = [(g, h) for g in range(HG_GROUPS) for h in range(HGRN_HEADS)]
    rows = [pl.ds(HG_T * g, HG_T) for g, _ in chains]
    lanes = [slice(HGRN_DIM * h, HGRN_DIM * (h + 1)) for _, h in chains]
    return chains, rows, lanes


def _hgrn_gates(qs, fs, lbs, row, causal):
    C = HGRN_CHUNK
    tri = jnp.where(causal, 1.0, 0.0).astype(F32)
    sgs = [_sigmoid(f) for f in fs]
    forgets = [lb + (1.0 - lb) * sg for lb, sg in zip(lbs, sgs)]
    logfs = [jnp.log(forget) for forget in forgets]
    bs = [_nn(tri, logf, HIGHEST) for logf in logfs]
    out = []
    for q, sg, forget, logf, b in zip(qs, sgs, forgets, logfs, bs):
        key = 1.0 - forget
        bend0 = jnp.sum(logf[:C], axis=0, keepdims=True)
        bend1 = jnp.sum(logf[C:], axis=0, keepdims=True)
        bend = jnp.where(row < C, bend0, bend1)
        eb, emb, eend = jnp.exp(b), jnp.exp(-b), jnp.exp(bend - b)
        sq = _sigmoid(q)
        out.append(dict(sg=sg, forget=forget, key=key, bend0=bend0, bend1=bend1, eb=eb, emb=emb, eend=eend, sq=sq,
                        qd=q * sq * eb, ki=key * emb, ke=key * eend))
    return out


def _hgrn_fwd(proj, proj_f, logits, *, name):
    S = proj.shape[0]
    W, C = HGRN_W, HGRN_CHUNK

    def body(q_ref, f_ref, i_ref, lg_ref, rec_ref, st_ref, s_ref):
        @pl.when(pl.program_id(0) == 0)
        def _():
            s_ref[...] = jnp.zeros_like(s_ref)

        row, causal, _ = _hgrn_consts()
        lb_all = _lower_bound(lg_ref)
        chains, rows, lanes = _hgrn_chains()
        n = range(len(chains))
        gts = _hgrn_gates([q_ref[rows[c], lanes[c]].astype(F32) for c in n], [f_ref[rows[c], lanes[c]] for c in n],
                          [lb_all[:, lanes[c]] for c in n], row, causal)
        qd, ki, ke = ([gt[k].astype(BF16) for gt in gts] for k in ("qd", "ki", "ke"))
        iv = [i_ref[rows[c], lanes[c]].astype(BF16) for c in n]
        a = [_nt(qd[c], ki[c]) for c in n]
        u0 = [_tn(iv[c][:C], ke[c][:C]) for c in n]
        u1 = [_tn(iv[c][C:], ke[c][C:]) for c in n]
        state = [s_ref[h] for h in range(HGRN_HEADS)]
        s0, s1 = [], []
        for c, (g, h) in enumerate(chains):
            s0.append(state[h])
            s1.append(jnp.exp(gts[c]["bend0"]) * s0[c] + u0[c])
            state[h] = jnp.exp(gts[c]["bend1"]) * s1[c] + u1[c]
        o0 = [_nt(qd[c][:C], s0[c].astype(BF16)) for c in n]
        o1 = [_nt(qd[c][C:], s1[c].astype(BF16)) for c in n]
        o = [_nn(jnp.where(causal, a[c], 0.0).astype(BF16), iv[c]) for c in n]
        for c, (g, h) in enumerate(chains):
            st_ref[2 * g, h] = s0[c]
            st_ref[2 * g + 1, h] = s1[c]
            rec_ref[rows[c], lanes[c]] = o[c] + jnp.concatenate([o0[c], o1[c]], axis=0)
        for h in range(HGRN_HEADS):
            s_ref[h] = state[h]

    blk = lambda j: pl.BlockSpec((HG_STEP, W), lambda t: (t, j))
    return pl.pallas_call(
        body, name=name, grid=(S // HG_STEP,),
        in_specs=[blk(0), blk(0), blk(2), pl.BlockSpec((2, W), lambda t: (0, 0))],
        out_specs=[blk(0), pl.BlockSpec((2 * HG_GROUPS, HGRN_HEADS, HGRN_DIM, HGRN_DIM), lambda t: (t, 0, 0, 0))],
        out_shape=[jax.ShapeDtypeStruct((S, W), F32),
                   jax.ShapeDtypeStruct((S // C, HGRN_HEADS, HGRN_DIM, HGRN_DIM), F32)],
        scratch_shapes=[pltpu.VMEM((HGRN_HEADS, HGRN_DIM, HGRN_DIM), F32)],
        compiler_params=_cparams("arbitrary"),
    )(proj, proj_f, proj, logits)


def _hgrn_bwd(proj, proj_f, logits, states, drec, *, name):
    S = proj.shape[0]
    W, C = HGRN_W, HGRN_CHUNK
    nt = S // HG_STEP

    def body(q_ref, f_ref, i_ref, lg_ref, st_ref, do_ref, dp_ref, dlg_ref, ds_ref, dlb_ref):
        t = pl.program_id(0)

        @pl.when(t == 0)
        def _():
            ds_ref[...] = jnp.zeros_like(ds_ref)
            dlb_ref[...] = jnp.zeros_like(dlb_ref)

        row, causal, anti = _hgrn_consts()
        lb_all = _lower_bound(lg_ref)
        chains, rows, lanes = _hgrn_chains()
        n = range(len(chains))
        qs, lbs = [q_ref[rows[c], lanes[c]].astype(F32) for c in n], [lb_all[:, lanes[c]] for c in n]
        gts = _hgrn_gates(qs, [f_ref[rows[c], lanes[c]] for c in n], lbs, row, causal)
        qd, ki, ke = ([gt[k] for gt in gts] for k in ("qd", "ki", "ke"))
        qdb, kib, keb = ([x.astype(BF16) for x in xs] for xs in (qd, ki, ke))
        iv = [i_ref[rows[c], lanes[c]].astype(BF16) for c in n]
        dob = [do_ref[rows[c], lanes[c]].astype(BF16) for c in n]
        s0 = [st_ref[2 * g, h] for g, h in chains]
        s1 = [st_ref[2 * g + 1, h] for g, h in chains]
        dec0, dec1 = [jnp.exp(gt["bend0"]) for gt in gts], [jnp.exp(gt["bend1"]) for gt in gts]
        a = [_nt(qdb[c], kib[c]) for c in n]
        da = [_nt(dob[c], iv[c]) for c in n]
        dqd1 = [_nn(dob[c][C:], s1[c].astype(BF16)) for c in n]
        dqd0 = [_nn(dob[c][:C], s0[c].astype(BF16)) for c in n]
        t1 = [_tn(dob[c][C:], qdb[c][C:]) for c in n]
        t0 = [_tn(dob[c][:C], qdb[c][:C]) for c in n]
        carry = [ds_ref[h] for h in range(HGRN_HEADS)]
        ds1, ds0 = [None] * len(chains), [None] * len(chains)
        for c in reversed(n):
            h = chains[c][1]
            ds1[c] = carry[h]
            ds0[c] = dec1[c] * ds1[c] + t1[c]
            carry[h] = dec0[c] * ds0[c] + t0[c]
        for h in range(HGRN_HEADS):
            ds_ref[h] = carry[h]
        ds1b, ds0b = [x.astype(BF16) for x in ds1], [x.astype(BF16) for x in ds0]
        a = [jnp.where(causal, x, 0.0).astype(BF16) for x in a]
        da = [jnp.where(causal, x, 0.0).astype(BF16) for x in da]
        di1 = [_nt(keb[c][C:], ds1b[c]) for c in n]
        dke1 = [_nn(iv[c][C:], ds1b[c]) for c in n]
        di0 = [_nt(keb[c][:C], ds0b[c]) for c in n]
        dke0 = [_nn(iv[c][:C], ds0b[c]) for c in n]
        dqd_a = [_nn(da[c], kib[c]) for c in n]
        dki = [_tn(da[c], qdb[c]) for c in n]
        di_a = [_tn(a[c], dob[c]) for c in n]
        dqd, dke, db = [], [], []
        for c in n:
            ddec1 = jnp.sum(ds1[c] * s1[c], axis=0, keepdims=True)
            ddec0 = jnp.sum(ds0[c] * s0[c], axis=0, keepdims=True)
            dqd.append(dqd_a[c] + jnp.concatenate([dqd0[c], dqd1[c]], axis=0))
            h = chains[c][1]
            dp_ref[rows[c], 2 * W + HGRN_DIM * h:2 * W + HGRN_DIM * (h + 1)] = (
                di_a[c] + jnp.concatenate([di0[c], di1[c]], axis=0)).astype(BF16)
            dke.append(jnp.concatenate([dke0[c], dke1[c]], axis=0))
            gke = dke[c] * ke[c]
            dbend0 = jnp.sum(gke[:C], axis=0, keepdims=True) + ddec0 * dec0[c]
            dbend1 = jnp.sum(gke[C:], axis=0, keepdims=True) + ddec1 * dec1[c]
            dbc = dqd[c] * qd[c] - dki[c] * ki[c] - gke
            db.append(dbc + jnp.where(row == C - 1, dbend0, 0.0) + jnp.where(row == HG_T - 1, dbend1, 0.0))
        tri = jnp.where(anti, 1.0, 0.0).astype(F32)
        dlogf = [_nn(tri, db[c], HIGHEST) for c in n]
        for c in n:
            gt, lb, q, h = gts[c], lbs[c], qs[c], chains[c][1]
            dforget = dlogf[c] / gt["forget"] - (dki[c] * gt["emb"] + dke[c] * gt["eend"])
            sg, sq = gt["sg"], gt["sq"]
            dp_ref[rows[c], W + HGRN_DIM * h:W + HGRN_DIM * (h + 1)] = (
                dforget * (1.0 - lb) * sg * (1.0 - sg)).astype(BF16)
            dlb_ref[:, lanes[c]] += jnp.sum(dforget * (1.0 - sg), axis=0, keepdims=True)
            dp_ref[rows[c], lanes[c]] = (dqd[c] * gt["eb"] * sq * (1.0 + q * (1.0 - sq))).astype(BF16)

        @pl.when(t == nt - 1)
        def _():
            dl0 = dlb_ref[...] * lb_all * (1.0 - lb_all)
            dlg_ref[0:1, :] = dl0
            dlg_ref[1:2, :] = -dl0

    blk = lambda j: pl.BlockSpec((HG_STEP, W), lambda t: (nt - 1 - t, j))
    full = pl.BlockSpec((2, W), lambda t: (0, 0))
    return pl.pallas_call(
        body, name=name, grid=(nt,),
        in_specs=[blk(0), blk(0), blk(2), full,
                  pl.BlockSpec((2 * HG_GROUPS, HGRN_HEADS, HGRN_DIM, HGRN_DIM), lambda t: (nt - 1 - t, 0, 0, 0)), blk(0)],
        out_specs=[pl.BlockSpec((HG_STEP, 3 * W), lambda t: (nt - 1 - t, 0)), full],
        out_shape=[jax.ShapeDtypeStruct((S, 3 * W), BF16), jax.ShapeDtypeStruct((2, W), F32)],
        scratch_shapes=[pltpu.VMEM((HGRN_HEADS, HGRN_DIM, HGRN_DIM), F32), pltpu.VMEM((1, W), F32)],
        compiler_params=_cparams("arbitrary"),
    )(proj, proj_f, proj, logits, states, drec)


def _out_proj(attn, rec, proj_h, x, g_attn, g_hgrn, g_norm2, w_out, *, name, tm=512):
    S, D = x.shape
    AW, W = ATTN_W, HGRN_W

    def body(a_ref, r_ref, hg_ref, x_ref, ga_ref, gh_ref, g2_ref, w_ref, h_ref, u_ref, m_ref):
        av = a_ref[...]
        m_ref[:, :AW] = (av * _rstd(av) * ga_ref[...]).astype(BF16)
        for h in range(HGRN_HEADS):
            sl = slice(HGRN_DIM * h, HGRN_DIM * (h + 1))
            rv, hg = r_ref[:, sl], hg_ref[:, sl].astype(F32)
            m_ref[:, AW + HGRN_DIM * h:AW + HGRN_DIM * (h + 1)] = (
                (rv * _rstd(rv) * gh_ref[:, sl]) * (hg * _sigmoid(hg))).astype(BF16)
        h1 = x_ref[...] + _nn(m_ref[...], w_ref[...])
        h_ref[...] = h1
        u_ref[...] = (h1 * _rstd(h1) * g2_ref[...]).astype(BF16)

    row = lambda w, j=0: pl.BlockSpec((tm, w), lambda i: (i, j))
    vec = lambda w: pl.BlockSpec((1, w), lambda i: (0, 0))
    return pl.pallas_call(
        body, name=name, grid=(S // tm,),
        in_specs=[row(AW), row(W), row(W, 3), row(D), vec(AW), vec(W), vec(D), _resident(w_out.shape)],
        out_specs=[row(D), row(D), row(AW + W)],
        out_shape=[jax.ShapeDtypeStruct((S, D), F32), jax.ShapeDtypeStruct((S, D), BF16),
                   jax.ShapeDtypeStruct((S, AW + W), BF16)],
        compiler_params=_cparams("parallel"),
    )(attn, rec, proj_h, x, g_attn, g_hgrn, g_norm2, w_out)


def _dmix_post_bwd(dh1b, w_out, attn, rec, proj_h, g_attn, g_hgrn, *, name, tm=512):
    S, D = dh1b.shape
    AW, W = ATTN_W, HGRN_W

    def body(dh_ref, w_ref, a_ref, r_ref, hg_ref, ga_ref, gh_ref, do_ref, dl_ref, dr_ref, dhg_ref, dga_ref, dgh_ref):
        first = pl.program_id(0) == 0
        dmix = _nt(dh_ref[...], w_ref[...])
        av = a_ref[...]
        dov, dga = _norm_bwd(av, ga_ref[...], dmix[:, :AW])
        do_ref[...] = dov
        shift = HEAD_DIM.bit_length() - 1
        hi = lax.shift_right_logical(lax.broadcasted_iota(jnp.int32, (AW, AW), 0), shift)
        hj = lax.shift_right_logical(lax.broadcasted_iota(jnp.int32, (AW, AW), 1), shift)
        prod = dov * av
        hi_part = prod.astype(BF16)
        lo_part = (prod - hi_part.astype(F32)).astype(BF16)
        same_head = jnp.where(hi == hj, 1.0, 0.0).astype(BF16)
        dl_ref[...] = _nn(hi_part, same_head) + _nn(lo_part, same_head)
        _accumulate(dga_ref, jnp.sum(dga, axis=0, keepdims=True), first)

        @pl.when(first)
        def _():
            dgh_ref[...] = jnp.zeros_like(dgh_ref)

        for h in range(HGRN_HEADS):
            sl = slice(HGRN_DIM * h, HGRN_DIM * (h + 1))
            rv, hg, gv = r_ref[:, sl], hg_ref[:, sl].astype(F32), gh_ref[:, sl]
            dout = dmix[:, AW + HGRN_DIM * h:AW + HGRN_DIM * (h + 1)]
            sg = _sigmoid(hg)
            drv, dgh = _norm_bwd(rv, gv, dout * (hg * sg))
            dr_ref[:, sl] = drv
            dgh_ref[:, sl] += jnp.sum(dgh, axis=0, keepdims=True)
            dhg_ref[:, sl] = (dout * (rv * _rstd(rv) * gv) * (sg * (1.0 + hg * (1.0 - sg)))).astype(BF16)

    row = lambda w, j=0: pl.BlockSpec((tm, w), lambda i: (i, j))
    vec = lambda w: pl.BlockSpec((1, w), lambda i: (0, 0))
    return pl.pallas_call(
        body, name=name, grid=(S // tm,),
        in_specs=[row(D), _resident(w_out.shape), row(AW), row(W), row(W, 3), vec(AW), vec(W)],
        out_specs=[row(AW), row(AW), row(W), row(W), vec(AW), vec(W)],
        out_shape=[jax.ShapeDtypeStruct((S, AW), F32), jax.ShapeDtypeStruct((S, AW), F32),
                   jax.ShapeDtypeStruct((S, W), F32), jax.ShapeDtypeStruct((S, W), BF16),
                   jax.ShapeDtypeStruct((1, AW), F32), jax.ShapeDtypeStruct((1, W), F32)],
        compiler_params=_cparams("arbitrary"),
    )(dh1b, w_out, attn, rec, proj_h, g_attn, g_hgrn)


def _conv_act(g, g1, g2, w_ref, b_ref):
    c = b_ref[...] + w_ref[0:1, :] * g2 + w_ref[1:2, :] * g1 + w_ref[2:3, :] * g
    return c, 0.5 * (1.0 + lax.erf(c * (2.0 ** -0.5)))


def _shift_down(g, halo, row):
    g1 = jnp.where(row == 0, halo[7:8], pltpu.roll(g, 1, 0))
    g2 = jnp.where(row == 0, halo[6:7], jnp.where(row == 1, halo[7:8], pltpu.roll(g, 2, 0)))
    return g1, g2


def _shift_up(x, halo, row):
    n = x.shape[0]
    x1 = jnp.where(row == n - 1, halo[0:1], pltpu.roll(x, n - 1, 0))
    x2 = jnp.where(row == n - 2, halo[0:1], jnp.where(row == n - 1, halo[1:2], pltpu.roll(x, n - 2, 0)))
    return x1, x2


def _up_glu(u, wt_up, conv_w, conv_b, *, name, tm=1024, tn=256):
    S, D = u.shape
    F = wt_up.shape[0] // 2
    nf = F // tn

    def body(u_ref, wg_ref, wv_ref, cw_ref, cb_ref, g_ref, v_ref, a_ref, halo_ref):
        i, j = pl.program_id(0), pl.program_id(1)

        @pl.when(i == 0)
        def _():
            halo_ref[j] = jnp.zeros((SUBLANES, tn), F32)

        uv = u_ref[...]
        g, v = _nt(uv, wg_ref[...]), _nt(uv, wv_ref[...])
        row = lax.broadcasted_iota(jnp.int32, (tm, tn), 0)
        g1, g2 = _shift_down(g, halo_ref[j], row)
        c, cdf = _conv_act(g, g1, g2, cw_ref, cb_ref)
        a_ref[...] = (c * cdf * v).astype(BF16)
        g_ref[...] = g.astype(BF16)
        v_ref[...] = v.astype(BF16)
        halo_ref[j] = g[tm - SUBLANES:, :]

    col = pl.BlockSpec((tm, tn), lambda i, j: (i, j))
    out = jax.ShapeDtypeStruct((S, F), BF16)
    return pl.pallas_call(
        body, name=name, grid=(S // tm, nf),
        in_specs=[pl.BlockSpec((tm, D), lambda i, j: (i, 0)), pl.BlockSpec((tn, D), lambda i, j: (j, 0)),
                  pl.BlockSpec((tn, D), lambda i, j: (j + nf, 0)), pl.BlockSpec((3, tn), lambda i, j: (0, j)),
                  pl.BlockSpec((1, tn), lambda i, j: (0, j))],
        out_specs=[col, col, col], out_shape=[out, out, out],
        scratch_shapes=[pltpu.VMEM((nf, SUBLANES, tn), F32)], compiler_params=_cparams("arbitrary", "arbitrary"),
    )(u, wt_up, wt_up, conv_w, conv_b)


def _dact_glu_bwd(dh2b, w_down, gate, val, conv_w, conv_b, *, name, tm=1024, tn=256):
    S, D = dh2b.shape
    F = gate.shape[1]
    nf, ni = F // tn, S // tm
    hb = tm // SUBLANES

    def body(dh_ref, wd_ref, g_ref, gh_ref, v_ref, cw_ref, cb_ref, dg_ref, dv_ref, dcw_ref, dcb_ref, halo_ref, acc_ref):
        i, j = pl.program_id(0), pl.program_id(1)

        @pl.when(i == 0)
        def _():
            halo_ref[j] = jnp.zeros((SUBLANES, tn), F32)
            acc_ref[j] = jnp.zeros((SUBLANES, tn), F32)

        g = g_ref[...].astype(F32)
        before = jnp.where(i < ni - 1, gh_ref[...].astype(F32), 0.0)
        row = lax.broadcasted_iota(jnp.int32, (tm, tn), 0)
        g1, g2 = _shift_down(g, before[SUBLANES:], row)
        c, cdf = _conv_act(g, g1, g2, cw_ref, cb_ref)
        da = _nt(dh_ref[...], wd_ref[...])
        dv_ref[...] = (da * (c * cdf)).astype(BF16)
        pdf = jnp.exp(-0.5 * c * c) * (1.0 / (2.0 * jnp.pi) ** 0.5)
        dc = da * v_ref[...].astype(F32) * (cdf + c * pdf)
        d1, d2 = _shift_up(dc, halo_ref[j], row)
        dg_ref[...] = (cw_ref[2:3, :] * dc + cw_ref[1:2, :] * d1 + cw_ref[0:1, :] * d2).astype(BF16)
        halo_ref[j] = dc[:SUBLANES, :]
        for k, t in enumerate((dc * g2, dc * g1, dc * g, dc)):
            acc_ref[j, k:k + 1, :] += jnp.sum(t, axis=0, keepdims=True)

        @pl.when((i == ni - 1) & (j == nf - 1))
        def _():
            for jj in range(nf):
                dcw_ref[:, jj * tn:(jj + 1) * tn] = acc_ref[jj, 0:3, :]
                dcb_ref[:, jj * tn:(jj + 1) * tn] = acc_ref[jj, 3:4, :]

    tile = pl.BlockSpec((tm, tn), lambda i, j: (ni - 1 - i, j))
    return pl.pallas_call(
        body, name=name, grid=(ni, nf),
        in_specs=[pl.BlockSpec((tm, D), lambda i, j: (ni - 1 - i, 0)), pl.BlockSpec((tn, D), lambda i, j: (j, 0)),
                  tile, pl.BlockSpec((SUBLANES * 2, tn), lambda i, j: (jnp.maximum((ni - 1 - i) * (hb // 2) - 1, 0), j)),
                  tile, pl.BlockSpec((3, tn), lambda i, j: (0, j)), pl.BlockSpec((1, tn), lambda i, j: (0, j))],
        out_specs=[tile, tile, pl.BlockSpec((3, F), lambda i, j: (0, 0)), pl.BlockSpec((1, F), lambda i, j: (0, 0))],
        out_shape=[jax.ShapeDtypeStruct((S, F), BF16), jax.ShapeDtypeStruct((S, F), BF16),
                   jax.ShapeDtypeStruct((3, F), F32), jax.ShapeDtypeStruct((1, F), F32)],
        scratch_shapes=[pltpu.VMEM((nf, SUBLANES, tn), F32), pltpu.VMEM((nf, SUBLANES, tn), F32)],
        compiler_params=_cparams("arbitrary", "arbitrary"),
    )(dh2b, w_down, gate, gate, val, conv_w, conv_b)


def _down_loss(act, w_down, h1, g, target, *, name, tm=512):
    S, F = act.shape
    D = h1.shape[1]

    def body(a_ref, w_ref, h_ref, g_ref, t_ref, dh_ref, dhb_ref, dg_ref, loss_ref):
        first = pl.program_id(0) == 0
        h2 = h_ref[...] + _nn(a_ref[...], w_ref[...])
        gv = g_ref[...]
        r = _rstd(h2)
        xh = h2 * r
        err = xh * gv - t_ref[...]
        part_loss = 0.5 * jnp.sum(jnp.mean(err * err, axis=-1, keepdims=True), axis=0, keepdims=True)
        dy = err * (1.0 / D)
        dxh = dy * gv
        dh = r * (dxh - xh * jnp.mean(dxh * xh, axis=-1, keepdims=True))
        dh_ref[...] = dh
        dhb_ref[...] = dh.astype(BF16)
        _accumulate(dg_ref, jnp.sum(dy * xh, axis=0, keepdims=True), first)
        _accumulate(loss_ref, jnp.broadcast_to(part_loss, (1, LANES)), first)

    row = lambda w: pl.BlockSpec((tm, w), lambda i: (i, 0))
    vec = lambda w: pl.BlockSpec((1, w), lambda i: (0, 0))
    return pl.pallas_call(
        body, name=name, grid=(S // tm,), in_specs=[row(F), _resident(w_down.shape), row(D), vec(D), row(D)],
        out_specs=[row(D), row(D), vec(D), vec(LANES)],
        out_shape=[jax.ShapeDtypeStruct((S, D), F32), jax.ShapeDtypeStruct((S, D), BF16),
                   jax.ShapeDtypeStruct((1, D), F32), jax.ShapeDtypeStruct((1, LANES), F32)],
        compiler_params=_cparams("arbitrary"),
    )(act, w_down, h1, g, target)


def _grad_norm_input(pieces, ws, x, g, add, *, name, tm=512):
    S, D = x.shape
    widths = [p.shape[1] for p in pieces]
    n, nw = len(pieces), len(ws)
    where, wi, off = [], 0, ws[0][1]
    for wd in widths:
        if off == ws[wi][0].shape[0]:
            wi, off = wi + 1, ws[wi + 1][1]
        where.append((wi, off))
        off += wd
    ws = [w for w, _ in ws]

    def body(*refs):
        p_refs, w_refs = refs[:n], refs[n:n + nw]
        x_ref, g_ref, add_ref, dx_ref, dxb_ref, dg_ref = refs[n + nw:]
        halves = _row_halves(tm)
        du = []
        for rows in halves:
            terms = [_nn(p_refs[k][rows, :], w_refs[wi][off:off + widths[k], :]) for k, (wi, off) in enumerate(where)]
            du.append(sum(terms[1:], terms[0]))
        dg_sum = None
        for rows, duh in zip(halves, du):
            dx, dg = _norm_bwd(x_ref[rows, :], g_ref[...], duh)
            dx = add_ref[rows, :] + dx
            dx_ref[rows, :] = dx
            dxb_ref[rows, :] = dx.astype(BF16)
            part = jnp.sum(dg, axis=0, keepdims=True)
            dg_sum = part if dg_sum is None else dg_sum + part
        _accumulate(dg_ref, dg_sum, pl.program_id(0) == 0)

    row = lambda w_: pl.BlockSpec((tm, w_), lambda i: (i, 0))
    vec = pl.BlockSpec((1, D), lambda i: (0, 0))
    return pl.pallas_call(
        body, name=name, grid=(S // tm,),
        in_specs=[row(wd) for wd in widths] + [_resident(w.shape) for w in ws] + [row(D), vec, row(D)],
        out_specs=[row(D), row(D), vec],
        out_shape=[jax.ShapeDtypeStruct((S, D), F32), jax.ShapeDtypeStruct((S, D), BF16),
                   jax.ShapeDtypeStruct((1, D), F32)],
        compiler_params=_cparams("arbitrary"),
    )(*pieces, *ws, x, g, add)


def _rows(a):
    return a.reshape(-1, a.shape[-1])


def _row_tile(rows, cols, itemsize=4, budget=1 << 20):
    t = rows
    while t % 32 == 0 and t * cols * itemsize > budget:
        t //= 2
    return t


def _sum_cast(arrs, out_dtype, *, name):
    shape = arrs[0].shape
    flat = [_rows(a) for a in arrs]
    R, C = flat[0].shape
    tr = _row_tile(R, C)

    def body(*refs):
        acc = refs[0][...].astype(F32)
        for r in refs[1:-1]:
            acc = acc + r[...].astype(F32)
        refs[-1][...] = acc.astype(out_dtype)

    spec = pl.BlockSpec((tr, C), lambda i: (i, 0))
    return pl.pallas_call(
        body, name=name, grid=(R // tr,), in_specs=[spec] * len(flat), out_specs=spec,
        out_shape=jax.ShapeDtypeStruct((R, C), out_dtype), compiler_params=_cparams("parallel"),
    )(*flat).reshape(shape)


def _adamw(parts, w, m, v, *, name):
    shape = w.shape
    w2, m2, v2 = _rows(w), _rows(m), _rows(v)
    R, C = w2.shape
    parts = [p.reshape(-1, R, C) for p in parts]
    tr = _row_tile(R, C)
    np_ = len(parts)
    c1, c2 = 1.0 - ADAM_B1 ** ADAM_STEP, 1.0 - ADAM_B2 ** ADAM_STEP

    def body(*refs):
        terms = [(r, k) for r in refs[:np_] for k in range(r.shape[0])]
        g = terms[0][0][terms[0][1]].astype(F32)
        for r, k in terms[1:]:
            g = g + r[k].astype(F32)
        w_ref, m_ref, v_ref, g_out, d_out, m_out, v_out = refs[np_:]
        mn = ADAM_B1 * m_ref[...] + (1.0 - ADAM_B1) * g
        vn = ADAM_B2 * v_ref[...] + (1.0 - ADAM_B2) * (g * g)
        g_out[...] = g
        d_out[...] = -ADAM_LR * ((mn / c1) / (jnp.sqrt(vn / c2) + ADAM_EPS) + ADAM_WD * w_ref[...])
        m_out[...] = mn
        v_out[...] = vn

    spec = pl.BlockSpec((tr, C), lambda i: (i, 0))
    out = jax.ShapeDtypeStruct((R, C), F32)
    stacks = [pl.BlockSpec((p.shape[0], tr, C), lambda i: (0, i, 0)) for p in parts]
    res = pl.pallas_call(
        body, name=name, grid=(R // tr,), in_specs=stacks + [spec] * 3, out_specs=[spec] * 4,
        out_shape=[out] * 4, compiler_params=_cparams("parallel"),
    )(*parts, w2, m2, v2)
    return [r.reshape(shape) for r in res]


def _adamw_packed(stack, widths, params, *, name):
    c1, c2 = 1.0 - ADAM_B1 ** ADAM_STEP, 1.0 - ADAM_B2 ** ADAM_STEP
    k = stack.shape[0]
    flat = [None if p is None else [_rows(a) for a in p] for p in params]
    n_in = sum(3 for p in flat if p is not None)

    def body(*refs):
        s_ref, ins, outs = refs[0], list(refs[1:1 + n_in]), list(refs[1 + n_in:])
        off = 0
        for width, p in zip(widths, flat):
            rows = 1 if p is None else p[0].shape[0]
            cols = width // rows
            w_ref, m_ref, v_ref = (None, None, None) if p is None else (ins.pop(0), ins.pop(0), ins.pop(0))
            o_refs = [outs.pop(0) for _ in range(1 if p is None else 4)]
            for r in range(rows):
                seg = slice(off + r * cols, off + (r + 1) * cols)
                g = s_ref[0, :, seg]
                for j in range(1, k):
                    g = g + s_ref[j, :, seg]
                o_refs[0][r:r + 1, :] = g
                if p is not None:
                    row = slice(r, r + 1)
                    mn = ADAM_B1 * m_ref[row, :] + (1.0 - ADAM_B1) * g
                    vn = ADAM_B2 * v_ref[row, :] + (1.0 - ADAM_B2) * (g * g)
                    o_refs[1][row, :] = -ADAM_LR * ((mn / c1) / (jnp.sqrt(vn / c2) + ADAM_EPS) + ADAM_WD * w_ref[row, :])
                    o_refs[2][row, :] = mn
                    o_refs[3][row, :] = vn
            off += width

    operands, out_shape = [stack], []
    for width, p in zip(widths, flat):
        if p is None:
            out_shape.append(jax.ShapeDtypeStruct((1, width), F32))
        else:
            operands += p
            out_shape += [jax.ShapeDtypeStruct(p[0].shape, F32)] * 4
    res = list(pl.pallas_call(body, name=name, out_shape=out_shape)(*operands))
    out = []
    for p, orig in zip(flat, params):
        n = 1 if p is None else 4
        out.append([r if orig is None else r.reshape(orig[0].shape) for r in res[:n]])
        res = res[n:]
    return out


def _coords():
    return lax.axis_index("x"), lax.axis_index("y"), lax.axis_index("c")


def _all_gather(shards, *, name):
    n = len(shards)

    def body(*refs):
        x_refs, out_refs = refs[:n], refs[n:2 * n]
        send_sems, recv_sems, local_sems = refs[2 * n:]
        x, y, c = _coords()
        me, sibling = (x, y, c), (x, y, 1 - c)
        chips = [(1 - x, y), (x, 1 - y), (1 - x, 1 - y)]

        def slot(a, dev):
            return out_refs[a].at[4 * dev[0] + 2 * dev[1] + dev[2]]

        def copy(a, k, block, to, src=None):
            return pltpu.make_async_remote_copy(
                src_ref=slot(a, block) if src is None else src, dst_ref=slot(a, block),
                send_sem=send_sems.at[7 * a + k], recv_sem=recv_sems.at[7 * a + k], device_id=to, device_id_type=MESH)

        mine = [pltpu.make_async_copy(x_refs[a], slot(a, me), local_sems.at[a]) for a in range(n)]
        for cp in mine:
            cp.start()
        first = []
        for a in range(n):
            first.append(copy(a, 0, me, sibling, src=x_refs[a]))
            first += [copy(a, 1 + j, me, (*chip, c), src=x_refs[a]) for j, chip in enumerate(chips)]
        for cp in first:
            cp.start()
        passed = []
        for j, chip in enumerate(chips):
            for a in range(n):
                copy(a, 1 + j, (*chip, c), me).wait_recv()
                fwd = copy(a, 4 + j, (*chip, c), sibling)
                fwd.start()
                passed.append(fwd)
        for a in range(n):
            copy(a, 0, sibling, me).wait_recv()
            for j, chip in enumerate(chips):
                copy(a, 4 + j, (*chip, 1 - c), me).wait_recv()
        for cp in first + passed:
            cp.wait_send()
        for cp in mine:
            cp.wait()

    return pl.pallas_call(
        body, name=name, in_specs=[HBM] * n, out_specs=[HBM] * n,
        out_shape=[jax.ShapeDtypeStruct((N_DEV, *s.shape), s.dtype) for s in shards],
        scratch_shapes=[pltpu.SemaphoreType.DMA((7 * n,)), pltpu.SemaphoreType.DMA((7 * n,)),
                        pltpu.SemaphoreType.DMA((n,))],
    )(*shards)


def _flip_y(x, y, c):
    return (x, 1 - y, c)


def _flip_x(x, y, c):
    return (1 - x, y, c)


def _flip_xy(x, y, c):
    return (1 - x, 1 - y, c)


SEM = pl.BlockSpec(memory_space=pltpu.SEMAPHORE)
SIDE_EFFECT = pltpu.SideEffectType.DATAFLOW_SIDE_EFFECTING


def _in_hbm(a):
    return pltpu.with_memory_space_constraint(a, pltpu.HBM)


def _copies_start(srcs, lands, plan, n_copies, *, name, after=None):
    ns, nl = len(srcs), len(lands)
    extra = [] if after is None else [after]

    def body(*refs):
        src_refs, land_refs = refs[:ns], refs[ns:ns + nl]
        send_sems, recv_sems = refs[ns + nl + len(extra):ns + nl + len(extra) + 2]
        token = refs[-1]
        for k, (src, dst, peer, _) in enumerate(plan(src_refs, land_refs, *_coords())):
            pltpu.make_async_remote_copy(src_ref=src, dst_ref=dst, send_sem=send_sems.at[k], recv_sem=recv_sems.at[k],
                                         device_id=peer, device_id_type=MESH).start()
        token[...] = jnp.zeros_like(token)

    bufs = [*srcs, *lands]
    res = pl.pallas_call(
        body, name=name, in_specs=[HBM] * (ns + nl) + [pl.BlockSpec(memory_space=pl.ANY)] * len(extra),
        out_specs=(SEM, SEM, *[HBM] * (ns + nl), pl.BlockSpec(memory_space=pltpu.VMEM)),
        out_shape=(pltpu.SemaphoreType.DMA((n_copies,)), pltpu.SemaphoreType.DMA((n_copies,)),
                   *[pltpu.HBM(b.shape, b.dtype) for b in bufs], jax.ShapeDtypeStruct((SUBLANES, LANES), F32)),
        input_output_aliases={i: 2 + i for i in range(ns + nl)},
        compiler_params=pltpu.CompilerParams(has_side_effects=SIDE_EFFECT),
    )(*[_in_hbm(b) for b in bufs], *extra)
    return res[0], res[1], list(res[2:2 + ns]), list(res[2 + ns:2 + ns + nl]), res[-1]


def _copies_wait(started, plan, after, *, name):
    send_sems, recv_sems, srcs, lands, _ = started
    ns, nl = len(srcs), len(lands)

    def body(*refs):
        src_refs, land_refs = refs[:ns], refs[ns:ns + nl]
        send_sems, recv_sems = refs[ns + nl:ns + nl + 2]
        for k, (src, dst, peer, here) in enumerate(plan(src_refs, land_refs, *_coords())):
            pltpu.make_async_remote_copy(src_ref=src, dst_ref=dst, send_sem=send_sems.at[k], recv_sem=recv_sems.at[k],
                                         device_id=peer, device_id_type=MESH).wait_send()
            pltpu.make_async_remote_copy(src_ref=src, dst_ref=here, send_sem=send_sems.at[k], recv_sem=recv_sems.at[k],
                                         device_id=peer, device_id_type=MESH).wait_recv()

    bufs = [*srcs, *lands]
    res = pl.pallas_call(
        body, name=name, in_specs=[HBM] * (ns + nl) + [SEM, SEM, pl.BlockSpec(memory_space=pl.ANY)],
        out_specs=[HBM] * (ns + nl), out_shape=[pltpu.HBM(b.shape, b.dtype) for b in bufs],
        input_output_aliases={i: i for i in range(ns + nl)},
        compiler_params=pltpu.CompilerParams(has_side_effects=SIDE_EFFECT),
    )(*bufs, send_sems, recv_sems, after)
    return list(res[ns:])


def _dev_index(dev):
    return 4 * dev[0] + 2 * dev[1] + dev[2]


def _ag_chips_plan(src_refs, land_refs, x, y, c):
    me = _dev_index((x, y, c))
    return [(src, land.at[me], peer, land.at[_dev_index(peer)])
            for src, land in zip(src_refs, land_refs) for peer in (_flip_y(x, y, c), _flip_x(x, y, c), _flip_xy(x, y, c))]


def _ag_sibling_plan(src_refs, land_refs, x, y, c):
    chips = [(x, y), (x, 1 - y), (1 - x, y), (1 - x, 1 - y)]
    return [(land.at[_dev_index((*chip, c))], land.at[_dev_index((*chip, c))], (x, y, 1 - c),
             land.at[_dev_index((*chip, 1 - c))]) for land in land_refs for chip in chips]


def _ag_direct_plan(src_refs, land_refs, x, y, c):
    me = _dev_index((x, y, c))
    plan = []
    for src, land in zip(src_refs, land_refs):
        for m in range(1, N_DEV):
            peer = (x + (m >> 2) * (1 - 2 * x), y + ((m >> 1) & 1) * (1 - 2 * y), c + (m & 1) * (1 - 2 * c))
            plan.append((src, land.at[me], peer, land.at[_dev_index(peer)]))
    return plan


def _rs_direct_plan(src_refs, land_refs, x, y, c):
    plan = []
    for src, land in zip(src_refs, land_refs):
        for m in range(1, N_DEV):
            peer = (x + (m >> 2) * (1 - 2 * x), y + ((m >> 1) & 1) * (1 - 2 * y), c + (m & 1) * (1 - 2 * c))
            plan.append((src.at[_dev_index(peer)], land.at[m - 1], peer, land.at[m - 1]))
    return plan


def _rs_start(grads, me, *, name, after=None):
    own = [lax.dynamic_index_in_dim(g, me, 0, keepdims=False) for g in grads]
    lands = [lax.empty((N_DEV - 1, *g.shape[1:]), g.dtype) for g in grads]
    return _copies_start(grads, lands, _rs_direct_plan, (N_DEV - 1) * len(grads), name=name, after=after), own


def _rs_finish(started, after, *, name):
    handle, own = started
    got = _copies_wait(handle, _rs_direct_plan, after, name=name)
    return [[o, land] for o, land in zip(own, got)]


def _gathered_cols(w8):
    return w8.transpose(1, 0, 2).reshape(w8.shape[1], -1)


def _pair_major(wt):
    return wt.reshape(3, ATTN_W // LANES, LANES, -1).transpose(1, 0, 2, 3).reshape(3 * ATTN_W, -1)


def kernel(x, norm1_g, w_in, attn_norm_g, hgrn_norm_g, hgrn_lb_logits, w_out, norm2_g, w_up, conv_w, conv_b, w_down, final_norm_g, loss_target, m_norm1_g, m_w_in, m_attn_norm_g, m_hgrn_norm_g, m_hgrn_lb_logits, m_w_out, m_norm2_g, m_w_up, m_conv_w, m_conv_b, m_w_down, m_final_norm_g, v_norm1_g, v_w_in, v_attn_norm_g, v_hgrn_norm_g, v_hgrn_lb_logits, v_w_out, v_norm2_g, v_w_up, v_conv_w, v_conv_b, v_w_down, v_final_norm_g):
    xs, target = x[0], loss_target[0]
    S, D = xs.shape
    NA = 3 * ATTN_W
    fng = final_norm_g.reshape(1, D)

    t = lambda a: a[0].T
    casts = [_sum_cast([w], BF16, name=f"cast_{nm}") for nm, w in
             (("w_in", t(w_in)), ("w_out", w_out[0]), ("w_up", t(w_up)), ("w_down", w_down[0]))]
    me = _dev_index(_coords())
    (g_in,) = _all_gather(casts[:1], name="ag_w_in")
    later = casts[1:] + [conv_w[0]]
    ag1 = _copies_start(later, [lax.empty((N_DEV, *s.shape), s.dtype) for s in later], _ag_chips_plan,
                        3 * len(later), name="ag_chips_start", after=g_in)
    wi = g_in.reshape(-1, D)
    wi_a = _pair_major(wi[:NA])

    u1, proj_a = _proj_attn(xs, norm1_g + ag1[4][0, 0], wi_a, name="proj_attn")
    proj_h, proj_f = _proj_hgrn(u1, wi, NA, name="proj_hgrn")
    attn, lse = _attn_fwd(proj_a, name="attn_fwd")
    lands = _copies_wait(ag1, _ag_chips_plan, attn, name="ag_chips_wait")
    lands = [lax.dynamic_update_index_in_dim(l, s, me, 0) for l, s in zip(lands, later)]
    ag2 = _copies_start([], lands, _ag_sibling_plan, 4 * len(later), name="ag_sibling_start")
    rec, states = _hgrn_fwd(proj_h, proj_f, hgrn_lb_logits + ag2[4][0, 0], name="hgrn_fwd")
    g_out, g_up, g_down, g_cw = _copies_wait(ag2, _ag_sibling_plan, rec, name="ag_sibling_wait")
    wo = g_out.reshape(-1, D)
    wu = g_up.reshape(-1, D)
    wd = g_down.reshape(-1, D)
    cw = _gathered_cols(g_cw)
    h1, u2, mixed = _out_proj(attn, rec, proj_h, xs, attn_norm_g, hgrn_norm_g, norm2_g, wo, name="out_proj")
    gate, val, act = _up_glu(u2, wu, cw, conv_b, name="up_glu")
    dh2, dh2b, d_fng, loss_part = _down_loss(act, wd, h1, fng, target, name="down_loss")

    dgate, dval, d_cw, d_cb = _dact_glu_bwd(dh2b, wd, gate, val, cw, conv_b, name="dact_glu_bwd")
    dw_down = _mm_tn(act, dh2b, tm=256, name="dw_down")
    dh1, dh1b, d_n2g = _grad_norm_input([dgate, dval], [(wu, 0)], h1, norm2_g, dh2, name="du2_norm2_bwd")
    F = dgate.shape[1]
    dw_up = _mm_tn(dgate, u2, tm=256, rows=2 * F, name="dw_up_gate")
    dw_up = _mm_tn(dval, u2, tm=256, rows=2 * F, row_block=lambda i: i + F // 256, into=dw_up, name="dw_up_val")
    rs_ffn = _rs_start([dw_down.reshape(N_DEV, -1, D), dw_up.reshape(N_DEV, -1, D)], me, name="rs_ffn_start")
    dattn, delta, drec, dhg, d_ang, d_hng = _dmix_post_bwd(dh1b, wo, attn, rec, proj_h, attn_norm_g + rs_ffn[0][4][0, 0],
                                                          hgrn_norm_g, name="dmix_post_bwd")
    dw_out = _mm_tn(mixed, dh1b, name="dw_out")
    rs_out = _rs_start([dw_out.reshape(N_DEV, -1, D)], me, name="rs_out_start")
    dproj_h, d_lbl = _hgrn_bwd(proj_h, proj_f, hgrn_lb_logits + rs_out[0][4][0, 0], states, drec, name="hgrn_bwd")
    small = [("loss", loss_part, None, None, None),
             ("attn_norm_g", d_ang, attn_norm_g, m_attn_norm_g, v_attn_norm_g),
             ("hgrn_norm_g", d_hng, hgrn_norm_g, m_hgrn_norm_g, v_hgrn_norm_g),
             ("hgrn_lb_logits", d_lbl, hgrn_lb_logits, m_hgrn_lb_logits, v_hgrn_lb_logits),
             ("norm2_g", d_n2g, norm2_g, m_norm2_g, v_norm2_g),
             ("conv_b", d_cb, conv_b, m_conv_b, v_conv_b),
             ("final_norm_g", d_fng, final_norm_g, m_final_norm_g, v_final_norm_g)]
    pack = lambda arrs: jnp.concatenate([a.reshape(1, -1) for a in arrs], axis=1)
    small_own = [pack([s[1] for s in small]), d_cw]
    ag_small = _copies_start(small_own, [lax.empty((N_DEV, *s.shape), s.dtype) for s in small_own], _ag_direct_plan,
                             (N_DEV - 1) * len(small_own), name="ag_small_start")
    dproj_a = _attn_bwd(proj_a, dattn, lse, delta, name="attn_bwd")
    pairs = ATTN_W // LANES
    dw_in = _mm_tn(dproj_a, u1, tm=LANES, rows=wi.shape[0], row_block=lambda i: pairs * (i % 3) + i // 3,
                   name="dw_in_attn")
    dw_in = _mm_tn(dproj_h, u1, tm=256, rows=wi.shape[0], row_block=lambda i: i + NA // 256, into=dw_in,
                   name="dw_in_hgrn")
    dw_in = _mm_tn(dhg, u1, tm=256, rows=wi.shape[0], row_block=lambda i: i + (NA + 3 * HGRN_W) // 256, into=dw_in,
                   name="dw_in_gate")
    rs_in = _rs_start([dw_in.reshape(N_DEV, -1, D)], me, name="rs_in_start", after=ag_small[4])
    grad_x, _, d_n1g = _grad_norm_input([dproj_a, dproj_h, dhg], [(wi_a, 0), (wi, NA)], xs,
                                        norm1_g + rs_in[0][4][0, 0], dh1, name="du1_norm1_bwd")

    res = {}

    def update(nm, parts, w, m, v, transposed=False):
        if transposed:
            res[nm] = [r.T[None] for r in _adamw(parts, t(w), t(m), t(v), name=f"adamw_{nm}")]
        else:
            res[nm] = _adamw(parts, w, m, v, name=f"adamw_{nm}")

    g_down, g_up = _rs_finish(rs_ffn, grad_x, name="rs_ffn_wait")
    update("w_down", g_down, w_down, m_w_down, v_w_down)
    update("w_up", g_up, w_up, m_w_up, v_w_up, transposed=True)
    (g_out,) = _rs_finish(rs_out, grad_x, name="rs_out_wait")
    update("w_out", g_out, w_out, m_w_out, v_w_out)
    (g_in,) = _rs_finish(rs_in, res["w_up"][1], name="rs_in_wait")
    update("w_in", g_in, w_in, m_w_in, v_w_in, transposed=True)

    g_small, g_dcw = [lax.dynamic_update_index_in_dim(l, s, me, 0)
                      for l, s in zip(_copies_wait(ag_small, _ag_direct_plan, grad_x, name="ag_small_wait"), small_own)]
    sm = _adamw_packed(g_small, [s[1].size for s in small], [None if s[2] is None else s[2:] for s in small],
                       name="adamw_small")
    for (nm, *_), r in zip(small, sm):
        res[nm] = r
    ncw = conv_w.shape[-1]
    mine_cw = lax.dynamic_slice_in_dim(g_dcw, me * ncw, ncw, axis=2)
    res["conv_w"] = _adamw([mine_cw], conv_w, m_conv_w, v_conv_w, name="adamw_conv_w")
    late, _ = lax.optimization_barrier((d_n1g, res["w_in"][1]))
    update("norm1_g", _all_gather([late], name="ag_norm1_grad"), norm1_g, m_norm1_g, v_norm1_g)

    loss = res["loss"][0][0, 0]
    order = ["norm1_g", "w_in", "attn_norm_g", "hgrn_norm_g", "hgrn_lb_logits", "w_out", "norm2_g", "w_up",
             "conv_w", "conv_b", "w_down", "final_norm_g"]
    return (loss, grad_x[None], *[res[nm][0] for nm in order], *[res[nm][1] for nm in order],
            *[res[nm][2] for nm in order], *[res[nm][3] for nm in order])
```

```python
import jax
import jax.numpy as jnp
from jax import lax
from jax.experimental import pallas as pl
from jax.experimental.pallas import tpu as pltpu

F32, BF16 = jnp.float32, jnp.bfloat16
NORM_EPS = 1e-6
ATTN_HEADS, HEAD_DIM, ATTN_BLOCK = 8, 64, 128
DILATIONS = (1, 4, 16)
ATTN_SCALE = HEAD_DIM ** -0.5
ATTN_W = ATTN_HEADS * HEAD_DIM
HGRN_HEADS, HGRN_DIM, HGRN_CHUNK = 4, 128, 64
HGRN_W = HGRN_HEADS * HGRN_DIM
ADAM_LR, ADAM_B1, ADAM_B2, ADAM_EPS, ADAM_WD, ADAM_STEP = 0.001, 0.9, 0.999, 1e-08, 0.01, 10
LANES, SUBLANES = 128, 8
VMEM_LIMIT_BYTES = 56 * 1024 * 1024
N_DEV = 8
MESH = pl.DeviceIdType.MESH
HBM = pl.BlockSpec(memory_space=pltpu.HBM)
HIGHEST = lax.Precision.HIGHEST


def _cparams(*sem):
    return pltpu.CompilerParams(dimension_semantics=sem, vmem_limit_bytes=VMEM_LIMIT_BYTES)


def _tile(n, pref):
    if n <= pref:
        return n
    t = (pref // LANES) * LANES
    while n % t:
        t -= LANES
    return t


def _resident(shape):
    return pl.BlockSpec(shape, lambda *_: (0,) * len(shape), pipeline_mode=pl.Buffered(1))


def _dot(a, b, dims, precision=None):
    return lax.dot_general(a, b, (dims, ((), ())), precision=precision, preferred_element_type=F32)


def _nn(a, b, precision=None):
    return _dot(a, b, ((1,), (0,)), precision)


def _nt(a, b):
    return _dot(a, b, ((1,), (1,)))


def _tn(a, b):
    return _dot(a, b, ((0,), (0,)))


def _sigmoid(x):
    return 1.0 / (1.0 + jnp.exp(-x))


def _rstd(x):
    return lax.rsqrt(jnp.mean(x * x, axis=-1, keepdims=True) + NORM_EPS)


def _norm_bwd(x, g, du):
    r = _rstd(x)
    xh = x * r
    dxh = du * g
    return r * (dxh - xh * jnp.mean(dxh * xh, axis=-1, keepdims=True)), du * xh


def _row_halves(tm):
    return [pl.ds(0, tm // 2), pl.ds(tm // 2, tm // 2)]


def _accumulate(ref, part, first):
    @pl.when(first)
    def _():
        ref[...] = part

    @pl.when(jnp.logical_not(first))
    def _():
        ref[...] += part


def _mm_tn(x, dy, *, name, tm=512, tn=1024, rows=None, row_block=None, into=None):
    S, M = x.shape
    N = dy.shape[1]
    tm, tn = _tile(M, tm), _tile(N, tn)
    row_block = row_block or (lambda i: i)

    def body(x_ref, dy_ref, *rest):
        o_ref, xt_ref = rest[-2:]

        @pl.when(pl.program_id(1) == 0)
        def _():
            xt_ref[...] = x_ref[...].T

        o_ref[...] = _nn(xt_ref[...], dy_ref[...]).astype(BF16)

    operands = [x, dy] + ([] if into is None else [into])
    return pl.pallas_call(
        body, name=name, grid=(M // tm, N // tn),
        in_specs=[pl.BlockSpec((S, tm), lambda i, j: (0, i)), pl.BlockSpec((S, tn), lambda i, j: (0, j))]
        + ([] if into is None else [pl.BlockSpec(memory_space=pl.ANY)]),
        out_specs=pl.BlockSpec((tm, tn), lambda i, j: (row_block(i), j)),
        out_shape=jax.ShapeDtypeStruct((rows or M, N), BF16),
        input_output_aliases={} if into is None else {2: 0},
        scratch_shapes=[pltpu.VMEM((tm, S), BF16)], compiler_params=_cparams("parallel", "arbitrary"),
    )(*operands)


def _in_proj(x, g, wt_attn, wt, row0, *, name, tm=512):
    S, D = x.shape
    NA, NH, W = wt_attn.shape[0], wt.shape[0] - row0, HGRN_W

    def body(x_ref, g_ref, wa_ref, w_ref, u_ref, a_ref, h_ref, f_ref):
        xv = x_ref[...]
        u = (xv * _rstd(xv) * g_ref[...]).astype(BF16)
        u_ref[...] = u
        a_ref[...] = _nt(u, wa_ref[...]).astype(BF16)
        ph = _nt(u, w_ref[row0:row0 + NH, :])
        h_ref[...] = ph.astype(BF16)
        f_ref[...] = ph[:, W:2 * W]

    row = lambda w: pl.BlockSpec((tm, w), lambda i: (i, 0))
    return pl.pallas_call(
        body, name=name, grid=(S // tm,),
        in_specs=[row(D), pl.BlockSpec((1, D), lambda i: (0, 0)), _resident(wt_attn.shape), _resident(wt.shape)],
        out_specs=[row(D), row(NA), row(NH), row(W)],
        out_shape=[jax.ShapeDtypeStruct((S, D), BF16), jax.ShapeDtypeStruct((S, NA), BF16),
                   jax.ShapeDtypeStruct((S, NH), BF16), jax.ShapeDtypeStruct((S, W), F32)],
        compiler_params=_cparams("parallel"),
    )(x, g, wt_attn, wt)


PAIR_W = 3 * LANES
ATTN_UNROLL_FWD, ATTN_UNROLL_BWD = 4, 4


def _attn_masks(first):
    qi = lax.broadcasted_iota(jnp.int32, (ATTN_BLOCK, 2 * ATTN_BLOCK), 0)
    kj = lax.broadcasted_iota(jnp.int32, (ATTN_BLOCK, 2 * ATTN_BLOCK), 1)
    dist = qi + ATTN_BLOCK - kj
    valid = (dist >= 0) & (dist <= ATTN_BLOCK) & jnp.logical_or(kj >= ATTN_BLOCK, jnp.logical_not(first))
    lane = lax.broadcasted_iota(jnp.int32, (1, LANES), 1)
    return valid, lane


def _for_residue_blocks(S, d, fn):
    span = ATTN_BLOCK * d
    nb = S // span

    def step(n, carry):
        base = pl.multiple_of(n * span, span)
        for r in range(d):
            off = pl.multiple_of((r * nb + n) * ATTN_BLOCK, ATTN_BLOCK)
            fn(lambda ref, r=r: _block_rows(ref, base, r, d),
               lambda ref, val, r=r: _set_block_rows(ref, base, r, d, val), off)
        return carry

    lax.fori_loop(0, nb, step, 0)


def _for_blocks(S, unroll, fn):
    def step(i, carry):
        fn([(pl.multiple_of((i * unroll + u) * ATTN_BLOCK, ATTN_BLOCK), i * unroll + u) for u in range(unroll)])
        return carry

    lax.fori_loop(0, S // ATTN_BLOCK // unroll, step, 0)


def _head_value(x2, lane, e):
    return jnp.sum(jnp.where(lane == HEAD_DIM * e, x2, 0.0), axis=-1, keepdims=True)


def _block_rows(ref, base, r, d):
    if d == 1:
        return ref[pl.ds(base, ATTN_BLOCK), :]
    return ref.at[pl.ds(base, ATTN_BLOCK * d)][pl.ds(r, ATTN_BLOCK, stride=d), :]


def _set_block_rows(ref, base, r, d, val):
    if d == 1:
        ref[pl.ds(base, ATTN_BLOCK), :] = val
    else:
        ref.at[pl.ds(base, ATTN_BLOCK * d)][pl.ds(r, ATTN_BLOCK, stride=d), :] = val


def _order4_to_16(src, dst, pad):
    S = src.shape[0]
    q4, q16 = S // 4, S // 16
    for r in range(4):
        for a in range(4):
            for n in range(q16 // ATTN_BLOCK):
                rows = src.at[pl.ds(r * q4 + 4 * ATTN_BLOCK * n, 4 * ATTN_BLOCK)][pl.ds(a, ATTN_BLOCK, stride=4), :]
                dst[pl.ds(pad + (4 * a + r) * q16 + ATTN_BLOCK * n, ATTN_BLOCK), :] = rows.astype(dst.dtype)


def _order16_to_4(src, pad, dst):
    S = dst.shape[0]
    q4, q16 = S // 4, S // 16
    for r in range(4):
        for a in range(4):
            for n in range(q16 // ATTN_BLOCK):
                rows = src[pl.ds(pad + (4 * a + r) * q16 + ATTN_BLOCK * n, ATTN_BLOCK), :]
                dst.at[pl.ds(r * q4 + 4 * ATTN_BLOCK * n, 4 * ATTN_BLOCK)][pl.ds(a, ATTN_BLOCK, stride=4), :] = rows


def _regroup(S, d, pairs, tmp):
    for src, dst, pad in pairs:
        if d == 16:
            def to_tmp(rows, _, off, src=src):
                tmp[pl.ds(off, ATTN_BLOCK), :] = rows(src)

            _for_residue_blocks(S, 4, to_tmp)
            _order4_to_16(tmp, dst, pad)
    if d != 16:
        def to_dst(rows, _, off):
            for src, dst, pad in pairs:
                dst[pl.ds(pad + off, ATTN_BLOCK), :] = rows(src).astype(dst.dtype)

        _for_residue_blocks(S, d, to_dst)


def _split_pair(p_ref, qs, ks, vs, bk, bv):
    qs[...] = p_ref[:, 0:LANES].astype(F32)
    ks[...] = p_ref[:, LANES:2 * LANES].astype(F32)
    vs[...] = p_ref[:, 2 * LANES:3 * LANES].astype(F32)
    bk[0:ATTN_BLOCK, :] = jnp.zeros((ATTN_BLOCK, LANES), bk.dtype)
    bv[0:ATTN_BLOCK, :] = jnp.zeros((ATTN_BLOCK, LANES), bv.dtype)


def _attn_fwd(proj_a, *, name):
    S = proj_a.shape[0]

    def body(p_ref, o_ref, l_ref, qs, ks, vs, bq, bk, bv, bo, bl, to, tl):
        _split_pair(p_ref, qs, ks, vs, bk, bv)
        for d in DILATIONS:
            nb = S // (ATTN_BLOCK * d)
            _regroup(S, d, ((qs, bq, 0), (ks, bk, ATTN_BLOCK), (vs, bv, ATTN_BLOCK)), to)

            def blocks(group, nb=nb):
                lane = lax.broadcasted_iota(jnp.int32, (1, LANES), 1)
                heads = [(lane >= HEAD_DIM * e) & (lane < HEAD_DIM * (e + 1)) for e in range(LANES // HEAD_DIM)]
                wins = [pl.ds(off, 2 * ATTN_BLOCK) for off, _ in group]
                s = [[_nt(jnp.where(mh, bq[pl.ds(off, ATTN_BLOCK), :], jnp.zeros((ATTN_BLOCK, LANES), BF16)), bk[win, :])
                      for mh in heads] for (off, _), win in zip(group, wins)]
                p, m, l = [], [], []
                for (off, b), su in zip(group, s):
                    valid, _ = _attn_masks(jnp.bitwise_and(b, nb - 1) == 0)
                    sm = [jnp.where(valid, x * ATTN_SCALE, -jnp.inf) for x in su]
                    m.append([jnp.max(x, axis=-1, keepdims=True) for x in sm])
                    p.append([jnp.exp(x - mx) for x, mx in zip(sm, m[-1])])
                    l.append([jnp.sum(x, axis=-1, keepdims=True) for x in p[-1]])
                o = [[_nn(x.astype(BF16), bv[win, :]) for x in pu] for pu, win in zip(p, wins)]
                for (off, _), ou, mu, lu in zip(group, o, m, l):
                    o2 = jnp.zeros((ATTN_BLOCK, LANES), F32)
                    l2 = jnp.zeros((ATTN_BLOCK, LANES), F32)
                    for mh, oe, me_, le in zip(heads, ou, mu, lu):
                        o2 = jnp.where(mh, oe / le, o2)
                        l2 = jnp.where(mh, me_ + jnp.log(le), l2)
                    bo[pl.ds(off, ATTN_BLOCK), :] = o2
                    bl[pl.ds(off, ATTN_BLOCK), :] = l2

            _for_blocks(S, ATTN_UNROLL_FWD, blocks)

            if d == 16:
                _order16_to_4(bo, 0, to)
                _order16_to_4(bl, 0, tl)
            src_o, src_l = (to, tl) if d == 16 else (bo, bl)

            def merge(rows, set_rows, off, d=d, src_o=src_o, src_l=src_l):
                blk = pl.ds(off, ATTN_BLOCK)
                o2, l2 = src_o[blk, :], src_l[blk, :]
                if d != DILATIONS[0]:
                    lo, oo = rows(l_ref), rows(o_ref)
                    ln = jnp.maximum(lo, l2)
                    wa, wb = jnp.exp(lo - ln), jnp.exp(l2 - ln)
                    o2 = (wa * oo + wb * o2) / (wa + wb)
                    l2 = ln + jnp.log(wa + wb)
                set_rows(o_ref, o2)
                set_rows(l_ref, l2)

            _for_residue_blocks(S, min(d, 4), merge)

    slab = pl.BlockSpec((S, LANES), lambda p: (0, p))
    f32_slab, bf16_slab = pltpu.VMEM((S, LANES), F32), pltpu.VMEM((S, LANES), BF16)
    bf16_window = pltpu.VMEM((S + ATTN_BLOCK, LANES), BF16)
    return pl.pallas_call(
        body, name=name, grid=(ATTN_W // LANES,), in_specs=[pl.BlockSpec((S, PAIR_W), lambda p: (0, p))],
        out_specs=[slab, slab],
        out_shape=[jax.ShapeDtypeStruct((S, ATTN_W), F32), jax.ShapeDtypeStruct((S, ATTN_W), F32)],
        scratch_shapes=[f32_slab] * 3 + [bf16_slab, bf16_window, bf16_window] + [f32_slab] * 4,
        compiler_params=_cparams("parallel"),
    )(proj_a)


def _attn_bwd(proj_a, do, lse, delta, *, name):
    S = proj_a.shape[0]

    def body(p_ref, do_ref, lse_ref, dl_ref, o_ref, qs, ks, vs, dqs, dks, dvs, bq, bk, bv, bdo, blse, bdl, bdq, bdk, bdv,
             tmp):
        _split_pair(p_ref, qs, ks, vs, bk, bv)
        bdk[0:ATTN_BLOCK, :] = jnp.zeros((ATTN_BLOCK, LANES), F32)
        bdv[0:ATTN_BLOCK, :] = jnp.zeros((ATTN_BLOCK, LANES), F32)
        for d in DILATIONS:
            nb = S // (ATTN_BLOCK * d)
            _regroup(S, d, ((qs, bq, 0), (ks, bk, ATTN_BLOCK), (vs, bv, ATTN_BLOCK), (do_ref, bdo, 0),
                            (lse_ref, blse, 0), (dl_ref, bdl, 0)), tmp)

            def blocks(group, nb=nb):
                lane = lax.broadcasted_iota(jnp.int32, (1, LANES), 1)
                heads = [(lane >= HEAD_DIM * e) & (lane < HEAD_DIM * (e + 1)) for e in range(LANES // HEAD_DIM)]
                zero = jnp.zeros((ATTN_BLOCK, LANES), BF16)
                chains = [(off, b, e, mh) for off, b in group for e, mh in enumerate(heads)]
                qm = [jnp.where(mh, bq[pl.ds(off, ATTN_BLOCK), :], zero) for off, _, _, mh in chains]
                dom = [jnp.where(mh, bdo[pl.ds(off, ATTN_BLOCK), :], zero) for off, _, _, mh in chains]
                s = [_nt(x, bk[pl.ds(off, 2 * ATTN_BLOCK), :]) for x, (off, _, _, _) in zip(qm, chains)]
                dp = [_nt(x, bv[pl.ds(off, 2 * ATTN_BLOCK), :]) for x, (off, _, _, _) in zip(dom, chains)]
                p, ds = [], []
                for (off, b, e, _), sc, dpc in zip(chains, s, dp):
                    valid, _ = _attn_masks(jnp.bitwise_and(b, nb - 1) == 0)
                    blk = pl.ds(off, ATTN_BLOCK)
                    pc = jnp.where(valid, jnp.exp(sc * ATTN_SCALE - _head_value(blse[blk, :], lane, e)), 0.0)
                    ds.append((pc * (dpc - _head_value(bdl[blk, :], lane, e)) * ATTN_SCALE).astype(BF16))
                    p.append(pc.astype(BF16))
                dq = [_nn(x, bk[pl.ds(off, 2 * ATTN_BLOCK), :]) for x, (off, _, _, _) in zip(ds, chains)]
                dk = [_tn(x, y) for x, y in zip(ds, qm)]
                dv = [_tn(x, y) for x, y in zip(p, dom)]
                nh = len(heads)
                for u, (off, _) in enumerate(group):
                    dq2 = jnp.zeros((ATTN_BLOCK, LANES), F32)
                    for mh, x in zip(heads, dq[nh * u:nh * (u + 1)]):
                        dq2 = jnp.where(mh, x, dq2)
                    bdq[pl.ds(off, ATTN_BLOCK), :] = dq2
                    for acc, grads in ((bdk, dk), (bdv, dv)):
                        win_grad = sum(grads[nh * u + 1:nh * (u + 1)], grads[nh * u])
                        acc[pl.ds(off, ATTN_BLOCK), :] += win_grad[:ATTN_BLOCK]
                        acc[pl.ds(off + ATTN_BLOCK, ATTN_BLOCK), :] = win_grad[ATTN_BLOCK:]

            _for_blocks(S, ATTN_UNROLL_BWD, blocks)

            outs = ((dqs, bdq, 0), (dks, bdk, ATTN_BLOCK), (dvs, bdv, ATTN_BLOCK))
            if d == 16:
                for acc, grad, pad in outs:
                    _order16_to_4(grad, pad, tmp)

                    def add(rows, set_rows, off, acc=acc):
                        set_rows(acc, rows(acc) + tmp[pl.ds(off, ATTN_BLOCK), :])

                    _for_residue_blocks(S, 4, add)
            else:
                def scatter(rows, set_rows, off, d=d):
                    for acc, grad, pad in outs:
                        part = grad[pl.ds(pad + off, ATTN_BLOCK), :]
                        set_rows(acc, part if d == DILATIONS[0] else rows(acc) + part)

                _for_residue_blocks(S, d, scatter)
        o_ref[:, 0:LANES] = dqs[...].astype(BF16)
        o_ref[:, LANES:2 * LANES] = dks[...].astype(BF16)
        o_ref[:, 2 * LANES:3 * LANES] = dvs[...].astype(BF16)

    slab = pl.BlockSpec((S, LANES), lambda p: (0, p))
    pair = pl.BlockSpec((S, PAIR_W), lambda p: (0, p))
    f32_slab, bf16_slab = pltpu.VMEM((S, LANES), F32), pltpu.VMEM((S, LANES), BF16)
    f32_window, bf16_window = pltpu.VMEM((S + ATTN_BLOCK, LANES), F32), pltpu.VMEM((S + ATTN_BLOCK, LANES), BF16)
    return pl.pallas_call(
        body, name=name, grid=(ATTN_W // LANES,), in_specs=[pair, slab, slab, slab], out_specs=pair,
        out_shape=jax.ShapeDtypeStruct(proj_a.shape, BF16),
        scratch_shapes=[f32_slab] * 6 + [bf16_slab, bf16_window, bf16_window, bf16_slab, f32_slab, f32_slab,
                                         f32_slab, f32_window, f32_window, f32_slab],
        compiler_params=_cparams("parallel"),
    )(proj_a, do, lse, delta)


HG_T = 2 * HGRN_CHUNK
HG_GROUPS = 2
HG_STEP = HG_GROUPS * HG_T


def _hgrn_consts():
    row = lax.broadcasted_iota(jnp.int32, (HG_T, HG_T), 0)
    col = lax.broadcasted_iota(jnp.int32, (HG_T, HG_T), 1)
    same = (row >= HGRN_CHUNK) == (col >= HGRN_CHUNK)
    return row, same & (col <= row), same & (col >= row)


def _lower_bound(logits_ref):
    l0, l1 = logits_ref[0:1, :], logits_ref[1:2, :]
    mx = jnp.maximum(l0, l1)
    e0, e1 = jnp.exp(l0 - mx), jnp.exp(l1 - mx)
    return e0 / (e0 + e1)


def _hgrn_chains():
    chains = [(g, h) for g in range(HG_GROUPS) for h in range(HGRN_HEADS)]
    rows = [pl.ds(HG_T * g, HG_T) for g, _ in chains]
    lanes = [slice(HGRN_DIM * h, HGRN_DIM * (h + 1)) for _, h in chains]
    return chains, rows, lanes


def _hgrn_gates(qs, fs, lbs, row, causal):
    C = HGRN_CHUNK
    tri = jnp.where(causal, 1.0, 0.0).astype(F32)
    sgs = [_sigmoid(f) for f in fs]
    forgets = [lb + (1.0 - lb) * sg for lb, sg in zip(lbs, sgs)]
    logfs = [jnp.log(forget) for forget in forgets]
    bs = [_nn(tri, logf, HIGHEST) for logf in logfs]
    out = []
    for q, sg, forget, logf, b in zip(qs, sgs, forgets, logfs, bs):
        key = 1.0 - forget
        bend0 = jnp.sum(logf[:C], axis=0, keepdims=True)
        bend1 = jnp.sum(logf[C:], axis=0, keepdims=True)
        bend = jnp.where(row < C, bend0, bend1)
        eb, emb, eend = jnp.exp(b), jnp.exp(-b), jnp.exp(bend - b)
        sq = _sigmoid(q)
        out.append(dict(sg=sg, forget=forget, key=key, bend0=bend0, bend1=bend1, eb=eb, emb=emb, eend=eend, sq=sq,
                        qd=q * sq * eb, ki=key * emb, ke=key * eend))
    return out


def _hgrn_fwd(proj, proj_f, logits, *, name):
    S = proj.shape[0]
    W, C = HGRN_W, HGRN_CHUNK

    def body(q_ref, f_ref, i_ref, lg_ref, rec_ref, st_ref, s_ref):
        @pl.when(pl.program_id(0) == 0)
        def _():
            s_ref[...] = jnp.zeros_like(s_ref)

        row, causal, _ = _hgrn_consts()
        lb_all = _lower_bound(lg_ref)
        chains, rows, lanes = _hgrn_chains()
        n = range(len(chains))
        gts = _hgrn_gates([q_ref[rows[c], lanes[c]].astype(F32) for c in n], [f_ref[rows[c], lanes[c]] for c in n],
                          [lb_all[:, lanes[c]] for c in n], row, causal)
        qd, ki, ke = ([gt[k].astype(BF16) for gt in gts] for k in ("qd", "ki", "ke"))
        iv = [i_ref[rows[c], lanes[c]].astype(BF16) for c in n]
        a = [_nt(qd[c], ki[c]) for c in n]
        u0 = [_tn(iv[c][:C], ke[c][:C]) for c in n]
        u1 = [_tn(iv[c][C:], ke[c][C:]) for c in n]
        state = [s_ref[h] for h in range(HGRN_HEADS)]
        s0, s1 = [], []
        for c, (g, h) in enumerate(chains):
            s0.append(state[h])
            s1.append(jnp.exp(gts[c]["bend0"]) * s0[c] + u0[c])
            state[h] = jnp.exp(gts[c]["bend1"]) * s1[c] + u1[c]
        o0 = [_nt(qd[c][:C], s0[c].astype(BF16)) for c in n]
        o1 = [_nt(qd[c][C:], s1[c].astype(BF16)) for c in n]
        o = [_nn(jnp.where(causal, a[c], 0.0).astype(BF16), iv[c]) for c in n]
        for c, (g, h) in enumerate(chains):
            st_ref[2 * g, h] = s0[c]
            st_ref[2 * g + 1, h] = s1[c]
            rec_ref[rows[c], lanes[c]] = o[c] + jnp.concatenate([o0[c], o1[c]], axis=0)
        for h in range(HGRN_HEADS):
            s_ref[h] = state[h]

    blk = lambda j: pl.BlockSpec((HG_STEP, W), lambda t: (t, j))
    return pl.pallas_call(
        body, name=name, grid=(S // HG_STEP,),
        in_specs=[blk(0), blk(0), blk(2), pl.BlockSpec((2, W), lambda t: (0, 0))],
        out_specs=[blk(0), pl.BlockSpec((2 * HG_GROUPS, HGRN_HEADS, HGRN_DIM, HGRN_DIM), lambda t: (t, 0, 0, 0))],
        out_shape=[jax.ShapeDtypeStruct((S, W), F32),
                   jax.ShapeDtypeStruct((S // C, HGRN_HEADS, HGRN_DIM, HGRN_DIM), F32)],
        scratch_shapes=[pltpu.VMEM((HGRN_HEADS, HGRN_DIM, HGRN_DIM), F32)],
        compiler_params=_cparams("arbitrary"),
    )(proj, proj_f, proj, logits)


def _hgrn_bwd(proj, proj_f, logits, states, drec, *, name):
    S = proj.shape[0]
    W, C = HGRN_W, HGRN_CHUNK
    nt = S // HG_STEP

    def body(q_ref, f_ref, i_ref, lg_ref, st_ref, do_ref, dp_ref, dlg_ref, ds_ref, dlb_ref):
        t = pl.program_id(0)

        @pl.when(t == 0)
        def _():
            ds_ref[...] = jnp.zeros_like(ds_ref)
            dlb_ref[...] = jnp.zeros_like(dlb_ref)

        row, causal, anti = _hgrn_consts()
        lb_all = _lower_bound(lg_ref)
        chains, rows, lanes = _hgrn_chains()
        n = range(len(chains))
        qs, lbs = [q_ref[rows[c], lanes[c]].astype(F32) for c in n], [lb_all[:, lanes[c]] for c in n]
        gts = _hgrn_gates(qs, [f_ref[rows[c], lanes[c]] for c in n], lbs, row, causal)
        qd, ki, ke = ([gt[k] for gt in gts] for k in ("qd", "ki", "ke"))
        qdb, kib, keb = ([x.astype(BF16) for x in xs] for xs in (qd, ki, ke))
        iv = [i_ref[rows[c], lanes[c]].astype(BF16) for c in n]
        dob = [do_ref[rows[c], lanes[c]].astype(BF16) for c in n]
        s0 = [st_ref[2 * g, h] for g, h in chains]
        s1 = [st_ref[2 * g + 1, h] for g, h in chains]
        dec0, dec1 = [jnp.exp(gt["bend0"]) for gt in gts], [jnp.exp(gt["bend1"]) for gt in gts]
        a = [_nt(qdb[c], kib[c]) for c in n]
        da = [_nt(dob[c], iv[c]) for c in n]
        dqd1 = [_nn(dob[c][C:], s1[c].astype(BF16)) for c in n]
        dqd0 = [_nn(dob[c][:C], s0[c].astype(BF16)) for c in n]
        t1 = [_tn(dob[c][C:], qdb[c][C:]) for c in n]
        t0 = [_tn(dob[c][:C], qdb[c][:C]) for c in n]
        carry = [ds_ref[h] for h in range(HGRN_HEADS)]
        ds1, ds0 = [None] * len(chains), [None] * len(chains)
        for c in reversed(n):
            h = chains[c][1]
            ds1[c] = carry[h]
            ds0[c] = dec1[c] * ds1[c] + t1[c]
            carry[h] = dec0[c] * ds0[c] + t0[c]
        for h in range(HGRN_HEADS):
            ds_ref[h] = carry[h]
        ds1b, ds0b = [x.astype(BF16) for x in ds1], [x.astype(BF16) for x in ds0]
        a = [jnp.where(causal, x, 0.0).astype(BF16) for x in a]
        da = [jnp.where(causal, x, 0.0).astype(BF16) for x in da]
        di1 = [_nt(keb[c][C:], ds1b[c]) for c in n]
        dke1 = [_nn(iv[c][C:], ds1b[c]) for c in n]
        di0 = [_nt(keb[c][:C], ds0b[c]) for c in n]
        dke0 = [_nn(iv[c][:C], ds0b[c]) for c in n]
        dqd_a = [_nn(da[c], kib[c]) for c in n]
        dki = [_tn(da[c], qdb[c]) for c in n]
        di_a = [_tn(a[c], dob[c]) for c in n]
        dqd, dke, db = [], [], []
        for c in n:
            ddec1 = jnp.sum(ds1[c] * s1[c], axis=0, keepdims=True)
            ddec0 = jnp.sum(ds0[c] * s0[c], axis=0, keepdims=True)
            dqd.append(dqd_a[c] + jnp.concatenate([dqd0[c], dqd1[c]], axis=0))
            h = chains[c][1]
            dp_ref[rows[c], 2 * W + HGRN_DIM * h:2 * W + HGRN_DIM * (h + 1)] = (
                di_a[c] + jnp.concatenate([di0[c], di1[c]], axis=0)).astype(BF16)
            dke.append(jnp.concatenate([dke0[c], dke1[c]], axis=0))
            gke = dke[c] * ke[c]
            dbend0 = jnp.sum(gke[:C], axis=0, keepdims=True) + ddec0 * dec0[c]
            dbend1 = jnp.sum(gke[C:], axis=0, keepdims=True) + ddec1 * dec1[c]
            dbc = dqd[c] * qd[c] - dki[c] * ki[c] - gke
            db.append(dbc + jnp.where(row == C - 1, dbend0, 0.0) + jnp.where(row == HG_T - 1, dbend1, 0.0))
        tri = jnp.where(anti, 1.0, 0.0).astype(F32)
        dlogf = [_nn(tri, db[c], HIGHEST) for c in n]
        for c in n:
            gt, lb, q, h = gts[c], lbs[c], qs[c], chains[c][1]
            dforget = dlogf[c] / gt["forget"] - (dki[c] * gt["emb"] + dke[c] * gt["eend"])
            sg, sq = gt["sg"], gt["sq"]
            dp_ref[rows[c], W + HGRN_DIM * h:W + HGRN_DIM * (h + 1)] = (
                dforget * (1.0 - lb) * sg * (1.0 - sg)).astype(BF16)
            dlb_ref[:, lanes[c]] += jnp.sum(dforget * (1.0 - sg), axis=0, keepdims=True)
            dp_ref[rows[c], lanes[c]] = (dqd[c] * gt["eb"] * sq * (1.0 + q * (1.0 - sq))).astype(BF16)

        @pl.when(t == nt - 1)
        def _():
            dl0 = dlb_ref[...] * lb_all * (1.0 - lb_all)
            dlg_ref[0:1, :] = dl0
            dlg_ref[1:2, :] = -dl0

    blk = lambda j: pl.BlockSpec((HG_STEP, W), lambda t: (nt - 1 - t, j))
    full = pl.BlockSpec((2, W), lambda t: (0, 0))
    return pl.pallas_call(
        body, name=name, grid=(nt,),
        in_specs=[blk(0), blk(0), blk(2), full,
                  pl.BlockSpec((2 * HG_GROUPS, HGRN_HEADS, HGRN_DIM, HGRN_DIM), lambda t: (nt - 1 - t, 0, 0, 0)), blk(0)],
        out_specs=[pl.BlockSpec((HG_STEP, 3 * W), lambda t: (nt - 1 - t, 0)), full],
        out_shape=[jax.ShapeDtypeStruct((S, 3 * W), BF16), jax.ShapeDtypeStruct((2, W), F32)],
        scratch_shapes=[pltpu.VMEM((HGRN_HEADS, HGRN_DIM, HGRN_DIM), F32), pltpu.VMEM((1, W), F32)],
        compiler_params=_cparams("arbitrary"),
    )(proj, proj_f, proj, logits, states, drec)


def _out_proj(attn, rec, proj_h, x, g_attn, g_hgrn, g_norm2, w_out, *, name, tm=512):
    S, D = x.shape
    AW, W = ATTN_W, HGRN_W

    def body(a_ref, r_ref, hg_ref, x_ref, ga_ref, gh_ref, g2_ref, w_ref, h_ref, u_ref, m_ref):
        av = a_ref[...]
        m_ref[:, :AW] = (av * _rstd(av) * ga_ref[...]).astype(BF16)
        for h in range(HGRN_HEADS):
            sl = slice(HGRN_DIM * h, HGRN_DIM * (h + 1))
            rv, hg = r_ref[:, sl], hg_ref[:, sl].astype(F32)
            m_ref[:, AW + HGRN_DIM * h:AW + HGRN_DIM * (h + 1)] = (
                (rv * _rstd(rv) * gh_ref[:, sl]) * (hg * _sigmoid(hg))).astype(BF16)
        h1 = x_ref[...] + _nn(m_ref[...], w_ref[...])
        h_ref[...] = h1
        u_ref[...] = (h1 * _rstd(h1) * g2_ref[...]).astype(BF16)

    row = lambda w, j=0: pl.BlockSpec((tm, w), lambda i: (i, j))
    vec = lambda w: pl.BlockSpec((1, w), lambda i: (0, 0))
    return pl.pallas_call(
        body, name=name, grid=(S // tm,),
        in_specs=[row(AW), row(W), row(W, 3), row(D), vec(AW), vec(W), vec(D), _resident(w_out.shape)],
        out_specs=[row(D), row(D), row(AW + W)],
        out_shape=[jax.ShapeDtypeStruct((S, D), F32), jax.ShapeDtypeStruct((S, D), BF16),
                   jax.ShapeDtypeStruct((S, AW + W), BF16)],
        compiler_params=_cparams("parallel"),
    )(attn, rec, proj_h, x, g_attn, g_hgrn, g_norm2, w_out)


def _dmix_post_bwd(dh1b, w_out, attn, rec, proj_h, g_attn, g_hgrn, *, name, tm=512):
    S, D = dh1b.shape
    AW, W = ATTN_W, HGRN_W

    def body(dh_ref, w_ref, a_ref, r_ref, hg_ref, ga_ref, gh_ref, do_ref, dl_ref, dr_ref, dhg_ref, dga_ref, dgh_ref):
        first = pl.program_id(0) == 0
        dmix = _nt(dh_ref[...], w_ref[...])
        av = a_ref[...]
        dov, dga = _norm_bwd(av, ga_ref[...], dmix[:, :AW])
        do_ref[...] = dov
        shift = HEAD_DIM.bit_length() - 1
        hi = lax.shift_right_logical(lax.broadcasted_iota(jnp.int32, (AW, AW), 0), shift)
        hj = lax.shift_right_logical(lax.broadcasted_iota(jnp.int32, (AW, AW), 1), shift)
        prod = dov * av
        hi_part = prod.astype(BF16)
        lo_part = (prod - hi_part.astype(F32)).astype(BF16)
        same_head = jnp.where(hi == hj, 1.0, 0.0).astype(BF16)
        dl_ref[...] = _nn(hi_part, same_head) + _nn(lo_part, same_head)
        _accumulate(dga_ref, jnp.sum(dga, axis=0, keepdims=True), first)

        @pl.when(first)
        def _():
            dgh_ref[...] = jnp.zeros_like(dgh_ref)

        for h in range(HGRN_HEADS):
            sl = slice(HGRN_DIM * h, HGRN_DIM * (h + 1))
            rv, hg, gv = r_ref[:, sl], hg_ref[:, sl].astype(F32), gh_ref[:, sl]
            dout = dmix[:, AW + HGRN_DIM * h:AW + HGRN_DIM * (h + 1)]
            sg = _sigmoid(hg)
            drv, dgh = _norm_bwd(rv, gv, dout * (hg * sg))
            dr_ref[:, sl] = drv
            dgh_ref[:, sl] += jnp.sum(dgh, axis=0, keepdims=True)
            dhg_ref[:, sl] = (dout * (rv * _rstd(rv) * gv) * (sg * (1.0 + hg * (1.0 - sg)))).astype(BF16)

    row = lambda w, j=0: pl.BlockSpec((tm, w), lambda i: (i, j))
    vec = lambda w: pl.BlockSpec((1, w), lambda i: (0, 0))
    return pl.pallas_call(
        body, name=name, grid=(S // tm,),
        in_specs=[row(D), _resident(w_out.shape), row(AW), row(W), row(W, 3), vec(AW), vec(W)],
        out_specs=[row(AW), row(AW), row(W), row(W), vec(AW), vec(W)],
        out_shape=[jax.ShapeDtypeStruct((S, AW), F32), jax.ShapeDtypeStruct((S, AW), F32),
                   jax.ShapeDtypeStruct((S, W), F32), jax.ShapeDtypeStruct((S, W), BF16),
                   jax.ShapeDtypeStruct((1, AW), F32), jax.ShapeDtypeStruct((1, W), F32)],
        compiler_params=_cparams("arbitrary"),
    )(dh1b, w_out, attn, rec, proj_h, g_attn, g_hgrn)


def _conv_act(g, g1, g2, w_ref, b_ref):
    c = b_ref[...] + w_ref[0:1, :] * g2 + w_ref[1:2, :] * g1 + w_ref[2:3, :] * g
    return c, 0.5 * (1.0 + lax.erf(c * (2.0 ** -0.5)))


def _shift_down(g, halo, row):
    g1 = jnp.where(row == 0, halo[7:8], pltpu.roll(g, 1, 0))
    g2 = jnp.where(row == 0, halo[6:7], jnp.where(row == 1, halo[7:8], pltpu.roll(g, 2, 0)))
    return g1, g2


def _shift_up(x, halo, row):
    n = x.shape[0]
    x1 = jnp.where(row == n - 1, halo[0:1], pltpu.roll(x, n - 1, 0))
    x2 = jnp.where(row == n - 2, halo[0:1], jnp.where(row == n - 1, halo[1:2], pltpu.roll(x, n - 2, 0)))
    return x1, x2


def _up_glu(u, wt_up, conv_w, conv_b, *, name, tm=1024, tn=256):
    S, D = u.shape
    F = wt_up.shape[0] // 2
    nf = F // tn

    def body(u_ref, wg_ref, wv_ref, cw_ref, cb_ref, g_ref, v_ref, a_ref, halo_ref):
        i, j = pl.program_id(0), pl.program_id(1)

        @pl.when(i == 0)
        def _():
            halo_ref[j] = jnp.zeros((SUBLANES, tn), F32)

        uv = u_ref[...]
        g, v = _nt(uv, wg_ref[...]), _nt(uv, wv_ref[...])
        row = lax.broadcasted_iota(jnp.int32, (tm, tn), 0)
        g1, g2 = _shift_down(g, halo_ref[j], row)
        c, cdf = _conv_act(g, g1, g2, cw_ref, cb_ref)
        a_ref[...] = (c * cdf * v).astype(BF16)
        g_ref[...] = g.astype(BF16)
        v_ref[...] = v.astype(BF16)
        halo_ref[j] = g[tm - SUBLANES:, :]

    col = pl.BlockSpec((tm, tn), lambda i, j: (i, j))
    out = jax.ShapeDtypeStruct((S, F), BF16)
    return pl.pallas_call(
        body, name=name, grid=(S // tm, nf),
        in_specs=[pl.BlockSpec((tm, D), lambda i, j: (i, 0)), pl.BlockSpec((tn, D), lambda i, j: (j, 0)),
                  pl.BlockSpec((tn, D), lambda i, j: (j + nf, 0)), pl.BlockSpec((3, tn), lambda i, j: (0, j)),
                  pl.BlockSpec((1, tn), lambda i, j: (0, j))],
        out_specs=[col, col, col], out_shape=[out, out, out],
        scratch_shapes=[pltpu.VMEM((nf, SUBLANES, tn), F32)], compiler_params=_cparams("arbitrary", "arbitrary"),
    )(u, wt_up, wt_up, conv_w, conv_b)


def _dact_glu_bwd(dh2b, w_down, gate, val, conv_w, conv_b, *, name, tm=1024, tn=256):
    S, D = dh2b.shape
    F = gate.shape[1]
    nf, ni = F // tn, S // tm
    hb = tm // SUBLANES

    def body(dh_ref, wd_ref, g_ref, gh_ref, v_ref, cw_ref, cb_ref, dg_ref, dv_ref, dcw_ref, dcb_ref, halo_ref, acc_ref):
        i, j = pl.program_id(0), pl.program_id(1)

        @pl.when(i == 0)
        def _():
            halo_ref[j] = jnp.zeros((SUBLANES, tn), F32)
            acc_ref[j] = jnp.zeros((SUBLANES, tn), F32)

        g = g_ref[...].astype(F32)
        before = jnp.where(i < ni - 1, gh_ref[...].astype(F32), 0.0)
        row = lax.broadcasted_iota(jnp.int32, (tm, tn), 0)
        g1, g2 = _shift_down(g, before[SUBLANES:], row)
        c, cdf = _conv_act(g, g1, g2, cw_ref, cb_ref)
        da = _nt(dh_ref[...], wd_ref[...])
        dv_ref[...] = (da * (c * cdf)).astype(BF16)
        pdf = jnp.exp(-0.5 * c * c) * (1.0 / (2.0 * jnp.pi) ** 0.5)
        dc = da * v_ref[...].astype(F32) * (cdf + c * pdf)
        d1, d2 = _shift_up(dc, halo_ref[j], row)
        dg_ref[...] = (cw_ref[2:3, :] * dc + cw_ref[1:2, :] * d1 + cw_ref[0:1, :] * d2).astype(BF16)
        halo_ref[j] = dc[:SUBLANES, :]
        for k, t in enumerate((dc * g2, dc * g1, dc * g, dc)):
            acc_ref[j, k:k + 1, :] += jnp.sum(t, axis=0, keepdims=True)

        @pl.when((i == ni - 1) & (j == nf - 1))
        def _():
            for jj in range(nf):
                dcw_ref[:, jj * tn:(jj + 1) * tn] = acc_ref[jj, 0:3, :]
                dcb_ref[:, jj * tn:(jj + 1) * tn] = acc_ref[jj, 3:4, :]

    tile = pl.BlockSpec((tm, tn), lambda i, j: (ni - 1 - i, j))
    return pl.pallas_call(
        body, name=name, grid=(ni, nf),
        in_specs=[pl.BlockSpec((tm, D), lambda i, j: (ni - 1 - i, 0)), pl.BlockSpec((tn, D), lambda i, j: (j, 0)),
                  tile, pl.BlockSpec((SUBLANES * 2, tn), lambda i, j: (jnp.maximum((ni - 1 - i) * (hb // 2) - 1, 0), j)),
                  tile, pl.BlockSpec((3, tn), lambda i, j: (0, j)), pl.BlockSpec((1, tn), lambda i, j: (0, j))],
        out_specs=[tile, tile, pl.BlockSpec((3, F), lambda i, j: (0, 0)), pl.BlockSpec((1, F), lambda i, j: (0, 0))],
        out_shape=[jax.ShapeDtypeStruct((S, F), BF16), jax.ShapeDtypeStruct((S, F), BF16),
                   jax.ShapeDtypeStruct((3, F), F32), jax.ShapeDtypeStruct((1, F), F32)],
        scratch_shapes=[pltpu.VMEM((nf, SUBLANES, tn), F32), pltpu.VMEM((nf, SUBLANES, tn), F32)],
        compiler_params=_cparams("arbitrary", "arbitrary"),
    )(dh2b, w_down, gate, gate, val, conv_w, conv_b)


def _down_loss(act, w_down, h1, g, target, *, name, tm=512):
    S, F = act.shape
    D = h1.shape[1]

    def body(a_ref, w_ref, h_ref, g_ref, t_ref, dh_ref, dhb_ref, dg_ref, loss_ref):
        first = pl.program_id(0) == 0
        h2 = h_ref[...] + _nn(a_ref[...], w_ref[...])
        gv = g_ref[...]
        r = _rstd(h2)
        xh = h2 * r
        err = xh * gv - t_ref[...]
        part_loss = 0.5 * jnp.sum(jnp.mean(err * err, axis=-1, keepdims=True), axis=0, keepdims=True)
        dy = err * (1.0 / D)
        dxh = dy * gv
        dh = r * (dxh - xh * jnp.mean(dxh * xh, axis=-1, keepdims=True))
        dh_ref[...] = dh
        dhb_ref[...] = dh.astype(BF16)
        _accumulate(dg_ref, jnp.sum(dy * xh, axis=0, keepdims=True), first)
        _accumulate(loss_ref, jnp.broadcast_to(part_loss, (1, LANES)), first)

    row = lambda w: pl.BlockSpec((tm, w), lambda i: (i, 0))
    vec = lambda w: pl.BlockSpec((1, w), lambda i: (0, 0))
    return pl.pallas_call(
        body, name=name, grid=(S // tm,), in_specs=[row(F), _resident(w_down.shape), row(D), vec(D), row(D)],
        out_specs=[row(D), row(D), vec(D), vec(LANES)],
        out_shape=[jax.ShapeDtypeStruct((S, D), F32), jax.ShapeDtypeStruct((S, D), BF16),
                   jax.ShapeDtypeStruct((1, D), F32), jax.ShapeDtypeStruct((1, LANES), F32)],
        compiler_params=_cparams("arbitrary"),
    )(act, w_down, h1, g, target)


def _grad_norm_input(pieces, ws, x, g, add, *, name, tm=512):
    S, D = x.shape
    widths = [p.shape[1] for p in pieces]
    n, nw = len(pieces), len(ws)
    where, wi, off = [], 0, ws[0][1]
    for wd in widths:
        if off == ws[wi][0].shape[0]:
            wi, off = wi + 1, ws[wi + 1][1]
        where.append((wi, off))
        off += wd
    ws = [w for w, _ in ws]

    def body(*refs):
        p_refs, w_refs = refs[:n], refs[n:n + nw]
        x_ref, g_ref, add_ref, dx_ref, dxb_ref, dg_ref = refs[n + nw:]
        halves = _row_halves(tm)
        du = []
        for rows in halves:
            terms = [_nn(p_refs[k][rows, :], w_refs[wi][off:off + widths[k], :]) for k, (wi, off) in enumerate(where)]
            du.append(sum(terms[1:], terms[0]))
        dg_sum = None
        for rows, duh in zip(halves, du):
            dx, dg = _norm_bwd(x_ref[rows, :], g_ref[...], duh)
            dx = add_ref[rows, :] + dx
            dx_ref[rows, :] = dx
            dxb_ref[rows, :] = dx.astype(BF16)
            part = jnp.sum(dg, axis=0, keepdims=True)
            dg_sum = part if dg_sum is None else dg_sum + part
        _accumulate(dg_ref, dg_sum, pl.program_id(0) == 0)

    row = lambda w_: pl.BlockSpec((tm, w_), lambda i: (i, 0))
    vec = pl.BlockSpec((1, D), lambda i: (0, 0))
    return pl.pallas_call(
        body, name=name, grid=(S // tm,),
        in_specs=[row(wd) for wd in widths] + [_resident(w.shape) for w in ws] + [row(D), vec, row(D)],
        out_specs=[row(D), row(D), vec],
        out_shape=[jax.ShapeDtypeStruct((S, D), F32), jax.ShapeDtypeStruct((S, D), BF16),
                   jax.ShapeDtypeStruct((1, D), F32)],
        compiler_params=_cparams("arbitrary"),
    )(*pieces, *ws, x, g, add)


def _rows(a):
    return a.reshape(-1, a.shape[-1])


def _row_tile(rows, cols, itemsize=4, budget=1 << 20):
    t = rows
    while t % 32 == 0 and t * cols * itemsize > budget:
        t //= 2
    return t


def _sum_cast(arrs, out_dtype, *, name):
    shape = arrs[0].shape
    flat = [_rows(a) for a in arrs]
    R, C = flat[0].shape
    tr = _row_tile(R, C)

    def body(*refs):
        acc = refs[0][...].astype(F32)
        for r in refs[1:-1]:
            acc = acc + r[...].astype(F32)
        refs[-1][...] = acc.astype(out_dtype)

    spec = pl.BlockSpec((tr, C), lambda i: (i, 0))
    return pl.pallas_call(
        body, name=name, grid=(R // tr,), in_specs=[spec] * len(flat), out_specs=spec,
        out_shape=jax.ShapeDtypeStruct((R, C), out_dtype), compiler_params=_cparams("parallel"),
    )(*flat).reshape(shape)


def _adamw(parts, w, m, v, *, name):
    shape = w.shape
    w2, m2, v2 = _rows(w), _rows(m), _rows(v)
    R, C = w2.shape
    parts = [p.reshape(-1, R, C) for p in parts]
    tr = _row_tile(R, C)
    np_ = len(parts)
    c1, c2 = 1.0 - ADAM_B1 ** ADAM_STEP, 1.0 - ADAM_B2 ** ADAM_STEP

    def body(*refs):
        terms = [(r, k) for r in refs[:np_] for k in range(r.shape[0])]
        g = terms[0][0][terms[0][1]].astype(F32)
        for r, k in terms[1:]:
            g = g + r[k].astype(F32)
        w_ref, m_ref, v_ref, g_out, d_out, m_out, v_out = refs[np_:]
        mn = ADAM_B1 * m_ref[...] + (1.0 - ADAM_B1) * g
        vn = ADAM_B2 * v_ref[...] + (1.0 - ADAM_B2) * (g * g)
        g_out[...] = g
        d_out[...] = -ADAM_LR * ((mn / c1) / (jnp.sqrt(vn / c2) + ADAM_EPS) + ADAM_WD * w_ref[...])
        m_out[...] = mn
        v_out[...] = vn

    spec = pl.BlockSpec((tr, C), lambda i: (i, 0))
    out = jax.ShapeDtypeStruct((R, C), F32)
    stacks = [pl.BlockSpec((p.shape[0], tr, C), lambda i: (0, i, 0)) for p in parts]
    res = pl.pallas_call(
        body, name=name, grid=(R // tr,), in_specs=stacks + [spec] * 3, out_specs=[spec] * 4,
        out_shape=[out] * 4, compiler_params=_cparams("parallel"),
    )(*parts, w2, m2, v2)
    return [r.reshape(shape) for r in res]


def _adamw_packed(stack, widths, params, *, name):
    c1, c2 = 1.0 - ADAM_B1 ** ADAM_STEP, 1.0 - ADAM_B2 ** ADAM_STEP
    k = stack.shape[0]
    flat = [None if p is None else [_rows(a) for a in p] for p in params]
    n_in = sum(3 for p in flat if p is not None)

    def body(*refs):
        s_ref, ins, outs = refs[0], list(refs[1:1 + n_in]), list(refs[1 + n_in:])
        off = 0
        for width, p in zip(widths, flat):
            rows = 1 if p is None else p[0].shape[0]
            cols = width // rows
            w_ref, m_ref, v_ref = (None, None, None) if p is None else (ins.pop(0), ins.pop(0), ins.pop(0))
            o_refs = [outs.pop(0) for _ in range(1 if p is None else 4)]
            for r in range(rows):
                seg = slice(off + r * cols, off + (r + 1) * cols)
                g = s_ref[0, :, seg]
                for j in range(1, k):
                    g = g + s_ref[j, :, seg]
                o_refs[0][r:r + 1, :] = g
                if p is not None:
                    row = slice(r, r + 1)
                    mn = ADAM_B1 * m_ref[row, :] + (1.0 - ADAM_B1) * g
                    vn = ADAM_B2 * v_ref[row, :] + (1.0 - ADAM_B2) * (g * g)
                    o_refs[1][row, :] = -ADAM_LR * ((mn / c1) / (jnp.sqrt(vn / c2) + ADAM_EPS) + ADAM_WD * w_ref[row, :])
                    o_refs[2][row, :] = mn
                    o_refs[3][row, :] = vn
            off += width

    operands, out_shape = [stack], []
    for width, p in zip(widths, flat):
        if p is None:
            out_shape.append(jax.ShapeDtypeStruct((1, width), F32))
        else:
            operands += p
            out_shape += [jax.ShapeDtypeStruct(p[0].shape, F32)] * 4
    res = list(pl.pallas_call(body, name=name, out_shape=out_shape)(*operands))
    out = []
    for p, orig in zip(flat, params):
        n = 1 if p is None else 4
        out.append([r if orig is None else r.reshape(orig[0].shape) for r in res[:n]])
        res = res[n:]
    return out


def _coords():
    return lax.axis_index("x"), lax.axis_index("y"), lax.axis_index("c")


def _all_gather(shards, *, name):
    n = len(shards)

    def body(*refs):
        x_refs, out_refs = refs[:n], refs[n:2 * n]
        send_sems, recv_sems, local_sems = refs[2 * n:]
        x, y, c = _coords()
        me, sibling = (x, y, c), (x, y, 1 - c)
        chips = [(1 - x, y), (x, 1 - y), (1 - x, 1 - y)]

        def slot(a, dev):
            return out_refs[a].at[4 * dev[0] + 2 * dev[1] + dev[2]]

        def copy(a, k, block, to, src=None):
            return pltpu.make_async_remote_copy(
                src_ref=slot(a, block) if src is None else src, dst_ref=slot(a, block),
                send_sem=send_sems.at[7 * a + k], recv_sem=recv_sems.at[7 * a + k], device_id=to, device_id_type=MESH)

        mine = [pltpu.make_async_copy(x_refs[a], slot(a, me), local_sems.at[a]) for a in range(n)]
        for cp in mine:
            cp.start()
        first = []
        for a in range(n):
            first.append(copy(a, 0, me, sibling, src=x_refs[a]))
            first += [copy(a, 1 + j, me, (*chip, c), src=x_refs[a]) for j, chip in enumerate(chips)]
        for cp in first:
            cp.start()
        passed = []
        for j, chip in enumerate(chips):
            for a in range(n):
                copy(a, 1 + j, (*chip, c), me).wait_recv()
                fwd = copy(a, 4 + j, (*chip, c), sibling)
                fwd.start()
                passed.append(fwd)
        for a in range(n):
            copy(a, 0, sibling, me).wait_recv()
            for j, chip in enumerate(chips):
                copy(a, 4 + j, (*chip, 1 - c), me).wait_recv()
        for cp in first + passed:
            cp.wait_send()
        for cp in mine:
            cp.wait()

    return pl.pallas_call(
        body, name=name, in_specs=[HBM] * n, out_specs=[HBM] * n,
        out_shape=[jax.ShapeDtypeStruct((N_DEV, *s.shape), s.dtype) for s in shards],
        scratch_shapes=[pltpu.SemaphoreType.DMA((7 * n,)), pltpu.SemaphoreType.DMA((7 * n,)),
                        pltpu.SemaphoreType.DMA((n,))],
    )(*shards)


def _flip_y(x, y, c):
    return (x, 1 - y, c)


def _flip_x(x, y, c):
    return (1 - x, y, c)


def _flip_xy(x, y, c):
    return (1 - x, 1 - y, c)


SEM = pl.BlockSpec(memory_space=pltpu.SEMAPHORE)
SIDE_EFFECT = pltpu.SideEffectType.DATAFLOW_SIDE_EFFECTING


def _in_hbm(a):
    return pltpu.with_memory_space_constraint(a, pltpu.HBM)


def _copies_start(srcs, lands, plan, n_copies, *, name, after=None):
    ns, nl = len(srcs), len(lands)
    extra = [] if after is None else [after]

    def body(*refs):
        src_refs, land_refs = refs[:ns], refs[ns:ns + nl]
        send_sems, recv_sems = refs[ns + nl + len(extra):ns + nl + len(extra) + 2]
        token = refs[-1]
        for k, (src, dst, peer, _) in enumerate(plan(src_refs, land_refs, *_coords())):
            pltpu.make_async_remote_copy(src_ref=src, dst_ref=dst, send_sem=send_sems.at[k], recv_sem=recv_sems.at[k],
                                         device_id=peer, device_id_type=MESH).start()
        token[...] = jnp.zeros_like(token)

    bufs = [*srcs, *lands]
    res = pl.pallas_call(
        body, name=name, in_specs=[HBM] * (ns + nl) + [pl.BlockSpec(memory_space=pl.ANY)] * len(extra),
        out_specs=(SEM, SEM, *[HBM] * (ns + nl), pl.BlockSpec(memory_space=pltpu.VMEM)),
        out_shape=(pltpu.SemaphoreType.DMA((n_copies,)), pltpu.SemaphoreType.DMA((n_copies,)),
                   *[pltpu.HBM(b.shape, b.dtype) for b in bufs], jax.ShapeDtypeStruct((SUBLANES, LANES), F32)),
        input_output_aliases={i: 2 + i for i in range(ns + nl)},
        compiler_params=pltpu.CompilerParams(has_side_effects=SIDE_EFFECT),
    )(*[_in_hbm(b) for b in bufs], *extra)
    return res[0], res[1], list(res[2:2 + ns]), list(res[2 + ns:2 + ns + nl]), res[-1]


def _copies_wait(started, plan, after, *, name):
    send_sems, recv_sems, srcs, lands, _ = started
    ns, nl = len(srcs), len(lands)

    def body(*refs):
        src_refs, land_refs = refs[:ns], refs[ns:ns + nl]
        send_sems, recv_sems = refs[ns + nl:ns + nl + 2]
        for k, (src, dst, peer, here) in enumerate(plan(src_refs, land_refs, *_coords())):
            pltpu.make_async_remote_copy(src_ref=src, dst_ref=dst, send_sem=send_sems.at[k], recv_sem=recv_sems.at[k],
                                         device_id=peer, device_id_type=MESH).wait_send()
            pltpu.make_async_remote_copy(src_ref=src, dst_ref=here, send_sem=send_sems.at[k], recv_sem=recv_sems.at[k],
                                         device_id=peer, device_id_type=MESH).wait_recv()

    bufs = [*srcs, *lands]
    res = pl.pallas_call(
        body, name=name, in_specs=[HBM] * (ns + nl) + [SEM, SEM, pl.BlockSpec(memory_space=pl.ANY)],
        out_specs=[HBM] * (ns + nl), out_shape=[pltpu.HBM(b.shape, b.dtype) for b in bufs],
        input_output_aliases={i: i for i in range(ns + nl)},
        compiler_params=pltpu.CompilerParams(has_side_effects=SIDE_EFFECT),
    )(*bufs, send_sems, recv_sems, after)
    return list(res[ns:])


def _dev_index(dev):
    return 4 * dev[0] + 2 * dev[1] + dev[2]


def _ag_chips_plan(src_refs, land_refs, x, y, c):
    me = _dev_index((x, y, c))
    return [(src, land.at[me], peer, land.at[_dev_index(peer)])
            for src, land in zip(src_refs, land_refs) for peer in (_flip_y(x, y, c), _flip_x(x, y, c), _flip_xy(x, y, c))]


def _ag_sibling_plan(src_refs, land_refs, x, y, c):
    chips = [(x, y), (x, 1 - y), (1 - x, y), (1 - x, 1 - y)]
    return [(land.at[_dev_index((*chip, c))], land.at[_dev_index((*chip, c))], (x, y, 1 - c),
             land.at[_dev_index((*chip, 1 - c))]) for land in land_refs for chip in chips]


def _ag_direct_plan(src_refs, land_refs, x, y, c):
    me = _dev_index((x, y, c))
    plan = []
    for src, land in zip(src_refs, land_refs):
        for m in range(1, N_DEV):
            peer = (x + (m >> 2) * (1 - 2 * x), y + ((m >> 1) & 1) * (1 - 2 * y), c + (m & 1) * (1 - 2 * c))
            plan.append((src, land.at[me], peer, land.at[_dev_index(peer)]))
    return plan


def _rs_direct_plan(src_refs, land_refs, x, y, c):
    plan = []
    for src, land in zip(src_refs, land_refs):
        for m in range(1, N_DEV):
            peer = (x + (m >> 2) * (1 - 2 * x), y + ((m >> 1) & 1) * (1 - 2 * y), c + (m & 1) * (1 - 2 * c))
            plan.append((src.at[_dev_index(peer)], land.at[m - 1], peer, land.at[m - 1]))
    return plan


def _rs_start(grads, me, *, name, after=None):
    own = [lax.dynamic_index_in_dim(g, me, 0, keepdims=False) for g in grads]
    lands = [lax.empty((N_DEV - 1, *g.shape[1:]), g.dtype) for g in grads]
    return _copies_start(grads, lands, _rs_direct_plan, (N_DEV - 1) * len(grads), name=name, after=after), own


def _rs_finish(started, after, *, name):
    handle, own = started
    got = _copies_wait(handle, _rs_direct_plan, after, name=name)
    return [[o, land] for o, land in zip(own, got)]


def _gathered_cols(w8):
    return w8.transpose(1, 0, 2).reshape(w8.shape[1], -1)


def _pair_major(wt):
    return wt.reshape(3, ATTN_W // LANES, LANES, -1).transpose(1, 0, 2, 3).reshape(3 * ATTN_W, -1)


def kernel(x, norm1_g, w_in, attn_norm_g, hgrn_norm_g, hgrn_lb_logits, w_out, norm2_g, w_up, conv_w, conv_b, w_down, final_norm_g, loss_target, m_norm1_g, m_w_in, m_attn_norm_g, m_hgrn_norm_g, m_hgrn_lb_logits, m_w_out, m_norm2_g, m_w_up, m_conv_w, m_conv_b, m_w_down, m_final_norm_g, v_norm1_g, v_w_in, v_attn_norm_g, v_hgrn_norm_g, v_hgrn_lb_logits, v_w_out, v_norm2_g, v_w_up, v_conv_w, v_conv_b, v_w_down, v_final_norm_g):
    xs, target = x[0], loss_target[0]
    S, D = xs.shape
    NA = 3 * ATTN_W
    fng = final_norm_g.reshape(1, D)

    t = lambda a: a[0].T
    casts = [_sum_cast([w], BF16, name=f"cast_{nm}") for nm, w in
             (("w_in", t(w_in)), ("w_out", w_out[0]), ("w_up", t(w_up)), ("w_down", w_down[0]))]
    me = _dev_index(_coords())
    (g_in,) = _all_gather(casts[:1], name="ag_w_in")
    later = casts[1:] + [conv_w[0]]
    ag1 = _copies_start(later, [lax.empty((N_DEV, *s.shape), s.dtype) for s in later], _ag_chips_plan,
                        3 * len(later), name="ag_chips_start", after=g_in)
    wi = g_in.reshape(-1, D)
    wi_a = _pair_major(wi[:NA])

    u1, proj_a, proj_h, proj_f = _in_proj(xs, norm1_g + ag1[4][0, 0], wi_a, wi, NA, name="in_proj")
    attn, lse = _attn_fwd(proj_a, name="attn_fwd")
    lands = _copies_wait(ag1, _ag_chips_plan, attn, name="ag_chips_wait")
    lands = [lax.dynamic_update_index_in_dim(l, s, me, 0) for l, s in zip(lands, later)]
    ag2 = _copies_start([], lands, _ag_sibling_plan, 4 * len(later), name="ag_sibling_start")
    rec, states = _hgrn_fwd(proj_h, proj_f, hgrn_lb_logits + ag2[4][0, 0], name="hgrn_fwd")
    g_out, g_up, g_down, g_cw = _copies_wait(ag2, _ag_sibling_plan, rec, name="ag_sibling_wait")
    wo = g_out.reshape(-1, D)
    wu = g_up.reshape(-1, D)
    wd = g_down.reshape(-1, D)
    cw = _gathered_cols(g_cw)
    h1, u2, mixed = _out_proj(attn, rec, proj_h, xs, attn_norm_g, hgrn_norm_g, norm2_g, wo, name="out_proj")
    gate, val, act = _up_glu(u2, wu, cw, conv_b, name="up_glu")
    dh2, dh2b, d_fng, loss_part = _down_loss(act, wd, h1, fng, target, name="down_loss")

    dgate, dval, d_cw, d_cb = _dact_glu_bwd(dh2b, wd, gate, val, cw, conv_b, name="dact_glu_bwd")
    dw_down = _mm_tn(act, dh2b, tm=256, name="dw_down")
    dh1, dh1b, d_n2g = _grad_norm_input([dgate, dval], [(wu, 0)], h1, norm2_g, dh2, name="du2_norm2_bwd")
    F = dgate.shape[1]
    dw_up = _mm_tn(dgate, u2, tm=256, rows=2 * F, name="dw_up_gate")
    dw_up = _mm_tn(dval, u2, tm=256, rows=2 * F, row_block=lambda i: i + F // 256, into=dw_up, name="dw_up_val")
    rs_ffn = _rs_start([dw_down.reshape(N_DEV, -1, D), dw_up.reshape(N_DEV, -1, D)], me, name="rs_ffn_start")
    dattn, delta, drec, dhg, d_ang, d_hng = _dmix_post_bwd(dh1b, wo, attn, rec, proj_h, attn_norm_g + rs_ffn[0][4][0, 0],
                                                          hgrn_norm_g, name="dmix_post_bwd")
    dw_out = _mm_tn(mixed, dh1b, name="dw_out")
    rs_out = _rs_start([dw_out.reshape(N_DEV, -1, D)], me, name="rs_out_start")
    dproj_h, d_lbl = _hgrn_bwd(proj_h, proj_f, hgrn_lb_logits + rs_out[0][4][0, 0], states, drec, name="hgrn_bwd")
    small = [("loss", loss_part, None, None, None),
             ("attn_norm_g", d_ang, attn_norm_g, m_attn_norm_g, v_attn_norm_g),
             ("hgrn_norm_g", d_hng, hgrn_norm_g, m_hgrn_norm_g, v_hgrn_norm_g),
             ("hgrn_lb_logits", d_lbl, hgrn_lb_logits, m_hgrn_lb_logits, v_hgrn_lb_logits),
             ("norm2_g", d_n2g, norm2_g, m_norm2_g, v_norm2_g),
             ("conv_b", d_cb, conv_b, m_conv_b, v_conv_b),
             ("final_norm_g", d_fng, final_norm_g, m_final_norm_g, v_final_norm_g)]
    pack = lambda arrs: jnp.concatenate([a.reshape(1, -1) for a in arrs], axis=1)
    small_own = [pack([s[1] for s in small]), d_cw]
    ag_small = _copies_start(small_own, [lax.empty((N_DEV, *s.shape), s.dtype) for s in small_own], _ag_direct_plan,
                             (N_DEV - 1) * len(small_own), name="ag_small_start")
    dproj_a = _attn_bwd(proj_a, dattn, lse, delta, name="attn_bwd")
    pairs = ATTN_W // LANES
    dw_in = _mm_tn(dproj_a, u1, tm=LANES, rows=wi.shape[0], row_block=lambda i: pairs * (i % 3) + i // 3,
                   name="dw_in_attn")
    dw_in = _mm_tn(dproj_h, u1, tm=256, rows=wi.shape[0], row_block=lambda i: i + NA // 256, into=dw_in,
                   name="dw_in_hgrn")
    dw_in = _mm_tn(dhg, u1, tm=256, rows=wi.shape[0], row_block=lambda i: i + (NA + 3 * HGRN_W) // 256, into=dw_in,
                   name="dw_in_gate")
    rs_in = _rs_start([dw_in.reshape(N_DEV, -1, D)], me, name="rs_in_start", after=ag_small[4])
    grad_x, _, d_n1g = _grad_norm_input([dproj_a, dproj_h, dhg], [(wi_a, 0), (wi, NA)], xs,
                                        norm1_g + rs_in[0][4][0, 0], dh1, name="du1_norm1_bwd")

    res = {}

    def update(nm, parts, w, m, v, transposed=False):
        if transposed:
            res[nm] = [r.T[None] for r in _adamw(parts, t(w), t(m), t(v), name=f"adamw_{nm}")]
        else:
            res[nm] = _adamw(parts, w, m, v, name=f"adamw_{nm}")

    g_down, g_up = _rs_finish(rs_ffn, grad_x, name="rs_ffn_wait")
    update("w_down", g_down, w_down, m_w_down, v_w_down)
    update("w_up", g_up, w_up, m_w_up, v_w_up, transposed=True)
    (g_out,) = _rs_finish(rs_out, grad_x, name="rs_out_wait")
    update("w_out", g_out, w_out, m_w_out, v_w_out)
    (g_in,) = _rs_finish(rs_in, res["w_up"][1], name="rs_in_wait")
    update("w_in", g_in, w_in, m_w_in, v_w_in, transposed=True)

    g_small, g_dcw = [lax.dynamic_update_index_in_dim(l, s, me, 0)
                      for l, s in zip(_copies_wait(ag_small, _ag_direct_plan, grad_x, name="ag_small_wait"), small_own)]
    sm = _adamw_packed(g_small, [s[1].size for s in small], [None if s[2] is None else s[2:] for s in small],
                       name="adamw_small")
    for (nm, *_), r in zip(small, sm):
        res[nm] = r
    ncw = conv_w.shape[-1]
    mine_cw = lax.dynamic_slice_in_dim(g_dcw, me * ncw, ncw, axis=2)
    res["conv_w"] = _adamw([mine_cw], conv_w, m_conv_w, v_conv_w, name="adamw_conv_w")
    late, _ = lax.optimization_barrier((d_n1g, res["w_in"][1]))
    update("norm1_g", _all_gather([late], name="ag_norm1_grad"), norm1_g, m_norm1_g, v_norm1_g)

    loss = res["loss"][0][0, 0]
    order = ["norm1_g", "w_in", "attn_norm_g", "hgrn_norm_g", "hgrn_lb_logits", "w_out", "norm2_g", "w_up",
             "conv_w", "conv_b", "w_down", "final_norm_g"]
    return (loss, grad_x[None], *[res[nm][0] for nm in order], *[res[nm][1] for nm in order],
            *[res[nm][2] for nm in order], *[res[nm][3] for nm in order])
```

```python
import jax
import jax.numpy as jnp
from jax import lax
from jax.experimental import pallas as pl
from jax.experimental.pallas import tpu as pltpu

F32, BF16 = jnp.float32, jnp.bfloat16
NORM_EPS = 1e-6
ATTN_HEADS, HEAD_DIM, ATTN_BLOCK = 8, 64, 128
DILATIONS = (1, 4, 16)
ATTN_SCALE = HEAD_DIM ** -0.5
ATTN_W = ATTN_HEADS * HEAD_DIM
HGRN_HEADS, HGRN_DIM, HGRN_CHUNK = 4, 128, 64
HGRN_W = HGRN_HEADS * HGRN_DIM
ADAM_LR, ADAM_B1, ADAM_B2, ADAM_EPS, ADAM_WD, ADAM_STEP = 0.001, 0.9, 0.999, 1e-08, 0.01, 10
LANES, SUBLANES = 128, 8
VMEM_LIMIT_BYTES = 56 * 1024 * 1024
N_DEV = 8
MESH = pl.DeviceIdType.MESH
HBM = pl.BlockSpec(memory_space=pltpu.HBM)
HIGHEST = lax.Precision.HIGHEST


def _cparams(*sem):
    return pltpu.CompilerParams(dimension_semantics=sem, vmem_limit_bytes=VMEM_LIMIT_BYTES)


def _tile(n, pref):
    if n <= pref:
        return n
    t = (pref // LANES) * LANES
    while n % t:
        t -= LANES
    return t


def _resident(shape):
    return pl.BlockSpec(shape, lambda *_: (0,) * len(shape), pipeline_mode=pl.Buffered(1))


def _dot(a, b, dims, precision=None):
    return lax.dot_general(a, b, (dims, ((), ())), precision=precision, preferred_element_type=F32)


def _nn(a, b, precision=None):
    return _dot(a, b, ((1,), (0,)), precision)


def _nt(a, b):
    return _dot(a, b, ((1,), (1,)))


def _tn(a, b):
    return _dot(a, b, ((0,), (0,)))


def _sigmoid(x):
    return 1.0 / (1.0 + jnp.exp(-x))


def _rstd(x):
    return lax.rsqrt(jnp.mean(x * x, axis=-1, keepdims=True) + NORM_EPS)


def _norm_bwd(x, g, du):
    r = _rstd(x)
    xh = x * r
    dxh = du * g
    return r * (dxh - xh * jnp.mean(dxh * xh, axis=-1, keepdims=True)), du * xh


def _row_halves(tm):
    return [pl.ds(0, tm // 2), pl.ds(tm // 2, tm // 2)]


def _accumulate(ref, part, first):
    @pl.when(first)
    def _():
        ref[...] = part

    @pl.when(jnp.logical_not(first))
    def _():
        ref[...] += part


def _mm_tn(x, dy, *, name, tm=512, tn=1024, rows=None, row_block=None, into=None):
    S, M = x.shape
    N = dy.shape[1]
    tm, tn = _tile(M, tm), _tile(N, tn)
    row_block = row_block or (lambda i: i)

    def body(x_ref, dy_ref, *rest):
        o_ref, xt_ref = rest[-2:]

        @pl.when(pl.program_id(1) == 0)
        def _():
            xt_ref[...] = x_ref[...].T

        o_ref[...] = _nn(xt_ref[...], dy_ref[...]).astype(BF16)

    operands = [x, dy] + ([] if into is None else [into])
    return pl.pallas_call(
        body, name=name, grid=(M // tm, N // tn),
        in_specs=[pl.BlockSpec((S, tm), lambda i, j: (0, i)), pl.BlockSpec((S, tn), lambda i, j: (0, j))]
        + ([] if into is None else [pl.BlockSpec(memory_space=pl.ANY)]),
        out_specs=pl.BlockSpec((tm, tn), lambda i, j: (row_block(i), j)),
        out_shape=jax.ShapeDtypeStruct((rows or M, N), BF16),
        input_output_aliases={} if into is None else {2: 0},
        scratch_shapes=[pltpu.VMEM((tm, S), BF16)], compiler_params=_cparams("parallel", "arbitrary"),
    )(*operands)


def _in_proj(x, g, wt_attn, wt, row0, *, name, tm=512):
    S, D = x.shape
    NA, NH, W = wt_attn.shape[0], wt.shape[0] - row0, HGRN_W

    def body(x_ref, g_ref, wa_ref, w_ref, u_ref, a_ref, h_ref, f_ref):
        xv = x_ref[...]
        u = (xv * _rstd(xv) * g_ref[...]).astype(BF16)
        u_ref[...] = u
        a_ref[...] = _nt(u, wa_ref[...]).astype(BF16)
        ph = _nt(u, w_ref[row0:row0 + NH, :])
        h_ref[...] = ph.astype(BF16)
        f_ref[...] = ph[:, W:2 * W]

    row = lambda w: pl.BlockSpec((tm, w), lambda i: (i, 0))
    return pl.pallas_call(
        body, name=name, grid=(S // tm,),
        in_specs=[row(D), pl.BlockSpec((1, D), lambda i: (0, 0)), _resident(wt_attn.shape), _resident(wt.shape)],
        out_specs=[row(D), row(NA), row(NH), row(W)],
        out_shape=[jax.ShapeDtypeStruct((S, D), BF16), jax.ShapeDtypeStruct((S, NA), BF16),
                   jax.ShapeDtypeStruct((S, NH), BF16), jax.ShapeDtypeStruct((S, W), F32)],
        compiler_params=_cparams("parallel"),
    )(x, g, wt_attn, wt)


PAIR_W = 3 * LANES
ATTN_UNROLL_FWD, ATTN_UNROLL_BWD = 8, 4


def _attn_masks(first):
    qi = lax.broadcasted_iota(jnp.int32, (ATTN_BLOCK, 2 * ATTN_BLOCK), 0)
    kj = lax.broadcasted_iota(jnp.int32, (ATTN_BLOCK, 2 * ATTN_BLOCK), 1)
    dist = qi + ATTN_BLOCK - kj
    valid = (dist >= 0) & (dist <= ATTN_BLOCK) & jnp.logical_or(kj >= ATTN_BLOCK, jnp.logical_not(first))
    lane = lax.broadcasted_iota(jnp.int32, (1, LANES), 1)
    return valid, lane


def _for_residue_blocks(S, d, fn):
    span = ATTN_BLOCK * d
    nb = S // span

    def step(n, carry):
        base = pl.multiple_of(n * span, span)
        for r in range(d):
            off = pl.multiple_of((r * nb + n) * ATTN_BLOCK, ATTN_BLOCK)
            fn(lambda ref, r=r: _block_rows(ref, base, r, d),
               lambda ref, val, r=r: _set_block_rows(ref, base, r, d, val), off)
        return carry

    lax.fori_loop(0, nb, step, 0)


def _for_blocks(S, unroll, fn):
    def step(i, carry):
        fn([(pl.multiple_of((i * unroll + u) * ATTN_BLOCK, ATTN_BLOCK), i * unroll + u) for u in range(unroll)])
        return carry

    lax.fori_loop(0, S // ATTN_BLOCK // unroll, step, 0)


def _head_value(x2, lane, e):
    return jnp.sum(jnp.where(lane == HEAD_DIM * e, x2, 0.0), axis=-1, keepdims=True)


def _block_rows(ref, base, r, d):
    if d == 1:
        return ref[pl.ds(base, ATTN_BLOCK), :]
    return ref.at[pl.ds(base, ATTN_BLOCK * d)][pl.ds(r, ATTN_BLOCK, stride=d), :]


def _set_block_rows(ref, base, r, d, val):
    if d == 1:
        ref[pl.ds(base, ATTN_BLOCK), :] = val
    else:
        ref.at[pl.ds(base, ATTN_BLOCK * d)][pl.ds(r, ATTN_BLOCK, stride=d), :] = val


def _order4_to_16(src, dst, pad):
    S = src.shape[0]
    q4, q16 = S // 4, S // 16
    for r in range(4):
        for a in range(4):
            for n in range(q16 // ATTN_BLOCK):
                rows = src.at[pl.ds(r * q4 + 4 * ATTN_BLOCK * n, 4 * ATTN_BLOCK)][pl.ds(a, ATTN_BLOCK, stride=4), :]
                dst[pl.ds(pad + (4 * a + r) * q16 + ATTN_BLOCK * n, ATTN_BLOCK), :] = rows.astype(dst.dtype)


def _order16_to_4(src, pad, dst):
    S = dst.shape[0]
    q4, q16 = S // 4, S // 16
    for r in range(4):
        for a in range(4):
            for n in range(q16 // ATTN_BLOCK):
                rows = src[pl.ds(pad + (4 * a + r) * q16 + ATTN_BLOCK * n, ATTN_BLOCK), :]
                dst.at[pl.ds(r * q4 + 4 * ATTN_BLOCK * n, 4 * ATTN_BLOCK)][pl.ds(a, ATTN_BLOCK, stride=4), :] = rows


def _regroup(S, d, pairs, tmp):
    for src, dst, pad in pairs:
        if d == 16:
            def to_tmp(rows, _, off, src=src):
                tmp[pl.ds(off, ATTN_BLOCK), :] = rows(src)

            _for_residue_blocks(S, 4, to_tmp)
            _order4_to_16(tmp, dst, pad)
    if d != 16:
        def to_dst(rows, _, off):
            for src, dst, pad in pairs:
                dst[pl.ds(pad + off, ATTN_BLOCK), :] = rows(src).astype(dst.dtype)

        _for_residue_blocks(S, d, to_dst)


def _split_pair(p_ref, qs, ks, vs, bk, bv):
    qs[...] = p_ref[:, 0:LANES].astype(F32)
    ks[...] = p_ref[:, LANES:2 * LANES].astype(F32)
    vs[...] = p_ref[:, 2 * LANES:3 * LANES].astype(F32)
    bk[0:ATTN_BLOCK, :] = jnp.zeros((ATTN_BLOCK, LANES), bk.dtype)
    bv[0:ATTN_BLOCK, :] = jnp.zeros((ATTN_BLOCK, LANES), bv.dtype)


def _attn_fwd(proj_a, *, name):
    S = proj_a.shape[0]

    def body(p_ref, o_ref, l_ref, qs, ks, vs, bq, bk, bv, bo, bl, to, tl):
        _split_pair(p_ref, qs, ks, vs, bk, bv)
        for d in DILATIONS:
            nb = S // (ATTN_BLOCK * d)
            _regroup(S, d, ((qs, bq, 0), (ks, bk, ATTN_BLOCK), (vs, bv, ATTN_BLOCK)), to)

            def blocks(group, nb=nb):
                lane = lax.broadcasted_iota(jnp.int32, (1, LANES), 1)
                heads = [(lane >= HEAD_DIM * e) & (lane < HEAD_DIM * (e + 1)) for e in range(LANES // HEAD_DIM)]
                wins = [pl.ds(off, 2 * ATTN_BLOCK) for off, _ in group]
                s = [[_nt(jnp.where(mh, bq[pl.ds(off, ATTN_BLOCK), :], jnp.zeros((ATTN_BLOCK, LANES), BF16)), bk[win, :])
                      for mh in heads] for (off, _), win in zip(group, wins)]
                p, m, l = [], [], []
                for (off, b), su in zip(group, s):
                    valid, _ = _attn_masks(jnp.bitwise_and(b, nb - 1) == 0)
                    sm = [jnp.where(valid, x * ATTN_SCALE, -jnp.inf) for x in su]
                    m.append([jnp.max(x, axis=-1, keepdims=True) for x in sm])
                    p.append([jnp.exp(x - mx) for x, mx in zip(sm, m[-1])])
                    l.append([jnp.sum(x, axis=-1, keepdims=True) for x in p[-1]])
                o = [[_nn(x.astype(BF16), bv[win, :]) for x in pu] for pu, win in zip(p, wins)]
                for (off, _), ou, mu, lu in zip(group, o, m, l):
                    o2 = jnp.zeros((ATTN_BLOCK, LANES), F32)
                    l2 = jnp.zeros((ATTN_BLOCK, LANES), F32)
                    for mh, oe, me_, le in zip(heads, ou, mu, lu):
                        o2 = jnp.where(mh, oe / le, o2)
                        l2 = jnp.where(mh, me_ + jnp.log(le), l2)
                    bo[pl.ds(off, ATTN_BLOCK), :] = o2
                    bl[pl.ds(off, ATTN_BLOCK), :] = l2

            _for_blocks(S, ATTN_UNROLL_FWD, blocks)

            if d == 16:
                _order16_to_4(bo, 0, to)
                _order16_to_4(bl, 0, tl)
            src_o, src_l = (to, tl) if d == 16 else (bo, bl)

            def merge(rows, set_rows, off, d=d, src_o=src_o, src_l=src_l):
                blk = pl.ds(off, ATTN_BLOCK)
                o2, l2 = src_o[blk, :], src_l[blk, :]
                if d != DILATIONS[0]:
                    lo, oo = rows(l_ref), rows(o_ref)
                    ln = jnp.maximum(lo, l2)
                    wa, wb = jnp.exp(lo - ln), jnp.exp(l2 - ln)
                    o2 = (wa * oo + wb * o2) / (wa + wb)
                    l2 = ln + jnp.log(wa + wb)
                set_rows(o_ref, o2)
                set_rows(l_ref, l2)

            _for_residue_blocks(S, min(d, 4), merge)

    slab = pl.BlockSpec((S, LANES), lambda p: (0, p))
    f32_slab, bf16_slab = pltpu.VMEM((S, LANES), F32), pltpu.VMEM((S, LANES), BF16)
    bf16_window = pltpu.VMEM((S + ATTN_BLOCK, LANES), BF16)
    return pl.pallas_call(
        body, name=name, grid=(ATTN_W // LANES,), in_specs=[pl.BlockSpec((S, PAIR_W), lambda p: (0, p))],
        out_specs=[slab, slab],
        out_shape=[jax.ShapeDtypeStruct((S, ATTN_W), F32), jax.ShapeDtypeStruct((S, ATTN_W), F32)],
        scratch_shapes=[f32_slab] * 3 + [bf16_slab, bf16_window, bf16_window] + [f32_slab] * 4,
        compiler_params=_cparams("parallel"),
    )(proj_a)


def _attn_bwd(proj_a, do, lse, delta, *, name):
    S = proj_a.shape[0]

    def body(p_ref, do_ref, lse_ref, dl_ref, o_ref, qs, ks, vs, dqs, dks, dvs, bq, bk, bv, bdo, blse, bdl, bdq, bdk, bdv,
             tmp):
        _split_pair(p_ref, qs, ks, vs, bk, bv)
        bdk[0:ATTN_BLOCK, :] = jnp.zeros((ATTN_BLOCK, LANES), F32)
        bdv[0:ATTN_BLOCK, :] = jnp.zeros((ATTN_BLOCK, LANES), F32)
        for d in DILATIONS:
            nb = S // (ATTN_BLOCK * d)
            _regroup(S, d, ((qs, bq, 0), (ks, bk, ATTN_BLOCK), (vs, bv, ATTN_BLOCK), (do_ref, bdo, 0),
                            (lse_ref, blse, 0), (dl_ref, bdl, 0)), tmp)

            def blocks(group, nb=nb):
                lane = lax.broadcasted_iota(jnp.int32, (1, LANES), 1)
                heads = [(lane >= HEAD_DIM * e) & (lane < HEAD_DIM * (e + 1)) for e in range(LANES // HEAD_DIM)]
                zero = jnp.zeros((ATTN_BLOCK, LANES), BF16)
                chains = [(off, b, e, mh) for off, b in group for e, mh in enumerate(heads)]
                qm = [jnp.where(mh, bq[pl.ds(off, ATTN_BLOCK), :], zero) for off, _, _, mh in chains]
                dom = [jnp.where(mh, bdo[pl.ds(off, ATTN_BLOCK), :], zero) for off, _, _, mh in chains]
                s = [_nt(x, bk[pl.ds(off, 2 * ATTN_BLOCK), :]) for x, (off, _, _, _) in zip(qm, chains)]
                dp = [_nt(x, bv[pl.ds(off, 2 * ATTN_BLOCK), :]) for x, (off, _, _, _) in zip(dom, chains)]
                p, ds = [], []
                for (off, b, e, _), sc, dpc in zip(chains, s, dp):
                    valid, _ = _attn_masks(jnp.bitwise_and(b, nb - 1) == 0)
                    blk = pl.ds(off, ATTN_BLOCK)
                    pc = jnp.where(valid, jnp.exp(sc * ATTN_SCALE - _head_value(blse[blk, :], lane, e)), 0.0)
                    ds.append((pc * (dpc - _head_value(bdl[blk, :], lane, e)) * ATTN_SCALE).astype(BF16))
                    p.append(pc.astype(BF16))
                dq = [_nn(x, bk[pl.ds(off, 2 * ATTN_BLOCK), :]) for x, (off, _, _, _) in zip(ds, chains)]
                dk = [_tn(x, y) for x, y in zip(ds, qm)]
                dv = [_tn(x, y) for x, y in zip(p, dom)]
                nh = len(heads)
                for u, (off, _) in enumerate(group):
                    dq2 = jnp.zeros((ATTN_BLOCK, LANES), F32)
                    for mh, x in zip(heads, dq[nh * u:nh * (u + 1)]):
                        dq2 = jnp.where(mh, x, dq2)
                    bdq[pl.ds(off, ATTN_BLOCK), :] = dq2
                    for acc, grads in ((bdk, dk), (bdv, dv)):
                        win_grad = sum(grads[nh * u + 1:nh * (u + 1)], grads[nh * u])
                        acc[pl.ds(off, ATTN_BLOCK), :] += win_grad[:ATTN_BLOCK]
                        acc[pl.ds(off + ATTN_BLOCK, ATTN_BLOCK), :] = win_grad[ATTN_BLOCK:]

            _for_blocks(S, ATTN_UNROLL_BWD, blocks)

            outs = ((dqs, bdq, 0), (dks, bdk, ATTN_BLOCK), (dvs, bdv, ATTN_BLOCK))
            if d == 16:
                for acc, grad, pad in outs:
                    _order16_to_4(grad, pad, tmp)

                    def add(rows, set_rows, off, acc=acc):
                        set_rows(acc, rows(acc) + tmp[pl.ds(off, ATTN_BLOCK), :])

                    _for_residue_blocks(S, 4, add)
            else:
                def scatter(rows, set_rows, off, d=d):
                    for acc, grad, pad in outs:
                        part = grad[pl.ds(pad + off, ATTN_BLOCK), :]
                        set_rows(acc, part if d == DILATIONS[0] else rows(acc) + part)

                _for_residue_blocks(S, d, scatter)
        o_ref[:, 0:LANES] = dqs[...].astype(BF16)
        o_ref[:, LANES:2 * LANES] = dks[...].astype(BF16)
        o_ref[:, 2 * LANES:3 * LANES] = dvs[...].astype(BF16)

    slab = pl.BlockSpec((S, LANES), lambda p: (0, p))
    pair = pl.BlockSpec((S, PAIR_W), lambda p: (0, p))
    f32_slab, bf16_slab = pltpu.VMEM((S, LANES), F32), pltpu.VMEM((S, LANES), BF16)
    f32_window, bf16_window = pltpu.VMEM((S + ATTN_BLOCK, LANES), F32), pltpu.VMEM((S + ATTN_BLOCK, LANES), BF16)
    return pl.pallas_call(
        body, name=name, grid=(ATTN_W // LANES,), in_specs=[pair, slab, slab, slab], out_specs=pair,
        out_shape=jax.ShapeDtypeStruct(proj_a.shape, BF16),
        scratch_shapes=[f32_slab] * 6 + [bf16_slab, bf16_window, bf16_window, bf16_slab, f32_slab, f32_slab,
                                         f32_slab, f32_window, f32_window, f32_slab],
        compiler_params=_cparams("parallel"),
    )(proj_a, do, lse, delta)


HG_T = 2 * HGRN_CHUNK
HG_GROUPS = 2
HG_STEP = HG_GROUPS * HG_T


def _hgrn_consts():
    row = lax.broadcasted_iota(jnp.int32, (HG_T, HG_T), 0)
    col = lax.broadcasted_iota(jnp.int32, (HG_T, HG_T), 1)
    same = (row >= HGRN_CHUNK) == (col >= HGRN_CHUNK)
    return row, same & (col <= row), same & (col >= row)


def _lower_bound(logits_ref):
    l0, l1 = logits_ref[0:1, :], logits_ref[1:2, :]
    mx = jnp.maximum(l0, l1)
    e0, e1 = jnp.exp(l0 - mx), jnp.exp(l1 - mx)
    return e0 / (e0 + e1)


def _hgrn_chains():
    chains = [(g, h) for g in range(HG_GROUPS) for h in range(HGRN_HEADS)]
    rows = [pl.ds(HG_T * g, HG_T) for g, _ in chains]
    lanes = [slice(HGRN_DIM * h, HGRN_DIM * (h + 1)) for _, h in chains]
    return chains, rows, lanes


def _hgrn_gates(qs, fs, lbs, row, causal):
    C = HGRN_CHUNK
    tri = jnp.where(causal, 1.0, 0.0).astype(F32)
    sgs = [_sigmoid(f) for f in fs]
    forgets = [lb + (1.0 - lb) * sg for lb, sg in zip(lbs, sgs)]
    logfs = [jnp.log(forget) for forget in forgets]
    bs = [_nn(tri, logf, HIGHEST) for logf in logfs]
    out = []
    for q, sg, forget, logf, b in zip(qs, sgs, forgets, logfs, bs):
        key = 1.0 - forget
        bend0 = jnp.sum(logf[:C], axis=0, keepdims=True)
        bend1 = jnp.sum(logf[C:], axis=0, keepdims=True)
        bend = jnp.where(row < C, bend0, bend1)
        eb, emb, eend = jnp.exp(b), jnp.exp(-b), jnp.exp(bend - b)
        sq = _sigmoid(q)
        out.append(dict(sg=sg, forget=forget, key=key, bend0=bend0, bend1=bend1, eb=eb, emb=emb, eend=eend, sq=sq,
                        qd=q * sq * eb, ki=key * emb, ke=key * eend))
    return out


def _hgrn_fwd(proj, proj_f, logits, *, name):
    S = proj.shape[0]
    W, C = HGRN_W, HGRN_CHUNK

    def body(q_ref, f_ref, i_ref, lg_ref, rec_ref, st_ref, s_ref):
        @pl.when(pl.program_id(0) == 0)
        def _():
            s_ref[...] = jnp.zeros_like(s_ref)

        row, causal, _ = _hgrn_consts()
        lb_all = _lower_bound(lg_ref)
        chains, rows, lanes = _hgrn_chains()
        n = range(len(chains))
        gts = _hgrn_gates([q_ref[rows[c], lanes[c]].astype(F32) for c in n], [f_ref[rows[c], lanes[c]] for c in n],
                          [lb_all[:, lanes[c]] for c in n], row, causal)
        qd, ki, ke = ([gt[k].astype(BF16) for gt in gts] for k in ("qd", "ki", "ke"))
        iv = [i_ref[rows[c], lanes[c]].astype(BF16) for c in n]
        a = [_nt(qd[c], ki[c]) for c in n]
        u0 = [_tn(iv[c][:C], ke[c][:C]) for c in n]
        u1 = [_tn(iv[c][C:], ke[c][C:]) for c in n]
        state = [s_ref[h] for h in range(HGRN_HEADS)]
        s0, s1 = [], []
        for c, (g, h) in enumerate(chains):
            s0.append(state[h])
            s1.append(jnp.exp(gts[c]["bend0"]) * s0[c] + u0[c])
            state[h] = jnp.exp(gts[c]["bend1"]) * s1[c] + u1[c]
        o0 = [_nt(qd[c][:C], s0[c].astype(BF16)) for c in n]
        o1 = [_nt(qd[c][C:], s1[c].astype(BF16)) for c in n]
        o = [_nn(jnp.where(causal, a[c], 0.0).astype(BF16), iv[c]) for c in n]
        for c, (g, h) in enumerate(chains):
            st_ref[2 * g, h] = s0[c]
            st_ref[2 * g + 1, h] = s1[c]
            rec_ref[rows[c], lanes[c]] = o[c] + jnp.concatenate([o0[c], o1[c]], axis=0)
        for h in range(HGRN_HEADS):
            s_ref[h] = state[h]

    blk = lambda j: pl.BlockSpec((HG_STEP, W), lambda t: (t, j))
    return pl.pallas_call(
        body, name=name, grid=(S // HG_STEP,),
        in_specs=[blk(0), blk(0), blk(2), pl.BlockSpec((2, W), lambda t: (0, 0))],
        out_specs=[blk(0), pl.BlockSpec((2 * HG_GROUPS, HGRN_HEADS, HGRN_DIM, HGRN_DIM), lambda t: (t, 0, 0, 0))],
        out_shape=[jax.ShapeDtypeStruct((S, W), F32),
                   jax.ShapeDtypeStruct((S // C, HGRN_HEADS, HGRN_DIM, HGRN_DIM), F32)],
        scratch_shapes=[pltpu.VMEM((HGRN_HEADS, HGRN_DIM, HGRN_DIM), F32)],
        compiler_params=_cparams("arbitrary"),
    )(proj, proj_f, proj, logits)


def _hgrn_bwd(proj, proj_f, logits, states, drec, *, name):
    S = proj.shape[0]
    W, C = HGRN_W, HGRN_CHUNK
    nt = S // HG_STEP

    def body(q_ref, f_ref, i_ref, lg_ref, st_ref, do_ref, dp_ref, dlg_ref, ds_ref, dlb_ref):
        t = pl.program_id(0)

        @pl.when(t == 0)
        def _():
            ds_ref[...] = jnp.zeros_like(ds_ref)
            dlb_ref[...] = jnp.zeros_like(dlb_ref)

        row, causal, anti = _hgrn_consts()
        lb_all = _lower_bound(lg_ref)
        chains, rows, lanes = _hgrn_chains()
        n = range(len(chains))
        qs, lbs = [q_ref[rows[c], lanes[c]].astype(F32) for c in n], [lb_all[:, lanes[c]] for c in n]
        gts = _hgrn_gates(qs, [f_ref[rows[c], lanes[c]] for c in n], lbs, row, causal)
        qd, ki, ke = ([gt[k] for gt in gts] for k in ("qd", "ki", "ke"))
        qdb, kib, keb = ([x.astype(BF16) for x in xs] for xs in (qd, ki, ke))
        iv = [i_ref[rows[c], lanes[c]].astype(BF16) for c in n]
        dob = [do_ref[rows[c], lanes[c]].astype(BF16) for c in n]
        s0 = [st_ref[2 * g, h] for g, h in chains]
        s1 = [st_ref[2 * g + 1, h] for g, h in chains]
        dec0, dec1 = [jnp.exp(gt["bend0"]) for gt in gts], [jnp.exp(gt["bend1"]) for gt in gts]
        a = [_nt(qdb[c], kib[c]) for c in n]
        da = [_nt(dob[c], iv[c]) for c in n]
        dqd1 = [_nn(dob[c][C:], s1[c].astype(BF16)) for c in n]
        dqd0 = [_nn(dob[c][:C], s0[c].astype(BF16)) for c in n]
        t1 = [_tn(dob[c][C:], qdb[c][C:]) for c in n]
        t0 = [_tn(dob[c][:C], qdb[c][:C]) for c in n]
        carry = [ds_ref[h] for h in range(HGRN_HEADS)]
        ds1, ds0 = [None] * len(chains), [None] * len(chains)
        for c in reversed(n):
            h = chains[c][1]
            ds1[c] = carry[h]
            ds0[c] = dec1[c] * ds1[c] + t1[c]
            carry[h] = dec0[c] * ds0[c] + t0[c]
        for h in range(HGRN_HEADS):
            ds_ref[h] = carry[h]
        ds1b, ds0b = [x.astype(BF16) for x in ds1], [x.astype(BF16) for x in ds0]
        a = [jnp.where(causal, x, 0.0).astype(BF16) for x in a]
        da = [jnp.where(causal, x, 0.0).astype(BF16) for x in da]
        di1 = [_nt(keb[c][C:], ds1b[c]) for c in n]
        dke1 = [_nn(iv[c][C:], ds1b[c]) for c in n]
        di0 = [_nt(keb[c][:C], ds0b[c]) for c in n]
        dke0 = [_nn(iv[c][:C], ds0b[c]) for c in n]
        dqd_a = [_nn(da[c], kib[c]) for c in n]
        dki = [_tn(da[c], qdb[c]) for c in n]
        di_a = [_tn(a[c], dob[c]) for c in n]
        dqd, dke, db = [], [], []
        for c in n:
            ddec1 = jnp.sum(ds1[c] * s1[c], axis=0, keepdims=True)
            ddec0 = jnp.sum(ds0[c] * s0[c], axis=0, keepdims=True)
            dqd.append(dqd_a[c] + jnp.concatenate([dqd0[c], dqd1[c]], axis=0))
            h = chains[c][1]
            dp_ref[rows[c], 2 * W + HGRN_DIM * h:2 * W + HGRN_DIM * (h + 1)] = (
                di_a[c] + jnp.concatenate([di0[c], di1[c]], axis=0)).astype(BF16)
            dke.append(jnp.concatenate([dke0[c], dke1[c]], axis=0))
            gke = dke[c] * ke[c]
            dbend0 = jnp.sum(gke[:C], axis=0, keepdims=True) + ddec0 * dec0[c]
            dbend1 = jnp.sum(gke[C:], axis=0, keepdims=True) + ddec1 * dec1[c]
            dbc = dqd[c] * qd[c] - dki[c] * ki[c] - gke
            db.append(dbc + jnp.where(row == C - 1, dbend0, 0.0) + jnp.where(row == HG_T - 1, dbend1, 0.0))
        tri = jnp.where(anti, 1.0, 0.0).astype(F32)
        dlogf = [_nn(tri, db[c], HIGHEST) for c in n]
        for c in n:
            gt, lb, q, h = gts[c], lbs[c], qs[c], chains[c][1]
            dforget = dlogf[c] / gt["forget"] - (dki[c] * gt["emb"] + dke[c] * gt["eend"])
            sg, sq = gt["sg"], gt["sq"]
            dp_ref[rows[c], W + HGRN_DIM * h:W + HGRN_DIM * (h + 1)] = (
                dforget * (1.0 - lb) * sg * (1.0 - sg)).astype(BF16)
            dlb_ref[:, lanes[c]] += jnp.sum(dforget * (1.0 - sg), axis=0, keepdims=True)
            dp_ref[rows[c], lanes[c]] = (dqd[c] * gt["eb"] * sq * (1.0 + q * (1.0 - sq))).astype(BF16)

        @pl.when(t == nt - 1)
        def _():
            dl0 = dlb_ref[...] * lb_all * (1.0 - lb_all)
            dlg_ref[0:1, :] = dl0
            dlg_ref[1:2, :] = -dl0

    blk = lambda j: pl.BlockSpec((HG_STEP, W), lambda t: (nt - 1 - t, j))
    full = pl.BlockSpec((2, W), lambda t: (0, 0))
    return pl.pallas_call(
        body, name=name, grid=(nt,),
        in_specs=[blk(0), blk(0), blk(2), full,
                  pl.BlockSpec((2 * HG_GROUPS, HGRN_HEADS, HGRN_DIM, HGRN_DIM), lambda t: (nt - 1 - t, 0, 0, 0)), blk(0)],
        out_specs=[pl.BlockSpec((HG_STEP, 3 * W), lambda t: (nt - 1 - t, 0)), full],
        out_shape=[jax.ShapeDtypeStruct((S, 3 * W), BF16), jax.ShapeDtypeStruct((2, W), F32)],
        scratch_shapes=[pltpu.VMEM((HGRN_HEADS, HGRN_DIM, HGRN_DIM), F32), pltpu.VMEM((1, W), F32)],
        compiler_params=_cparams("arbitrary"),
    )(proj, proj_f, proj, logits, states, drec)


def _out_proj(attn, rec, proj_h, x, g_attn, g_hgrn, g_norm2, w_out, *, name, tm=512):
    S, D = x.shape
    AW, W = ATTN_W, HGRN_W

    def body(a_ref, r_ref, hg_ref, x_ref, ga_ref, gh_ref, g2_ref, w_ref, h_ref, u_ref, m_ref):
        av = a_ref[...]
        m_ref[:, :AW] = (av * _rstd(av) * ga_ref[...]).astype(BF16)
        for h in range(HGRN_HEADS):
            sl = slice(HGRN_DIM * h, HGRN_DIM * (h + 1))
            rv, hg = r_ref[:, sl], hg_ref[:, sl].astype(F32)
            m_ref[:, AW + HGRN_DIM * h:AW + HGRN_DIM * (h + 1)] = (
                (rv * _rstd(rv) * gh_ref[:, sl]) * (hg * _sigmoid(hg))).astype(BF16)
        h1 = x_ref[...] + _nn(m_ref[...], w_ref[...])
        h_ref[...] = h1
        u_ref[...] = (h1 * _rstd(h1) * g2_ref[...]).astype(BF16)

    row = lambda w, j=0: pl.BlockSpec((tm, w), lambda i: (i, j))
    vec = lambda w: pl.BlockSpec((1, w), lambda i: (0, 0))
    return pl.pallas_call(
        body, name=name, grid=(S // tm,),
        in_specs=[row(AW), row(W), row(W, 3), row(D), vec(AW), vec(W), vec(D), _resident(w_out.shape)],
        out_specs=[row(D), row(D), row(AW + W)],
        out_shape=[jax.ShapeDtypeStruct((S, D), F32), jax.ShapeDtypeStruct((S, D), BF16),
                   jax.ShapeDtypeStruct((S, AW + W), BF16)],
        compiler_params=_cparams("parallel"),
    )(attn, rec, proj_h, x, g_attn, g_hgrn, g_norm2, w_out)


def _dmix_post_bwd(dh1b, w_out, attn, rec, proj_h, g_attn, g_hgrn, *, name, tm=512):
    S, D = dh1b.shape
    AW, W = ATTN_W, HGRN_W

    def body(dh_ref, w_ref, a_ref, r_ref, hg_ref, ga_ref, gh_ref, do_ref, dl_ref, dr_ref, dhg_ref, dga_ref, dgh_ref):
        first = pl.program_id(0) == 0
        dmix = _nt(dh_ref[...], w_ref[...])
        av = a_ref[...]
        dov, dga = _norm_bwd(av, ga_ref[...], dmix[:, :AW])
        do_ref[...] = dov
        shift = HEAD_DIM.bit_length() - 1
        hi = lax.shift_right_logical(lax.broadcasted_iota(jnp.int32, (AW, AW), 0), shift)
        hj = lax.shift_right_logical(lax.broadcasted_iota(jnp.int32, (AW, AW), 1), shift)
        prod = dov * av
        hi_part = prod.astype(BF16)
        lo_part = (prod - hi_part.astype(F32)).astype(BF16)
        same_head = jnp.where(hi == hj, 1.0, 0.0).astype(BF16)
        dl_ref[...] = _nn(hi_part, same_head) + _nn(lo_part, same_head)
        _accumulate(dga_ref, jnp.sum(dga, axis=0, keepdims=True), first)

        @pl.when(first)
        def _():
            dgh_ref[...] = jnp.zeros_like(dgh_ref)

        for h in range(HGRN_HEADS):
            sl = slice(HGRN_DIM * h, HGRN_DIM * (h + 1))
            rv, hg, gv = r_ref[:, sl], hg_ref[:, sl].astype(F32), gh_ref[:, sl]
            dout = dmix[:, AW + HGRN_DIM * h:AW + HGRN_DIM * (h + 1)]
            sg = _sigmoid(hg)
            drv, dgh = _norm_bwd(rv, gv, dout * (hg * sg))
            dr_ref[:, sl] = drv
            dgh_ref[:, sl] += jnp.sum(dgh, axis=0, keepdims=True)
            dhg_ref[:, sl] = (dout * (rv * _rstd(rv) * gv) * (sg * (1.0 + hg * (1.0 - sg)))).astype(BF16)

    row = lambda w, j=0: pl.BlockSpec((tm, w), lambda i: (i, j))
    vec = lambda w: pl.BlockSpec((1, w), lambda i: (0, 0))
    return pl.pallas_call(
        body, name=name, grid=(S // tm,),
        in_specs=[row(D), _resident(w_out.shape), row(AW), row(W), row(W, 3), vec(AW), vec(W)],
        out_specs=[row(AW), row(AW), row(W), row(W), vec(AW), vec(W)],
        out_shape=[jax.ShapeDtypeStruct((S, AW), F32), jax.ShapeDtypeStruct((S, AW), F32),
                   jax.ShapeDtypeStruct((S, W), F32), jax.ShapeDtypeStruct((S, W), BF16),
                   jax.ShapeDtypeStruct((1, AW), F32), jax.ShapeDtypeStruct((1, W), F32)],
        compiler_params=_cparams("arbitrary"),
    )(dh1b, w_out, attn, rec, proj_h, g_attn, g_hgrn)


def _conv_act(g, g1, g2, w_ref, b_ref):
    c = b_ref[...] + w_ref[0:1, :] * g2 + w_ref[1:2, :] * g1 + w_ref[2:3, :] * g
    return c, 0.5 * (1.0 + lax.erf(c * (2.0 ** -0.5)))


def _shift_down(g, halo, row):
    g1 = jnp.where(row == 0, halo[7:8], pltpu.roll(g, 1, 0))
    g2 = jnp.where(row == 0, halo[6:7], jnp.where(row == 1, halo[7:8], pltpu.roll(g, 2, 0)))
    return g1, g2


def _shift_up(x, halo, row):
    n = x.shape[0]
    x1 = jnp.where(row == n - 1, halo[0:1], pltpu.roll(x, n - 1, 0))
    x2 = jnp.where(row == n - 2, halo[0:1], jnp.where(row == n - 1, halo[1:2], pltpu.roll(x, n - 2, 0)))
    return x1, x2


def _up_glu(u, wt_up, conv_w, conv_b, *, name, tm=1024, tn=256):
    S, D = u.shape
    F = wt_up.shape[0] // 2
    nf = F // tn

    def body(u_ref, wg_ref, wv_ref, cw_ref, cb_ref, g_ref, v_ref, a_ref, halo_ref):
        i, j = pl.program_id(0), pl.program_id(1)

        @pl.when(i == 0)
        def _():
            halo_ref[j] = jnp.zeros((SUBLANES, tn), F32)

        uv = u_ref[...]
        g, v = _nt(uv, wg_ref[...]), _nt(uv, wv_ref[...])
        row = lax.broadcasted_iota(jnp.int32, (tm, tn), 0)
        g1, g2 = _shift_down(g, halo_ref[j], row)
        c, cdf = _conv_act(g, g1, g2, cw_ref, cb_ref)
        a_ref[...] = (c * cdf * v).astype(BF16)
        g_ref[...] = g.astype(BF16)
        v_ref[...] = v.astype(BF16)
        halo_ref[j] = g[tm - SUBLANES:, :]

    col = pl.BlockSpec((tm, tn), lambda i, j: (i, j))
    out = jax.ShapeDtypeStruct((S, F), BF16)
    return pl.pallas_call(
        body, name=name, grid=(S // tm, nf),
        in_specs=[pl.BlockSpec((tm, D), lambda i, j: (i, 0)), pl.BlockSpec((tn, D), lambda i, j: (j, 0)),
                  pl.BlockSpec((tn, D), lambda i, j: (j + nf, 0)), pl.BlockSpec((3, tn), lambda i, j: (0, j)),
                  pl.BlockSpec((1, tn), lambda i, j: (0, j))],
        out_specs=[col, col, col], out_shape=[out, out, out],
        scratch_shapes=[pltpu.VMEM((nf, SUBLANES, tn), F32)], compiler_params=_cparams("arbitrary", "arbitrary"),
    )(u, wt_up, wt_up, conv_w, conv_b)


def _dact_glu_bwd(dh2b, w_down, gate, val, conv_w, conv_b, *, name, tm=1024, tn=256):
    S, D = dh2b.shape
    F = gate.shape[1]
    nf, ni = F // tn, S // tm
    hb = tm // SUBLANES

    def body(dh_ref, wd_ref, g_ref, gh_ref, v_ref, cw_ref, cb_ref, dg_ref, dv_ref, dcw_ref, dcb_ref, halo_ref, acc_ref):
        i, j = pl.program_id(0), pl.program_id(1)

        @pl.when(i == 0)
        def _():
            halo_ref[j] = jnp.zeros((SUBLANES, tn), F32)
            acc_ref[j] = jnp.zeros((SUBLANES, tn), F32)

        g = g_ref[...].astype(F32)
        before = jnp.where(i < ni - 1, gh_ref[...].astype(F32), 0.0)
        row = lax.broadcasted_iota(jnp.int32, (tm, tn), 0)
        g1, g2 = _shift_down(g, before[SUBLANES:], row)
        c, cdf = _conv_act(g, g1, g2, cw_ref, cb_ref)
        da = _nt(dh_ref[...], wd_ref[...])
        dv_ref[...] = (da * (c * cdf)).astype(BF16)
        pdf = jnp.exp(-0.5 * c * c) * (1.0 / (2.0 * jnp.pi) ** 0.5)
        dc = da * v_ref[...].astype(F32) * (cdf + c * pdf)
        d1, d2 = _shift_up(dc, halo_ref[j], row)
        dg_ref[...] = (cw_ref[2:3, :] * dc + cw_ref[1:2, :] * d1 + cw_ref[0:1, :] * d2).astype(BF16)
        halo_ref[j] = dc[:SUBLANES, :]
        for k, t in enumerate((dc * g2, dc * g1, dc * g, dc)):
            acc_ref[j, k:k + 1, :] += jnp.sum(t, axis=0, keepdims=True)

        @pl.when((i == ni - 1) & (j == nf - 1))
        def _():
            for jj in range(nf):
                dcw_ref[:, jj * tn:(jj + 1) * tn] = acc_ref[jj, 0:3, :]
                dcb_ref[:, jj * tn:(jj + 1) * tn] = acc_ref[jj, 3:4, :]

    tile = pl.BlockSpec((tm, tn), lambda i, j: (ni - 1 - i, j))
    return pl.pallas_call(
        body, name=name, grid=(ni, nf),
        in_specs=[pl.BlockSpec((tm, D), lambda i, j: (ni - 1 - i, 0)), pl.BlockSpec((tn, D), lambda i, j: (j, 0)),
                  tile, pl.BlockSpec((SUBLANES * 2, tn), lambda i, j: (jnp.maximum((ni - 1 - i) * (hb // 2) - 1, 0), j)),
                  tile, pl.BlockSpec((3, tn), lambda i, j: (0, j)), pl.BlockSpec((1, tn), lambda i, j: (0, j))],
        out_specs=[tile, tile, pl.BlockSpec((3, F), lambda i, j: (0, 0)), pl.BlockSpec((1, F), lambda i, j: (0, 0))],
        out_shape=[jax.ShapeDtypeStruct((S, F), BF16), jax.ShapeDtypeStruct((S, F), BF16),
                   jax.ShapeDtypeStruct((3, F), F32), jax.ShapeDtypeStruct((1, F), F32)],
        scratch_shapes=[pltpu.VMEM((nf, SUBLANES, tn), F32), pltpu.VMEM((nf, SUBLANES, tn), F32)],
        compiler_params=_cparams("arbitrary", "arbitrary"),
    )(dh2b, w_down, gate, gate, val, conv_w, conv_b)


def _down_loss(act, w_down, h1, g, target, *, name, tm=512):
    S, F = act.shape
    D = h1.shape[1]

    def body(a_ref, w_ref, h_ref, g_ref, t_ref, dh_ref, dhb_ref, dg_ref, loss_ref):
        first = pl.program_id(0) == 0
        h2 = h_ref[...] + _nn(a_ref[...], w_ref[...])
        gv = g_ref[...]
        r = _rstd(h2)
        xh = h2 * r
        err = xh * gv - t_ref[...]
        part_loss = 0.5 * jnp.sum(jnp.mean(err * err, axis=-1, keepdims=True), axis=0, keepdims=True)
        dy = err * (1.0 / D)
        dxh = dy * gv
        dh = r * (dxh - xh * jnp.mean(dxh * xh, axis=-1, keepdims=True))
        dh_ref[...] = dh
        dhb_ref[...] = dh.astype(BF16)
        _accumulate(dg_ref, jnp.sum(dy * xh, axis=0, keepdims=True), first)
        _accumulate(loss_ref, jnp.broadcast_to(part_loss, (1, LANES)), first)

    row = lambda w: pl.BlockSpec((tm, w), lambda i: (i, 0))
    vec = lambda w: pl.BlockSpec((1, w), lambda i: (0, 0))
    return pl.pallas_call(
        body, name=name, grid=(S // tm,), in_specs=[row(F), _resident(w_down.shape), row(D), vec(D), row(D)],
        out_specs=[row(D), row(D), vec(D), vec(LANES)],
        out_shape=[jax.ShapeDtypeStruct((S, D), F32), jax.ShapeDtypeStruct((S, D), BF16),
                   jax.ShapeDtypeStruct((1, D), F32), jax.ShapeDtypeStruct((1, LANES), F32)],
        compiler_params=_cparams("arbitrary"),
    )(act, w_down, h1, g, target)


def _grad_norm_input(pieces, ws, x, g, add, *, name, tm=512):
    S, D = x.shape
    widths = [p.shape[1] for p in pieces]
    n, nw = len(pieces), len(ws)
    where, wi, off = [], 0, ws[0][1]
    for wd in widths:
        if off == ws[wi][0].shape[0]:
            wi, off = wi + 1, ws[wi + 1][1]
        where.append((wi, off))
        off += wd
    ws = [w for w, _ in ws]

    def body(*refs):
        p_refs, w_refs = refs[:n], refs[n:n + nw]
        x_ref, g_ref, add_ref, dx_ref, dxb_ref, dg_ref = refs[n + nw:]
        halves = _row_halves(tm)
        du = []
        for rows in halves:
            terms = [_nn(p_refs[k][rows, :], w_refs[wi][off:off + widths[k], :]) for k, (wi, off) in enumerate(where)]
            du.append(sum(terms[1:], terms[0]))
        dg_sum = None
        for rows, duh in zip(halves, du):
            dx, dg = _norm_bwd(x_ref[rows, :], g_ref[...], duh)
            dx = add_ref[rows, :] + dx
            dx_ref[rows, :] = dx
            dxb_ref[rows, :] = dx.astype(BF16)
            part = jnp.sum(dg, axis=0, keepdims=True)
            dg_sum = part if dg_sum is None else dg_sum + part
        _accumulate(dg_ref, dg_sum, pl.program_id(0) == 0)

    row = lambda w_: pl.BlockSpec((tm, w_), lambda i: (i, 0))
    vec = pl.BlockSpec((1, D), lambda i: (0, 0))
    return pl.pallas_call(
        body, name=name, grid=(S // tm,),
        in_specs=[row(wd) for wd in widths] + [_resident(w.shape) for w in ws] + [row(D), vec, row(D)],
        out_specs=[row(D), row(D), vec],
        out_shape=[jax.ShapeDtypeStruct((S, D), F32), jax.ShapeDtypeStruct((S, D), BF16),
                   jax.ShapeDtypeStruct((1, D), F32)],
        compiler_params=_cparams("arbitrary"),
    )(*pieces, *ws, x, g, add)


def _rows(a):
    return a.reshape(-1, a.shape[-1])


def _row_tile(rows, cols, itemsize=4, budget=1 << 20):
    t = rows
    while t % 32 == 0 and t * cols * itemsize > budget:
        t //= 2
    return t


def _sum_cast(arrs, out_dtype, *, name):
    shape = arrs[0].shape
    flat = [_rows(a) for a in arrs]
    R, C = flat[0].shape
    tr = _row_tile(R, C)

    def body(*refs):
        acc = refs[0][...].astype(F32)
        for r in refs[1:-1]:
            acc = acc + r[...].astype(F32)
        refs[-1][...] = acc.astype(out_dtype)

    spec = pl.BlockSpec((tr, C), lambda i: (i, 0))
    return pl.pallas_call(
        body, name=name, grid=(R // tr,), in_specs=[spec] * len(flat), out_specs=spec,
        out_shape=jax.ShapeDtypeStruct((R, C), out_dtype), compiler_params=_cparams("parallel"),
    )(*flat).reshape(shape)


def _adamw(parts, w, m, v, *, name):
    shape = w.shape
    w2, m2, v2 = _rows(w), _rows(m), _rows(v)
    R, C = w2.shape
    parts = [p.reshape(-1, R, C) for p in parts]
    tr = _row_tile(R, C)
    np_ = len(parts)
    c1, c2 = 1.0 - ADAM_B1 ** ADAM_STEP, 1.0 - ADAM_B2 ** ADAM_STEP

    def body(*refs):
        terms = [(r, k) for r in refs[:np_] for k in range(r.shape[0])]
        g = terms[0][0][terms[0][1]].astype(F32)
        for r, k in terms[1:]:
            g = g + r[k].astype(F32)
        w_ref, m_ref, v_ref, g_out, d_out, m_out, v_out = refs[np_:]
        mn = ADAM_B1 * m_ref[...] + (1.0 - ADAM_B1) * g
        vn = ADAM_B2 * v_ref[...] + (1.0 - ADAM_B2) * (g * g)
        g_out[...] = g
        d_out[...] = -ADAM_LR * ((mn / c1) / (jnp.sqrt(vn / c2) + ADAM_EPS) + ADAM_WD * w_ref[...])
        m_out[...] = mn
        v_out[...] = vn

    spec = pl.BlockSpec((tr, C), lambda i: (i, 0))
    out = jax.ShapeDtypeStruct((R, C), F32)
    stacks = [pl.BlockSpec((p.shape[0], tr, C), lambda i: (0, i, 0)) for p in parts]
    res = pl.pallas_call(
        body, name=name, grid=(R // tr,), in_specs=stacks + [spec] * 3, out_specs=[spec] * 4,
        out_shape=[out] * 4, compiler_params=_cparams("parallel"),
    )(*parts, w2, m2, v2)
    return [r.reshape(shape) for r in res]


def _adamw_packed(stack, widths, params, *, name):
    c1, c2 = 1.0 - ADAM_B1 ** ADAM_STEP, 1.0 - ADAM_B2 ** ADAM_STEP
    k = stack.shape[0]
    flat = [None if p is None else [_rows(a) for a in p] for p in params]
    n_in = sum(3 for p in flat if p is not None)

    def body(*refs):
        s_ref, ins, outs = refs[0], list(refs[1:1 + n_in]), list(refs[1 + n_in:])
        off = 0
        for width, p in zip(widths, flat):
            rows = 1 if p is None else p[0].shape[0]
            cols = width // rows
            w_ref, m_ref, v_ref = (None, None, None) if p is None else (ins.pop(0), ins.pop(0), ins.pop(0))
            o_refs = [outs.pop(0) for _ in range(1 if p is None else 4)]
            for r in range(rows):
                seg = slice(off + r * cols, off + (r + 1) * cols)
                g = s_ref[0, :, seg]
                for j in range(1, k):
                    g = g + s_ref[j, :, seg]
                o_refs[0][r:r + 1, :] = g
                if p is not None:
                    row = slice(r, r + 1)
                    mn = ADAM_B1 * m_ref[row, :] + (1.0 - ADAM_B1) * g
                    vn = ADAM_B2 * v_ref[row, :] + (1.0 - ADAM_B2) * (g * g)
                    o_refs[1][row, :] = -ADAM_LR * ((mn / c1) / (jnp.sqrt(vn / c2) + ADAM_EPS) + ADAM_WD * w_ref[row, :])
                    o_refs[2][row, :] = mn
                    o_refs[3][row, :] = vn
            off += width

    operands, out_shape = [stack], []
    for width, p in zip(widths, flat):
        if p is None:
            out_shape.append(jax.ShapeDtypeStruct((1, width), F32))
        else:
            operands += p
            out_shape += [jax.ShapeDtypeStruct(p[0].shape, F32)] * 4
    res = list(pl.pallas_call(body, name=name, out_shape=out_shape)(*operands))
    out = []
    for p, orig in zip(flat, params):
        n = 1 if p is None else 4
        out.append([r if orig is None else r.reshape(orig[0].shape) for r in res[:n]])
        res = res[n:]
    return out


def _coords():
    return lax.axis_index("x"), lax.axis_index("y"), lax.axis_index("c")


def _all_gather(shards, *, name):
    n = len(shards)

    def body(*refs):
        x_refs, out_refs = refs[:n], refs[n:2 * n]
        send_sems, recv_sems, local_sems = refs[2 * n:]
        x, y, c = _coords()
        me, sibling = (x, y, c), (x, y, 1 - c)
        chips = [(1 - x, y), (x, 1 - y), (1 - x, 1 - y)]

        def slot(a, dev):
            return out_refs[a].at[4 * dev[0] + 2 * dev[1] + dev[2]]

        def copy(a, k, block, to, src=None):
            return pltpu.make_async_remote_copy(
                src_ref=slot(a, block) if src is None else src, dst_ref=slot(a, block),
                send_sem=send_sems.at[7 * a + k], recv_sem=recv_sems.at[7 * a + k], device_id=to, device_id_type=MESH)

        mine = [pltpu.make_async_copy(x_refs[a], slot(a, me), local_sems.at[a]) for a in range(n)]
        for cp in mine:
            cp.start()
        first = []
        for a in range(n):
            first.append(copy(a, 0, me, sibling, src=x_refs[a]))
            first += [copy(a, 1 + j, me, (*chip, c), src=x_refs[a]) for j, chip in enumerate(chips)]
        for cp in first:
            cp.start()
        passed = []
        for j, chip in enumerate(chips):
            for a in range(n):
                copy(a, 1 + j, (*chip, c), me).wait_recv()
                fwd = copy(a, 4 + j, (*chip, c), sibling)
                fwd.start()
                passed.append(fwd)
        for a in range(n):
            copy(a, 0, sibling, me).wait_recv()
            for j, chip in enumerate(chips):
                copy(a, 4 + j, (*chip, 1 - c), me).wait_recv()
        for cp in first + passed:
            cp.wait_send()
        for cp in mine:
            cp.wait()

    return pl.pallas_call(
        body, name=name, in_specs=[HBM] * n, out_specs=[HBM] * n,
        out_shape=[jax.ShapeDtypeStruct((N_DEV, *s.shape), s.dtype) for s in shards],
        scratch_shapes=[pltpu.SemaphoreType.DMA((7 * n,)), pltpu.SemaphoreType.DMA((7 * n,)),
                        pltpu.SemaphoreType.DMA((n,))],
    )(*shards)


def _flip_y(x, y, c):
    return (x, 1 - y, c)


def _flip_x(x, y, c):
    return (1 - x, y, c)


def _flip_xy(x, y, c):
    return (1 - x, 1 - y, c)


SEM = pl.BlockSpec(memory_space=pltpu.SEMAPHORE)
SIDE_EFFECT = pltpu.SideEffectType.DATAFLOW_SIDE_EFFECTING


def _in_hbm(a):
    return pltpu.with_memory_space_constraint(a, pltpu.HBM)


def _copies_start(srcs, lands, plan, n_copies, *, name, after=None):
    ns, nl = len(srcs), len(lands)
    extra = [] if after is None else [after]

    def body(*refs):
        src_refs, land_refs = refs[:ns], refs[ns:ns + nl]
        send_sems, recv_sems = refs[ns + nl + len(extra):ns + nl + len(extra) + 2]
        token = refs[-1]
        for k, (src, dst, peer, _) in enumerate(plan(src_refs, land_refs, *_coords())):
            pltpu.make_async_remote_copy(src_ref=src, dst_ref=dst, send_sem=send_sems.at[k], recv_sem=recv_sems.at[k],
                                         device_id=peer, device_id_type=MESH).start()
        token[...] = jnp.zeros_like(token)

    bufs = [*srcs, *lands]
    res = pl.pallas_call(
        body, name=name, in_specs=[HBM] * (ns + nl) + [pl.BlockSpec(memory_space=pl.ANY)] * len(extra),
        out_specs=(SEM, SEM, *[HBM] * (ns + nl), pl.BlockSpec(memory_space=pltpu.VMEM)),
        out_shape=(pltpu.SemaphoreType.DMA((n_copies,)), pltpu.SemaphoreType.DMA((n_copies,)),
                   *[pltpu.HBM(b.shape, b.dtype) for b in bufs], jax.ShapeDtypeStruct((SUBLANES, LANES), F32)),
        input_output_aliases={i: 2 + i for i in range(ns + nl)},
        compiler_params=pltpu.CompilerParams(has_side_effects=SIDE_EFFECT),
    )(*[_in_hbm(b) for b in bufs], *extra)
    return res[0], res[1], list(res[2:2 + ns]), list(res[2 + ns:2 + ns + nl]), res[-1]


def _copies_wait(started, plan, after, *, name):
    send_sems, recv_sems, srcs, lands, _ = started
    ns, nl = len(srcs), len(lands)

    def body(*refs):
        src_refs, land_refs = refs[:ns], refs[ns:ns + nl]
        send_sems, recv_sems = refs[ns + nl:ns + nl + 2]
        for k, (src, dst, peer, here) in enumerate(plan(src_refs, land_refs, *_coords())):
            pltpu.make_async_remote_copy(src_ref=src, dst_ref=dst, send_sem=send_sems.at[k], recv_sem=recv_sems.at[k],
                                         device_id=peer, device_id_type=MESH).wait_send()
            pltpu.make_async_remote_copy(src_ref=src, dst_ref=here, send_sem=send_sems.at[k], recv_sem=recv_sems.at[k],
                                         device_id=peer, device_id_type=MESH).wait_recv()

    bufs = [*srcs, *lands]
    res = pl.pallas_call(
        body, name=name, in_specs=[HBM] * (ns + nl) + [SEM, SEM, pl.BlockSpec(memory_space=pl.ANY)],
        out_specs=[HBM] * (ns + nl), out_shape=[pltpu.HBM(b.shape, b.dtype) for b in bufs],
        input_output_aliases={i: i for i in range(ns + nl)},
        compiler_params=pltpu.CompilerParams(has_side_effects=SIDE_EFFECT),
    )(*bufs, send_sems, recv_sems, after)
    return list(res[ns:])


def _dev_index(dev):
    return 4 * dev[0] + 2 * dev[1] + dev[2]


def _ag_chips_plan(src_refs, land_refs, x, y, c):
    me = _dev_index((x, y, c))
    return [(src, land.at[me], peer, land.at[_dev_index(peer)])
            for src, land in zip(src_refs, land_refs) for peer in (_flip_y(x, y, c), _flip_x(x, y, c), _flip_xy(x, y, c))]


def _ag_sibling_plan(src_refs, land_refs, x, y, c):
    chips = [(x, y), (x, 1 - y), (1 - x, y), (1 - x, 1 - y)]
    return [(land.at[_dev_index((*chip, c))], land.at[_dev_index((*chip, c))], (x, y, 1 - c),
             land.at[_dev_index((*chip, 1 - c))]) for land in land_refs for chip in chips]


def _ag_direct_plan(src_refs, land_refs, x, y, c):
    me = _dev_index((x, y, c))
    plan = []
    for src, land in zip(src_refs, land_refs):
        for m in range(1, N_DEV):
            peer = (x + (m >> 2) * (1 - 2 * x), y + ((m >> 1) & 1) * (1 - 2 * y), c + (m & 1) * (1 - 2 * c))
            plan.append((src, land.at[me], peer, land.at[_dev_index(peer)]))
    return plan


def _rs_direct_plan(src_refs, land_refs, x, y, c):
    plan = []
    for src, land in zip(src_refs, land_refs):
        for m in range(1, N_DEV):
            peer = (x + (m >> 2) * (1 - 2 * x), y + ((m >> 1) & 1) * (1 - 2 * y), c + (m & 1) * (1 - 2 * c))
            plan.append((src.at[_dev_index(peer)], land.at[m - 1], peer, land.at[m - 1]))
    return plan


def _rs_start(grads, me, *, name, after=None):
    own = [lax.dynamic_index_in_dim(g, me, 0, keepdims=False) for g in grads]
    lands = [lax.empty((N_DEV - 1, *g.shape[1:]), g.dtype) for g in grads]
    return _copies_start(grads, lands, _rs_direct_plan, (N_DEV - 1) * len(grads), name=name, after=after), own


def _rs_finish(started, after, *, name):
    handle, own = started
    got = _copies_wait(handle, _rs_direct_plan, after, name=name)
    return [[o, land] for o, land in zip(own, got)]


def _gathered_cols(w8):
    return w8.transpose(1, 0, 2).reshape(w8.shape[1], -1)


def _pair_major(wt):
    return wt.reshape(3, ATTN_W // LANES, LANES, -1).transpose(1, 0, 2, 3).reshape(3 * ATTN_W, -1)


def kernel(x, norm1_g, w_in, attn_norm_g, hgrn_norm_g, hgrn_lb_logits, w_out, norm2_g, w_up, conv_w, conv_b, w_down, final_norm_g, loss_target, m_norm1_g, m_w_in, m_attn_norm_g, m_hgrn_norm_g, m_hgrn_lb_logits, m_w_out, m_norm2_g, m_w_up, m_conv_w, m_conv_b, m_w_down, m_final_norm_g, v_norm1_g, v_w_in, v_attn_norm_g, v_hgrn_norm_g, v_hgrn_lb_logits, v_w_out, v_norm2_g, v_w_up, v_conv_w, v_conv_b, v_w_down, v_final_norm_g):
    xs, target = x[0], loss_target[0]
    S, D = xs.shape
    NA = 3 * ATTN_W
    fng = final_norm_g.reshape(1, D)

    t = lambda a: a[0].T
    casts = [_sum_cast([w], BF16, name=f"cast_{nm}") for nm, w in
             (("w_in", t(w_in)), ("w_out", w_out[0]), ("w_up", t(w_up)), ("w_down", w_down[0]))]
    me = _dev_index(_coords())
    (g_in,) = _all_gather(casts[:1], name="ag_w_in")
    later = casts[1:] + [conv_w[0]]
    ag1 = _copies_start(later, [lax.empty((N_DEV, *s.shape), s.dtype) for s in later], _ag_chips_plan,
                        3 * len(later), name="ag_chips_start", after=g_in)
    wi = g_in.reshape(-1, D)
    wi_a = _pair_major(wi[:NA])

    u1, proj_a, proj_h, proj_f = _in_proj(xs, norm1_g + ag1[4][0, 0], wi_a, wi, NA, name="in_proj")
    attn, lse = _attn_fwd(proj_a, name="attn_fwd")
    lands = _copies_wait(ag1, _ag_chips_plan, attn, name="ag_chips_wait")
    lands = [lax.dynamic_update_index_in_dim(l, s, me, 0) for l, s in zip(lands, later)]
    ag2 = _copies_start([], lands, _ag_sibling_plan, 4 * len(later), name="ag_sibling_start")
    rec, states = _hgrn_fwd(proj_h, proj_f, hgrn_lb_logits + ag2[4][0, 0], name="hgrn_fwd")
    g_out, g_up, g_down, g_cw = _copies_wait(ag2, _ag_sibling_plan, rec, name="ag_sibling_wait")
    wo = g_out.reshape(-1, D)
    wu = g_up.reshape(-1, D)
    wd = g_down.reshape(-1, D)
    cw = _gathered_cols(g_cw)
    h1, u2, mixed = _out_proj(attn, rec, proj_h, xs, attn_norm_g, hgrn_norm_g, norm2_g, wo, name="out_proj")
    gate, val, act = _up_glu(u2, wu, cw, conv_b, name="up_glu")
    dh2, dh2b, d_fng, loss_part = _down_loss(act, wd, h1, fng, target, name="down_loss")

    dgate, dval, d_cw, d_cb = _dact_glu_bwd(dh2b, wd, gate, val, cw, conv_b, name="dact_glu_bwd")
    dw_down = _mm_tn(act, dh2b, tm=256, name="dw_down")
    dh1, dh1b, d_n2g = _grad_norm_input([dgate, dval], [(wu, 0)], h1, norm2_g, dh2, name="du2_norm2_bwd")
    F = dgate.shape[1]
    dw_up = _mm_tn(dgate, u2, tm=256, rows=2 * F, name="dw_up_gate")
    dw_up = _mm_tn(dval, u2, tm=256, rows=2 * F, row_block=lambda i: i + F // 256, into=dw_up, name="dw_up_val")
    rs_ffn = _rs_start([dw_down.reshape(N_DEV, -1, D), dw_up.reshape(N_DEV, -1, D)], me, name="rs_ffn_start")
    dattn, delta, drec, dhg, d_ang, d_hng = _dmix_post_bwd(dh1b, wo, attn, rec, proj_h, attn_norm_g + rs_ffn[0][4][0, 0],
                                                          hgrn_norm_g, name="dmix_post_bwd")
    dw_out = _mm_tn(mixed, dh1b, name="dw_out")
    rs_out = _rs_start([dw_out.reshape(N_DEV, -1, D)], me, name="rs_out_start")
    dproj_h, d_lbl = _hgrn_bwd(proj_h, proj_f, hgrn_lb_logits + rs_out[0][4][0, 0], states, drec, name="hgrn_bwd")
    small = [("loss", loss_part, None, None, None),
             ("attn_norm_g", d_ang, attn_norm_g, m_attn_norm_g, v_attn_norm_g),
             ("hgrn_norm_g", d_hng, hgrn_norm_g, m_hgrn_norm_g, v_hgrn_norm_g),
             ("hgrn_lb_logits", d_lbl, hgrn_lb_logits, m_hgrn_lb_logits, v_hgrn_lb_logits),
             ("norm2_g", d_n2g, norm2_g, m_norm2_g, v_norm2_g),
             ("conv_b", d_cb, conv_b, m_conv_b, v_conv_b),
             ("final_norm_g", d_fng, final_norm_g, m_final_norm_g, v_final_norm_g)]
    pack = lambda arrs: jnp.concatenate([a.reshape(1, -1) for a in arrs], axis=1)
    small_own = [pack([s[1] for s in small]), d_cw]
    ag_small = _copies_start(small_own, [lax.empty((N_DEV, *s.shape), s.dtype) for s in small_own], _ag_direct_plan,
                             (N_DEV - 1) * len(small_own), name="ag_small_start")
    dproj_a = _attn_bwd(proj_a, dattn, lse, delta, name="attn_bwd")
    pairs = ATTN_W // LANES
    dw_in = _mm_tn(dproj_a, u1, tm=LANES, rows=wi.shape[0], row_block=lambda i: pairs * (i % 3) + i // 3,
                   name="dw_in_attn")
    dw_in = _mm_tn(dproj_h, u1, tm=256, rows=wi.shape[0], row_block=lambda i: i + NA // 256, into=dw_in,
                   name="dw_in_hgrn")
    dw_in = _mm_tn(dhg, u1, tm=256, rows=wi.shape[0], row_block=lambda i: i + (NA + 3 * HGRN_W) // 256, into=dw_in,
                   name="dw_in_gate")
    rs_in = _rs_start([dw_in.reshape(N_DEV, -1, D)], me, name="rs_in_start", after=ag_small[4])
    grad_x, _, d_n1g = _grad_norm_input([dproj_a, dproj_h, dhg], [(wi_a, 0), (wi, NA)], xs,
                                        norm1_g + rs_in[0][4][0, 0], dh1, name="du1_norm1_bwd")

    res = {}

    def update(nm, parts, w, m, v, transposed=False):
        if transposed:
            res[nm] = [r.T[None] for r in _adamw(parts, t(w), t(m), t(v), name=f"adamw_{nm}")]
        else:
            res[nm] = _adamw(parts, w, m, v, name=f"adamw_{nm}")

    g_down, g_up = _rs_finish(rs_ffn, grad_x, name="rs_ffn_wait")
    update("w_down", g_down, w_down, m_w_down, v_w_down)
    update("w_up", g_up, w_up, m_w_up, v_w_up, transposed=True)
    (g_out,) = _rs_finish(rs_out, grad_x, name="rs_out_wait")
    update("w_out", g_out, w_out, m_w_out, v_w_out)
    (g_in,) = _rs_finish(rs_in, res["w_up"][1], name="rs_in_wait")
    update("w_in", g_in, w_in, m_w_in, v_w_in, transposed=True)

    g_small, g_dcw = [lax.dynamic_update_index_in_dim(l, s, me, 0)
                      for l, s in zip(_copies_wait(ag_small, _ag_direct_plan, grad_x, name="ag_small_wait"), small_own)]
    sm = _adamw_packed(g_small, [s[1].size for s in small], [None if s[2] is None else s[2:] for s in small],
                       name="adamw_small")
    for (nm, *_), r in zip(small, sm):
        res[nm] = r
    ncw = conv_w.shape[-1]
    mine_cw = lax.dynamic_slice_in_dim(g_dcw, me * ncw, ncw, axis=2)
    res["conv_w"] = _adamw([mine_cw], conv_w, m_conv_w, v_conv_w, name="adamw_conv_w")
    late, _ = lax.optimization_barrier((d_n1g, res["w_in"][1]))
    update("norm1_g", _all_gather([late], name="ag_norm1_grad"), norm1_g, m_norm1_g, v_norm1_g)

    loss = res["loss"][0][0, 0]
    order = ["norm1_g", "w_in", "attn_norm_g", "hgrn_norm_g", "hgrn_lb_logits", "w_out", "norm2_g", "w_up",
             "conv_w", "conv_b", "w_down", "final_norm_g"]
    return (loss, grad_x[None], *[res[nm][0] for nm in order], *[res[nm][1] for nm in order],
            *[res[nm][2] for nm in order], *[res[nm][3] for nm in order])
```

```python
import jax
import jax.numpy as jnp
from jax import lax
from jax.experimental import pallas as pl
from jax.experimental.pallas import tpu as pltpu

F32, BF16 = jnp.float32, jnp.bfloat16
NORM_EPS = 1e-6
ATTN_HEADS, HEAD_DIM, ATTN_BLOCK = 8, 64, 128
DILATIONS = (1, 4, 16)
ATTN_SCALE = HEAD_DIM ** -0.5
ATTN_W = ATTN_HEADS * HEAD_DIM
HGRN_HEADS, HGRN_DIM, HGRN_CHUNK = 4, 128, 64
HGRN_W = HGRN_HEADS * HGRN_DIM
ADAM_LR, ADAM_B1, ADAM_B2, ADAM_EPS, ADAM_WD, ADAM_STEP = 0.001, 0.9, 0.999, 1e-08, 0.01, 10
LANES, SUBLANES = 128, 8
VMEM_LIMIT_BYTES = 56 * 1024 * 1024
N_DEV = 8
MESH = pl.DeviceIdType.MESH
HBM = pl.BlockSpec(memory_space=pltpu.HBM)
HIGHEST = lax.Precision.HIGHEST


def _cparams(*sem):
    return pltpu.CompilerParams(dimension_semantics=sem, vmem_limit_bytes=VMEM_LIMIT_BYTES)


def _tile(n, pref):
    if n <= pref:
        return n
    t = (pref // LANES) * LANES
    while n % t:
        t -= LANES
    return t


def _resident(shape):
    return pl.BlockSpec(shape, lambda *_: (0,) * len(shape), pipeline_mode=pl.Buffered(1))


def _dot(a, b, dims, precision=None):
    return lax.dot_general(a, b, (dims, ((), ())), precision=precision, preferred_element_type=F32)


def _nn(a, b, precision=None):
    return _dot(a, b, ((1,), (0,)), precision)


def _nt(a, b):
    return _dot(a, b, ((1,), (1,)))


def _tn(a, b):
    return _dot(a, b, ((0,), (0,)))


def _sigmoid(x):
    return 1.0 / (1.0 + jnp.exp(-x))


def _rstd(x):
    return lax.rsqrt(jnp.mean(x * x, axis=-1, keepdims=True) + NORM_EPS)


def _norm_bwd(x, g, du):
    r = _rstd(x)
    xh = x * r
    dxh = du * g
    return r * (dxh - xh * jnp.mean(dxh * xh, axis=-1, keepdims=True)), du * xh


def _row_halves(tm):
    return [pl.ds(0, tm // 2), pl.ds(tm // 2, tm // 2)]


def _accumulate(ref, part, first):
    @pl.when(first)
    def _():
        ref[...] = part

    @pl.when(jnp.logical_not(first))
    def _():
        ref[...] += part


def _mm_tn(x, dy, *, name, tm=512, tn=1024, rows=None, row_block=None, into=None):
    S, M = x.shape
    N = dy.shape[1]
    tm, tn = _tile(M, tm), _tile(N, tn)
    row_block = row_block or (lambda i: i)

    def body(x_ref, dy_ref, *rest):
        o_ref, xt_ref = rest[-2:]

        @pl.when(pl.program_id(1) == 0)
        def _():
            xt_ref[...] = x_ref[...].T

        o_ref[...] = _nn(xt_ref[...], dy_ref[...]).astype(BF16)

    operands = [x, dy] + ([] if into is None else [into])
    return pl.pallas_call(
        body, name=name, grid=(M // tm, N // tn),
        in_specs=[pl.BlockSpec((S, tm), lambda i, j: (0, i)), pl.BlockSpec((S, tn), lambda i, j: (0, j))]
        + ([] if into is None else [pl.BlockSpec(memory_space=pl.ANY)]),
        out_specs=pl.BlockSpec((tm, tn), lambda i, j: (row_block(i), j)),
        out_shape=jax.ShapeDtypeStruct((rows or M, N), BF16),
        input_output_aliases={} if into is None else {2: 0},
        scratch_shapes=[pltpu.VMEM((tm, S), BF16)], compiler_params=_cparams("parallel", "arbitrary"),
    )(*operands)


def _in_proj(x, g, wt_attn, wt, row0, *, name, tm=512):
    S, D = x.shape
    NA, NH, W = wt_attn.shape[0], wt.shape[0] - row0, HGRN_W

    def body(x_ref, g_ref, wa_ref, w_ref, u_ref, a_ref, h_ref, f_ref):
        xv = x_ref[...]
        u = (xv * _rstd(xv) * g_ref[...]).astype(BF16)
        u_ref[...] = u
        a_ref[...] = _nt(u, wa_ref[...]).astype(BF16)
        ph = _nt(u, w_ref[row0:row0 + NH, :])
        h_ref[...] = ph.astype(BF16)
        f_ref[...] = ph[:, W:2 * W]

    row = lambda w: pl.BlockSpec((tm, w), lambda i: (i, 0))
    return pl.pallas_call(
        body, name=name, grid=(S // tm,),
        in_specs=[row(D), pl.BlockSpec((1, D), lambda i: (0, 0)), _resident(wt_attn.shape), _resident(wt.shape)],
        out_specs=[row(D), row(NA), row(NH), row(W)],
        out_shape=[jax.ShapeDtypeStruct((S, D), BF16), jax.ShapeDtypeStruct((S, NA), BF16),
                   jax.ShapeDtypeStruct((S, NH), BF16), jax.ShapeDtypeStruct((S, W), F32)],
        compiler_params=_cparams("parallel"),
    )(x, g, wt_attn, wt)


PAIR_W = 3 * LANES
ATTN_UNROLL_FWD, ATTN_UNROLL_BWD = 8, 4


def _attn_masks(first):
    qi = lax.broadcasted_iota(jnp.int32, (ATTN_BLOCK, 2 * ATTN_BLOCK), 0)
    kj = lax.broadcasted_iota(jnp.int32, (ATTN_BLOCK, 2 * ATTN_BLOCK), 1)
    dist = qi + ATTN_BLOCK - kj
    valid = (dist >= 0) & (dist <= ATTN_BLOCK) & jnp.logical_or(kj >= ATTN_BLOCK, jnp.logical_not(first))
    lane = lax.broadcasted_iota(jnp.int32, (1, LANES), 1)
    return valid, lane


def _for_residue_blocks(S, d, fn):
    span = ATTN_BLOCK * d
    nb = S // span

    def step(n, carry):
        base = pl.multiple_of(n * span, span)
        for r in range(d):
            off = pl.multiple_of((r * nb + n) * ATTN_BLOCK, ATTN_BLOCK)
            fn(lambda ref, r=r: _block_rows(ref, base, r, d),
               lambda ref, val, r=r: _set_block_rows(ref, base, r, d, val), off)
        return carry

    lax.fori_loop(0, nb, step, 0)


def _for_blocks(S, unroll, fn):
    def step(i, carry):
        fn([(pl.multiple_of((i * unroll + u) * ATTN_BLOCK, ATTN_BLOCK), i * unroll + u) for u in range(unroll)])
        return carry

    lax.fori_loop(0, S // ATTN_BLOCK // unroll, step, 0)


def _head_value(x2, lane, e):
    return jnp.sum(jnp.where(lane == HEAD_DIM * e, x2, 0.0), axis=-1, keepdims=True)


def _block_rows(ref, base, r, d):
    if d == 1:
        return ref[pl.ds(base, ATTN_BLOCK), :]
    return ref.at[pl.ds(base, ATTN_BLOCK * d)][pl.ds(r, ATTN_BLOCK, stride=d), :]


def _set_block_rows(ref, base, r, d, val):
    if d == 1:
        ref[pl.ds(base, ATTN_BLOCK), :] = val
    else:
        ref.at[pl.ds(base, ATTN_BLOCK * d)][pl.ds(r, ATTN_BLOCK, stride=d), :] = val


def _order4_to_16(src, dst, pad):
    S = src.shape[0]
    q4, q16 = S // 4, S // 16
    for r in range(4):
        for a in range(4):
            for n in range(q16 // ATTN_BLOCK):
                rows = src.at[pl.ds(r * q4 + 4 * ATTN_BLOCK * n, 4 * ATTN_BLOCK)][pl.ds(a, ATTN_BLOCK, stride=4), :]
                dst[pl.ds(pad + (4 * a + r) * q16 + ATTN_BLOCK * n, ATTN_BLOCK), :] = rows.astype(dst.dtype)


def _order16_to_4(src, pad, dst):
    S = dst.shape[0]
    q4, q16 = S // 4, S // 16
    for r in range(4):
        for a in range(4):
            for n in range(q16 // ATTN_BLOCK):
                rows = src[pl.ds(pad + (4 * a + r) * q16 + ATTN_BLOCK * n, ATTN_BLOCK), :]
                dst.at[pl.ds(r * q4 + 4 * ATTN_BLOCK * n, 4 * ATTN_BLOCK)][pl.ds(a, ATTN_BLOCK, stride=4), :] = rows


def _regroup(S, d, pairs, tmp):
    for src, dst, pad in pairs:
        if d == 16:
            def to_tmp(rows, _, off, src=src):
                tmp[pl.ds(off, ATTN_BLOCK), :] = rows(src)

            _for_residue_blocks(S, 4, to_tmp)
            _order4_to_16(tmp, dst, pad)
    if d != 16:
        def to_dst(rows, _, off):
            for src, dst, pad in pairs:
                dst[pl.ds(pad + off, ATTN_BLOCK), :] = rows(src).astype(dst.dtype)

        _for_residue_blocks(S, d, to_dst)


def _split_pair(p_ref, qs, ks, vs, bk, bv):
    qs[...] = p_ref[:, 0:LANES].astype(F32)
    ks[...] = p_ref[:, LANES:2 * LANES].astype(F32)
    vs[...] = p_ref[:, 2 * LANES:3 * LANES].astype(F32)
    bk[0:ATTN_BLOCK, :] = jnp.zeros((ATTN_BLOCK, LANES), bk.dtype)
    bv[0:ATTN_BLOCK, :] = jnp.zeros((ATTN_BLOCK, LANES), bv.dtype)


def _attn_fwd(proj_a, *, name):
    S = proj_a.shape[0]

    def body(p_ref, o_ref, l_ref, qs, ks, vs, bq, bk, bv, bo, bl, to, tl):
        _split_pair(p_ref, qs, ks, vs, bk, bv)
        for d in DILATIONS:
            nb = S // (ATTN_BLOCK * d)
            _regroup(S, d, ((qs, bq, 0), (ks, bk, ATTN_BLOCK), (vs, bv, ATTN_BLOCK)), to)

            def blocks(group, nb=nb):
                lane = lax.broadcasted_iota(jnp.int32, (1, LANES), 1)
                heads = [(lane >= HEAD_DIM * e) & (lane < HEAD_DIM * (e + 1)) for e in range(LANES // HEAD_DIM)]
                wins = [pl.ds(off, 2 * ATTN_BLOCK) for off, _ in group]
                s = [[_nt(jnp.where(mh, bq[pl.ds(off, ATTN_BLOCK), :], jnp.zeros((ATTN_BLOCK, LANES), BF16)), bk[win, :])
                      for mh in heads] for (off, _), win in zip(group, wins)]
                p, m, l = [], [], []
                for (off, b), su in zip(group, s):
                    valid, _ = _attn_masks(jnp.bitwise_and(b, nb - 1) == 0)
                    sm = [jnp.where(valid, x * ATTN_SCALE, -jnp.inf) for x in su]
                    m.append([jnp.max(x, axis=-1, keepdims=True) for x in sm])
                    p.append([jnp.exp(x - mx) for x, mx in zip(sm, m[-1])])
                    l.append([jnp.sum(x, axis=-1, keepdims=True) for x in p[-1]])
                o = [[_nn(x.astype(BF16), bv[win, :]) for x in pu] for pu, win in zip(p, wins)]
                for (off, _), ou, mu, lu in zip(group, o, m, l):
                    o2 = jnp.zeros((ATTN_BLOCK, LANES), F32)
                    l2 = jnp.zeros((ATTN_BLOCK, LANES), F32)
                    for mh, oe, me_, le in zip(heads, ou, mu, lu):
                        o2 = jnp.where(mh, oe / le, o2)
                        l2 = jnp.where(mh, me_ + jnp.log(le), l2)
                    bo[pl.ds(off, ATTN_BLOCK), :] = o2
                    bl[pl.ds(off, ATTN_BLOCK), :] = l2

            _for_blocks(S, ATTN_UNROLL_FWD, blocks)

            if d == 16:
                _order16_to_4(bo, 0, to)
                _order16_to_4(bl, 0, tl)
            src_o, src_l = (to, tl) if d == 16 else (bo, bl)

            def merge(rows, set_rows, off, d=d, src_o=src_o, src_l=src_l):
                blk = pl.ds(off, ATTN_BLOCK)
                o2, l2 = src_o[blk, :], src_l[blk, :]
                if d != DILATIONS[0]:
                    lo, oo = rows(l_ref), rows(o_ref)
                    ln = jnp.maximum(lo, l2)
                    wa, wb = jnp.exp(lo - ln), jnp.exp(l2 - ln)
                    o2 = (wa * oo + wb * o2) / (wa + wb)
                    l2 = ln + jnp.log(wa + wb)
                set_rows(o_ref, o2)
                set_rows(l_ref, l2)

            _for_residue_blocks(S, min(d, 4), merge)

    slab = pl.BlockSpec((S, LANES), lambda p: (0, p))
    f32_slab, bf16_slab = pltpu.VMEM((S, LANES), F32), pltpu.VMEM((S, LANES), BF16)
    bf16_window = pltpu.VMEM((S + ATTN_BLOCK, LANES), BF16)
    return pl.pallas_call(
        body, name=name, grid=(ATTN_W // LANES,), in_specs=[pl.BlockSpec((S, PAIR_W), lambda p: (0, p))],
        out_specs=[slab, slab],
        out_shape=[jax.ShapeDtypeStruct((S, ATTN_W), F32), jax.ShapeDtypeStruct((S, ATTN_W), F32)],
        scratch_shapes=[f32_slab] * 3 + [bf16_slab, bf16_window, bf16_window] + [f32_slab] * 4,
        compiler_params=_cparams("parallel"),
    )(proj_a)


def _attn_bwd(proj_a, do, lse, delta, *, name):
    S = proj_a.shape[0]

    def body(p_ref, do_ref, lse_ref, dl_ref, o_ref, qs, ks, vs, dqs, dks, dvs, bq, bk, bv, bdo, blse, bdl, bdq, bdk, bdv,
             tmp):
        _split_pair(p_ref, qs, ks, vs, bk, bv)
        bdk[0:ATTN_BLOCK, :] = jnp.zeros((ATTN_BLOCK, LANES), F32)
        bdv[0:ATTN_BLOCK, :] = jnp.zeros((ATTN_BLOCK, LANES), F32)
        for d in DILATIONS:
            nb = S // (ATTN_BLOCK * d)
            _regroup(S, d, ((qs, bq, 0), (ks, bk, ATTN_BLOCK), (vs, bv, ATTN_BLOCK), (do_ref, bdo, 0),
                            (lse_ref, blse, 0), (dl_ref, bdl, 0)), tmp)

            def blocks(group, nb=nb):
                lane = lax.broadcasted_iota(jnp.int32, (1, LANES), 1)
                heads = [(lane >= HEAD_DIM * e) & (lane < HEAD_DIM * (e + 1)) for e in range(LANES // HEAD_DIM)]
                zero = jnp.zeros((ATTN_BLOCK, LANES), BF16)
                chains = [(off, b, e, mh) for off, b in group for e, mh in enumerate(heads)]
                qm = [jnp.where(mh, bq[pl.ds(off, ATTN_BLOCK), :], zero) for off, _, _, mh in chains]
                dom = [jnp.where(mh, bdo[pl.ds(off, ATTN_BLOCK), :], zero) for off, _, _, mh in chains]
                s = [_nt(x, bk[pl.ds(off, 2 * ATTN_BLOCK), :]) for x, (off, _, _, _) in zip(qm, chains)]
                dp = [_nt(x, bv[pl.ds(off, 2 * ATTN_BLOCK), :]) for x, (off, _, _, _) in zip(dom, chains)]
                p, ds = [], []
                for (off, b, e, _), sc, dpc in zip(chains, s, dp):
                    valid, _ = _attn_masks(jnp.bitwise_and(b, nb - 1) == 0)
                    blk = pl.ds(off, ATTN_BLOCK)
                    pc = jnp.where(valid, jnp.exp(sc * ATTN_SCALE - _head_value(blse[blk, :], lane, e)), 0.0)
                    ds.append((pc * (dpc - _head_value(bdl[blk, :], lane, e)) * ATTN_SCALE).astype(BF16))
                    p.append(pc.astype(BF16))
                dq = [_nn(x, bk[pl.ds(off, 2 * ATTN_BLOCK), :]) for x, (off, _, _, _) in zip(ds, chains)]
                dk = [_tn(x, y) for x, y in zip(ds, qm)]
                dv = [_tn(x, y) for x, y in zip(p, dom)]
                nh = len(heads)
                for u, (off, _) in enumerate(group):
                    dq2 = jnp.zeros((ATTN_BLOCK, LANES), F32)
                    for mh, x in zip(heads, dq[nh * u:nh * (u + 1)]):
                        dq2 = jnp.where(mh, x, dq2)
                    bdq[pl.ds(off, ATTN_BLOCK), :] = dq2
                    for acc, grads in ((bdk, dk), (bdv, dv)):
                        win_grad = sum(grads[nh * u + 1:nh * (u + 1)], grads[nh * u])
                        acc[pl.ds(off, ATTN_BLOCK), :] += win_grad[:ATTN_BLOCK]
                        acc[pl.ds(off + ATTN_BLOCK, ATTN_BLOCK), :] = win_grad[ATTN_BLOCK:]

            _for_blocks(S, ATTN_UNROLL_BWD, blocks)

            outs = ((dqs, bdq, 0), (dks, bdk, ATTN_BLOCK), (dvs, bdv, ATTN_BLOCK))
            if d == 16:
                for acc, grad, pad in outs:
                    _order16_to_4(grad, pad, tmp)

                    def add(rows, set_rows, off, acc=acc):
                        set_rows(acc, rows(acc) + tmp[pl.ds(off, ATTN_BLOCK), :])

                    _for_residue_blocks(S, 4, add)
            else:
                def scatter(rows, set_rows, off, d=d):
                    for acc, grad, pad in outs:
                        part = grad[pl.ds(pad + off, ATTN_BLOCK), :]
                        set_rows(acc, part if d == DILATIONS[0] else rows(acc) + part)

                _for_residue_blocks(S, d, scatter)
        o_ref[:, 0:LANES] = dqs[...].astype(BF16)
        o_ref[:, LANES:2 * LANES] = dks[...].astype(BF16)
        o_ref[:, 2 * LANES:3 * LANES] = dvs[...].astype(BF16)

    slab = pl.BlockSpec((S, LANES), lambda p: (0, p))
    pair = pl.BlockSpec((S, PAIR_W), lambda p: (0, p))
    f32_slab, bf16_slab = pltpu.VMEM((S, LANES), F32), pltpu.VMEM((S, LANES), BF16)
    f32_window, bf16_window = pltpu.VMEM((S + ATTN_BLOCK, LANES), F32), pltpu.VMEM((S + ATTN_BLOCK, LANES), BF16)
    return pl.pallas_call(
        body, name=name, grid=(ATTN_W // LANES,), in_specs=[pair, slab, slab, slab], out_specs=pair,
        out_shape=jax.ShapeDtypeStruct(proj_a.shape, BF16),
        scratch_shapes=[f32_slab] * 6 + [bf16_slab, bf16_window, bf16_window, bf16_slab, f32_slab, f32_slab,
                                         f32_slab, f32_window, f32_window, f32_slab],
        compiler_params=_cparams("parallel"),
    )(proj_a, do, lse, delta)


HG_T = 2 * HGRN_CHUNK
HG_GROUPS = 2
HG_STEP = HG_GROUPS * HG_T


def _hgrn_consts():
    row = lax.broadcasted_iota(jnp.int32, (HG_T, HG_T), 0)
    col = lax.broadcasted_iota(jnp.int32, (HG_T, HG_T), 1)
    same = (row >= HGRN_CHUNK) == (col >= HGRN_CHUNK)
    return row, same & (col <= row), same & (col >= row)


def _lower_bound(logits_ref):
    l0, l1 = logits_ref[0:1, :], logits_ref[1:2, :]
    mx = jnp.maximum(l0, l1)
    e0, e1 = jnp.exp(l0 - mx), jnp.exp(l1 - mx)
    return e0 / (e0 + e1)


def _hgrn_chains():
    chains = [(g, h) for g in range(HG_GROUPS) for h in range(HGRN_HEADS)]
    rows = [pl.ds(HG_T * g, HG_T) for g, _ in chains]
    lanes = [slice(HGRN_DIM * h, HGRN_DIM * (h + 1)) for _, h in chains]
    return chains, rows, lanes


def _hgrn_gates(qs, fs, lbs, row, causal):
    C = HGRN_CHUNK
    tri = jnp.where(causal, 1.0, 0.0).astype(F32)
    sgs = [_sigmoid(f) for f in fs]
    forgets = [lb + (1.0 - lb) * sg for lb, sg in zip(lbs, sgs)]
    logfs = [jnp.log(forget) for forget in forgets]
    bs = [_nn(tri, logf, HIGHEST) for logf in logfs]
    out = []
    for q, sg, forget, logf, b in zip(qs, sgs, forgets, logfs, bs):
        key = 1.0 - forget
        bend0 = jnp.sum(logf[:C], axis=0, keepdims=True)
        bend1 = jnp.sum(logf[C:], axis=0, keepdims=True)
        bend = jnp.where(row < C, bend0, bend1)
        eb, emb, eend = jnp.exp(b), jnp.exp(-b), jnp.exp(bend - b)
        sq = _sigmoid(q)
        out.append(dict(sg=sg, forget=forget, key=key, bend0=bend0, bend1=bend1, eb=eb, emb=emb, eend=eend, sq=sq,
                        qd=q * sq * eb, ki=key * emb, ke=key * eend))
    return out


def _hgrn_fwd(proj, proj_f, logits, *, name):
    S = proj.shape[0]
    W, C = HGRN_W, HGRN_CHUNK

    def body(q_ref, f_ref, i_ref, lg_ref, rec_ref, st_ref, s_ref):
        @pl.when(pl.program_id(0) == 0)
        def _():
            s_ref[...] = jnp.zeros_like(s_ref)

        row, causal, _ = _hgrn_consts()
        lb_all = _lower_bound(lg_ref)
        chains, rows, lanes = _hgrn_chains()
        n = range(len(chains))
        gts = _hgrn_gates([q_ref[rows[c], lanes[c]].astype(F32) for c in n], [f_ref[rows[c], lanes[c]] for c in n],
                          [lb_all[:, lanes[c]] for c in n], row, causal)
        qd, ki, ke = ([gt[k].astype(BF16) for gt in gts] for k in ("qd", "ki", "ke"))
        iv = [i_ref[rows[c], lanes[c]].astype(BF16) for c in n]
        a = [_nt(qd[c], ki[c]) for c in n]
        u0 = [_tn(iv[c][:C], ke[c][:C]) for c in n]
        u1 = [_tn(iv[c][C:], ke[c][C:]) for c in n]
        state = [s_ref[h] for h in range(HGRN_HEADS)]
        s0, s1 = [], []
        for c, (g, h) in enumerate(chains):
            s0.append(state[h])
            s1.append(jnp.exp(gts[c]["bend0"]) * s0[c] + u0[c])
            state[h] = jnp.exp(gts[c]["bend1"]) * s1[c] + u1[c]
        o0 = [_nt(qd[c][:C], s0[c].astype(BF16)) for c in n]
        o1 = [_nt(qd[c][C:], s1[c].astype(BF16)) for c in n]
        o = [_nn(jnp.where(causal, a[c], 0.0).astype(BF16), iv[c]) for c in n]
        for c, (g, h) in enumerate(chains):
            st_ref[2 * g, h] = s0[c]
            st_ref[2 * g + 1, h] = s1[c]
            rec_ref[rows[c], lanes[c]] = o[c] + jnp.concatenate([o0[c], o1[c]], axis=0)
        for h in range(HGRN_HEADS):
            s_ref[h] = state[h]

    blk = lambda j: pl.BlockSpec((HG_STEP, W), lambda t: (t, j))
    return pl.pallas_call(
        body, name=name, grid=(S // HG_STEP,),
        in_specs=[blk(0), blk(0), blk(2), pl.BlockSpec((2, W), lambda t: (0, 0))],
        out_specs=[blk(0), pl.BlockSpec((2 * HG_GROUPS, HGRN_HEADS, HGRN_DIM, HGRN_DIM), lambda t: (t, 0, 0, 0))],
        out_shape=[jax.ShapeDtypeStruct((S, W), F32),
                   jax.ShapeDtypeStruct((S // C, HGRN_HEADS, HGRN_DIM, HGRN_DIM), F32)],
        scratch_shapes=[pltpu.VMEM((HGRN_HEADS, HGRN_DIM, HGRN_DIM), F32)],
        compiler_params=_cparams("arbitrary"),
    )(proj, proj_f, proj, logits)


def _hgrn_bwd(proj, proj_f, logits, states, drec, *, name):
    S = proj.shape[0]
    W, C = HGRN_W, HGRN_CHUNK
    nt = S // HG_STEP

    def body(q_ref, f_ref, i_ref, lg_ref, st_ref, do_ref, dp_ref, dlg_ref, ds_ref, dlb_ref):
        t = pl.program_id(0)

        @pl.when(t == 0)
        def _():
            ds_ref[...] = jnp.zeros_like(ds_ref)
            dlb_ref[...] = jnp.zeros_like(dlb_ref)

        row, causal, anti = _hgrn_consts()
        lb_all = _lower_bound(lg_ref)
        chains, rows, lanes = _hgrn_chains()
        n = range(len(chains))
        qs, lbs = [q_ref[rows[c], lanes[c]].astype(F32) for c in n], [lb_all[:, lanes[c]] for c in n]
        gts = _hgrn_gates(qs, [f_ref[rows[c], lanes[c]] for c in n], lbs, row, causal)
        qd, ki, ke = ([gt[k] for gt in gts] for k in ("qd", "ki", "ke"))
        qdb, kib, keb = ([x.astype(BF16) for x in xs] for xs in (qd, ki, ke))
        iv = [i_ref[rows[c], lanes[c]].astype(BF16) for c in n]
        dob = [do_ref[rows[c], lanes[c]].astype(BF16) for c in n]
        s0 = [st_ref[2 * g, h] for g, h in chains]
        s1 = [st_ref[2 * g + 1, h] for g, h in chains]
        dec0, dec1 = [jnp.exp(gt["bend0"]) for gt in gts], [jnp.exp(gt["bend1"]) for gt in gts]
        a = [_nt(qdb[c], kib[c]) for c in n]
        da = [_nt(dob[c], iv[c]) for c in n]
        dqd1 = [_nn(dob[c][C:], s1[c].astype(BF16)) for c in n]
        dqd0 = [_nn(dob[c][:C], s0[c].astype(BF16)) for c in n]
        t1 = [_tn(dob[c][C:], qdb[c][C:]) for c in n]
        t0 = [_tn(dob[c][:C], qdb[c][:C]) for c in n]
        carry = [ds_ref[h] for h in range(HGRN_HEADS)]
        ds1, ds0 = [None] * len(chains), [None] * len(chains)
        for c in reversed(n):
            h = chains[c][1]
            ds1[c] = carry[h]
            ds0[c] = dec1[c] * ds1[c] + t1[c]
            carry[h] = dec0[c] * ds0[c] + t0[c]
        for h in range(HGRN_HEADS):
            ds_ref[h] = carry[h]
        ds1b, ds0b = [x.astype(BF16) for x in ds1], [x.astype(BF16) for x in ds0]
        a = [jnp.where(causal, x, 0.0).astype(BF16) for x in a]
        da = [jnp.where(causal, x, 0.0).astype(BF16) for x in da]
        di1 = [_nt(keb[c][C:], ds1b[c]) for c in n]
        dke1 = [_nn(iv[c][C:], ds1b[c]) for c in n]
        di0 = [_nt(keb[c][:C], ds0b[c]) for c in n]
        dke0 = [_nn(iv[c][:C], ds0b[c]) for c in n]
        dqd_a = [_nn(da[c], kib[c]) for c in n]
        dki = [_tn(da[c], qdb[c]) for c in n]
        di_a = [_tn(a[c], dob[c]) for c in n]
        dqd, dke, db = [], [], []
        for c in n:
            ddec1 = jnp.sum(ds1[c] * s1[c], axis=0, keepdims=True)
            ddec0 = jnp.sum(ds0[c] * s0[c], axis=0, keepdims=True)
            dqd.append(dqd_a[c] + jnp.concatenate([dqd0[c], dqd1[c]], axis=0))
            h = chains[c][1]
            dp_ref[rows[c], 2 * W + HGRN_DIM * h:2 * W + HGRN_DIM * (h + 1)] = (
                di_a[c] + jnp.concatenate([di0[c], di1[c]], axis=0)).astype(BF16)
            dke.append(jnp.concatenate([dke0[c], dke1[c]], axis=0))
            gke = dke[c] * ke[c]
            dbend0 = jnp.sum(gke[:C], axis=0, keepdims=True) + ddec0 * dec0[c]
            dbend1 = jnp.sum(gke[C:], axis=0, keepdims=True) + ddec1 * dec1[c]
            dbc = dqd[c] * qd[c] - dki[c] * ki[c] - gke
            db.append(dbc + jnp.where(row == C - 1, dbend0, 0.0) + jnp.where(row == HG_T - 1, dbend1, 0.0))
        tri = jnp.where(anti, 1.0, 0.0).astype(F32)
        dlogf = [_nn(tri, db[c], HIGHEST) for c in n]
        for c in n:
            gt, lb, q, h = gts[c], lbs[c], qs[c], chains[c][1]
            dforget = dlogf[c] / gt["forget"] - (dki[c] * gt["emb"] + dke[c] * gt["eend"])
            sg, sq = gt["sg"], gt["sq"]
            dp_ref[rows[c], W + HGRN_DIM * h:W + HGRN_DIM * (h + 1)] = (
                dforget * (1.0 - lb) * sg * (1.0 - sg)).astype(BF16)
            dlb_ref[:, lanes[c]] += jnp.sum(dforget * (1.0 - sg), axis=0, keepdims=True)
            dp_ref[rows[c], lanes[c]] = (dqd[c] * gt["eb"] * sq * (1.0 + q * (1.0 - sq))).astype(BF16)

        @pl.when(t == nt - 1)
        def _():
            dl0 = dlb_ref[...] * lb_all * (1.0 - lb_all)
            dlg_ref[0:1, :] = dl0
            dlg_ref[1:2, :] = -dl0

    blk = lambda j: pl.BlockSpec((HG_STEP, W), lambda t: (nt - 1 - t, j))
    full = pl.BlockSpec((2, W), lambda t: (0, 0))
    return pl.pallas_call(
        body, name=name, grid=(nt,),
        in_specs=[blk(0), blk(0), blk(2), full,
                  pl.BlockSpec((2 * HG_GROUPS, HGRN_HEADS, HGRN_DIM, HGRN_DIM), lambda t: (nt - 1 - t, 0, 0, 0)), blk(0)],
        out_specs=[pl.BlockSpec((HG_STEP, 3 * W), lambda t: (nt - 1 - t, 0)), full],
        out_shape=[jax.ShapeDtypeStruct((S, 3 * W), BF16), jax.ShapeDtypeStruct((2, W), F32)],
        scratch_shapes=[pltpu.VMEM((HGRN_HEADS, HGRN_DIM, HGRN_DIM), F32), pltpu.VMEM((1, W), F32)],
        compiler_params=_cparams("arbitrary"),
    )(proj, proj_f, proj, logits, states, drec)


def _out_proj(attn, rec, proj_h, x, g_attn, g_hgrn, g_norm2, w_out, *, name, tm=512):
    S, D = x.shape
    AW, W = ATTN_W, HGRN_W

    def body(a_ref, r_ref, hg_ref, x_ref, ga_ref, gh_ref, g2_ref, w_ref, h_ref, u_ref, m_ref):
        av = a_ref[...]
        m_ref[:, :AW] = (av * _rstd(av) * ga_ref[...]).astype(BF16)
        for h in range(HGRN_HEADS):
            sl = slice(HGRN_DIM * h, HGRN_DIM * (h + 1))
            rv, hg = r_ref[:, sl], hg_ref[:, sl].astype(F32)
            m_ref[:, AW + HGRN_DIM * h:AW + HGRN_DIM * (h + 1)] = (
                (rv * _rstd(rv) * gh_ref[:, sl]) * (hg * _sigmoid(hg))).astype(BF16)
        h1 = x_ref[...] + _nn(m_ref[...], w_ref[...])
        h_ref[...] = h1
        u_ref[...] = (h1 * _rstd(h1) * g2_ref[...]).astype(BF16)

    row = lambda w, j=0: pl.BlockSpec((tm, w), lambda i: (i, j))
    vec = lambda w: pl.BlockSpec((1, w), lambda i: (0, 0))
    return pl.pallas_call(
        body, name=name, grid=(S // tm,),
        in_specs=[row(AW), row(W), row(W, 3), row(D), vec(AW), vec(W), vec(D), _resident(w_out.shape)],
        out_specs=[row(D), row(D), row(AW + W)],
        out_shape=[jax.ShapeDtypeStruct((S, D), F32), jax.ShapeDtypeStruct((S, D), BF16),
                   jax.ShapeDtypeStruct((S, AW + W), BF16)],
        compiler_params=_cparams("parallel"),
    )(attn, rec, proj_h, x, g_attn, g_hgrn, g_norm2, w_out)


def _dmix_post_bwd(dh1b, w_out, attn, rec, proj_h, g_attn, g_hgrn, *, name, tm=512):
    S, D = dh1b.shape
    AW, W = ATTN_W, HGRN_W

    def body(dh_ref, w_ref, a_ref, r_ref, hg_ref, ga_ref, gh_ref, do_ref, dl_ref, dr_ref, dhg_ref, dga_ref, dgh_ref):
        first = pl.program_id(0) == 0
        dmix = _nt(dh_ref[...], w_ref[...])
        av = a_ref[...]
        dov, dga = _norm_bwd(av, ga_ref[...], dmix[:, :AW])
        do_ref[...] = dov
        shift = HEAD_DIM.bit_length() - 1
        hi = lax.shift_right_logical(lax.broadcasted_iota(jnp.int32, (AW, AW), 0), shift)
        hj = lax.shift_right_logical(lax.broadcasted_iota(jnp.int32, (AW, AW), 1), shift)
        prod = dov * av
        hi_part = prod.astype(BF16)
        lo_part = (prod - hi_part.astype(F32)).astype(BF16)
        same_head = jnp.where(hi == hj, 1.0, 0.0).astype(BF16)
        dl_ref[...] = _nn(hi_part, same_head) + _nn(lo_part, same_head)
        _accumulate(dga_ref, jnp.sum(dga, axis=0, keepdims=True), first)

        @pl.when(first)
        def _():
            dgh_ref[...] = jnp.zeros_like(dgh_ref)

        for h in range(HGRN_HEADS):
            sl = slice(HGRN_DIM * h, HGRN_DIM * (h + 1))
            rv, hg, gv = r_ref[:, sl], hg_ref[:, sl].astype(F32), gh_ref[:, sl]
            dout = dmix[:, AW + HGRN_DIM * h:AW + HGRN_DIM * (h + 1)]
            sg = _sigmoid(hg)
            drv, dgh = _norm_bwd(rv, gv, dout * (hg * sg))
            dr_ref[:, sl] = drv
            dgh_ref[:, sl] += jnp.sum(dgh, axis=0, keepdims=True)
            dhg_ref[:, sl] = (dout * (rv * _rstd(rv) * gv) * (sg * (1.0 + hg * (1.0 - sg)))).astype(BF16)

    row = lambda w, j=0: pl.BlockSpec((tm, w), lambda i: (i, j))
    vec = lambda w: pl.BlockSpec((1, w), lambda i: (0, 0))
    return pl.pallas_call(
        body, name=name, grid=(S // tm,),
        in_specs=[row(D), _resident(w_out.shape), row(AW), row(W), row(W, 3), vec(AW), vec(W)],
        out_specs=[row(AW), row(AW), row(W), row(W), vec(AW), vec(W)],
        out_shape=[jax.ShapeDtypeStruct((S, AW), F32), jax.ShapeDtypeStruct((S, AW), F32),
                   jax.ShapeDtypeStruct((S, W), F32), jax.ShapeDtypeStruct((S, W), BF16),
                   jax.ShapeDtypeStruct((1, AW), F32), jax.ShapeDtypeStruct((1, W), F32)],
        compiler_params=_cparams("arbitrary"),
    )(dh1b, w_out, attn, rec, proj_h, g_attn, g_hgrn)


def _conv_act(g, g1, g2, w_ref, b_ref):
    c = b_ref[...] + w_ref[0:1, :] * g2 + w_ref[1:2, :] * g1 + w_ref[2:3, :] * g
    return c, 0.5 * (1.0 + lax.erf(c * (2.0 ** -0.5)))


def _shift_down(g, halo, row):
    g1 = jnp.where(row == 0, halo[7:8], pltpu.roll(g, 1, 0))
    g2 = jnp.where(row == 0, halo[6:7], jnp.where(row == 1, halo[7:8], pltpu.roll(g, 2, 0)))
    return g1, g2


def _shift_up(x, halo, row):
    n = x.shape[0]
    x1 = jnp.where(row == n - 1, halo[0:1], pltpu.roll(x, n - 1, 0))
    x2 = jnp.where(row == n - 2, halo[0:1], jnp.where(row == n - 1, halo[1:2], pltpu.roll(x, n - 2, 0)))
    return x1, x2


def _up_glu(u, wt_up, conv_w, conv_b, *, name, tm=1024, tn=1408):
    S, D = u.shape
    F = wt_up.shape[0] // 2
    tn = _tile(F, tn)
    nf = F // tn

    def body(u_ref, wg_ref, wv_ref, cw_ref, cb_ref, g_ref, v_ref, a_ref, halo_ref):
        i, j = pl.program_id(0), pl.program_id(1)

        @pl.when(i == 0)
        def _():
            halo_ref[j] = jnp.zeros((SUBLANES, tn), F32)

        uv = u_ref[...]
        g, v = _nt(uv, wg_ref[...]), _nt(uv, wv_ref[...])
        row = lax.broadcasted_iota(jnp.int32, (tm, tn), 0)
        g1, g2 = _shift_down(g, halo_ref[j], row)
        c, cdf = _conv_act(g, g1, g2, cw_ref, cb_ref)
        a_ref[...] = (c * cdf * v).astype(BF16)
        g_ref[...] = g.astype(BF16)
        v_ref[...] = v.astype(BF16)
        halo_ref[j] = g[tm - SUBLANES:, :]

    col = pl.BlockSpec((tm, tn), lambda i, j: (i, j))
    out = jax.ShapeDtypeStruct((S, F), BF16)
    return pl.pallas_call(
        body, name=name, grid=(S // tm, nf),
        in_specs=[pl.BlockSpec((tm, D), lambda i, j: (i, 0)), pl.BlockSpec((tn, D), lambda i, j: (j, 0)),
                  pl.BlockSpec((tn, D), lambda i, j: (j + nf, 0)), pl.BlockSpec((3, tn), lambda i, j: (0, j)),
                  pl.BlockSpec((1, tn), lambda i, j: (0, j))],
        out_specs=[col, col, col], out_shape=[out, out, out],
        scratch_shapes=[pltpu.VMEM((nf, SUBLANES, tn), F32)], compiler_params=_cparams("arbitrary", "arbitrary"),
    )(u, wt_up, wt_up, conv_w, conv_b)


def _dact_glu_bwd(dh2b, w_down, gate, val, conv_w, conv_b, *, name, tm=512, tn=1408):
    S, D = dh2b.shape
    F = gate.shape[1]
    tn = _tile(F, tn)
    nf, ni = F // tn, S // tm
    hb = tm // SUBLANES

    def body(dh_ref, wd_ref, g_ref, gh_ref, v_ref, cw_ref, cb_ref, dg_ref, dv_ref, dcw_ref, dcb_ref, halo_ref, acc_ref):
        i, j = pl.program_id(0), pl.program_id(1)

        @pl.when(i == 0)
        def _():
            halo_ref[j] = jnp.zeros((SUBLANES, tn), F32)
            acc_ref[j] = jnp.zeros((SUBLANES, tn), F32)

        g = g_ref[...].astype(F32)
        before = jnp.where(i < ni - 1, gh_ref[...].astype(F32), 0.0)
        row = lax.broadcasted_iota(jnp.int32, (tm, tn), 0)
        g1, g2 = _shift_down(g, before[SUBLANES:], row)
        c, cdf = _conv_act(g, g1, g2, cw_ref, cb_ref)
        da = _nt(dh_ref[...], wd_ref[...])
        dv_ref[...] = (da * (c * cdf)).astype(BF16)
        pdf = jnp.exp(-0.5 * c * c) * (1.0 / (2.0 * jnp.pi) ** 0.5)
        dc = da * v_ref[...].astype(F32) * (cdf + c * pdf)
        d1, d2 = _shift_up(dc, halo_ref[j], row)
        dg_ref[...] = (cw_ref[2:3, :] * dc + cw_ref[1:2, :] * d1 + cw_ref[0:1, :] * d2).astype(BF16)
        halo_ref[j] = dc[:SUBLANES, :]
        for k, t in enumerate((dc * g2, dc * g1, dc * g, dc)):
            acc_ref[j, k:k + 1, :] += jnp.sum(t, axis=0, keepdims=True)

        @pl.when((i == ni - 1) & (j == nf - 1))
        def _():
            for jj in range(nf):
                dcw_ref[:, jj * tn:(jj + 1) * tn] = acc_ref[jj, 0:3, :]
                dcb_ref[:, jj * tn:(jj + 1) * tn] = acc_ref[jj, 3:4, :]

    tile = pl.BlockSpec((tm, tn), lambda i, j: (ni - 1 - i, j))
    return pl.pallas_call(
        body, name=name, grid=(ni, nf),
        in_specs=[pl.BlockSpec((tm, D), lambda i, j: (ni - 1 - i, 0)), pl.BlockSpec((tn, D), lambda i, j: (j, 0)),
                  tile, pl.BlockSpec((SUBLANES * 2, tn), lambda i, j: (jnp.maximum((ni - 1 - i) * (hb // 2) - 1, 0), j)),
                  tile, pl.BlockSpec((3, tn), lambda i, j: (0, j)), pl.BlockSpec((1, tn), lambda i, j: (0, j))],
        out_specs=[tile, tile, pl.BlockSpec((3, F), lambda i, j: (0, 0)), pl.BlockSpec((1, F), lambda i, j: (0, 0))],
        out_shape=[jax.ShapeDtypeStruct((S, F), BF16), jax.ShapeDtypeStruct((S, F), BF16),
                   jax.ShapeDtypeStruct((3, F), F32), jax.ShapeDtypeStruct((1, F), F32)],
        scratch_shapes=[pltpu.VMEM((nf, SUBLANES, tn), F32), pltpu.VMEM((nf, SUBLANES, tn), F32)],
        compiler_params=_cparams("arbitrary", "arbitrary"),
    )(dh2b, w_down, gate, gate, val, conv_w, conv_b)


def _down_loss(act, w_down, h1, g, target, *, name, tm=512):
    S, F = act.shape
    D = h1.shape[1]

    def body(a_ref, w_ref, h_ref, g_ref, t_ref, dh_ref, dhb_ref, dg_ref, loss_ref):
        first = pl.program_id(0) == 0
        h2 = h_ref[...] + _nn(a_ref[...], w_ref[...])
        gv = g_ref[...]
        r = _rstd(h2)
        xh = h2 * r
        err = xh * gv - t_ref[...]
        part_loss = 0.5 * jnp.sum(jnp.mean(err * err, axis=-1, keepdims=True), axis=0, keepdims=True)
        dy = err * (1.0 / D)
        dxh = dy * gv
        dh = r * (dxh - xh * jnp.mean(dxh * xh, axis=-1, keepdims=True))
        dh_ref[...] = dh
        dhb_ref[...] = dh.astype(BF16)
        _accumulate(dg_ref, jnp.sum(dy * xh, axis=0, keepdims=True), first)
        _accumulate(loss_ref, jnp.broadcast_to(part_loss, (1, LANES)), first)

    row = lambda w: pl.BlockSpec((tm, w), lambda i: (i, 0))
    vec = lambda w: pl.BlockSpec((1, w), lambda i: (0, 0))
    return pl.pallas_call(
        body, name=name, grid=(S // tm,), in_specs=[row(F), _resident(w_down.shape), row(D), vec(D), row(D)],
        out_specs=[row(D), row(D), vec(D), vec(LANES)],
        out_shape=[jax.ShapeDtypeStruct((S, D), F32), jax.ShapeDtypeStruct((S, D), BF16),
                   jax.ShapeDtypeStruct((1, D), F32), jax.ShapeDtypeStruct((1, LANES), F32)],
        compiler_params=_cparams("arbitrary"),
    )(act, w_down, h1, g, target)


def _grad_norm_input(pieces, ws, x, g, add, *, name, tm=512):
    S, D = x.shape
    widths = [p.shape[1] for p in pieces]
    n, nw = len(pieces), len(ws)
    where, wi, off = [], 0, ws[0][1]
    for wd in widths:
        if off == ws[wi][0].shape[0]:
            wi, off = wi + 1, ws[wi + 1][1]
        where.append((wi, off))
        off += wd
    ws = [w for w, _ in ws]

    def body(*refs):
        p_refs, w_refs = refs[:n], refs[n:n + nw]
        x_ref, g_ref, add_ref, dx_ref, dxb_ref, dg_ref = refs[n + nw:]
        halves = _row_halves(tm)
        du = []
        for rows in halves:
            terms = [_nn(p_refs[k][rows, :], w_refs[wi][off:off + widths[k], :]) for k, (wi, off) in enumerate(where)]
            du.append(sum(terms[1:], terms[0]))
        dg_sum = None
        for rows, duh in zip(halves, du):
            dx, dg = _norm_bwd(x_ref[rows, :], g_ref[...], duh)
            dx = add_ref[rows, :] + dx
            dx_ref[rows, :] = dx
            dxb_ref[rows, :] = dx.astype(BF16)
            part = jnp.sum(dg, axis=0, keepdims=True)
            dg_sum = part if dg_sum is None else dg_sum + part
        _accumulate(dg_ref, dg_sum, pl.program_id(0) == 0)

    row = lambda w_: pl.BlockSpec((tm, w_), lambda i: (i, 0))
    vec = pl.BlockSpec((1, D), lambda i: (0, 0))
    return pl.pallas_call(
        body, name=name, grid=(S // tm,),
        in_specs=[row(wd) for wd in widths] + [_resident(w.shape) for w in ws] + [row(D), vec, row(D)],
        out_specs=[row(D), row(D), vec],
        out_shape=[jax.ShapeDtypeStruct((S, D), F32), jax.ShapeDtypeStruct((S, D), BF16),
                   jax.ShapeDtypeStruct((1, D), F32)],
        compiler_params=_cparams("arbitrary"),
    )(*pieces, *ws, x, g, add)


def _rows(a):
    return a.reshape(-1, a.shape[-1])


def _row_tile(rows, cols, itemsize=4, budget=1 << 20):
    t = rows
    while t % 32 == 0 and t * cols * itemsize > budget:
        t //= 2
    return t


def _sum_cast(arrs, out_dtype, *, name):
    shape = arrs[0].shape
    flat = [_rows(a) for a in arrs]
    R, C = flat[0].shape
    tr = _row_tile(R, C)

    def body(*refs):
        acc = refs[0][...].astype(F32)
        for r in refs[1:-1]:
            acc = acc + r[...].astype(F32)
        refs[-1][...] = acc.astype(out_dtype)

    spec = pl.BlockSpec((tr, C), lambda i: (i, 0))
    return pl.pallas_call(
        body, name=name, grid=(R // tr,), in_specs=[spec] * len(flat), out_specs=spec,
        out_shape=jax.ShapeDtypeStruct((R, C), out_dtype), compiler_params=_cparams("parallel"),
    )(*flat).reshape(shape)


def _adamw(parts, w, m, v, *, name):
    shape = w.shape
    w2, m2, v2 = _rows(w), _rows(m), _rows(v)
    R, C = w2.shape
    parts = [p.reshape(-1, R, C) for p in parts]
    tr = _row_tile(R, C)
    np_ = len(parts)
    c1, c2 = 1.0 - ADAM_B1 ** ADAM_STEP, 1.0 - ADAM_B2 ** ADAM_STEP

    def body(*refs):
        terms = [(r, k) for r in refs[:np_] for k in range(r.shape[0])]
        g = terms[0][0][terms[0][1]].astype(F32)
        for r, k in terms[1:]:
            g = g + r[k].astype(F32)
        w_ref, m_ref, v_ref, g_out, d_out, m_out, v_out = refs[np_:]
        mn = ADAM_B1 * m_ref[...] + (1.0 - ADAM_B1) * g
        vn = ADAM_B2 * v_ref[...] + (1.0 - ADAM_B2) * (g * g)
        g_out[...] = g
        d_out[...] = -ADAM_LR * ((mn / c1) / (jnp.sqrt(vn / c2) + ADAM_EPS) + ADAM_WD * w_ref[...])
        m_out[...] = mn
        v_out[...] = vn

    spec = pl.BlockSpec((tr, C), lambda i: (i, 0))
    out = jax.ShapeDtypeStruct((R, C), F32)
    stacks = [pl.BlockSpec((p.shape[0], tr, C), lambda i: (0, i, 0)) for p in parts]
    res = pl.pallas_call(
        body, name=name, grid=(R // tr,), in_specs=stacks + [spec] * 3, out_specs=[spec] * 4,
        out_shape=[out] * 4, compiler_params=_cparams("parallel"),
    )(*parts, w2, m2, v2)
    return [r.reshape(shape) for r in res]


def _adamw_packed(stack, widths, params, *, name):
    c1, c2 = 1.0 - ADAM_B1 ** ADAM_STEP, 1.0 - ADAM_B2 ** ADAM_STEP
    k = stack.shape[0]
    flat = [None if p is None else [_rows(a) for a in p] for p in params]
    n_in = sum(3 for p in flat if p is not None)

    def body(*refs):
        s_ref, ins, outs = refs[0], list(refs[1:1 + n_in]), list(refs[1 + n_in:])
        off = 0
        for width, p in zip(widths, flat):
            rows = 1 if p is None else p[0].shape[0]
            cols = width // rows
            w_ref, m_ref, v_ref = (None, None, None) if p is None else (ins.pop(0), ins.pop(0), ins.pop(0))
            o_refs = [outs.pop(0) for _ in range(1 if p is None else 4)]
            for r in range(rows):
                seg = slice(off + r * cols, off + (r + 1) * cols)
                g = s_ref[0, :, seg]
                for j in range(1, k):
                    g = g + s_ref[j, :, seg]
                o_refs[0][r:r + 1, :] = g
                if p is not None:
                    row = slice(r, r + 1)
                    mn = ADAM_B1 * m_ref[row, :] + (1.0 - ADAM_B1) * g
                    vn = ADAM_B2 * v_ref[row, :] + (1.0 - ADAM_B2) * (g * g)
                    o_refs[1][row, :] = -ADAM_LR * ((mn / c1) / (jnp.sqrt(vn / c2) + ADAM_EPS) + ADAM_WD * w_ref[row, :])
                    o_refs[2][row, :] = mn
                    o_refs[3][row, :] = vn
            off += width

    operands, out_shape = [stack], []
    for width, p in zip(widths, flat):
        if p is None:
            out_shape.append(jax.ShapeDtypeStruct((1, width), F32))
        else:
            operands += p
            out_shape += [jax.ShapeDtypeStruct(p[0].shape, F32)] * 4
    res = list(pl.pallas_call(body, name=name, out_shape=out_shape)(*operands))
    out = []
    for p, orig in zip(flat, params):
        n = 1 if p is None else 4
        out.append([r if orig is None else r.reshape(orig[0].shape) for r in res[:n]])
        res = res[n:]
    return out


def _coords():
    return lax.axis_index("x"), lax.axis_index("y"), lax.axis_index("c")


def _all_gather(shards, *, name):
    n = len(shards)

    def body(*refs):
        x_refs, out_refs = refs[:n], refs[n:2 * n]
        send_sems, recv_sems, local_sems = refs[2 * n:]
        x, y, c = _coords()
        me, sibling = (x, y, c), (x, y, 1 - c)
        chips = [(1 - x, y), (x, 1 - y), (1 - x, 1 - y)]

        def slot(a, dev):
            return out_refs[a].at[4 * dev[0] + 2 * dev[1] + dev[2]]

        def copy(a, k, block, to, src=None):
            return pltpu.make_async_remote_copy(
                src_ref=slot(a, block) if src is None else src, dst_ref=slot(a, block),
                send_sem=send_sems.at[7 * a + k], recv_sem=recv_sems.at[7 * a + k], device_id=to, device_id_type=MESH)

        mine = [pltpu.make_async_copy(x_refs[a], slot(a, me), local_sems.at[a]) for a in range(n)]
        for cp in mine:
            cp.start()
        first = []
        for a in range(n):
            first.append(copy(a, 0, me, sibling, src=x_refs[a]))
            first += [copy(a, 1 + j, me, (*chip, c), src=x_refs[a]) for j, chip in enumerate(chips)]
        for cp in first:
            cp.start()
        passed = []
        for j, chip in enumerate(chips):
            for a in range(n):
                copy(a, 1 + j, (*chip, c), me).wait_recv()
                fwd = copy(a, 4 + j, (*chip, c), sibling)
                fwd.start()
                passed.append(fwd)
        for a in range(n):
            copy(a, 0, sibling, me).wait_recv()
            for j, chip in enumerate(chips):
                copy(a, 4 + j, (*chip, 1 - c), me).wait_recv()
        for cp in first + passed:
            cp.wait_send()
        for cp in mine:
            cp.wait()

    return pl.pallas_call(
        body, name=name, in_specs=[HBM] * n, out_specs=[HBM] * n,
        out_shape=[jax.ShapeDtypeStruct((N_DEV, *s.shape), s.dtype) for s in shards],
        scratch_shapes=[pltpu.SemaphoreType.DMA((7 * n,)), pltpu.SemaphoreType.DMA((7 * n,)),
                        pltpu.SemaphoreType.DMA((n,))],
    )(*shards)


def _flip_y(x, y, c):
    return (x, 1 - y, c)


def _flip_x(x, y, c):
    return (1 - x, y, c)


def _flip_xy(x, y, c):
    return (1 - x, 1 - y, c)


SEM = pl.BlockSpec(memory_space=pltpu.SEMAPHORE)
SIDE_EFFECT = pltpu.SideEffectType.DATAFLOW_SIDE_EFFECTING


def _in_hbm(a):
    return pltpu.with_memory_space_constraint(a, pltpu.HBM)


def _copies_start(srcs, lands, plan, n_copies, *, name, after=None):
    ns, nl = len(srcs), len(lands)
    extra = [] if after is None else [after]

    def body(*refs):
        src_refs, land_refs = refs[:ns], refs[ns:ns + nl]
        send_sems, recv_sems = refs[ns + nl + len(extra):ns + nl + len(extra) + 2]
        token = refs[-1]
        for k, (src, dst, peer, _) in enumerate(plan(src_refs, land_refs, *_coords())):
            pltpu.make_async_remote_copy(src_ref=src, dst_ref=dst, send_sem=send_sems.at[k], recv_sem=recv_sems.at[k],
                                         device_id=peer, device_id_type=MESH).start()
        token[...] = jnp.zeros_like(token)

    bufs = [*srcs, *lands]
    res = pl.pallas_call(
        body, name=name, in_specs=[HBM] * (ns + nl) + [pl.BlockSpec(memory_space=pl.ANY)] * len(extra),
        out_specs=(SEM, SEM, *[HBM] * (ns + nl), pl.BlockSpec(memory_space=pltpu.VMEM)),
        out_shape=(pltpu.SemaphoreType.DMA((n_copies,)), pltpu.SemaphoreType.DMA((n_copies,)),
                   *[pltpu.HBM(b.shape, b.dtype) for b in bufs], jax.ShapeDtypeStruct((SUBLANES, LANES), F32)),
        input_output_aliases={i: 2 + i for i in range(ns + nl)},
        compiler_params=pltpu.CompilerParams(has_side_effects=SIDE_EFFECT),
    )(*[_in_hbm(b) for b in bufs], *extra)
    return res[0], res[1], list(res[2:2 + ns]), list(res[2 + ns:2 + ns + nl]), res[-1]


def _copies_wait(started, plan, after, *, name):
    send_sems, recv_sems, srcs, lands, _ = started
    ns, nl = len(srcs), len(lands)

    def body(*refs):
        src_refs, land_refs = refs[:ns], refs[ns:ns + nl]
        send_sems, recv_sems = refs[ns + nl:ns + nl + 2]
        for k, (src, dst, peer, here) in enumerate(plan(src_refs, land_refs, *_coords())):
            pltpu.make_async_remote_copy(src_ref=src, dst_ref=dst, send_sem=send_sems.at[k], recv_sem=recv_sems.at[k],
                                         device_id=peer, device_id_type=MESH).wait_send()
            pltpu.make_async_remote_copy(src_ref=src, dst_ref=here, send_sem=send_sems.at[k], recv_sem=recv_sems.at[k],
                                         device_id=peer, device_id_type=MESH).wait_recv()

    bufs = [*srcs, *lands]
    res = pl.pallas_call(
        body, name=name, in_specs=[HBM] * (ns + nl) + [SEM, SEM, pl.BlockSpec(memory_space=pl.ANY)],
        out_specs=[HBM] * (ns + nl), out_shape=[pltpu.HBM(b.shape, b.dtype) for b in bufs],
        input_output_aliases={i: i for i in range(ns + nl)},
        compiler_params=pltpu.CompilerParams(has_side_effects=SIDE_EFFECT),
    )(*bufs, send_sems, recv_sems, after)
    return list(res[ns:])


def _dev_index(dev):
    return 4 * dev[0] + 2 * dev[1] + dev[2]


def _ag_chips_plan(src_refs, land_refs, x, y, c):
    me = _dev_index((x, y, c))
    return [(src, land.at[me], peer, land.at[_dev_index(peer)])
            for src, land in zip(src_refs, land_refs) for peer in (_flip_y(x, y, c), _flip_x(x, y, c), _flip_xy(x, y, c))]


def _ag_sibling_plan(src_refs, land_refs, x, y, c):
    chips = [(x, y), (x, 1 - y), (1 - x, y), (1 - x, 1 - y)]
    return [(land.at[_dev_index((*chip, c))], land.at[_dev_index((*chip, c))], (x, y, 1 - c),
             land.at[_dev_index((*chip, 1 - c))]) for land in land_refs for chip in chips]


def _ag_direct_plan(src_refs, land_refs, x, y, c):
    me = _dev_index((x, y, c))
    plan = []
    for src, land in zip(src_refs, land_refs):
        for m in range(1, N_DEV):
            peer = (x + (m >> 2) * (1 - 2 * x), y + ((m >> 1) & 1) * (1 - 2 * y), c + (m & 1) * (1 - 2 * c))
            plan.append((src, land.at[me], peer, land.at[_dev_index(peer)]))
    return plan


def _rs_direct_plan(src_refs, land_refs, x, y, c):
    plan = []
    for src, land in zip(src_refs, land_refs):
        for m in range(1, N_DEV):
            peer = (x + (m >> 2) * (1 - 2 * x), y + ((m >> 1) & 1) * (1 - 2 * y), c + (m & 1) * (1 - 2 * c))
            plan.append((src.at[_dev_index(peer)], land.at[m - 1], peer, land.at[m - 1]))
    return plan


def _rs_start(grads, me, *, name, after=None):
    own = [lax.dynamic_index_in_dim(g, me, 0, keepdims=False) for g in grads]
    lands = [lax.empty((N_DEV - 1, *g.shape[1:]), g.dtype) for g in grads]
    return _copies_start(grads, lands, _rs_direct_plan, (N_DEV - 1) * len(grads), name=name, after=after), own


def _rs_finish(started, after, *, name):
    handle, own = started
    got = _copies_wait(handle, _rs_direct_plan, after, name=name)
    return [[o, land] for o, land in zip(own, got)]


def _gathered_cols(w8):
    return w8.transpose(1, 0, 2).reshape(w8.shape[1], -1)


def _pair_major(wt):
    return wt.reshape(3, ATTN_W // LANES, LANES, -1).transpose(1, 0, 2, 3).reshape(3 * ATTN_W, -1)


def kernel(x, norm1_g, w_in, attn_norm_g, hgrn_norm_g, hgrn_lb_logits, w_out, norm2_g, w_up, conv_w, conv_b, w_down, final_norm_g, loss_target, m_norm1_g, m_w_in, m_attn_norm_g, m_hgrn_norm_g, m_hgrn_lb_logits, m_w_out, m_norm2_g, m_w_up, m_conv_w, m_conv_b, m_w_down, m_final_norm_g, v_norm1_g, v_w_in, v_attn_norm_g, v_hgrn_norm_g, v_hgrn_lb_logits, v_w_out, v_norm2_g, v_w_up, v_conv_w, v_conv_b, v_w_down, v_final_norm_g):
    xs, target = x[0], loss_target[0]
    S, D = xs.shape
    NA = 3 * ATTN_W
    fng = final_norm_g.reshape(1, D)

    t = lambda a: a[0].T
    casts = [_sum_cast([w], BF16, name=f"cast_{nm}") for nm, w in
             (("w_in", t(w_in)), ("w_out", w_out[0]), ("w_up", t(w_up)), ("w_down", w_down[0]))]
    me = _dev_index(_coords())
    (g_in,) = _all_gather(casts[:1], name="ag_w_in")
    later = casts[1:] + [conv_w[0]]
    ag1 = _copies_start(later, [lax.empty((N_DEV, *s.shape), s.dtype) for s in later], _ag_chips_plan,
                        3 * len(later), name="ag_chips_start", after=g_in)
    wi = g_in.reshape(-1, D)
    wi_a = _pair_major(wi[:NA])

    u1, proj_a, proj_h, proj_f = _in_proj(xs, norm1_g + ag1[4][0, 0], wi_a, wi, NA, name="in_proj")
    attn, lse = _attn_fwd(proj_a, name="attn_fwd")
    lands = _copies_wait(ag1, _ag_chips_plan, attn, name="ag_chips_wait")
    lands = [lax.dynamic_update_index_in_dim(l, s, me, 0) for l, s in zip(lands, later)]
    ag2 = _copies_start([], lands, _ag_sibling_plan, 4 * len(later), name="ag_sibling_start")
    rec, states = _hgrn_fwd(proj_h, proj_f, hgrn_lb_logits + ag2[4][0, 0], name="hgrn_fwd")
    g_out, g_up, g_down, g_cw = _copies_wait(ag2, _ag_sibling_plan, rec, name="ag_sibling_wait")
    wo = g_out.reshape(-1, D)
    wu = g_up.reshape(-1, D)
    wd = g_down.reshape(-1, D)
    cw = _gathered_cols(g_cw)
    h1, u2, mixed = _out_proj(attn, rec, proj_h, xs, attn_norm_g, hgrn_norm_g, norm2_g, wo, name="out_proj")
    gate, val, act = _up_glu(u2, wu, cw, conv_b, name="up_glu")
    dh2, dh2b, d_fng, loss_part = _down_loss(act, wd, h1, fng, target, name="down_loss")

    dgate, dval, d_cw, d_cb = _dact_glu_bwd(dh2b, wd, gate, val, cw, conv_b, name="dact_glu_bwd")
    dw_down = _mm_tn(act, dh2b, tm=256, name="dw_down")
    dh1, dh1b, d_n2g = _grad_norm_input([dgate, dval], [(wu, 0)], h1, norm2_g, dh2, name="du2_norm2_bwd")
    F = dgate.shape[1]
    dw_up = _mm_tn(dgate, u2, tm=256, rows=2 * F, name="dw_up_gate")
    dw_up = _mm_tn(dval, u2, tm=256, rows=2 * F, row_block=lambda i: i + F // 256, into=dw_up, name="dw_up_val")
    rs_ffn = _rs_start([dw_down.reshape(N_DEV, -1, D), dw_up.reshape(N_DEV, -1, D)], me, name="rs_ffn_start")
    dattn, delta, drec, dhg, d_ang, d_hng = _dmix_post_bwd(dh1b, wo, attn, rec, proj_h, attn_norm_g + rs_ffn[0][4][0, 0],
                                                          hgrn_norm_g, name="dmix_post_bwd")
    dw_out = _mm_tn(mixed, dh1b, name="dw_out")
    rs_out = _rs_start([dw_out.reshape(N_DEV, -1, D)], me, name="rs_out_start")
    dproj_h, d_lbl = _hgrn_bwd(proj_h, proj_f, hgrn_lb_logits + rs_out[0][4][0, 0], states, drec, name="hgrn_bwd")
    small = [("loss", loss_part, None, None, None),
             ("attn_norm_g", d_ang, attn_norm_g, m_attn_norm_g, v_attn_norm_g),
             ("hgrn_norm_g", d_hng, hgrn_norm_g, m_hgrn_norm_g, v_hgrn_norm_g),
             ("hgrn_lb_logits", d_lbl, hgrn_lb_logits, m_hgrn_lb_logits, v_hgrn_lb_logits),
             ("norm2_g", d_n2g, norm2_g, m_norm2_g, v_norm2_g),
             ("conv_b", d_cb, conv_b, m_conv_b, v_conv_b),
             ("final_norm_g", d_fng, final_norm_g, m_final_norm_g, v_final_norm_g)]
    pack = lambda arrs: jnp.concatenate([a.reshape(1, -1) for a in arrs], axis=1)
    small_own = [pack([s[1] for s in small]), d_cw]
    ag_small = _copies_start(small_own, [lax.empty((N_DEV, *s.shape), s.dtype) for s in small_own], _ag_direct_plan,
                             (N_DEV - 1) * len(small_own), name="ag_small_start")
    dproj_a = _attn_bwd(proj_a, dattn, lse, delta, name="attn_bwd")
    pairs = ATTN_W // LANES
    dw_in = _mm_tn(dproj_a, u1, tm=LANES, rows=wi.shape[0], row_block=lambda i: pairs * (i % 3) + i // 3,
                   name="dw_in_attn")
    dw_in = _mm_tn(dproj_h, u1, tm=256, rows=wi.shape[0], row_block=lambda i: i + NA // 256, into=dw_in,
                   name="dw_in_hgrn")
    dw_in = _mm_tn(dhg, u1, tm=256, rows=wi.shape[0], row_block=lambda i: i + (NA + 3 * HGRN_W) // 256, into=dw_in,
                   name="dw_in_gate")
    rs_in = _rs_start([dw_in.reshape(N_DEV, -1, D)], me, name="rs_in_start", after=ag_small[4])
    grad_x, _, d_n1g = _grad_norm_input([dproj_a, dproj_h, dhg], [(wi_a, 0), (wi, NA)], xs,
                                        norm1_g + rs_in[0][4][0, 0], dh1, name="du1_norm1_bwd")

    res = {}

    def update(nm, parts, w, m, v, transposed=False):
        if transposed:
            res[nm] = [r.T[None] for r in _adamw(parts, t(w), t(m), t(v), name=f"adamw_{nm}")]
        else:
            res[nm] = _adamw(parts, w, m, v, name=f"adamw_{nm}")

    g_down, g_up = _rs_finish(rs_ffn, grad_x, name="rs_ffn_wait")
    update("w_down", g_down, w_down, m_w_down, v_w_down)
    update("w_up", g_up, w_up, m_w_up, v_w_up, transposed=True)
    (g_out,) = _rs_finish(rs_out, grad_x, name="rs_out_wait")
    update("w_out", g_out, w_out, m_w_out, v_w_out)
    (g_in,) = _rs_finish(rs_in, res["w_up"][1], name="rs_in_wait")
    update("w_in", g_in, w_in, m_w_in, v_w_in, transposed=True)

    g_small, g_dcw = [lax.dynamic_update_index_in_dim(l, s, me, 0)
                      for l, s in zip(_copies_wait(ag_small, _ag_direct_plan, grad_x, name="ag_small_wait"), small_own)]
    sm = _adamw_packed(g_small, [s[1].size for s in small], [None if s[2] is None else s[2:] for s in small],
                       name="adamw_small")
    for (nm, *_), r in zip(small, sm):
        res[nm] = r
    ncw = conv_w.shape[-1]
    mine_cw = lax.dynamic_slice_in_dim(g_dcw, me * ncw, ncw, axis=2)
    res["conv_w"] = _adamw([mine_cw], conv_w, m_conv_w, v_conv_w, name="adamw_conv_w")
    late, _ = lax.optimization_barrier((d_n1g, res["w_in"][1]))
    update("norm1_g", _all_gather([late], name="ag_norm1_grad"), norm1_g, m_norm1_g, v_norm1_g)

    loss = res["loss"][0][0, 0]
    order = ["norm1_g", "w_in", "attn_norm_g", "hgrn_norm_g", "hgrn_lb_logits", "w_out", "norm2_g", "w_up",
             "conv_w", "conv_b", "w_down", "final_norm_g"]
    return (loss, grad_x[None], *[res[nm][0] for nm in order], *[res[nm][1] for nm in order],
            *[res[nm][2] for nm in order], *[res[nm][3] for nm in order])
```

```python
import jax
import jax.numpy as jnp
from jax import lax
from jax.experimental import pallas as pl
from jax.experimental.pallas import tpu as pltpu

F32, BF16 = jnp.float32, jnp.bfloat16
NORM_EPS = 1e-6
ATTN_HEADS, HEAD_DIM, ATTN_BLOCK = 8, 64, 128
DILATIONS = (1, 4, 16)
ATTN_SCALE = HEAD_DIM ** -0.5
ATTN_W = ATTN_HEADS * HEAD_DIM
HGRN_HEADS, HGRN_DIM, HGRN_CHUNK = 4, 128, 64
HGRN_W = HGRN_HEADS * HGRN_DIM
ADAM_LR, ADAM_B1, ADAM_B2, ADAM_EPS, ADAM_WD, ADAM_STEP = 0.001, 0.9, 0.999, 1e-08, 0.01, 10
LANES, SUBLANES = 128, 8
VMEM_LIMIT_BYTES = 56 * 1024 * 1024
N_DEV = 8
MESH = pl.DeviceIdType.MESH
HBM = pl.BlockSpec(memory_space=pltpu.HBM)
HIGHEST = lax.Precision.HIGHEST


def _cparams(*sem):
    return pltpu.CompilerParams(dimension_semantics=sem, vmem_limit_bytes=VMEM_LIMIT_BYTES)


def _tile(n, pref):
    if n <= pref:
        return n
    t = (pref // LANES) * LANES
    while n % t:
        t -= LANES
    return t


def _resident(shape):
    return pl.BlockSpec(shape, lambda *_: (0,) * len(shape), pipeline_mode=pl.Buffered(1))


def _dot(a, b, dims, precision=None):
    return lax.dot_general(a, b, (dims, ((), ())), precision=precision, preferred_element_type=F32)


def _nn(a, b, precision=None):
    return _dot(a, b, ((1,), (0,)), precision)


def _nt(a, b):
    return _dot(a, b, ((1,), (1,)))


def _tn(a, b):
    return _dot(a, b, ((0,), (0,)))


def _sigmoid(x):
    return 1.0 / (1.0 + jnp.exp(-x))


def _rstd(x):
    return lax.rsqrt(jnp.mean(x * x, axis=-1, keepdims=True) + NORM_EPS)


def _norm_bwd(x, g, du):
    r = _rstd(x)
    xh = x * r
    dxh = du * g
    return r * (dxh - xh * jnp.mean(dxh * xh, axis=-1, keepdims=True)), du * xh


def _row_halves(tm):
    return [pl.ds(0, tm // 2), pl.ds(tm // 2, tm // 2)]


def _accumulate(ref, part, first):
    @pl.when(first)
    def _():
        ref[...] = part

    @pl.when(jnp.logical_not(first))
    def _():
        ref[...] += part


def _mm_tn(x, dy, *, name, tm=512, tn=1024, rows=None, row_block=None, into=None):
    S, M = x.shape
    N = dy.shape[1]
    tm, tn = _tile(M, tm), _tile(N, tn)
    row_block = row_block or (lambda i: i)

    def body(x_ref, dy_ref, *rest):
        o_ref, xt_ref = rest[-2:]

        @pl.when(pl.program_id(1) == 0)
        def _():
            xt_ref[...] = x_ref[...].T

        o_ref[...] = _nn(xt_ref[...], dy_ref[...]).astype(BF16)

    operands = [x, dy] + ([] if into is None else [into])
    return pl.pallas_call(
        body, name=name, grid=(M // tm, N // tn),
        in_specs=[pl.BlockSpec((S, tm), lambda i, j: (0, i)), pl.BlockSpec((S, tn), lambda i, j: (0, j))]
        + ([] if into is None else [pl.BlockSpec(memory_space=pl.ANY)]),
        out_specs=pl.BlockSpec((tm, tn), lambda i, j: (row_block(i), j)),
        out_shape=jax.ShapeDtypeStruct((rows or M, N), BF16),
        input_output_aliases={} if into is None else {2: 0},
        scratch_shapes=[pltpu.VMEM((tm, S), BF16)], compiler_params=_cparams("parallel", "arbitrary"),
    )(*operands)


def _in_proj(x, g, wt_attn, wt, row0, *, name, tm=512):
    S, D = x.shape
    NA, NH, W = wt_attn.shape[0], wt.shape[0] - row0, HGRN_W

    def body(x_ref, g_ref, wa_ref, w_ref, u_ref, a_ref, h_ref, f_ref):
        xv = x_ref[...]
        u = (xv * _rstd(xv) * g_ref[...]).astype(BF16)
        u_ref[...] = u
        a_ref[...] = _nt(u, wa_ref[...]).astype(BF16)
        ph = _nt(u, w_ref[row0:row0 + NH, :])
        h_ref[...] = ph.astype(BF16)
        f_ref[...] = ph[:, W:2 * W]

    row = lambda w: pl.BlockSpec((tm, w), lambda i: (i, 0))
    return pl.pallas_call(
        body, name=name, grid=(S // tm,),
        in_specs=[row(D), pl.BlockSpec((1, D), lambda i: (0, 0)), _resident(wt_attn.shape), _resident(wt.shape)],
        out_specs=[row(D), row(NA), row(NH), row(W)],
        out_shape=[jax.ShapeDtypeStruct((S, D), BF16), jax.ShapeDtypeStruct((S, NA), BF16),
                   jax.ShapeDtypeStruct((S, NH), BF16), jax.ShapeDtypeStruct((S, W), F32)],
        compiler_params=_cparams("parallel"),
    )(x, g, wt_attn, wt)


PAIR_W = 3 * LANES
ATTN_UNROLL_FWD, ATTN_UNROLL_BWD = 8, 4


def _attn_masks(first):
    qi = lax.broadcasted_iota(jnp.int32, (ATTN_BLOCK, 2 * ATTN_BLOCK), 0)
    kj = lax.broadcasted_iota(jnp.int32, (ATTN_BLOCK, 2 * ATTN_BLOCK), 1)
    dist = qi + ATTN_BLOCK - kj
    valid = (dist >= 0) & (dist <= ATTN_BLOCK) & jnp.logical_or(kj >= ATTN_BLOCK, jnp.logical_not(first))
    lane = lax.broadcasted_iota(jnp.int32, (1, LANES), 1)
    return valid, lane


def _for_residue_blocks(S, d, fn):
    span = ATTN_BLOCK * d
    nb = S // span

    def step(n, carry):
        base = pl.multiple_of(n * span, span)
        for r in range(d):
            off = pl.multiple_of((r * nb + n) * ATTN_BLOCK, ATTN_BLOCK)
            fn(lambda ref, r=r: _block_rows(ref, base, r, d),
               lambda ref, val, r=r: _set_block_rows(ref, base, r, d, val), off)
        return carry

    lax.fori_loop(0, nb, step, 0)


def _for_blocks(S, unroll, fn):
    def step(i, carry):
        fn([(pl.multiple_of((i * unroll + u) * ATTN_BLOCK, ATTN_BLOCK), i * unroll + u) for u in range(unroll)])
        return carry

    lax.fori_loop(0, S // ATTN_BLOCK // unroll, step, 0)


def _head_value(x2, lane, e):
    return jnp.sum(jnp.where(lane == HEAD_DIM * e, x2, 0.0), axis=-1, keepdims=True)


def _block_rows(ref, base, r, d):
    if d == 1:
        return ref[pl.ds(base, ATTN_BLOCK), :]
    return ref.at[pl.ds(base, ATTN_BLOCK * d)][pl.ds(r, ATTN_BLOCK, stride=d), :]


def _set_block_rows(ref, base, r, d, val):
    if d == 1:
        ref[pl.ds(base, ATTN_BLOCK), :] = val
    else:
        ref.at[pl.ds(base, ATTN_BLOCK * d)][pl.ds(r, ATTN_BLOCK, stride=d), :] = val


def _order4_to_16(src, dst, pad):
    S = src.shape[0]
    q4, q16 = S // 4, S // 16
    for r in range(4):
        for a in range(4):
            for n in range(q16 // ATTN_BLOCK):
                rows = src.at[pl.ds(r * q4 + 4 * ATTN_BLOCK * n, 4 * ATTN_BLOCK)][pl.ds(a, ATTN_BLOCK, stride=4), :]
                dst[pl.ds(pad + (4 * a + r) * q16 + ATTN_BLOCK * n, ATTN_BLOCK), :] = rows.astype(dst.dtype)


def _order16_to_4(src, pad, dst):
    S = dst.shape[0]
    q4, q16 = S // 4, S // 16
    for r in range(4):
        for a in range(4):
            for n in range(q16 // ATTN_BLOCK):
                rows = src[pl.ds(pad + (4 * a + r) * q16 + ATTN_BLOCK * n, ATTN_BLOCK), :]
                dst.at[pl.ds(r * q4 + 4 * ATTN_BLOCK * n, 4 * ATTN_BLOCK)][pl.ds(a, ATTN_BLOCK, stride=4), :] = rows


def _regroup(S, d, pairs, tmp):
    for src, dst, pad in pairs:
        if d == 16:
            def to_tmp(rows, _, off, src=src):
                tmp[pl.ds(off, ATTN_BLOCK), :] = rows(src)

            _for_residue_blocks(S, 4, to_tmp)
            _order4_to_16(tmp, dst, pad)
    if d != 16:
        def to_dst(rows, _, off):
            for src, dst, pad in pairs:
                dst[pl.ds(pad + off, ATTN_BLOCK), :] = rows(src).astype(dst.dtype)

        _for_residue_blocks(S, d, to_dst)


def _split_pair(p_ref, qs, ks, vs, bk, bv):
    qs[...] = p_ref[:, 0:LANES].astype(F32)
    ks[...] = p_ref[:, LANES:2 * LANES].astype(F32)
    vs[...] = p_ref[:, 2 * LANES:3 * LANES].astype(F32)
    bk[0:ATTN_BLOCK, :] = jnp.zeros((ATTN_BLOCK, LANES), bk.dtype)
    bv[0:ATTN_BLOCK, :] = jnp.zeros((ATTN_BLOCK, LANES), bv.dtype)


def _attn_fwd(proj_a, *, name):
    S = proj_a.shape[0]

    def body(p_ref, o_ref, l_ref, qs, ks, vs, bq, bk, bv, bo, bl, to, tl):
        _split_pair(p_ref, qs, ks, vs, bk, bv)
        for d in DILATIONS:
            nb = S // (ATTN_BLOCK * d)
            _regroup(S, d, ((qs, bq, 0), (ks, bk, ATTN_BLOCK), (vs, bv, ATTN_BLOCK)), to)

            def blocks(group, nb=nb):
                lane = lax.broadcasted_iota(jnp.int32, (1, LANES), 1)
                heads = [(lane >= HEAD_DIM * e) & (lane < HEAD_DIM * (e + 1)) for e in range(LANES // HEAD_DIM)]
                wins = [pl.ds(off, 2 * ATTN_BLOCK) for off, _ in group]
                s = [[_nt(jnp.where(mh, bq[pl.ds(off, ATTN_BLOCK), :], jnp.zeros((ATTN_BLOCK, LANES), BF16)), bk[win, :])
                      for mh in heads] for (off, _), win in zip(group, wins)]
                p, m, l = [], [], []
                for (off, b), su in zip(group, s):
                    valid, _ = _attn_masks(jnp.bitwise_and(b, nb - 1) == 0)
                    sm = [jnp.where(valid, x * ATTN_SCALE, -jnp.inf) for x in su]
                    m.append([jnp.max(x, axis=-1, keepdims=True) for x in sm])
                    p.append([jnp.exp(x - mx) for x, mx in zip(sm, m[-1])])
                    l.append([jnp.sum(x, axis=-1, keepdims=True) for x in p[-1]])
                o = [[_nn(x.astype(BF16), bv[win, :]) for x in pu] for pu, win in zip(p, wins)]
                for (off, _), ou, mu, lu in zip(group, o, m, l):
                    o2 = jnp.zeros((ATTN_BLOCK, LANES), F32)
                    l2 = jnp.zeros((ATTN_BLOCK, LANES), F32)
                    for mh, oe, me_, le in zip(heads, ou, mu, lu):
                        o2 = jnp.where(mh, oe / le, o2)
                        l2 = jnp.where(mh, me_ + jnp.log(le), l2)
                    bo[pl.ds(off, ATTN_BLOCK), :] = o2
                    bl[pl.ds(off, ATTN_BLOCK), :] = l2

            _for_blocks(S, ATTN_UNROLL_FWD, blocks)

            if d == 16:
                _order16_to_4(bo, 0, to)
                _order16_to_4(bl, 0, tl)
            src_o, src_l = (to, tl) if d == 16 else (bo, bl)

            def merge(rows, set_rows, off, d=d, src_o=src_o, src_l=src_l):
                blk = pl.ds(off, ATTN_BLOCK)
                o2, l2 = src_o[blk, :], src_l[blk, :]
                if d != DILATIONS[0]:
                    lo, oo = rows(l_ref), rows(o_ref)
                    ln = jnp.maximum(lo, l2)
                    wa, wb = jnp.exp(lo - ln), jnp.exp(l2 - ln)
                    o2 = (wa * oo + wb * o2) / (wa + wb)
                    l2 = ln + jnp.log(wa + wb)
                set_rows(o_ref, o2)
                set_rows(l_ref, l2)

            _for_residue_blocks(S, min(d, 4), merge)

    slab = pl.BlockSpec((S, LANES), lambda p: (0, p))
    f32_slab, bf16_slab = pltpu.VMEM((S, LANES), F32), pltpu.VMEM((S, LANES), BF16)
    bf16_window = pltpu.VMEM((S + ATTN_BLOCK, LANES), BF16)
    return pl.pallas_call(
        body, name=name, grid=(ATTN_W // LANES,), in_specs=[pl.BlockSpec((S, PAIR_W), lambda p: (0, p))],
        out_specs=[slab, slab],
        out_shape=[jax.ShapeDtypeStruct((S, ATTN_W), F32), jax.ShapeDtypeStruct((S, ATTN_W), F32)],
        scratch_shapes=[f32_slab] * 3 + [bf16_slab, bf16_window, bf16_window] + [f32_slab] * 4,
        compiler_params=_cparams("parallel"),
    )(proj_a)


def _attn_bwd(proj_a, do, lse, delta, *, name):
    S = proj_a.shape[0]

    def body(p_ref, do_ref, lse_ref, dl_ref, o_ref, qs, ks, vs, dqs, dks, dvs, bq, bk, bv, bdo, blse, bdl, bdq, bdk, bdv,
             tmp):
        _split_pair(p_ref, qs, ks, vs, bk, bv)
        bdk[0:ATTN_BLOCK, :] = jnp.zeros((ATTN_BLOCK, LANES), F32)
        bdv[0:ATTN_BLOCK, :] = jnp.zeros((ATTN_BLOCK, LANES), F32)
        for d in DILATIONS:
            nb = S // (ATTN_BLOCK * d)
            _regroup(S, d, ((qs, bq, 0), (ks, bk, ATTN_BLOCK), (vs, bv, ATTN_BLOCK), (do_ref, bdo, 0),
                            (lse_ref, blse, 0), (dl_ref, bdl, 0)), tmp)

            def blocks(group, nb=nb):
                lane = lax.broadcasted_iota(jnp.int32, (1, LANES), 1)
                heads = [(lane >= HEAD_DIM * e) & (lane < HEAD_DIM * (e + 1)) for e in range(LANES // HEAD_DIM)]
                zero = jnp.zeros((ATTN_BLOCK, LANES), BF16)
                chains = [(off, b, e, mh) for off, b in group for e, mh in enumerate(heads)]
                qm = [jnp.where(mh, bq[pl.ds(off, ATTN_BLOCK), :], zero) for off, _, _, mh in chains]
                dom = [jnp.where(mh, bdo[pl.ds(off, ATTN_BLOCK), :], zero) for off, _, _, mh in chains]
                s = [_nt(x, bk[pl.ds(off, 2 * ATTN_BLOCK), :]) for x, (off, _, _, _) in zip(qm, chains)]
                dp = [_nt(x, bv[pl.ds(off, 2 * ATTN_BLOCK), :]) for x, (off, _, _, _) in zip(dom, chains)]
                p, ds = [], []
                for (off, b, e, _), sc, dpc in zip(chains, s, dp):
                    valid, _ = _attn_masks(jnp.bitwise_and(b, nb - 1) == 0)
                    blk = pl.ds(off, ATTN_BLOCK)
                    pc = jnp.where(valid, jnp.exp(sc * ATTN_SCALE - _head_value(blse[blk, :], lane, e)), 0.0)
                    ds.append((pc * (dpc - _head_value(bdl[blk, :], lane, e)) * ATTN_SCALE).astype(BF16))
                    p.append(pc.astype(BF16))
                dq = [_nn(x, bk[pl.ds(off, 2 * ATTN_BLOCK), :]) for x, (off, _, _, _) in zip(ds, chains)]
                dk = [_tn(x, y) for x, y in zip(ds, qm)]
                dv = [_tn(x, y) for x, y in zip(p, dom)]
                nh = len(heads)
                for u, (off, _) in enumerate(group):
                    dq2 = jnp.zeros((ATTN_BLOCK, LANES), F32)
                    for mh, x in zip(heads, dq[nh * u:nh * (u + 1)]):
                        dq2 = jnp.where(mh, x, dq2)
                    bdq[pl.ds(off, ATTN_BLOCK), :] = dq2
                    for acc, grads in ((bdk, dk), (bdv, dv)):
                        win_grad = sum(grads[nh * u + 1:nh * (u + 1)], grads[nh * u])
                        acc[pl.ds(off, ATTN_BLOCK), :] += win_grad[:ATTN_BLOCK]
                        acc[pl.ds(off + ATTN_BLOCK, ATTN_BLOCK), :] = win_grad[ATTN_BLOCK:]

            _for_blocks(S, ATTN_UNROLL_BWD, blocks)

            outs = ((dqs, bdq, 0), (dks, bdk, ATTN_BLOCK), (dvs, bdv, ATTN_BLOCK))
            if d == 16:
                for acc, grad, pad in outs:
                    _order16_to_4(grad, pad, tmp)

                    def add(rows, set_rows, off, acc=acc):
                        set_rows(acc, rows(acc) + tmp[pl.ds(off, ATTN_BLOCK), :])

                    _for_residue_blocks(S, 4, add)
            else:
                def scatter(rows, set_rows, off, d=d):
                    for acc, grad, pad in outs:
                        part = grad[pl.ds(pad + off, ATTN_BLOCK), :]
                        set_rows(acc, part if d == DILATIONS[0] else rows(acc) + part)

                _for_residue_blocks(S, d, scatter)
        o_ref[:, 0:LANES] = dqs[...].astype(BF16)
        o_ref[:, LANES:2 * LANES] = dks[...].astype(BF16)
        o_ref[:, 2 * LANES:3 * LANES] = dvs[...].astype(BF16)

    slab = pl.BlockSpec((S, LANES), lambda p: (0, p))
    pair = pl.BlockSpec((S, PAIR_W), lambda p: (0, p))
    f32_slab, bf16_slab = pltpu.VMEM((S, LANES), F32), pltpu.VMEM((S, LANES), BF16)
    f32_window, bf16_window = pltpu.VMEM((S + ATTN_BLOCK, LANES), F32), pltpu.VMEM((S + ATTN_BLOCK, LANES), BF16)
    return pl.pallas_call(
        body, name=name, grid=(ATTN_W // LANES,), in_specs=[pair, slab, slab, slab], out_specs=pair,
        out_shape=jax.ShapeDtypeStruct(proj_a.shape, BF16),
        scratch_shapes=[f32_slab] * 6 + [bf16_slab, bf16_window, bf16_window, bf16_slab, f32_slab, f32_slab,
                                         f32_slab, f32_window, f32_window, f32_slab],
        compiler_params=_cparams("parallel"),
    )(proj_a, do, lse, delta)


HG_T = 2 * HGRN_CHUNK
HG_GROUPS = 2
HG_STEP = HG_GROUPS * HG_T


def _hgrn_consts():
    row = lax.broadcasted_iota(jnp.int32, (HG_T, HG_T), 0)
    col = lax.broadcasted_iota(jnp.int32, (HG_T, HG_T), 1)
    same = (row >= HGRN_CHUNK) == (col >= HGRN_CHUNK)
    return row, same & (col <= row), same & (col >= row)


def _lower_bound(logits_ref):
    l0, l1 = logits_ref[0:1, :], logits_ref[1:2, :]
    mx = jnp.maximum(l0, l1)
    e0, e1 = jnp.exp(l0 - mx), jnp.exp(l1 - mx)
    return e0 / (e0 + e1)


def _hgrn_chains():
    chains = [(g, h) for g in range(HG_GROUPS) for h in range(HGRN_HEADS)]
    rows = [pl.ds(HG_T * g, HG_T) for g, _ in chains]
    lanes = [slice(HGRN_DIM * h, HGRN_DIM * (h + 1)) for _, h in chains]
    return chains, rows, lanes


def _hgrn_gates(qs, fs, lbs, row, causal):
    C = HGRN_CHUNK
    tri = jnp.where(causal, 1.0, 0.0).astype(F32)
    sgs = [_sigmoid(f) for f in fs]
    forgets = [lb + (1.0 - lb) * sg for lb, sg in zip(lbs, sgs)]
    logfs = [jnp.log(forget) for forget in forgets]
    bs = [_nn(tri, logf, HIGHEST) for logf in logfs]
    out = []
    for q, sg, forget, logf, b in zip(qs, sgs, forgets, logfs, bs):
        key = 1.0 - forget
        bend0 = jnp.sum(logf[:C], axis=0, keepdims=True)
        bend1 = jnp.sum(logf[C:], axis=0, keepdims=True)
        bend = jnp.where(row < C, bend0, bend1)
        eb, emb, eend = jnp.exp(b), jnp.exp(-b), jnp.exp(bend - b)
        sq = _sigmoid(q)
        out.append(dict(sg=sg, forget=forget, key=key, bend0=bend0, bend1=bend1, eb=eb, emb=emb, eend=eend, sq=sq,
                        qd=q * sq * eb, ki=key * emb, ke=key * eend))
    return out


def _hgrn_fwd(proj, proj_f, logits, *, name):
    S = proj.shape[0]
    W, C = HGRN_W, HGRN_CHUNK

    def body(q_ref, f_ref, i_ref, lg_ref, rec_ref, st_ref, s_ref):
        @pl.when(pl.program_id(0) == 0)
        def _():
            s_ref[...] = jnp.zeros_like(s_ref)

        row, causal, _ = _hgrn_consts()
        lb_all = _lower_bound(lg_ref)
        chains, rows, lanes = _hgrn_chains()
        n = range(len(chains))
        gts = _hgrn_gates([q_ref[rows[c], lanes[c]].astype(F32) for c in n], [f_ref[rows[c], lanes[c]] for c in n],
                          [lb_all[:, lanes[c]] for c in n], row, causal)
        qd, ki, ke = ([gt[k].astype(BF16) for gt in gts] for k in ("qd", "ki", "ke"))
        iv = [i_ref[rows[c], lanes[c]].astype(BF16) for c in n]
        a = [_nt(qd[c], ki[c]) for c in n]
        u0 = [_tn(iv[c][:C], ke[c][:C]) for c in n]
        u1 = [_tn(iv[c][C:], ke[c][C:]) for c in n]
        state = [s_ref[h] for h in range(HGRN_HEADS)]
        s0, s1 = [], []
        for c, (g, h) in enumerate(chains):
            s0.append(state[h])
            s1.append(jnp.exp(gts[c]["bend0"]) * s0[c] + u0[c])
            state[h] = jnp.exp(gts[c]["bend1"]) * s1[c] + u1[c]
        o0 = [_nt(qd[c][:C], s0[c].astype(BF16)) for c in n]
        o1 = [_nt(qd[c][C:], s1[c].astype(BF16)) for c in n]
        o = [_nn(jnp.where(causal, a[c], 0.0).astype(BF16), iv[c]) for c in n]
        for c, (g, h) in enumerate(chains):
            st_ref[2 * g, h] = s0[c]
            st_ref[2 * g + 1, h] = s1[c]
            rec_ref[rows[c], lanes[c]] = o[c] + jnp.concatenate([o0[c], o1[c]], axis=0)
        for h in range(HGRN_HEADS):
            s_ref[h] = state[h]

    blk = lambda j: pl.BlockSpec((HG_STEP, W), lambda t: (t, j))
    return pl.pallas_call(
        body, name=name, grid=(S // HG_STEP,),
        in_specs=[blk(0), blk(0), blk(2), pl.BlockSpec((2, W), lambda t: (0, 0))],
        out_specs=[blk(0), pl.BlockSpec((2 * HG_GROUPS, HGRN_HEADS, HGRN_DIM, HGRN_DIM), lambda t: (t, 0, 0, 0))],
        out_shape=[jax.ShapeDtypeStruct((S, W), F32),
                   jax.ShapeDtypeStruct((S // C, HGRN_HEADS, HGRN_DIM, HGRN_DIM), F32)],
        scratch_shapes=[pltpu.VMEM((HGRN_HEADS, HGRN_DIM, HGRN_DIM), F32)],
        compiler_params=_cparams("arbitrary"),
    )(proj, proj_f, proj, logits)


def _hgrn_bwd(proj, proj_f, logits, states, drec, *, name):
    S = proj.shape[0]
    W, C = HGRN_W, HGRN_CHUNK
    nt = S // HG_STEP

    def body(q_ref, f_ref, i_ref, lg_ref, st_ref, do_ref, dp_ref, dlg_ref, ds_ref, dlb_ref):
        t = pl.program_id(0)

        @pl.when(t == 0)
        def _():
            ds_ref[...] = jnp.zeros_like(ds_ref)
            dlb_ref[...] = jnp.zeros_like(dlb_ref)

        row, causal, anti = _hgrn_consts()
        lb_all = _lower_bound(lg_ref)
        chains, rows, lanes = _hgrn_chains()
        n = range(len(chains))
        qs, lbs = [q_ref[rows[c], lanes[c]].astype(F32) for c in n], [lb_all[:, lanes[c]] for c in n]
        gts = _hgrn_gates(qs, [f_ref[rows[c], lanes[c]] for c in n], lbs, row, causal)
        qd, ki, ke = ([gt[k] for gt in gts] for k in ("qd", "ki", "ke"))
        qdb, kib, keb = ([x.astype(BF16) for x in xs] for xs in (qd, ki, ke))
        iv = [i_ref[rows[c], lanes[c]].astype(BF16) for c in n]
        dob = [do_ref[rows[c], lanes[c]].astype(BF16) for c in n]
        s0 = [st_ref[2 * g, h] for g, h in chains]
        s1 = [st_ref[2 * g + 1, h] for g, h in chains]
        dec0, dec1 = [jnp.exp(gt["bend0"]) for gt in gts], [jnp.exp(gt["bend1"]) for gt in gts]
        a = [_nt(qdb[c], kib[c]) for c in n]
        da = [_nt(dob[c], iv[c]) for c in n]
        dqd1 = [_nn(dob[c][C:], s1[c].astype(BF16)) for c in n]
        dqd0 = [_nn(dob[c][:C], s0[c].astype(BF16)) for c in n]
        t1 = [_tn(dob[c][C:], qdb[c][C:]) for c in n]
        t0 = [_tn(dob[c][:C], qdb[c][:C]) for c in n]
        carry = [ds_ref[h] for h in range(HGRN_HEADS)]
        ds1, ds0 = [None] * len(chains), [None] * len(chains)
        for c in reversed(n):
            h = chains[c][1]
            ds1[c] = carry[h]
            ds0[c] = dec1[c] * ds1[c] + t1[c]
            carry[h] = dec0[c] * ds0[c] + t0[c]
        for h in range(HGRN_HEADS):
            ds_ref[h] = carry[h]
        ds1b, ds0b = [x.astype(BF16) for x in ds1], [x.astype(BF16) for x in ds0]
        a = [jnp.where(causal, x, 0.0).astype(BF16) for x in a]
        da = [jnp.where(causal, x, 0.0).astype(BF16) for x in da]
        di1 = [_nt(keb[c][C:], ds1b[c]) for c in n]
        dke1 = [_nn(iv[c][C:], ds1b[c]) for c in n]
        di0 = [_nt(keb[c][:C], ds0b[c]) for c in n]
        dke0 = [_nn(iv[c][:C], ds0b[c]) for c in n]
        dqd_a = [_nn(da[c], kib[c]) for c in n]
        dki = [_tn(da[c], qdb[c]) for c in n]
        di_a = [_tn(a[c], dob[c]) for c in n]
        dqd, dke, db = [], [], []
        for c in n:
            ddec1 = jnp.sum(ds1[c] * s1[c], axis=0, keepdims=True)
            ddec0 = jnp.sum(ds0[c] * s0[c], axis=0, keepdims=True)
            dqd.append(dqd_a[c] + jnp.concatenate([dqd0[c], dqd1[c]], axis=0))
            h = chains[c][1]
            dp_ref[rows[c], 2 * W + HGRN_DIM * h:2 * W + HGRN_DIM * (h + 1)] = (
                di_a[c] + jnp.concatenate([di0[c], di1[c]], axis=0)).astype(BF16)
            dke.append(jnp.concatenate([dke0[c], dke1[c]], axis=0))
            gke = dke[c] * ke[c]
            dbend0 = jnp.sum(gke[:C], axis=0, keepdims=True) + ddec0 * dec0[c]
            dbend1 = jnp.sum(gke[C:], axis=0, keepdims=True) + ddec1 * dec1[c]
            dbc = dqd[c] * qd[c] - dki[c] * ki[c] - gke
            db.append(dbc + jnp.where(row == C - 1, dbend0, 0.0) + jnp.where(row == HG_T - 1, dbend1, 0.0))
        tri = jnp.where(anti, 1.0, 0.0).astype(F32)
        dlogf = [_nn(tri, db[c], HIGHEST) for c in n]
        for c in n:
            gt, lb, q, h = gts[c], lbs[c], qs[c], chains[c][1]
            dforget = dlogf[c] / gt["forget"] - (dki[c] * gt["emb"] + dke[c] * gt["eend"])
            sg, sq = gt["sg"], gt["sq"]
            dp_ref[rows[c], W + HGRN_DIM * h:W + HGRN_DIM * (h + 1)] = (
                dforget * (1.0 - lb) * sg * (1.0 - sg)).astype(BF16)
            dlb_ref[:, lanes[c]] += jnp.sum(dforget * (1.0 - sg), axis=0, keepdims=True)
            dp_ref[rows[c], lanes[c]] = (dqd[c] * gt["eb"] * sq * (1.0 + q * (1.0 - sq))).astype(BF16)

        @pl.when(t == nt - 1)
        def _():
            dl0 = dlb_ref[...] * lb_all * (1.0 - lb_all)
            dlg_ref[0:1, :] = dl0
            dlg_ref[1:2, :] = -dl0

    blk = lambda j: pl.BlockSpec((HG_STEP, W), lambda t: (nt - 1 - t, j))
    full = pl.BlockSpec((2, W), lambda t: (0, 0))
    return pl.pallas_call(
        body, name=name, grid=(nt,),
        in_specs=[blk(0), blk(0), blk(2), full,
                  pl.BlockSpec((2 * HG_GROUPS, HGRN_HEADS, HGRN_DIM, HGRN_DIM), lambda t: (nt - 1 - t, 0, 0, 0)), blk(0)],
        out_specs=[pl.BlockSpec((HG_STEP, 3 * W), lambda t: (nt - 1 - t, 0)), full],
        out_shape=[jax.ShapeDtypeStruct((S, 3 * W), BF16), jax.ShapeDtypeStruct((2, W), F32)],
        scratch_shapes=[pltpu.VMEM((HGRN_HEADS, HGRN_DIM, HGRN_DIM), F32), pltpu.VMEM((1, W), F32)],
        compiler_params=_cparams("arbitrary"),
    )(proj, proj_f, proj, logits, states, drec)


def _out_proj(attn, rec, proj_h, x, g_attn, g_hgrn, g_norm2, w_out, *, name, tm=512):
    S, D = x.shape
    AW, W = ATTN_W, HGRN_W

    def body(a_ref, r_ref, hg_ref, x_ref, ga_ref, gh_ref, g2_ref, w_ref, h_ref, u_ref, m_ref):
        av = a_ref[...]
        m_ref[:, :AW] = (av * _rstd(av) * ga_ref[...]).astype(BF16)
        for h in range(HGRN_HEADS):
            sl = slice(HGRN_DIM * h, HGRN_DIM * (h + 1))
            rv, hg = r_ref[:, sl], hg_ref[:, sl].astype(F32)
            m_ref[:, AW + HGRN_DIM * h:AW + HGRN_DIM * (h + 1)] = (
                (rv * _rstd(rv) * gh_ref[:, sl]) * (hg * _sigmoid(hg))).astype(BF16)
        h1 = x_ref[...] + _nn(m_ref[...], w_ref[...])
        h_ref[...] = h1
        u_ref[...] = (h1 * _rstd(h1) * g2_ref[...]).astype(BF16)

    row = lambda w, j=0: pl.BlockSpec((tm, w), lambda i: (i, j))
    vec = lambda w: pl.BlockSpec((1, w), lambda i: (0, 0))
    return pl.pallas_call(
        body, name=name, grid=(S // tm,),
        in_specs=[row(AW), row(W), row(W, 3), row(D), vec(AW), vec(W), vec(D), _resident(w_out.shape)],
        out_specs=[row(D), row(D), row(AW + W)],
        out_shape=[jax.ShapeDtypeStruct((S, D), F32), jax.ShapeDtypeStruct((S, D), BF16),
                   jax.ShapeDtypeStruct((S, AW + W), BF16)],
        compiler_params=_cparams("parallel"),
    )(attn, rec, proj_h, x, g_attn, g_hgrn, g_norm2, w_out)


def _dmix_post_bwd(dh1b, w_out, attn, rec, proj_h, g_attn, g_hgrn, *, name, tm=512):
    S, D = dh1b.shape
    AW, W = ATTN_W, HGRN_W

    def body(dh_ref, w_ref, a_ref, r_ref, hg_ref, ga_ref, gh_ref, do_ref, dl_ref, dr_ref, dhg_ref, dga_ref, dgh_ref):
        first = pl.program_id(0) == 0
        dmix = _nt(dh_ref[...], w_ref[...])
        av = a_ref[...]
        dov, dga = _norm_bwd(av, ga_ref[...], dmix[:, :AW])
        do_ref[...] = dov
        shift = HEAD_DIM.bit_length() - 1
        hi = lax.shift_right_logical(lax.broadcasted_iota(jnp.int32, (AW, AW), 0), shift)
        hj = lax.shift_right_logical(lax.broadcasted_iota(jnp.int32, (AW, AW), 1), shift)
        prod = dov * av
        hi_part = prod.astype(BF16)
        lo_part = (prod - hi_part.astype(F32)).astype(BF16)
        same_head = jnp.where(hi == hj, 1.0, 0.0).astype(BF16)
        dl_ref[...] = _nn(hi_part, same_head) + _nn(lo_part, same_head)
        _accumulate(dga_ref, jnp.sum(dga, axis=0, keepdims=True), first)

        @pl.when(first)
        def _():
            dgh_ref[...] = jnp.zeros_like(dgh_ref)

        for h in range(HGRN_HEADS):
            sl = slice(HGRN_DIM * h, HGRN_DIM * (h + 1))
            rv, hg, gv = r_ref[:, sl], hg_ref[:, sl].astype(F32), gh_ref[:, sl]
            dout = dmix[:, AW + HGRN_DIM * h:AW + HGRN_DIM * (h + 1)]
            sg = _sigmoid(hg)
            drv, dgh = _norm_bwd(rv, gv, dout * (hg * sg))
            dr_ref[:, sl] = drv
            dgh_ref[:, sl] += jnp.sum(dgh, axis=0, keepdims=True)
            dhg_ref[:, sl] = (dout * (rv * _rstd(rv) * gv) * (sg * (1.0 + hg * (1.0 - sg)))).astype(BF16)

    row = lambda w, j=0: pl.BlockSpec((tm, w), lambda i: (i, j))
    vec = lambda w: pl.BlockSpec((1, w), lambda i: (0, 0))
    return pl.pallas_call(
        body, name=name, grid=(S // tm,),
        in_specs=[row(D), _resident(w_out.shape), row(AW), row(W), row(W, 3), vec(AW), vec(W)],
        out_specs=[row(AW), row(AW), row(W), row(W), vec(AW), vec(W)],
        out_shape=[jax.ShapeDtypeStruct((S, AW), F32), jax.ShapeDtypeStruct((S, AW), F32),
                   jax.ShapeDtypeStruct((S, W), F32), jax.ShapeDtypeStruct((S, W), BF16),
                   jax.ShapeDtypeStruct((1, AW), F32), jax.ShapeDtypeStruct((1, W), F32)],
        compiler_params=_cparams("arbitrary"),
    )(dh1b, w_out, attn, rec, proj_h, g_attn, g_hgrn)


def _conv_act(g, g1, g2, w_ref, b_ref):
    c = b_ref[...] + w_ref[0:1, :] * g2 + w_ref[1:2, :] * g1 + w_ref[2:3, :] * g
    return c, 0.5 * (1.0 + lax.erf(c * (2.0 ** -0.5)))


def _shift_down(g, halo, row):
    g1 = jnp.where(row == 0, halo[7:8], pltpu.roll(g, 1, 0))
    g2 = jnp.where(row == 0, halo[6:7], jnp.where(row == 1, halo[7:8], pltpu.roll(g, 2, 0)))
    return g1, g2


def _shift_up(x, halo, row):
    n = x.shape[0]
    x1 = jnp.where(row == n - 1, halo[0:1], pltpu.roll(x, n - 1, 0))
    x2 = jnp.where(row == n - 2, halo[0:1], jnp.where(row == n - 1, halo[1:2], pltpu.roll(x, n - 2, 0)))
    return x1, x2


def _up_glu(u, wt_up, conv_w, conv_b, *, name, tm=1024, tn=1408):
    S, D = u.shape
    F = wt_up.shape[0] // 2
    tn = _tile(F, tn)
    nf = F // tn

    def body(u_ref, wg_ref, wv_ref, cw_ref, cb_ref, g_ref, ge_ref, t_ref, a_ref, halo_ref):
        i, j = pl.program_id(0), pl.program_id(1)

        @pl.when(i == 0)
        def _():
            halo_ref[j] = jnp.zeros((SUBLANES, tn), F32)

        uv = u_ref[...]
        g, v = _nt(uv, wg_ref[...]), _nt(uv, wv_ref[...])
        row = lax.broadcasted_iota(jnp.int32, (tm, tn), 0)
        g1, g2 = _shift_down(g, halo_ref[j], row)
        c, cdf = _conv_act(g, g1, g2, cw_ref, cb_ref)
        gelu = c * cdf
        pdf = jnp.exp(-0.5 * c * c) * (1.0 / (2.0 * jnp.pi) ** 0.5)
        a_ref[...] = (gelu * v).astype(BF16)
        g_ref[...] = g.astype(BF16)
        ge_ref[...] = gelu.astype(BF16)
        t_ref[...] = (v * (cdf + c * pdf)).astype(BF16)
        halo_ref[j] = g[tm - SUBLANES:, :]

    col = pl.BlockSpec((tm, tn), lambda i, j: (i, j))
    out = jax.ShapeDtypeStruct((S, F), BF16)
    return pl.pallas_call(
        body, name=name, grid=(S // tm, nf),
        in_specs=[pl.BlockSpec((tm, D), lambda i, j: (i, 0)), pl.BlockSpec((tn, D), lambda i, j: (j, 0)),
                  pl.BlockSpec((tn, D), lambda i, j: (j + nf, 0)), pl.BlockSpec((3, tn), lambda i, j: (0, j)),
                  pl.BlockSpec((1, tn), lambda i, j: (0, j))],
        out_specs=[col, col, col, col], out_shape=[out, out, out, out],
        scratch_shapes=[pltpu.VMEM((nf, SUBLANES, tn), F32)], compiler_params=_cparams("arbitrary", "arbitrary"),
    )(u, wt_up, wt_up, conv_w, conv_b)


def _dact_glu_bwd(dh2b, w_down, gate, gelu, vslope, conv_w, *, name, tm=512, tn=1408):
    S, D = dh2b.shape
    F = gate.shape[1]
    tn = _tile(F, tn)
    nf, ni = F // tn, S // tm

    def body(dh_ref, wd_ref, g_ref, ge_ref, t_ref, cw_ref, dg_ref, dv_ref, dcw_ref, dcb_ref, halo_ref, acc_ref):
        i, j = pl.program_id(0), pl.program_id(1)

        @pl.when(i == 0)
        def _():
            halo_ref[j] = jnp.zeros((SUBLANES, tn), F32)
            acc_ref[j] = jnp.zeros((SUBLANES, tn), F32)

        g = g_ref[...].astype(F32)
        row = lax.broadcasted_iota(jnp.int32, (tm, tn), 0)
        da = _nt(dh_ref[...], wd_ref[...])
        dv_ref[...] = (da * ge_ref[...].astype(F32)).astype(BF16)
        dc = da * t_ref[...].astype(F32)
        d1, d2 = _shift_up(dc, halo_ref[j], row)
        dg_ref[...] = (cw_ref[2:3, :] * dc + cw_ref[1:2, :] * d1 + cw_ref[0:1, :] * d2).astype(BF16)
        halo_ref[j] = dc[:SUBLANES, :]
        for k, t in enumerate((d2 * g, d1 * g, dc * g, dc)):
            acc_ref[j, k:k + 1, :] += jnp.sum(t, axis=0, keepdims=True)

        @pl.when((i == ni - 1) & (j == nf - 1))
        def _():
            for jj in range(nf):
                dcw_ref[:, jj * tn:(jj + 1) * tn] = acc_ref[jj, 0:3, :]
                dcb_ref[:, jj * tn:(jj + 1) * tn] = acc_ref[jj, 3:4, :]

    tile = pl.BlockSpec((tm, tn), lambda i, j: (ni - 1 - i, j))
    return pl.pallas_call(
        body, name=name, grid=(ni, nf),
        in_specs=[pl.BlockSpec((tm, D), lambda i, j: (ni - 1 - i, 0)), pl.BlockSpec((tn, D), lambda i, j: (j, 0)),
                  tile, tile, tile, pl.BlockSpec((3, tn), lambda i, j: (0, j))],
        out_specs=[tile, tile, pl.BlockSpec((3, F), lambda i, j: (0, 0)), pl.BlockSpec((1, F), lambda i, j: (0, 0))],
        out_shape=[jax.ShapeDtypeStruct((S, F), BF16), jax.ShapeDtypeStruct((S, F), BF16),
                   jax.ShapeDtypeStruct((3, F), F32), jax.ShapeDtypeStruct((1, F), F32)],
        scratch_shapes=[pltpu.VMEM((nf, SUBLANES, tn), F32), pltpu.VMEM((nf, SUBLANES, tn), F32)],
        compiler_params=_cparams("arbitrary", "arbitrary"),
    )(dh2b, w_down, gate, gelu, vslope, conv_w)


def _down_loss(act, w_down, h1, g, target, *, name, tm=512):
    S, F = act.shape
    D = h1.shape[1]

    def body(a_ref, w_ref, h_ref, g_ref, t_ref, dh_ref, dhb_ref, dg_ref, loss_ref):
        first = pl.program_id(0) == 0
        h2 = h_ref[...] + _nn(a_ref[...], w_ref[...])
        gv = g_ref[...]
        r = _rstd(h2)
        xh = h2 * r
        err = xh * gv - t_ref[...]
        part_loss = 0.5 * jnp.sum(jnp.mean(err * err, axis=-1, keepdims=True), axis=0, keepdims=True)
        dy = err * (1.0 / D)
        dxh = dy * gv
        dh = r * (dxh - xh * jnp.mean(dxh * xh, axis=-1, keepdims=True))
        dh_ref[...] = dh
        dhb_ref[...] = dh.astype(BF16)
        _accumulate(dg_ref, jnp.sum(dy * xh, axis=0, keepdims=True), first)
        _accumulate(loss_ref, jnp.broadcast_to(part_loss, (1, LANES)), first)

    row = lambda w: pl.BlockSpec((tm, w), lambda i: (i, 0))
    vec = lambda w: pl.BlockSpec((1, w), lambda i: (0, 0))
    return pl.pallas_call(
        body, name=name, grid=(S // tm,), in_specs=[row(F), _resident(w_down.shape), row(D), vec(D), row(D)],
        out_specs=[row(D), row(D), vec(D), vec(LANES)],
        out_shape=[jax.ShapeDtypeStruct((S, D), F32), jax.ShapeDtypeStruct((S, D), BF16),
                   jax.ShapeDtypeStruct((1, D), F32), jax.ShapeDtypeStruct((1, LANES), F32)],
        compiler_params=_cparams("arbitrary"),
    )(act, w_down, h1, g, target)


def _grad_norm_input(pieces, ws, x, g, add, *, name, tm=512):
    S, D = x.shape
    widths = [p.shape[1] for p in pieces]
    n, nw = len(pieces), len(ws)
    where, wi, off = [], 0, ws[0][1]
    for wd in widths:
        if off == ws[wi][0].shape[0]:
            wi, off = wi + 1, ws[wi + 1][1]
        where.append((wi, off))
        off += wd
    ws = [w for w, _ in ws]

    def body(*refs):
        p_refs, w_refs = refs[:n], refs[n:n + nw]
        x_ref, g_ref, add_ref, dx_ref, dxb_ref, dg_ref = refs[n + nw:]
        halves = _row_halves(tm)
        du = []
        for rows in halves:
            terms = [_nn(p_refs[k][rows, :], w_refs[wi][off:off + widths[k], :]) for k, (wi, off) in enumerate(where)]
            du.append(sum(terms[1:], terms[0]))
        dg_sum = None
        for rows, duh in zip(halves, du):
            dx, dg = _norm_bwd(x_ref[rows, :], g_ref[...], duh)
            dx = add_ref[rows, :] + dx
            dx_ref[rows, :] = dx
            dxb_ref[rows, :] = dx.astype(BF16)
            part = jnp.sum(dg, axis=0, keepdims=True)
            dg_sum = part if dg_sum is None else dg_sum + part
        _accumulate(dg_ref, dg_sum, pl.program_id(0) == 0)

    row = lambda w_: pl.BlockSpec((tm, w_), lambda i: (i, 0))
    vec = pl.BlockSpec((1, D), lambda i: (0, 0))
    return pl.pallas_call(
        body, name=name, grid=(S // tm,),
        in_specs=[row(wd) for wd in widths] + [_resident(w.shape) for w in ws] + [row(D), vec, row(D)],
        out_specs=[row(D), row(D), vec],
        out_shape=[jax.ShapeDtypeStruct((S, D), F32), jax.ShapeDtypeStruct((S, D), BF16),
                   jax.ShapeDtypeStruct((1, D), F32)],
        compiler_params=_cparams("arbitrary"),
    )(*pieces, *ws, x, g, add)


def _rows(a):
    return a.reshape(-1, a.shape[-1])


def _row_tile(rows, cols, itemsize=4, budget=1 << 20):
    t = rows
    while t % 32 == 0 and t * cols * itemsize > budget:
        t //= 2
    return t


def _sum_cast(arrs, out_dtype, *, name):
    shape = arrs[0].shape
    flat = [_rows(a) for a in arrs]
    R, C = flat[0].shape
    tr = _row_tile(R, C)

    def body(*refs):
        acc = refs[0][...].astype(F32)
        for r in refs[1:-1]:
            acc = acc + r[...].astype(F32)
        refs[-1][...] = acc.astype(out_dtype)

    spec = pl.BlockSpec((tr, C), lambda i: (i, 0))
    return pl.pallas_call(
        body, name=name, grid=(R // tr,), in_specs=[spec] * len(flat), out_specs=spec,
        out_shape=jax.ShapeDtypeStruct((R, C), out_dtype), compiler_params=_cparams("parallel"),
    )(*flat).reshape(shape)


def _adamw(parts, w, m, v, *, name):
    shape = w.shape
    w2, m2, v2 = _rows(w), _rows(m), _rows(v)
    R, C = w2.shape
    parts = [p.reshape(-1, R, C) for p in parts]
    tr = _row_tile(R, C)
    np_ = len(parts)
    c1, c2 = 1.0 - ADAM_B1 ** ADAM_STEP, 1.0 - ADAM_B2 ** ADAM_STEP

    def body(*refs):
        terms = [(r, k) for r in refs[:np_] for k in range(r.shape[0])]
        g = terms[0][0][terms[0][1]].astype(F32)
        for r, k in terms[1:]:
            g = g + r[k].astype(F32)
        w_ref, m_ref, v_ref, g_out, d_out, m_out, v_out = refs[np_:]
        mn = ADAM_B1 * m_ref[...] + (1.0 - ADAM_B1) * g
        vn = ADAM_B2 * v_ref[...] + (1.0 - ADAM_B2) * (g * g)
        g_out[...] = g
        d_out[...] = -ADAM_LR * ((mn / c1) / (jnp.sqrt(vn / c2) + ADAM_EPS) + ADAM_WD * w_ref[...])
        m_out[...] = mn
        v_out[...] = vn

    spec = pl.BlockSpec((tr, C), lambda i: (i, 0))
    out = jax.ShapeDtypeStruct((R, C), F32)
    stacks = [pl.BlockSpec((p.shape[0], tr, C), lambda i: (0, i, 0)) for p in parts]
    res = pl.pallas_call(
        body, name=name, grid=(R // tr,), in_specs=stacks + [spec] * 3, out_specs=[spec] * 4,
        out_shape=[out] * 4, compiler_params=_cparams("parallel"),
    )(*parts, w2, m2, v2)
    return [r.reshape(shape) for r in res]


def _adamw_packed(stack, widths, params, *, name):
    c1, c2 = 1.0 - ADAM_B1 ** ADAM_STEP, 1.0 - ADAM_B2 ** ADAM_STEP
    k = stack.shape[0]
    flat = [None if p is None else [_rows(a) for a in p] for p in params]
    n_in = sum(3 for p in flat if p is not None)

    def body(*refs):
        s_ref, ins, outs = refs[0], list(refs[1:1 + n_in]), list(refs[1 + n_in:])
        off = 0
        for width, p in zip(widths, flat):
            rows = 1 if p is None else p[0].shape[0]
            cols = width // rows
            w_ref, m_ref, v_ref = (None, None, None) if p is None else (ins.pop(0), ins.pop(0), ins.pop(0))
            o_refs = [outs.pop(0) for _ in range(1 if p is None else 4)]
            for r in range(rows):
                seg = slice(off + r * cols, off + (r + 1) * cols)
                g = s_ref[0, :, seg]
                for j in range(1, k):
                    g = g + s_ref[j, :, seg]
                o_refs[0][r:r + 1, :] = g
                if p is not None:
                    row = slice(r, r + 1)
                    mn = ADAM_B1 * m_ref[row, :] + (1.0 - ADAM_B1) * g
                    vn = ADAM_B2 * v_ref[row, :] + (1.0 - ADAM_B2) * (g * g)
                    o_refs[1][row, :] = -ADAM_LR * ((mn / c1) / (jnp.sqrt(vn / c2) + ADAM_EPS) + ADAM_WD * w_ref[row, :])
                    o_refs[2][row, :] = mn
                    o_refs[3][row, :] = vn
            off += width

    operands, out_shape = [stack], []
    for width, p in zip(widths, flat):
        if p is None:
            out_shape.append(jax.ShapeDtypeStruct((1, width), F32))
        else:
            operands += p
            out_shape += [jax.ShapeDtypeStruct(p[0].shape, F32)] * 4
    res = list(pl.pallas_call(body, name=name, out_shape=out_shape)(*operands))
    out = []
    for p, orig in zip(flat, params):
        n = 1 if p is None else 4
        out.append([r if orig is None else r.reshape(orig[0].shape) for r in res[:n]])
        res = res[n:]
    return out


def _coords():
    return lax.axis_index("x"), lax.axis_index("y"), lax.axis_index("c")


def _all_gather(shards, *, name):
    n = len(shards)

    def body(*refs):
        x_refs, out_refs = refs[:n], refs[n:2 * n]
        send_sems, recv_sems, local_sems = refs[2 * n:]
        x, y, c = _coords()
        me, sibling = (x, y, c), (x, y, 1 - c)
        chips = [(1 - x, y), (x, 1 - y), (1 - x, 1 - y)]

        def slot(a, dev):
            return out_refs[a].at[4 * dev[0] + 2 * dev[1] + dev[2]]

        def copy(a, k, block, to, src=None):
            return pltpu.make_async_remote_copy(
                src_ref=slot(a, block) if src is None else src, dst_ref=slot(a, block),
                send_sem=send_sems.at[7 * a + k], recv_sem=recv_sems.at[7 * a + k], device_id=to, device_id_type=MESH)

        mine = [pltpu.make_async_copy(x_refs[a], slot(a, me), local_sems.at[a]) for a in range(n)]
        for cp in mine:
            cp.start()
        first = []
        for a in range(n):
            first.append(copy(a, 0, me, sibling, src=x_refs[a]))
            first += [copy(a, 1 + j, me, (*chip, c), src=x_refs[a]) for j, chip in enumerate(chips)]
        for cp in first:
            cp.start()
        passed = []
        for j, chip in enumerate(chips):
            for a in range(n):
                copy(a, 1 + j, (*chip, c), me).wait_recv()
                fwd = copy(a, 4 + j, (*chip, c), sibling)
                fwd.start()
                passed.append(fwd)
        for a in range(n):
            copy(a, 0, sibling, me).wait_recv()
            for j, chip in enumerate(chips):
                copy(a, 4 + j, (*chip, 1 - c), me).wait_recv()
        for cp in first + passed:
            cp.wait_send()
        for cp in mine:
            cp.wait()

    return pl.pallas_call(
        body, name=name, in_specs=[HBM] * n, out_specs=[HBM] * n,
        out_shape=[jax.ShapeDtypeStruct((N_DEV, *s.shape), s.dtype) for s in shards],
        scratch_shapes=[pltpu.SemaphoreType.DMA((7 * n,)), pltpu.SemaphoreType.DMA((7 * n,)),
                        pltpu.SemaphoreType.DMA((n,))],
    )(*shards)


def _flip_y(x, y, c):
    return (x, 1 - y, c)


def _flip_x(x, y, c):
    return (1 - x, y, c)


def _flip_xy(x, y, c):
    return (1 - x, 1 - y, c)


SEM = pl.BlockSpec(memory_space=pltpu.SEMAPHORE)
SIDE_EFFECT = pltpu.SideEffectType.DATAFLOW_SIDE_EFFECTING


def _in_hbm(a):
    return pltpu.with_memory_space_constraint(a, pltpu.HBM)


def _copies_start(srcs, lands, plan, n_copies, *, name, after=None):
    ns, nl = len(srcs), len(lands)
    extra = [] if after is None else [after]

    def body(*refs):
        src_refs, land_refs = refs[:ns], refs[ns:ns + nl]
        send_sems, recv_sems = refs[ns + nl + len(extra):ns + nl + len(extra) + 2]
        token = refs[-1]
        for k, (src, dst, peer, _) in enumerate(plan(src_refs, land_refs, *_coords())):
            pltpu.make_async_remote_copy(src_ref=src, dst_ref=dst, send_sem=send_sems.at[k], recv_sem=recv_sems.at[k],
                                         device_id=peer, device_id_type=MESH).start()
        token[...] = jnp.zeros_like(token)

    bufs = [*srcs, *lands]
    res = pl.pallas_call(
        body, name=name, in_specs=[HBM] * (ns + nl) + [pl.BlockSpec(memory_space=pl.ANY)] * len(extra),
        out_specs=(SEM, SEM, *[HBM] * (ns + nl), pl.BlockSpec(memory_space=pltpu.VMEM)),
        out_shape=(pltpu.SemaphoreType.DMA((n_copies,)), pltpu.SemaphoreType.DMA((n_copies,)),
                   *[pltpu.HBM(b.shape, b.dtype) for b in bufs], jax.ShapeDtypeStruct((SUBLANES, LANES), F32)),
        input_output_aliases={i: 2 + i for i in range(ns + nl)},
        compiler_params=pltpu.CompilerParams(has_side_effects=SIDE_EFFECT),
    )(*[_in_hbm(b) for b in bufs], *extra)
    return res[0], res[1], list(res[2:2 + ns]), list(res[2 + ns:2 + ns + nl]), res[-1]


def _copies_wait(started, plan, after, *, name):
    send_sems, recv_sems, srcs, lands, _ = started
    ns, nl = len(srcs), len(lands)

    def body(*refs):
        src_refs, land_refs = refs[:ns], refs[ns:ns + nl]
        send_sems, recv_sems = refs[ns + nl:ns + nl + 2]
        for k, (src, dst, peer, here) in enumerate(plan(src_refs, land_refs, *_coords())):
            pltpu.make_async_remote_copy(src_ref=src, dst_ref=dst, send_sem=send_sems.at[k], recv_sem=recv_sems.at[k],
                                         device_id=peer, device_id_type=MESH).wait_send()
            pltpu.make_async_remote_copy(src_ref=src, dst_ref=here, send_sem=send_sems.at[k], recv_sem=recv_sems.at[k],
                                         device_id=peer, device_id_type=MESH).wait_recv()

    bufs = [*srcs, *lands]
    res = pl.pallas_call(
        body, name=name, in_specs=[HBM] * (ns + nl) + [SEM, SEM, pl.BlockSpec(memory_space=pl.ANY)],
        out_specs=[HBM] * (ns + nl), out_shape=[pltpu.HBM(b.shape, b.dtype) for b in bufs],
        input_output_aliases={i: i for i in range(ns + nl)},
        compiler_params=pltpu.CompilerParams(has_side_effects=SIDE_EFFECT),
    )(*bufs, send_sems, recv_sems, after)
    return list(res[ns:])


def _dev_index(dev):
    return 4 * dev[0] + 2 * dev[1] + dev[2]


def _ag_chips_plan(src_refs, land_refs, x, y, c):
    me = _dev_index((x, y, c))
    return [(src, land.at[me], peer, land.at[_dev_index(peer)])
            for src, land in zip(src_refs, land_refs) for peer in (_flip_y(x, y, c), _flip_x(x, y, c), _flip_xy(x, y, c))]


def _ag_sibling_plan(src_refs, land_refs, x, y, c):
    chips = [(x, y), (x, 1 - y), (1 - x, y), (1 - x, 1 - y)]
    return [(land.at[_dev_index((*chip, c))], land.at[_dev_index((*chip, c))], (x, y, 1 - c),
             land.at[_dev_index((*chip, 1 - c))]) for land in land_refs for chip in chips]


def _ag_direct_plan(src_refs, land_refs, x, y, c):
    me = _dev_index((x, y, c))
    plan = []
    for src, land in zip(src_refs, land_refs):
        for m in range(1, N_DEV):
            peer = (x + (m >> 2) * (1 - 2 * x), y + ((m >> 1) & 1) * (1 - 2 * y), c + (m & 1) * (1 - 2 * c))
            plan.append((src, land.at[me], peer, land.at[_dev_index(peer)]))
    return plan


def _rs_direct_plan(src_refs, land_refs, x, y, c):
    plan = []
    for src, land in zip(src_refs, land_refs):
        for m in range(1, N_DEV):
            peer = (x + (m >> 2) * (1 - 2 * x), y + ((m >> 1) & 1) * (1 - 2 * y), c + (m & 1) * (1 - 2 * c))
            plan.append((src.at[_dev_index(peer)], land.at[m - 1], peer, land.at[m - 1]))
    return plan


def _rs_start(grads, me, *, name, after=None):
    own = [lax.dynamic_index_in_dim(g, me, 0, keepdims=False) for g in grads]
    lands = [lax.empty((N_DEV - 1, *g.shape[1:]), g.dtype) for g in grads]
    return _copies_start(grads, lands, _rs_direct_plan, (N_DEV - 1) * len(grads), name=name, after=after), own


def _rs_finish(started, after, *, name):
    handle, own = started
    got = _copies_wait(handle, _rs_direct_plan, after, name=name)
    return [[o, land] for o, land in zip(own, got)]


def _gathered_cols(w8):
    return w8.transpose(1, 0, 2).reshape(w8.shape[1], -1)


def _pair_major(wt):
    return wt.reshape(3, ATTN_W // LANES, LANES, -1).transpose(1, 0, 2, 3).reshape(3 * ATTN_W, -1)


def kernel(x, norm1_g, w_in, attn_norm_g, hgrn_norm_g, hgrn_lb_logits, w_out, norm2_g, w_up, conv_w, conv_b, w_down, final_norm_g, loss_target, m_norm1_g, m_w_in, m_attn_norm_g, m_hgrn_norm_g, m_hgrn_lb_logits, m_w_out, m_norm2_g, m_w_up, m_conv_w, m_conv_b, m_w_down, m_final_norm_g, v_norm1_g, v_w_in, v_attn_norm_g, v_hgrn_norm_g, v_hgrn_lb_logits, v_w_out, v_norm2_g, v_w_up, v_conv_w, v_conv_b, v_w_down, v_final_norm_g):
    xs, target = x[0], loss_target[0]
    S, D = xs.shape
    NA = 3 * ATTN_W
    fng = final_norm_g.reshape(1, D)

    t = lambda a: a[0].T
    casts = [_sum_cast([w], BF16, name=f"cast_{nm}") for nm, w in
             (("w_in", t(w_in)), ("w_out", w_out[0]), ("w_up", t(w_up)), ("w_down", w_down[0]))]
    me = _dev_index(_coords())
    (g_in,) = _all_gather(casts[:1], name="ag_w_in")
    later = casts[1:] + [conv_w[0]]
    ag1 = _copies_start(later, [lax.empty((N_DEV, *s.shape), s.dtype) for s in later], _ag_chips_plan,
                        3 * len(later), name="ag_chips_start", after=g_in)
    wi = g_in.reshape(-1, D)
    wi_a = _pair_major(wi[:NA])

    u1, proj_a, proj_h, proj_f = _in_proj(xs, norm1_g + ag1[4][0, 0], wi_a, wi, NA, name="in_proj")
    attn, lse = _attn_fwd(proj_a, name="attn_fwd")
    lands = _copies_wait(ag1, _ag_chips_plan, attn, name="ag_chips_wait")
    lands = [lax.dynamic_update_index_in_dim(l, s, me, 0) for l, s in zip(lands, later)]
    ag2 = _copies_start([], lands, _ag_sibling_plan, 4 * len(later), name="ag_sibling_start")
    rec, states = _hgrn_fwd(proj_h, proj_f, hgrn_lb_logits + ag2[4][0, 0], name="hgrn_fwd")
    g_out, g_up, g_down, g_cw = _copies_wait(ag2, _ag_sibling_plan, rec, name="ag_sibling_wait")
    wo = g_out.reshape(-1, D)
    wu = g_up.reshape(-1, D)
    wd = g_down.reshape(-1, D)
    cw = _gathered_cols(g_cw)
    h1, u2, mixed = _out_proj(attn, rec, proj_h, xs, attn_norm_g, hgrn_norm_g, norm2_g, wo, name="out_proj")
    gate, gelu, vslope, act = _up_glu(u2, wu, cw, conv_b, name="up_glu")
    dh2, dh2b, d_fng, loss_part = _down_loss(act, wd, h1, fng, target, name="down_loss")

    dgate, dval, d_cw, d_cb = _dact_glu_bwd(dh2b, wd, gate, gelu, vslope, cw, name="dact_glu_bwd")
    dw_down = _mm_tn(act, dh2b, tm=256, name="dw_down")
    dh1, dh1b, d_n2g = _grad_norm_input([dgate, dval], [(wu, 0)], h1, norm2_g, dh2, name="du2_norm2_bwd")
    F = dgate.shape[1]
    dw_up = _mm_tn(dgate, u2, tm=256, rows=2 * F, name="dw_up_gate")
    dw_up = _mm_tn(dval, u2, tm=256, rows=2 * F, row_block=lambda i: i + F // 256, into=dw_up, name="dw_up_val")
    rs_ffn = _rs_start([dw_down.reshape(N_DEV, -1, D), dw_up.reshape(N_DEV, -1, D)], me, name="rs_ffn_start")
    dattn, delta, drec, dhg, d_ang, d_hng = _dmix_post_bwd(dh1b, wo, attn, rec, proj_h, attn_norm_g + rs_ffn[0][4][0, 0],
                                                          hgrn_norm_g, name="dmix_post_bwd")
    dw_out = _mm_tn(mixed, dh1b, name="dw_out")
    rs_out = _rs_start([dw_out.reshape(N_DEV, -1, D)], me, name="rs_out_start")
    dproj_h, d_lbl = _hgrn_bwd(proj_h, proj_f, hgrn_lb_logits + rs_out[0][4][0, 0], states, drec, name="hgrn_bwd")
    small = [("loss", loss_part, None, None, None),
             ("attn_norm_g", d_ang, attn_norm_g, m_attn_norm_g, v_attn_norm_g),
             ("hgrn_norm_g", d_hng, hgrn_norm_g, m_hgrn_norm_g, v_hgrn_norm_g),
             ("hgrn_lb_logits", d_lbl, hgrn_lb_logits, m_hgrn_lb_logits, v_hgrn_lb_logits),
             ("norm2_g", d_n2g, norm2_g, m_norm2_g, v_norm2_g),
             ("conv_b", d_cb, conv_b, m_conv_b, v_conv_b),
             ("final_norm_g", d_fng, final_norm_g, m_final_norm_g, v_final_norm_g)]
    pack = lambda arrs: jnp.concatenate([a.reshape(1, -1) for a in arrs], axis=1)
    small_own = [pack([s[1] for s in small]), d_cw]
    ag_small = _copies_start(small_own, [lax.empty((N_DEV, *s.shape), s.dtype) for s in small_own], _ag_direct_plan,
                             (N_DEV - 1) * len(small_own), name="ag_small_start")
    dproj_a = _attn_bwd(proj_a, dattn, lse, delta, name="attn_bwd")
    pairs = ATTN_W // LANES
    dw_in = _mm_tn(dproj_a, u1, tm=LANES, rows=wi.shape[0], row_block=lambda i: pairs * (i % 3) + i // 3,
                   name="dw_in_attn")
    dw_in = _mm_tn(dproj_h, u1, tm=256, rows=wi.shape[0], row_block=lambda i: i + NA // 256, into=dw_in,
                   name="dw_in_hgrn")
    dw_in = _mm_tn(dhg, u1, tm=256, rows=wi.shape[0], row_block=lambda i: i + (NA + 3 * HGRN_W) // 256, into=dw_in,
                   name="dw_in_gate")
    rs_in = _rs_start([dw_in.reshape(N_DEV, -1, D)], me, name="rs_in_start", after=ag_small[4])
    grad_x, _, d_n1g = _grad_norm_input([dproj_a, dproj_h, dhg], [(wi_a, 0), (wi, NA)], xs,
                                        norm1_g + rs_in[0][4][0, 0], dh1, name="du1_norm1_bwd")

    res = {}

    def update(nm, parts, w, m, v, transposed=False):
        if transposed:
            res[nm] = [r.T[None] for r in _adamw(parts, t(w), t(m), t(v), name=f"adamw_{nm}")]
        else:
            res[nm] = _adamw(parts, w, m, v, name=f"adamw_{nm}")

    g_down, g_up = _rs_finish(rs_ffn, grad_x, name="rs_ffn_wait")
    update("w_down", g_down, w_down, m_w_down, v_w_down)
    update("w_up", g_up, w_up, m_w_up, v_w_up, transposed=True)
    (g_out,) = _rs_finish(rs_out, grad_x, name="rs_out_wait")
    update("w_out", g_out, w_out, m_w_out, v_w_out)
    (g_in,) = _rs_finish(rs_in, res["w_up"][1], name="rs_in_wait")
    update("w_in", g_in, w_in, m_w_in, v_w_in, transposed=True)

    g_small, g_dcw = [lax.dynamic_update_index_in_dim(l, s, me, 0)
                      for l, s in zip(_copies_wait(ag_small, _ag_direct_plan, grad_x, name="ag_small_wait"), small_own)]
    sm = _adamw_packed(g_small, [s[1].size for s in small], [None if s[2] is None else s[2:] for s in small],
                       name="adamw_small")
    for (nm, *_), r in zip(small, sm):
        res[nm] = r
    ncw = conv_w.shape[-1]
    mine_cw = lax.dynamic_slice_in_dim(g_dcw, me * ncw, ncw, axis=2)
    res["conv_w"] = _adamw([mine_cw], conv_w, m_conv_w, v_conv_w, name="adamw_conv_w")
    late, _ = lax.optimization_barrier((d_n1g, res["w_in"][1]))
    update("norm1_g", _all_gather([late], name="ag_norm1_grad"), norm1_g, m_norm1_g, v_norm1_g)

    loss = res["loss"][0][0, 0]
    order = ["norm1_g", "w_in", "attn_norm_g", "hgrn_norm_g", "hgrn_lb_logits", "w_out", "norm2_g", "w_up",
             "conv_w", "conv_b", "w_down", "final_norm_g"]
    return (loss, grad_x[None], *[res[nm][0] for nm in order], *[res[nm][1] for nm in order],
            *[res[nm][2] for nm in order], *[res[nm][3] for nm in order])
```

```python
import jax
import jax.numpy as jnp
from jax import lax
from jax.experimental import pallas as pl
from jax.experimental.pallas import tpu as pltpu

F32, BF16 = jnp.float32, jnp.bfloat16
NORM_EPS = 1e-6
ATTN_HEADS, HEAD_DIM, ATTN_BLOCK = 8, 64, 128
DILATIONS = (1, 4, 16)
ATTN_SCALE = HEAD_DIM ** -0.5
ATTN_W = ATTN_HEADS * HEAD_DIM
HGRN_HEADS, HGRN_DIM, HGRN_CHUNK = 4, 128, 64
HGRN_W = HGRN_HEADS * HGRN_DIM
ADAM_LR, ADAM_B1, ADAM_B2, ADAM_EPS, ADAM_WD, ADAM_STEP = 0.001, 0.9, 0.999, 1e-08, 0.01, 10
LANES, SUBLANES = 128, 8
VMEM_LIMIT_BYTES = 56 * 1024 * 1024
N_DEV = 8
MESH = pl.DeviceIdType.MESH
HBM = pl.BlockSpec(memory_space=pltpu.HBM)
HIGHEST = lax.Precision.HIGHEST


def _cparams(*sem):
    return pltpu.CompilerParams(dimension_semantics=sem, vmem_limit_bytes=VMEM_LIMIT_BYTES)


def _tile(n, pref):
    if n <= pref:
        return n
    t = (pref // LANES) * LANES
    while n % t:
        t -= LANES
    return t


def _resident(shape):
    return pl.BlockSpec(shape, lambda *_: (0,) * len(shape), pipeline_mode=pl.Buffered(1))


def _dot(a, b, dims, precision=None):
    return lax.dot_general(a, b, (dims, ((), ())), precision=precision, preferred_element_type=F32)


def _nn(a, b, precision=None):
    return _dot(a, b, ((1,), (0,)), precision)


def _nt(a, b):
    return _dot(a, b, ((1,), (1,)))


def _tn(a, b):
    return _dot(a, b, ((0,), (0,)))


def _sigmoid(x):
    return 1.0 / (1.0 + jnp.exp(-x))


def _rstd(x):
    return lax.rsqrt(jnp.mean(x * x, axis=-1, keepdims=True) + NORM_EPS)


def _norm_bwd(x, g, du):
    r = _rstd(x)
    xh = x * r
    dxh = du * g
    return r * (dxh - xh * jnp.mean(dxh * xh, axis=-1, keepdims=True)), du * xh


def _row_halves(tm):
    return [pl.ds(0, tm // 2), pl.ds(tm // 2, tm // 2)]


def _accumulate(ref, part, first):
    @pl.when(first)
    def _():
        ref[...] = part

    @pl.when(jnp.logical_not(first))
    def _():
        ref[...] += part


def _mm_tn(x, dy, *, name, tm=512, tn=1024, rows=None, row_block=None, into=None):
    S, M = x.shape
    N = dy.shape[1]
    tm, tn = _tile(M, tm), _tile(N, tn)
    row_block = row_block or (lambda i: i)

    def body(x_ref, dy_ref, *rest):
        o_ref, xt_ref = rest[-2:]

        @pl.when(pl.program_id(1) == 0)
        def _():
            xt_ref[...] = x_ref[...].T

        o_ref[...] = _nn(xt_ref[...], dy_ref[...]).astype(BF16)

    operands = [x, dy] + ([] if into is None else [into])
    return pl.pallas_call(
        body, name=name, grid=(M // tm, N // tn),
        in_specs=[pl.BlockSpec((S, tm), lambda i, j: (0, i)), pl.BlockSpec((S, tn), lambda i, j: (0, j))]
        + ([] if into is None else [pl.BlockSpec(memory_space=pl.ANY)]),
        out_specs=pl.BlockSpec((tm, tn), lambda i, j: (row_block(i), j)),
        out_shape=jax.ShapeDtypeStruct((rows or M, N), BF16),
        input_output_aliases={} if into is None else {2: 0},
        scratch_shapes=[pltpu.VMEM((tm, S), BF16)], compiler_params=_cparams("parallel", "arbitrary"),
    )(*operands)


def _in_proj(x, g, wt_attn, wt, row0, *, name, tm=512):
    S, D = x.shape
    NA, NH, W = wt_attn.shape[0], wt.shape[0] - row0, HGRN_W

    def body(x_ref, g_ref, wa_ref, w_ref, u_ref, a_ref, h_ref, f_ref):
        xv = x_ref[...]
        u = (xv * _rstd(xv) * g_ref[...]).astype(BF16)
        u_ref[...] = u
        a_ref[...] = _nt(u, wa_ref[...]).astype(BF16)
        ph = _nt(u, w_ref[row0:row0 + NH, :])
        h_ref[...] = ph.astype(BF16)
        f_ref[...] = ph[:, W:2 * W]

    row = lambda w: pl.BlockSpec((tm, w), lambda i: (i, 0))
    return pl.pallas_call(
        body, name=name, grid=(S // tm,),
        in_specs=[row(D), pl.BlockSpec((1, D), lambda i: (0, 0)), _resident(wt_attn.shape), _resident(wt.shape)],
        out_specs=[row(D), row(NA), row(NH), row(W)],
        out_shape=[jax.ShapeDtypeStruct((S, D), BF16), jax.ShapeDtypeStruct((S, NA), BF16),
                   jax.ShapeDtypeStruct((S, NH), BF16), jax.ShapeDtypeStruct((S, W), F32)],
        compiler_params=_cparams("parallel"),
    )(x, g, wt_attn, wt)


PAIR_W = 3 * LANES
ATTN_UNROLL_FWD, ATTN_UNROLL_BWD = 8, 4


def _attn_masks(first):
    qi = lax.broadcasted_iota(jnp.int32, (ATTN_BLOCK, 2 * ATTN_BLOCK), 0)
    kj = lax.broadcasted_iota(jnp.int32, (ATTN_BLOCK, 2 * ATTN_BLOCK), 1)
    dist = qi + ATTN_BLOCK - kj
    valid = (dist >= 0) & (dist <= ATTN_BLOCK) & jnp.logical_or(kj >= ATTN_BLOCK, jnp.logical_not(first))
    lane = lax.broadcasted_iota(jnp.int32, (1, LANES), 1)
    return valid, lane


def _for_residue_blocks(S, d, fn):
    span = ATTN_BLOCK * d
    nb = S // span

    def step(n, carry):
        base = pl.multiple_of(n * span, span)
        for r in range(d):
            off = pl.multiple_of((r * nb + n) * ATTN_BLOCK, ATTN_BLOCK)
            fn(lambda ref, r=r: _block_rows(ref, base, r, d),
               lambda ref, val, r=r: _set_block_rows(ref, base, r, d, val), off)
        return carry

    lax.fori_loop(0, nb, step, 0)


def _for_blocks(S, unroll, fn):
    def step(i, carry):
        fn([(pl.multiple_of((i * unroll + u) * ATTN_BLOCK, ATTN_BLOCK), i * unroll + u) for u in range(unroll)])
        return carry

    lax.fori_loop(0, S // ATTN_BLOCK // unroll, step, 0)


def _head_value(x2, lane, e):
    return jnp.sum(jnp.where(lane == HEAD_DIM * e, x2, 0.0), axis=-1, keepdims=True)


def _block_rows(ref, base, r, d):
    if d == 1:
        return ref[pl.ds(base, ATTN_BLOCK), :]
    return ref.at[pl.ds(base, ATTN_BLOCK * d)][pl.ds(r, ATTN_BLOCK, stride=d), :]


def _set_block_rows(ref, base, r, d, val):
    if d == 1:
        ref[pl.ds(base, ATTN_BLOCK), :] = val
    else:
        ref.at[pl.ds(base, ATTN_BLOCK * d)][pl.ds(r, ATTN_BLOCK, stride=d), :] = val


def _order4_to_16(src, dst, pad):
    S = src.shape[0]
    q4, q16 = S // 4, S // 16
    for r in range(4):
        for a in range(4):
            for n in range(q16 // ATTN_BLOCK):
                rows = src.at[pl.ds(r * q4 + 4 * ATTN_BLOCK * n, 4 * ATTN_BLOCK)][pl.ds(a, ATTN_BLOCK, stride=4), :]
                dst[pl.ds(pad + (4 * a + r) * q16 + ATTN_BLOCK * n, ATTN_BLOCK), :] = rows.astype(dst.dtype)


def _order16_to_4(src, pad, dst):
    S = dst.shape[0]
    q4, q16 = S // 4, S // 16
    for r in range(4):
        for a in range(4):
            for n in range(q16 // ATTN_BLOCK):
                rows = src[pl.ds(pad + (4 * a + r) * q16 + ATTN_BLOCK * n, ATTN_BLOCK), :]
                dst.at[pl.ds(r * q4 + 4 * ATTN_BLOCK * n, 4 * ATTN_BLOCK)][pl.ds(a, ATTN_BLOCK, stride=4), :] = rows


def _regroup(S, d, pairs, tmp):
    for src, dst, pad in pairs:
        if d == 16:
            def to_tmp(rows, _, off, src=src):
                tmp[pl.ds(off, ATTN_BLOCK), :] = rows(src)

            _for_residue_blocks(S, 4, to_tmp)
            _order4_to_16(tmp, dst, pad)
    if d != 16:
        def to_dst(rows, _, off):
            for src, dst, pad in pairs:
                dst[pl.ds(pad + off, ATTN_BLOCK), :] = rows(src).astype(dst.dtype)

        _for_residue_blocks(S, d, to_dst)


def _split_pair(p_ref, qs, ks, vs, bk, bv):
    qs[...] = p_ref[:, 0:LANES].astype(F32) * ATTN_SCALE
    ks[...] = p_ref[:, LANES:2 * LANES].astype(F32)
    vs[...] = p_ref[:, 2 * LANES:3 * LANES].astype(F32)
    bk[0:ATTN_BLOCK, :] = jnp.zeros((ATTN_BLOCK, LANES), bk.dtype)
    bv[0:ATTN_BLOCK, :] = jnp.zeros((ATTN_BLOCK, LANES), bv.dtype)


def _attn_fwd(proj_a, *, name):
    S = proj_a.shape[0]

    def body(p_ref, o_ref, l_ref, qs, ks, vs, bq, bk, bv, bo, bl, to, tl):
        _split_pair(p_ref, qs, ks, vs, bk, bv)
        for d in DILATIONS:
            nb = S // (ATTN_BLOCK * d)
            _regroup(S, d, ((qs, bq, 0), (ks, bk, ATTN_BLOCK), (vs, bv, ATTN_BLOCK)), to)

            def blocks(group, nb=nb):
                lane = lax.broadcasted_iota(jnp.int32, (1, LANES), 1)
                heads = [(lane >= HEAD_DIM * e) & (lane < HEAD_DIM * (e + 1)) for e in range(LANES // HEAD_DIM)]
                wins = [pl.ds(off, 2 * ATTN_BLOCK) for off, _ in group]
                s = [[_nt(jnp.where(mh, bq[pl.ds(off, ATTN_BLOCK), :], jnp.zeros((ATTN_BLOCK, LANES), BF16)), bk[win, :])
                      for mh in heads] for (off, _), win in zip(group, wins)]
                p, m, l = [], [], []
                for (off, b), su in zip(group, s):
                    valid, _ = _attn_masks(jnp.bitwise_and(b, nb - 1) == 0)
                    sm = [jnp.where(valid, x, -jnp.inf) for x in su]
                    m.append([jnp.max(x, axis=-1, keepdims=True) for x in sm])
                    p.append([jnp.exp(x - mx) for x, mx in zip(sm, m[-1])])
                    l.append([jnp.sum(x, axis=-1, keepdims=True) for x in p[-1]])
                o = [[_nn(x.astype(BF16), bv[win, :]) for x in pu] for pu, win in zip(p, wins)]
                for (off, _), ou, mu, lu in zip(group, o, m, l):
                    o2 = jnp.zeros((ATTN_BLOCK, LANES), F32)
                    l2 = jnp.zeros((ATTN_BLOCK, LANES), F32)
                    for mh, oe, me_, le in zip(heads, ou, mu, lu):
                        o2 = jnp.where(mh, oe / le, o2)
                        l2 = jnp.where(mh, me_ + jnp.log(le), l2)
                    bo[pl.ds(off, ATTN_BLOCK), :] = o2
                    bl[pl.ds(off, ATTN_BLOCK), :] = l2

            _for_blocks(S, ATTN_UNROLL_FWD, blocks)

            if d == 16:
                _order16_to_4(bo, 0, to)
                _order16_to_4(bl, 0, tl)
            src_o, src_l = (to, tl) if d == 16 else (bo, bl)

            def merge(rows, set_rows, off, d=d, src_o=src_o, src_l=src_l):
                blk = pl.ds(off, ATTN_BLOCK)
                o2, l2 = src_o[blk, :], src_l[blk, :]
                if d != DILATIONS[0]:
                    lo, oo = rows(l_ref), rows(o_ref)
                    ln = jnp.maximum(lo, l2)
                    wa, wb = jnp.exp(lo - ln), jnp.exp(l2 - ln)
                    o2 = (wa * oo + wb * o2) / (wa + wb)
                    l2 = ln + jnp.log(wa + wb)
                set_rows(o_ref, o2)
                set_rows(l_ref, l2)

            _for_residue_blocks(S, min(d, 4), merge)

    slab = pl.BlockSpec((S, LANES), lambda p: (0, p))
    f32_slab, bf16_slab = pltpu.VMEM((S, LANES), F32), pltpu.VMEM((S, LANES), BF16)
    bf16_window = pltpu.VMEM((S + ATTN_BLOCK, LANES), BF16)
    return pl.pallas_call(
        body, name=name, grid=(ATTN_W // LANES,), in_specs=[pl.BlockSpec((S, PAIR_W), lambda p: (0, p))],
        out_specs=[slab, slab],
        out_shape=[jax.ShapeDtypeStruct((S, ATTN_W), F32), jax.ShapeDtypeStruct((S, ATTN_W), F32)],
        scratch_shapes=[f32_slab] * 3 + [bf16_slab, bf16_window, bf16_window] + [f32_slab] * 4,
        compiler_params=_cparams("parallel"),
    )(proj_a)


def _attn_bwd(proj_a, do, lse, delta, *, name):
    S = proj_a.shape[0]

    def body(p_ref, do_ref, lse_ref, dl_ref, o_ref, qs, ks, vs, dqs, dks, dvs, bq, bk, bv, bdo, blse, bdl, bdq, bdk, bdv,
             tmp):
        _split_pair(p_ref, qs, ks, vs, bk, bv)
        bdk[0:ATTN_BLOCK, :] = jnp.zeros((ATTN_BLOCK, LANES), F32)
        bdv[0:ATTN_BLOCK, :] = jnp.zeros((ATTN_BLOCK, LANES), F32)
        for d in DILATIONS:
            nb = S // (ATTN_BLOCK * d)
            _regroup(S, d, ((qs, bq, 0), (ks, bk, ATTN_BLOCK), (vs, bv, ATTN_BLOCK), (do_ref, bdo, 0),
                            (lse_ref, blse, 0), (dl_ref, bdl, 0)), tmp)

            def blocks(group, nb=nb):
                lane = lax.broadcasted_iota(jnp.int32, (1, LANES), 1)
                heads = [(lane >= HEAD_DIM * e) & (lane < HEAD_DIM * (e + 1)) for e in range(LANES // HEAD_DIM)]
                zero = jnp.zeros((ATTN_BLOCK, LANES), BF16)
                chains = [(off, b, e, mh) for off, b in group for e, mh in enumerate(heads)]
                qm = [jnp.where(mh, bq[pl.ds(off, ATTN_BLOCK), :], zero) for off, _, _, mh in chains]
                dom = [jnp.where(mh, bdo[pl.ds(off, ATTN_BLOCK), :], zero) for off, _, _, mh in chains]
                s = [_nt(x, bk[pl.ds(off, 2 * ATTN_BLOCK), :]) for x, (off, _, _, _) in zip(qm, chains)]
                dp = [_nt(x, bv[pl.ds(off, 2 * ATTN_BLOCK), :]) for x, (off, _, _, _) in zip(dom, chains)]
                p, ds = [], []
                for (off, b, e, _), sc, dpc in zip(chains, s, dp):
                    valid, _ = _attn_masks(jnp.bitwise_and(b, nb - 1) == 0)
                    blk = pl.ds(off, ATTN_BLOCK)
                    pc = jnp.where(valid, jnp.exp(sc - _head_value(blse[blk, :], lane, e)), 0.0)
                    ds.append((pc * (dpc - _head_value(bdl[blk, :], lane, e))).astype(BF16))
                    p.append(pc.astype(BF16))
                dq = [_nn(x, bk[pl.ds(off, 2 * ATTN_BLOCK), :]) for x, (off, _, _, _) in zip(ds, chains)]
                dk = [_tn(x, y) for x, y in zip(ds, qm)]
                dv = [_tn(x, y) for x, y in zip(p, dom)]
                nh = len(heads)
                for u, (off, _) in enumerate(group):
                    dq2 = jnp.zeros((ATTN_BLOCK, LANES), F32)
                    for mh, x in zip(heads, dq[nh * u:nh * (u + 1)]):
                        dq2 = jnp.where(mh, x, dq2)
                    bdq[pl.ds(off, ATTN_BLOCK), :] = dq2 * ATTN_SCALE
                    for acc, grads in ((bdk, dk), (bdv, dv)):
                        win_grad = sum(grads[nh * u + 1:nh * (u + 1)], grads[nh * u])
                        acc[pl.ds(off, ATTN_BLOCK), :] += win_grad[:ATTN_BLOCK]
                        acc[pl.ds(off + ATTN_BLOCK, ATTN_BLOCK), :] = win_grad[ATTN_BLOCK:]

            _for_blocks(S, ATTN_UNROLL_BWD, blocks)

            outs = ((dqs, bdq, 0), (dks, bdk, ATTN_BLOCK), (dvs, bdv, ATTN_BLOCK))
            if d == 16:
                for acc, grad, pad in outs:
                    _order16_to_4(grad, pad, tmp)

                    def add(rows, set_rows, off, acc=acc):
                        set_rows(acc, rows(acc) + tmp[pl.ds(off, ATTN_BLOCK), :])

                    _for_residue_blocks(S, 4, add)
            else:
                def scatter(rows, set_rows, off, d=d):
                    for acc, grad, pad in outs:
                        part = grad[pl.ds(pad + off, ATTN_BLOCK), :]
                        set_rows(acc, part if d == DILATIONS[0] else rows(acc) + part)

                _for_residue_blocks(S, d, scatter)
        o_ref[:, 0:LANES] = dqs[...].astype(BF16)
        o_ref[:, LANES:2 * LANES] = dks[...].astype(BF16)
        o_ref[:, 2 * LANES:3 * LANES] = dvs[...].astype(BF16)

    slab = pl.BlockSpec((S, LANES), lambda p: (0, p))
    pair = pl.BlockSpec((S, PAIR_W), lambda p: (0, p))
    f32_slab, bf16_slab = pltpu.VMEM((S, LANES), F32), pltpu.VMEM((S, LANES), BF16)
    f32_window, bf16_window = pltpu.VMEM((S + ATTN_BLOCK, LANES), F32), pltpu.VMEM((S + ATTN_BLOCK, LANES), BF16)
    return pl.pallas_call(
        body, name=name, grid=(ATTN_W // LANES,), in_specs=[pair, slab, slab, slab], out_specs=pair,
        out_shape=jax.ShapeDtypeStruct(proj_a.shape, BF16),
        scratch_shapes=[f32_slab] * 6 + [bf16_slab, bf16_window, bf16_window, bf16_slab, f32_slab, f32_slab,
                                         f32_slab, f32_window, f32_window, f32_slab],
        compiler_params=_cparams("parallel"),
    )(proj_a, do, lse, delta)


HG_T = 2 * HGRN_CHUNK
HG_GROUPS = 2
HG_STEP = HG_GROUPS * HG_T


def _hgrn_consts():
    row = lax.broadcasted_iota(jnp.int32, (HG_T, HG_T), 0)
    col = lax.broadcasted_iota(jnp.int32, (HG_T, HG_T), 1)
    same = (row >= HGRN_CHUNK) == (col >= HGRN_CHUNK)
    return row, same & (col <= row), same & (col >= row)


def _lower_bound(logits_ref):
    l0, l1 = logits_ref[0:1, :], logits_ref[1:2, :]
    mx = jnp.maximum(l0, l1)
    e0, e1 = jnp.exp(l0 - mx), jnp.exp(l1 - mx)
    return e0 / (e0 + e1)


def _hgrn_chains():
    chains = [(g, h) for g in range(HG_GROUPS) for h in range(HGRN_HEADS)]
    rows = [pl.ds(HG_T * g, HG_T) for g, _ in chains]
    lanes = [slice(HGRN_DIM * h, HGRN_DIM * (h + 1)) for _, h in chains]
    return chains, rows, lanes


def _hgrn_gates(qs, fs, lbs, row, causal):
    C = HGRN_CHUNK
    tri = jnp.where(causal, 1.0, 0.0).astype(F32)
    sgs = [_sigmoid(f) for f in fs]
    forgets = [lb + (1.0 - lb) * sg for lb, sg in zip(lbs, sgs)]
    logfs = [jnp.log(forget) for forget in forgets]
    bs = [_nn(tri, logf, HIGHEST) for logf in logfs]
    out = []
    for q, sg, forget, logf, b in zip(qs, sgs, forgets, logfs, bs):
        key = 1.0 - forget
        bend0 = jnp.sum(logf[:C], axis=0, keepdims=True)
        bend1 = jnp.sum(logf[C:], axis=0, keepdims=True)
        bend = jnp.where(row < C, bend0, bend1)
        eb, emb, eend = jnp.exp(b), jnp.exp(-b), jnp.exp(bend - b)
        sq = _sigmoid(q)
        out.append(dict(sg=sg, forget=forget, key=key, bend0=bend0, bend1=bend1, eb=eb, emb=emb, eend=eend, sq=sq,
                        qd=q * sq * eb, ki=key * emb, ke=key * eend))
    return out


def _hgrn_fwd(proj, proj_f, logits, *, name):
    S = proj.shape[0]
    W, C = HGRN_W, HGRN_CHUNK

    def body(q_ref, f_ref, i_ref, lg_ref, rec_ref, st_ref, s_ref):
        @pl.when(pl.program_id(0) == 0)
        def _():
            s_ref[...] = jnp.zeros_like(s_ref)

        row, causal, _ = _hgrn_consts()
        lb_all = _lower_bound(lg_ref)
        chains, rows, lanes = _hgrn_chains()
        n = range(len(chains))
        gts = _hgrn_gates([q_ref[rows[c], lanes[c]].astype(F32) for c in n], [f_ref[rows[c], lanes[c]] for c in n],
                          [lb_all[:, lanes[c]] for c in n], row, causal)
        qd, ki, ke = ([gt[k].astype(BF16) for gt in gts] for k in ("qd", "ki", "ke"))
        iv = [i_ref[rows[c], lanes[c]].astype(BF16) for c in n]
        a = [_nt(qd[c], ki[c]) for c in n]
        u0 = [_tn(iv[c][:C], ke[c][:C]) for c in n]
        u1 = [_tn(iv[c][C:], ke[c][C:]) for c in n]
        state = [s_ref[h] for h in range(HGRN_HEADS)]
        s0, s1 = [], []
        for c, (g, h) in enumerate(chains):
            s0.append(state[h])
            s1.append(jnp.exp(gts[c]["bend0"]) * s0[c] + u0[c])
            state[h] = jnp.exp(gts[c]["bend1"]) * s1[c] + u1[c]
        o0 = [_nt(qd[c][:C], s0[c].astype(BF16)) for c in n]
        o1 = [_nt(qd[c][C:], s1[c].astype(BF16)) for c in n]
        o = [_nn(jnp.where(causal, a[c], 0.0).astype(BF16), iv[c]) for c in n]
        for c, (g, h) in enumerate(chains):
            st_ref[2 * g, h] = s0[c]
            st_ref[2 * g + 1, h] = s1[c]
            rec_ref[rows[c], lanes[c]] = o[c] + jnp.concatenate([o0[c], o1[c]], axis=0)
        for h in range(HGRN_HEADS):
            s_ref[h] = state[h]

    blk = lambda j: pl.BlockSpec((HG_STEP, W), lambda t: (t, j))
    return pl.pallas_call(
        body, name=name, grid=(S // HG_STEP,),
        in_specs=[blk(0), blk(0), blk(2), pl.BlockSpec((2, W), lambda t: (0, 0))],
        out_specs=[blk(0), pl.BlockSpec((2 * HG_GROUPS, HGRN_HEADS, HGRN_DIM, HGRN_DIM), lambda t: (t, 0, 0, 0))],
        out_shape=[jax.ShapeDtypeStruct((S, W), F32),
                   jax.ShapeDtypeStruct((S // C, HGRN_HEADS, HGRN_DIM, HGRN_DIM), F32)],
        scratch_shapes=[pltpu.VMEM((HGRN_HEADS, HGRN_DIM, HGRN_DIM), F32)],
        compiler_params=_cparams("arbitrary"),
    )(proj, proj_f, proj, logits)


def _hgrn_bwd(proj, proj_f, logits, states, drec, *, name):
    S = proj.shape[0]
    W, C = HGRN_W, HGRN_CHUNK
    nt = S // HG_STEP

    def body(q_ref, f_ref, i_ref, lg_ref, st_ref, do_ref, dp_ref, dlg_ref, ds_ref, dlb_ref):
        t = pl.program_id(0)

        @pl.when(t == 0)
        def _():
            ds_ref[...] = jnp.zeros_like(ds_ref)
            dlb_ref[...] = jnp.zeros_like(dlb_ref)

        row, causal, anti = _hgrn_consts()
        lb_all = _lower_bound(lg_ref)
        chains, rows, lanes = _hgrn_chains()
        n = range(len(chains))
        qs, lbs = [q_ref[rows[c], lanes[c]].astype(F32) for c in n], [lb_all[:, lanes[c]] for c in n]
        gts = _hgrn_gates(qs, [f_ref[rows[c], lanes[c]] for c in n], lbs, row, causal)
        qd, ki, ke = ([gt[k] for gt in gts] for k in ("qd", "ki", "ke"))
        qdb, kib, keb = ([x.astype(BF16) for x in xs] for xs in (qd, ki, ke))
        iv = [i_ref[rows[c], lanes[c]].astype(BF16) for c in n]
        dob = [do_ref[rows[c], lanes[c]].astype(BF16) for c in n]
        s0 = [st_ref[2 * g, h] for g, h in chains]
        s1 = [st_ref[2 * g + 1, h] for g, h in chains]
        dec0, dec1 = [jnp.exp(gt["bend0"]) for gt in gts], [jnp.exp(gt["bend1"]) for gt in gts]
        a = [_nt(qdb[c], kib[c]) for c in n]
        da = [_nt(dob[c], iv[c]) for c in n]
        dqd1 = [_nn(dob[c][C:], s1[c].astype(BF16)) for c in n]
        dqd0 = [_nn(dob[c][:C], s0[c].astype(BF16)) for c in n]
        t1 = [_tn(dob[c][C:], qdb[c][C:]) for c in n]
        t0 = [_tn(dob[c][:C], qdb[c][:C]) for c in n]
        carry = [ds_ref[h] for h in range(HGRN_HEADS)]
        ds1, ds0 = [None] * len(chains), [None] * len(chains)
        for c in reversed(n):
            h = chains[c][1]
            ds1[c] = carry[h]
            ds0[c] = dec1[c] * ds1[c] + t1[c]
            carry[h] = dec0[c] * ds0[c] + t0[c]
        for h in range(HGRN_HEADS):
            ds_ref[h] = carry[h]
        ds1b, ds0b = [x.astype(BF16) for x in ds1], [x.astype(BF16) for x in ds0]
        a = [jnp.where(causal, x, 0.0).astype(BF16) for x in a]
        da = [jnp.where(causal, x, 0.0).astype(BF16) for x in da]
        di1 = [_nt(keb[c][C:], ds1b[c]) for c in n]
        dke1 = [_nn(iv[c][C:], ds1b[c]) for c in n]
        di0 = [_nt(keb[c][:C], ds0b[c]) for c in n]
        dke0 = [_nn(iv[c][:C], ds0b[c]) for c in n]
        dqd_a = [_nn(da[c], kib[c]) for c in n]
        dki = [_tn(da[c], qdb[c]) for c in n]
        di_a = [_tn(a[c], dob[c]) for c in n]
        dqd, dke, db = [], [], []
        for c in n:
            ddec1 = jnp.sum(ds1[c] * s1[c], axis=0, keepdims=True)
            ddec0 = jnp.sum(ds0[c] * s0[c], axis=0, keepdims=True)
            dqd.append(dqd_a[c] + jnp.concatenate([dqd0[c], dqd1[c]], axis=0))
            h = chains[c][1]
            dp_ref[rows[c], 2 * W + HGRN_DIM * h:2 * W + HGRN_DIM * (h + 1)] = (
                di_a[c] + jnp.concatenate([di0[c], di1[c]], axis=0)).astype(BF16)
            dke.append(jnp.concatenate([dke0[c], dke1[c]], axis=0))
            gke = dke[c] * ke[c]
            dbend0 = jnp.sum(gke[:C], axis=0, keepdims=True) + ddec0 * dec0[c]
            dbend1 = jnp.sum(gke[C:], axis=0, keepdims=True) + ddec1 * dec1[c]
            dbc = dqd[c] * qd[c] - dki[c] * ki[c] - gke
            db.append(dbc + jnp.where(row == C - 1, dbend0, 0.0) + jnp.where(row == HG_T - 1, dbend1, 0.0))
        tri = jnp.where(anti, 1.0, 0.0).astype(F32)
        dlogf = [_nn(tri, db[c], HIGHEST) for c in n]
        for c in n:
            gt, lb, q, h = gts[c], lbs[c], qs[c], chains[c][1]
            dforget = dlogf[c] / gt["forget"] - (dki[c] * gt["emb"] + dke[c] * gt["eend"])
            sg, sq = gt["sg"], gt["sq"]
            dp_ref[rows[c], W + HGRN_DIM * h:W + HGRN_DIM * (h + 1)] = (
                dforget * (1.0 - lb) * sg * (1.0 - sg)).astype(BF16)
            dlb_ref[:, lanes[c]] += jnp.sum(dforget * (1.0 - sg), axis=0, keepdims=True)
            dp_ref[rows[c], lanes[c]] = (dqd[c] * gt["eb"] * sq * (1.0 + q * (1.0 - sq))).astype(BF16)

        @pl.when(t == nt - 1)
        def _():
            dl0 = dlb_ref[...] * lb_all * (1.0 - lb_all)
            dlg_ref[0:1, :] = dl0
            dlg_ref[1:2, :] = -dl0

    blk = lambda j: pl.BlockSpec((HG_STEP, W), lambda t: (nt - 1 - t, j))
    full = pl.BlockSpec((2, W), lambda t: (0, 0))
    return pl.pallas_call(
        body, name=name, grid=(nt,),
        in_specs=[blk(0), blk(0), blk(2), full,
                  pl.BlockSpec((2 * HG_GROUPS, HGRN_HEADS, HGRN_DIM, HGRN_DIM), lambda t: (nt - 1 - t, 0, 0, 0)), blk(0)],
        out_specs=[pl.BlockSpec((HG_STEP, 3 * W), lambda t: (nt - 1 - t, 0)), full],
        out_shape=[jax.ShapeDtypeStruct((S, 3 * W), BF16), jax.ShapeDtypeStruct((2, W), F32)],
        scratch_shapes=[pltpu.VMEM((HGRN_HEADS, HGRN_DIM, HGRN_DIM), F32), pltpu.VMEM((1, W), F32)],
        compiler_params=_cparams("arbitrary"),
    )(proj, proj_f, proj, logits, states, drec)


def _out_proj(attn, rec, proj_h, x, g_attn, g_hgrn, g_norm2, w_out, *, name, tm=512):
    S, D = x.shape
    AW, W = ATTN_W, HGRN_W

    def body(a_ref, r_ref, hg_ref, x_ref, ga_ref, gh_ref, g2_ref, w_ref, h_ref, u_ref, m_ref):
        av = a_ref[...]
        m_ref[:, :AW] = (av * _rstd(av) * ga_ref[...]).astype(BF16)
        for h in range(HGRN_HEADS):
            sl = slice(HGRN_DIM * h, HGRN_DIM * (h + 1))
            rv, hg = r_ref[:, sl], hg_ref[:, sl].astype(F32)
            m_ref[:, AW + HGRN_DIM * h:AW + HGRN_DIM * (h + 1)] = (
                (rv * _rstd(rv) * gh_ref[:, sl]) * (hg * _sigmoid(hg))).astype(BF16)
        h1 = x_ref[...] + _nn(m_ref[...], w_ref[...])
        h_ref[...] = h1
        u_ref[...] = (h1 * _rstd(h1) * g2_ref[...]).astype(BF16)

    row = lambda w, j=0: pl.BlockSpec((tm, w), lambda i: (i, j))
    vec = lambda w: pl.BlockSpec((1, w), lambda i: (0, 0))
    return pl.pallas_call(
        body, name=name, grid=(S // tm,),
        in_specs=[row(AW), row(W), row(W, 3), row(D), vec(AW), vec(W), vec(D), _resident(w_out.shape)],
        out_specs=[row(D), row(D), row(AW + W)],
        out_shape=[jax.ShapeDtypeStruct((S, D), F32), jax.ShapeDtypeStruct((S, D), BF16),
                   jax.ShapeDtypeStruct((S, AW + W), BF16)],
        compiler_params=_cparams("parallel"),
    )(attn, rec, proj_h, x, g_attn, g_hgrn, g_norm2, w_out)


def _dmix_post_bwd(dh1b, w_out, attn, rec, proj_h, g_attn, g_hgrn, *, name, tm=512):
    S, D = dh1b.shape
    AW, W = ATTN_W, HGRN_W

    def body(dh_ref, w_ref, a_ref, r_ref, hg_ref, ga_ref, gh_ref, do_ref, dl_ref, dr_ref, dhg_ref, dga_ref, dgh_ref):
        first = pl.program_id(0) == 0
        dmix = _nt(dh_ref[...], w_ref[...])
        av = a_ref[...]
        dov, dga = _norm_bwd(av, ga_ref[...], dmix[:, :AW])
        do_ref[...] = dov
        shift = HEAD_DIM.bit_length() - 1
        hi = lax.shift_right_logical(lax.broadcasted_iota(jnp.int32, (AW, AW), 0), shift)
        hj = lax.shift_right_logical(lax.broadcasted_iota(jnp.int32, (AW, AW), 1), shift)
        prod = dov * av
        hi_part = prod.astype(BF16)
        lo_part = (prod - hi_part.astype(F32)).astype(BF16)
        same_head = jnp.where(hi == hj, 1.0, 0.0).astype(BF16)
        dl_ref[...] = _nn(hi_part, same_head) + _nn(lo_part, same_head)
        _accumulate(dga_ref, jnp.sum(dga, axis=0, keepdims=True), first)

        @pl.when(first)
        def _():
            dgh_ref[...] = jnp.zeros_like(dgh_ref)

        for h in range(HGRN_HEADS):
            sl = slice(HGRN_DIM * h, HGRN_DIM * (h + 1))
            rv, hg, gv = r_ref[:, sl], hg_ref[:, sl].astype(F32), gh_ref[:, sl]
            dout = dmix[:, AW + HGRN_DIM * h:AW + HGRN_DIM * (h + 1)]
            sg = _sigmoid(hg)
            drv, dgh = _norm_bwd(rv, gv, dout * (hg * sg))
            dr_ref[:, sl] = drv
            dgh_ref[:, sl] += jnp.sum(dgh, axis=0, keepdims=True)
            dhg_ref[:, sl] = (dout * (rv * _rstd(rv) * gv) * (sg * (1.0 + hg * (1.0 - sg)))).astype(BF16)

    row = lambda w, j=0: pl.BlockSpec((tm, w), lambda i: (i, j))
    vec = lambda w: pl.BlockSpec((1, w), lambda i: (0, 0))
    return pl.pallas_call(
        body, name=name, grid=(S // tm,),
        in_specs=[row(D), _resident(w_out.shape), row(AW), row(W), row(W, 3), vec(AW), vec(W)],
        out_specs=[row(AW), row(AW), row(W), row(W), vec(AW), vec(W)],
        out_shape=[jax.ShapeDtypeStruct((S, AW), F32), jax.ShapeDtypeStruct((S, AW), F32),
                   jax.ShapeDtypeStruct((S, W), F32), jax.ShapeDtypeStruct((S, W), BF16),
                   jax.ShapeDtypeStruct((1, AW), F32), jax.ShapeDtypeStruct((1, W), F32)],
        compiler_params=_cparams("arbitrary"),
    )(dh1b, w_out, attn, rec, proj_h, g_attn, g_hgrn)


def _conv_act(g, g1, g2, w_ref, b_ref):
    c = b_ref[...] + w_ref[0:1, :] * g2 + w_ref[1:2, :] * g1 + w_ref[2:3, :] * g
    return c, 0.5 * (1.0 + lax.erf(c * (2.0 ** -0.5)))


def _shift_down(g, halo, row):
    g1 = jnp.where(row == 0, halo[7:8], pltpu.roll(g, 1, 0))
    g2 = jnp.where(row == 0, halo[6:7], jnp.where(row == 1, halo[7:8], pltpu.roll(g, 2, 0)))
    return g1, g2


def _shift_up(x, halo, row):
    n = x.shape[0]
    x1 = jnp.where(row == n - 1, halo[0:1], pltpu.roll(x, n - 1, 0))
    x2 = jnp.where(row == n - 2, halo[0:1], jnp.where(row == n - 1, halo[1:2], pltpu.roll(x, n - 2, 0)))
    return x1, x2


def _up_glu(u, wt_up, conv_w, conv_b, *, name, tm=1024, tn=1408):
    S, D = u.shape
    F = wt_up.shape[0] // 2
    tn = _tile(F, tn)
    nf = F // tn

    def body(u_ref, wg_ref, wv_ref, cw_ref, cb_ref, g_ref, ge_ref, t_ref, a_ref, halo_ref):
        i, j = pl.program_id(0), pl.program_id(1)

        @pl.when(i == 0)
        def _():
            halo_ref[j] = jnp.zeros((SUBLANES, tn), F32)

        uv = u_ref[...]
        g, v = _nt(uv, wg_ref[...]), _nt(uv, wv_ref[...])
        row = lax.broadcasted_iota(jnp.int32, (tm, tn), 0)
        g1, g2 = _shift_down(g, halo_ref[j], row)
        c, cdf = _conv_act(g, g1, g2, cw_ref, cb_ref)
        gelu = c * cdf
        pdf = jnp.exp(-0.5 * c * c) * (1.0 / (2.0 * jnp.pi) ** 0.5)
        a_ref[...] = (gelu * v).astype(BF16)
        g_ref[...] = g.astype(BF16)
        ge_ref[...] = gelu.astype(BF16)
        t_ref[...] = (v * (cdf + c * pdf)).astype(BF16)
        halo_ref[j] = g[tm - SUBLANES:, :]

    col = pl.BlockSpec((tm, tn), lambda i, j: (i, j))
    out = jax.ShapeDtypeStruct((S, F), BF16)
    return pl.pallas_call(
        body, name=name, grid=(S // tm, nf),
        in_specs=[pl.BlockSpec((tm, D), lambda i, j: (i, 0)), pl.BlockSpec((tn, D), lambda i, j: (j, 0)),
                  pl.BlockSpec((tn, D), lambda i, j: (j + nf, 0)), pl.BlockSpec((3, tn), lambda i, j: (0, j)),
                  pl.BlockSpec((1, tn), lambda i, j: (0, j))],
        out_specs=[col, col, col, col], out_shape=[out, out, out, out],
        scratch_shapes=[pltpu.VMEM((nf, SUBLANES, tn), F32)], compiler_params=_cparams("arbitrary", "arbitrary"),
    )(u, wt_up, wt_up, conv_w, conv_b)


def _dact_glu_bwd(dh2b, w_down, gate, gelu, vslope, conv_w, *, name, tm=1024, tn=1408):
    S, D = dh2b.shape
    F = gate.shape[1]
    tn = _tile(F, tn)
    nf, ni = F // tn, S // tm

    def body(dh_ref, wd_ref, g_ref, ge_ref, t_ref, cw_ref, dg_ref, dv_ref, dcw_ref, dcb_ref, halo_ref, acc_ref):
        i, j = pl.program_id(0), pl.program_id(1)

        @pl.when(i == 0)
        def _():
            halo_ref[j] = jnp.zeros((SUBLANES, tn), F32)
            acc_ref[j] = jnp.zeros((SUBLANES, tn), F32)

        g = g_ref[...].astype(F32)
        row = lax.broadcasted_iota(jnp.int32, (tm, tn), 0)
        da = _nt(dh_ref[...], wd_ref[...])
        dv_ref[...] = (da * ge_ref[...].astype(F32)).astype(BF16)
        dc = da * t_ref[...].astype(F32)
        d1, d2 = _shift_up(dc, halo_ref[j], row)
        dg_ref[...] = (cw_ref[2:3, :] * dc + cw_ref[1:2, :] * d1 + cw_ref[0:1, :] * d2).astype(BF16)
        halo_ref[j] = dc[:SUBLANES, :]
        for k, t in enumerate((d2 * g, d1 * g, dc * g, dc)):
            acc_ref[j, k:k + 1, :] += jnp.sum(t, axis=0, keepdims=True)

        @pl.when((i == ni - 1) & (j == nf - 1))
        def _():
            for jj in range(nf):
                dcw_ref[:, jj * tn:(jj + 1) * tn] = acc_ref[jj, 0:3, :]
                dcb_ref[:, jj * tn:(jj + 1) * tn] = acc_ref[jj, 3:4, :]

    tile = pl.BlockSpec((tm, tn), lambda i, j: (ni - 1 - i, j))
    return pl.pallas_call(
        body, name=name, grid=(ni, nf),
        in_specs=[pl.BlockSpec((tm, D), lambda i, j: (ni - 1 - i, 0)), pl.BlockSpec((tn, D), lambda i, j: (j, 0)),
                  tile, tile, tile, pl.BlockSpec((3, tn), lambda i, j: (0, j))],
        out_specs=[tile, tile, pl.BlockSpec((3, F), lambda i, j: (0, 0)), pl.BlockSpec((1, F), lambda i, j: (0, 0))],
        out_shape=[jax.ShapeDtypeStruct((S, F), BF16), jax.ShapeDtypeStruct((S, F), BF16),
                   jax.ShapeDtypeStruct((3, F), F32), jax.ShapeDtypeStruct((1, F), F32)],
        scratch_shapes=[pltpu.VMEM((nf, SUBLANES, tn), F32), pltpu.VMEM((nf, SUBLANES, tn), F32)],
        compiler_params=_cparams("arbitrary", "arbitrary"),
    )(dh2b, w_down, gate, gelu, vslope, conv_w)


def _down_loss(act, w_down, h1, g, target, *, name, tm=512):
    S, F = act.shape
    D = h1.shape[1]

    def body(a_ref, w_ref, h_ref, g_ref, t_ref, dh_ref, dhb_ref, dg_ref, loss_ref):
        first = pl.program_id(0) == 0
        h2 = h_ref[...] + _nn(a_ref[...], w_ref[...])
        gv = g_ref[...]
        r = _rstd(h2)
        xh = h2 * r
        err = xh * gv - t_ref[...]
        part_loss = 0.5 * jnp.sum(jnp.mean(err * err, axis=-1, keepdims=True), axis=0, keepdims=True)
        dy = err * (1.0 / D)
        dxh = dy * gv
        dh = r * (dxh - xh * jnp.mean(dxh * xh, axis=-1, keepdims=True))
        dh_ref[...] = dh
        dhb_ref[...] = dh.astype(BF16)
        _accumulate(dg_ref, jnp.sum(dy * xh, axis=0, keepdims=True), first)
        _accumulate(loss_ref, jnp.broadcast_to(part_loss, (1, LANES)), first)

    row = lambda w: pl.BlockSpec((tm, w), lambda i: (i, 0))
    vec = lambda w: pl.BlockSpec((1, w), lambda i: (0, 0))
    return pl.pallas_call(
        body, name=name, grid=(S // tm,), in_specs=[row(F), _resident(w_down.shape), row(D), vec(D), row(D)],
        out_specs=[row(D), row(D), vec(D), vec(LANES)],
        out_shape=[jax.ShapeDtypeStruct((S, D), F32), jax.ShapeDtypeStruct((S, D), BF16),
                   jax.ShapeDtypeStruct((1, D), F32), jax.ShapeDtypeStruct((1, LANES), F32)],
        compiler_params=_cparams("arbitrary"),
    )(act, w_down, h1, g, target)


def _grad_norm_input(pieces, ws, x, g, add, *, name, tm=512):
    S, D = x.shape
    widths = [p.shape[1] for p in pieces]
    n, nw = len(pieces), len(ws)
    where, wi, off = [], 0, ws[0][1]
    for wd in widths:
        if off == ws[wi][0].shape[0]:
            wi, off = wi + 1, ws[wi + 1][1]
        where.append((wi, off))
        off += wd
    ws = [w for w, _ in ws]

    def body(*refs):
        p_refs, w_refs = refs[:n], refs[n:n + nw]
        x_ref, g_ref, add_ref, dx_ref, dxb_ref, dg_ref = refs[n + nw:]
        halves = _row_halves(tm)
        du = []
        for rows in halves:
            terms = [_nn(p_refs[k][rows, :], w_refs[wi][off:off + widths[k], :]) for k, (wi, off) in enumerate(where)]
            du.append(sum(terms[1:], terms[0]))
        dg_sum = None
        for rows, duh in zip(halves, du):
            dx, dg = _norm_bwd(x_ref[rows, :], g_ref[...], duh)
            dx = add_ref[rows, :] + dx
            dx_ref[rows, :] = dx
            dxb_ref[rows, :] = dx.astype(BF16)
            part = jnp.sum(dg, axis=0, keepdims=True)
            dg_sum = part if dg_sum is None else dg_sum + part
        _accumulate(dg_ref, dg_sum, pl.program_id(0) == 0)

    row = lambda w_: pl.BlockSpec((tm, w_), lambda i: (i, 0))
    vec = pl.BlockSpec((1, D), lambda i: (0, 0))
    return pl.pallas_call(
        body, name=name, grid=(S // tm,),
        in_specs=[row(wd) for wd in widths] + [_resident(w.shape) for w in ws] + [row(D), vec, row(D)],
        out_specs=[row(D), row(D), vec],
        out_shape=[jax.ShapeDtypeStruct((S, D), F32), jax.ShapeDtypeStruct((S, D), BF16),
                   jax.ShapeDtypeStruct((1, D), F32)],
        compiler_params=_cparams("arbitrary"),
    )(*pieces, *ws, x, g, add)


def _rows(a):
    return a.reshape(-1, a.shape[-1])


def _row_tile(rows, cols, itemsize=4, budget=1 << 20):
    t = rows
    while t % 32 == 0 and t * cols * itemsize > budget:
        t //= 2
    return t


def _sum_cast(arrs, out_dtype, *, name):
    shape = arrs[0].shape
    flat = [_rows(a) for a in arrs]
    R, C = flat[0].shape
    tr = _row_tile(R, C)

    def body(*refs):
        acc = refs[0][...].astype(F32)
        for r in refs[1:-1]:
            acc = acc + r[...].astype(F32)
        refs[-1][...] = acc.astype(out_dtype)

    spec = pl.BlockSpec((tr, C), lambda i: (i, 0))
    return pl.pallas_call(
        body, name=name, grid=(R // tr,), in_specs=[spec] * len(flat), out_specs=spec,
        out_shape=jax.ShapeDtypeStruct((R, C), out_dtype), compiler_params=_cparams("parallel"),
    )(*flat).reshape(shape)


def _adamw(parts, w, m, v, *, name):
    shape = w.shape
    w2, m2, v2 = _rows(w), _rows(m), _rows(v)
    R, C = w2.shape
    parts = [p.reshape(-1, R, C) for p in parts]
    tr = _row_tile(R, C)
    np_ = len(parts)
    c1, c2 = 1.0 - ADAM_B1 ** ADAM_STEP, 1.0 - ADAM_B2 ** ADAM_STEP

    def body(*refs):
        terms = [(r, k) for r in refs[:np_] for k in range(r.shape[0])]
        g = terms[0][0][terms[0][1]].astype(F32)
        for r, k in terms[1:]:
            g = g + r[k].astype(F32)
        w_ref, m_ref, v_ref, g_out, d_out, m_out, v_out = refs[np_:]
        mn = ADAM_B1 * m_ref[...] + (1.0 - ADAM_B1) * g
        vn = ADAM_B2 * v_ref[...] + (1.0 - ADAM_B2) * (g * g)
        g_out[...] = g
        d_out[...] = -ADAM_LR * ((mn / c1) / (jnp.sqrt(vn / c2) + ADAM_EPS) + ADAM_WD * w_ref[...])
        m_out[...] = mn
        v_out[...] = vn

    spec = pl.BlockSpec((tr, C), lambda i: (i, 0))
    out = jax.ShapeDtypeStruct((R, C), F32)
    stacks = [pl.BlockSpec((p.shape[0], tr, C), lambda i: (0, i, 0)) for p in parts]
    res = pl.pallas_call(
        body, name=name, grid=(R // tr,), in_specs=stacks + [spec] * 3, out_specs=[spec] * 4,
        out_shape=[out] * 4, compiler_params=_cparams("parallel"),
    )(*parts, w2, m2, v2)
    return [r.reshape(shape) for r in res]


def _adamw_packed(stack, widths, params, *, name):
    c1, c2 = 1.0 - ADAM_B1 ** ADAM_STEP, 1.0 - ADAM_B2 ** ADAM_STEP
    k = stack.shape[0]
    flat = [None if p is None else [_rows(a) for a in p] for p in params]
    n_in = sum(3 for p in flat if p is not None)

    def body(*refs):
        s_ref, ins, outs = refs[0], list(refs[1:1 + n_in]), list(refs[1 + n_in:])
        off = 0
        for width, p in zip(widths, flat):
            rows = 1 if p is None else p[0].shape[0]
            cols = width // rows
            w_ref, m_ref, v_ref = (None, None, None) if p is None else (ins.pop(0), ins.pop(0), ins.pop(0))
            o_refs = [outs.pop(0) for _ in range(1 if p is None else 4)]
            for r in range(rows):
                seg = slice(off + r * cols, off + (r + 1) * cols)
                g = s_ref[0, :, seg]
                for j in range(1, k):
                    g = g + s_ref[j, :, seg]
                o_refs[0][r:r + 1, :] = g
                if p is not None:
                    row = slice(r, r + 1)
                    mn = ADAM_B1 * m_ref[row, :] + (1.0 - ADAM_B1) * g
                    vn = ADAM_B2 * v_ref[row, :] + (1.0 - ADAM_B2) * (g * g)
                    o_refs[1][row, :] = -ADAM_LR * ((mn / c1) / (jnp.sqrt(vn / c2) + ADAM_EPS) + ADAM_WD * w_ref[row, :])
                    o_refs[2][row, :] = mn
                    o_refs[3][row, :] = vn
            off += width

    operands, out_shape = [stack], []
    for width, p in zip(widths, flat):
        if p is None:
            out_shape.append(jax.ShapeDtypeStruct((1, width), F32))
        else:
            operands += p
            out_shape += [jax.ShapeDtypeStruct(p[0].shape, F32)] * 4
    res = list(pl.pallas_call(body, name=name, out_shape=out_shape)(*operands))
    out = []
    for p, orig in zip(flat, params):
        n = 1 if p is None else 4
        out.append([r if orig is None else r.reshape(orig[0].shape) for r in res[:n]])
        res = res[n:]
    return out


def _coords():
    return lax.axis_index("x"), lax.axis_index("y"), lax.axis_index("c")


def _all_gather(shards, *, name):
    n = len(shards)

    def body(*refs):
        x_refs, out_refs = refs[:n], refs[n:2 * n]
        send_sems, recv_sems, local_sems = refs[2 * n:]
        x, y, c = _coords()
        me, sibling = (x, y, c), (x, y, 1 - c)
        chips = [(1 - x, y), (x, 1 - y), (1 - x, 1 - y)]

        def slot(a, dev):
            return out_refs[a].at[4 * dev[0] + 2 * dev[1] + dev[2]]

        def copy(a, k, block, to, src=None):
            return pltpu.make_async_remote_copy(
                src_ref=slot(a, block) if src is None else src, dst_ref=slot(a, block),
                send_sem=send_sems.at[7 * a + k], recv_sem=recv_sems.at[7 * a + k], device_id=to, device_id_type=MESH)

        mine = [pltpu.make_async_copy(x_refs[a], slot(a, me), local_sems.at[a]) for a in range(n)]
        for cp in mine:
            cp.start()
        first = []
        for a in range(n):
            first.append(copy(a, 0, me, sibling, src=x_refs[a]))
            first += [copy(a, 1 + j, me, (*chip, c), src=x_refs[a]) for j, chip in enumerate(chips)]
        for cp in first:
            cp.start()
        passed = []
        for j, chip in enumerate(chips):
            for a in range(n):
                copy(a, 1 + j, (*chip, c), me).wait_recv()
                fwd = copy(a, 4 + j, (*chip, c), sibling)
                fwd.start()
                passed.append(fwd)
        for a in range(n):
            copy(a, 0, sibling, me).wait_recv()
            for j, chip in enumerate(chips):
                copy(a, 4 + j, (*chip, 1 - c), me).wait_recv()
        for cp in first + passed:
            cp.wait_send()
        for cp in mine:
            cp.wait()

    return pl.pallas_call(
        body, name=name, in_specs=[HBM] * n, out_specs=[HBM] * n,
        out_shape=[jax.ShapeDtypeStruct((N_DEV, *s.shape), s.dtype) for s in shards],
        scratch_shapes=[pltpu.SemaphoreType.DMA((7 * n,)), pltpu.SemaphoreType.DMA((7 * n,)),
                        pltpu.SemaphoreType.DMA((n,))],
    )(*shards)


def _flip_y(x, y, c):
    return (x, 1 - y, c)


def _flip_x(x, y, c):
    return (1 - x, y, c)


def _flip_xy(x, y, c):
    return (1 - x, 1 - y, c)


SEM = pl.BlockSpec(memory_space=pltpu.SEMAPHORE)
SIDE_EFFECT = pltpu.SideEffectType.DATAFLOW_SIDE_EFFECTING


def _in_hbm(a):
    return pltpu.with_memory_space_constraint(a, pltpu.HBM)


def _copies_start(srcs, lands, plan, n_copies, *, name, after=None):
    ns, nl = len(srcs), len(lands)
    extra = [] if after is None else [after]

    def body(*refs):
        src_refs, land_refs = refs[:ns], refs[ns:ns + nl]
        send_sems, recv_sems = refs[ns + nl + len(extra):ns + nl + len(extra) + 2]
        token = refs[-1]
        for k, (src, dst, peer, _) in enumerate(plan(src_refs, land_refs, *_coords())):
            pltpu.make_async_remote_copy(src_ref=src, dst_ref=dst, send_sem=send_sems.at[k], recv_sem=recv_sems.at[k],
                                         device_id=peer, device_id_type=MESH).start()
        token[...] = jnp.zeros_like(token)

    bufs = [*srcs, *lands]
    res = pl.pallas_call(
        body, name=name, in_specs=[HBM] * (ns + nl) + [pl.BlockSpec(memory_space=pl.ANY)] * len(extra),
        out_specs=(SEM, SEM, *[HBM] * (ns + nl), pl.BlockSpec(memory_space=pltpu.VMEM)),
        out_shape=(pltpu.SemaphoreType.DMA((n_copies,)), pltpu.SemaphoreType.DMA((n_copies,)),
                   *[pltpu.HBM(b.shape, b.dtype) for b in bufs], jax.ShapeDtypeStruct((SUBLANES, LANES), F32)),
        input_output_aliases={i: 2 + i for i in range(ns + nl)},
        compiler_params=pltpu.CompilerParams(has_side_effects=SIDE_EFFECT),
    )(*[_in_hbm(b) for b in bufs], *extra)
    return res[0], res[1], list(res[2:2 + ns]), list(res[2 + ns:2 + ns + nl]), res[-1]


def _copies_wait(started, plan, after, *, name):
    send_sems, recv_sems, srcs, lands, _ = started
    ns, nl = len(srcs), len(lands)

    def body(*refs):
        src_refs, land_refs = refs[:ns], refs[ns:ns + nl]
        send_sems, recv_sems = refs[ns + nl:ns + nl + 2]
        for k, (src, dst, peer, here) in enumerate(plan(src_refs, land_refs, *_coords())):
            pltpu.make_async_remote_copy(src_ref=src, dst_ref=dst, send_sem=send_sems.at[k], recv_sem=recv_sems.at[k],
                                         device_id=peer, device_id_type=MESH).wait_send()
            pltpu.make_async_remote_copy(src_ref=src, dst_ref=here, send_sem=send_sems.at[k], recv_sem=recv_sems.at[k],
                                         device_id=peer, device_id_type=MESH).wait_recv()

    bufs = [*srcs, *lands]
    res = pl.pallas_call(
        body, name=name, in_specs=[HBM] * (ns + nl) + [SEM, SEM, pl.BlockSpec(memory_space=pl.ANY)],
        out_specs=[HBM] * (ns + nl), out_shape=[pltpu.HBM(b.shape, b.dtype) for b in bufs],
        input_output_aliases={i: i for i in range(ns + nl)},
        compiler_params=pltpu.CompilerParams(has_side_effects=SIDE_EFFECT),
    )(*bufs, send_sems, recv_sems, after)
    return list(res[ns:])


def _dev_index(dev):
    return 4 * dev[0] + 2 * dev[1] + dev[2]


def _ag_chips_plan(src_refs, land_refs, x, y, c):
    me = _dev_index((x, y, c))
    return [(src, land.at[me], peer, land.at[_dev_index(peer)])
            for src, land in zip(src_refs, land_refs) for peer in (_flip_y(x, y, c), _flip_x(x, y, c), _flip_xy(x, y, c))]


def _ag_sibling_plan(src_refs, land_refs, x, y, c):
    chips = [(x, y), (x, 1 - y), (1 - x, y), (1 - x, 1 - y)]
    return [(land.at[_dev_index((*chip, c))], land.at[_dev_index((*chip, c))], (x, y, 1 - c),
             land.at[_dev_index((*chip, 1 - c))]) for land in land_refs for chip in chips]


def _ag_direct_plan(src_refs, land_refs, x, y, c):
    me = _dev_index((x, y, c))
    plan = []
    for src, land in zip(src_refs, land_refs):
        for m in range(1, N_DEV):
            peer = (x + (m >> 2) * (1 - 2 * x), y + ((m >> 1) & 1) * (1 - 2 * y), c + (m & 1) * (1 - 2 * c))
            plan.append((src, land.at[me], peer, land.at[_dev_index(peer)]))
    return plan


def _rs_direct_plan(src_refs, land_refs, x, y, c):
    plan = []
    for src, land in zip(src_refs, land_refs):
        for m in range(1, N_DEV):
            peer = (x + (m >> 2) * (1 - 2 * x), y + ((m >> 1) & 1) * (1 - 2 * y), c + (m & 1) * (1 - 2 * c))
            plan.append((src.at[_dev_index(peer)], land.at[m - 1], peer, land.at[m - 1]))
    return plan


def _rs_start(grads, me, *, name, after=None):
    own = [lax.dynamic_index_in_dim(g, me, 0, keepdims=False) for g in grads]
    lands = [lax.empty((N_DEV - 1, *g.shape[1:]), g.dtype) for g in grads]
    return _copies_start(grads, lands, _rs_direct_plan, (N_DEV - 1) * len(grads), name=name, after=after), own


def _rs_finish(started, after, *, name):
    handle, own = started
    got = _copies_wait(handle, _rs_direct_plan, after, name=name)
    return [[o, land] for o, land in zip(own, got)]


def _gathered_cols(w8):
    return w8.transpose(1, 0, 2).reshape(w8.shape[1], -1)


def _pair_major(wt):
    return wt.reshape(3, ATTN_W // LANES, LANES, -1).transpose(1, 0, 2, 3).reshape(3 * ATTN_W, -1)


def kernel(x, norm1_g, w_in, attn_norm_g, hgrn_norm_g, hgrn_lb_logits, w_out, norm2_g, w_up, conv_w, conv_b, w_down, final_norm_g, loss_target, m_norm1_g, m_w_in, m_attn_norm_g, m_hgrn_norm_g, m_hgrn_lb_logits, m_w_out, m_norm2_g, m_w_up, m_conv_w, m_conv_b, m_w_down, m_final_norm_g, v_norm1_g, v_w_in, v_attn_norm_g, v_hgrn_norm_g, v_hgrn_lb_logits, v_w_out, v_norm2_g, v_w_up, v_conv_w, v_conv_b, v_w_down, v_final_norm_g):
    xs, target = x[0], loss_target[0]
    S, D = xs.shape
    NA = 3 * ATTN_W
    fng = final_norm_g.reshape(1, D)

    t = lambda a: a[0].T
    casts = [_sum_cast([w], BF16, name=f"cast_{nm}") for nm, w in
             (("w_in", t(w_in)), ("w_out", w_out[0]), ("w_up", t(w_up)), ("w_down", w_down[0]))]
    me = _dev_index(_coords())
    (g_in,) = _all_gather(casts[:1], name="ag_w_in")
    later = casts[1:] + [conv_w[0]]
    ag1 = _copies_start(later, [lax.empty((N_DEV, *s.shape), s.dtype) for s in later], _ag_chips_plan,
                        3 * len(later), name="ag_chips_start", after=g_in)
    wi = g_in.reshape(-1, D)
    wi_a = _pair_major(wi[:NA])

    u1, proj_a, proj_h, proj_f = _in_proj(xs, norm1_g + ag1[4][0, 0], wi_a, wi, NA, name="in_proj")
    attn, lse = _attn_fwd(proj_a, name="attn_fwd")
    lands = _copies_wait(ag1, _ag_chips_plan, attn, name="ag_chips_wait")
    lands = [lax.dynamic_update_index_in_dim(l, s, me, 0) for l, s in zip(lands, later)]
    ag2 = _copies_start([], lands, _ag_sibling_plan, 4 * len(later), name="ag_sibling_start")
    rec, states = _hgrn_fwd(proj_h, proj_f, hgrn_lb_logits + ag2[4][0, 0], name="hgrn_fwd")
    g_out, g_up, g_down, g_cw = _copies_wait(ag2, _ag_sibling_plan, rec, name="ag_sibling_wait")
    wo = g_out.reshape(-1, D)
    wu = g_up.reshape(-1, D)
    wd = g_down.reshape(-1, D)
    cw = _gathered_cols(g_cw)
    h1, u2, mixed = _out_proj(attn, rec, proj_h, xs, attn_norm_g, hgrn_norm_g, norm2_g, wo, name="out_proj")
    gate, gelu, vslope, act = _up_glu(u2, wu, cw, conv_b, name="up_glu")
    dh2, dh2b, d_fng, loss_part = _down_loss(act, wd, h1, fng, target, name="down_loss")

    dgate, dval, d_cw, d_cb = _dact_glu_bwd(dh2b, wd, gate, gelu, vslope, cw, name="dact_glu_bwd")
    dw_down = _mm_tn(act, dh2b, tm=256, name="dw_down")
    dh1, dh1b, d_n2g = _grad_norm_input([dgate, dval], [(wu, 0)], h1, norm2_g, dh2, name="du2_norm2_bwd")
    F = dgate.shape[1]
    dw_up = _mm_tn(dgate, u2, tm=256, rows=2 * F, name="dw_up_gate")
    dw_up = _mm_tn(dval, u2, tm=256, rows=2 * F, row_block=lambda i: i + F // 256, into=dw_up, name="dw_up_val")
    rs_ffn = _rs_start([dw_down.reshape(N_DEV, -1, D), dw_up.reshape(N_DEV, -1, D)], me, name="rs_ffn_start")
    dattn, delta, drec, dhg, d_ang, d_hng = _dmix_post_bwd(dh1b, wo, attn, rec, proj_h, attn_norm_g + rs_ffn[0][4][0, 0],
                                                          hgrn_norm_g, name="dmix_post_bwd")
    dw_out = _mm_tn(mixed, dh1b, name="dw_out")
    rs_out = _rs_start([dw_out.reshape(N_DEV, -1, D)], me, name="rs_out_start")
    dproj_h, d_lbl = _hgrn_bwd(proj_h, proj_f, hgrn_lb_logits + rs_out[0][4][0, 0], states, drec, name="hgrn_bwd")
    small = [("loss", loss_part, None, None, None),
             ("attn_norm_g", d_ang, attn_norm_g, m_attn_norm_g, v_attn_norm_g),
             ("hgrn_norm_g", d_hng, hgrn_norm_g, m_hgrn_norm_g, v_hgrn_norm_g),
             ("hgrn_lb_logits", d_lbl, hgrn_lb_logits, m_hgrn_lb_logits, v_hgrn_lb_logits),
             ("norm2_g", d_n2g, norm2_g, m_norm2_g, v_norm2_g),
             ("conv_b", d_cb, conv_b, m_conv_b, v_conv_b),
             ("final_norm_g", d_fng, final_norm_g, m_final_norm_g, v_final_norm_g)]
    pack = lambda arrs: jnp.concatenate([a.reshape(1, -1) for a in arrs], axis=1)
    small_own = [pack([s[1] for s in small]), d_cw]
    ag_small = _copies_start(small_own, [lax.empty((N_DEV, *s.shape), s.dtype) for s in small_own], _ag_direct_plan,
                             (N_DEV - 1) * len(small_own), name="ag_small_start")
    dproj_a = _attn_bwd(proj_a, dattn, lse, delta, name="attn_bwd")
    pairs = ATTN_W // LANES
    dw_in = _mm_tn(dproj_a, u1, tm=LANES, rows=wi.shape[0], row_block=lambda i: pairs * (i % 3) + i // 3,
                   name="dw_in_attn")
    dw_in = _mm_tn(dproj_h, u1, tm=256, rows=wi.shape[0], row_block=lambda i: i + NA // 256, into=dw_in,
                   name="dw_in_hgrn")
    dw_in = _mm_tn(dhg, u1, tm=256, rows=wi.shape[0], row_block=lambda i: i + (NA + 3 * HGRN_W) // 256, into=dw_in,
                   name="dw_in_gate")
    rs_in = _rs_start([dw_in.reshape(N_DEV, -1, D)], me, name="rs_in_start", after=ag_small[4])
    grad_x, _, d_n1g = _grad_norm_input([dproj_a, dproj_h, dhg], [(wi_a, 0), (wi, NA)], xs,
                                        norm1_g + rs_in[0][4][0, 0], dh1, name="du1_norm1_bwd")

    res = {}

    def update(nm, parts, w, m, v, transposed=False):
        if transposed:
            res[nm] = [r.T[None] for r in _adamw(parts, t(w), t(m), t(v), name=f"adamw_{nm}")]
        else:
            res[nm] = _adamw(parts, w, m, v, name=f"adamw_{nm}")

    g_down, g_up = _rs_finish(rs_ffn, grad_x, name="rs_ffn_wait")
    update("w_down", g_down, w_down, m_w_down, v_w_down)
    update("w_up", g_up, w_up, m_w_up, v_w_up, transposed=True)
    (g_out,) = _rs_finish(rs_out, grad_x, name="rs_out_wait")
    update("w_out", g_out, w_out, m_w_out, v_w_out)
    (g_in,) = _rs_finish(rs_in, res["w_up"][1], name="rs_in_wait")
    update("w_in", g_in, w_in, m_w_in, v_w_in, transposed=True)

    g_small, g_dcw = [lax.dynamic_update_index_in_dim(l, s, me, 0)
                      for l, s in zip(_copies_wait(ag_small, _ag_direct_plan, grad_x, name="ag_small_wait"), small_own)]
    sm = _adamw_packed(g_small, [s[1].size for s in small], [None if s[2] is None else s[2:] for s in small],
                       name="adamw_small")
    for (nm, *_), r in zip(small, sm):
        res[nm] = r
    ncw = conv_w.shape[-1]
    mine_cw = lax.dynamic_slice_in_dim(g_dcw, me * ncw, ncw, axis=2)
    res["conv_w"] = _adamw([mine_cw], conv_w, m_conv_w, v_conv_w, name="adamw_conv_w")
    late, _ = lax.optimization_barrier((d_n1g, res["w_in"][1]))
    update("norm1_g", _all_gather([late], name="ag_norm1_grad"), norm1_g, m_norm1_g, v_norm1_g)

    loss = res["loss"][0][0, 0]
    order = ["norm1_g", "w_in", "attn_norm_g", "hgrn_norm_g", "hgrn_lb_logits", "w_out", "norm2_g", "w_up",
             "conv_w", "conv_b", "w_down", "final_norm_g"]
    return (loss, grad_x[None], *[res[nm][0] for nm in order], *[res[nm][1] for nm in order],
            *[res[nm][2] for nm in order], *[res[nm][3] for nm in order])
```

```python
import jax
import jax.numpy as jnp
from jax import lax
from jax.experimental import pallas as pl
from jax.experimental.pallas import tpu as pltpu

F32, BF16 = jnp.float32, jnp.bfloat16
NORM_EPS = 1e-6
ATTN_HEADS, HEAD_DIM, ATTN_BLOCK = 8, 64, 128
DILATIONS = (1, 4, 16)
ATTN_SCALE = HEAD_DIM ** -0.5
ATTN_W = ATTN_HEADS * HEAD_DIM
HGRN_HEADS, HGRN_DIM, HGRN_CHUNK = 4, 128, 64
HGRN_W = HGRN_HEADS * HGRN_DIM
ADAM_LR, ADAM_B1, ADAM_B2, ADAM_EPS, ADAM_WD, ADAM_STEP = 0.001, 0.9, 0.999, 1e-08, 0.01, 10
LANES, SUBLANES = 128, 8
VMEM_LIMIT_BYTES = 56 * 1024 * 1024
N_DEV = 8
MESH = pl.DeviceIdType.MESH
HBM = pl.BlockSpec(memory_space=pltpu.HBM)
HIGHEST = lax.Precision.HIGHEST


def _cparams(*sem):
    return pltpu.CompilerParams(dimension_semantics=sem, vmem_limit_bytes=VMEM_LIMIT_BYTES)


def _tile(n, pref):
    if n <= pref:
        return n
    t = (pref // LANES) * LANES
    while n % t:
        t -= LANES
    return t


def _resident(shape):
    return pl.BlockSpec(shape, lambda *_: (0,) * len(shape), pipeline_mode=pl.Buffered(1))


def _dot(a, b, dims, precision=None):
    return lax.dot_general(a, b, (dims, ((), ())), precision=precision, preferred_element_type=F32)


def _nn(a, b, precision=None):
    return _dot(a, b, ((1,), (0,)), precision)


def _nt(a, b):
    return _dot(a, b, ((1,), (1,)))


def _tn(a, b):
    return _dot(a, b, ((0,), (0,)))


def _sigmoid(x):
    return 1.0 / (1.0 + jnp.exp(-x))


def _rstd(x):
    return lax.rsqrt(jnp.mean(x * x, axis=-1, keepdims=True) + NORM_EPS)


def _norm_bwd(x, g, du):
    r = _rstd(x)
    xh = x * r
    dxh = du * g
    return r * (dxh - xh * jnp.mean(dxh * xh, axis=-1, keepdims=True)), du * xh


def _row_halves(tm):
    return [pl.ds(0, tm // 2), pl.ds(tm // 2, tm // 2)]


def _behind(after, n_in):
    if after is None:
        return [], [], (lambda body: body)
    return ([after], [pl.BlockSpec(memory_space=pl.ANY)],
            lambda body: (lambda *refs: body(*refs[:n_in], *refs[n_in + 1:])))


def _accumulate(ref, part, first):
    @pl.when(first)
    def _():
        ref[...] = part

    @pl.when(jnp.logical_not(first))
    def _():
        ref[...] += part


def _mm_tn(x, dy, *, name, tm=512, tn=1024, rows=None, row_block=None, into=None):
    S, M = x.shape
    N = dy.shape[1]
    tm, tn = _tile(M, tm), _tile(N, tn)
    row_block = row_block or (lambda i: i)

    def body(x_ref, dy_ref, *rest):
        o_ref, xt_ref = rest[-2:]

        @pl.when(pl.program_id(1) == 0)
        def _():
            xt_ref[...] = x_ref[...].T

        o_ref[...] = _nn(xt_ref[...], dy_ref[...]).astype(BF16)

    operands = [x, dy] + ([] if into is None else [into])
    return pl.pallas_call(
        body, name=name, grid=(M // tm, N // tn),
        in_specs=[pl.BlockSpec((S, tm), lambda i, j: (0, i)), pl.BlockSpec((S, tn), lambda i, j: (0, j))]
        + ([] if into is None else [pl.BlockSpec(memory_space=pl.ANY)]),
        out_specs=pl.BlockSpec((tm, tn), lambda i, j: (row_block(i), j)),
        out_shape=jax.ShapeDtypeStruct((rows or M, N), BF16),
        input_output_aliases={} if into is None else {2: 0},
        scratch_shapes=[pltpu.VMEM((tm, S), BF16)], compiler_params=_cparams("parallel", "arbitrary"),
    )(*operands)


def _in_proj(x, g, wt_attn, wt, row0, *, name, tm=512, after=None):
    S, D = x.shape
    NA, NH, W = wt_attn.shape[0], wt.shape[0] - row0, HGRN_W

    def body(x_ref, g_ref, wa_ref, w_ref, u_ref, a_ref, h_ref, f_ref):
        xv = x_ref[...]
        u = (xv * _rstd(xv) * g_ref[...]).astype(BF16)
        u_ref[...] = u
        a_ref[...] = _nt(u, wa_ref[...]).astype(BF16)
        ph = _nt(u, w_ref[row0:row0 + NH, :])
        h_ref[...] = ph.astype(BF16)
        f_ref[...] = ph[:, W:2 * W]

    row = lambda w: pl.BlockSpec((tm, w), lambda i: (i, 0))
    extra, extra_specs, adapt = _behind(after, 4)
    return pl.pallas_call(
        adapt(body), name=name, grid=(S // tm,),
        in_specs=[row(D), pl.BlockSpec((1, D), lambda i: (0, 0)), _resident(wt_attn.shape), _resident(wt.shape)]
        + extra_specs,
        out_specs=[row(D), row(NA), row(NH), row(W)],
        out_shape=[jax.ShapeDtypeStruct((S, D), BF16), jax.ShapeDtypeStruct((S, NA), BF16),
                   jax.ShapeDtypeStruct((S, NH), BF16), jax.ShapeDtypeStruct((S, W), F32)],
        compiler_params=_cparams("parallel"),
    )(x, g, wt_attn, wt, *extra)


PAIR_W = 3 * LANES
ATTN_UNROLL_FWD, ATTN_UNROLL_BWD = 8, 4


def _attn_masks(first):
    qi = lax.broadcasted_iota(jnp.int32, (ATTN_BLOCK, 2 * ATTN_BLOCK), 0)
    kj = lax.broadcasted_iota(jnp.int32, (ATTN_BLOCK, 2 * ATTN_BLOCK), 1)
    dist = qi + ATTN_BLOCK - kj
    valid = (dist >= 0) & (dist <= ATTN_BLOCK) & jnp.logical_or(kj >= ATTN_BLOCK, jnp.logical_not(first))
    lane = lax.broadcasted_iota(jnp.int32, (1, LANES), 1)
    return valid, lane


def _for_residue_blocks(S, d, fn):
    span = ATTN_BLOCK * d
    nb = S // span

    def step(n, carry):
        base = pl.multiple_of(n * span, span)
        for r in range(d):
            off = pl.multiple_of((r * nb + n) * ATTN_BLOCK, ATTN_BLOCK)
            fn(lambda ref, r=r: _block_rows(ref, base, r, d),
               lambda ref, val, r=r: _set_block_rows(ref, base, r, d, val), off)
        return carry

    lax.fori_loop(0, nb, step, 0)


def _for_blocks(S, unroll, fn):
    def step(i, carry):
        fn([(pl.multiple_of((i * unroll + u) * ATTN_BLOCK, ATTN_BLOCK), i * unroll + u) for u in range(unroll)])
        return carry

    lax.fori_loop(0, S // ATTN_BLOCK // unroll, step, 0)


def _head_value(x2, lane, e):
    return jnp.sum(jnp.where(lane == HEAD_DIM * e, x2, 0.0), axis=-1, keepdims=True)


def _block_rows(ref, base, r, d):
    if d == 1:
        return ref[pl.ds(base, ATTN_BLOCK), :]
    return ref.at[pl.ds(base, ATTN_BLOCK * d)][pl.ds(r, ATTN_BLOCK, stride=d), :]


def _set_block_rows(ref, base, r, d, val):
    if d == 1:
        ref[pl.ds(base, ATTN_BLOCK), :] = val
    else:
        ref.at[pl.ds(base, ATTN_BLOCK * d)][pl.ds(r, ATTN_BLOCK, stride=d), :] = val


def _order4_to_16(src, dst, pad):
    S = src.shape[0]
    q4, q16 = S // 4, S // 16
    for r in range(4):
        for a in range(4):
            for n in range(q16 // ATTN_BLOCK):
                rows = src.at[pl.ds(r * q4 + 4 * ATTN_BLOCK * n, 4 * ATTN_BLOCK)][pl.ds(a, ATTN_BLOCK, stride=4), :]
                dst[pl.ds(pad + (4 * a + r) * q16 + ATTN_BLOCK * n, ATTN_BLOCK), :] = rows.astype(dst.dtype)


def _order16_to_4(src, pad, dst):
    S = dst.shape[0]
    q4, q16 = S // 4, S // 16
    for r in range(4):
        for a in range(4):
            for n in range(q16 // ATTN_BLOCK):
                rows = src[pl.ds(pad + (4 * a + r) * q16 + ATTN_BLOCK * n, ATTN_BLOCK), :]
                dst.at[pl.ds(r * q4 + 4 * ATTN_BLOCK * n, 4 * ATTN_BLOCK)][pl.ds(a, ATTN_BLOCK, stride=4), :] = rows


def _regroup(S, d, pairs, tmp):
    for src, dst, pad in pairs:
        if d == 16:
            def to_tmp(rows, _, off, src=src):
                tmp[pl.ds(off, ATTN_BLOCK), :] = rows(src)

            _for_residue_blocks(S, 4, to_tmp)
            _order4_to_16(tmp, dst, pad)
    if d != 16:
        def to_dst(rows, _, off):
            for src, dst, pad in pairs:
                dst[pl.ds(pad + off, ATTN_BLOCK), :] = rows(src).astype(dst.dtype)

        _for_residue_blocks(S, d, to_dst)


def _split_pair(p_ref, qs, ks, vs, bk, bv):
    qs[...] = p_ref[:, 0:LANES].astype(F32) * ATTN_SCALE
    ks[...] = p_ref[:, LANES:2 * LANES].astype(F32)
    vs[...] = p_ref[:, 2 * LANES:3 * LANES].astype(F32)
    bk[0:ATTN_BLOCK, :] = jnp.zeros((ATTN_BLOCK, LANES), bk.dtype)
    bv[0:ATTN_BLOCK, :] = jnp.zeros((ATTN_BLOCK, LANES), bv.dtype)


def _attn_fwd(proj_a, *, name):
    S = proj_a.shape[0]

    def body(p_ref, o_ref, l_ref, qs, ks, vs, bq, bk, bv, bo, bl, to, tl):
        _split_pair(p_ref, qs, ks, vs, bk, bv)
        for d in DILATIONS:
            nb = S // (ATTN_BLOCK * d)
            _regroup(S, d, ((qs, bq, 0), (ks, bk, ATTN_BLOCK), (vs, bv, ATTN_BLOCK)), to)

            def blocks(group, nb=nb):
                lane = lax.broadcasted_iota(jnp.int32, (1, LANES), 1)
                heads = [(lane >= HEAD_DIM * e) & (lane < HEAD_DIM * (e + 1)) for e in range(LANES // HEAD_DIM)]
                wins = [pl.ds(off, 2 * ATTN_BLOCK) for off, _ in group]
                s = [[_nt(jnp.where(mh, bq[pl.ds(off, ATTN_BLOCK), :], jnp.zeros((ATTN_BLOCK, LANES), BF16)), bk[win, :])
                      for mh in heads] for (off, _), win in zip(group, wins)]
                p, m, l = [], [], []
                for (off, b), su in zip(group, s):
                    valid, _ = _attn_masks(jnp.bitwise_and(b, nb - 1) == 0)
                    sm = [jnp.where(valid, x, -jnp.inf) for x in su]
                    m.append([jnp.max(x, axis=-1, keepdims=True) for x in sm])
                    p.append([jnp.exp(x - mx) for x, mx in zip(sm, m[-1])])
                    l.append([jnp.sum(x, axis=-1, keepdims=True) for x in p[-1]])
                o = [[_nn(x.astype(BF16), bv[win, :]) for x in pu] for pu, win in zip(p, wins)]
                for (off, _), ou, mu, lu in zip(group, o, m, l):
                    o2 = jnp.zeros((ATTN_BLOCK, LANES), F32)
                    l2 = jnp.zeros((ATTN_BLOCK, LANES), F32)
                    for mh, oe, me_, le in zip(heads, ou, mu, lu):
                        o2 = jnp.where(mh, oe / le, o2)
                        l2 = jnp.where(mh, me_ + jnp.log(le), l2)
                    bo[pl.ds(off, ATTN_BLOCK), :] = o2
                    bl[pl.ds(off, ATTN_BLOCK), :] = l2

            _for_blocks(S, ATTN_UNROLL_FWD, blocks)

            if d == 16:
                _order16_to_4(bo, 0, to)
                _order16_to_4(bl, 0, tl)
            src_o, src_l = (to, tl) if d == 16 else (bo, bl)

            def merge(rows, set_rows, off, d=d, src_o=src_o, src_l=src_l):
                blk = pl.ds(off, ATTN_BLOCK)
                o2, l2 = src_o[blk, :], src_l[blk, :]
                if d != DILATIONS[0]:
                    lo, oo = rows(l_ref), rows(o_ref)
                    ln = jnp.maximum(lo, l2)
                    wa, wb = jnp.exp(lo - ln), jnp.exp(l2 - ln)
                    o2 = (wa * oo + wb * o2) / (wa + wb)
                    l2 = ln + jnp.log(wa + wb)
                set_rows(o_ref, o2)
                set_rows(l_ref, l2)

            _for_residue_blocks(S, min(d, 4), merge)

    slab = pl.BlockSpec((S, LANES), lambda p: (0, p))
    f32_slab, bf16_slab = pltpu.VMEM((S, LANES), F32), pltpu.VMEM((S, LANES), BF16)
    bf16_window = pltpu.VMEM((S + ATTN_BLOCK, LANES), BF16)
    return pl.pallas_call(
        body, name=name, grid=(ATTN_W // LANES,), in_specs=[pl.BlockSpec((S, PAIR_W), lambda p: (0, p))],
        out_specs=[slab, slab],
        out_shape=[jax.ShapeDtypeStruct((S, ATTN_W), F32), jax.ShapeDtypeStruct((S, ATTN_W), F32)],
        scratch_shapes=[f32_slab] * 3 + [bf16_slab, bf16_window, bf16_window] + [f32_slab] * 4,
        compiler_params=_cparams("parallel"),
    )(proj_a)


def _attn_bwd(proj_a, do, lse, delta, *, name):
    S = proj_a.shape[0]

    def body(p_ref, do_ref, lse_ref, dl_ref, o_ref, qs, ks, vs, dqs, dks, dvs, bq, bk, bv, bdo, blse, bdl, bdq, bdk, bdv,
             tmp):
        _split_pair(p_ref, qs, ks, vs, bk, bv)
        bdk[0:ATTN_BLOCK, :] = jnp.zeros((ATTN_BLOCK, LANES), F32)
        bdv[0:ATTN_BLOCK, :] = jnp.zeros((ATTN_BLOCK, LANES), F32)
        for d in DILATIONS:
            nb = S // (ATTN_BLOCK * d)
            _regroup(S, d, ((qs, bq, 0), (ks, bk, ATTN_BLOCK), (vs, bv, ATTN_BLOCK), (do_ref, bdo, 0),
                            (lse_ref, blse, 0), (dl_ref, bdl, 0)), tmp)

            def blocks(group, nb=nb):
                lane = lax.broadcasted_iota(jnp.int32, (1, LANES), 1)
                heads = [(lane >= HEAD_DIM * e) & (lane < HEAD_DIM * (e + 1)) for e in range(LANES // HEAD_DIM)]
                zero = jnp.zeros((ATTN_BLOCK, LANES), BF16)
                chains = [(off, b, e, mh) for off, b in group for e, mh in enumerate(heads)]
                qm = [jnp.where(mh, bq[pl.ds(off, ATTN_BLOCK), :], zero) for off, _, _, mh in chains]
                dom = [jnp.where(mh, bdo[pl.ds(off, ATTN_BLOCK), :], zero) for off, _, _, mh in chains]
                s = [_nt(x, bk[pl.ds(off, 2 * ATTN_BLOCK), :]) for x, (off, _, _, _) in zip(qm, chains)]
                dp = [_nt(x, bv[pl.ds(off, 2 * ATTN_BLOCK), :]) for x, (off, _, _, _) in zip(dom, chains)]
                p, ds = [], []
                for (off, b, e, _), sc, dpc in zip(chains, s, dp):
                    valid, _ = _attn_masks(jnp.bitwise_and(b, nb - 1) == 0)
                    blk = pl.ds(off, ATTN_BLOCK)
                    pc = jnp.where(valid, jnp.exp(sc - _head_value(blse[blk, :], lane, e)), 0.0)
                    ds.append((pc * (dpc - _head_value(bdl[blk, :], lane, e))).astype(BF16))
                    p.append(pc.astype(BF16))
                dq = [_nn(x, bk[pl.ds(off, 2 * ATTN_BLOCK), :]) for x, (off, _, _, _) in zip(ds, chains)]
                dk = [_tn(x, y) for x, y in zip(ds, qm)]
                dv = [_tn(x, y) for x, y in zip(p, dom)]
                nh = len(heads)
                for u, (off, _) in enumerate(group):
                    dq2 = jnp.zeros((ATTN_BLOCK, LANES), F32)
                    for mh, x in zip(heads, dq[nh * u:nh * (u + 1)]):
                        dq2 = jnp.where(mh, x, dq2)
                    bdq[pl.ds(off, ATTN_BLOCK), :] = dq2 * ATTN_SCALE
                    for acc, grads in ((bdk, dk), (bdv, dv)):
                        win_grad = sum(grads[nh * u + 1:nh * (u + 1)], grads[nh * u])
                        acc[pl.ds(off, ATTN_BLOCK), :] += win_grad[:ATTN_BLOCK]
                        acc[pl.ds(off + ATTN_BLOCK, ATTN_BLOCK), :] = win_grad[ATTN_BLOCK:]

            _for_blocks(S, ATTN_UNROLL_BWD, blocks)

            outs = ((dqs, bdq, 0), (dks, bdk, ATTN_BLOCK), (dvs, bdv, ATTN_BLOCK))
            if d == 16:
                for acc, grad, pad in outs:
                    _order16_to_4(grad, pad, tmp)

                    def add(rows, set_rows, off, acc=acc):
                        set_rows(acc, rows(acc) + tmp[pl.ds(off, ATTN_BLOCK), :])

                    _for_residue_blocks(S, 4, add)
            else:
                def scatter(rows, set_rows, off, d=d):
                    for acc, grad, pad in outs:
                        part = grad[pl.ds(pad + off, ATTN_BLOCK), :]
                        set_rows(acc, part if d == DILATIONS[0] else rows(acc) + part)

                _for_residue_blocks(S, d, scatter)
        o_ref[:, 0:LANES] = dqs[...].astype(BF16)
        o_ref[:, LANES:2 * LANES] = dks[...].astype(BF16)
        o_ref[:, 2 * LANES:3 * LANES] = dvs[...].astype(BF16)

    slab = pl.BlockSpec((S, LANES), lambda p: (0, p))
    pair = pl.BlockSpec((S, PAIR_W), lambda p: (0, p))
    f32_slab, bf16_slab = pltpu.VMEM((S, LANES), F32), pltpu.VMEM((S, LANES), BF16)
    f32_window, bf16_window = pltpu.VMEM((S + ATTN_BLOCK, LANES), F32), pltpu.VMEM((S + ATTN_BLOCK, LANES), BF16)
    return pl.pallas_call(
        body, name=name, grid=(ATTN_W // LANES,), in_specs=[pair, slab, slab, slab], out_specs=pair,
        out_shape=jax.ShapeDtypeStruct(proj_a.shape, BF16),
        scratch_shapes=[f32_slab] * 6 + [bf16_slab, bf16_window, bf16_window, bf16_slab, f32_slab, f32_slab,
                                         f32_slab, f32_window, f32_window, f32_slab],
        compiler_params=_cparams("parallel"),
    )(proj_a, do, lse, delta)


HG_T = 2 * HGRN_CHUNK
HG_GROUPS = 2
HG_STEP = HG_GROUPS * HG_T


def _hgrn_consts():
    row = lax.broadcasted_iota(jnp.int32, (HG_T, HG_T), 0)
    col = lax.broadcasted_iota(jnp.int32, (HG_T, HG_T), 1)
    same = (row >= HGRN_CHUNK) == (col >= HGRN_CHUNK)
    return row, same & (col <= row), same & (col >= row)


def _lower_bound(logits_ref):
    l0, l1 = logits_ref[0:1, :], logits_ref[1:2, :]
    mx = jnp.maximum(l0, l1)
    e0, e1 = jnp.exp(l0 - mx), jnp.exp(l1 - mx)
    return e0 / (e0 + e1)


def _hgrn_chains():
    chains = [(g, h) for g in range(HG_GROUPS) for h in range(HGRN_HEADS)]
    rows = [pl.ds(HG_T * g, HG_T) for g, _ in chains]
    lanes = [slice(HGRN_DIM * h, HGRN_DIM * (h + 1)) for _, h in chains]
    return chains, rows, lanes


def _hgrn_gates(qs, fs, lbs, row, causal):
    C = HGRN_CHUNK
    tri = jnp.where(causal, 1.0, 0.0).astype(F32)
    sgs = [_sigmoid(f) for f in fs]
    forgets = [lb + (1.0 - lb) * sg for lb, sg in zip(lbs, sgs)]
    logfs = [jnp.log(forget) for forget in forgets]
    bs = [_nn(tri, logf, HIGHEST) for logf in logfs]
    out = []
    for q, sg, forget, logf, b in zip(qs, sgs, forgets, logfs, bs):
        key = 1.0 - forget
        bend0 = jnp.sum(logf[:C], axis=0, keepdims=True)
        bend1 = jnp.sum(logf[C:], axis=0, keepdims=True)
        bend = jnp.where(row < C, bend0, bend1)
        eb, emb, eend = jnp.exp(b), jnp.exp(-b), jnp.exp(bend - b)
        sq = _sigmoid(q)
        out.append(dict(sg=sg, forget=forget, key=key, bend0=bend0, bend1=bend1, eb=eb, emb=emb, eend=eend, sq=sq,
                        qd=q * sq * eb, ki=key * emb, ke=key * eend))
    return out


def _hgrn_fwd(proj, proj_f, logits, *, name, after=None):
    S = proj.shape[0]
    W, C = HGRN_W, HGRN_CHUNK

    def body(q_ref, f_ref, i_ref, lg_ref, rec_ref, st_ref, s_ref):
        @pl.when(pl.program_id(0) == 0)
        def _():
            s_ref[...] = jnp.zeros_like(s_ref)

        row, causal, _ = _hgrn_consts()
        lb_all = _lower_bound(lg_ref)
        chains, rows, lanes = _hgrn_chains()
        n = range(len(chains))
        gts = _hgrn_gates([q_ref[rows[c], lanes[c]].astype(F32) for c in n], [f_ref[rows[c], lanes[c]] for c in n],
                          [lb_all[:, lanes[c]] for c in n], row, causal)
        qd, ki, ke = ([gt[k].astype(BF16) for gt in gts] for k in ("qd", "ki", "ke"))
        iv = [i_ref[rows[c], lanes[c]].astype(BF16) for c in n]
        a = [_nt(qd[c], ki[c]) for c in n]
        u0 = [_tn(iv[c][:C], ke[c][:C]) for c in n]
        u1 = [_tn(iv[c][C:], ke[c][C:]) for c in n]
        state = [s_ref[h] for h in range(HGRN_HEADS)]
        s0, s1 = [], []
        for c, (g, h) in enumerate(chains):
            s0.append(state[h])
            s1.append(jnp.exp(gts[c]["bend0"]) * s0[c] + u0[c])
            state[h] = jnp.exp(gts[c]["bend1"]) * s1[c] + u1[c]
        o0 = [_nt(qd[c][:C], s0[c].astype(BF16)) for c in n]
        o1 = [_nt(qd[c][C:], s1[c].astype(BF16)) for c in n]
        o = [_nn(jnp.where(causal, a[c], 0.0).astype(BF16), iv[c]) for c in n]
        for c, (g, h) in enumerate(chains):
            st_ref[2 * g, h] = s0[c]
            st_ref[2 * g + 1, h] = s1[c]
            rec_ref[rows[c], lanes[c]] = o[c] + jnp.concatenate([o0[c], o1[c]], axis=0)
        for h in range(HGRN_HEADS):
            s_ref[h] = state[h]

    blk = lambda j: pl.BlockSpec((HG_STEP, W), lambda t: (t, j))
    extra, extra_specs, adapt = _behind(after, 4)
    return pl.pallas_call(
        adapt(body), name=name, grid=(S // HG_STEP,),
        in_specs=[blk(0), blk(0), blk(2), pl.BlockSpec((2, W), lambda t: (0, 0))] + extra_specs,
        out_specs=[blk(0), pl.BlockSpec((2 * HG_GROUPS, HGRN_HEADS, HGRN_DIM, HGRN_DIM), lambda t: (t, 0, 0, 0))],
        out_shape=[jax.ShapeDtypeStruct((S, W), F32),
                   jax.ShapeDtypeStruct((S // C, HGRN_HEADS, HGRN_DIM, HGRN_DIM), F32)],
        scratch_shapes=[pltpu.VMEM((HGRN_HEADS, HGRN_DIM, HGRN_DIM), F32)],
        compiler_params=_cparams("arbitrary"),
    )(proj, proj_f, proj, logits, *extra)


def _hgrn_bwd(proj, proj_f, logits, states, drec, *, name, after=None):
    S = proj.shape[0]
    W, C = HGRN_W, HGRN_CHUNK
    nt = S // HG_STEP

    def body(q_ref, f_ref, i_ref, lg_ref, st_ref, do_ref, dp_ref, dlg_ref, ds_ref, dlb_ref):
        t = pl.program_id(0)

        @pl.when(t == 0)
        def _():
            ds_ref[...] = jnp.zeros_like(ds_ref)
            dlb_ref[...] = jnp.zeros_like(dlb_ref)

        row, causal, anti = _hgrn_consts()
        lb_all = _lower_bound(lg_ref)
        chains, rows, lanes = _hgrn_chains()
        n = range(len(chains))
        qs, lbs = [q_ref[rows[c], lanes[c]].astype(F32) for c in n], [lb_all[:, lanes[c]] for c in n]
        gts = _hgrn_gates(qs, [f_ref[rows[c], lanes[c]] for c in n], lbs, row, causal)
        qd, ki, ke = ([gt[k] for gt in gts] for k in ("qd", "ki", "ke"))
        qdb, kib, keb = ([x.astype(BF16) for x in xs] for xs in (qd, ki, ke))
        iv = [i_ref[rows[c], lanes[c]].astype(BF16) for c in n]
        dob = [do_ref[rows[c], lanes[c]].astype(BF16) for c in n]
        s0 = [st_ref[2 * g, h] for g, h in chains]
        s1 = [st_ref[2 * g + 1, h] for g, h in chains]
        dec0, dec1 = [jnp.exp(gt["bend0"]) for gt in gts], [jnp.exp(gt["bend1"]) for gt in gts]
        a = [_nt(qdb[c], kib[c]) for c in n]
        da = [_nt(dob[c], iv[c]) for c in n]
        dqd1 = [_nn(dob[c][C:], s1[c].astype(BF16)) for c in n]
        dqd0 = [_nn(dob[c][:C], s0[c].astype(BF16)) for c in n]
        t1 = [_tn(dob[c][C:], qdb[c][C:]) for c in n]
        t0 = [_tn(dob[c][:C], qdb[c][:C]) for c in n]
        carry = [ds_ref[h] for h in range(HGRN_HEADS)]
        ds1, ds0 = [None] * len(chains), [None] * len(chains)
        for c in reversed(n):
            h = chains[c][1]
            ds1[c] = carry[h]
            ds0[c] = dec1[c] * ds1[c] + t1[c]
            carry[h] = dec0[c] * ds0[c] + t0[c]
        for h in range(HGRN_HEADS):
            ds_ref[h] = carry[h]
        ds1b, ds0b = [x.astype(BF16) for x in ds1], [x.astype(BF16) for x in ds0]
        a = [jnp.where(causal, x, 0.0).astype(BF16) for x in a]
        da = [jnp.where(causal, x, 0.0).astype(BF16) for x in da]
        di1 = [_nt(keb[c][C:], ds1b[c]) for c in n]
        dke1 = [_nn(iv[c][C:], ds1b[c]) for c in n]
        di0 = [_nt(keb[c][:C], ds0b[c]) for c in n]
        dke0 = [_nn(iv[c][:C], ds0b[c]) for c in n]
        dqd_a = [_nn(da[c], kib[c]) for c in n]
        dki = [_tn(da[c], qdb[c]) for c in n]
        di_a = [_tn(a[c], dob[c]) for c in n]
        dqd, dke, db = [], [], []
        for c in n:
            ddec1 = jnp.sum(ds1[c] * s1[c], axis=0, keepdims=True)
            ddec0 = jnp.sum(ds0[c] * s0[c], axis=0, keepdims=True)
            dqd.append(dqd_a[c] + jnp.concatenate([dqd0[c], dqd1[c]], axis=0))
            h = chains[c][1]
            dp_ref[rows[c], 2 * W + HGRN_DIM * h:2 * W + HGRN_DIM * (h + 1)] = (
                di_a[c] + jnp.concatenate([di0[c], di1[c]], axis=0)).astype(BF16)
            dke.append(jnp.concatenate([dke0[c], dke1[c]], axis=0))
            gke = dke[c] * ke[c]
            dbend0 = jnp.sum(gke[:C], axis=0, keepdims=True) + ddec0 * dec0[c]
            dbend1 = jnp.sum(gke[C:], axis=0, keepdims=True) + ddec1 * dec1[c]
            dbc = dqd[c] * qd[c] - dki[c] * ki[c] - gke
            db.append(dbc + jnp.where(row == C - 1, dbend0, 0.0) + jnp.where(row == HG_T - 1, dbend1, 0.0))
        tri = jnp.where(anti, 1.0, 0.0).astype(F32)
        dlogf = [_nn(tri, db[c], HIGHEST) for c in n]
        for c in n:
            gt, lb, q, h = gts[c], lbs[c], qs[c], chains[c][1]
            dforget = dlogf[c] / gt["forget"] - (dki[c] * gt["emb"] + dke[c] * gt["eend"])
            sg, sq = gt["sg"], gt["sq"]
            dp_ref[rows[c], W + HGRN_DIM * h:W + HGRN_DIM * (h + 1)] = (
                dforget * (1.0 - lb) * sg * (1.0 - sg)).astype(BF16)
            dlb_ref[:, lanes[c]] += jnp.sum(dforget * (1.0 - sg), axis=0, keepdims=True)
            dp_ref[rows[c], lanes[c]] = (dqd[c] * gt["eb"] * sq * (1.0 + q * (1.0 - sq))).astype(BF16)

        @pl.when(t == nt - 1)
        def _():
            dl0 = dlb_ref[...] * lb_all * (1.0 - lb_all)
            dlg_ref[0:1, :] = dl0
            dlg_ref[1:2, :] = -dl0

    blk = lambda j: pl.BlockSpec((HG_STEP, W), lambda t: (nt - 1 - t, j))
    full = pl.BlockSpec((2, W), lambda t: (0, 0))
    extra, extra_specs, adapt = _behind(after, 6)
    return pl.pallas_call(
        adapt(body), name=name, grid=(nt,),
        in_specs=[blk(0), blk(0), blk(2), full,
                  pl.BlockSpec((2 * HG_GROUPS, HGRN_HEADS, HGRN_DIM, HGRN_DIM), lambda t: (nt - 1 - t, 0, 0, 0)), blk(0)]
        + extra_specs,
        out_specs=[pl.BlockSpec((HG_STEP, 3 * W), lambda t: (nt - 1 - t, 0)), full],
        out_shape=[jax.ShapeDtypeStruct((S, 3 * W), BF16), jax.ShapeDtypeStruct((2, W), F32)],
        scratch_shapes=[pltpu.VMEM((HGRN_HEADS, HGRN_DIM, HGRN_DIM), F32), pltpu.VMEM((1, W), F32)],
        compiler_params=_cparams("arbitrary"),
    )(proj, proj_f, proj, logits, states, drec, *extra)


def _out_proj(attn, rec, proj_h, x, g_attn, g_hgrn, g_norm2, w_out, *, name, tm=512):
    S, D = x.shape
    AW, W = ATTN_W, HGRN_W

    def body(a_ref, r_ref, hg_ref, x_ref, ga_ref, gh_ref, g2_ref, w_ref, h_ref, u_ref, m_ref):
        av = a_ref[...]
        m_ref[:, :AW] = (av * _rstd(av) * ga_ref[...]).astype(BF16)
        for h in range(HGRN_HEADS):
            sl = slice(HGRN_DIM * h, HGRN_DIM * (h + 1))
            rv, hg = r_ref[:, sl], hg_ref[:, sl].astype(F32)
            m_ref[:, AW + HGRN_DIM * h:AW + HGRN_DIM * (h + 1)] = (
                (rv * _rstd(rv) * gh_ref[:, sl]) * (hg * _sigmoid(hg))).astype(BF16)
        h1 = x_ref[...] + _nn(m_ref[...], w_ref[...])
        h_ref[...] = h1
        u_ref[...] = (h1 * _rstd(h1) * g2_ref[...]).astype(BF16)

    row = lambda w, j=0: pl.BlockSpec((tm, w), lambda i: (i, j))
    vec = lambda w: pl.BlockSpec((1, w), lambda i: (0, 0))
    return pl.pallas_call(
        body, name=name, grid=(S // tm,),
        in_specs=[row(AW), row(W), row(W, 3), row(D), vec(AW), vec(W), vec(D), _resident(w_out.shape)],
        out_specs=[row(D), row(D), row(AW + W)],
        out_shape=[jax.ShapeDtypeStruct((S, D), F32), jax.ShapeDtypeStruct((S, D), BF16),
                   jax.ShapeDtypeStruct((S, AW + W), BF16)],
        compiler_params=_cparams("parallel"),
    )(attn, rec, proj_h, x, g_attn, g_hgrn, g_norm2, w_out)


def _dmix_post_bwd(dh1b, w_out, attn, rec, proj_h, g_attn, g_hgrn, *, name, tm=512, after=None):
    S, D = dh1b.shape
    AW, W = ATTN_W, HGRN_W

    def body(dh_ref, w_ref, a_ref, r_ref, hg_ref, ga_ref, gh_ref, do_ref, dl_ref, dr_ref, dhg_ref, dga_ref, dgh_ref):
        first = pl.program_id(0) == 0
        dmix = _nt(dh_ref[...], w_ref[...])
        av = a_ref[...]
        dov, dga = _norm_bwd(av, ga_ref[...], dmix[:, :AW])
        do_ref[...] = dov
        shift = HEAD_DIM.bit_length() - 1
        hi = lax.shift_right_logical(lax.broadcasted_iota(jnp.int32, (AW, AW), 0), shift)
        hj = lax.shift_right_logical(lax.broadcasted_iota(jnp.int32, (AW, AW), 1), shift)
        prod = dov * av
        hi_part = prod.astype(BF16)
        lo_part = (prod - hi_part.astype(F32)).astype(BF16)
        same_head = jnp.where(hi == hj, 1.0, 0.0).astype(BF16)
        dl_ref[...] = _nn(hi_part, same_head) + _nn(lo_part, same_head)
        _accumulate(dga_ref, jnp.sum(dga, axis=0, keepdims=True), first)

        @pl.when(first)
        def _():
            dgh_ref[...] = jnp.zeros_like(dgh_ref)

        for h in range(HGRN_HEADS):
            sl = slice(HGRN_DIM * h, HGRN_DIM * (h + 1))
            rv, hg, gv = r_ref[:, sl], hg_ref[:, sl].astype(F32), gh_ref[:, sl]
            dout = dmix[:, AW + HGRN_DIM * h:AW + HGRN_DIM * (h + 1)]
            sg = _sigmoid(hg)
            drv, dgh = _norm_bwd(rv, gv, dout * (hg * sg))
            dr_ref[:, sl] = drv
            dgh_ref[:, sl] += jnp.sum(dgh, axis=0, keepdims=True)
            dhg_ref[:, sl] = (dout * (rv * _rstd(rv) * gv) * (sg * (1.0 + hg * (1.0 - sg)))).astype(BF16)

    row = lambda w, j=0: pl.BlockSpec((tm, w), lambda i: (i, j))
    vec = lambda w: pl.BlockSpec((1, w), lambda i: (0, 0))
    extra, extra_specs, adapt = _behind(after, 7)
    return pl.pallas_call(
        adapt(body), name=name, grid=(S // tm,),
        in_specs=[row(D), _resident(w_out.shape), row(AW), row(W), row(W, 3), vec(AW), vec(W)] + extra_specs,
        out_specs=[row(AW), row(AW), row(W), row(W), vec(AW), vec(W)],
        out_shape=[jax.ShapeDtypeStruct((S, AW), F32), jax.ShapeDtypeStruct((S, AW), F32),
                   jax.ShapeDtypeStruct((S, W), F32), jax.ShapeDtypeStruct((S, W), BF16),
                   jax.ShapeDtypeStruct((1, AW), F32), jax.ShapeDtypeStruct((1, W), F32)],
        compiler_params=_cparams("arbitrary"),
    )(dh1b, w_out, attn, rec, proj_h, g_attn, g_hgrn, *extra)


def _conv_act(g, g1, g2, w_ref, b_ref):
    c = b_ref[...] + w_ref[0:1, :] * g2 + w_ref[1:2, :] * g1 + w_ref[2:3, :] * g
    return c, 0.5 * (1.0 + lax.erf(c * (2.0 ** -0.5)))


def _shift_down(g, halo, row):
    g1 = jnp.where(row == 0, halo[7:8], pltpu.roll(g, 1, 0))
    g2 = jnp.where(row == 0, halo[6:7], jnp.where(row == 1, halo[7:8], pltpu.roll(g, 2, 0)))
    return g1, g2


def _shift_up(x, halo, row):
    n = x.shape[0]
    x1 = jnp.where(row == n - 1, halo[0:1], pltpu.roll(x, n - 1, 0))
    x2 = jnp.where(row == n - 2, halo[0:1], jnp.where(row == n - 1, halo[1:2], pltpu.roll(x, n - 2, 0)))
    return x1, x2


def _up_glu(u, wt_up, conv_w, conv_b, *, name, tm=1024, tn=1408):
    S, D = u.shape
    F = wt_up.shape[0] // 2
    tn = _tile(F, tn)
    nf = F // tn

    def body(u_ref, wg_ref, wv_ref, cw_ref, cb_ref, g_ref, ge_ref, t_ref, a_ref, halo_ref):
        i, j = pl.program_id(0), pl.program_id(1)

        @pl.when(i == 0)
        def _():
            halo_ref[j] = jnp.zeros((SUBLANES, tn), F32)

        uv = u_ref[...]
        g, v = _nt(uv, wg_ref[...]), _nt(uv, wv_ref[...])
        row = lax.broadcasted_iota(jnp.int32, (tm, tn), 0)
        g1, g2 = _shift_down(g, halo_ref[j], row)
        c, cdf = _conv_act(g, g1, g2, cw_ref, cb_ref)
        gelu = c * cdf
        pdf = jnp.exp(-0.5 * c * c) * (1.0 / (2.0 * jnp.pi) ** 0.5)
        a_ref[...] = (gelu * v).astype(BF16)
        g_ref[...] = g.astype(BF16)
        ge_ref[...] = gelu.astype(BF16)
        t_ref[...] = (v * (cdf + c * pdf)).astype(BF16)
        halo_ref[j] = g[tm - SUBLANES:, :]

    col = pl.BlockSpec((tm, tn), lambda i, j: (i, j))
    out = jax.ShapeDtypeStruct((S, F), BF16)
    return pl.pallas_call(
        body, name=name, grid=(S // tm, nf),
        in_specs=[pl.BlockSpec((tm, D), lambda i, j: (i, 0)), pl.BlockSpec((tn, D), lambda i, j: (j, 0)),
                  pl.BlockSpec((tn, D), lambda i, j: (j + nf, 0)), pl.BlockSpec((3, tn), lambda i, j: (0, j)),
                  pl.BlockSpec((1, tn), lambda i, j: (0, j))],
        out_specs=[col, col, col, col], out_shape=[out, out, out, out],
        scratch_shapes=[pltpu.VMEM((nf, SUBLANES, tn), F32)], compiler_params=_cparams("arbitrary", "arbitrary"),
    )(u, wt_up, wt_up, conv_w, conv_b)


def _dact_glu_bwd(dh2b, w_down, gate, gelu, vslope, conv_w, *, name, tm=1024, tn=1408):
    S, D = dh2b.shape
    F = gate.shape[1]
    tn = _tile(F, tn)
    nf, ni = F // tn, S // tm

    def body(dh_ref, wd_ref, g_ref, ge_ref, t_ref, cw_ref, dg_ref, dv_ref, dcw_ref, dcb_ref, halo_ref, acc_ref):
        i, j = pl.program_id(0), pl.program_id(1)

        @pl.when(i == 0)
        def _():
            halo_ref[j] = jnp.zeros((SUBLANES, tn), F32)
            acc_ref[j] = jnp.zeros((SUBLANES, tn), F32)

        g = g_ref[...].astype(F32)
        row = lax.broadcasted_iota(jnp.int32, (tm, tn), 0)
        da = _nt(dh_ref[...], wd_ref[...])
        dv_ref[...] = (da * ge_ref[...].astype(F32)).astype(BF16)
        dc = da * t_ref[...].astype(F32)
        d1, d2 = _shift_up(dc, halo_ref[j], row)
        dg_ref[...] = (cw_ref[2:3, :] * dc + cw_ref[1:2, :] * d1 + cw_ref[0:1, :] * d2).astype(BF16)
        halo_ref[j] = dc[:SUBLANES, :]
        for k, t in enumerate((d2 * g, d1 * g, dc * g, dc)):
            acc_ref[j, k:k + 1, :] += jnp.sum(t, axis=0, keepdims=True)

        @pl.when((i == ni - 1) & (j == nf - 1))
        def _():
            for jj in range(nf):
                dcw_ref[:, jj * tn:(jj + 1) * tn] = acc_ref[jj, 0:3, :]
                dcb_ref[:, jj * tn:(jj + 1) * tn] = acc_ref[jj, 3:4, :]

    tile = pl.BlockSpec((tm, tn), lambda i, j: (ni - 1 - i, j))
    return pl.pallas_call(
        body, name=name, grid=(ni, nf),
        in_specs=[pl.BlockSpec((tm, D), lambda i, j: (ni - 1 - i, 0)), pl.BlockSpec((tn, D), lambda i, j: (j, 0)),
                  tile, tile, tile, pl.BlockSpec((3, tn), lambda i, j: (0, j))],
        out_specs=[tile, tile, pl.BlockSpec((3, F), lambda i, j: (0, 0)), pl.BlockSpec((1, F), lambda i, j: (0, 0))],
        out_shape=[jax.ShapeDtypeStruct((S, F), BF16), jax.ShapeDtypeStruct((S, F), BF16),
                   jax.ShapeDtypeStruct((3, F), F32), jax.ShapeDtypeStruct((1, F), F32)],
        scratch_shapes=[pltpu.VMEM((nf, SUBLANES, tn), F32), pltpu.VMEM((nf, SUBLANES, tn), F32)],
        compiler_params=_cparams("arbitrary", "arbitrary"),
    )(dh2b, w_down, gate, gelu, vslope, conv_w)


def _down_loss(act, w_down, h1, g, target, *, name, tm=512):
    S, F = act.shape
    D = h1.shape[1]

    def body(a_ref, w_ref, h_ref, g_ref, t_ref, dh_ref, dhb_ref, dg_ref, loss_ref):
        first = pl.program_id(0) == 0
        h2 = h_ref[...] + _nn(a_ref[...], w_ref[...])
        gv = g_ref[...]
        r = _rstd(h2)
        xh = h2 * r
        err = xh * gv - t_ref[...]
        part_loss = 0.5 * jnp.sum(jnp.mean(err * err, axis=-1, keepdims=True), axis=0, keepdims=True)
        dy = err * (1.0 / D)
        dxh = dy * gv
        dh = r * (dxh - xh * jnp.mean(dxh * xh, axis=-1, keepdims=True))
        dh_ref[...] = dh
        dhb_ref[...] = dh.astype(BF16)
        _accumulate(dg_ref, jnp.sum(dy * xh, axis=0, keepdims=True), first)
        _accumulate(loss_ref, jnp.broadcast_to(part_loss, (1, LANES)), first)

    row = lambda w: pl.BlockSpec((tm, w), lambda i: (i, 0))
    vec = lambda w: pl.BlockSpec((1, w), lambda i: (0, 0))
    return pl.pallas_call(
        body, name=name, grid=(S // tm,), in_specs=[row(F), _resident(w_down.shape), row(D), vec(D), row(D)],
        out_specs=[row(D), row(D), vec(D), vec(LANES)],
        out_shape=[jax.ShapeDtypeStruct((S, D), F32), jax.ShapeDtypeStruct((S, D), BF16),
                   jax.ShapeDtypeStruct((1, D), F32), jax.ShapeDtypeStruct((1, LANES), F32)],
        compiler_params=_cparams("arbitrary"),
    )(act, w_down, h1, g, target)


def _grad_norm_input(pieces, ws, x, g, add, *, name, tm=512, after=None):
    S, D = x.shape
    widths = [p.shape[1] for p in pieces]
    n, nw = len(pieces), len(ws)
    where, wi, off = [], 0, ws[0][1]
    for wd in widths:
        if off == ws[wi][0].shape[0]:
            wi, off = wi + 1, ws[wi + 1][1]
        where.append((wi, off))
        off += wd
    ws = [w for w, _ in ws]

    def body(*refs):
        p_refs, w_refs = refs[:n], refs[n:n + nw]
        x_ref, g_ref, add_ref, dx_ref, dxb_ref, dg_ref = refs[n + nw:]
        halves = _row_halves(tm)
        du = []
        for rows in halves:
            terms = [_nn(p_refs[k][rows, :], w_refs[wi][off:off + widths[k], :]) for k, (wi, off) in enumerate(where)]
            du.append(sum(terms[1:], terms[0]))
        dg_sum = None
        for rows, duh in zip(halves, du):
            dx, dg = _norm_bwd(x_ref[rows, :], g_ref[...], duh)
            dx = add_ref[rows, :] + dx
            dx_ref[rows, :] = dx
            dxb_ref[rows, :] = dx.astype(BF16)
            part = jnp.sum(dg, axis=0, keepdims=True)
            dg_sum = part if dg_sum is None else dg_sum + part
        _accumulate(dg_ref, dg_sum, pl.program_id(0) == 0)

    row = lambda w_: pl.BlockSpec((tm, w_), lambda i: (i, 0))
    vec = pl.BlockSpec((1, D), lambda i: (0, 0))
    extra, extra_specs, adapt = _behind(after, n + nw + 3)
    return pl.pallas_call(
        adapt(body), name=name, grid=(S // tm,),
        in_specs=[row(wd) for wd in widths] + [_resident(w.shape) for w in ws] + [row(D), vec, row(D)] + extra_specs,
        out_specs=[row(D), row(D), vec],
        out_shape=[jax.ShapeDtypeStruct((S, D), F32), jax.ShapeDtypeStruct((S, D), BF16),
                   jax.ShapeDtypeStruct((1, D), F32)],
        compiler_params=_cparams("arbitrary"),
    )(*pieces, *ws, x, g, add, *extra)


def _rows(a):
    return a.reshape(-1, a.shape[-1])


def _row_tile(rows, cols, itemsize=4, budget=1 << 20):
    t = rows
    while t % 32 == 0 and t * cols * itemsize > budget:
        t //= 2
    return t


def _sum_cast(arrs, out_dtype, *, name):
    shape = arrs[0].shape
    flat = [_rows(a) for a in arrs]
    R, C = flat[0].shape
    tr = _row_tile(R, C)

    def body(*refs):
        acc = refs[0][...].astype(F32)
        for r in refs[1:-1]:
            acc = acc + r[...].astype(F32)
        refs[-1][...] = acc.astype(out_dtype)

    spec = pl.BlockSpec((tr, C), lambda i: (i, 0))
    return pl.pallas_call(
        body, name=name, grid=(R // tr,), in_specs=[spec] * len(flat), out_specs=spec,
        out_shape=jax.ShapeDtypeStruct((R, C), out_dtype), compiler_params=_cparams("parallel"),
    )(*flat).reshape(shape)


def _adamw(parts, w, m, v, *, name):
    shape = w.shape
    w2, m2, v2 = _rows(w), _rows(m), _rows(v)
    R, C = w2.shape
    parts = [p.reshape(-1, R, C) for p in parts]
    tr = _row_tile(R, C)
    np_ = len(parts)
    c1, c2 = 1.0 - ADAM_B1 ** ADAM_STEP, 1.0 - ADAM_B2 ** ADAM_STEP

    def body(*refs):
        terms = [(r, k) for r in refs[:np_] for k in range(r.shape[0])]
        g = terms[0][0][terms[0][1]].astype(F32)
        for r, k in terms[1:]:
            g = g + r[k].astype(F32)
        w_ref, m_ref, v_ref, g_out, d_out, m_out, v_out = refs[np_:]
        mn = ADAM_B1 * m_ref[...] + (1.0 - ADAM_B1) * g
        vn = ADAM_B2 * v_ref[...] + (1.0 - ADAM_B2) * (g * g)
        g_out[...] = g
        d_out[...] = -ADAM_LR * ((mn / c1) / (jnp.sqrt(vn / c2) + ADAM_EPS) + ADAM_WD * w_ref[...])
        m_out[...] = mn
        v_out[...] = vn

    spec = pl.BlockSpec((tr, C), lambda i: (i, 0))
    out = jax.ShapeDtypeStruct((R, C), F32)
    stacks = [pl.BlockSpec((p.shape[0], tr, C), lambda i: (0, i, 0)) for p in parts]
    res = pl.pallas_call(
        body, name=name, grid=(R // tr,), in_specs=stacks + [spec] * 3, out_specs=[spec] * 4,
        out_shape=[out] * 4, compiler_params=_cparams("parallel"),
    )(*parts, w2, m2, v2)
    return [r.reshape(shape) for r in res]


def _adamw_packed(stack, widths, params, *, name):
    c1, c2 = 1.0 - ADAM_B1 ** ADAM_STEP, 1.0 - ADAM_B2 ** ADAM_STEP
    k = stack.shape[0]
    flat = [None if p is None else [_rows(a) for a in p] for p in params]
    n_in = sum(3 for p in flat if p is not None)

    def body(*refs):
        s_ref, ins, outs = refs[0], list(refs[1:1 + n_in]), list(refs[1 + n_in:])
        off = 0
        for width, p in zip(widths, flat):
            rows = 1 if p is None else p[0].shape[0]
            cols = width // rows
            w_ref, m_ref, v_ref = (None, None, None) if p is None else (ins.pop(0), ins.pop(0), ins.pop(0))
            o_refs = [outs.pop(0) for _ in range(1 if p is None else 4)]
            for r in range(rows):
                seg = slice(off + r * cols, off + (r + 1) * cols)
                g = s_ref[0, :, seg]
                for j in range(1, k):
                    g = g + s_ref[j, :, seg]
                o_refs[0][r:r + 1, :] = g
                if p is not None:
                    row = slice(r, r + 1)
                    mn = ADAM_B1 * m_ref[row, :] + (1.0 - ADAM_B1) * g
                    vn = ADAM_B2 * v_ref[row, :] + (1.0 - ADAM_B2) * (g * g)
                    o_refs[1][row, :] = -ADAM_LR * ((mn / c1) / (jnp.sqrt(vn / c2) + ADAM_EPS) + ADAM_WD * w_ref[row, :])
                    o_refs[2][row, :] = mn
                    o_refs[3][row, :] = vn
            off += width

    operands, out_shape = [stack], []
    for width, p in zip(widths, flat):
        if p is None:
            out_shape.append(jax.ShapeDtypeStruct((1, width), F32))
        else:
            operands += p
            out_shape += [jax.ShapeDtypeStruct(p[0].shape, F32)] * 4
    res = list(pl.pallas_call(body, name=name, out_shape=out_shape)(*operands))
    out = []
    for p, orig in zip(flat, params):
        n = 1 if p is None else 4
        out.append([r if orig is None else r.reshape(orig[0].shape) for r in res[:n]])
        res = res[n:]
    return out


def _coords():
    return lax.axis_index("x"), lax.axis_index("y"), lax.axis_index("c")


def _all_gather(shards, *, name):
    n = len(shards)

    def body(*refs):
        x_refs, out_refs = refs[:n], refs[n:2 * n]
        send_sems, recv_sems, local_sems = refs[2 * n:]
        x, y, c = _coords()
        me, sibling = (x, y, c), (x, y, 1 - c)
        chips = [(1 - x, y), (x, 1 - y), (1 - x, 1 - y)]

        def slot(a, dev):
            return out_refs[a].at[4 * dev[0] + 2 * dev[1] + dev[2]]

        def copy(a, k, block, to, src=None):
            return pltpu.make_async_remote_copy(
                src_ref=slot(a, block) if src is None else src, dst_ref=slot(a, block),
                send_sem=send_sems.at[7 * a + k], recv_sem=recv_sems.at[7 * a + k], device_id=to, device_id_type=MESH)

        mine = [pltpu.make_async_copy(x_refs[a], slot(a, me), local_sems.at[a]) for a in range(n)]
        for cp in mine:
            cp.start()
        first = []
        for a in range(n):
            first.append(copy(a, 0, me, sibling, src=x_refs[a]))
            first += [copy(a, 1 + j, me, (*chip, c), src=x_refs[a]) for j, chip in enumerate(chips)]
        for cp in first:
            cp.start()
        passed = []
        for j, chip in enumerate(chips):
            for a in range(n):
                copy(a, 1 + j, (*chip, c), me).wait_recv()
                fwd = copy(a, 4 + j, (*chip, c), sibling)
                fwd.start()
                passed.append(fwd)
        for a in range(n):
            copy(a, 0, sibling, me).wait_recv()
            for j, chip in enumerate(chips):
                copy(a, 4 + j, (*chip, 1 - c), me).wait_recv()
        for cp in first + passed:
            cp.wait_send()
        for cp in mine:
            cp.wait()

    return pl.pallas_call(
        body, name=name, in_specs=[HBM] * n, out_specs=[HBM] * n,
        out_shape=[jax.ShapeDtypeStruct((N_DEV, *s.shape), s.dtype) for s in shards],
        scratch_shapes=[pltpu.SemaphoreType.DMA((7 * n,)), pltpu.SemaphoreType.DMA((7 * n,)),
                        pltpu.SemaphoreType.DMA((n,))],
    )(*shards)


def _flip_y(x, y, c):
    return (x, 1 - y, c)


def _flip_x(x, y, c):
    return (1 - x, y, c)


def _flip_xy(x, y, c):
    return (1 - x, 1 - y, c)


SEM = pl.BlockSpec(memory_space=pltpu.SEMAPHORE)
SIDE_EFFECT = pltpu.SideEffectType.DATAFLOW_SIDE_EFFECTING


def _in_hbm(a):
    return pltpu.with_memory_space_constraint(a, pltpu.HBM)


def _copies_start(srcs, lands, plan, n_copies, *, name, after=None):
    ns, nl = len(srcs), len(lands)
    extra = [] if after is None else [after]

    def body(*refs):
        src_refs, land_refs = refs[:ns], refs[ns:ns + nl]
        send_sems, recv_sems = refs[ns + nl + len(extra):ns + nl + len(extra) + 2]
        token = refs[-1]
        for k, (src, dst, peer, _) in enumerate(plan(src_refs, land_refs, *_coords())):
            pltpu.make_async_remote_copy(src_ref=src, dst_ref=dst, send_sem=send_sems.at[k], recv_sem=recv_sems.at[k],
                                         device_id=peer, device_id_type=MESH).start()
        token[...] = jnp.zeros_like(token)

    bufs = [*srcs, *lands]
    res = pl.pallas_call(
        body, name=name, in_specs=[HBM] * (ns + nl) + [pl.BlockSpec(memory_space=pl.ANY)] * len(extra),
        out_specs=(SEM, SEM, *[HBM] * (ns + nl), pl.BlockSpec(memory_space=pltpu.VMEM)),
        out_shape=(pltpu.SemaphoreType.DMA((n_copies,)), pltpu.SemaphoreType.DMA((n_copies,)),
                   *[pltpu.HBM(b.shape, b.dtype) for b in bufs], jax.ShapeDtypeStruct((SUBLANES, LANES), F32)),
        input_output_aliases={i: 2 + i for i in range(ns + nl)},
        compiler_params=pltpu.CompilerParams(has_side_effects=SIDE_EFFECT),
    )(*[_in_hbm(b) for b in bufs], *extra)
    return res[0], res[1], list(res[2:2 + ns]), list(res[2 + ns:2 + ns + nl]), res[-1]


def _copies_wait(started, plan, after, *, name):
    send_sems, recv_sems, srcs, lands, _ = started
    ns, nl = len(srcs), len(lands)

    def body(*refs):
        src_refs, land_refs = refs[:ns], refs[ns:ns + nl]
        send_sems, recv_sems = refs[ns + nl:ns + nl + 2]
        for k, (src, dst, peer, here) in enumerate(plan(src_refs, land_refs, *_coords())):
            pltpu.make_async_remote_copy(src_ref=src, dst_ref=dst, send_sem=send_sems.at[k], recv_sem=recv_sems.at[k],
                                         device_id=peer, device_id_type=MESH).wait_send()
            pltpu.make_async_remote_copy(src_ref=src, dst_ref=here, send_sem=send_sems.at[k], recv_sem=recv_sems.at[k],
                                         device_id=peer, device_id_type=MESH).wait_recv()

    bufs = [*srcs, *lands]
    res = pl.pallas_call(
        body, name=name, in_specs=[HBM] * (ns + nl) + [SEM, SEM, pl.BlockSpec(memory_space=pl.ANY)],
        out_specs=[HBM] * (ns + nl), out_shape=[pltpu.HBM(b.shape, b.dtype) for b in bufs],
        input_output_aliases={i: i for i in range(ns + nl)},
        compiler_params=pltpu.CompilerParams(has_side_effects=SIDE_EFFECT),
    )(*bufs, send_sems, recv_sems, after)
    return list(res[ns:])


def _dev_index(dev):
    return 4 * dev[0] + 2 * dev[1] + dev[2]


def _ag_chips_plan(src_refs, land_refs, x, y, c):
    me = _dev_index((x, y, c))
    return [(src, land.at[me], peer, land.at[_dev_index(peer)])
            for src, land in zip(src_refs, land_refs) for peer in (_flip_y(x, y, c), _flip_x(x, y, c), _flip_xy(x, y, c))]


def _ag_sibling_plan(src_refs, land_refs, x, y, c):
    chips = [(x, y), (x, 1 - y), (1 - x, y), (1 - x, 1 - y)]
    return [(land.at[_dev_index((*chip, c))], land.at[_dev_index((*chip, c))], (x, y, 1 - c),
             land.at[_dev_index((*chip, 1 - c))]) for land in land_refs for chip in chips]


def _ag_direct_plan(src_refs, land_refs, x, y, c):
    me = _dev_index((x, y, c))
    plan = []
    for src, land in zip(src_refs, land_refs):
        for m in range(1, N_DEV):
            peer = (x + (m >> 2) * (1 - 2 * x), y + ((m >> 1) & 1) * (1 - 2 * y), c + (m & 1) * (1 - 2 * c))
            plan.append((src, land.at[me], peer, land.at[_dev_index(peer)]))
    return plan


def _rs_direct_plan(src_refs, land_refs, x, y, c):
    plan = []
    for src, land in zip(src_refs, land_refs):
        for m in range(1, N_DEV):
            peer = (x + (m >> 2) * (1 - 2 * x), y + ((m >> 1) & 1) * (1 - 2 * y), c + (m & 1) * (1 - 2 * c))
            plan.append((src.at[_dev_index(peer)], land.at[m - 1], peer, land.at[m - 1]))
    return plan


def _rs_start(grads, me, *, name, after=None):
    own = [lax.dynamic_index_in_dim(g, me, 0, keepdims=False) for g in grads]
    lands = [lax.empty((N_DEV - 1, *g.shape[1:]), g.dtype) for g in grads]
    return _copies_start(grads, lands, _rs_direct_plan, (N_DEV - 1) * len(grads), name=name, after=after), own


def _rs_finish(started, after, *, name):
    handle, own = started
    got = _copies_wait(handle, _rs_direct_plan, after, name=name)
    return [[o, land] for o, land in zip(own, got)]


def _gathered_cols(w8):
    return w8.transpose(1, 0, 2).reshape(w8.shape[1], -1)


def _pair_major(wt):
    return wt.reshape(3, ATTN_W // LANES, LANES, -1).transpose(1, 0, 2, 3).reshape(3 * ATTN_W, -1)


def kernel(x, norm1_g, w_in, attn_norm_g, hgrn_norm_g, hgrn_lb_logits, w_out, norm2_g, w_up, conv_w, conv_b, w_down, final_norm_g, loss_target, m_norm1_g, m_w_in, m_attn_norm_g, m_hgrn_norm_g, m_hgrn_lb_logits, m_w_out, m_norm2_g, m_w_up, m_conv_w, m_conv_b, m_w_down, m_final_norm_g, v_norm1_g, v_w_in, v_attn_norm_g, v_hgrn_norm_g, v_hgrn_lb_logits, v_w_out, v_norm2_g, v_w_up, v_conv_w, v_conv_b, v_w_down, v_final_norm_g):
    xs, target = x[0], loss_target[0]
    S, D = xs.shape
    NA = 3 * ATTN_W
    fng = final_norm_g.reshape(1, D)

    t = lambda a: a[0].T
    casts = [_sum_cast([w], BF16, name=f"cast_{nm}") for nm, w in
             (("w_in", t(w_in)), ("w_out", w_out[0]), ("w_up", t(w_up)), ("w_down", w_down[0]))]
    me = _dev_index(_coords())
    (g_in,) = _all_gather(casts[:1], name="ag_w_in")
    later = casts[1:] + [conv_w[0]]
    ag1 = _copies_start(later, [lax.empty((N_DEV, *s.shape), s.dtype) for s in later], _ag_chips_plan,
                        3 * len(later), name="ag_chips_start", after=g_in)
    wi = g_in.reshape(-1, D)
    wi_a = _pair_major(wi[:NA])

    u1, proj_a, proj_h, proj_f = _in_proj(xs, norm1_g, wi_a, wi, NA, name="in_proj", after=ag1[4])
    attn, lse = _attn_fwd(proj_a, name="attn_fwd")
    lands = _copies_wait(ag1, _ag_chips_plan, attn, name="ag_chips_wait")
    lands = [lax.dynamic_update_index_in_dim(l, s, me, 0) for l, s in zip(lands, later)]
    ag2 = _copies_start([], lands, _ag_sibling_plan, 4 * len(later), name="ag_sibling_start")
    rec, states = _hgrn_fwd(proj_h, proj_f, hgrn_lb_logits, name="hgrn_fwd", after=ag2[4])
    g_out, g_up, g_down, g_cw = _copies_wait(ag2, _ag_sibling_plan, rec, name="ag_sibling_wait")
    wo = g_out.reshape(-1, D)
    wu = g_up.reshape(-1, D)
    wd = g_down.reshape(-1, D)
    cw = _gathered_cols(g_cw)
    h1, u2, mixed = _out_proj(attn, rec, proj_h, xs, attn_norm_g, hgrn_norm_g, norm2_g, wo, name="out_proj")
    gate, gelu, vslope, act = _up_glu(u2, wu, cw, conv_b, name="up_glu")
    dh2, dh2b, d_fng, loss_part = _down_loss(act, wd, h1, fng, target, name="down_loss")

    dgate, dval, d_cw, d_cb = _dact_glu_bwd(dh2b, wd, gate, gelu, vslope, cw, name="dact_glu_bwd")
    dw_down = _mm_tn(act, dh2b, tm=256, name="dw_down")
    dh1, dh1b, d_n2g = _grad_norm_input([dgate, dval], [(wu, 0)], h1, norm2_g, dh2, name="du2_norm2_bwd")
    F = dgate.shape[1]
    dw_up = _mm_tn(dgate, u2, tm=256, rows=2 * F, name="dw_up_gate")
    dw_up = _mm_tn(dval, u2, tm=256, rows=2 * F, row_block=lambda i: i + F // 256, into=dw_up, name="dw_up_val")
    rs_ffn = _rs_start([dw_down.reshape(N_DEV, -1, D), dw_up.reshape(N_DEV, -1, D)], me, name="rs_ffn_start")
    dattn, delta, drec, dhg, d_ang, d_hng = _dmix_post_bwd(dh1b, wo, attn, rec, proj_h, attn_norm_g, hgrn_norm_g,
                                                          name="dmix_post_bwd", after=rs_ffn[0][4])
    dw_out = _mm_tn(mixed, dh1b, name="dw_out")
    rs_out = _rs_start([dw_out.reshape(N_DEV, -1, D)], me, name="rs_out_start")
    dproj_h, d_lbl = _hgrn_bwd(proj_h, proj_f, hgrn_lb_logits, states, drec, name="hgrn_bwd", after=rs_out[0][4])
    small = [("loss", loss_part, None, None, None),
             ("attn_norm_g", d_ang, attn_norm_g, m_attn_norm_g, v_attn_norm_g),
             ("hgrn_norm_g", d_hng, hgrn_norm_g, m_hgrn_norm_g, v_hgrn_norm_g),
             ("hgrn_lb_logits", d_lbl, hgrn_lb_logits, m_hgrn_lb_logits, v_hgrn_lb_logits),
             ("norm2_g", d_n2g, norm2_g, m_norm2_g, v_norm2_g),
             ("conv_b", d_cb, conv_b, m_conv_b, v_conv_b),
             ("final_norm_g", d_fng, final_norm_g, m_final_norm_g, v_final_norm_g)]
    pack = lambda arrs: jnp.concatenate([a.reshape(1, -1) for a in arrs], axis=1)
    small_own = [pack([s[1] for s in small]), d_cw]
    ag_small = _copies_start(small_own, [lax.empty((N_DEV, *s.shape), s.dtype) for s in small_own], _ag_direct_plan,
                             (N_DEV - 1) * len(small_own), name="ag_small_start")
    dproj_a = _attn_bwd(proj_a, dattn, lse, delta, name="attn_bwd")
    pairs = ATTN_W // LANES
    dw_in = _mm_tn(dproj_a, u1, tm=LANES, rows=wi.shape[0], row_block=lambda i: pairs * (i % 3) + i // 3,
                   name="dw_in_attn")
    dw_in = _mm_tn(dproj_h, u1, tm=256, rows=wi.shape[0], row_block=lambda i: i + NA // 256, into=dw_in,
                   name="dw_in_hgrn")
    dw_in = _mm_tn(dhg, u1, tm=256, rows=wi.shape[0], row_block=lambda i: i + (NA + 3 * HGRN_W) // 256, into=dw_in,
                   name="dw_in_gate")
    rs_in = _rs_start([dw_in.reshape(N_DEV, -1, D)], me, name="rs_in_start", after=ag_small[4])
    grad_x, _, d_n1g = _grad_norm_input([dproj_a, dproj_h, dhg], [(wi_a, 0), (wi, NA)], xs,
                                        norm1_g, dh1, name="du1_norm1_bwd", after=rs_in[0][4])

    res = {}

    def update(nm, parts, w, m, v, transposed=False):
        if transposed:
            res[nm] = [r.T[None] for r in _adamw(parts, t(w), t(m), t(v), name=f"adamw_{nm}")]
        else:
            res[nm] = _adamw(parts, w, m, v, name=f"adamw_{nm}")

    g_down, g_up = _rs_finish(rs_ffn, grad_x, name="rs_ffn_wait")
    update("w_down", g_down, w_down, m_w_down, v_w_down)
    update("w_up", g_up, w_up, m_w_up, v_w_up, transposed=True)
    (g_out,) = _rs_finish(rs_out, grad_x, name="rs_out_wait")
    update("w_out", g_out, w_out, m_w_out, v_w_out)
    (g_in,) = _rs_finish(rs_in, res["w_up"][1], name="rs_in_wait")
    update("w_in", g_in, w_in, m_w_in, v_w_in, transposed=True)

    g_small, g_dcw = [lax.dynamic_update_index_in_dim(l, s, me, 0)
                      for l, s in zip(_copies_wait(ag_small, _ag_direct_plan, grad_x, name="ag_small_wait"), small_own)]
    sm = _adamw_packed(g_small, [s[1].size for s in small], [None if s[2] is None else s[2:] for s in small],
                       name="adamw_small")
    for (nm, *_), r in zip(small, sm):
        res[nm] = r
    ncw = conv_w.shape[-1]
    mine_cw = lax.dynamic_slice_in_dim(g_dcw, me * ncw, ncw, axis=2)
    res["conv_w"] = _adamw([mine_cw], conv_w, m_conv_w, v_conv_w, name="adamw_conv_w")
    late, _ = lax.optimization_barrier((d_n1g, res["w_in"][1]))
    update("norm1_g", _all_gather([late], name="ag_norm1_grad"), norm1_g, m_norm1_g, v_norm1_g)

    loss = res["loss"][0][0, 0]
    order = ["norm1_g", "w_in", "attn_norm_g", "hgrn_norm_g", "hgrn_lb_logits", "w_out", "norm2_g", "w_up",
             "conv_w", "conv_b", "w_down", "final_norm_g"]
    return (loss, grad_x[None], *[res[nm][0] for nm in order], *[res[nm][1] for nm in order],
            *[res[nm][2] for nm in order], *[res[nm][3] for nm in order])
```

```python
import jax
import jax.numpy as jnp
from jax import lax
from jax.experimental import pallas as pl
from jax.experimental.pallas import tpu as pltpu

F32, BF16 = jnp.float32, jnp.bfloat16
NORM_EPS = 1e-6
ATTN_HEADS, HEAD_DIM, ATTN_BLOCK = 8, 64, 128
DILATIONS = (1, 4, 16)
ATTN_SCALE = HEAD_DIM ** -0.5
ATTN_W = ATTN_HEADS * HEAD_DIM
HGRN_HEADS, HGRN_DIM, HGRN_CHUNK = 4, 128, 64
HGRN_W = HGRN_HEADS * HGRN_DIM
ADAM_LR, ADAM_B1, ADAM_B2, ADAM_EPS, ADAM_WD, ADAM_STEP = 0.001, 0.9, 0.999, 1e-08, 0.01, 10
LANES, SUBLANES = 128, 8
VMEM_LIMIT_BYTES = 56 * 1024 * 1024
N_DEV = 8
MESH = pl.DeviceIdType.MESH
HBM = pl.BlockSpec(memory_space=pltpu.HBM)
HIGHEST = lax.Precision.HIGHEST


def _cparams(*sem):
    return pltpu.CompilerParams(dimension_semantics=sem, vmem_limit_bytes=VMEM_LIMIT_BYTES)


def _tile(n, pref):
    if n <= pref:
        return n
    t = (pref // LANES) * LANES
    while n % t:
        t -= LANES
    return t


def _resident(shape):
    return pl.BlockSpec(shape, lambda *_: (0,) * len(shape), pipeline_mode=pl.Buffered(1))


def _dot(a, b, dims, precision=None):
    return lax.dot_general(a, b, (dims, ((), ())), precision=precision, preferred_element_type=F32)


def _nn(a, b, precision=None):
    return _dot(a, b, ((1,), (0,)), precision)


def _nt(a, b):
    return _dot(a, b, ((1,), (1,)))


def _tn(a, b):
    return _dot(a, b, ((0,), (0,)))


def _sigmoid(x):
    return 1.0 / (1.0 + jnp.exp(-x))


def _rstd(x):
    return lax.rsqrt(jnp.mean(x * x, axis=-1, keepdims=True) + NORM_EPS)


def _norm_bwd(x, g, du):
    r = _rstd(x)
    xh = x * r
    dxh = du * g
    return r * (dxh - xh * jnp.mean(dxh * xh, axis=-1, keepdims=True)), du * xh


def _row_halves(tm):
    return [pl.ds(0, tm // 2), pl.ds(tm // 2, tm // 2)]


def _behind(after, n_in):
    if after is None:
        return [], [], (lambda body: body)
    return ([after], [pl.BlockSpec(memory_space=pl.ANY)],
            lambda body: (lambda *refs: body(*refs[:n_in], *refs[n_in + 1:])))


def _accumulate(ref, part, first):
    @pl.when(first)
    def _():
        ref[...] = part

    @pl.when(jnp.logical_not(first))
    def _():
        ref[...] += part


def _mm_tn(x, dy, *, name, tm=512, tn=1024, rows=None, row_block=None, into=None):
    S, M = x.shape
    N = dy.shape[1]
    tm, tn = _tile(M, tm), _tile(N, tn)
    row_block = row_block or (lambda i: i)

    def body(x_ref, dy_ref, *rest):
        o_ref, xt_ref = rest[-2:]

        @pl.when(pl.program_id(1) == 0)
        def _():
            xt_ref[...] = x_ref[...].T

        o_ref[...] = _nn(xt_ref[...], dy_ref[...]).astype(BF16)

    operands = [x, dy] + ([] if into is None else [into])
    return pl.pallas_call(
        body, name=name, grid=(M // tm, N // tn),
        in_specs=[pl.BlockSpec((S, tm), lambda i, j: (0, i)), pl.BlockSpec((S, tn), lambda i, j: (0, j))]
        + ([] if into is None else [pl.BlockSpec(memory_space=pl.ANY)]),
        out_specs=pl.BlockSpec((tm, tn), lambda i, j: (row_block(i), j)),
        out_shape=jax.ShapeDtypeStruct((rows or M, N), BF16),
        input_output_aliases={} if into is None else {2: 0},
        scratch_shapes=[pltpu.VMEM((tm, S), BF16)], compiler_params=_cparams("parallel", "arbitrary"),
    )(*operands)


def _in_proj(x, g, wt_attn, wt, row0, *, name, tm=512, after=None):
    S, D = x.shape
    NA, NH, W = wt_attn.shape[0], wt.shape[0] - row0, HGRN_W

    def body(x_ref, g_ref, wa_ref, w_ref, u_ref, a_ref, h_ref, f_ref):
        xv = x_ref[...]
        u = (xv * _rstd(xv) * g_ref[...]).astype(BF16)
        u_ref[...] = u
        a_ref[...] = _nt(u, wa_ref[...]).astype(BF16)
        ph = _nt(u, w_ref[row0:row0 + NH, :])
        h_ref[...] = ph.astype(BF16)
        f_ref[...] = ph[:, W:2 * W]

    row = lambda w: pl.BlockSpec((tm, w), lambda i: (i, 0))
    extra, extra_specs, adapt = _behind(after, 4)
    return pl.pallas_call(
        adapt(body), name=name, grid=(S // tm,),
        in_specs=[row(D), pl.BlockSpec((1, D), lambda i: (0, 0)), _resident(wt_attn.shape), _resident(wt.shape)]
        + extra_specs,
        out_specs=[row(D), row(NA), row(NH), row(W)],
        out_shape=[jax.ShapeDtypeStruct((S, D), BF16), jax.ShapeDtypeStruct((S, NA), BF16),
                   jax.ShapeDtypeStruct((S, NH), BF16), jax.ShapeDtypeStruct((S, W), F32)],
        compiler_params=_cparams("parallel"),
    )(x, g, wt_attn, wt, *extra)


PAIR_W = 3 * LANES
ATTN_UNROLL_FWD, ATTN_UNROLL_BWD = 8, 4


def _attn_masks(first):
    qi = lax.broadcasted_iota(jnp.int32, (ATTN_BLOCK, 2 * ATTN_BLOCK), 0)
    kj = lax.broadcasted_iota(jnp.int32, (ATTN_BLOCK, 2 * ATTN_BLOCK), 1)
    dist = qi + ATTN_BLOCK - kj
    valid = (dist >= 0) & (dist <= ATTN_BLOCK) & jnp.logical_or(kj >= ATTN_BLOCK, jnp.logical_not(first))
    lane = lax.broadcasted_iota(jnp.int32, (1, LANES), 1)
    return valid, lane


def _for_residue_blocks(S, d, fn):
    span = ATTN_BLOCK * d
    nb = S // span

    def step(n, carry):
        base = pl.multiple_of(n * span, span)
        for r in range(d):
            off = pl.multiple_of((r * nb + n) * ATTN_BLOCK, ATTN_BLOCK)
            fn(lambda ref, r=r: _block_rows(ref, base, r, d),
               lambda ref, val, r=r: _set_block_rows(ref, base, r, d, val), off)
        return carry

    lax.fori_loop(0, nb, step, 0)


def _for_blocks(S, unroll, fn):
    def step(i, carry):
        fn([(pl.multiple_of((i * unroll + u) * ATTN_BLOCK, ATTN_BLOCK), i * unroll + u) for u in range(unroll)])
        return carry

    lax.fori_loop(0, S // ATTN_BLOCK // unroll, step, 0)


def _head_value(x2, lane, e):
    return jnp.sum(jnp.where(lane == HEAD_DIM * e, x2, 0.0), axis=-1, keepdims=True)


def _block_rows(ref, base, r, d):
    if d == 1:
        return ref[pl.ds(base, ATTN_BLOCK), :]
    return ref.at[pl.ds(base, ATTN_BLOCK * d)][pl.ds(r, ATTN_BLOCK, stride=d), :]


def _set_block_rows(ref, base, r, d, val):
    if d == 1:
        ref[pl.ds(base, ATTN_BLOCK), :] = val
    else:
        ref.at[pl.ds(base, ATTN_BLOCK * d)][pl.ds(r, ATTN_BLOCK, stride=d), :] = val


def _order4_to_16(src, dst, pad):
    S = src.shape[0]
    q4, q16 = S // 4, S // 16
    for r in range(4):
        for a in range(4):
            for n in range(q16 // ATTN_BLOCK):
                rows = src.at[pl.ds(r * q4 + 4 * ATTN_BLOCK * n, 4 * ATTN_BLOCK)][pl.ds(a, ATTN_BLOCK, stride=4), :]
                dst[pl.ds(pad + (4 * a + r) * q16 + ATTN_BLOCK * n, ATTN_BLOCK), :] = rows.astype(dst.dtype)


def _order16_to_4(src, pad, dst):
    S = dst.shape[0]
    q4, q16 = S // 4, S // 16
    for r in range(4):
        for a in range(4):
            for n in range(q16 // ATTN_BLOCK):
                rows = src[pl.ds(pad + (4 * a + r) * q16 + ATTN_BLOCK * n, ATTN_BLOCK), :]
                dst.at[pl.ds(r * q4 + 4 * ATTN_BLOCK * n, 4 * ATTN_BLOCK)][pl.ds(a, ATTN_BLOCK, stride=4), :] = rows


def _regroup(S, d, pairs, tmp):
    for src, dst, pad in pairs:
        if d == 16:
            def to_tmp(rows, _, off, src=src):
                tmp[pl.ds(off, ATTN_BLOCK), :] = rows(src)

            _for_residue_blocks(S, 4, to_tmp)
            _order4_to_16(tmp, dst, pad)
    if d != 16:
        def to_dst(rows, _, off):
            for src, dst, pad in pairs:
                dst[pl.ds(pad + off, ATTN_BLOCK), :] = rows(src).astype(dst.dtype)

        _for_residue_blocks(S, d, to_dst)


def _split_pair(p_ref, qs, ks, vs, bk, bv):
    qs[...] = p_ref[:, 0:LANES].astype(F32) * ATTN_SCALE
    ks[...] = p_ref[:, LANES:2 * LANES].astype(F32)
    vs[...] = p_ref[:, 2 * LANES:3 * LANES].astype(F32)
    bk[0:ATTN_BLOCK, :] = jnp.zeros((ATTN_BLOCK, LANES), bk.dtype)
    bv[0:ATTN_BLOCK, :] = jnp.zeros((ATTN_BLOCK, LANES), bv.dtype)


def _attn_fwd(proj_a, *, name):
    S = proj_a.shape[0]

    def body(p_ref, o_ref, l_ref, qs, ks, vs, bq, bk, bv, bo, bl, to, tl):
        _split_pair(p_ref, qs, ks, vs, bk, bv)
        for d in DILATIONS:
            nb = S // (ATTN_BLOCK * d)
            _regroup(S, d, ((qs, bq, 0), (ks, bk, ATTN_BLOCK), (vs, bv, ATTN_BLOCK)), to)

            def blocks(group, nb=nb):
                lane = lax.broadcasted_iota(jnp.int32, (1, LANES), 1)
                heads = [(lane >= HEAD_DIM * e) & (lane < HEAD_DIM * (e + 1)) for e in range(LANES // HEAD_DIM)]
                wins = [pl.ds(off, 2 * ATTN_BLOCK) for off, _ in group]
                s = [[_nt(jnp.where(mh, bq[pl.ds(off, ATTN_BLOCK), :], jnp.zeros((ATTN_BLOCK, LANES), BF16)), bk[win, :])
                      for mh in heads] for (off, _), win in zip(group, wins)]
                p, m, l = [], [], []
                for (off, b), su in zip(group, s):
                    valid, _ = _attn_masks(jnp.bitwise_and(b, nb - 1) == 0)
                    sm = [jnp.where(valid, x, -jnp.inf) for x in su]
                    m.append([jnp.max(x, axis=-1, keepdims=True) for x in sm])
                    p.append([jnp.exp(x - mx) for x, mx in zip(sm, m[-1])])
                    l.append([jnp.sum(x, axis=-1, keepdims=True) for x in p[-1]])
                o = [[_nn(x.astype(BF16), bv[win, :]) for x in pu] for pu, win in zip(p, wins)]
                for (off, _), ou, mu, lu in zip(group, o, m, l):
                    o2 = jnp.zeros((ATTN_BLOCK, LANES), F32)
                    l2 = jnp.zeros((ATTN_BLOCK, LANES), F32)
                    for mh, oe, me_, le in zip(heads, ou, mu, lu):
                        o2 = jnp.where(mh, oe / le, o2)
                        l2 = jnp.where(mh, me_ + jnp.log(le), l2)
                    bo[pl.ds(off, ATTN_BLOCK), :] = o2
                    bl[pl.ds(off, ATTN_BLOCK), :] = l2

            _for_blocks(S, ATTN_UNROLL_FWD, blocks)

            if d == 16:
                _order16_to_4(bo, 0, to)
                _order16_to_4(bl, 0, tl)
            src_o, src_l = (to, tl) if d == 16 else (bo, bl)

            def merge(rows, set_rows, off, d=d, src_o=src_o, src_l=src_l):
                blk = pl.ds(off, ATTN_BLOCK)
                o2, l2 = src_o[blk, :], src_l[blk, :]
                if d != DILATIONS[0]:
                    lo, oo = rows(l_ref), rows(o_ref)
                    ln = jnp.maximum(lo, l2)
                    wa, wb = jnp.exp(lo - ln), jnp.exp(l2 - ln)
                    o2 = (wa * oo + wb * o2) / (wa + wb)
                    l2 = ln + jnp.log(wa + wb)
                set_rows(o_ref, o2)
                set_rows(l_ref, l2)

            _for_residue_blocks(S, min(d, 4), merge)

    slab = pl.BlockSpec((S, LANES), lambda p: (0, p))
    f32_slab, bf16_slab = pltpu.VMEM((S, LANES), F32), pltpu.VMEM((S, LANES), BF16)
    bf16_window = pltpu.VMEM((S + ATTN_BLOCK, LANES), BF16)
    return pl.pallas_call(
        body, name=name, grid=(ATTN_W // LANES,), in_specs=[pl.BlockSpec((S, PAIR_W), lambda p: (0, p))],
        out_specs=[slab, slab],
        out_shape=[jax.ShapeDtypeStruct((S, ATTN_W), F32), jax.ShapeDtypeStruct((S, ATTN_W), F32)],
        scratch_shapes=[f32_slab] * 3 + [bf16_slab, bf16_window, bf16_window] + [f32_slab] * 4,
        compiler_params=_cparams("parallel"),
    )(proj_a)


def _attn_bwd(proj_a, do, lse, delta, *, name):
    S = proj_a.shape[0]

    def body(p_ref, do_ref, lse_ref, dl_ref, o_ref, qs, ks, vs, dqs, dks, dvs, bq, bk, bv, bdo, blse, bdl, bdq, bdk, bdv,
             tmp):
        _split_pair(p_ref, qs, ks, vs, bk, bv)
        bdk[0:ATTN_BLOCK, :] = jnp.zeros((ATTN_BLOCK, LANES), F32)
        bdv[0:ATTN_BLOCK, :] = jnp.zeros((ATTN_BLOCK, LANES), F32)
        for d in DILATIONS:
            nb = S // (ATTN_BLOCK * d)
            _regroup(S, d, ((qs, bq, 0), (ks, bk, ATTN_BLOCK), (vs, bv, ATTN_BLOCK), (do_ref, bdo, 0),
                            (lse_ref, blse, 0), (dl_ref, bdl, 0)), tmp)

            def blocks(group, nb=nb):
                lane = lax.broadcasted_iota(jnp.int32, (1, LANES), 1)
                heads = [(lane >= HEAD_DIM * e) & (lane < HEAD_DIM * (e + 1)) for e in range(LANES // HEAD_DIM)]
                zero = jnp.zeros((ATTN_BLOCK, LANES), BF16)
                chains = [(off, b, e, mh) for off, b in group for e, mh in enumerate(heads)]
                qm = [jnp.where(mh, bq[pl.ds(off, ATTN_BLOCK), :], zero) for off, _, _, mh in chains]
                dom = [jnp.where(mh, bdo[pl.ds(off, ATTN_BLOCK), :], zero) for off, _, _, mh in chains]
                s = [_nt(x, bk[pl.ds(off, 2 * ATTN_BLOCK), :]) for x, (off, _, _, _) in zip(qm, chains)]
                dp = [_nt(x, bv[pl.ds(off, 2 * ATTN_BLOCK), :]) for x, (off, _, _, _) in zip(dom, chains)]
                p, ds = [], []
                for (off, b, e, _), sc, dpc in zip(chains, s, dp):
                    valid, _ = _attn_masks(jnp.bitwise_and(b, nb - 1) == 0)
                    blk = pl.ds(off, ATTN_BLOCK)
                    pc = jnp.where(valid, jnp.exp(sc - _head_value(blse[blk, :], lane, e)), 0.0)
                    ds.append((pc * (dpc - _head_value(bdl[blk, :], lane, e))).astype(BF16))
                    p.append(pc.astype(BF16))
                dq = [_nn(x, bk[pl.ds(off, 2 * ATTN_BLOCK), :]) for x, (off, _, _, _) in zip(ds, chains)]
                dk = [_tn(x, y) for x, y in zip(ds, qm)]
                dv = [_tn(x, y) for x, y in zip(p, dom)]
                nh = len(heads)
                for u, (off, _) in enumerate(group):
                    dq2 = jnp.zeros((ATTN_BLOCK, LANES), F32)
                    for mh, x in zip(heads, dq[nh * u:nh * (u + 1)]):
                        dq2 = jnp.where(mh, x, dq2)
                    bdq[pl.ds(off, ATTN_BLOCK), :] = dq2 * ATTN_SCALE
                    for acc, grads in ((bdk, dk), (bdv, dv)):
                        win_grad = sum(grads[nh * u + 1:nh * (u + 1)], grads[nh * u])
                        acc[pl.ds(off, ATTN_BLOCK), :] += win_grad[:ATTN_BLOCK]
                        acc[pl.ds(off + ATTN_BLOCK, ATTN_BLOCK), :] = win_grad[ATTN_BLOCK:]

            _for_blocks(S, ATTN_UNROLL_BWD, blocks)

            outs = ((dqs, bdq, 0), (dks, bdk, ATTN_BLOCK), (dvs, bdv, ATTN_BLOCK))
            if d == 16:
                for acc, grad, pad in outs:
                    _order16_to_4(grad, pad, tmp)

                    def add(rows, set_rows, off, acc=acc):
                        set_rows(acc, rows(acc) + tmp[pl.ds(off, ATTN_BLOCK), :])

                    _for_residue_blocks(S, 4, add)
            else:
                def scatter(rows, set_rows, off, d=d):
                    for acc, grad, pad in outs:
                        part = grad[pl.ds(pad + off, ATTN_BLOCK), :]
                        set_rows(acc, part if d == DILATIONS[0] else rows(acc) + part)

                _for_residue_blocks(S, d, scatter)
        o_ref[:, 0:LANES] = dqs[...].astype(BF16)
        o_ref[:, LANES:2 * LANES] = dks[...].astype(BF16)
        o_ref[:, 2 * LANES:3 * LANES] = dvs[...].astype(BF16)

    slab = pl.BlockSpec((S, LANES), lambda p: (0, p))
    pair = pl.BlockSpec((S, PAIR_W), lambda p: (0, p))
    f32_slab, bf16_slab = pltpu.VMEM((S, LANES), F32), pltpu.VMEM((S, LANES), BF16)
    f32_window, bf16_window = pltpu.VMEM((S + ATTN_BLOCK, LANES), F32), pltpu.VMEM((S + ATTN_BLOCK, LANES), BF16)
    return pl.pallas_call(
        body, name=name, grid=(ATTN_W // LANES,), in_specs=[pair, slab, slab, slab], out_specs=pair,
        out_shape=jax.ShapeDtypeStruct(proj_a.shape, BF16),
        scratch_shapes=[f32_slab] * 6 + [bf16_slab, bf16_window, bf16_window, bf16_slab, f32_slab, f32_slab,
                                         f32_slab, f32_window, f32_window, f32_slab],
        compiler_params=_cparams("parallel"),
    )(proj_a, do, lse, delta)


HG_T = 2 * HGRN_CHUNK
HG_GROUPS = 2
HG_STEP = HG_GROUPS * HG_T


def _hgrn_consts():
    row = lax.broadcasted_iota(jnp.int32, (HG_T, HG_T), 0)
    col = lax.broadcasted_iota(jnp.int32, (HG_T, HG_T), 1)
    same = (row >= HGRN_CHUNK) == (col >= HGRN_CHUNK)
    return row, same & (col <= row), same & (col >= row)


def _lower_bound(logits_ref):
    l0, l1 = logits_ref[0:1, :], logits_ref[1:2, :]
    mx = jnp.maximum(l0, l1)
    e0, e1 = jnp.exp(l0 - mx), jnp.exp(l1 - mx)
    return e0 / (e0 + e1)


def _hgrn_chains():
    chains = [(g, h) for g in range(HG_GROUPS) for h in range(HGRN_HEADS)]
    rows = [pl.ds(HG_T * g, HG_T) for g, _ in chains]
    lanes = [slice(HGRN_DIM * h, HGRN_DIM * (h + 1)) for _, h in chains]
    return chains, rows, lanes


def _hgrn_gates(qs, fs, lbs, row, causal):
    C = HGRN_CHUNK
    tri = jnp.where(causal, 1.0, 0.0).astype(F32)
    sgs = [_sigmoid(f) for f in fs]
    forgets = [lb + (1.0 - lb) * sg for lb, sg in zip(lbs, sgs)]
    logfs = [jnp.log(forget) for forget in forgets]
    bs = [_nn(tri, logf, HIGHEST) for logf in logfs]
    out = []
    for q, sg, forget, logf, b in zip(qs, sgs, forgets, logfs, bs):
        key = 1.0 - forget
        bend0 = jnp.sum(logf[:C], axis=0, keepdims=True)
        bend1 = jnp.sum(logf[C:], axis=0, keepdims=True)
        bend = jnp.where(row < C, bend0, bend1)
        eb, emb, eend = jnp.exp(b), jnp.exp(-b), jnp.exp(bend - b)
        sq = _sigmoid(q)
        out.append(dict(sg=sg, forget=forget, key=key, bend0=bend0, bend1=bend1, eb=eb, emb=emb, eend=eend, sq=sq,
                        qd=q * sq * eb, ki=key * emb, ke=key * eend))
    return out


def _hgrn_fwd(proj, proj_f, logits, *, name, after=None):
    S = proj.shape[0]
    W, C = HGRN_W, HGRN_CHUNK

    def body(q_ref, f_ref, i_ref, lg_ref, rec_ref, st_ref, s_ref):
        @pl.when(pl.program_id(0) == 0)
        def _():
            s_ref[...] = jnp.zeros_like(s_ref)

        row, causal, _ = _hgrn_consts()
        lb_all = _lower_bound(lg_ref)
        chains, rows, lanes = _hgrn_chains()
        n = range(len(chains))
        gts = _hgrn_gates([q_ref[rows[c], lanes[c]].astype(F32) for c in n], [f_ref[rows[c], lanes[c]] for c in n],
                          [lb_all[:, lanes[c]] for c in n], row, causal)
        qd, ki, ke = ([gt[k].astype(BF16) for gt in gts] for k in ("qd", "ki", "ke"))
        iv = [i_ref[rows[c], lanes[c]].astype(BF16) for c in n]
        a = [_nt(qd[c], ki[c]) for c in n]
        u0 = [_tn(iv[c][:C], ke[c][:C]) for c in n]
        u1 = [_tn(iv[c][C:], ke[c][C:]) for c in n]
        state = [s_ref[h] for h in range(HGRN_HEADS)]
        s0, s1 = [], []
        for c, (g, h) in enumerate(chains):
            s0.append(state[h])
            s1.append(jnp.exp(gts[c]["bend0"]) * s0[c] + u0[c])
            state[h] = jnp.exp(gts[c]["bend1"]) * s1[c] + u1[c]
        o0 = [_nt(qd[c][:C], s0[c].astype(BF16)) for c in n]
        o1 = [_nt(qd[c][C:], s1[c].astype(BF16)) for c in n]
        o = [_nn(jnp.where(causal, a[c], 0.0).astype(BF16), iv[c]) for c in n]
        for c, (g, h) in enumerate(chains):
            st_ref[2 * g, h] = s0[c]
            st_ref[2 * g + 1, h] = s1[c]
            rec_ref[rows[c], lanes[c]] = o[c] + jnp.concatenate([o0[c], o1[c]], axis=0)
        for h in range(HGRN_HEADS):
            s_ref[h] = state[h]

    blk = lambda j: pl.BlockSpec((HG_STEP, W), lambda t: (t, j))
    extra, extra_specs, adapt = _behind(after, 4)
    return pl.pallas_call(
        adapt(body), name=name, grid=(S // HG_STEP,),
        in_specs=[blk(0), blk(0), blk(2), pl.BlockSpec((2, W), lambda t: (0, 0))] + extra_specs,
        out_specs=[blk(0), pl.BlockSpec((2 * HG_GROUPS, HGRN_HEADS, HGRN_DIM, HGRN_DIM), lambda t: (t, 0, 0, 0))],
        out_shape=[jax.ShapeDtypeStruct((S, W), F32),
                   jax.ShapeDtypeStruct((S // C, HGRN_HEADS, HGRN_DIM, HGRN_DIM), F32)],
        scratch_shapes=[pltpu.VMEM((HGRN_HEADS, HGRN_DIM, HGRN_DIM), F32)],
        compiler_params=_cparams("arbitrary"),
    )(proj, proj_f, proj, logits, *extra)


def _hgrn_bwd(proj, proj_f, logits, states, drec, *, name, after=None):
    S = proj.shape[0]
    W, C = HGRN_W, HGRN_CHUNK
    nt = S // HG_STEP

    def body(q_ref, f_ref, i_ref, lg_ref, st_ref, do_ref, dp_ref, dlg_ref, ds_ref, dlb_ref):
        t = pl.program_id(0)

        @pl.when(t == 0)
        def _():
            ds_ref[...] = jnp.zeros_like(ds_ref)
            dlb_ref[...] = jnp.zeros_like(dlb_ref)

        row, causal, anti = _hgrn_consts()
        lb_all = _lower_bound(lg_ref)
        chains, rows, lanes = _hgrn_chains()
        n = range(len(chains))
        qs, lbs = [q_ref[rows[c], lanes[c]].astype(F32) for c in n], [lb_all[:, lanes[c]] for c in n]
        gts = _hgrn_gates(qs, [f_ref[rows[c], lanes[c]] for c in n], lbs, row, causal)
        qd, ki, ke = ([gt[k] for gt in gts] for k in ("qd", "ki", "ke"))
        qdb, kib, keb = ([x.astype(BF16) for x in xs] for xs in (qd, ki, ke))
        iv = [i_ref[rows[c], lanes[c]].astype(BF16) for c in n]
        dob = [do_ref[rows[c], lanes[c]].astype(BF16) for c in n]
        s0 = [st_ref[2 * g, h] for g, h in chains]
        s1 = [st_ref[2 * g + 1, h] for g, h in chains]
        dec0, dec1 = [jnp.exp(gt["bend0"]) for gt in gts], [jnp.exp(gt["bend1"]) for gt in gts]
        a = [_nt(qdb[c], kib[c]) for c in n]
        da = [_nt(dob[c], iv[c]) for c in n]
        dqd1 = [_nn(dob[c][C:], s1[c].astype(BF16)) for c in n]
        dqd0 = [_nn(dob[c][:C], s0[c].astype(BF16)) for c in n]
        t1 = [_tn(dob[c][C:], qdb[c][C:]) for c in n]
        t0 = [_tn(dob[c][:C], qdb[c][:C]) for c in n]
        carry = [ds_ref[h] for h in range(HGRN_HEADS)]
        ds1, ds0 = [None] * len(chains), [None] * len(chains)
        for c in reversed(n):
            h = chains[c][1]
            ds1[c] = carry[h]
            ds0[c] = dec1[c] * ds1[c] + t1[c]
            carry[h] = dec0[c] * ds0[c] + t0[c]
        for h in range(HGRN_HEADS):
            ds_ref[h] = carry[h]
        ds1b, ds0b = [x.astype(BF16) for x in ds1], [x.astype(BF16) for x in ds0]
        a = [jnp.where(causal, x, 0.0).astype(BF16) for x in a]
        da = [jnp.where(causal, x, 0.0).astype(BF16) for x in da]
        di1 = [_nt(keb[c][C:], ds1b[c]) for c in n]
        dke1 = [_nn(iv[c][C:], ds1b[c]) for c in n]
        di0 = [_nt(keb[c][:C], ds0b[c]) for c in n]
        dke0 = [_nn(iv[c][:C], ds0b[c]) for c in n]
        dqd_a = [_nn(da[c], kib[c]) for c in n]
        dki = [_tn(da[c], qdb[c]) for c in n]
        di_a = [_tn(a[c], dob[c]) for c in n]
        dqd, dke, db = [], [], []
        for c in n:
            ddec1 = jnp.sum(ds1[c] * s1[c], axis=0, keepdims=True)
            ddec0 = jnp.sum(ds0[c] * s0[c], axis=0, keepdims=True)
            dqd.append(dqd_a[c] + jnp.concatenate([dqd0[c], dqd1[c]], axis=0))
            h = chains[c][1]
            dp_ref[rows[c], 2 * W + HGRN_DIM * h:2 * W + HGRN_DIM * (h + 1)] = (
                di_a[c] + jnp.concatenate([di0[c], di1[c]], axis=0)).astype(BF16)
            dke.append(jnp.concatenate([dke0[c], dke1[c]], axis=0))
            gke = dke[c] * ke[c]
            dbend0 = jnp.sum(gke[:C], axis=0, keepdims=True) + ddec0 * dec0[c]
            dbend1 = jnp.sum(gke[C:], axis=0, keepdims=True) + ddec1 * dec1[c]
            dbc = dqd[c] * qd[c] - dki[c] * ki[c] - gke
            db.append(dbc + jnp.where(row == C - 1, dbend0, 0.0) + jnp.where(row == HG_T - 1, dbend1, 0.0))
        tri = jnp.where(anti, 1.0, 0.0).astype(F32)
        dlogf = [_nn(tri, db[c], HIGHEST) for c in n]
        for c in n:
            gt, lb, q, h = gts[c], lbs[c], qs[c], chains[c][1]
            dforget = dlogf[c] / gt["forget"] - (dki[c] * gt["emb"] + dke[c] * gt["eend"])
            sg, sq = gt["sg"], gt["sq"]
            dp_ref[rows[c], W + HGRN_DIM * h:W + HGRN_DIM * (h + 1)] = (
                dforget * (1.0 - lb) * sg * (1.0 - sg)).astype(BF16)
            dlb_ref[:, lanes[c]] += jnp.sum(dforget * (1.0 - sg), axis=0, keepdims=True)
            dp_ref[rows[c], lanes[c]] = (dqd[c] * gt["eb"] * sq * (1.0 + q * (1.0 - sq))).astype(BF16)

        @pl.when(t == nt - 1)
        def _():
            dl0 = dlb_ref[...] * lb_all * (1.0 - lb_all)
            dlg_ref[0:1, :] = dl0
            dlg_ref[1:2, :] = -dl0

    blk = lambda j: pl.BlockSpec((HG_STEP, W), lambda t: (nt - 1 - t, j))
    full = pl.BlockSpec((2, W), lambda t: (0, 0))
    extra, extra_specs, adapt = _behind(after, 6)
    return pl.pallas_call(
        adapt(body), name=name, grid=(nt,),
        in_specs=[blk(0), blk(0), blk(2), full,
                  pl.BlockSpec((2 * HG_GROUPS, HGRN_HEADS, HGRN_DIM, HGRN_DIM), lambda t: (nt - 1 - t, 0, 0, 0)), blk(0)]
        + extra_specs,
        out_specs=[pl.BlockSpec((HG_STEP, 3 * W), lambda t: (nt - 1 - t, 0)), full],
        out_shape=[jax.ShapeDtypeStruct((S, 3 * W), BF16), jax.ShapeDtypeStruct((2, W), F32)],
        scratch_shapes=[pltpu.VMEM((HGRN_HEADS, HGRN_DIM, HGRN_DIM), F32), pltpu.VMEM((1, W), F32)],
        compiler_params=_cparams("arbitrary"),
    )(proj, proj_f, proj, logits, states, drec, *extra)


def _out_proj(attn, rec, proj_h, x, g_attn, g_hgrn, g_norm2, w_out, *, name, tm=512):
    S, D = x.shape
    AW, W = ATTN_W, HGRN_W

    def body(a_ref, r_ref, hg_ref, x_ref, ga_ref, gh_ref, g2_ref, w_ref, h_ref, u_ref, m_ref):
        av = a_ref[...]
        m_ref[:, :AW] = (av * _rstd(av) * ga_ref[...]).astype(BF16)
        for h in range(HGRN_HEADS):
            sl = slice(HGRN_DIM * h, HGRN_DIM * (h + 1))
            rv, hg = r_ref[:, sl], hg_ref[:, sl].astype(F32)
            m_ref[:, AW + HGRN_DIM * h:AW + HGRN_DIM * (h + 1)] = (
                (rv * _rstd(rv) * gh_ref[:, sl]) * (hg * _sigmoid(hg))).astype(BF16)
        h1 = x_ref[...] + _nn(m_ref[...], w_ref[...])
        h_ref[...] = h1
        u_ref[...] = (h1 * _rstd(h1) * g2_ref[...]).astype(BF16)

    row = lambda w, j=0: pl.BlockSpec((tm, w), lambda i: (i, j))
    vec = lambda w: pl.BlockSpec((1, w), lambda i: (0, 0))
    return pl.pallas_call(
        body, name=name, grid=(S // tm,),
        in_specs=[row(AW), row(W), row(W, 3), row(D), vec(AW), vec(W), vec(D), _resident(w_out.shape)],
        out_specs=[row(D), row(D), row(AW + W)],
        out_shape=[jax.ShapeDtypeStruct((S, D), F32), jax.ShapeDtypeStruct((S, D), BF16),
                   jax.ShapeDtypeStruct((S, AW + W), BF16)],
        compiler_params=_cparams("parallel"),
    )(attn, rec, proj_h, x, g_attn, g_hgrn, g_norm2, w_out)


def _dmix_post_bwd(dh1b, w_out, attn, rec, proj_h, g_attn, g_hgrn, *, name, tm=512, after=None):
    S, D = dh1b.shape
    AW, W = ATTN_W, HGRN_W

    def body(dh_ref, w_ref, a_ref, r_ref, hg_ref, ga_ref, gh_ref, do_ref, dl_ref, dr_ref, dhg_ref, dga_ref, dgh_ref):
        first = pl.program_id(0) == 0
        dmix = _nt(dh_ref[...], w_ref[...])
        av = a_ref[...]
        dov, dga = _norm_bwd(av, ga_ref[...], dmix[:, :AW])
        do_ref[...] = dov
        shift = HEAD_DIM.bit_length() - 1
        hi = lax.shift_right_logical(lax.broadcasted_iota(jnp.int32, (AW, AW), 0), shift)
        hj = lax.shift_right_logical(lax.broadcasted_iota(jnp.int32, (AW, AW), 1), shift)
        prod = dov * av
        hi_part = prod.astype(BF16)
        lo_part = (prod - hi_part.astype(F32)).astype(BF16)
        same_head = jnp.where(hi == hj, 1.0, 0.0).astype(BF16)
        dl_ref[...] = _nn(hi_part, same_head) + _nn(lo_part, same_head)
        _accumulate(dga_ref, jnp.sum(dga, axis=0, keepdims=True), first)

        @pl.when(first)
        def _():
            dgh_ref[...] = jnp.zeros_like(dgh_ref)

        for h in range(HGRN_HEADS):
            sl = slice(HGRN_DIM * h, HGRN_DIM * (h + 1))
            rv, hg, gv = r_ref[:, sl], hg_ref[:, sl].astype(F32), gh_ref[:, sl]
            dout = dmix[:, AW + HGRN_DIM * h:AW + HGRN_DIM * (h + 1)]
            sg = _sigmoid(hg)
            drv, dgh = _norm_bwd(rv, gv, dout * (hg * sg))
            dr_ref[:, sl] = drv
            dgh_ref[:, sl] += jnp.sum(dgh, axis=0, keepdims=True)
            dhg_ref[:, sl] = (dout * (rv * _rstd(rv) * gv) * (sg * (1.0 + hg * (1.0 - sg)))).astype(BF16)

    row = lambda w, j=0: pl.BlockSpec((tm, w), lambda i: (i, j))
    vec = lambda w: pl.BlockSpec((1, w), lambda i: (0, 0))
    extra, extra_specs, adapt = _behind(after, 7)
    return pl.pallas_call(
        adapt(body), name=name, grid=(S // tm,),
        in_specs=[row(D), _resident(w_out.shape), row(AW), row(W), row(W, 3), vec(AW), vec(W)] + extra_specs,
        out_specs=[row(AW), row(AW), row(W), row(W), vec(AW), vec(W)],
        out_shape=[jax.ShapeDtypeStruct((S, AW), F32), jax.ShapeDtypeStruct((S, AW), F32),
                   jax.ShapeDtypeStruct((S, W), F32), jax.ShapeDtypeStruct((S, W), BF16),
                   jax.ShapeDtypeStruct((1, AW), F32), jax.ShapeDtypeStruct((1, W), F32)],
        compiler_params=_cparams("arbitrary"),
    )(dh1b, w_out, attn, rec, proj_h, g_attn, g_hgrn, *extra)


def _conv_act(g, g1, g2, w_ref, b_ref):
    c = b_ref[...] + w_ref[0:1, :] * g2 + w_ref[1:2, :] * g1 + w_ref[2:3, :] * g
    return c, 0.5 * (1.0 + lax.erf(c * (2.0 ** -0.5)))


def _shift_down(g, halo, row):
    g1 = jnp.where(row == 0, halo[7:8], pltpu.roll(g, 1, 0))
    g2 = jnp.where(row == 0, halo[6:7], jnp.where(row == 1, halo[7:8], pltpu.roll(g, 2, 0)))
    return g1, g2


def _shift_up(x, halo, row):
    n = x.shape[0]
    x1 = jnp.where(row == n - 1, halo[0:1], pltpu.roll(x, n - 1, 0))
    x2 = jnp.where(row == n - 2, halo[0:1], jnp.where(row == n - 1, halo[1:2], pltpu.roll(x, n - 2, 0)))
    return x1, x2


def _up_glu(u, wt_up, conv_w, conv_b, *, name, tm=1024, tn=1408):
    S, D = u.shape
    F = wt_up.shape[0] // 2
    tn = _tile(F, tn)
    nf = F // tn

    def body(u_ref, wg_ref, wv_ref, cw_ref, cb_ref, g_ref, ge_ref, t_ref, a_ref, halo_ref):
        i, j = pl.program_id(0), pl.program_id(1)

        @pl.when(i == 0)
        def _():
            halo_ref[j] = jnp.zeros((SUBLANES, tn), F32)

        uv = u_ref[...]
        g, v = _nt(uv, wg_ref[...]), _nt(uv, wv_ref[...])
        row = lax.broadcasted_iota(jnp.int32, (tm, tn), 0)
        g1, g2 = _shift_down(g, halo_ref[j], row)
        c, cdf = _conv_act(g, g1, g2, cw_ref, cb_ref)
        gelu = c * cdf
        pdf = jnp.exp(-0.5 * c * c) * (1.0 / (2.0 * jnp.pi) ** 0.5)
        a_ref[...] = (gelu * v).astype(BF16)
        g_ref[...] = g.astype(BF16)
        ge_ref[...] = gelu.astype(BF16)
        t_ref[...] = (v * (cdf + c * pdf)).astype(BF16)
        halo_ref[j] = g[tm - SUBLANES:, :]

    col = pl.BlockSpec((tm, tn), lambda i, j: (i, j))
    out = jax.ShapeDtypeStruct((S, F), BF16)
    return pl.pallas_call(
        body, name=name, grid=(S // tm, nf),
        in_specs=[pl.BlockSpec((tm, D), lambda i, j: (i, 0)), pl.BlockSpec((tn, D), lambda i, j: (j, 0)),
                  pl.BlockSpec((tn, D), lambda i, j: (j + nf, 0)), pl.BlockSpec((3, tn), lambda i, j: (0, j)),
                  pl.BlockSpec((1, tn), lambda i, j: (0, j))],
        out_specs=[col, col, col, col], out_shape=[out, out, out, out],
        scratch_shapes=[pltpu.VMEM((nf, SUBLANES, tn), F32)], compiler_params=_cparams("arbitrary", "arbitrary"),
    )(u, wt_up, wt_up, conv_w, conv_b)


def _dact_glu_bwd(dh2b, w_down, gate, gelu, vslope, conv_w, *, name, tm=1024, tn=1408):
    S, D = dh2b.shape
    F = gate.shape[1]
    tn = _tile(F, tn)
    nf, ni = F // tn, S // tm

    def body(dh_ref, wd_ref, g_ref, ge_ref, t_ref, cw_ref, dg_ref, dv_ref, dcw_ref, dcb_ref, halo_ref, acc_ref):
        i, j = pl.program_id(0), pl.program_id(1)

        @pl.when(i == 0)
        def _():
            halo_ref[j] = jnp.zeros((SUBLANES, tn), F32)
            acc_ref[j] = jnp.zeros((SUBLANES, tn), F32)

        g = g_ref[...].astype(F32)
        row = lax.broadcasted_iota(jnp.int32, (tm, tn), 0)
        da = _nt(dh_ref[...], wd_ref[...])
        dv_ref[...] = (da * ge_ref[...].astype(F32)).astype(BF16)
        dc = da * t_ref[...].astype(F32)
        d1, d2 = _shift_up(dc, halo_ref[j], row)
        dg_ref[...] = (cw_ref[2:3, :] * dc + cw_ref[1:2, :] * d1 + cw_ref[0:1, :] * d2).astype(BF16)
        halo_ref[j] = dc[:SUBLANES, :]
        for k, t in enumerate((d2 * g, d1 * g, dc * g, dc)):
            acc_ref[j, k:k + 1, :] += jnp.sum(t, axis=0, keepdims=True)

        @pl.when((i == ni - 1) & (j == nf - 1))
        def _():
            for jj in range(nf):
                dcw_ref[:, jj * tn:(jj + 1) * tn] = acc_ref[jj, 0:3, :]
                dcb_ref[:, jj * tn:(jj + 1) * tn] = acc_ref[jj, 3:4, :]

    tile = pl.BlockSpec((tm, tn), lambda i, j: (ni - 1 - i, j))
    return pl.pallas_call(
        body, name=name, grid=(ni, nf),
        in_specs=[pl.BlockSpec((tm, D), lambda i, j: (ni - 1 - i, 0)), pl.BlockSpec((tn, D), lambda i, j: (j, 0)),
                  tile, tile, tile, pl.BlockSpec((3, tn), lambda i, j: (0, j))],
        out_specs=[tile, tile, pl.BlockSpec((3, F), lambda i, j: (0, 0)), pl.BlockSpec((1, F), lambda i, j: (0, 0))],
        out_shape=[jax.ShapeDtypeStruct((S, F), BF16), jax.ShapeDtypeStruct((S, F), BF16),
                   jax.ShapeDtypeStruct((3, F), F32), jax.ShapeDtypeStruct((1, F), F32)],
        scratch_shapes=[pltpu.VMEM((nf, SUBLANES, tn), F32), pltpu.VMEM((nf, SUBLANES, tn), F32)],
        compiler_params=_cparams("arbitrary", "arbitrary"),
    )(dh2b, w_down, gate, gelu, vslope, conv_w)


def _down_loss(act, w_down, h1, g, target, *, name, tm=512):
    S, F = act.shape
    D = h1.shape[1]

    def body(a_ref, w_ref, h_ref, g_ref, t_ref, dh_ref, dhb_ref, dg_ref, loss_ref):
        first = pl.program_id(0) == 0
        h2 = h_ref[...] + _nn(a_ref[...], w_ref[...])
        gv = g_ref[...]
        r = _rstd(h2)
        xh = h2 * r
        err = xh * gv - t_ref[...]
        part_loss = 0.5 * jnp.sum(jnp.mean(err * err, axis=-1, keepdims=True), axis=0, keepdims=True)
        dy = err * (1.0 / D)
        dxh = dy * gv
        dh = r * (dxh - xh * jnp.mean(dxh * xh, axis=-1, keepdims=True))
        dh_ref[...] = dh
        dhb_ref[...] = dh.astype(BF16)
        _accumulate(dg_ref, jnp.sum(dy * xh, axis=0, keepdims=True), first)
        _accumulate(loss_ref, jnp.broadcast_to(part_loss, (1, LANES)), first)

    row = lambda w: pl.BlockSpec((tm, w), lambda i: (i, 0))
    vec = lambda w: pl.BlockSpec((1, w), lambda i: (0, 0))
    return pl.pallas_call(
        body, name=name, grid=(S // tm,), in_specs=[row(F), _resident(w_down.shape), row(D), vec(D), row(D)],
        out_specs=[row(D), row(D), vec(D), vec(LANES)],
        out_shape=[jax.ShapeDtypeStruct((S, D), F32), jax.ShapeDtypeStruct((S, D), BF16),
                   jax.ShapeDtypeStruct((1, D), F32), jax.ShapeDtypeStruct((1, LANES), F32)],
        compiler_params=_cparams("arbitrary"),
    )(act, w_down, h1, g, target)


def _grad_norm_input(pieces, ws, x, g, add, *, name, tm=512, after=None):
    S, D = x.shape
    widths = [p.shape[1] for p in pieces]
    n, nw = len(pieces), len(ws)
    where, wi, off = [], 0, ws[0][1]
    for wd in widths:
        if off == ws[wi][0].shape[0]:
            wi, off = wi + 1, ws[wi + 1][1]
        where.append((wi, off))
        off += wd
    ws = [w for w, _ in ws]

    def body(*refs):
        p_refs, w_refs = refs[:n], refs[n:n + nw]
        x_ref, g_ref, add_ref, dx_ref, dxb_ref, dg_ref = refs[n + nw:]
        halves = _row_halves(tm)
        du = []
        for rows in halves:
            terms = [_nn(p_refs[k][rows, :], w_refs[wi][off:off + widths[k], :]) for k, (wi, off) in enumerate(where)]
            du.append(sum(terms[1:], terms[0]))
        dg_sum = None
        for rows, duh in zip(halves, du):
            dx, dg = _norm_bwd(x_ref[rows, :], g_ref[...], duh)
            dx = add_ref[rows, :] + dx
            dx_ref[rows, :] = dx
            dxb_ref[rows, :] = dx.astype(BF16)
            part = jnp.sum(dg, axis=0, keepdims=True)
            dg_sum = part if dg_sum is None else dg_sum + part
        _accumulate(dg_ref, dg_sum, pl.program_id(0) == 0)

    row = lambda w_: pl.BlockSpec((tm, w_), lambda i: (i, 0))
    vec = pl.BlockSpec((1, D), lambda i: (0, 0))
    extra, extra_specs, adapt = _behind(after, n + nw + 3)
    return pl.pallas_call(
        adapt(body), name=name, grid=(S // tm,),
        in_specs=[row(wd) for wd in widths] + [_resident(w.shape) for w in ws] + [row(D), vec, row(D)] + extra_specs,
        out_specs=[row(D), row(D), vec],
        out_shape=[jax.ShapeDtypeStruct((S, D), F32), jax.ShapeDtypeStruct((S, D), BF16),
                   jax.ShapeDtypeStruct((1, D), F32)],
        compiler_params=_cparams("arbitrary"),
    )(*pieces, *ws, x, g, add, *extra)


def _rows(a):
    return a.reshape(-1, a.shape[-1])


def _row_tile(rows, cols, itemsize=4, budget=1 << 20):
    t = rows
    while t % 32 == 0 and t * cols * itemsize > budget:
        t //= 2
    return t


def _sum_cast(arrs, out_dtype, *, name):
    shape = arrs[0].shape
    flat = [_rows(a) for a in arrs]
    R, C = flat[0].shape
    tr = _row_tile(R, C)

    def body(*refs):
        acc = refs[0][...].astype(F32)
        for r in refs[1:-1]:
            acc = acc + r[...].astype(F32)
        refs[-1][...] = acc.astype(out_dtype)

    spec = pl.BlockSpec((tr, C), lambda i: (i, 0))
    return pl.pallas_call(
        body, name=name, grid=(R // tr,), in_specs=[spec] * len(flat), out_specs=spec,
        out_shape=jax.ShapeDtypeStruct((R, C), out_dtype), compiler_params=_cparams("parallel"),
    )(*flat).reshape(shape)


def _adamw(parts, w, m, v, *, name):
    shape = w.shape
    w2, m2, v2 = _rows(w), _rows(m), _rows(v)
    R, C = w2.shape
    parts = [p.reshape(-1, R, C) for p in parts]
    tr = _row_tile(R, C)
    np_ = len(parts)
    c1, c2 = 1.0 - ADAM_B1 ** ADAM_STEP, 1.0 - ADAM_B2 ** ADAM_STEP

    def body(*refs):
        terms = [(r, k) for r in refs[:np_] for k in range(r.shape[0])]
        g = terms[0][0][terms[0][1]].astype(F32)
        for r, k in terms[1:]:
            g = g + r[k].astype(F32)
        w_ref, m_ref, v_ref, g_out, d_out, m_out, v_out = refs[np_:]
        mn = ADAM_B1 * m_ref[...] + (1.0 - ADAM_B1) * g
        vn = ADAM_B2 * v_ref[...] + (1.0 - ADAM_B2) * (g * g)
        g_out[...] = g
        d_out[...] = -ADAM_LR * ((mn / c1) / (jnp.sqrt(vn / c2) + ADAM_EPS) + ADAM_WD * w_ref[...])
        m_out[...] = mn
        v_out[...] = vn

    spec = pl.BlockSpec((tr, C), lambda i: (i, 0))
    out = jax.ShapeDtypeStruct((R, C), F32)
    stacks = [pl.BlockSpec((p.shape[0], tr, C), lambda i: (0, i, 0)) for p in parts]
    res = pl.pallas_call(
        body, name=name, grid=(R // tr,), in_specs=stacks + [spec] * 3, out_specs=[spec] * 4,
        out_shape=[out] * 4, compiler_params=_cparams("parallel"),
    )(*parts, w2, m2, v2)
    return [r.reshape(shape) for r in res]


def _adamw_packed(stack, widths, params, *, name):
    c1, c2 = 1.0 - ADAM_B1 ** ADAM_STEP, 1.0 - ADAM_B2 ** ADAM_STEP
    k = stack.shape[0]
    flat = [None if p is None else [_rows(a) for a in p] for p in params]
    n_in = sum(3 for p in flat if p is not None)

    def body(*refs):
        s_ref, ins, outs = refs[0], list(refs[1:1 + n_in]), list(refs[1 + n_in:])
        off = 0
        for width, p in zip(widths, flat):
            rows = 1 if p is None else p[0].shape[0]
            cols = width // rows
            w_ref, m_ref, v_ref = (None, None, None) if p is None else (ins.pop(0), ins.pop(0), ins.pop(0))
            o_refs = [outs.pop(0) for _ in range(1 if p is None else 4)]
            for r in range(rows):
                seg = slice(off + r * cols, off + (r + 1) * cols)
                g = s_ref[0, :, seg]
                for j in range(1, k):
                    g = g + s_ref[j, :, seg]
                o_refs[0][r:r + 1, :] = g
                if p is not None:
                    row = slice(r, r + 1)
                    mn = ADAM_B1 * m_ref[row, :] + (1.0 - ADAM_B1) * g
                    vn = ADAM_B2 * v_ref[row, :] + (1.0 - ADAM_B2) * (g * g)
                    o_refs[1][row, :] = -ADAM_LR * ((mn / c1) / (jnp.sqrt(vn / c2) + ADAM_EPS) + ADAM_WD * w_ref[row, :])
                    o_refs[2][row, :] = mn
                    o_refs[3][row, :] = vn
            off += width

    operands, out_shape = [stack], []
    for width, p in zip(widths, flat):
        if p is None:
            out_shape.append(jax.ShapeDtypeStruct((1, width), F32))
        else:
            operands += p
            out_shape += [jax.ShapeDtypeStruct(p[0].shape, F32)] * 4
    res = list(pl.pallas_call(body, name=name, out_shape=out_shape)(*operands))
    out = []
    for p, orig in zip(flat, params):
        n = 1 if p is None else 4
        out.append([r if orig is None else r.reshape(orig[0].shape) for r in res[:n]])
        res = res[n:]
    return out


def _coords():
    return lax.axis_index("x"), lax.axis_index("y"), lax.axis_index("c")


def _all_gather(shards, *, name):
    n = len(shards)

    def body(*refs):
        x_refs, out_refs = refs[:n], refs[n:2 * n]
        send_sems, recv_sems, local_sems = refs[2 * n:]
        x, y, c = _coords()
        me, sibling = (x, y, c), (x, y, 1 - c)
        chips = [(1 - x, y), (x, 1 - y), (1 - x, 1 - y)]

        def slot(a, dev):
            return out_refs[a].at[4 * dev[0] + 2 * dev[1] + dev[2]]

        def copy(a, k, block, to, src=None):
            return pltpu.make_async_remote_copy(
                src_ref=slot(a, block) if src is None else src, dst_ref=slot(a, block),
                send_sem=send_sems.at[7 * a + k], recv_sem=recv_sems.at[7 * a + k], device_id=to, device_id_type=MESH)

        mine = [pltpu.make_async_copy(x_refs[a], slot(a, me), local_sems.at[a]) for a in range(n)]
        for cp in mine:
            cp.start()
        first = []
        for a in range(n):
            first.append(copy(a, 0, me, sibling, src=x_refs[a]))
            first += [copy(a, 1 + j, me, (*chip, c), src=x_refs[a]) for j, chip in enumerate(chips)]
        for cp in first:
            cp.start()
        passed = []
        for j, chip in enumerate(chips):
            for a in range(n):
                copy(a, 1 + j, (*chip, c), me).wait_recv()
                fwd = copy(a, 4 + j, (*chip, c), sibling)
                fwd.start()
                passed.append(fwd)
        for a in range(n):
            copy(a, 0, sibling, me).wait_recv()
            for j, chip in enumerate(chips):
                copy(a, 4 + j, (*chip, 1 - c), me).wait_recv()
        for cp in first + passed:
            cp.wait_send()
        for cp in mine:
            cp.wait()

    return pl.pallas_call(
        body, name=name, in_specs=[HBM] * n, out_specs=[HBM] * n,
        out_shape=[jax.ShapeDtypeStruct((N_DEV, *s.shape), s.dtype) for s in shards],
        scratch_shapes=[pltpu.SemaphoreType.DMA((7 * n,)), pltpu.SemaphoreType.DMA((7 * n,)),
                        pltpu.SemaphoreType.DMA((n,))],
    )(*shards)


def _flip_y(x, y, c):
    return (x, 1 - y, c)


def _flip_x(x, y, c):
    return (1 - x, y, c)


def _flip_xy(x, y, c):
    return (1 - x, 1 - y, c)


SEM = pl.BlockSpec(memory_space=pltpu.SEMAPHORE)
SIDE_EFFECT = pltpu.SideEffectType.DATAFLOW_SIDE_EFFECTING


def _in_hbm(a):
    return pltpu.with_memory_space_constraint(a, pltpu.HBM)


def _copies_start(srcs, lands, plan, n_copies, *, name, after=None):
    ns, nl = len(srcs), len(lands)
    extra = [] if after is None else [after]

    def body(*refs):
        src_refs, land_refs = refs[:ns], refs[ns:ns + nl]
        send_sems, recv_sems = refs[ns + nl + len(extra):ns + nl + len(extra) + 2]
        token = refs[-1]
        for k, (src, dst, peer, _) in enumerate(plan(src_refs, land_refs, *_coords())):
            pltpu.make_async_remote_copy(src_ref=src, dst_ref=dst, send_sem=send_sems.at[k], recv_sem=recv_sems.at[k],
                                         device_id=peer, device_id_type=MESH).start()
        token[...] = jnp.zeros_like(token)

    bufs = [*srcs, *lands]
    res = pl.pallas_call(
        body, name=name, in_specs=[HBM] * (ns + nl) + [pl.BlockSpec(memory_space=pl.ANY)] * len(extra),
        out_specs=(SEM, SEM, *[HBM] * (ns + nl), pl.BlockSpec(memory_space=pltpu.VMEM)),
        out_shape=(pltpu.SemaphoreType.DMA((n_copies,)), pltpu.SemaphoreType.DMA((n_copies,)),
                   *[pltpu.HBM(b.shape, b.dtype) for b in bufs], jax.ShapeDtypeStruct((SUBLANES, LANES), F32)),
        input_output_aliases={i: 2 + i for i in range(ns + nl)},
        compiler_params=pltpu.CompilerParams(has_side_effects=SIDE_EFFECT),
    )(*[_in_hbm(b) for b in bufs], *extra)
    return res[0], res[1], list(res[2:2 + ns]), list(res[2 + ns:2 + ns + nl]), res[-1]


def _copies_wait(started, plan, after, *, name, with_srcs=False):
    send_sems, recv_sems, srcs, lands, _ = started
    ns, nl = len(srcs), len(lands)

    def body(*refs):
        src_refs, land_refs = refs[:ns], refs[ns:ns + nl]
        send_sems, recv_sems = refs[ns + nl:ns + nl + 2]
        for k, (src, dst, peer, here) in enumerate(plan(src_refs, land_refs, *_coords())):
            pltpu.make_async_remote_copy(src_ref=src, dst_ref=dst, send_sem=send_sems.at[k], recv_sem=recv_sems.at[k],
                                         device_id=peer, device_id_type=MESH).wait_send()
            pltpu.make_async_remote_copy(src_ref=src, dst_ref=here, send_sem=send_sems.at[k], recv_sem=recv_sems.at[k],
                                         device_id=peer, device_id_type=MESH).wait_recv()

    bufs = [*srcs, *lands]
    res = pl.pallas_call(
        body, name=name, in_specs=[HBM] * (ns + nl) + [SEM, SEM, pl.BlockSpec(memory_space=pl.ANY)],
        out_specs=[HBM] * (ns + nl), out_shape=[pltpu.HBM(b.shape, b.dtype) for b in bufs],
        input_output_aliases={i: i for i in range(ns + nl)},
        compiler_params=pltpu.CompilerParams(has_side_effects=SIDE_EFFECT),
    )(*bufs, send_sems, recv_sems, after)
    return (list(res[:ns]), list(res[ns:])) if with_srcs else list(res[ns:])


def _dev_index(dev):
    return 4 * dev[0] + 2 * dev[1] + dev[2]


def _ag_chips_plan(src_refs, land_refs, x, y, c):
    me = _dev_index((x, y, c))
    return [(src, land.at[me], peer, land.at[_dev_index(peer)])
            for src, land in zip(src_refs, land_refs) for peer in (_flip_y(x, y, c), _flip_x(x, y, c), _flip_xy(x, y, c))]


def _ag_sibling_plan(src_refs, land_refs, x, y, c):
    chips = [(x, y), (x, 1 - y), (1 - x, y), (1 - x, 1 - y)]
    return [(land.at[_dev_index((*chip, c))], land.at[_dev_index((*chip, c))], (x, y, 1 - c),
             land.at[_dev_index((*chip, 1 - c))]) for land in land_refs for chip in chips]


def _ag_direct_plan(src_refs, land_refs, x, y, c):
    me = _dev_index((x, y, c))
    plan = []
    for src, land in zip(src_refs, land_refs):
        for m in range(1, N_DEV):
            peer = (x + (m >> 2) * (1 - 2 * x), y + ((m >> 1) & 1) * (1 - 2 * y), c + (m & 1) * (1 - 2 * c))
            plan.append((src, land.at[me], peer, land.at[_dev_index(peer)]))
    return plan


def _rs_direct_plan(src_refs, land_refs, x, y, c):
    plan = []
    for src, land in zip(src_refs, land_refs):
        for m in range(1, N_DEV):
            peer = (x + (m >> 2) * (1 - 2 * x), y + ((m >> 1) & 1) * (1 - 2 * y), c + (m & 1) * (1 - 2 * c))
            plan.append((src.at[_dev_index(peer)], land.at[m - 1], peer, land.at[m - 1]))
    return plan


def _rs_start(grads, me, *, name, after=None):
    own = [lax.dynamic_index_in_dim(g, me, 0, keepdims=False) for g in grads]
    lands = [lax.empty((N_DEV - 1, *g.shape[1:]), g.dtype) for g in grads]
    return _copies_start(grads, lands, _rs_direct_plan, (N_DEV - 1) * len(grads), name=name, after=after), own


def _rs_finish(started, after, *, name):
    handle, own = started
    got = _copies_wait(handle, _rs_direct_plan, after, name=name)
    return [[o, land] for o, land in zip(own, got)]


def _gathered_cols(w8):
    return w8.transpose(1, 0, 2).reshape(w8.shape[1], -1)


def _pair_major(wt):
    return wt.reshape(3, ATTN_W // LANES, LANES, -1).transpose(1, 0, 2, 3).reshape(3 * ATTN_W, -1)


def kernel(x, norm1_g, w_in, attn_norm_g, hgrn_norm_g, hgrn_lb_logits, w_out, norm2_g, w_up, conv_w, conv_b, w_down, final_norm_g, loss_target, m_norm1_g, m_w_in, m_attn_norm_g, m_hgrn_norm_g, m_hgrn_lb_logits, m_w_out, m_norm2_g, m_w_up, m_conv_w, m_conv_b, m_w_down, m_final_norm_g, v_norm1_g, v_w_in, v_attn_norm_g, v_hgrn_norm_g, v_hgrn_lb_logits, v_w_out, v_norm2_g, v_w_up, v_conv_w, v_conv_b, v_w_down, v_final_norm_g):
    xs, target = x[0], loss_target[0]
    S, D = xs.shape
    NA = 3 * ATTN_W
    fng = final_norm_g.reshape(1, D)

    t = lambda a: a[0].T
    casts = [_sum_cast([w], BF16, name=f"cast_{nm}") for nm, w in
             (("w_in", t(w_in)), ("w_out", w_out[0]), ("w_up", t(w_up)), ("w_down", w_down[0]))]
    me = _dev_index(_coords())
    (g_in,) = _all_gather(casts[:1], name="ag_w_in")
    later = casts[1:] + [conv_w[0]]
    ag1 = _copies_start(later, [lax.empty((N_DEV, *s.shape), s.dtype) for s in later], _ag_chips_plan,
                        3 * len(later), name="ag_chips_start", after=g_in)
    wi = g_in.reshape(-1, D)
    wi_a = _pair_major(wi[:NA])

    u1, proj_a, proj_h, proj_f = _in_proj(xs, norm1_g, wi_a, wi, NA, name="in_proj", after=ag1[4])
    attn, lse = _attn_fwd(proj_a, name="attn_fwd")
    later, lands = _copies_wait(ag1, _ag_chips_plan, attn, name="ag_chips_wait", with_srcs=True)
    lands = [lax.dynamic_update_index_in_dim(l, s, me, 0) for l, s in zip(lands, later)]
    ag2 = _copies_start([], lands, _ag_sibling_plan, 4 * len(later), name="ag_sibling_start")
    rec, states = _hgrn_fwd(proj_h, proj_f, hgrn_lb_logits, name="hgrn_fwd", after=ag2[4])
    g_out, g_up, g_down, g_cw = _copies_wait(ag2, _ag_sibling_plan, rec, name="ag_sibling_wait")
    wo = g_out.reshape(-1, D)
    wu = g_up.reshape(-1, D)
    wd = g_down.reshape(-1, D)
    cw = _gathered_cols(g_cw)
    h1, u2, mixed = _out_proj(attn, rec, proj_h, xs, attn_norm_g, hgrn_norm_g, norm2_g, wo, name="out_proj")
    gate, gelu, vslope, act = _up_glu(u2, wu, cw, conv_b, name="up_glu")
    dh2, dh2b, d_fng, loss_part = _down_loss(act, wd, h1, fng, target, name="down_loss")

    dgate, dval, d_cw, d_cb = _dact_glu_bwd(dh2b, wd, gate, gelu, vslope, cw, name="dact_glu_bwd")
    dw_down = _mm_tn(act, dh2b, tm=256, name="dw_down")
    dh1, dh1b, d_n2g = _grad_norm_input([dgate, dval], [(wu, 0)], h1, norm2_g, dh2, name="du2_norm2_bwd")
    F = dgate.shape[1]
    dw_up = _mm_tn(dgate, u2, tm=256, rows=2 * F, name="dw_up_gate")
    dw_up = _mm_tn(dval, u2, tm=256, rows=2 * F, row_block=lambda i: i + F // 256, into=dw_up, name="dw_up_val")
    rs_ffn = _rs_start([dw_down.reshape(N_DEV, -1, D), dw_up.reshape(N_DEV, -1, D)], me, name="rs_ffn_start")
    dattn, delta, drec, dhg, d_ang, d_hng = _dmix_post_bwd(dh1b, wo, attn, rec, proj_h, attn_norm_g, hgrn_norm_g,
                                                          name="dmix_post_bwd", after=rs_ffn[0][4])
    dw_out = _mm_tn(mixed, dh1b, name="dw_out")
    rs_out = _rs_start([dw_out.reshape(N_DEV, -1, D)], me, name="rs_out_start")
    dproj_h, d_lbl = _hgrn_bwd(proj_h, proj_f, hgrn_lb_logits, states, drec, name="hgrn_bwd", after=rs_out[0][4])
    small = [("loss", loss_part, None, None, None),
             ("attn_norm_g", d_ang, attn_norm_g, m_attn_norm_g, v_attn_norm_g),
             ("hgrn_norm_g", d_hng, hgrn_norm_g, m_hgrn_norm_g, v_hgrn_norm_g),
             ("hgrn_lb_logits", d_lbl, hgrn_lb_logits, m_hgrn_lb_logits, v_hgrn_lb_logits),
             ("norm2_g", d_n2g, norm2_g, m_norm2_g, v_norm2_g),
             ("conv_b", d_cb, conv_b, m_conv_b, v_conv_b),
             ("final_norm_g", d_fng, final_norm_g, m_final_norm_g, v_final_norm_g)]
    pack = lambda arrs: jnp.concatenate([a.reshape(1, -1) for a in arrs], axis=1)
    small_own = [pack([s[1] for s in small]), d_cw]
    ag_small = _copies_start(small_own, [lax.empty((N_DEV, *s.shape), s.dtype) for s in small_own], _ag_direct_plan,
                             (N_DEV - 1) * len(small_own), name="ag_small_start")
    dproj_a = _attn_bwd(proj_a, dattn, lse, delta, name="attn_bwd")
    pairs = ATTN_W // LANES
    dw_in = _mm_tn(dproj_a, u1, tm=LANES, rows=wi.shape[0], row_block=lambda i: pairs * (i % 3) + i // 3,
                   name="dw_in_attn")
    dw_in = _mm_tn(dproj_h, u1, tm=256, rows=wi.shape[0], row_block=lambda i: i + NA // 256, into=dw_in,
                   name="dw_in_hgrn")
    dw_in = _mm_tn(dhg, u1, tm=256, rows=wi.shape[0], row_block=lambda i: i + (NA + 3 * HGRN_W) // 256, into=dw_in,
                   name="dw_in_gate")
    rs_in = _rs_start([dw_in.reshape(N_DEV, -1, D)], me, name="rs_in_start", after=ag_small[4])
    grad_x, _, d_n1g = _grad_norm_input([dproj_a, dproj_h, dhg], [(wi_a, 0), (wi, NA)], xs,
                                        norm1_g, dh1, name="du1_norm1_bwd", after=rs_in[0][4])

    res = {}

    def update(nm, parts, w, m, v, transposed=False):
        if transposed:
            raw = _adamw(parts, t(w), t(m), t(v), name=f"adamw_{nm}")
            res[nm] = [r.T[None] for r in raw]
        else:
            raw = res[nm] = _adamw(parts, w, m, v, name=f"adamw_{nm}")
        return raw[1]

    g_down, g_up = _rs_finish(rs_ffn, grad_x, name="rs_ffn_wait")
    update("w_down", g_down, w_down, m_w_down, v_w_down)
    done_up = update("w_up", g_up, w_up, m_w_up, v_w_up, transposed=True)
    (g_out,) = _rs_finish(rs_out, grad_x, name="rs_out_wait")
    update("w_out", g_out, w_out, m_w_out, v_w_out)
    (g_in,) = _rs_finish(rs_in, done_up, name="rs_in_wait")
    done_in = update("w_in", g_in, w_in, m_w_in, v_w_in, transposed=True)

    small_own, small_all = _copies_wait(ag_small, _ag_direct_plan, grad_x, name="ag_small_wait", with_srcs=True)
    g_small, g_dcw = [lax.dynamic_update_index_in_dim(l, s, me, 0) for l, s in zip(small_all, small_own)]
    sm = _adamw_packed(g_small, [s[1].size for s in small], [None if s[2] is None else s[2:] for s in small],
                       name="adamw_small")
    for (nm, *_), r in zip(small, sm):
        res[nm] = r
    ncw = conv_w.shape[-1]
    mine_cw = lax.dynamic_slice_in_dim(g_dcw, me * ncw, ncw, axis=2)
    res["conv_w"] = _adamw([mine_cw], conv_w, m_conv_w, v_conv_w, name="adamw_conv_w")
    late, _ = lax.optimization_barrier((d_n1g, done_in))
    update("norm1_g", _all_gather([late], name="ag_norm1_grad"), norm1_g, m_norm1_g, v_norm1_g)

    loss = res["loss"][0][0, 0]
    order = ["norm1_g", "w_in", "attn_norm_g", "hgrn_norm_g", "hgrn_lb_logits", "w_out", "norm2_g", "w_up",
             "conv_w", "conv_b", "w_down", "final_norm_g"]
    return (loss, grad_x[None], *[res[nm][0] for nm in order], *[res[nm][1] for nm in order],
            *[res[nm][2] for nm in order], *[res[nm][3] for nm in order])
```

```python
import math

import jax
import jax.numpy as jnp
from jax import lax
from jax.experimental import pallas as pl
from jax.experimental.pallas import tpu as pltpu

F32, BF16 = jnp.float32, jnp.bfloat16
NORM_EPS = 1e-6
ATTN_HEADS, HEAD_DIM, ATTN_BLOCK = 8, 64, 128
DILATIONS = (1, 4, 16)
ATTN_SCALE = HEAD_DIM ** -0.5
ATTN_W = ATTN_HEADS * HEAD_DIM
HGRN_HEADS, HGRN_DIM, HGRN_CHUNK = 4, 128, 64
HGRN_W = HGRN_HEADS * HGRN_DIM
ADAM_LR, ADAM_B1, ADAM_B2, ADAM_EPS, ADAM_WD, ADAM_STEP = 0.001, 0.9, 0.999, 1e-08, 0.01, 10
LANES, SUBLANES = 128, 8
VMEM_LIMIT_BYTES = 56 * 1024 * 1024
N_DEV = 8
MESH = pl.DeviceIdType.MESH
HBM = pl.BlockSpec(memory_space=pltpu.HBM)
HIGHEST = lax.Precision.HIGHEST


def _cparams(*sem):
    return pltpu.CompilerParams(dimension_semantics=sem, vmem_limit_bytes=VMEM_LIMIT_BYTES)


def _tile(n, pref):
    if n <= pref:
        return n
    t = (pref // LANES) * LANES
    while n % t:
        t -= LANES
    return t


def _resident(shape):
    return pl.BlockSpec(shape, lambda *_: (0,) * len(shape), pipeline_mode=pl.Buffered(1))


def _dot(a, b, dims, precision=None):
    return lax.dot_general(a, b, (dims, ((), ())), precision=precision, preferred_element_type=F32)


def _nn(a, b, precision=None):
    return _dot(a, b, ((1,), (0,)), precision)


def _nt(a, b):
    return _dot(a, b, ((1,), (1,)))


def _tn(a, b):
    return _dot(a, b, ((0,), (0,)))


def _sigmoid(x):
    return 1.0 / (1.0 + jnp.exp(-x))


def _rstd(x):
    return lax.rsqrt(jnp.mean(x * x, axis=-1, keepdims=True) + NORM_EPS)


def _norm_bwd(x, g, du):
    r = _rstd(x)
    xh = x * r
    dxh = du * g
    return r * (dxh - xh * jnp.mean(dxh * xh, axis=-1, keepdims=True)), du * xh


def _row_halves(tm):
    return [pl.ds(0, tm // 2), pl.ds(tm // 2, tm // 2)]


def _behind(after, n_in):
    if after is None:
        return [], [], (lambda body: body)
    return ([after], [pl.BlockSpec(memory_space=pl.ANY)],
            lambda body: (lambda *refs: body(*refs[:n_in], *refs[n_in + 1:])))


def _accumulate(ref, part, first):
    @pl.when(first)
    def _():
        ref[...] = part

    @pl.when(jnp.logical_not(first))
    def _():
        ref[...] += part


def _mm_tn(xs, dy, *, name, tm=512, tn=1024, rows=None, row_block=None, into=None):
    S, N = dy.shape
    tm = _tile(math.gcd(*[x.shape[1] for x in xs]), tm)
    tn = _tile(N, tn)
    blocks = [x.shape[1] // tm for x in xs]
    first = [sum(blocks[:k]) for k in range(len(xs))]
    row_block = row_block or (lambda i: i)
    n = len(xs)

    def body(*refs):
        x_refs, dy_ref = refs[:n], refs[n]
        o_ref, xt_ref = refs[-2:]
        i = pl.program_id(0)
        for x_ref, b0, nb in zip(x_refs, first, blocks):
            @pl.when((pl.program_id(1) == 0) & (i >= b0) & (i < b0 + nb))
            def _(x_ref=x_ref):
                xt_ref[...] = x_ref[...].T

        o_ref[...] = _nn(xt_ref[...], dy_ref[...]).astype(BF16)

    def x_spec(b0, nb):
        return pl.BlockSpec((S, tm), lambda i, j: (0, jnp.clip(i - b0, 0, nb - 1)))

    operands = [*xs, dy] + ([] if into is None else [into])
    return pl.pallas_call(
        body, name=name, grid=(sum(blocks), N // tn),
        in_specs=[x_spec(b0, nb) for b0, nb in zip(first, blocks)] + [pl.BlockSpec((S, tn), lambda i, j: (0, j))]
        + ([] if into is None else [pl.BlockSpec(memory_space=pl.ANY)]),
        out_specs=pl.BlockSpec((tm, tn), lambda i, j: (row_block(i), j)),
        out_shape=jax.ShapeDtypeStruct((rows or sum(blocks) * tm, N), BF16),
        input_output_aliases={} if into is None else {n + 1: 0},
        scratch_shapes=[pltpu.VMEM((tm, S), BF16)], compiler_params=_cparams("parallel", "arbitrary"),
    )(*operands)


def _in_proj(x, g, wt_attn, wt, row0, *, name, tm=512, after=None):
    S, D = x.shape
    NA, NH, W = wt_attn.shape[0], wt.shape[0] - row0, HGRN_W

    def body(x_ref, g_ref, wa_ref, w_ref, u_ref, a_ref, h_ref, f_ref):
        xv = x_ref[...]
        u = (xv * _rstd(xv) * g_ref[...]).astype(BF16)
        u_ref[...] = u
        a_ref[...] = _nt(u, wa_ref[...]).astype(BF16)
        ph = _nt(u, w_ref[row0:row0 + NH, :])
        h_ref[...] = ph.astype(BF16)
        f_ref[...] = ph[:, W:2 * W]

    row = lambda w: pl.BlockSpec((tm, w), lambda i: (i, 0))
    extra, extra_specs, adapt = _behind(after, 4)
    return pl.pallas_call(
        adapt(body), name=name, grid=(S // tm,),
        in_specs=[row(D), pl.BlockSpec((1, D), lambda i: (0, 0)), _resident(wt_attn.shape), _resident(wt.shape)]
        + extra_specs,
        out_specs=[row(D), row(NA), row(NH), row(W)],
        out_shape=[jax.ShapeDtypeStruct((S, D), BF16), jax.ShapeDtypeStruct((S, NA), BF16),
                   jax.ShapeDtypeStruct((S, NH), BF16), jax.ShapeDtypeStruct((S, W), F32)],
        compiler_params=_cparams("parallel"),
    )(x, g, wt_attn, wt, *extra)


PAIR_W = 3 * LANES
ATTN_UNROLL_FWD, ATTN_UNROLL_BWD = 8, 4


def _attn_masks(first):
    qi = lax.broadcasted_iota(jnp.int32, (ATTN_BLOCK, 2 * ATTN_BLOCK), 0)
    kj = lax.broadcasted_iota(jnp.int32, (ATTN_BLOCK, 2 * ATTN_BLOCK), 1)
    dist = qi + ATTN_BLOCK - kj
    valid = (dist >= 0) & (dist <= ATTN_BLOCK) & jnp.logical_or(kj >= ATTN_BLOCK, jnp.logical_not(first))
    lane = lax.broadcasted_iota(jnp.int32, (1, LANES), 1)
    return valid, lane


def _for_residue_blocks(S, d, fn):
    span = ATTN_BLOCK * d
    nb = S // span

    def step(n, carry):
        base = pl.multiple_of(n * span, span)
        for r in range(d):
            off = pl.multiple_of((r * nb + n) * ATTN_BLOCK, ATTN_BLOCK)
            fn(lambda ref, r=r: _block_rows(ref, base, r, d),
               lambda ref, val, r=r: _set_block_rows(ref, base, r, d, val), off)
        return carry

    lax.fori_loop(0, nb, step, 0)


def _for_blocks(S, unroll, fn):
    def step(i, carry):
        fn([(pl.multiple_of((i * unroll + u) * ATTN_BLOCK, ATTN_BLOCK), i * unroll + u) for u in range(unroll)])
        return carry

    lax.fori_loop(0, S // ATTN_BLOCK // unroll, step, 0)


def _head_value(x2, lane, e):
    return jnp.sum(jnp.where(lane == HEAD_DIM * e, x2, 0.0), axis=-1, keepdims=True)


def _block_rows(ref, base, r, d):
    if d == 1:
        return ref[pl.ds(base, ATTN_BLOCK), :]
    return ref.at[pl.ds(base, ATTN_BLOCK * d)][pl.ds(r, ATTN_BLOCK, stride=d), :]


def _set_block_rows(ref, base, r, d, val):
    if d == 1:
        ref[pl.ds(base, ATTN_BLOCK), :] = val
    else:
        ref.at[pl.ds(base, ATTN_BLOCK * d)][pl.ds(r, ATTN_BLOCK, stride=d), :] = val


def _order4_to_16(src, dst, pad):
    S = src.shape[0]
    q4, q16 = S // 4, S // 16
    for r in range(4):
        for a in range(4):
            for n in range(q16 // ATTN_BLOCK):
                rows = src.at[pl.ds(r * q4 + 4 * ATTN_BLOCK * n, 4 * ATTN_BLOCK)][pl.ds(a, ATTN_BLOCK, stride=4), :]
                dst[pl.ds(pad + (4 * a + r) * q16 + ATTN_BLOCK * n, ATTN_BLOCK), :] = rows.astype(dst.dtype)


def _order16_to_4(src, pad, dst):
    S = dst.shape[0]
    q4, q16 = S // 4, S // 16
    for r in range(4):
        for a in range(4):
            for n in range(q16 // ATTN_BLOCK):
                rows = src[pl.ds(pad + (4 * a + r) * q16 + ATTN_BLOCK * n, ATTN_BLOCK), :]
                dst.at[pl.ds(r * q4 + 4 * ATTN_BLOCK * n, 4 * ATTN_BLOCK)][pl.ds(a, ATTN_BLOCK, stride=4), :] = rows


def _regroup(S, d, pairs, tmp):
    for src, dst, pad in pairs:
        if d == 16:
            def to_tmp(rows, _, off, src=src):
                tmp[pl.ds(off, ATTN_BLOCK), :] = rows(src)

            _for_residue_blocks(S, 4, to_tmp)
            _order4_to_16(tmp, dst, pad)
    if d != 16:
        def to_dst(rows, _, off):
            for src, dst, pad in pairs:
                dst[pl.ds(pad + off, ATTN_BLOCK), :] = rows(src).astype(dst.dtype)

        _for_residue_blocks(S, d, to_dst)


def _split_pair(p_ref, qs, ks, vs, bk, bv):
    qs[...] = p_ref[:, 0:LANES].astype(F32) * ATTN_SCALE
    ks[...] = p_ref[:, LANES:2 * LANES].astype(F32)
    vs[...] = p_ref[:, 2 * LANES:3 * LANES].astype(F32)
    bk[0:ATTN_BLOCK, :] = jnp.zeros((ATTN_BLOCK, LANES), bk.dtype)
    bv[0:ATTN_BLOCK, :] = jnp.zeros((ATTN_BLOCK, LANES), bv.dtype)


def _attn_fwd(proj_a, *, name):
    S = proj_a.shape[0]

    def body(p_ref, o_ref, l_ref, qs, ks, vs, bq, bk, bv, bo, bl, to, tl):
        _split_pair(p_ref, qs, ks, vs, bk, bv)
        for d in DILATIONS:
            nb = S // (ATTN_BLOCK * d)
            _regroup(S, d, ((qs, bq, 0), (ks, bk, ATTN_BLOCK), (vs, bv, ATTN_BLOCK)), to)

            def blocks(group, nb=nb):
                lane = lax.broadcasted_iota(jnp.int32, (1, LANES), 1)
                heads = [(lane >= HEAD_DIM * e) & (lane < HEAD_DIM * (e + 1)) for e in range(LANES // HEAD_DIM)]
                wins = [pl.ds(off, 2 * ATTN_BLOCK) for off, _ in group]
                s = [[_nt(jnp.where(mh, bq[pl.ds(off, ATTN_BLOCK), :], jnp.zeros((ATTN_BLOCK, LANES), BF16)), bk[win, :])
                      for mh in heads] for (off, _), win in zip(group, wins)]
                p, m, l = [], [], []
                for (off, b), su in zip(group, s):
                    valid, _ = _attn_masks(jnp.bitwise_and(b, nb - 1) == 0)
                    sm = [jnp.where(valid, x, -jnp.inf) for x in su]
                    m.append([jnp.max(x, axis=-1, keepdims=True) for x in sm])
                    p.append([jnp.exp(x - mx) for x, mx in zip(sm, m[-1])])
                    l.append([jnp.sum(x, axis=-1, keepdims=True) for x in p[-1]])
                o = [[_nn(x.astype(BF16), bv[win, :]) for x in pu] for pu, win in zip(p, wins)]
                for (off, _), ou, mu, lu in zip(group, o, m, l):
                    o2 = jnp.zeros((ATTN_BLOCK, LANES), F32)
                    l2 = jnp.zeros((ATTN_BLOCK, LANES), F32)
                    for mh, oe, me_, le in zip(heads, ou, mu, lu):
                        o2 = jnp.where(mh, oe / le, o2)
                        l2 = jnp.where(mh, me_ + jnp.log(le), l2)
                    bo[pl.ds(off, ATTN_BLOCK), :] = o2
                    bl[pl.ds(off, ATTN_BLOCK), :] = l2

            _for_blocks(S, ATTN_UNROLL_FWD, blocks)

            if d == 16:
                _order16_to_4(bo, 0, to)
                _order16_to_4(bl, 0, tl)
            src_o, src_l = (to, tl) if d == 16 else (bo, bl)

            def merge(rows, set_rows, off, d=d, src_o=src_o, src_l=src_l):
                blk = pl.ds(off, ATTN_BLOCK)
                o2, l2 = src_o[blk, :], src_l[blk, :]
                if d != DILATIONS[0]:
                    lo, oo = rows(l_ref), rows(o_ref)
                    ln = jnp.maximum(lo, l2)
                    wa, wb = jnp.exp(lo - ln), jnp.exp(l2 - ln)
                    o2 = (wa * oo + wb * o2) / (wa + wb)
                    l2 = ln + jnp.log(wa + wb)
                set_rows(o_ref, o2)
                set_rows(l_ref, l2)

            _for_residue_blocks(S, min(d, 4), merge)

    slab = pl.BlockSpec((S, LANES), lambda p: (0, p))
    f32_slab, bf16_slab = pltpu.VMEM((S, LANES), F32), pltpu.VMEM((S, LANES), BF16)
    bf16_window = pltpu.VMEM((S + ATTN_BLOCK, LANES), BF16)
    return pl.pallas_call(
        body, name=name, grid=(ATTN_W // LANES,), in_specs=[pl.BlockSpec((S, PAIR_W), lambda p: (0, p))],
        out_specs=[slab, slab],
        out_shape=[jax.ShapeDtypeStruct((S, ATTN_W), F32), jax.ShapeDtypeStruct((S, ATTN_W), F32)],
        scratch_shapes=[f32_slab] * 3 + [bf16_slab, bf16_window, bf16_window] + [f32_slab] * 4,
        compiler_params=_cparams("parallel"),
    )(proj_a)


def _attn_bwd(proj_a, do, lse, delta, *, name):
    S = proj_a.shape[0]

    def body(p_ref, do_ref, lse_ref, dl_ref, o_ref, qs, ks, vs, dqs, dks, dvs, bq, bk, bv, bdo, blse, bdl, bdq, bdk, bdv,
             tmp):
        _split_pair(p_ref, qs, ks, vs, bk, bv)
        bdk[0:ATTN_BLOCK, :] = jnp.zeros((ATTN_BLOCK, LANES), F32)
        bdv[0:ATTN_BLOCK, :] = jnp.zeros((ATTN_BLOCK, LANES), F32)
        for d in DILATIONS:
            nb = S // (ATTN_BLOCK * d)
            _regroup(S, d, ((qs, bq, 0), (ks, bk, ATTN_BLOCK), (vs, bv, ATTN_BLOCK), (do_ref, bdo, 0),
                            (lse_ref, blse, 0), (dl_ref, bdl, 0)), tmp)

            def blocks(group, nb=nb):
                lane = lax.broadcasted_iota(jnp.int32, (1, LANES), 1)
                heads = [(lane >= HEAD_DIM * e) & (lane < HEAD_DIM * (e + 1)) for e in range(LANES // HEAD_DIM)]
                zero = jnp.zeros((ATTN_BLOCK, LANES), BF16)
                chains = [(off, b, e, mh) for off, b in group for e, mh in enumerate(heads)]
                qm = [jnp.where(mh, bq[pl.ds(off, ATTN_BLOCK), :], zero) for off, _, _, mh in chains]
                dom = [jnp.where(mh, bdo[pl.ds(off, ATTN_BLOCK), :], zero) for off, _, _, mh in chains]
                s = [_nt(x, bk[pl.ds(off, 2 * ATTN_BLOCK), :]) for x, (off, _, _, _) in zip(qm, chains)]
                dp = [_nt(x, bv[pl.ds(off, 2 * ATTN_BLOCK), :]) for x, (off, _, _, _) in zip(dom, chains)]
                p, ds = [], []
                for (off, b, e, _), sc, dpc in zip(chains, s, dp):
                    valid, _ = _attn_masks(jnp.bitwise_and(b, nb - 1) == 0)
                    blk = pl.ds(off, ATTN_BLOCK)
                    pc = jnp.where(valid, jnp.exp(sc - _head_value(blse[blk, :], lane, e)), 0.0)
                    ds.append((pc * (dpc - _head_value(bdl[blk, :], lane, e))).astype(BF16))
                    p.append(pc.astype(BF16))
                dq = [_nn(x, bk[pl.ds(off, 2 * ATTN_BLOCK), :]) for x, (off, _, _, _) in zip(ds, chains)]
                dk = [_tn(x, y) for x, y in zip(ds, qm)]
                dv = [_tn(x, y) for x, y in zip(p, dom)]
                nh = len(heads)
                for u, (off, _) in enumerate(group):
                    dq2 = jnp.zeros((ATTN_BLOCK, LANES), F32)
                    for mh, x in zip(heads, dq[nh * u:nh * (u + 1)]):
                        dq2 = jnp.where(mh, x, dq2)
                    bdq[pl.ds(off, ATTN_BLOCK), :] = dq2 * ATTN_SCALE
                    for acc, grads in ((bdk, dk), (bdv, dv)):
                        win_grad = sum(grads[nh * u + 1:nh * (u + 1)], grads[nh * u])
                        acc[pl.ds(off, ATTN_BLOCK), :] += win_grad[:ATTN_BLOCK]
                        acc[pl.ds(off + ATTN_BLOCK, ATTN_BLOCK), :] = win_grad[ATTN_BLOCK:]

            _for_blocks(S, ATTN_UNROLL_BWD, blocks)

            outs = ((dqs, bdq, 0), (dks, bdk, ATTN_BLOCK), (dvs, bdv, ATTN_BLOCK))
            if d == 16:
                for acc, grad, pad in outs:
                    _order16_to_4(grad, pad, tmp)

                    def add(rows, set_rows, off, acc=acc):
                        set_rows(acc, rows(acc) + tmp[pl.ds(off, ATTN_BLOCK), :])

                    _for_residue_blocks(S, 4, add)
            else:
                def scatter(rows, set_rows, off, d=d):
                    for acc, grad, pad in outs:
                        part = grad[pl.ds(pad + off, ATTN_BLOCK), :]
                        set_rows(acc, part if d == DILATIONS[0] else rows(acc) + part)

                _for_residue_blocks(S, d, scatter)
        o_ref[:, 0:LANES] = dqs[...].astype(BF16)
        o_ref[:, LANES:2 * LANES] = dks[...].astype(BF16)
        o_ref[:, 2 * LANES:3 * LANES] = dvs[...].astype(BF16)

    slab = pl.BlockSpec((S, LANES), lambda p: (0, p))
    pair = pl.BlockSpec((S, PAIR_W), lambda p: (0, p))
    f32_slab, bf16_slab = pltpu.VMEM((S, LANES), F32), pltpu.VMEM((S, LANES), BF16)
    f32_window, bf16_window = pltpu.VMEM((S + ATTN_BLOCK, LANES), F32), pltpu.VMEM((S + ATTN_BLOCK, LANES), BF16)
    return pl.pallas_call(
        body, name=name, grid=(ATTN_W // LANES,), in_specs=[pair, slab, slab, slab], out_specs=pair,
        out_shape=jax.ShapeDtypeStruct(proj_a.shape, BF16),
        scratch_shapes=[f32_slab] * 6 + [bf16_slab, bf16_window, bf16_window, bf16_slab, f32_slab, f32_slab,
                                         f32_slab, f32_window, f32_window, f32_slab],
        compiler_params=_cparams("parallel"),
    )(proj_a, do, lse, delta)


HG_T = 2 * HGRN_CHUNK
HG_GROUPS = 2
HG_STEP = HG_GROUPS * HG_T


def _hgrn_consts():
    row = lax.broadcasted_iota(jnp.int32, (HG_T, HG_T), 0)
    col = lax.broadcasted_iota(jnp.int32, (HG_T, HG_T), 1)
    same = (row >= HGRN_CHUNK) == (col >= HGRN_CHUNK)
    return row, same & (col <= row), same & (col >= row)


def _lower_bound(logits_ref):
    l0, l1 = logits_ref[0:1, :], logits_ref[1:2, :]
    mx = jnp.maximum(l0, l1)
    e0, e1 = jnp.exp(l0 - mx), jnp.exp(l1 - mx)
    return e0 / (e0 + e1)


def _hgrn_chains():
    chains = [(g, h) for g in range(HG_GROUPS) for h in range(HGRN_HEADS)]
    rows = [pl.ds(HG_T * g, HG_T) for g, _ in chains]
    lanes = [slice(HGRN_DIM * h, HGRN_DIM * (h + 1)) for _, h in chains]
    return chains, rows, lanes


def _hgrn_gates(qs, fs, lbs, row, causal):
    C = HGRN_CHUNK
    tri = jnp.where(causal, 1.0, 0.0).astype(F32)
    sgs = [_sigmoid(f) for f in fs]
    forgets = [lb + (1.0 - lb) * sg for lb, sg in zip(lbs, sgs)]
    logfs = [jnp.log(forget) for forget in forgets]
    bs = [_nn(tri, logf, HIGHEST) for logf in logfs]
    out = []
    for q, sg, forget, logf, b in zip(qs, sgs, forgets, logfs, bs):
        key = 1.0 - forget
        bend0 = jnp.sum(logf[:C], axis=0, keepdims=True)
        bend1 = jnp.sum(logf[C:], axis=0, keepdims=True)
        bend = jnp.where(row < C, bend0, bend1)
        eb, emb, eend = jnp.exp(b), jnp.exp(-b), jnp.exp(bend - b)
        sq = _sigmoid(q)
        out.append(dict(sg=sg, forget=forget, key=key, bend0=bend0, bend1=bend1, eb=eb, emb=emb, eend=eend, sq=sq,
                        qd=q * sq * eb, ki=key * emb, ke=key * eend))
    return out


def _hgrn_fwd(proj, proj_f, logits, *, name, after=None):
    S = proj.shape[0]
    W, C = HGRN_W, HGRN_CHUNK

    def body(q_ref, f_ref, i_ref, lg_ref, rec_ref, st_ref, s_ref):
        @pl.when(pl.program_id(0) == 0)
        def _():
            s_ref[...] = jnp.zeros_like(s_ref)

        row, causal, _ = _hgrn_consts()
        lb_all = _lower_bound(lg_ref)
        chains, rows, lanes = _hgrn_chains()
        n = range(len(chains))
        gts = _hgrn_gates([q_ref[rows[c], lanes[c]].astype(F32) for c in n], [f_ref[rows[c], lanes[c]] for c in n],
                          [lb_all[:, lanes[c]] for c in n], row, causal)
        qd, ki, ke = ([gt[k].astype(BF16) for gt in gts] for k in ("qd", "ki", "ke"))
        iv = [i_ref[rows[c], lanes[c]].astype(BF16) for c in n]
        a = [_nt(qd[c], ki[c]) for c in n]
        u0 = [_tn(iv[c][:C], ke[c][:C]) for c in n]
        u1 = [_tn(iv[c][C:], ke[c][C:]) for c in n]
        state = [s_ref[h] for h in range(HGRN_HEADS)]
        s0, s1 = [], []
        for c, (g, h) in enumerate(chains):
            s0.append(state[h])
            s1.append(jnp.exp(gts[c]["bend0"]) * s0[c] + u0[c])
            state[h] = jnp.exp(gts[c]["bend1"]) * s1[c] + u1[c]
        o0 = [_nt(qd[c][:C], s0[c].astype(BF16)) for c in n]
        o1 = [_nt(qd[c][C:], s1[c].astype(BF16)) for c in n]
        o = [_nn(jnp.where(causal, a[c], 0.0).astype(BF16), iv[c]) for c in n]
        for c, (g, h) in enumerate(chains):
            st_ref[2 * g, h] = s0[c]
            st_ref[2 * g + 1, h] = s1[c]
            rec_ref[rows[c], lanes[c]] = o[c] + jnp.concatenate([o0[c], o1[c]], axis=0)
        for h in range(HGRN_HEADS):
            s_ref[h] = state[h]

    blk = lambda j: pl.BlockSpec((HG_STEP, W), lambda t: (t, j))
    extra, extra_specs, adapt = _behind(after, 4)
    return pl.pallas_call(
        adapt(body), name=name, grid=(S // HG_STEP,),
        in_specs=[blk(0), blk(0), blk(2), pl.BlockSpec((2, W), lambda t: (0, 0))] + extra_specs,
        out_specs=[blk(0), pl.BlockSpec((2 * HG_GROUPS, HGRN_HEADS, HGRN_DIM, HGRN_DIM), lambda t: (t, 0, 0, 0))],
        out_shape=[jax.ShapeDtypeStruct((S, W), F32),
                   jax.ShapeDtypeStruct((S // C, HGRN_HEADS, HGRN_DIM, HGRN_DIM), F32)],
        scratch_shapes=[pltpu.VMEM((HGRN_HEADS, HGRN_DIM, HGRN_DIM), F32)],
        compiler_params=_cparams("arbitrary"),
    )(proj, proj_f, proj, logits, *extra)


def _hgrn_bwd(proj, proj_f, logits, states, drec, *, name, after=None):
    S = proj.shape[0]
    W, C = HGRN_W, HGRN_CHUNK
    nt = S // HG_STEP

    def body(q_ref, f_ref, i_ref, lg_ref, st_ref, do_ref, dp_ref, dlg_ref, ds_ref, dlb_ref):
        t = pl.program_id(0)

        @pl.when(t == 0)
        def _():
            ds_ref[...] = jnp.zeros_like(ds_ref)
            dlb_ref[...] = jnp.zeros_like(dlb_ref)

        row, causal, anti = _hgrn_consts()
        lb_all = _lower_bound(lg_ref)
        chains, rows, lanes = _hgrn_chains()
        n = range(len(chains))
        qs, lbs = [q_ref[rows[c], lanes[c]].astype(F32) for c in n], [lb_all[:, lanes[c]] for c in n]
        gts = _hgrn_gates(qs, [f_ref[rows[c], lanes[c]] for c in n], lbs, row, causal)
        qd, ki, ke = ([gt[k] for gt in gts] for k in ("qd", "ki", "ke"))
        qdb, kib, keb = ([x.astype(BF16) for x in xs] for xs in (qd, ki, ke))
        iv = [i_ref[rows[c], lanes[c]].astype(BF16) for c in n]
        dob = [do_ref[rows[c], lanes[c]].astype(BF16) for c in n]
        s0 = [st_ref[2 * g, h] for g, h in chains]
        s1 = [st_ref[2 * g + 1, h] for g, h in chains]
        dec0, dec1 = [jnp.exp(gt["bend0"]) for gt in gts], [jnp.exp(gt["bend1"]) for gt in gts]
        a = [_nt(qdb[c], kib[c]) for c in n]
        da = [_nt(dob[c], iv[c]) for c in n]
        dqd1 = [_nn(dob[c][C:], s1[c].astype(BF16)) for c in n]
        dqd0 = [_nn(dob[c][:C], s0[c].astype(BF16)) for c in n]
        t1 = [_tn(dob[c][C:], qdb[c][C:]) for c in n]
        t0 = [_tn(dob[c][:C], qdb[c][:C]) for c in n]
        carry = [ds_ref[h] for h in range(HGRN_HEADS)]
        ds1, ds0 = [None] * len(chains), [None] * len(chains)
        for c in reversed(n):
            h = chains[c][1]
            ds1[c] = carry[h]
            ds0[c] = dec1[c] * ds1[c] + t1[c]
            carry[h] = dec0[c] * ds0[c] + t0[c]
        for h in range(HGRN_HEADS):
            ds_ref[h] = carry[h]
        ds1b, ds0b = [x.astype(BF16) for x in ds1], [x.astype(BF16) for x in ds0]
        a = [jnp.where(causal, x, 0.0).astype(BF16) for x in a]
        da = [jnp.where(causal, x, 0.0).astype(BF16) for x in da]
        di1 = [_nt(keb[c][C:], ds1b[c]) for c in n]
        dke1 = [_nn(iv[c][C:], ds1b[c]) for c in n]
        di0 = [_nt(keb[c][:C], ds0b[c]) for c in n]
        dke0 = [_nn(iv[c][:C], ds0b[c]) for c in n]
        dqd_a = [_nn(da[c], kib[c]) for c in n]
        dki = [_tn(da[c], qdb[c]) for c in n]
        di_a = [_tn(a[c], dob[c]) for c in n]
        dqd, dke, db = [], [], []
        for c in n:
            ddec1 = jnp.sum(ds1[c] * s1[c], axis=0, keepdims=True)
            ddec0 = jnp.sum(ds0[c] * s0[c], axis=0, keepdims=True)
            dqd.append(dqd_a[c] + jnp.concatenate([dqd0[c], dqd1[c]], axis=0))
            h = chains[c][1]
            dp_ref[rows[c], 2 * W + HGRN_DIM * h:2 * W + HGRN_DIM * (h + 1)] = (
                di_a[c] + jnp.concatenate([di0[c], di1[c]], axis=0)).astype(BF16)
            dke.append(jnp.concatenate([dke0[c], dke1[c]], axis=0))
            gke = dke[c] * ke[c]
            dbend0 = jnp.sum(gke[:C], axis=0, keepdims=True) + ddec0 * dec0[c]
            dbend1 = jnp.sum(gke[C:], axis=0, keepdims=True) + ddec1 * dec1[c]
            dbc = dqd[c] * qd[c] - dki[c] * ki[c] - gke
            db.append(dbc + jnp.where(row == C - 1, dbend0, 0.0) + jnp.where(row == HG_T - 1, dbend1, 0.0))
        tri = jnp.where(anti, 1.0, 0.0).astype(F32)
        dlogf = [_nn(tri, db[c], HIGHEST) for c in n]
        for c in n:
            gt, lb, q, h = gts[c], lbs[c], qs[c], chains[c][1]
            dforget = dlogf[c] / gt["forget"] - (dki[c] * gt["emb"] + dke[c] * gt["eend"])
            sg, sq = gt["sg"], gt["sq"]
            dp_ref[rows[c], W + HGRN_DIM * h:W + HGRN_DIM * (h + 1)] = (
                dforget * (1.0 - lb) * sg * (1.0 - sg)).astype(BF16)
            dlb_ref[:, lanes[c]] += jnp.sum(dforget * (1.0 - sg), axis=0, keepdims=True)
            dp_ref[rows[c], lanes[c]] = (dqd[c] * gt["eb"] * sq * (1.0 + q * (1.0 - sq))).astype(BF16)

        @pl.when(t == nt - 1)
        def _():
            dl0 = dlb_ref[...] * lb_all * (1.0 - lb_all)
            dlg_ref[0:1, :] = dl0
            dlg_ref[1:2, :] = -dl0

    blk = lambda j: pl.BlockSpec((HG_STEP, W), lambda t: (nt - 1 - t, j))
    full = pl.BlockSpec((2, W), lambda t: (0, 0))
    extra, extra_specs, adapt = _behind(after, 6)
    return pl.pallas_call(
        adapt(body), name=name, grid=(nt,),
        in_specs=[blk(0), blk(0), blk(2), full,
                  pl.BlockSpec((2 * HG_GROUPS, HGRN_HEADS, HGRN_DIM, HGRN_DIM), lambda t: (nt - 1 - t, 0, 0, 0)), blk(0)]
        + extra_specs,
        out_specs=[pl.BlockSpec((HG_STEP, 3 * W), lambda t: (nt - 1 - t, 0)), full],
        out_shape=[jax.ShapeDtypeStruct((S, 3 * W), BF16), jax.ShapeDtypeStruct((2, W), F32)],
        scratch_shapes=[pltpu.VMEM((HGRN_HEADS, HGRN_DIM, HGRN_DIM), F32), pltpu.VMEM((1, W), F32)],
        compiler_params=_cparams("arbitrary"),
    )(proj, proj_f, proj, logits, states, drec, *extra)


def _out_proj(attn, rec, proj_h, x, g_attn, g_hgrn, g_norm2, w_out, *, name, tm=512):
    S, D = x.shape
    AW, W = ATTN_W, HGRN_W

    def body(a_ref, r_ref, hg_ref, x_ref, ga_ref, gh_ref, g2_ref, w_ref, h_ref, u_ref, m_ref):
        av = a_ref[...]
        m_ref[:, :AW] = (av * _rstd(av) * ga_ref[...]).astype(BF16)
        for h in range(HGRN_HEADS):
            sl = slice(HGRN_DIM * h, HGRN_DIM * (h + 1))
            rv, hg = r_ref[:, sl], hg_ref[:, sl].astype(F32)
            m_ref[:, AW + HGRN_DIM * h:AW + HGRN_DIM * (h + 1)] = (
                (rv * _rstd(rv) * gh_ref[:, sl]) * (hg * _sigmoid(hg))).astype(BF16)
        h1 = x_ref[...] + _nn(m_ref[...], w_ref[...])
        h_ref[...] = h1
        u_ref[...] = (h1 * _rstd(h1) * g2_ref[...]).astype(BF16)

    row = lambda w, j=0: pl.BlockSpec((tm, w), lambda i: (i, j))
    vec = lambda w: pl.BlockSpec((1, w), lambda i: (0, 0))
    return pl.pallas_call(
        body, name=name, grid=(S // tm,),
        in_specs=[row(AW), row(W), row(W, 3), row(D), vec(AW), vec(W), vec(D), _resident(w_out.shape)],
        out_specs=[row(D), row(D), row(AW + W)],
        out_shape=[jax.ShapeDtypeStruct((S, D), F32), jax.ShapeDtypeStruct((S, D), BF16),
                   jax.ShapeDtypeStruct((S, AW + W), BF16)],
        compiler_params=_cparams("parallel"),
    )(attn, rec, proj_h, x, g_attn, g_hgrn, g_norm2, w_out)


def _dmix_post_bwd(dh1b, w_out, attn, rec, proj_h, g_attn, g_hgrn, *, name, tm=512, after=None):
    S, D = dh1b.shape
    AW, W = ATTN_W, HGRN_W

    def body(dh_ref, w_ref, a_ref, r_ref, hg_ref, ga_ref, gh_ref, do_ref, dl_ref, dr_ref, dhg_ref, dga_ref, dgh_ref):
        first = pl.program_id(0) == 0
        dmix = _nt(dh_ref[...], w_ref[...])
        av = a_ref[...]
        dov, dga = _norm_bwd(av, ga_ref[...], dmix[:, :AW])
        do_ref[...] = dov
        shift = HEAD_DIM.bit_length() - 1
        hi = lax.shift_right_logical(lax.broadcasted_iota(jnp.int32, (AW, AW), 0), shift)
        hj = lax.shift_right_logical(lax.broadcasted_iota(jnp.int32, (AW, AW), 1), shift)
        prod = dov * av
        hi_part = prod.astype(BF16)
        lo_part = (prod - hi_part.astype(F32)).astype(BF16)
        same_head = jnp.where(hi == hj, 1.0, 0.0).astype(BF16)
        dl_ref[...] = _nn(hi_part, same_head) + _nn(lo_part, same_head)
        _accumulate(dga_ref, jnp.sum(dga, axis=0, keepdims=True), first)

        @pl.when(first)
        def _():
            dgh_ref[...] = jnp.zeros_like(dgh_ref)

        for h in range(HGRN_HEADS):
            sl = slice(HGRN_DIM * h, HGRN_DIM * (h + 1))
            rv, hg, gv = r_ref[:, sl], hg_ref[:, sl].astype(F32), gh_ref[:, sl]
            dout = dmix[:, AW + HGRN_DIM * h:AW + HGRN_DIM * (h + 1)]
            sg = _sigmoid(hg)
            drv, dgh = _norm_bwd(rv, gv, dout * (hg * sg))
            dr_ref[:, sl] = drv
            dgh_ref[:, sl] += jnp.sum(dgh, axis=0, keepdims=True)
            dhg_ref[:, sl] = (dout * (rv * _rstd(rv) * gv) * (sg * (1.0 + hg * (1.0 - sg)))).astype(BF16)

    row = lambda w, j=0: pl.BlockSpec((tm, w), lambda i: (i, j))
    vec = lambda w: pl.BlockSpec((1, w), lambda i: (0, 0))
    extra, extra_specs, adapt = _behind(after, 7)
    return pl.pallas_call(
        adapt(body), name=name, grid=(S // tm,),
        in_specs=[row(D), _resident(w_out.shape), row(AW), row(W), row(W, 3), vec(AW), vec(W)] + extra_specs,
        out_specs=[row(AW), row(AW), row(W), row(W), vec(AW), vec(W)],
        out_shape=[jax.ShapeDtypeStruct((S, AW), F32), jax.ShapeDtypeStruct((S, AW), F32),
                   jax.ShapeDtypeStruct((S, W), F32), jax.ShapeDtypeStruct((S, W), BF16),
                   jax.ShapeDtypeStruct((1, AW), F32), jax.ShapeDtypeStruct((1, W), F32)],
        compiler_params=_cparams("arbitrary"),
    )(dh1b, w_out, attn, rec, proj_h, g_attn, g_hgrn, *extra)


def _conv_act(g, g1, g2, w_ref, b_ref):
    c = b_ref[...] + w_ref[0:1, :] * g2 + w_ref[1:2, :] * g1 + w_ref[2:3, :] * g
    return c, 0.5 * (1.0 + lax.erf(c * (2.0 ** -0.5)))


def _shift_down(g, halo, row):
    g1 = jnp.where(row == 0, halo[7:8], pltpu.roll(g, 1, 0))
    g2 = jnp.where(row == 0, halo[6:7], jnp.where(row == 1, halo[7:8], pltpu.roll(g, 2, 0)))
    return g1, g2


def _shift_up(x, halo, row):
    n = x.shape[0]
    x1 = jnp.where(row == n - 1, halo[0:1], pltpu.roll(x, n - 1, 0))
    x2 = jnp.where(row == n - 2, halo[0:1], jnp.where(row == n - 1, halo[1:2], pltpu.roll(x, n - 2, 0)))
    return x1, x2


def _up_glu(u, wt_up, conv_w, conv_b, *, name, tm=1024, tn=1408):
    S, D = u.shape
    F = wt_up.shape[0] // 2
    tn = _tile(F, tn)
    nf = F // tn

    def body(u_ref, wg_ref, wv_ref, cw_ref, cb_ref, g_ref, ge_ref, t_ref, a_ref, halo_ref):
        i, j = pl.program_id(0), pl.program_id(1)

        @pl.when(i == 0)
        def _():
            halo_ref[j] = jnp.zeros((SUBLANES, tn), F32)

        uv = u_ref[...]
        g, v = _nt(uv, wg_ref[...]), _nt(uv, wv_ref[...])
        row = lax.broadcasted_iota(jnp.int32, (tm, tn), 0)
        g1, g2 = _shift_down(g, halo_ref[j], row)
        c, cdf = _conv_act(g, g1, g2, cw_ref, cb_ref)
        gelu = c * cdf
        pdf = jnp.exp(-0.5 * c * c) * (1.0 / (2.0 * jnp.pi) ** 0.5)
        a_ref[...] = (gelu * v).astype(BF16)
        g_ref[...] = g.astype(BF16)
        ge_ref[...] = gelu.astype(BF16)
        t_ref[...] = (v * (cdf + c * pdf)).astype(BF16)
        halo_ref[j] = g[tm - SUBLANES:, :]

    col = pl.BlockSpec((tm, tn), lambda i, j: (i, j))
    out = jax.ShapeDtypeStruct((S, F), BF16)
    return pl.pallas_call(
        body, name=name, grid=(S // tm, nf),
        in_specs=[pl.BlockSpec((tm, D), lambda i, j: (i, 0)), pl.BlockSpec((tn, D), lambda i, j: (j, 0)),
                  pl.BlockSpec((tn, D), lambda i, j: (j + nf, 0)), pl.BlockSpec((3, tn), lambda i, j: (0, j)),
                  pl.BlockSpec((1, tn), lambda i, j: (0, j))],
        out_specs=[col, col, col, col], out_shape=[out, out, out, out],
        scratch_shapes=[pltpu.VMEM((nf, SUBLANES, tn), F32)], compiler_params=_cparams("arbitrary", "arbitrary"),
    )(u, wt_up, wt_up, conv_w, conv_b)


def _dact_glu_bwd(dh2b, w_down, gate, gelu, vslope, conv_w, *, name, tm=1024, tn=1408):
    S, D = dh2b.shape
    F = gate.shape[1]
    tn = _tile(F, tn)
    nf, ni = F // tn, S // tm

    def body(dh_ref, wd_ref, g_ref, ge_ref, t_ref, cw_ref, dg_ref, dv_ref, dcw_ref, dcb_ref, halo_ref, acc_ref):
        i, j = pl.program_id(0), pl.program_id(1)

        @pl.when(i == 0)
        def _():
            halo_ref[j] = jnp.zeros((SUBLANES, tn), F32)
            acc_ref[j] = jnp.zeros((SUBLANES, tn), F32)

        g = g_ref[...].astype(F32)
        row = lax.broadcasted_iota(jnp.int32, (tm, tn), 0)
        da = _nt(dh_ref[...], wd_ref[...])
        dv_ref[...] = (da * ge_ref[...].astype(F32)).astype(BF16)
        dc = da * t_ref[...].astype(F32)
        d1, d2 = _shift_up(dc, halo_ref[j], row)
        dg_ref[...] = (cw_ref[2:3, :] * dc + cw_ref[1:2, :] * d1 + cw_ref[0:1, :] * d2).astype(BF16)
        halo_ref[j] = dc[:SUBLANES, :]
        for k, t in enumerate((d2 * g, d1 * g, dc * g, dc)):
            acc_ref[j, k:k + 1, :] += jnp.sum(t, axis=0, keepdims=True)

        @pl.when((i == ni - 1) & (j == nf - 1))
        def _():
            for jj in range(nf):
                dcw_ref[:, jj * tn:(jj + 1) * tn] = acc_ref[jj, 0:3, :]
                dcb_ref[:, jj * tn:(jj + 1) * tn] = acc_ref[jj, 3:4, :]

    tile = pl.BlockSpec((tm, tn), lambda i, j: (ni - 1 - i, j))
    return pl.pallas_call(
        body, name=name, grid=(ni, nf),
        in_specs=[pl.BlockSpec((tm, D), lambda i, j: (ni - 1 - i, 0)), pl.BlockSpec((tn, D), lambda i, j: (j, 0)),
                  tile, tile, tile, pl.BlockSpec((3, tn), lambda i, j: (0, j))],
        out_specs=[tile, tile, pl.BlockSpec((3, F), lambda i, j: (0, 0)), pl.BlockSpec((1, F), lambda i, j: (0, 0))],
        out_shape=[jax.ShapeDtypeStruct((S, F), BF16), jax.ShapeDtypeStruct((S, F), BF16),
                   jax.ShapeDtypeStruct((3, F), F32), jax.ShapeDtypeStruct((1, F), F32)],
        scratch_shapes=[pltpu.VMEM((nf, SUBLANES, tn), F32), pltpu.VMEM((nf, SUBLANES, tn), F32)],
        compiler_params=_cparams("arbitrary", "arbitrary"),
    )(dh2b, w_down, gate, gelu, vslope, conv_w)


def _down_loss(act, w_down, h1, g, target, *, name, tm=512):
    S, F = act.shape
    D = h1.shape[1]

    def body(a_ref, w_ref, h_ref, g_ref, t_ref, dh_ref, dhb_ref, dg_ref, loss_ref):
        first = pl.program_id(0) == 0
        h2 = h_ref[...] + _nn(a_ref[...], w_ref[...])
        gv = g_ref[...]
        r = _rstd(h2)
        xh = h2 * r
        err = xh * gv - t_ref[...]
        part_loss = 0.5 * jnp.sum(jnp.mean(err * err, axis=-1, keepdims=True), axis=0, keepdims=True)
        dy = err * (1.0 / D)
        dxh = dy * gv
        dh = r * (dxh - xh * jnp.mean(dxh * xh, axis=-1, keepdims=True))
        dh_ref[...] = dh
        dhb_ref[...] = dh.astype(BF16)
        _accumulate(dg_ref, jnp.sum(dy * xh, axis=0, keepdims=True), first)
        _accumulate(loss_ref, jnp.broadcast_to(part_loss, (1, LANES)), first)

    row = lambda w: pl.BlockSpec((tm, w), lambda i: (i, 0))
    vec = lambda w: pl.BlockSpec((1, w), lambda i: (0, 0))
    return pl.pallas_call(
        body, name=name, grid=(S // tm,), in_specs=[row(F), _resident(w_down.shape), row(D), vec(D), row(D)],
        out_specs=[row(D), row(D), vec(D), vec(LANES)],
        out_shape=[jax.ShapeDtypeStruct((S, D), F32), jax.ShapeDtypeStruct((S, D), BF16),
                   jax.ShapeDtypeStruct((1, D), F32), jax.ShapeDtypeStruct((1, LANES), F32)],
        compiler_params=_cparams("arbitrary"),
    )(act, w_down, h1, g, target)


def _grad_norm_input(pieces, ws, x, g, add, *, name, tm=512, after=None):
    S, D = x.shape
    widths = [p.shape[1] for p in pieces]
    n, nw = len(pieces), len(ws)
    where, wi, off = [], 0, ws[0][1]
    for wd in widths:
        if off == ws[wi][0].shape[0]:
            wi, off = wi + 1, ws[wi + 1][1]
        where.append((wi, off))
        off += wd
    ws = [w for w, _ in ws]

    def body(*refs):
        p_refs, w_refs = refs[:n], refs[n:n + nw]
        x_ref, g_ref, add_ref, dx_ref, dxb_ref, dg_ref = refs[n + nw:]
        halves = _row_halves(tm)
        du = []
        for rows in halves:
            terms = [_nn(p_refs[k][rows, :], w_refs[wi][off:off + widths[k], :]) for k, (wi, off) in enumerate(where)]
            du.append(sum(terms[1:], terms[0]))
        dg_sum = None
        for rows, duh in zip(halves, du):
            dx, dg = _norm_bwd(x_ref[rows, :], g_ref[...], duh)
            dx = add_ref[rows, :] + dx
            dx_ref[rows, :] = dx
            dxb_ref[rows, :] = dx.astype(BF16)
            part = jnp.sum(dg, axis=0, keepdims=True)
            dg_sum = part if dg_sum is None else dg_sum + part
        _accumulate(dg_ref, dg_sum, pl.program_id(0) == 0)

    row = lambda w_: pl.BlockSpec((tm, w_), lambda i: (i, 0))
    vec = pl.BlockSpec((1, D), lambda i: (0, 0))
    extra, extra_specs, adapt = _behind(after, n + nw + 3)
    return pl.pallas_call(
        adapt(body), name=name, grid=(S // tm,),
        in_specs=[row(wd) for wd in widths] + [_resident(w.shape) for w in ws] + [row(D), vec, row(D)] + extra_specs,
        out_specs=[row(D), row(D), vec],
        out_shape=[jax.ShapeDtypeStruct((S, D), F32), jax.ShapeDtypeStruct((S, D), BF16),
                   jax.ShapeDtypeStruct((1, D), F32)],
        compiler_params=_cparams("arbitrary"),
    )(*pieces, *ws, x, g, add, *extra)


def _rows(a):
    return a.reshape(-1, a.shape[-1])


def _row_tile(rows, cols, itemsize=4, budget=1 << 20):
    t = rows
    while t % 32 == 0 and t * cols * itemsize > budget:
        t //= 2
    return t


def _sum_cast(arrs, out_dtype, *, name):
    shape = arrs[0].shape
    flat = [_rows(a) for a in arrs]
    R, C = flat[0].shape
    tr = _row_tile(R, C)

    def body(*refs):
        acc = refs[0][...].astype(F32)
        for r in refs[1:-1]:
            acc = acc + r[...].astype(F32)
        refs[-1][...] = acc.astype(out_dtype)

    spec = pl.BlockSpec((tr, C), lambda i: (i, 0))
    return pl.pallas_call(
        body, name=name, grid=(R // tr,), in_specs=[spec] * len(flat), out_specs=spec,
        out_shape=jax.ShapeDtypeStruct((R, C), out_dtype), compiler_params=_cparams("parallel"),
    )(*flat).reshape(shape)


def _adamw(parts, w, m, v, *, name):
    shape = w.shape
    w2, m2, v2 = _rows(w), _rows(m), _rows(v)
    R, C = w2.shape
    parts = [p.reshape(-1, R, C) for p in parts]
    tr = _row_tile(R, C)
    np_ = len(parts)
    c1, c2 = 1.0 - ADAM_B1 ** ADAM_STEP, 1.0 - ADAM_B2 ** ADAM_STEP

    def body(*refs):
        terms = [(r, k) for r in refs[:np_] for k in range(r.shape[0])]
        g = terms[0][0][terms[0][1]].astype(F32)
        for r, k in terms[1:]:
            g = g + r[k].astype(F32)
        w_ref, m_ref, v_ref, g_out, d_out, m_out, v_out = refs[np_:]
        mn = ADAM_B1 * m_ref[...] + (1.0 - ADAM_B1) * g
        vn = ADAM_B2 * v_ref[...] + (1.0 - ADAM_B2) * (g * g)
        g_out[...] = g
        d_out[...] = -ADAM_LR * ((mn / c1) / (jnp.sqrt(vn / c2) + ADAM_EPS) + ADAM_WD * w_ref[...])
        m_out[...] = mn
        v_out[...] = vn

    spec = pl.BlockSpec((tr, C), lambda i: (i, 0))
    out = jax.ShapeDtypeStruct((R, C), F32)
    stacks = [pl.BlockSpec((p.shape[0], tr, C), lambda i: (0, i, 0)) for p in parts]
    res = pl.pallas_call(
        body, name=name, grid=(R // tr,), in_specs=stacks + [spec] * 3, out_specs=[spec] * 4,
        out_shape=[out] * 4, compiler_params=_cparams("parallel"),
    )(*parts, w2, m2, v2)
    return [r.reshape(shape) for r in res]


def _adamw_packed(stack, widths, params, *, name):
    c1, c2 = 1.0 - ADAM_B1 ** ADAM_STEP, 1.0 - ADAM_B2 ** ADAM_STEP
    k = stack.shape[0]
    flat = [None if p is None else [_rows(a) for a in p] for p in params]
    n_in = sum(3 for p in flat if p is not None)

    def body(*refs):
        s_ref, ins, outs = refs[0], list(refs[1:1 + n_in]), list(refs[1 + n_in:])
        off = 0
        for width, p in zip(widths, flat):
            rows = 1 if p is None else p[0].shape[0]
            cols = width // rows
            w_ref, m_ref, v_ref = (None, None, None) if p is None else (ins.pop(0), ins.pop(0), ins.pop(0))
            o_refs = [outs.pop(0) for _ in range(1 if p is None else 4)]
            for r in range(rows):
                seg = slice(off + r * cols, off + (r + 1) * cols)
                g = s_ref[0, :, seg]
                for j in range(1, k):
                    g = g + s_ref[j, :, seg]
                o_refs[0][r:r + 1, :] = g
                if p is not None:
                    row = slice(r, r + 1)
                    mn = ADAM_B1 * m_ref[row, :] + (1.0 - ADAM_B1) * g
                    vn = ADAM_B2 * v_ref[row, :] + (1.0 - ADAM_B2) * (g * g)
                    o_refs[1][row, :] = -ADAM_LR * ((mn / c1) / (jnp.sqrt(vn / c2) + ADAM_EPS) + ADAM_WD * w_ref[row, :])
                    o_refs[2][row, :] = mn
                    o_refs[3][row, :] = vn
            off += width

    operands, out_shape = [stack], []
    for width, p in zip(widths, flat):
        if p is None:
            out_shape.append(jax.ShapeDtypeStruct((1, width), F32))
        else:
            operands += p
            out_shape += [jax.ShapeDtypeStruct(p[0].shape, F32)] * 4
    res = list(pl.pallas_call(body, name=name, out_shape=out_shape)(*operands))
    out = []
    for p, orig in zip(flat, params):
        n = 1 if p is None else 4
        out.append([r if orig is None else r.reshape(orig[0].shape) for r in res[:n]])
        res = res[n:]
    return out


def _coords():
    return lax.axis_index("x"), lax.axis_index("y"), lax.axis_index("c")


def _all_gather(shards, *, name):
    n = len(shards)

    def body(*refs):
        x_refs, out_refs = refs[:n], refs[n:2 * n]
        send_sems, recv_sems, local_sems = refs[2 * n:]
        x, y, c = _coords()
        me, sibling = (x, y, c), (x, y, 1 - c)
        chips = [(1 - x, y), (x, 1 - y), (1 - x, 1 - y)]

        def slot(a, dev):
            return out_refs[a].at[4 * dev[0] + 2 * dev[1] + dev[2]]

        def copy(a, k, block, to, src=None):
            return pltpu.make_async_remote_copy(
                src_ref=slot(a, block) if src is None else src, dst_ref=slot(a, block),
                send_sem=send_sems.at[7 * a + k], recv_sem=recv_sems.at[7 * a + k], device_id=to, device_id_type=MESH)

        mine = [pltpu.make_async_copy(x_refs[a], slot(a, me), local_sems.at[a]) for a in range(n)]
        for cp in mine:
            cp.start()
        first = []
        for a in range(n):
            first.append(copy(a, 0, me, sibling, src=x_refs[a]))
            first += [copy(a, 1 + j, me, (*chip, c), src=x_refs[a]) for j, chip in enumerate(chips)]
        for cp in first:
            cp.start()
        passed = []
        for j, chip in enumerate(chips):
            for a in range(n):
                copy(a, 1 + j, (*chip, c), me).wait_recv()
                fwd = copy(a, 4 + j, (*chip, c), sibling)
                fwd.start()
                passed.append(fwd)
        for a in range(n):
            copy(a, 0, sibling, me).wait_recv()
            for j, chip in enumerate(chips):
                copy(a, 4 + j, (*chip, 1 - c), me).wait_recv()
        for cp in first + passed:
            cp.wait_send()
        for cp in mine:
            cp.wait()

    return pl.pallas_call(
        body, name=name, in_specs=[HBM] * n, out_specs=[HBM] * n,
        out_shape=[jax.ShapeDtypeStruct((N_DEV, *s.shape), s.dtype) for s in shards],
        scratch_shapes=[pltpu.SemaphoreType.DMA((7 * n,)), pltpu.SemaphoreType.DMA((7 * n,)),
                        pltpu.SemaphoreType.DMA((n,))],
    )(*shards)


def _flip_y(x, y, c):
    return (x, 1 - y, c)


def _flip_x(x, y, c):
    return (1 - x, y, c)


def _flip_xy(x, y, c):
    return (1 - x, 1 - y, c)


SEM = pl.BlockSpec(memory_space=pltpu.SEMAPHORE)
SIDE_EFFECT = pltpu.SideEffectType.DATAFLOW_SIDE_EFFECTING


def _in_hbm(a):
    return pltpu.with_memory_space_constraint(a, pltpu.HBM)


def _copies_start(srcs, lands, plan, n_copies, *, name, after=None):
    ns, nl = len(srcs), len(lands)
    extra = [] if after is None else [after]

    def body(*refs):
        src_refs, land_refs = refs[:ns], refs[ns:ns + nl]
        send_sems, recv_sems = refs[ns + nl + len(extra):ns + nl + len(extra) + 2]
        token = refs[-1]
        for k, (src, dst, peer, _) in enumerate(plan(src_refs, land_refs, *_coords())):
            pltpu.make_async_remote_copy(src_ref=src, dst_ref=dst, send_sem=send_sems.at[k], recv_sem=recv_sems.at[k],
                                         device_id=peer, device_id_type=MESH).start()
        token[...] = jnp.zeros_like(token)

    bufs = [*srcs, *lands]
    res = pl.pallas_call(
        body, name=name, in_specs=[HBM] * (ns + nl) + [pl.BlockSpec(memory_space=pl.ANY)] * len(extra),
        out_specs=(SEM, SEM, *[HBM] * (ns + nl), pl.BlockSpec(memory_space=pltpu.VMEM)),
        out_shape=(pltpu.SemaphoreType.DMA((n_copies,)), pltpu.SemaphoreType.DMA((n_copies,)),
                   *[pltpu.HBM(b.shape, b.dtype) for b in bufs], jax.ShapeDtypeStruct((SUBLANES, LANES), F32)),
        input_output_aliases={i: 2 + i for i in range(ns + nl)},
        compiler_params=pltpu.CompilerParams(has_side_effects=SIDE_EFFECT),
    )(*[_in_hbm(b) for b in bufs], *extra)
    return res[0], res[1], list(res[2:2 + ns]), list(res[2 + ns:2 + ns + nl]), res[-1]


def _copies_wait(started, plan, after, *, name, with_srcs=False):
    send_sems, recv_sems, srcs, lands, _ = started
    ns, nl = len(srcs), len(lands)

    def body(*refs):
        src_refs, land_refs = refs[:ns], refs[ns:ns + nl]
        send_sems, recv_sems = refs[ns + nl:ns + nl + 2]
        for k, (src, dst, peer, here) in enumerate(plan(src_refs, land_refs, *_coords())):
            pltpu.make_async_remote_copy(src_ref=src, dst_ref=dst, send_sem=send_sems.at[k], recv_sem=recv_sems.at[k],
                                         device_id=peer, device_id_type=MESH).wait_send()
            pltpu.make_async_remote_copy(src_ref=src, dst_ref=here, send_sem=send_sems.at[k], recv_sem=recv_sems.at[k],
                                         device_id=peer, device_id_type=MESH).wait_recv()

    bufs = [*srcs, *lands]
    res = pl.pallas_call(
        body, name=name, in_specs=[HBM] * (ns + nl) + [SEM, SEM, pl.BlockSpec(memory_space=pl.ANY)],
        out_specs=[HBM] * (ns + nl), out_shape=[pltpu.HBM(b.shape, b.dtype) for b in bufs],
        input_output_aliases={i: i for i in range(ns + nl)},
        compiler_params=pltpu.CompilerParams(has_side_effects=SIDE_EFFECT),
    )(*bufs, send_sems, recv_sems, after)
    return (list(res[:ns]), list(res[ns:])) if with_srcs else list(res[ns:])


def _dev_index(dev):
    return 4 * dev[0] + 2 * dev[1] + dev[2]


def _ag_chips_plan(src_refs, land_refs, x, y, c):
    me = _dev_index((x, y, c))
    return [(src, land.at[me], peer, land.at[_dev_index(peer)])
            for src, land in zip(src_refs, land_refs) for peer in (_flip_y(x, y, c), _flip_x(x, y, c), _flip_xy(x, y, c))]


def _ag_sibling_plan(src_refs, land_refs, x, y, c):
    chips = [(x, y), (x, 1 - y), (1 - x, y), (1 - x, 1 - y)]
    return [(land.at[_dev_index((*chip, c))], land.at[_dev_index((*chip, c))], (x, y, 1 - c),
             land.at[_dev_index((*chip, 1 - c))]) for land in land_refs for chip in chips]


def _ag_direct_plan(src_refs, land_refs, x, y, c):
    me = _dev_index((x, y, c))
    plan = []
    for src, land in zip(src_refs, land_refs):
        for m in range(1, N_DEV):
            peer = (x + (m >> 2) * (1 - 2 * x), y + ((m >> 1) & 1) * (1 - 2 * y), c + (m & 1) * (1 - 2 * c))
            plan.append((src, land.at[me], peer, land.at[_dev_index(peer)]))
    return plan


def _rs_direct_plan(src_refs, land_refs, x, y, c):
    plan = []
    for src, land in zip(src_refs, land_refs):
        for m in range(1, N_DEV):
            peer = (x + (m >> 2) * (1 - 2 * x), y + ((m >> 1) & 1) * (1 - 2 * y), c + (m & 1) * (1 - 2 * c))
            plan.append((src.at[_dev_index(peer)], land.at[m - 1], peer, land.at[m - 1]))
    return plan


def _rs_start(grads, me, *, name, after=None):
    own = [lax.dynamic_index_in_dim(g, me, 0, keepdims=False) for g in grads]
    lands = [lax.empty((N_DEV - 1, *g.shape[1:]), g.dtype) for g in grads]
    return _copies_start(grads, lands, _rs_direct_plan, (N_DEV - 1) * len(grads), name=name, after=after), own


def _rs_finish(started, after, *, name):
    handle, own = started
    got = _copies_wait(handle, _rs_direct_plan, after, name=name)
    return [[o, land] for o, land in zip(own, got)]


def _gathered_cols(w8):
    return w8.transpose(1, 0, 2).reshape(w8.shape[1], -1)


def _pair_major(wt):
    return wt.reshape(3, ATTN_W // LANES, LANES, -1).transpose(1, 0, 2, 3).reshape(3 * ATTN_W, -1)


def kernel(x, norm1_g, w_in, attn_norm_g, hgrn_norm_g, hgrn_lb_logits, w_out, norm2_g, w_up, conv_w, conv_b, w_down, final_norm_g, loss_target, m_norm1_g, m_w_in, m_attn_norm_g, m_hgrn_norm_g, m_hgrn_lb_logits, m_w_out, m_norm2_g, m_w_up, m_conv_w, m_conv_b, m_w_down, m_final_norm_g, v_norm1_g, v_w_in, v_attn_norm_g, v_hgrn_norm_g, v_hgrn_lb_logits, v_w_out, v_norm2_g, v_w_up, v_conv_w, v_conv_b, v_w_down, v_final_norm_g):
    xs, target = x[0], loss_target[0]
    S, D = xs.shape
    NA = 3 * ATTN_W
    fng = final_norm_g.reshape(1, D)

    t = lambda a: a[0].T
    casts = [_sum_cast([w], BF16, name=f"cast_{nm}") for nm, w in
             (("w_in", t(w_in)), ("w_out", w_out[0]), ("w_up", t(w_up)), ("w_down", w_down[0]))]
    me = _dev_index(_coords())
    (g_in,) = _all_gather(casts[:1], name="ag_w_in")
    later = casts[1:] + [conv_w[0]]
    ag1 = _copies_start(later, [lax.empty((N_DEV, *s.shape), s.dtype) for s in later], _ag_chips_plan,
                        3 * len(later), name="ag_chips_start", after=g_in)
    wi = g_in.reshape(-1, D)
    wi_a = _pair_major(wi[:NA])

    u1, proj_a, proj_h, proj_f = _in_proj(xs, norm1_g, wi_a, wi, NA, name="in_proj", after=ag1[4])
    attn, lse = _attn_fwd(proj_a, name="attn_fwd")
    later, lands = _copies_wait(ag1, _ag_chips_plan, attn, name="ag_chips_wait", with_srcs=True)
    lands = [lax.dynamic_update_index_in_dim(l, s, me, 0) for l, s in zip(lands, later)]
    ag2 = _copies_start([], lands, _ag_sibling_plan, 4 * len(later), name="ag_sibling_start")
    rec, states = _hgrn_fwd(proj_h, proj_f, hgrn_lb_logits, name="hgrn_fwd", after=ag2[4])
    g_out, g_up, g_down, g_cw = _copies_wait(ag2, _ag_sibling_plan, rec, name="ag_sibling_wait")
    wo = g_out.reshape(-1, D)
    wu = g_up.reshape(-1, D)
    wd = g_down.reshape(-1, D)
    cw = _gathered_cols(g_cw)
    h1, u2, mixed = _out_proj(attn, rec, proj_h, xs, attn_norm_g, hgrn_norm_g, norm2_g, wo, name="out_proj")
    gate, gelu, vslope, act = _up_glu(u2, wu, cw, conv_b, name="up_glu")
    dh2, dh2b, d_fng, loss_part = _down_loss(act, wd, h1, fng, target, name="down_loss")

    dgate, dval, d_cw, d_cb = _dact_glu_bwd(dh2b, wd, gate, gelu, vslope, cw, name="dact_glu_bwd")
    dw_down = _mm_tn([act], dh2b, tm=256, name="dw_down")
    dh1, dh1b, d_n2g = _grad_norm_input([dgate, dval], [(wu, 0)], h1, norm2_g, dh2, name="du2_norm2_bwd")
    F = dgate.shape[1]
    dw_up = _mm_tn([dgate, dval], u2, tm=256, name="dw_up")
    rs_ffn = _rs_start([dw_down.reshape(N_DEV, -1, D), dw_up.reshape(N_DEV, -1, D)], me, name="rs_ffn_start")
    dattn, delta, drec, dhg, d_ang, d_hng = _dmix_post_bwd(dh1b, wo, attn, rec, proj_h, attn_norm_g, hgrn_norm_g,
                                                          name="dmix_post_bwd", after=rs_ffn[0][4])
    dw_out = _mm_tn([mixed], dh1b, name="dw_out")
    rs_out = _rs_start([dw_out.reshape(N_DEV, -1, D)], me, name="rs_out_start")
    dproj_h, d_lbl = _hgrn_bwd(proj_h, proj_f, hgrn_lb_logits, states, drec, name="hgrn_bwd", after=rs_out[0][4])
    small = [("loss", loss_part, None, None, None),
             ("attn_norm_g", d_ang, attn_norm_g, m_attn_norm_g, v_attn_norm_g),
             ("hgrn_norm_g", d_hng, hgrn_norm_g, m_hgrn_norm_g, v_hgrn_norm_g),
             ("hgrn_lb_logits", d_lbl, hgrn_lb_logits, m_hgrn_lb_logits, v_hgrn_lb_logits),
             ("norm2_g", d_n2g, norm2_g, m_norm2_g, v_norm2_g),
             ("conv_b", d_cb, conv_b, m_conv_b, v_conv_b),
             ("final_norm_g", d_fng, final_norm_g, m_final_norm_g, v_final_norm_g)]
    pack = lambda arrs: jnp.concatenate([a.reshape(1, -1) for a in arrs], axis=1)
    small_own = [pack([s[1] for s in small]), d_cw]
    ag_small = _copies_start(small_own, [lax.empty((N_DEV, *s.shape), s.dtype) for s in small_own], _ag_direct_plan,
                             (N_DEV - 1) * len(small_own), name="ag_small_start")
    dproj_a = _attn_bwd(proj_a, dattn, lse, delta, name="attn_bwd")
    pairs = ATTN_W // LANES
    dw_in = _mm_tn([dproj_a], u1, tm=LANES, rows=wi.shape[0], row_block=lambda i: pairs * (i % 3) + i // 3,
                   name="dw_in_attn")
    dw_in = _mm_tn([dproj_h, dhg], u1, tm=256, rows=wi.shape[0], row_block=lambda i: i + NA // 256, into=dw_in,
                   name="dw_in_hgrn")
    rs_in = _rs_start([dw_in.reshape(N_DEV, -1, D)], me, name="rs_in_start", after=ag_small[4])
    grad_x, _, d_n1g = _grad_norm_input([dproj_a, dproj_h, dhg], [(wi_a, 0), (wi, NA)], xs,
                                        norm1_g, dh1, name="du1_norm1_bwd", after=rs_in[0][4])

    res = {}

    def update(nm, parts, w, m, v, transposed=False):
        if transposed:
            raw = _adamw(parts, t(w), t(m), t(v), name=f"adamw_{nm}")
            res[nm] = [r.T[None] for r in raw]
        else:
            raw = res[nm] = _adamw(parts, w, m, v, name=f"adamw_{nm}")
        return raw[1]

    g_down, g_up = _rs_finish(rs_ffn, grad_x, name="rs_ffn_wait")
    update("w_down", g_down, w_down, m_w_down, v_w_down)
    done_up = update("w_up", g_up, w_up, m_w_up, v_w_up, transposed=True)
    (g_out,) = _rs_finish(rs_out, grad_x, name="rs_out_wait")
    update("w_out", g_out, w_out, m_w_out, v_w_out)
    (g_in,) = _rs_finish(rs_in, done_up, name="rs_in_wait")
    done_in = update("w_in", g_in, w_in, m_w_in, v_w_in, transposed=True)

    small_own, small_all = _copies_wait(ag_small, _ag_direct_plan, grad_x, name="ag_small_wait", with_srcs=True)
    g_small, g_dcw = [lax.dynamic_update_index_in_dim(l, s, me, 0) for l, s in zip(small_all, small_own)]
    sm = _adamw_packed(g_small, [s[1].size for s in small], [None if s[2] is None else s[2:] for s in small],
                       name="adamw_small")
    for (nm, *_), r in zip(small, sm):
        res[nm] = r
    ncw = conv_w.shape[-1]
    mine_cw = lax.dynamic_slice_in_dim(g_dcw, me * ncw, ncw, axis=2)
    res["conv_w"] = _adamw([mine_cw], conv_w, m_conv_w, v_conv_w, name="adamw_conv_w")
    late, _ = lax.optimization_barrier((d_n1g, done_in))
    update("norm1_g", _all_gather([late], name="ag_norm1_grad"), norm1_g, m_norm1_g, v_norm1_g)

    loss = res["loss"][0][0, 0]
    order = ["norm1_g", "w_in", "attn_norm_g", "hgrn_norm_g", "hgrn_lb_logits", "w_out", "norm2_g", "w_up",
             "conv_w", "conv_b", "w_down", "final_norm_g"]
    return (loss, grad_x[None], *[res[nm][0] for nm in order], *[res[nm][1] for nm in order],
            *[res[nm][2] for nm in order], *[res[nm][3] for nm in order])
```

```python
import math

import jax
import jax.numpy as jnp
from jax import lax
from jax.experimental import pallas as pl
from jax.experimental.pallas import tpu as pltpu

F32, BF16 = jnp.float32, jnp.bfloat16
NORM_EPS = 1e-6
ATTN_HEADS, HEAD_DIM, ATTN_BLOCK = 8, 64, 128
DILATIONS = (1, 4, 16)
ATTN_SCALE = HEAD_DIM ** -0.5
ATTN_W = ATTN_HEADS * HEAD_DIM
HGRN_HEADS, HGRN_DIM, HGRN_CHUNK = 4, 128, 64
HGRN_W = HGRN_HEADS * HGRN_DIM
ADAM_LR, ADAM_B1, ADAM_B2, ADAM_EPS, ADAM_WD, ADAM_STEP = 0.001, 0.9, 0.999, 1e-08, 0.01, 10
LANES, SUBLANES = 128, 8
VMEM_LIMIT_BYTES = 56 * 1024 * 1024
N_DEV = 8
MESH = pl.DeviceIdType.MESH
HBM = pl.BlockSpec(memory_space=pltpu.HBM)
HIGHEST = lax.Precision.HIGHEST


def _cparams(*sem):
    return pltpu.CompilerParams(dimension_semantics=sem, vmem_limit_bytes=VMEM_LIMIT_BYTES)


def _tile(n, pref):
    if n <= pref:
        return n
    t = (pref // LANES) * LANES
    while n % t:
        t -= LANES
    return t


def _resident(shape):
    return pl.BlockSpec(shape, lambda *_: (0,) * len(shape), pipeline_mode=pl.Buffered(1))


def _dot(a, b, dims, precision=None):
    return lax.dot_general(a, b, (dims, ((), ())), precision=precision, preferred_element_type=F32)


def _nn(a, b, precision=None):
    return _dot(a, b, ((1,), (0,)), precision)


def _nt(a, b):
    return _dot(a, b, ((1,), (1,)))


def _tn(a, b):
    return _dot(a, b, ((0,), (0,)))


def _sigmoid(x):
    return 1.0 / (1.0 + jnp.exp(-x))


def _rstd(x):
    return lax.rsqrt(jnp.mean(x * x, axis=-1, keepdims=True) + NORM_EPS)


def _norm_bwd(x, g, du):
    r = _rstd(x)
    xh = x * r
    dxh = du * g
    return r * (dxh - xh * jnp.mean(dxh * xh, axis=-1, keepdims=True)), du * xh


def _row_halves(tm):
    return [pl.ds(0, tm // 2), pl.ds(tm // 2, tm // 2)]


def _behind(after, n_in):
    if after is None:
        return [], [], (lambda body: body)
    return ([after], [pl.BlockSpec(memory_space=pl.ANY)],
            lambda body: (lambda *refs: body(*refs[:n_in], *refs[n_in + 1:])))


def _accumulate(ref, part, first):
    @pl.when(first)
    def _():
        ref[...] = part

    @pl.when(jnp.logical_not(first))
    def _():
        ref[...] += part


def _mm_tn(xs, dy, *, name, tm=512, tn=1024, rows=None, row_block=None, into=None):
    S, N = dy.shape
    tm = _tile(math.gcd(*[x.shape[1] for x in xs]), tm)
    tn = _tile(N, tn)
    blocks = [x.shape[1] // tm for x in xs]
    first = [sum(blocks[:k]) for k in range(len(xs))]
    row_block = row_block or (lambda i: i)
    n = len(xs)

    def body(*refs):
        x_refs, dy_ref = refs[:n], refs[n]
        o_ref, xt_ref = refs[-2:]
        @pl.when(pl.program_id(1) == 0)
        def _():
            xv = x_refs[0][...]
            for x_ref, b0 in zip(x_refs[1:], first[1:]):
                xv = jnp.where(pl.program_id(0) >= b0, x_ref[...], xv)
            xt_ref[...] = xv.T

        o_ref[...] = _nn(xt_ref[...], dy_ref[...]).astype(BF16)

    def x_spec(b0, nb):
        return pl.BlockSpec((S, tm), lambda i, j: (0, jnp.clip(i - b0, 0, nb - 1)))

    operands = [*xs, dy] + ([] if into is None else [into])
    return pl.pallas_call(
        body, name=name, grid=(sum(blocks), N // tn),
        in_specs=[x_spec(b0, nb) for b0, nb in zip(first, blocks)] + [pl.BlockSpec((S, tn), lambda i, j: (0, j))]
        + ([] if into is None else [pl.BlockSpec(memory_space=pl.ANY)]),
        out_specs=pl.BlockSpec((tm, tn), lambda i, j: (row_block(i), j)),
        out_shape=jax.ShapeDtypeStruct((rows or sum(blocks) * tm, N), BF16),
        input_output_aliases={} if into is None else {n + 1: 0},
        scratch_shapes=[pltpu.VMEM((tm, S), BF16)], compiler_params=_cparams("parallel", "arbitrary"),
    )(*operands)


def _in_proj(x, g, wt_attn, wt, row0, *, name, tm=512, after=None):
    S, D = x.shape
    NA, NH, W = wt_attn.shape[0], wt.shape[0] - row0, HGRN_W

    def body(x_ref, g_ref, wa_ref, w_ref, u_ref, a_ref, h_ref, f_ref):
        xv = x_ref[...]
        u = (xv * _rstd(xv) * g_ref[...]).astype(BF16)
        u_ref[...] = u
        a_ref[...] = _nt(u, wa_ref[...]).astype(BF16)
        ph = _nt(u, w_ref[row0:row0 + NH, :])
        h_ref[...] = ph.astype(BF16)
        f_ref[...] = ph[:, W:2 * W]

    row = lambda w: pl.BlockSpec((tm, w), lambda i: (i, 0))
    extra, extra_specs, adapt = _behind(after, 4)
    return pl.pallas_call(
        adapt(body), name=name, grid=(S // tm,),
        in_specs=[row(D), pl.BlockSpec((1, D), lambda i: (0, 0)), _resident(wt_attn.shape), _resident(wt.shape)]
        + extra_specs,
        out_specs=[row(D), row(NA), row(NH), row(W)],
        out_shape=[jax.ShapeDtypeStruct((S, D), BF16), jax.ShapeDtypeStruct((S, NA), BF16),
                   jax.ShapeDtypeStruct((S, NH), BF16), jax.ShapeDtypeStruct((S, W), F32)],
        compiler_params=_cparams("parallel"),
    )(x, g, wt_attn, wt, *extra)


PAIR_W = 3 * LANES
ATTN_UNROLL_FWD, ATTN_UNROLL_BWD = 8, 4


def _attn_masks(first):
    qi = lax.broadcasted_iota(jnp.int32, (ATTN_BLOCK, 2 * ATTN_BLOCK), 0)
    kj = lax.broadcasted_iota(jnp.int32, (ATTN_BLOCK, 2 * ATTN_BLOCK), 1)
    dist = qi + ATTN_BLOCK - kj
    valid = (dist >= 0) & (dist <= ATTN_BLOCK) & jnp.logical_or(kj >= ATTN_BLOCK, jnp.logical_not(first))
    lane = lax.broadcasted_iota(jnp.int32, (1, LANES), 1)
    return valid, lane


def _for_residue_blocks(S, d, fn):
    span = ATTN_BLOCK * d
    nb = S // span

    def step(n, carry):
        base = pl.multiple_of(n * span, span)
        for r in range(d):
            off = pl.multiple_of((r * nb + n) * ATTN_BLOCK, ATTN_BLOCK)
            fn(lambda ref, r=r: _block_rows(ref, base, r, d),
               lambda ref, val, r=r: _set_block_rows(ref, base, r, d, val), off)
        return carry

    lax.fori_loop(0, nb, step, 0)


def _for_blocks(S, unroll, fn):
    def step(i, carry):
        fn([(pl.multiple_of((i * unroll + u) * ATTN_BLOCK, ATTN_BLOCK), i * unroll + u) for u in range(unroll)])
        return carry

    lax.fori_loop(0, S // ATTN_BLOCK // unroll, step, 0)


def _head_value(x2, lane, e):
    return jnp.sum(jnp.where(lane == HEAD_DIM * e, x2, 0.0), axis=-1, keepdims=True)


def _block_rows(ref, base, r, d):
    if d == 1:
        return ref[pl.ds(base, ATTN_BLOCK), :]
    return ref.at[pl.ds(base, ATTN_BLOCK * d)][pl.ds(r, ATTN_BLOCK, stride=d), :]


def _set_block_rows(ref, base, r, d, val):
    if d == 1:
        ref[pl.ds(base, ATTN_BLOCK), :] = val
    else:
        ref.at[pl.ds(base, ATTN_BLOCK * d)][pl.ds(r, ATTN_BLOCK, stride=d), :] = val


def _order4_to_16(src, dst, pad):
    S = src.shape[0]
    q4, q16 = S // 4, S // 16
    for r in range(4):
        for a in range(4):
            for n in range(q16 // ATTN_BLOCK):
                rows = src.at[pl.ds(r * q4 + 4 * ATTN_BLOCK * n, 4 * ATTN_BLOCK)][pl.ds(a, ATTN_BLOCK, stride=4), :]
                dst[pl.ds(pad + (4 * a + r) * q16 + ATTN_BLOCK * n, ATTN_BLOCK), :] = rows.astype(dst.dtype)


def _order16_to_4(src, pad, dst):
    S = dst.shape[0]
    q4, q16 = S // 4, S // 16
    for r in range(4):
        for a in range(4):
            for n in range(q16 // ATTN_BLOCK):
                rows = src[pl.ds(pad + (4 * a + r) * q16 + ATTN_BLOCK * n, ATTN_BLOCK), :]
                dst.at[pl.ds(r * q4 + 4 * ATTN_BLOCK * n, 4 * ATTN_BLOCK)][pl.ds(a, ATTN_BLOCK, stride=4), :] = rows


def _regroup(S, d, pairs, tmp):
    for src, dst, pad in pairs:
        if d == 16:
            def to_tmp(rows, _, off, src=src):
                tmp[pl.ds(off, ATTN_BLOCK), :] = rows(src)

            _for_residue_blocks(S, 4, to_tmp)
            _order4_to_16(tmp, dst, pad)
    if d != 16:
        def to_dst(rows, _, off):
            for src, dst, pad in pairs:
                dst[pl.ds(pad + off, ATTN_BLOCK), :] = rows(src).astype(dst.dtype)

        _for_residue_blocks(S, d, to_dst)


def _split_pair(p_ref, qs, ks, vs, bk, bv):
    qs[...] = p_ref[:, 0:LANES].astype(F32) * ATTN_SCALE
    ks[...] = p_ref[:, LANES:2 * LANES].astype(F32)
    vs[...] = p_ref[:, 2 * LANES:3 * LANES].astype(F32)
    bk[0:ATTN_BLOCK, :] = jnp.zeros((ATTN_BLOCK, LANES), bk.dtype)
    bv[0:ATTN_BLOCK, :] = jnp.zeros((ATTN_BLOCK, LANES), bv.dtype)


def _attn_fwd(proj_a, *, name):
    S = proj_a.shape[0]

    def body(p_ref, o_ref, l_ref, qs, ks, vs, bq, bk, bv, bo, bl, to, tl):
        _split_pair(p_ref, qs, ks, vs, bk, bv)
        for d in DILATIONS:
            nb = S // (ATTN_BLOCK * d)
            _regroup(S, d, ((qs, bq, 0), (ks, bk, ATTN_BLOCK), (vs, bv, ATTN_BLOCK)), to)

            def blocks(group, nb=nb):
                lane = lax.broadcasted_iota(jnp.int32, (1, LANES), 1)
                heads = [(lane >= HEAD_DIM * e) & (lane < HEAD_DIM * (e + 1)) for e in range(LANES // HEAD_DIM)]
                wins = [pl.ds(off, 2 * ATTN_BLOCK) for off, _ in group]
                s = [[_nt(jnp.where(mh, bq[pl.ds(off, ATTN_BLOCK), :], jnp.zeros((ATTN_BLOCK, LANES), BF16)), bk[win, :])
                      for mh in heads] for (off, _), win in zip(group, wins)]
                p, m, l = [], [], []
                for (off, b), su in zip(group, s):
                    valid, _ = _attn_masks(jnp.bitwise_and(b, nb - 1) == 0)
                    sm = [jnp.where(valid, x, -jnp.inf) for x in su]
                    m.append([jnp.max(x, axis=-1, keepdims=True) for x in sm])
                    p.append([jnp.exp(x - mx) for x, mx in zip(sm, m[-1])])
                    l.append([jnp.sum(x, axis=-1, keepdims=True) for x in p[-1]])
                o = [[_nn(x.astype(BF16), bv[win, :]) for x in pu] for pu, win in zip(p, wins)]
                for (off, _), ou, mu, lu in zip(group, o, m, l):
                    o2 = jnp.zeros((ATTN_BLOCK, LANES), F32)
                    l2 = jnp.zeros((ATTN_BLOCK, LANES), F32)
                    for mh, oe, me_, le in zip(heads, ou, mu, lu):
                        o2 = jnp.where(mh, oe / le, o2)
                        l2 = jnp.where(mh, me_ + jnp.log(le), l2)
                    bo[pl.ds(off, ATTN_BLOCK), :] = o2
                    bl[pl.ds(off, ATTN_BLOCK), :] = l2

            _for_blocks(S, ATTN_UNROLL_FWD, blocks)

            if d == 16:
                _order16_to_4(bo, 0, to)
                _order16_to_4(bl, 0, tl)
            src_o, src_l = (to, tl) if d == 16 else (bo, bl)

            def merge(rows, set_rows, off, d=d, src_o=src_o, src_l=src_l):
                blk = pl.ds(off, ATTN_BLOCK)
                o2, l2 = src_o[blk, :], src_l[blk, :]
                if d != DILATIONS[0]:
                    lo, oo = rows(l_ref), rows(o_ref)
                    ln = jnp.maximum(lo, l2)
                    wa, wb = jnp.exp(lo - ln), jnp.exp(l2 - ln)
                    o2 = (wa * oo + wb * o2) / (wa + wb)
                    l2 = ln + jnp.log(wa + wb)
                set_rows(o_ref, o2)
                set_rows(l_ref, l2)

            _for_residue_blocks(S, min(d, 4), merge)

    slab = pl.BlockSpec((S, LANES), lambda p: (0, p))
    f32_slab, bf16_slab = pltpu.VMEM((S, LANES), F32), pltpu.VMEM((S, LANES), BF16)
    bf16_window = pltpu.VMEM((S + ATTN_BLOCK, LANES), BF16)
    return pl.pallas_call(
        body, name=name, grid=(ATTN_W // LANES,), in_specs=[pl.BlockSpec((S, PAIR_W), lambda p: (0, p))],
        out_specs=[slab, slab],
        out_shape=[jax.ShapeDtypeStruct((S, ATTN_W), F32), jax.ShapeDtypeStruct((S, ATTN_W), F32)],
        scratch_shapes=[f32_slab] * 3 + [bf16_slab, bf16_window, bf16_window] + [f32_slab] * 4,
        compiler_params=_cparams("parallel"),
    )(proj_a)


def _attn_bwd(proj_a, do, lse, delta, *, name):
    S = proj_a.shape[0]

    def body(p_ref, do_ref, lse_ref, dl_ref, o_ref, qs, ks, vs, dqs, dks, dvs, bq, bk, bv, bdo, blse, bdl, bdq, bdk, bdv,
             tmp):
        _split_pair(p_ref, qs, ks, vs, bk, bv)
        bdk[0:ATTN_BLOCK, :] = jnp.zeros((ATTN_BLOCK, LANES), F32)
        bdv[0:ATTN_BLOCK, :] = jnp.zeros((ATTN_BLOCK, LANES), F32)
        for d in DILATIONS:
            nb = S // (ATTN_BLOCK * d)
            _regroup(S, d, ((qs, bq, 0), (ks, bk, ATTN_BLOCK), (vs, bv, ATTN_BLOCK), (do_ref, bdo, 0),
                            (lse_ref, blse, 0), (dl_ref, bdl, 0)), tmp)

            def blocks(group, nb=nb):
                lane = lax.broadcasted_iota(jnp.int32, (1, LANES), 1)
                heads = [(lane >= HEAD_DIM * e) & (lane < HEAD_DIM * (e + 1)) for e in range(LANES // HEAD_DIM)]
                zero = jnp.zeros((ATTN_BLOCK, LANES), BF16)
                chains = [(off, b, e, mh) for off, b in group for e, mh in enumerate(heads)]
                qm = [jnp.where(mh, bq[pl.ds(off, ATTN_BLOCK), :], zero) for off, _, _, mh in chains]
                dom = [jnp.where(mh, bdo[pl.ds(off, ATTN_BLOCK), :], zero) for off, _, _, mh in chains]
                s = [_nt(x, bk[pl.ds(off, 2 * ATTN_BLOCK), :]) for x, (off, _, _, _) in zip(qm, chains)]
                dp = [_nt(x, bv[pl.ds(off, 2 * ATTN_BLOCK), :]) for x, (off, _, _, _) in zip(dom, chains)]
                p, ds = [], []
                for (off, b, e, _), sc, dpc in zip(chains, s, dp):
                    valid, _ = _attn_masks(jnp.bitwise_and(b, nb - 1) == 0)
                    blk = pl.ds(off, ATTN_BLOCK)
                    pc = jnp.where(valid, jnp.exp(sc - _head_value(blse[blk, :], lane, e)), 0.0)
                    ds.append((pc * (dpc - _head_value(bdl[blk, :], lane, e))).astype(BF16))
                    p.append(pc.astype(BF16))
                dq = [_nn(x, bk[pl.ds(off, 2 * ATTN_BLOCK), :]) for x, (off, _, _, _) in zip(ds, chains)]
                dk = [_tn(x, y) for x, y in zip(ds, qm)]
                dv = [_tn(x, y) for x, y in zip(p, dom)]
                nh = len(heads)
                for u, (off, _) in enumerate(group):
                    dq2 = jnp.zeros((ATTN_BLOCK, LANES), F32)
                    for mh, x in zip(heads, dq[nh * u:nh * (u + 1)]):
                        dq2 = jnp.where(mh, x, dq2)
                    bdq[pl.ds(off, ATTN_BLOCK), :] = dq2 * ATTN_SCALE
                    for acc, grads in ((bdk, dk), (bdv, dv)):
                        win_grad = sum(grads[nh * u + 1:nh * (u + 1)], grads[nh * u])
                        acc[pl.ds(off, ATTN_BLOCK), :] += win_grad[:ATTN_BLOCK]
                        acc[pl.ds(off + ATTN_BLOCK, ATTN_BLOCK), :] = win_grad[ATTN_BLOCK:]

            _for_blocks(S, ATTN_UNROLL_BWD, blocks)

            outs = ((dqs, bdq, 0), (dks, bdk, ATTN_BLOCK), (dvs, bdv, ATTN_BLOCK))
            if d == 16:
                for acc, grad, pad in outs:
                    _order16_to_4(grad, pad, tmp)

                    def add(rows, set_rows, off, acc=acc):
                        set_rows(acc, rows(acc) + tmp[pl.ds(off, ATTN_BLOCK), :])

                    _for_residue_blocks(S, 4, add)
            else:
                def scatter(rows, set_rows, off, d=d):
                    for acc, grad, pad in outs:
                        part = grad[pl.ds(pad + off, ATTN_BLOCK), :]
                        set_rows(acc, part if d == DILATIONS[0] else rows(acc) + part)

                _for_residue_blocks(S, d, scatter)
        o_ref[:, 0:LANES] = dqs[...].astype(BF16)
        o_ref[:, LANES:2 * LANES] = dks[...].astype(BF16)
        o_ref[:, 2 * LANES:3 * LANES] = dvs[...].astype(BF16)

    slab = pl.BlockSpec((S, LANES), lambda p: (0, p))
    pair = pl.BlockSpec((S, PAIR_W), lambda p: (0, p))
    f32_slab, bf16_slab = pltpu.VMEM((S, LANES), F32), pltpu.VMEM((S, LANES), BF16)
    f32_window, bf16_window = pltpu.VMEM((S + ATTN_BLOCK, LANES), F32), pltpu.VMEM((S + ATTN_BLOCK, LANES), BF16)
    return pl.pallas_call(
        body, name=name, grid=(ATTN_W // LANES,), in_specs=[pair, slab, slab, slab], out_specs=pair,
        out_shape=jax.ShapeDtypeStruct(proj_a.shape, BF16),
        scratch_shapes=[f32_slab] * 6 + [bf16_slab, bf16_window, bf16_window, bf16_slab, f32_slab, f32_slab,
                                         f32_slab, f32_window, f32_window, f32_slab],
        compiler_params=_cparams("parallel"),
    )(proj_a, do, lse, delta)


HG_T = 2 * HGRN_CHUNK
HG_GROUPS = 2
HG_STEP = HG_GROUPS * HG_T


def _hgrn_consts():
    row = lax.broadcasted_iota(jnp.int32, (HG_T, HG_T), 0)
    col = lax.broadcasted_iota(jnp.int32, (HG_T, HG_T), 1)
    same = (row >= HGRN_CHUNK) == (col >= HGRN_CHUNK)
    return row, same & (col <= row), same & (col >= row)


def _lower_bound(logits_ref):
    l0, l1 = logits_ref[0:1, :], logits_ref[1:2, :]
    mx = jnp.maximum(l0, l1)
    e0, e1 = jnp.exp(l0 - mx), jnp.exp(l1 - mx)
    return e0 / (e0 + e1)


def _hgrn_chains():
    chains = [(g, h) for g in range(HG_GROUPS) for h in range(HGRN_HEADS)]
    rows = [pl.ds(HG_T * g, HG_T) for g, _ in chains]
    lanes = [slice(HGRN_DIM * h, HGRN_DIM * (h + 1)) for _, h in chains]
    return chains, rows, lanes


def _hgrn_gates(qs, fs, lbs, row, causal):
    C = HGRN_CHUNK
    tri = jnp.where(causal, 1.0, 0.0).astype(F32)
    sgs = [_sigmoid(f) for f in fs]
    forgets = [lb + (1.0 - lb) * sg for lb, sg in zip(lbs, sgs)]
    logfs = [jnp.log(forget) for forget in forgets]
    bs = [_nn(tri, logf, HIGHEST) for logf in logfs]
    out = []
    for q, sg, forget, logf, b in zip(qs, sgs, forgets, logfs, bs):
        key = 1.0 - forget
        bend0 = jnp.sum(logf[:C], axis=0, keepdims=True)
        bend1 = jnp.sum(logf[C:], axis=0, keepdims=True)
        bend = jnp.where(row < C, bend0, bend1)
        eb, emb, eend = jnp.exp(b), jnp.exp(-b), jnp.exp(bend - b)
        sq = _sigmoid(q)
        out.append(dict(sg=sg, forget=forget, key=key, bend0=bend0, bend1=bend1, eb=eb, emb=emb, eend=eend, sq=sq,
                        qd=q * sq * eb, ki=key * emb, ke=key * eend))
    return out


def _hgrn_fwd(proj, proj_f, logits, *, name, after=None):
    S = proj.shape[0]
    W, C = HGRN_W, HGRN_CHUNK

    def body(q_ref, f_ref, i_ref, lg_ref, rec_ref, st_ref, s_ref):
        @pl.when(pl.program_id(0) == 0)
        def _():
            s_ref[...] = jnp.zeros_like(s_ref)

        row, causal, _ = _hgrn_consts()
        lb_all = _lower_bound(lg_ref)
        chains, rows, lanes = _hgrn_chains()
        n = range(len(chains))
        gts = _hgrn_gates([q_ref[rows[c], lanes[c]].astype(F32) for c in n], [f_ref[rows[c], lanes[c]] for c in n],
                          [lb_all[:, lanes[c]] for c in n], row, causal)
        qd, ki, ke = ([gt[k].astype(BF16) for gt in gts] for k in ("qd", "ki", "ke"))
        iv = [i_ref[rows[c], lanes[c]].astype(BF16) for c in n]
        a = [_nt(qd[c], ki[c]) for c in n]
        u0 = [_tn(iv[c][:C], ke[c][:C]) for c in n]
        u1 = [_tn(iv[c][C:], ke[c][C:]) for c in n]
        state = [s_ref[h] for h in range(HGRN_HEADS)]
        s0, s1 = [], []
        for c, (g, h) in enumerate(chains):
            s0.append(state[h])
            s1.append(jnp.exp(gts[c]["bend0"]) * s0[c] + u0[c])
            state[h] = jnp.exp(gts[c]["bend1"]) * s1[c] + u1[c]
        o0 = [_nt(qd[c][:C], s0[c].astype(BF16)) for c in n]
        o1 = [_nt(qd[c][C:], s1[c].astype(BF16)) for c in n]
        o = [_nn(jnp.where(causal, a[c], 0.0).astype(BF16), iv[c]) for c in n]
        for c, (g, h) in enumerate(chains):
            st_ref[2 * g, h] = s0[c]
            st_ref[2 * g + 1, h] = s1[c]
            rec_ref[rows[c], lanes[c]] = o[c] + jnp.concatenate([o0[c], o1[c]], axis=0)
        for h in range(HGRN_HEADS):
            s_ref[h] = state[h]

    blk = lambda j: pl.BlockSpec((HG_STEP, W), lambda t: (t, j))
    extra, extra_specs, adapt = _behind(after, 4)
    return pl.pallas_call(
        adapt(body), name=name, grid=(S // HG_STEP,),
        in_specs=[blk(0), blk(0), blk(2), pl.BlockSpec((2, W), lambda t: (0, 0))] + extra_specs,
        out_specs=[blk(0), pl.BlockSpec((2 * HG_GROUPS, HGRN_HEADS, HGRN_DIM, HGRN_DIM), lambda t: (t, 0, 0, 0))],
        out_shape=[jax.ShapeDtypeStruct((S, W), F32),
                   jax.ShapeDtypeStruct((S // C, HGRN_HEADS, HGRN_DIM, HGRN_DIM), F32)],
        scratch_shapes=[pltpu.VMEM((HGRN_HEADS, HGRN_DIM, HGRN_DIM), F32)],
        compiler_params=_cparams("arbitrary"),
    )(proj, proj_f, proj, logits, *extra)


def _hgrn_bwd(proj, proj_f, logits, states, drec, *, name, after=None):
    S = proj.shape[0]
    W, C = HGRN_W, HGRN_CHUNK
    nt = S // HG_STEP

    def body(q_ref, f_ref, i_ref, lg_ref, st_ref, do_ref, dp_ref, dlg_ref, ds_ref, dlb_ref):
        t = pl.program_id(0)

        @pl.when(t == 0)
        def _():
            ds_ref[...] = jnp.zeros_like(ds_ref)
            dlb_ref[...] = jnp.zeros_like(dlb_ref)

        row, causal, anti = _hgrn_consts()
        lb_all = _lower_bound(lg_ref)
        chains, rows, lanes = _hgrn_chains()
        n = range(len(chains))
        qs, lbs = [q_ref[rows[c], lanes[c]].astype(F32) for c in n], [lb_all[:, lanes[c]] for c in n]
        gts = _hgrn_gates(qs, [f_ref[rows[c], lanes[c]] for c in n], lbs, row, causal)
        qd, ki, ke = ([gt[k] for gt in gts] for k in ("qd", "ki", "ke"))
        qdb, kib, keb = ([x.astype(BF16) for x in xs] for xs in (qd, ki, ke))
        iv = [i_ref[rows[c], lanes[c]].astype(BF16) for c in n]
        dob = [do_ref[rows[c], lanes[c]].astype(BF16) for c in n]
        s0 = [st_ref[2 * g, h] for g, h in chains]
        s1 = [st_ref[2 * g + 1, h] for g, h in chains]
        dec0, dec1 = [jnp.exp(gt["bend0"]) for gt in gts], [jnp.exp(gt["bend1"]) for gt in gts]
        a = [_nt(qdb[c], kib[c]) for c in n]
        da = [_nt(dob[c], iv[c]) for c in n]
        dqd1 = [_nn(dob[c][C:], s1[c].astype(BF16)) for c in n]
        dqd0 = [_nn(dob[c][:C], s0[c].astype(BF16)) for c in n]
        t1 = [_tn(dob[c][C:], qdb[c][C:]) for c in n]
        t0 = [_tn(dob[c][:C], qdb[c][:C]) for c in n]
        carry = [ds_ref[h] for h in range(HGRN_HEADS)]
        ds1, ds0 = [None] * len(chains), [None] * len(chains)
        for c in reversed(n):
            h = chains[c][1]
            ds1[c] = carry[h]
            ds0[c] = dec1[c] * ds1[c] + t1[c]
            carry[h] = dec0[c] * ds0[c] + t0[c]
        for h in range(HGRN_HEADS):
            ds_ref[h] = carry[h]
        ds1b, ds0b = [x.astype(BF16) for x in ds1], [x.astype(BF16) for x in ds0]
        a = [jnp.where(causal, x, 0.0).astype(BF16) for x in a]
        da = [jnp.where(causal, x, 0.0).astype(BF16) for x in da]
        di1 = [_nt(keb[c][C:], ds1b[c]) for c in n]
        dke1 = [_nn(iv[c][C:], ds1b[c]) for c in n]
        di0 = [_nt(keb[c][:C], ds0b[c]) for c in n]
        dke0 = [_nn(iv[c][:C], ds0b[c]) for c in n]
        dqd_a = [_nn(da[c], kib[c]) for c in n]
        dki = [_tn(da[c], qdb[c]) for c in n]
        di_a = [_tn(a[c], dob[c]) for c in n]
        dqd, dke, db = [], [], []
        for c in n:
            ddec1 = jnp.sum(ds1[c] * s1[c], axis=0, keepdims=True)
            ddec0 = jnp.sum(ds0[c] * s0[c], axis=0, keepdims=True)
            dqd.append(dqd_a[c] + jnp.concatenate([dqd0[c], dqd1[c]], axis=0))
            h = chains[c][1]
            dp_ref[rows[c], 2 * W + HGRN_DIM * h:2 * W + HGRN_DIM * (h + 1)] = (
                di_a[c] + jnp.concatenate([di0[c], di1[c]], axis=0)).astype(BF16)
            dke.append(jnp.concatenate([dke0[c], dke1[c]], axis=0))
            gke = dke[c] * ke[c]
            dbend0 = jnp.sum(gke[:C], axis=0, keepdims=True) + ddec0 * dec0[c]
            dbend1 = jnp.sum(gke[C:], axis=0, keepdims=True) + ddec1 * dec1[c]
            dbc = dqd[c] * qd[c] - dki[c] * ki[c] - gke
            db.append(dbc + jnp.where(row == C - 1, dbend0, 0.0) + jnp.where(row == HG_T - 1, dbend1, 0.0))
        tri = jnp.where(anti, 1.0, 0.0).astype(F32)
        dlogf = [_nn(tri, db[c], HIGHEST) for c in n]
        for c in n:
            gt, lb, q, h = gts[c], lbs[c], qs[c], chains[c][1]
            dforget = dlogf[c] / gt["forget"] - (dki[c] * gt["emb"] + dke[c] * gt["eend"])
            sg, sq = gt["sg"], gt["sq"]
            dp_ref[rows[c], W + HGRN_DIM * h:W + HGRN_DIM * (h + 1)] = (
                dforget * (1.0 - lb) * sg * (1.0 - sg)).astype(BF16)
            dlb_ref[:, lanes[c]] += jnp.sum(dforget * (1.0 - sg), axis=0, keepdims=True)
            dp_ref[rows[c], lanes[c]] = (dqd[c] * gt["eb"] * sq * (1.0 + q * (1.0 - sq))).astype(BF16)

        @pl.when(t == nt - 1)
        def _():
            dl0 = dlb_ref[...] * lb_all * (1.0 - lb_all)
            dlg_ref[0:1, :] = dl0
            dlg_ref[1:2, :] = -dl0

    blk = lambda j: pl.BlockSpec((HG_STEP, W), lambda t: (nt - 1 - t, j))
    full = pl.BlockSpec((2, W), lambda t: (0, 0))
    extra, extra_specs, adapt = _behind(after, 6)
    return pl.pallas_call(
        adapt(body), name=name, grid=(nt,),
        in_specs=[blk(0), blk(0), blk(2), full,
                  pl.BlockSpec((2 * HG_GROUPS, HGRN_HEADS, HGRN_DIM, HGRN_DIM), lambda t: (nt - 1 - t, 0, 0, 0)), blk(0)]
        + extra_specs,
        out_specs=[pl.BlockSpec((HG_STEP, 3 * W), lambda t: (nt - 1 - t, 0)), full],
        out_shape=[jax.ShapeDtypeStruct((S, 3 * W), BF16), jax.ShapeDtypeStruct((2, W), F32)],
        scratch_shapes=[pltpu.VMEM((HGRN_HEADS, HGRN_DIM, HGRN_DIM), F32), pltpu.VMEM((1, W), F32)],
        compiler_params=_cparams("arbitrary"),
    )(proj, proj_f, proj, logits, states, drec, *extra)


def _out_proj(attn, rec, proj_h, x, g_attn, g_hgrn, g_norm2, w_out, *, name, tm=512):
    S, D = x.shape
    AW, W = ATTN_W, HGRN_W

    def body(a_ref, r_ref, hg_ref, x_ref, ga_ref, gh_ref, g2_ref, w_ref, h_ref, u_ref, m_ref):
        av = a_ref[...]
        m_ref[:, :AW] = (av * _rstd(av) * ga_ref[...]).astype(BF16)
        for h in range(HGRN_HEADS):
            sl = slice(HGRN_DIM * h, HGRN_DIM * (h + 1))
            rv, hg = r_ref[:, sl], hg_ref[:, sl].astype(F32)
            m_ref[:, AW + HGRN_DIM * h:AW + HGRN_DIM * (h + 1)] = (
                (rv * _rstd(rv) * gh_ref[:, sl]) * (hg * _sigmoid(hg))).astype(BF16)
        h1 = x_ref[...] + _nn(m_ref[...], w_ref[...])
        h_ref[...] = h1
        u_ref[...] = (h1 * _rstd(h1) * g2_ref[...]).astype(BF16)

    row = lambda w, j=0: pl.BlockSpec((tm, w), lambda i: (i, j))
    vec = lambda w: pl.BlockSpec((1, w), lambda i: (0, 0))
    return pl.pallas_call(
        body, name=name, grid=(S // tm,),
        in_specs=[row(AW), row(W), row(W, 3), row(D), vec(AW), vec(W), vec(D), _resident(w_out.shape)],
        out_specs=[row(D), row(D), row(AW + W)],
        out_shape=[jax.ShapeDtypeStruct((S, D), F32), jax.ShapeDtypeStruct((S, D), BF16),
                   jax.ShapeDtypeStruct((S, AW + W), BF16)],
        compiler_params=_cparams("parallel"),
    )(attn, rec, proj_h, x, g_attn, g_hgrn, g_norm2, w_out)


def _dmix_post_bwd(dh1b, w_out, attn, rec, proj_h, g_attn, g_hgrn, *, name, tm=512, after=None):
    S, D = dh1b.shape
    AW, W = ATTN_W, HGRN_W

    def body(dh_ref, w_ref, a_ref, r_ref, hg_ref, ga_ref, gh_ref, do_ref, dl_ref, dr_ref, dhg_ref, dga_ref, dgh_ref):
        first = pl.program_id(0) == 0
        dmix = _nt(dh_ref[...], w_ref[...])
        av = a_ref[...]
        dov, dga = _norm_bwd(av, ga_ref[...], dmix[:, :AW])
        do_ref[...] = dov
        shift = HEAD_DIM.bit_length() - 1
        hi = lax.shift_right_logical(lax.broadcasted_iota(jnp.int32, (AW, AW), 0), shift)
        hj = lax.shift_right_logical(lax.broadcasted_iota(jnp.int32, (AW, AW), 1), shift)
        prod = dov * av
        hi_part = prod.astype(BF16)
        lo_part = (prod - hi_part.astype(F32)).astype(BF16)
        same_head = jnp.where(hi == hj, 1.0, 0.0).astype(BF16)
        dl_ref[...] = _nn(hi_part, same_head) + _nn(lo_part, same_head)
        _accumulate(dga_ref, jnp.sum(dga, axis=0, keepdims=True), first)

        @pl.when(first)
        def _():
            dgh_ref[...] = jnp.zeros_like(dgh_ref)

        for h in range(HGRN_HEADS):
            sl = slice(HGRN_DIM * h, HGRN_DIM * (h + 1))
            rv, hg, gv = r_ref[:, sl], hg_ref[:, sl].astype(F32), gh_ref[:, sl]
            dout = dmix[:, AW + HGRN_DIM * h:AW + HGRN_DIM * (h + 1)]
            sg = _sigmoid(hg)
            drv, dgh = _norm_bwd(rv, gv, dout * (hg * sg))
            dr_ref[:, sl] = drv
            dgh_ref[:, sl] += jnp.sum(dgh, axis=0, keepdims=True)
            dhg_ref[:, sl] = (dout * (rv * _rstd(rv) * gv) * (sg * (1.0 + hg * (1.0 - sg)))).astype(BF16)

    row = lambda w, j=0: pl.BlockSpec((tm, w), lambda i: (i, j))
    vec = lambda w: pl.BlockSpec((1, w), lambda i: (0, 0))
    extra, extra_specs, adapt = _behind(after, 7)
    return pl.pallas_call(
        adapt(body), name=name, grid=(S // tm,),
        in_specs=[row(D), _resident(w_out.shape), row(AW), row(W), row(W, 3), vec(AW), vec(W)] + extra_specs,
        out_specs=[row(AW), row(AW), row(W), row(W), vec(AW), vec(W)],
        out_shape=[jax.ShapeDtypeStruct((S, AW), F32), jax.ShapeDtypeStruct((S, AW), F32),
                   jax.ShapeDtypeStruct((S, W), F32), jax.ShapeDtypeStruct((S, W), BF16),
                   jax.ShapeDtypeStruct((1, AW), F32), jax.ShapeDtypeStruct((1, W), F32)],
        compiler_params=_cparams("arbitrary"),
    )(dh1b, w_out, attn, rec, proj_h, g_attn, g_hgrn, *extra)


def _conv_act(g, g1, g2, w_ref, b_ref):
    c = b_ref[...] + w_ref[0:1, :] * g2 + w_ref[1:2, :] * g1 + w_ref[2:3, :] * g
    return c, 0.5 * (1.0 + lax.erf(c * (2.0 ** -0.5)))


def _shift_down(g, halo, row):
    g1 = jnp.where(row == 0, halo[7:8], pltpu.roll(g, 1, 0))
    g2 = jnp.where(row == 0, halo[6:7], jnp.where(row == 1, halo[7:8], pltpu.roll(g, 2, 0)))
    return g1, g2


def _shift_up(x, halo, row):
    n = x.shape[0]
    x1 = jnp.where(row == n - 1, halo[0:1], pltpu.roll(x, n - 1, 0))
    x2 = jnp.where(row == n - 2, halo[0:1], jnp.where(row == n - 1, halo[1:2], pltpu.roll(x, n - 2, 0)))
    return x1, x2


def _up_glu(u, wt_up, conv_w, conv_b, *, name, tm=1024, tn=1408):
    S, D = u.shape
    F = wt_up.shape[0] // 2
    tn = _tile(F, tn)
    nf = F // tn

    def body(u_ref, wg_ref, wv_ref, cw_ref, cb_ref, g_ref, ge_ref, t_ref, a_ref, halo_ref):
        i, j = pl.program_id(0), pl.program_id(1)

        @pl.when(i == 0)
        def _():
            halo_ref[j] = jnp.zeros((SUBLANES, tn), F32)

        uv = u_ref[...]
        g, v = _nt(uv, wg_ref[...]), _nt(uv, wv_ref[...])
        row = lax.broadcasted_iota(jnp.int32, (tm, tn), 0)
        g1, g2 = _shift_down(g, halo_ref[j], row)
        c, cdf = _conv_act(g, g1, g2, cw_ref, cb_ref)
        gelu = c * cdf
        pdf = jnp.exp(-0.5 * c * c) * (1.0 / (2.0 * jnp.pi) ** 0.5)
        a_ref[...] = (gelu * v).astype(BF16)
        g_ref[...] = g.astype(BF16)
        ge_ref[...] = gelu.astype(BF16)
        t_ref[...] = (v * (cdf + c * pdf)).astype(BF16)
        halo_ref[j] = g[tm - SUBLANES:, :]

    col = pl.BlockSpec((tm, tn), lambda i, j: (i, j))
    out = jax.ShapeDtypeStruct((S, F), BF16)
    return pl.pallas_call(
        body, name=name, grid=(S // tm, nf),
        in_specs=[pl.BlockSpec((tm, D), lambda i, j: (i, 0)), pl.BlockSpec((tn, D), lambda i, j: (j, 0)),
                  pl.BlockSpec((tn, D), lambda i, j: (j + nf, 0)), pl.BlockSpec((3, tn), lambda i, j: (0, j)),
                  pl.BlockSpec((1, tn), lambda i, j: (0, j))],
        out_specs=[col, col, col, col], out_shape=[out, out, out, out],
        scratch_shapes=[pltpu.VMEM((nf, SUBLANES, tn), F32)], compiler_params=_cparams("arbitrary", "arbitrary"),
    )(u, wt_up, wt_up, conv_w, conv_b)


def _dact_glu_bwd(dh2b, w_down, gate, gelu, vslope, conv_w, *, name, tm=1024, tn=1408):
    S, D = dh2b.shape
    F = gate.shape[1]
    tn = _tile(F, tn)
    nf, ni = F // tn, S // tm

    def body(dh_ref, wd_ref, g_ref, ge_ref, t_ref, cw_ref, dg_ref, dv_ref, dcw_ref, dcb_ref, halo_ref, acc_ref):
        i, j = pl.program_id(0), pl.program_id(1)

        @pl.when(i == 0)
        def _():
            halo_ref[j] = jnp.zeros((SUBLANES, tn), F32)
            acc_ref[j] = jnp.zeros((SUBLANES, tn), F32)

        g = g_ref[...].astype(F32)
        row = lax.broadcasted_iota(jnp.int32, (tm, tn), 0)
        da = _nt(dh_ref[...], wd_ref[...])
        dv_ref[...] = (da * ge_ref[...].astype(F32)).astype(BF16)
        dc = da * t_ref[...].astype(F32)
        d1, d2 = _shift_up(dc, halo_ref[j], row)
        dg_ref[...] = (cw_ref[2:3, :] * dc + cw_ref[1:2, :] * d1 + cw_ref[0:1, :] * d2).astype(BF16)
        halo_ref[j] = dc[:SUBLANES, :]
        for k, t in enumerate((d2 * g, d1 * g, dc * g, dc)):
            acc_ref[j, k:k + 1, :] += jnp.sum(t, axis=0, keepdims=True)

        @pl.when((i == ni - 1) & (j == nf - 1))
        def _():
            for jj in range(nf):
                dcw_ref[:, jj * tn:(jj + 1) * tn] = acc_ref[jj, 0:3, :]
                dcb_ref[:, jj * tn:(jj + 1) * tn] = acc_ref[jj, 3:4, :]

    tile = pl.BlockSpec((tm, tn), lambda i, j: (ni - 1 - i, j))
    return pl.pallas_call(
        body, name=name, grid=(ni, nf),
        in_specs=[pl.BlockSpec((tm, D), lambda i, j: (ni - 1 - i, 0)), pl.BlockSpec((tn, D), lambda i, j: (j, 0)),
                  tile, tile, tile, pl.BlockSpec((3, tn), lambda i, j: (0, j))],
        out_specs=[tile, tile, pl.BlockSpec((3, F), lambda i, j: (0, 0)), pl.BlockSpec((1, F), lambda i, j: (0, 0))],
        out_shape=[jax.ShapeDtypeStruct((S, F), BF16), jax.ShapeDtypeStruct((S, F), BF16),
                   jax.ShapeDtypeStruct((3, F), F32), jax.ShapeDtypeStruct((1, F), F32)],
        scratch_shapes=[pltpu.VMEM((nf, SUBLANES, tn), F32), pltpu.VMEM((nf, SUBLANES, tn), F32)],
        compiler_params=_cparams("arbitrary", "arbitrary"),
    )(dh2b, w_down, gate, gelu, vslope, conv_w)


def _down_loss(act, w_down, h1, g, target, *, name, tm=512):
    S, F = act.shape
    D = h1.shape[1]

    def body(a_ref, w_ref, h_ref, g_ref, t_ref, dh_ref, dhb_ref, dg_ref, loss_ref):
        first = pl.program_id(0) == 0
        h2 = h_ref[...] + _nn(a_ref[...], w_ref[...])
        gv = g_ref[...]
        r = _rstd(h2)
        xh = h2 * r
        err = xh * gv - t_ref[...]
        part_loss = 0.5 * jnp.sum(jnp.mean(err * err, axis=-1, keepdims=True), axis=0, keepdims=True)
        dy = err * (1.0 / D)
        dxh = dy * gv
        dh = r * (dxh - xh * jnp.mean(dxh * xh, axis=-1, keepdims=True))
        dh_ref[...] = dh
        dhb_ref[...] = dh.astype(BF16)
        _accumulate(dg_ref, jnp.sum(dy * xh, axis=0, keepdims=True), first)
        _accumulate(loss_ref, jnp.broadcast_to(part_loss, (1, LANES)), first)

    row = lambda w: pl.BlockSpec((tm, w), lambda i: (i, 0))
    vec = lambda w: pl.BlockSpec((1, w), lambda i: (0, 0))
    return pl.pallas_call(
        body, name=name, grid=(S // tm,), in_specs=[row(F), _resident(w_down.shape), row(D), vec(D), row(D)],
        out_specs=[row(D), row(D), vec(D), vec(LANES)],
        out_shape=[jax.ShapeDtypeStruct((S, D), F32), jax.ShapeDtypeStruct((S, D), BF16),
                   jax.ShapeDtypeStruct((1, D), F32), jax.ShapeDtypeStruct((1, LANES), F32)],
        compiler_params=_cparams("arbitrary"),
    )(act, w_down, h1, g, target)


def _grad_norm_input(pieces, ws, x, g, add, *, name, tm=512, after=None):
    S, D = x.shape
    widths = [p.shape[1] for p in pieces]
    n, nw = len(pieces), len(ws)
    where, wi, off = [], 0, ws[0][1]
    for wd in widths:
        if off == ws[wi][0].shape[0]:
            wi, off = wi + 1, ws[wi + 1][1]
        where.append((wi, off))
        off += wd
    ws = [w for w, _ in ws]

    def body(*refs):
        p_refs, w_refs = refs[:n], refs[n:n + nw]
        x_ref, g_ref, add_ref, dx_ref, dxb_ref, dg_ref = refs[n + nw:]
        halves = _row_halves(tm)
        du = []
        for rows in halves:
            terms = [_nn(p_refs[k][rows, :], w_refs[wi][off:off + widths[k], :]) for k, (wi, off) in enumerate(where)]
            du.append(sum(terms[1:], terms[0]))
        dg_sum = None
        for rows, duh in zip(halves, du):
            dx, dg = _norm_bwd(x_ref[rows, :], g_ref[...], duh)
            dx = add_ref[rows, :] + dx
            dx_ref[rows, :] = dx
            dxb_ref[rows, :] = dx.astype(BF16)
            part = jnp.sum(dg, axis=0, keepdims=True)
            dg_sum = part if dg_sum is None else dg_sum + part
        _accumulate(dg_ref, dg_sum, pl.program_id(0) == 0)

    row = lambda w_: pl.BlockSpec((tm, w_), lambda i: (i, 0))
    vec = pl.BlockSpec((1, D), lambda i: (0, 0))
    extra, extra_specs, adapt = _behind(after, n + nw + 3)
    return pl.pallas_call(
        adapt(body), name=name, grid=(S // tm,),
        in_specs=[row(wd) for wd in widths] + [_resident(w.shape) for w in ws] + [row(D), vec, row(D)] + extra_specs,
        out_specs=[row(D), row(D), vec],
        out_shape=[jax.ShapeDtypeStruct((S, D), F32), jax.ShapeDtypeStruct((S, D), BF16),
                   jax.ShapeDtypeStruct((1, D), F32)],
        compiler_params=_cparams("arbitrary"),
    )(*pieces, *ws, x, g, add, *extra)


def _rows(a):
    return a.reshape(-1, a.shape[-1])


def _row_tile(rows, cols, itemsize=4, budget=1 << 20):
    t = rows
    while t % 32 == 0 and t * cols * itemsize > budget:
        t //= 2
    return t


def _sum_cast(arrs, out_dtype, *, name):
    shape = arrs[0].shape
    flat = [_rows(a) for a in arrs]
    R, C = flat[0].shape
    tr = _row_tile(R, C)

    def body(*refs):
        acc = refs[0][...].astype(F32)
        for r in refs[1:-1]:
            acc = acc + r[...].astype(F32)
        refs[-1][...] = acc.astype(out_dtype)

    spec = pl.BlockSpec((tr, C), lambda i: (i, 0))
    return pl.pallas_call(
        body, name=name, grid=(R // tr,), in_specs=[spec] * len(flat), out_specs=spec,
        out_shape=jax.ShapeDtypeStruct((R, C), out_dtype), compiler_params=_cparams("parallel"),
    )(*flat).reshape(shape)


def _adamw(parts, w, m, v, *, name):
    shape = w.shape
    w2, m2, v2 = _rows(w), _rows(m), _rows(v)
    R, C = w2.shape
    parts = [p.reshape(-1, R, C) for p in parts]
    tr = _row_tile(R, C)
    np_ = len(parts)
    c1, c2 = 1.0 - ADAM_B1 ** ADAM_STEP, 1.0 - ADAM_B2 ** ADAM_STEP

    def body(*refs):
        terms = [(r, k) for r in refs[:np_] for k in range(r.shape[0])]
        g = terms[0][0][terms[0][1]].astype(F32)
        for r, k in terms[1:]:
            g = g + r[k].astype(F32)
        w_ref, m_ref, v_ref, g_out, d_out, m_out, v_out = refs[np_:]
        mn = ADAM_B1 * m_ref[...] + (1.0 - ADAM_B1) * g
        vn = ADAM_B2 * v_ref[...] + (1.0 - ADAM_B2) * (g * g)
        g_out[...] = g
        d_out[...] = -ADAM_LR * ((mn / c1) / (jnp.sqrt(vn / c2) + ADAM_EPS) + ADAM_WD * w_ref[...])
        m_out[...] = mn
        v_out[...] = vn

    spec = pl.BlockSpec((tr, C), lambda i: (i, 0))
    out = jax.ShapeDtypeStruct((R, C), F32)
    stacks = [pl.BlockSpec((p.shape[0], tr, C), lambda i: (0, i, 0)) for p in parts]
    res = pl.pallas_call(
        body, name=name, grid=(R // tr,), in_specs=stacks + [spec] * 3, out_specs=[spec] * 4,
        out_shape=[out] * 4, compiler_params=_cparams("parallel"),
    )(*parts, w2, m2, v2)
    return [r.reshape(shape) for r in res]


def _adamw_packed(stack, widths, params, *, name):
    c1, c2 = 1.0 - ADAM_B1 ** ADAM_STEP, 1.0 - ADAM_B2 ** ADAM_STEP
    k = stack.shape[0]
    flat = [None if p is None else [_rows(a) for a in p] for p in params]
    n_in = sum(3 for p in flat if p is not None)

    def body(*refs):
        s_ref, ins, outs = refs[0], list(refs[1:1 + n_in]), list(refs[1 + n_in:])
        off = 0
        for width, p in zip(widths, flat):
            rows = 1 if p is None else p[0].shape[0]
            cols = width // rows
            w_ref, m_ref, v_ref = (None, None, None) if p is None else (ins.pop(0), ins.pop(0), ins.pop(0))
            o_refs = [outs.pop(0) for _ in range(1 if p is None else 4)]
            for r in range(rows):
                seg = slice(off + r * cols, off + (r + 1) * cols)
                g = s_ref[0, :, seg]
                for j in range(1, k):
                    g = g + s_ref[j, :, seg]
                o_refs[0][r:r + 1, :] = g
                if p is not None:
                    row = slice(r, r + 1)
                    mn = ADAM_B1 * m_ref[row, :] + (1.0 - ADAM_B1) * g
                    vn = ADAM_B2 * v_ref[row, :] + (1.0 - ADAM_B2) * (g * g)
                    o_refs[1][row, :] = -ADAM_LR * ((mn / c1) / (jnp.sqrt(vn / c2) + ADAM_EPS) + ADAM_WD * w_ref[row, :])
                    o_refs[2][row, :] = mn
                    o_refs[3][row, :] = vn
            off += width

    operands, out_shape = [stack], []
    for width, p in zip(widths, flat):
        if p is None:
            out_shape.append(jax.ShapeDtypeStruct((1, width), F32))
        else:
            operands += p
            out_shape += [jax.ShapeDtypeStruct(p[0].shape, F32)] * 4
    res = list(pl.pallas_call(body, name=name, out_shape=out_shape)(*operands))
    out = []
    for p, orig in zip(flat, params):
        n = 1 if p is None else 4
        out.append([r if orig is None else r.reshape(orig[0].shape) for r in res[:n]])
        res = res[n:]
    return out


def _coords():
    return lax.axis_index("x"), lax.axis_index("y"), lax.axis_index("c")


def _all_gather(shards, *, name):
    n = len(shards)

    def body(*refs):
        x_refs, out_refs = refs[:n], refs[n:2 * n]
        send_sems, recv_sems, local_sems = refs[2 * n:]
        x, y, c = _coords()
        me, sibling = (x, y, c), (x, y, 1 - c)
        chips = [(1 - x, y), (x, 1 - y), (1 - x, 1 - y)]

        def slot(a, dev):
            return out_refs[a].at[4 * dev[0] + 2 * dev[1] + dev[2]]

        def copy(a, k, block, to, src=None):
            return pltpu.make_async_remote_copy(
                src_ref=slot(a, block) if src is None else src, dst_ref=slot(a, block),
                send_sem=send_sems.at[7 * a + k], recv_sem=recv_sems.at[7 * a + k], device_id=to, device_id_type=MESH)

        mine = [pltpu.make_async_copy(x_refs[a], slot(a, me), local_sems.at[a]) for a in range(n)]
        for cp in mine:
            cp.start()
        first = []
        for a in range(n):
            first.append(copy(a, 0, me, sibling, src=x_refs[a]))
            first += [copy(a, 1 + j, me, (*chip, c), src=x_refs[a]) for j, chip in enumerate(chips)]
        for cp in first:
            cp.start()
        passed = []
        for j, chip in enumerate(chips):
            for a in range(n):
                copy(a, 1 + j, (*chip, c), me).wait_recv()
                fwd = copy(a, 4 + j, (*chip, c), sibling)
                fwd.start()
                passed.append(fwd)
        for a in range(n):
            copy(a, 0, sibling, me).wait_recv()
            for j, chip in enumerate(chips):
                copy(a, 4 + j, (*chip, 1 - c), me).wait_recv()
        for cp in first + passed:
            cp.wait_send()
        for cp in mine:
            cp.wait()

    return pl.pallas_call(
        body, name=name, in_specs=[HBM] * n, out_specs=[HBM] * n,
        out_shape=[jax.ShapeDtypeStruct((N_DEV, *s.shape), s.dtype) for s in shards],
        scratch_shapes=[pltpu.SemaphoreType.DMA((7 * n,)), pltpu.SemaphoreType.DMA((7 * n,)),
                        pltpu.SemaphoreType.DMA((n,))],
    )(*shards)


def _flip_y(x, y, c):
    return (x, 1 - y, c)


def _flip_x(x, y, c):
    return (1 - x, y, c)


def _flip_xy(x, y, c):
    return (1 - x, 1 - y, c)


SEM = pl.BlockSpec(memory_space=pltpu.SEMAPHORE)
SIDE_EFFECT = pltpu.SideEffectType.DATAFLOW_SIDE_EFFECTING


def _in_hbm(a):
    return pltpu.with_memory_space_constraint(a, pltpu.HBM)


def _copies_start(srcs, lands, plan, n_copies, *, name, after=None):
    ns, nl = len(srcs), len(lands)
    extra = [] if after is None else [after]

    def body(*refs):
        src_refs, land_refs = refs[:ns], refs[ns:ns + nl]
        send_sems, recv_sems = refs[ns + nl + len(extra):ns + nl + len(extra) + 2]
        token = refs[-1]
        for k, (src, dst, peer, _) in enumerate(plan(src_refs, land_refs, *_coords())):
            pltpu.make_async_remote_copy(src_ref=src, dst_ref=dst, send_sem=send_sems.at[k], recv_sem=recv_sems.at[k],
                                         device_id=peer, device_id_type=MESH).start()
        token[...] = jnp.zeros_like(token)

    bufs = [*srcs, *lands]
    res = pl.pallas_call(
        body, name=name, in_specs=[HBM] * (ns + nl) + [pl.BlockSpec(memory_space=pl.ANY)] * len(extra),
        out_specs=(SEM, SEM, *[HBM] * (ns + nl), pl.BlockSpec(memory_space=pltpu.VMEM)),
        out_shape=(pltpu.SemaphoreType.DMA((n_copies,)), pltpu.SemaphoreType.DMA((n_copies,)),
                   *[pltpu.HBM(b.shape, b.dtype) for b in bufs], jax.ShapeDtypeStruct((SUBLANES, LANES), F32)),
        input_output_aliases={i: 2 + i for i in range(ns + nl)},
        compiler_params=pltpu.CompilerParams(has_side_effects=SIDE_EFFECT),
    )(*[_in_hbm(b) for b in bufs], *extra)
    return res[0], res[1], list(res[2:2 + ns]), list(res[2 + ns:2 + ns + nl]), res[-1]


def _copies_wait(started, plan, after, *, name, with_srcs=False):
    send_sems, recv_sems, srcs, lands, _ = started
    ns, nl = len(srcs), len(lands)

    def body(*refs):
        src_refs, land_refs = refs[:ns], refs[ns:ns + nl]
        send_sems, recv_sems = refs[ns + nl:ns + nl + 2]
        for k, (src, dst, peer, here) in enumerate(plan(src_refs, land_refs, *_coords())):
            pltpu.make_async_remote_copy(src_ref=src, dst_ref=dst, send_sem=send_sems.at[k], recv_sem=recv_sems.at[k],
                                         device_id=peer, device_id_type=MESH).wait_send()
            pltpu.make_async_remote_copy(src_ref=src, dst_ref=here, send_sem=send_sems.at[k], recv_sem=recv_sems.at[k],
                                         device_id=peer, device_id_type=MESH).wait_recv()

    bufs = [*srcs, *lands]
    res = pl.pallas_call(
        body, name=name, in_specs=[HBM] * (ns + nl) + [SEM, SEM, pl.BlockSpec(memory_space=pl.ANY)],
        out_specs=[HBM] * (ns + nl), out_shape=[pltpu.HBM(b.shape, b.dtype) for b in bufs],
        input_output_aliases={i: i for i in range(ns + nl)},
        compiler_params=pltpu.CompilerParams(has_side_effects=SIDE_EFFECT),
    )(*bufs, send_sems, recv_sems, after)
    return (list(res[:ns]), list(res[ns:])) if with_srcs else list(res[ns:])


def _dev_index(dev):
    return 4 * dev[0] + 2 * dev[1] + dev[2]


def _ag_chips_plan(src_refs, land_refs, x, y, c):
    me = _dev_index((x, y, c))
    return [(src, land.at[me], peer, land.at[_dev_index(peer)])
            for src, land in zip(src_refs, land_refs) for peer in (_flip_y(x, y, c), _flip_x(x, y, c), _flip_xy(x, y, c))]


def _ag_sibling_plan(src_refs, land_refs, x, y, c):
    chips = [(x, y), (x, 1 - y), (1 - x, y), (1 - x, 1 - y)]
    return [(land.at[_dev_index((*chip, c))], land.at[_dev_index((*chip, c))], (x, y, 1 - c),
             land.at[_dev_index((*chip, 1 - c))]) for land in land_refs for chip in chips]


def _ag_direct_plan(src_refs, land_refs, x, y, c):
    me = _dev_index((x, y, c))
    plan = []
    for src, land in zip(src_refs, land_refs):
        for m in range(1, N_DEV):
            peer = (x + (m >> 2) * (1 - 2 * x), y + ((m >> 1) & 1) * (1 - 2 * y), c + (m & 1) * (1 - 2 * c))
            plan.append((src, land.at[me], peer, land.at[_dev_index(peer)]))
    return plan


def _rs_direct_plan(src_refs, land_refs, x, y, c):
    plan = []
    for src, land in zip(src_refs, land_refs):
        for m in range(1, N_DEV):
            peer = (x + (m >> 2) * (1 - 2 * x), y + ((m >> 1) & 1) * (1 - 2 * y), c + (m & 1) * (1 - 2 * c))
            plan.append((src.at[_dev_index(peer)], land.at[m - 1], peer, land.at[m - 1]))
    return plan


def _rs_start(grads, me, *, name, after=None):
    own = [lax.dynamic_index_in_dim(g, me, 0, keepdims=False) for g in grads]
    lands = [lax.empty((N_DEV - 1, *g.shape[1:]), g.dtype) for g in grads]
    return _copies_start(grads, lands, _rs_direct_plan, (N_DEV - 1) * len(grads), name=name, after=after), own


def _rs_finish(started, after, *, name):
    handle, own = started
    got = _copies_wait(handle, _rs_direct_plan, after, name=name)
    return [[o, land] for o, land in zip(own, got)]


def _gathered_cols(w8):
    return w8.transpose(1, 0, 2).reshape(w8.shape[1], -1)


def _pair_major(wt):
    return wt.reshape(3, ATTN_W // LANES, LANES, -1).transpose(1, 0, 2, 3).reshape(3 * ATTN_W, -1)


def kernel(x, norm1_g, w_in, attn_norm_g, hgrn_norm_g, hgrn_lb_logits, w_out, norm2_g, w_up, conv_w, conv_b, w_down, final_norm_g, loss_target, m_norm1_g, m_w_in, m_attn_norm_g, m_hgrn_norm_g, m_hgrn_lb_logits, m_w_out, m_norm2_g, m_w_up, m_conv_w, m_conv_b, m_w_down, m_final_norm_g, v_norm1_g, v_w_in, v_attn_norm_g, v_hgrn_norm_g, v_hgrn_lb_logits, v_w_out, v_norm2_g, v_w_up, v_conv_w, v_conv_b, v_w_down, v_final_norm_g):
    xs, target = x[0], loss_target[0]
    S, D = xs.shape
    NA = 3 * ATTN_W
    fng = final_norm_g.reshape(1, D)

    t = lambda a: a[0].T
    casts = [_sum_cast([w], BF16, name=f"cast_{nm}") for nm, w in
             (("w_in", t(w_in)), ("w_out", w_out[0]), ("w_up", t(w_up)), ("w_down", w_down[0]))]
    me = _dev_index(_coords())
    (g_in,) = _all_gather(casts[:1], name="ag_w_in")
    later = casts[1:] + [conv_w[0]]
    ag1 = _copies_start(later, [lax.empty((N_DEV, *s.shape), s.dtype) for s in later], _ag_chips_plan,
                        3 * len(later), name="ag_chips_start", after=g_in)
    wi = g_in.reshape(-1, D)
    wi_a = _pair_major(wi[:NA])

    u1, proj_a, proj_h, proj_f = _in_proj(xs, norm1_g, wi_a, wi, NA, name="in_proj", after=ag1[4])
    attn, lse = _attn_fwd(proj_a, name="attn_fwd")
    later, lands = _copies_wait(ag1, _ag_chips_plan, attn, name="ag_chips_wait", with_srcs=True)
    lands = [lax.dynamic_update_index_in_dim(l, s, me, 0) for l, s in zip(lands, later)]
    ag2 = _copies_start([], lands, _ag_sibling_plan, 4 * len(later), name="ag_sibling_start")
    rec, states = _hgrn_fwd(proj_h, proj_f, hgrn_lb_logits, name="hgrn_fwd", after=ag2[4])
    g_out, g_up, g_down, g_cw = _copies_wait(ag2, _ag_sibling_plan, rec, name="ag_sibling_wait")
    wo = g_out.reshape(-1, D)
    wu = g_up.reshape(-1, D)
    wd = g_down.reshape(-1, D)
    cw = _gathered_cols(g_cw)
    h1, u2, mixed = _out_proj(attn, rec, proj_h, xs, attn_norm_g, hgrn_norm_g, norm2_g, wo, name="out_proj")
    gate, gelu, vslope, act = _up_glu(u2, wu, cw, conv_b, name="up_glu")
    dh2, dh2b, d_fng, loss_part = _down_loss(act, wd, h1, fng, target, name="down_loss")

    dgate, dval, d_cw, d_cb = _dact_glu_bwd(dh2b, wd, gate, gelu, vslope, cw, name="dact_glu_bwd")
    dw_down = _mm_tn([act], dh2b, tm=256, name="dw_down")
    dh1, dh1b, d_n2g = _grad_norm_input([dgate, dval], [(wu, 0)], h1, norm2_g, dh2, name="du2_norm2_bwd")
    F = dgate.shape[1]
    dw_up = _mm_tn([dgate, dval], u2, tm=256, name="dw_up")
    rs_ffn = _rs_start([dw_down.reshape(N_DEV, -1, D), dw_up.reshape(N_DEV, -1, D)], me, name="rs_ffn_start")
    dattn, delta, drec, dhg, d_ang, d_hng = _dmix_post_bwd(dh1b, wo, attn, rec, proj_h, attn_norm_g, hgrn_norm_g,
                                                          name="dmix_post_bwd", after=rs_ffn[0][4])
    dw_out = _mm_tn([mixed], dh1b, name="dw_out")
    rs_out = _rs_start([dw_out.reshape(N_DEV, -1, D)], me, name="rs_out_start")
    dproj_h, d_lbl = _hgrn_bwd(proj_h, proj_f, hgrn_lb_logits, states, drec, name="hgrn_bwd", after=rs_out[0][4])
    small = [("loss", loss_part, None, None, None),
             ("attn_norm_g", d_ang, attn_norm_g, m_attn_norm_g, v_attn_norm_g),
             ("hgrn_norm_g", d_hng, hgrn_norm_g, m_hgrn_norm_g, v_hgrn_norm_g),
             ("hgrn_lb_logits", d_lbl, hgrn_lb_logits, m_hgrn_lb_logits, v_hgrn_lb_logits),
             ("norm2_g", d_n2g, norm2_g, m_norm2_g, v_norm2_g),
             ("conv_b", d_cb, conv_b, m_conv_b, v_conv_b),
             ("final_norm_g", d_fng, final_norm_g, m_final_norm_g, v_final_norm_g)]
    pack = lambda arrs: jnp.concatenate([a.reshape(1, -1) for a in arrs], axis=1)
    small_own = [pack([s[1] for s in small]), d_cw]
    ag_small = _copies_start(small_own, [lax.empty((N_DEV, *s.shape), s.dtype) for s in small_own], _ag_direct_plan,
                             (N_DEV - 1) * len(small_own), name="ag_small_start")
    dproj_a = _attn_bwd(proj_a, dattn, lse, delta, name="attn_bwd")
    pairs = ATTN_W // LANES
    dw_in = _mm_tn([dproj_a], u1, tm=LANES, rows=wi.shape[0], row_block=lambda i: pairs * (i % 3) + i // 3,
                   name="dw_in_attn")
    dw_in = _mm_tn([dproj_h, dhg], u1, tm=256, rows=wi.shape[0], row_block=lambda i: i + NA // 256, into=dw_in,
                   name="dw_in_hgrn")
    rs_in = _rs_start([dw_in.reshape(N_DEV, -1, D)], me, name="rs_in_start", after=ag_small[4])
    grad_x, _, d_n1g = _grad_norm_input([dproj_a, dproj_h, dhg], [(wi_a, 0), (wi, NA)], xs,
                                        norm1_g, dh1, name="du1_norm1_bwd", after=rs_in[0][4])

    res = {}

    def update(nm, parts, w, m, v, transposed=False):
        if transposed:
            raw = _adamw(parts, t(w), t(m), t(v), name=f"adamw_{nm}")
            res[nm] = [r.T[None] for r in raw]
        else:
            raw = res[nm] = _adamw(parts, w, m, v, name=f"adamw_{nm}")
        return raw[1]

    g_down, g_up = _rs_finish(rs_ffn, grad_x, name="rs_ffn_wait")
    update("w_down", g_down, w_down, m_w_down, v_w_down)
    done_up = update("w_up", g_up, w_up, m_w_up, v_w_up, transposed=True)
    (g_out,) = _rs_finish(rs_out, grad_x, name="rs_out_wait")
    update("w_out", g_out, w_out, m_w_out, v_w_out)
    (g_in,) = _rs_finish(rs_in, done_up, name="rs_in_wait")
    done_in = update("w_in", g_in, w_in, m_w_in, v_w_in, transposed=True)

    small_own, small_all = _copies_wait(ag_small, _ag_direct_plan, grad_x, name="ag_small_wait", with_srcs=True)
    g_small, g_dcw = [lax.dynamic_update_index_in_dim(l, s, me, 0) for l, s in zip(small_all, small_own)]
    sm = _adamw_packed(g_small, [s[1].size for s in small], [None if s[2] is None else s[2:] for s in small],
                       name="adamw_small")
    for (nm, *_), r in zip(small, sm):
        res[nm] = r
    ncw = conv_w.shape[-1]
    mine_cw = lax.dynamic_slice_in_dim(g_dcw, me * ncw, ncw, axis=2)
    res["conv_w"] = _adamw([mine_cw], conv_w, m_conv_w, v_conv_w, name="adamw_conv_w")
    late, _ = lax.optimization_barrier((d_n1g, done_in))
    update("norm1_g", _all_gather([late], name="ag_norm1_grad"), norm1_g, m_norm1_g, v_norm1_g)

    loss = res["loss"][0][0, 0]
    order = ["norm1_g", "w_in", "attn_norm_g", "hgrn_norm_g", "hgrn_lb_logits", "w_out", "norm2_g", "w_up",
             "conv_w", "conv_b", "w_down", "final_norm_g"]
    return (loss, grad_x[None], *[res[nm][0] for nm in order], *[res[nm][1] for nm in order],
            *[res[nm][2] for nm in order], *[res[nm][3] for nm in order])
```

```python
import math

import jax
import jax.numpy as jnp
from jax import lax
from jax.experimental import pallas as pl
from jax.experimental.pallas import tpu as pltpu

F32, BF16 = jnp.float32, jnp.bfloat16
NORM_EPS = 1e-6
ATTN_HEADS, HEAD_DIM, ATTN_BLOCK = 8, 64, 128
DILATIONS = (1, 4, 16)
ATTN_SCALE = HEAD_DIM ** -0.5
ATTN_W = ATTN_HEADS * HEAD_DIM
HGRN_HEADS, HGRN_DIM, HGRN_CHUNK = 4, 128, 64
HGRN_W = HGRN_HEADS * HGRN_DIM
ADAM_LR, ADAM_B1, ADAM_B2, ADAM_EPS, ADAM_WD, ADAM_STEP = 0.001, 0.9, 0.999, 1e-08, 0.01, 10
LANES, SUBLANES = 128, 8
VMEM_LIMIT_BYTES = 56 * 1024 * 1024
N_DEV = 8
MESH = pl.DeviceIdType.MESH
HBM = pl.BlockSpec(memory_space=pltpu.HBM)
HIGHEST = lax.Precision.HIGHEST


def _cparams(*sem):
    return pltpu.CompilerParams(dimension_semantics=sem, vmem_limit_bytes=VMEM_LIMIT_BYTES)


def _tile(n, pref):
    if n <= pref:
        return n
    t = (pref // LANES) * LANES
    while n % t:
        t -= LANES
    return t


def _resident(shape):
    return pl.BlockSpec(shape, lambda *_: (0,) * len(shape), pipeline_mode=pl.Buffered(1))


def _dot(a, b, dims, precision=None):
    return lax.dot_general(a, b, (dims, ((), ())), precision=precision, preferred_element_type=F32)


def _nn(a, b, precision=None):
    return _dot(a, b, ((1,), (0,)), precision)


def _nt(a, b):
    return _dot(a, b, ((1,), (1,)))


def _tn(a, b):
    return _dot(a, b, ((0,), (0,)))


def _sigmoid(x):
    return 1.0 / (1.0 + jnp.exp(-x))


def _rstd(x):
    return lax.rsqrt(jnp.mean(x * x, axis=-1, keepdims=True) + NORM_EPS)


def _norm_bwd(x, g, du):
    r = _rstd(x)
    xh = x * r
    dxh = du * g
    return r * (dxh - xh * jnp.mean(dxh * xh, axis=-1, keepdims=True)), du * xh


def _row_halves(tm):
    return [pl.ds(0, tm // 2), pl.ds(tm // 2, tm // 2)]


def _behind(after, n_in):
    if after is None:
        return [], [], (lambda body: body)
    return ([after], [pl.BlockSpec(memory_space=pl.ANY)],
            lambda body: (lambda *refs: body(*refs[:n_in], *refs[n_in + 1:])))


def _accumulate(ref, part, first):
    @pl.when(first)
    def _():
        ref[...] = part

    @pl.when(jnp.logical_not(first))
    def _():
        ref[...] += part


def _mm_tn(xs, dy, *, name, tm=512, tn=1024, rows=None, row_block=None, into=None):
    S, N = dy.shape
    tm = _tile(math.gcd(*[x.shape[1] for x in xs]), tm)
    tn = _tile(N, tn)
    blocks = [x.shape[1] // tm for x in xs]
    first = [sum(blocks[:k]) for k in range(len(xs))]
    row_block = row_block or (lambda i: i)
    n = len(xs)

    def body(*refs):
        x_refs, dy_ref = refs[:n], refs[n]
        o_ref, xt_ref = refs[-2:]
        @pl.when(pl.program_id(1) == 0)
        def _():
            xv = x_refs[0][...]
            for x_ref, b0 in zip(x_refs[1:], first[1:]):
                xv = jnp.where(pl.program_id(0) >= b0, x_ref[...], xv)
            xt_ref[...] = xv.T

        o_ref[...] = _nn(xt_ref[...], dy_ref[...]).astype(BF16)

    def x_spec(b0, nb):
        return pl.BlockSpec((S, tm), lambda i, j: (0, jnp.clip(i - b0, 0, nb - 1)))

    operands = [*xs, dy] + ([] if into is None else [into])
    return pl.pallas_call(
        body, name=name, grid=(sum(blocks), N // tn),
        in_specs=[x_spec(b0, nb) for b0, nb in zip(first, blocks)] + [pl.BlockSpec((S, tn), lambda i, j: (0, j))]
        + ([] if into is None else [pl.BlockSpec(memory_space=pl.ANY)]),
        out_specs=pl.BlockSpec((tm, tn), lambda i, j: (row_block(i), j)),
        out_shape=jax.ShapeDtypeStruct((rows or sum(blocks) * tm, N), BF16),
        input_output_aliases={} if into is None else {n + 1: 0},
        scratch_shapes=[pltpu.VMEM((tm, S), BF16)], compiler_params=_cparams("parallel", "arbitrary"),
    )(*operands)


def _in_proj(x, g, wt_attn, wt, row0, *, name, tm=512, after=None):
    S, D = x.shape
    NA, NH, W = wt_attn.shape[0], wt.shape[0] - row0, HGRN_W

    def body(x_ref, g_ref, wa_ref, w_ref, u_ref, a_ref, h_ref, f_ref):
        xv = x_ref[...]
        u = (xv * _rstd(xv) * g_ref[...]).astype(BF16)
        u_ref[...] = u
        a_ref[...] = _nt(u, wa_ref[...]).astype(BF16)
        ph = _nt(u, w_ref[row0:row0 + NH, :])
        h_ref[...] = ph.astype(BF16)
        f_ref[...] = ph[:, W:2 * W]

    row = lambda w: pl.BlockSpec((tm, w), lambda i: (i, 0))
    extra, extra_specs, adapt = _behind(after, 4)
    return pl.pallas_call(
        adapt(body), name=name, grid=(S // tm,),
        in_specs=[row(D), pl.BlockSpec((1, D), lambda i: (0, 0)), _resident(wt_attn.shape), _resident(wt.shape)]
        + extra_specs,
        out_specs=[row(D), row(NA), row(NH), row(W)],
        out_shape=[jax.ShapeDtypeStruct((S, D), BF16), jax.ShapeDtypeStruct((S, NA), BF16),
                   jax.ShapeDtypeStruct((S, NH), BF16), jax.ShapeDtypeStruct((S, W), F32)],
        compiler_params=_cparams("parallel"),
    )(x, g, wt_attn, wt, *extra)


PAIR_W = 3 * LANES
ATTN_UNROLL_FWD, ATTN_UNROLL_BWD = 8, 4


def _attn_masks(first):
    qi = lax.broadcasted_iota(jnp.int32, (ATTN_BLOCK, 2 * ATTN_BLOCK), 0)
    kj = lax.broadcasted_iota(jnp.int32, (ATTN_BLOCK, 2 * ATTN_BLOCK), 1)
    dist = qi + ATTN_BLOCK - kj
    valid = (dist >= 0) & (dist <= ATTN_BLOCK) & jnp.logical_or(kj >= ATTN_BLOCK, jnp.logical_not(first))
    lane = lax.broadcasted_iota(jnp.int32, (1, LANES), 1)
    return valid, lane


def _for_residue_blocks(S, d, fn):
    span = ATTN_BLOCK * d
    nb = S // span

    def step(n, carry):
        base = pl.multiple_of(n * span, span)
        for r in range(d):
            off = pl.multiple_of((r * nb + n) * ATTN_BLOCK, ATTN_BLOCK)
            fn(lambda ref, r=r: _block_rows(ref, base, r, d),
               lambda ref, val, r=r: _set_block_rows(ref, base, r, d, val), off)
        return carry

    lax.fori_loop(0, nb, step, 0)


def _for_blocks(S, unroll, fn):
    def step(i, carry):
        fn([(pl.multiple_of((i * unroll + u) * ATTN_BLOCK, ATTN_BLOCK), i * unroll + u) for u in range(unroll)])
        return carry

    lax.fori_loop(0, S // ATTN_BLOCK // unroll, step, 0)


def _head_value(x2, lane, e):
    return jnp.sum(jnp.where(lane == HEAD_DIM * e, x2, 0.0), axis=-1, keepdims=True)


def _block_rows(ref, base, r, d):
    if d == 1:
        return ref[pl.ds(base, ATTN_BLOCK), :]
    return ref.at[pl.ds(base, ATTN_BLOCK * d)][pl.ds(r, ATTN_BLOCK, stride=d), :]


def _set_block_rows(ref, base, r, d, val):
    if d == 1:
        ref[pl.ds(base, ATTN_BLOCK), :] = val
    else:
        ref.at[pl.ds(base, ATTN_BLOCK * d)][pl.ds(r, ATTN_BLOCK, stride=d), :] = val


def _order4_to_16(src, dst, pad):
    S = src.shape[0]
    q4, q16 = S // 4, S // 16
    for r in range(4):
        for a in range(4):
            for n in range(q16 // ATTN_BLOCK):
                rows = src.at[pl.ds(r * q4 + 4 * ATTN_BLOCK * n, 4 * ATTN_BLOCK)][pl.ds(a, ATTN_BLOCK, stride=4), :]
                dst[pl.ds(pad + (4 * a + r) * q16 + ATTN_BLOCK * n, ATTN_BLOCK), :] = rows.astype(dst.dtype)


def _order16_to_4(src, pad, dst):
    S = dst.shape[0]
    q4, q16 = S // 4, S // 16
    for r in range(4):
        for a in range(4):
            for n in range(q16 // ATTN_BLOCK):
                rows = src[pl.ds(pad + (4 * a + r) * q16 + ATTN_BLOCK * n, ATTN_BLOCK), :]
                dst.at[pl.ds(r * q4 + 4 * ATTN_BLOCK * n, 4 * ATTN_BLOCK)][pl.ds(a, ATTN_BLOCK, stride=4), :] = rows


def _regroup(S, d, pairs, tmp):
    for src, dst, pad in pairs:
        if d == 16:
            def to_tmp(rows, _, off, src=src):
                tmp[pl.ds(off, ATTN_BLOCK), :] = rows(src)

            _for_residue_blocks(S, 4, to_tmp)
            _order4_to_16(tmp, dst, pad)
    if d != 16:
        def to_dst(rows, _, off):
            for src, dst, pad in pairs:
                dst[pl.ds(pad + off, ATTN_BLOCK), :] = rows(src).astype(dst.dtype)

        _for_residue_blocks(S, d, to_dst)


def _split_pair(p_ref, qs, ks, vs, bk, bv):
    qs[...] = p_ref[:, 0:LANES].astype(F32) * ATTN_SCALE
    ks[...] = p_ref[:, LANES:2 * LANES].astype(F32)
    vs[...] = p_ref[:, 2 * LANES:3 * LANES].astype(F32)
    bk[0:ATTN_BLOCK, :] = jnp.zeros((ATTN_BLOCK, LANES), bk.dtype)
    bv[0:ATTN_BLOCK, :] = jnp.zeros((ATTN_BLOCK, LANES), bv.dtype)


def _attn_fwd(proj_a, *, name):
    S = proj_a.shape[0]

    def body(p_ref, o_ref, l_ref, qs, ks, vs, bq, bk, bv, bo, bl, to, tl):
        _split_pair(p_ref, qs, ks, vs, bk, bv)
        for d in DILATIONS:
            nb = S // (ATTN_BLOCK * d)
            _regroup(S, d, ((qs, bq, 0), (ks, bk, ATTN_BLOCK), (vs, bv, ATTN_BLOCK)), to)

            def blocks(group, nb=nb):
                lane = lax.broadcasted_iota(jnp.int32, (1, LANES), 1)
                heads = [(lane >= HEAD_DIM * e) & (lane < HEAD_DIM * (e + 1)) for e in range(LANES // HEAD_DIM)]
                wins = [pl.ds(off, 2 * ATTN_BLOCK) for off, _ in group]
                s = [[_nt(jnp.where(mh, bq[pl.ds(off, ATTN_BLOCK), :], jnp.zeros((ATTN_BLOCK, LANES), BF16)), bk[win, :])
                      for mh in heads] for (off, _), win in zip(group, wins)]
                p, m, l = [], [], []
                for (off, b), su in zip(group, s):
                    valid, _ = _attn_masks(jnp.bitwise_and(b, nb - 1) == 0)
                    sm = [jnp.where(valid, x, -jnp.inf) for x in su]
                    m.append([jnp.max(x, axis=-1, keepdims=True) for x in sm])
                    p.append([jnp.exp(x - mx) for x, mx in zip(sm, m[-1])])
                    l.append([jnp.sum(x, axis=-1, keepdims=True) for x in p[-1]])
                o = [[_nn(x.astype(BF16), bv[win, :]) for x in pu] for pu, win in zip(p, wins)]
                for (off, _), ou, mu, lu in zip(group, o, m, l):
                    o2 = jnp.zeros((ATTN_BLOCK, LANES), F32)
                    l2 = jnp.zeros((ATTN_BLOCK, LANES), F32)
                    for mh, oe, me_, le in zip(heads, ou, mu, lu):
                        o2 = jnp.where(mh, oe / le, o2)
                        l2 = jnp.where(mh, me_ + jnp.log(le), l2)
                    bo[pl.ds(off, ATTN_BLOCK), :] = o2
                    bl[pl.ds(off, ATTN_BLOCK), :] = l2

            _for_blocks(S, ATTN_UNROLL_FWD, blocks)

            if d == 16:
                _order16_to_4(bo, 0, to)
                _order16_to_4(bl, 0, tl)
            src_o, src_l = (to, tl) if d == 16 else (bo, bl)

            def merge(rows, set_rows, off, d=d, src_o=src_o, src_l=src_l):
                blk = pl.ds(off, ATTN_BLOCK)
                o2, l2 = src_o[blk, :], src_l[blk, :]
                if d != DILATIONS[0]:
                    lo, oo = rows(l_ref), rows(o_ref)
                    ln = jnp.maximum(lo, l2)
                    wa, wb = jnp.exp(lo - ln), jnp.exp(l2 - ln)
                    o2 = (wa * oo + wb * o2) / (wa + wb)
                    l2 = ln + jnp.log(wa + wb)
                set_rows(o_ref, o2)
                set_rows(l_ref, l2)

            _for_residue_blocks(S, min(d, 4), merge)

    slab = pl.BlockSpec((S, LANES), lambda p: (0, p))
    f32_slab, bf16_slab = pltpu.VMEM((S, LANES), F32), pltpu.VMEM((S, LANES), BF16)
    bf16_window = pltpu.VMEM((S + ATTN_BLOCK, LANES), BF16)
    return pl.pallas_call(
        body, name=name, grid=(ATTN_W // LANES,), in_specs=[pl.BlockSpec((S, PAIR_W), lambda p: (0, p))],
        out_specs=[slab, slab],
        out_shape=[jax.ShapeDtypeStruct((S, ATTN_W), F32), jax.ShapeDtypeStruct((S, ATTN_W), F32)],
        scratch_shapes=[f32_slab] * 3 + [bf16_slab, bf16_window, bf16_window] + [f32_slab] * 4,
        compiler_params=_cparams("parallel"),
    )(proj_a)


def _attn_bwd(proj_a, do, lse, delta, *, name):
    S = proj_a.shape[0]

    def body(p_ref, do_ref, lse_ref, dl_ref, o_ref, qs, ks, vs, dqs, dks, dvs, bq, bk, bv, bdo, blse, bdl, bdq, bdk, bdv,
             tmp):
        _split_pair(p_ref, qs, ks, vs, bk, bv)
        bdk[0:ATTN_BLOCK, :] = jnp.zeros((ATTN_BLOCK, LANES), F32)
        bdv[0:ATTN_BLOCK, :] = jnp.zeros((ATTN_BLOCK, LANES), F32)
        for d in DILATIONS:
            nb = S // (ATTN_BLOCK * d)
            _regroup(S, d, ((qs, bq, 0), (ks, bk, ATTN_BLOCK), (vs, bv, ATTN_BLOCK), (do_ref, bdo, 0),
                            (lse_ref, blse, 0), (dl_ref, bdl, 0)), tmp)

            def blocks(group, nb=nb):
                lane = lax.broadcasted_iota(jnp.int32, (1, LANES), 1)
                heads = [(lane >= HEAD_DIM * e) & (lane < HEAD_DIM * (e + 1)) for e in range(LANES // HEAD_DIM)]
                zero = jnp.zeros((ATTN_BLOCK, LANES), BF16)
                chains = [(off, b, e, mh) for off, b in group for e, mh in enumerate(heads)]
                qm = [jnp.where(mh, bq[pl.ds(off, ATTN_BLOCK), :], zero) for off, _, _, mh in chains]
                dom = [jnp.where(mh, bdo[pl.ds(off, ATTN_BLOCK), :], zero) for off, _, _, mh in chains]
                s = [_nt(x, bk[pl.ds(off, 2 * ATTN_BLOCK), :]) for x, (off, _, _, _) in zip(qm, chains)]
                dp = [_nt(x, bv[pl.ds(off, 2 * ATTN_BLOCK), :]) for x, (off, _, _, _) in zip(dom, chains)]
                p, ds = [], []
                for (off, b, e, _), sc, dpc in zip(chains, s, dp):
                    valid, _ = _attn_masks(jnp.bitwise_and(b, nb - 1) == 0)
                    blk = pl.ds(off, ATTN_BLOCK)
                    pc = jnp.where(valid, jnp.exp(sc - _head_value(blse[blk, :], lane, e)), 0.0)
                    ds.append((pc * (dpc - _head_value(bdl[blk, :], lane, e))).astype(BF16))
                    p.append(pc.astype(BF16))
                dq = [_nn(x, bk[pl.ds(off, 2 * ATTN_BLOCK), :]) for x, (off, _, _, _) in zip(ds, chains)]
                dk = [_tn(x, y) for x, y in zip(ds, qm)]
                dv = [_tn(x, y) for x, y in zip(p, dom)]
                nh = len(heads)
                for u, (off, _) in enumerate(group):
                    dq2 = jnp.zeros((ATTN_BLOCK, LANES), F32)
                    for mh, x in zip(heads, dq[nh * u:nh * (u + 1)]):
                        dq2 = jnp.where(mh, x, dq2)
                    bdq[pl.ds(off, ATTN_BLOCK), :] = dq2 * ATTN_SCALE
                    for acc, grads in ((bdk, dk), (bdv, dv)):
                        win_grad = sum(grads[nh * u + 1:nh * (u + 1)], grads[nh * u])
                        acc[pl.ds(off, ATTN_BLOCK), :] += win_grad[:ATTN_BLOCK]
                        acc[pl.ds(off + ATTN_BLOCK, ATTN_BLOCK), :] = win_grad[ATTN_BLOCK:]

            _for_blocks(S, ATTN_UNROLL_BWD, blocks)

            outs = ((dqs, bdq, 0), (dks, bdk, ATTN_BLOCK), (dvs, bdv, ATTN_BLOCK))
            if d == 16:
                for acc, grad, pad in outs:
                    _order16_to_4(grad, pad, tmp)

                    def add(rows, set_rows, off, acc=acc):
                        set_rows(acc, rows(acc) + tmp[pl.ds(off, ATTN_BLOCK), :])

                    _for_residue_blocks(S, 4, add)
            else:
                def scatter(rows, set_rows, off, d=d):
                    for acc, grad, pad in outs:
                        part = grad[pl.ds(pad + off, ATTN_BLOCK), :]
                        set_rows(acc, part if d == DILATIONS[0] else rows(acc) + part)

                _for_residue_blocks(S, d, scatter)
        o_ref[:, 0:LANES] = dqs[...].astype(BF16)
        o_ref[:, LANES:2 * LANES] = dks[...].astype(BF16)
        o_ref[:, 2 * LANES:3 * LANES] = dvs[...].astype(BF16)

    slab = pl.BlockSpec((S, LANES), lambda p: (0, p))
    pair = pl.BlockSpec((S, PAIR_W), lambda p: (0, p))
    f32_slab, bf16_slab = pltpu.VMEM((S, LANES), F32), pltpu.VMEM((S, LANES), BF16)
    f32_window, bf16_window = pltpu.VMEM((S + ATTN_BLOCK, LANES), F32), pltpu.VMEM((S + ATTN_BLOCK, LANES), BF16)
    return pl.pallas_call(
        body, name=name, grid=(ATTN_W // LANES,), in_specs=[pair, slab, slab, slab], out_specs=pair,
        out_shape=jax.ShapeDtypeStruct(proj_a.shape, BF16),
        scratch_shapes=[f32_slab] * 6 + [bf16_slab, bf16_window, bf16_window, bf16_slab, f32_slab, f32_slab,
                                         f32_slab, f32_window, f32_window, f32_slab],
        compiler_params=_cparams("parallel"),
    )(proj_a, do, lse, delta)


HG_T = 2 * HGRN_CHUNK
HG_GROUPS = 2
HG_STEP = HG_GROUPS * HG_T


def _hgrn_consts():
    row = lax.broadcasted_iota(jnp.int32, (HG_T, HG_T), 0)
    col = lax.broadcasted_iota(jnp.int32, (HG_T, HG_T), 1)
    same = (row >= HGRN_CHUNK) == (col >= HGRN_CHUNK)
    return row, same & (col <= row), same & (col >= row)


def _lower_bound(logits_ref):
    l0, l1 = logits_ref[0:1, :], logits_ref[1:2, :]
    mx = jnp.maximum(l0, l1)
    e0, e1 = jnp.exp(l0 - mx), jnp.exp(l1 - mx)
    return e0 / (e0 + e1)


def _hgrn_chains():
    chains = [(g, h) for g in range(HG_GROUPS) for h in range(HGRN_HEADS)]
    rows = [pl.ds(HG_T * g, HG_T) for g, _ in chains]
    lanes = [slice(HGRN_DIM * h, HGRN_DIM * (h + 1)) for _, h in chains]
    return chains, rows, lanes


def _hgrn_gates(qs, fs, lbs, row, causal):
    C = HGRN_CHUNK
    tri = jnp.where(causal, 1.0, 0.0).astype(F32)
    sgs = [_sigmoid(f) for f in fs]
    forgets = [lb + (1.0 - lb) * sg for lb, sg in zip(lbs, sgs)]
    logfs = [jnp.log(forget) for forget in forgets]
    bs = [_nn(tri, logf, HIGHEST) for logf in logfs]
    out = []
    for q, sg, forget, logf, b in zip(qs, sgs, forgets, logfs, bs):
        key = 1.0 - forget
        bend0 = jnp.sum(logf[:C], axis=0, keepdims=True)
        bend1 = jnp.sum(logf[C:], axis=0, keepdims=True)
        bend = jnp.where(row < C, bend0, bend1)
        eb, emb, eend = jnp.exp(b), jnp.exp(-b), jnp.exp(bend - b)
        sq = _sigmoid(q)
        out.append(dict(sg=sg, forget=forget, key=key, bend0=bend0, bend1=bend1, eb=eb, emb=emb, eend=eend, sq=sq,
                        qd=q * sq * eb, ki=key * emb, ke=key * eend))
    return out


def _hgrn_fwd(proj, proj_f, logits, *, name, after=None):
    S = proj.shape[0]
    W, C = HGRN_W, HGRN_CHUNK

    def body(q_ref, f_ref, i_ref, lg_ref, rec_ref, st_ref, s_ref):
        @pl.when(pl.program_id(0) == 0)
        def _():
            s_ref[...] = jnp.zeros_like(s_ref)

        row, causal, _ = _hgrn_consts()
        lb_all = _lower_bound(lg_ref)
        chains, rows, lanes = _hgrn_chains()
        n = range(len(chains))
        gts = _hgrn_gates([q_ref[rows[c], lanes[c]].astype(F32) for c in n], [f_ref[rows[c], lanes[c]] for c in n],
                          [lb_all[:, lanes[c]] for c in n], row, causal)
        qd, ki, ke = ([gt[k].astype(BF16) for gt in gts] for k in ("qd", "ki", "ke"))
        iv = [i_ref[rows[c], lanes[c]].astype(BF16) for c in n]
        a = [_nt(qd[c], ki[c]) for c in n]
        u0 = [_tn(iv[c][:C], ke[c][:C]) for c in n]
        u1 = [_tn(iv[c][C:], ke[c][C:]) for c in n]
        state = [s_ref[h] for h in range(HGRN_HEADS)]
        s0, s1 = [], []
        for c, (g, h) in enumerate(chains):
            s0.append(state[h])
            s1.append(jnp.exp(gts[c]["bend0"]) * s0[c] + u0[c])
            state[h] = jnp.exp(gts[c]["bend1"]) * s1[c] + u1[c]
        o0 = [_nt(qd[c][:C], s0[c].astype(BF16)) for c in n]
        o1 = [_nt(qd[c][C:], s1[c].astype(BF16)) for c in n]
        o = [_nn(jnp.where(causal, a[c], 0.0).astype(BF16), iv[c]) for c in n]
        for c, (g, h) in enumerate(chains):
            st_ref[2 * g, h] = s0[c]
            st_ref[2 * g + 1, h] = s1[c]
            rec_ref[rows[c], lanes[c]] = o[c] + jnp.concatenate([o0[c], o1[c]], axis=0)
        for h in range(HGRN_HEADS):
            s_ref[h] = state[h]

    blk = lambda j: pl.BlockSpec((HG_STEP, W), lambda t: (t, j))
    extra, extra_specs, adapt = _behind(after, 4)
    return pl.pallas_call(
        adapt(body), name=name, grid=(S // HG_STEP,),
        in_specs=[blk(0), blk(0), blk(2), pl.BlockSpec((2, W), lambda t: (0, 0))] + extra_specs,
        out_specs=[blk(0), pl.BlockSpec((2 * HG_GROUPS, HGRN_HEADS, HGRN_DIM, HGRN_DIM), lambda t: (t, 0, 0, 0))],
        out_shape=[jax.ShapeDtypeStruct((S, W), F32),
                   jax.ShapeDtypeStruct((S // C, HGRN_HEADS, HGRN_DIM, HGRN_DIM), F32)],
        scratch_shapes=[pltpu.VMEM((HGRN_HEADS, HGRN_DIM, HGRN_DIM), F32)],
        compiler_params=_cparams("arbitrary"),
    )(proj, proj_f, proj, logits, *extra)


def _hgrn_bwd(proj, proj_f, logits, states, drec, *, name, after=None):
    S = proj.shape[0]
    W, C = HGRN_W, HGRN_CHUNK
    nt = S // HG_STEP

    def body(q_ref, f_ref, i_ref, lg_ref, st_ref, do_ref, dp_ref, dlg_ref, ds_ref, dlb_ref):
        t = pl.program_id(0)

        @pl.when(t == 0)
        def _():
            ds_ref[...] = jnp.zeros_like(ds_ref)
            dlb_ref[...] = jnp.zeros_like(dlb_ref)

        row, causal, anti = _hgrn_consts()
        lb_all = _lower_bound(lg_ref)
        chains, rows, lanes = _hgrn_chains()
        n = range(len(chains))
        qs, lbs = [q_ref[rows[c], lanes[c]].astype(F32) for c in n], [lb_all[:, lanes[c]] for c in n]
        gts = _hgrn_gates(qs, [f_ref[rows[c], lanes[c]] for c in n], lbs, row, causal)
        qd, ki, ke = ([gt[k] for gt in gts] for k in ("qd", "ki", "ke"))
        qdb, kib, keb = ([x.astype(BF16) for x in xs] for xs in (qd, ki, ke))
        iv = [i_ref[rows[c], lanes[c]].astype(BF16) for c in n]
        dob = [do_ref[rows[c], lanes[c]].astype(BF16) for c in n]
        s0 = [st_ref[2 * g, h] for g, h in chains]
        s1 = [st_ref[2 * g + 1, h] for g, h in chains]
        dec0, dec1 = [jnp.exp(gt["bend0"]) for gt in gts], [jnp.exp(gt["bend1"]) for gt in gts]
        a = [_nt(qdb[c], kib[c]) for c in n]
        da = [_nt(dob[c], iv[c]) for c in n]
        dqd1 = [_nn(dob[c][C:], s1[c].astype(BF16)) for c in n]
        dqd0 = [_nn(dob[c][:C], s0[c].astype(BF16)) for c in n]
        t1 = [_tn(dob[c][C:], qdb[c][C:]) for c in n]
        t0 = [_tn(dob[c][:C], qdb[c][:C]) for c in n]
        carry = [ds_ref[h] for h in range(HGRN_HEADS)]
        ds1, ds0 = [None] * len(chains), [None] * len(chains)
        for c in reversed(n):
            h = chains[c][1]
            ds1[c] = carry[h]
            ds0[c] = dec1[c] * ds1[c] + t1[c]
            carry[h] = dec0[c] * ds0[c] + t0[c]
        for h in range(HGRN_HEADS):
            ds_ref[h] = carry[h]
        ds1b, ds0b = [x.astype(BF16) for x in ds1], [x.astype(BF16) for x in ds0]
        a = [jnp.where(causal, x, 0.0).astype(BF16) for x in a]
        da = [jnp.where(causal, x, 0.0).astype(BF16) for x in da]
        di1 = [_nt(keb[c][C:], ds1b[c]) for c in n]
        dke1 = [_nn(iv[c][C:], ds1b[c]) for c in n]
        di0 = [_nt(keb[c][:C], ds0b[c]) for c in n]
        dke0 = [_nn(iv[c][:C], ds0b[c]) for c in n]
        dqd_a = [_nn(da[c], kib[c]) for c in n]
        dki = [_tn(da[c], qdb[c]) for c in n]
        di_a = [_tn(a[c], dob[c]) for c in n]
        dqd, dke, db = [], [], []
        for c in n:
            ddec1 = jnp.sum(ds1[c] * s1[c], axis=0, keepdims=True)
            ddec0 = jnp.sum(ds0[c] * s0[c], axis=0, keepdims=True)
            dqd.append(dqd_a[c] + jnp.concatenate([dqd0[c], dqd1[c]], axis=0))
            h = chains[c][1]
            dp_ref[rows[c], 2 * W + HGRN_DIM * h:2 * W + HGRN_DIM * (h + 1)] = (
                di_a[c] + jnp.concatenate([di0[c], di1[c]], axis=0)).astype(BF16)
            dke.append(jnp.concatenate([dke0[c], dke1[c]], axis=0))
            gke = dke[c] * ke[c]
            dbend0 = jnp.sum(gke[:C], axis=0, keepdims=True) + ddec0 * dec0[c]
            dbend1 = jnp.sum(gke[C:], axis=0, keepdims=True) + ddec1 * dec1[c]
            dbc = dqd[c] * qd[c] - dki[c] * ki[c] - gke
            db.append(dbc + jnp.where(row == C - 1, dbend0, 0.0) + jnp.where(row == HG_T - 1, dbend1, 0.0))
        tri = jnp.where(anti, 1.0, 0.0).astype(F32)
        dlogf = [_nn(tri, db[c], HIGHEST) for c in n]
        for c in n:
            gt, lb, q, h = gts[c], lbs[c], qs[c], chains[c][1]
            dforget = dlogf[c] / gt["forget"] - (dki[c] * gt["emb"] + dke[c] * gt["eend"])
            sg, sq = gt["sg"], gt["sq"]
            dp_ref[rows[c], W + HGRN_DIM * h:W + HGRN_DIM * (h + 1)] = (
                dforget * (1.0 - lb) * sg * (1.0 - sg)).astype(BF16)
            dlb_ref[:, lanes[c]] += jnp.sum(dforget * (1.0 - sg), axis=0, keepdims=True)
            dp_ref[rows[c], lanes[c]] = (dqd[c] * gt["eb"] * sq * (1.0 + q * (1.0 - sq))).astype(BF16)

        @pl.when(t == nt - 1)
        def _():
            dl0 = dlb_ref[...] * lb_all * (1.0 - lb_all)
            dlg_ref[0:1, :] = dl0
            dlg_ref[1:2, :] = -dl0

    blk = lambda j: pl.BlockSpec((HG_STEP, W), lambda t: (nt - 1 - t, j))
    full = pl.BlockSpec((2, W), lambda t: (0, 0))
    extra, extra_specs, adapt = _behind(after, 6)
    return pl.pallas_call(
        adapt(body), name=name, grid=(nt,),
        in_specs=[blk(0), blk(0), blk(2), full,
                  pl.BlockSpec((2 * HG_GROUPS, HGRN_HEADS, HGRN_DIM, HGRN_DIM), lambda t: (nt - 1 - t, 0, 0, 0)), blk(0)]
        + extra_specs,
        out_specs=[pl.BlockSpec((HG_STEP, 3 * W), lambda t: (nt - 1 - t, 0)), full],
        out_shape=[jax.ShapeDtypeStruct((S, 3 * W), BF16), jax.ShapeDtypeStruct((2, W), F32)],
        scratch_shapes=[pltpu.VMEM((HGRN_HEADS, HGRN_DIM, HGRN_DIM), F32), pltpu.VMEM((1, W), F32)],
        compiler_params=_cparams("arbitrary"),
    )(proj, proj_f, proj, logits, states, drec, *extra)


def _out_proj(attn, rec, proj_h, x, g_attn, g_hgrn, g_norm2, w_out, *, name, tm=512):
    S, D = x.shape
    AW, W = ATTN_W, HGRN_W

    def body(a_ref, r_ref, hg_ref, x_ref, ga_ref, gh_ref, g2_ref, w_ref, h_ref, u_ref, m_ref):
        av = a_ref[...]
        m_ref[:, :AW] = (av * _rstd(av) * ga_ref[...]).astype(BF16)
        for h in range(HGRN_HEADS):
            sl = slice(HGRN_DIM * h, HGRN_DIM * (h + 1))
            rv, hg = r_ref[:, sl], hg_ref[:, sl].astype(F32)
            m_ref[:, AW + HGRN_DIM * h:AW + HGRN_DIM * (h + 1)] = (
                (rv * _rstd(rv) * gh_ref[:, sl]) * (hg * _sigmoid(hg))).astype(BF16)
        h1 = x_ref[...] + _nn(m_ref[...], w_ref[...])
        h_ref[...] = h1
        u_ref[...] = (h1 * _rstd(h1) * g2_ref[...]).astype(BF16)

    row = lambda w, j=0: pl.BlockSpec((tm, w), lambda i: (i, j))
    vec = lambda w: pl.BlockSpec((1, w), lambda i: (0, 0))
    return pl.pallas_call(
        body, name=name, grid=(S // tm,),
        in_specs=[row(AW), row(W), row(W, 3), row(D), vec(AW), vec(W), vec(D), _resident(w_out.shape)],
        out_specs=[row(D), row(D), row(AW + W)],
        out_shape=[jax.ShapeDtypeStruct((S, D), F32), jax.ShapeDtypeStruct((S, D), BF16),
                   jax.ShapeDtypeStruct((S, AW + W), BF16)],
        compiler_params=_cparams("parallel"),
    )(attn, rec, proj_h, x, g_attn, g_hgrn, g_norm2, w_out)


def _dmix_post_bwd(dh1b, w_out, attn, rec, proj_h, g_attn, g_hgrn, *, name, tm=512, after=None):
    S, D = dh1b.shape
    AW, W = ATTN_W, HGRN_W

    def body(dh_ref, w_ref, a_ref, r_ref, hg_ref, ga_ref, gh_ref, do_ref, dl_ref, dr_ref, dhg_ref, dga_ref, dgh_ref):
        first = pl.program_id(0) == 0
        dmix = _nt(dh_ref[...], w_ref[...])
        av = a_ref[...]
        dov, dga = _norm_bwd(av, ga_ref[...], dmix[:, :AW])
        do_ref[...] = dov
        shift = HEAD_DIM.bit_length() - 1
        hi = lax.shift_right_logical(lax.broadcasted_iota(jnp.int32, (AW, AW), 0), shift)
        hj = lax.shift_right_logical(lax.broadcasted_iota(jnp.int32, (AW, AW), 1), shift)
        prod = dov * av
        hi_part = prod.astype(BF16)
        lo_part = (prod - hi_part.astype(F32)).astype(BF16)
        same_head = jnp.where(hi == hj, 1.0, 0.0).astype(BF16)
        dl_ref[...] = _nn(hi_part, same_head) + _nn(lo_part, same_head)
        _accumulate(dga_ref, jnp.sum(dga, axis=0, keepdims=True), first)

        @pl.when(first)
        def _():
            dgh_ref[...] = jnp.zeros_like(dgh_ref)

        for h in range(HGRN_HEADS):
            sl = slice(HGRN_DIM * h, HGRN_DIM * (h + 1))
            rv, hg, gv = r_ref[:, sl], hg_ref[:, sl].astype(F32), gh_ref[:, sl]
            dout = dmix[:, AW + HGRN_DIM * h:AW + HGRN_DIM * (h + 1)]
            sg = _sigmoid(hg)
            drv, dgh = _norm_bwd(rv, gv, dout * (hg * sg))
            dr_ref[:, sl] = drv
            dgh_ref[:, sl] += jnp.sum(dgh, axis=0, keepdims=True)
            dhg_ref[:, sl] = (dout * (rv * _rstd(rv) * gv) * (sg * (1.0 + hg * (1.0 - sg)))).astype(BF16)

    row = lambda w, j=0: pl.BlockSpec((tm, w), lambda i: (i, j))
    vec = lambda w: pl.BlockSpec((1, w), lambda i: (0, 0))
    extra, extra_specs, adapt = _behind(after, 7)
    return pl.pallas_call(
        adapt(body), name=name, grid=(S // tm,),
        in_specs=[row(D), _resident(w_out.shape), row(AW), row(W), row(W, 3), vec(AW), vec(W)] + extra_specs,
        out_specs=[row(AW), row(AW), row(W), row(W), vec(AW), vec(W)],
        out_shape=[jax.ShapeDtypeStruct((S, AW), F32), jax.ShapeDtypeStruct((S, AW), F32),
                   jax.ShapeDtypeStruct((S, W), F32), jax.ShapeDtypeStruct((S, W), BF16),
                   jax.ShapeDtypeStruct((1, AW), F32), jax.ShapeDtypeStruct((1, W), F32)],
        compiler_params=_cparams("arbitrary"),
    )(dh1b, w_out, attn, rec, proj_h, g_attn, g_hgrn, *extra)


def _conv_act(g, g1, g2, w_ref, b_ref):
    c = b_ref[...] + w_ref[0:1, :] * g2 + w_ref[1:2, :] * g1 + w_ref[2:3, :] * g
    return c, 0.5 * (1.0 + lax.erf(c * (2.0 ** -0.5)))


def _shift_down(g, halo, row):
    g1 = jnp.where(row == 0, halo[7:8], pltpu.roll(g, 1, 0))
    g2 = jnp.where(row == 0, halo[6:7], jnp.where(row == 1, halo[7:8], pltpu.roll(g, 2, 0)))
    return g1, g2


def _shift_up(x, halo, row):
    n = x.shape[0]
    x1 = jnp.where(row == n - 1, halo[0:1], pltpu.roll(x, n - 1, 0))
    x2 = jnp.where(row == n - 2, halo[0:1], jnp.where(row == n - 1, halo[1:2], pltpu.roll(x, n - 2, 0)))
    return x1, x2


def _up_glu(u, wt_up, conv_w, conv_b, *, name, tm=1024, tn=1408):
    S, D = u.shape
    F = wt_up.shape[0] // 2
    tn = _tile(F, tn)
    nf = F // tn

    def body(u_ref, wg_ref, wv_ref, cw_ref, cb_ref, g_ref, ge_ref, t_ref, a_ref, halo_ref):
        i, j = pl.program_id(0), pl.program_id(1)

        @pl.when(i == 0)
        def _():
            halo_ref[j] = jnp.zeros((SUBLANES, tn), F32)

        uv = u_ref[...]
        g, v = _nt(uv, wg_ref[...]), _nt(uv, wv_ref[...])
        row = lax.broadcasted_iota(jnp.int32, (tm, tn), 0)
        g1, g2 = _shift_down(g, halo_ref[j], row)
        c, cdf = _conv_act(g, g1, g2, cw_ref, cb_ref)
        gelu = c * cdf
        pdf = jnp.exp(-0.5 * c * c) * (1.0 / (2.0 * jnp.pi) ** 0.5)
        a_ref[...] = (gelu * v).astype(BF16)
        g_ref[...] = g.astype(BF16)
        ge_ref[...] = gelu.astype(BF16)
        t_ref[...] = (v * (cdf + c * pdf)).astype(BF16)
        halo_ref[j] = g[tm - SUBLANES:, :]

    col = pl.BlockSpec((tm, tn), lambda i, j: (i, j))
    out = jax.ShapeDtypeStruct((S, F), BF16)
    return pl.pallas_call(
        body, name=name, grid=(S // tm, nf),
        in_specs=[pl.BlockSpec((tm, D), lambda i, j: (i, 0)), pl.BlockSpec((tn, D), lambda i, j: (j, 0)),
                  pl.BlockSpec((tn, D), lambda i, j: (j + nf, 0)), pl.BlockSpec((3, tn), lambda i, j: (0, j)),
                  pl.BlockSpec((1, tn), lambda i, j: (0, j))],
        out_specs=[col, col, col, col], out_shape=[out, out, out, out],
        scratch_shapes=[pltpu.VMEM((nf, SUBLANES, tn), F32)], compiler_params=_cparams("arbitrary", "arbitrary"),
    )(u, wt_up, wt_up, conv_w, conv_b)


def _dact_glu_bwd(dh2b, w_down, gate, gelu, vslope, conv_w, *, name, tm=1024, tn=1408):
    S, D = dh2b.shape
    F = gate.shape[1]
    tn = _tile(F, tn)
    nf, ni = F // tn, S // tm

    def body(dh_ref, wd_ref, g_ref, ge_ref, t_ref, cw_ref, dg_ref, dv_ref, dcw_ref, dcb_ref, halo_ref, acc_ref):
        i, j = pl.program_id(0), pl.program_id(1)

        @pl.when(i == 0)
        def _():
            halo_ref[j] = jnp.zeros((SUBLANES, tn), F32)
            acc_ref[j] = jnp.zeros((SUBLANES, tn), F32)

        g = g_ref[...].astype(F32)
        row = lax.broadcasted_iota(jnp.int32, (tm, tn), 0)
        da = _nt(dh_ref[...], wd_ref[...])
        dv_ref[...] = (da * ge_ref[...].astype(F32)).astype(BF16)
        dc = da * t_ref[...].astype(F32)
        d1, d2 = _shift_up(dc, halo_ref[j], row)
        dg_ref[...] = (cw_ref[2:3, :] * dc + cw_ref[1:2, :] * d1 + cw_ref[0:1, :] * d2).astype(BF16)
        halo_ref[j] = dc[:SUBLANES, :]
        for k, t in enumerate((d2 * g, d1 * g, dc * g, dc)):
            acc_ref[j, k:k + 1, :] += jnp.sum(t, axis=0, keepdims=True)

        @pl.when((i == ni - 1) & (j == nf - 1))
        def _():
            for jj in range(nf):
                dcw_ref[:, jj * tn:(jj + 1) * tn] = acc_ref[jj, 0:3, :]
                dcb_ref[:, jj * tn:(jj + 1) * tn] = acc_ref[jj, 3:4, :]

    tile = pl.BlockSpec((tm, tn), lambda i, j: (ni - 1 - i, j))
    return pl.pallas_call(
        body, name=name, grid=(ni, nf),
        in_specs=[pl.BlockSpec((tm, D), lambda i, j: (ni - 1 - i, 0)), pl.BlockSpec((tn, D), lambda i, j: (j, 0)),
                  tile, tile, tile, pl.BlockSpec((3, tn), lambda i, j: (0, j))],
        out_specs=[tile, tile, pl.BlockSpec((3, F), lambda i, j: (0, 0)), pl.BlockSpec((1, F), lambda i, j: (0, 0))],
        out_shape=[jax.ShapeDtypeStruct((S, F), BF16), jax.ShapeDtypeStruct((S, F), BF16),
                   jax.ShapeDtypeStruct((3, F), F32), jax.ShapeDtypeStruct((1, F), F32)],
        scratch_shapes=[pltpu.VMEM((nf, SUBLANES, tn), F32), pltpu.VMEM((nf, SUBLANES, tn), F32)],
        compiler_params=_cparams("arbitrary", "arbitrary"),
    )(dh2b, w_down, gate, gelu, vslope, conv_w)


def _down_loss(act, w_down, h1, g, target, *, name, tm=512):
    S, F = act.shape
    D = h1.shape[1]

    def body(a_ref, w_ref, h_ref, g_ref, t_ref, dh_ref, dhb_ref, dg_ref, loss_ref):
        first = pl.program_id(0) == 0
        h2 = h_ref[...] + _nn(a_ref[...], w_ref[...])
        gv = g_ref[...]
        r = _rstd(h2)
        xh = h2 * r
        err = xh * gv - t_ref[...]
        part_loss = 0.5 * jnp.sum(jnp.mean(err * err, axis=-1, keepdims=True), axis=0, keepdims=True)
        dy = err * (1.0 / D)
        dxh = dy * gv
        dh = r * (dxh - xh * jnp.mean(dxh * xh, axis=-1, keepdims=True))
        dh_ref[...] = dh
        dhb_ref[...] = dh.astype(BF16)
        _accumulate(dg_ref, jnp.sum(dy * xh, axis=0, keepdims=True), first)
        _accumulate(loss_ref, jnp.broadcast_to(part_loss, (1, LANES)), first)

    row = lambda w: pl.BlockSpec((tm, w), lambda i: (i, 0))
    vec = lambda w: pl.BlockSpec((1, w), lambda i: (0, 0))
    return pl.pallas_call(
        body, name=name, grid=(S // tm,), in_specs=[row(F), _resident(w_down.shape), row(D), vec(D), row(D)],
        out_specs=[row(D), row(D), vec(D), vec(LANES)],
        out_shape=[jax.ShapeDtypeStruct((S, D), F32), jax.ShapeDtypeStruct((S, D), BF16),
                   jax.ShapeDtypeStruct((1, D), F32), jax.ShapeDtypeStruct((1, LANES), F32)],
        compiler_params=_cparams("arbitrary"),
    )(act, w_down, h1, g, target)


def _grad_norm_input(pieces, ws, x, g, add, *, name, tm=512, after=None):
    S, D = x.shape
    widths = [p.shape[1] for p in pieces]
    n, nw = len(pieces), len(ws)
    where, wi, off = [], 0, ws[0][1]
    for wd in widths:
        if off == ws[wi][0].shape[0]:
            wi, off = wi + 1, ws[wi + 1][1]
        where.append((wi, off))
        off += wd
    ws = [w for w, _ in ws]

    def body(*refs):
        p_refs, w_refs = refs[:n], refs[n:n + nw]
        x_ref, g_ref, add_ref, dx_ref, dxb_ref, dg_ref = refs[n + nw:]
        halves = _row_halves(tm)
        du = []
        for rows in halves:
            terms = [_nn(p_refs[k][rows, :], w_refs[wi][off:off + widths[k], :]) for k, (wi, off) in enumerate(where)]
            du.append(sum(terms[1:], terms[0]))
        dg_sum = None
        for rows, duh in zip(halves, du):
            dx, dg = _norm_bwd(x_ref[rows, :], g_ref[...], duh)
            dx = add_ref[rows, :] + dx
            dx_ref[rows, :] = dx
            dxb_ref[rows, :] = dx.astype(BF16)
            part = jnp.sum(dg, axis=0, keepdims=True)
            dg_sum = part if dg_sum is None else dg_sum + part
        _accumulate(dg_ref, dg_sum, pl.program_id(0) == 0)

    row = lambda w_: pl.BlockSpec((tm, w_), lambda i: (i, 0))
    vec = pl.BlockSpec((1, D), lambda i: (0, 0))
    extra, extra_specs, adapt = _behind(after, n + nw + 3)
    return pl.pallas_call(
        adapt(body), name=name, grid=(S // tm,),
        in_specs=[row(wd) for wd in widths] + [_resident(w.shape) for w in ws] + [row(D), vec, row(D)] + extra_specs,
        out_specs=[row(D), row(D), vec],
        out_shape=[jax.ShapeDtypeStruct((S, D), F32), jax.ShapeDtypeStruct((S, D), BF16),
                   jax.ShapeDtypeStruct((1, D), F32)],
        compiler_params=_cparams("arbitrary"),
    )(*pieces, *ws, x, g, add, *extra)


def _rows(a):
    return a.reshape(-1, a.shape[-1])


def _row_tile(rows, cols, itemsize=4, budget=1 << 20):
    t = rows
    while t % 32 == 0 and t * cols * itemsize > budget:
        t //= 2
    return t


def _sum_cast(arrs, out_dtype, *, name):
    shape = arrs[0].shape
    flat = [_rows(a) for a in arrs]
    R, C = flat[0].shape
    tr = _row_tile(R, C)

    def body(*refs):
        acc = refs[0][...].astype(F32)
        for r in refs[1:-1]:
            acc = acc + r[...].astype(F32)
        refs[-1][...] = acc.astype(out_dtype)

    spec = pl.BlockSpec((tr, C), lambda i: (i, 0))
    return pl.pallas_call(
        body, name=name, grid=(R // tr,), in_specs=[spec] * len(flat), out_specs=spec,
        out_shape=jax.ShapeDtypeStruct((R, C), out_dtype), compiler_params=_cparams("parallel"),
    )(*flat).reshape(shape)


def _cast_together(arrs, out_dtype, *, name):
    n = len(arrs)

    def body(*refs):
        for i_ref, o_ref in zip(refs[:n], refs[n:]):
            o_ref[...] = i_ref[...].astype(out_dtype)

    return pl.pallas_call(body, name=name, out_shape=[jax.ShapeDtypeStruct(a.shape, out_dtype) for a in arrs],
                          compiler_params=_cparams())(*arrs)


def _adamw(parts, w, m, v, *, name):
    shape = w.shape
    w2, m2, v2 = _rows(w), _rows(m), _rows(v)
    R, C = w2.shape
    parts = [p.reshape(-1, R, C) for p in parts]
    tr = _row_tile(R, C)
    np_ = len(parts)
    c1, c2 = 1.0 - ADAM_B1 ** ADAM_STEP, 1.0 - ADAM_B2 ** ADAM_STEP

    def body(*refs):
        terms = [(r, k) for r in refs[:np_] for k in range(r.shape[0])]
        g = terms[0][0][terms[0][1]].astype(F32)
        for r, k in terms[1:]:
            g = g + r[k].astype(F32)
        w_ref, m_ref, v_ref, g_out, d_out, m_out, v_out = refs[np_:]
        mn = ADAM_B1 * m_ref[...] + (1.0 - ADAM_B1) * g
        vn = ADAM_B2 * v_ref[...] + (1.0 - ADAM_B2) * (g * g)
        g_out[...] = g
        d_out[...] = -ADAM_LR * ((mn / c1) / (jnp.sqrt(vn / c2) + ADAM_EPS) + ADAM_WD * w_ref[...])
        m_out[...] = mn
        v_out[...] = vn

    spec = pl.BlockSpec((tr, C), lambda i: (i, 0))
    out = jax.ShapeDtypeStruct((R, C), F32)
    stacks = [pl.BlockSpec((p.shape[0], tr, C), lambda i: (0, i, 0)) for p in parts]
    res = pl.pallas_call(
        body, name=name, grid=(R // tr,), in_specs=stacks + [spec] * 3, out_specs=[spec] * 4,
        out_shape=[out] * 4, compiler_params=_cparams("parallel"),
    )(*parts, w2, m2, v2)
    return [r.reshape(shape) for r in res]


def _adamw_packed(stack, widths, params, *, name):
    c1, c2 = 1.0 - ADAM_B1 ** ADAM_STEP, 1.0 - ADAM_B2 ** ADAM_STEP
    k = stack.shape[0]
    flat = [None if p is None else [_rows(a) for a in p] for p in params]
    n_in = sum(3 for p in flat if p is not None)

    def body(*refs):
        s_ref, ins, outs = refs[0], list(refs[1:1 + n_in]), list(refs[1 + n_in:])
        off = 0
        for width, p in zip(widths, flat):
            rows = 1 if p is None else p[0].shape[0]
            cols = width // rows
            w_ref, m_ref, v_ref = (None, None, None) if p is None else (ins.pop(0), ins.pop(0), ins.pop(0))
            o_refs = [outs.pop(0) for _ in range(1 if p is None else 4)]
            for r in range(rows):
                seg = slice(off + r * cols, off + (r + 1) * cols)
                g = s_ref[0, :, seg]
                for j in range(1, k):
                    g = g + s_ref[j, :, seg]
                o_refs[0][r:r + 1, :] = g
                if p is not None:
                    row = slice(r, r + 1)
                    mn = ADAM_B1 * m_ref[row, :] + (1.0 - ADAM_B1) * g
                    vn = ADAM_B2 * v_ref[row, :] + (1.0 - ADAM_B2) * (g * g)
                    o_refs[1][row, :] = -ADAM_LR * ((mn / c1) / (jnp.sqrt(vn / c2) + ADAM_EPS) + ADAM_WD * w_ref[row, :])
                    o_refs[2][row, :] = mn
                    o_refs[3][row, :] = vn
            off += width

    operands, out_shape = [stack], []
    for width, p in zip(widths, flat):
        if p is None:
            out_shape.append(jax.ShapeDtypeStruct((1, width), F32))
        else:
            operands += p
            out_shape += [jax.ShapeDtypeStruct(p[0].shape, F32)] * 4
    res = list(pl.pallas_call(body, name=name, out_shape=out_shape)(*operands))
    out = []
    for p, orig in zip(flat, params):
        n = 1 if p is None else 4
        out.append([r if orig is None else r.reshape(orig[0].shape) for r in res[:n]])
        res = res[n:]
    return out


def _coords():
    return lax.axis_index("x"), lax.axis_index("y"), lax.axis_index("c")


def _all_gather(shards, *, name):
    n = len(shards)

    def body(*refs):
        x_refs, out_refs = refs[:n], refs[n:2 * n]
        send_sems, recv_sems, local_sems = refs[2 * n:]
        x, y, c = _coords()
        me, sibling = (x, y, c), (x, y, 1 - c)
        chips = [(1 - x, y), (x, 1 - y), (1 - x, 1 - y)]

        def slot(a, dev):
            return out_refs[a].at[4 * dev[0] + 2 * dev[1] + dev[2]]

        def copy(a, k, block, to, src=None):
            return pltpu.make_async_remote_copy(
                src_ref=slot(a, block) if src is None else src, dst_ref=slot(a, block),
                send_sem=send_sems.at[7 * a + k], recv_sem=recv_sems.at[7 * a + k], device_id=to, device_id_type=MESH)

        mine = [pltpu.make_async_copy(x_refs[a], slot(a, me), local_sems.at[a]) for a in range(n)]
        for cp in mine:
            cp.start()
        first = []
        for a in range(n):
            first.append(copy(a, 0, me, sibling, src=x_refs[a]))
            first += [copy(a, 1 + j, me, (*chip, c), src=x_refs[a]) for j, chip in enumerate(chips)]
        for cp in first:
            cp.start()
        passed = []
        for j, chip in enumerate(chips):
            for a in range(n):
                copy(a, 1 + j, (*chip, c), me).wait_recv()
                fwd = copy(a, 4 + j, (*chip, c), sibling)
                fwd.start()
                passed.append(fwd)
        for a in range(n):
            copy(a, 0, sibling, me).wait_recv()
            for j, chip in enumerate(chips):
                copy(a, 4 + j, (*chip, 1 - c), me).wait_recv()
        for cp in first + passed:
            cp.wait_send()
        for cp in mine:
            cp.wait()

    return pl.pallas_call(
        body, name=name, in_specs=[HBM] * n, out_specs=[HBM] * n,
        out_shape=[jax.ShapeDtypeStruct((N_DEV, *s.shape), s.dtype) for s in shards],
        scratch_shapes=[pltpu.SemaphoreType.DMA((7 * n,)), pltpu.SemaphoreType.DMA((7 * n,)),
                        pltpu.SemaphoreType.DMA((n,))],
    )(*shards)


def _flip_y(x, y, c):
    return (x, 1 - y, c)


def _flip_x(x, y, c):
    return (1 - x, y, c)


def _flip_xy(x, y, c):
    return (1 - x, 1 - y, c)


SEM = pl.BlockSpec(memory_space=pltpu.SEMAPHORE)
SIDE_EFFECT = pltpu.SideEffectType.DATAFLOW_SIDE_EFFECTING


def _in_hbm(a):
    return pltpu.with_memory_space_constraint(a, pltpu.HBM)


def _copies_start(srcs, lands, plan, n_copies, *, name, after=None):
    ns, nl = len(srcs), len(lands)
    extra = [] if after is None else [after]

    def body(*refs):
        src_refs, land_refs = refs[:ns], refs[ns:ns + nl]
        send_sems, recv_sems = refs[ns + nl + len(extra):ns + nl + len(extra) + 2]
        token = refs[-1]
        for k, (src, dst, peer, _) in enumerate(plan(src_refs, land_refs, *_coords())):
            pltpu.make_async_remote_copy(src_ref=src, dst_ref=dst, send_sem=send_sems.at[k], recv_sem=recv_sems.at[k],
                                         device_id=peer, device_id_type=MESH).start()
        token[...] = jnp.zeros_like(token)

    bufs = [*srcs, *lands]
    res = pl.pallas_call(
        body, name=name, in_specs=[HBM] * (ns + nl) + [pl.BlockSpec(memory_space=pl.ANY)] * len(extra),
        out_specs=(SEM, SEM, *[HBM] * (ns + nl), pl.BlockSpec(memory_space=pltpu.VMEM)),
        out_shape=(pltpu.SemaphoreType.DMA((n_copies,)), pltpu.SemaphoreType.DMA((n_copies,)),
                   *[pltpu.HBM(b.shape, b.dtype) for b in bufs], jax.ShapeDtypeStruct((SUBLANES, LANES), F32)),
        input_output_aliases={i: 2 + i for i in range(ns + nl)},
        compiler_params=pltpu.CompilerParams(has_side_effects=SIDE_EFFECT),
    )(*[_in_hbm(b) for b in bufs], *extra)
    return res[0], res[1], list(res[2:2 + ns]), list(res[2 + ns:2 + ns + nl]), res[-1]


def _copies_wait(started, plan, after, *, name, with_srcs=False):
    send_sems, recv_sems, srcs, lands, _ = started
    ns, nl = len(srcs), len(lands)

    def body(*refs):
        src_refs, land_refs = refs[:ns], refs[ns:ns + nl]
        send_sems, recv_sems = refs[ns + nl:ns + nl + 2]
        for k, (src, dst, peer, here) in enumerate(plan(src_refs, land_refs, *_coords())):
            pltpu.make_async_remote_copy(src_ref=src, dst_ref=dst, send_sem=send_sems.at[k], recv_sem=recv_sems.at[k],
                                         device_id=peer, device_id_type=MESH).wait_send()
            pltpu.make_async_remote_copy(src_ref=src, dst_ref=here, send_sem=send_sems.at[k], recv_sem=recv_sems.at[k],
                                         device_id=peer, device_id_type=MESH).wait_recv()

    bufs = [*srcs, *lands]
    res = pl.pallas_call(
        body, name=name, in_specs=[HBM] * (ns + nl) + [SEM, SEM, pl.BlockSpec(memory_space=pl.ANY)],
        out_specs=[HBM] * (ns + nl), out_shape=[pltpu.HBM(b.shape, b.dtype) for b in bufs],
        input_output_aliases={i: i for i in range(ns + nl)},
        compiler_params=pltpu.CompilerParams(has_side_effects=SIDE_EFFECT),
    )(*bufs, send_sems, recv_sems, after)
    return (list(res[:ns]), list(res[ns:])) if with_srcs else list(res[ns:])


def _dev_index(dev):
    return 4 * dev[0] + 2 * dev[1] + dev[2]


def _ag_chips_plan(src_refs, land_refs, x, y, c):
    me = _dev_index((x, y, c))
    return [(src, land.at[me], peer, land.at[_dev_index(peer)])
            for src, land in zip(src_refs, land_refs) for peer in (_flip_y(x, y, c), _flip_x(x, y, c), _flip_xy(x, y, c))]


def _ag_sibling_plan(src_refs, land_refs, x, y, c):
    chips = [(x, y), (x, 1 - y), (1 - x, y), (1 - x, 1 - y)]
    return [(land.at[_dev_index((*chip, c))], land.at[_dev_index((*chip, c))], (x, y, 1 - c),
             land.at[_dev_index((*chip, 1 - c))]) for land in land_refs for chip in chips]


def _ag_direct_plan(src_refs, land_refs, x, y, c):
    me = _dev_index((x, y, c))
    plan = []
    for src, land in zip(src_refs, land_refs):
        for m in range(1, N_DEV):
            peer = (x + (m >> 2) * (1 - 2 * x), y + ((m >> 1) & 1) * (1 - 2 * y), c + (m & 1) * (1 - 2 * c))
            plan.append((src, land.at[me], peer, land.at[_dev_index(peer)]))
    return plan


def _rs_direct_plan(src_refs, land_refs, x, y, c):
    plan = []
    for src, land in zip(src_refs, land_refs):
        for m in range(1, N_DEV):
            peer = (x + (m >> 2) * (1 - 2 * x), y + ((m >> 1) & 1) * (1 - 2 * y), c + (m & 1) * (1 - 2 * c))
            plan.append((src.at[_dev_index(peer)], land.at[m - 1], peer, land.at[m - 1]))
    return plan


def _rs_start(grads, me, *, name, after=None):
    own = [lax.dynamic_index_in_dim(g, me, 0, keepdims=False) for g in grads]
    lands = [lax.empty((N_DEV - 1, *g.shape[1:]), g.dtype) for g in grads]
    return _copies_start(grads, lands, _rs_direct_plan, (N_DEV - 1) * len(grads), name=name, after=after), own


def _rs_finish(started, after, *, name):
    handle, own = started
    got = _copies_wait(handle, _rs_direct_plan, after, name=name)
    return [[o, land] for o, land in zip(own, got)]


def _gathered_cols(w8):
    return w8.transpose(1, 0, 2).reshape(w8.shape[1], -1)


def _pair_major(wt):
    return wt.reshape(3, ATTN_W // LANES, LANES, -1).transpose(1, 0, 2, 3).reshape(3 * ATTN_W, -1)


def kernel(x, norm1_g, w_in, attn_norm_g, hgrn_norm_g, hgrn_lb_logits, w_out, norm2_g, w_up, conv_w, conv_b, w_down, final_norm_g, loss_target, m_norm1_g, m_w_in, m_attn_norm_g, m_hgrn_norm_g, m_hgrn_lb_logits, m_w_out, m_norm2_g, m_w_up, m_conv_w, m_conv_b, m_w_down, m_final_norm_g, v_norm1_g, v_w_in, v_attn_norm_g, v_hgrn_norm_g, v_hgrn_lb_logits, v_w_out, v_norm2_g, v_w_up, v_conv_w, v_conv_b, v_w_down, v_final_norm_g):
    xs, target = x[0], loss_target[0]
    S, D = xs.shape
    NA = 3 * ATTN_W
    fng = final_norm_g.reshape(1, D)

    t = lambda a: a[0].T
    casts = [_sum_cast([t(w_in)], BF16, name="cast_w_in"),
             *_cast_together([w_out[0], t(w_up), w_down[0]], BF16, name="cast_later_weights")]
    me = _dev_index(_coords())
    (g_in,) = _all_gather(casts[:1], name="ag_w_in")
    later = casts[1:] + [conv_w[0]]
    ag1 = _copies_start(later, [lax.empty((N_DEV, *s.shape), s.dtype) for s in later], _ag_chips_plan,
                        3 * len(later), name="ag_chips_start", after=g_in)
    wi = g_in.reshape(-1, D)
    wi_a = _pair_major(wi[:NA])

    u1, proj_a, proj_h, proj_f = _in_proj(xs, norm1_g, wi_a, wi, NA, name="in_proj", after=ag1[4])
    attn, lse = _attn_fwd(proj_a, name="attn_fwd")
    later, lands = _copies_wait(ag1, _ag_chips_plan, attn, name="ag_chips_wait", with_srcs=True)
    lands = [lax.dynamic_update_index_in_dim(l, s, me, 0) for l, s in zip(lands, later)]
    ag2 = _copies_start([], lands, _ag_sibling_plan, 4 * len(later), name="ag_sibling_start")
    rec, states = _hgrn_fwd(proj_h, proj_f, hgrn_lb_logits, name="hgrn_fwd", after=ag2[4])
    g_out, g_up, g_down, g_cw = _copies_wait(ag2, _ag_sibling_plan, rec, name="ag_sibling_wait")
    wo = g_out.reshape(-1, D)
    wu = g_up.reshape(-1, D)
    wd = g_down.reshape(-1, D)
    cw = _gathered_cols(g_cw)
    h1, u2, mixed = _out_proj(attn, rec, proj_h, xs, attn_norm_g, hgrn_norm_g, norm2_g, wo, name="out_proj")
    gate, gelu, vslope, act = _up_glu(u2, wu, cw, conv_b, name="up_glu")
    dh2, dh2b, d_fng, loss_part = _down_loss(act, wd, h1, fng, target, name="down_loss")

    dgate, dval, d_cw, d_cb = _dact_glu_bwd(dh2b, wd, gate, gelu, vslope, cw, name="dact_glu_bwd")
    dw_down = _mm_tn([act], dh2b, tm=256, name="dw_down")
    dh1, dh1b, d_n2g = _grad_norm_input([dgate, dval], [(wu, 0)], h1, norm2_g, dh2, name="du2_norm2_bwd")
    F = dgate.shape[1]
    dw_up = _mm_tn([dgate, dval], u2, tm=256, name="dw_up")
    rs_ffn = _rs_start([dw_down.reshape(N_DEV, -1, D), dw_up.reshape(N_DEV, -1, D)], me, name="rs_ffn_start")
    dattn, delta, drec, dhg, d_ang, d_hng = _dmix_post_bwd(dh1b, wo, attn, rec, proj_h, attn_norm_g, hgrn_norm_g,
                                                          name="dmix_post_bwd", after=rs_ffn[0][4])
    dw_out = _mm_tn([mixed], dh1b, name="dw_out")
    rs_out = _rs_start([dw_out.reshape(N_DEV, -1, D)], me, name="rs_out_start")
    dproj_h, d_lbl = _hgrn_bwd(proj_h, proj_f, hgrn_lb_logits, states, drec, name="hgrn_bwd", after=rs_out[0][4])
    small = [("loss", loss_part, None, None, None),
             ("attn_norm_g", d_ang, attn_norm_g, m_attn_norm_g, v_attn_norm_g),
             ("hgrn_norm_g", d_hng, hgrn_norm_g, m_hgrn_norm_g, v_hgrn_norm_g),
             ("hgrn_lb_logits", d_lbl, hgrn_lb_logits, m_hgrn_lb_logits, v_hgrn_lb_logits),
             ("norm2_g", d_n2g, norm2_g, m_norm2_g, v_norm2_g),
             ("conv_b", d_cb, conv_b, m_conv_b, v_conv_b),
             ("final_norm_g", d_fng, final_norm_g, m_final_norm_g, v_final_norm_g)]
    pack = lambda arrs: jnp.concatenate([a.reshape(1, -1) for a in arrs], axis=1)
    small_own = [pack([s[1] for s in small]), d_cw]
    ag_small = _copies_start(small_own, [lax.empty((N_DEV, *s.shape), s.dtype) for s in small_own], _ag_direct_plan,
                             (N_DEV - 1) * len(small_own), name="ag_small_start")
    dproj_a = _attn_bwd(proj_a, dattn, lse, delta, name="attn_bwd")
    pairs = ATTN_W // LANES
    dw_in = _mm_tn([dproj_a], u1, tm=LANES, rows=wi.shape[0], row_block=lambda i: pairs * (i % 3) + i // 3,
                   name="dw_in_attn")
    dw_in = _mm_tn([dproj_h, dhg], u1, tm=256, rows=wi.shape[0], row_block=lambda i: i + NA // 256, into=dw_in,
                   name="dw_in_hgrn")
    rs_in = _rs_start([dw_in.reshape(N_DEV, -1, D)], me, name="rs_in_start", after=ag_small[4])
    grad_x, _, d_n1g = _grad_norm_input([dproj_a, dproj_h, dhg], [(wi_a, 0), (wi, NA)], xs,
                                        norm1_g, dh1, name="du1_norm1_bwd", after=rs_in[0][4])

    res = {}

    def update(nm, parts, w, m, v, transposed=False):
        if transposed:
            raw = _adamw(parts, t(w), t(m), t(v), name=f"adamw_{nm}")
            res[nm] = [r.T[None] for r in raw]
        else:
            raw = res[nm] = _adamw(parts, w, m, v, name=f"adamw_{nm}")
        return raw[1]

    g_down, g_up = _rs_finish(rs_ffn, grad_x, name="rs_ffn_wait")
    update("w_down", g_down, w_down, m_w_down, v_w_down)
    done_up = update("w_up", g_up, w_up, m_w_up, v_w_up, transposed=True)
    (g_out,) = _rs_finish(rs_out, grad_x, name="rs_out_wait")
    update("w_out", g_out, w_out, m_w_out, v_w_out)
    (g_in,) = _rs_finish(rs_in, done_up, name="rs_in_wait")
    done_in = update("w_in", g_in, w_in, m_w_in, v_w_in, transposed=True)

    small_own, small_all = _copies_wait(ag_small, _ag_direct_plan, grad_x, name="ag_small_wait", with_srcs=True)
    g_small, g_dcw = [lax.dynamic_update_index_in_dim(l, s, me, 0) for l, s in zip(small_all, small_own)]
    sm = _adamw_packed(g_small, [s[1].size for s in small], [None if s[2] is None else s[2:] for s in small],
                       name="adamw_small")
    for (nm, *_), r in zip(small, sm):
        res[nm] = r
    ncw = conv_w.shape[-1]
    mine_cw = lax.dynamic_slice_in_dim(g_dcw, me * ncw, ncw, axis=2)
    res["conv_w"] = _adamw([mine_cw], conv_w, m_conv_w, v_conv_w, name="adamw_conv_w")
    late, _ = lax.optimization_barrier((d_n1g, done_in))
    update("norm1_g", _all_gather([late], name="ag_norm1_grad"), norm1_g, m_norm1_g, v_norm1_g)

    loss = res["loss"][0][0, 0]
    order = ["norm1_g", "w_in", "attn_norm_g", "hgrn_norm_g", "hgrn_lb_logits", "w_out", "norm2_g", "w_up",
             "conv_w", "conv_b", "w_down", "final_norm_g"]
    return (loss, grad_x[None], *[res[nm][0] for nm in order], *[res[nm][1] for nm in order],
            *[res[nm][2] for nm in order], *[res[nm][3] for nm in order])
```

```python
import math

import jax
import jax.numpy as jnp
from jax import lax
from jax.experimental import pallas as pl
from jax.experimental.pallas import tpu as pltpu

F32, BF16 = jnp.float32, jnp.bfloat16
NORM_EPS = 1e-6
ATTN_HEADS, HEAD_DIM, ATTN_BLOCK = 8, 64, 128
DILATIONS = (1, 4, 16)
ATTN_SCALE = HEAD_DIM ** -0.5
ATTN_W = ATTN_HEADS * HEAD_DIM
HGRN_HEADS, HGRN_DIM, HGRN_CHUNK = 4, 128, 64
HGRN_W = HGRN_HEADS * HGRN_DIM
ADAM_LR, ADAM_B1, ADAM_B2, ADAM_EPS, ADAM_WD, ADAM_STEP = 0.001, 0.9, 0.999, 1e-08, 0.01, 10
LANES, SUBLANES = 128, 8
VMEM_LIMIT_BYTES = 56 * 1024 * 1024
N_DEV = 8
MESH = pl.DeviceIdType.MESH
HBM = pl.BlockSpec(memory_space=pltpu.HBM)
HIGHEST = lax.Precision.HIGHEST


def _cparams(*sem):
    return pltpu.CompilerParams(dimension_semantics=sem, vmem_limit_bytes=VMEM_LIMIT_BYTES)


def _tile(n, pref):
    if n <= pref:
        return n
    t = (pref // LANES) * LANES
    while n % t:
        t -= LANES
    return t


def _resident(shape):
    return pl.BlockSpec(shape, lambda *_: (0,) * len(shape), pipeline_mode=pl.Buffered(1))


def _dot(a, b, dims, precision=None):
    return lax.dot_general(a, b, (dims, ((), ())), precision=precision, preferred_element_type=F32)


def _nn(a, b, precision=None):
    return _dot(a, b, ((1,), (0,)), precision)


def _nt(a, b):
    return _dot(a, b, ((1,), (1,)))


def _tn(a, b):
    return _dot(a, b, ((0,), (0,)))


def _sigmoid(x):
    return 1.0 / (1.0 + jnp.exp(-x))


def _rstd(x):
    return lax.rsqrt(jnp.mean(x * x, axis=-1, keepdims=True) + NORM_EPS)


def _norm_bwd(x, g, du):
    r = _rstd(x)
    xh = x * r
    dxh = du * g
    return r * (dxh - xh * jnp.mean(dxh * xh, axis=-1, keepdims=True)), du * xh


def _row_halves(tm):
    return [pl.ds(0, tm // 2), pl.ds(tm // 2, tm // 2)]


def _behind(after, n_in):
    if after is None:
        return [], [], (lambda body: body)
    return ([after], [pl.BlockSpec(memory_space=pl.ANY)],
            lambda body: (lambda *refs: body(*refs[:n_in], *refs[n_in + 1:])))


def _accumulate(ref, part, first):
    @pl.when(first)
    def _():
        ref[...] = part

    @pl.when(jnp.logical_not(first))
    def _():
        ref[...] += part


def _mm_tn(xs, dy, *, name, tm=512, tn=1024, rows=None, row_block=None, into=None):
    S, N = dy.shape
    tm = _tile(math.gcd(*[x.shape[1] for x in xs]), tm)
    tn = _tile(N, tn)
    blocks = [x.shape[1] // tm for x in xs]
    first = [sum(blocks[:k]) for k in range(len(xs))]
    row_block = row_block or (lambda i: i)
    n = len(xs)

    def body(*refs):
        x_refs, dy_ref = refs[:n], refs[n]
        o_ref, xt_ref = refs[-2:]
        @pl.when(pl.program_id(1) == 0)
        def _():
            xv = x_refs[0][...]
            for x_ref, b0 in zip(x_refs[1:], first[1:]):
                xv = jnp.where(pl.program_id(0) >= b0, x_ref[...], xv)
            xt_ref[...] = xv.T

        o_ref[...] = _nn(xt_ref[...], dy_ref[...]).astype(BF16)

    def x_spec(b0, nb):
        return pl.BlockSpec((S, tm), lambda i, j: (0, jnp.clip(i - b0, 0, nb - 1)))

    operands = [*xs, dy] + ([] if into is None else [into])
    return pl.pallas_call(
        body, name=name, grid=(sum(blocks), N // tn),
        in_specs=[x_spec(b0, nb) for b0, nb in zip(first, blocks)] + [pl.BlockSpec((S, tn), lambda i, j: (0, j))]
        + ([] if into is None else [pl.BlockSpec(memory_space=pl.ANY)]),
        out_specs=pl.BlockSpec((tm, tn), lambda i, j: (row_block(i), j)),
        out_shape=jax.ShapeDtypeStruct((rows or sum(blocks) * tm, N), BF16),
        input_output_aliases={} if into is None else {n + 1: 0},
        scratch_shapes=[pltpu.VMEM((tm, S), BF16)], compiler_params=_cparams("parallel", "arbitrary"),
    )(*operands)


def _in_proj(x, g, wt_attn, wt, row0, *, name, tm=512, after=None):
    S, D = x.shape
    NA, NH, W = wt_attn.shape[0], wt.shape[0] - row0, HGRN_W

    def body(x_ref, g_ref, wa_ref, w_ref, u_ref, a_ref, h_ref, f_ref):
        xv = x_ref[...]
        u = (xv * _rstd(xv) * g_ref[...]).astype(BF16)
        u_ref[...] = u
        a_ref[...] = _nt(u, wa_ref[...]).astype(BF16)
        ph = _nt(u, w_ref[row0:row0 + NH, :])
        h_ref[...] = ph.astype(BF16)
        f_ref[...] = ph[:, W:2 * W]

    row = lambda w: pl.BlockSpec((tm, w), lambda i: (i, 0))
    extra, extra_specs, adapt = _behind(after, 4)
    return pl.pallas_call(
        adapt(body), name=name, grid=(S // tm,),
        in_specs=[row(D), pl.BlockSpec((1, D), lambda i: (0, 0)), _resident(wt_attn.shape), _resident(wt.shape)]
        + extra_specs,
        out_specs=[row(D), row(NA), row(NH), row(W)],
        out_shape=[jax.ShapeDtypeStruct((S, D), BF16), jax.ShapeDtypeStruct((S, NA), BF16),
                   jax.ShapeDtypeStruct((S, NH), BF16), jax.ShapeDtypeStruct((S, W), F32)],
        compiler_params=_cparams("parallel"),
    )(x, g, wt_attn, wt, *extra)


PAIR_W = 3 * LANES
ATTN_UNROLL_FWD, ATTN_UNROLL_BWD = 8, 4


def _attn_masks(first):
    qi = lax.broadcasted_iota(jnp.int32, (ATTN_BLOCK, 2 * ATTN_BLOCK), 0)
    kj = lax.broadcasted_iota(jnp.int32, (ATTN_BLOCK, 2 * ATTN_BLOCK), 1)
    dist = qi + ATTN_BLOCK - kj
    valid = (dist >= 0) & (dist <= ATTN_BLOCK) & jnp.logical_or(kj >= ATTN_BLOCK, jnp.logical_not(first))
    lane = lax.broadcasted_iota(jnp.int32, (1, LANES), 1)
    return valid, lane


def _for_residue_blocks(S, d, fn):
    span = ATTN_BLOCK * d
    nb = S // span

    def step(n, carry):
        base = pl.multiple_of(n * span, span)
        for r in range(d):
            off = pl.multiple_of((r * nb + n) * ATTN_BLOCK, ATTN_BLOCK)
            fn(lambda ref, r=r: _block_rows(ref, base, r, d),
               lambda ref, val, r=r: _set_block_rows(ref, base, r, d, val), off)
        return carry

    lax.fori_loop(0, nb, step, 0)


def _for_blocks(S, unroll, fn):
    def step(i, carry):
        fn([(pl.multiple_of((i * unroll + u) * ATTN_BLOCK, ATTN_BLOCK), i * unroll + u) for u in range(unroll)])
        return carry

    lax.fori_loop(0, S // ATTN_BLOCK // unroll, step, 0)


def _head_value(x2, lane, e):
    return jnp.sum(jnp.where(lane == HEAD_DIM * e, x2, 0.0), axis=-1, keepdims=True)


def _block_rows(ref, base, r, d):
    if d == 1:
        return ref[pl.ds(base, ATTN_BLOCK), :]
    return ref.at[pl.ds(base, ATTN_BLOCK * d)][pl.ds(r, ATTN_BLOCK, stride=d), :]


def _set_block_rows(ref, base, r, d, val):
    if d == 1:
        ref[pl.ds(base, ATTN_BLOCK), :] = val
    else:
        ref.at[pl.ds(base, ATTN_BLOCK * d)][pl.ds(r, ATTN_BLOCK, stride=d), :] = val


def _order4_to_16(src, dst, pad):
    S = src.shape[0]
    q4, q16 = S // 4, S // 16
    for r in range(4):
        for a in range(4):
            for n in range(q16 // ATTN_BLOCK):
                rows = src.at[pl.ds(r * q4 + 4 * ATTN_BLOCK * n, 4 * ATTN_BLOCK)][pl.ds(a, ATTN_BLOCK, stride=4), :]
                dst[pl.ds(pad + (4 * a + r) * q16 + ATTN_BLOCK * n, ATTN_BLOCK), :] = rows.astype(dst.dtype)


def _order16_to_4(src, pad, dst):
    S = dst.shape[0]
    q4, q16 = S // 4, S // 16
    for r in range(4):
        for a in range(4):
            for n in range(q16 // ATTN_BLOCK):
                rows = src[pl.ds(pad + (4 * a + r) * q16 + ATTN_BLOCK * n, ATTN_BLOCK), :]
                dst.at[pl.ds(r * q4 + 4 * ATTN_BLOCK * n, 4 * ATTN_BLOCK)][pl.ds(a, ATTN_BLOCK, stride=4), :] = rows


def _regroup(S, d, pairs, tmp):
    for src, dst, pad in pairs:
        if d == 16:
            def to_tmp(rows, _, off, src=src):
                tmp[pl.ds(off, ATTN_BLOCK), :] = rows(src)

            _for_residue_blocks(S, 4, to_tmp)
            _order4_to_16(tmp, dst, pad)
    if d != 16:
        def to_dst(rows, _, off):
            for src, dst, pad in pairs:
                dst[pl.ds(pad + off, ATTN_BLOCK), :] = rows(src).astype(dst.dtype)

        _for_residue_blocks(S, d, to_dst)


def _split_pair(p_ref, qs, ks, vs, bk, bv):
    qs[...] = p_ref[:, 0:LANES].astype(F32) * ATTN_SCALE
    ks[...] = p_ref[:, LANES:2 * LANES].astype(F32)
    vs[...] = p_ref[:, 2 * LANES:3 * LANES].astype(F32)
    bk[0:ATTN_BLOCK, :] = jnp.zeros((ATTN_BLOCK, LANES), bk.dtype)
    bv[0:ATTN_BLOCK, :] = jnp.zeros((ATTN_BLOCK, LANES), bv.dtype)


def _attn_fwd(proj_a, *, name):
    S = proj_a.shape[0]

    def body(p_ref, o_ref, l_ref, qs, ks, vs, bq, bk, bv, bo, bl, to, tl):
        _split_pair(p_ref, qs, ks, vs, bk, bv)
        for d in DILATIONS:
            nb = S // (ATTN_BLOCK * d)
            _regroup(S, d, ((qs, bq, 0), (ks, bk, ATTN_BLOCK), (vs, bv, ATTN_BLOCK)), to)

            def blocks(group, nb=nb):
                lane = lax.broadcasted_iota(jnp.int32, (1, LANES), 1)
                heads = [(lane >= HEAD_DIM * e) & (lane < HEAD_DIM * (e + 1)) for e in range(LANES // HEAD_DIM)]
                wins = [pl.ds(off, 2 * ATTN_BLOCK) for off, _ in group]
                s = [[_nt(jnp.where(mh, bq[pl.ds(off, ATTN_BLOCK), :], jnp.zeros((ATTN_BLOCK, LANES), BF16)), bk[win, :])
                      for mh in heads] for (off, _), win in zip(group, wins)]
                p, m, l = [], [], []
                for (off, b), su in zip(group, s):
                    valid, _ = _attn_masks(jnp.bitwise_and(b, nb - 1) == 0)
                    sm = [jnp.where(valid, x, -jnp.inf) for x in su]
                    m.append([jnp.max(x, axis=-1, keepdims=True) for x in sm])
                    p.append([jnp.exp(x - mx) for x, mx in zip(sm, m[-1])])
                    l.append([jnp.sum(x, axis=-1, keepdims=True) for x in p[-1]])
                o = [[_nn(x.astype(BF16), bv[win, :]) for x in pu] for pu, win in zip(p, wins)]
                for (off, _), ou, mu, lu in zip(group, o, m, l):
                    o2 = jnp.zeros((ATTN_BLOCK, LANES), F32)
                    l2 = jnp.zeros((ATTN_BLOCK, LANES), F32)
                    for mh, oe, me_, le in zip(heads, ou, mu, lu):
                        o2 = jnp.where(mh, oe / le, o2)
                        l2 = jnp.where(mh, me_ + jnp.log(le), l2)
                    bo[pl.ds(off, ATTN_BLOCK), :] = o2
                    bl[pl.ds(off, ATTN_BLOCK), :] = l2

            _for_blocks(S, ATTN_UNROLL_FWD, blocks)

            if d == 16:
                _order16_to_4(bo, 0, to)
                _order16_to_4(bl, 0, tl)
            src_o, src_l = (to, tl) if d == 16 else (bo, bl)

            def merge(rows, set_rows, off, d=d, src_o=src_o, src_l=src_l):
                blk = pl.ds(off, ATTN_BLOCK)
                o2, l2 = src_o[blk, :], src_l[blk, :]
                if d != DILATIONS[0]:
                    lo, oo = rows(l_ref), rows(o_ref)
                    ln = jnp.maximum(lo, l2)
                    wa, wb = jnp.exp(lo - ln), jnp.exp(l2 - ln)
                    o2 = (wa * oo + wb * o2) / (wa + wb)
                    l2 = ln + jnp.log(wa + wb)
                set_rows(o_ref, o2)
                set_rows(l_ref, l2)

            _for_residue_blocks(S, min(d, 4), merge)

    slab = pl.BlockSpec((S, LANES), lambda p: (0, p))
    f32_slab, bf16_slab = pltpu.VMEM((S, LANES), F32), pltpu.VMEM((S, LANES), BF16)
    bf16_window = pltpu.VMEM((S + ATTN_BLOCK, LANES), BF16)
    return pl.pallas_call(
        body, name=name, grid=(ATTN_W // LANES,), in_specs=[pl.BlockSpec((S, PAIR_W), lambda p: (0, p))],
        out_specs=[slab, slab],
        out_shape=[jax.ShapeDtypeStruct((S, ATTN_W), F32), jax.ShapeDtypeStruct((S, ATTN_W), F32)],
        scratch_shapes=[f32_slab] * 3 + [bf16_slab, bf16_window, bf16_window] + [f32_slab] * 4,
        compiler_params=_cparams("parallel"),
    )(proj_a)


def _attn_bwd(proj_a, do, lse, delta, *, name):
    S = proj_a.shape[0]

    def body(p_ref, do_ref, lse_ref, dl_ref, o_ref, qs, ks, vs, dqs, dks, dvs, bq, bk, bv, bdo, blse, bdl, bdq, bdk, bdv,
             tmp):
        _split_pair(p_ref, qs, ks, vs, bk, bv)
        bdk[0:ATTN_BLOCK, :] = jnp.zeros((ATTN_BLOCK, LANES), F32)
        bdv[0:ATTN_BLOCK, :] = jnp.zeros((ATTN_BLOCK, LANES), F32)
        for d in DILATIONS:
            nb = S // (ATTN_BLOCK * d)
            _regroup(S, d, ((qs, bq, 0), (ks, bk, ATTN_BLOCK), (vs, bv, ATTN_BLOCK), (do_ref, bdo, 0),
                            (lse_ref, blse, 0), (dl_ref, bdl, 0)), tmp)

            def blocks(group, nb=nb):
                lane = lax.broadcasted_iota(jnp.int32, (1, LANES), 1)
                heads = [(lane >= HEAD_DIM * e) & (lane < HEAD_DIM * (e + 1)) for e in range(LANES // HEAD_DIM)]
                zero = jnp.zeros((ATTN_BLOCK, LANES), BF16)
                chains = [(off, b, e, mh) for off, b in group for e, mh in enumerate(heads)]
                qm = [jnp.where(mh, bq[pl.ds(off, ATTN_BLOCK), :], zero) for off, _, _, mh in chains]
                dom = [jnp.where(mh, bdo[pl.ds(off, ATTN_BLOCK), :], zero) for off, _, _, mh in chains]
                s = [_nt(x, bk[pl.ds(off, 2 * ATTN_BLOCK), :]) for x, (off, _, _, _) in zip(qm, chains)]
                dp = [_nt(x, bv[pl.ds(off, 2 * ATTN_BLOCK), :]) for x, (off, _, _, _) in zip(dom, chains)]
                p, ds = [], []
                for (off, b, e, _), sc, dpc in zip(chains, s, dp):
                    valid, _ = _attn_masks(jnp.bitwise_and(b, nb - 1) == 0)
                    blk = pl.ds(off, ATTN_BLOCK)
                    pc = jnp.where(valid, jnp.exp(sc - _head_value(blse[blk, :], lane, e)), 0.0)
                    ds.append((pc * (dpc - _head_value(bdl[blk, :], lane, e))).astype(BF16))
                    p.append(pc.astype(BF16))
                dq = [_nn(x, bk[pl.ds(off, 2 * ATTN_BLOCK), :]) for x, (off, _, _, _) in zip(ds, chains)]
                dk = [_tn(x, y) for x, y in zip(ds, qm)]
                dv = [_tn(x, y) for x, y in zip(p, dom)]
                nh = len(heads)
                for u, (off, _) in enumerate(group):
                    dq2 = jnp.zeros((ATTN_BLOCK, LANES), F32)
                    for mh, x in zip(heads, dq[nh * u:nh * (u + 1)]):
                        dq2 = jnp.where(mh, x, dq2)
                    bdq[pl.ds(off, ATTN_BLOCK), :] = dq2 * ATTN_SCALE
                    for acc, grads in ((bdk, dk), (bdv, dv)):
                        win_grad = sum(grads[nh * u + 1:nh * (u + 1)], grads[nh * u])
                        acc[pl.ds(off, ATTN_BLOCK), :] += win_grad[:ATTN_BLOCK]
                        acc[pl.ds(off + ATTN_BLOCK, ATTN_BLOCK), :] = win_grad[ATTN_BLOCK:]

            _for_blocks(S, ATTN_UNROLL_BWD, blocks)

            outs = ((dqs, bdq, 0), (dks, bdk, ATTN_BLOCK), (dvs, bdv, ATTN_BLOCK))
            if d == 16:
                for acc, grad, pad in outs:
                    _order16_to_4(grad, pad, tmp)

                    def add(rows, set_rows, off, acc=acc):
                        set_rows(acc, rows(acc) + tmp[pl.ds(off, ATTN_BLOCK), :])

                    _for_residue_blocks(S, 4, add)
            else:
                def scatter(rows, set_rows, off, d=d):
                    for acc, grad, pad in outs:
                        part = grad[pl.ds(pad + off, ATTN_BLOCK), :]
                        set_rows(acc, part if d == DILATIONS[0] else rows(acc) + part)

                _for_residue_blocks(S, d, scatter)
        o_ref[:, 0:LANES] = dqs[...].astype(BF16)
        o_ref[:, LANES:2 * LANES] = dks[...].astype(BF16)
        o_ref[:, 2 * LANES:3 * LANES] = dvs[...].astype(BF16)

    slab = pl.BlockSpec((S, LANES), lambda p: (0, p))
    pair = pl.BlockSpec((S, PAIR_W), lambda p: (0, p))
    f32_slab, bf16_slab = pltpu.VMEM((S, LANES), F32), pltpu.VMEM((S, LANES), BF16)
    f32_window, bf16_window = pltpu.VMEM((S + ATTN_BLOCK, LANES), F32), pltpu.VMEM((S + ATTN_BLOCK, LANES), BF16)
    return pl.pallas_call(
        body, name=name, grid=(ATTN_W // LANES,), in_specs=[pair, slab, slab, slab], out_specs=pair,
        out_shape=jax.ShapeDtypeStruct(proj_a.shape, BF16),
        scratch_shapes=[f32_slab] * 6 + [bf16_slab, bf16_window, bf16_window, bf16_slab, f32_slab, f32_slab,
                                         f32_slab, f32_window, f32_window, f32_slab],
        compiler_params=_cparams("parallel"),
    )(proj_a, do, lse, delta)


HG_T = 2 * HGRN_CHUNK
HG_GROUPS = 2
HG_STEP = HG_GROUPS * HG_T


def _hgrn_consts():
    row = lax.broadcasted_iota(jnp.int32, (HG_T, HG_T), 0)
    col = lax.broadcasted_iota(jnp.int32, (HG_T, HG_T), 1)
    same = (row >= HGRN_CHUNK) == (col >= HGRN_CHUNK)
    return row, same & (col <= row), same & (col >= row)


def _lower_bound(logits_ref):
    l0, l1 = logits_ref[0:1, :], logits_ref[1:2, :]
    mx = jnp.maximum(l0, l1)
    e0, e1 = jnp.exp(l0 - mx), jnp.exp(l1 - mx)
    return e0 / (e0 + e1)


def _hgrn_chains():
    chains = [(g, h) for g in range(HG_GROUPS) for h in range(HGRN_HEADS)]
    rows = [pl.ds(HG_T * g, HG_T) for g, _ in chains]
    lanes = [slice(HGRN_DIM * h, HGRN_DIM * (h + 1)) for _, h in chains]
    return chains, rows, lanes


def _hgrn_gates(qs, fs, lbs, row, causal):
    C = HGRN_CHUNK
    tri = jnp.where(causal, 1.0, 0.0).astype(F32)
    sgs = [_sigmoid(f) for f in fs]
    forgets = [lb + (1.0 - lb) * sg for lb, sg in zip(lbs, sgs)]
    logfs = [jnp.log(forget) for forget in forgets]
    bs = [_nn(tri, logf, HIGHEST) for logf in logfs]
    out = []
    for q, sg, forget, logf, b in zip(qs, sgs, forgets, logfs, bs):
        key = 1.0 - forget
        bend0 = jnp.sum(logf[:C], axis=0, keepdims=True)
        bend1 = jnp.sum(logf[C:], axis=0, keepdims=True)
        bend = jnp.where(row < C, bend0, bend1)
        eb, emb, eend = jnp.exp(b), jnp.exp(-b), jnp.exp(bend - b)
        sq = _sigmoid(q)
        out.append(dict(sg=sg, forget=forget, key=key, bend0=bend0, bend1=bend1, eb=eb, emb=emb, eend=eend, sq=sq,
                        qd=q * sq * eb, ki=key * emb, ke=key * eend))
    return out


def _hgrn_fwd(proj, proj_f, logits, *, name, after=None):
    S = proj.shape[0]
    W, C = HGRN_W, HGRN_CHUNK

    def body(q_ref, f_ref, i_ref, lg_ref, rec_ref, st_ref, s_ref):
        @pl.when(pl.program_id(0) == 0)
        def _():
            s_ref[...] = jnp.zeros_like(s_ref)

        row, causal, _ = _hgrn_consts()
        lb_all = _lower_bound(lg_ref)
        chains, rows, lanes = _hgrn_chains()
        n = range(len(chains))
        gts = _hgrn_gates([q_ref[rows[c], lanes[c]].astype(F32) for c in n], [f_ref[rows[c], lanes[c]] for c in n],
                          [lb_all[:, lanes[c]] for c in n], row, causal)
        qd, ki, ke = ([gt[k].astype(BF16) for gt in gts] for k in ("qd", "ki", "ke"))
        iv = [i_ref[rows[c], lanes[c]].astype(BF16) for c in n]
        a = [_nt(qd[c], ki[c]) for c in n]
        u0 = [_tn(iv[c][:C], ke[c][:C]) for c in n]
        u1 = [_tn(iv[c][C:], ke[c][C:]) for c in n]
        state = [s_ref[h] for h in range(HGRN_HEADS)]
        s0, s1 = [], []
        for c, (g, h) in enumerate(chains):
            s0.append(state[h])
            s1.append(jnp.exp(gts[c]["bend0"]) * s0[c] + u0[c])
            state[h] = jnp.exp(gts[c]["bend1"]) * s1[c] + u1[c]
        o0 = [_nt(qd[c][:C], s0[c].astype(BF16)) for c in n]
        o1 = [_nt(qd[c][C:], s1[c].astype(BF16)) for c in n]
        o = [_nn(jnp.where(causal, a[c], 0.0).astype(BF16), iv[c]) for c in n]
        for c, (g, h) in enumerate(chains):
            st_ref[2 * g, h] = s0[c]
            st_ref[2 * g + 1, h] = s1[c]
            rec_ref[rows[c], lanes[c]] = o[c] + jnp.concatenate([o0[c], o1[c]], axis=0)
        for h in range(HGRN_HEADS):
            s_ref[h] = state[h]

    blk = lambda j: pl.BlockSpec((HG_STEP, W), lambda t: (t, j))
    extra, extra_specs, adapt = _behind(after, 4)
    return pl.pallas_call(
        adapt(body), name=name, grid=(S // HG_STEP,),
        in_specs=[blk(0), blk(0), blk(2), pl.BlockSpec((2, W), lambda t: (0, 0))] + extra_specs,
        out_specs=[blk(0), pl.BlockSpec((2 * HG_GROUPS, HGRN_HEADS, HGRN_DIM, HGRN_DIM), lambda t: (t, 0, 0, 0))],
        out_shape=[jax.ShapeDtypeStruct((S, W), F32),
                   jax.ShapeDtypeStruct((S // C, HGRN_HEADS, HGRN_DIM, HGRN_DIM), F32)],
        scratch_shapes=[pltpu.VMEM((HGRN_HEADS, HGRN_DIM, HGRN_DIM), F32)],
        compiler_params=_cparams("arbitrary"),
    )(proj, proj_f, proj, logits, *extra)


def _hgrn_bwd(proj, proj_f, logits, states, drec, *, name, after=None):
    S = proj.shape[0]
    W, C = HGRN_W, HGRN_CHUNK
    nt = S // HG_STEP

    def body(q_ref, f_ref, i_ref, lg_ref, st_ref, do_ref, dp_ref, dlg_ref, ds_ref, dlb_ref):
        t = pl.program_id(0)

        @pl.when(t == 0)
        def _():
            ds_ref[...] = jnp.zeros_like(ds_ref)
            dlb_ref[...] = jnp.zeros_like(dlb_ref)

        row, causal, anti = _hgrn_consts()
        lb_all = _lower_bound(lg_ref)
        chains, rows, lanes = _hgrn_chains()
        n = range(len(chains))
        qs, lbs = [q_ref[rows[c], lanes[c]].astype(F32) for c in n], [lb_all[:, lanes[c]] for c in n]
        gts = _hgrn_gates(qs, [f_ref[rows[c], lanes[c]] for c in n], lbs, row, causal)
        qd, ki, ke = ([gt[k] for gt in gts] for k in ("qd", "ki", "ke"))
        qdb, kib, keb = ([x.astype(BF16) for x in xs] for xs in (qd, ki, ke))
        iv = [i_ref[rows[c], lanes[c]].astype(BF16) for c in n]
        dob = [do_ref[rows[c], lanes[c]].astype(BF16) for c in n]
        s0 = [st_ref[2 * g, h] for g, h in chains]
        s1 = [st_ref[2 * g + 1, h] for g, h in chains]
        dec0, dec1 = [jnp.exp(gt["bend0"]) for gt in gts], [jnp.exp(gt["bend1"]) for gt in gts]
        a = [_nt(qdb[c], kib[c]) for c in n]
        da = [_nt(dob[c], iv[c]) for c in n]
        dqd1 = [_nn(dob[c][C:], s1[c].astype(BF16)) for c in n]
        dqd0 = [_nn(dob[c][:C], s0[c].astype(BF16)) for c in n]
        t1 = [_tn(dob[c][C:], qdb[c][C:]) for c in n]
        t0 = [_tn(dob[c][:C], qdb[c][:C]) for c in n]
        carry = [ds_ref[h] for h in range(HGRN_HEADS)]
        ds1, ds0 = [None] * len(chains), [None] * len(chains)
        for c in reversed(n):
            h = chains[c][1]
            ds1[c] = carry[h]
            ds0[c] = dec1[c] * ds1[c] + t1[c]
            carry[h] = dec0[c] * ds0[c] + t0[c]
        for h in range(HGRN_HEADS):
            ds_ref[h] = carry[h]
        ds1b, ds0b = [x.astype(BF16) for x in ds1], [x.astype(BF16) for x in ds0]
        a = [jnp.where(causal, x, 0.0).astype(BF16) for x in a]
        da = [jnp.where(causal, x, 0.0).astype(BF16) for x in da]
        di1 = [_nt(keb[c][C:], ds1b[c]) for c in n]
        dke1 = [_nn(iv[c][C:], ds1b[c]) for c in n]
        di0 = [_nt(keb[c][:C], ds0b[c]) for c in n]
        dke0 = [_nn(iv[c][:C], ds0b[c]) for c in n]
        dqd_a = [_nn(da[c], kib[c]) for c in n]
        dki = [_tn(da[c], qdb[c]) for c in n]
        di_a = [_tn(a[c], dob[c]) for c in n]
        dqd, dke, db = [], [], []
        for c in n:
            ddec1 = jnp.sum(ds1[c] * s1[c], axis=0, keepdims=True)
            ddec0 = jnp.sum(ds0[c] * s0[c], axis=0, keepdims=True)
            dqd.append(dqd_a[c] + jnp.concatenate([dqd0[c], dqd1[c]], axis=0))
            h = chains[c][1]
            dp_ref[rows[c], 2 * W + HGRN_DIM * h:2 * W + HGRN_DIM * (h + 1)] = (
                di_a[c] + jnp.concatenate([di0[c], di1[c]], axis=0)).astype(BF16)
            dke.append(jnp.concatenate([dke0[c], dke1[c]], axis=0))
            gke = dke[c] * ke[c]
            dbend0 = jnp.sum(gke[:C], axis=0, keepdims=True) + ddec0 * dec0[c]
            dbend1 = jnp.sum(gke[C:], axis=0, keepdims=True) + ddec1 * dec1[c]
            dbc = dqd[c] * qd[c] - dki[c] * ki[c] - gke
            db.append(dbc + jnp.where(row == C - 1, dbend0, 0.0) + jnp.where(row == HG_T - 1, dbend1, 0.0))
        tri = jnp.where(anti, 1.0, 0.0).astype(F32)
        dlogf = [_nn(tri, db[c], HIGHEST) for c in n]
        for c in n:
            gt, lb, q, h = gts[c], lbs[c], qs[c], chains[c][1]
            dforget = dlogf[c] / gt["forget"] - (dki[c] * gt["emb"] + dke[c] * gt["eend"])
            sg, sq = gt["sg"], gt["sq"]
            dp_ref[rows[c], W + HGRN_DIM * h:W + HGRN_DIM * (h + 1)] = (
                dforget * (1.0 - lb) * sg * (1.0 - sg)).astype(BF16)
            dlb_ref[:, lanes[c]] += jnp.sum(dforget * (1.0 - sg), axis=0, keepdims=True)
            dp_ref[rows[c], lanes[c]] = (dqd[c] * gt["eb"] * sq * (1.0 + q * (1.0 - sq))).astype(BF16)

        @pl.when(t == nt - 1)
        def _():
            dl0 = dlb_ref[...] * lb_all * (1.0 - lb_all)
            dlg_ref[0:1, :] = dl0
            dlg_ref[1:2, :] = -dl0

    blk = lambda j: pl.BlockSpec((HG_STEP, W), lambda t: (nt - 1 - t, j))
    full = pl.BlockSpec((2, W), lambda t: (0, 0))
    extra, extra_specs, adapt = _behind(after, 6)
    return pl.pallas_call(
        adapt(body), name=name, grid=(nt,),
        in_specs=[blk(0), blk(0), blk(2), full,
                  pl.BlockSpec((2 * HG_GROUPS, HGRN_HEADS, HGRN_DIM, HGRN_DIM), lambda t: (nt - 1 - t, 0, 0, 0)), blk(0)]
        + extra_specs,
        out_specs=[pl.BlockSpec((HG_STEP, 3 * W), lambda t: (nt - 1 - t, 0)), full],
        out_shape=[jax.ShapeDtypeStruct((S, 3 * W), BF16), jax.ShapeDtypeStruct((2, W), F32)],
        scratch_shapes=[pltpu.VMEM((HGRN_HEADS, HGRN_DIM, HGRN_DIM), F32), pltpu.VMEM((1, W), F32)],
        compiler_params=_cparams("arbitrary"),
    )(proj, proj_f, proj, logits, states, drec, *extra)


def _out_proj(attn, rec, proj_h, x, g_attn, g_hgrn, g_norm2, w_out, *, name, tm=512):
    S, D = x.shape
    AW, W = ATTN_W, HGRN_W

    def body(a_ref, r_ref, hg_ref, x_ref, ga_ref, gh_ref, g2_ref, w_ref, h_ref, u_ref, m_ref):
        av = a_ref[...]
        m_ref[:, :AW] = (av * _rstd(av) * ga_ref[...]).astype(BF16)
        for h in range(HGRN_HEADS):
            sl = slice(HGRN_DIM * h, HGRN_DIM * (h + 1))
            rv, hg = r_ref[:, sl], hg_ref[:, sl].astype(F32)
            m_ref[:, AW + HGRN_DIM * h:AW + HGRN_DIM * (h + 1)] = (
                (rv * _rstd(rv) * gh_ref[:, sl]) * (hg * _sigmoid(hg))).astype(BF16)
        h1 = x_ref[...] + _nn(m_ref[...], w_ref[...])
        h_ref[...] = h1
        u_ref[...] = (h1 * _rstd(h1) * g2_ref[...]).astype(BF16)

    row = lambda w, j=0: pl.BlockSpec((tm, w), lambda i: (i, j))
    vec = lambda w: pl.BlockSpec((1, w), lambda i: (0, 0))
    return pl.pallas_call(
        body, name=name, grid=(S // tm,),
        in_specs=[row(AW), row(W), row(W, 3), row(D), vec(AW), vec(W), vec(D), _resident(w_out.shape)],
        out_specs=[row(D), row(D), row(AW + W)],
        out_shape=[jax.ShapeDtypeStruct((S, D), F32), jax.ShapeDtypeStruct((S, D), BF16),
                   jax.ShapeDtypeStruct((S, AW + W), BF16)],
        compiler_params=_cparams("parallel"),
    )(attn, rec, proj_h, x, g_attn, g_hgrn, g_norm2, w_out)


def _dmix_post_bwd(dh1b, w_out, attn, rec, proj_h, g_attn, g_hgrn, *, name, tm=512, after=None):
    S, D = dh1b.shape
    AW, W = ATTN_W, HGRN_W

    def body(dh_ref, w_ref, a_ref, r_ref, hg_ref, ga_ref, gh_ref, do_ref, dl_ref, dr_ref, dhg_ref, dga_ref, dgh_ref):
        first = pl.program_id(0) == 0
        dmix = _nt(dh_ref[...], w_ref[...])
        av = a_ref[...]
        dov, dga = _norm_bwd(av, ga_ref[...], dmix[:, :AW])
        do_ref[...] = dov
        shift = HEAD_DIM.bit_length() - 1
        hi = lax.shift_right_logical(lax.broadcasted_iota(jnp.int32, (AW, AW), 0), shift)
        hj = lax.shift_right_logical(lax.broadcasted_iota(jnp.int32, (AW, AW), 1), shift)
        prod = dov * av
        hi_part = prod.astype(BF16)
        lo_part = (prod - hi_part.astype(F32)).astype(BF16)
        same_head = jnp.where(hi == hj, 1.0, 0.0).astype(BF16)
        dl_ref[...] = _nn(hi_part, same_head) + _nn(lo_part, same_head)
        _accumulate(dga_ref, jnp.sum(dga, axis=0, keepdims=True), first)

        @pl.when(first)
        def _():
            dgh_ref[...] = jnp.zeros_like(dgh_ref)

        for h in range(HGRN_HEADS):
            sl = slice(HGRN_DIM * h, HGRN_DIM * (h + 1))
            rv, hg, gv = r_ref[:, sl], hg_ref[:, sl].astype(F32), gh_ref[:, sl]
            dout = dmix[:, AW + HGRN_DIM * h:AW + HGRN_DIM * (h + 1)]
            sg = _sigmoid(hg)
            drv, dgh = _norm_bwd(rv, gv, dout * (hg * sg))
            dr_ref[:, sl] = drv
            dgh_ref[:, sl] += jnp.sum(dgh, axis=0, keepdims=True)
            dhg_ref[:, sl] = (dout * (rv * _rstd(rv) * gv) * (sg * (1.0 + hg * (1.0 - sg)))).astype(BF16)

    row = lambda w, j=0: pl.BlockSpec((tm, w), lambda i: (i, j))
    vec = lambda w: pl.BlockSpec((1, w), lambda i: (0, 0))
    extra, extra_specs, adapt = _behind(after, 7)
    return pl.pallas_call(
        adapt(body), name=name, grid=(S // tm,),
        in_specs=[row(D), _resident(w_out.shape), row(AW), row(W), row(W, 3), vec(AW), vec(W)] + extra_specs,
        out_specs=[row(AW), row(AW), row(W), row(W), vec(AW), vec(W)],
        out_shape=[jax.ShapeDtypeStruct((S, AW), F32), jax.ShapeDtypeStruct((S, AW), F32),
                   jax.ShapeDtypeStruct((S, W), F32), jax.ShapeDtypeStruct((S, W), BF16),
                   jax.ShapeDtypeStruct((1, AW), F32), jax.ShapeDtypeStruct((1, W), F32)],
        compiler_params=_cparams("arbitrary"),
    )(dh1b, w_out, attn, rec, proj_h, g_attn, g_hgrn, *extra)


def _conv_act(g, g1, g2, w_ref, b_ref):
    c = b_ref[...] + w_ref[0:1, :] * g2 + w_ref[1:2, :] * g1 + w_ref[2:3, :] * g
    return c, 0.5 * (1.0 + lax.erf(c * (2.0 ** -0.5)))


def _shift_down(g, halo, row):
    g1 = jnp.where(row == 0, halo[7:8], pltpu.roll(g, 1, 0))
    g2 = jnp.where(row == 0, halo[6:7], jnp.where(row == 1, halo[7:8], pltpu.roll(g, 2, 0)))
    return g1, g2


def _shift_up(x, halo, row):
    n = x.shape[0]
    x1 = jnp.where(row == n - 1, halo[0:1], pltpu.roll(x, n - 1, 0))
    x2 = jnp.where(row == n - 2, halo[0:1], jnp.where(row == n - 1, halo[1:2], pltpu.roll(x, n - 2, 0)))
    return x1, x2


def _up_glu(u, wt_up, conv_w, conv_b, *, name, tm=1024, tn=1408):
    S, D = u.shape
    F = wt_up.shape[0] // 2
    tn = _tile(F, tn)
    nf = F // tn

    def body(u_ref, wg_ref, wv_ref, cw_ref, cb_ref, g_ref, ge_ref, t_ref, a_ref, halo_ref):
        i, j = pl.program_id(0), pl.program_id(1)

        @pl.when(i == 0)
        def _():
            halo_ref[j] = jnp.zeros((SUBLANES, tn), F32)

        uv = u_ref[...]
        g, v = _nt(uv, wg_ref[...]), _nt(uv, wv_ref[...])
        row = lax.broadcasted_iota(jnp.int32, (tm, tn), 0)
        g1, g2 = _shift_down(g, halo_ref[j], row)
        c, cdf = _conv_act(g, g1, g2, cw_ref, cb_ref)
        gelu = c * cdf
        pdf = jnp.exp(-0.5 * c * c) * (1.0 / (2.0 * jnp.pi) ** 0.5)
        a_ref[...] = (gelu * v).astype(BF16)
        g_ref[...] = g.astype(BF16)
        ge_ref[...] = gelu.astype(BF16)
        t_ref[...] = (v * (cdf + c * pdf)).astype(BF16)
        halo_ref[j] = g[tm - SUBLANES:, :]

    col = pl.BlockSpec((tm, tn), lambda i, j: (i, j))
    out = jax.ShapeDtypeStruct((S, F), BF16)
    return pl.pallas_call(
        body, name=name, grid=(S // tm, nf),
        in_specs=[pl.BlockSpec((tm, D), lambda i, j: (i, 0)), pl.BlockSpec((tn, D), lambda i, j: (j, 0)),
                  pl.BlockSpec((tn, D), lambda i, j: (j + nf, 0)), pl.BlockSpec((3, tn), lambda i, j: (0, j)),
                  pl.BlockSpec((1, tn), lambda i, j: (0, j))],
        out_specs=[col, col, col, col], out_shape=[out, out, out, out],
        scratch_shapes=[pltpu.VMEM((nf, SUBLANES, tn), F32)], compiler_params=_cparams("arbitrary", "arbitrary"),
    )(u, wt_up, wt_up, conv_w, conv_b)


def _dact_glu_bwd(dh2b, w_down, gate, gelu, vslope, conv_w, *, name, tm=1024, tn=1408):
    S, D = dh2b.shape
    F = gate.shape[1]
    tn = _tile(F, tn)
    nf, ni = F // tn, S // tm

    def body(dh_ref, wd_ref, g_ref, ge_ref, t_ref, cw_ref, dg_ref, dv_ref, dcw_ref, dcb_ref, halo_ref, acc_ref):
        i, j = pl.program_id(0), pl.program_id(1)

        @pl.when(i == 0)
        def _():
            halo_ref[j] = jnp.zeros((SUBLANES, tn), F32)
            acc_ref[j] = jnp.zeros((SUBLANES, tn), F32)

        g = g_ref[...].astype(F32)
        row = lax.broadcasted_iota(jnp.int32, (tm, tn), 0)
        da = _nt(dh_ref[...], wd_ref[...])
        dv_ref[...] = (da * ge_ref[...].astype(F32)).astype(BF16)
        dc = da * t_ref[...].astype(F32)
        d1, d2 = _shift_up(dc, halo_ref[j], row)
        dg_ref[...] = (cw_ref[2:3, :] * dc + cw_ref[1:2, :] * d1 + cw_ref[0:1, :] * d2).astype(BF16)
        halo_ref[j] = dc[:SUBLANES, :]
        for k, t in enumerate((d2 * g, d1 * g, dc * g, dc)):
            acc_ref[j, k:k + 1, :] += jnp.sum(t, axis=0, keepdims=True)

        @pl.when((i == ni - 1) & (j == nf - 1))
        def _():
            for jj in range(nf):
                dcw_ref[:, jj * tn:(jj + 1) * tn] = acc_ref[jj, 0:3, :]
                dcb_ref[:, jj * tn:(jj + 1) * tn] = acc_ref[jj, 3:4, :]

    tile = pl.BlockSpec((tm, tn), lambda i, j: (ni - 1 - i, j))
    return pl.pallas_call(
        body, name=name, grid=(ni, nf),
        in_specs=[pl.BlockSpec((tm, D), lambda i, j: (ni - 1 - i, 0)), pl.BlockSpec((tn, D), lambda i, j: (j, 0)),
                  tile, tile, tile, pl.BlockSpec((3, tn), lambda i, j: (0, j))],
        out_specs=[tile, tile, pl.BlockSpec((3, F), lambda i, j: (0, 0)), pl.BlockSpec((1, F), lambda i, j: (0, 0))],
        out_shape=[jax.ShapeDtypeStruct((S, F), BF16), jax.ShapeDtypeStruct((S, F), BF16),
                   jax.ShapeDtypeStruct((3, F), F32), jax.ShapeDtypeStruct((1, F), F32)],
        scratch_shapes=[pltpu.VMEM((nf, SUBLANES, tn), F32), pltpu.VMEM((nf, SUBLANES, tn), F32)],
        compiler_params=_cparams("arbitrary", "arbitrary"),
    )(dh2b, w_down, gate, gelu, vslope, conv_w)


def _down_loss(act, w_down, h1, g, target, *, name, tm=512):
    S, F = act.shape
    D = h1.shape[1]

    def body(a_ref, w_ref, h_ref, g_ref, t_ref, dh_ref, dhb_ref, dg_ref, loss_ref):
        first = pl.program_id(0) == 0
        h2 = h_ref[...] + _nn(a_ref[...], w_ref[...])
        gv = g_ref[...]
        r = _rstd(h2)
        xh = h2 * r
        err = xh * gv - t_ref[...]
        part_loss = 0.5 * jnp.sum(jnp.mean(err * err, axis=-1, keepdims=True), axis=0, keepdims=True)
        dy = err * (1.0 / D)
        dxh = dy * gv
        dh = r * (dxh - xh * jnp.mean(dxh * xh, axis=-1, keepdims=True))
        dh_ref[...] = dh
        dhb_ref[...] = dh.astype(BF16)
        _accumulate(dg_ref, jnp.sum(dy * xh, axis=0, keepdims=True), first)
        _accumulate(loss_ref, jnp.broadcast_to(part_loss, (1, LANES)), first)

    row = lambda w: pl.BlockSpec((tm, w), lambda i: (i, 0))
    vec = lambda w: pl.BlockSpec((1, w), lambda i: (0, 0))
    return pl.pallas_call(
        body, name=name, grid=(S // tm,), in_specs=[row(F), _resident(w_down.shape), row(D), vec(D), row(D)],
        out_specs=[row(D), row(D), vec(D), vec(LANES)],
        out_shape=[jax.ShapeDtypeStruct((S, D), F32), jax.ShapeDtypeStruct((S, D), BF16),
                   jax.ShapeDtypeStruct((1, D), F32), jax.ShapeDtypeStruct((1, LANES), F32)],
        compiler_params=_cparams("arbitrary"),
    )(act, w_down, h1, g, target)


def _grad_norm_input(pieces, ws, x, g, add, *, name, tm=512, after=None):
    S, D = x.shape
    widths = [p.shape[1] for p in pieces]
    n, nw = len(pieces), len(ws)
    where, wi, off = [], 0, ws[0][1]
    for wd in widths:
        if off == ws[wi][0].shape[0]:
            wi, off = wi + 1, ws[wi + 1][1]
        where.append((wi, off))
        off += wd
    ws = [w for w, _ in ws]

    def body(*refs):
        p_refs, w_refs = refs[:n], refs[n:n + nw]
        x_ref, g_ref, add_ref, dx_ref, dxb_ref, dg_ref = refs[n + nw:]
        halves = _row_halves(tm)
        du = []
        for rows in halves:
            terms = [_nn(p_refs[k][rows, :], w_refs[wi][off:off + widths[k], :]) for k, (wi, off) in enumerate(where)]
            du.append(sum(terms[1:], terms[0]))
        dg_sum = None
        for rows, duh in zip(halves, du):
            dx, dg = _norm_bwd(x_ref[rows, :], g_ref[...], duh)
            dx = add_ref[rows, :] + dx
            dx_ref[rows, :] = dx
            dxb_ref[rows, :] = dx.astype(BF16)
            part = jnp.sum(dg, axis=0, keepdims=True)
            dg_sum = part if dg_sum is None else dg_sum + part
        _accumulate(dg_ref, dg_sum, pl.program_id(0) == 0)

    row = lambda w_: pl.BlockSpec((tm, w_), lambda i: (i, 0))
    vec = pl.BlockSpec((1, D), lambda i: (0, 0))
    extra, extra_specs, adapt = _behind(after, n + nw + 3)
    return pl.pallas_call(
        adapt(body), name=name, grid=(S // tm,),
        in_specs=[row(wd) for wd in widths] + [_resident(w.shape) for w in ws] + [row(D), vec, row(D)] + extra_specs,
        out_specs=[row(D), row(D), vec],
        out_shape=[jax.ShapeDtypeStruct((S, D), F32), jax.ShapeDtypeStruct((S, D), BF16),
                   jax.ShapeDtypeStruct((1, D), F32)],
        compiler_params=_cparams("arbitrary"),
    )(*pieces, *ws, x, g, add, *extra)


def _rows(a):
    return a.reshape(-1, a.shape[-1])


def _row_tile(rows, cols, itemsize=4, budget=1 << 20):
    t = rows
    while t % 32 == 0 and t * cols * itemsize > budget:
        t //= 2
    return t


def _sum_cast(arrs, out_dtype, *, name):
    shape = arrs[0].shape
    flat = [_rows(a) for a in arrs]
    R, C = flat[0].shape
    tr = _row_tile(R, C)

    def body(*refs):
        acc = refs[0][...].astype(F32)
        for r in refs[1:-1]:
            acc = acc + r[...].astype(F32)
        refs[-1][...] = acc.astype(out_dtype)

    spec = pl.BlockSpec((tr, C), lambda i: (i, 0))
    return pl.pallas_call(
        body, name=name, grid=(R // tr,), in_specs=[spec] * len(flat), out_specs=spec,
        out_shape=jax.ShapeDtypeStruct((R, C), out_dtype), compiler_params=_cparams("parallel"),
    )(*flat).reshape(shape)


def _cast_together(arrs, out_dtype, *, name):
    n = len(arrs)

    def body(*refs):
        for i_ref, o_ref in zip(refs[:n], refs[n:]):
            o_ref[...] = i_ref[...].astype(out_dtype)

    return pl.pallas_call(body, name=name, out_shape=[jax.ShapeDtypeStruct(a.shape, out_dtype) for a in arrs],
                          compiler_params=_cparams())(*arrs)


def _adamw(parts, w, m, v, *, name):
    shape = w.shape
    w2, m2, v2 = _rows(w), _rows(m), _rows(v)
    R, C = w2.shape
    parts = [p.reshape(-1, R, C) for p in parts]
    tr = _row_tile(R, C)
    np_ = len(parts)
    c1, c2 = 1.0 - ADAM_B1 ** ADAM_STEP, 1.0 - ADAM_B2 ** ADAM_STEP

    def body(*refs):
        terms = [(r, k) for r in refs[:np_] for k in range(r.shape[0])]
        g = terms[0][0][terms[0][1]].astype(F32)
        for r, k in terms[1:]:
            g = g + r[k].astype(F32)
        w_ref, m_ref, v_ref, g_out, d_out, m_out, v_out = refs[np_:]
        mn = ADAM_B1 * m_ref[...] + (1.0 - ADAM_B1) * g
        vn = ADAM_B2 * v_ref[...] + (1.0 - ADAM_B2) * (g * g)
        g_out[...] = g
        d_out[...] = -ADAM_LR * ((mn / c1) / (jnp.sqrt(vn / c2) + ADAM_EPS) + ADAM_WD * w_ref[...])
        m_out[...] = mn
        v_out[...] = vn

    spec = pl.BlockSpec((tr, C), lambda i: (i, 0))
    out = jax.ShapeDtypeStruct((R, C), F32)
    stacks = [pl.BlockSpec((p.shape[0], tr, C), lambda i: (0, i, 0)) for p in parts]
    res = pl.pallas_call(
        body, name=name, grid=(R // tr,), in_specs=stacks + [spec] * 3, out_specs=[spec] * 4,
        out_shape=[out] * 4, compiler_params=_cparams("parallel"),
    )(*parts, w2, m2, v2)
    return [r.reshape(shape) for r in res]


def _adamw_packed(stack, widths, params, *, name):
    c1, c2 = 1.0 - ADAM_B1 ** ADAM_STEP, 1.0 - ADAM_B2 ** ADAM_STEP
    k = stack.shape[0]
    flat = [None if p is None else [_rows(a) for a in p] for p in params]
    n_in = sum(3 for p in flat if p is not None)

    def body(*refs):
        s_ref, ins, outs = refs[0], list(refs[1:1 + n_in]), list(refs[1 + n_in:])
        off = 0
        for width, p in zip(widths, flat):
            rows = 1 if p is None else p[0].shape[0]
            cols = width // rows
            w_ref, m_ref, v_ref = (None, None, None) if p is None else (ins.pop(0), ins.pop(0), ins.pop(0))
            o_refs = [outs.pop(0) for _ in range(1 if p is None else 4)]
            for r in range(rows):
                seg = slice(off + r * cols, off + (r + 1) * cols)
                g = s_ref[0, :, seg]
                for j in range(1, k):
                    g = g + s_ref[j, :, seg]
                o_refs[0][r:r + 1, :] = g
                if p is not None:
                    row = slice(r, r + 1)
                    mn = ADAM_B1 * m_ref[row, :] + (1.0 - ADAM_B1) * g
                    vn = ADAM_B2 * v_ref[row, :] + (1.0 - ADAM_B2) * (g * g)
                    o_refs[1][row, :] = -ADAM_LR * ((mn / c1) / (jnp.sqrt(vn / c2) + ADAM_EPS) + ADAM_WD * w_ref[row, :])
                    o_refs[2][row, :] = mn
                    o_refs[3][row, :] = vn
            off += width

    operands, out_shape = [stack], []
    for width, p in zip(widths, flat):
        if p is None:
            out_shape.append(jax.ShapeDtypeStruct((1, width), F32))
        else:
            operands += p
            out_shape += [jax.ShapeDtypeStruct(p[0].shape, F32)] * 4
    res = list(pl.pallas_call(body, name=name, out_shape=out_shape)(*operands))
    out = []
    for p, orig in zip(flat, params):
        n = 1 if p is None else 4
        out.append([r if orig is None else r.reshape(orig[0].shape) for r in res[:n]])
        res = res[n:]
    return out


def _coords():
    return lax.axis_index("x"), lax.axis_index("y"), lax.axis_index("c")


def _all_gather(shards, *, name):
    n = len(shards)

    def body(*refs):
        x_refs, out_refs = refs[:n], refs[n:2 * n]
        send_sems, recv_sems, local_sems = refs[2 * n:]
        x, y, c = _coords()
        me, sibling = (x, y, c), (x, y, 1 - c)
        chips = [(1 - x, y), (x, 1 - y), (1 - x, 1 - y)]

        def slot(a, dev):
            return out_refs[a].at[4 * dev[0] + 2 * dev[1] + dev[2]]

        def copy(a, k, block, to, src=None):
            return pltpu.make_async_remote_copy(
                src_ref=slot(a, block) if src is None else src, dst_ref=slot(a, block),
                send_sem=send_sems.at[7 * a + k], recv_sem=recv_sems.at[7 * a + k], device_id=to, device_id_type=MESH)

        mine = [pltpu.make_async_copy(x_refs[a], slot(a, me), local_sems.at[a]) for a in range(n)]
        for cp in mine:
            cp.start()
        first = []
        for a in range(n):
            first.append(copy(a, 0, me, sibling, src=x_refs[a]))
            first += [copy(a, 1 + j, me, (*chip, c), src=x_refs[a]) for j, chip in enumerate(chips)]
        for cp in first:
            cp.start()
        passed = []
        for j, chip in enumerate(chips):
            for a in range(n):
                copy(a, 1 + j, (*chip, c), me).wait_recv()
                fwd = copy(a, 4 + j, (*chip, c), sibling)
                fwd.start()
                passed.append(fwd)
        for a in range(n):
            copy(a, 0, sibling, me).wait_recv()
            for j, chip in enumerate(chips):
                copy(a, 4 + j, (*chip, 1 - c), me).wait_recv()
        for cp in first + passed:
            cp.wait_send()
        for cp in mine:
            cp.wait()

    return pl.pallas_call(
        body, name=name, in_specs=[HBM] * n, out_specs=[HBM] * n,
        out_shape=[jax.ShapeDtypeStruct((N_DEV, *s.shape), s.dtype) for s in shards],
        scratch_shapes=[pltpu.SemaphoreType.DMA((7 * n,)), pltpu.SemaphoreType.DMA((7 * n,)),
                        pltpu.SemaphoreType.DMA((n,))],
    )(*shards)


def _flip_y(x, y, c):
    return (x, 1 - y, c)


def _flip_x(x, y, c):
    return (1 - x, y, c)


def _flip_xy(x, y, c):
    return (1 - x, 1 - y, c)


SEM = pl.BlockSpec(memory_space=pltpu.SEMAPHORE)
SIDE_EFFECT = pltpu.SideEffectType.DATAFLOW_SIDE_EFFECTING


def _in_hbm(a):
    return pltpu.with_memory_space_constraint(a, pltpu.HBM)


def _copies_start(srcs, lands, plan, n_copies, *, name, after=None):
    ns, nl = len(srcs), len(lands)
    extra = [] if after is None else [after]

    def body(*refs):
        src_refs, land_refs = refs[:ns], refs[ns:ns + nl]
        send_sems, recv_sems = refs[ns + nl + len(extra):ns + nl + len(extra) + 2]
        token = refs[-1]
        for k, (src, dst, peer, _) in enumerate(plan(src_refs, land_refs, *_coords())):
            pltpu.make_async_remote_copy(src_ref=src, dst_ref=dst, send_sem=send_sems.at[k], recv_sem=recv_sems.at[k],
                                         device_id=peer, device_id_type=MESH).start()
        token[...] = jnp.zeros_like(token)

    bufs = [*srcs, *lands]
    res = pl.pallas_call(
        body, name=name, in_specs=[HBM] * (ns + nl) + [pl.BlockSpec(memory_space=pl.ANY)] * len(extra),
        out_specs=(SEM, SEM, *[HBM] * (ns + nl), pl.BlockSpec(memory_space=pltpu.VMEM)),
        out_shape=(pltpu.SemaphoreType.DMA((n_copies,)), pltpu.SemaphoreType.DMA((n_copies,)),
                   *[pltpu.HBM(b.shape, b.dtype) for b in bufs], jax.ShapeDtypeStruct((SUBLANES, LANES), F32)),
        input_output_aliases={i: 2 + i for i in range(ns + nl)},
        compiler_params=pltpu.CompilerParams(has_side_effects=SIDE_EFFECT),
    )(*[_in_hbm(b) for b in bufs], *extra)
    return res[0], res[1], list(res[2:2 + ns]), list(res[2 + ns:2 + ns + nl]), res[-1]


def _copies_wait(started, plan, after, *, name, with_srcs=False):
    send_sems, recv_sems, srcs, lands, _ = started
    ns, nl = len(srcs), len(lands)

    def body(*refs):
        src_refs, land_refs = refs[:ns], refs[ns:ns + nl]
        send_sems, recv_sems = refs[ns + nl:ns + nl + 2]
        for k, (src, dst, peer, here) in enumerate(plan(src_refs, land_refs, *_coords())):
            pltpu.make_async_remote_copy(src_ref=src, dst_ref=dst, send_sem=send_sems.at[k], recv_sem=recv_sems.at[k],
                                         device_id=peer, device_id_type=MESH).wait_send()
            pltpu.make_async_remote_copy(src_ref=src, dst_ref=here, send_sem=send_sems.at[k], recv_sem=recv_sems.at[k],
                                         device_id=peer, device_id_type=MESH).wait_recv()

    bufs = [*srcs, *lands]
    res = pl.pallas_call(
        body, name=name, in_specs=[HBM] * (ns + nl) + [SEM, SEM, pl.BlockSpec(memory_space=pl.ANY)],
        out_specs=[HBM] * (ns + nl), out_shape=[pltpu.HBM(b.shape, b.dtype) for b in bufs],
        input_output_aliases={i: i for i in range(ns + nl)},
        compiler_params=pltpu.CompilerParams(has_side_effects=SIDE_EFFECT),
    )(*bufs, send_sems, recv_sems, after)
    return (list(res[:ns]), list(res[ns:])) if with_srcs else list(res[ns:])


def _dev_index(dev):
    return 4 * dev[0] + 2 * dev[1] + dev[2]


def _ag_chips_plan(src_refs, land_refs, x, y, c):
    me = _dev_index((x, y, c))
    return [(src, land.at[me], peer, land.at[_dev_index(peer)])
            for src, land in zip(src_refs, land_refs) for peer in (_flip_y(x, y, c), _flip_x(x, y, c), _flip_xy(x, y, c))]


def _ag_sibling_plan(src_refs, land_refs, x, y, c):
    chips = [(x, y), (x, 1 - y), (1 - x, y), (1 - x, 1 - y)]
    return [(land.at[_dev_index((*chip, c))], land.at[_dev_index((*chip, c))], (x, y, 1 - c),
             land.at[_dev_index((*chip, 1 - c))]) for land in land_refs for chip in chips]


def _ag_direct_plan(src_refs, land_refs, x, y, c):
    me = _dev_index((x, y, c))
    plan = []
    for src, land in zip(src_refs, land_refs):
        for m in range(1, N_DEV):
            peer = (x + (m >> 2) * (1 - 2 * x), y + ((m >> 1) & 1) * (1 - 2 * y), c + (m & 1) * (1 - 2 * c))
            plan.append((src, land.at[me], peer, land.at[_dev_index(peer)]))
    return plan


def _rs_direct_plan(src_refs, land_refs, x, y, c):
    plan = []
    for src, land in zip(src_refs, land_refs):
        for m in range(1, N_DEV):
            peer = (x + (m >> 2) * (1 - 2 * x), y + ((m >> 1) & 1) * (1 - 2 * y), c + (m & 1) * (1 - 2 * c))
            plan.append((src.at[_dev_index(peer)], land.at[m - 1], peer, land.at[m - 1]))
    return plan


def _rs_start(grads, me, *, name, after=None):
    own = [lax.dynamic_index_in_dim(g, me, 0, keepdims=False) for g in grads]
    lands = [lax.empty((N_DEV - 1, *g.shape[1:]), g.dtype) for g in grads]
    return _copies_start(grads, lands, _rs_direct_plan, (N_DEV - 1) * len(grads), name=name, after=after), own


def _rs_finish(started, after, *, name):
    handle, own = started
    got = _copies_wait(handle, _rs_direct_plan, after, name=name)
    return [[o, land] for o, land in zip(own, got)]


def _gathered_cols(w8):
    return w8.transpose(1, 0, 2).reshape(w8.shape[1], -1)


def _pair_major(wt):
    return wt.reshape(3, ATTN_W // LANES, LANES, -1).transpose(1, 0, 2, 3).reshape(3 * ATTN_W, -1)


def kernel(x, norm1_g, w_in, attn_norm_g, hgrn_norm_g, hgrn_lb_logits, w_out, norm2_g, w_up, conv_w, conv_b, w_down, final_norm_g, loss_target, m_norm1_g, m_w_in, m_attn_norm_g, m_hgrn_norm_g, m_hgrn_lb_logits, m_w_out, m_norm2_g, m_w_up, m_conv_w, m_conv_b, m_w_down, m_final_norm_g, v_norm1_g, v_w_in, v_attn_norm_g, v_hgrn_norm_g, v_hgrn_lb_logits, v_w_out, v_norm2_g, v_w_up, v_conv_w, v_conv_b, v_w_down, v_final_norm_g):
    xs, target = x[0], loss_target[0]
    S, D = xs.shape
    NA = 3 * ATTN_W
    fng = final_norm_g.reshape(1, D)

    t = lambda a: a[0].T
    casts = [_sum_cast([t(w_in)], BF16, name="cast_w_in"),
             *_cast_together([w_out[0], t(w_up), w_down[0]], BF16, name="cast_later_weights")]
    me = _dev_index(_coords())
    (g_in,) = _all_gather(casts[:1], name="ag_w_in")
    later = casts[1:] + [conv_w[0]]
    ag1 = _copies_start(later, [lax.empty((N_DEV, *s.shape), s.dtype) for s in later], _ag_chips_plan,
                        3 * len(later), name="ag_chips_start", after=g_in)
    wi = g_in.reshape(-1, D)
    wi_a = _pair_major(wi[:NA])

    u1, proj_a, proj_h, proj_f = _in_proj(xs, norm1_g, wi_a, wi, NA, name="in_proj", after=ag1[4])
    attn, lse = _attn_fwd(proj_a, name="attn_fwd")
    later, lands = _copies_wait(ag1, _ag_chips_plan, attn, name="ag_chips_wait", with_srcs=True)
    lands = [lax.dynamic_update_index_in_dim(l, s, me, 0) for l, s in zip(lands, later)]
    ag2 = _copies_start([], lands, _ag_sibling_plan, 4 * len(later), name="ag_sibling_start")
    rec, states = _hgrn_fwd(proj_h, proj_f, hgrn_lb_logits, name="hgrn_fwd", after=ag2[4])
    g_out, g_up, g_down, g_cw = _copies_wait(ag2, _ag_sibling_plan, rec, name="ag_sibling_wait")
    wo = g_out.reshape(-1, D)
    wu = g_up.reshape(-1, D)
    wd = g_down.reshape(-1, D)
    cw = _gathered_cols(g_cw)
    h1, u2, mixed = _out_proj(attn, rec, proj_h, xs, attn_norm_g, hgrn_norm_g, norm2_g, wo, name="out_proj")
    gate, gelu, vslope, act = _up_glu(u2, wu, cw, conv_b, name="up_glu")
    dh2, dh2b, d_fng, loss_part = _down_loss(act, wd, h1, fng, target, name="down_loss")

    dgate, dval, d_cw, d_cb = _dact_glu_bwd(dh2b, wd, gate, gelu, vslope, cw, name="dact_glu_bwd")
    dw_down = _mm_tn([act], dh2b, tm=256, name="dw_down")
    dh1, dh1b, d_n2g = _grad_norm_input([dgate, dval], [(wu, 0)], h1, norm2_g, dh2, name="du2_norm2_bwd")
    F = dgate.shape[1]
    dw_up = _mm_tn([dgate, dval], u2, tm=256, name="dw_up")
    rs_ffn = _rs_start([dw_down.reshape(N_DEV, -1, D), dw_up.reshape(N_DEV, -1, D)], me, name="rs_ffn_start")
    dattn, delta, drec, dhg, d_ang, d_hng = _dmix_post_bwd(dh1b, wo, attn, rec, proj_h, attn_norm_g, hgrn_norm_g,
                                                          name="dmix_post_bwd", after=rs_ffn[0][4])
    dw_out = _mm_tn([mixed], dh1b, name="dw_out")
    rs_out = _rs_start([dw_out.reshape(N_DEV, -1, D)], me, name="rs_out_start")
    dproj_h, d_lbl = _hgrn_bwd(proj_h, proj_f, hgrn_lb_logits, states, drec, name="hgrn_bwd", after=rs_out[0][4])
    dproj_a = _attn_bwd(proj_a, dattn, lse, delta, name="attn_bwd")
    pairs = ATTN_W // LANES
    dw_in = _mm_tn([dproj_a], u1, tm=LANES, rows=wi.shape[0], row_block=lambda i: pairs * (i % 3) + i // 3,
                   name="dw_in_attn")
    dw_in = _mm_tn([dproj_h, dhg], u1, tm=256, rows=wi.shape[0], row_block=lambda i: i + NA // 256, into=dw_in,
                   name="dw_in_hgrn")
    rs_in = _rs_start([dw_in.reshape(N_DEV, -1, D)], me, name="rs_in_start")
    grad_x, _, d_n1g = _grad_norm_input([dproj_a, dproj_h, dhg], [(wi_a, 0), (wi, NA)], xs,
                                        norm1_g, dh1, name="du1_norm1_bwd", after=rs_in[0][4])
    small = [("loss", loss_part, None, None, None),
             ("norm1_g", d_n1g, norm1_g, m_norm1_g, v_norm1_g),
             ("attn_norm_g", d_ang, attn_norm_g, m_attn_norm_g, v_attn_norm_g),
             ("hgrn_norm_g", d_hng, hgrn_norm_g, m_hgrn_norm_g, v_hgrn_norm_g),
             ("hgrn_lb_logits", d_lbl, hgrn_lb_logits, m_hgrn_lb_logits, v_hgrn_lb_logits),
             ("norm2_g", d_n2g, norm2_g, m_norm2_g, v_norm2_g),
             ("conv_b", d_cb, conv_b, m_conv_b, v_conv_b),
             ("final_norm_g", d_fng, final_norm_g, m_final_norm_g, v_final_norm_g)]
    pack = lambda arrs: jnp.concatenate([a.reshape(1, -1) for a in arrs], axis=1)
    small_own = [pack([s[1] for s in small]), d_cw]
    ag_small = _copies_start(small_own, [lax.empty((N_DEV, *s.shape), s.dtype) for s in small_own], _ag_direct_plan,
                             (N_DEV - 1) * len(small_own), name="ag_small_start")

    res = {}

    def update(nm, parts, w, m, v, transposed=False):
        if transposed:
            raw = _adamw(parts, t(w), t(m), t(v), name=f"adamw_{nm}")
            res[nm] = [r.T[None] for r in raw]
        else:
            raw = res[nm] = _adamw(parts, w, m, v, name=f"adamw_{nm}")
        return raw[1]

    g_down, g_up = _rs_finish(rs_ffn, grad_x, name="rs_ffn_wait")
    update("w_down", g_down, w_down, m_w_down, v_w_down)
    done_up = update("w_up", g_up, w_up, m_w_up, v_w_up, transposed=True)
    (g_out,) = _rs_finish(rs_out, grad_x, name="rs_out_wait")
    update("w_out", g_out, w_out, m_w_out, v_w_out)
    (g_in,) = _rs_finish(rs_in, done_up, name="rs_in_wait")
    done_in = update("w_in", g_in, w_in, m_w_in, v_w_in, transposed=True)

    small_own, small_all = _copies_wait(ag_small, _ag_direct_plan, done_in, name="ag_small_wait", with_srcs=True)
    g_small, g_dcw = [lax.dynamic_update_index_in_dim(l, s, me, 0) for l, s in zip(small_all, small_own)]
    sm = _adamw_packed(g_small, [s[1].size for s in small], [None if s[2] is None else s[2:] for s in small],
                       name="adamw_small")
    for (nm, *_), r in zip(small, sm):
        res[nm] = r
    ncw = conv_w.shape[-1]
    mine_cw = lax.dynamic_slice_in_dim(g_dcw, me * ncw, ncw, axis=2)
    res["conv_w"] = _adamw([mine_cw], conv_w, m_conv_w, v_conv_w, name="adamw_conv_w")

    loss = res["loss"][0][0, 0]
    order = ["norm1_g", "w_in", "attn_norm_g", "hgrn_norm_g", "hgrn_lb_logits", "w_out", "norm2_g", "w_up",
             "conv_w", "conv_b", "w_down", "final_norm_g"]
    return (loss, grad_x[None], *[res[nm][0] for nm in order], *[res[nm][1] for nm in order],
            *[res[nm][2] for nm in order], *[res[nm][3] for nm in order])
```

```python
import math

import jax
import jax.numpy as jnp
from jax import lax
from jax.experimental import pallas as pl
from jax.experimental.pallas import tpu as pltpu

F32, BF16 = jnp.float32, jnp.bfloat16
NORM_EPS = 1e-6
ATTN_HEADS, HEAD_DIM, ATTN_BLOCK = 8, 64, 128
DILATIONS = (1, 4, 16)
ATTN_SCALE = HEAD_DIM ** -0.5
ATTN_W = ATTN_HEADS * HEAD_DIM
HGRN_HEADS, HGRN_DIM, HGRN_CHUNK = 4, 128, 64
HGRN_W = HGRN_HEADS * HGRN_DIM
ADAM_LR, ADAM_B1, ADAM_B2, ADAM_EPS, ADAM_WD, ADAM_STEP = 0.001, 0.9, 0.999, 1e-08, 0.01, 10
LANES, SUBLANES = 128, 8
VMEM_LIMIT_BYTES = 56 * 1024 * 1024
N_DEV = 8
MESH = pl.DeviceIdType.MESH
HBM = pl.BlockSpec(memory_space=pltpu.HBM)
HIGHEST = lax.Precision.HIGHEST


def _cparams(*sem):
    return pltpu.CompilerParams(dimension_semantics=sem, vmem_limit_bytes=VMEM_LIMIT_BYTES)


def _tile(n, pref):
    if n <= pref:
        return n
    t = (pref // LANES) * LANES
    while n % t:
        t -= LANES
    return t


def _resident(shape):
    return pl.BlockSpec(shape, lambda *_: (0,) * len(shape), pipeline_mode=pl.Buffered(1))


def _dot(a, b, dims, precision=None):
    return lax.dot_general(a, b, (dims, ((), ())), precision=precision, preferred_element_type=F32)


def _nn(a, b, precision=None):
    return _dot(a, b, ((1,), (0,)), precision)


def _nt(a, b):
    return _dot(a, b, ((1,), (1,)))


def _tn(a, b):
    return _dot(a, b, ((0,), (0,)))


def _sigmoid(x):
    return 1.0 / (1.0 + jnp.exp(-x))


def _rstd(x):
    return lax.rsqrt(jnp.mean(x * x, axis=-1, keepdims=True) + NORM_EPS)


def _norm_bwd(x, g, du):
    r = _rstd(x)
    xh = x * r
    dxh = du * g
    return r * (dxh - xh * jnp.mean(dxh * xh, axis=-1, keepdims=True)), du * xh


def _row_halves(tm):
    return [pl.ds(0, tm // 2), pl.ds(tm // 2, tm // 2)]


def _behind(after, n_in):
    if after is None:
        return [], [], (lambda body: body)
    return ([after], [pl.BlockSpec(memory_space=pl.ANY)],
            lambda body: (lambda *refs: body(*refs[:n_in], *refs[n_in + 1:])))


def _accumulate(ref, part, first):
    @pl.when(first)
    def _():
        ref[...] = part

    @pl.when(jnp.logical_not(first))
    def _():
        ref[...] += part


def _mm_tn(xs, dy, *, name, tm=512, tn=1024, rows=None, row_block=None, into=None):
    S, N = dy.shape
    tm = _tile(math.gcd(*[x.shape[1] for x in xs]), tm)
    tn = _tile(N, tn)
    blocks = [x.shape[1] // tm for x in xs]
    first = [sum(blocks[:k]) for k in range(len(xs))]
    row_block = row_block or (lambda i: i)
    n = len(xs)

    def body(*refs):
        x_refs, dy_ref = refs[:n], refs[n]
        o_ref, xt_ref = refs[-2:]
        @pl.when(pl.program_id(1) == 0)
        def _():
            xv = x_refs[0][...]
            for x_ref, b0 in zip(x_refs[1:], first[1:]):
                xv = jnp.where(pl.program_id(0) >= b0, x_ref[...], xv)
            xt_ref[...] = xv.T

        o_ref[...] = _nn(xt_ref[...], dy_ref[...]).astype(BF16)

    def x_spec(b0, nb):
        return pl.BlockSpec((S, tm), lambda i, j: (0, jnp.clip(i - b0, 0, nb - 1)))

    operands = [*xs, dy] + ([] if into is None else [into])
    return pl.pallas_call(
        body, name=name, grid=(sum(blocks), N // tn),
        in_specs=[x_spec(b0, nb) for b0, nb in zip(first, blocks)] + [pl.BlockSpec((S, tn), lambda i, j: (0, j))]
        + ([] if into is None else [pl.BlockSpec(memory_space=pl.ANY)]),
        out_specs=pl.BlockSpec((tm, tn), lambda i, j: (row_block(i), j)),
        out_shape=jax.ShapeDtypeStruct((rows or sum(blocks) * tm, N), BF16),
        input_output_aliases={} if into is None else {n + 1: 0},
        scratch_shapes=[pltpu.VMEM((tm, S), BF16)], compiler_params=_cparams("parallel", "arbitrary"),
    )(*operands)


def _in_proj(x, g, wt_attn, wt, row0, *, name, tm=512, after=None):
    S, D = x.shape
    NA, NH, W = wt_attn.shape[0], wt.shape[0] - row0, HGRN_W

    def body(x_ref, g_ref, wa_ref, w_ref, u_ref, a_ref, h_ref, f_ref):
        xv = x_ref[...]
        u = (xv * _rstd(xv) * g_ref[...]).astype(BF16)
        u_ref[...] = u
        a_ref[...] = _nt(u, wa_ref[...]).astype(BF16)
        ph = _nt(u, w_ref[row0:row0 + NH, :])
        h_ref[...] = ph.astype(BF16)
        f_ref[...] = ph[:, W:2 * W]

    row = lambda w: pl.BlockSpec((tm, w), lambda i: (i, 0))
    extra, extra_specs, adapt = _behind(after, 4)
    return pl.pallas_call(
        adapt(body), name=name, grid=(S // tm,),
        in_specs=[row(D), pl.BlockSpec((1, D), lambda i: (0, 0)), _resident(wt_attn.shape), _resident(wt.shape)]
        + extra_specs,
        out_specs=[row(D), row(NA), row(NH), row(W)],
        out_shape=[jax.ShapeDtypeStruct((S, D), BF16), jax.ShapeDtypeStruct((S, NA), BF16),
                   jax.ShapeDtypeStruct((S, NH), BF16), jax.ShapeDtypeStruct((S, W), F32)],
        compiler_params=_cparams("parallel"),
    )(x, g, wt_attn, wt, *extra)


PAIR_W = 3 * LANES
ATTN_UNROLL_FWD, ATTN_UNROLL_BWD = 8, 4


def _attn_masks(first):
    qi = lax.broadcasted_iota(jnp.int32, (ATTN_BLOCK, 2 * ATTN_BLOCK), 0)
    kj = lax.broadcasted_iota(jnp.int32, (ATTN_BLOCK, 2 * ATTN_BLOCK), 1)
    dist = qi + ATTN_BLOCK - kj
    valid = (dist >= 0) & (dist <= ATTN_BLOCK) & jnp.logical_or(kj >= ATTN_BLOCK, jnp.logical_not(first))
    lane = lax.broadcasted_iota(jnp.int32, (1, LANES), 1)
    return valid, lane


def _for_residue_blocks(S, d, fn):
    span = ATTN_BLOCK * d
    nb = S // span

    def step(n, carry):
        base = pl.multiple_of(n * span, span)
        for r in range(d):
            off = pl.multiple_of((r * nb + n) * ATTN_BLOCK, ATTN_BLOCK)
            fn(lambda ref, r=r: _block_rows(ref, base, r, d),
               lambda ref, val, r=r: _set_block_rows(ref, base, r, d, val), off)
        return carry

    lax.fori_loop(0, nb, step, 0)


def _for_blocks(S, unroll, fn):
    def step(i, carry):
        fn([(pl.multiple_of((i * unroll + u) * ATTN_BLOCK, ATTN_BLOCK), i * unroll + u) for u in range(unroll)])
        return carry

    lax.fori_loop(0, S // ATTN_BLOCK // unroll, step, 0)


def _head_value(x2, lane, e):
    return jnp.sum(jnp.where(lane == HEAD_DIM * e, x2, 0.0), axis=-1, keepdims=True)


def _block_rows(ref, base, r, d):
    if d == 1:
        return ref[pl.ds(base, ATTN_BLOCK), :]
    return ref.at[pl.ds(base, ATTN_BLOCK * d)][pl.ds(r, ATTN_BLOCK, stride=d), :]


def _set_block_rows(ref, base, r, d, val):
    if d == 1:
        ref[pl.ds(base, ATTN_BLOCK), :] = val
    else:
        ref.at[pl.ds(base, ATTN_BLOCK * d)][pl.ds(r, ATTN_BLOCK, stride=d), :] = val


def _order4_to_16(src, dst, pad):
    S = src.shape[0]
    q4, q16 = S // 4, S // 16
    for r in range(4):
        for a in range(4):
            for n in range(q16 // ATTN_BLOCK):
                rows = src.at[pl.ds(r * q4 + 4 * ATTN_BLOCK * n, 4 * ATTN_BLOCK)][pl.ds(a, ATTN_BLOCK, stride=4), :]
                dst[pl.ds(pad + (4 * a + r) * q16 + ATTN_BLOCK * n, ATTN_BLOCK), :] = rows.astype(dst.dtype)


def _order16_to_4(src, pad, dst):
    S = dst.shape[0]
    q4, q16 = S // 4, S // 16
    for r in range(4):
        for a in range(4):
            for n in range(q16 // ATTN_BLOCK):
                rows = src[pl.ds(pad + (4 * a + r) * q16 + ATTN_BLOCK * n, ATTN_BLOCK), :]
                dst.at[pl.ds(r * q4 + 4 * ATTN_BLOCK * n, 4 * ATTN_BLOCK)][pl.ds(a, ATTN_BLOCK, stride=4), :] = rows


def _regroup(S, d, pairs, tmp):
    for src, dst, pad in pairs:
        if d == 16:
            def to_tmp(rows, _, off, src=src):
                tmp[pl.ds(off, ATTN_BLOCK), :] = rows(src)

            _for_residue_blocks(S, 4, to_tmp)
            _order4_to_16(tmp, dst, pad)
    if d != 16:
        def to_dst(rows, _, off):
            for src, dst, pad in pairs:
                dst[pl.ds(pad + off, ATTN_BLOCK), :] = rows(src).astype(dst.dtype)

        _for_residue_blocks(S, d, to_dst)


def _split_pair(p_ref, qs, ks, vs, bk, bv):
    qs[...] = p_ref[:, 0:LANES].astype(F32) * ATTN_SCALE
    ks[...] = p_ref[:, LANES:2 * LANES].astype(F32)
    vs[...] = p_ref[:, 2 * LANES:3 * LANES].astype(F32)
    bk[0:ATTN_BLOCK, :] = jnp.zeros((ATTN_BLOCK, LANES), bk.dtype)
    bv[0:ATTN_BLOCK, :] = jnp.zeros((ATTN_BLOCK, LANES), bv.dtype)


def _attn_fwd(proj_a, *, name):
    S = proj_a.shape[0]

    def body(p_ref, o_ref, l_ref, qs, ks, vs, bq, bk, bv, bo, bl, to, tl):
        _split_pair(p_ref, qs, ks, vs, bk, bv)
        for d in DILATIONS:
            nb = S // (ATTN_BLOCK * d)
            _regroup(S, d, ((qs, bq, 0), (ks, bk, ATTN_BLOCK), (vs, bv, ATTN_BLOCK)), to)

            def blocks(group, nb=nb):
                lane = lax.broadcasted_iota(jnp.int32, (1, LANES), 1)
                heads = [(lane >= HEAD_DIM * e) & (lane < HEAD_DIM * (e + 1)) for e in range(LANES // HEAD_DIM)]
                wins = [pl.ds(off, 2 * ATTN_BLOCK) for off, _ in group]
                s = [[_nt(jnp.where(mh, bq[pl.ds(off, ATTN_BLOCK), :], jnp.zeros((ATTN_BLOCK, LANES), BF16)), bk[win, :])
                      for mh in heads] for (off, _), win in zip(group, wins)]
                p, m, l = [], [], []
                for (off, b), su in zip(group, s):
                    valid, _ = _attn_masks(jnp.bitwise_and(b, nb - 1) == 0)
                    sm = [jnp.where(valid, x, -jnp.inf) for x in su]
                    m.append([jnp.max(x, axis=-1, keepdims=True) for x in sm])
                    p.append([jnp.exp(x - mx) for x, mx in zip(sm, m[-1])])
                    l.append([jnp.sum(x, axis=-1, keepdims=True) for x in p[-1]])
                o = [[_nn(x.astype(BF16), bv[win, :]) for x in pu] for pu, win in zip(p, wins)]
                for (off, _), ou, mu, lu in zip(group, o, m, l):
                    o2 = jnp.zeros((ATTN_BLOCK, LANES), F32)
                    l2 = jnp.zeros((ATTN_BLOCK, LANES), F32)
                    for mh, oe, me_, le in zip(heads, ou, mu, lu):
                        o2 = jnp.where(mh, oe / le, o2)
                        l2 = jnp.where(mh, me_ + jnp.log(le), l2)
                    bo[pl.ds(off, ATTN_BLOCK), :] = o2
                    bl[pl.ds(off, ATTN_BLOCK), :] = l2

            _for_blocks(S, ATTN_UNROLL_FWD, blocks)

            if d == 16:
                _order16_to_4(bo, 0, to)
                _order16_to_4(bl, 0, tl)
            src_o, src_l = (to, tl) if d == 16 else (bo, bl)

            def merge(rows, set_rows, off, d=d, src_o=src_o, src_l=src_l):
                blk = pl.ds(off, ATTN_BLOCK)
                o2, l2 = src_o[blk, :], src_l[blk, :]
                if d != DILATIONS[0]:
                    lo, oo = rows(l_ref), rows(o_ref)
                    ln = jnp.maximum(lo, l2)
                    wa, wb = jnp.exp(lo - ln), jnp.exp(l2 - ln)
                    o2 = (wa * oo + wb * o2) / (wa + wb)
                    l2 = ln + jnp.log(wa + wb)
                set_rows(o_ref, o2)
                set_rows(l_ref, l2)

            _for_residue_blocks(S, min(d, 4), merge)

    slab = pl.BlockSpec((S, LANES), lambda p: (0, p))
    f32_slab, bf16_slab = pltpu.VMEM((S, LANES), F32), pltpu.VMEM((S, LANES), BF16)
    bf16_window = pltpu.VMEM((S + ATTN_BLOCK, LANES), BF16)
    return pl.pallas_call(
        body, name=name, grid=(ATTN_W // LANES,), in_specs=[pl.BlockSpec((S, PAIR_W), lambda p: (0, p))],
        out_specs=[slab, slab],
        out_shape=[jax.ShapeDtypeStruct((S, ATTN_W), F32), jax.ShapeDtypeStruct((S, ATTN_W), F32)],
        scratch_shapes=[f32_slab] * 3 + [bf16_slab, bf16_window, bf16_window] + [f32_slab] * 4,
        compiler_params=_cparams("parallel"),
    )(proj_a)


def _attn_bwd(proj_a, do, lse, delta, *, name):
    S = proj_a.shape[0]

    def body(p_ref, do_ref, lse_ref, dl_ref, o_ref, qs, ks, vs, dqs, dks, dvs, bq, bk, bv, bdo, blse, bdl, bdq, bdk, bdv,
             tmp):
        _split_pair(p_ref, qs, ks, vs, bk, bv)
        bdk[0:ATTN_BLOCK, :] = jnp.zeros((ATTN_BLOCK, LANES), F32)
        bdv[0:ATTN_BLOCK, :] = jnp.zeros((ATTN_BLOCK, LANES), F32)
        for d in DILATIONS:
            nb = S // (ATTN_BLOCK * d)
            _regroup(S, d, ((qs, bq, 0), (ks, bk, ATTN_BLOCK), (vs, bv, ATTN_BLOCK), (do_ref, bdo, 0),
                            (lse_ref, blse, 0), (dl_ref, bdl, 0)), tmp)

            def blocks(group, nb=nb):
                lane = lax.broadcasted_iota(jnp.int32, (1, LANES), 1)
                heads = [(lane >= HEAD_DIM * e) & (lane < HEAD_DIM * (e + 1)) for e in range(LANES // HEAD_DIM)]
                zero = jnp.zeros((ATTN_BLOCK, LANES), BF16)
                chains = [(off, b, e, mh) for off, b in group for e, mh in enumerate(heads)]
                qm = [jnp.where(mh, bq[pl.ds(off, ATTN_BLOCK), :], zero) for off, _, _, mh in chains]
                dom = [jnp.where(mh, bdo[pl.ds(off, ATTN_BLOCK), :], zero) for off, _, _, mh in chains]
                s = [_nt(x, bk[pl.ds(off, 2 * ATTN_BLOCK), :]) for x, (off, _, _, _) in zip(qm, chains)]
                dp = [_nt(x, bv[pl.ds(off, 2 * ATTN_BLOCK), :]) for x, (off, _, _, _) in zip(dom, chains)]
                p, ds = [], []
                for (off, b, e, _), sc, dpc in zip(chains, s, dp):
                    valid, _ = _attn_masks(jnp.bitwise_and(b, nb - 1) == 0)
                    blk = pl.ds(off, ATTN_BLOCK)
                    pc = jnp.where(valid, jnp.exp(sc - _head_value(blse[blk, :], lane, e)), 0.0)
                    ds.append((pc * (dpc - _head_value(bdl[blk, :], lane, e))).astype(BF16))
                    p.append(pc.astype(BF16))
                dq = [_nn(x, bk[pl.ds(off, 2 * ATTN_BLOCK), :]) for x, (off, _, _, _) in zip(ds, chains)]
                dk = [_tn(x, y) for x, y in zip(ds, qm)]
                dv = [_tn(x, y) for x, y in zip(p, dom)]
                nh = len(heads)
                for u, (off, _) in enumerate(group):
                    dq2 = jnp.zeros((ATTN_BLOCK, LANES), F32)
                    for mh, x in zip(heads, dq[nh * u:nh * (u + 1)]):
                        dq2 = jnp.where(mh, x, dq2)
                    bdq[pl.ds(off, ATTN_BLOCK), :] = dq2 * ATTN_SCALE
                    for acc, grads in ((bdk, dk), (bdv, dv)):
                        win_grad = sum(grads[nh * u + 1:nh * (u + 1)], grads[nh * u])
                        acc[pl.ds(off, ATTN_BLOCK), :] += win_grad[:ATTN_BLOCK]
                        acc[pl.ds(off + ATTN_BLOCK, ATTN_BLOCK), :] = win_grad[ATTN_BLOCK:]

            _for_blocks(S, ATTN_UNROLL_BWD, blocks)

            outs = ((dqs, bdq, 0), (dks, bdk, ATTN_BLOCK), (dvs, bdv, ATTN_BLOCK))
            if d == 16:
                for acc, grad, pad in outs:
                    _order16_to_4(grad, pad, tmp)

                    def add(rows, set_rows, off, acc=acc):
                        set_rows(acc, rows(acc) + tmp[pl.ds(off, ATTN_BLOCK), :])

                    _for_residue_blocks(S, 4, add)
            else:
                def scatter(rows, set_rows, off, d=d):
                    for acc, grad, pad in outs:
                        part = grad[pl.ds(pad + off, ATTN_BLOCK), :]
                        set_rows(acc, part if d == DILATIONS[0] else rows(acc) + part)

                _for_residue_blocks(S, d, scatter)
        o_ref[:, 0:LANES] = dqs[...].astype(BF16)
        o_ref[:, LANES:2 * LANES] = dks[...].astype(BF16)
        o_ref[:, 2 * LANES:3 * LANES] = dvs[...].astype(BF16)

    slab = pl.BlockSpec((S, LANES), lambda p: (0, p))
    pair = pl.BlockSpec((S, PAIR_W), lambda p: (0, p))
    f32_slab, bf16_slab = pltpu.VMEM((S, LANES), F32), pltpu.VMEM((S, LANES), BF16)
    f32_window, bf16_window = pltpu.VMEM((S + ATTN_BLOCK, LANES), F32), pltpu.VMEM((S + ATTN_BLOCK, LANES), BF16)
    return pl.pallas_call(
        body, name=name, grid=(ATTN_W // LANES,), in_specs=[pair, slab, slab, slab], out_specs=pair,
        out_shape=jax.ShapeDtypeStruct(proj_a.shape, BF16),
        scratch_shapes=[f32_slab] * 6 + [bf16_slab, bf16_window, bf16_window, bf16_slab, f32_slab, f32_slab,
                                         f32_slab, f32_window, f32_window, f32_slab],
        compiler_params=_cparams("parallel"),
    )(proj_a, do, lse, delta)


HG_T = 2 * HGRN_CHUNK
HG_GROUPS = 2
HG_STEP = HG_GROUPS * HG_T


def _hgrn_consts():
    row = lax.broadcasted_iota(jnp.int32, (HG_T, HG_T), 0)
    col = lax.broadcasted_iota(jnp.int32, (HG_T, HG_T), 1)
    same = (row >= HGRN_CHUNK) == (col >= HGRN_CHUNK)
    return row, same & (col <= row), same & (col >= row)


def _lower_bound(logits_ref):
    l0, l1 = logits_ref[0:1, :], logits_ref[1:2, :]
    mx = jnp.maximum(l0, l1)
    e0, e1 = jnp.exp(l0 - mx), jnp.exp(l1 - mx)
    return e0 / (e0 + e1)


def _hgrn_chains():
    chains = [(g, h) for g in range(HG_GROUPS) for h in range(HGRN_HEADS)]
    rows = [pl.ds(HG_T * g, HG_T) for g, _ in chains]
    lanes = [slice(HGRN_DIM * h, HGRN_DIM * (h + 1)) for _, h in chains]
    return chains, rows, lanes


def _hgrn_gates(qs, fs, lbs, row, causal):
    C = HGRN_CHUNK
    tri = jnp.where(causal, 1.0, 0.0).astype(F32)
    sgs = [_sigmoid(f) for f in fs]
    forgets = [lb + (1.0 - lb) * sg for lb, sg in zip(lbs, sgs)]
    logfs = [jnp.log(forget) for forget in forgets]
    bs = [_nn(tri, logf, HIGHEST) for logf in logfs]
    out = []
    for q, sg, forget, logf, b in zip(qs, sgs, forgets, logfs, bs):
        key = 1.0 - forget
        bend0 = jnp.sum(logf[:C], axis=0, keepdims=True)
        bend1 = jnp.sum(logf[C:], axis=0, keepdims=True)
        bend = jnp.where(row < C, bend0, bend1)
        eb, emb, eend = jnp.exp(b), jnp.exp(-b), jnp.exp(bend - b)
        sq = _sigmoid(q)
        out.append(dict(sg=sg, forget=forget, key=key, bend0=bend0, bend1=bend1, eb=eb, emb=emb, eend=eend, sq=sq,
                        qd=q * sq * eb, ki=key * emb, ke=key * eend))
    return out


def _hgrn_fwd(proj, proj_f, logits, *, name, after=None):
    S = proj.shape[0]
    W, C = HGRN_W, HGRN_CHUNK

    def body(q_ref, f_ref, i_ref, lg_ref, rec_ref, st_ref, s_ref):
        @pl.when(pl.program_id(0) == 0)
        def _():
            s_ref[...] = jnp.zeros_like(s_ref)

        row, causal, _ = _hgrn_consts()
        lb_all = _lower_bound(lg_ref)
        chains, rows, lanes = _hgrn_chains()
        n = range(len(chains))
        gts = _hgrn_gates([q_ref[rows[c], lanes[c]].astype(F32) for c in n], [f_ref[rows[c], lanes[c]] for c in n],
                          [lb_all[:, lanes[c]] for c in n], row, causal)
        qd, ki, ke = ([gt[k].astype(BF16) for gt in gts] for k in ("qd", "ki", "ke"))
        iv = [i_ref[rows[c], lanes[c]].astype(BF16) for c in n]
        a = [_nt(qd[c], ki[c]) for c in n]
        u0 = [_tn(iv[c][:C], ke[c][:C]) for c in n]
        u1 = [_tn(iv[c][C:], ke[c][C:]) for c in n]
        state = [s_ref[h] for h in range(HGRN_HEADS)]
        s0, s1 = [], []
        for c, (g, h) in enumerate(chains):
            s0.append(state[h])
            s1.append(jnp.exp(gts[c]["bend0"]) * s0[c] + u0[c])
            state[h] = jnp.exp(gts[c]["bend1"]) * s1[c] + u1[c]
        o0 = [_nt(qd[c][:C], s0[c].astype(BF16)) for c in n]
        o1 = [_nt(qd[c][C:], s1[c].astype(BF16)) for c in n]
        o = [_nn(jnp.where(causal, a[c], 0.0).astype(BF16), iv[c]) for c in n]
        for c, (g, h) in enumerate(chains):
            st_ref[2 * g, h] = s0[c]
            st_ref[2 * g + 1, h] = s1[c]
            rec_ref[rows[c], lanes[c]] = o[c] + jnp.concatenate([o0[c], o1[c]], axis=0)
        for h in range(HGRN_HEADS):
            s_ref[h] = state[h]

    blk = lambda j: pl.BlockSpec((HG_STEP, W), lambda t: (t, j))
    extra, extra_specs, adapt = _behind(after, 4)
    return pl.pallas_call(
        adapt(body), name=name, grid=(S // HG_STEP,),
        in_specs=[blk(0), blk(0), blk(2), pl.BlockSpec((2, W), lambda t: (0, 0))] + extra_specs,
        out_specs=[blk(0), pl.BlockSpec((2 * HG_GROUPS, HGRN_HEADS, HGRN_DIM, HGRN_DIM), lambda t: (t, 0, 0, 0))],
        out_shape=[jax.ShapeDtypeStruct((S, W), F32),
                   jax.ShapeDtypeStruct((S // C, HGRN_HEADS, HGRN_DIM, HGRN_DIM), F32)],
        scratch_shapes=[pltpu.VMEM((HGRN_HEADS, HGRN_DIM, HGRN_DIM), F32)],
        compiler_params=_cparams("arbitrary"),
    )(proj, proj_f, proj, logits, *extra)


def _hgrn_bwd(proj, proj_f, logits, states, drec, *, name, after=None):
    S = proj.shape[0]
    W, C = HGRN_W, HGRN_CHUNK
    nt = S // HG_STEP

    def body(q_ref, f_ref, i_ref, lg_ref, st_ref, do_ref, dp_ref, dlg_ref, ds_ref, dlb_ref):
        t = pl.program_id(0)

        @pl.when(t == 0)
        def _():
            ds_ref[...] = jnp.zeros_like(ds_ref)
            dlb_ref[...] = jnp.zeros_like(dlb_ref)

        row, causal, anti = _hgrn_consts()
        lb_all = _lower_bound(lg_ref)
        chains, rows, lanes = _hgrn_chains()
        n = range(len(chains))
        qs, lbs = [q_ref[rows[c], lanes[c]].astype(F32) for c in n], [lb_all[:, lanes[c]] for c in n]
        gts = _hgrn_gates(qs, [f_ref[rows[c], lanes[c]] for c in n], lbs, row, causal)
        qd, ki, ke = ([gt[k] for gt in gts] for k in ("qd", "ki", "ke"))
        qdb, kib, keb = ([x.astype(BF16) for x in xs] for xs in (qd, ki, ke))
        iv = [i_ref[rows[c], lanes[c]].astype(BF16) for c in n]
        dob = [do_ref[rows[c], lanes[c]].astype(BF16) for c in n]
        s0 = [st_ref[2 * g, h] for g, h in chains]
        s1 = [st_ref[2 * g + 1, h] for g, h in chains]
        dec0, dec1 = [jnp.exp(gt["bend0"]) for gt in gts], [jnp.exp(gt["bend1"]) for gt in gts]
        a = [_nt(qdb[c], kib[c]) for c in n]
        da = [_nt(dob[c], iv[c]) for c in n]
        dqd1 = [_nn(dob[c][C:], s1[c].astype(BF16)) for c in n]
        dqd0 = [_nn(dob[c][:C], s0[c].astype(BF16)) for c in n]
        t1 = [_tn(dob[c][C:], qdb[c][C:]) for c in n]
        t0 = [_tn(dob[c][:C], qdb[c][:C]) for c in n]
        carry = [ds_ref[h] for h in range(HGRN_HEADS)]
        ds1, ds0 = [None] * len(chains), [None] * len(chains)
        for c in reversed(n):
            h = chains[c][1]
            ds1[c] = carry[h]
            ds0[c] = dec1[c] * ds1[c] + t1[c]
            carry[h] = dec0[c] * ds0[c] + t0[c]
        for h in range(HGRN_HEADS):
            ds_ref[h] = carry[h]
        ds1b, ds0b = [x.astype(BF16) for x in ds1], [x.astype(BF16) for x in ds0]
        a = [jnp.where(causal, x, 0.0).astype(BF16) for x in a]
        da = [jnp.where(causal, x, 0.0).astype(BF16) for x in da]
        di1 = [_nt(keb[c][C:], ds1b[c]) for c in n]
        dke1 = [_nn(iv[c][C:], ds1b[c]) for c in n]
        di0 = [_nt(keb[c][:C], ds0b[c]) for c in n]
        dke0 = [_nn(iv[c][:C], ds0b[c]) for c in n]
        dqd_a = [_nn(da[c], kib[c]) for c in n]
        dki = [_tn(da[c], qdb[c]) for c in n]
        di_a = [_tn(a[c], dob[c]) for c in n]
        dqd, dke, db = [], [], []
        for c in n:
            ddec1 = jnp.sum(ds1[c] * s1[c], axis=0, keepdims=True)
            ddec0 = jnp.sum(ds0[c] * s0[c], axis=0, keepdims=True)
            dqd.append(dqd_a[c] + jnp.concatenate([dqd0[c], dqd1[c]], axis=0))
            h = chains[c][1]
            dp_ref[rows[c], 2 * W + HGRN_DIM * h:2 * W + HGRN_DIM * (h + 1)] = (
                di_a[c] + jnp.concatenate([di0[c], di1[c]], axis=0)).astype(BF16)
            dke.append(jnp.concatenate([dke0[c], dke1[c]], axis=0))
            gke = dke[c] * ke[c]
            dbend0 = jnp.sum(gke[:C], axis=0, keepdims=True) + ddec0 * dec0[c]
            dbend1 = jnp.sum(gke[C:], axis=0, keepdims=True) + ddec1 * dec1[c]
            dbc = dqd[c] * qd[c] - dki[c] * ki[c] - gke
            db.append(dbc + jnp.where(row == C - 1, dbend0, 0.0) + jnp.where(row == HG_T - 1, dbend1, 0.0))
        tri = jnp.where(anti, 1.0, 0.0).astype(F32)
        dlogf = [_nn(tri, db[c], HIGHEST) for c in n]
        for c in n:
            gt, lb, q, h = gts[c], lbs[c], qs[c], chains[c][1]
            dforget = dlogf[c] / gt["forget"] - (dki[c] * gt["emb"] + dke[c] * gt["eend"])
            sg, sq = gt["sg"], gt["sq"]
            dp_ref[rows[c], W + HGRN_DIM * h:W + HGRN_DIM * (h + 1)] = (
                dforget * (1.0 - lb) * sg * (1.0 - sg)).astype(BF16)
            dlb_ref[:, lanes[c]] += jnp.sum(dforget * (1.0 - sg), axis=0, keepdims=True)
            dp_ref[rows[c], lanes[c]] = (dqd[c] * gt["eb"] * sq * (1.0 + q * (1.0 - sq))).astype(BF16)

        @pl.when(t == nt - 1)
        def _():
            dl0 = dlb_ref[...] * lb_all * (1.0 - lb_all)
            dlg_ref[0:1, :] = dl0
            dlg_ref[1:2, :] = -dl0

    blk = lambda j: pl.BlockSpec((HG_STEP, W), lambda t: (nt - 1 - t, j))
    full = pl.BlockSpec((2, W), lambda t: (0, 0))
    extra, extra_specs, adapt = _behind(after, 6)
    return pl.pallas_call(
        adapt(body), name=name, grid=(nt,),
        in_specs=[blk(0), blk(0), blk(2), full,
                  pl.BlockSpec((2 * HG_GROUPS, HGRN_HEADS, HGRN_DIM, HGRN_DIM), lambda t: (nt - 1 - t, 0, 0, 0)), blk(0)]
        + extra_specs,
        out_specs=[pl.BlockSpec((HG_STEP, 3 * W), lambda t: (nt - 1 - t, 0)), full],
        out_shape=[jax.ShapeDtypeStruct((S, 3 * W), BF16), jax.ShapeDtypeStruct((2, W), F32)],
        scratch_shapes=[pltpu.VMEM((HGRN_HEADS, HGRN_DIM, HGRN_DIM), F32), pltpu.VMEM((1, W), F32)],
        compiler_params=_cparams("arbitrary"),
    )(proj, proj_f, proj, logits, states, drec, *extra)


def _out_proj(attn, rec, proj_h, x, g_attn, g_hgrn, g_norm2, w_out, *, name, tm=512):
    S, D = x.shape
    AW, W = ATTN_W, HGRN_W

    def body(a_ref, r_ref, hg_ref, x_ref, ga_ref, gh_ref, g2_ref, w_ref, h_ref, u_ref, m_ref):
        av = a_ref[...]
        m_ref[:, :AW] = (av * _rstd(av) * ga_ref[...]).astype(BF16)
        for h in range(HGRN_HEADS):
            sl = slice(HGRN_DIM * h, HGRN_DIM * (h + 1))
            rv, hg = r_ref[:, sl], hg_ref[:, sl].astype(F32)
            m_ref[:, AW + HGRN_DIM * h:AW + HGRN_DIM * (h + 1)] = (
                (rv * _rstd(rv) * gh_ref[:, sl]) * (hg * _sigmoid(hg))).astype(BF16)
        h1 = x_ref[...] + _nn(m_ref[...], w_ref[...])
        h_ref[...] = h1
        u_ref[...] = (h1 * _rstd(h1) * g2_ref[...]).astype(BF16)

    row = lambda w, j=0: pl.BlockSpec((tm, w), lambda i: (i, j))
    vec = lambda w: pl.BlockSpec((1, w), lambda i: (0, 0))
    return pl.pallas_call(
        body, name=name, grid=(S // tm,),
        in_specs=[row(AW), row(W), row(W, 3), row(D), vec(AW), vec(W), vec(D), _resident(w_out.shape)],
        out_specs=[row(D), row(D), row(AW + W)],
        out_shape=[jax.ShapeDtypeStruct((S, D), F32), jax.ShapeDtypeStruct((S, D), BF16),
                   jax.ShapeDtypeStruct((S, AW + W), BF16)],
        compiler_params=_cparams("parallel"),
    )(attn, rec, proj_h, x, g_attn, g_hgrn, g_norm2, w_out)


def _dmix_post_bwd(dh1b, w_out, attn, rec, proj_h, g_attn, g_hgrn, *, name, tm=512, after=None):
    S, D = dh1b.shape
    AW, W = ATTN_W, HGRN_W

    def body(dh_ref, w_ref, a_ref, r_ref, hg_ref, ga_ref, gh_ref, do_ref, dl_ref, dr_ref, dhg_ref, dga_ref, dgh_ref):
        first = pl.program_id(0) == 0
        dmix = _nt(dh_ref[...], w_ref[...])
        av = a_ref[...]
        dov, dga = _norm_bwd(av, ga_ref[...], dmix[:, :AW])
        do_ref[...] = dov
        shift = HEAD_DIM.bit_length() - 1
        hi = lax.shift_right_logical(lax.broadcasted_iota(jnp.int32, (AW, AW), 0), shift)
        hj = lax.shift_right_logical(lax.broadcasted_iota(jnp.int32, (AW, AW), 1), shift)
        prod = dov * av
        hi_part = prod.astype(BF16)
        lo_part = (prod - hi_part.astype(F32)).astype(BF16)
        same_head = jnp.where(hi == hj, 1.0, 0.0).astype(BF16)
        dl_ref[...] = _nn(hi_part, same_head) + _nn(lo_part, same_head)
        _accumulate(dga_ref, jnp.sum(dga, axis=0, keepdims=True), first)

        @pl.when(first)
        def _():
            dgh_ref[...] = jnp.zeros_like(dgh_ref)

        for h in range(HGRN_HEADS):
            sl = slice(HGRN_DIM * h, HGRN_DIM * (h + 1))
            rv, hg, gv = r_ref[:, sl], hg_ref[:, sl].astype(F32), gh_ref[:, sl]
            dout = dmix[:, AW + HGRN_DIM * h:AW + HGRN_DIM * (h + 1)]
            sg = _sigmoid(hg)
            drv, dgh = _norm_bwd(rv, gv, dout * (hg * sg))
            dr_ref[:, sl] = drv
            dgh_ref[:, sl] += jnp.sum(dgh, axis=0, keepdims=True)
            dhg_ref[:, sl] = (dout * (rv * _rstd(rv) * gv) * (sg * (1.0 + hg * (1.0 - sg)))).astype(BF16)

    row = lambda w, j=0: pl.BlockSpec((tm, w), lambda i: (i, j))
    vec = lambda w: pl.BlockSpec((1, w), lambda i: (0, 0))
    extra, extra_specs, adapt = _behind(after, 7)
    return pl.pallas_call(
        adapt(body), name=name, grid=(S // tm,),
        in_specs=[row(D), _resident(w_out.shape), row(AW), row(W), row(W, 3), vec(AW), vec(W)] + extra_specs,
        out_specs=[row(AW), row(AW), row(W), row(W), vec(AW), vec(W)],
        out_shape=[jax.ShapeDtypeStruct((S, AW), F32), jax.ShapeDtypeStruct((S, AW), F32),
                   jax.ShapeDtypeStruct((S, W), F32), jax.ShapeDtypeStruct((S, W), BF16),
                   jax.ShapeDtypeStruct((1, AW), F32), jax.ShapeDtypeStruct((1, W), F32)],
        compiler_params=_cparams("arbitrary"),
    )(dh1b, w_out, attn, rec, proj_h, g_attn, g_hgrn, *extra)


def _conv_act(g, g1, g2, w_ref, b_ref):
    c = b_ref[...] + w_ref[0:1, :] * g2 + w_ref[1:2, :] * g1 + w_ref[2:3, :] * g
    return c, 0.5 * (1.0 + lax.erf(c * (2.0 ** -0.5)))


def _shift_down(g, halo, row):
    g1 = jnp.where(row == 0, halo[7:8], pltpu.roll(g, 1, 0))
    g2 = jnp.where(row == 0, halo[6:7], jnp.where(row == 1, halo[7:8], pltpu.roll(g, 2, 0)))
    return g1, g2


def _shift_up(x, halo, row):
    n = x.shape[0]
    x1 = jnp.where(row == n - 1, halo[0:1], pltpu.roll(x, n - 1, 0))
    x2 = jnp.where(row == n - 2, halo[0:1], jnp.where(row == n - 1, halo[1:2], pltpu.roll(x, n - 2, 0)))
    return x1, x2


def _up_glu(u, wt_up, conv_w, conv_b, *, name, tm=1024, tn=1408):
    S, D = u.shape
    F = wt_up.shape[0] // 2
    tn = _tile(F, tn)
    nf = F // tn

    def body(u_ref, wg_ref, wv_ref, cw_ref, cb_ref, g_ref, ge_ref, t_ref, a_ref, halo_ref):
        i, j = pl.program_id(0), pl.program_id(1)

        @pl.when(i == 0)
        def _():
            halo_ref[j] = jnp.zeros((SUBLANES, tn), F32)

        uv = u_ref[...]
        g, v = _nt(uv, wg_ref[...]), _nt(uv, wv_ref[...])
        row = lax.broadcasted_iota(jnp.int32, (tm, tn), 0)
        g1, g2 = _shift_down(g, halo_ref[j], row)
        c, cdf = _conv_act(g, g1, g2, cw_ref, cb_ref)
        gelu = c * cdf
        pdf = jnp.exp(-0.5 * c * c) * (1.0 / (2.0 * jnp.pi) ** 0.5)
        a_ref[...] = (gelu * v).astype(BF16)
        g_ref[...] = g.astype(BF16)
        ge_ref[...] = gelu.astype(BF16)
        t_ref[...] = (v * (cdf + c * pdf)).astype(BF16)
        halo_ref[j] = g[tm - SUBLANES:, :]

    col = pl.BlockSpec((tm, tn), lambda i, j: (i, j))
    out = jax.ShapeDtypeStruct((S, F), BF16)
    return pl.pallas_call(
        body, name=name, grid=(S // tm, nf),
        in_specs=[pl.BlockSpec((tm, D), lambda i, j: (i, 0)), pl.BlockSpec((tn, D), lambda i, j: (j, 0)),
                  pl.BlockSpec((tn, D), lambda i, j: (j + nf, 0)), pl.BlockSpec((3, tn), lambda i, j: (0, j)),
                  pl.BlockSpec((1, tn), lambda i, j: (0, j))],
        out_specs=[col, col, col, col], out_shape=[out, out, out, out],
        scratch_shapes=[pltpu.VMEM((nf, SUBLANES, tn), F32)], compiler_params=_cparams("arbitrary", "arbitrary"),
    )(u, wt_up, wt_up, conv_w, conv_b)


def _dact_glu_bwd(dh2b, w_down, gate, gelu, vslope, conv_w, *, name, tm=1024, tn=1408):
    S, D = dh2b.shape
    F = gate.shape[1]
    tn = _tile(F, tn)
    nf, ni = F // tn, S // tm

    def body(dh_ref, wd_ref, g_ref, ge_ref, t_ref, cw_ref, dg_ref, dv_ref, dcw_ref, dcb_ref, halo_ref, acc_ref):
        i, j = pl.program_id(0), pl.program_id(1)

        @pl.when(i == 0)
        def _():
            halo_ref[j] = jnp.zeros((SUBLANES, tn), F32)
            acc_ref[j] = jnp.zeros((SUBLANES, tn), F32)

        g = g_ref[...].astype(F32)
        row = lax.broadcasted_iota(jnp.int32, (tm, tn), 0)
        da = _nt(dh_ref[...], wd_ref[...])
        dv_ref[...] = (da * ge_ref[...].astype(F32)).astype(BF16)
        dc = da * t_ref[...].astype(F32)
        d1, d2 = _shift_up(dc, halo_ref[j], row)
        dg_ref[...] = (cw_ref[2:3, :] * dc + cw_ref[1:2, :] * d1 + cw_ref[0:1, :] * d2).astype(BF16)
        halo_ref[j] = dc[:SUBLANES, :]
        for k, t in enumerate((d2 * g, d1 * g, dc * g, dc)):
            acc_ref[j, k:k + 1, :] += jnp.sum(t, axis=0, keepdims=True)

        @pl.when((i == ni - 1) & (j == nf - 1))
        def _():
            for jj in range(nf):
                dcw_ref[:, jj * tn:(jj + 1) * tn] = acc_ref[jj, 0:3, :]
                dcb_ref[:, jj * tn:(jj + 1) * tn] = acc_ref[jj, 3:4, :]

    tile = pl.BlockSpec((tm, tn), lambda i, j: (ni - 1 - i, j))
    return pl.pallas_call(
        body, name=name, grid=(ni, nf),
        in_specs=[pl.BlockSpec((tm, D), lambda i, j: (ni - 1 - i, 0)), pl.BlockSpec((tn, D), lambda i, j: (j, 0)),
                  tile, tile, tile, pl.BlockSpec((3, tn), lambda i, j: (0, j))],
        out_specs=[tile, tile, pl.BlockSpec((3, F), lambda i, j: (0, 0)), pl.BlockSpec((1, F), lambda i, j: (0, 0))],
        out_shape=[jax.ShapeDtypeStruct((S, F), BF16), jax.ShapeDtypeStruct((S, F), BF16),
                   jax.ShapeDtypeStruct((3, F), F32), jax.ShapeDtypeStruct((1, F), F32)],
        scratch_shapes=[pltpu.VMEM((nf, SUBLANES, tn), F32), pltpu.VMEM((nf, SUBLANES, tn), F32)],
        compiler_params=_cparams("arbitrary", "arbitrary"),
    )(dh2b, w_down, gate, gelu, vslope, conv_w)


def _down_loss(act, w_down, h1, g, target, *, name, tm=512):
    S, F = act.shape
    D = h1.shape[1]

    def body(a_ref, w_ref, h_ref, g_ref, t_ref, dh_ref, dhb_ref, dg_ref, loss_ref):
        first = pl.program_id(0) == 0
        h2 = h_ref[...] + _nn(a_ref[...], w_ref[...])
        gv = g_ref[...]
        r = _rstd(h2)
        xh = h2 * r
        err = xh * gv - t_ref[...]
        part_loss = 0.5 * jnp.sum(jnp.mean(err * err, axis=-1, keepdims=True), axis=0, keepdims=True)
        dy = err * (1.0 / D)
        dxh = dy * gv
        dh = r * (dxh - xh * jnp.mean(dxh * xh, axis=-1, keepdims=True))
        dh_ref[...] = dh
        dhb_ref[...] = dh.astype(BF16)
        _accumulate(dg_ref, jnp.sum(dy * xh, axis=0, keepdims=True), first)
        _accumulate(loss_ref, jnp.broadcast_to(part_loss, (1, LANES)), first)

    row = lambda w: pl.BlockSpec((tm, w), lambda i: (i, 0))
    vec = lambda w: pl.BlockSpec((1, w), lambda i: (0, 0))
    return pl.pallas_call(
        body, name=name, grid=(S // tm,), in_specs=[row(F), _resident(w_down.shape), row(D), vec(D), row(D)],
        out_specs=[row(D), row(D), vec(D), vec(LANES)],
        out_shape=[jax.ShapeDtypeStruct((S, D), F32), jax.ShapeDtypeStruct((S, D), BF16),
                   jax.ShapeDtypeStruct((1, D), F32), jax.ShapeDtypeStruct((1, LANES), F32)],
        compiler_params=_cparams("arbitrary"),
    )(act, w_down, h1, g, target)


def _grad_norm_input(pieces, ws, x, g, add, *, name, tm=512, after=None):
    S, D = x.shape
    widths = [p.shape[1] for p in pieces]
    n, nw = len(pieces), len(ws)
    where, wi, off = [], 0, ws[0][1]
    for wd in widths:
        if off == ws[wi][0].shape[0]:
            wi, off = wi + 1, ws[wi + 1][1]
        where.append((wi, off))
        off += wd
    ws = [w for w, _ in ws]

    def body(*refs):
        p_refs, w_refs = refs[:n], refs[n:n + nw]
        x_ref, g_ref, add_ref, dx_ref, dxb_ref, dg_ref = refs[n + nw:]
        halves = _row_halves(tm)
        du = []
        for rows in halves:
            terms = [_nn(p_refs[k][rows, :], w_refs[wi][off:off + widths[k], :]) for k, (wi, off) in enumerate(where)]
            du.append(sum(terms[1:], terms[0]))
        dg_sum = None
        for rows, duh in zip(halves, du):
            dx, dg = _norm_bwd(x_ref[rows, :], g_ref[...], duh)
            dx = add_ref[rows, :] + dx
            dx_ref[rows, :] = dx
            dxb_ref[rows, :] = dx.astype(BF16)
            part = jnp.sum(dg, axis=0, keepdims=True)
            dg_sum = part if dg_sum is None else dg_sum + part
        _accumulate(dg_ref, dg_sum, pl.program_id(0) == 0)

    row = lambda w_: pl.BlockSpec((tm, w_), lambda i: (i, 0))
    vec = pl.BlockSpec((1, D), lambda i: (0, 0))
    extra, extra_specs, adapt = _behind(after, n + nw + 3)
    return pl.pallas_call(
        adapt(body), name=name, grid=(S // tm,),
        in_specs=[row(wd) for wd in widths] + [_resident(w.shape) for w in ws] + [row(D), vec, row(D)] + extra_specs,
        out_specs=[row(D), row(D), vec],
        out_shape=[jax.ShapeDtypeStruct((S, D), F32), jax.ShapeDtypeStruct((S, D), BF16),
                   jax.ShapeDtypeStruct((1, D), F32)],
        compiler_params=_cparams("arbitrary"),
    )(*pieces, *ws, x, g, add, *extra)


def _rows(a):
    return a.reshape(-1, a.shape[-1])


def _row_tile(rows, cols, itemsize=4, budget=1 << 20):
    t = rows
    while t % 32 == 0 and t * cols * itemsize > budget:
        t //= 2
    return t


def _cast_together(arrs, out_dtype, *, name):
    n = len(arrs)

    def body(*refs):
        for i_ref, o_ref in zip(refs[:n], refs[n:]):
            o_ref[...] = i_ref[...].astype(out_dtype)

    return pl.pallas_call(body, name=name, out_shape=[jax.ShapeDtypeStruct(a.shape, out_dtype) for a in arrs],
                          compiler_params=_cparams())(*arrs)


def _adamw(parts, w, m, v, *, name):
    shape = w.shape
    w2, m2, v2 = _rows(w), _rows(m), _rows(v)
    R, C = w2.shape
    parts = [p.reshape(-1, R, C) for p in parts]
    tr = _row_tile(R, C)
    np_ = len(parts)
    c1, c2 = 1.0 - ADAM_B1 ** ADAM_STEP, 1.0 - ADAM_B2 ** ADAM_STEP

    def body(*refs):
        terms = [(r, k) for r in refs[:np_] for k in range(r.shape[0])]
        g = terms[0][0][terms[0][1]].astype(F32)
        for r, k in terms[1:]:
            g = g + r[k].astype(F32)
        w_ref, m_ref, v_ref, g_out, d_out, m_out, v_out = refs[np_:]
        mn = ADAM_B1 * m_ref[...] + (1.0 - ADAM_B1) * g
        vn = ADAM_B2 * v_ref[...] + (1.0 - ADAM_B2) * (g * g)
        g_out[...] = g
        d_out[...] = -ADAM_LR * ((mn / c1) / (jnp.sqrt(vn / c2) + ADAM_EPS) + ADAM_WD * w_ref[...])
        m_out[...] = mn
        v_out[...] = vn

    spec = pl.BlockSpec((tr, C), lambda i: (i, 0))
    out = jax.ShapeDtypeStruct((R, C), F32)
    stacks = [pl.BlockSpec((p.shape[0], tr, C), lambda i: (0, i, 0)) for p in parts]
    res = pl.pallas_call(
        body, name=name, grid=(R // tr,), in_specs=stacks + [spec] * 3, out_specs=[spec] * 4,
        out_shape=[out] * 4, compiler_params=_cparams("parallel"),
    )(*parts, w2, m2, v2)
    return [r.reshape(shape) for r in res]


def _adamw_packed(stack, widths, params, *, name):
    c1, c2 = 1.0 - ADAM_B1 ** ADAM_STEP, 1.0 - ADAM_B2 ** ADAM_STEP
    k = stack.shape[0]
    flat = [None if p is None else [_rows(a) for a in p] for p in params]
    n_in = sum(3 for p in flat if p is not None)

    def body(*refs):
        s_ref, ins, outs = refs[0], list(refs[1:1 + n_in]), list(refs[1 + n_in:])
        off = 0
        for width, p in zip(widths, flat):
            rows = 1 if p is None else p[0].shape[0]
            cols = width // rows
            w_ref, m_ref, v_ref = (None, None, None) if p is None else (ins.pop(0), ins.pop(0), ins.pop(0))
            o_refs = [outs.pop(0) for _ in range(1 if p is None else 4)]
            for r in range(rows):
                seg = slice(off + r * cols, off + (r + 1) * cols)
                g = s_ref[0, :, seg]
                for j in range(1, k):
                    g = g + s_ref[j, :, seg]
                o_refs[0][r:r + 1, :] = g
                if p is not None:
                    row = slice(r, r + 1)
                    mn = ADAM_B1 * m_ref[row, :] + (1.0 - ADAM_B1) * g
                    vn = ADAM_B2 * v_ref[row, :] + (1.0 - ADAM_B2) * (g * g)
                    o_refs[1][row, :] = -ADAM_LR * ((mn / c1) / (jnp.sqrt(vn / c2) + ADAM_EPS) + ADAM_WD * w_ref[row, :])
                    o_refs[2][row, :] = mn
                    o_refs[3][row, :] = vn
            off += width

    operands, out_shape = [stack], []
    for width, p in zip(widths, flat):
        if p is None:
            out_shape.append(jax.ShapeDtypeStruct((1, width), F32))
        else:
            operands += p
            out_shape += [jax.ShapeDtypeStruct(p[0].shape, F32)] * 4
    res = list(pl.pallas_call(body, name=name, out_shape=out_shape)(*operands))
    out = []
    for p, orig in zip(flat, params):
        n = 1 if p is None else 4
        out.append([r if orig is None else r.reshape(orig[0].shape) for r in res[:n]])
        res = res[n:]
    return out


def _coords():
    return lax.axis_index("x"), lax.axis_index("y"), lax.axis_index("c")


def _all_gather(shards, *, name):
    n = len(shards)

    def body(*refs):
        x_refs, out_refs = refs[:n], refs[n:2 * n]
        send_sems, recv_sems, local_sems = refs[2 * n:]
        x, y, c = _coords()
        me, sibling = (x, y, c), (x, y, 1 - c)
        chips = [(1 - x, y), (x, 1 - y), (1 - x, 1 - y)]

        def slot(a, dev):
            return out_refs[a].at[4 * dev[0] + 2 * dev[1] + dev[2]]

        def copy(a, k, block, to, src=None):
            return pltpu.make_async_remote_copy(
                src_ref=slot(a, block) if src is None else src, dst_ref=slot(a, block),
                send_sem=send_sems.at[7 * a + k], recv_sem=recv_sems.at[7 * a + k], device_id=to, device_id_type=MESH)

        mine = [pltpu.make_async_copy(x_refs[a], slot(a, me), local_sems.at[a]) for a in range(n)]
        for cp in mine:
            cp.start()
        first = []
        for a in range(n):
            first.append(copy(a, 0, me, sibling, src=x_refs[a]))
            first += [copy(a, 1 + j, me, (*chip, c), src=x_refs[a]) for j, chip in enumerate(chips)]
        for cp in first:
            cp.start()
        passed = []
        for j, chip in enumerate(chips):
            for a in range(n):
                copy(a, 1 + j, (*chip, c), me).wait_recv()
                fwd = copy(a, 4 + j, (*chip, c), sibling)
                fwd.start()
                passed.append(fwd)
        for a in range(n):
            copy(a, 0, sibling, me).wait_recv()
            for j, chip in enumerate(chips):
                copy(a, 4 + j, (*chip, 1 - c), me).wait_recv()
        for cp in first + passed:
            cp.wait_send()
        for cp in mine:
            cp.wait()

    return pl.pallas_call(
        body, name=name, in_specs=[HBM] * n, out_specs=[HBM] * n,
        out_shape=[jax.ShapeDtypeStruct((N_DEV, *s.shape), s.dtype) for s in shards],
        scratch_shapes=[pltpu.SemaphoreType.DMA((7 * n,)), pltpu.SemaphoreType.DMA((7 * n,)),
                        pltpu.SemaphoreType.DMA((n,))],
    )(*shards)


def _flip_y(x, y, c):
    return (x, 1 - y, c)


def _flip_x(x, y, c):
    return (1 - x, y, c)


def _flip_xy(x, y, c):
    return (1 - x, 1 - y, c)


SEM = pl.BlockSpec(memory_space=pltpu.SEMAPHORE)
SIDE_EFFECT = pltpu.SideEffectType.DATAFLOW_SIDE_EFFECTING


def _in_hbm(a):
    return pltpu.with_memory_space_constraint(a, pltpu.HBM)


def _copies_start(srcs, lands, plan, n_copies, *, name, after=None):
    ns, nl = len(srcs), len(lands)
    extra = [] if after is None else [after]

    def body(*refs):
        src_refs, land_refs = refs[:ns], refs[ns:ns + nl]
        send_sems, recv_sems = refs[ns + nl + len(extra):ns + nl + len(extra) + 2]
        token = refs[-1]
        for k, (src, dst, peer, _) in enumerate(plan(src_refs, land_refs, *_coords())):
            pltpu.make_async_remote_copy(src_ref=src, dst_ref=dst, send_sem=send_sems.at[k], recv_sem=recv_sems.at[k],
                                         device_id=peer, device_id_type=MESH).start()
        token[...] = jnp.zeros_like(token)

    bufs = [*srcs, *lands]
    res = pl.pallas_call(
        body, name=name, in_specs=[HBM] * (ns + nl) + [pl.BlockSpec(memory_space=pl.ANY)] * len(extra),
        out_specs=(SEM, SEM, *[HBM] * (ns + nl), pl.BlockSpec(memory_space=pltpu.VMEM)),
        out_shape=(pltpu.SemaphoreType.DMA((n_copies,)), pltpu.SemaphoreType.DMA((n_copies,)),
                   *[pltpu.HBM(b.shape, b.dtype) for b in bufs], jax.ShapeDtypeStruct((SUBLANES, LANES), F32)),
        input_output_aliases={i: 2 + i for i in range(ns + nl)},
        compiler_params=pltpu.CompilerParams(has_side_effects=SIDE_EFFECT),
    )(*[_in_hbm(b) for b in bufs], *extra)
    return res[0], res[1], list(res[2:2 + ns]), list(res[2 + ns:2 + ns + nl]), res[-1]


def _copies_wait(started, plan, after, *, name, with_srcs=False):
    send_sems, recv_sems, srcs, lands, _ = started
    ns, nl = len(srcs), len(lands)

    def body(*refs):
        src_refs, land_refs = refs[:ns], refs[ns:ns + nl]
        send_sems, recv_sems = refs[ns + nl:ns + nl + 2]
        for k, (src, dst, peer, here) in enumerate(plan(src_refs, land_refs, *_coords())):
            pltpu.make_async_remote_copy(src_ref=src, dst_ref=dst, send_sem=send_sems.at[k], recv_sem=recv_sems.at[k],
                                         device_id=peer, device_id_type=MESH).wait_send()
            pltpu.make_async_remote_copy(src_ref=src, dst_ref=here, send_sem=send_sems.at[k], recv_sem=recv_sems.at[k],
                                         device_id=peer, device_id_type=MESH).wait_recv()

    bufs = [*srcs, *lands]
    res = pl.pallas_call(
        body, name=name, in_specs=[HBM] * (ns + nl) + [SEM, SEM, pl.BlockSpec(memory_space=pl.ANY)],
        out_specs=[HBM] * (ns + nl), out_shape=[pltpu.HBM(b.shape, b.dtype) for b in bufs],
        input_output_aliases={i: i for i in range(ns + nl)},
        compiler_params=pltpu.CompilerParams(has_side_effects=SIDE_EFFECT),
    )(*bufs, send_sems, recv_sems, after)
    return (list(res[:ns]), list(res[ns:])) if with_srcs else list(res[ns:])


def _dev_index(dev):
    return 4 * dev[0] + 2 * dev[1] + dev[2]


def _ag_chips_plan(src_refs, land_refs, x, y, c):
    me = _dev_index((x, y, c))
    return [(src, land.at[me], peer, land.at[_dev_index(peer)])
            for src, land in zip(src_refs, land_refs) for peer in (_flip_y(x, y, c), _flip_x(x, y, c), _flip_xy(x, y, c))]


def _ag_sibling_plan(src_refs, land_refs, x, y, c):
    chips = [(x, y), (x, 1 - y), (1 - x, y), (1 - x, 1 - y)]
    return [(land.at[_dev_index((*chip, c))], land.at[_dev_index((*chip, c))], (x, y, 1 - c),
             land.at[_dev_index((*chip, 1 - c))]) for land in land_refs for chip in chips]


def _ag_direct_plan(src_refs, land_refs, x, y, c):
    me = _dev_index((x, y, c))
    plan = []
    for src, land in zip(src_refs, land_refs):
        for m in range(1, N_DEV):
            peer = (x + (m >> 2) * (1 - 2 * x), y + ((m >> 1) & 1) * (1 - 2 * y), c + (m & 1) * (1 - 2 * c))
            plan.append((src, land.at[me], peer, land.at[_dev_index(peer)]))
    return plan


def _rs_direct_plan(src_refs, land_refs, x, y, c):
    plan = []
    for src, land in zip(src_refs, land_refs):
        for m in range(1, N_DEV):
            peer = (x + (m >> 2) * (1 - 2 * x), y + ((m >> 1) & 1) * (1 - 2 * y), c + (m & 1) * (1 - 2 * c))
            plan.append((src.at[_dev_index(peer)], land.at[m - 1], peer, land.at[m - 1]))
    return plan


def _rs_start(grads, me, *, name, after=None):
    own = [lax.dynamic_index_in_dim(g, me, 0, keepdims=False) for g in grads]
    lands = [lax.empty((N_DEV - 1, *g.shape[1:]), g.dtype) for g in grads]
    return _copies_start(grads, lands, _rs_direct_plan, (N_DEV - 1) * len(grads), name=name, after=after), own


def _rs_finish(started, after, *, name):
    handle, own = started
    got = _copies_wait(handle, _rs_direct_plan, after, name=name)
    return [[o, land] for o, land in zip(own, got)]


def _gathered_cols(w8):
    return w8.transpose(1, 0, 2).reshape(w8.shape[1], -1)


def _pair_major(wt):
    return wt.reshape(3, ATTN_W // LANES, LANES, -1).transpose(1, 0, 2, 3).reshape(3 * ATTN_W, -1)


def kernel(x, norm1_g, w_in, attn_norm_g, hgrn_norm_g, hgrn_lb_logits, w_out, norm2_g, w_up, conv_w, conv_b, w_down, final_norm_g, loss_target, m_norm1_g, m_w_in, m_attn_norm_g, m_hgrn_norm_g, m_hgrn_lb_logits, m_w_out, m_norm2_g, m_w_up, m_conv_w, m_conv_b, m_w_down, m_final_norm_g, v_norm1_g, v_w_in, v_attn_norm_g, v_hgrn_norm_g, v_hgrn_lb_logits, v_w_out, v_norm2_g, v_w_up, v_conv_w, v_conv_b, v_w_down, v_final_norm_g):
    xs, target = x[0], loss_target[0]
    S, D = xs.shape
    NA = 3 * ATTN_W
    fng = final_norm_g.reshape(1, D)

    t = lambda a: a[0].T
    casts = [*_cast_together([t(w_in)], BF16, name="cast_w_in"),
             *_cast_together([w_out[0], t(w_up), w_down[0]], BF16, name="cast_later_weights")]
    me = _dev_index(_coords())
    (g_in,) = _all_gather(casts[:1], name="ag_w_in")
    later = casts[1:] + [conv_w[0]]
    ag1 = _copies_start(later, [lax.empty((N_DEV, *s.shape), s.dtype) for s in later], _ag_chips_plan,
                        3 * len(later), name="ag_chips_start", after=g_in)
    wi = g_in.reshape(-1, D)
    wi_a = _pair_major(wi[:NA])

    u1, proj_a, proj_h, proj_f = _in_proj(xs, norm1_g, wi_a, wi, NA, name="in_proj", after=ag1[4])
    attn, lse = _attn_fwd(proj_a, name="attn_fwd")
    later, lands = _copies_wait(ag1, _ag_chips_plan, attn, name="ag_chips_wait", with_srcs=True)
    lands = [lax.dynamic_update_index_in_dim(l, s, me, 0) for l, s in zip(lands, later)]
    ag2 = _copies_start([], lands, _ag_sibling_plan, 4 * len(later), name="ag_sibling_start")
    rec, states = _hgrn_fwd(proj_h, proj_f, hgrn_lb_logits, name="hgrn_fwd", after=ag2[4])
    g_out, g_up, g_down, g_cw = _copies_wait(ag2, _ag_sibling_plan, rec, name="ag_sibling_wait")
    wo = g_out.reshape(-1, D)
    wu = g_up.reshape(-1, D)
    wd = g_down.reshape(-1, D)
    cw = _gathered_cols(g_cw)
    h1, u2, mixed = _out_proj(attn, rec, proj_h, xs, attn_norm_g, hgrn_norm_g, norm2_g, wo, name="out_proj")
    gate, gelu, vslope, act = _up_glu(u2, wu, cw, conv_b, name="up_glu")
    dh2, dh2b, d_fng, loss_part = _down_loss(act, wd, h1, fng, target, name="down_loss")

    dgate, dval, d_cw, d_cb = _dact_glu_bwd(dh2b, wd, gate, gelu, vslope, cw, name="dact_glu_bwd")
    dw_down = _mm_tn([act], dh2b, tm=256, name="dw_down")
    dh1, dh1b, d_n2g = _grad_norm_input([dgate, dval], [(wu, 0)], h1, norm2_g, dh2, name="du2_norm2_bwd")
    F = dgate.shape[1]
    dw_up = _mm_tn([dgate, dval], u2, tm=256, name="dw_up")
    rs_ffn = _rs_start([dw_down.reshape(N_DEV, -1, D), dw_up.reshape(N_DEV, -1, D)], me, name="rs_ffn_start")
    dattn, delta, drec, dhg, d_ang, d_hng = _dmix_post_bwd(dh1b, wo, attn, rec, proj_h, attn_norm_g, hgrn_norm_g,
                                                          name="dmix_post_bwd", after=rs_ffn[0][4])
    dw_out = _mm_tn([mixed], dh1b, name="dw_out")
    rs_out = _rs_start([dw_out.reshape(N_DEV, -1, D)], me, name="rs_out_start")
    dproj_h, d_lbl = _hgrn_bwd(proj_h, proj_f, hgrn_lb_logits, states, drec, name="hgrn_bwd", after=rs_out[0][4])
    small = [("loss", loss_part, None, None, None),
             ("attn_norm_g", d_ang, attn_norm_g, m_attn_norm_g, v_attn_norm_g),
             ("hgrn_norm_g", d_hng, hgrn_norm_g, m_hgrn_norm_g, v_hgrn_norm_g),
             ("hgrn_lb_logits", d_lbl, hgrn_lb_logits, m_hgrn_lb_logits, v_hgrn_lb_logits),
             ("norm2_g", d_n2g, norm2_g, m_norm2_g, v_norm2_g),
             ("conv_b", d_cb, conv_b, m_conv_b, v_conv_b),
             ("final_norm_g", d_fng, final_norm_g, m_final_norm_g, v_final_norm_g)]
    pack = lambda arrs: jnp.concatenate([a.reshape(1, -1) for a in arrs], axis=1)
    small_own = [pack([s[1] for s in small]), d_cw]
    ag_small = _copies_start(small_own, [lax.empty((N_DEV, *s.shape), s.dtype) for s in small_own], _ag_direct_plan,
                             (N_DEV - 1) * len(small_own), name="ag_small_start")
    dproj_a = _attn_bwd(proj_a, dattn, lse, delta, name="attn_bwd")
    pairs = ATTN_W // LANES
    dw_in = _mm_tn([dproj_a], u1, tm=LANES, rows=wi.shape[0], row_block=lambda i: pairs * (i % 3) + i // 3,
                   name="dw_in_attn")
    dw_in = _mm_tn([dproj_h, dhg], u1, tm=256, rows=wi.shape[0], row_block=lambda i: i + NA // 256, into=dw_in,
                   name="dw_in_hgrn")
    rs_in = _rs_start([dw_in.reshape(N_DEV, -1, D)], me, name="rs_in_start", after=ag_small[4])
    grad_x, _, d_n1g = _grad_norm_input([dproj_a, dproj_h, dhg], [(wi_a, 0), (wi, NA)], xs,
                                        norm1_g, dh1, name="du1_norm1_bwd", after=rs_in[0][4])

    res = {}

    def update(nm, parts, w, m, v, transposed=False):
        if transposed:
            raw = _adamw(parts, t(w), t(m), t(v), name=f"adamw_{nm}")
            res[nm] = [r.T[None] for r in raw]
        else:
            raw = res[nm] = _adamw(parts, w, m, v, name=f"adamw_{nm}")
        return raw[1]

    g_down, g_up = _rs_finish(rs_ffn, grad_x, name="rs_ffn_wait")
    update("w_down", g_down, w_down, m_w_down, v_w_down)
    done_up = update("w_up", g_up, w_up, m_w_up, v_w_up, transposed=True)
    (g_out,) = _rs_finish(rs_out, grad_x, name="rs_out_wait")
    update("w_out", g_out, w_out, m_w_out, v_w_out)
    (g_in,) = _rs_finish(rs_in, done_up, name="rs_in_wait")
    done_in = update("w_in", g_in, w_in, m_w_in, v_w_in, transposed=True)

    small_own, small_all = _copies_wait(ag_small, _ag_direct_plan, grad_x, name="ag_small_wait", with_srcs=True)
    g_small, g_dcw = [lax.dynamic_update_index_in_dim(l, s, me, 0) for l, s in zip(small_all, small_own)]
    sm = _adamw_packed(g_small, [s[1].size for s in small], [None if s[2] is None else s[2:] for s in small],
                       name="adamw_small")
    for (nm, *_), r in zip(small, sm):
        res[nm] = r
    ncw = conv_w.shape[-1]
    mine_cw = lax.dynamic_slice_in_dim(g_dcw, me * ncw, ncw, axis=2)
    res["conv_w"] = _adamw([mine_cw], conv_w, m_conv_w, v_conv_w, name="adamw_conv_w")
    late, _ = lax.optimization_barrier((d_n1g, done_in))
    update("norm1_g", _all_gather([late], name="ag_norm1_grad"), norm1_g, m_norm1_g, v_norm1_g)

    loss = res["loss"][0][0, 0]
    order = ["norm1_g", "w_in", "attn_norm_g", "hgrn_norm_g", "hgrn_lb_logits", "w_out", "norm2_g", "w_up",
             "conv_w", "conv_b", "w_down", "final_norm_g"]
    return (loss, grad_x[None], *[res[nm][0] for nm in order], *[res[nm][1] for nm in order],
            *[res[nm][2] for nm in order], *[res[nm][3] for nm in order])
```
